```python
import jax, jax.numpy as jnp
from jax import lax
import numpy as np

D_MODEL = 1024
BATCH = 8
SEQ = 4096
DEPTH = 2

N_EVEN = (DEPTH + 1) // 2
N_ODD = DEPTH // 2
NORM_EPS = 1e-6

RG_WIDTH = D_MODEL
RG_HEADS = 8
RG_HEAD_DIM = RG_WIDTH // RG_HEADS
RG_CONV = 4
RG_C = 8.0
RG_CONV_LEFT = 2
SC_WIDTH = D_MODEL
SC_CONV = 3
SC_CONV_LEFT = 1
EVEN_IN = 2 * RG_WIDTH + 4 * SC_WIDTH
EVEN_MIX = RG_WIDTH + SC_WIDTH
GLA_HEADS = 4
GLA_KEY = D_MODEL // 2
GLA_VAL = D_MODEL
GLA_DK = GLA_KEY // GLA_HEADS
GLA_DV = GLA_VAL // GLA_HEADS
GLA_RANK = 16
GLA_NORMALIZER = 16.0
GLA_CHUNK = 64
ODD_IN = 2 * GLA_KEY + 2 * GLA_VAL + 2 * GLA_RANK

kernel_name = 'hybrid_rglru_shortconv_gla_encoder'


def rmsnorm(x, g):
    xf = x.astype(jnp.float32)
    y = xf * lax.rsqrt(jnp.mean(xf * xf, axis=-1, keepdims=True) + NORM_EPS)
    return (y * g.astype(jnp.float32)).astype(x.dtype)


def centred_dwconv(u, w, left):
    seq = u.shape[1]
    width = w.shape[0]
    up = jnp.pad(u, ((0, 0), (left, width - 1 - left), (0, 0)))
    out = up[:, 0:seq] * w[0]
    for k in range(1, width):
        out = out + up[:, k:k + seq] * w[k]
    return out


def rg_lru(u, gate_w, gate_b, lam, reverse):
    bsz, seq, width = u.shape
    uf = u.astype(jnp.float32)
    uh = uf.reshape(bsz, seq, RG_HEADS, RG_HEAD_DIM)
    gates = jax.nn.sigmoid(
        jnp.einsum('bshi,ghij->gbshj', uh, gate_w.astype(jnp.float32))
        + gate_b[:, None, None].astype(jnp.float32))
    r = gates[0].reshape(bsz, seq, width)
    i = gates[1].reshape(bsz, seq, width)
    log_a = -RG_C * r * jax.nn.softplus(-lam.astype(jnp.float32))
    a = jnp.exp(log_a)
    b = jnp.sqrt(-jnp.expm1(2.0 * log_a)) * (i * uf)

    def combine(left, right):
        a_l, b_l = left
        a_r, b_r = right
        return (a_l * a_r, a_r * b_l + b_r)

    _, h = lax.associative_scan(combine, (a, b), reverse=reverse, axis=1)
    return h


def gla_chunked(q, k, v, g):
    bsz, nh, seq, dk = q.shape
    dv = v.shape[-1]
    nc = seq // GLA_CHUNK
    qc = q.astype(jnp.float32).reshape(bsz, nh, nc, GLA_CHUNK, dk) * (dk ** -0.5)
    kc = k.astype(jnp.float32).reshape(bsz, nh, nc, GLA_CHUNK, dk)
    vc = v.astype(jnp.float32).reshape(bsz, nh, nc, GLA_CHUNK, dv)
    gc = g.reshape(bsz, nh, nc, GLA_CHUNK, dk)
    bcum = jnp.cumsum(gc, axis=3)
    btot = bcum[..., -1:, :]
    q_in = qc * jnp.exp(bcum)
    k_in = kc * jnp.exp(-bcum)
    k_st = kc * jnp.exp(btot - bcum)
    mask = jnp.tril(jnp.ones((GLA_CHUNK, GLA_CHUNK), dtype=bool))
    scores = jnp.einsum('bhnid,bhnjd->bhnij', q_in, k_in)
    scores = jnp.where(mask, scores, 0.0)
    o_intra = jnp.einsum('bhnij,bhnje->bhnie', scores, vc)
    decay = jnp.exp(btot[..., 0, :])

    def step(state, xs):
        q_n, k_n, v_n, dec_n = xs
        o_n = jnp.einsum('bhcd,bhde->bhce', q_n, state)
        state = dec_n[..., None] * state + jnp.einsum('bhcd,bhce->bhde', k_n, v_n)
        return state, o_n

    xs = (jnp.moveaxis(q_in, 2, 0), jnp.moveaxis(k_st, 2, 0),
          jnp.moveaxis(vc, 2, 0), jnp.moveaxis(decay, 2, 0))
    state0 = jnp.zeros((bsz, nh, dk, dv), jnp.float32)
    _, o_inter = lax.scan(step, state0, xs)
    o_inter = jnp.moveaxis(o_inter, 0, 2)
    return (o_intra + o_inter).reshape(bsz, nh, seq, dv)


def even_layer(x, norm_pre, norm_post, w_in, conv_w, conv_b, gate_w, gate_b, lam, sc_w, w_out):
    h = rmsnorm(x, norm_pre)
    proj = h @ w_in
    xa, za, xb, gb, gc, zb = jnp.split(
        proj, [RG_WIDTH, 2 * RG_WIDTH, 2 * RG_WIDTH + SC_WIDTH,
               2 * RG_WIDTH + 2 * SC_WIDTH, 2 * RG_WIDTH + 3 * SC_WIDTH], axis=-1)
    ua = centred_dwconv(xa, conv_w, RG_CONV_LEFT) + conv_b
    ya = rg_lru(ua, gate_w[0], gate_b[0], lam[0], False) + rg_lru(ua, gate_w[1], gate_b[1], lam[1], True)
    ya = ya * jax.nn.silu(za.astype(jnp.float32))
    yb = gb * centred_dwconv(gc * xb, sc_w, SC_CONV_LEFT)
    yb = yb * jax.nn.silu(zb)
    y = jnp.concatenate([ya, yb.astype(jnp.float32)], axis=-1) @ w_out
    return (x + rmsnorm(y, norm_post)).astype(x.dtype)


def odd_layer(x, norm_pre, norm_post, w_in, w_gate_lr, b_gate, head_norm_g, w_out):
    bsz, seq, _ = x.shape
    h = rmsnorm(x, norm_pre)
    proj = h @ w_in
    q, k, v, r, lr = jnp.split(
        proj, [GLA_KEY, 2 * GLA_KEY, 2 * GLA_KEY + GLA_VAL, 2 * GLA_KEY + 2 * GLA_VAL], axis=-1)
    lr = lr.reshape(bsz, seq, 2, GLA_RANK).astype(jnp.float32)
    z = jnp.einsum('bsdr,drk->dbsk', lr, w_gate_lr.astype(jnp.float32)) + b_gate[:, None, None].astype(jnp.float32)
    log_alpha = jax.nn.log_sigmoid(z) / GLA_NORMALIZER

    def heads(t, dh):
        return jnp.transpose(t.reshape(bsz, seq, GLA_HEADS, dh), (0, 2, 1, 3))

    qh, kh, vh = heads(q, GLA_DK), heads(k, GLA_DK), heads(v, GLA_DV)
    g_f, g_b = heads(log_alpha[0], GLA_DK), heads(log_alpha[1], GLA_DK)
    o_f = gla_chunked(qh, kh, vh, g_f)
    o_b = jnp.flip(gla_chunked(jnp.flip(qh, 2), jnp.flip(kh, 2), jnp.flip(vh, 2), jnp.flip(g_b, 2)), 2)
    o = rmsnorm(o_f + o_b, head_norm_g)
    o = jnp.transpose(o, (0, 2, 1, 3)).reshape(bsz, seq, GLA_VAL)
    y = (o * jax.nn.silu(r.astype(jnp.float32))) @ w_out
    return (x + rmsnorm(y, norm_post)).astype(x.dtype)


def _fwd_setup_inputs(seed: int = 0) -> dict:
    key = jax.random.key(seed)
    ks = jax.random.split(key, 20)
    nrm = jax.random.normal
    f32 = jnp.float32
    lam_a = jax.random.uniform(ks[8], (N_EVEN, 2, RG_WIDTH), f32, minval=0.9, maxval=0.999)
    lam_s = lam_a ** (1.0 / RG_C)
    rg_lambda = jnp.log(lam_s) - jnp.log1p(-lam_s)
    return {
        'x': nrm(ks[0], (BATCH, SEQ, D_MODEL), f32),
        'even_norm_pre': 1.0 + 0.05 * nrm(ks[1], (N_EVEN, D_MODEL), f32),
        'even_norm_post': 1.0 + 0.05 * nrm(ks[2], (N_EVEN, D_MODEL), f32),
        'even_w_in': nrm(ks[3], (N_EVEN, D_MODEL, EVEN_IN), f32) * D_MODEL ** -0.5,
        'rg_conv_w': nrm(ks[4], (N_EVEN, RG_CONV, RG_WIDTH), f32) * RG_CONV ** -0.5,
        'rg_conv_b': 0.01 * nrm(ks[5], (N_EVEN, RG_WIDTH), f32),
        'rg_gate_w': nrm(ks[6], (N_EVEN, 2, 2, RG_HEADS, RG_HEAD_DIM, RG_HEAD_DIM), f32) * RG_HEAD_DIM ** -0.5,
        'rg_gate_b': 0.01 * nrm(ks[7], (N_EVEN, 2, 2, RG_HEADS, RG_HEAD_DIM), f32),
        'rg_lambda': rg_lambda,
        'sc_conv_w': nrm(ks[9], (N_EVEN, SC_CONV, SC_WIDTH), f32) * SC_CONV ** -0.5,
        'even_w_out': nrm(ks[10], (N_EVEN, EVEN_MIX, D_MODEL), f32) * EVEN_MIX ** -0.5,
        'odd_norm_pre': 1.0 + 0.05 * nrm(ks[11], (N_ODD, D_MODEL), f32),
        'odd_norm_post': 1.0 + 0.05 * nrm(ks[12], (N_ODD, D_MODEL), f32),
        'odd_w_in': nrm(ks[13], (N_ODD, D_MODEL, ODD_IN), f32) * D_MODEL ** -0.5,
        'gla_w_gate_lr': nrm(ks[14], (N_ODD, 2, GLA_RANK, GLA_KEY), f32) * GLA_RANK ** -0.5,
        'gla_b_gate': 1.0 + 0.3 * nrm(ks[15], (N_ODD, 2, GLA_KEY), f32),
        'gla_norm_g': 1.0 + 0.05 * nrm(ks[16], (N_ODD, GLA_DV), f32),
        'odd_w_out': nrm(ks[17], (N_ODD, GLA_VAL, D_MODEL), f32) * GLA_VAL ** -0.5,
    }


def _fwd_reference(x, even_norm_pre, even_norm_post, even_w_in, rg_conv_w, rg_conv_b, rg_gate_w,
              rg_gate_b, rg_lambda, sc_conv_w, even_w_out, odd_norm_pre, odd_norm_post,
              odd_w_in, gla_w_gate_lr, gla_b_gate, gla_norm_g, odd_w_out):
    for layer in range(DEPTH):
        j = layer // 2
        if layer % 2 == 0:
            x = even_layer(x, even_norm_pre[j], even_norm_post[j], even_w_in[j], rg_conv_w[j],
                           rg_conv_b[j], rg_gate_w[j], rg_gate_b[j], rg_lambda[j],
                           sc_conv_w[j], even_w_out[j])
        else:
            x = odd_layer(x, odd_norm_pre[j], odd_norm_post[j], odd_w_in[j], gla_w_gate_lr[j],
                          gla_b_gate[j], gla_norm_g[j], odd_w_out[j])
    return x


import jax as _jax
import jax.numpy as _jnp

TWIN_FORMAT = 'train_step'
FWD_PARAMS = ['x', 'even_norm_pre', 'even_norm_post', 'even_w_in', 'rg_conv_w', 'rg_conv_b', 'rg_gate_w', 'rg_gate_b', 'rg_lambda', 'sc_conv_w', 'even_w_out', 'odd_norm_pre', 'odd_norm_post', 'odd_w_in', 'gla_w_gate_lr', 'gla_b_gate', 'gla_norm_g', 'odd_w_out']
TWIN_WEIGHTS = ['even_norm_pre', 'even_norm_post', 'even_w_in', 'rg_conv_w', 'rg_conv_b', 'rg_gate_w', 'rg_gate_b', 'rg_lambda', 'sc_conv_w', 'even_w_out', 'odd_norm_pre', 'odd_norm_post', 'odd_w_in', 'gla_w_gate_lr', 'gla_b_gate', 'gla_norm_g', 'odd_w_out']
TWIN_DIFF_INPUT = 'x'
TWIN_INPUTS = ['x', 'even_norm_pre', 'even_norm_post', 'even_w_in', 'rg_conv_w', 'rg_conv_b', 'rg_gate_w', 'rg_gate_b', 'rg_lambda', 'sc_conv_w', 'even_w_out', 'odd_norm_pre', 'odd_norm_post', 'odd_w_in', 'gla_w_gate_lr', 'gla_b_gate', 'gla_norm_g', 'odd_w_out', 'loss_target', 'm_even_norm_pre', 'm_even_norm_post', 'm_even_w_in', 'm_rg_conv_w', 'm_rg_conv_b', 'm_rg_gate_w', 'm_rg_gate_b', 'm_rg_lambda', 'm_sc_conv_w', 'm_even_w_out', 'm_odd_norm_pre', 'm_odd_norm_post', 'm_odd_w_in', 'm_gla_w_gate_lr', 'm_gla_b_gate', 'm_gla_norm_g', 'm_odd_w_out', 'v_even_norm_pre', 'v_even_norm_post', 'v_even_w_in', 'v_rg_conv_w', 'v_rg_conv_b', 'v_rg_gate_w', 'v_rg_gate_b', 'v_rg_lambda', 'v_sc_conv_w', 'v_even_w_out', 'v_odd_norm_pre', 'v_odd_norm_post', 'v_odd_w_in', 'v_gla_w_gate_lr', 'v_gla_b_gate', 'v_gla_norm_g', 'v_odd_w_out']
TWIN_OUTPUTS = ['loss', 'grad_x', 'grad_even_norm_pre', 'grad_even_norm_post', 'grad_even_w_in', 'grad_rg_conv_w', 'grad_rg_conv_b', 'grad_rg_gate_w', 'grad_rg_gate_b', 'grad_rg_lambda', 'grad_sc_conv_w', 'grad_even_w_out', 'grad_odd_norm_pre', 'grad_odd_norm_post', 'grad_odd_w_in', 'grad_gla_w_gate_lr', 'grad_gla_b_gate', 'grad_gla_norm_g', 'grad_odd_w_out', 'delta_even_norm_pre', 'delta_even_norm_post', 'delta_even_w_in', 'delta_rg_conv_w', 'delta_rg_conv_b', 'delta_rg_gate_w', 'delta_rg_gate_b', 'delta_rg_lambda', 'delta_sc_conv_w', 'delta_even_w_out', 'delta_odd_norm_pre', 'delta_odd_norm_post', 'delta_odd_w_in', 'delta_gla_w_gate_lr', 'delta_gla_b_gate', 'delta_gla_norm_g', 'delta_odd_w_out', 'new_m_even_norm_pre', 'new_m_even_norm_post', 'new_m_even_w_in', 'new_m_rg_conv_w', 'new_m_rg_conv_b', 'new_m_rg_gate_w', 'new_m_rg_gate_b', 'new_m_rg_lambda', 'new_m_sc_conv_w', 'new_m_even_w_out', 'new_m_odd_norm_pre', 'new_m_odd_norm_post', 'new_m_odd_w_in', 'new_m_gla_w_gate_lr', 'new_m_gla_b_gate', 'new_m_gla_norm_g', 'new_m_odd_w_out', 'new_v_even_norm_pre', 'new_v_even_norm_post', 'new_v_even_w_in', 'new_v_rg_conv_w', 'new_v_rg_conv_b', 'new_v_rg_gate_w', 'new_v_rg_gate_b', 'new_v_rg_lambda', 'new_v_sc_conv_w', 'new_v_even_w_out', 'new_v_odd_norm_pre', 'new_v_odd_norm_post', 'new_v_odd_w_in', 'new_v_gla_w_gate_lr', 'new_v_gla_b_gate', 'new_v_gla_norm_g', 'new_v_odd_w_out']
TWIN_LEAF_KINDS = {'loss': 'loss', 'grad_x': 'grad_x', 'grad_even_norm_pre': 'grad_w', 'grad_even_norm_post': 'grad_w', 'grad_even_w_in': 'grad_w', 'grad_rg_conv_w': 'grad_w', 'grad_rg_conv_b': 'grad_w', 'grad_rg_gate_w': 'grad_w', 'grad_rg_gate_b': 'grad_w', 'grad_rg_lambda': 'grad_w', 'grad_sc_conv_w': 'grad_w', 'grad_even_w_out': 'grad_w', 'grad_odd_norm_pre': 'grad_w', 'grad_odd_norm_post': 'grad_w', 'grad_odd_w_in': 'grad_w', 'grad_gla_w_gate_lr': 'grad_w', 'grad_gla_b_gate': 'grad_w', 'grad_gla_norm_g': 'grad_w', 'grad_odd_w_out': 'grad_w', 'delta_even_norm_pre': 'delta_w', 'delta_even_norm_post': 'delta_w', 'delta_even_w_in': 'delta_w', 'delta_rg_conv_w': 'delta_w', 'delta_rg_conv_b': 'delta_w', 'delta_rg_gate_w': 'delta_w', 'delta_rg_gate_b': 'delta_w', 'delta_rg_lambda': 'delta_w', 'delta_sc_conv_w': 'delta_w', 'delta_even_w_out': 'delta_w', 'delta_odd_norm_pre': 'delta_w', 'delta_odd_norm_post': 'delta_w', 'delta_odd_w_in': 'delta_w', 'delta_gla_w_gate_lr': 'delta_w', 'delta_gla_b_gate': 'delta_w', 'delta_gla_norm_g': 'delta_w', 'delta_odd_w_out': 'delta_w', 'new_m_even_norm_pre': 'new_m', 'new_m_even_norm_post': 'new_m', 'new_m_even_w_in': 'new_m', 'new_m_rg_conv_w': 'new_m', 'new_m_rg_conv_b': 'new_m', 'new_m_rg_gate_w': 'new_m', 'new_m_rg_gate_b': 'new_m', 'new_m_rg_lambda': 'new_m', 'new_m_sc_conv_w': 'new_m', 'new_m_even_w_out': 'new_m', 'new_m_odd_norm_pre': 'new_m', 'new_m_odd_norm_post': 'new_m', 'new_m_odd_w_in': 'new_m', 'new_m_gla_w_gate_lr': 'new_m', 'new_m_gla_b_gate': 'new_m', 'new_m_gla_norm_g': 'new_m', 'new_m_odd_w_out': 'new_m', 'new_v_even_norm_pre': 'new_v', 'new_v_even_norm_post': 'new_v', 'new_v_even_w_in': 'new_v', 'new_v_rg_conv_w': 'new_v', 'new_v_rg_conv_b': 'new_v', 'new_v_rg_gate_w': 'new_v', 'new_v_rg_gate_b': 'new_v', 'new_v_rg_lambda': 'new_v', 'new_v_sc_conv_w': 'new_v', 'new_v_even_w_out': 'new_v', 'new_v_odd_norm_pre': 'new_v', 'new_v_odd_norm_post': 'new_v', 'new_v_odd_w_in': 'new_v', 'new_v_gla_w_gate_lr': 'new_v', 'new_v_gla_b_gate': 'new_v', 'new_v_gla_norm_g': 'new_v', 'new_v_odd_w_out': 'new_v'}


def _forward(args):
    return _fwd_reference(*[args[k] for k in FWD_PARAMS])


def _output_shape():
    out = _jax.eval_shape(lambda: _forward(_fwd_setup_inputs(0)))
    return out.shape, out.dtype

N_MICROBATCH = 1
ADAM_LR = 0.001
ADAM_B1 = 0.9
ADAM_B2 = 0.999
ADAM_EPS = 1e-08
ADAM_WD = 0.01
ADAM_STEP = 10
PER_EXAMPLE_BATCH_AXIS = {'x': 0, 'loss_target': 0}
SHARED_INPUTS = []
_WEIGHT_DTYPES = {'even_norm_pre': _jnp.float32, 'even_norm_post': _jnp.float32, 'even_w_in': _jnp.float32, 'rg_conv_w': _jnp.float32, 'rg_conv_b': _jnp.float32, 'rg_gate_w': _jnp.float32, 'rg_gate_b': _jnp.float32, 'rg_lambda': _jnp.float32, 'sc_conv_w': _jnp.float32, 'even_w_out': _jnp.float32, 'odd_norm_pre': _jnp.float32, 'odd_norm_post': _jnp.float32, 'odd_w_in': _jnp.float32, 'gla_w_gate_lr': _jnp.float32, 'gla_b_gate': _jnp.float32, 'gla_norm_g': _jnp.float32, 'odd_w_out': _jnp.float32}
MOMENT_SCALE = {'even_norm_pre': 9.756579e-01, 'even_norm_post': 3.166879e+01, 'even_w_in': 3.923205e-01, 'rg_conv_w': 4.592120e-01, 'rg_conv_b': 8.539120e+00, 'rg_gate_w': 1.589748e-01, 'rg_gate_b': 9.137341e-02, 'rg_lambda': 1.518936e-01, 'sc_conv_w': 4.148638e-01, 'even_w_out': 5.827053e-01, 'odd_norm_pre': 6.797080e-01, 'odd_norm_post': 3.212477e+01, 'odd_w_in': 3.895759e-01, 'gla_w_gate_lr': 4.527046e-02, 'gla_b_gate': 1.875707e-01, 'gla_norm_g': 6.553772e-01, 'odd_w_out': 3.423848e-01}


def _to_microbatches(a, axis):
    t = _jnp.moveaxis(a, axis, 0)
    t = t.reshape((N_MICROBATCH, t.shape[0] // N_MICROBATCH) + t.shape[1:])
    return _jnp.moveaxis(t, 1, axis + 1)


def setup_inputs(seed: int = 0) -> dict:
    inp = _fwd_setup_inputs(seed)
    key = _jax.random.fold_in(_jax.random.key(seed), 7919)
    shape, _ = _output_shape()
    out = dict(inp)
    out["loss_target"] = _jax.random.normal(_jax.random.fold_in(key, 0), shape, _jnp.float32)
    for i, name in enumerate(TWIN_WEIGHTS):
        w = inp[name].astype(_jnp.float32)
        if MOMENT_SCALE is None:
            s = _jnp.sqrt(_jnp.mean(_jnp.square(w)) + 1e-30)
        else:
            s = MOMENT_SCALE[name]
        km, kv = _jax.random.split(_jax.random.fold_in(key, i + 1))
        out[name] = w
        out["m_" + name] = s * _jax.random.normal(km, w.shape, _jnp.float32)
        out["v_" + name] = (s * s) * _jax.random.uniform(kv, w.shape, _jnp.float32, 0.5, 1.5)
    if N_MICROBATCH > 1:
        for name, axis in PER_EXAMPLE_BATCH_AXIS.items():
            out[name] = _to_microbatches(out[name], axis)
    return {'x': out['x'], 'even_norm_pre': out['even_norm_pre'], 'even_norm_post': out['even_norm_post'], 'even_w_in': out['even_w_in'], 'rg_conv_w': out['rg_conv_w'], 'rg_conv_b': out['rg_conv_b'], 'rg_gate_w': out['rg_gate_w'], 'rg_gate_b': out['rg_gate_b'], 'rg_lambda': out['rg_lambda'], 'sc_conv_w': out['sc_conv_w'], 'even_w_out': out['even_w_out'], 'odd_norm_pre': out['odd_norm_pre'], 'odd_norm_post': out['odd_norm_post'], 'odd_w_in': out['odd_w_in'], 'gla_w_gate_lr': out['gla_w_gate_lr'], 'gla_b_gate': out['gla_b_gate'], 'gla_norm_g': out['gla_norm_g'], 'odd_w_out': out['odd_w_out'], 'loss_target': out['loss_target'], 'm_even_norm_pre': out['m_even_norm_pre'], 'm_even_norm_post': out['m_even_norm_post'], 'm_even_w_in': out['m_even_w_in'], 'm_rg_conv_w': out['m_rg_conv_w'], 'm_rg_conv_b': out['m_rg_conv_b'], 'm_rg_gate_w': out['m_rg_gate_w'], 'm_rg_gate_b': out['m_rg_gate_b'], 'm_rg_lambda': out['m_rg_lambda'], 'm_sc_conv_w': out['m_sc_conv_w'], 'm_even_w_out': out['m_even_w_out'], 'm_odd_norm_pre': out['m_odd_norm_pre'], 'm_odd_norm_post': out['m_odd_norm_post'], 'm_odd_w_in': out['m_odd_w_in'], 'm_gla_w_gate_lr': out['m_gla_w_gate_lr'], 'm_gla_b_gate': out['m_gla_b_gate'], 'm_gla_norm_g': out['m_gla_norm_g'], 'm_odd_w_out': out['m_odd_w_out'], 'v_even_norm_pre': out['v_even_norm_pre'], 'v_even_norm_post': out['v_even_norm_post'], 'v_even_w_in': out['v_even_w_in'], 'v_rg_conv_w': out['v_rg_conv_w'], 'v_rg_conv_b': out['v_rg_conv_b'], 'v_rg_gate_w': out['v_rg_gate_w'], 'v_rg_gate_b': out['v_rg_gate_b'], 'v_rg_lambda': out['v_rg_lambda'], 'v_sc_conv_w': out['v_sc_conv_w'], 'v_even_w_out': out['v_even_w_out'], 'v_odd_norm_pre': out['v_odd_norm_pre'], 'v_odd_norm_post': out['v_odd_norm_post'], 'v_odd_w_in': out['v_odd_w_in'], 'v_gla_w_gate_lr': out['v_gla_w_gate_lr'], 'v_gla_b_gate': out['v_gla_b_gate'], 'v_gla_norm_g': out['v_gla_norm_g'], 'v_odd_w_out': out['v_odd_w_out']}


def _loss(weights, diff, rest, loss_target):
    with _jax.named_scope("forward"):
        args = {**rest, TWIN_DIFF_INPUT: diff, **{k: w.astype(_WEIGHT_DTYPES[k]) for k, w in weights.items()}}
        y = _forward(args)
    with _jax.named_scope("loss_head"):
        err = _jnp.square(y.astype(_jnp.float32) - loss_target)
        return 0.5 * _jnp.sum(_jnp.mean(err, axis=-1)) if err.ndim else 0.5 * err


def _adamw(w, g, m, v):
    m = ADAM_B1 * m + (1.0 - ADAM_B1) * g
    v = ADAM_B2 * v + (1.0 - ADAM_B2) * _jnp.square(g)
    m_hat = m / (1.0 - ADAM_B1 ** ADAM_STEP)
    v_hat = v / (1.0 - ADAM_B2 ** ADAM_STEP)
    delta = -ADAM_LR * (m_hat / (_jnp.sqrt(v_hat) + ADAM_EPS) + ADAM_WD * w)
    return delta, m, v


def reference(x, even_norm_pre, even_norm_post, even_w_in, rg_conv_w, rg_conv_b, rg_gate_w, rg_gate_b, rg_lambda, sc_conv_w, even_w_out, odd_norm_pre, odd_norm_post, odd_w_in, gla_w_gate_lr, gla_b_gate, gla_norm_g, odd_w_out, loss_target, m_even_norm_pre, m_even_norm_post, m_even_w_in, m_rg_conv_w, m_rg_conv_b, m_rg_gate_w, m_rg_gate_b, m_rg_lambda, m_sc_conv_w, m_even_w_out, m_odd_norm_pre, m_odd_norm_post, m_odd_w_in, m_gla_w_gate_lr, m_gla_b_gate, m_gla_norm_g, m_odd_w_out, v_even_norm_pre, v_even_norm_post, v_even_w_in, v_rg_conv_w, v_rg_conv_b, v_rg_gate_w, v_rg_gate_b, v_rg_lambda, v_sc_conv_w, v_even_w_out, v_odd_norm_pre, v_odd_norm_post, v_odd_w_in, v_gla_w_gate_lr, v_gla_b_gate, v_gla_norm_g, v_odd_w_out):
    given = dict(x=x, even_norm_pre=even_norm_pre, even_norm_post=even_norm_post, even_w_in=even_w_in, rg_conv_w=rg_conv_w, rg_conv_b=rg_conv_b, rg_gate_w=rg_gate_w, rg_gate_b=rg_gate_b, rg_lambda=rg_lambda, sc_conv_w=sc_conv_w, even_w_out=even_w_out, odd_norm_pre=odd_norm_pre, odd_norm_post=odd_norm_post, odd_w_in=odd_w_in, gla_w_gate_lr=gla_w_gate_lr, gla_b_gate=gla_b_gate, gla_norm_g=gla_norm_g, odd_w_out=odd_w_out, loss_target=loss_target, m_even_norm_pre=m_even_norm_pre, m_even_norm_post=m_even_norm_post, m_even_w_in=m_even_w_in, m_rg_conv_w=m_rg_conv_w, m_rg_conv_b=m_rg_conv_b, m_rg_gate_w=m_rg_gate_w, m_rg_gate_b=m_rg_gate_b, m_rg_lambda=m_rg_lambda, m_sc_conv_w=m_sc_conv_w, m_even_w_out=m_even_w_out, m_odd_norm_pre=m_odd_norm_pre, m_odd_norm_post=m_odd_norm_post, m_odd_w_in=m_odd_w_in, m_gla_w_gate_lr=m_gla_w_gate_lr, m_gla_b_gate=m_gla_b_gate, m_gla_norm_g=m_gla_norm_g, m_odd_w_out=m_odd_w_out, v_even_norm_pre=v_even_norm_pre, v_even_norm_post=v_even_norm_post, v_even_w_in=v_even_w_in, v_rg_conv_w=v_rg_conv_w, v_rg_conv_b=v_rg_conv_b, v_rg_gate_w=v_rg_gate_w, v_rg_gate_b=v_rg_gate_b, v_rg_lambda=v_rg_lambda, v_sc_conv_w=v_sc_conv_w, v_even_w_out=v_even_w_out, v_odd_norm_pre=v_odd_norm_pre, v_odd_norm_post=v_odd_norm_post, v_odd_w_in=v_odd_w_in, v_gla_w_gate_lr=v_gla_w_gate_lr, v_gla_b_gate=v_gla_b_gate, v_gla_norm_g=v_gla_norm_g, v_odd_w_out=v_odd_w_out)
    weights = {n: given[n] for n in TWIN_WEIGHTS}
    shared = {n: given[n] for n in SHARED_INPUTS}
    per_example = {n: given[n] for n in ['x']}
    grad_fn = _jax.value_and_grad(_loss, argnums=(0, 1))

    def one_microbatch(ex, loss_target):
        ex = dict(ex)
        diff = ex.pop(TWIN_DIFF_INPUT)
        return grad_fn(weights, diff, {**shared, **ex}, loss_target)

    if N_MICROBATCH == 1:
        loss, (grad_w, grad_x) = one_microbatch(per_example, given["loss_target"])
    else:
        def body(carry, xs):
            loss_sum, grad_sum = carry
            l_k, (gw_k, gx_k) = one_microbatch(xs[0], xs[1])
            with _jax.named_scope("update"):
                return (loss_sum + l_k, _jax.tree.map(_jnp.add, grad_sum, gw_k)), gx_k

        init = (_jnp.zeros((), _jnp.float32), _jax.tree.map(_jnp.zeros_like, weights))
        (loss, grad_w), grad_x = _jax.lax.scan(body, init, (per_example, given["loss_target"]))
    with _jax.named_scope("update"):
        delta_w, new_m, new_v = {}, {}, {}
        for n in TWIN_WEIGHTS:
            delta_w[n], new_m[n], new_v[n] = _adamw(weights[n], grad_w[n], given["m_" + n], given["v_" + n])
    return (loss, grad_x, *[grad_w[n] for n in TWIN_WEIGHTS], *[delta_w[n] for n in TWIN_WEIGHTS],
            *[new_m[n] for n in TWIN_WEIGHTS], *[new_v[n] for n in TWIN_WEIGHTS])
```

```python
import functools

import jax
import jax.numpy as jnp
from jax import lax
from jax.experimental import pallas as pl
from jax.experimental.pallas import tpu as pltpu

F32 = jnp.float32
BF16 = jnp.bfloat16
MESH = pl.DeviceIdType.MESH

D_MODEL = 1024
NORM_EPS = 1e-6
RG_HEADS = 8
RG_HEAD_DIM = 128
RG_C = 8.0
EVEN_IN = 6144
ODD_IN = 3104
ODD_IN_PAD = 3200
GLA_HEADS = 4
GLA_DK = 128
GLA_DV = 256
GLA_RANK = 16
GLA_NORMALIZER = 16.0
GLA_CHUNK = 64
LR_COL = 3072

ADAM_LR = 0.001
ADAM_B1 = 0.9
ADAM_B2 = 0.999
ADAM_EPS = 1e-08
ADAM_WD = 0.01
ADAM_STEP = 10

SUBLANES = 8
LANES = 128
VMEM_LIMIT = 56 * 2 ** 20

ROW_TILE = 512
SCAN_TILE = 256
GLA_BLOCK = 1024
MIX_COLS = 512


def _params(*sem):
    return pltpu.CompilerParams(dimension_semantics=sem, vmem_limit_bytes=VMEM_LIMIT)


def _full(shape):
    n = len(shape)
    return pl.BlockSpec(shape, lambda *_: (0,) * n)


def _sigmoid(x):
    return 1.0 / (1.0 + jnp.exp(-x))


def _softplus(x):
    return jnp.maximum(x, 0.0) + jnp.log(1.0 + jnp.exp(-jnp.abs(x)))


def _one_minus_exp(x):
    series = -x * (1.0 + x * (1.0 / 2.0) * (1.0 + x * (1.0 / 3.0) * (1.0 + x * (1.0 / 4.0) * (
        1.0 + x * (1.0 / 5.0) * (1.0 + x * (1.0 / 6.0))))))
    return jnp.where(x > -0.25, series, 1.0 - jnp.exp(x))


def _dot(a, b):
    return jnp.dot(a, b, preferred_element_type=F32)


def _dot_nt(a, b):
    return lax.dot_general(a, b, (((1,), (1,)), ((), ())), preferred_element_type=F32)


def _dot_tn(a, b):
    return lax.dot_general(a, b, (((0,), (0,)), ((), ())), preferred_element_type=F32)


def _bdot(a, b, ca, cb):
    return lax.dot_general(a, b, (((ca,), (cb,)), ((0,), (0,))), preferred_element_type=F32)


def _halo_specs(rows, cols, col_block, n_row_tiles, tix):
    per = rows // SUBLANES
    last = n_row_tiles * per - 1

    def split(args):
        if len(args) == 2:
            return tix(args[1]), col_block + args[0]
        return tix(args[0]), col_block

    def prev(*args):
        t, c = split(args)
        return (jnp.maximum(t * per - 1, 0), c)

    def main(*args):
        return split(args)

    def nxt(*args):
        t, c = split(args)
        return (jnp.minimum((t + 1) * per, last), c)

    return [pl.BlockSpec((SUBLANES, cols), prev), pl.BlockSpec((rows, cols), main),
            pl.BlockSpec((SUBLANES, cols), nxt)]


def _extend(prev_ref, main_ref, next_ref, is_first, is_last):
    p = jnp.where(is_first, 0.0, prev_ref[...])
    n = jnp.where(is_last, 0.0, next_ref[...])
    return jnp.concatenate([p, main_ref[...], n], axis=0)


def _shifted(ext, offset, rows):
    if offset == 0:
        return ext[SUBLANES:SUBLANES + rows]
    n = ext.shape[0]
    return pltpu.roll(ext, (-offset) % n, 0)[SUBLANES:SUBLANES + rows]


def _conv(ext, w, left, rows):
    out = None
    for k in range(w.shape[0]):
        term = _shifted(ext, k - left, rows) * w[k:k + 1]
        out = term if out is None else out + term
    return out


def _conv_transpose(ext, w, left, rows):
    out = None
    for k in range(w.shape[0]):
        term = _shifted(ext, left - k, rows) * w[k:k + 1]
        out = term if out is None else out + term
    return out


def _colsum(x):
    return jnp.sum(x, axis=0, keepdims=True)


def norm_matmul(x, gain, w, name):
    rows, d = x.shape
    n_col_tiles, _, tn = w.shape
    tm = min(ROW_TILE, rows)

    def body(x_ref, g_ref, w_ref, proj_ref, h_ref, h_scr):
        @pl.when(pl.program_id(1) == 0)
        def _():
            xv = x_ref[...]
            rstd = lax.rsqrt(jnp.mean(xv * xv, axis=-1, keepdims=True) + NORM_EPS)
            hv = (xv * rstd * g_ref[...]).astype(BF16)
            h_scr[...] = hv
            h_ref[...] = hv

        proj_ref[...] = _dot(h_scr[...], w_ref[0])

    return pl.pallas_call(
        body, name=name,
        out_shape=(jax.ShapeDtypeStruct((rows, n_col_tiles * tn), F32), jax.ShapeDtypeStruct((rows, d), BF16)),
        grid=(rows // tm, n_col_tiles),
        in_specs=[pl.BlockSpec((tm, d), lambda i, j: (i, 0)), _full((1, d)),
                  pl.BlockSpec((1, d, tn), lambda i, j: (j, 0, 0))],
        out_specs=(pl.BlockSpec((tm, tn), lambda i, j: (i, j)), pl.BlockSpec((tm, d), lambda i, j: (i, 0))),
        scratch_shapes=[pltpu.VMEM((tm, d), BF16)],
        compiler_params=_params("parallel", "arbitrary"),
    )(x, gain, w)


def inproj_bwd(dproj, w, x, gain, dres, name):
    rows, d = x.shape
    n_col_tiles, _, tn = w.shape
    tm = min(ROW_TILE, rows)

    def body(dp_ref, w_ref, x_ref, g_ref, dres_ref, dx_ref, dg_ref, acc):
        i, j = pl.program_id(0), pl.program_id(1)
        part = _dot_nt(dp_ref[...], w_ref[0])

        @pl.when(j == 0)
        def _():
            acc[...] = part

        @pl.when(j > 0)
        def _():
            acc[...] += part

        @pl.when(j == n_col_tiles - 1)
        def _():
            dh = acc[...]
            xv = x_ref[...]
            rstd = lax.rsqrt(jnp.mean(xv * xv, axis=-1, keepdims=True) + NORM_EPS)
            xhat = xv * rstd
            dxn = dh * g_ref[...]
            dx_ref[...] = dres_ref[...] + rstd * (dxn - xhat * jnp.mean(dxn * xhat, axis=-1, keepdims=True))
            dg = _colsum(dh * xhat)

            @pl.when(i == 0)
            def _():
                dg_ref[...] = dg

            @pl.when(i > 0)
            def _():
                dg_ref[...] += dg

    return pl.pallas_call(
        body, name=name,
        out_shape=(jax.ShapeDtypeStruct((rows, d), F32), jax.ShapeDtypeStruct((1, d), F32)),
        grid=(rows // tm, n_col_tiles),
        in_specs=[pl.BlockSpec((tm, tn), lambda i, j: (i, j)), pl.BlockSpec((1, d, tn), lambda i, j: (j, 0, 0)),
                  pl.BlockSpec((tm, d), lambda i, j: (i, 0)), _full((1, d)),
                  pl.BlockSpec((tm, d), lambda i, j: (i, 0))],
        out_specs=(pl.BlockSpec((tm, d), lambda i, j: (i, 0)), _full((1, d))),
        scratch_shapes=[pltpu.VMEM((tm, d), F32)],
        compiler_params=_params("arbitrary", "arbitrary"),
    )(dproj, w, x, gain, dres)


def matmul_dw(a, b, bn, name):
    rows, m = a.shape
    n = b.shape[1]
    tk = min(ROW_TILE, rows)
    steps = rows // tk

    def body(a_ref, b_ref, o_ref):
        part = _dot_tn(a_ref[...], b_ref[...])

        @pl.when(pl.program_id(1) == 0)
        def _():
            o_ref[0] = part

        @pl.when(pl.program_id(1) > 0)
        def _():
            o_ref[0] += part

    return pl.pallas_call(
        body, name=name,
        out_shape=jax.ShapeDtypeStruct((n // bn, m, bn), F32),
        grid=(n // bn, steps),
        in_specs=[pl.BlockSpec((tk, m), lambda j, k: (k, 0)), pl.BlockSpec((tk, bn), lambda j, k: (k, j))],
        out_specs=pl.BlockSpec((1, m, bn), lambda j, k: (j, 0, 0)),
        compiler_params=_params("parallel", "arbitrary"),
    )(a, b)


def _scan(a, b, reverse):
    n = a.shape[0]
    row = lax.broadcasted_iota(jnp.int32, a.shape, 0)
    s = 1
    while s < n:
        if reverse:
            a_s, b_s, valid = pltpu.roll(a, n - s, 0), pltpu.roll(b, n - s, 0), row < n - s
        else:
            a_s, b_s, valid = pltpu.roll(a, s, 0), pltpu.roll(b, s, 0), row >= s
        b = jnp.where(valid, a * b_s + b, b)
        a = jnp.where(valid, a * a_s, a)
        s *= 2
    return a, b


def _rg_gates(ua, gw_ref, gb, lam):
    ub = ua.astype(BF16)
    pre_r, pre_i = [], []
    for h in range(RG_HEADS):
        z = _dot(ub[:, h * RG_HEAD_DIM:(h + 1) * RG_HEAD_DIM], gw_ref[h])
        pre_r.append(z[:, :RG_HEAD_DIM])
        pre_i.append(z[:, RG_HEAD_DIM:])
    r = _sigmoid(jnp.concatenate(pre_r, axis=1) + gb[0:1])
    i = _sigmoid(jnp.concatenate(pre_i, axis=1) + gb[1:2])
    sp = _softplus(-lam)
    log_a = -RG_C * r * sp
    a = jnp.exp(log_a)
    mult = jnp.sqrt(_one_minus_exp(2.0 * log_a))
    return r, i, sp, a, mult


def _rg_weight_specs():
    return [_full((4, D_MODEL)), _full((1, D_MODEL)), _full((RG_HEADS, RG_HEAD_DIM, 2 * RG_HEAD_DIM)),
            _full((2, D_MODEL)), _full((1, D_MODEL))]


def rglru_fwd(proj, conv_w, conv_b, gate_w, gate_b, lam, reverse, name):
    rows_total = proj.shape[0]
    rows = min(SCAN_TILE, rows_total)
    n_tiles = rows_total // rows
    tix = (lambda i: n_tiles - 1 - i) if reverse else (lambda i: i)

    def body(xp, xm, xn, cw_ref, cb_ref, gw_ref, gb_ref, lam_ref, h_ref, carry):
        i = pl.program_id(0)
        t = tix(i)
        ext = _extend(xp, xm, xn, t == 0, t == n_tiles - 1)
        ua = _conv(ext, cw_ref[...], 2, rows) + cb_ref[...]
        _, gi, _, a, mult = _rg_gates(ua, gw_ref, gb_ref[...], lam_ref[...])
        b = mult * (gi * ua)
        a_cum, h0 = _scan(a, b, reverse)

        @pl.when(i == 0)
        def _():
            carry[...] = jnp.zeros_like(carry)

        h = a_cum * carry[0:1] + h0
        h_ref[...] = h
        edge = h[0:1] if reverse else h[rows - 1:rows]
        carry[...] = jnp.broadcast_to(edge, carry.shape)

    return pl.pallas_call(
        body, name=name,
        out_shape=jax.ShapeDtypeStruct((rows_total, D_MODEL), F32),
        grid=(n_tiles,),
        in_specs=_halo_specs(rows, D_MODEL, 0, n_tiles, tix) + _rg_weight_specs(),
        out_specs=pl.BlockSpec((rows, D_MODEL), lambda i: (tix(i), 0)),
        scratch_shapes=[pltpu.VMEM((SUBLANES, D_MODEL), F32)],
        compiler_params=_params("arbitrary"),
    )(proj, proj, proj, conv_w, conv_b, gate_w, gate_b, lam)


def rglru_bwd(proj, dycat, h_dir, conv_w, conv_b, gate_w, gate_b, lam, reverse, name):
    rows_total = proj.shape[0]
    rows = min(SCAN_TILE, rows_total)
    n_tiles = rows_total // rows
    tix = (lambda i: i) if reverse else (lambda i: n_tiles - 1 - i)
    za_block = 1

    def body(xp, xm, xn, za_ref, dya_ref, hp, hm, hn, cw_ref, cb_ref, gw_ref, gb_ref, lam_ref,
             dua_ref, dgw_ref, dgb_ref, dlam_ref, carry):
        step = pl.program_id(0)
        t = tix(step)
        first, last = t == 0, t == n_tiles - 1
        ext = _extend(xp, xm, xn, first, last)
        ua = _conv(ext, cw_ref[...], 2, rows) + cb_ref[...]
        lam_v = lam_ref[...]
        r, gi, sp, a, mult = _rg_gates(ua, gw_ref, gb_ref[...], lam_v)
        za = za_ref[...]
        dh = dya_ref[...] * (za * _sigmoid(za))

        @pl.when(step == 0)
        def _():
            carry[...] = jnp.zeros_like(carry)

        a_cum, mu0 = _scan(a, a * dh, not reverse)
        old = carry[0:1]
        mu = a_cum * old + mu0
        row = lax.broadcasted_iota(jnp.int32, mu.shape, 0)
        if reverse:
            mu_next = jnp.where(row == 0, old, pltpu.roll(mu, 1, 0))
            carry[...] = jnp.broadcast_to(mu[rows - 1:rows], carry.shape)
            h_ext = _extend(hp, hm, hn, first, last)
            h_prev = _shifted(h_ext, 1, rows)
        else:
            mu_next = jnp.where(row == rows - 1, old, pltpu.roll(mu, rows - 1, 0))
            carry[...] = jnp.broadcast_to(mu[0:1], carry.shape)
            h_ext = _extend(hp, hm, hn, first, last)
            h_prev = _shifted(h_ext, -1, rows)
        db = dh + mu_next
        da = db * h_prev
        d_mult = db * (gi * ua)
        di = db * (mult * ua)
        dua = db * (mult * gi)
        dlog_a = da * a - d_mult * (a * a) / mult
        dr = dlog_a * (-RG_C * sp)
        dlam = _colsum(dlog_a * (-RG_C * r)) * (-_sigmoid(-lam_v))
        dpr = dr * (r * (1.0 - r))
        dpi = di * (gi * (1.0 - gi))
        dgb = jnp.concatenate([_colsum(dpr), _colsum(dpi)], axis=0)
        ub = ua.astype(BF16)
        dua_heads, dgw_heads = [], []
        for h in range(RG_HEADS):
            cols = slice(h * RG_HEAD_DIM, (h + 1) * RG_HEAD_DIM)
            dz = jnp.concatenate([dpr[:, cols], dpi[:, cols]], axis=1).astype(BF16)
            dgw_heads.append(_dot_tn(ub[:, cols], dz))
            dua_heads.append(_dot_nt(dz, gw_ref[h]))
        dua_ref[...] = dua + jnp.concatenate(dua_heads, axis=1)

        @pl.when(step == 0)
        def _():
            for h in range(RG_HEADS):
                dgw_ref[h] = dgw_heads[h]
            dgb_ref[...] = dgb
            dlam_ref[...] = dlam

        @pl.when(step > 0)
        def _():
            for h in range(RG_HEADS):
                dgw_ref[h] += dgw_heads[h]
            dgb_ref[...] += dgb
            dlam_ref[...] += dlam

    row_spec = lambda col: pl.BlockSpec((rows, D_MODEL), lambda i: (tix(i), col))
    return pl.pallas_call(
        body, name=name,
        out_shape=(jax.ShapeDtypeStruct((rows_total, D_MODEL), F32),
                   jax.ShapeDtypeStruct((RG_HEADS, RG_HEAD_DIM, 2 * RG_HEAD_DIM), F32),
                   jax.ShapeDtypeStruct((2, D_MODEL), F32), jax.ShapeDtypeStruct((1, D_MODEL), F32)),
        grid=(n_tiles,),
        in_specs=(_halo_specs(rows, D_MODEL, 0, n_tiles, tix) + [row_spec(za_block), row_spec(0)]
                  + _halo_specs(rows, D_MODEL, 0, n_tiles, tix) + _rg_weight_specs()),
        out_specs=(row_spec(0), _full((RG_HEADS, RG_HEAD_DIM, 2 * RG_HEAD_DIM)), _full((2, D_MODEL)),
                   _full((1, D_MODEL))),
        scratch_shapes=[pltpu.VMEM((SUBLANES, D_MODEL), F32)],
        compiler_params=_params("arbitrary"),
    )(proj, proj, proj, proj, dycat, h_dir, h_dir, h_dir, conv_w, conv_b, gate_w, gate_b, lam)


def even_mix_fwd(proj, h_f, h_b, sc_w, name):
    rows_total = proj.shape[0]
    rows = min(SCAN_TILE, rows_total)
    n_tiles = rows_total // rows
    cb = MIX_COLS
    n_cb = D_MODEL // cb
    ident = lambda i: i

    def body(za_ref, hf_ref, hb_ref, xbp, xbm, xbn, gcp, gcm, gcn, gb_ref, zb_ref, w_ref, ya_ref, yb_ref):
        t = pl.program_id(1)
        first, last = t == 0, t == n_tiles - 1
        za = za_ref[...]
        ya_ref[...] = ((hf_ref[...] + hb_ref[...]) * (za * _sigmoid(za))).astype(BF16)
        p_ext = _extend(xbp, xbm, xbn, first, last) * _extend(gcp, gcm, gcn, first, last)
        cv = _conv(p_ext, w_ref[...], 1, rows)
        zb = zb_ref[...]
        yb_ref[...] = (gb_ref[...] * cv * (zb * _sigmoid(zb))).astype(BF16)

    blk = lambda col: pl.BlockSpec((rows, cb), lambda c, i: (i, col * n_cb + c))
    own = pl.BlockSpec((rows, cb), lambda c, i: (i, c))
    ya, yb = pl.pallas_call(
        body, name=name,
        out_shape=(jax.ShapeDtypeStruct((rows_total, D_MODEL), BF16),) * 2,
        grid=(n_cb, n_tiles),
        in_specs=([blk(1), own, own] + _halo_specs(rows, cb, 2 * n_cb, n_tiles, ident)
                  + _halo_specs(rows, cb, 4 * n_cb, n_tiles, ident)
                  + [blk(3), blk(5), pl.BlockSpec((3, cb), lambda c, i: (0, c))]),
        out_specs=(own, own),
        compiler_params=_params("parallel", "arbitrary"),
    )(proj, h_f, h_b, proj, proj, proj, proj, proj, proj, proj, proj, sc_w)
    return jnp.concatenate([ya, yb], axis=1)


def even_mix_bwd(proj, dycat, h_f, h_b, dua_f, dua_b, conv_w, sc_w, name):
    rows_total = proj.shape[0]
    rows = min(SCAN_TILE, rows_total)
    n_tiles = rows_total // rows
    cb = MIX_COLS
    n_cb = D_MODEL // cb
    ident = lambda i: i

    def body(xap, xam, xan, za_ref, xbp, xbm, xbn, gbp, gbm, gbn, gcp, gcm, gcn, zbp, zbm, zbn,
             dya_ref, dybp, dybm, dybn, hf_ref, hb_ref, dfp, dfm, dfn, dbp, dbm, dbn, cw_ref, sw_ref,
             dxa_ref, dza_ref, dxb_ref, dgb_ref, dgc_ref, dzb_ref, dcw_ref, dcb_ref, dsw_ref):
        t = pl.program_id(1)
        first, last = t == 0, t == n_tiles - 1
        za = za_ref[...]
        sa = _sigmoid(za)
        dza_ref[...] = (dya_ref[...] * (hf_ref[...] + hb_ref[...]) * (sa * (1.0 + za * (1.0 - sa)))).astype(BF16)
        dua_ext = _extend(dfp, dfm, dfn, first, last) + _extend(dbp, dbm, dbn, first, last)
        cw = cw_ref[...]
        dxa_ref[...] = _conv_transpose(dua_ext, cw, 2, rows).astype(BF16)
        dua = dua_ext[SUBLANES:SUBLANES + rows]
        xa_ext = _extend(xap, xam, xan, first, last)
        dcw = jnp.concatenate([_colsum(dua * _shifted(xa_ext, k - 2, rows)) for k in range(4)], axis=0)
        dcb = _colsum(dua)
        xb_ext = _extend(xbp, xbm, xbn, first, last)
        gc_ext = _extend(gcp, gcm, gcn, first, last)
        p_ext = xb_ext * gc_ext
        zb_ext = _extend(zbp, zbm, zbn, first, last)
        sb_ext = _sigmoid(zb_ext)
        dyb_ext = _extend(dybp, dybm, dybn, first, last)
        gb_ext = _extend(gbp, gbm, gbn, first, last)
        dcv_ext = dyb_ext * gb_ext * (zb_ext * sb_ext)
        sw = sw_ref[...]
        cv = _conv(p_ext, sw, 1, rows)
        mid = slice(SUBLANES, SUBLANES + rows)
        zb, sb, dyb, gb = zb_ext[mid], sb_ext[mid], dyb_ext[mid], gb_ext[mid]
        dgb_ref[...] = (dyb * cv * (zb * sb)).astype(BF16)
        dzb_ref[...] = (dyb * gb * cv * (sb * (1.0 + zb * (1.0 - sb)))).astype(BF16)
        dp = _conv_transpose(dcv_ext, sw, 1, rows)
        dgc_ref[...] = (dp * xb_ext[mid]).astype(BF16)
        dxb_ref[...] = (dp * gc_ext[mid]).astype(BF16)
        dcv = dcv_ext[mid]
        dsw = jnp.concatenate([_colsum(dcv * _shifted(p_ext, k - 1, rows)) for k in range(3)], axis=0)

        @pl.when(t == 0)
        def _():
            dcw_ref[...] = dcw
            dcb_ref[...] = dcb
            dsw_ref[...] = dsw

        @pl.when(t > 0)
        def _():
            dcw_ref[...] += dcw
            dcb_ref[...] += dcb
            dsw_ref[...] += dsw

    blk = lambda col: pl.BlockSpec((rows, cb), lambda c, i: (i, col * n_cb + c))
    halo = lambda col: _halo_specs(rows, cb, col * n_cb, n_tiles, ident)
    own = pl.BlockSpec((rows, cb), lambda c, i: (i, c))
    wspec = lambda k: pl.BlockSpec((k, cb), lambda c, i: (0, c))
    outs = pl.pallas_call(
        body, name=name,
        out_shape=(jax.ShapeDtypeStruct((rows_total, D_MODEL), BF16),) * 6 + (
            jax.ShapeDtypeStruct((4, D_MODEL), F32), jax.ShapeDtypeStruct((1, D_MODEL), F32),
            jax.ShapeDtypeStruct((3, D_MODEL), F32)),
        grid=(n_cb, n_tiles),
        in_specs=(halo(0) + [blk(1)] + halo(2) + halo(3) + halo(4) + halo(5) + [blk(0)] + halo(1)
                  + [own, own] + halo(0) + halo(0) + [wspec(4), wspec(3)]),
        out_specs=(own,) * 6 + (wspec(4), wspec(1), wspec(3)),
        compiler_params=_params("parallel", "arbitrary"),
    )(proj, proj, proj, proj, proj, proj, proj, proj, proj, proj, proj, proj, proj, proj, proj, proj,
      dycat, dycat, dycat, dycat, h_f, h_b, dua_f, dua_f, dua_f, dua_b, dua_b, dua_b, conv_w, sc_w)
    dxa, dza, dxb, dgb, dgc, dzb, dcw, dcb, dsw = outs
    return jnp.concatenate([dxa, dza, dxb, dgb, dgc, dzb], axis=1), dcw, dcb, dsw


def even_out_fwd(ycat, w_out, gain, x, name):
    rows, d = x.shape
    k = ycat.shape[1]
    tm = min(ROW_TILE, rows)

    def body(yc_ref, w_ref, g_ref, x_ref, x1_ref, y_ref):
        y = _dot(yc_ref[...], w_ref[...])
        y_ref[...] = y
        rstd = lax.rsqrt(jnp.mean(y * y, axis=-1, keepdims=True) + NORM_EPS)
        x1_ref[...] = x_ref[...] + y * rstd * g_ref[...]

    row = lambda n: pl.BlockSpec((tm, n), lambda i: (i, 0))
    return pl.pallas_call(
        body, name=name,
        out_shape=(jax.ShapeDtypeStruct((rows, d), F32),) * 2,
        grid=(rows // tm,),
        in_specs=[row(k), _full((k, d)), _full((1, d)), row(d)],
        out_specs=(row(d), row(d)),
        compiler_params=_params("parallel"),
    )(ycat, w_out, gain, x)


def _rmsnorm_bwd(dout, y, gain):
    rstd = lax.rsqrt(jnp.mean(y * y, axis=-1, keepdims=True) + NORM_EPS)
    yhat = y * rstd
    dyn = dout * gain
    dy = rstd * (dyn - yhat * jnp.mean(dyn * yhat, axis=-1, keepdims=True))
    return dy, dout * yhat


def _accumulate(ref, value, step):
    @pl.when(step == 0)
    def _():
        ref[...] = value

    @pl.when(step > 0)
    def _():
        ref[...] += value


def even_out_bwd(dx1, y, gain, w_out, name):
    rows, d = y.shape
    k = w_out.shape[0]
    tm = min(ROW_TILE, rows)

    def body(dx_ref, y_ref, g_ref, w_ref, dy_ref, dyc_ref, dg_ref):
        dy, dg_rows = _rmsnorm_bwd(dx_ref[...], y_ref[...], g_ref[...])
        dyb = dy.astype(BF16)
        dy_ref[...] = dyb
        dyc_ref[...] = _dot_nt(dyb, w_ref[...])
        _accumulate(dg_ref, _colsum(dg_rows), pl.program_id(0))

    row = lambda n: pl.BlockSpec((tm, n), lambda i: (i, 0))
    return pl.pallas_call(
        body, name=name,
        out_shape=(jax.ShapeDtypeStruct((rows, d), BF16), jax.ShapeDtypeStruct((rows, k), F32),
                   jax.ShapeDtypeStruct((1, d), F32)),
        grid=(rows // tm,),
        in_specs=[row(d), row(d), _full((1, d)), _full((k, d))],
        out_specs=(row(d), row(k), _full((1, d))),
        compiler_params=_params("arbitrary"),
    )(dx1, y, gain, w_out)


def _chunk_cumsum(g, reverse):
    n = g.shape[0]
    pos = lax.broadcasted_iota(jnp.int32, g.shape, 0) % GLA_CHUNK
    s = 1
    while s < GLA_CHUNK:
        if reverse:
            g = g + jnp.where(pos < GLA_CHUNK - s, pltpu.roll(g, n - s, 0), 0.0)
        else:
            g = g + jnp.where(pos >= s, pltpu.roll(g, s, 0), 0.0)
        s *= 2
    return g


def _gla_prepare(q_ref, k_ref, lr_ref, wg_ref, bg_ref, reverse, n_chunks):
    z = _dot(lr_ref[...].astype(BF16), wg_ref[0]) + bg_ref[0]
    g = -_softplus(-z) * (1.0 / GLA_NORMALIZER)
    bcum = _chunk_cumsum(g, reverse).reshape(n_chunks, GLA_CHUNK, GLA_DK)
    edge = 0 if reverse else GLA_CHUNK - 1
    btot = bcum[:, edge:edge + 1, :]
    e_pos = jnp.exp(bcum)
    e_neg = jnp.exp(-bcum)
    e_st = jnp.exp(btot - bcum)
    q3 = q_ref[...].reshape(n_chunks, GLA_CHUNK, GLA_DK)
    k3 = k_ref[...].reshape(n_chunks, GLA_CHUNK, GLA_DK)
    scale = GLA_DK ** -0.5
    q_in = q3 * scale * e_pos
    k_in = k3 * e_neg
    k_st = k3 * e_st
    dec = jnp.exp(btot)
    return z, q_in, k_in, k_st, dec, (scale * e_pos, e_neg, e_st)


def _gla_mask(reverse):
    i = lax.broadcasted_iota(jnp.int32, (GLA_CHUNK, GLA_CHUNK), 0)
    j = lax.broadcasted_iota(jnp.int32, (GLA_CHUNK, GLA_CHUNK), 1)
    return (j >= i) if reverse else (j <= i)


def _gla_specs(rows, n_blocks, reverse):
    tix = (lambda s: n_blocks - 1 - s) if reverse else (lambda s: s)
    d = 1 if reverse else 0
    lr_block = LR_COL // LANES
    specs = [pl.BlockSpec((rows, GLA_DK), lambda h, s: (tix(s), h)),
             pl.BlockSpec((rows, GLA_DK), lambda h, s: (tix(s), GLA_HEADS + h)),
             pl.BlockSpec((rows, GLA_DV), lambda h, s: (tix(s), GLA_HEADS + h)),
             pl.BlockSpec((rows, LANES), lambda h, s: (tix(s), lr_block)),
             pl.BlockSpec((1, LANES, GLA_DK), lambda h, s: (d, 0, h)),
             pl.BlockSpec((1, 1, GLA_DK), lambda h, s: (d, 0, h))]
    return specs, tix


def gla_fwd(proj, wg_pad, bg, reverse, name):
    rows_total = proj.shape[0]
    rows = min(GLA_BLOCK, rows_total)
    n_blocks = rows_total // rows
    n_chunks = rows // GLA_CHUNK
    specs, tix = _gla_specs(rows, n_blocks, reverse)

    def body(q_ref, k_ref, v_ref, lr_ref, wg_ref, bg_ref, o_ref, st_ref, state, kv_scr, dec_scr):
        _, q_in, k_in, k_st, dec, _ = _gla_prepare(q_ref, k_ref, lr_ref, wg_ref, bg_ref, reverse, n_chunks)
        vb = v_ref[...].reshape(n_chunks, GLA_CHUNK, GLA_DV).astype(BF16)
        qb = q_in.astype(BF16)
        p = jnp.where(_gla_mask(reverse), _bdot(qb, k_in.astype(BF16), 2, 2), 0.0)
        o = _bdot(p.astype(BF16), vb, 2, 1)
        kv_scr[...] = _bdot(vb, k_st.astype(BF16), 1, 1)
        dec_scr[...] = jnp.broadcast_to(dec, dec_scr.shape)

        @pl.when(pl.program_id(1) == 0)
        def _():
            state[...] = jnp.zeros_like(state)

        for c in range(n_chunks):
            cc = n_chunks - 1 - c if reverse else c
            st_ref[0, cc] = state[...]
            state[...] = state[...] * dec_scr[cc, 0:1] + kv_scr[cc]
        o = o + _bdot(qb, st_ref[0].astype(BF16), 2, 2)
        o_ref[...] = o.reshape(rows, GLA_DV)

    return pl.pallas_call(
        body, name=name,
        out_shape=(jax.ShapeDtypeStruct((rows_total, GLA_HEADS * GLA_DV), F32),
                   jax.ShapeDtypeStruct((GLA_HEADS, rows_total // GLA_CHUNK, GLA_DV, GLA_DK), F32)),
        grid=(GLA_HEADS, n_blocks),
        in_specs=specs,
        out_specs=(pl.BlockSpec((rows, GLA_DV), lambda h, s: (tix(s), h)),
                   pl.BlockSpec((1, n_chunks, GLA_DV, GLA_DK), lambda h, s: (h, tix(s), 0, 0))),
        scratch_shapes=[pltpu.VMEM((GLA_DV, GLA_DK), F32), pltpu.VMEM((n_chunks, GLA_DV, GLA_DK), F32),
                        pltpu.VMEM((n_chunks, SUBLANES, GLA_DK), F32)],
        compiler_params=_params("parallel", "arbitrary"),
    )(proj, proj, proj, proj, wg_pad, bg)


def gla_bwd(proj, wg_pad, bg, d_o, states, dqkv_in, reverse, name):
    rows_total = proj.shape[0]
    rows = min(GLA_BLOCK, rows_total)
    n_blocks = rows_total // rows
    n_chunks = rows // GLA_CHUNK
    specs, tix = _gla_specs(rows, n_blocks, not reverse)
    d = 1 if reverse else 0
    specs[4] = pl.BlockSpec((1, LANES, GLA_DK), lambda h, s: (d, 0, h))
    specs[5] = pl.BlockSpec((1, 1, GLA_DK), lambda h, s: (d, 0, h))
    add = dqkv_in is not None

    def body(*refs):
        q_ref, k_ref, v_ref, lr_ref, wg_ref, bg_ref, do_ref, st_ref = refs[:8]
        refs = refs[8:]
        if add:
            aq_ref, ak_ref, av_ref = refs[:3]
            refs = refs[3:]
        dq_ref, dk_ref, dv_ref, dz_ref, dstate, g_scr, dec_scr, dsn_scr = refs
        z, q_in, k_in, k_st, dec, (f_q, f_k, f_s) = _gla_prepare(q_ref, k_ref, lr_ref, wg_ref, bg_ref, reverse,
                                                                 n_chunks)
        mask = _gla_mask(reverse)
        vb = v_ref[...].reshape(n_chunks, GLA_CHUNK, GLA_DV).astype(BF16)
        dob = do_ref[...].reshape(n_chunks, GLA_CHUNK, GLA_DV).astype(BF16)
        qb, kb, ksb = q_in.astype(BF16), k_in.astype(BF16), k_st.astype(BF16)
        st = st_ref[0]
        stb = st.astype(BF16)
        pb = jnp.where(mask, _bdot(qb, kb, 2, 2), 0.0).astype(BF16)
        dpb = jnp.where(mask, _bdot(dob, vb, 2, 2), 0.0).astype(BF16)
        d_qin = _bdot(dpb, kb, 2, 1) + _bdot(dob, stb, 2, 1)
        d_kin = _bdot(dpb, qb, 1, 1)
        dv = _bdot(pb, dob, 1, 1)
        g_scr[...] = _bdot(dob, qb, 1, 1)
        dec_scr[...] = jnp.broadcast_to(dec, dec_scr.shape)

        @pl.when(pl.program_id(1) == 0)
        def _():
            dstate[...] = jnp.zeros_like(dstate)

        for c in range(n_chunks):
            cc = c if reverse else n_chunks - 1 - c
            dsn_scr[cc] = dstate[...]
            dstate[...] = dstate[...] * dec_scr[cc, 0:1] + g_scr[cc]
        dsn = dsn_scr[...]
        dsnb = dsn.astype(BF16)
        dv = dv + _bdot(ksb, dsnb, 2, 2)
        d_kst = _bdot(vb, dsnb, 2, 1)
        d_dec = jnp.sum(dsn * st, axis=1, keepdims=True)
        ks_term = d_kst * k_st
        d_btot = d_dec * dec + jnp.sum(ks_term, axis=1, keepdims=True)
        d_b = d_qin * q_in - d_kin * k_in - ks_term
        pos = lax.broadcasted_iota(jnp.int32, d_b.shape, 1)
        edge = 0 if reverse else GLA_CHUNK - 1
        d_b = d_b + jnp.where(pos == edge, d_btot, 0.0)
        dg = _chunk_cumsum(d_b.reshape(rows, GLA_DK), not reverse)
        dz_ref[...] = dg * (1.0 / GLA_NORMALIZER) * _sigmoid(-z)
        dq = (d_qin * f_q).reshape(rows, GLA_DK)
        dk = (d_kin * f_k + d_kst * f_s).reshape(rows, GLA_DK)
        dv = dv.reshape(rows, GLA_DV)
        if add:
            dq_ref[...] = (dq + aq_ref[...]).astype(BF16)
            dk_ref[...] = (dk + ak_ref[...]).astype(BF16)
            dv_ref[...] = (dv + av_ref[...]).astype(BF16)
        else:
            dq_ref[...] = dq
            dk_ref[...] = dk
            dv_ref[...] = dv

    qkv_specs = [pl.BlockSpec((rows, GLA_DK), lambda h, s: (tix(s), h)),
                 pl.BlockSpec((rows, GLA_DK), lambda h, s: (tix(s), GLA_HEADS + h)),
                 pl.BlockSpec((rows, GLA_DV), lambda h, s: (tix(s), 2 * GLA_HEADS * GLA_DK // GLA_DV + h))]
    in_specs = specs + [pl.BlockSpec((rows, GLA_DV), lambda h, s: (tix(s), h)),
                        pl.BlockSpec((1, n_chunks, GLA_DV, GLA_DK), lambda h, s: (h, tix(s), 0, 0))]
    args = [proj, proj, proj, proj, wg_pad, bg, d_o, states]
    out_dtype = F32
    if add:
        in_specs += qkv_specs
        args += [dqkv_in, dqkv_in, dqkv_in]
        out_dtype = BF16
    dq, dk, dv, dz = pl.pallas_call(
        body, name=name,
        out_shape=(jax.ShapeDtypeStruct((rows_total, GLA_HEADS * GLA_DK), out_dtype),
                   jax.ShapeDtypeStruct((rows_total, GLA_HEADS * GLA_DK), out_dtype),
                   jax.ShapeDtypeStruct((rows_total, GLA_HEADS * GLA_DV), out_dtype),
                   jax.ShapeDtypeStruct((rows_total, GLA_HEADS * GLA_DK), F32)),
        grid=(GLA_HEADS, n_blocks),
        in_specs=in_specs,
        out_specs=(pl.BlockSpec((rows, GLA_DK), lambda h, s: (tix(s), h)),
                   pl.BlockSpec((rows, GLA_DK), lambda h, s: (tix(s), h)),
                   pl.BlockSpec((rows, GLA_DV), lambda h, s: (tix(s), h)),
                   pl.BlockSpec((rows, GLA_DK), lambda h, s: (tix(s), h))),
        scratch_shapes=[pltpu.VMEM((GLA_DV, GLA_DK), F32), pltpu.VMEM((n_chunks, GLA_DV, GLA_DK), F32),
                        pltpu.VMEM((n_chunks, SUBLANES, GLA_DK), F32),
                        pltpu.VMEM((n_chunks, GLA_DV, GLA_DK), F32)],
        compiler_params=_params("parallel", "arbitrary"),
    )(*args)
    return jnp.concatenate([dq, dk, dv], axis=1), dz


def gla_gate_bwd(proj, dz_f, dz_b, wg_pad, name):
    rows_total = proj.shape[0]
    tm = min(ROW_TILE, rows_total)
    n_key = GLA_HEADS * GLA_DK

    def body(lr_ref, dzf_ref, dzb_ref, wg_ref, dlr_ref, dwg_ref, dbg_ref):
        step = pl.program_id(0)
        lr_t = jnp.transpose(lr_ref[...])
        dzf, dzb = dzf_ref[...], dzb_ref[...]
        dzf16, dzb16 = dzf.astype(BF16), dzb.astype(BF16)
        dlr_ref[...] = (_dot_nt(dzf16, wg_ref[0]) + _dot_nt(dzb16, wg_ref[1])).astype(BF16)
        dwf = _dot(lr_t[0:GLA_RANK].astype(BF16), dzf16)
        dwb = _dot(lr_t[GLA_RANK:2 * GLA_RANK].astype(BF16), dzb16)
        dbg = jnp.concatenate([_colsum(dzf), _colsum(dzb)], axis=0)

        @pl.when(step == 0)
        def _():
            dwg_ref[0] = dwf
            dwg_ref[1] = dwb
            dbg_ref[...] = dbg

        @pl.when(step > 0)
        def _():
            dwg_ref[0] += dwf
            dwg_ref[1] += dwb
            dbg_ref[...] += dbg

    return pl.pallas_call(
        body, name=name,
        out_shape=(jax.ShapeDtypeStruct((rows_total, LANES), BF16), jax.ShapeDtypeStruct((2, GLA_RANK, n_key), F32),
                   jax.ShapeDtypeStruct((2, n_key), F32)),
        grid=(rows_total // tm,),
        in_specs=[pl.BlockSpec((tm, LANES), lambda i: (i, LR_COL // LANES)),
                  pl.BlockSpec((tm, n_key), lambda i: (i, 0)), pl.BlockSpec((tm, n_key), lambda i: (i, 0)),
                  _full((2, LANES, n_key))],
        out_specs=(pl.BlockSpec((tm, LANES), lambda i: (i, 0)), _full((2, GLA_RANK, n_key)), _full((2, n_key))),
        compiler_params=_params("arbitrary"),
    )(proj, dz_f, dz_b, wg_pad)


def _head_norm(o, gain):
    outs, hats, rstds = [], [], []
    for h in range(GLA_HEADS):
        oh = o[:, h * GLA_DV:(h + 1) * GLA_DV]
        rstd = lax.rsqrt(jnp.mean(oh * oh, axis=-1, keepdims=True) + NORM_EPS)
        hat = oh * rstd
        outs.append(hat * gain)
        hats.append(hat)
        rstds.append(rstd)
    return outs, hats, rstds


def odd_out_fwd(o_f, o_b, proj, head_gain, w_out, gain, x1, target, name):
    rows, d = x1.shape
    tm = min(ROW_TILE, rows)
    r_block = (2 * GLA_HEADS * GLA_DK + GLA_HEADS * GLA_DV) // d

    def body(of_ref, ob_ref, r_ref, hg_ref, w_ref, g_ref, x1_ref, tgt_ref, y2_ref, dy_ref, dx2_ref, loss_ref,
             dg_ref):
        step = pl.program_id(0)
        on, _, _ = _head_norm(of_ref[...] + ob_ref[...], hg_ref[...])
        r = r_ref[...]
        y2 = (jnp.concatenate(on, axis=1) * (r * _sigmoid(r))).astype(BF16)
        y2_ref[...] = y2
        y = _dot(y2, w_ref[...])
        gain_v = g_ref[...]
        rstd = lax.rsqrt(jnp.mean(y * y, axis=-1, keepdims=True) + NORM_EPS)
        x2 = x1_ref[...] + y * rstd * gain_v
        diff = x2 - tgt_ref[...]
        loss = 0.5 * jnp.sum(jnp.mean(diff * diff, axis=-1, keepdims=True), axis=0, keepdims=True)
        dx2 = diff * (1.0 / d)
        dx2_ref[...] = dx2
        dy, dg_rows = _rmsnorm_bwd(dx2, y, gain_v)
        dy_ref[...] = dy.astype(BF16)
        _accumulate(loss_ref, jnp.broadcast_to(loss, loss_ref.shape), step)
        _accumulate(dg_ref, _colsum(dg_rows), step)

    row = lambda n, col=0: pl.BlockSpec((tm, n), lambda i: (i, col))
    return pl.pallas_call(
        body, name=name,
        out_shape=(jax.ShapeDtypeStruct((rows, d), BF16), jax.ShapeDtypeStruct((rows, d), BF16),
                   jax.ShapeDtypeStruct((rows, d), F32), jax.ShapeDtypeStruct((SUBLANES, LANES), F32),
                   jax.ShapeDtypeStruct((1, d), F32)),
        grid=(rows // tm,),
        in_specs=[row(d), row(d), row(d, r_block), _full((1, GLA_DV)), _full((d, d)), _full((1, d)), row(d), row(d)],
        out_specs=(row(d), row(d), row(d), _full((SUBLANES, LANES)), _full((1, d))),
        compiler_params=_params("arbitrary"),
    )(o_f, o_b, proj, head_gain, w_out, gain, x1, target)


def odd_out_bwd(dy, w_out, o_f, o_b, proj, head_gain, name):
    rows, d = dy.shape
    tm = min(ROW_TILE, rows)
    r_block = (2 * GLA_HEADS * GLA_DK + GLA_HEADS * GLA_DV) // d

    def body(dy_ref, w_ref, of_ref, ob_ref, r_ref, hg_ref, dr_ref, do_ref, dhg_ref):
        dy2 = _dot_nt(dy_ref[...], w_ref[...])
        hg = hg_ref[...]
        on, hats, rstds = _head_norm(of_ref[...] + ob_ref[...], hg)
        r = r_ref[...]
        sr = _sigmoid(r)
        dr_ref[...] = (dy2 * jnp.concatenate(on, axis=1) * (sr * (1.0 + r * (1.0 - sr)))).astype(BF16)
        d_on = dy2 * (r * sr)
        d_os, dhg = [], None
        for h in range(GLA_HEADS):
            dn = d_on[:, h * GLA_DV:(h + 1) * GLA_DV]
            part = _colsum(dn * hats[h])
            dhg = part if dhg is None else dhg + part
            dng = dn * hg
            d_os.append(rstds[h] * (dng - hats[h] * jnp.mean(dng * hats[h], axis=-1, keepdims=True)))
        do_ref[...] = jnp.concatenate(d_os, axis=1)
        _accumulate(dhg_ref, dhg, pl.program_id(0))

    row = lambda n, col=0: pl.BlockSpec((tm, n), lambda i: (i, col))
    return pl.pallas_call(
        body, name=name,
        out_shape=(jax.ShapeDtypeStruct((rows, d), BF16), jax.ShapeDtypeStruct((rows, d), F32),
                   jax.ShapeDtypeStruct((1, GLA_DV), F32)),
        grid=(rows // tm,),
        in_specs=[row(d), _full((d, d)), row(d), row(d), row(d, r_block), _full((1, GLA_DV))],
        out_specs=(row(d), row(d), _full((1, GLA_DV))),
        compiler_params=_params("arbitrary"),
    )(dy, w_out, o_f, o_b, proj, head_gain)


def local_step(x, target, w):
    g = {}
    proj_e, h0 = norm_matmul(x, w["even_norm_pre"], w["even_w_in"], "even_in_proj")
    h_dir = [rglru_fwd(proj_e, w["rg_conv_w"], w["rg_conv_b"], w["rg_gate_w"][d], w["rg_gate_b"][d],
                       w["rg_lambda"][d], d == 1, "rglru_fwd_%d" % d) for d in range(2)]
    ycat = even_mix_fwd(proj_e, h_dir[0], h_dir[1], w["sc_conv_w"], "even_mix_fwd")
    x1, y_e = even_out_fwd(ycat, w["even_w_out"], w["even_norm_post"], x, "even_out_fwd")
    proj_o, h1 = norm_matmul(x1, w["odd_norm_pre"], w["odd_w_in"], "odd_in_proj")
    o_dir, st_dir = [], []
    for d in range(2):
        o, st = gla_fwd(proj_o, w["gla_wg_pad"], w["gla_b_gate"], d == 1, "gla_fwd_%d" % d)
        o_dir.append(o)
        st_dir.append(st)
    y2, dy_o, dx2, loss, g["odd_norm_post"] = odd_out_fwd(
        o_dir[0], o_dir[1], proj_o, w["gla_norm_g"], w["odd_w_out"], w["odd_norm_post"], x1, target, "odd_out_fwd")
    g["odd_w_out"] = matmul_dw(y2, dy_o, D_MODEL, "odd_w_out_grad")[0]
    dr, d_o, g["gla_norm_g"] = odd_out_bwd(dy_o, w["odd_w_out"], o_dir[0], o_dir[1], proj_o, w["gla_norm_g"],
                                           "odd_out_bwd")
    dqkv_f, dz_f = gla_bwd(proj_o, w["gla_wg_pad"], w["gla_b_gate"], d_o, st_dir[0], None, False, "gla_bwd_0")
    dqkv, dz_b = gla_bwd(proj_o, w["gla_wg_pad"], w["gla_b_gate"], d_o, st_dir[1], dqkv_f, True, "gla_bwd_1")
    dlr, g["gla_w_gate_lr"], g["gla_b_gate"] = gla_gate_bwd(proj_o, dz_f, dz_b, w["gla_wg_pad"], "gla_gate_bwd")
    dproj_o = jnp.concatenate([dqkv, dr, dlr], axis=1)
    g["odd_w_in"] = matmul_dw(h1, dproj_o, ODD_IN_PAD, "odd_w_in_grad")[0][:, :ODD_IN]
    dx1, g["odd_norm_pre"] = inproj_bwd(dproj_o, w["odd_w_in"], x1, w["odd_norm_pre"], dx2, "odd_in_proj_bwd")
    dy_e, dycat, g["even_norm_post"] = even_out_bwd(dx1, y_e, w["even_norm_post"], w["even_w_out"], "even_out_bwd")
    g["even_w_out"] = matmul_dw(ycat, dy_e, D_MODEL, "even_w_out_grad")[0]
    dua, dgw, dgb, dlam = [], [], [], []
    for d in range(2):
        a, b, c, e = rglru_bwd(proj_e, dycat, h_dir[d], w["rg_conv_w"], w["rg_conv_b"], w["rg_gate_w"][d],
                               w["rg_gate_b"][d], w["rg_lambda"][d], d == 1, "rglru_bwd_%d" % d)
        dua.append(a)
        dgw.append(b)
        dgb.append(c)
        dlam.append(e)
    dproj_e, g["rg_conv_w"], g["rg_conv_b"], g["sc_conv_w"] = even_mix_bwd(
        proj_e, dycat, h_dir[0], h_dir[1], dua[0], dua[1], w["rg_conv_w"], w["sc_conv_w"], "even_mix_bwd")
    dgw = jnp.stack(dgw).reshape(2, RG_HEADS, RG_HEAD_DIM, 2, RG_HEAD_DIM)
    g["rg_gate_w"] = jnp.transpose(dgw, (0, 3, 1, 2, 4))
    g["rg_gate_b"] = jnp.stack(dgb).reshape(2, 2, RG_HEADS, RG_HEAD_DIM)
    g["rg_lambda"] = jnp.concatenate(dlam, axis=0)
    g["even_w_in"] = matmul_dw(h0, dproj_e, EVEN_IN // 4, "even_w_in_grad")
    grad_x, g["even_norm_pre"] = inproj_bwd(dproj_e, w["even_w_in"], x, w["even_norm_pre"], dx1, "even_in_proj_bwd")
    return loss, grad_x, g


def _prepare_weights(full):
    w = {}
    for name in ("even_norm_pre", "even_norm_post", "rg_conv_b", "odd_norm_pre", "odd_norm_post", "gla_norm_g"):
        w[name] = full[name].reshape(1, -1)
    w["rg_conv_w"] = full["rg_conv_w"]
    w["sc_conv_w"] = full["sc_conv_w"]
    w["even_w_in"] = full["even_w_in"].astype(BF16)
    if w["even_w_in"].ndim == 2:
        w["even_w_in"] = jnp.transpose(w["even_w_in"].reshape(D_MODEL, 4, EVEN_IN // 4), (1, 0, 2))
    w["even_w_out"] = full["even_w_out"].astype(BF16)
    gw = jnp.transpose(full["rg_gate_w"].astype(BF16), (0, 2, 3, 1, 4))
    w["rg_gate_w"] = gw.reshape(2, RG_HEADS, RG_HEAD_DIM, 2 * RG_HEAD_DIM)
    w["rg_gate_b"] = full["rg_gate_b"].reshape(2, 2, D_MODEL)
    w["rg_lambda"] = full["rg_lambda"].reshape(2, 1, D_MODEL)
    w_in = jnp.pad(full["odd_w_in"].astype(BF16), ((0, 0), (0, ODD_IN_PAD - ODD_IN)))
    w["odd_w_in"] = w_in.reshape(1, D_MODEL, ODD_IN_PAD)
    w["odd_w_out"] = full["odd_w_out"].astype(BF16)
    wg = full["gla_w_gate_lr"].astype(BF16)
    w["gla_wg_pad"] = jnp.stack([jnp.pad(wg[d], ((d * GLA_RANK, LANES - (d + 1) * GLA_RANK), (0, 0)))
                                 for d in range(2)])
    w["gla_b_gate"] = full["gla_b_gate"].reshape(2, 1, GLA_HEADS * GLA_DK)
    return w


SHARDED_SMALL = (("rg_conv_w", (4, 256)), ("rg_lambda", (2, 256)), ("sc_conv_w", (3, 256)),
                 ("odd_norm_pre", (256,)), ("odd_norm_post", (256,)), ("gla_w_gate_lr", (2, 16, 128)),
                 ("gla_b_gate", (2, 128)), ("gla_norm_g", (64,)))
SHARDED_ROWS = 64
REPLICATED = (("even_norm_pre", (1024,)), ("even_norm_post", (1024,)), ("rg_conv_b", (1024,)),
              ("rg_gate_b", (2, 2, 8, 128)), ("rg_gate_w", (2, 2, 8, 128, 128)))
REPLICATED_ROWS = 4160
REP_PART = REPLICATED_ROWS // 8
HALF_SHARDED = SHARDED_ROWS // 2
PACK_HALF = HALF_SHARDED + REP_PART


def _seg_rows(shape):
    n = 1
    for s in shape:
        n *= s
    return -(-n // LANES)


def _pack(arrays, spec, total_rows, lead=()):
    parts = []
    for name, shape in spec:
        flat = arrays[name].reshape(lead + (-1,))
        pad = _seg_rows(shape) * LANES - flat.shape[-1]
        if pad:
            flat = jnp.pad(flat, [(0, 0)] * len(lead) + [(0, pad)])
        parts.append(flat.reshape(lead + (-1, LANES)))
    rows = jnp.concatenate(parts, axis=len(lead))
    pad = total_rows - rows.shape[len(lead)]
    return jnp.pad(rows, [(0, 0)] * len(lead) + [(0, pad), (0, 0)])


def _unpack(rows, spec, lead=()):
    out, at = {}, 0
    for name, shape in spec:
        n = 1
        for s in shape:
            n *= s
        k = _seg_rows(shape)
        seg = lax.slice_in_dim(rows, at, at + k, axis=len(lead)).reshape(lead + (-1,))
        out[name] = lax.slice_in_dim(seg, 0, n, axis=len(lead)).reshape(lead + shape)
        at += k
    return out


def _split_owners(arr):
    a = arr.reshape(arr.shape[:-1] + (4, arr.shape[-1] // 4))
    return jnp.moveaxis(a, -2, 0)


def _merge_owners(arr):
    a = jnp.moveaxis(arr, 0, -2)
    return a.reshape(a.shape[:-2] + (-1,))


HBM_SPEC = pl.BlockSpec(memory_space=pltpu.HBM)


def _position():
    x, y, c = lax.axis_index("x"), lax.axis_index("y"), lax.axis_index("c")
    chips = [(1 - x, y), (x, 1 - y), (1 - x, 1 - y)]
    return x, y, c, chips


def _remote(src, dst, send_sem, recv_sem, device):
    return pltpu.make_async_remote_copy(src_ref=src, dst_ref=dst, send_sem=send_sem, recv_sem=recv_sem,
                                        device_id=device, device_id_type=MESH)


def gather_weights(halved, whole):
    n_h, n_w = len(halved), len(whole)
    n = n_h + n_w

    def body(*refs):
        ins, outs = refs[:n], refs[n:2 * n]
        send_ici, recv_ici, send_fwd, recv_fwd, local_sem = refs[2 * n:]
        x, y, c, chips = _position()
        me = 2 * x + y
        sibling = (x, y, 1 - c)
        local = []
        for a in range(n):
            cp = pltpu.make_async_copy(ins[a], outs[a].at[me], local_sem.at[a])
            cp.start()
            local.append(cp)
        sends = []
        for a in range(n):
            for k, chip in enumerate(chips):
                src = ins[a].at[c] if a < n_h else ins[a]
                dst = outs[a].at[me, c] if a < n_h else outs[a].at[me]
                cp = _remote(src, dst, send_ici.at[3 * a + k], recv_ici.at[3 * a + k], (chip[0], chip[1], c))
                cp.start()
                sends.append(cp)
        for a in range(n):
            for k, chip in enumerate(chips):
                q = 2 * chip[0] + chip[1]
                landed = outs[a].at[q, c] if a < n_h else outs[a].at[q]
                _remote(landed, landed, send_ici.at[3 * a + k], recv_ici.at[3 * a + k], sibling).wait_recv()
                if a < n_h:
                    cp = _remote(landed, landed, send_fwd.at[3 * a + k], recv_fwd.at[3 * a + k], sibling)
                    cp.start()
                    sends.append(cp)
        for a in range(n_h):
            for k, chip in enumerate(chips):
                q = 2 * chip[0] + chip[1]
                passed = outs[a].at[q, 1 - c]
                _remote(passed, passed, send_fwd.at[3 * a + k], recv_fwd.at[3 * a + k], sibling).wait_recv()
        for cp in sends:
            cp.wait_send()
        for cp in local:
            cp.wait()

    arrays = list(halved) + list(whole)
    out_shape = [jax.ShapeDtypeStruct((4,) + a.shape, a.dtype) for a in arrays]
    outs = pl.pallas_call(
        body, name="gather_weights",
        out_shape=out_shape,
        in_specs=[HBM_SPEC] * n, out_specs=[HBM_SPEC] * n,
        scratch_shapes=[pltpu.SemaphoreType.DMA((3 * n,)), pltpu.SemaphoreType.DMA((3 * n,)),
                        pltpu.SemaphoreType.DMA((3 * n_h,)), pltpu.SemaphoreType.DMA((3 * n_h,)),
                        pltpu.SemaphoreType.DMA((n,))],
    )(*arrays)
    return outs[:n_h], outs[n_h:]


def exchange_with_sibling(arrays):
    n = len(arrays)

    def body(*refs):
        ins, outs = refs[:n], refs[n:2 * n]
        send_sems, recv_sems = refs[2 * n:]
        x, y, c, _ = _position()
        copies = []
        for a in range(n):
            cp = _remote(ins[a].at[:, 1 - c], outs[a], send_sems.at[a], recv_sems.at[a], (x, y, 1 - c))
            cp.start()
            copies.append(cp)
        for cp in copies:
            cp.wait()

    return pl.pallas_call(
        body, name="grad_exchange_sibling",
        out_shape=[jax.ShapeDtypeStruct((a.shape[0],) + a.shape[2:], a.dtype) for a in arrays],
        in_specs=[HBM_SPEC] * n, out_specs=[HBM_SPEC] * n,
        scratch_shapes=[pltpu.SemaphoreType.DMA((n,)), pltpu.SemaphoreType.DMA((n,))],
    )(*arrays)


def exchange_with_chips(arrays):
    n = len(arrays)

    def body(*refs):
        ins, outs = refs[:n], refs[n:2 * n]
        send_sems, recv_sems = refs[2 * n:]
        x, y, c, chips = _position()
        copies = []
        for a in range(n):
            for k, chip in enumerate(chips):
                q = 2 * chip[0] + chip[1]
                cp = _remote(ins[a].at[q], outs[a].at[k], send_sems.at[3 * a + k], recv_sems.at[3 * a + k],
                             (chip[0], chip[1], c))
                cp.start()
                copies.append(cp)
        for cp in copies:
            cp.wait()

    return pl.pallas_call(
        body, name="grad_exchange_chips",
        out_shape=[jax.ShapeDtypeStruct((3,) + a.shape[1:], a.dtype) for a in arrays],
        in_specs=[HBM_SPEC] * n, out_specs=[HBM_SPEC] * n,
        scratch_shapes=[pltpu.SemaphoreType.DMA((3 * n,)), pltpu.SemaphoreType.DMA((3 * n,))],
    )(*arrays)


def share_totals(totals, pack_total):
    arrays = list(totals) + [pack_total]
    n = len(arrays)

    def body(*refs):
        ins, outs, rep = refs[:n], refs[n:2 * n], refs[2 * n]
        send_sems, recv_sems, rep_send, rep_recv, local_sem = refs[2 * n + 1:]
        x, y, c, chips = _position()
        sibling = (x, y, 1 - c)
        local, sends = [], []
        for a in range(n):
            cp = pltpu.make_async_copy(ins[a], outs[a].at[c], local_sem.at[a])
            cp.start()
            local.append(cp)
            cp = _remote(ins[a], outs[a].at[c], send_sems.at[a], recv_sems.at[a], sibling)
            cp.start()
            sends.append(cp)
        mine = ins[n - 1].at[pl.ds(HALF_SHARDED, REP_PART)]
        slot = rep.at[4 * x + 2 * y + c]
        cp = pltpu.make_async_copy(mine, slot, local_sem.at[n])
        cp.start()
        local.append(cp)
        peers = [sibling]
        for chip in chips:
            peers += [(chip[0], chip[1], c), (chip[0], chip[1], 1 - c)]
        for j, peer in enumerate(peers):
            cp = _remote(mine, slot, rep_send.at[j], rep_recv.at[j], peer)
            cp.start()
            sends.append(cp)
        for a in range(n):
            landed = outs[a].at[1 - c]
            _remote(landed, landed, send_sems.at[a], recv_sems.at[a], sibling).wait_recv()
        for j, peer in enumerate(peers):
            landed = rep.at[4 * peer[0] + 2 * peer[1] + peer[2]]
            _remote(landed, landed, rep_send.at[j], rep_recv.at[j], peer).wait_recv()
        for cp in sends:
            cp.wait_send()
        for cp in local:
            cp.wait()

    outs = pl.pallas_call(
        body, name="grad_share_totals",
        out_shape=[jax.ShapeDtypeStruct((2,) + a.shape, a.dtype) for a in arrays]
        + [jax.ShapeDtypeStruct((8, REP_PART, LANES), F32)],
        in_specs=[HBM_SPEC] * n, out_specs=[HBM_SPEC] * (n + 1),
        scratch_shapes=[pltpu.SemaphoreType.DMA((n,)), pltpu.SemaphoreType.DMA((n,)),
                        pltpu.SemaphoreType.DMA((7,)), pltpu.SemaphoreType.DMA((7,)),
                        pltpu.SemaphoreType.DMA((n + 1,))],
    )(*arrays)
    return outs[:n], outs[n]


TILE_BYTES = 1 << 20


def _row_tile(rows, cols):
    best = None
    for t in range(SUBLANES, rows + 1, SUBLANES):
        if rows % t == 0 and t * cols * 4 <= TILE_BYTES:
            best = t
    return best if best is not None else rows


def add_sibling(mine, received, core, name):
    _, _, r, cols = mine.shape
    tr = _row_tile(r, cols)

    def body(c_ref, a_ref, b_ref, o_ref):
        o_ref[...] = a_ref[0] + b_ref[...]

    return pl.pallas_call(
        body, name=name,
        out_shape=jax.ShapeDtypeStruct((4, r, cols), F32),
        grid_spec=pltpu.PrefetchScalarGridSpec(
            num_scalar_prefetch=1, grid=(4, r // tr),
            in_specs=[pl.BlockSpec((1, 1, tr, cols), lambda o, i, c_ref: (o, c_ref[0], i, 0)),
                      pl.BlockSpec((1, tr, cols), lambda o, i, c_ref: (o, i, 0))],
            out_specs=pl.BlockSpec((1, tr, cols), lambda o, i, c_ref: (o, i, 0))),
        compiler_params=_params("parallel", "parallel"),
    )(core, mine, received)


def add_chips(own, received, chip, name):
    _, r, cols = own.shape
    tr = _row_tile(r, cols)

    def body(p_ref, a_ref, b0, b1, b2, o_ref):
        o_ref[...] = ((a_ref[0] + b0[0]) + b1[0]) + b2[0]

    rb = lambda k: pl.BlockSpec((1, tr, cols), lambda i, p_ref: (k, i, 0))
    return pl.pallas_call(
        body, name=name,
        out_shape=jax.ShapeDtypeStruct((r, cols), F32),
        grid_spec=pltpu.PrefetchScalarGridSpec(
            num_scalar_prefetch=1, grid=(r // tr,),
            in_specs=[pl.BlockSpec((1, tr, cols), lambda i, p_ref: (p_ref[0], i, 0)), rb(0), rb(1), rb(2)],
            out_specs=pl.BlockSpec((tr, cols), lambda i, p_ref: (i, 0))),
        compiler_params=_params("parallel"),
    )(chip, own, received, received, received)


def adamw(w, g, m, v, name):
    r, cols = w.shape
    tr = _row_tile(r, cols)

    def body(w_ref, g_ref, m_ref, v_ref, d_ref, nm_ref, nv_ref):
        gv = g_ref[...]
        nm = ADAM_B1 * m_ref[...] + (1.0 - ADAM_B1) * gv
        nv = ADAM_B2 * v_ref[...] + (1.0 - ADAM_B2) * (gv * gv)
        nm_ref[...] = nm
        nv_ref[...] = nv
        m_hat = nm / (1.0 - ADAM_B1 ** ADAM_STEP)
        v_hat = nv / (1.0 - ADAM_B2 ** ADAM_STEP)
        d_ref[...] = -ADAM_LR * (m_hat / (jnp.sqrt(v_hat) + ADAM_EPS) + ADAM_WD * w_ref[...])

    blk = pl.BlockSpec((tr, cols), lambda i: (i, 0))
    return pl.pallas_call(
        body, name=name,
        out_shape=(jax.ShapeDtypeStruct((r, cols), F32),) * 3,
        grid=(r // tr,),
        in_specs=[blk] * 4, out_specs=(blk,) * 3,
        compiler_params=_params("parallel"),
    )(w, g, m, v)


WEIGHTS = ("even_norm_pre", "even_norm_post", "even_w_in", "rg_conv_w", "rg_conv_b", "rg_gate_w", "rg_gate_b",
           "rg_lambda", "sc_conv_w", "even_w_out", "odd_norm_pre", "odd_norm_post", "odd_w_in", "gla_w_gate_lr",
           "gla_b_gate", "gla_norm_g", "odd_w_out")
BIG = ("even_w_in", "even_w_out", "odd_w_in", "odd_w_out")


def _halves(a):
    return a.reshape((2, a.shape[0] // 2) + a.shape[1:])


def kernel(x, even_norm_pre, even_norm_post, even_w_in, rg_conv_w, rg_conv_b, rg_gate_w, rg_gate_b, rg_lambda, sc_conv_w, even_w_out, odd_norm_pre, odd_norm_post, odd_w_in, gla_w_gate_lr, gla_b_gate, gla_norm_g, odd_w_out, loss_target, m_even_norm_pre, m_even_norm_post, m_even_w_in, m_rg_conv_w, m_rg_conv_b, m_rg_gate_w, m_rg_gate_b, m_rg_lambda, m_sc_conv_w, m_even_w_out, m_odd_norm_pre, m_odd_norm_post, m_odd_w_in, m_gla_w_gate_lr, m_gla_b_gate, m_gla_norm_g, m_odd_w_out, v_even_norm_pre, v_even_norm_post, v_even_w_in, v_rg_conv_w, v_rg_conv_b, v_rg_gate_w, v_rg_gate_b, v_rg_lambda, v_sc_conv_w, v_even_w_out, v_odd_norm_pre, v_odd_norm_post, v_odd_w_in, v_gla_w_gate_lr, v_gla_b_gate, v_gla_norm_g, v_odd_w_out):
    given = dict(locals())
    shard = {n: given[n][0] for n in WEIGHTS}
    m_in = {n: given["m_" + n][0] for n in WEIGHTS}
    v_in = {n: given["v_" + n][0] for n in WEIGHTS}
    mx, my, mc = lax.axis_index("x"), lax.axis_index("y"), lax.axis_index("c")
    core = jnp.reshape(mc, (1,)).astype(jnp.int32)
    chip = jnp.reshape(2 * mx + my, (1,)).astype(jnp.int32)

    small_shard = _pack(shard, SHARDED_SMALL, SHARDED_ROWS)
    big_full, (small_full,) = gather_weights([_halves(shard[n].astype(BF16)) for n in BIG], [small_shard])
    full = {n: shard[n] for n, _ in REPLICATED}
    full.update({n: _merge_owners(a) for n, a in _unpack(small_full, SHARDED_SMALL, lead=(4,)).items()})
    full["even_w_in"] = big_full[0].reshape(4, D_MODEL, EVEN_IN // 4)
    full["even_w_out"] = big_full[1].reshape(2 * D_MODEL, D_MODEL)
    full["odd_w_in"] = jnp.transpose(big_full[2].reshape(4, D_MODEL, ODD_IN // 4), (1, 0, 2)).reshape(D_MODEL, ODD_IN)
    full["odd_w_out"] = big_full[3].reshape(D_MODEL, D_MODEL)

    loss, grad_x, g = local_step(x[0], loss_target[0], _prepare_weights(full))
    loss = lax.psum(loss[0, 0], ("x", "y", "c"))

    rep_rows = _pack(g, REPLICATED, REPLICATED_ROWS).reshape(4, 2, REP_PART, LANES)
    sh_rows = _pack({n: _split_owners(g[n]) for n, _ in SHARDED_SMALL}, SHARDED_SMALL, SHARDED_ROWS, lead=(4,))
    pack = jnp.concatenate([sh_rows.reshape(4, 2, HALF_SHARDED, LANES), rep_rows], axis=2)
    slabs = [g["even_w_in"],
             g["even_w_out"].reshape(4, D_MODEL // 2, D_MODEL),
             jnp.transpose(g["odd_w_in"].reshape(D_MODEL, 4, ODD_IN // 4), (1, 0, 2)),
             g["odd_w_out"].reshape(4, D_MODEL // 4, D_MODEL),
             pack.reshape(4, 2 * PACK_HALF, LANES)]
    slabs = [a.reshape((4, 2, a.shape[1] // 2) + a.shape[2:]) for a in slabs]
    from_sibling = exchange_with_sibling(slabs)
    chip_sums = [add_sibling(a, b, core, "grad_add_sibling_%d" % i)
                 for i, (a, b) in enumerate(zip(slabs, from_sibling))]
    from_chips = exchange_with_chips(chip_sums)
    totals = [add_chips(a, b, chip, "grad_add_chips_%d" % i) for i, (a, b) in enumerate(zip(chip_sums, from_chips))]
    both, rep_all = share_totals(totals[:4], totals[4])
    grads = {n: both[i].reshape(shard[n].shape) for i, n in enumerate(BIG)}
    sh_total = jnp.concatenate([both[4][0, :HALF_SHARDED], both[4][1, :HALF_SHARDED]], axis=0)
    rep_total = rep_all.reshape(REPLICATED_ROWS, LANES)
    grads.update(_unpack(sh_total, SHARDED_SMALL))
    grads.update(_unpack(rep_total, REPLICATED))

    delta, new_m, new_v = {}, {}, {}
    for n in BIG:
        delta[n], new_m[n], new_v[n] = adamw(shard[n], grads[n], m_in[n], v_in[n], "adamw_" + n)
    packed = [jnp.concatenate([_pack(src, SHARDED_SMALL, SHARDED_ROWS), _pack(src, REPLICATED, REPLICATED_ROWS)])
              for src in (shard, m_in, v_in)]
    small_g = jnp.concatenate([sh_total, rep_total], axis=0)
    outs = adamw(packed[0], small_g, packed[1], packed[2], "adamw_small")
    for dst, rows in zip((delta, new_m, new_v), outs):
        dst.update(_unpack(rows[:SHARDED_ROWS], SHARDED_SMALL))
        dst.update(_unpack(rows[SHARDED_ROWS:], REPLICATED))
    result = [loss, grad_x[None]]
    for group in (grads, delta, new_m, new_v):
        result += [group[n].reshape(given[n].shape) for n in WEIGHTS]
    return tuple(result)
```

```python
import functools

import jax
import jax.numpy as jnp
from jax import lax
from jax.experimental import pallas as pl
from jax.experimental.pallas import tpu as pltpu

F32 = jnp.float32
BF16 = jnp.bfloat16
MESH = pl.DeviceIdType.MESH

D_MODEL = 1024
NORM_EPS = 1e-6
RG_HEADS = 8
RG_HEAD_DIM = 128
RG_C = 8.0
EVEN_IN = 6144
ODD_IN = 3104
ODD_IN_PAD = 3200
GLA_HEADS = 4
GLA_DK = 128
GLA_DV = 256
GLA_RANK = 16
GLA_NORMALIZER = 16.0
GLA_CHUNK = 64
LR_COL = 3072

ADAM_LR = 0.001
ADAM_B1 = 0.9
ADAM_B2 = 0.999
ADAM_EPS = 1e-08
ADAM_WD = 0.01
ADAM_STEP = 10

SUBLANES = 8
LANES = 128
VMEM_LIMIT = 56 * 2 ** 20

ROW_TILE = 512
SCAN_TILE = 256
GLA_BLOCK = 1024
MIX_TILE = 128


def _params(*sem):
    return pltpu.CompilerParams(dimension_semantics=sem, vmem_limit_bytes=VMEM_LIMIT)


def _full(shape):
    n = len(shape)
    return pl.BlockSpec(shape, lambda *_: (0,) * n)


def _sigmoid(x):
    return 1.0 / (1.0 + jnp.exp(-x))


def _softplus(x):
    return jnp.maximum(x, 0.0) + jnp.log(1.0 + jnp.exp(-jnp.abs(x)))


def _one_minus_exp(x):
    series = -x * (1.0 + x * (1.0 / 2.0) * (1.0 + x * (1.0 / 3.0) * (1.0 + x * (1.0 / 4.0) * (
        1.0 + x * (1.0 / 5.0) * (1.0 + x * (1.0 / 6.0))))))
    return jnp.where(x > -0.25, series, 1.0 - jnp.exp(x))


def _dot(a, b):
    return jnp.dot(a, b, preferred_element_type=F32)


def _dot_nt(a, b):
    return lax.dot_general(a, b, (((1,), (1,)), ((), ())), preferred_element_type=F32)


def _dot_tn(a, b):
    return lax.dot_general(a, b, (((0,), (0,)), ((), ())), preferred_element_type=F32)


def _bdot(a, b, ca, cb):
    return lax.dot_general(a, b, (((ca,), (cb,)), ((0,), (0,))), preferred_element_type=F32)


def _halo_specs(rows, cols, col_block, n_row_tiles, tix):
    per = rows // SUBLANES
    last = n_row_tiles * per - 1

    def split(args):
        if len(args) == 2:
            return tix(args[1]), col_block + args[0]
        return tix(args[0]), col_block

    def prev(*args):
        t, c = split(args)
        return (jnp.maximum(t * per - 1, 0), c)

    def main(*args):
        return split(args)

    def nxt(*args):
        t, c = split(args)
        return (jnp.minimum((t + 1) * per, last), c)

    return [pl.BlockSpec((SUBLANES, cols), prev), pl.BlockSpec((rows, cols), main),
            pl.BlockSpec((SUBLANES, cols), nxt)]


def _extend(prev_ref, main_ref, next_ref, is_first, is_last):
    p = jnp.where(is_first, 0.0, prev_ref[...])
    n = jnp.where(is_last, 0.0, next_ref[...])
    return jnp.concatenate([p, main_ref[...], n], axis=0)


def _shifted(ext, offset, rows):
    if offset == 0:
        return ext[SUBLANES:SUBLANES + rows]
    n = ext.shape[0]
    return pltpu.roll(ext, (-offset) % n, 0)[SUBLANES:SUBLANES + rows]


def _conv(ext, w, left, rows):
    out = None
    for k in range(w.shape[0]):
        term = _shifted(ext, k - left, rows) * w[k:k + 1]
        out = term if out is None else out + term
    return out


def _conv_transpose(ext, w, left, rows):
    out = None
    for k in range(w.shape[0]):
        term = _shifted(ext, left - k, rows) * w[k:k + 1]
        out = term if out is None else out + term
    return out


def _colsum(x):
    return jnp.sum(x, axis=0, keepdims=True)


def _accumulate(ref, value, step):
    @pl.when(step == 0)
    def _():
        ref[...] = value

    @pl.when(step > 0)
    def _():
        ref[...] += value


def norm_matmul(x, gain, w, name):
    rows, d = x.shape
    n_col_tiles, _, tn = w.shape
    tm = min(ROW_TILE, rows)

    def body(x_ref, g_ref, w_ref, proj_ref, h_ref, h_scr):
        @pl.when(pl.program_id(1) == 0)
        def _():
            xv = x_ref[...]
            rstd = lax.rsqrt(jnp.mean(xv * xv, axis=-1, keepdims=True) + NORM_EPS)
            hv = (xv * rstd * g_ref[...]).astype(BF16)
            h_scr[...] = hv
            h_ref[...] = hv

        proj_ref[...] = _dot(h_scr[...], w_ref[0])

    return pl.pallas_call(
        body, name=name,
        out_shape=(jax.ShapeDtypeStruct((rows, n_col_tiles * tn), F32), jax.ShapeDtypeStruct((rows, d), BF16)),
        grid=(rows // tm, n_col_tiles),
        in_specs=[pl.BlockSpec((tm, d), lambda i, j: (i, 0)), _full((1, d)),
                  pl.BlockSpec((1, d, tn), lambda i, j: (j, 0, 0))],
        out_specs=(pl.BlockSpec((tm, tn), lambda i, j: (i, j)), pl.BlockSpec((tm, d), lambda i, j: (i, 0))),
        scratch_shapes=[pltpu.VMEM((tm, d), BF16)],
        compiler_params=_params("parallel", "arbitrary"),
    )(x, gain, w)


def inproj_bwd(dproj, w, x, gain, dres, name):
    rows, d = x.shape
    n_col_tiles, _, tn = w.shape
    tm = min(ROW_TILE, rows)

    def body(dp_ref, w_ref, x_ref, g_ref, dres_ref, dx_ref, dg_ref, acc):
        i, j = pl.program_id(0), pl.program_id(1)
        part = _dot_nt(dp_ref[...], w_ref[0])

        @pl.when(j == 0)
        def _():
            acc[...] = part

        @pl.when(j > 0)
        def _():
            acc[...] += part

        @pl.when(j == n_col_tiles - 1)
        def _():
            _inproj_finish(acc[...], x_ref, g_ref, dres_ref, dx_ref, dg_ref, i)

    return pl.pallas_call(
        body, name=name,
        out_shape=(jax.ShapeDtypeStruct((rows, d), F32), jax.ShapeDtypeStruct((1, d), F32)),
        grid=(rows // tm, n_col_tiles),
        in_specs=[pl.BlockSpec((tm, tn), lambda i, j: (i, j)), pl.BlockSpec((1, d, tn), lambda i, j: (j, 0, 0)),
                  pl.BlockSpec((tm, d), lambda i, j: (i, 0)), _full((1, d)),
                  pl.BlockSpec((tm, d), lambda i, j: (i, 0))],
        out_specs=(pl.BlockSpec((tm, d), lambda i, j: (i, 0)), _full((1, d))),
        scratch_shapes=[pltpu.VMEM((tm, d), F32)],
        compiler_params=_params("arbitrary", "arbitrary"),
    )(dproj, w, x, gain, dres)


def _inproj_finish(dh, x_ref, g_ref, dres_ref, dx_ref, dg_ref, step):
    xv = x_ref[...]
    rstd = lax.rsqrt(jnp.mean(xv * xv, axis=-1, keepdims=True) + NORM_EPS)
    xhat = xv * rstd
    dxn = dh * g_ref[...]
    dx_ref[...] = dres_ref[...] + rstd * (dxn - xhat * jnp.mean(dxn * xhat, axis=-1, keepdims=True))
    _accumulate(dg_ref, _colsum(dh * xhat), step)


def inproj_bwd_pieces(pieces, w, x, gain, dres, name):
    rows, d = x.shape
    tm = min(ROW_TILE, rows)
    n = len(pieces)
    widths = [p.shape[1] for p in pieces]
    starts = [sum(widths[:k]) for k in range(n)]
    assert sum(widths) == w.shape[2]

    def body(*refs):
        w_ref, x_ref, g_ref, dres_ref, dx_ref, dg_ref = refs[n:]
        dh = None
        for k in range(n):
            part = _dot_nt(refs[k][...], w_ref[0, :, starts[k]:starts[k] + widths[k]])
            dh = part if dh is None else dh + part
        _inproj_finish(dh, x_ref, g_ref, dres_ref, dx_ref, dg_ref, pl.program_id(0))

    row = lambda cols: pl.BlockSpec((tm, cols), lambda i: (i, 0))
    return pl.pallas_call(
        body, name=name,
        out_shape=(jax.ShapeDtypeStruct((rows, d), F32), jax.ShapeDtypeStruct((1, d), F32)),
        grid=(rows // tm,),
        in_specs=[row(wd) for wd in widths] + [_full(w.shape), row(d), _full((1, d)), row(d)],
        out_specs=(row(d), _full((1, d))),
        compiler_params=_params("arbitrary"),
    )(*pieces, w, x, gain, dres)


def matmul_dw_pieces(a, pieces, name):
    rows, m = a.shape
    tk = min(ROW_TILE, rows)
    n = len(pieces)

    def body(*refs):
        a_ref, ins, outs = refs[0], refs[1:1 + n], refs[1 + n:]
        av = a_ref[...]
        for k in range(n):
            _accumulate(outs[k], _dot_tn(av, ins[k][...]), pl.program_id(0))

    return pl.pallas_call(
        body, name=name,
        out_shape=[jax.ShapeDtypeStruct((m, p.shape[1]), F32) for p in pieces],
        grid=(rows // tk,),
        in_specs=[pl.BlockSpec((tk, m), lambda k: (k, 0))]
        + [pl.BlockSpec((tk, p.shape[1]), lambda k: (k, 0)) for p in pieces],
        out_specs=[_full((m, p.shape[1])) for p in pieces],
        compiler_params=_params("arbitrary"),
    )(a, *pieces)


def matmul_dw(a, b, bn, name):
    rows, m = a.shape
    n = b.shape[1]
    tk = min(ROW_TILE, rows)
    steps = rows // tk

    def body(a_ref, b_ref, o_ref):
        part = _dot_tn(a_ref[...], b_ref[...])

        @pl.when(pl.program_id(1) == 0)
        def _():
            o_ref[0] = part

        @pl.when(pl.program_id(1) > 0)
        def _():
            o_ref[0] += part

    return pl.pallas_call(
        body, name=name,
        out_shape=jax.ShapeDtypeStruct((n // bn, m, bn), F32),
        grid=(n // bn, steps),
        in_specs=[pl.BlockSpec((tk, m), lambda j, k: (k, 0)), pl.BlockSpec((tk, bn), lambda j, k: (k, j))],
        out_specs=pl.BlockSpec((1, m, bn), lambda j, k: (j, 0, 0)),
        compiler_params=_params("parallel", "arbitrary"),
    )(a, b)


def _scan(a, b, reverse):
    n = a.shape[0]
    row = lax.broadcasted_iota(jnp.int32, a.shape, 0)
    s = 1
    while s < n:
        if reverse:
            a_s, b_s, valid = pltpu.roll(a, n - s, 0), pltpu.roll(b, n - s, 0), row < n - s
        else:
            a_s, b_s, valid = pltpu.roll(a, s, 0), pltpu.roll(b, s, 0), row >= s
        b = jnp.where(valid, a * b_s + b, b)
        a = jnp.where(valid, a * a_s, a)
        s *= 2
    return a, b


def _rg_gates(ua, gw_ref, gb, lam):
    ub = ua.astype(BF16)
    pre_r, pre_i = [], []
    for h in range(RG_HEADS):
        z = _dot(ub[:, h * RG_HEAD_DIM:(h + 1) * RG_HEAD_DIM], gw_ref[h])
        pre_r.append(z[:, :RG_HEAD_DIM])
        pre_i.append(z[:, RG_HEAD_DIM:])
    r = _sigmoid(jnp.concatenate(pre_r, axis=1) + gb[0:1])
    i = _sigmoid(jnp.concatenate(pre_i, axis=1) + gb[1:2])
    sp = _softplus(-lam)
    log_a = -RG_C * r * sp
    a = jnp.exp(log_a)
    mult = jnp.sqrt(_one_minus_exp(2.0 * log_a))
    return r, i, sp, a, mult


def _rg_weight_specs():
    return [_full((4, D_MODEL)), _full((1, D_MODEL)), _full((RG_HEADS, RG_HEAD_DIM, 2 * RG_HEAD_DIM)),
            _full((2, D_MODEL)), _full((1, D_MODEL))]


def rglru_fwd(proj, conv_w, conv_b, gate_w, gate_b, lam, reverse, name):
    rows_total = proj.shape[0]
    rows = min(SCAN_TILE, rows_total)
    n_tiles = rows_total // rows
    tix = (lambda i: n_tiles - 1 - i) if reverse else (lambda i: i)

    def body(xp, xm, xn, cw_ref, cb_ref, gw_ref, gb_ref, lam_ref, h_ref, carry):
        i = pl.program_id(0)
        t = tix(i)
        ext = _extend(xp, xm, xn, t == 0, t == n_tiles - 1)
        ua = _conv(ext, cw_ref[...], 2, rows) + cb_ref[...]
        _, gi, _, a, mult = _rg_gates(ua, gw_ref, gb_ref[...], lam_ref[...])
        b = mult * (gi * ua)
        a_cum, h0 = _scan(a, b, reverse)

        @pl.when(i == 0)
        def _():
            carry[...] = jnp.zeros_like(carry)

        h = a_cum * carry[0:1] + h0
        h_ref[...] = h
        edge = h[0:1] if reverse else h[rows - 1:rows]
        carry[...] = jnp.broadcast_to(edge, carry.shape)

    return pl.pallas_call(
        body, name=name,
        out_shape=jax.ShapeDtypeStruct((rows_total, D_MODEL), F32),
        grid=(n_tiles,),
        in_specs=_halo_specs(rows, D_MODEL, 0, n_tiles, tix) + _rg_weight_specs(),
        out_specs=pl.BlockSpec((rows, D_MODEL), lambda i: (tix(i), 0)),
        scratch_shapes=[pltpu.VMEM((SUBLANES, D_MODEL), F32)],
        compiler_params=_params("arbitrary"),
    )(proj, proj, proj, conv_w, conv_b, gate_w, gate_b, lam)


def rglru_bwd(proj, dycat, h_dir, conv_w, conv_b, gate_w, gate_b, lam, reverse, name):
    rows_total = proj.shape[0]
    rows = min(SCAN_TILE, rows_total)
    n_tiles = rows_total // rows
    tix = (lambda i: i) if reverse else (lambda i: n_tiles - 1 - i)
    za_block = 1

    def body(xp, xm, xn, za_ref, dya_ref, hp, hm, hn, cw_ref, cb_ref, gw_ref, gb_ref, lam_ref,
             dua_ref, dgw_ref, dgb_ref, dlam_ref, carry):
        step = pl.program_id(0)
        t = tix(step)
        first, last = t == 0, t == n_tiles - 1
        ext = _extend(xp, xm, xn, first, last)
        ua = _conv(ext, cw_ref[...], 2, rows) + cb_ref[...]
        lam_v = lam_ref[...]
        r, gi, sp, a, mult = _rg_gates(ua, gw_ref, gb_ref[...], lam_v)
        za = za_ref[...]
        dh = dya_ref[...] * (za * _sigmoid(za))

        @pl.when(step == 0)
        def _():
            carry[...] = jnp.zeros_like(carry)

        a_cum, mu0 = _scan(a, a * dh, not reverse)
        old = carry[0:1]
        mu = a_cum * old + mu0
        row = lax.broadcasted_iota(jnp.int32, mu.shape, 0)
        if reverse:
            mu_next = jnp.where(row == 0, old, pltpu.roll(mu, 1, 0))
            carry[...] = jnp.broadcast_to(mu[rows - 1:rows], carry.shape)
            h_ext = _extend(hp, hm, hn, first, last)
            h_prev = _shifted(h_ext, 1, rows)
        else:
            mu_next = jnp.where(row == rows - 1, old, pltpu.roll(mu, rows - 1, 0))
            carry[...] = jnp.broadcast_to(mu[0:1], carry.shape)
            h_ext = _extend(hp, hm, hn, first, last)
            h_prev = _shifted(h_ext, -1, rows)
        db = dh + mu_next
        da = db * h_prev
        d_mult = db * (gi * ua)
        di = db * (mult * ua)
        dua = db * (mult * gi)
        dlog_a = da * a - d_mult * (a * a) / mult
        dr = dlog_a * (-RG_C * sp)
        dlam = _colsum(dlog_a * (-RG_C * r)) * (-_sigmoid(-lam_v))
        dpr = dr * (r * (1.0 - r))
        dpi = di * (gi * (1.0 - gi))
        dgb = jnp.concatenate([_colsum(dpr), _colsum(dpi)], axis=0)
        ub = ua.astype(BF16)
        dua_heads, dgw_heads = [], []
        for h in range(RG_HEADS):
            cols = slice(h * RG_HEAD_DIM, (h + 1) * RG_HEAD_DIM)
            dz = jnp.concatenate([dpr[:, cols], dpi[:, cols]], axis=1).astype(BF16)
            dgw_heads.append(_dot_tn(ub[:, cols], dz))
            dua_heads.append(_dot_nt(dz, gw_ref[h]))
        dua_ref[...] = dua + jnp.concatenate(dua_heads, axis=1)

        @pl.when(step == 0)
        def _():
            for h in range(RG_HEADS):
                dgw_ref[h] = dgw_heads[h]
            dgb_ref[...] = dgb
            dlam_ref[...] = dlam

        @pl.when(step > 0)
        def _():
            for h in range(RG_HEADS):
                dgw_ref[h] += dgw_heads[h]
            dgb_ref[...] += dgb
            dlam_ref[...] += dlam

    row_spec = lambda col: pl.BlockSpec((rows, D_MODEL), lambda i: (tix(i), col))
    return pl.pallas_call(
        body, name=name,
        out_shape=(jax.ShapeDtypeStruct((rows_total, D_MODEL), F32),
                   jax.ShapeDtypeStruct((RG_HEADS, RG_HEAD_DIM, 2 * RG_HEAD_DIM), F32),
                   jax.ShapeDtypeStruct((2, D_MODEL), F32), jax.ShapeDtypeStruct((1, D_MODEL), F32)),
        grid=(n_tiles,),
        in_specs=(_halo_specs(rows, D_MODEL, 0, n_tiles, tix) + [row_spec(za_block), row_spec(0)]
                  + _halo_specs(rows, D_MODEL, 0, n_tiles, tix) + _rg_weight_specs()),
        out_specs=(row_spec(0), _full((RG_HEADS, RG_HEAD_DIM, 2 * RG_HEAD_DIM)), _full((2, D_MODEL)),
                   _full((1, D_MODEL))),
        scratch_shapes=[pltpu.VMEM((SUBLANES, D_MODEL), F32)],
        compiler_params=_params("arbitrary"),
    )(proj, proj, proj, proj, dycat, h_dir, h_dir, h_dir, conv_w, conv_b, gate_w, gate_b, lam)


def even_mix_fwd(proj, h_f, h_b, sc_w, name):
    rows_total = proj.shape[0]
    rows = min(MIX_TILE, rows_total)
    n_tiles = rows_total // rows
    cb = D_MODEL
    n_cb = 1
    ident = lambda i: i

    def body(za_ref, hf_ref, hb_ref, xbp, xbm, xbn, gcp, gcm, gcn, gb_ref, zb_ref, w_ref, y_ref):
        t = pl.program_id(1)
        first, last = t == 0, t == n_tiles - 1
        za = za_ref[...]
        y_ref[:, 0:cb] = ((hf_ref[...] + hb_ref[...]) * (za * _sigmoid(za))).astype(BF16)
        p_ext = _extend(xbp, xbm, xbn, first, last) * _extend(gcp, gcm, gcn, first, last)
        cv = _conv(p_ext, w_ref[...], 1, rows)
        zb = zb_ref[...]
        y_ref[:, cb:2 * cb] = (gb_ref[...] * cv * (zb * _sigmoid(zb))).astype(BF16)

    blk = lambda col: pl.BlockSpec((rows, cb), lambda c, i: (i, col * n_cb + c))
    own = pl.BlockSpec((rows, cb), lambda c, i: (i, c))
    return pl.pallas_call(
        body, name=name,
        out_shape=jax.ShapeDtypeStruct((rows_total, 2 * D_MODEL), BF16),
        grid=(n_cb, n_tiles),
        in_specs=([blk(1), own, own] + _halo_specs(rows, cb, 2 * n_cb, n_tiles, ident)
                  + _halo_specs(rows, cb, 4 * n_cb, n_tiles, ident)
                  + [blk(3), blk(5), pl.BlockSpec((3, cb), lambda c, i: (0, c))]),
        out_specs=pl.BlockSpec((rows, 2 * cb), lambda c, i: (i, 0)),
        compiler_params=_params("parallel", "arbitrary"),
    )(proj, h_f, h_b, proj, proj, proj, proj, proj, proj, proj, proj, sc_w)


def even_mix_bwd(proj, dycat, h_f, h_b, dua_f, dua_b, conv_w, sc_w, name):
    rows_total = proj.shape[0]
    rows = min(MIX_TILE, rows_total)
    n_tiles = rows_total // rows
    cb = D_MODEL
    n_cb = 1
    ident = lambda i: i

    def body(xap, xam, xan, za_ref, xbp, xbm, xbn, gbp, gbm, gbn, gcp, gcm, gcn, zbp, zbm, zbn,
             dya_ref, dybp, dybm, dybn, hf_ref, hb_ref, dfp, dfm, dfn, dbp, dbm, dbn, cw_ref, sw_ref,
             dp_ref, dcw_ref, dcb_ref, dsw_ref):
        def put(k, value):
            dp_ref[:, k * cb:(k + 1) * cb] = value.astype(BF16)

        t = pl.program_id(1)
        first, last = t == 0, t == n_tiles - 1
        za = za_ref[...]
        sa = _sigmoid(za)
        put(1, dya_ref[...] * (hf_ref[...] + hb_ref[...]) * (sa * (1.0 + za * (1.0 - sa))))
        dua_ext = _extend(dfp, dfm, dfn, first, last) + _extend(dbp, dbm, dbn, first, last)
        cw = cw_ref[...]
        put(0, _conv_transpose(dua_ext, cw, 2, rows))
        dua = dua_ext[SUBLANES:SUBLANES + rows]
        xa_ext = _extend(xap, xam, xan, first, last)
        dcw = jnp.concatenate([_colsum(dua * _shifted(xa_ext, k - 2, rows)) for k in range(4)], axis=0)
        dcb = _colsum(dua)
        xb_ext = _extend(xbp, xbm, xbn, first, last)
        gc_ext = _extend(gcp, gcm, gcn, first, last)
        p_ext = xb_ext * gc_ext
        zb_ext = _extend(zbp, zbm, zbn, first, last)
        sb_ext = _sigmoid(zb_ext)
        dyb_ext = _extend(dybp, dybm, dybn, first, last)
        gb_ext = _extend(gbp, gbm, gbn, first, last)
        dcv_ext = dyb_ext * gb_ext * (zb_ext * sb_ext)
        sw = sw_ref[...]
        cv = _conv(p_ext, sw, 1, rows)
        mid = slice(SUBLANES, SUBLANES + rows)
        zb, sb, dyb, gb = zb_ext[mid], sb_ext[mid], dyb_ext[mid], gb_ext[mid]
        put(3, dyb * cv * (zb * sb))
        put(5, dyb * gb * cv * (sb * (1.0 + zb * (1.0 - sb))))
        dp = _conv_transpose(dcv_ext, sw, 1, rows)
        put(4, dp * xb_ext[mid])
        put(2, dp * gc_ext[mid])
        dcv = dcv_ext[mid]
        dsw = jnp.concatenate([_colsum(dcv * _shifted(p_ext, k - 1, rows)) for k in range(3)], axis=0)

        @pl.when(t == 0)
        def _():
            dcw_ref[...] = dcw
            dcb_ref[...] = dcb
            dsw_ref[...] = dsw

        @pl.when(t > 0)
        def _():
            dcw_ref[...] += dcw
            dcb_ref[...] += dcb
            dsw_ref[...] += dsw

    blk = lambda col: pl.BlockSpec((rows, cb), lambda c, i: (i, col * n_cb + c))
    halo = lambda col: _halo_specs(rows, cb, col * n_cb, n_tiles, ident)
    own = pl.BlockSpec((rows, cb), lambda c, i: (i, c))
    wspec = lambda k: pl.BlockSpec((k, cb), lambda c, i: (0, c))
    return pl.pallas_call(
        body, name=name,
        out_shape=(jax.ShapeDtypeStruct((rows_total, 6 * D_MODEL), BF16),
                   jax.ShapeDtypeStruct((4, D_MODEL), F32), jax.ShapeDtypeStruct((1, D_MODEL), F32),
                   jax.ShapeDtypeStruct((3, D_MODEL), F32)),
        grid=(n_cb, n_tiles),
        in_specs=(halo(0) + [blk(1)] + halo(2) + halo(3) + halo(4) + halo(5) + [blk(0)] + halo(1)
                  + [own, own] + halo(0) + halo(0) + [wspec(4), wspec(3)]),
        out_specs=(pl.BlockSpec((rows, 6 * cb), lambda c, i: (i, 0)), wspec(4), wspec(1), wspec(3)),
        compiler_params=_params("parallel", "arbitrary"),
    )(proj, proj, proj, proj, proj, proj, proj, proj, proj, proj, proj, proj, proj, proj, proj, proj,
      dycat, dycat, dycat, dycat, h_f, h_b, dua_f, dua_f, dua_f, dua_b, dua_b, dua_b, conv_w, sc_w)


def even_out_fwd(ycat, w_out, gain, x, name):
    rows, d = x.shape
    k = ycat.shape[1]
    tm = min(ROW_TILE, rows)

    def body(yc_ref, w_ref, g_ref, x_ref, x1_ref, y_ref):
        y = _dot(yc_ref[...], w_ref[...])
        y_ref[...] = y
        rstd = lax.rsqrt(jnp.mean(y * y, axis=-1, keepdims=True) + NORM_EPS)
        x1_ref[...] = x_ref[...] + y * rstd * g_ref[...]

    row = lambda n: pl.BlockSpec((tm, n), lambda i: (i, 0))
    return pl.pallas_call(
        body, name=name,
        out_shape=(jax.ShapeDtypeStruct((rows, d), F32),) * 2,
        grid=(rows // tm,),
        in_specs=[row(k), _full((k, d)), _full((1, d)), row(d)],
        out_specs=(row(d), row(d)),
        compiler_params=_params("parallel"),
    )(ycat, w_out, gain, x)


def _rmsnorm_bwd(dout, y, gain):
    rstd = lax.rsqrt(jnp.mean(y * y, axis=-1, keepdims=True) + NORM_EPS)
    yhat = y * rstd
    dyn = dout * gain
    dy = rstd * (dyn - yhat * jnp.mean(dyn * yhat, axis=-1, keepdims=True))
    return dy, dout * yhat


def even_out_bwd(dx1, y, gain, w_out, name):
    rows, d = y.shape
    k = w_out.shape[0]
    tm = min(ROW_TILE, rows)

    def body(dx_ref, y_ref, g_ref, w_ref, dy_ref, dyc_ref, dg_ref):
        dy, dg_rows = _rmsnorm_bwd(dx_ref[...], y_ref[...], g_ref[...])
        dyb = dy.astype(BF16)
        dy_ref[...] = dyb
        dyc_ref[...] = _dot_nt(dyb, w_ref[...])
        _accumulate(dg_ref, _colsum(dg_rows), pl.program_id(0))

    row = lambda n: pl.BlockSpec((tm, n), lambda i: (i, 0))
    return pl.pallas_call(
        body, name=name,
        out_shape=(jax.ShapeDtypeStruct((rows, d), BF16), jax.ShapeDtypeStruct((rows, k), F32),
                   jax.ShapeDtypeStruct((1, d), F32)),
        grid=(rows // tm,),
        in_specs=[row(d), row(d), _full((1, d)), _full((k, d))],
        out_specs=(row(d), row(k), _full((1, d))),
        compiler_params=_params("arbitrary"),
    )(dx1, y, gain, w_out)


def _chunk_cumsum(g, reverse):
    n = g.shape[0]
    pos = lax.broadcasted_iota(jnp.int32, g.shape, 0) % GLA_CHUNK
    s = 1
    while s < GLA_CHUNK:
        if reverse:
            g = g + jnp.where(pos < GLA_CHUNK - s, pltpu.roll(g, n - s, 0), 0.0)
        else:
            g = g + jnp.where(pos >= s, pltpu.roll(g, s, 0), 0.0)
        s *= 2
    return g


def _gla_prepare(q_ref, k_ref, lr_ref, wg_ref, bg_ref, reverse, n_chunks):
    z = _dot(lr_ref[...].astype(BF16), wg_ref[0]) + bg_ref[0]
    g = -_softplus(-z) * (1.0 / GLA_NORMALIZER)
    bcum = _chunk_cumsum(g, reverse).reshape(n_chunks, GLA_CHUNK, GLA_DK)
    edge = 0 if reverse else GLA_CHUNK - 1
    btot = bcum[:, edge:edge + 1, :]
    e_pos = jnp.exp(bcum)
    e_neg = jnp.exp(-bcum)
    e_st = jnp.exp(btot - bcum)
    q3 = q_ref[...].reshape(n_chunks, GLA_CHUNK, GLA_DK)
    k3 = k_ref[...].reshape(n_chunks, GLA_CHUNK, GLA_DK)
    scale = GLA_DK ** -0.5
    q_in = q3 * scale * e_pos
    k_in = k3 * e_neg
    k_st = k3 * e_st
    dec = jnp.exp(btot)
    return z, q_in, k_in, k_st, dec, (scale * e_pos, e_neg, e_st)


def _gla_mask(reverse):
    i = lax.broadcasted_iota(jnp.int32, (GLA_CHUNK, GLA_CHUNK), 0)
    j = lax.broadcasted_iota(jnp.int32, (GLA_CHUNK, GLA_CHUNK), 1)
    return (j >= i) if reverse else (j <= i)


def _gla_specs(rows, n_blocks, reverse):
    tix = (lambda s: n_blocks - 1 - s) if reverse else (lambda s: s)
    d = 1 if reverse else 0
    lr_block = LR_COL // LANES
    specs = [pl.BlockSpec((rows, GLA_DK), lambda h, s: (tix(s), h)),
             pl.BlockSpec((rows, GLA_DK), lambda h, s: (tix(s), GLA_HEADS + h)),
             pl.BlockSpec((rows, GLA_DV), lambda h, s: (tix(s), GLA_HEADS + h)),
             pl.BlockSpec((rows, LANES), lambda h, s: (tix(s), lr_block)),
             pl.BlockSpec((1, LANES, GLA_DK), lambda h, s: (d, 0, h)),
             pl.BlockSpec((1, 1, GLA_DK), lambda h, s: (d, 0, h))]
    return specs, tix


def gla_fwd(proj, wg_pad, bg, reverse, name):
    rows_total = proj.shape[0]
    rows = min(GLA_BLOCK, rows_total)
    n_blocks = rows_total // rows
    n_chunks = rows // GLA_CHUNK
    specs, tix = _gla_specs(rows, n_blocks, reverse)

    def body(q_ref, k_ref, v_ref, lr_ref, wg_ref, bg_ref, o_ref, st_ref, state, kv_scr, dec_scr):
        _, q_in, k_in, k_st, dec, _ = _gla_prepare(q_ref, k_ref, lr_ref, wg_ref, bg_ref, reverse, n_chunks)
        vb = v_ref[...].reshape(n_chunks, GLA_CHUNK, GLA_DV).astype(BF16)
        qb = q_in.astype(BF16)
        p = jnp.where(_gla_mask(reverse), _bdot(qb, k_in.astype(BF16), 2, 2), 0.0)
        o = _bdot(p.astype(BF16), vb, 2, 1)
        kv_scr[...] = _bdot(vb, k_st.astype(BF16), 1, 1)
        dec_scr[...] = jnp.broadcast_to(dec, dec_scr.shape)

        @pl.when(pl.program_id(1) == 0)
        def _():
            state[...] = jnp.zeros_like(state)

        for c in range(n_chunks):
            cc = n_chunks - 1 - c if reverse else c
            st_ref[0, cc] = state[...]
            state[...] = state[...] * dec_scr[cc, 0:1] + kv_scr[cc]
        o = o + _bdot(qb, st_ref[0].astype(BF16), 2, 2)
        o_ref[...] = o.reshape(rows, GLA_DV)

    return pl.pallas_call(
        body, name=name,
        out_shape=(jax.ShapeDtypeStruct((rows_total, GLA_HEADS * GLA_DV), F32),
                   jax.ShapeDtypeStruct((GLA_HEADS, rows_total // GLA_CHUNK, GLA_DV, GLA_DK), F32)),
        grid=(GLA_HEADS, n_blocks),
        in_specs=specs,
        out_specs=(pl.BlockSpec((rows, GLA_DV), lambda h, s: (tix(s), h)),
                   pl.BlockSpec((1, n_chunks, GLA_DV, GLA_DK), lambda h, s: (h, tix(s), 0, 0))),
        scratch_shapes=[pltpu.VMEM((GLA_DV, GLA_DK), F32), pltpu.VMEM((n_chunks, GLA_DV, GLA_DK), F32),
                        pltpu.VMEM((n_chunks, SUBLANES, GLA_DK), F32)],
        compiler_params=_params("parallel", "arbitrary"),
    )(proj, proj, proj, proj, wg_pad, bg)


def gla_bwd(proj, wg_pad, bg, d_o, states, dqkv_in, reverse, name):
    rows_total = proj.shape[0]
    rows = min(GLA_BLOCK, rows_total)
    n_blocks = rows_total // rows
    n_chunks = rows // GLA_CHUNK
    specs, tix = _gla_specs(rows, n_blocks, not reverse)
    d = 1 if reverse else 0
    specs[4] = pl.BlockSpec((1, LANES, GLA_DK), lambda h, s: (d, 0, h))
    specs[5] = pl.BlockSpec((1, 1, GLA_DK), lambda h, s: (d, 0, h))
    add = dqkv_in is not None

    def body(*refs):
        q_ref, k_ref, v_ref, lr_ref, wg_ref, bg_ref, do_ref, st_ref = refs[:8]
        refs = refs[8:]
        if add:
            aq_ref, ak_ref, av_ref = refs[:3]
            refs = refs[3:]
        dq_ref, dk_ref, dv_ref, dz_ref, dstate, g_scr, dec_scr, dsn_scr = refs
        z, q_in, k_in, k_st, dec, (f_q, f_k, f_s) = _gla_prepare(q_ref, k_ref, lr_ref, wg_ref, bg_ref, reverse,
                                                                 n_chunks)
        mask = _gla_mask(reverse)
        vb = v_ref[...].reshape(n_chunks, GLA_CHUNK, GLA_DV).astype(BF16)
        dob = do_ref[...].reshape(n_chunks, GLA_CHUNK, GLA_DV).astype(BF16)
        qb, kb, ksb = q_in.astype(BF16), k_in.astype(BF16), k_st.astype(BF16)
        st = st_ref[0]
        stb = st.astype(BF16)
        pb = jnp.where(mask, _bdot(qb, kb, 2, 2), 0.0).astype(BF16)
        dpb = jnp.where(mask, _bdot(dob, vb, 2, 2), 0.0).astype(BF16)
        d_qin = _bdot(dpb, kb, 2, 1) + _bdot(dob, stb, 2, 1)
        d_kin = _bdot(dpb, qb, 1, 1)
        dv = _bdot(pb, dob, 1, 1)
        g_scr[...] = _bdot(dob, qb, 1, 1)
        dec_scr[...] = jnp.broadcast_to(dec, dec_scr.shape)

        @pl.when(pl.program_id(1) == 0)
        def _():
            dstate[...] = jnp.zeros_like(dstate)

        for c in range(n_chunks):
            cc = c if reverse else n_chunks - 1 - c
            dsn_scr[cc] = dstate[...]
            dstate[...] = dstate[...] * dec_scr[cc, 0:1] + g_scr[cc]
        dsn = dsn_scr[...]
        dsnb = dsn.astype(BF16)
        dv = dv + _bdot(ksb, dsnb, 2, 2)
        d_kst = _bdot(vb, dsnb, 2, 1)
        d_dec = jnp.sum(dsn * st, axis=1, keepdims=True)
        ks_term = d_kst * k_st
        d_btot = d_dec * dec + jnp.sum(ks_term, axis=1, keepdims=True)
        d_b = d_qin * q_in - d_kin * k_in - ks_term
        pos = lax.broadcasted_iota(jnp.int32, d_b.shape, 1)
        edge = 0 if reverse else GLA_CHUNK - 1
        d_b = d_b + jnp.where(pos == edge, d_btot, 0.0)
        dg = _chunk_cumsum(d_b.reshape(rows, GLA_DK), not reverse)
        dz_ref[...] = dg * (1.0 / GLA_NORMALIZER) * _sigmoid(-z)
        dq = (d_qin * f_q).reshape(rows, GLA_DK)
        dk = (d_kin * f_k + d_kst * f_s).reshape(rows, GLA_DK)
        dv = dv.reshape(rows, GLA_DV)
        if add:
            dq_ref[...] = (dq + aq_ref[...]).astype(BF16)
            dk_ref[...] = (dk + ak_ref[...]).astype(BF16)
            dv_ref[...] = (dv + av_ref[...]).astype(BF16)
        else:
            dq_ref[...] = dq
            dk_ref[...] = dk
            dv_ref[...] = dv

    qkv_specs = [pl.BlockSpec((rows, GLA_DK), lambda h, s: (tix(s), h)),
                 pl.BlockSpec((rows, GLA_DK), lambda h, s: (tix(s), h)),
                 pl.BlockSpec((rows, GLA_DV), lambda h, s: (tix(s), h))]
    in_specs = specs + [pl.BlockSpec((rows, GLA_DV), lambda h, s: (tix(s), h)),
                        pl.BlockSpec((1, n_chunks, GLA_DV, GLA_DK), lambda h, s: (h, tix(s), 0, 0))]
    args = [proj, proj, proj, proj, wg_pad, bg, d_o, states]
    out_dtype = F32
    if add:
        in_specs += qkv_specs
        args += list(dqkv_in)
        out_dtype = BF16
    return pl.pallas_call(
        body, name=name,
        out_shape=(jax.ShapeDtypeStruct((rows_total, GLA_HEADS * GLA_DK), out_dtype),
                   jax.ShapeDtypeStruct((rows_total, GLA_HEADS * GLA_DK), out_dtype),
                   jax.ShapeDtypeStruct((rows_total, GLA_HEADS * GLA_DV), out_dtype),
                   jax.ShapeDtypeStruct((rows_total, GLA_HEADS * GLA_DK), F32)),
        grid=(GLA_HEADS, n_blocks),
        in_specs=in_specs,
        out_specs=(pl.BlockSpec((rows, GLA_DK), lambda h, s: (tix(s), h)),
                   pl.BlockSpec((rows, GLA_DK), lambda h, s: (tix(s), h)),
                   pl.BlockSpec((rows, GLA_DV), lambda h, s: (tix(s), h)),
                   pl.BlockSpec((rows, GLA_DK), lambda h, s: (tix(s), h))),
        scratch_shapes=[pltpu.VMEM((GLA_DV, GLA_DK), F32), pltpu.VMEM((n_chunks, GLA_DV, GLA_DK), F32),
                        pltpu.VMEM((n_chunks, SUBLANES, GLA_DK), F32),
                        pltpu.VMEM((n_chunks, GLA_DV, GLA_DK), F32)],
        compiler_params=_params("parallel", "arbitrary"),
    )(*args)


def gla_gate_bwd(proj, dz_f, dz_b, wg_pad, name):
    rows_total = proj.shape[0]
    tm = min(ROW_TILE, rows_total)
    n_key = GLA_HEADS * GLA_DK

    def body(lr_ref, dzf_ref, dzb_ref, wg_ref, dlr_ref, dwg_ref, dbg_ref):
        step = pl.program_id(0)
        lr_t = jnp.transpose(lr_ref[...])
        dzf, dzb = dzf_ref[...], dzb_ref[...]
        dzf16, dzb16 = dzf.astype(BF16), dzb.astype(BF16)
        dlr_ref[...] = (_dot_nt(dzf16, wg_ref[0]) + _dot_nt(dzb16, wg_ref[1])).astype(BF16)
        dwf = _dot(lr_t[0:GLA_RANK].astype(BF16), dzf16)
        dwb = _dot(lr_t[GLA_RANK:2 * GLA_RANK].astype(BF16), dzb16)
        dbg = jnp.concatenate([_colsum(dzf), _colsum(dzb)], axis=0)

        @pl.when(step == 0)
        def _():
            dwg_ref[0] = dwf
            dwg_ref[1] = dwb
            dbg_ref[...] = dbg

        @pl.when(step > 0)
        def _():
            dwg_ref[0] += dwf
            dwg_ref[1] += dwb
            dbg_ref[...] += dbg

    return pl.pallas_call(
        body, name=name,
        out_shape=(jax.ShapeDtypeStruct((rows_total, LANES), BF16), jax.ShapeDtypeStruct((2, GLA_RANK, n_key), F32),
                   jax.ShapeDtypeStruct((2, n_key), F32)),
        grid=(rows_total // tm,),
        in_specs=[pl.BlockSpec((tm, LANES), lambda i: (i, LR_COL // LANES)),
                  pl.BlockSpec((tm, n_key), lambda i: (i, 0)), pl.BlockSpec((tm, n_key), lambda i: (i, 0)),
                  _full((2, LANES, n_key))],
        out_specs=(pl.BlockSpec((tm, LANES), lambda i: (i, 0)), _full((2, GLA_RANK, n_key)), _full((2, n_key))),
        compiler_params=_params("arbitrary"),
    )(proj, dz_f, dz_b, wg_pad)


def _head_norm(o, gain):
    outs, hats, rstds = [], [], []
    for h in range(GLA_HEADS):
        oh = o[:, h * GLA_DV:(h + 1) * GLA_DV]
        rstd = lax.rsqrt(jnp.mean(oh * oh, axis=-1, keepdims=True) + NORM_EPS)
        hat = oh * rstd
        outs.append(hat * gain)
        hats.append(hat)
        rstds.append(rstd)
    return outs, hats, rstds


def odd_out_fwd(o_f, o_b, proj, head_gain, w_out, gain, x1, target, name):
    rows, d = x1.shape
    tm = min(ROW_TILE, rows)
    r_block = (2 * GLA_HEADS * GLA_DK + GLA_HEADS * GLA_DV) // d

    def body(of_ref, ob_ref, r_ref, hg_ref, w_ref, g_ref, x1_ref, tgt_ref, y2_ref, dy_ref, dx2_ref, loss_ref,
             dg_ref):
        step = pl.program_id(0)
        on, _, _ = _head_norm(of_ref[...] + ob_ref[...], hg_ref[...])
        r = r_ref[...]
        y2 = (jnp.concatenate(on, axis=1) * (r * _sigmoid(r))).astype(BF16)
        y2_ref[...] = y2
        y = _dot(y2, w_ref[...])
        gain_v = g_ref[...]
        rstd = lax.rsqrt(jnp.mean(y * y, axis=-1, keepdims=True) + NORM_EPS)
        x2 = x1_ref[...] + y * rstd * gain_v
        diff = x2 - tgt_ref[...]
        loss = 0.5 * jnp.sum(jnp.mean(diff * diff, axis=-1, keepdims=True), axis=0, keepdims=True)
        dx2 = diff * (1.0 / d)
        dx2_ref[...] = dx2
        dy, dg_rows = _rmsnorm_bwd(dx2, y, gain_v)
        dy_ref[...] = dy.astype(BF16)
        _accumulate(loss_ref, jnp.broadcast_to(loss, loss_ref.shape), step)
        _accumulate(dg_ref, _colsum(dg_rows), step)

    row = lambda n, col=0: pl.BlockSpec((tm, n), lambda i: (i, col))
    return pl.pallas_call(
        body, name=name,
        out_shape=(jax.ShapeDtypeStruct((rows, d), BF16), jax.ShapeDtypeStruct((rows, d), BF16),
                   jax.ShapeDtypeStruct((rows, d), F32), jax.ShapeDtypeStruct((SUBLANES, LANES), F32),
                   jax.ShapeDtypeStruct((1, d), F32)),
        grid=(rows // tm,),
        in_specs=[row(d), row(d), row(d, r_block), _full((1, GLA_DV)), _full((d, d)), _full((1, d)), row(d), row(d)],
        out_specs=(row(d), row(d), row(d), _full((SUBLANES, LANES)), _full((1, d))),
        compiler_params=_params("arbitrary"),
    )(o_f, o_b, proj, head_gain, w_out, gain, x1, target)


def odd_out_bwd(dy, w_out, o_f, o_b, proj, head_gain, name):
    rows, d = dy.shape
    tm = min(ROW_TILE, rows)
    r_block = (2 * GLA_HEADS * GLA_DK + GLA_HEADS * GLA_DV) // d

    def body(dy_ref, w_ref, of_ref, ob_ref, r_ref, hg_ref, dr_ref, do_ref, dhg_ref):
        dy2 = _dot_nt(dy_ref[...], w_ref[...])
        hg = hg_ref[...]
        on, hats, rstds = _head_norm(of_ref[...] + ob_ref[...], hg)
        r = r_ref[...]
        sr = _sigmoid(r)
        dr_ref[...] = (dy2 * jnp.concatenate(on, axis=1) * (sr * (1.0 + r * (1.0 - sr)))).astype(BF16)
        d_on = dy2 * (r * sr)
        d_os, dhg = [], None
        for h in range(GLA_HEADS):
            dn = d_on[:, h * GLA_DV:(h + 1) * GLA_DV]
            part = _colsum(dn * hats[h])
            dhg = part if dhg is None else dhg + part
            dng = dn * hg
            d_os.append(rstds[h] * (dng - hats[h] * jnp.mean(dng * hats[h], axis=-1, keepdims=True)))
        do_ref[...] = jnp.concatenate(d_os, axis=1)
        _accumulate(dhg_ref, dhg, pl.program_id(0))

    row = lambda n, col=0: pl.BlockSpec((tm, n), lambda i: (i, col))
    return pl.pallas_call(
        body, name=name,
        out_shape=(jax.ShapeDtypeStruct((rows, d), BF16), jax.ShapeDtypeStruct((rows, d), F32),
                   jax.ShapeDtypeStruct((1, GLA_DV), F32)),
        grid=(rows // tm,),
        in_specs=[row(d), _full((d, d)), row(d), row(d), row(d, r_block), _full((1, GLA_DV))],
        out_specs=(row(d), row(d), _full((1, GLA_DV))),
        compiler_params=_params("arbitrary"),
    )(dy, w_out, o_f, o_b, proj, head_gain)


def local_step(x, target, w):
    g = {}
    proj_e, h0 = norm_matmul(x, w["even_norm_pre"], w["even_w_in"], "even_in_proj")
    h_dir = [rglru_fwd(proj_e, w["rg_conv_w"], w["rg_conv_b"], w["rg_gate_w"][d], w["rg_gate_b"][d],
                       w["rg_lambda"][d], d == 1, "rglru_fwd_%d" % d) for d in range(2)]
    ycat = even_mix_fwd(proj_e, h_dir[0], h_dir[1], w["sc_conv_w"], "even_mix_fwd")
    x1, y_e = even_out_fwd(ycat, w["even_w_out"], w["even_norm_post"], x, "even_out_fwd")
    proj_o, h1 = norm_matmul(x1, w["odd_norm_pre"], w["odd_w_in"], "odd_in_proj")
    o_dir, st_dir = [], []
    for d in range(2):
        o, st = gla_fwd(proj_o, w["gla_wg_pad"], w["gla_b_gate"], d == 1, "gla_fwd_%d" % d)
        o_dir.append(o)
        st_dir.append(st)
    y2, dy_o, dx2, loss, g["odd_norm_post"] = odd_out_fwd(
        o_dir[0], o_dir[1], proj_o, w["gla_norm_g"], w["odd_w_out"], w["odd_norm_post"], x1, target, "odd_out_fwd")
    g["odd_w_out"] = matmul_dw(y2, dy_o, D_MODEL, "odd_w_out_grad")[0]
    dr, d_o, g["gla_norm_g"] = odd_out_bwd(dy_o, w["odd_w_out"], o_dir[0], o_dir[1], proj_o, w["gla_norm_g"],
                                           "odd_out_bwd")
    dq, dk, dv, dz_f = gla_bwd(proj_o, w["gla_wg_pad"], w["gla_b_gate"], d_o, st_dir[0], None, False, "gla_bwd_0")
    dq, dk, dv, dz_b = gla_bwd(proj_o, w["gla_wg_pad"], w["gla_b_gate"], d_o, st_dir[1], (dq, dk, dv), True,
                               "gla_bwd_1")
    dlr, g["gla_w_gate_lr"], g["gla_b_gate"] = gla_gate_bwd(proj_o, dz_f, dz_b, w["gla_wg_pad"], "gla_gate_bwd")
    dproj_o = [dq, dk, dv, dr, dlr]
    g["odd_w_in"] = jnp.concatenate(matmul_dw_pieces(h1, dproj_o, "odd_w_in_grad"), axis=1)[:, :ODD_IN]
    dx1, g["odd_norm_pre"] = inproj_bwd_pieces(dproj_o, w["odd_w_in"], x1, w["odd_norm_pre"], dx2, "odd_in_proj_bwd")
    dy_e, dycat, g["even_norm_post"] = even_out_bwd(dx1, y_e, w["even_norm_post"], w["even_w_out"], "even_out_bwd")
    g["even_w_out"] = matmul_dw(ycat, dy_e, D_MODEL, "even_w_out_grad")[0]
    dua, dgw, dgb, dlam = [], [], [], []
    for d in range(2):
        a, b, c, e = rglru_bwd(proj_e, dycat, h_dir[d], w["rg_conv_w"], w["rg_conv_b"], w["rg_gate_w"][d],
                               w["rg_gate_b"][d], w["rg_lambda"][d], d == 1, "rglru_bwd_%d" % d)
        dua.append(a)
        dgw.append(b)
        dgb.append(c)
        dlam.append(e)
    dproj_e, g["rg_conv_w"], g["rg_conv_b"], g["sc_conv_w"] = even_mix_bwd(
        proj_e, dycat, h_dir[0], h_dir[1], dua[0], dua[1], w["rg_conv_w"], w["sc_conv_w"], "even_mix_bwd")
    dgw = jnp.stack(dgw).reshape(2, RG_HEADS, RG_HEAD_DIM, 2, RG_HEAD_DIM)
    g["rg_gate_w"] = jnp.transpose(dgw, (0, 3, 1, 2, 4))
    g["rg_gate_b"] = jnp.stack(dgb).reshape(2, 2, RG_HEADS, RG_HEAD_DIM)
    g["rg_lambda"] = jnp.concatenate(dlam, axis=0)
    g["even_w_in"] = matmul_dw(h0, dproj_e, EVEN_IN // 4, "even_w_in_grad")
    grad_x, g["even_norm_pre"] = inproj_bwd(dproj_e, w["even_w_in"], x, w["even_norm_pre"], dx1, "even_in_proj_bwd")
    return loss, grad_x, g


def _prepare_weights(full):
    w = {}
    for name in ("even_norm_pre", "even_norm_post", "rg_conv_b", "odd_norm_pre", "odd_norm_post", "gla_norm_g"):
        w[name] = full[name].reshape(1, -1)
    w["rg_conv_w"] = full["rg_conv_w"]
    w["sc_conv_w"] = full["sc_conv_w"]
    w["even_w_in"] = full["even_w_in"].astype(BF16)
    if w["even_w_in"].ndim == 2:
        w["even_w_in"] = jnp.transpose(w["even_w_in"].reshape(D_MODEL, 4, EVEN_IN // 4), (1, 0, 2))
    w["even_w_out"] = full["even_w_out"].astype(BF16)
    gw = jnp.transpose(full["rg_gate_w"].astype(BF16), (0, 2, 3, 1, 4))
    w["rg_gate_w"] = gw.reshape(2, RG_HEADS, RG_HEAD_DIM, 2 * RG_HEAD_DIM)
    w["rg_gate_b"] = full["rg_gate_b"].reshape(2, 2, D_MODEL)
    w["rg_lambda"] = full["rg_lambda"].reshape(2, 1, D_MODEL)
    w_in = jnp.pad(full["odd_w_in"].astype(BF16), ((0, 0), (0, ODD_IN_PAD - ODD_IN)))
    w["odd_w_in"] = w_in.reshape(1, D_MODEL, ODD_IN_PAD)
    w["odd_w_out"] = full["odd_w_out"].astype(BF16)
    wg = full["gla_w_gate_lr"].astype(BF16)
    w["gla_wg_pad"] = jnp.stack([jnp.pad(wg[d], ((d * GLA_RANK, LANES - (d + 1) * GLA_RANK), (0, 0)))
                                 for d in range(2)])
    w["gla_b_gate"] = full["gla_b_gate"].reshape(2, 1, GLA_HEADS * GLA_DK)
    return w


SHARDED_SMALL = (("rg_conv_w", (4, 256)), ("rg_lambda", (2, 256)), ("sc_conv_w", (3, 256)),
                 ("odd_norm_pre", (256,)), ("odd_norm_post", (256,)), ("gla_w_gate_lr", (2, 16, 128)),
                 ("gla_b_gate", (2, 128)), ("gla_norm_g", (64,)))
SHARDED_ROWS = 64
REPLICATED = (("even_norm_pre", (1024,)), ("even_norm_post", (1024,)), ("rg_conv_b", (1024,)),
              ("rg_gate_b", (2, 2, 8, 128)), ("rg_gate_w", (2, 2, 8, 128, 128)))
REPLICATED_ROWS = 4160
REP_PART = REPLICATED_ROWS // 8
HALF_SHARDED = SHARDED_ROWS // 2
PACK_HALF = HALF_SHARDED + REP_PART


def _seg_rows(shape):
    n = 1
    for s in shape:
        n *= s
    return -(-n // LANES)


def _pack(arrays, spec, total_rows, lead=()):
    parts = []
    for name, shape in spec:
        flat = arrays[name].reshape(lead + (-1,))
        pad = _seg_rows(shape) * LANES - flat.shape[-1]
        if pad:
            flat = jnp.pad(flat, [(0, 0)] * len(lead) + [(0, pad)])
        parts.append(flat.reshape(lead + (-1, LANES)))
    rows = jnp.concatenate(parts, axis=len(lead))
    pad = total_rows - rows.shape[len(lead)]
    return jnp.pad(rows, [(0, 0)] * len(lead) + [(0, pad), (0, 0)])


def _unpack(rows, spec, lead=()):
    out, at = {}, 0
    for name, shape in spec:
        n = 1
        for s in shape:
            n *= s
        k = _seg_rows(shape)
        seg = lax.slice_in_dim(rows, at, at + k, axis=len(lead)).reshape(lead + (-1,))
        out[name] = lax.slice_in_dim(seg, 0, n, axis=len(lead)).reshape(lead + shape)
        at += k
    return out


def _split_owners(arr):
    a = arr.reshape(arr.shape[:-1] + (4, arr.shape[-1] // 4))
    return jnp.moveaxis(a, -2, 0)


def _merge_owners(arr):
    a = jnp.moveaxis(arr, 0, -2)
    return a.reshape(a.shape[:-2] + (-1,))


HBM_SPEC = pl.BlockSpec(memory_space=pltpu.HBM)


def _position():
    x, y, c = lax.axis_index("x"), lax.axis_index("y"), lax.axis_index("c")
    chips = [(1 - x, y), (x, 1 - y), (1 - x, 1 - y)]
    return x, y, c, chips


def _remote(src, dst, send_sem, recv_sem, device):
    return pltpu.make_async_remote_copy(src_ref=src, dst_ref=dst, send_sem=send_sem, recv_sem=recv_sem,
                                        device_id=device, device_id_type=MESH)


def gather_weights(halved, whole):
    n_h, n_w = len(halved), len(whole)
    n = n_h + n_w

    def body(*refs):
        ins, outs = refs[:n], refs[n:2 * n]
        send_ici, recv_ici, send_fwd, recv_fwd, local_sem = refs[2 * n:]
        x, y, c, chips = _position()
        me = 2 * x + y
        sibling = (x, y, 1 - c)
        local = []
        for a in range(n):
            cp = pltpu.make_async_copy(ins[a], outs[a].at[me], local_sem.at[a])
            cp.start()
            local.append(cp)
        sends = []
        for a in range(n):
            for k, chip in enumerate(chips):
                src = ins[a].at[c] if a < n_h else ins[a]
                dst = outs[a].at[me, c] if a < n_h else outs[a].at[me]
                cp = _remote(src, dst, send_ici.at[3 * a + k], recv_ici.at[3 * a + k], (chip[0], chip[1], c))
                cp.start()
                sends.append(cp)
        for a in range(n):
            for k, chip in enumerate(chips):
                q = 2 * chip[0] + chip[1]
                landed = outs[a].at[q, c] if a < n_h else outs[a].at[q]
                _remote(landed, landed, send_ici.at[3 * a + k], recv_ici.at[3 * a + k], sibling).wait_recv()
                if a < n_h:
                    cp = _remote(landed, landed, send_fwd.at[3 * a + k], recv_fwd.at[3 * a + k], sibling)
                    cp.start()
                    sends.append(cp)
        for a in range(n_h):
            for k, chip in enumerate(chips):
                q = 2 * chip[0] + chip[1]
                passed = outs[a].at[q, 1 - c]
                _remote(passed, passed, send_fwd.at[3 * a + k], recv_fwd.at[3 * a + k], sibling).wait_recv()
        for cp in sends:
            cp.wait_send()
        for cp in local:
            cp.wait()

    arrays = list(halved) + list(whole)
    out_shape = [jax.ShapeDtypeStruct((4,) + a.shape, a.dtype) for a in arrays]
    outs = pl.pallas_call(
        body, name="gather_weights",
        out_shape=out_shape,
        in_specs=[HBM_SPEC] * n, out_specs=[HBM_SPEC] * n,
        scratch_shapes=[pltpu.SemaphoreType.DMA((3 * n,)), pltpu.SemaphoreType.DMA((3 * n,)),
                        pltpu.SemaphoreType.DMA((3 * n_h,)), pltpu.SemaphoreType.DMA((3 * n_h,)),
                        pltpu.SemaphoreType.DMA((n,))],
    )(*arrays)
    return outs[:n_h], outs[n_h:]


def exchange_with_sibling(arrays):
    n = len(arrays)

    def body(*refs):
        ins, outs = refs[:n], refs[n:2 * n]
        send_sems, recv_sems = refs[2 * n:]
        x, y, c, _ = _position()
        copies = []
        for a in range(n):
            cp = _remote(ins[a].at[:, 1 - c], outs[a], send_sems.at[a], recv_sems.at[a], (x, y, 1 - c))
            cp.start()
            copies.append(cp)
        for cp in copies:
            cp.wait()

    return pl.pallas_call(
        body, name="grad_exchange_sibling",
        out_shape=[jax.ShapeDtypeStruct((a.shape[0],) + a.shape[2:], a.dtype) for a in arrays],
        in_specs=[HBM_SPEC] * n, out_specs=[HBM_SPEC] * n,
        scratch_shapes=[pltpu.SemaphoreType.DMA((n,)), pltpu.SemaphoreType.DMA((n,))],
    )(*arrays)


def exchange_with_chips(arrays):
    n = len(arrays)

    def body(*refs):
        ins, outs = refs[:n], refs[n:2 * n]
        send_sems, recv_sems = refs[2 * n:]
        x, y, c, chips = _position()
        copies = []
        for a in range(n):
            for k, chip in enumerate(chips):
                q = 2 * chip[0] + chip[1]
                cp = _remote(ins[a].at[q], outs[a].at[k], send_sems.at[3 * a + k], recv_sems.at[3 * a + k],
                             (chip[0], chip[1], c))
                cp.start()
                copies.append(cp)
        for cp in copies:
            cp.wait()

    return pl.pallas_call(
        body, name="grad_exchange_chips",
        out_shape=[jax.ShapeDtypeStruct((3,) + a.shape[1:], a.dtype) for a in arrays],
        in_specs=[HBM_SPEC] * n, out_specs=[HBM_SPEC] * n,
        scratch_shapes=[pltpu.SemaphoreType.DMA((3 * n,)), pltpu.SemaphoreType.DMA((3 * n,))],
    )(*arrays)


def share_totals(totals, pack_total):
    arrays = list(totals) + [pack_total]
    n = len(arrays)

    def body(*refs):
        ins, outs, rep = refs[:n], refs[n:2 * n], refs[2 * n]
        send_sems, recv_sems, rep_send, rep_recv, local_sem = refs[2 * n + 1:]
        x, y, c, chips = _position()
        sibling = (x, y, 1 - c)
        local, sends = [], []
        for a in range(n):
            cp = pltpu.make_async_copy(ins[a], outs[a].at[c], local_sem.at[a])
            cp.start()
            local.append(cp)
            cp = _remote(ins[a], outs[a].at[c], send_sems.at[a], recv_sems.at[a], sibling)
            cp.start()
            sends.append(cp)
        mine = ins[n - 1].at[pl.ds(HALF_SHARDED, REP_PART)]
        slot = rep.at[4 * x + 2 * y + c]
        cp = pltpu.make_async_copy(mine, slot, local_sem.at[n])
        cp.start()
        local.append(cp)
        peers = [sibling]
        for chip in chips:
            peers += [(chip[0], chip[1], c), (chip[0], chip[1], 1 - c)]
        for j, peer in enumerate(peers):
            cp = _remote(mine, slot, rep_send.at[j], rep_recv.at[j], peer)
            cp.start()
            sends.append(cp)
        for a in range(n):
            landed = outs[a].at[1 - c]
            _remote(landed, landed, send_sems.at[a], recv_sems.at[a], sibling).wait_recv()
        for j, peer in enumerate(peers):
            landed = rep.at[4 * peer[0] + 2 * peer[1] + peer[2]]
            _remote(landed, landed, rep_send.at[j], rep_recv.at[j], peer).wait_recv()
        for cp in sends:
            cp.wait_send()
        for cp in local:
            cp.wait()

    outs = pl.pallas_call(
        body, name="grad_share_totals",
        out_shape=[jax.ShapeDtypeStruct((2,) + a.shape, a.dtype) for a in arrays]
        + [jax.ShapeDtypeStruct((8, REP_PART, LANES), F32)],
        in_specs=[HBM_SPEC] * n, out_specs=[HBM_SPEC] * (n + 1),
        scratch_shapes=[pltpu.SemaphoreType.DMA((n,)), pltpu.SemaphoreType.DMA((n,)),
                        pltpu.SemaphoreType.DMA((7,)), pltpu.SemaphoreType.DMA((7,)),
                        pltpu.SemaphoreType.DMA((n + 1,))],
    )(*arrays)
    return outs[:n], outs[n]


TILE_BYTES = 1 << 20


def _row_tile(rows, cols):
    best = None
    for t in range(SUBLANES, rows + 1, SUBLANES):
        if rows % t == 0 and t * cols * 4 <= TILE_BYTES:
            best = t
    return best if best is not None else rows


def add_sibling(mine, received, core, out_dtype, name):
    _, _, r, cols = mine.shape
    tr = _row_tile(r, cols)

    def body(c_ref, a_ref, b_ref, o_ref):
        o_ref[...] = (a_ref[0] + b_ref[...]).astype(out_dtype)

    return pl.pallas_call(
        body, name=name,
        out_shape=jax.ShapeDtypeStruct((4, r, cols), out_dtype),
        grid_spec=pltpu.PrefetchScalarGridSpec(
            num_scalar_prefetch=1, grid=(4, r // tr),
            in_specs=[pl.BlockSpec((1, 1, tr, cols), lambda o, i, c_ref: (o, c_ref[0], i, 0)),
                      pl.BlockSpec((1, tr, cols), lambda o, i, c_ref: (o, i, 0))],
            out_specs=pl.BlockSpec((1, tr, cols), lambda o, i, c_ref: (o, i, 0))),
        compiler_params=_params("parallel", "parallel"),
    )(core, mine, received)


def add_chips(own, received, chip, name):
    _, r, cols = own.shape
    tr = _row_tile(r, cols)

    def body(p_ref, a_ref, b0, b1, b2, o_ref):
        o_ref[...] = ((a_ref[0].astype(F32) + b0[0].astype(F32)) + b1[0].astype(F32)) + b2[0].astype(F32)

    rb = lambda k: pl.BlockSpec((1, tr, cols), lambda i, p_ref: (k, i, 0))
    return pl.pallas_call(
        body, name=name,
        out_shape=jax.ShapeDtypeStruct((r, cols), F32),
        grid_spec=pltpu.PrefetchScalarGridSpec(
            num_scalar_prefetch=1, grid=(r // tr,),
            in_specs=[pl.BlockSpec((1, tr, cols), lambda i, p_ref: (p_ref[0], i, 0)), rb(0), rb(1), rb(2)],
            out_specs=pl.BlockSpec((tr, cols), lambda i, p_ref: (i, 0))),
        compiler_params=_params("parallel"),
    )(chip, own, received, received, received)


def adamw(w, g, m, v, name):
    r, cols = w.shape
    tr = _row_tile(r, cols)

    def body(w_ref, g_ref, m_ref, v_ref, d_ref, nm_ref, nv_ref):
        gv = g_ref[...]
        nm = ADAM_B1 * m_ref[...] + (1.0 - ADAM_B1) * gv
        nv = ADAM_B2 * v_ref[...] + (1.0 - ADAM_B2) * (gv * gv)
        nm_ref[...] = nm
        nv_ref[...] = nv
        m_hat = nm / (1.0 - ADAM_B1 ** ADAM_STEP)
        v_hat = nv / (1.0 - ADAM_B2 ** ADAM_STEP)
        d_ref[...] = -ADAM_LR * (m_hat / (jnp.sqrt(v_hat) + ADAM_EPS) + ADAM_WD * w_ref[...])

    blk = pl.BlockSpec((tr, cols), lambda i: (i, 0))
    return pl.pallas_call(
        body, name=name,
        out_shape=(jax.ShapeDtypeStruct((r, cols), F32),) * 3,
        grid=(r // tr,),
        in_specs=[blk] * 4, out_specs=(blk,) * 3,
        compiler_params=_params("parallel"),
    )(w, g, m, v)


WEIGHTS = ("even_norm_pre", "even_norm_post", "even_w_in", "rg_conv_w", "rg_conv_b", "rg_gate_w", "rg_gate_b",
           "rg_lambda", "sc_conv_w", "even_w_out", "odd_norm_pre", "odd_norm_post", "odd_w_in", "gla_w_gate_lr",
           "gla_b_gate", "gla_norm_g", "odd_w_out")
BIG = ("even_w_in", "even_w_out", "odd_w_in", "odd_w_out")


def _halves(a):
    return a.reshape((2, a.shape[0] // 2) + a.shape[1:])


def kernel(x, even_norm_pre, even_norm_post, even_w_in, rg_conv_w, rg_conv_b, rg_gate_w, rg_gate_b, rg_lambda, sc_conv_w, even_w_out, odd_norm_pre, odd_norm_post, odd_w_in, gla_w_gate_lr, gla_b_gate, gla_norm_g, odd_w_out, loss_target, m_even_norm_pre, m_even_norm_post, m_even_w_in, m_rg_conv_w, m_rg_conv_b, m_rg_gate_w, m_rg_gate_b, m_rg_lambda, m_sc_conv_w, m_even_w_out, m_odd_norm_pre, m_odd_norm_post, m_odd_w_in, m_gla_w_gate_lr, m_gla_b_gate, m_gla_norm_g, m_odd_w_out, v_even_norm_pre, v_even_norm_post, v_even_w_in, v_rg_conv_w, v_rg_conv_b, v_rg_gate_w, v_rg_gate_b, v_rg_lambda, v_sc_conv_w, v_even_w_out, v_odd_norm_pre, v_odd_norm_post, v_odd_w_in, v_gla_w_gate_lr, v_gla_b_gate, v_gla_norm_g, v_odd_w_out):
    given = dict(locals())
    shard = {n: given[n][0] for n in WEIGHTS}
    m_in = {n: given["m_" + n][0] for n in WEIGHTS}
    v_in = {n: given["v_" + n][0] for n in WEIGHTS}
    mx, my, mc = lax.axis_index("x"), lax.axis_index("y"), lax.axis_index("c")
    core = jnp.reshape(mc, (1,)).astype(jnp.int32)
    chip = jnp.reshape(2 * mx + my, (1,)).astype(jnp.int32)

    small_shard = _pack(shard, SHARDED_SMALL, SHARDED_ROWS)
    big_full, (small_full,) = gather_weights([_halves(shard[n].astype(BF16)) for n in BIG], [small_shard])
    full = {n: shard[n] for n, _ in REPLICATED}
    full.update({n: _merge_owners(a) for n, a in _unpack(small_full, SHARDED_SMALL, lead=(4,)).items()})
    full["even_w_in"] = big_full[0].reshape(4, D_MODEL, EVEN_IN // 4)
    full["even_w_out"] = big_full[1].reshape(2 * D_MODEL, D_MODEL)
    full["odd_w_in"] = jnp.transpose(big_full[2].reshape(4, D_MODEL, ODD_IN // 4), (1, 0, 2)).reshape(D_MODEL, ODD_IN)
    full["odd_w_out"] = big_full[3].reshape(D_MODEL, D_MODEL)

    loss, grad_x, g = local_step(x[0], loss_target[0], _prepare_weights(full))
    loss = lax.psum(loss[0, 0], ("x", "y", "c"))

    rep_rows = _pack(g, REPLICATED, REPLICATED_ROWS).reshape(4, 2, REP_PART, LANES)
    sh_rows = _pack({n: _split_owners(g[n]) for n, _ in SHARDED_SMALL}, SHARDED_SMALL, SHARDED_ROWS, lead=(4,))
    pack = jnp.concatenate([sh_rows.reshape(4, 2, HALF_SHARDED, LANES), rep_rows], axis=2)
    slabs = [g["even_w_in"],
             g["even_w_out"].reshape(4, D_MODEL // 2, D_MODEL),
             jnp.transpose(g["odd_w_in"].reshape(D_MODEL, 4, ODD_IN // 4), (1, 0, 2)),
             g["odd_w_out"].reshape(4, D_MODEL // 4, D_MODEL),
             pack.reshape(4, 2 * PACK_HALF, LANES)]
    slabs = [a.reshape((4, 2, a.shape[1] // 2) + a.shape[2:]) for a in slabs]
    from_sibling = exchange_with_sibling(slabs)
    chip_sums = [add_sibling(a, b, core, BF16 if i < 4 else F32, "grad_add_sibling_%d" % i)
                 for i, (a, b) in enumerate(zip(slabs, from_sibling))]
    from_chips = exchange_with_chips(chip_sums)
    totals = [add_chips(a, b, chip, "grad_add_chips_%d" % i) for i, (a, b) in enumerate(zip(chip_sums, from_chips))]
    both, rep_all = share_totals(totals[:4], totals[4])
    grads = {n: both[i].reshape(shard[n].shape) for i, n in enumerate(BIG)}
    sh_total = jnp.concatenate([both[4][0, :HALF_SHARDED], both[4][1, :HALF_SHARDED]], axis=0)
    rep_total = rep_all.reshape(REPLICATED_ROWS, LANES)
    grads.update(_unpack(sh_total, SHARDED_SMALL))
    grads.update(_unpack(rep_total, REPLICATED))

    delta, new_m, new_v = {}, {}, {}
    for n in BIG:
        delta[n], new_m[n], new_v[n] = adamw(shard[n], grads[n], m_in[n], v_in[n], "adamw_" + n)
    packed = [jnp.concatenate([_pack(src, SHARDED_SMALL, SHARDED_ROWS), _pack(src, REPLICATED, REPLICATED_ROWS)])
              for src in (shard, m_in, v_in)]
    small_g = jnp.concatenate([sh_total, rep_total], axis=0)
    outs = adamw(packed[0], small_g, packed[1], packed[2], "adamw_small")
    for dst, rows in zip((delta, new_m, new_v), outs):
        dst.update(_unpack(rows[:SHARDED_ROWS], SHARDED_SMALL))
        dst.update(_unpack(rows[SHARDED_ROWS:], REPLICATED))
    result = [loss, grad_x[None]]
    for group in (grads, delta, new_m, new_v):
        result += [group[n].reshape(given[n].shape) for n in WEIGHTS]
    return tuple(result)
```

```python
import functools

import jax
import jax.numpy as jnp
from jax import lax
from jax.experimental import pallas as pl
from jax.experimental.pallas import tpu as pltpu

F32 = jnp.float32
BF16 = jnp.bfloat16
MESH = pl.DeviceIdType.MESH

D_MODEL = 1024
NORM_EPS = 1e-6
RG_HEADS = 8
RG_HEAD_DIM = 128
RG_C = 8.0
EVEN_IN = 6144
ODD_IN = 3104
ODD_IN_PAD = 3200
GLA_HEADS = 4
GLA_DK = 128
GLA_DV = 256
GLA_RANK = 16
GLA_NORMALIZER = 16.0
GLA_CHUNK = 64
LR_COL = 3072

ADAM_LR = 0.001
ADAM_B1 = 0.9
ADAM_B2 = 0.999
ADAM_EPS = 1e-08
ADAM_WD = 0.01
ADAM_STEP = 10

SUBLANES = 8
LANES = 128
VMEM_LIMIT = 56 * 2 ** 20

ROW_TILE = 512
SCAN_TILE = 256
GLA_BLOCK = 1024
MIX_TILE = 128


def _params(*sem):
    return pltpu.CompilerParams(dimension_semantics=sem, vmem_limit_bytes=VMEM_LIMIT)


def _full(shape):
    n = len(shape)
    return pl.BlockSpec(shape, lambda *_: (0,) * n)


def _sigmoid(x):
    return 1.0 / (1.0 + jnp.exp(-x))


def _softplus(x):
    return jnp.maximum(x, 0.0) + jnp.log(1.0 + jnp.exp(-jnp.abs(x)))


def _one_minus_exp(x):
    series = -x * (1.0 + x * (1.0 / 2.0) * (1.0 + x * (1.0 / 3.0) * (1.0 + x * (1.0 / 4.0) * (
        1.0 + x * (1.0 / 5.0) * (1.0 + x * (1.0 / 6.0))))))
    return jnp.where(x > -0.25, series, 1.0 - jnp.exp(x))


def _dot(a, b):
    return jnp.dot(a, b, preferred_element_type=F32)


def _dot_nt(a, b):
    return lax.dot_general(a, b, (((1,), (1,)), ((), ())), preferred_element_type=F32)


def _dot_tn(a, b):
    return lax.dot_general(a, b, (((0,), (0,)), ((), ())), preferred_element_type=F32)


def _bdot(a, b, ca, cb):
    return lax.dot_general(a, b, (((ca,), (cb,)), ((0,), (0,))), preferred_element_type=F32)


def _halo_specs(rows, cols, col_block, n_row_tiles, tix):
    per = rows // SUBLANES
    last = n_row_tiles * per - 1

    def split(args):
        if len(args) == 2:
            return tix(args[1]), col_block + args[0]
        return tix(args[0]), col_block

    def prev(*args):
        t, c = split(args)
        return (jnp.maximum(t * per - 1, 0), c)

    def main(*args):
        return split(args)

    def nxt(*args):
        t, c = split(args)
        return (jnp.minimum((t + 1) * per, last), c)

    return [pl.BlockSpec((SUBLANES, cols), prev), pl.BlockSpec((rows, cols), main),
            pl.BlockSpec((SUBLANES, cols), nxt)]


def _extend(prev_ref, main_ref, next_ref, is_first, is_last):
    p = jnp.where(is_first, 0.0, prev_ref[...])
    n = jnp.where(is_last, 0.0, next_ref[...])
    return jnp.concatenate([p, main_ref[...], n], axis=0)


def _shifted(ext, offset, rows):
    if offset == 0:
        return ext[SUBLANES:SUBLANES + rows]
    n = ext.shape[0]
    return pltpu.roll(ext, (-offset) % n, 0)[SUBLANES:SUBLANES + rows]


def _conv(ext, w, left, rows):
    out = None
    for k in range(w.shape[0]):
        term = _shifted(ext, k - left, rows) * w[k:k + 1]
        out = term if out is None else out + term
    return out


def _conv_transpose(ext, w, left, rows):
    out = None
    for k in range(w.shape[0]):
        term = _shifted(ext, left - k, rows) * w[k:k + 1]
        out = term if out is None else out + term
    return out


def _colsum(x):
    return jnp.sum(x, axis=0, keepdims=True)


def _accumulate(ref, value, step):
    @pl.when(step == 0)
    def _():
        ref[...] = value

    @pl.when(step > 0)
    def _():
        ref[...] += value


def norm_matmul(x, gain, w, name):
    rows, d = x.shape
    n_col_tiles, _, tn = w.shape
    tm = min(ROW_TILE, rows)

    def body(x_ref, g_ref, w_ref, proj_ref, h_ref, h_scr):
        @pl.when(pl.program_id(1) == 0)
        def _():
            xv = x_ref[...]
            rstd = lax.rsqrt(jnp.mean(xv * xv, axis=-1, keepdims=True) + NORM_EPS)
            hv = (xv * rstd * g_ref[...]).astype(BF16)
            h_scr[...] = hv
            h_ref[...] = hv

        proj_ref[...] = _dot(h_scr[...], w_ref[0])

    return pl.pallas_call(
        body, name=name,
        out_shape=(jax.ShapeDtypeStruct((rows, n_col_tiles * tn), F32), jax.ShapeDtypeStruct((rows, d), BF16)),
        grid=(rows // tm, n_col_tiles),
        in_specs=[pl.BlockSpec((tm, d), lambda i, j: (i, 0)), _full((1, d)),
                  pl.BlockSpec((1, d, tn), lambda i, j: (j, 0, 0))],
        out_specs=(pl.BlockSpec((tm, tn), lambda i, j: (i, j)), pl.BlockSpec((tm, d), lambda i, j: (i, 0))),
        scratch_shapes=[pltpu.VMEM((tm, d), BF16)],
        compiler_params=_params("parallel", "arbitrary"),
    )(x, gain, w)


def inproj_bwd(dproj, w, x, gain, dres, name):
    rows, d = x.shape
    n_col_tiles, _, tn = w.shape
    tm = min(ROW_TILE, rows)

    def body(dp_ref, w_ref, x_ref, g_ref, dres_ref, dx_ref, dg_ref, acc):
        i, j = pl.program_id(0), pl.program_id(1)
        part = _dot_nt(dp_ref[...], w_ref[0])

        @pl.when(j == 0)
        def _():
            acc[...] = part

        @pl.when(j > 0)
        def _():
            acc[...] += part

        @pl.when(j == n_col_tiles - 1)
        def _():
            _inproj_finish(acc[...], x_ref, g_ref, dres_ref, dx_ref, dg_ref, i)

    return pl.pallas_call(
        body, name=name,
        out_shape=(jax.ShapeDtypeStruct((rows, d), F32), jax.ShapeDtypeStruct((1, d), F32)),
        grid=(rows // tm, n_col_tiles),
        in_specs=[pl.BlockSpec((tm, tn), lambda i, j: (i, j)), pl.BlockSpec((1, d, tn), lambda i, j: (j, 0, 0)),
                  pl.BlockSpec((tm, d), lambda i, j: (i, 0)), _full((1, d)),
                  pl.BlockSpec((tm, d), lambda i, j: (i, 0))],
        out_specs=(pl.BlockSpec((tm, d), lambda i, j: (i, 0)), _full((1, d))),
        scratch_shapes=[pltpu.VMEM((tm, d), F32)],
        compiler_params=_params("arbitrary", "arbitrary"),
    )(dproj, w, x, gain, dres)


def _inproj_finish(dh, x_ref, g_ref, dres_ref, dx_ref, dg_ref, step):
    xv = x_ref[...]
    rstd = lax.rsqrt(jnp.mean(xv * xv, axis=-1, keepdims=True) + NORM_EPS)
    xhat = xv * rstd
    dxn = dh * g_ref[...]
    dx_ref[...] = dres_ref[...] + rstd * (dxn - xhat * jnp.mean(dxn * xhat, axis=-1, keepdims=True))
    _accumulate(dg_ref, _colsum(dh * xhat), step)


def inproj_bwd_pieces(pieces, w, x, gain, dres, name):
    rows, d = x.shape
    tm = min(ROW_TILE, rows)
    n = len(pieces)
    widths = [p.shape[1] for p in pieces]
    starts = [sum(widths[:k]) for k in range(n)]
    assert sum(widths) == w.shape[2]

    def body(*refs):
        w_ref, x_ref, g_ref, dres_ref, dx_ref, dg_ref = refs[n:]
        dh = None
        for k in range(n):
            part = _dot_nt(refs[k][...], w_ref[0, :, starts[k]:starts[k] + widths[k]])
            dh = part if dh is None else dh + part
        _inproj_finish(dh, x_ref, g_ref, dres_ref, dx_ref, dg_ref, pl.program_id(0))

    row = lambda cols: pl.BlockSpec((tm, cols), lambda i: (i, 0))
    return pl.pallas_call(
        body, name=name,
        out_shape=(jax.ShapeDtypeStruct((rows, d), F32), jax.ShapeDtypeStruct((1, d), F32)),
        grid=(rows // tm,),
        in_specs=[row(wd) for wd in widths] + [_full(w.shape), row(d), _full((1, d)), row(d)],
        out_specs=(row(d), _full((1, d))),
        compiler_params=_params("arbitrary"),
    )(*pieces, w, x, gain, dres)


def matmul_dw_pieces(a, pieces, name):
    rows, m = a.shape
    tk = min(ROW_TILE, rows)
    n = len(pieces)

    def body(*refs):
        a_ref, ins, outs = refs[0], refs[1:1 + n], refs[1 + n:]
        av = a_ref[...]
        for k in range(n):
            _accumulate(outs[k], _dot_tn(av, ins[k][...]), pl.program_id(0))

    return pl.pallas_call(
        body, name=name,
        out_shape=[jax.ShapeDtypeStruct((m, p.shape[1]), F32) for p in pieces],
        grid=(rows // tk,),
        in_specs=[pl.BlockSpec((tk, m), lambda k: (k, 0))]
        + [pl.BlockSpec((tk, p.shape[1]), lambda k: (k, 0)) for p in pieces],
        out_specs=[_full((m, p.shape[1])) for p in pieces],
        compiler_params=_params("arbitrary"),
    )(a, *pieces)


def matmul_dw(a, b, bn, name):
    rows, m = a.shape
    n = b.shape[1]
    tk = min(ROW_TILE, rows)
    steps = rows // tk

    def body(a_ref, b_ref, o_ref):
        part = _dot_tn(a_ref[...], b_ref[...])

        @pl.when(pl.program_id(1) == 0)
        def _():
            o_ref[0] = part

        @pl.when(pl.program_id(1) > 0)
        def _():
            o_ref[0] += part

    return pl.pallas_call(
        body, name=name,
        out_shape=jax.ShapeDtypeStruct((n // bn, m, bn), F32),
        grid=(n // bn, steps),
        in_specs=[pl.BlockSpec((tk, m), lambda j, k: (k, 0)), pl.BlockSpec((tk, bn), lambda j, k: (k, j))],
        out_specs=pl.BlockSpec((1, m, bn), lambda j, k: (j, 0, 0)),
        compiler_params=_params("parallel", "arbitrary"),
    )(a, b)


def _scan(a, b, reverse):
    n = a.shape[0]
    row = lax.broadcasted_iota(jnp.int32, a.shape, 0)
    s = 1
    while s < n:
        if reverse:
            a_s, b_s, valid = pltpu.roll(a, n - s, 0), pltpu.roll(b, n - s, 0), row < n - s
        else:
            a_s, b_s, valid = pltpu.roll(a, s, 0), pltpu.roll(b, s, 0), row >= s
        b = jnp.where(valid, a * b_s + b, b)
        a = jnp.where(valid, a * a_s, a)
        s *= 2
    return a, b


def _rg_gates(ua, gw_ref, gb, lam):
    ub = ua.astype(BF16)
    pre_r, pre_i = [], []
    for h in range(RG_HEADS):
        z = _dot(ub[:, h * RG_HEAD_DIM:(h + 1) * RG_HEAD_DIM], gw_ref[h])
        pre_r.append(z[:, :RG_HEAD_DIM])
        pre_i.append(z[:, RG_HEAD_DIM:])
    r = _sigmoid(jnp.concatenate(pre_r, axis=1) + gb[0:1])
    i = _sigmoid(jnp.concatenate(pre_i, axis=1) + gb[1:2])
    sp = _softplus(-lam)
    log_a = -RG_C * r * sp
    a = jnp.exp(log_a)
    mult = jnp.sqrt(_one_minus_exp(2.0 * log_a))
    return r, i, sp, a, mult


def _rg_weight_specs():
    return [_full((4, D_MODEL)), _full((1, D_MODEL)), _full((RG_HEADS, RG_HEAD_DIM, 2 * RG_HEAD_DIM)),
            _full((2, D_MODEL)), _full((1, D_MODEL))]


def rglru_fwd(proj, conv_w, conv_b, gate_w, gate_b, lam, reverse, name):
    rows_total = proj.shape[0]
    rows = min(SCAN_TILE, rows_total)
    n_tiles = rows_total // rows
    tix = (lambda i: n_tiles - 1 - i) if reverse else (lambda i: i)

    def body(xp, xm, xn, cw_ref, cb_ref, gw_ref, gb_ref, lam_ref, h_ref, carry):
        i = pl.program_id(0)
        t = tix(i)
        ext = _extend(xp, xm, xn, t == 0, t == n_tiles - 1)
        ua = _conv(ext, cw_ref[...], 2, rows) + cb_ref[...]
        _, gi, _, a, mult = _rg_gates(ua, gw_ref, gb_ref[...], lam_ref[...])
        b = mult * (gi * ua)
        a_cum, h0 = _scan(a, b, reverse)

        @pl.when(i == 0)
        def _():
            carry[...] = jnp.zeros_like(carry)

        h = a_cum * carry[0:1] + h0
        h_ref[...] = h
        edge = h[0:1] if reverse else h[rows - 1:rows]
        carry[...] = jnp.broadcast_to(edge, carry.shape)

    return pl.pallas_call(
        body, name=name,
        out_shape=jax.ShapeDtypeStruct((rows_total, D_MODEL), F32),
        grid=(n_tiles,),
        in_specs=_halo_specs(rows, D_MODEL, 0, n_tiles, tix) + _rg_weight_specs(),
        out_specs=pl.BlockSpec((rows, D_MODEL), lambda i: (tix(i), 0)),
        scratch_shapes=[pltpu.VMEM((SUBLANES, D_MODEL), F32)],
        compiler_params=_params("arbitrary"),
    )(proj, proj, proj, conv_w, conv_b, gate_w, gate_b, lam)


def rglru_bwd(proj, dycat, h_dir, conv_w, conv_b, gate_w, gate_b, lam, reverse, name):
    rows_total = proj.shape[0]
    rows = min(SCAN_TILE, rows_total)
    n_tiles = rows_total // rows
    tix = (lambda i: i) if reverse else (lambda i: n_tiles - 1 - i)
    za_block = 1

    def body(xp, xm, xn, za_ref, dya_ref, hp, hm, hn, cw_ref, cb_ref, gw_ref, gb_ref, lam_ref,
             dua_ref, dgw_ref, dgb_ref, dlam_ref, carry):
        step = pl.program_id(0)
        t = tix(step)
        first, last = t == 0, t == n_tiles - 1
        ext = _extend(xp, xm, xn, first, last)
        ua = _conv(ext, cw_ref[...], 2, rows) + cb_ref[...]
        lam_v = lam_ref[...]
        r, gi, sp, a, mult = _rg_gates(ua, gw_ref, gb_ref[...], lam_v)
        za = za_ref[...]
        dh = dya_ref[...] * (za * _sigmoid(za))

        @pl.when(step == 0)
        def _():
            carry[...] = jnp.zeros_like(carry)

        a_cum, mu0 = _scan(a, a * dh, not reverse)
        old = carry[0:1]
        mu = a_cum * old + mu0
        row = lax.broadcasted_iota(jnp.int32, mu.shape, 0)
        if reverse:
            mu_next = jnp.where(row == 0, old, pltpu.roll(mu, 1, 0))
            carry[...] = jnp.broadcast_to(mu[rows - 1:rows], carry.shape)
            h_ext = _extend(hp, hm, hn, first, last)
            h_prev = _shifted(h_ext, 1, rows)
        else:
            mu_next = jnp.where(row == rows - 1, old, pltpu.roll(mu, rows - 1, 0))
            carry[...] = jnp.broadcast_to(mu[0:1], carry.shape)
            h_ext = _extend(hp, hm, hn, first, last)
            h_prev = _shifted(h_ext, -1, rows)
        db = dh + mu_next
        da = db * h_prev
        d_mult = db * (gi * ua)
        di = db * (mult * ua)
        dua = db * (mult * gi)
        dlog_a = da * a - d_mult * (a * a) / mult
        dr = dlog_a * (-RG_C * sp)
        dlam = _colsum(dlog_a * (-RG_C * r)) * (-_sigmoid(-lam_v))
        dpr = dr * (r * (1.0 - r))
        dpi = di * (gi * (1.0 - gi))
        dgb = jnp.concatenate([_colsum(dpr), _colsum(dpi)], axis=0)
        ub = ua.astype(BF16)
        dua_heads, dgw_heads = [], []
        for h in range(RG_HEADS):
            cols = slice(h * RG_HEAD_DIM, (h + 1) * RG_HEAD_DIM)
            dz = jnp.concatenate([dpr[:, cols], dpi[:, cols]], axis=1).astype(BF16)
            dgw_heads.append(_dot_tn(ub[:, cols], dz))
            dua_heads.append(_dot_nt(dz, gw_ref[h]))
        dua_ref[...] = dua + jnp.concatenate(dua_heads, axis=1)

        @pl.when(step == 0)
        def _():
            for h in range(RG_HEADS):
                dgw_ref[h] = dgw_heads[h]
            dgb_ref[...] = dgb
            dlam_ref[...] = dlam

        @pl.when(step > 0)
        def _():
            for h in range(RG_HEADS):
                dgw_ref[h] += dgw_heads[h]
            dgb_ref[...] += dgb
            dlam_ref[...] += dlam

    row_spec = lambda col: pl.BlockSpec((rows, D_MODEL), lambda i: (tix(i), col))
    return pl.pallas_call(
        body, name=name,
        out_shape=(jax.ShapeDtypeStruct((rows_total, D_MODEL), F32),
                   jax.ShapeDtypeStruct((RG_HEADS, RG_HEAD_DIM, 2 * RG_HEAD_DIM), F32),
                   jax.ShapeDtypeStruct((2, D_MODEL), F32), jax.ShapeDtypeStruct((1, D_MODEL), F32)),
        grid=(n_tiles,),
        in_specs=(_halo_specs(rows, D_MODEL, 0, n_tiles, tix) + [row_spec(za_block), row_spec(0)]
                  + _halo_specs(rows, D_MODEL, 0, n_tiles, tix) + _rg_weight_specs()),
        out_specs=(row_spec(0), _full((RG_HEADS, RG_HEAD_DIM, 2 * RG_HEAD_DIM)), _full((2, D_MODEL)),
                   _full((1, D_MODEL))),
        scratch_shapes=[pltpu.VMEM((SUBLANES, D_MODEL), F32)],
        compiler_params=_params("arbitrary"),
    )(proj, proj, proj, proj, dycat, h_dir, h_dir, h_dir, conv_w, conv_b, gate_w, gate_b, lam)


def even_mix_fwd(proj, h_f, h_b, sc_w, name):
    rows_total = proj.shape[0]
    rows = min(MIX_TILE, rows_total)
    n_tiles = rows_total // rows
    cb = D_MODEL
    n_cb = 1
    ident = lambda i: i

    def body(za_ref, hf_ref, hb_ref, xbp, xbm, xbn, gcp, gcm, gcn, gb_ref, zb_ref, w_ref, y_ref):
        t = pl.program_id(1)
        first, last = t == 0, t == n_tiles - 1
        za = za_ref[...]
        y_ref[:, 0:cb] = ((hf_ref[...] + hb_ref[...]) * (za * _sigmoid(za))).astype(BF16)
        p_ext = _extend(xbp, xbm, xbn, first, last) * _extend(gcp, gcm, gcn, first, last)
        cv = _conv(p_ext, w_ref[...], 1, rows)
        zb = zb_ref[...]
        y_ref[:, cb:2 * cb] = (gb_ref[...] * cv * (zb * _sigmoid(zb))).astype(BF16)

    blk = lambda col: pl.BlockSpec((rows, cb), lambda c, i: (i, col * n_cb + c))
    own = pl.BlockSpec((rows, cb), lambda c, i: (i, c))
    return pl.pallas_call(
        body, name=name,
        out_shape=jax.ShapeDtypeStruct((rows_total, 2 * D_MODEL), BF16),
        grid=(n_cb, n_tiles),
        in_specs=([blk(1), own, own] + _halo_specs(rows, cb, 2 * n_cb, n_tiles, ident)
                  + _halo_specs(rows, cb, 4 * n_cb, n_tiles, ident)
                  + [blk(3), blk(5), pl.BlockSpec((3, cb), lambda c, i: (0, c))]),
        out_specs=pl.BlockSpec((rows, 2 * cb), lambda c, i: (i, 0)),
        compiler_params=_params("parallel", "arbitrary"),
    )(proj, h_f, h_b, proj, proj, proj, proj, proj, proj, proj, proj, sc_w)


def even_mix_bwd(proj, dycat, h_f, h_b, dua_f, dua_b, conv_w, sc_w, name):
    rows_total = proj.shape[0]
    rows = min(MIX_TILE, rows_total)
    n_tiles = rows_total // rows
    cb = D_MODEL
    n_cb = 1
    ident = lambda i: i

    def body(xap, xam, xan, za_ref, xbp, xbm, xbn, gbp, gbm, gbn, gcp, gcm, gcn, zbp, zbm, zbn,
             dya_ref, dybp, dybm, dybn, hf_ref, hb_ref, dfp, dfm, dfn, dbp, dbm, dbn, cw_ref, sw_ref,
             dp_ref, dcw_ref, dcb_ref, dsw_ref):
        def put(k, value):
            dp_ref[:, k * cb:(k + 1) * cb] = value.astype(BF16)

        t = pl.program_id(1)
        first, last = t == 0, t == n_tiles - 1
        za = za_ref[...]
        sa = _sigmoid(za)
        put(1, dya_ref[...] * (hf_ref[...] + hb_ref[...]) * (sa * (1.0 + za * (1.0 - sa))))
        dua_ext = _extend(dfp, dfm, dfn, first, last) + _extend(dbp, dbm, dbn, first, last)
        cw = cw_ref[...]
        put(0, _conv_transpose(dua_ext, cw, 2, rows))
        dua = dua_ext[SUBLANES:SUBLANES + rows]
        xa_ext = _extend(xap, xam, xan, first, last)
        dcw = jnp.concatenate([_colsum(dua * _shifted(xa_ext, k - 2, rows)) for k in range(4)], axis=0)
        dcb = _colsum(dua)
        xb_ext = _extend(xbp, xbm, xbn, first, last)
        gc_ext = _extend(gcp, gcm, gcn, first, last)
        p_ext = xb_ext * gc_ext
        zb_ext = _extend(zbp, zbm, zbn, first, last)
        sb_ext = _sigmoid(zb_ext)
        dyb_ext = _extend(dybp, dybm, dybn, first, last)
        gb_ext = _extend(gbp, gbm, gbn, first, last)
        dcv_ext = dyb_ext * gb_ext * (zb_ext * sb_ext)
        sw = sw_ref[...]
        cv = _conv(p_ext, sw, 1, rows)
        mid = slice(SUBLANES, SUBLANES + rows)
        zb, sb, dyb, gb = zb_ext[mid], sb_ext[mid], dyb_ext[mid], gb_ext[mid]
        put(3, dyb * cv * (zb * sb))
        put(5, dyb * gb * cv * (sb * (1.0 + zb * (1.0 - sb))))
        dp = _conv_transpose(dcv_ext, sw, 1, rows)
        put(4, dp * xb_ext[mid])
        put(2, dp * gc_ext[mid])
        dcv = dcv_ext[mid]
        dsw = jnp.concatenate([_colsum(dcv * _shifted(p_ext, k - 1, rows)) for k in range(3)], axis=0)

        @pl.when(t == 0)
        def _():
            dcw_ref[...] = dcw
            dcb_ref[...] = dcb
            dsw_ref[...] = dsw

        @pl.when(t > 0)
        def _():
            dcw_ref[...] += dcw
            dcb_ref[...] += dcb
            dsw_ref[...] += dsw

    blk = lambda col: pl.BlockSpec((rows, cb), lambda c, i: (i, col * n_cb + c))
    halo = lambda col: _halo_specs(rows, cb, col * n_cb, n_tiles, ident)
    own = pl.BlockSpec((rows, cb), lambda c, i: (i, c))
    wspec = lambda k: pl.BlockSpec((k, cb), lambda c, i: (0, c))
    return pl.pallas_call(
        body, name=name,
        out_shape=(jax.ShapeDtypeStruct((rows_total, 6 * D_MODEL), BF16),
                   jax.ShapeDtypeStruct((4, D_MODEL), F32), jax.ShapeDtypeStruct((1, D_MODEL), F32),
                   jax.ShapeDtypeStruct((3, D_MODEL), F32)),
        grid=(n_cb, n_tiles),
        in_specs=(halo(0) + [blk(1)] + halo(2) + halo(3) + halo(4) + halo(5) + [blk(0)] + halo(1)
                  + [own, own] + halo(0) + halo(0) + [wspec(4), wspec(3)]),
        out_specs=(pl.BlockSpec((rows, 6 * cb), lambda c, i: (i, 0)), wspec(4), wspec(1), wspec(3)),
        compiler_params=_params("parallel", "arbitrary"),
    )(proj, proj, proj, proj, proj, proj, proj, proj, proj, proj, proj, proj, proj, proj, proj, proj,
      dycat, dycat, dycat, dycat, h_f, h_b, dua_f, dua_f, dua_f, dua_b, dua_b, dua_b, conv_w, sc_w)


def even_out_fwd(ycat, w_out, gain, x, name):
    rows, d = x.shape
    k = ycat.shape[1]
    tm = min(ROW_TILE, rows)

    def body(yc_ref, w_ref, g_ref, x_ref, x1_ref, y_ref):
        y = _dot(yc_ref[...], w_ref[...])
        y_ref[...] = y
        rstd = lax.rsqrt(jnp.mean(y * y, axis=-1, keepdims=True) + NORM_EPS)
        x1_ref[...] = x_ref[...] + y * rstd * g_ref[...]

    row = lambda n: pl.BlockSpec((tm, n), lambda i: (i, 0))
    return pl.pallas_call(
        body, name=name,
        out_shape=(jax.ShapeDtypeStruct((rows, d), F32),) * 2,
        grid=(rows // tm,),
        in_specs=[row(k), _full((k, d)), _full((1, d)), row(d)],
        out_specs=(row(d), row(d)),
        compiler_params=_params("parallel"),
    )(ycat, w_out, gain, x)


def _rmsnorm_bwd(dout, y, gain):
    rstd = lax.rsqrt(jnp.mean(y * y, axis=-1, keepdims=True) + NORM_EPS)
    yhat = y * rstd
    dyn = dout * gain
    dy = rstd * (dyn - yhat * jnp.mean(dyn * yhat, axis=-1, keepdims=True))
    return dy, dout * yhat


def even_out_bwd(dx1, y, gain, w_out, name):
    rows, d = y.shape
    k = w_out.shape[0]
    tm = min(ROW_TILE, rows)

    def body(dx_ref, y_ref, g_ref, w_ref, dy_ref, dyc_ref, dg_ref):
        dy, dg_rows = _rmsnorm_bwd(dx_ref[...], y_ref[...], g_ref[...])
        dyb = dy.astype(BF16)
        dy_ref[...] = dyb
        dyc_ref[...] = _dot_nt(dyb, w_ref[...])
        _accumulate(dg_ref, _colsum(dg_rows), pl.program_id(0))

    row = lambda n: pl.BlockSpec((tm, n), lambda i: (i, 0))
    return pl.pallas_call(
        body, name=name,
        out_shape=(jax.ShapeDtypeStruct((rows, d), BF16), jax.ShapeDtypeStruct((rows, k), F32),
                   jax.ShapeDtypeStruct((1, d), F32)),
        grid=(rows // tm,),
        in_specs=[row(d), row(d), _full((1, d)), _full((k, d))],
        out_specs=(row(d), row(k), _full((1, d))),
        compiler_params=_params("arbitrary"),
    )(dx1, y, gain, w_out)


def _chunk_cumsum(g, reverse):
    n = g.shape[0]
    pos = lax.broadcasted_iota(jnp.int32, g.shape, 0) % GLA_CHUNK
    s = 1
    while s < GLA_CHUNK:
        if reverse:
            g = g + jnp.where(pos < GLA_CHUNK - s, pltpu.roll(g, n - s, 0), 0.0)
        else:
            g = g + jnp.where(pos >= s, pltpu.roll(g, s, 0), 0.0)
        s *= 2
    return g


def _gla_prepare(q_ref, k_ref, lr_ref, wg_ref, bg_ref, reverse, n_chunks):
    z = _dot(lr_ref[...].astype(BF16), wg_ref[0]) + bg_ref[0]
    g = -_softplus(-z) * (1.0 / GLA_NORMALIZER)
    bcum = _chunk_cumsum(g, reverse).reshape(n_chunks, GLA_CHUNK, GLA_DK)
    edge = 0 if reverse else GLA_CHUNK - 1
    btot = bcum[:, edge:edge + 1, :]
    e_pos = jnp.exp(bcum)
    e_neg = jnp.exp(-bcum)
    e_st = jnp.exp(btot - bcum)
    q3 = q_ref[...].reshape(n_chunks, GLA_CHUNK, GLA_DK)
    k3 = k_ref[...].reshape(n_chunks, GLA_CHUNK, GLA_DK)
    scale = GLA_DK ** -0.5
    q_in = q3 * scale * e_pos
    k_in = k3 * e_neg
    k_st = k3 * e_st
    dec = jnp.exp(btot)
    return z, q_in, k_in, k_st, dec, (scale * e_pos, e_neg, e_st)


def _gla_mask(reverse):
    i = lax.broadcasted_iota(jnp.int32, (GLA_CHUNK, GLA_CHUNK), 0)
    j = lax.broadcasted_iota(jnp.int32, (GLA_CHUNK, GLA_CHUNK), 1)
    return (j >= i) if reverse else (j <= i)


def _gla_specs(rows, n_blocks, reverse):
    tix = (lambda s: n_blocks - 1 - s) if reverse else (lambda s: s)
    d = 1 if reverse else 0
    lr_block = LR_COL // LANES
    specs = [pl.BlockSpec((rows, GLA_DK), lambda h, s: (tix(s), h)),
             pl.BlockSpec((rows, GLA_DK), lambda h, s: (tix(s), GLA_HEADS + h)),
             pl.BlockSpec((rows, GLA_DV), lambda h, s: (tix(s), GLA_HEADS + h)),
             pl.BlockSpec((rows, LANES), lambda h, s: (tix(s), lr_block)),
             pl.BlockSpec((1, LANES, GLA_DK), lambda h, s: (d, 0, h)),
             pl.BlockSpec((1, 1, GLA_DK), lambda h, s: (d, 0, h))]
    return specs, tix


def gla_fwd(proj, wg_pad, bg, reverse, name):
    rows_total = proj.shape[0]
    rows = min(GLA_BLOCK, rows_total)
    n_blocks = rows_total // rows
    n_chunks = rows // GLA_CHUNK
    specs, tix = _gla_specs(rows, n_blocks, reverse)

    def body(q_ref, k_ref, v_ref, lr_ref, wg_ref, bg_ref, o_ref, st_ref, state, kv_scr, dec_scr):
        _, q_in, k_in, k_st, dec, _ = _gla_prepare(q_ref, k_ref, lr_ref, wg_ref, bg_ref, reverse, n_chunks)
        vb = v_ref[...].reshape(n_chunks, GLA_CHUNK, GLA_DV).astype(BF16)
        qb = q_in.astype(BF16)
        p = jnp.where(_gla_mask(reverse), _bdot(qb, k_in.astype(BF16), 2, 2), 0.0)
        o = _bdot(p.astype(BF16), vb, 2, 1)
        kv_scr[...] = _bdot(vb, k_st.astype(BF16), 1, 1)
        dec_scr[...] = jnp.broadcast_to(dec, dec_scr.shape)

        @pl.when(pl.program_id(1) == 0)
        def _():
            state[...] = jnp.zeros_like(state)

        for c in range(n_chunks):
            cc = n_chunks - 1 - c if reverse else c
            st_ref[0, cc] = state[...]
            state[...] = state[...] * dec_scr[cc, 0:1] + kv_scr[cc]
        o = o + _bdot(qb, st_ref[0].astype(BF16), 2, 2)
        o_ref[...] = o.reshape(rows, GLA_DV)

    return pl.pallas_call(
        body, name=name,
        out_shape=(jax.ShapeDtypeStruct((rows_total, GLA_HEADS * GLA_DV), F32),
                   jax.ShapeDtypeStruct((GLA_HEADS, rows_total // GLA_CHUNK, GLA_DV, GLA_DK), F32)),
        grid=(GLA_HEADS, n_blocks),
        in_specs=specs,
        out_specs=(pl.BlockSpec((rows, GLA_DV), lambda h, s: (tix(s), h)),
                   pl.BlockSpec((1, n_chunks, GLA_DV, GLA_DK), lambda h, s: (h, tix(s), 0, 0))),
        scratch_shapes=[pltpu.VMEM((GLA_DV, GLA_DK), F32), pltpu.VMEM((n_chunks, GLA_DV, GLA_DK), F32),
                        pltpu.VMEM((n_chunks, SUBLANES, GLA_DK), F32)],
        compiler_params=_params("parallel", "arbitrary"),
    )(proj, proj, proj, proj, wg_pad, bg)


def gla_bwd(proj, wg_pad, bg, d_o, states, dqkv_in, reverse, name):
    rows_total = proj.shape[0]
    rows = min(GLA_BLOCK, rows_total)
    n_blocks = rows_total // rows
    n_chunks = rows // GLA_CHUNK
    specs, tix = _gla_specs(rows, n_blocks, not reverse)
    d = 1 if reverse else 0
    specs[4] = pl.BlockSpec((1, LANES, GLA_DK), lambda h, s: (d, 0, h))
    specs[5] = pl.BlockSpec((1, 1, GLA_DK), lambda h, s: (d, 0, h))
    add = dqkv_in is not None

    def body(*refs):
        q_ref, k_ref, v_ref, lr_ref, wg_ref, bg_ref, do_ref, st_ref = refs[:8]
        refs = refs[8:]
        if add:
            aq_ref, ak_ref, av_ref = refs[:3]
            refs = refs[3:]
        dq_ref, dk_ref, dv_ref, dz_ref, dstate, g_scr, dec_scr, dsn_scr = refs
        z, q_in, k_in, k_st, dec, (f_q, f_k, f_s) = _gla_prepare(q_ref, k_ref, lr_ref, wg_ref, bg_ref, reverse,
                                                                 n_chunks)
        mask = _gla_mask(reverse)
        vb = v_ref[...].reshape(n_chunks, GLA_CHUNK, GLA_DV).astype(BF16)
        dob = do_ref[...].reshape(n_chunks, GLA_CHUNK, GLA_DV).astype(BF16)
        qb, kb, ksb = q_in.astype(BF16), k_in.astype(BF16), k_st.astype(BF16)
        st = st_ref[0]
        stb = st.astype(BF16)
        pb = jnp.where(mask, _bdot(qb, kb, 2, 2), 0.0).astype(BF16)
        dpb = jnp.where(mask, _bdot(dob, vb, 2, 2), 0.0).astype(BF16)
        d_qin = _bdot(dpb, kb, 2, 1) + _bdot(dob, stb, 2, 1)
        d_kin = _bdot(dpb, qb, 1, 1)
        dv = _bdot(pb, dob, 1, 1)
        g_scr[...] = _bdot(dob, qb, 1, 1)
        dec_scr[...] = jnp.broadcast_to(dec, dec_scr.shape)

        @pl.when(pl.program_id(1) == 0)
        def _():
            dstate[...] = jnp.zeros_like(dstate)

        for c in range(n_chunks):
            cc = c if reverse else n_chunks - 1 - c
            dsn_scr[cc] = dstate[...]
            dstate[...] = dstate[...] * dec_scr[cc, 0:1] + g_scr[cc]
        dsn = dsn_scr[...]
        dsnb = dsn.astype(BF16)
        dv = dv + _bdot(ksb, dsnb, 2, 2)
        d_kst = _bdot(vb, dsnb, 2, 1)
        d_dec = jnp.sum(dsn * st, axis=1, keepdims=True)
        ks_term = d_kst * k_st
        d_btot = d_dec * dec + jnp.sum(ks_term, axis=1, keepdims=True)
        d_b = d_qin * q_in - d_kin * k_in - ks_term
        pos = lax.broadcasted_iota(jnp.int32, d_b.shape, 1)
        edge = 0 if reverse else GLA_CHUNK - 1
        d_b = d_b + jnp.where(pos == edge, d_btot, 0.0)
        dg = _chunk_cumsum(d_b.reshape(rows, GLA_DK), not reverse)
        dz_ref[...] = dg * (1.0 / GLA_NORMALIZER) * _sigmoid(-z)
        dq = (d_qin * f_q).reshape(rows, GLA_DK)
        dk = (d_kin * f_k + d_kst * f_s).reshape(rows, GLA_DK)
        dv = dv.reshape(rows, GLA_DV)
        if add:
            dq_ref[...] = (dq + aq_ref[...]).astype(BF16)
            dk_ref[...] = (dk + ak_ref[...]).astype(BF16)
            dv_ref[...] = (dv + av_ref[...]).astype(BF16)
        else:
            dq_ref[...] = dq
            dk_ref[...] = dk
            dv_ref[...] = dv

    qkv_specs = [pl.BlockSpec((rows, GLA_DK), lambda h, s: (tix(s), h)),
                 pl.BlockSpec((rows, GLA_DK), lambda h, s: (tix(s), h)),
                 pl.BlockSpec((rows, GLA_DV), lambda h, s: (tix(s), h))]
    in_specs = specs + [pl.BlockSpec((rows, GLA_DV), lambda h, s: (tix(s), h)),
                        pl.BlockSpec((1, n_chunks, GLA_DV, GLA_DK), lambda h, s: (h, tix(s), 0, 0))]
    args = [proj, proj, proj, proj, wg_pad, bg, d_o, states]
    out_dtype = F32
    if add:
        in_specs += qkv_specs
        args += list(dqkv_in)
        out_dtype = BF16
    return pl.pallas_call(
        body, name=name,
        out_shape=(jax.ShapeDtypeStruct((rows_total, GLA_HEADS * GLA_DK), out_dtype),
                   jax.ShapeDtypeStruct((rows_total, GLA_HEADS * GLA_DK), out_dtype),
                   jax.ShapeDtypeStruct((rows_total, GLA_HEADS * GLA_DV), out_dtype),
                   jax.ShapeDtypeStruct((rows_total, GLA_HEADS * GLA_DK), F32)),
        grid=(GLA_HEADS, n_blocks),
        in_specs=in_specs,
        out_specs=(pl.BlockSpec((rows, GLA_DK), lambda h, s: (tix(s), h)),
                   pl.BlockSpec((rows, GLA_DK), lambda h, s: (tix(s), h)),
                   pl.BlockSpec((rows, GLA_DV), lambda h, s: (tix(s), h)),
                   pl.BlockSpec((rows, GLA_DK), lambda h, s: (tix(s), h))),
        scratch_shapes=[pltpu.VMEM((GLA_DV, GLA_DK), F32), pltpu.VMEM((n_chunks, GLA_DV, GLA_DK), F32),
                        pltpu.VMEM((n_chunks, SUBLANES, GLA_DK), F32),
                        pltpu.VMEM((n_chunks, GLA_DV, GLA_DK), F32)],
        compiler_params=_params("parallel", "arbitrary"),
    )(*args)


def gla_gate_bwd(proj, dz_f, dz_b, wg_pad, name):
    rows_total = proj.shape[0]
    tm = min(ROW_TILE, rows_total)
    n_key = GLA_HEADS * GLA_DK

    def body(lr_ref, dzf_ref, dzb_ref, wg_ref, dlr_ref, dwg_ref, dbg_ref):
        step = pl.program_id(0)
        lr_t = jnp.transpose(lr_ref[...])
        dzf, dzb = dzf_ref[...], dzb_ref[...]
        dzf16, dzb16 = dzf.astype(BF16), dzb.astype(BF16)
        dlr_ref[...] = (_dot_nt(dzf16, wg_ref[0]) + _dot_nt(dzb16, wg_ref[1])).astype(BF16)
        dwf = _dot(lr_t[0:GLA_RANK].astype(BF16), dzf16)
        dwb = _dot(lr_t[GLA_RANK:2 * GLA_RANK].astype(BF16), dzb16)
        dbg = jnp.concatenate([_colsum(dzf), _colsum(dzb)], axis=0)

        @pl.when(step == 0)
        def _():
            dwg_ref[0] = dwf
            dwg_ref[1] = dwb
            dbg_ref[...] = dbg

        @pl.when(step > 0)
        def _():
            dwg_ref[0] += dwf
            dwg_ref[1] += dwb
            dbg_ref[...] += dbg

    return pl.pallas_call(
        body, name=name,
        out_shape=(jax.ShapeDtypeStruct((rows_total, LANES), BF16), jax.ShapeDtypeStruct((2, GLA_RANK, n_key), F32),
                   jax.ShapeDtypeStruct((2, n_key), F32)),
        grid=(rows_total // tm,),
        in_specs=[pl.BlockSpec((tm, LANES), lambda i: (i, LR_COL // LANES)),
                  pl.BlockSpec((tm, n_key), lambda i: (i, 0)), pl.BlockSpec((tm, n_key), lambda i: (i, 0)),
                  _full((2, LANES, n_key))],
        out_specs=(pl.BlockSpec((tm, LANES), lambda i: (i, 0)), _full((2, GLA_RANK, n_key)), _full((2, n_key))),
        compiler_params=_params("arbitrary"),
    )(proj, dz_f, dz_b, wg_pad)


def _head_norm(o, gain):
    outs, hats, rstds = [], [], []
    for h in range(GLA_HEADS):
        oh = o[:, h * GLA_DV:(h + 1) * GLA_DV]
        rstd = lax.rsqrt(jnp.mean(oh * oh, axis=-1, keepdims=True) + NORM_EPS)
        hat = oh * rstd
        outs.append(hat * gain)
        hats.append(hat)
        rstds.append(rstd)
    return outs, hats, rstds


def odd_out_fwd(o_f, o_b, proj, head_gain, w_out, gain, x1, target, name):
    rows, d = x1.shape
    tm = min(ROW_TILE, rows)
    r_block = (2 * GLA_HEADS * GLA_DK + GLA_HEADS * GLA_DV) // d

    def body(of_ref, ob_ref, r_ref, hg_ref, w_ref, g_ref, x1_ref, tgt_ref, y2_ref, dy_ref, dx2_ref, loss_ref,
             dg_ref):
        step = pl.program_id(0)
        on, _, _ = _head_norm(of_ref[...] + ob_ref[...], hg_ref[...])
        r = r_ref[...]
        y2 = (jnp.concatenate(on, axis=1) * (r * _sigmoid(r))).astype(BF16)
        y2_ref[...] = y2
        y = _dot(y2, w_ref[...])
        gain_v = g_ref[...]
        rstd = lax.rsqrt(jnp.mean(y * y, axis=-1, keepdims=True) + NORM_EPS)
        x2 = x1_ref[...] + y * rstd * gain_v
        diff = x2 - tgt_ref[...]
        loss = 0.5 * jnp.sum(jnp.mean(diff * diff, axis=-1, keepdims=True), axis=0, keepdims=True)
        dx2 = diff * (1.0 / d)
        dx2_ref[...] = dx2
        dy, dg_rows = _rmsnorm_bwd(dx2, y, gain_v)
        dy_ref[...] = dy.astype(BF16)
        _accumulate(loss_ref, jnp.broadcast_to(loss, loss_ref.shape), step)
        _accumulate(dg_ref, _colsum(dg_rows), step)

    row = lambda n, col=0: pl.BlockSpec((tm, n), lambda i: (i, col))
    return pl.pallas_call(
        body, name=name,
        out_shape=(jax.ShapeDtypeStruct((rows, d), BF16), jax.ShapeDtypeStruct((rows, d), BF16),
                   jax.ShapeDtypeStruct((rows, d), F32), jax.ShapeDtypeStruct((SUBLANES, LANES), F32),
                   jax.ShapeDtypeStruct((1, d), F32)),
        grid=(rows // tm,),
        in_specs=[row(d), row(d), row(d, r_block), _full((1, GLA_DV)), _full((d, d)), _full((1, d)), row(d), row(d)],
        out_specs=(row(d), row(d), row(d), _full((SUBLANES, LANES)), _full((1, d))),
        compiler_params=_params("arbitrary"),
    )(o_f, o_b, proj, head_gain, w_out, gain, x1, target)


def odd_out_bwd(dy, w_out, o_f, o_b, proj, head_gain, name):
    rows, d = dy.shape
    tm = min(ROW_TILE, rows)
    r_block = (2 * GLA_HEADS * GLA_DK + GLA_HEADS * GLA_DV) // d

    def body(dy_ref, w_ref, of_ref, ob_ref, r_ref, hg_ref, dr_ref, do_ref, dhg_ref):
        dy2 = _dot_nt(dy_ref[...], w_ref[...])
        hg = hg_ref[...]
        on, hats, rstds = _head_norm(of_ref[...] + ob_ref[...], hg)
        r = r_ref[...]
        sr = _sigmoid(r)
        dr_ref[...] = (dy2 * jnp.concatenate(on, axis=1) * (sr * (1.0 + r * (1.0 - sr)))).astype(BF16)
        d_on = dy2 * (r * sr)
        d_os, dhg = [], None
        for h in range(GLA_HEADS):
            dn = d_on[:, h * GLA_DV:(h + 1) * GLA_DV]
            part = _colsum(dn * hats[h])
            dhg = part if dhg is None else dhg + part
            dng = dn * hg
            d_os.append(rstds[h] * (dng - hats[h] * jnp.mean(dng * hats[h], axis=-1, keepdims=True)))
        do_ref[...] = jnp.concatenate(d_os, axis=1)
        _accumulate(dhg_ref, dhg, pl.program_id(0))

    row = lambda n, col=0: pl.BlockSpec((tm, n), lambda i: (i, col))
    return pl.pallas_call(
        body, name=name,
        out_shape=(jax.ShapeDtypeStruct((rows, d), BF16), jax.ShapeDtypeStruct((rows, d), F32),
                   jax.ShapeDtypeStruct((1, GLA_DV), F32)),
        grid=(rows // tm,),
        in_specs=[row(d), _full((d, d)), row(d), row(d), row(d, r_block), _full((1, GLA_DV))],
        out_specs=(row(d), row(d), _full((1, GLA_DV))),
        compiler_params=_params("arbitrary"),
    )(dy, w_out, o_f, o_b, proj, head_gain)


def local_step(x, target, w):
    g = {}
    proj_e, h0 = norm_matmul(x, w["even_norm_pre"], w["even_w_in"], "even_in_proj")
    h_dir = [rglru_fwd(proj_e, w["rg_conv_w"], w["rg_conv_b"], w["rg_gate_w"][d], w["rg_gate_b"][d],
                       w["rg_lambda"][d], d == 1, "rglru_fwd_%d" % d) for d in range(2)]
    ycat = even_mix_fwd(proj_e, h_dir[0], h_dir[1], w["sc_conv_w"], "even_mix_fwd")
    x1, y_e = even_out_fwd(ycat, w["even_w_out"], w["even_norm_post"], x, "even_out_fwd")
    proj_o, h1 = norm_matmul(x1, w["odd_norm_pre"], w["odd_w_in"], "odd_in_proj")
    o_dir, st_dir = [], []
    for d in range(2):
        o, st = gla_fwd(proj_o, w["gla_wg_pad"], w["gla_b_gate"], d == 1, "gla_fwd_%d" % d)
        o_dir.append(o)
        st_dir.append(st)
    y2, dy_o, dx2, loss, g["odd_norm_post"] = odd_out_fwd(
        o_dir[0], o_dir[1], proj_o, w["gla_norm_g"], w["odd_w_out"], w["odd_norm_post"], x1, target, "odd_out_fwd")
    g["odd_w_out"] = matmul_dw(y2, dy_o, D_MODEL, "odd_w_out_grad")[0]
    dr, d_o, g["gla_norm_g"] = odd_out_bwd(dy_o, w["odd_w_out"], o_dir[0], o_dir[1], proj_o, w["gla_norm_g"],
                                           "odd_out_bwd")
    dq, dk, dv, dz_f = gla_bwd(proj_o, w["gla_wg_pad"], w["gla_b_gate"], d_o, st_dir[0], None, False, "gla_bwd_0")
    dq, dk, dv, dz_b = gla_bwd(proj_o, w["gla_wg_pad"], w["gla_b_gate"], d_o, st_dir[1], (dq, dk, dv), True,
                               "gla_bwd_1")
    dlr, g["gla_w_gate_lr"], g["gla_b_gate"] = gla_gate_bwd(proj_o, dz_f, dz_b, w["gla_wg_pad"], "gla_gate_bwd")
    dproj_o = [dq, dk, dv, dr, dlr]
    g["odd_w_in"] = jnp.concatenate(matmul_dw_pieces(h1, dproj_o, "odd_w_in_grad"), axis=1)[:, :ODD_IN]
    dx1, g["odd_norm_pre"] = inproj_bwd_pieces(dproj_o, w["odd_w_in"], x1, w["odd_norm_pre"], dx2, "odd_in_proj_bwd")
    dy_e, dycat, g["even_norm_post"] = even_out_bwd(dx1, y_e, w["even_norm_post"], w["even_w_out"], "even_out_bwd")
    g["even_w_out"] = matmul_dw(ycat, dy_e, D_MODEL, "even_w_out_grad")[0]
    dua, dgw, dgb, dlam = [], [], [], []
    for d in range(2):
        a, b, c, e = rglru_bwd(proj_e, dycat, h_dir[d], w["rg_conv_w"], w["rg_conv_b"], w["rg_gate_w"][d],
                               w["rg_gate_b"][d], w["rg_lambda"][d], d == 1, "rglru_bwd_%d" % d)
        dua.append(a)
        dgw.append(b)
        dgb.append(c)
        dlam.append(e)
    dproj_e, g["rg_conv_w"], g["rg_conv_b"], g["sc_conv_w"] = even_mix_bwd(
        proj_e, dycat, h_dir[0], h_dir[1], dua[0], dua[1], w["rg_conv_w"], w["sc_conv_w"], "even_mix_bwd")
    dgw = jnp.stack(dgw).reshape(2, RG_HEADS, RG_HEAD_DIM, 2, RG_HEAD_DIM)
    g["rg_gate_w"] = jnp.transpose(dgw, (0, 3, 1, 2, 4))
    g["rg_gate_b"] = jnp.stack(dgb).reshape(2, 2, RG_HEADS, RG_HEAD_DIM)
    g["rg_lambda"] = jnp.concatenate(dlam, axis=0)
    g["even_w_in"] = matmul_dw(h0, dproj_e, EVEN_IN // 4, "even_w_in_grad")
    grad_x, g["even_norm_pre"] = inproj_bwd(dproj_e, w["even_w_in"], x, w["even_norm_pre"], dx1, "even_in_proj_bwd")
    return loss, grad_x, g


def _prepare_weights(full):
    w = {}
    for name in ("even_norm_pre", "even_norm_post", "rg_conv_b", "odd_norm_pre", "odd_norm_post", "gla_norm_g"):
        w[name] = full[name].reshape(1, -1)
    w["rg_conv_w"] = full["rg_conv_w"]
    w["sc_conv_w"] = full["sc_conv_w"]
    w["even_w_in"] = full["even_w_in"].astype(BF16)
    if w["even_w_in"].ndim == 2:
        w["even_w_in"] = jnp.transpose(w["even_w_in"].reshape(D_MODEL, 4, EVEN_IN // 4), (1, 0, 2))
    w["even_w_out"] = full["even_w_out"].astype(BF16)
    gw = jnp.transpose(full["rg_gate_w"].astype(BF16), (0, 2, 3, 1, 4))
    w["rg_gate_w"] = gw.reshape(2, RG_HEADS, RG_HEAD_DIM, 2 * RG_HEAD_DIM)
    w["rg_gate_b"] = full["rg_gate_b"].reshape(2, 2, D_MODEL)
    w["rg_lambda"] = full["rg_lambda"].reshape(2, 1, D_MODEL)
    w_in = jnp.pad(full["odd_w_in"].astype(BF16), ((0, 0), (0, ODD_IN_PAD - ODD_IN)))
    w["odd_w_in"] = w_in.reshape(1, D_MODEL, ODD_IN_PAD)
    w["odd_w_out"] = full["odd_w_out"].astype(BF16)
    wg = full["gla_w_gate_lr"].astype(BF16)
    w["gla_wg_pad"] = jnp.stack([jnp.pad(wg[d], ((d * GLA_RANK, LANES - (d + 1) * GLA_RANK), (0, 0)))
                                 for d in range(2)])
    w["gla_b_gate"] = full["gla_b_gate"].reshape(2, 1, GLA_HEADS * GLA_DK)
    return w


SHARDED_SMALL = (("rg_conv_w", (4, 256)), ("rg_lambda", (2, 256)), ("sc_conv_w", (3, 256)),
                 ("odd_norm_pre", (256,)), ("odd_norm_post", (256,)), ("gla_w_gate_lr", (2, 16, 128)),
                 ("gla_b_gate", (2, 128)), ("gla_norm_g", (64,)))
SHARDED_ROWS = 64
REPLICATED = (("even_norm_pre", (1024,)), ("even_norm_post", (1024,)), ("rg_conv_b", (1024,)),
              ("rg_gate_b", (2, 2, 8, 128)), ("rg_gate_w", (2, 2, 8, 128, 128)))
REPLICATED_ROWS = 4160
REP_PART = REPLICATED_ROWS // 8
HALF_SHARDED = SHARDED_ROWS // 2
PACK_HALF = HALF_SHARDED + REP_PART


def _seg_rows(shape):
    n = 1
    for s in shape:
        n *= s
    return -(-n // LANES)


def _pack(arrays, spec, total_rows, lead=()):
    parts = []
    for name, shape in spec:
        flat = arrays[name].reshape(lead + (-1,))
        pad = _seg_rows(shape) * LANES - flat.shape[-1]
        if pad:
            flat = jnp.pad(flat, [(0, 0)] * len(lead) + [(0, pad)])
        parts.append(flat.reshape(lead + (-1, LANES)))
    rows = jnp.concatenate(parts, axis=len(lead))
    pad = total_rows - rows.shape[len(lead)]
    return jnp.pad(rows, [(0, 0)] * len(lead) + [(0, pad), (0, 0)])


def _unpack(rows, spec, lead=()):
    out, at = {}, 0
    for name, shape in spec:
        n = 1
        for s in shape:
            n *= s
        k = _seg_rows(shape)
        seg = lax.slice_in_dim(rows, at, at + k, axis=len(lead)).reshape(lead + (-1,))
        out[name] = lax.slice_in_dim(seg, 0, n, axis=len(lead)).reshape(lead + shape)
        at += k
    return out


def _split_owners(arr):
    a = arr.reshape(arr.shape[:-1] + (4, arr.shape[-1] // 4))
    return jnp.moveaxis(a, -2, 0)


def _merge_owners(arr):
    a = jnp.moveaxis(arr, 0, -2)
    return a.reshape(a.shape[:-2] + (-1,))


HBM_SPEC = pl.BlockSpec(memory_space=pltpu.HBM)


def _position():
    x, y, c = lax.axis_index("x"), lax.axis_index("y"), lax.axis_index("c")
    chips = [(1 - x, y), (x, 1 - y), (1 - x, 1 - y)]
    return x, y, c, chips


def _remote(src, dst, send_sem, recv_sem, device):
    return pltpu.make_async_remote_copy(src_ref=src, dst_ref=dst, send_sem=send_sem, recv_sem=recv_sem,
                                        device_id=device, device_id_type=MESH)


def gather_weights(halved, whole):
    n_h, n_w = len(halved), len(whole)
    n = n_h + n_w

    def body(*refs):
        ins, outs = refs[:n], refs[n:2 * n]
        send_ici, recv_ici, send_fwd, recv_fwd = refs[2 * n:]
        x, y, c, chips = _position()
        me = 2 * x + y
        sibling = (x, y, 1 - c)
        sends = []
        for a in range(n):
            for k, chip in enumerate(chips):
                src = ins[a].at[c] if a < n_h else ins[a]
                dst = outs[a].at[me, c] if a < n_h else outs[a].at[me]
                cp = _remote(src, dst, send_ici.at[3 * a + k], recv_ici.at[3 * a + k], (chip[0], chip[1], c))
                cp.start()
                sends.append(cp)
        for a in range(n):
            for k, chip in enumerate(chips):
                q = 2 * chip[0] + chip[1]
                landed = outs[a].at[q, c] if a < n_h else outs[a].at[q]
                _remote(landed, landed, send_ici.at[3 * a + k], recv_ici.at[3 * a + k], sibling).wait_recv()
                if a < n_h:
                    cp = _remote(landed, landed, send_fwd.at[3 * a + k], recv_fwd.at[3 * a + k], sibling)
                    cp.start()
                    sends.append(cp)
        for a in range(n_h):
            for k, chip in enumerate(chips):
                q = 2 * chip[0] + chip[1]
                passed = outs[a].at[q, 1 - c]
                _remote(passed, passed, send_fwd.at[3 * a + k], recv_fwd.at[3 * a + k], sibling).wait_recv()
        for cp in sends:
            cp.wait_send()

    arrays = list(halved) + list(whole)
    out_shape = [jax.ShapeDtypeStruct((4,) + a.shape, a.dtype) for a in arrays]
    outs = pl.pallas_call(
        body, name="gather_weights",
        out_shape=out_shape,
        in_specs=[HBM_SPEC] * n, out_specs=[HBM_SPEC] * n,
        scratch_shapes=[pltpu.SemaphoreType.DMA((3 * n,)), pltpu.SemaphoreType.DMA((3 * n,)),
                        pltpu.SemaphoreType.DMA((3 * n_h,)), pltpu.SemaphoreType.DMA((3 * n_h,))],
    )(*arrays)
    return outs[:n_h], outs[n_h:]


def place_own(full, own, chip, name):
    _, _, r, cols = full.shape
    tr = _row_tile(r, cols)

    def body(p_ref, own_ref, full_ref, o_ref):
        o_ref[0] = own_ref[...]

    return pl.pallas_call(
        body, name=name,
        out_shape=jax.ShapeDtypeStruct(full.shape, full.dtype),
        grid_spec=pltpu.PrefetchScalarGridSpec(
            num_scalar_prefetch=1, grid=(2, r // tr),
            in_specs=[pl.BlockSpec((1, tr, cols), lambda h, i, p_ref: (h, i, 0)), pl.BlockSpec(memory_space=pl.ANY)],
            out_specs=pl.BlockSpec((1, 1, tr, cols), lambda h, i, p_ref: (p_ref[0], h, i, 0))),
        input_output_aliases={2: 0},
        compiler_params=_params("parallel", "parallel"),
    )(chip, own, full)


def exchange_with_sibling(arrays):
    n = len(arrays)

    def body(*refs):
        ins, outs = refs[:n], refs[n:2 * n]
        send_sems, recv_sems = refs[2 * n:]
        x, y, c, _ = _position()
        copies = []
        for a in range(n):
            cp = _remote(ins[a].at[:, 1 - c], outs[a], send_sems.at[a], recv_sems.at[a], (x, y, 1 - c))
            cp.start()
            copies.append(cp)
        for cp in copies:
            cp.wait()

    return pl.pallas_call(
        body, name="grad_exchange_sibling",
        out_shape=[jax.ShapeDtypeStruct((a.shape[0],) + a.shape[2:], a.dtype) for a in arrays],
        in_specs=[HBM_SPEC] * n, out_specs=[HBM_SPEC] * n,
        scratch_shapes=[pltpu.SemaphoreType.DMA((n,)), pltpu.SemaphoreType.DMA((n,))],
    )(*arrays)


def exchange_with_chips(arrays):
    n = len(arrays)

    def body(*refs):
        ins, outs = refs[:n], refs[n:2 * n]
        send_sems, recv_sems = refs[2 * n:]
        x, y, c, chips = _position()
        copies = []
        for a in range(n):
            for k, chip in enumerate(chips):
                q = 2 * chip[0] + chip[1]
                cp = _remote(ins[a].at[q], outs[a].at[k], send_sems.at[3 * a + k], recv_sems.at[3 * a + k],
                             (chip[0], chip[1], c))
                cp.start()
                copies.append(cp)
        for cp in copies:
            cp.wait()

    return pl.pallas_call(
        body, name="grad_exchange_chips",
        out_shape=[jax.ShapeDtypeStruct((3,) + a.shape[1:], a.dtype) for a in arrays],
        in_specs=[HBM_SPEC] * n, out_specs=[HBM_SPEC] * n,
        scratch_shapes=[pltpu.SemaphoreType.DMA((3 * n,)), pltpu.SemaphoreType.DMA((3 * n,))],
    )(*arrays)


def share_totals(totals, pack_total):
    arrays = list(totals) + [pack_total]
    n = len(arrays)

    def body(*refs):
        ins, outs, rep = refs[:n], refs[n:2 * n], refs[2 * n]
        send_sems, recv_sems, rep_send, rep_recv = refs[2 * n + 1:]
        x, y, c, chips = _position()
        sibling = (x, y, 1 - c)
        sends = []
        for a in range(n):
            cp = _remote(ins[a], outs[a], send_sems.at[a], recv_sems.at[a], sibling)
            cp.start()
            sends.append(cp)
        mine = ins[n - 1].at[pl.ds(HALF_SHARDED, REP_PART)]
        slot = rep.at[4 * x + 2 * y + c]
        peers = [sibling]
        for chip in chips:
            peers += [(chip[0], chip[1], c), (chip[0], chip[1], 1 - c)]
        for j, peer in enumerate(peers):
            cp = _remote(mine, slot, rep_send.at[j], rep_recv.at[j], peer)
            cp.start()
            sends.append(cp)
        for a in range(n):
            _remote(outs[a], outs[a], send_sems.at[a], recv_sems.at[a], sibling).wait_recv()
        for j, peer in enumerate(peers):
            landed = rep.at[4 * peer[0] + 2 * peer[1] + peer[2]]
            _remote(landed, landed, rep_send.at[j], rep_recv.at[j], peer).wait_recv()
        for cp in sends:
            cp.wait_send()

    outs = pl.pallas_call(
        body, name="grad_share_totals",
        out_shape=[jax.ShapeDtypeStruct(a.shape, a.dtype) for a in arrays]
        + [jax.ShapeDtypeStruct((8, REP_PART, LANES), F32)],
        in_specs=[HBM_SPEC] * n, out_specs=[HBM_SPEC] * (n + 1),
        scratch_shapes=[pltpu.SemaphoreType.DMA((n,)), pltpu.SemaphoreType.DMA((n,)),
                        pltpu.SemaphoreType.DMA((7,)), pltpu.SemaphoreType.DMA((7,))],
    )(*arrays)
    return outs[:n], outs[n]


TILE_BYTES = 1 << 20


def _row_tile(rows, cols):
    best = None
    for t in range(SUBLANES, rows + 1, SUBLANES):
        if rows % t == 0 and t * cols * 4 <= TILE_BYTES:
            best = t
    return best if best is not None else rows


def add_sibling(mine, received, core, out_dtype, name):
    _, _, r, cols = mine.shape
    tr = _row_tile(r, cols)

    def body(c_ref, a_ref, b_ref, o_ref):
        o_ref[...] = (a_ref[0] + b_ref[...]).astype(out_dtype)

    return pl.pallas_call(
        body, name=name,
        out_shape=jax.ShapeDtypeStruct((4, r, cols), out_dtype),
        grid_spec=pltpu.PrefetchScalarGridSpec(
            num_scalar_prefetch=1, grid=(4, r // tr),
            in_specs=[pl.BlockSpec((1, 1, tr, cols), lambda o, i, c_ref: (o, c_ref[0], i, 0)),
                      pl.BlockSpec((1, tr, cols), lambda o, i, c_ref: (o, i, 0))],
            out_specs=pl.BlockSpec((1, tr, cols), lambda o, i, c_ref: (o, i, 0))),
        compiler_params=_params("parallel", "parallel"),
    )(core, mine, received)


def add_chips(own, received, chip, name):
    _, r, cols = own.shape
    tr = _row_tile(r, cols)

    def body(p_ref, a_ref, b0, b1, b2, o_ref):
        o_ref[...] = ((a_ref[0].astype(F32) + b0[0].astype(F32)) + b1[0].astype(F32)) + b2[0].astype(F32)

    rb = lambda k: pl.BlockSpec((1, tr, cols), lambda i, p_ref: (k, i, 0))
    return pl.pallas_call(
        body, name=name,
        out_shape=jax.ShapeDtypeStruct((r, cols), F32),
        grid_spec=pltpu.PrefetchScalarGridSpec(
            num_scalar_prefetch=1, grid=(r // tr,),
            in_specs=[pl.BlockSpec((1, tr, cols), lambda i, p_ref: (p_ref[0], i, 0)), rb(0), rb(1), rb(2)],
            out_specs=pl.BlockSpec((tr, cols), lambda i, p_ref: (i, 0))),
        compiler_params=_params("parallel"),
    )(chip, own, received, received, received)


def _adamw_update(gv, w_ref, m_ref, v_ref, d_ref, nm_ref, nv_ref):
    nm = ADAM_B1 * m_ref[...] + (1.0 - ADAM_B1) * gv
    nv = ADAM_B2 * v_ref[...] + (1.0 - ADAM_B2) * (gv * gv)
    nm_ref[...] = nm
    nv_ref[...] = nv
    m_hat = nm / (1.0 - ADAM_B1 ** ADAM_STEP)
    v_hat = nv / (1.0 - ADAM_B2 ** ADAM_STEP)
    d_ref[...] = -ADAM_LR * (m_hat / (jnp.sqrt(v_hat) + ADAM_EPS) + ADAM_WD * w_ref[...])


def adamw_halves(w, own, received, m, v, core, name):
    rows, cols = w.shape
    r = rows // 2
    tr = _row_tile(r, cols)
    nr = r // tr

    def body(c_ref, w_ref, own_ref, rec_ref, m_ref, v_ref, g_ref, d_ref, nm_ref, nv_ref):
        gv = jnp.where(pl.program_id(0) == c_ref[0], own_ref[...], rec_ref[...])
        g_ref[...] = gv
        _adamw_update(gv, w_ref, m_ref, v_ref, d_ref, nm_ref, nv_ref)

    whole = pl.BlockSpec((tr, cols), lambda h, i, c_ref: (h * nr + i, 0))
    half = pl.BlockSpec((tr, cols), lambda h, i, c_ref: (i, 0))
    return pl.pallas_call(
        body, name=name,
        out_shape=(jax.ShapeDtypeStruct((rows, cols), F32),) * 4,
        grid_spec=pltpu.PrefetchScalarGridSpec(
            num_scalar_prefetch=1, grid=(2, nr),
            in_specs=[whole, half, half, whole, whole], out_specs=(whole,) * 4),
        compiler_params=_params("parallel", "parallel"),
    )(core, w, own, received, m, v)


def adamw(w, g, m, v, name):
    r, cols = w.shape
    tr = _row_tile(r, cols)

    def body(w_ref, g_ref, m_ref, v_ref, d_ref, nm_ref, nv_ref):
        _adamw_update(g_ref[...], w_ref, m_ref, v_ref, d_ref, nm_ref, nv_ref)

    blk = pl.BlockSpec((tr, cols), lambda i: (i, 0))
    return pl.pallas_call(
        body, name=name,
        out_shape=(jax.ShapeDtypeStruct((r, cols), F32),) * 3,
        grid=(r // tr,),
        in_specs=[blk] * 4, out_specs=(blk,) * 3,
        compiler_params=_params("parallel"),
    )(w, g, m, v)


WEIGHTS = ("even_norm_pre", "even_norm_post", "even_w_in", "rg_conv_w", "rg_conv_b", "rg_gate_w", "rg_gate_b",
           "rg_lambda", "sc_conv_w", "even_w_out", "odd_norm_pre", "odd_norm_post", "odd_w_in", "gla_w_gate_lr",
           "gla_b_gate", "gla_norm_g", "odd_w_out")
BIG = ("even_w_in", "even_w_out", "odd_w_in", "odd_w_out")


def _halves(a):
    return a.reshape((2, a.shape[0] // 2) + a.shape[1:])


def kernel(x, even_norm_pre, even_norm_post, even_w_in, rg_conv_w, rg_conv_b, rg_gate_w, rg_gate_b, rg_lambda, sc_conv_w, even_w_out, odd_norm_pre, odd_norm_post, odd_w_in, gla_w_gate_lr, gla_b_gate, gla_norm_g, odd_w_out, loss_target, m_even_norm_pre, m_even_norm_post, m_even_w_in, m_rg_conv_w, m_rg_conv_b, m_rg_gate_w, m_rg_gate_b, m_rg_lambda, m_sc_conv_w, m_even_w_out, m_odd_norm_pre, m_odd_norm_post, m_odd_w_in, m_gla_w_gate_lr, m_gla_b_gate, m_gla_norm_g, m_odd_w_out, v_even_norm_pre, v_even_norm_post, v_even_w_in, v_rg_conv_w, v_rg_conv_b, v_rg_gate_w, v_rg_gate_b, v_rg_lambda, v_sc_conv_w, v_even_w_out, v_odd_norm_pre, v_odd_norm_post, v_odd_w_in, v_gla_w_gate_lr, v_gla_b_gate, v_gla_norm_g, v_odd_w_out):
    given = dict(locals())
    shard = {n: given[n][0] for n in WEIGHTS}
    m_in = {n: given["m_" + n][0] for n in WEIGHTS}
    v_in = {n: given["v_" + n][0] for n in WEIGHTS}
    mx, my, mc = lax.axis_index("x"), lax.axis_index("y"), lax.axis_index("c")
    core = jnp.reshape(mc, (1,)).astype(jnp.int32)
    chip = jnp.reshape(2 * mx + my, (1,)).astype(jnp.int32)

    small_shard = _pack(shard, SHARDED_SMALL, SHARDED_ROWS)
    big_own = [_halves(shard[n].astype(BF16)) for n in BIG]
    big_full, (small_full,) = gather_weights(big_own, [small_shard])
    big_full = [place_own(a, b, chip, "place_" + n) for a, b, n in zip(big_full, big_own, BIG)]
    small_full = lax.dynamic_update_slice(small_full, small_shard[None], (chip[0], 0, 0))
    full = {n: shard[n] for n, _ in REPLICATED}
    full.update({n: _merge_owners(a) for n, a in _unpack(small_full, SHARDED_SMALL, lead=(4,)).items()})
    full["even_w_in"] = big_full[0].reshape(4, D_MODEL, EVEN_IN // 4)
    full["even_w_out"] = big_full[1].reshape(2 * D_MODEL, D_MODEL)
    full["odd_w_in"] = jnp.transpose(big_full[2].reshape(4, D_MODEL, ODD_IN // 4), (1, 0, 2)).reshape(D_MODEL, ODD_IN)
    full["odd_w_out"] = big_full[3].reshape(D_MODEL, D_MODEL)

    loss, grad_x, g = local_step(x[0], loss_target[0], _prepare_weights(full))
    loss = lax.psum(loss[0, 0], ("x", "y", "c"))

    rep_rows = _pack(g, REPLICATED, REPLICATED_ROWS).reshape(4, 2, REP_PART, LANES)
    sh_rows = _pack({n: _split_owners(g[n]) for n, _ in SHARDED_SMALL}, SHARDED_SMALL, SHARDED_ROWS, lead=(4,))
    pack = jnp.concatenate([sh_rows.reshape(4, 2, HALF_SHARDED, LANES), rep_rows], axis=2)
    slabs = [g["even_w_in"],
             g["even_w_out"].reshape(4, D_MODEL // 2, D_MODEL),
             jnp.transpose(g["odd_w_in"].reshape(D_MODEL, 4, ODD_IN // 4), (1, 0, 2)),
             g["odd_w_out"].reshape(4, D_MODEL // 4, D_MODEL),
             pack.reshape(4, 2 * PACK_HALF, LANES)]
    slabs = [a.reshape((4, 2, a.shape[1] // 2) + a.shape[2:]) for a in slabs]
    from_sibling = exchange_with_sibling(slabs)
    chip_sums = [add_sibling(a, b, core, BF16 if i < 4 else F32, "grad_add_sibling_%d" % i)
                 for i, (a, b) in enumerate(zip(slabs, from_sibling))]
    from_chips = exchange_with_chips(chip_sums)
    totals = [add_chips(a, b, chip, "grad_add_chips_%d" % i) for i, (a, b) in enumerate(zip(chip_sums, from_chips))]
    from_core, rep_all = share_totals(totals[:4], totals[4])
    mine, theirs = totals[4][:HALF_SHARDED], from_core[4][:HALF_SHARDED]
    sh_total = jnp.where(mc == 0, jnp.concatenate([mine, theirs]), jnp.concatenate([theirs, mine]))
    rep_all = lax.dynamic_update_slice(rep_all, totals[4][None, HALF_SHARDED:], (2 * chip[0] + core[0], 0, 0))
    rep_total = rep_all.reshape(REPLICATED_ROWS, LANES)
    grads = {}
    grads.update(_unpack(sh_total, SHARDED_SMALL))
    grads.update(_unpack(rep_total, REPLICATED))

    delta, new_m, new_v = {}, {}, {}
    for i, n in enumerate(BIG):
        grads[n], delta[n], new_m[n], new_v[n] = adamw_halves(shard[n], totals[i], from_core[i], m_in[n], v_in[n],
                                                              core, "adamw_" + n)
    packed = [jnp.concatenate([_pack(src, SHARDED_SMALL, SHARDED_ROWS), _pack(src, REPLICATED, REPLICATED_ROWS)])
              for src in (shard, m_in, v_in)]
    small_g = jnp.concatenate([sh_total, rep_total], axis=0)
    outs = adamw(packed[0], small_g, packed[1], packed[2], "adamw_small")
    for dst, rows in zip((delta, new_m, new_v), outs):
        dst.update(_unpack(rows[:SHARDED_ROWS], SHARDED_SMALL))
        dst.update(_unpack(rows[SHARDED_ROWS:], REPLICATED))
    result = [loss, grad_x[None]]
    for group in (grads, delta, new_m, new_v):
        result += [group[n].reshape(given[n].shape) for n in WEIGHTS]
    return tuple(result)
```

```python
import functools

import jax
import jax.numpy as jnp
from jax import lax
from jax.experimental import pallas as pl
from jax.experimental.pallas import tpu as pltpu

F32 = jnp.float32
BF16 = jnp.bfloat16
MESH = pl.DeviceIdType.MESH

D_MODEL = 1024
NORM_EPS = 1e-6
RG_HEADS = 8
RG_HEAD_DIM = 128
RG_C = 8.0
EVEN_IN = 6144
ODD_IN = 3104
ODD_IN_PAD = 3200
GLA_HEADS = 4
GLA_DK = 128
GLA_DV = 256
GLA_RANK = 16
GLA_NORMALIZER = 16.0
GLA_CHUNK = 64
LR_COL = 3072

ADAM_LR = 0.001
ADAM_B1 = 0.9
ADAM_B2 = 0.999
ADAM_EPS = 1e-08
ADAM_WD = 0.01
ADAM_STEP = 10

SUBLANES = 8
LANES = 128
VMEM_LIMIT = 56 * 2 ** 20

ROW_TILE = 512
SCAN_TILE = 256
GLA_BLOCK = 1024
MIX_TILE = 128


def _params(*sem):
    return pltpu.CompilerParams(dimension_semantics=sem, vmem_limit_bytes=VMEM_LIMIT)


def _full(shape):
    n = len(shape)
    return pl.BlockSpec(shape, lambda *_: (0,) * n)


def _sigmoid(x):
    return 1.0 / (1.0 + jnp.exp(-x))


def _softplus(x):
    return jnp.maximum(x, 0.0) + jnp.log(1.0 + jnp.exp(-jnp.abs(x)))


def _one_minus_exp(x):
    series = -x * (1.0 + x * (1.0 / 2.0) * (1.0 + x * (1.0 / 3.0) * (1.0 + x * (1.0 / 4.0) * (
        1.0 + x * (1.0 / 5.0) * (1.0 + x * (1.0 / 6.0))))))
    return jnp.where(x > -0.25, series, 1.0 - jnp.exp(x))


def _dot(a, b):
    return jnp.dot(a, b, preferred_element_type=F32)


def _dot_nt(a, b):
    return lax.dot_general(a, b, (((1,), (1,)), ((), ())), preferred_element_type=F32)


def _dot_tn(a, b):
    return lax.dot_general(a, b, (((0,), (0,)), ((), ())), preferred_element_type=F32)


def _bdot(a, b, ca, cb):
    return lax.dot_general(a, b, (((ca,), (cb,)), ((0,), (0,))), preferred_element_type=F32)


def _halo_specs(rows, cols, col_block, n_row_tiles, tix):
    per = rows // SUBLANES
    last = n_row_tiles * per - 1

    def split(args):
        if len(args) == 2:
            return tix(args[1]), col_block + args[0]
        return tix(args[0]), col_block

    def prev(*args):
        t, c = split(args)
        return (jnp.maximum(t * per - 1, 0), c)

    def main(*args):
        return split(args)

    def nxt(*args):
        t, c = split(args)
        return (jnp.minimum((t + 1) * per, last), c)

    return [pl.BlockSpec((SUBLANES, cols), prev), pl.BlockSpec((rows, cols), main),
            pl.BlockSpec((SUBLANES, cols), nxt)]


def _extend(prev_ref, main_ref, next_ref, is_first, is_last):
    p = jnp.where(is_first, 0.0, prev_ref[...])
    n = jnp.where(is_last, 0.0, next_ref[...])
    return jnp.concatenate([p, main_ref[...], n], axis=0)


def _shifted(ext, offset, rows):
    if offset == 0:
        return ext[SUBLANES:SUBLANES + rows]
    n = ext.shape[0]
    return pltpu.roll(ext, (-offset) % n, 0)[SUBLANES:SUBLANES + rows]


def _conv(ext, w, left, rows):
    out = None
    for k in range(w.shape[0]):
        term = _shifted(ext, k - left, rows) * w[k:k + 1]
        out = term if out is None else out + term
    return out


def _conv_transpose(ext, w, left, rows):
    out = None
    for k in range(w.shape[0]):
        term = _shifted(ext, left - k, rows) * w[k:k + 1]
        out = term if out is None else out + term
    return out


def _colsum(x):
    return jnp.sum(x, axis=0, keepdims=True)


def _accumulate(ref, value, step):
    @pl.when(step == 0)
    def _():
        ref[...] = value

    @pl.when(step > 0)
    def _():
        ref[...] += value


def norm_matmul(x, gain, w, name):
    rows, d = x.shape
    n_col_tiles, _, tn = w.shape
    tm = min(ROW_TILE, rows)

    def body(x_ref, g_ref, w_ref, proj_ref, h_ref, h_scr):
        @pl.when(pl.program_id(1) == 0)
        def _():
            xv = x_ref[...]
            rstd = lax.rsqrt(jnp.mean(xv * xv, axis=-1, keepdims=True) + NORM_EPS)
            hv = (xv * rstd * g_ref[...]).astype(BF16)
            h_scr[...] = hv
            h_ref[...] = hv

        proj_ref[...] = _dot(h_scr[...], w_ref[0])

    return pl.pallas_call(
        body, name=name,
        out_shape=(jax.ShapeDtypeStruct((rows, n_col_tiles * tn), F32), jax.ShapeDtypeStruct((rows, d), BF16)),
        grid=(rows // tm, n_col_tiles),
        in_specs=[pl.BlockSpec((tm, d), lambda i, j: (i, 0)), _full((1, d)),
                  pl.BlockSpec((1, d, tn), lambda i, j: (j, 0, 0))],
        out_specs=(pl.BlockSpec((tm, tn), lambda i, j: (i, j)), pl.BlockSpec((tm, d), lambda i, j: (i, 0))),
        scratch_shapes=[pltpu.VMEM((tm, d), BF16)],
        compiler_params=_params("parallel", "arbitrary"),
    )(x, gain, w)


def inproj_bwd(dproj, w, x, gain, dres, name):
    rows, d = x.shape
    n_col_tiles, _, tn = w.shape
    tm = min(ROW_TILE, rows)

    def body(dp_ref, w_ref, x_ref, g_ref, dres_ref, dx_ref, dg_ref, acc):
        i, j = pl.program_id(0), pl.program_id(1)
        part = _dot_nt(dp_ref[...], w_ref[0])

        @pl.when(j == 0)
        def _():
            acc[...] = part

        @pl.when(j > 0)
        def _():
            acc[...] += part

        @pl.when(j == n_col_tiles - 1)
        def _():
            _inproj_finish(acc[...], x_ref, g_ref, dres_ref, dx_ref, dg_ref, i)

    return pl.pallas_call(
        body, name=name,
        out_shape=(jax.ShapeDtypeStruct((rows, d), F32), jax.ShapeDtypeStruct((1, d), F32)),
        grid=(rows // tm, n_col_tiles),
        in_specs=[pl.BlockSpec((tm, tn), lambda i, j: (i, j)), pl.BlockSpec((1, d, tn), lambda i, j: (j, 0, 0)),
                  pl.BlockSpec((tm, d), lambda i, j: (i, 0)), _full((1, d)),
                  pl.BlockSpec((tm, d), lambda i, j: (i, 0))],
        out_specs=(pl.BlockSpec((tm, d), lambda i, j: (i, 0)), _full((1, d))),
        scratch_shapes=[pltpu.VMEM((tm, d), F32)],
        compiler_params=_params("arbitrary", "arbitrary"),
    )(dproj, w, x, gain, dres)


def _inproj_finish(dh, x_ref, g_ref, dres_ref, dx_ref, dg_ref, step):
    xv = x_ref[...]
    rstd = lax.rsqrt(jnp.mean(xv * xv, axis=-1, keepdims=True) + NORM_EPS)
    xhat = xv * rstd
    dxn = dh * g_ref[...]
    dx_ref[...] = dres_ref[...] + rstd * (dxn - xhat * jnp.mean(dxn * xhat, axis=-1, keepdims=True))
    _accumulate(dg_ref, _colsum(dh * xhat), step)


def inproj_bwd_pieces(pieces, w, x, gain, dres, name):
    rows, d = x.shape
    tm = min(ROW_TILE, rows)
    n = len(pieces)
    widths = [p.shape[1] for p in pieces]
    starts = [sum(widths[:k]) for k in range(n)]
    assert sum(widths) == w.shape[2]

    def body(*refs):
        w_ref, x_ref, g_ref, dres_ref, dx_ref, dg_ref = refs[n:]
        dh = None
        for k in range(n):
            part = _dot_nt(refs[k][...], w_ref[0, :, starts[k]:starts[k] + widths[k]])
            dh = part if dh is None else dh + part
        _inproj_finish(dh, x_ref, g_ref, dres_ref, dx_ref, dg_ref, pl.program_id(0))

    row = lambda cols: pl.BlockSpec((tm, cols), lambda i: (i, 0))
    return pl.pallas_call(
        body, name=name,
        out_shape=(jax.ShapeDtypeStruct((rows, d), F32), jax.ShapeDtypeStruct((1, d), F32)),
        grid=(rows // tm,),
        in_specs=[row(wd) for wd in widths] + [_full(w.shape), row(d), _full((1, d)), row(d)],
        out_specs=(row(d), _full((1, d))),
        compiler_params=_params("arbitrary"),
    )(*pieces, w, x, gain, dres)


def matmul_dw_pieces(a, pieces, name):
    rows, m = a.shape
    tk = min(ROW_TILE, rows)
    n = len(pieces)

    def body(*refs):
        a_ref, ins, outs = refs[0], refs[1:1 + n], refs[1 + n:]
        av = a_ref[...]
        for k in range(n):
            _accumulate(outs[k], _dot_tn(av, ins[k][...]), pl.program_id(0))

    return pl.pallas_call(
        body, name=name,
        out_shape=[jax.ShapeDtypeStruct((m, p.shape[1]), F32) for p in pieces],
        grid=(rows // tk,),
        in_specs=[pl.BlockSpec((tk, m), lambda k: (k, 0))]
        + [pl.BlockSpec((tk, p.shape[1]), lambda k: (k, 0)) for p in pieces],
        out_specs=[_full((m, p.shape[1])) for p in pieces],
        compiler_params=_params("arbitrary"),
    )(a, *pieces)


def matmul_dw(a, b, bn, name):
    rows, m = a.shape
    n = b.shape[1]
    tk = min(ROW_TILE, rows)
    steps = rows // tk

    def body(a_ref, b_ref, o_ref):
        part = _dot_tn(a_ref[...], b_ref[...])

        @pl.when(pl.program_id(1) == 0)
        def _():
            o_ref[0] = part

        @pl.when(pl.program_id(1) > 0)
        def _():
            o_ref[0] += part

    return pl.pallas_call(
        body, name=name,
        out_shape=jax.ShapeDtypeStruct((n // bn, m, bn), F32),
        grid=(n // bn, steps),
        in_specs=[pl.BlockSpec((tk, m), lambda j, k: (k, 0)), pl.BlockSpec((tk, bn), lambda j, k: (k, j))],
        out_specs=pl.BlockSpec((1, m, bn), lambda j, k: (j, 0, 0)),
        compiler_params=_params("parallel", "arbitrary"),
    )(a, b)


def _scan(a, b, reverse):
    n = a.shape[0]
    row = lax.broadcasted_iota(jnp.int32, a.shape, 0)
    s = 1
    while s < n:
        if reverse:
            a_s, b_s, valid = pltpu.roll(a, n - s, 0), pltpu.roll(b, n - s, 0), row < n - s
        else:
            a_s, b_s, valid = pltpu.roll(a, s, 0), pltpu.roll(b, s, 0), row >= s
        b = jnp.where(valid, a * b_s + b, b)
        a = jnp.where(valid, a * a_s, a)
        s *= 2
    return a, b


def _rg_gates(ua, gw_ref, gb, lam):
    ub = ua.astype(BF16)
    pre_r, pre_i = [], []
    for h in range(RG_HEADS):
        z = _dot(ub[:, h * RG_HEAD_DIM:(h + 1) * RG_HEAD_DIM], gw_ref[h])
        pre_r.append(z[:, :RG_HEAD_DIM])
        pre_i.append(z[:, RG_HEAD_DIM:])
    r = _sigmoid(jnp.concatenate(pre_r, axis=1) + gb[0:1])
    i = _sigmoid(jnp.concatenate(pre_i, axis=1) + gb[1:2])
    sp = _softplus(-lam)
    log_a = -RG_C * r * sp
    a = jnp.exp(log_a)
    mult = jnp.sqrt(_one_minus_exp(2.0 * log_a))
    return r, i, sp, a, mult


def _rg_weight_specs():
    return [_full((4, D_MODEL)), _full((1, D_MODEL)), _full((RG_HEADS, RG_HEAD_DIM, 2 * RG_HEAD_DIM)),
            _full((2, D_MODEL)), _full((1, D_MODEL))]


def rglru_fwd(proj, conv_w, conv_b, gate_w, gate_b, lam, reverse, name):
    rows_total = proj.shape[0]
    rows = min(SCAN_TILE, rows_total)
    n_tiles = rows_total // rows
    tix = (lambda i: n_tiles - 1 - i) if reverse else (lambda i: i)

    def body(xp, xm, xn, cw_ref, cb_ref, gw_ref, gb_ref, lam_ref, h_ref, carry):
        i = pl.program_id(0)
        t = tix(i)
        ext = _extend(xp, xm, xn, t == 0, t == n_tiles - 1)
        ua = _conv(ext, cw_ref[...], 2, rows) + cb_ref[...]
        _, gi, _, a, mult = _rg_gates(ua, gw_ref, gb_ref[...], lam_ref[...])
        b = mult * (gi * ua)
        a_cum, h0 = _scan(a, b, reverse)

        @pl.when(i == 0)
        def _():
            carry[...] = jnp.zeros_like(carry)

        h = a_cum * carry[0:1] + h0
        h_ref[...] = h
        edge = h[0:1] if reverse else h[rows - 1:rows]
        carry[...] = jnp.broadcast_to(edge, carry.shape)

    return pl.pallas_call(
        body, name=name,
        out_shape=jax.ShapeDtypeStruct((rows_total, D_MODEL), F32),
        grid=(n_tiles,),
        in_specs=_halo_specs(rows, D_MODEL, 0, n_tiles, tix) + _rg_weight_specs(),
        out_specs=pl.BlockSpec((rows, D_MODEL), lambda i: (tix(i), 0)),
        scratch_shapes=[pltpu.VMEM((SUBLANES, D_MODEL), F32)],
        compiler_params=_params("arbitrary"),
    )(proj, proj, proj, conv_w, conv_b, gate_w, gate_b, lam)


def rglru_bwd(proj, dycat, h_dir, conv_w, conv_b, gate_w, gate_b, lam, reverse, name):
    rows_total = proj.shape[0]
    rows = min(SCAN_TILE, rows_total)
    n_tiles = rows_total // rows
    tix = (lambda i: i) if reverse else (lambda i: n_tiles - 1 - i)
    za_block = 1

    def body(xp, xm, xn, za_ref, dya_ref, hp, hm, hn, cw_ref, cb_ref, gw_ref, gb_ref, lam_ref,
             dua_ref, dgw_ref, dgb_ref, dlam_ref, carry):
        step = pl.program_id(0)
        t = tix(step)
        first, last = t == 0, t == n_tiles - 1
        ext = _extend(xp, xm, xn, first, last)
        ua = _conv(ext, cw_ref[...], 2, rows) + cb_ref[...]
        lam_v = lam_ref[...]
        r, gi, sp, a, mult = _rg_gates(ua, gw_ref, gb_ref[...], lam_v)
        za = za_ref[...]
        dh = dya_ref[...] * (za * _sigmoid(za))

        @pl.when(step == 0)
        def _():
            carry[...] = jnp.zeros_like(carry)

        a_cum, mu0 = _scan(a, a * dh, not reverse)
        old = carry[0:1]
        mu = a_cum * old + mu0
        row = lax.broadcasted_iota(jnp.int32, mu.shape, 0)
        if reverse:
            mu_next = jnp.where(row == 0, old, pltpu.roll(mu, 1, 0))
            carry[...] = jnp.broadcast_to(mu[rows - 1:rows], carry.shape)
            h_ext = _extend(hp, hm, hn, first, last)
            h_prev = _shifted(h_ext, 1, rows)
        else:
            mu_next = jnp.where(row == rows - 1, old, pltpu.roll(mu, rows - 1, 0))
            carry[...] = jnp.broadcast_to(mu[0:1], carry.shape)
            h_ext = _extend(hp, hm, hn, first, last)
            h_prev = _shifted(h_ext, -1, rows)
        db = dh + mu_next
        da = db * h_prev
        d_mult = db * (gi * ua)
        di = db * (mult * ua)
        dua = db * (mult * gi)
        dlog_a = da * a - d_mult * (a * a) / mult
        dr = dlog_a * (-RG_C * sp)
        dlam = _colsum(dlog_a * (-RG_C * r)) * (-_sigmoid(-lam_v))
        dpr = dr * (r * (1.0 - r))
        dpi = di * (gi * (1.0 - gi))
        dgb = jnp.concatenate([_colsum(dpr), _colsum(dpi)], axis=0)
        ub = ua.astype(BF16)
        dua_heads, dgw_heads = [], []
        for h in range(RG_HEADS):
            cols = slice(h * RG_HEAD_DIM, (h + 1) * RG_HEAD_DIM)
            dz = jnp.concatenate([dpr[:, cols], dpi[:, cols]], axis=1).astype(BF16)
            dgw_heads.append(_dot_tn(ub[:, cols], dz))
            dua_heads.append(_dot_nt(dz, gw_ref[h]))
        dua_ref[...] = dua + jnp.concatenate(dua_heads, axis=1)

        @pl.when(step == 0)
        def _():
            for h in range(RG_HEADS):
                dgw_ref[h] = dgw_heads[h]
            dgb_ref[...] = dgb
            dlam_ref[...] = dlam

        @pl.when(step > 0)
        def _():
            for h in range(RG_HEADS):
                dgw_ref[h] += dgw_heads[h]
            dgb_ref[...] += dgb
            dlam_ref[...] += dlam

    row_spec = lambda col: pl.BlockSpec((rows, D_MODEL), lambda i: (tix(i), col))
    return pl.pallas_call(
        body, name=name,
        out_shape=(jax.ShapeDtypeStruct((rows_total, D_MODEL), F32),
                   jax.ShapeDtypeStruct((RG_HEADS, RG_HEAD_DIM, 2 * RG_HEAD_DIM), F32),
                   jax.ShapeDtypeStruct((2, D_MODEL), F32), jax.ShapeDtypeStruct((1, D_MODEL), F32)),
        grid=(n_tiles,),
        in_specs=(_halo_specs(rows, D_MODEL, 0, n_tiles, tix) + [row_spec(za_block), row_spec(0)]
                  + _halo_specs(rows, D_MODEL, 0, n_tiles, tix) + _rg_weight_specs()),
        out_specs=(row_spec(0), _full((RG_HEADS, RG_HEAD_DIM, 2 * RG_HEAD_DIM)), _full((2, D_MODEL)),
                   _full((1, D_MODEL))),
        scratch_shapes=[pltpu.VMEM((SUBLANES, D_MODEL), F32)],
        compiler_params=_params("arbitrary"),
    )(proj, proj, proj, proj, dycat, h_dir, h_dir, h_dir, conv_w, conv_b, gate_w, gate_b, lam)


def even_mix_fwd(proj, h_f, h_b, sc_w, name):
    rows_total = proj.shape[0]
    rows = min(MIX_TILE, rows_total)
    n_tiles = rows_total // rows
    cb = D_MODEL
    n_cb = 1
    ident = lambda i: i

    def body(za_ref, hf_ref, hb_ref, xbp, xbm, xbn, gcp, gcm, gcn, gb_ref, zb_ref, w_ref, y_ref):
        t = pl.program_id(1)
        first, last = t == 0, t == n_tiles - 1
        za = za_ref[...]
        y_ref[:, 0:cb] = ((hf_ref[...] + hb_ref[...]) * (za * _sigmoid(za))).astype(BF16)
        p_ext = _extend(xbp, xbm, xbn, first, last) * _extend(gcp, gcm, gcn, first, last)
        cv = _conv(p_ext, w_ref[...], 1, rows)
        zb = zb_ref[...]
        y_ref[:, cb:2 * cb] = (gb_ref[...] * cv * (zb * _sigmoid(zb))).astype(BF16)

    blk = lambda col: pl.BlockSpec((rows, cb), lambda c, i: (i, col * n_cb + c))
    own = pl.BlockSpec((rows, cb), lambda c, i: (i, c))
    return pl.pallas_call(
        body, name=name,
        out_shape=jax.ShapeDtypeStruct((rows_total, 2 * D_MODEL), BF16),
        grid=(n_cb, n_tiles),
        in_specs=([blk(1), own, own] + _halo_specs(rows, cb, 2 * n_cb, n_tiles, ident)
                  + _halo_specs(rows, cb, 4 * n_cb, n_tiles, ident)
                  + [blk(3), blk(5), pl.BlockSpec((3, cb), lambda c, i: (0, c))]),
        out_specs=pl.BlockSpec((rows, 2 * cb), lambda c, i: (i, 0)),
        compiler_params=_params("parallel", "arbitrary"),
    )(proj, h_f, h_b, proj, proj, proj, proj, proj, proj, proj, proj, sc_w)


def even_mix_bwd(proj, dycat, h_f, h_b, dua_f, dua_b, conv_w, sc_w, name):
    rows_total = proj.shape[0]
    rows = min(MIX_TILE, rows_total)
    n_tiles = rows_total // rows
    cb = D_MODEL
    n_cb = 1
    ident = lambda i: i

    def body(xap, xam, xan, za_ref, xbp, xbm, xbn, gbp, gbm, gbn, gcp, gcm, gcn, zbp, zbm, zbn,
             dya_ref, dybp, dybm, dybn, hf_ref, hb_ref, dfp, dfm, dfn, dbp, dbm, dbn, cw_ref, sw_ref,
             dp_ref, dcw_ref, dcb_ref, dsw_ref):
        def put(k, value):
            dp_ref[:, k * cb:(k + 1) * cb] = value.astype(BF16)

        t = pl.program_id(1)
        first, last = t == 0, t == n_tiles - 1
        za = za_ref[...]
        sa = _sigmoid(za)
        put(1, dya_ref[...] * (hf_ref[...] + hb_ref[...]) * (sa * (1.0 + za * (1.0 - sa))))
        dua_ext = _extend(dfp, dfm, dfn, first, last) + _extend(dbp, dbm, dbn, first, last)
        cw = cw_ref[...]
        put(0, _conv_transpose(dua_ext, cw, 2, rows))
        dua = dua_ext[SUBLANES:SUBLANES + rows]
        xa_ext = _extend(xap, xam, xan, first, last)
        dcw = jnp.concatenate([_colsum(dua * _shifted(xa_ext, k - 2, rows)) for k in range(4)], axis=0)
        dcb = _colsum(dua)
        xb_ext = _extend(xbp, xbm, xbn, first, last)
        gc_ext = _extend(gcp, gcm, gcn, first, last)
        p_ext = xb_ext * gc_ext
        zb_ext = _extend(zbp, zbm, zbn, first, last)
        sb_ext = _sigmoid(zb_ext)
        dyb_ext = _extend(dybp, dybm, dybn, first, last)
        gb_ext = _extend(gbp, gbm, gbn, first, last)
        dcv_ext = dyb_ext * gb_ext * (zb_ext * sb_ext)
        sw = sw_ref[...]
        cv = _conv(p_ext, sw, 1, rows)
        mid = slice(SUBLANES, SUBLANES + rows)
        zb, sb, dyb, gb = zb_ext[mid], sb_ext[mid], dyb_ext[mid], gb_ext[mid]
        put(3, dyb * cv * (zb * sb))
        put(5, dyb * gb * cv * (sb * (1.0 + zb * (1.0 - sb))))
        dp = _conv_transpose(dcv_ext, sw, 1, rows)
        put(4, dp * xb_ext[mid])
        put(2, dp * gc_ext[mid])
        dcv = dcv_ext[mid]
        dsw = jnp.concatenate([_colsum(dcv * _shifted(p_ext, k - 1, rows)) for k in range(3)], axis=0)

        @pl.when(t == 0)
        def _():
            dcw_ref[...] = dcw
            dcb_ref[...] = dcb
            dsw_ref[...] = dsw

        @pl.when(t > 0)
        def _():
            dcw_ref[...] += dcw
            dcb_ref[...] += dcb
            dsw_ref[...] += dsw

    blk = lambda col: pl.BlockSpec((rows, cb), lambda c, i: (i, col * n_cb + c))
    halo = lambda col: _halo_specs(rows, cb, col * n_cb, n_tiles, ident)
    own = pl.BlockSpec((rows, cb), lambda c, i: (i, c))
    wspec = lambda k: pl.BlockSpec((k, cb), lambda c, i: (0, c))
    return pl.pallas_call(
        body, name=name,
        out_shape=(jax.ShapeDtypeStruct((rows_total, 6 * D_MODEL), BF16),
                   jax.ShapeDtypeStruct((4, D_MODEL), F32), jax.ShapeDtypeStruct((1, D_MODEL), F32),
                   jax.ShapeDtypeStruct((3, D_MODEL), F32)),
        grid=(n_cb, n_tiles),
        in_specs=(halo(0) + [blk(1)] + halo(2) + halo(3) + halo(4) + halo(5) + [blk(0)] + halo(1)
                  + [own, own] + halo(0) + halo(0) + [wspec(4), wspec(3)]),
        out_specs=(pl.BlockSpec((rows, 6 * cb), lambda c, i: (i, 0)), wspec(4), wspec(1), wspec(3)),
        compiler_params=_params("parallel", "arbitrary"),
    )(proj, proj, proj, proj, proj, proj, proj, proj, proj, proj, proj, proj, proj, proj, proj, proj,
      dycat, dycat, dycat, dycat, h_f, h_b, dua_f, dua_f, dua_f, dua_b, dua_b, dua_b, conv_w, sc_w)


def even_out_fwd(ycat, w_out, gain, x, name):
    rows, d = x.shape
    k = ycat.shape[1]
    tm = min(ROW_TILE, rows)

    def body(yc_ref, w_ref, g_ref, x_ref, x1_ref, y_ref):
        y = _dot(yc_ref[...], w_ref[...])
        y_ref[...] = y
        rstd = lax.rsqrt(jnp.mean(y * y, axis=-1, keepdims=True) + NORM_EPS)
        x1_ref[...] = x_ref[...] + y * rstd * g_ref[...]

    row = lambda n: pl.BlockSpec((tm, n), lambda i: (i, 0))
    return pl.pallas_call(
        body, name=name,
        out_shape=(jax.ShapeDtypeStruct((rows, d), F32),) * 2,
        grid=(rows // tm,),
        in_specs=[row(k), _full((k, d)), _full((1, d)), row(d)],
        out_specs=(row(d), row(d)),
        compiler_params=_params("parallel"),
    )(ycat, w_out, gain, x)


def _rmsnorm_bwd(dout, y, gain):
    rstd = lax.rsqrt(jnp.mean(y * y, axis=-1, keepdims=True) + NORM_EPS)
    yhat = y * rstd
    dyn = dout * gain
    dy = rstd * (dyn - yhat * jnp.mean(dyn * yhat, axis=-1, keepdims=True))
    return dy, dout * yhat


def even_out_bwd(dx1, y, gain, w_out, name):
    rows, d = y.shape
    k = w_out.shape[0]
    tm = min(ROW_TILE, rows)

    def body(dx_ref, y_ref, g_ref, w_ref, dy_ref, dyc_ref, dg_ref):
        dy, dg_rows = _rmsnorm_bwd(dx_ref[...], y_ref[...], g_ref[...])
        dyb = dy.astype(BF16)
        dy_ref[...] = dyb
        dyc_ref[...] = _dot_nt(dyb, w_ref[...])
        _accumulate(dg_ref, _colsum(dg_rows), pl.program_id(0))

    row = lambda n: pl.BlockSpec((tm, n), lambda i: (i, 0))
    return pl.pallas_call(
        body, name=name,
        out_shape=(jax.ShapeDtypeStruct((rows, d), BF16), jax.ShapeDtypeStruct((rows, k), F32),
                   jax.ShapeDtypeStruct((1, d), F32)),
        grid=(rows // tm,),
        in_specs=[row(d), row(d), _full((1, d)), _full((k, d))],
        out_specs=(row(d), row(k), _full((1, d))),
        compiler_params=_params("arbitrary"),
    )(dx1, y, gain, w_out)


def _chunk_cumsum(g, reverse):
    n = g.shape[0]
    pos = lax.broadcasted_iota(jnp.int32, g.shape, 0) % GLA_CHUNK
    s = 1
    while s < GLA_CHUNK:
        if reverse:
            g = g + jnp.where(pos < GLA_CHUNK - s, pltpu.roll(g, n - s, 0), 0.0)
        else:
            g = g + jnp.where(pos >= s, pltpu.roll(g, s, 0), 0.0)
        s *= 2
    return g


def _gla_prepare(q_ref, k_ref, lr_ref, wg_ref, bg_ref, reverse, n_chunks):
    z = _dot(lr_ref[...].astype(BF16), wg_ref[0]) + bg_ref[0]
    g = -_softplus(-z) * (1.0 / GLA_NORMALIZER)
    bcum = _chunk_cumsum(g, reverse).reshape(n_chunks, GLA_CHUNK, GLA_DK)
    edge = 0 if reverse else GLA_CHUNK - 1
    btot = bcum[:, edge:edge + 1, :]
    e_pos = jnp.exp(bcum)
    e_neg = jnp.exp(-bcum)
    e_st = jnp.exp(btot - bcum)
    q3 = q_ref[...].reshape(n_chunks, GLA_CHUNK, GLA_DK)
    k3 = k_ref[...].reshape(n_chunks, GLA_CHUNK, GLA_DK)
    scale = GLA_DK ** -0.5
    q_in = q3 * scale * e_pos
    k_in = k3 * e_neg
    k_st = k3 * e_st
    dec = jnp.exp(btot)
    return z, q_in, k_in, k_st, dec, (scale * e_pos, e_neg, e_st)


def _gla_mask(reverse):
    i = lax.broadcasted_iota(jnp.int32, (GLA_CHUNK, GLA_CHUNK), 0)
    j = lax.broadcasted_iota(jnp.int32, (GLA_CHUNK, GLA_CHUNK), 1)
    return (j >= i) if reverse else (j <= i)


def _gla_specs(rows, n_blocks, reverse):
    tix = (lambda s: n_blocks - 1 - s) if reverse else (lambda s: s)
    d = 1 if reverse else 0
    lr_block = LR_COL // LANES
    specs = [pl.BlockSpec((rows, GLA_DK), lambda h, s: (tix(s), h)),
             pl.BlockSpec((rows, GLA_DK), lambda h, s: (tix(s), GLA_HEADS + h)),
             pl.BlockSpec((rows, GLA_DV), lambda h, s: (tix(s), GLA_HEADS + h)),
             pl.BlockSpec((rows, LANES), lambda h, s: (tix(s), lr_block)),
             pl.BlockSpec((1, LANES, GLA_DK), lambda h, s: (d, 0, h)),
             pl.BlockSpec((1, 1, GLA_DK), lambda h, s: (d, 0, h))]
    return specs, tix


def gla_fwd(proj, wg_pad, bg, reverse, name):
    rows_total = proj.shape[0]
    rows = min(GLA_BLOCK, rows_total)
    n_blocks = rows_total // rows
    n_chunks = rows // GLA_CHUNK
    specs, tix = _gla_specs(rows, n_blocks, reverse)

    def body(q_ref, k_ref, v_ref, lr_ref, wg_ref, bg_ref, o_ref, st_ref, state, kv_scr, dec_scr):
        _, q_in, k_in, k_st, dec, _ = _gla_prepare(q_ref, k_ref, lr_ref, wg_ref, bg_ref, reverse, n_chunks)
        vb = v_ref[...].reshape(n_chunks, GLA_CHUNK, GLA_DV).astype(BF16)
        qb = q_in.astype(BF16)
        p = jnp.where(_gla_mask(reverse), _bdot(qb, k_in.astype(BF16), 2, 2), 0.0)
        o = _bdot(p.astype(BF16), vb, 2, 1)
        kv_scr[...] = _bdot(vb, k_st.astype(BF16), 1, 1)
        dec_scr[...] = jnp.broadcast_to(dec, dec_scr.shape)

        @pl.when(pl.program_id(1) == 0)
        def _():
            state[...] = jnp.zeros_like(state)

        for c in range(n_chunks):
            cc = n_chunks - 1 - c if reverse else c
            st_ref[0, cc] = state[...]
            state[...] = state[...] * dec_scr[cc, 0:1] + kv_scr[cc]
        o = o + _bdot(qb, st_ref[0].astype(BF16), 2, 2)
        o_ref[...] = o.reshape(rows, GLA_DV)

    return pl.pallas_call(
        body, name=name,
        out_shape=(jax.ShapeDtypeStruct((rows_total, GLA_HEADS * GLA_DV), F32),
                   jax.ShapeDtypeStruct((GLA_HEADS, rows_total // GLA_CHUNK, GLA_DV, GLA_DK), F32)),
        grid=(GLA_HEADS, n_blocks),
        in_specs=specs,
        out_specs=(pl.BlockSpec((rows, GLA_DV), lambda h, s: (tix(s), h)),
                   pl.BlockSpec((1, n_chunks, GLA_DV, GLA_DK), lambda h, s: (h, tix(s), 0, 0))),
        scratch_shapes=[pltpu.VMEM((GLA_DV, GLA_DK), F32), pltpu.VMEM((n_chunks, GLA_DV, GLA_DK), F32),
                        pltpu.VMEM((n_chunks, SUBLANES, GLA_DK), F32)],
        compiler_params=_params("parallel", "arbitrary"),
    )(proj, proj, proj, proj, wg_pad, bg)


def gla_bwd(proj, wg_pad, bg, d_o, states, dqkv_in, reverse, name):
    rows_total = proj.shape[0]
    rows = min(GLA_BLOCK, rows_total)
    n_blocks = rows_total // rows
    n_chunks = rows // GLA_CHUNK
    specs, tix = _gla_specs(rows, n_blocks, not reverse)
    d = 1 if reverse else 0
    specs[4] = pl.BlockSpec((1, LANES, GLA_DK), lambda h, s: (d, 0, h))
    specs[5] = pl.BlockSpec((1, 1, GLA_DK), lambda h, s: (d, 0, h))
    add = dqkv_in is not None

    def body(*refs):
        q_ref, k_ref, v_ref, lr_ref, wg_ref, bg_ref, do_ref, st_ref = refs[:8]
        refs = refs[8:]
        if add:
            aq_ref, ak_ref, av_ref = refs[:3]
            refs = refs[3:]
        dq_ref, dk_ref, dv_ref, dz_ref, dstate, g_scr, dec_scr, dsn_scr = refs
        z, q_in, k_in, k_st, dec, (f_q, f_k, f_s) = _gla_prepare(q_ref, k_ref, lr_ref, wg_ref, bg_ref, reverse,
                                                                 n_chunks)
        mask = _gla_mask(reverse)
        vb = v_ref[...].reshape(n_chunks, GLA_CHUNK, GLA_DV).astype(BF16)
        dob = do_ref[...].reshape(n_chunks, GLA_CHUNK, GLA_DV).astype(BF16)
        qb, kb, ksb = q_in.astype(BF16), k_in.astype(BF16), k_st.astype(BF16)
        st = st_ref[0]
        stb = st.astype(BF16)
        pb = jnp.where(mask, _bdot(qb, kb, 2, 2), 0.0).astype(BF16)
        dpb = jnp.where(mask, _bdot(dob, vb, 2, 2), 0.0).astype(BF16)
        d_qin = _bdot(dpb, kb, 2, 1) + _bdot(dob, stb, 2, 1)
        d_kin = _bdot(dpb, qb, 1, 1)
        dv = _bdot(pb, dob, 1, 1)
        g_scr[...] = _bdot(dob, qb, 1, 1)
        dec_scr[...] = jnp.broadcast_to(dec, dec_scr.shape)

        @pl.when(pl.program_id(1) == 0)
        def _():
            dstate[...] = jnp.zeros_like(dstate)

        for c in range(n_chunks):
            cc = c if reverse else n_chunks - 1 - c
            dsn_scr[cc] = dstate[...]
            dstate[...] = dstate[...] * dec_scr[cc, 0:1] + g_scr[cc]
        dsn = dsn_scr[...]
        dsnb = dsn.astype(BF16)
        dv = dv + _bdot(ksb, dsnb, 2, 2)
        d_kst = _bdot(vb, dsnb, 2, 1)
        d_dec = jnp.sum(dsn * st, axis=1, keepdims=True)
        ks_term = d_kst * k_st
        d_btot = d_dec * dec + jnp.sum(ks_term, axis=1, keepdims=True)
        d_b = d_qin * q_in - d_kin * k_in - ks_term
        pos = lax.broadcasted_iota(jnp.int32, d_b.shape, 1)
        edge = 0 if reverse else GLA_CHUNK - 1
        d_b = d_b + jnp.where(pos == edge, d_btot, 0.0)
        dg = _chunk_cumsum(d_b.reshape(rows, GLA_DK), not reverse)
        dz_ref[...] = dg * (1.0 / GLA_NORMALIZER) * _sigmoid(-z)
        dq = (d_qin * f_q).reshape(rows, GLA_DK)
        dk = (d_kin * f_k + d_kst * f_s).reshape(rows, GLA_DK)
        dv = dv.reshape(rows, GLA_DV)
        if add:
            dq_ref[...] = (dq + aq_ref[...]).astype(BF16)
            dk_ref[...] = (dk + ak_ref[...]).astype(BF16)
            dv_ref[...] = (dv + av_ref[...]).astype(BF16)
        else:
            dq_ref[...] = dq
            dk_ref[...] = dk
            dv_ref[...] = dv

    qkv_specs = [pl.BlockSpec((rows, GLA_DK), lambda h, s: (tix(s), h)),
                 pl.BlockSpec((rows, GLA_DK), lambda h, s: (tix(s), h)),
                 pl.BlockSpec((rows, GLA_DV), lambda h, s: (tix(s), h))]
    in_specs = specs + [pl.BlockSpec((rows, GLA_DV), lambda h, s: (tix(s), h)),
                        pl.BlockSpec((1, n_chunks, GLA_DV, GLA_DK), lambda h, s: (h, tix(s), 0, 0))]
    args = [proj, proj, proj, proj, wg_pad, bg, d_o, states]
    out_dtype = F32
    if add:
        in_specs += qkv_specs
        args += list(dqkv_in)
        out_dtype = BF16
    return pl.pallas_call(
        body, name=name,
        out_shape=(jax.ShapeDtypeStruct((rows_total, GLA_HEADS * GLA_DK), out_dtype),
                   jax.ShapeDtypeStruct((rows_total, GLA_HEADS * GLA_DK), out_dtype),
                   jax.ShapeDtypeStruct((rows_total, GLA_HEADS * GLA_DV), out_dtype),
                   jax.ShapeDtypeStruct((rows_total, GLA_HEADS * GLA_DK), F32)),
        grid=(GLA_HEADS, n_blocks),
        in_specs=in_specs,
        out_specs=(pl.BlockSpec((rows, GLA_DK), lambda h, s: (tix(s), h)),
                   pl.BlockSpec((rows, GLA_DK), lambda h, s: (tix(s), h)),
                   pl.BlockSpec((rows, GLA_DV), lambda h, s: (tix(s), h)),
                   pl.BlockSpec((rows, GLA_DK), lambda h, s: (tix(s), h))),
        scratch_shapes=[pltpu.VMEM((GLA_DV, GLA_DK), F32), pltpu.VMEM((n_chunks, GLA_DV, GLA_DK), F32),
                        pltpu.VMEM((n_chunks, SUBLANES, GLA_DK), F32),
                        pltpu.VMEM((n_chunks, GLA_DV, GLA_DK), F32)],
        compiler_params=_params("parallel", "arbitrary"),
    )(*args)


def gla_gate_bwd(proj, dz_f, dz_b, wg_pad, name):
    rows_total = proj.shape[0]
    tm = min(ROW_TILE, rows_total)
    n_key = GLA_HEADS * GLA_DK

    def body(lr_ref, dzf_ref, dzb_ref, wg_ref, dlr_ref, dwg_ref, dbg_ref):
        step = pl.program_id(0)
        lr_t = jnp.transpose(lr_ref[...])
        dzf, dzb = dzf_ref[...], dzb_ref[...]
        dzf16, dzb16 = dzf.astype(BF16), dzb.astype(BF16)
        dlr_ref[...] = (_dot_nt(dzf16, wg_ref[0]) + _dot_nt(dzb16, wg_ref[1])).astype(BF16)
        dwf = _dot(lr_t[0:GLA_RANK].astype(BF16), dzf16)
        dwb = _dot(lr_t[GLA_RANK:2 * GLA_RANK].astype(BF16), dzb16)
        dbg = jnp.concatenate([_colsum(dzf), _colsum(dzb)], axis=0)

        @pl.when(step == 0)
        def _():
            dwg_ref[0] = dwf
            dwg_ref[1] = dwb
            dbg_ref[...] = dbg

        @pl.when(step > 0)
        def _():
            dwg_ref[0] += dwf
            dwg_ref[1] += dwb
            dbg_ref[...] += dbg

    return pl.pallas_call(
        body, name=name,
        out_shape=(jax.ShapeDtypeStruct((rows_total, LANES), BF16), jax.ShapeDtypeStruct((2, GLA_RANK, n_key), F32),
                   jax.ShapeDtypeStruct((2, n_key), F32)),
        grid=(rows_total // tm,),
        in_specs=[pl.BlockSpec((tm, LANES), lambda i: (i, LR_COL // LANES)),
                  pl.BlockSpec((tm, n_key), lambda i: (i, 0)), pl.BlockSpec((tm, n_key), lambda i: (i, 0)),
                  _full((2, LANES, n_key))],
        out_specs=(pl.BlockSpec((tm, LANES), lambda i: (i, 0)), _full((2, GLA_RANK, n_key)), _full((2, n_key))),
        compiler_params=_params("arbitrary"),
    )(proj, dz_f, dz_b, wg_pad)


def _head_norm(o, gain):
    outs, hats, rstds = [], [], []
    for h in range(GLA_HEADS):
        oh = o[:, h * GLA_DV:(h + 1) * GLA_DV]
        rstd = lax.rsqrt(jnp.mean(oh * oh, axis=-1, keepdims=True) + NORM_EPS)
        hat = oh * rstd
        outs.append(hat * gain)
        hats.append(hat)
        rstds.append(rstd)
    return outs, hats, rstds


def odd_out_fwd(o_f, o_b, proj, head_gain, w_out, gain, x1, target, name):
    rows, d = x1.shape
    tm = min(ROW_TILE, rows)
    r_block = (2 * GLA_HEADS * GLA_DK + GLA_HEADS * GLA_DV) // d

    def body(of_ref, ob_ref, r_ref, hg_ref, w_ref, g_ref, x1_ref, tgt_ref, y2_ref, dy_ref, dx2_ref, loss_ref,
             dg_ref):
        step = pl.program_id(0)
        on, _, _ = _head_norm(of_ref[...] + ob_ref[...], hg_ref[...])
        r = r_ref[...]
        y2 = (jnp.concatenate(on, axis=1) * (r * _sigmoid(r))).astype(BF16)
        y2_ref[...] = y2
        y = _dot(y2, w_ref[...])
        gain_v = g_ref[...]
        rstd = lax.rsqrt(jnp.mean(y * y, axis=-1, keepdims=True) + NORM_EPS)
        x2 = x1_ref[...] + y * rstd * gain_v
        diff = x2 - tgt_ref[...]
        loss = 0.5 * jnp.sum(jnp.mean(diff * diff, axis=-1, keepdims=True), axis=0, keepdims=True)
        dx2 = diff * (1.0 / d)
        dx2_ref[...] = dx2
        dy, dg_rows = _rmsnorm_bwd(dx2, y, gain_v)
        dy_ref[...] = dy.astype(BF16)
        _accumulate(loss_ref, jnp.broadcast_to(loss, loss_ref.shape), step)
        _accumulate(dg_ref, _colsum(dg_rows), step)

    row = lambda n, col=0: pl.BlockSpec((tm, n), lambda i: (i, col))
    return pl.pallas_call(
        body, name=name,
        out_shape=(jax.ShapeDtypeStruct((rows, d), BF16), jax.ShapeDtypeStruct((rows, d), BF16),
                   jax.ShapeDtypeStruct((rows, d), F32), jax.ShapeDtypeStruct((SUBLANES, LANES), F32),
                   jax.ShapeDtypeStruct((1, d), F32)),
        grid=(rows // tm,),
        in_specs=[row(d), row(d), row(d, r_block), _full((1, GLA_DV)), _full((d, d)), _full((1, d)), row(d), row(d)],
        out_specs=(row(d), row(d), row(d), _full((SUBLANES, LANES)), _full((1, d))),
        compiler_params=_params("arbitrary"),
    )(o_f, o_b, proj, head_gain, w_out, gain, x1, target)


def odd_out_bwd(dy, w_out, o_f, o_b, proj, head_gain, name):
    rows, d = dy.shape
    tm = min(ROW_TILE, rows)
    r_block = (2 * GLA_HEADS * GLA_DK + GLA_HEADS * GLA_DV) // d

    def body(dy_ref, w_ref, of_ref, ob_ref, r_ref, hg_ref, dr_ref, do_ref, dhg_ref):
        dy2 = _dot_nt(dy_ref[...], w_ref[...])
        hg = hg_ref[...]
        on, hats, rstds = _head_norm(of_ref[...] + ob_ref[...], hg)
        r = r_ref[...]
        sr = _sigmoid(r)
        dr_ref[...] = (dy2 * jnp.concatenate(on, axis=1) * (sr * (1.0 + r * (1.0 - sr)))).astype(BF16)
        d_on = dy2 * (r * sr)
        d_os, dhg = [], None
        for h in range(GLA_HEADS):
            dn = d_on[:, h * GLA_DV:(h + 1) * GLA_DV]
            part = _colsum(dn * hats[h])
            dhg = part if dhg is None else dhg + part
            dng = dn * hg
            d_os.append(rstds[h] * (dng - hats[h] * jnp.mean(dng * hats[h], axis=-1, keepdims=True)))
        do_ref[...] = jnp.concatenate(d_os, axis=1)
        _accumulate(dhg_ref, dhg, pl.program_id(0))

    row = lambda n, col=0: pl.BlockSpec((tm, n), lambda i: (i, col))
    return pl.pallas_call(
        body, name=name,
        out_shape=(jax.ShapeDtypeStruct((rows, d), BF16), jax.ShapeDtypeStruct((rows, d), F32),
                   jax.ShapeDtypeStruct((1, GLA_DV), F32)),
        grid=(rows // tm,),
        in_specs=[row(d), _full((d, d)), row(d), row(d), row(d, r_block), _full((1, GLA_DV))],
        out_specs=(row(d), row(d), _full((1, GLA_DV))),
        compiler_params=_params("arbitrary"),
    )(dy, w_out, o_f, o_b, proj, head_gain)


def local_step(x, target, w, reduce_first=None, reduce_second=None):
    g = {}
    proj_e, h0 = norm_matmul(x, w["even_norm_pre"], w["even_w_in"], "even_in_proj")
    h_dir = [rglru_fwd(proj_e, w["rg_conv_w"], w["rg_conv_b"], w["rg_gate_w"][d], w["rg_gate_b"][d],
                       w["rg_lambda"][d], d == 1, "rglru_fwd_%d" % d) for d in range(2)]
    ycat = even_mix_fwd(proj_e, h_dir[0], h_dir[1], w["sc_conv_w"], "even_mix_fwd")
    x1, y_e = even_out_fwd(ycat, w["even_w_out"], w["even_norm_post"], x, "even_out_fwd")
    proj_o, h1 = norm_matmul(x1, w["odd_norm_pre"], w["odd_w_in"], "odd_in_proj")
    o_dir, st_dir = [], []
    for d in range(2):
        o, st = gla_fwd(proj_o, w["gla_wg_pad"], w["gla_b_gate"], d == 1, "gla_fwd_%d" % d)
        o_dir.append(o)
        st_dir.append(st)
    y2, dy_o, dx2, loss, g["odd_norm_post"] = odd_out_fwd(
        o_dir[0], o_dir[1], proj_o, w["gla_norm_g"], w["odd_w_out"], w["odd_norm_post"], x1, target, "odd_out_fwd")
    g["odd_w_out"] = matmul_dw(y2, dy_o, D_MODEL, "odd_w_out_grad")[0]
    dr, d_o, g["gla_norm_g"] = odd_out_bwd(dy_o, w["odd_w_out"], o_dir[0], o_dir[1], proj_o, w["gla_norm_g"],
                                           "odd_out_bwd")
    dq, dk, dv, dz_f = gla_bwd(proj_o, w["gla_wg_pad"], w["gla_b_gate"], d_o, st_dir[0], None, False, "gla_bwd_0")
    dq, dk, dv, dz_b = gla_bwd(proj_o, w["gla_wg_pad"], w["gla_b_gate"], d_o, st_dir[1], (dq, dk, dv), True,
                               "gla_bwd_1")
    dlr, g["gla_w_gate_lr"], g["gla_b_gate"] = gla_gate_bwd(proj_o, dz_f, dz_b, w["gla_wg_pad"], "gla_gate_bwd")
    dproj_o = [dq, dk, dv, dr, dlr]
    g["odd_w_in"] = jnp.concatenate(matmul_dw_pieces(h1, dproj_o, "odd_w_in_grad"), axis=1)[:, :ODD_IN]
    dx1, g["odd_norm_pre"] = inproj_bwd_pieces(dproj_o, w["odd_w_in"], x1, w["odd_norm_pre"], dx2, "odd_in_proj_bwd")
    dy_e, dycat, g["even_norm_post"] = even_out_bwd(dx1, y_e, w["even_norm_post"], w["even_w_out"], "even_out_bwd")
    g["even_w_out"] = matmul_dw(ycat, dy_e, D_MODEL, "even_w_out_grad")[0]
    conv_b = w["rg_conv_b"] if reduce_first is None else w["rg_conv_b"] + reduce_first(g)
    dua, dgw, dgb, dlam = [], [], [], []
    for d in range(2):
        a, b, c, e = rglru_bwd(proj_e, dycat, h_dir[d], w["rg_conv_w"], conv_b, w["rg_gate_w"][d],
                               w["rg_gate_b"][d], w["rg_lambda"][d], d == 1, "rglru_bwd_%d" % d)
        dua.append(a)
        dgw.append(b)
        dgb.append(c)
        dlam.append(e)
    dproj_e, g["rg_conv_w"], g["rg_conv_b"], g["sc_conv_w"] = even_mix_bwd(
        proj_e, dycat, h_dir[0], h_dir[1], dua[0], dua[1], w["rg_conv_w"], w["sc_conv_w"], "even_mix_bwd")
    dgw = jnp.stack(dgw).reshape(2, RG_HEADS, RG_HEAD_DIM, 2, RG_HEAD_DIM)
    g["rg_gate_w"] = jnp.transpose(dgw, (0, 3, 1, 2, 4))
    g["rg_gate_b"] = jnp.stack(dgb).reshape(2, 2, RG_HEADS, RG_HEAD_DIM)
    g["rg_lambda"] = jnp.concatenate(dlam, axis=0)
    g["even_w_in"] = matmul_dw(h0, dproj_e, EVEN_IN // 4, "even_w_in_grad")
    gain = w["even_norm_pre"] if reduce_second is None else w["even_norm_pre"] + reduce_second(g)
    grad_x, g["even_norm_pre"] = inproj_bwd(dproj_e, w["even_w_in"], x, gain, dx1, "even_in_proj_bwd")
    return loss, grad_x, g


def _prepare_weights(full):
    w = {}
    for name in ("even_norm_pre", "even_norm_post", "rg_conv_b", "odd_norm_pre", "odd_norm_post", "gla_norm_g"):
        w[name] = full[name].reshape(1, -1)
    w["rg_conv_w"] = full["rg_conv_w"]
    w["sc_conv_w"] = full["sc_conv_w"]
    w["even_w_in"] = full["even_w_in"].astype(BF16)
    if w["even_w_in"].ndim == 2:
        w["even_w_in"] = jnp.transpose(w["even_w_in"].reshape(D_MODEL, 4, EVEN_IN // 4), (1, 0, 2))
    w["even_w_out"] = full["even_w_out"].astype(BF16)
    gw = jnp.transpose(full["rg_gate_w"].astype(BF16), (0, 2, 3, 1, 4))
    w["rg_gate_w"] = gw.reshape(2, RG_HEADS, RG_HEAD_DIM, 2 * RG_HEAD_DIM)
    w["rg_gate_b"] = full["rg_gate_b"].reshape(2, 2, D_MODEL)
    w["rg_lambda"] = full["rg_lambda"].reshape(2, 1, D_MODEL)
    w_in = jnp.pad(full["odd_w_in"].astype(BF16), ((0, 0), (0, ODD_IN_PAD - ODD_IN)))
    w["odd_w_in"] = w_in.reshape(1, D_MODEL, ODD_IN_PAD)
    w["odd_w_out"] = full["odd_w_out"].astype(BF16)
    wg = full["gla_w_gate_lr"].astype(BF16)
    w["gla_wg_pad"] = jnp.stack([jnp.pad(wg[d], ((d * GLA_RANK, LANES - (d + 1) * GLA_RANK), (0, 0)))
                                 for d in range(2)])
    w["gla_b_gate"] = full["gla_b_gate"].reshape(2, 1, GLA_HEADS * GLA_DK)
    return w


SHARDED_SMALL = (("rg_conv_w", (4, 256)), ("rg_lambda", (2, 256)), ("sc_conv_w", (3, 256)),
                 ("odd_norm_pre", (256,)), ("odd_norm_post", (256,)), ("gla_w_gate_lr", (2, 16, 128)),
                 ("gla_b_gate", (2, 128)), ("gla_norm_g", (64,)))
SHARDED_ROWS = 64
REPLICATED = (("even_norm_post", (1024,)), ("rg_conv_b", (1024,)),
              ("rg_gate_b", (2, 2, 8, 128)), ("rg_gate_w", (2, 2, 8, 128, 128)))
LAST_REPLICATED = (("even_norm_pre", (1024,)),)
LAST_ROWS = 8
REPLICATED_ROWS = 4160
REP_PART = REPLICATED_ROWS // 8
HALF_SHARDED = SHARDED_ROWS // 2
PACK_HALF = HALF_SHARDED + REP_PART


def _seg_rows(shape):
    n = 1
    for s in shape:
        n *= s
    return -(-n // LANES)


def _pack(arrays, spec, total_rows, lead=()):
    parts = []
    for name, shape in spec:
        flat = arrays[name].reshape(lead + (-1,))
        pad = _seg_rows(shape) * LANES - flat.shape[-1]
        if pad:
            flat = jnp.pad(flat, [(0, 0)] * len(lead) + [(0, pad)])
        parts.append(flat.reshape(lead + (-1, LANES)))
    rows = jnp.concatenate(parts, axis=len(lead))
    pad = total_rows - rows.shape[len(lead)]
    return jnp.pad(rows, [(0, 0)] * len(lead) + [(0, pad), (0, 0)])


def _unpack(rows, spec, lead=()):
    out, at = {}, 0
    for name, shape in spec:
        n = 1
        for s in shape:
            n *= s
        k = _seg_rows(shape)
        seg = lax.slice_in_dim(rows, at, at + k, axis=len(lead)).reshape(lead + (-1,))
        out[name] = lax.slice_in_dim(seg, 0, n, axis=len(lead)).reshape(lead + shape)
        at += k
    return out


def _split_owners(arr):
    a = arr.reshape(arr.shape[:-1] + (4, arr.shape[-1] // 4))
    return jnp.moveaxis(a, -2, 0)


def _merge_owners(arr):
    a = jnp.moveaxis(arr, 0, -2)
    return a.reshape(a.shape[:-2] + (-1,))


HBM_SPEC = pl.BlockSpec(memory_space=pltpu.HBM)


def _position():
    x, y, c = lax.axis_index("x"), lax.axis_index("y"), lax.axis_index("c")
    chips = [(1 - x, y), (x, 1 - y), (1 - x, 1 - y)]
    return x, y, c, chips


def _remote(src, dst, send_sem, recv_sem, device):
    return pltpu.make_async_remote_copy(src_ref=src, dst_ref=dst, send_sem=send_sem, recv_sem=recv_sem,
                                        device_id=device, device_id_type=MESH)


def gather_weights(halved, whole):
    n_h, n_w = len(halved), len(whole)
    n = n_h + n_w

    def body(*refs):
        ins, outs = refs[:n], refs[n:2 * n]
        send_ici, recv_ici, send_fwd, recv_fwd = refs[2 * n:]
        x, y, c, chips = _position()
        me = 2 * x + y
        sibling = (x, y, 1 - c)
        sends = []
        for a in range(n):
            for k, chip in enumerate(chips):
                src = ins[a].at[c] if a < n_h else ins[a]
                dst = outs[a].at[me, c] if a < n_h else outs[a].at[me]
                cp = _remote(src, dst, send_ici.at[3 * a + k], recv_ici.at[3 * a + k], (chip[0], chip[1], c))
                cp.start()
                sends.append(cp)
        for a in range(n):
            for k, chip in enumerate(chips):
                q = 2 * chip[0] + chip[1]
                landed = outs[a].at[q, c] if a < n_h else outs[a].at[q]
                _remote(landed, landed, send_ici.at[3 * a + k], recv_ici.at[3 * a + k], sibling).wait_recv()
                if a < n_h:
                    cp = _remote(landed, landed, send_fwd.at[3 * a + k], recv_fwd.at[3 * a + k], sibling)
                    cp.start()
                    sends.append(cp)
        for a in range(n_h):
            for k, chip in enumerate(chips):
                q = 2 * chip[0] + chip[1]
                passed = outs[a].at[q, 1 - c]
                _remote(passed, passed, send_fwd.at[3 * a + k], recv_fwd.at[3 * a + k], sibling).wait_recv()
        for cp in sends:
            cp.wait_send()

    arrays = list(halved) + list(whole)
    out_shape = [jax.ShapeDtypeStruct((4,) + a.shape, a.dtype) for a in arrays]
    outs = pl.pallas_call(
        body, name="gather_weights",
        out_shape=out_shape,
        in_specs=[HBM_SPEC] * n, out_specs=[HBM_SPEC] * n,
        scratch_shapes=[pltpu.SemaphoreType.DMA((3 * n,)), pltpu.SemaphoreType.DMA((3 * n,)),
                        pltpu.SemaphoreType.DMA((3 * n_h,)), pltpu.SemaphoreType.DMA((3 * n_h,))],
    )(*arrays)
    return outs[:n_h], outs[n_h:]


def place_own(full, own, chip, name):
    _, _, r, cols = full.shape
    tr = _row_tile(r, cols)

    def body(p_ref, own_ref, full_ref, o_ref):
        o_ref[0] = own_ref[...]

    return pl.pallas_call(
        body, name=name,
        out_shape=jax.ShapeDtypeStruct(full.shape, full.dtype),
        grid_spec=pltpu.PrefetchScalarGridSpec(
            num_scalar_prefetch=1, grid=(2, r // tr),
            in_specs=[pl.BlockSpec((1, tr, cols), lambda h, i, p_ref: (h, i, 0)), pl.BlockSpec(memory_space=pl.ANY)],
            out_specs=pl.BlockSpec((1, 1, tr, cols), lambda h, i, p_ref: (p_ref[0], h, i, 0))),
        input_output_aliases={2: 0},
        compiler_params=_params("parallel", "parallel"),
    )(chip, own, full)


def exchange_with_sibling(arrays, name):
    n = len(arrays)

    def body(*refs):
        ins, outs = refs[:n], refs[n:2 * n]
        send_sems, recv_sems = refs[2 * n:]
        x, y, c, _ = _position()
        copies = []
        for a in range(n):
            cp = _remote(ins[a].at[:, 1 - c], outs[a], send_sems.at[a], recv_sems.at[a], (x, y, 1 - c))
            cp.start()
            copies.append(cp)
        for cp in copies:
            cp.wait()

    return pl.pallas_call(
        body, name=name,
        out_shape=[jax.ShapeDtypeStruct((a.shape[0],) + a.shape[2:], a.dtype) for a in arrays],
        in_specs=[HBM_SPEC] * n, out_specs=[HBM_SPEC] * n,
        scratch_shapes=[pltpu.SemaphoreType.DMA((n,)), pltpu.SemaphoreType.DMA((n,))],
    )(*arrays)


SEM_SPEC = pl.BlockSpec(memory_space=pltpu.SEMAPHORE)
SIDE_EFFECT = pltpu.SideEffectType.DATAFLOW_SIDE_EFFECTING


def _chip_copies(ins, lands, send_sems, recv_sems):
    x, y, c, chips = _position()
    copies = []
    for a in range(len(ins)):
        for k, chip in enumerate(chips):
            q = 2 * chip[0] + chip[1]
            copies.append(_remote(ins[a].at[q], lands[a].at[k], send_sems.at[3 * a + k], recv_sems.at[3 * a + k],
                                  (chip[0], chip[1], c)))
    return copies


def exchange_with_chips_start(arrays, name):
    n = len(arrays)
    lands = [lax.empty((3,) + a.shape[1:], a.dtype) for a in arrays]

    def body(*refs):
        ins, lz, send_sems, recv_sems, token = refs[:n], refs[n:2 * n], refs[2 * n], refs[2 * n + 1], refs[-1]
        for cp in _chip_copies(ins, lz, send_sems, recv_sems):
            cp.start()
        token[...] = jnp.zeros_like(token)

    operands = [pltpu.with_memory_space_constraint(a, pltpu.HBM) for a in list(arrays) + lands]
    return pl.pallas_call(
        body, name=name,
        out_shape=(pltpu.SemaphoreType.DMA((3 * n,)), pltpu.SemaphoreType.DMA((3 * n,)))
        + tuple(pltpu.HBM(a.shape, a.dtype) for a in operands) + (jax.ShapeDtypeStruct((SUBLANES, LANES), F32),),
        in_specs=[HBM_SPEC] * (2 * n),
        out_specs=(SEM_SPEC, SEM_SPEC) + (HBM_SPEC,) * (2 * n) + (pl.BlockSpec(memory_space=pltpu.VMEM),),
        input_output_aliases={i: 2 + i for i in range(2 * n)},
        compiler_params=pltpu.CompilerParams(has_side_effects=SIDE_EFFECT),
    )(*operands)


def exchange_with_chips_wait(started, after, name):
    send_sems, recv_sems = started[0], started[1]
    operands = list(started[2:-1])
    n = len(operands) // 2

    def body(*refs):
        ins, lz, send_ref, recv_ref = refs[:n], refs[n:2 * n], refs[2 * n], refs[2 * n + 1]
        for cp in _chip_copies(ins, lz, send_ref, recv_ref):
            cp.wait_send()
            cp.wait_recv()

    outs = pl.pallas_call(
        body, name=name,
        out_shape=tuple(pltpu.HBM(a.shape, a.dtype) for a in operands),
        in_specs=[HBM_SPEC] * (2 * n) + [SEM_SPEC, SEM_SPEC, pl.BlockSpec(memory_space=pl.ANY)],
        out_specs=(HBM_SPEC,) * (2 * n),
        input_output_aliases={i: i for i in range(2 * n)},
        compiler_params=pltpu.CompilerParams(has_side_effects=SIDE_EFFECT),
    )(*operands, send_sems, recv_sems, after)
    return outs[:n], outs[n:]


def share_totals(totals, pack_total, last_part):
    arrays = list(totals) + [pack_total]
    n = len(arrays)

    def body(*refs):
        ins, last, outs, rep, last_all = refs[:n], refs[n], refs[n + 1:2 * n + 1], refs[2 * n + 1], refs[2 * n + 2]
        send_sems, recv_sems, rep_send, rep_recv, last_send, last_recv = refs[2 * n + 3:]
        x, y, c, chips = _position()
        sibling = (x, y, 1 - c)
        me = 4 * x + 2 * y + c
        sends = []
        for a in range(n):
            cp = _remote(ins[a], outs[a], send_sems.at[a], recv_sems.at[a], sibling)
            cp.start()
            sends.append(cp)
        mine = ins[n - 1].at[pl.ds(HALF_SHARDED, REP_PART)]
        peers = [sibling]
        for chip in chips:
            peers += [(chip[0], chip[1], c), (chip[0], chip[1], 1 - c)]
        for j, peer in enumerate(peers):
            for src, dst, s_sem, r_sem in ((mine, rep, rep_send, rep_recv), (last, last_all, last_send, last_recv)):
                cp = _remote(src, dst.at[me], s_sem.at[j], r_sem.at[j], peer)
                cp.start()
                sends.append(cp)
        for a in range(n):
            _remote(outs[a], outs[a], send_sems.at[a], recv_sems.at[a], sibling).wait_recv()
        for j, peer in enumerate(peers):
            it = 4 * peer[0] + 2 * peer[1] + peer[2]
            _remote(rep.at[it], rep.at[it], rep_send.at[j], rep_recv.at[j], peer).wait_recv()
            _remote(last_all.at[it], last_all.at[it], last_send.at[j], last_recv.at[j], peer).wait_recv()
        for cp in sends:
            cp.wait_send()

    outs = pl.pallas_call(
        body, name="grad_share_totals",
        out_shape=[jax.ShapeDtypeStruct(a.shape, a.dtype) for a in arrays]
        + [jax.ShapeDtypeStruct((8, REP_PART, LANES), F32), jax.ShapeDtypeStruct((8, LAST_ROWS, LANES), F32)],
        in_specs=[HBM_SPEC] * (n + 1), out_specs=[HBM_SPEC] * (n + 2),
        scratch_shapes=[pltpu.SemaphoreType.DMA((n,)), pltpu.SemaphoreType.DMA((n,))]
        + [pltpu.SemaphoreType.DMA((7,))] * 4,
    )(*arrays, last_part)
    return outs[:n], outs[n], outs[n + 1]


def sum_parts(parts, name):
    def body(p_ref, o_ref):
        total = p_ref[0]
        for k in range(1, parts.shape[0]):
            total = total + p_ref[k]
        o_ref[...] = total

    return pl.pallas_call(body, name=name, out_shape=jax.ShapeDtypeStruct(parts.shape[1:], parts.dtype))(parts)


TILE_BYTES = 1 << 20


def _row_tile(rows, cols):
    best = None
    for t in range(SUBLANES, rows + 1, SUBLANES):
        if rows % t == 0 and t * cols * 4 <= TILE_BYTES:
            best = t
    return best if best is not None else rows


def add_sibling(mine, received, core, out_dtype, name):
    _, _, r, cols = mine.shape
    tr = _row_tile(r, cols)

    def body(c_ref, a_ref, b_ref, o_ref):
        o_ref[...] = (a_ref[0] + b_ref[...]).astype(out_dtype)

    return pl.pallas_call(
        body, name=name,
        out_shape=jax.ShapeDtypeStruct((4, r, cols), out_dtype),
        grid_spec=pltpu.PrefetchScalarGridSpec(
            num_scalar_prefetch=1, grid=(4, r // tr),
            in_specs=[pl.BlockSpec((1, 1, tr, cols), lambda o, i, c_ref: (o, c_ref[0], i, 0)),
                      pl.BlockSpec((1, tr, cols), lambda o, i, c_ref: (o, i, 0))],
            out_specs=pl.BlockSpec((1, tr, cols), lambda o, i, c_ref: (o, i, 0))),
        compiler_params=_params("parallel", "parallel"),
    )(core, mine, received)


def add_chips(own, received, chip, name):
    _, r, cols = own.shape
    tr = _row_tile(r, cols)

    def body(p_ref, a_ref, b0, b1, b2, o_ref):
        o_ref[...] = ((a_ref[0].astype(F32) + b0[0].astype(F32)) + b1[0].astype(F32)) + b2[0].astype(F32)

    rb = lambda k: pl.BlockSpec((1, tr, cols), lambda i, p_ref: (k, i, 0))
    return pl.pallas_call(
        body, name=name,
        out_shape=jax.ShapeDtypeStruct((r, cols), F32),
        grid_spec=pltpu.PrefetchScalarGridSpec(
            num_scalar_prefetch=1, grid=(r // tr,),
            in_specs=[pl.BlockSpec((1, tr, cols), lambda i, p_ref: (p_ref[0], i, 0)), rb(0), rb(1), rb(2)],
            out_specs=pl.BlockSpec((tr, cols), lambda i, p_ref: (i, 0))),
        compiler_params=_params("parallel"),
    )(chip, own, received, received, received)


def _adamw_update(gv, w_ref, m_ref, v_ref, d_ref, nm_ref, nv_ref):
    nm = ADAM_B1 * m_ref[...] + (1.0 - ADAM_B1) * gv
    nv = ADAM_B2 * v_ref[...] + (1.0 - ADAM_B2) * (gv * gv)
    nm_ref[...] = nm
    nv_ref[...] = nv
    m_hat = nm / (1.0 - ADAM_B1 ** ADAM_STEP)
    v_hat = nv / (1.0 - ADAM_B2 ** ADAM_STEP)
    d_ref[...] = -ADAM_LR * (m_hat / (jnp.sqrt(v_hat) + ADAM_EPS) + ADAM_WD * w_ref[...])


def adamw_halves(w, own, received, m, v, core, name):
    rows, cols = w.shape
    r = rows // 2
    tr = _row_tile(r, cols)
    nr = r // tr

    def body(c_ref, w_ref, own_ref, rec_ref, m_ref, v_ref, g_ref, d_ref, nm_ref, nv_ref):
        gv = jnp.where(pl.program_id(0) == c_ref[0], own_ref[...], rec_ref[...])
        g_ref[...] = gv
        _adamw_update(gv, w_ref, m_ref, v_ref, d_ref, nm_ref, nv_ref)

    whole = pl.BlockSpec((tr, cols), lambda h, i, c_ref: (h * nr + i, 0))
    half = pl.BlockSpec((tr, cols), lambda h, i, c_ref: (i, 0))
    return pl.pallas_call(
        body, name=name,
        out_shape=(jax.ShapeDtypeStruct((rows, cols), F32),) * 4,
        grid_spec=pltpu.PrefetchScalarGridSpec(
            num_scalar_prefetch=1, grid=(2, nr),
            in_specs=[whole, half, half, whole, whole], out_specs=(whole,) * 4),
        compiler_params=_params("parallel", "parallel"),
    )(core, w, own, received, m, v)


def adamw(w, g, m, v, name):
    r, cols = w.shape
    tr = _row_tile(r, cols)

    def body(w_ref, g_ref, m_ref, v_ref, d_ref, nm_ref, nv_ref):
        _adamw_update(g_ref[...], w_ref, m_ref, v_ref, d_ref, nm_ref, nv_ref)

    blk = pl.BlockSpec((tr, cols), lambda i: (i, 0))
    return pl.pallas_call(
        body, name=name,
        out_shape=(jax.ShapeDtypeStruct((r, cols), F32),) * 3,
        grid=(r // tr,),
        in_specs=[blk] * 4, out_specs=(blk,) * 3,
        compiler_params=_params("parallel"),
    )(w, g, m, v)


WEIGHTS = ("even_norm_pre", "even_norm_post", "even_w_in", "rg_conv_w", "rg_conv_b", "rg_gate_w", "rg_gate_b",
           "rg_lambda", "sc_conv_w", "even_w_out", "odd_norm_pre", "odd_norm_post", "odd_w_in", "gla_w_gate_lr",
           "gla_b_gate", "gla_norm_g", "odd_w_out")
BIG = ("even_w_in", "even_w_out", "odd_w_in", "odd_w_out")


def _halves(a):
    return a.reshape((2, a.shape[0] // 2) + a.shape[1:])


def kernel(x, even_norm_pre, even_norm_post, even_w_in, rg_conv_w, rg_conv_b, rg_gate_w, rg_gate_b, rg_lambda, sc_conv_w, even_w_out, odd_norm_pre, odd_norm_post, odd_w_in, gla_w_gate_lr, gla_b_gate, gla_norm_g, odd_w_out, loss_target, m_even_norm_pre, m_even_norm_post, m_even_w_in, m_rg_conv_w, m_rg_conv_b, m_rg_gate_w, m_rg_gate_b, m_rg_lambda, m_sc_conv_w, m_even_w_out, m_odd_norm_pre, m_odd_norm_post, m_odd_w_in, m_gla_w_gate_lr, m_gla_b_gate, m_gla_norm_g, m_odd_w_out, v_even_norm_pre, v_even_norm_post, v_even_w_in, v_rg_conv_w, v_rg_conv_b, v_rg_gate_w, v_rg_gate_b, v_rg_lambda, v_sc_conv_w, v_even_w_out, v_odd_norm_pre, v_odd_norm_post, v_odd_w_in, v_gla_w_gate_lr, v_gla_b_gate, v_gla_norm_g, v_odd_w_out):
    given = dict(locals())
    shard = {n: given[n][0] for n in WEIGHTS}
    m_in = {n: given["m_" + n][0] for n in WEIGHTS}
    v_in = {n: given["v_" + n][0] for n in WEIGHTS}
    mx, my, mc = lax.axis_index("x"), lax.axis_index("y"), lax.axis_index("c")
    core = jnp.reshape(mc, (1,)).astype(jnp.int32)
    chip = jnp.reshape(2 * mx + my, (1,)).astype(jnp.int32)

    small_shard = _pack(shard, SHARDED_SMALL, SHARDED_ROWS)
    big_own = [_halves(shard[n].astype(BF16)) for n in BIG]
    big_full, (small_full,) = gather_weights(big_own, [small_shard])
    big_full = [place_own(a, b, chip, "place_" + n) for a, b, n in zip(big_full, big_own, BIG)]
    small_full = lax.dynamic_update_slice(small_full, small_shard[None], (chip[0], 0, 0))
    full = {n: shard[n] for n, _ in REPLICATED + LAST_REPLICATED}
    full.update({n: _merge_owners(a) for n, a in _unpack(small_full, SHARDED_SMALL, lead=(4,)).items()})
    full["even_w_in"] = big_full[0].reshape(4, D_MODEL, EVEN_IN // 4)
    full["even_w_out"] = big_full[1].reshape(2 * D_MODEL, D_MODEL)
    full["odd_w_in"] = jnp.transpose(big_full[2].reshape(4, D_MODEL, ODD_IN // 4), (1, 0, 2)).reshape(D_MODEL, ODD_IN)
    full["odd_w_out"] = big_full[3].reshape(D_MODEL, D_MODEL)

    pending = {}

    def slab(a):
        return a.reshape((4, 2, a.shape[1] // 2) + a.shape[2:])

    def begin(tag, slabs, dtypes):
        got = exchange_with_sibling(slabs, "grad_sibling_" + tag)
        sums = [add_sibling(a, b, core, dt, "grad_add_sibling_%s%d" % (tag, i))
                for i, (a, b, dt) in enumerate(zip(slabs, got, dtypes))]
        pending[tag] = exchange_with_chips_start(sums, "grad_chips_start_" + tag)
        return pending[tag][-1][0, 0]

    def finish(tag, after):
        sums, got = exchange_with_chips_wait(pending[tag], after, "grad_chips_wait_" + tag)
        return [add_chips(a, b, chip, "grad_add_chips_%s%d" % (tag, i)) for i, (a, b) in enumerate(zip(sums, got))]

    def reduce_first(g):
        return begin("a", [slab(jnp.transpose(g["odd_w_in"].reshape(D_MODEL, 4, ODD_IN // 4), (1, 0, 2))),
                           slab(g["odd_w_out"].reshape(4, D_MODEL // 4, D_MODEL)),
                           slab(g["even_w_out"].reshape(4, D_MODEL // 2, D_MODEL))], [BF16] * 3)

    def reduce_second(g):
        pending["totals_a"] = finish("a", g["even_w_in"])
        rep_rows = _pack(g, REPLICATED, REPLICATED_ROWS).reshape(4, 2, REP_PART, LANES)
        sh_rows = _pack({n: _split_owners(g[n]) for n, _ in SHARDED_SMALL}, SHARDED_SMALL, SHARDED_ROWS, lead=(4,))
        pack = jnp.concatenate([sh_rows.reshape(4, 2, HALF_SHARDED, LANES), rep_rows], axis=2)
        return begin("b", [slab(g["even_w_in"]), pack], [BF16, F32])

    loss, grad_x, g = local_step(x[0], loss_target[0], _prepare_weights(full), reduce_first, reduce_second)
    loss = lax.psum(loss[0, 0], ("x", "y", "c"))
    odd_w_in_t, odd_w_out_t, even_w_out_t = pending["totals_a"]
    even_w_in_t, pack_t = finish("b", grad_x)
    totals = [even_w_in_t, even_w_out_t, odd_w_in_t, odd_w_out_t]
    last_part = _pack(g, LAST_REPLICATED, LAST_ROWS)
    from_core, rep_all, last_all = share_totals(totals, pack_t, last_part)
    me = 2 * chip[0] + core[0]
    mine, theirs = pack_t[:HALF_SHARDED], from_core[4][:HALF_SHARDED]
    sh_total = jnp.where(mc == 0, jnp.concatenate([mine, theirs]), jnp.concatenate([theirs, mine]))
    rep_all = lax.dynamic_update_slice(rep_all, pack_t[None, HALF_SHARDED:], (me, 0, 0))
    rep_total = rep_all.reshape(REPLICATED_ROWS, LANES)
    last_total = sum_parts(lax.dynamic_update_slice(last_all, last_part[None], (me, 0, 0)), "grad_sum_last")
    grads = {}
    grads.update(_unpack(sh_total, SHARDED_SMALL))
    grads.update(_unpack(rep_total, REPLICATED))
    grads.update(_unpack(last_total, LAST_REPLICATED))

    delta, new_m, new_v = {}, {}, {}
    for i, n in enumerate(BIG):
        grads[n], delta[n], new_m[n], new_v[n] = adamw_halves(shard[n], totals[i], from_core[i], m_in[n], v_in[n],
                                                              core, "adamw_" + n)
    small = ((SHARDED_SMALL, SHARDED_ROWS), (REPLICATED, REPLICATED_ROWS), (LAST_REPLICATED, LAST_ROWS))
    packed = [jnp.concatenate([_pack(src, spec, rows) for spec, rows in small]) for src in (shard, m_in, v_in)]
    small_g = jnp.concatenate([sh_total, rep_total, last_total], axis=0)
    outs = adamw(packed[0], small_g, packed[1], packed[2], "adamw_small")
    for dst, packed_rows in zip((delta, new_m, new_v), outs):
        at = 0
        for spec, rows in small:
            dst.update(_unpack(packed_rows[at:at + rows], spec))
            at += rows
    result = [loss, grad_x[None]]
    for group in (grads, delta, new_m, new_v):
        result += [group[n].reshape(given[n].shape) for n in WEIGHTS]
    return tuple(result)
```

```python
import functools

import jax
import jax.numpy as jnp
from jax import lax
from jax.experimental import pallas as pl
from jax.experimental.pallas import tpu as pltpu

F32 = jnp.float32
BF16 = jnp.bfloat16
MESH = pl.DeviceIdType.MESH

D_MODEL = 1024
NORM_EPS = 1e-6
RG_HEADS = 8
RG_HEAD_DIM = 128
RG_C = 8.0
EVEN_IN = 6144
ODD_IN = 3104
ODD_IN_PAD = 3200
GLA_HEADS = 4
GLA_DK = 128
GLA_DV = 256
GLA_RANK = 16
GLA_NORMALIZER = 16.0
GLA_CHUNK = 64
LR_COL = 3072

ADAM_LR = 0.001
ADAM_B1 = 0.9
ADAM_B2 = 0.999
ADAM_EPS = 1e-08
ADAM_WD = 0.01
ADAM_STEP = 10

SUBLANES = 8
LANES = 128
VMEM_LIMIT = 56 * 2 ** 20

ROW_TILE = 512
SCAN_TILE = 256
GLA_BLOCK = 1024
MIX_TILE = 128


def _params(*sem):
    return pltpu.CompilerParams(dimension_semantics=sem, vmem_limit_bytes=VMEM_LIMIT)


def _full(shape):
    n = len(shape)
    return pl.BlockSpec(shape, lambda *_: (0,) * n)


def _sigmoid(x):
    return 1.0 / (1.0 + jnp.exp(-x))


def _softplus(x):
    return jnp.maximum(x, 0.0) + jnp.log(1.0 + jnp.exp(-jnp.abs(x)))


def _one_minus_exp(x):
    series = -x * (1.0 + x * (1.0 / 2.0) * (1.0 + x * (1.0 / 3.0) * (1.0 + x * (1.0 / 4.0) * (
        1.0 + x * (1.0 / 5.0) * (1.0 + x * (1.0 / 6.0))))))
    return jnp.where(x > -0.25, series, 1.0 - jnp.exp(x))


def _dot(a, b):
    return jnp.dot(a, b, preferred_element_type=F32)


def _dot_nt(a, b):
    return lax.dot_general(a, b, (((1,), (1,)), ((), ())), preferred_element_type=F32)


def _dot_tn(a, b):
    return lax.dot_general(a, b, (((0,), (0,)), ((), ())), preferred_element_type=F32)


def _bdot(a, b, ca, cb):
    return lax.dot_general(a, b, (((ca,), (cb,)), ((0,), (0,))), preferred_element_type=F32)


def _halo_specs(rows, cols, col_block, n_row_tiles, tix):
    per = rows // SUBLANES
    last = n_row_tiles * per - 1

    def split(args):
        if len(args) == 2:
            return tix(args[1]), col_block + args[0]
        return tix(args[0]), col_block

    def prev(*args):
        t, c = split(args)
        return (jnp.maximum(t * per - 1, 0), c)

    def main(*args):
        return split(args)

    def nxt(*args):
        t, c = split(args)
        return (jnp.minimum((t + 1) * per, last), c)

    return [pl.BlockSpec((SUBLANES, cols), prev), pl.BlockSpec((rows, cols), main),
            pl.BlockSpec((SUBLANES, cols), nxt)]


def _extend(prev_ref, main_ref, next_ref, is_first, is_last):
    p = jnp.where(is_first, 0.0, prev_ref[...])
    n = jnp.where(is_last, 0.0, next_ref[...])
    return jnp.concatenate([p, main_ref[...], n], axis=0)


def _shifted(ext, offset, rows):
    if offset == 0:
        return ext[SUBLANES:SUBLANES + rows]
    n = ext.shape[0]
    return pltpu.roll(ext, (-offset) % n, 0)[SUBLANES:SUBLANES + rows]


def _conv(ext, w, left, rows):
    out = None
    for k in range(w.shape[0]):
        term = _shifted(ext, k - left, rows) * w[k:k + 1]
        out = term if out is None else out + term
    return out


def _conv_transpose(ext, w, left, rows):
    out = None
    for k in range(w.shape[0]):
        term = _shifted(ext, left - k, rows) * w[k:k + 1]
        out = term if out is None else out + term
    return out


def _colsum(x):
    return jnp.sum(x, axis=0, keepdims=True)


def _accumulate(ref, value, step):
    @pl.when(step == 0)
    def _():
        ref[...] = value

    @pl.when(step > 0)
    def _():
        ref[...] += value


PROJ_TILE_BYTES = 7 * 2 ** 20


def _proj_row_tile(rows, width):
    tm = min(ROW_TILE, rows)
    while tm * width * 4 > PROJ_TILE_BYTES and tm % (2 * SUBLANES) == 0:
        tm //= 2
    return tm


def norm_matmul(x, gain, w, name):
    rows, d = x.shape
    n_col_tiles, _, tn = w.shape
    tm = _proj_row_tile(rows, n_col_tiles * tn)

    def body(x_ref, g_ref, w_ref, proj_ref, h_ref):
        xv = x_ref[...]
        rstd = lax.rsqrt(jnp.mean(xv * xv, axis=-1, keepdims=True) + NORM_EPS)
        hv = (xv * rstd * g_ref[...]).astype(BF16)
        h_ref[...] = hv
        for j in range(n_col_tiles):
            proj_ref[:, j * tn:(j + 1) * tn] = _dot(hv, w_ref[j])

    row = lambda cols: pl.BlockSpec((tm, cols), lambda i: (i, 0))
    return pl.pallas_call(
        body, name=name,
        out_shape=(jax.ShapeDtypeStruct((rows, n_col_tiles * tn), F32), jax.ShapeDtypeStruct((rows, d), BF16)),
        grid=(rows // tm,),
        in_specs=[row(d), _full((1, d)), _full(w.shape)],
        out_specs=(row(n_col_tiles * tn), row(d)),
        compiler_params=_params("parallel"),
    )(x, gain, w)


def inproj_bwd(dproj, w, x, gain, dres, name):
    rows, d = x.shape
    n_col_tiles, _, tn = w.shape
    tm = _proj_row_tile(rows, n_col_tiles * tn)

    def body(dp_ref, w_ref, x_ref, g_ref, dres_ref, dx_ref, dg_ref):
        dh = None
        for j in range(n_col_tiles):
            part = _dot_nt(dp_ref[:, j * tn:(j + 1) * tn], w_ref[j])
            dh = part if dh is None else dh + part
        _inproj_finish(dh, x_ref, g_ref, dres_ref, dx_ref, dg_ref, pl.program_id(0))

    row = lambda cols: pl.BlockSpec((tm, cols), lambda i: (i, 0))
    return pl.pallas_call(
        body, name=name,
        out_shape=(jax.ShapeDtypeStruct((rows, d), F32), jax.ShapeDtypeStruct((1, d), F32)),
        grid=(rows // tm,),
        in_specs=[row(n_col_tiles * tn), _full(w.shape), row(d), _full((1, d)), row(d)],
        out_specs=(row(d), _full((1, d))),
        compiler_params=_params("arbitrary"),
    )(dproj, w, x, gain, dres)


def _inproj_finish(dh, x_ref, g_ref, dres_ref, dx_ref, dg_ref, step):
    xv = x_ref[...]
    rstd = lax.rsqrt(jnp.mean(xv * xv, axis=-1, keepdims=True) + NORM_EPS)
    xhat = xv * rstd
    dxn = dh * g_ref[...]
    dx_ref[...] = dres_ref[...] + rstd * (dxn - xhat * jnp.mean(dxn * xhat, axis=-1, keepdims=True))
    _accumulate(dg_ref, _colsum(dh * xhat), step)


def inproj_bwd_pieces(pieces, w, x, gain, dres, name):
    rows, d = x.shape
    tm = min(ROW_TILE, rows)
    n = len(pieces)
    widths = [p.shape[1] for p in pieces]
    starts = [sum(widths[:k]) for k in range(n)]
    assert sum(widths) == w.shape[2]

    def body(*refs):
        w_ref, x_ref, g_ref, dres_ref, dx_ref, dg_ref = refs[n:]
        dh = None
        for k in range(n):
            part = _dot_nt(refs[k][...], w_ref[0, :, starts[k]:starts[k] + widths[k]])
            dh = part if dh is None else dh + part
        _inproj_finish(dh, x_ref, g_ref, dres_ref, dx_ref, dg_ref, pl.program_id(0))

    row = lambda cols: pl.BlockSpec((tm, cols), lambda i: (i, 0))
    return pl.pallas_call(
        body, name=name,
        out_shape=(jax.ShapeDtypeStruct((rows, d), F32), jax.ShapeDtypeStruct((1, d), F32)),
        grid=(rows // tm,),
        in_specs=[row(wd) for wd in widths] + [_full(w.shape), row(d), _full((1, d)), row(d)],
        out_specs=(row(d), _full((1, d))),
        compiler_params=_params("arbitrary"),
    )(*pieces, w, x, gain, dres)


def matmul_dw_pieces(a, pieces, name):
    rows, m = a.shape
    tk = min(ROW_TILE, rows)
    n = len(pieces)

    def body(*refs):
        a_ref, ins, outs = refs[0], refs[1:1 + n], refs[1 + n:]
        av = a_ref[...]
        for k in range(n):
            _accumulate(outs[k], _dot_tn(av, ins[k][...]), pl.program_id(0))

    return pl.pallas_call(
        body, name=name,
        out_shape=[jax.ShapeDtypeStruct((m, p.shape[1]), F32) for p in pieces],
        grid=(rows // tk,),
        in_specs=[pl.BlockSpec((tk, m), lambda k: (k, 0))]
        + [pl.BlockSpec((tk, p.shape[1]), lambda k: (k, 0)) for p in pieces],
        out_specs=[_full((m, p.shape[1])) for p in pieces],
        compiler_params=_params("arbitrary"),
    )(a, *pieces)


def matmul_dw(a, b, bn, name):
    rows, m = a.shape
    n = b.shape[1]
    tk = min(2 * ROW_TILE, rows)
    steps = rows // tk

    def body(a_ref, b_ref, o_ref):
        part = _dot_tn(a_ref[...], b_ref[...])

        @pl.when(pl.program_id(1) == 0)
        def _():
            o_ref[0] = part

        @pl.when(pl.program_id(1) > 0)
        def _():
            o_ref[0] += part

    return pl.pallas_call(
        body, name=name,
        out_shape=jax.ShapeDtypeStruct((n // bn, m, bn), F32),
        grid=(n // bn, steps),
        in_specs=[pl.BlockSpec((tk, m), lambda j, k: (k, 0)), pl.BlockSpec((tk, bn), lambda j, k: (k, j))],
        out_specs=pl.BlockSpec((1, m, bn), lambda j, k: (j, 0, 0)),
        compiler_params=_params("parallel", "arbitrary"),
    )(a, b)


def _scan(a, b, carry, reverse):
    n = a.shape[0]
    pos = lax.broadcasted_iota(jnp.int32, a.shape, 0) % SUBLANES
    s = 1
    while s < SUBLANES:
        if reverse:
            a_s, b_s, valid = pltpu.roll(a, n - s, 0), pltpu.roll(b, n - s, 0), pos < SUBLANES - s
        else:
            a_s, b_s, valid = pltpu.roll(a, s, 0), pltpu.roll(b, s, 0), pos >= s
        b = jnp.where(valid, a * b_s + b, b)
        a = jnp.where(valid, a * a_s, a)
        s *= 2
    blocks = n // SUBLANES
    out = [None] * blocks
    for k in (range(blocks - 1, -1, -1) if reverse else range(blocks)):
        rows = slice(k * SUBLANES, (k + 1) * SUBLANES)
        h = a[rows] * carry + b[rows]
        out[k] = h
        carry = h[0:1] if reverse else h[SUBLANES - 1:SUBLANES]
    return jnp.concatenate(out, axis=0)


def _rg_gates(ua, gw_ref, gb, lam):
    ub = ua.astype(BF16)
    pre_r, pre_i = [], []
    for h in range(RG_HEADS):
        z = _dot(ub[:, h * RG_HEAD_DIM:(h + 1) * RG_HEAD_DIM], gw_ref[h])
        pre_r.append(z[:, :RG_HEAD_DIM])
        pre_i.append(z[:, RG_HEAD_DIM:])
    r = _sigmoid(jnp.concatenate(pre_r, axis=1) + gb[0:1])
    i = _sigmoid(jnp.concatenate(pre_i, axis=1) + gb[1:2])
    sp = _softplus(-lam)
    log_a = -RG_C * r * sp
    a = jnp.exp(log_a)
    mult = jnp.sqrt(_one_minus_exp(2.0 * log_a))
    return r, i, sp, a, mult


def _rg_weight_specs():
    return [_full((4, D_MODEL)), _full((1, D_MODEL)), _full((RG_HEADS, RG_HEAD_DIM, 2 * RG_HEAD_DIM)),
            _full((2, D_MODEL)), _full((1, D_MODEL))]


def rglru_fwd(proj, conv_w, conv_b, gate_w, gate_b, lam, reverse, name):
    rows_total = proj.shape[0]
    rows = min(SCAN_TILE, rows_total)
    n_tiles = rows_total // rows
    tix = (lambda i: n_tiles - 1 - i) if reverse else (lambda i: i)

    def body(xp, xm, xn, cw_ref, cb_ref, gw_ref, gb_ref, lam_ref, h_ref, carry):
        i = pl.program_id(0)
        t = tix(i)
        ext = _extend(xp, xm, xn, t == 0, t == n_tiles - 1)
        ua = _conv(ext, cw_ref[...], 2, rows) + cb_ref[...]
        _, gi, _, a, mult = _rg_gates(ua, gw_ref, gb_ref[...], lam_ref[...])
        b = mult * (gi * ua)

        @pl.when(i == 0)
        def _():
            carry[...] = jnp.zeros_like(carry)

        h = _scan(a, b, carry[0:1], reverse)
        h_ref[...] = h
        edge = h[0:1] if reverse else h[rows - 1:rows]
        carry[...] = jnp.broadcast_to(edge, carry.shape)

    return pl.pallas_call(
        body, name=name,
        out_shape=jax.ShapeDtypeStruct((rows_total, D_MODEL), F32),
        grid=(n_tiles,),
        in_specs=_halo_specs(rows, D_MODEL, 0, n_tiles, tix) + _rg_weight_specs(),
        out_specs=pl.BlockSpec((rows, D_MODEL), lambda i: (tix(i), 0)),
        scratch_shapes=[pltpu.VMEM((SUBLANES, D_MODEL), F32)],
        compiler_params=_params("arbitrary"),
    )(proj, proj, proj, conv_w, conv_b, gate_w, gate_b, lam)


def rglru_bwd(proj, dycat, h_dir, conv_w, conv_b, gate_w, gate_b, lam, reverse, name):
    rows_total = proj.shape[0]
    rows = min(SCAN_TILE, rows_total)
    n_tiles = rows_total // rows
    tix = (lambda i: i) if reverse else (lambda i: n_tiles - 1 - i)
    za_block = 1

    def body(xp, xm, xn, za_ref, dya_ref, hp, hm, hn, cw_ref, cb_ref, gw_ref, gb_ref, lam_ref,
             dua_ref, dgw_ref, dgb_ref, dlam_ref, carry):
        step = pl.program_id(0)
        t = tix(step)
        first, last = t == 0, t == n_tiles - 1
        ext = _extend(xp, xm, xn, first, last)
        ua = _conv(ext, cw_ref[...], 2, rows) + cb_ref[...]
        lam_v = lam_ref[...]
        r, gi, sp, a, mult = _rg_gates(ua, gw_ref, gb_ref[...], lam_v)
        za = za_ref[...]
        dh = dya_ref[...] * (za * _sigmoid(za))

        @pl.when(step == 0)
        def _():
            carry[...] = jnp.zeros_like(carry)

        old = carry[0:1]
        mu = _scan(a, a * dh, old, not reverse)
        row = lax.broadcasted_iota(jnp.int32, mu.shape, 0)
        if reverse:
            mu_next = jnp.where(row == 0, old, pltpu.roll(mu, 1, 0))
            carry[...] = jnp.broadcast_to(mu[rows - 1:rows], carry.shape)
            h_ext = _extend(hp, hm, hn, first, last)
            h_prev = _shifted(h_ext, 1, rows)
        else:
            mu_next = jnp.where(row == rows - 1, old, pltpu.roll(mu, rows - 1, 0))
            carry[...] = jnp.broadcast_to(mu[0:1], carry.shape)
            h_ext = _extend(hp, hm, hn, first, last)
            h_prev = _shifted(h_ext, -1, rows)
        db = dh + mu_next
        da = db * h_prev
        d_mult = db * (gi * ua)
        di = db * (mult * ua)
        dua = db * (mult * gi)
        dlog_a = da * a - d_mult * (a * a) / mult
        dr = dlog_a * (-RG_C * sp)
        dlam = _colsum(dlog_a * (-RG_C * r)) * (-_sigmoid(-lam_v))
        dpr = dr * (r * (1.0 - r))
        dpi = di * (gi * (1.0 - gi))
        dgb = jnp.concatenate([_colsum(dpr), _colsum(dpi)], axis=0)
        ub = ua.astype(BF16)
        dua_heads, dgw_heads = [], []
        for h in range(RG_HEADS):
            cols = slice(h * RG_HEAD_DIM, (h + 1) * RG_HEAD_DIM)
            dz = jnp.concatenate([dpr[:, cols], dpi[:, cols]], axis=1).astype(BF16)
            dgw_heads.append(_dot_tn(ub[:, cols], dz))
            dua_heads.append(_dot_nt(dz, gw_ref[h]))
        dua_ref[...] = dua + jnp.concatenate(dua_heads, axis=1)

        @pl.when(step == 0)
        def _():
            for h in range(RG_HEADS):
                dgw_ref[h] = dgw_heads[h]
            dgb_ref[...] = dgb
            dlam_ref[...] = dlam

        @pl.when(step > 0)
        def _():
            for h in range(RG_HEADS):
                dgw_ref[h] += dgw_heads[h]
            dgb_ref[...] += dgb
            dlam_ref[...] += dlam

    row_spec = lambda col: pl.BlockSpec((rows, D_MODEL), lambda i: (tix(i), col))
    return pl.pallas_call(
        body, name=name,
        out_shape=(jax.ShapeDtypeStruct((rows_total, D_MODEL), F32),
                   jax.ShapeDtypeStruct((RG_HEADS, RG_HEAD_DIM, 2 * RG_HEAD_DIM), F32),
                   jax.ShapeDtypeStruct((2, D_MODEL), F32), jax.ShapeDtypeStruct((1, D_MODEL), F32)),
        grid=(n_tiles,),
        in_specs=(_halo_specs(rows, D_MODEL, 0, n_tiles, tix) + [row_spec(za_block), row_spec(0)]
                  + _halo_specs(rows, D_MODEL, 0, n_tiles, tix) + _rg_weight_specs()),
        out_specs=(row_spec(0), _full((RG_HEADS, RG_HEAD_DIM, 2 * RG_HEAD_DIM)), _full((2, D_MODEL)),
                   _full((1, D_MODEL))),
        scratch_shapes=[pltpu.VMEM((SUBLANES, D_MODEL), F32)],
        compiler_params=_params("arbitrary"),
    )(proj, proj, proj, proj, dycat, h_dir, h_dir, h_dir, conv_w, conv_b, gate_w, gate_b, lam)


def even_mix_fwd(proj, h_f, h_b, sc_w, name):
    rows_total = proj.shape[0]
    rows = min(MIX_TILE, rows_total)
    n_tiles = rows_total // rows
    cb = D_MODEL
    n_cb = 1
    ident = lambda i: i

    def body(za_ref, hf_ref, hb_ref, xbp, xbm, xbn, gcp, gcm, gcn, gb_ref, zb_ref, w_ref, y_ref):
        t = pl.program_id(1)
        first, last = t == 0, t == n_tiles - 1
        za = za_ref[...]
        y_ref[:, 0:cb] = ((hf_ref[...] + hb_ref[...]) * (za * _sigmoid(za))).astype(BF16)
        p_ext = _extend(xbp, xbm, xbn, first, last) * _extend(gcp, gcm, gcn, first, last)
        cv = _conv(p_ext, w_ref[...], 1, rows)
        zb = zb_ref[...]
        y_ref[:, cb:2 * cb] = (gb_ref[...] * cv * (zb * _sigmoid(zb))).astype(BF16)

    blk = lambda col: pl.BlockSpec((rows, cb), lambda c, i: (i, col * n_cb + c))
    own = pl.BlockSpec((rows, cb), lambda c, i: (i, c))
    return pl.pallas_call(
        body, name=name,
        out_shape=jax.ShapeDtypeStruct((rows_total, 2 * D_MODEL), BF16),
        grid=(n_cb, n_tiles),
        in_specs=([blk(1), own, own] + _halo_specs(rows, cb, 2 * n_cb, n_tiles, ident)
                  + _halo_specs(rows, cb, 4 * n_cb, n_tiles, ident)
                  + [blk(3), blk(5), pl.BlockSpec((3, cb), lambda c, i: (0, c))]),
        out_specs=pl.BlockSpec((rows, 2 * cb), lambda c, i: (i, 0)),
        compiler_params=_params("parallel", "arbitrary"),
    )(proj, h_f, h_b, proj, proj, proj, proj, proj, proj, proj, proj, sc_w)


def even_mix_bwd(proj, dycat, h_f, h_b, dua_f, dua_b, conv_w, sc_w, name):
    rows_total = proj.shape[0]
    rows = min(MIX_TILE, rows_total)
    n_tiles = rows_total // rows
    cb = D_MODEL
    n_cb = 1
    ident = lambda i: i

    def body(xap, xam, xan, za_ref, xbp, xbm, xbn, gbp, gbm, gbn, gcp, gcm, gcn, zbp, zbm, zbn,
             dya_ref, dybp, dybm, dybn, hf_ref, hb_ref, dfp, dfm, dfn, dbp, dbm, dbn, cw_ref, sw_ref,
             dp_ref, dcw_ref, dcb_ref, dsw_ref):
        def put(k, value):
            dp_ref[:, k * cb:(k + 1) * cb] = value.astype(BF16)

        t = pl.program_id(1)
        first, last = t == 0, t == n_tiles - 1
        za = za_ref[...]
        sa = _sigmoid(za)
        put(1, dya_ref[...] * (hf_ref[...] + hb_ref[...]) * (sa * (1.0 + za * (1.0 - sa))))
        dua_ext = _extend(dfp, dfm, dfn, first, last) + _extend(dbp, dbm, dbn, first, last)
        cw = cw_ref[...]
        put(0, _conv_transpose(dua_ext, cw, 2, rows))
        dua = dua_ext[SUBLANES:SUBLANES + rows]
        xa_ext = _extend(xap, xam, xan, first, last)
        dcw = jnp.concatenate([_colsum(dua * _shifted(xa_ext, k - 2, rows)) for k in range(4)], axis=0)
        dcb = _colsum(dua)
        xb_ext = _extend(xbp, xbm, xbn, first, last)
        gc_ext = _extend(gcp, gcm, gcn, first, last)
        p_ext = xb_ext * gc_ext
        zb_ext = _extend(zbp, zbm, zbn, first, last)
        sb_ext = _sigmoid(zb_ext)
        dyb_ext = _extend(dybp, dybm, dybn, first, last)
        gb_ext = _extend(gbp, gbm, gbn, first, last)
        dcv_ext = dyb_ext * gb_ext * (zb_ext * sb_ext)
        sw = sw_ref[...]
        cv = _conv(p_ext, sw, 1, rows)
        mid = slice(SUBLANES, SUBLANES + rows)
        zb, sb, dyb, gb = zb_ext[mid], sb_ext[mid], dyb_ext[mid], gb_ext[mid]
        put(3, dyb * cv * (zb * sb))
        put(5, dyb * gb * cv * (sb * (1.0 + zb * (1.0 - sb))))
        dp = _conv_transpose(dcv_ext, sw, 1, rows)
        put(4, dp * xb_ext[mid])
        put(2, dp * gc_ext[mid])
        dcv = dcv_ext[mid]
        dsw = jnp.concatenate([_colsum(dcv * _shifted(p_ext, k - 1, rows)) for k in range(3)], axis=0)

        @pl.when(t == 0)
        def _():
            dcw_ref[...] = dcw
            dcb_ref[...] = dcb
            dsw_ref[...] = dsw

        @pl.when(t > 0)
        def _():
            dcw_ref[...] += dcw
            dcb_ref[...] += dcb
            dsw_ref[...] += dsw

    blk = lambda col: pl.BlockSpec((rows, cb), lambda c, i: (i, col * n_cb + c))
    halo = lambda col: _halo_specs(rows, cb, col * n_cb, n_tiles, ident)
    own = pl.BlockSpec((rows, cb), lambda c, i: (i, c))
    wspec = lambda k: pl.BlockSpec((k, cb), lambda c, i: (0, c))
    return pl.pallas_call(
        body, name=name,
        out_shape=(jax.ShapeDtypeStruct((rows_total, 6 * D_MODEL), BF16),
                   jax.ShapeDtypeStruct((4, D_MODEL), F32), jax.ShapeDtypeStruct((1, D_MODEL), F32),
                   jax.ShapeDtypeStruct((3, D_MODEL), F32)),
        grid=(n_cb, n_tiles),
        in_specs=(halo(0) + [blk(1)] + halo(2) + halo(3) + halo(4) + halo(5) + [blk(0)] + halo(1)
                  + [own, own] + halo(0) + halo(0) + [wspec(4), wspec(3)]),
        out_specs=(pl.BlockSpec((rows, 6 * cb), lambda c, i: (i, 0)), wspec(4), wspec(1), wspec(3)),
        compiler_params=_params("parallel", "arbitrary"),
    )(proj, proj, proj, proj, proj, proj, proj, proj, proj, proj, proj, proj, proj, proj, proj, proj,
      dycat, dycat, dycat, dycat, h_f, h_b, dua_f, dua_f, dua_f, dua_b, dua_b, dua_b, conv_w, sc_w)


def even_out_fwd(ycat, w_out, gain, x, name):
    rows, d = x.shape
    k = ycat.shape[1]
    tm = min(ROW_TILE, rows)

    def body(yc_ref, w_ref, g_ref, x_ref, x1_ref, y_ref):
        y = _dot(yc_ref[...], w_ref[...])
        y_ref[...] = y
        rstd = lax.rsqrt(jnp.mean(y * y, axis=-1, keepdims=True) + NORM_EPS)
        x1_ref[...] = x_ref[...] + y * rstd * g_ref[...]

    row = lambda n: pl.BlockSpec((tm, n), lambda i: (i, 0))
    return pl.pallas_call(
        body, name=name,
        out_shape=(jax.ShapeDtypeStruct((rows, d), F32),) * 2,
        grid=(rows // tm,),
        in_specs=[row(k), _full((k, d)), _full((1, d)), row(d)],
        out_specs=(row(d), row(d)),
        compiler_params=_params("parallel"),
    )(ycat, w_out, gain, x)


def _rmsnorm_bwd(dout, y, gain):
    rstd = lax.rsqrt(jnp.mean(y * y, axis=-1, keepdims=True) + NORM_EPS)
    yhat = y * rstd
    dyn = dout * gain
    dy = rstd * (dyn - yhat * jnp.mean(dyn * yhat, axis=-1, keepdims=True))
    return dy, dout * yhat


def even_out_bwd(dx1, y, gain, w_out, name):
    rows, d = y.shape
    k = w_out.shape[0]
    tm = min(ROW_TILE, rows)

    def body(dx_ref, y_ref, g_ref, w_ref, dy_ref, dyc_ref, dg_ref):
        dy, dg_rows = _rmsnorm_bwd(dx_ref[...], y_ref[...], g_ref[...])
        dyb = dy.astype(BF16)
        dy_ref[...] = dyb
        dyc_ref[...] = _dot_nt(dyb, w_ref[...])
        _accumulate(dg_ref, _colsum(dg_rows), pl.program_id(0))

    row = lambda n: pl.BlockSpec((tm, n), lambda i: (i, 0))
    return pl.pallas_call(
        body, name=name,
        out_shape=(jax.ShapeDtypeStruct((rows, d), BF16), jax.ShapeDtypeStruct((rows, k), F32),
                   jax.ShapeDtypeStruct((1, d), F32)),
        grid=(rows // tm,),
        in_specs=[row(d), row(d), _full((1, d)), _full((k, d))],
        out_specs=(row(d), row(k), _full((1, d))),
        compiler_params=_params("arbitrary"),
    )(dx1, y, gain, w_out)


def _chunk_cumsum(g, reverse):
    n = g.shape[0]
    pos = lax.broadcasted_iota(jnp.int32, g.shape, 0) % GLA_CHUNK
    s = 1
    while s < GLA_CHUNK:
        if reverse:
            g = g + jnp.where(pos < GLA_CHUNK - s, pltpu.roll(g, n - s, 0), 0.0)
        else:
            g = g + jnp.where(pos >= s, pltpu.roll(g, s, 0), 0.0)
        s *= 2
    return g


def _gla_prepare(q_ref, k_ref, lr_ref, wg_ref, bg_ref, reverse, n_chunks):
    z = _dot(lr_ref[...].astype(BF16), wg_ref[0]) + bg_ref[0]
    g = -_softplus(-z) * (1.0 / GLA_NORMALIZER)
    bcum = _chunk_cumsum(g, reverse).reshape(n_chunks, GLA_CHUNK, GLA_DK)
    edge = 0 if reverse else GLA_CHUNK - 1
    btot = bcum[:, edge:edge + 1, :]
    e_pos = jnp.exp(bcum)
    e_neg = jnp.exp(-bcum)
    e_st = jnp.exp(btot - bcum)
    q3 = q_ref[...].reshape(n_chunks, GLA_CHUNK, GLA_DK)
    k3 = k_ref[...].reshape(n_chunks, GLA_CHUNK, GLA_DK)
    scale = GLA_DK ** -0.5
    q_in = q3 * scale * e_pos
    k_in = k3 * e_neg
    k_st = k3 * e_st
    dec = jnp.exp(btot)
    return z, q_in, k_in, k_st, dec, (scale * e_pos, e_neg, e_st)


def _gla_mask(reverse):
    i = lax.broadcasted_iota(jnp.int32, (GLA_CHUNK, GLA_CHUNK), 0)
    j = lax.broadcasted_iota(jnp.int32, (GLA_CHUNK, GLA_CHUNK), 1)
    return (j >= i) if reverse else (j <= i)


def _gla_specs(rows, n_blocks, reverse):
    tix = (lambda s: n_blocks - 1 - s) if reverse else (lambda s: s)
    d = 1 if reverse else 0
    lr_block = LR_COL // LANES
    specs = [pl.BlockSpec((rows, GLA_DK), lambda h, s: (tix(s), h)),
             pl.BlockSpec((rows, GLA_DK), lambda h, s: (tix(s), GLA_HEADS + h)),
             pl.BlockSpec((rows, GLA_DV), lambda h, s: (tix(s), GLA_HEADS + h)),
             pl.BlockSpec((rows, LANES), lambda h, s: (tix(s), lr_block)),
             pl.BlockSpec((1, LANES, GLA_DK), lambda h, s: (d, 0, h)),
             pl.BlockSpec((1, 1, GLA_DK), lambda h, s: (d, 0, h))]
    return specs, tix


def gla_fwd(proj, wg_pad, bg, reverse, name):
    rows_total = proj.shape[0]
    rows = min(GLA_BLOCK, rows_total)
    n_blocks = rows_total // rows
    n_chunks = rows // GLA_CHUNK
    specs, tix = _gla_specs(rows, n_blocks, reverse)

    def body(q_ref, k_ref, v_ref, lr_ref, wg_ref, bg_ref, o_ref, st_ref, state, kv_scr, dec_scr):
        _, q_in, k_in, k_st, dec, _ = _gla_prepare(q_ref, k_ref, lr_ref, wg_ref, bg_ref, reverse, n_chunks)
        vb = v_ref[...].reshape(n_chunks, GLA_CHUNK, GLA_DV).astype(BF16)
        qb = q_in.astype(BF16)
        p = jnp.where(_gla_mask(reverse), _bdot(qb, k_in.astype(BF16), 2, 2), 0.0)
        o = _bdot(p.astype(BF16), vb, 2, 1)
        kv_scr[...] = _bdot(vb, k_st.astype(BF16), 1, 1)
        dec_scr[...] = jnp.broadcast_to(dec, dec_scr.shape)

        @pl.when(pl.program_id(1) == 0)
        def _():
            state[...] = jnp.zeros_like(state)

        for c in range(n_chunks):
            cc = n_chunks - 1 - c if reverse else c
            st_ref[0, cc] = state[...]
            state[...] = state[...] * dec_scr[cc, 0:1] + kv_scr[cc]
        o = o + _bdot(qb, st_ref[0].astype(BF16), 2, 2)
        o_ref[...] = o.reshape(rows, GLA_DV)

    return pl.pallas_call(
        body, name=name,
        out_shape=(jax.ShapeDtypeStruct((rows_total, GLA_HEADS * GLA_DV), F32),
                   jax.ShapeDtypeStruct((GLA_HEADS, rows_total // GLA_CHUNK, GLA_DV, GLA_DK), F32)),
        grid=(GLA_HEADS, n_blocks),
        in_specs=specs,
        out_specs=(pl.BlockSpec((rows, GLA_DV), lambda h, s: (tix(s), h)),
                   pl.BlockSpec((1, n_chunks, GLA_DV, GLA_DK), lambda h, s: (h, tix(s), 0, 0))),
        scratch_shapes=[pltpu.VMEM((GLA_DV, GLA_DK), F32), pltpu.VMEM((n_chunks, GLA_DV, GLA_DK), F32),
                        pltpu.VMEM((n_chunks, SUBLANES, GLA_DK), F32)],
        compiler_params=_params("parallel", "arbitrary"),
    )(proj, proj, proj, proj, wg_pad, bg)


def gla_bwd(proj, wg_pad, bg, d_o, states, dqkv_in, reverse, name):
    rows_total = proj.shape[0]
    rows = min(GLA_BLOCK, rows_total)
    n_blocks = rows_total // rows
    n_chunks = rows // GLA_CHUNK
    specs, tix = _gla_specs(rows, n_blocks, not reverse)
    d = 1 if reverse else 0
    specs[4] = pl.BlockSpec((1, LANES, GLA_DK), lambda h, s: (d, 0, h))
    specs[5] = pl.BlockSpec((1, 1, GLA_DK), lambda h, s: (d, 0, h))
    add = dqkv_in is not None

    def body(*refs):
        q_ref, k_ref, v_ref, lr_ref, wg_ref, bg_ref, do_ref, st_ref = refs[:8]
        refs = refs[8:]
        if add:
            aq_ref, ak_ref, av_ref = refs[:3]
            refs = refs[3:]
        dq_ref, dk_ref, dv_ref, dz_ref, dstate, g_scr, dec_scr, dsn_scr = refs
        z, q_in, k_in, k_st, dec, (f_q, f_k, f_s) = _gla_prepare(q_ref, k_ref, lr_ref, wg_ref, bg_ref, reverse,
                                                                 n_chunks)
        mask = _gla_mask(reverse)
        vb = v_ref[...].reshape(n_chunks, GLA_CHUNK, GLA_DV).astype(BF16)
        dob = do_ref[...].reshape(n_chunks, GLA_CHUNK, GLA_DV).astype(BF16)
        qb, kb, ksb = q_in.astype(BF16), k_in.astype(BF16), k_st.astype(BF16)
        st = st_ref[0]
        stb = st.astype(BF16)
        pb = jnp.where(mask, _bdot(qb, kb, 2, 2), 0.0).astype(BF16)
        dpb = jnp.where(mask, _bdot(dob, vb, 2, 2), 0.0).astype(BF16)
        d_qin = _bdot(dpb, kb, 2, 1) + _bdot(dob, stb, 2, 1)
        d_kin = _bdot(dpb, qb, 1, 1)
        dv = _bdot(pb, dob, 1, 1)
        g_scr[...] = _bdot(dob, qb, 1, 1)
        dec_scr[...] = jnp.broadcast_to(dec, dec_scr.shape)

        @pl.when(pl.program_id(1) == 0)
        def _():
            dstate[...] = jnp.zeros_like(dstate)

        for c in range(n_chunks):
            cc = c if reverse else n_chunks - 1 - c
            dsn_scr[cc] = dstate[...]
            dstate[...] = dstate[...] * dec_scr[cc, 0:1] + g_scr[cc]
        dsn = dsn_scr[...]
        dsnb = dsn.astype(BF16)
        dv = dv + _bdot(ksb, dsnb, 2, 2)
        d_kst = _bdot(vb, dsnb, 2, 1)
        d_dec = jnp.sum(dsn * st, axis=1, keepdims=True)
        ks_term = d_kst * k_st
        d_btot = d_dec * dec + jnp.sum(ks_term, axis=1, keepdims=True)
        d_b = d_qin * q_in - d_kin * k_in - ks_term
        pos = lax.broadcasted_iota(jnp.int32, d_b.shape, 1)
        edge = 0 if reverse else GLA_CHUNK - 1
        d_b = d_b + jnp.where(pos == edge, d_btot, 0.0)
        dg = _chunk_cumsum(d_b.reshape(rows, GLA_DK), not reverse)
        dz_ref[...] = dg * (1.0 / GLA_NORMALIZER) * _sigmoid(-z)
        dq = (d_qin * f_q).reshape(rows, GLA_DK)
        dk = (d_kin * f_k + d_kst * f_s).reshape(rows, GLA_DK)
        dv = dv.reshape(rows, GLA_DV)
        if add:
            dq_ref[...] = (dq + aq_ref[...]).astype(BF16)
            dk_ref[...] = (dk + ak_ref[...]).astype(BF16)
            dv_ref[...] = (dv + av_ref[...]).astype(BF16)
        else:
            dq_ref[...] = dq
            dk_ref[...] = dk
            dv_ref[...] = dv

    qkv_specs = [pl.BlockSpec((rows, GLA_DK), lambda h, s: (tix(s), h)),
                 pl.BlockSpec((rows, GLA_DK), lambda h, s: (tix(s), h)),
                 pl.BlockSpec((rows, GLA_DV), lambda h, s: (tix(s), h))]
    in_specs = specs + [pl.BlockSpec((rows, GLA_DV), lambda h, s: (tix(s), h)),
                        pl.BlockSpec((1, n_chunks, GLA_DV, GLA_DK), lambda h, s: (h, tix(s), 0, 0))]
    args = [proj, proj, proj, proj, wg_pad, bg, d_o, states]
    out_dtype = F32
    if add:
        in_specs += qkv_specs
        args += list(dqkv_in)
        out_dtype = BF16
    return pl.pallas_call(
        body, name=name,
        out_shape=(jax.ShapeDtypeStruct((rows_total, GLA_HEADS * GLA_DK), out_dtype),
                   jax.ShapeDtypeStruct((rows_total, GLA_HEADS * GLA_DK), out_dtype),
                   jax.ShapeDtypeStruct((rows_total, GLA_HEADS * GLA_DV), out_dtype),
                   jax.ShapeDtypeStruct((rows_total, GLA_HEADS * GLA_DK), F32)),
        grid=(GLA_HEADS, n_blocks),
        in_specs=in_specs,
        out_specs=(pl.BlockSpec((rows, GLA_DK), lambda h, s: (tix(s), h)),
                   pl.BlockSpec((rows, GLA_DK), lambda h, s: (tix(s), h)),
                   pl.BlockSpec((rows, GLA_DV), lambda h, s: (tix(s), h)),
                   pl.BlockSpec((rows, GLA_DK), lambda h, s: (tix(s), h))),
        scratch_shapes=[pltpu.VMEM((GLA_DV, GLA_DK), F32), pltpu.VMEM((n_chunks, GLA_DV, GLA_DK), F32),
                        pltpu.VMEM((n_chunks, SUBLANES, GLA_DK), F32),
                        pltpu.VMEM((n_chunks, GLA_DV, GLA_DK), F32)],
        compiler_params=_params("parallel", "arbitrary"),
    )(*args)


def gla_gate_bwd(proj, dz_f, dz_b, wg_pad, name):
    rows_total = proj.shape[0]
    tm = min(ROW_TILE, rows_total)
    n_key = GLA_HEADS * GLA_DK

    def body(lr_ref, dzf_ref, dzb_ref, wg_ref, dlr_ref, dwg_ref, dbg_ref):
        step = pl.program_id(0)
        lr_t = jnp.transpose(lr_ref[...])
        dzf, dzb = dzf_ref[...], dzb_ref[...]
        dzf16, dzb16 = dzf.astype(BF16), dzb.astype(BF16)
        dlr_ref[...] = (_dot_nt(dzf16, wg_ref[0]) + _dot_nt(dzb16, wg_ref[1])).astype(BF16)
        dwf = _dot(lr_t[0:GLA_RANK].astype(BF16), dzf16)
        dwb = _dot(lr_t[GLA_RANK:2 * GLA_RANK].astype(BF16), dzb16)
        dbg = jnp.concatenate([_colsum(dzf), _colsum(dzb)], axis=0)

        @pl.when(step == 0)
        def _():
            dwg_ref[0] = dwf
            dwg_ref[1] = dwb
            dbg_ref[...] = dbg

        @pl.when(step > 0)
        def _():
            dwg_ref[0] += dwf
            dwg_ref[1] += dwb
            dbg_ref[...] += dbg

    return pl.pallas_call(
        body, name=name,
        out_shape=(jax.ShapeDtypeStruct((rows_total, LANES), BF16), jax.ShapeDtypeStruct((2, GLA_RANK, n_key), F32),
                   jax.ShapeDtypeStruct((2, n_key), F32)),
        grid=(rows_total // tm,),
        in_specs=[pl.BlockSpec((tm, LANES), lambda i: (i, LR_COL // LANES)),
                  pl.BlockSpec((tm, n_key), lambda i: (i, 0)), pl.BlockSpec((tm, n_key), lambda i: (i, 0)),
                  _full((2, LANES, n_key))],
        out_specs=(pl.BlockSpec((tm, LANES), lambda i: (i, 0)), _full((2, GLA_RANK, n_key)), _full((2, n_key))),
        compiler_params=_params("arbitrary"),
    )(proj, dz_f, dz_b, wg_pad)


def _head_norm(o, gain):
    outs, hats, rstds = [], [], []
    for h in range(GLA_HEADS):
        oh = o[:, h * GLA_DV:(h + 1) * GLA_DV]
        rstd = lax.rsqrt(jnp.mean(oh * oh, axis=-1, keepdims=True) + NORM_EPS)
        hat = oh * rstd
        outs.append(hat * gain)
        hats.append(hat)
        rstds.append(rstd)
    return outs, hats, rstds


def odd_out_fwd(o_f, o_b, proj, head_gain, w_out, gain, x1, target, name):
    rows, d = x1.shape
    tm = min(ROW_TILE, rows)
    r_block = (2 * GLA_HEADS * GLA_DK + GLA_HEADS * GLA_DV) // d

    def body(of_ref, ob_ref, r_ref, hg_ref, w_ref, g_ref, x1_ref, tgt_ref, y2_ref, dy_ref, dx2_ref, loss_ref,
             dg_ref):
        step = pl.program_id(0)
        on, _, _ = _head_norm(of_ref[...] + ob_ref[...], hg_ref[...])
        r = r_ref[...]
        y2 = (jnp.concatenate(on, axis=1) * (r * _sigmoid(r))).astype(BF16)
        y2_ref[...] = y2
        y = _dot(y2, w_ref[...])
        gain_v = g_ref[...]
        rstd = lax.rsqrt(jnp.mean(y * y, axis=-1, keepdims=True) + NORM_EPS)
        x2 = x1_ref[...] + y * rstd * gain_v
        diff = x2 - tgt_ref[...]
        loss = 0.5 * jnp.sum(jnp.mean(diff * diff, axis=-1, keepdims=True), axis=0, keepdims=True)
        dx2 = diff * (1.0 / d)
        dx2_ref[...] = dx2
        dy, dg_rows = _rmsnorm_bwd(dx2, y, gain_v)
        dy_ref[...] = dy.astype(BF16)
        _accumulate(loss_ref, jnp.broadcast_to(loss, loss_ref.shape), step)
        _accumulate(dg_ref, _colsum(dg_rows), step)

    row = lambda n, col=0: pl.BlockSpec((tm, n), lambda i: (i, col))
    return pl.pallas_call(
        body, name=name,
        out_shape=(jax.ShapeDtypeStruct((rows, d), BF16), jax.ShapeDtypeStruct((rows, d), BF16),
                   jax.ShapeDtypeStruct((rows, d), F32), jax.ShapeDtypeStruct((SUBLANES, LANES), F32),
                   jax.ShapeDtypeStruct((1, d), F32)),
        grid=(rows // tm,),
        in_specs=[row(d), row(d), row(d, r_block), _full((1, GLA_DV)), _full((d, d)), _full((1, d)), row(d), row(d)],
        out_specs=(row(d), row(d), row(d), _full((SUBLANES, LANES)), _full((1, d))),
        compiler_params=_params("arbitrary"),
    )(o_f, o_b, proj, head_gain, w_out, gain, x1, target)


def odd_out_bwd(dy, w_out, o_f, o_b, proj, head_gain, name):
    rows, d = dy.shape
    tm = min(ROW_TILE, rows)
    r_block = (2 * GLA_HEADS * GLA_DK + GLA_HEADS * GLA_DV) // d

    def body(dy_ref, w_ref, of_ref, ob_ref, r_ref, hg_ref, dr_ref, do_ref, dhg_ref):
        dy2 = _dot_nt(dy_ref[...], w_ref[...])
        hg = hg_ref[...]
        on, hats, rstds = _head_norm(of_ref[...] + ob_ref[...], hg)
        r = r_ref[...]
        sr = _sigmoid(r)
        dr_ref[...] = (dy2 * jnp.concatenate(on, axis=1) * (sr * (1.0 + r * (1.0 - sr)))).astype(BF16)
        d_on = dy2 * (r * sr)
        d_os, dhg = [], None
        for h in range(GLA_HEADS):
            dn = d_on[:, h * GLA_DV:(h + 1) * GLA_DV]
            part = _colsum(dn * hats[h])
            dhg = part if dhg is None else dhg + part
            dng = dn * hg
            d_os.append(rstds[h] * (dng - hats[h] * jnp.mean(dng * hats[h], axis=-1, keepdims=True)))
        do_ref[...] = jnp.concatenate(d_os, axis=1)
        _accumulate(dhg_ref, dhg, pl.program_id(0))

    row = lambda n, col=0: pl.BlockSpec((tm, n), lambda i: (i, col))
    return pl.pallas_call(
        body, name=name,
        out_shape=(jax.ShapeDtypeStruct((rows, d), BF16), jax.ShapeDtypeStruct((rows, d), F32),
                   jax.ShapeDtypeStruct((1, GLA_DV), F32)),
        grid=(rows // tm,),
        in_specs=[row(d), _full((d, d)), row(d), row(d), row(d, r_block), _full((1, GLA_DV))],
        out_specs=(row(d), row(d), _full((1, GLA_DV))),
        compiler_params=_params("arbitrary"),
    )(dy, w_out, o_f, o_b, proj, head_gain)


def local_step(x, target, w, reduce_first=None, reduce_second=None):
    g = {}
    proj_e, h0 = norm_matmul(x, w["even_norm_pre"], w["even_w_in"], "even_in_proj")
    h_dir = [rglru_fwd(proj_e, w["rg_conv_w"], w["rg_conv_b"], w["rg_gate_w"][d], w["rg_gate_b"][d],
                       w["rg_lambda"][d], d == 1, "rglru_fwd_%d" % d) for d in range(2)]
    ycat = even_mix_fwd(proj_e, h_dir[0], h_dir[1], w["sc_conv_w"], "even_mix_fwd")
    x1, y_e = even_out_fwd(ycat, w["even_w_out"], w["even_norm_post"], x, "even_out_fwd")
    proj_o, h1 = norm_matmul(x1, w["odd_norm_pre"], w["odd_w_in"], "odd_in_proj")
    o_dir, st_dir = [], []
    for d in range(2):
        o, st = gla_fwd(proj_o, w["gla_wg_pad"], w["gla_b_gate"], d == 1, "gla_fwd_%d" % d)
        o_dir.append(o)
        st_dir.append(st)
    y2, dy_o, dx2, loss, g["odd_norm_post"] = odd_out_fwd(
        o_dir[0], o_dir[1], proj_o, w["gla_norm_g"], w["odd_w_out"], w["odd_norm_post"], x1, target, "odd_out_fwd")
    g["odd_w_out"] = matmul_dw(y2, dy_o, D_MODEL, "odd_w_out_grad")[0]
    dr, d_o, g["gla_norm_g"] = odd_out_bwd(dy_o, w["odd_w_out"], o_dir[0], o_dir[1], proj_o, w["gla_norm_g"],
                                           "odd_out_bwd")
    dq, dk, dv, dz_f = gla_bwd(proj_o, w["gla_wg_pad"], w["gla_b_gate"], d_o, st_dir[0], None, False, "gla_bwd_0")
    dq, dk, dv, dz_b = gla_bwd(proj_o, w["gla_wg_pad"], w["gla_b_gate"], d_o, st_dir[1], (dq, dk, dv), True,
                               "gla_bwd_1")
    dlr, g["gla_w_gate_lr"], g["gla_b_gate"] = gla_gate_bwd(proj_o, dz_f, dz_b, w["gla_wg_pad"], "gla_gate_bwd")
    dproj_o = [dq, dk, dv, dr, dlr]
    g["odd_w_in"] = jnp.concatenate(matmul_dw_pieces(h1, dproj_o, "odd_w_in_grad"), axis=1)[:, :ODD_IN]
    dx1, g["odd_norm_pre"] = inproj_bwd_pieces(dproj_o, w["odd_w_in"], x1, w["odd_norm_pre"], dx2, "odd_in_proj_bwd")
    dy_e, dycat, g["even_norm_post"] = even_out_bwd(dx1, y_e, w["even_norm_post"], w["even_w_out"], "even_out_bwd")
    g["even_w_out"] = matmul_dw(ycat, dy_e, D_MODEL, "even_w_out_grad")[0]
    conv_b = w["rg_conv_b"] if reduce_first is None else w["rg_conv_b"] + reduce_first(g)
    dua, dgw, dgb, dlam = [], [], [], []
    for d in range(2):
        a, b, c, e = rglru_bwd(proj_e, dycat, h_dir[d], w["rg_conv_w"], conv_b, w["rg_gate_w"][d],
                               w["rg_gate_b"][d], w["rg_lambda"][d], d == 1, "rglru_bwd_%d" % d)
        dua.append(a)
        dgw.append(b)
        dgb.append(c)
        dlam.append(e)
    dproj_e, g["rg_conv_w"], g["rg_conv_b"], g["sc_conv_w"] = even_mix_bwd(
        proj_e, dycat, h_dir[0], h_dir[1], dua[0], dua[1], w["rg_conv_w"], w["sc_conv_w"], "even_mix_bwd")
    dgw = jnp.stack(dgw).reshape(2, RG_HEADS, RG_HEAD_DIM, 2, RG_HEAD_DIM)
    g["rg_gate_w"] = jnp.transpose(dgw, (0, 3, 1, 2, 4))
    g["rg_gate_b"] = jnp.stack(dgb).reshape(2, 2, RG_HEADS, RG_HEAD_DIM)
    g["rg_lambda"] = jnp.concatenate(dlam, axis=0)
    g["even_w_in"] = matmul_dw(h0, dproj_e, EVEN_IN // 4, "even_w_in_grad")
    gain = w["even_norm_pre"] if reduce_second is None else w["even_norm_pre"] + reduce_second(g)
    grad_x, g["even_norm_pre"] = inproj_bwd(dproj_e, w["even_w_in"], x, gain, dx1, "even_in_proj_bwd")
    return loss, grad_x, g


def _prepare_weights(full):
    w = {}
    for name in ("even_norm_pre", "even_norm_post", "rg_conv_b", "odd_norm_pre", "odd_norm_post", "gla_norm_g"):
        w[name] = full[name].reshape(1, -1)
    w["rg_conv_w"] = full["rg_conv_w"]
    w["sc_conv_w"] = full["sc_conv_w"]
    w["even_w_in"] = full["even_w_in"].astype(BF16)
    if w["even_w_in"].ndim == 2:
        w["even_w_in"] = jnp.transpose(w["even_w_in"].reshape(D_MODEL, 4, EVEN_IN // 4), (1, 0, 2))
    w["even_w_out"] = full["even_w_out"].astype(BF16)
    gw = jnp.transpose(full["rg_gate_w"].astype(BF16), (0, 2, 3, 1, 4))
    w["rg_gate_w"] = gw.reshape(2, RG_HEADS, RG_HEAD_DIM, 2 * RG_HEAD_DIM)
    w["rg_gate_b"] = full["rg_gate_b"].reshape(2, 2, D_MODEL)
    w["rg_lambda"] = full["rg_lambda"].reshape(2, 1, D_MODEL)
    w_in = jnp.pad(full["odd_w_in"].astype(BF16), ((0, 0), (0, ODD_IN_PAD - ODD_IN)))
    w["odd_w_in"] = w_in.reshape(1, D_MODEL, ODD_IN_PAD)
    w["odd_w_out"] = full["odd_w_out"].astype(BF16)
    wg = full["gla_w_gate_lr"].astype(BF16)
    w["gla_wg_pad"] = jnp.stack([jnp.pad(wg[d], ((d * GLA_RANK, LANES - (d + 1) * GLA_RANK), (0, 0)))
                                 for d in range(2)])
    w["gla_b_gate"] = full["gla_b_gate"].reshape(2, 1, GLA_HEADS * GLA_DK)
    return w


SHARDED_SMALL = (("rg_conv_w", (4, 256)), ("rg_lambda", (2, 256)), ("sc_conv_w", (3, 256)),
                 ("odd_norm_pre", (256,)), ("odd_norm_post", (256,)), ("gla_w_gate_lr", (2, 16, 128)),
                 ("gla_b_gate", (2, 128)), ("gla_norm_g", (64,)))
SHARDED_ROWS = 64
REPLICATED = (("even_norm_post", (1024,)), ("rg_conv_b", (1024,)),
              ("rg_gate_b", (2, 2, 8, 128)), ("rg_gate_w", (2, 2, 8, 128, 128)))
LAST_REPLICATED = (("even_norm_pre", (1024,)),)
LAST_ROWS = 8
REPLICATED_ROWS = 4160
REP_PART = REPLICATED_ROWS // 8
HALF_SHARDED = SHARDED_ROWS // 2
PACK_HALF = HALF_SHARDED + REP_PART


def _seg_rows(shape):
    n = 1
    for s in shape:
        n *= s
    return -(-n // LANES)


def _pack(arrays, spec, total_rows, lead=()):
    parts = []
    for name, shape in spec:
        flat = arrays[name].reshape(lead + (-1,))
        pad = _seg_rows(shape) * LANES - flat.shape[-1]
        if pad:
            flat = jnp.pad(flat, [(0, 0)] * len(lead) + [(0, pad)])
        parts.append(flat.reshape(lead + (-1, LANES)))
    rows = jnp.concatenate(parts, axis=len(lead))
    pad = total_rows - rows.shape[len(lead)]
    return jnp.pad(rows, [(0, 0)] * len(lead) + [(0, pad), (0, 0)])


def _unpack(rows, spec, lead=()):
    out, at = {}, 0
    for name, shape in spec:
        n = 1
        for s in shape:
            n *= s
        k = _seg_rows(shape)
        seg = lax.slice_in_dim(rows, at, at + k, axis=len(lead)).reshape(lead + (-1,))
        out[name] = lax.slice_in_dim(seg, 0, n, axis=len(lead)).reshape(lead + shape)
        at += k
    return out


def _split_owners(arr):
    a = arr.reshape(arr.shape[:-1] + (4, arr.shape[-1] // 4))
    return jnp.moveaxis(a, -2, 0)


def _merge_owners(arr):
    a = jnp.moveaxis(arr, 0, -2)
    return a.reshape(a.shape[:-2] + (-1,))


HBM_SPEC = pl.BlockSpec(memory_space=pltpu.HBM)


def _position():
    x, y, c = lax.axis_index("x"), lax.axis_index("y"), lax.axis_index("c")
    chips = [(1 - x, y), (x, 1 - y), (1 - x, 1 - y)]
    return x, y, c, chips


def _remote(src, dst, send_sem, recv_sem, device):
    return pltpu.make_async_remote_copy(src_ref=src, dst_ref=dst, send_sem=send_sem, recv_sem=recv_sem,
                                        device_id=device, device_id_type=MESH)


def gather_weights(halved, whole):
    n_h, n_w = len(halved), len(whole)
    n = n_h + n_w

    def body(*refs):
        ins, outs = refs[:n], refs[n:2 * n]
        send_ici, recv_ici, send_fwd, recv_fwd = refs[2 * n:]
        x, y, c, chips = _position()
        me = 2 * x + y
        sibling = (x, y, 1 - c)
        sends = []
        for a in range(n):
            for k, chip in enumerate(chips):
                src = ins[a].at[c] if a < n_h else ins[a]
                dst = outs[a].at[me, c] if a < n_h else outs[a].at[me]
                cp = _remote(src, dst, send_ici.at[3 * a + k], recv_ici.at[3 * a + k], (chip[0], chip[1], c))
                cp.start()
                sends.append(cp)
        for a in range(n):
            for k, chip in enumerate(chips):
                q = 2 * chip[0] + chip[1]
                landed = outs[a].at[q, c] if a < n_h else outs[a].at[q]
                _remote(landed, landed, send_ici.at[3 * a + k], recv_ici.at[3 * a + k], sibling).wait_recv()
                if a < n_h:
                    cp = _remote(landed, landed, send_fwd.at[3 * a + k], recv_fwd.at[3 * a + k], sibling)
                    cp.start()
                    sends.append(cp)
        for a in range(n_h):
            for k, chip in enumerate(chips):
                q = 2 * chip[0] + chip[1]
                passed = outs[a].at[q, 1 - c]
                _remote(passed, passed, send_fwd.at[3 * a + k], recv_fwd.at[3 * a + k], sibling).wait_recv()
        for cp in sends:
            cp.wait_send()

    arrays = list(halved) + list(whole)
    out_shape = [jax.ShapeDtypeStruct((4,) + a.shape, a.dtype) for a in arrays]
    outs = pl.pallas_call(
        body, name="gather_weights",
        out_shape=out_shape,
        in_specs=[HBM_SPEC] * n, out_specs=[HBM_SPEC] * n,
        scratch_shapes=[pltpu.SemaphoreType.DMA((3 * n,)), pltpu.SemaphoreType.DMA((3 * n,)),
                        pltpu.SemaphoreType.DMA((3 * n_h,)), pltpu.SemaphoreType.DMA((3 * n_h,))],
    )(*arrays)
    return outs[:n_h], outs[n_h:]


def place_own(full, own, chip, name):
    _, _, r, cols = full.shape
    tr = _row_tile(r, cols)

    def body(p_ref, own_ref, full_ref, o_ref):
        o_ref[0] = own_ref[...]

    return pl.pallas_call(
        body, name=name,
        out_shape=jax.ShapeDtypeStruct(full.shape, full.dtype),
        grid_spec=pltpu.PrefetchScalarGridSpec(
            num_scalar_prefetch=1, grid=(2, r // tr),
            in_specs=[pl.BlockSpec((1, tr, cols), lambda h, i, p_ref: (h, i, 0)), pl.BlockSpec(memory_space=pl.ANY)],
            out_specs=pl.BlockSpec((1, 1, tr, cols), lambda h, i, p_ref: (p_ref[0], h, i, 0))),
        input_output_aliases={2: 0},
        compiler_params=_params("parallel", "parallel"),
    )(chip, own, full)


def exchange_with_sibling(arrays, name):
    n = len(arrays)

    def body(*refs):
        ins, outs = refs[:n], refs[n:2 * n]
        send_sems, recv_sems = refs[2 * n:]
        x, y, c, _ = _position()
        copies = []
        for a in range(n):
            cp = _remote(ins[a].at[:, 1 - c], outs[a], send_sems.at[a], recv_sems.at[a], (x, y, 1 - c))
            cp.start()
            copies.append(cp)
        for cp in copies:
            cp.wait()

    return pl.pallas_call(
        body, name=name,
        out_shape=[jax.ShapeDtypeStruct((a.shape[0],) + a.shape[2:], a.dtype) for a in arrays],
        in_specs=[HBM_SPEC] * n, out_specs=[HBM_SPEC] * n,
        scratch_shapes=[pltpu.SemaphoreType.DMA((n,)), pltpu.SemaphoreType.DMA((n,))],
    )(*arrays)


SEM_SPEC = pl.BlockSpec(memory_space=pltpu.SEMAPHORE)
SIDE_EFFECT = pltpu.SideEffectType.DATAFLOW_SIDE_EFFECTING


def _chip_copies(ins, lands, send_sems, recv_sems):
    x, y, c, chips = _position()
    copies = []
    for a in range(len(ins)):
        for k, chip in enumerate(chips):
            q = 2 * chip[0] + chip[1]
            copies.append(_remote(ins[a].at[q], lands[a].at[k], send_sems.at[3 * a + k], recv_sems.at[3 * a + k],
                                  (chip[0], chip[1], c)))
    return copies


def exchange_with_chips_start(arrays, name):
    n = len(arrays)
    lands = [lax.empty((3,) + a.shape[1:], a.dtype) for a in arrays]

    def body(*refs):
        ins, lz, send_sems, recv_sems, token = refs[:n], refs[n:2 * n], refs[2 * n], refs[2 * n + 1], refs[-1]
        for cp in _chip_copies(ins, lz, send_sems, recv_sems):
            cp.start()
        token[...] = jnp.zeros_like(token)

    operands = [pltpu.with_memory_space_constraint(a, pltpu.HBM) for a in list(arrays) + lands]
    return pl.pallas_call(
        body, name=name,
        out_shape=(pltpu.SemaphoreType.DMA((3 * n,)), pltpu.SemaphoreType.DMA((3 * n,)))
        + tuple(pltpu.HBM(a.shape, a.dtype) for a in operands) + (jax.ShapeDtypeStruct((SUBLANES, LANES), F32),),
        in_specs=[HBM_SPEC] * (2 * n),
        out_specs=(SEM_SPEC, SEM_SPEC) + (HBM_SPEC,) * (2 * n) + (pl.BlockSpec(memory_space=pltpu.VMEM),),
        input_output_aliases={i: 2 + i for i in range(2 * n)},
        compiler_params=pltpu.CompilerParams(has_side_effects=SIDE_EFFECT),
    )(*operands)


def exchange_with_chips_wait(started, after, name):
    send_sems, recv_sems = started[0], started[1]
    operands = list(started[2:-1])
    n = len(operands) // 2

    def body(*refs):
        ins, lz, send_ref, recv_ref = refs[:n], refs[n:2 * n], refs[2 * n], refs[2 * n + 1]
        for cp in _chip_copies(ins, lz, send_ref, recv_ref):
            cp.wait_send()
            cp.wait_recv()

    outs = pl.pallas_call(
        body, name=name,
        out_shape=tuple(pltpu.HBM(a.shape, a.dtype) for a in operands),
        in_specs=[HBM_SPEC] * (2 * n) + [SEM_SPEC, SEM_SPEC, pl.BlockSpec(memory_space=pl.ANY)],
        out_specs=(HBM_SPEC,) * (2 * n),
        input_output_aliases={i: i for i in range(2 * n)},
        compiler_params=pltpu.CompilerParams(has_side_effects=SIDE_EFFECT),
    )(*operands, send_sems, recv_sems, after)
    return outs[:n], outs[n:]


def share_totals(totals, pack_total, last_part):
    arrays = list(totals) + [pack_total]
    n = len(arrays)

    def body(*refs):
        ins, last, outs, rep, last_all = refs[:n], refs[n], refs[n + 1:2 * n + 1], refs[2 * n + 1], refs[2 * n + 2]
        send_sems, recv_sems, rep_send, rep_recv, last_send, last_recv = refs[2 * n + 3:]
        x, y, c, chips = _position()
        sibling = (x, y, 1 - c)
        me = 4 * x + 2 * y + c
        sends = []
        for a in range(n):
            cp = _remote(ins[a], outs[a], send_sems.at[a], recv_sems.at[a], sibling)
            cp.start()
            sends.append(cp)
        mine = ins[n - 1].at[pl.ds(HALF_SHARDED, REP_PART)]
        peers = [sibling]
        for chip in chips:
            peers += [(chip[0], chip[1], c), (chip[0], chip[1], 1 - c)]
        for j, peer in enumerate(peers):
            for src, dst, s_sem, r_sem in ((mine, rep, rep_send, rep_recv), (last, last_all, last_send, last_recv)):
                cp = _remote(src, dst.at[me], s_sem.at[j], r_sem.at[j], peer)
                cp.start()
                sends.append(cp)
        for a in range(n):
            _remote(outs[a], outs[a], send_sems.at[a], recv_sems.at[a], sibling).wait_recv()
        for j, peer in enumerate(peers):
            it = 4 * peer[0] + 2 * peer[1] + peer[2]
            _remote(rep.at[it], rep.at[it], rep_send.at[j], rep_recv.at[j], peer).wait_recv()
            _remote(last_all.at[it], last_all.at[it], last_send.at[j], last_recv.at[j], peer).wait_recv()
        for cp in sends:
            cp.wait_send()

    outs = pl.pallas_call(
        body, name="grad_share_totals",
        out_shape=[jax.ShapeDtypeStruct(a.shape, a.dtype) for a in arrays]
        + [jax.ShapeDtypeStruct((8, REP_PART, LANES), F32), jax.ShapeDtypeStruct((8, LAST_ROWS, LANES), F32)],
        in_specs=[HBM_SPEC] * (n + 1), out_specs=[HBM_SPEC] * (n + 2),
        scratch_shapes=[pltpu.SemaphoreType.DMA((n,)), pltpu.SemaphoreType.DMA((n,))]
        + [pltpu.SemaphoreType.DMA((7,))] * 4,
    )(*arrays, last_part)
    return outs[:n], outs[n], outs[n + 1]


def sum_parts(parts, name):
    def body(p_ref, o_ref):
        total = p_ref[0]
        for k in range(1, parts.shape[0]):
            total = total + p_ref[k]
        o_ref[...] = total

    return pl.pallas_call(body, name=name, out_shape=jax.ShapeDtypeStruct(parts.shape[1:], parts.dtype))(parts)


TILE_BYTES = 1 << 20


def _row_tile(rows, cols):
    best = None
    for t in range(SUBLANES, rows + 1, SUBLANES):
        if rows % t == 0 and t * cols * 4 <= TILE_BYTES:
            best = t
    return best if best is not None else rows


def add_sibling(mine, received, core, out_dtype, name):
    _, _, r, cols = mine.shape
    tr = _row_tile(r, cols)

    def body(c_ref, a_ref, b_ref, o_ref):
        o_ref[...] = (a_ref[0] + b_ref[...]).astype(out_dtype)

    return pl.pallas_call(
        body, name=name,
        out_shape=jax.ShapeDtypeStruct((4, r, cols), out_dtype),
        grid_spec=pltpu.PrefetchScalarGridSpec(
            num_scalar_prefetch=1, grid=(4, r // tr),
            in_specs=[pl.BlockSpec((1, 1, tr, cols), lambda o, i, c_ref: (o, c_ref[0], i, 0)),
                      pl.BlockSpec((1, tr, cols), lambda o, i, c_ref: (o, i, 0))],
            out_specs=pl.BlockSpec((1, tr, cols), lambda o, i, c_ref: (o, i, 0))),
        compiler_params=_params("parallel", "parallel"),
    )(core, mine, received)


def add_chips(own, received, chip, name):
    _, r, cols = own.shape
    tr = _row_tile(r, cols)

    def body(p_ref, a_ref, b0, b1, b2, o_ref):
        o_ref[...] = ((a_ref[0].astype(F32) + b0[0].astype(F32)) + b1[0].astype(F32)) + b2[0].astype(F32)

    rb = lambda k: pl.BlockSpec((1, tr, cols), lambda i, p_ref: (k, i, 0))
    return pl.pallas_call(
        body, name=name,
        out_shape=jax.ShapeDtypeStruct((r, cols), F32),
        grid_spec=pltpu.PrefetchScalarGridSpec(
            num_scalar_prefetch=1, grid=(r // tr,),
            in_specs=[pl.BlockSpec((1, tr, cols), lambda i, p_ref: (p_ref[0], i, 0)), rb(0), rb(1), rb(2)],
            out_specs=pl.BlockSpec((tr, cols), lambda i, p_ref: (i, 0))),
        compiler_params=_params("parallel"),
    )(chip, own, received, received, received)


def _adamw_update(gv, w_ref, m_ref, v_ref, d_ref, nm_ref, nv_ref):
    nm = ADAM_B1 * m_ref[...] + (1.0 - ADAM_B1) * gv
    nv = ADAM_B2 * v_ref[...] + (1.0 - ADAM_B2) * (gv * gv)
    nm_ref[...] = nm
    nv_ref[...] = nv
    m_hat = nm / (1.0 - ADAM_B1 ** ADAM_STEP)
    v_hat = nv / (1.0 - ADAM_B2 ** ADAM_STEP)
    d_ref[...] = -ADAM_LR * (m_hat / (jnp.sqrt(v_hat) + ADAM_EPS) + ADAM_WD * w_ref[...])


def adamw_halves(w, own, received, m, v, core, name):
    rows, cols = w.shape
    r = rows // 2
    tr = _row_tile(r, cols)
    nr = r // tr

    def body(c_ref, w_ref, own_ref, rec_ref, m_ref, v_ref, g_ref, d_ref, nm_ref, nv_ref):
        gv = jnp.where(pl.program_id(0) == c_ref[0], own_ref[...], rec_ref[...])
        g_ref[...] = gv
        _adamw_update(gv, w_ref, m_ref, v_ref, d_ref, nm_ref, nv_ref)

    whole = pl.BlockSpec((tr, cols), lambda h, i, c_ref: (h * nr + i, 0))
    half = pl.BlockSpec((tr, cols), lambda h, i, c_ref: (i, 0))
    return pl.pallas_call(
        body, name=name,
        out_shape=(jax.ShapeDtypeStruct((rows, cols), F32),) * 4,
        grid_spec=pltpu.PrefetchScalarGridSpec(
            num_scalar_prefetch=1, grid=(2, nr),
            in_specs=[whole, half, half, whole, whole], out_specs=(whole,) * 4),
        compiler_params=_params("parallel", "parallel"),
    )(core, w, own, received, m, v)


def adamw(w, g, m, v, name):
    r, cols = w.shape
    tr = _row_tile(r, cols)

    def body(w_ref, g_ref, m_ref, v_ref, d_ref, nm_ref, nv_ref):
        _adamw_update(g_ref[...], w_ref, m_ref, v_ref, d_ref, nm_ref, nv_ref)

    blk = pl.BlockSpec((tr, cols), lambda i: (i, 0))
    return pl.pallas_call(
        body, name=name,
        out_shape=(jax.ShapeDtypeStruct((r, cols), F32),) * 3,
        grid=(r // tr,),
        in_specs=[blk] * 4, out_specs=(blk,) * 3,
        compiler_params=_params("parallel"),
    )(w, g, m, v)


WEIGHTS = ("even_norm_pre", "even_norm_post", "even_w_in", "rg_conv_w", "rg_conv_b", "rg_gate_w", "rg_gate_b",
           "rg_lambda", "sc_conv_w", "even_w_out", "odd_norm_pre", "odd_norm_post", "odd_w_in", "gla_w_gate_lr",
           "gla_b_gate", "gla_norm_g", "odd_w_out")
BIG = ("even_w_in", "even_w_out", "odd_w_in", "odd_w_out")


def _halves(a):
    return a.reshape((2, a.shape[0] // 2) + a.shape[1:])


def kernel(x, even_norm_pre, even_norm_post, even_w_in, rg_conv_w, rg_conv_b, rg_gate_w, rg_gate_b, rg_lambda, sc_conv_w, even_w_out, odd_norm_pre, odd_norm_post, odd_w_in, gla_w_gate_lr, gla_b_gate, gla_norm_g, odd_w_out, loss_target, m_even_norm_pre, m_even_norm_post, m_even_w_in, m_rg_conv_w, m_rg_conv_b, m_rg_gate_w, m_rg_gate_b, m_rg_lambda, m_sc_conv_w, m_even_w_out, m_odd_norm_pre, m_odd_norm_post, m_odd_w_in, m_gla_w_gate_lr, m_gla_b_gate, m_gla_norm_g, m_odd_w_out, v_even_norm_pre, v_even_norm_post, v_even_w_in, v_rg_conv_w, v_rg_conv_b, v_rg_gate_w, v_rg_gate_b, v_rg_lambda, v_sc_conv_w, v_even_w_out, v_odd_norm_pre, v_odd_norm_post, v_odd_w_in, v_gla_w_gate_lr, v_gla_b_gate, v_gla_norm_g, v_odd_w_out):
    given = dict(locals())
    shard = {n: given[n][0] for n in WEIGHTS}
    m_in = {n: given["m_" + n][0] for n in WEIGHTS}
    v_in = {n: given["v_" + n][0] for n in WEIGHTS}
    mx, my, mc = lax.axis_index("x"), lax.axis_index("y"), lax.axis_index("c")
    core = jnp.reshape(mc, (1,)).astype(jnp.int32)
    chip = jnp.reshape(2 * mx + my, (1,)).astype(jnp.int32)

    small_shard = _pack(shard, SHARDED_SMALL, SHARDED_ROWS)
    big_own = [_halves(shard[n].astype(BF16)) for n in BIG]
    big_full, (small_full,) = gather_weights(big_own, [small_shard])
    big_full = [place_own(a, b, chip, "place_" + n) for a, b, n in zip(big_full, big_own, BIG)]
    small_full = lax.dynamic_update_slice(small_full, small_shard[None], (chip[0], 0, 0))
    full = {n: shard[n] for n, _ in REPLICATED + LAST_REPLICATED}
    full.update({n: _merge_owners(a) for n, a in _unpack(small_full, SHARDED_SMALL, lead=(4,)).items()})
    full["even_w_in"] = big_full[0].reshape(4, D_MODEL, EVEN_IN // 4)
    full["even_w_out"] = big_full[1].reshape(2 * D_MODEL, D_MODEL)
    full["odd_w_in"] = jnp.transpose(big_full[2].reshape(4, D_MODEL, ODD_IN // 4), (1, 0, 2)).reshape(D_MODEL, ODD_IN)
    full["odd_w_out"] = big_full[3].reshape(D_MODEL, D_MODEL)

    pending = {}

    def slab(a):
        return a.reshape((4, 2, a.shape[1] // 2) + a.shape[2:])

    def begin(tag, slabs, dtypes):
        got = exchange_with_sibling(slabs, "grad_sibling_" + tag)
        sums = [add_sibling(a, b, core, dt, "grad_add_sibling_%s%d" % (tag, i))
                for i, (a, b, dt) in enumerate(zip(slabs, got, dtypes))]
        pending[tag] = exchange_with_chips_start(sums, "grad_chips_start_" + tag)
        return pending[tag][-1][0, 0]

    def finish(tag, after):
        sums, got = exchange_with_chips_wait(pending[tag], after, "grad_chips_wait_" + tag)
        return [add_chips(a, b, chip, "grad_add_chips_%s%d" % (tag, i)) for i, (a, b) in enumerate(zip(sums, got))]

    def reduce_first(g):
        return begin("a", [slab(jnp.transpose(g["odd_w_in"].reshape(D_MODEL, 4, ODD_IN // 4), (1, 0, 2))),
                           slab(g["odd_w_out"].reshape(4, D_MODEL // 4, D_MODEL)),
                           slab(g["even_w_out"].reshape(4, D_MODEL // 2, D_MODEL))], [BF16] * 3)

    def reduce_second(g):
        pending["totals_a"] = finish("a", g["even_w_in"])
        rep_rows = _pack(g, REPLICATED, REPLICATED_ROWS).reshape(4, 2, REP_PART, LANES)
        sh_rows = _pack({n: _split_owners(g[n]) for n, _ in SHARDED_SMALL}, SHARDED_SMALL, SHARDED_ROWS, lead=(4,))
        pack = jnp.concatenate([sh_rows.reshape(4, 2, HALF_SHARDED, LANES), rep_rows], axis=2)
        return begin("b", [slab(g["even_w_in"]), pack], [BF16, F32])

    loss, grad_x, g = local_step(x[0], loss_target[0], _prepare_weights(full), reduce_first, reduce_second)
    loss = lax.psum(loss[0, 0], ("x", "y", "c"))
    odd_w_in_t, odd_w_out_t, even_w_out_t = pending["totals_a"]
    even_w_in_t, pack_t = finish("b", grad_x)
    totals = [even_w_in_t, even_w_out_t, odd_w_in_t, odd_w_out_t]
    last_part = _pack(g, LAST_REPLICATED, LAST_ROWS)
    from_core, rep_all, last_all = share_totals(totals, pack_t, last_part)
    me = 2 * chip[0] + core[0]
    mine, theirs = pack_t[:HALF_SHARDED], from_core[4][:HALF_SHARDED]
    sh_total = jnp.where(mc == 0, jnp.concatenate([mine, theirs]), jnp.concatenate([theirs, mine]))
    rep_all = lax.dynamic_update_slice(rep_all, pack_t[None, HALF_SHARDED:], (me, 0, 0))
    rep_total = rep_all.reshape(REPLICATED_ROWS, LANES)
    last_total = sum_parts(lax.dynamic_update_slice(last_all, last_part[None], (me, 0, 0)), "grad_sum_last")
    grads = {}
    grads.update(_unpack(sh_total, SHARDED_SMALL))
    grads.update(_unpack(rep_total, REPLICATED))
    grads.update(_unpack(last_total, LAST_REPLICATED))

    delta, new_m, new_v = {}, {}, {}
    for i, n in enumerate(BIG):
        grads[n], delta[n], new_m[n], new_v[n] = adamw_halves(shard[n], totals[i], from_core[i], m_in[n], v_in[n],
                                                              core, "adamw_" + n)
    small = ((SHARDED_SMALL, SHARDED_ROWS), (REPLICATED, REPLICATED_ROWS), (LAST_REPLICATED, LAST_ROWS))
    packed = [jnp.concatenate([_pack(src, spec, rows) for spec, rows in small]) for src in (shard, m_in, v_in)]
    small_g = jnp.concatenate([sh_total, rep_total, last_total], axis=0)
    outs = adamw(packed[0], small_g, packed[1], packed[2], "adamw_small")
    for dst, packed_rows in zip((delta, new_m, new_v), outs):
        at = 0
        for spec, rows in small:
            dst.update(_unpack(packed_rows[at:at + rows], spec))
            at += rows
    result = [loss, grad_x[None]]
    for group in (grads, delta, new_m, new_v):
        result += [group[n].reshape(given[n].shape) for n in WEIGHTS]
    return tuple(result)
```

```python
import functools

import jax
import jax.numpy as jnp
from jax import lax
from jax.experimental import pallas as pl
from jax.experimental.pallas import tpu as pltpu

F32 = jnp.float32
BF16 = jnp.bfloat16
MESH = pl.DeviceIdType.MESH

D_MODEL = 1024
NORM_EPS = 1e-6
RG_HEADS = 8
RG_HEAD_DIM = 128
RG_C = 8.0
EVEN_IN = 6144
ODD_IN = 3104
ODD_IN_PAD = 3200
GLA_HEADS = 4
GLA_DK = 128
GLA_DV = 256
GLA_RANK = 16
GLA_NORMALIZER = 16.0
GLA_CHUNK = 64
LR_COL = 3072

ADAM_LR = 0.001
ADAM_B1 = 0.9
ADAM_B2 = 0.999
ADAM_EPS = 1e-08
ADAM_WD = 0.01
ADAM_STEP = 10

SUBLANES = 8
LANES = 128
VMEM_LIMIT = 56 * 2 ** 20

ROW_TILE = 512
SCAN_TILE = 256
GLA_BLOCK = 1024
MIX_TILE = 128


def _params(*sem):
    return pltpu.CompilerParams(dimension_semantics=sem, vmem_limit_bytes=VMEM_LIMIT)


def _full(shape):
    n = len(shape)
    return pl.BlockSpec(shape, lambda *_: (0,) * n)


def _sigmoid(x):
    return 0.5 + 0.5 * jnp.tanh(0.5 * x)


def _softplus(x):
    return jnp.maximum(x, 0.0) + jnp.log(1.0 + jnp.exp(-jnp.abs(x)))


def _dot(a, b):
    return jnp.dot(a, b, preferred_element_type=F32)


def _dot_nt(a, b):
    return lax.dot_general(a, b, (((1,), (1,)), ((), ())), preferred_element_type=F32)


def _dot_tn(a, b):
    return lax.dot_general(a, b, (((0,), (0,)), ((), ())), preferred_element_type=F32)


def _bdot(a, b, ca, cb):
    return lax.dot_general(a, b, (((ca,), (cb,)), ((0,), (0,))), preferred_element_type=F32)


def _halo_specs(rows, cols, col_block, n_row_tiles, tix):
    per = rows // SUBLANES
    last = n_row_tiles * per - 1

    def split(args):
        if len(args) == 2:
            return tix(args[1]), col_block + args[0]
        return tix(args[0]), col_block

    def prev(*args):
        t, c = split(args)
        return (jnp.maximum(t * per - 1, 0), c)

    def main(*args):
        return split(args)

    def nxt(*args):
        t, c = split(args)
        return (jnp.minimum((t + 1) * per, last), c)

    return [pl.BlockSpec((SUBLANES, cols), prev), pl.BlockSpec((rows, cols), main),
            pl.BlockSpec((SUBLANES, cols), nxt)]


def _extend(prev_ref, main_ref, next_ref, is_first, is_last):
    p = jnp.where(is_first, 0.0, prev_ref[...])
    n = jnp.where(is_last, 0.0, next_ref[...])
    return jnp.concatenate([p, main_ref[...], n], axis=0)


def _shifted(ext, offset, rows):
    if offset == 0:
        return ext[SUBLANES:SUBLANES + rows]
    n = ext.shape[0]
    return pltpu.roll(ext, (-offset) % n, 0)[SUBLANES:SUBLANES + rows]


def _conv(ext, w, left, rows):
    out = None
    for k in range(w.shape[0]):
        term = _shifted(ext, k - left, rows) * w[k:k + 1]
        out = term if out is None else out + term
    return out


def _conv_transpose(ext, w, left, rows):
    out = None
    for k in range(w.shape[0]):
        term = _shifted(ext, left - k, rows) * w[k:k + 1]
        out = term if out is None else out + term
    return out


def _colsum(x):
    return jnp.sum(x, axis=0, keepdims=True)


def _accumulate(ref, value, step):
    @pl.when(step == 0)
    def _():
        ref[...] = value

    @pl.when(step > 0)
    def _():
        ref[...] += value


PROJ_TILE_BYTES = 7 * 2 ** 20


def _proj_row_tile(rows, width):
    tm = min(ROW_TILE, rows)
    while tm * width * 4 > PROJ_TILE_BYTES and tm % (2 * SUBLANES) == 0:
        tm //= 2
    return tm


def norm_matmul(x, gain, w, name):
    rows, d = x.shape
    n_col_tiles, _, tn = w.shape
    tm = _proj_row_tile(rows, n_col_tiles * tn)

    def body(x_ref, g_ref, w_ref, proj_ref, h_ref):
        xv = x_ref[...]
        rstd = lax.rsqrt(jnp.mean(xv * xv, axis=-1, keepdims=True) + NORM_EPS)
        hv = (xv * rstd * g_ref[...]).astype(BF16)
        h_ref[...] = hv
        for j in range(n_col_tiles):
            proj_ref[:, j * tn:(j + 1) * tn] = _dot(hv, w_ref[j])

    row = lambda cols: pl.BlockSpec((tm, cols), lambda i: (i, 0))
    return pl.pallas_call(
        body, name=name,
        out_shape=(jax.ShapeDtypeStruct((rows, n_col_tiles * tn), F32), jax.ShapeDtypeStruct((rows, d), BF16)),
        grid=(rows // tm,),
        in_specs=[row(d), _full((1, d)), _full(w.shape)],
        out_specs=(row(n_col_tiles * tn), row(d)),
        compiler_params=_params("parallel"),
    )(x, gain, w)


def inproj_bwd(dproj, w, x, gain, dres, name):
    rows, d = x.shape
    n_col_tiles, _, tn = w.shape
    tm = _proj_row_tile(rows, n_col_tiles * tn)

    def body(dp_ref, w_ref, x_ref, g_ref, dres_ref, dx_ref, dg_ref):
        dh = None
        for j in range(n_col_tiles):
            part = _dot_nt(dp_ref[:, j * tn:(j + 1) * tn], w_ref[j])
            dh = part if dh is None else dh + part
        _inproj_finish(dh, x_ref, g_ref, dres_ref, dx_ref, dg_ref, pl.program_id(0))

    row = lambda cols: pl.BlockSpec((tm, cols), lambda i: (i, 0))
    return pl.pallas_call(
        body, name=name,
        out_shape=(jax.ShapeDtypeStruct((rows, d), F32), jax.ShapeDtypeStruct((1, d), F32)),
        grid=(rows // tm,),
        in_specs=[row(n_col_tiles * tn), _full(w.shape), row(d), _full((1, d)), row(d)],
        out_specs=(row(d), _full((1, d))),
        compiler_params=_params("arbitrary"),
    )(dproj, w, x, gain, dres)


def _inproj_finish(dh, x_ref, g_ref, dres_ref, dx_ref, dg_ref, step):
    xv = x_ref[...]
    rstd = lax.rsqrt(jnp.mean(xv * xv, axis=-1, keepdims=True) + NORM_EPS)
    xhat = xv * rstd
    dxn = dh * g_ref[...]
    dx_ref[...] = dres_ref[...] + rstd * (dxn - xhat * jnp.mean(dxn * xhat, axis=-1, keepdims=True))
    _accumulate(dg_ref, _colsum(dh * xhat), step)


def inproj_bwd_pieces(pieces, w, x, gain, dres, name):
    rows, d = x.shape
    tm = min(ROW_TILE, rows)
    n = len(pieces)
    widths = [p.shape[1] for p in pieces]
    starts = [sum(widths[:k]) for k in range(n)]
    assert sum(widths) == w.shape[2]

    def body(*refs):
        w_ref, x_ref, g_ref, dres_ref, dx_ref, dg_ref = refs[n:]
        dh = None
        for k in range(n):
            part = _dot_nt(refs[k][...], w_ref[0, :, starts[k]:starts[k] + widths[k]])
            dh = part if dh is None else dh + part
        _inproj_finish(dh, x_ref, g_ref, dres_ref, dx_ref, dg_ref, pl.program_id(0))

    row = lambda cols: pl.BlockSpec((tm, cols), lambda i: (i, 0))
    return pl.pallas_call(
        body, name=name,
        out_shape=(jax.ShapeDtypeStruct((rows, d), F32), jax.ShapeDtypeStruct((1, d), F32)),
        grid=(rows // tm,),
        in_specs=[row(wd) for wd in widths] + [_full(w.shape), row(d), _full((1, d)), row(d)],
        out_specs=(row(d), _full((1, d))),
        compiler_params=_params("arbitrary"),
    )(*pieces, w, x, gain, dres)


def matmul_dw_pieces(a, pieces, name):
    rows, m = a.shape
    tk = min(ROW_TILE, rows)
    n = len(pieces)

    def body(*refs):
        a_ref, ins, outs = refs[0], refs[1:1 + n], refs[1 + n:]
        av = a_ref[...]
        for k in range(n):
            _accumulate(outs[k], _dot_tn(av, ins[k][...]), pl.program_id(0))

    return pl.pallas_call(
        body, name=name,
        out_shape=[jax.ShapeDtypeStruct((m, p.shape[1]), F32) for p in pieces],
        grid=(rows // tk,),
        in_specs=[pl.BlockSpec((tk, m), lambda k: (k, 0))]
        + [pl.BlockSpec((tk, p.shape[1]), lambda k: (k, 0)) for p in pieces],
        out_specs=[_full((m, p.shape[1])) for p in pieces],
        compiler_params=_params("arbitrary"),
    )(a, *pieces)


def matmul_dw(a, b, bn, name):
    rows, m = a.shape
    n = b.shape[1]
    tk = min(2 * ROW_TILE, rows)
    steps = rows // tk

    def body(a_ref, b_ref, o_ref):
        part = _dot_tn(a_ref[...], b_ref[...])

        @pl.when(pl.program_id(1) == 0)
        def _():
            o_ref[0] = part

        @pl.when(pl.program_id(1) > 0)
        def _():
            o_ref[0] += part

    return pl.pallas_call(
        body, name=name,
        out_shape=jax.ShapeDtypeStruct((n // bn, m, bn), F32),
        grid=(n // bn, steps),
        in_specs=[pl.BlockSpec((tk, m), lambda j, k: (k, 0)), pl.BlockSpec((tk, bn), lambda j, k: (k, j))],
        out_specs=pl.BlockSpec((1, m, bn), lambda j, k: (j, 0, 0)),
        compiler_params=_params("parallel", "arbitrary"),
    )(a, b)


def _scan(a, b, carry, reverse):
    n = a.shape[0]
    pos = lax.broadcasted_iota(jnp.int32, a.shape, 0) % SUBLANES
    s = 1
    while s < SUBLANES:
        if reverse:
            a_s, b_s, valid = pltpu.roll(a, n - s, 0), pltpu.roll(b, n - s, 0), pos < SUBLANES - s
        else:
            a_s, b_s, valid = pltpu.roll(a, s, 0), pltpu.roll(b, s, 0), pos >= s
        b = jnp.where(valid, a * b_s + b, b)
        a = jnp.where(valid, a * a_s, a)
        s *= 2
    blocks = n // SUBLANES
    out = [None] * blocks
    for k in (range(blocks - 1, -1, -1) if reverse else range(blocks)):
        rows = slice(k * SUBLANES, (k + 1) * SUBLANES)
        h = a[rows] * carry + b[rows]
        out[k] = h
        carry = h[0:1] if reverse else h[SUBLANES - 1:SUBLANES]
    return jnp.concatenate(out, axis=0)


def _rg_gates(ua, gw_ref, gb, lam):
    ub = ua.astype(BF16)
    pre_r, pre_i = [], []
    for h in range(RG_HEADS):
        z = _dot(ub[:, h * RG_HEAD_DIM:(h + 1) * RG_HEAD_DIM], gw_ref[h])
        pre_r.append(z[:, :RG_HEAD_DIM])
        pre_i.append(z[:, RG_HEAD_DIM:])
    r = _sigmoid(jnp.concatenate(pre_r, axis=1) + gb[0:1])
    i = _sigmoid(jnp.concatenate(pre_i, axis=1) + gb[1:2])
    sp = _softplus(-lam)
    log_a = -RG_C * r * sp
    a = jnp.exp(log_a)
    mult = jnp.sqrt(1.0 - a * a)
    return r, i, sp, a, mult


def _rg_weight_specs():
    return [_full((4, D_MODEL)), _full((1, D_MODEL)), _full((RG_HEADS, RG_HEAD_DIM, 2 * RG_HEAD_DIM)),
            _full((2, D_MODEL)), _full((1, D_MODEL))]


def rglru_fwd(proj, conv_w, conv_b, gate_w, gate_b, lam, reverse, name):
    rows_total = proj.shape[0]
    rows = min(SCAN_TILE, rows_total)
    n_tiles = rows_total // rows
    tix = (lambda i: n_tiles - 1 - i) if reverse else (lambda i: i)

    def body(xp, xm, xn, cw_ref, cb_ref, gw_ref, gb_ref, lam_ref, h_ref, carry):
        i = pl.program_id(0)
        t = tix(i)
        ext = _extend(xp, xm, xn, t == 0, t == n_tiles - 1)
        ua = _conv(ext, cw_ref[...], 2, rows) + cb_ref[...]
        _, gi, _, a, mult = _rg_gates(ua, gw_ref, gb_ref[...], lam_ref[...])
        b = mult * (gi * ua)

        @pl.when(i == 0)
        def _():
            carry[...] = jnp.zeros_like(carry)

        h = _scan(a, b, carry[0:1], reverse)
        h_ref[...] = h
        edge = h[0:1] if reverse else h[rows - 1:rows]
        carry[...] = jnp.broadcast_to(edge, carry.shape)

    return pl.pallas_call(
        body, name=name,
        out_shape=jax.ShapeDtypeStruct((rows_total, D_MODEL), F32),
        grid=(n_tiles,),
        in_specs=_halo_specs(rows, D_MODEL, 0, n_tiles, tix) + _rg_weight_specs(),
        out_specs=pl.BlockSpec((rows, D_MODEL), lambda i: (tix(i), 0)),
        scratch_shapes=[pltpu.VMEM((SUBLANES, D_MODEL), F32)],
        compiler_params=_params("arbitrary"),
    )(proj, proj, proj, conv_w, conv_b, gate_w, gate_b, lam)


def rglru_bwd(proj, dycat, h_dir, conv_w, conv_b, gate_w, gate_b, lam, reverse, name):
    rows_total = proj.shape[0]
    rows = min(SCAN_TILE, rows_total)
    n_tiles = rows_total // rows
    tix = (lambda i: i) if reverse else (lambda i: n_tiles - 1 - i)
    za_block = 1

    def body(xp, xm, xn, za_ref, dya_ref, hp, hm, hn, cw_ref, cb_ref, gw_ref, gb_ref, lam_ref,
             dua_ref, dgw_ref, dgb_ref, dlam_ref, carry):
        step = pl.program_id(0)
        t = tix(step)
        first, last = t == 0, t == n_tiles - 1
        ext = _extend(xp, xm, xn, first, last)
        ua = _conv(ext, cw_ref[...], 2, rows) + cb_ref[...]
        lam_v = lam_ref[...]
        r, gi, sp, a, mult = _rg_gates(ua, gw_ref, gb_ref[...], lam_v)
        za = za_ref[...]
        dh = dya_ref[...] * (za * _sigmoid(za))

        @pl.when(step == 0)
        def _():
            carry[...] = jnp.zeros_like(carry)

        old = carry[0:1]
        mu = _scan(a, a * dh, old, not reverse)
        row = lax.broadcasted_iota(jnp.int32, mu.shape, 0)
        if reverse:
            mu_next = jnp.where(row == 0, old, pltpu.roll(mu, 1, 0))
            carry[...] = jnp.broadcast_to(mu[rows - 1:rows], carry.shape)
            h_ext = _extend(hp, hm, hn, first, last)
            h_prev = _shifted(h_ext, 1, rows)
        else:
            mu_next = jnp.where(row == rows - 1, old, pltpu.roll(mu, rows - 1, 0))
            carry[...] = jnp.broadcast_to(mu[0:1], carry.shape)
            h_ext = _extend(hp, hm, hn, first, last)
            h_prev = _shifted(h_ext, -1, rows)
        db = dh + mu_next
        da = db * h_prev
        d_mult = db * (gi * ua)
        di = db * (mult * ua)
        dua = db * (mult * gi)
        dlog_a = da * a - d_mult * (a * a) / mult
        dr = dlog_a * (-RG_C * sp)
        dlam = _colsum(dlog_a * (-RG_C * r)) * (-_sigmoid(-lam_v))
        dpr = dr * (r * (1.0 - r))
        dpi = di * (gi * (1.0 - gi))
        dgb = jnp.concatenate([_colsum(dpr), _colsum(dpi)], axis=0)
        ub = ua.astype(BF16)
        dua_heads, dgw_heads = [], []
        for h in range(RG_HEADS):
            cols = slice(h * RG_HEAD_DIM, (h + 1) * RG_HEAD_DIM)
            dz = jnp.concatenate([dpr[:, cols], dpi[:, cols]], axis=1).astype(BF16)
            dgw_heads.append(_dot_tn(ub[:, cols], dz))
            dua_heads.append(_dot_nt(dz, gw_ref[h]))
        dua_ref[...] = dua + jnp.concatenate(dua_heads, axis=1)

        @pl.when(step == 0)
        def _():
            for h in range(RG_HEADS):
                dgw_ref[h] = dgw_heads[h]
            dgb_ref[...] = dgb
            dlam_ref[...] = dlam

        @pl.when(step > 0)
        def _():
            for h in range(RG_HEADS):
                dgw_ref[h] += dgw_heads[h]
            dgb_ref[...] += dgb
            dlam_ref[...] += dlam

    row_spec = lambda col: pl.BlockSpec((rows, D_MODEL), lambda i: (tix(i), col))
    return pl.pallas_call(
        body, name=name,
        out_shape=(jax.ShapeDtypeStruct((rows_total, D_MODEL), F32),
                   jax.ShapeDtypeStruct((RG_HEADS, RG_HEAD_DIM, 2 * RG_HEAD_DIM), F32),
                   jax.ShapeDtypeStruct((2, D_MODEL), F32), jax.ShapeDtypeStruct((1, D_MODEL), F32)),
        grid=(n_tiles,),
        in_specs=(_halo_specs(rows, D_MODEL, 0, n_tiles, tix) + [row_spec(za_block), row_spec(0)]
                  + _halo_specs(rows, D_MODEL, 0, n_tiles, tix) + _rg_weight_specs()),
        out_specs=(row_spec(0), _full((RG_HEADS, RG_HEAD_DIM, 2 * RG_HEAD_DIM)), _full((2, D_MODEL)),
                   _full((1, D_MODEL))),
        scratch_shapes=[pltpu.VMEM((SUBLANES, D_MODEL), F32)],
        compiler_params=_params("arbitrary"),
    )(proj, proj, proj, proj, dycat, h_dir, h_dir, h_dir, conv_w, conv_b, gate_w, gate_b, lam)


def even_mix_fwd(proj, h_f, h_b, sc_w, name):
    rows_total = proj.shape[0]
    rows = min(MIX_TILE, rows_total)
    n_tiles = rows_total // rows
    cb = D_MODEL
    n_cb = 1
    ident = lambda i: i

    def body(za_ref, hf_ref, hb_ref, xbp, xbm, xbn, gcp, gcm, gcn, gb_ref, zb_ref, w_ref, y_ref):
        t = pl.program_id(1)
        first, last = t == 0, t == n_tiles - 1
        za = za_ref[...]
        y_ref[:, 0:cb] = ((hf_ref[...] + hb_ref[...]) * (za * _sigmoid(za))).astype(BF16)
        p_ext = _extend(xbp, xbm, xbn, first, last) * _extend(gcp, gcm, gcn, first, last)
        cv = _conv(p_ext, w_ref[...], 1, rows)
        zb = zb_ref[...]
        y_ref[:, cb:2 * cb] = (gb_ref[...] * cv * (zb * _sigmoid(zb))).astype(BF16)

    blk = lambda col: pl.BlockSpec((rows, cb), lambda c, i: (i, col * n_cb + c))
    own = pl.BlockSpec((rows, cb), lambda c, i: (i, c))
    return pl.pallas_call(
        body, name=name,
        out_shape=jax.ShapeDtypeStruct((rows_total, 2 * D_MODEL), BF16),
        grid=(n_cb, n_tiles),
        in_specs=([blk(1), own, own] + _halo_specs(rows, cb, 2 * n_cb, n_tiles, ident)
                  + _halo_specs(rows, cb, 4 * n_cb, n_tiles, ident)
                  + [blk(3), blk(5), pl.BlockSpec((3, cb), lambda c, i: (0, c))]),
        out_specs=pl.BlockSpec((rows, 2 * cb), lambda c, i: (i, 0)),
        compiler_params=_params("parallel", "arbitrary"),
    )(proj, h_f, h_b, proj, proj, proj, proj, proj, proj, proj, proj, sc_w)


def even_mix_bwd(proj, dycat, h_f, h_b, dua_f, dua_b, conv_w, sc_w, name):
    rows_total = proj.shape[0]
    rows = min(MIX_TILE, rows_total)
    n_tiles = rows_total // rows
    cb = D_MODEL
    n_cb = 1
    ident = lambda i: i

    def body(xap, xam, xan, za_ref, xbp, xbm, xbn, gbp, gbm, gbn, gcp, gcm, gcn, zbp, zbm, zbn,
             dya_ref, dybp, dybm, dybn, hf_ref, hb_ref, dfp, dfm, dfn, dbp, dbm, dbn, cw_ref, sw_ref,
             dp_ref, dcw_ref, dcb_ref, dsw_ref):
        def put(k, value):
            dp_ref[:, k * cb:(k + 1) * cb] = value.astype(BF16)

        t = pl.program_id(1)
        first, last = t == 0, t == n_tiles - 1
        za = za_ref[...]
        sa = _sigmoid(za)
        put(1, dya_ref[...] * (hf_ref[...] + hb_ref[...]) * (sa * (1.0 + za * (1.0 - sa))))
        dua_ext = _extend(dfp, dfm, dfn, first, last) + _extend(dbp, dbm, dbn, first, last)
        cw = cw_ref[...]
        put(0, _conv_transpose(dua_ext, cw, 2, rows))
        dua = dua_ext[SUBLANES:SUBLANES + rows]
        xa_ext = _extend(xap, xam, xan, first, last)
        dcw = jnp.concatenate([_colsum(dua * _shifted(xa_ext, k - 2, rows)) for k in range(4)], axis=0)
        dcb = _colsum(dua)
        xb_ext = _extend(xbp, xbm, xbn, first, last)
        gc_ext = _extend(gcp, gcm, gcn, first, last)
        p_ext = xb_ext * gc_ext
        zb_ext = _extend(zbp, zbm, zbn, first, last)
        sb_ext = _sigmoid(zb_ext)
        dyb_ext = _extend(dybp, dybm, dybn, first, last)
        gb_ext = _extend(gbp, gbm, gbn, first, last)
        dcv_ext = dyb_ext * gb_ext * (zb_ext * sb_ext)
        sw = sw_ref[...]
        cv = _conv(p_ext, sw, 1, rows)
        mid = slice(SUBLANES, SUBLANES + rows)
        zb, sb, dyb, gb = zb_ext[mid], sb_ext[mid], dyb_ext[mid], gb_ext[mid]
        put(3, dyb * cv * (zb * sb))
        put(5, dyb * gb * cv * (sb * (1.0 + zb * (1.0 - sb))))
        dp = _conv_transpose(dcv_ext, sw, 1, rows)
        put(4, dp * xb_ext[mid])
        put(2, dp * gc_ext[mid])
        dcv = dcv_ext[mid]
        dsw = jnp.concatenate([_colsum(dcv * _shifted(p_ext, k - 1, rows)) for k in range(3)], axis=0)

        @pl.when(t == 0)
        def _():
            dcw_ref[...] = dcw
            dcb_ref[...] = dcb
            dsw_ref[...] = dsw

        @pl.when(t > 0)
        def _():
            dcw_ref[...] += dcw
            dcb_ref[...] += dcb
            dsw_ref[...] += dsw

    blk = lambda col: pl.BlockSpec((rows, cb), lambda c, i: (i, col * n_cb + c))
    halo = lambda col: _halo_specs(rows, cb, col * n_cb, n_tiles, ident)
    own = pl.BlockSpec((rows, cb), lambda c, i: (i, c))
    wspec = lambda k: pl.BlockSpec((k, cb), lambda c, i: (0, c))
    return pl.pallas_call(
        body, name=name,
        out_shape=(jax.ShapeDtypeStruct((rows_total, 6 * D_MODEL), BF16),
                   jax.ShapeDtypeStruct((4, D_MODEL), F32), jax.ShapeDtypeStruct((1, D_MODEL), F32),
                   jax.ShapeDtypeStruct((3, D_MODEL), F32)),
        grid=(n_cb, n_tiles),
        in_specs=(halo(0) + [blk(1)] + halo(2) + halo(3) + halo(4) + halo(5) + [blk(0)] + halo(1)
                  + [own, own] + halo(0) + halo(0) + [wspec(4), wspec(3)]),
        out_specs=(pl.BlockSpec((rows, 6 * cb), lambda c, i: (i, 0)), wspec(4), wspec(1), wspec(3)),
        compiler_params=_params("parallel", "arbitrary"),
    )(proj, proj, proj, proj, proj, proj, proj, proj, proj, proj, proj, proj, proj, proj, proj, proj,
      dycat, dycat, dycat, dycat, h_f, h_b, dua_f, dua_f, dua_f, dua_b, dua_b, dua_b, conv_w, sc_w)


def even_out_fwd(ycat, w_out, gain, x, name):
    rows, d = x.shape
    k = ycat.shape[1]
    tm = min(ROW_TILE, rows)

    def body(yc_ref, w_ref, g_ref, x_ref, x1_ref, y_ref):
        y = _dot(yc_ref[...], w_ref[...])
        y_ref[...] = y
        rstd = lax.rsqrt(jnp.mean(y * y, axis=-1, keepdims=True) + NORM_EPS)
        x1_ref[...] = x_ref[...] + y * rstd * g_ref[...]

    row = lambda n: pl.BlockSpec((tm, n), lambda i: (i, 0))
    return pl.pallas_call(
        body, name=name,
        out_shape=(jax.ShapeDtypeStruct((rows, d), F32),) * 2,
        grid=(rows // tm,),
        in_specs=[row(k), _full((k, d)), _full((1, d)), row(d)],
        out_specs=(row(d), row(d)),
        compiler_params=_params("parallel"),
    )(ycat, w_out, gain, x)


def _rmsnorm_bwd(dout, y, gain):
    rstd = lax.rsqrt(jnp.mean(y * y, axis=-1, keepdims=True) + NORM_EPS)
    yhat = y * rstd
    dyn = dout * gain
    dy = rstd * (dyn - yhat * jnp.mean(dyn * yhat, axis=-1, keepdims=True))
    return dy, dout * yhat


def even_out_bwd(dx1, y, gain, w_out, name):
    rows, d = y.shape
    k = w_out.shape[0]
    tm = min(ROW_TILE, rows)

    def body(dx_ref, y_ref, g_ref, w_ref, dy_ref, dyc_ref, dg_ref):
        dy, dg_rows = _rmsnorm_bwd(dx_ref[...], y_ref[...], g_ref[...])
        dyb = dy.astype(BF16)
        dy_ref[...] = dyb
        dyc_ref[...] = _dot_nt(dyb, w_ref[...])
        _accumulate(dg_ref, _colsum(dg_rows), pl.program_id(0))

    row = lambda n: pl.BlockSpec((tm, n), lambda i: (i, 0))
    return pl.pallas_call(
        body, name=name,
        out_shape=(jax.ShapeDtypeStruct((rows, d), BF16), jax.ShapeDtypeStruct((rows, k), F32),
                   jax.ShapeDtypeStruct((1, d), F32)),
        grid=(rows // tm,),
        in_specs=[row(d), row(d), _full((1, d)), _full((k, d))],
        out_specs=(row(d), row(k), _full((1, d))),
        compiler_params=_params("arbitrary"),
    )(dx1, y, gain, w_out)


def _chunk_cumsum(g, reverse):
    n = g.shape[0]
    pos = lax.broadcasted_iota(jnp.int32, g.shape, 0) % GLA_CHUNK
    s = 1
    while s < GLA_CHUNK:
        if reverse:
            g = g + jnp.where(pos < GLA_CHUNK - s, pltpu.roll(g, n - s, 0), 0.0)
        else:
            g = g + jnp.where(pos >= s, pltpu.roll(g, s, 0), 0.0)
        s *= 2
    return g


def _gla_prepare(q_ref, k_ref, lr_ref, wg_ref, bg_ref, reverse, n_chunks):
    z = _dot(lr_ref[...].astype(BF16), wg_ref[0]) + bg_ref[0]
    g = -_softplus(-z) * (1.0 / GLA_NORMALIZER)
    bcum = _chunk_cumsum(g, reverse).reshape(n_chunks, GLA_CHUNK, GLA_DK)
    edge = 0 if reverse else GLA_CHUNK - 1
    btot = bcum[:, edge:edge + 1, :]
    e_pos = jnp.exp(bcum)
    e_neg = jnp.exp(-bcum)
    e_st = jnp.exp(btot - bcum)
    q3 = q_ref[...].reshape(n_chunks, GLA_CHUNK, GLA_DK)
    k3 = k_ref[...].reshape(n_chunks, GLA_CHUNK, GLA_DK)
    scale = GLA_DK ** -0.5
    q_in = q3 * scale * e_pos
    k_in = k3 * e_neg
    k_st = k3 * e_st
    dec = jnp.exp(btot)
    return z, q_in, k_in, k_st, dec, (scale * e_pos, e_neg, e_st)


def _gla_mask(reverse):
    i = lax.broadcasted_iota(jnp.int32, (GLA_CHUNK, GLA_CHUNK), 0)
    j = lax.broadcasted_iota(jnp.int32, (GLA_CHUNK, GLA_CHUNK), 1)
    return (j >= i) if reverse else (j <= i)


def _gla_specs(rows, n_blocks, reverse):
    tix = (lambda s: n_blocks - 1 - s) if reverse else (lambda s: s)
    d = 1 if reverse else 0
    lr_block = LR_COL // LANES
    specs = [pl.BlockSpec((rows, GLA_DK), lambda h, s: (tix(s), h)),
             pl.BlockSpec((rows, GLA_DK), lambda h, s: (tix(s), GLA_HEADS + h)),
             pl.BlockSpec((rows, GLA_DV), lambda h, s: (tix(s), GLA_HEADS + h)),
             pl.BlockSpec((rows, LANES), lambda h, s: (tix(s), lr_block)),
             pl.BlockSpec((1, LANES, GLA_DK), lambda h, s: (d, 0, h)),
             pl.BlockSpec((1, 1, GLA_DK), lambda h, s: (d, 0, h))]
    return specs, tix


def gla_fwd(proj, wg_pad, bg, reverse, name):
    rows_total = proj.shape[0]
    rows = min(GLA_BLOCK, rows_total)
    n_blocks = rows_total // rows
    n_chunks = rows // GLA_CHUNK
    specs, tix = _gla_specs(rows, n_blocks, reverse)

    def body(q_ref, k_ref, v_ref, lr_ref, wg_ref, bg_ref, o_ref, st_ref, state, kv_scr, dec_scr):
        _, q_in, k_in, k_st, dec, _ = _gla_prepare(q_ref, k_ref, lr_ref, wg_ref, bg_ref, reverse, n_chunks)
        vb = v_ref[...].reshape(n_chunks, GLA_CHUNK, GLA_DV).astype(BF16)
        qb = q_in.astype(BF16)
        p = jnp.where(_gla_mask(reverse), _bdot(qb, k_in.astype(BF16), 2, 2), 0.0)
        o = _bdot(p.astype(BF16), vb, 2, 1)
        kv_scr[...] = _bdot(vb, k_st.astype(BF16), 1, 1)
        dec_scr[...] = jnp.broadcast_to(dec, dec_scr.shape)

        @pl.when(pl.program_id(1) == 0)
        def _():
            state[...] = jnp.zeros_like(state)

        for c in range(n_chunks):
            cc = n_chunks - 1 - c if reverse else c
            st_ref[0, cc] = state[...]
            state[...] = state[...] * dec_scr[cc, 0:1] + kv_scr[cc]
        o = o + _bdot(qb, st_ref[0].astype(BF16), 2, 2)
        o_ref[...] = o.reshape(rows, GLA_DV)

    return pl.pallas_call(
        body, name=name,
        out_shape=(jax.ShapeDtypeStruct((rows_total, GLA_HEADS * GLA_DV), F32),
                   jax.ShapeDtypeStruct((GLA_HEADS, rows_total // GLA_CHUNK, GLA_DV, GLA_DK), F32)),
        grid=(GLA_HEADS, n_blocks),
        in_specs=specs,
        out_specs=(pl.BlockSpec((rows, GLA_DV), lambda h, s: (tix(s), h)),
                   pl.BlockSpec((1, n_chunks, GLA_DV, GLA_DK), lambda h, s: (h, tix(s), 0, 0))),
        scratch_shapes=[pltpu.VMEM((GLA_DV, GLA_DK), F32), pltpu.VMEM((n_chunks, GLA_DV, GLA_DK), F32),
                        pltpu.VMEM((n_chunks, SUBLANES, GLA_DK), F32)],
        compiler_params=_params("parallel", "arbitrary"),
    )(proj, proj, proj, proj, wg_pad, bg)


def gla_bwd(proj, wg_pad, bg, d_o, states, dqkv_in, reverse, name):
    rows_total = proj.shape[0]
    rows = min(GLA_BLOCK, rows_total)
    n_blocks = rows_total // rows
    n_chunks = rows // GLA_CHUNK
    specs, tix = _gla_specs(rows, n_blocks, not reverse)
    d = 1 if reverse else 0
    specs[4] = pl.BlockSpec((1, LANES, GLA_DK), lambda h, s: (d, 0, h))
    specs[5] = pl.BlockSpec((1, 1, GLA_DK), lambda h, s: (d, 0, h))
    add = dqkv_in is not None

    def body(*refs):
        q_ref, k_ref, v_ref, lr_ref, wg_ref, bg_ref, do_ref, st_ref = refs[:8]
        refs = refs[8:]
        if add:
            aq_ref, ak_ref, av_ref = refs[:3]
            refs = refs[3:]
        dq_ref, dk_ref, dv_ref, dz_ref, dstate, g_scr, dec_scr, dsn_scr = refs
        z, q_in, k_in, k_st, dec, (f_q, f_k, f_s) = _gla_prepare(q_ref, k_ref, lr_ref, wg_ref, bg_ref, reverse,
                                                                 n_chunks)
        mask = _gla_mask(reverse)
        vb = v_ref[...].reshape(n_chunks, GLA_CHUNK, GLA_DV).astype(BF16)
        dob = do_ref[...].reshape(n_chunks, GLA_CHUNK, GLA_DV).astype(BF16)
        qb, kb, ksb = q_in.astype(BF16), k_in.astype(BF16), k_st.astype(BF16)
        st = st_ref[0]
        stb = st.astype(BF16)
        pb = jnp.where(mask, _bdot(qb, kb, 2, 2), 0.0).astype(BF16)
        dpb = jnp.where(mask, _bdot(dob, vb, 2, 2), 0.0).astype(BF16)
        d_qin = _bdot(dpb, kb, 2, 1) + _bdot(dob, stb, 2, 1)
        d_kin = _bdot(dpb, qb, 1, 1)
        dv = _bdot(pb, dob, 1, 1)
        g_scr[...] = _bdot(dob, qb, 1, 1)
        dec_scr[...] = jnp.broadcast_to(dec, dec_scr.shape)

        @pl.when(pl.program_id(1) == 0)
        def _():
            dstate[...] = jnp.zeros_like(dstate)

        for c in range(n_chunks):
            cc = c if reverse else n_chunks - 1 - c
            dsn_scr[cc] = dstate[...]
            dstate[...] = dstate[...] * dec_scr[cc, 0:1] + g_scr[cc]
        dsn = dsn_scr[...]
        dsnb = dsn.astype(BF16)
        dv = dv + _bdot(ksb, dsnb, 2, 2)
        d_kst = _bdot(vb, dsnb, 2, 1)
        d_dec = jnp.sum(dsn * st, axis=1, keepdims=True)
        ks_term = d_kst * k_st
        d_btot = d_dec * dec + jnp.sum(ks_term, axis=1, keepdims=True)
        d_b = d_qin * q_in - d_kin * k_in - ks_term
        pos = lax.broadcasted_iota(jnp.int32, d_b.shape, 1)
        edge = 0 if reverse else GLA_CHUNK - 1
        d_b = d_b + jnp.where(pos == edge, d_btot, 0.0)
        dg = _chunk_cumsum(d_b.reshape(rows, GLA_DK), not reverse)
        dz_ref[...] = dg * (1.0 / GLA_NORMALIZER) * _sigmoid(-z)
        dq = (d_qin * f_q).reshape(rows, GLA_DK)
        dk = (d_kin * f_k + d_kst * f_s).reshape(rows, GLA_DK)
        dv = dv.reshape(rows, GLA_DV)
        if add:
            dq_ref[...] = (dq + aq_ref[...]).astype(BF16)
            dk_ref[...] = (dk + ak_ref[...]).astype(BF16)
            dv_ref[...] = (dv + av_ref[...]).astype(BF16)
        else:
            dq_ref[...] = dq
            dk_ref[...] = dk
            dv_ref[...] = dv

    qkv_specs = [pl.BlockSpec((rows, GLA_DK), lambda h, s: (tix(s), h)),
                 pl.BlockSpec((rows, GLA_DK), lambda h, s: (tix(s), h)),
                 pl.BlockSpec((rows, GLA_DV), lambda h, s: (tix(s), h))]
    in_specs = specs + [pl.BlockSpec((rows, GLA_DV), lambda h, s: (tix(s), h)),
                        pl.BlockSpec((1, n_chunks, GLA_DV, GLA_DK), lambda h, s: (h, tix(s), 0, 0))]
    args = [proj, proj, proj, proj, wg_pad, bg, d_o, states]
    out_dtype = F32
    if add:
        in_specs += qkv_specs
        args += list(dqkv_in)
        out_dtype = BF16
    return pl.pallas_call(
        body, name=name,
        out_shape=(jax.ShapeDtypeStruct((rows_total, GLA_HEADS * GLA_DK), out_dtype),
                   jax.ShapeDtypeStruct((rows_total, GLA_HEADS * GLA_DK), out_dtype),
                   jax.ShapeDtypeStruct((rows_total, GLA_HEADS * GLA_DV), out_dtype),
                   jax.ShapeDtypeStruct((rows_total, GLA_HEADS * GLA_DK), F32)),
        grid=(GLA_HEADS, n_blocks),
        in_specs=in_specs,
        out_specs=(pl.BlockSpec((rows, GLA_DK), lambda h, s: (tix(s), h)),
                   pl.BlockSpec((rows, GLA_DK), lambda h, s: (tix(s), h)),
                   pl.BlockSpec((rows, GLA_DV), lambda h, s: (tix(s), h)),
                   pl.BlockSpec((rows, GLA_DK), lambda h, s: (tix(s), h))),
        scratch_shapes=[pltpu.VMEM((GLA_DV, GLA_DK), F32), pltpu.VMEM((n_chunks, GLA_DV, GLA_DK), F32),
                        pltpu.VMEM((n_chunks, SUBLANES, GLA_DK), F32),
                        pltpu.VMEM((n_chunks, GLA_DV, GLA_DK), F32)],
        compiler_params=_params("parallel", "arbitrary"),
    )(*args)


def gla_gate_bwd(proj, dz_f, dz_b, wg_pad, name):
    rows_total = proj.shape[0]
    tm = min(ROW_TILE, rows_total)
    n_key = GLA_HEADS * GLA_DK

    def body(lr_ref, dzf_ref, dzb_ref, wg_ref, dlr_ref, dwg_ref, dbg_ref):
        step = pl.program_id(0)
        lr_t = jnp.transpose(lr_ref[...])
        dzf, dzb = dzf_ref[...], dzb_ref[...]
        dzf16, dzb16 = dzf.astype(BF16), dzb.astype(BF16)
        dlr_ref[...] = (_dot_nt(dzf16, wg_ref[0]) + _dot_nt(dzb16, wg_ref[1])).astype(BF16)
        dwf = _dot(lr_t[0:GLA_RANK].astype(BF16), dzf16)
        dwb = _dot(lr_t[GLA_RANK:2 * GLA_RANK].astype(BF16), dzb16)
        dbg = jnp.concatenate([_colsum(dzf), _colsum(dzb)], axis=0)

        @pl.when(step == 0)
        def _():
            dwg_ref[0] = dwf
            dwg_ref[1] = dwb
            dbg_ref[...] = dbg

        @pl.when(step > 0)
        def _():
            dwg_ref[0] += dwf
            dwg_ref[1] += dwb
            dbg_ref[...] += dbg

    return pl.pallas_call(
        body, name=name,
        out_shape=(jax.ShapeDtypeStruct((rows_total, LANES), BF16), jax.ShapeDtypeStruct((2, GLA_RANK, n_key), F32),
                   jax.ShapeDtypeStruct((2, n_key), F32)),
        grid=(rows_total // tm,),
        in_specs=[pl.BlockSpec((tm, LANES), lambda i: (i, LR_COL // LANES)),
                  pl.BlockSpec((tm, n_key), lambda i: (i, 0)), pl.BlockSpec((tm, n_key), lambda i: (i, 0)),
                  _full((2, LANES, n_key))],
        out_specs=(pl.BlockSpec((tm, LANES), lambda i: (i, 0)), _full((2, GLA_RANK, n_key)), _full((2, n_key))),
        compiler_params=_params("arbitrary"),
    )(proj, dz_f, dz_b, wg_pad)


def _head_norm(o, gain):
    outs, hats, rstds = [], [], []
    for h in range(GLA_HEADS):
        oh = o[:, h * GLA_DV:(h + 1) * GLA_DV]
        rstd = lax.rsqrt(jnp.mean(oh * oh, axis=-1, keepdims=True) + NORM_EPS)
        hat = oh * rstd
        outs.append(hat * gain)
        hats.append(hat)
        rstds.append(rstd)
    return outs, hats, rstds


def odd_out_fwd(o_f, o_b, proj, head_gain, w_out, gain, x1, target, name):
    rows, d = x1.shape
    tm = min(ROW_TILE, rows)
    r_block = (2 * GLA_HEADS * GLA_DK + GLA_HEADS * GLA_DV) // d

    def body(of_ref, ob_ref, r_ref, hg_ref, w_ref, g_ref, x1_ref, tgt_ref, y2_ref, dy_ref, dx2_ref, loss_ref,
             dg_ref):
        step = pl.program_id(0)
        on, _, _ = _head_norm(of_ref[...] + ob_ref[...], hg_ref[...])
        r = r_ref[...]
        y2 = (jnp.concatenate(on, axis=1) * (r * _sigmoid(r))).astype(BF16)
        y2_ref[...] = y2
        y = _dot(y2, w_ref[...])
        gain_v = g_ref[...]
        rstd = lax.rsqrt(jnp.mean(y * y, axis=-1, keepdims=True) + NORM_EPS)
        x2 = x1_ref[...] + y * rstd * gain_v
        diff = x2 - tgt_ref[...]
        loss = 0.5 * jnp.sum(jnp.mean(diff * diff, axis=-1, keepdims=True), axis=0, keepdims=True)
        dx2 = diff * (1.0 / d)
        dx2_ref[...] = dx2
        dy, dg_rows = _rmsnorm_bwd(dx2, y, gain_v)
        dy_ref[...] = dy.astype(BF16)
        _accumulate(loss_ref, jnp.broadcast_to(loss, loss_ref.shape), step)
        _accumulate(dg_ref, _colsum(dg_rows), step)

    row = lambda n, col=0: pl.BlockSpec((tm, n), lambda i: (i, col))
    return pl.pallas_call(
        body, name=name,
        out_shape=(jax.ShapeDtypeStruct((rows, d), BF16), jax.ShapeDtypeStruct((rows, d), BF16),
                   jax.ShapeDtypeStruct((rows, d), F32), jax.ShapeDtypeStruct((SUBLANES, LANES), F32),
                   jax.ShapeDtypeStruct((1, d), F32)),
        grid=(rows // tm,),
        in_specs=[row(d), row(d), row(d, r_block), _full((1, GLA_DV)), _full((d, d)), _full((1, d)), row(d), row(d)],
        out_specs=(row(d), row(d), row(d), _full((SUBLANES, LANES)), _full((1, d))),
        compiler_params=_params("arbitrary"),
    )(o_f, o_b, proj, head_gain, w_out, gain, x1, target)


def odd_out_bwd(dy, w_out, o_f, o_b, proj, head_gain, name):
    rows, d = dy.shape
    tm = min(ROW_TILE, rows)
    r_block = (2 * GLA_HEADS * GLA_DK + GLA_HEADS * GLA_DV) // d

    def body(dy_ref, w_ref, of_ref, ob_ref, r_ref, hg_ref, dr_ref, do_ref, dhg_ref):
        dy2 = _dot_nt(dy_ref[...], w_ref[...])
        hg = hg_ref[...]
        on, hats, rstds = _head_norm(of_ref[...] + ob_ref[...], hg)
        r = r_ref[...]
        sr = _sigmoid(r)
        dr_ref[...] = (dy2 * jnp.concatenate(on, axis=1) * (sr * (1.0 + r * (1.0 - sr)))).astype(BF16)
        d_on = dy2 * (r * sr)
        d_os, dhg = [], None
        for h in range(GLA_HEADS):
            dn = d_on[:, h * GLA_DV:(h + 1) * GLA_DV]
            part = _colsum(dn * hats[h])
            dhg = part if dhg is None else dhg + part
            dng = dn * hg
            d_os.append(rstds[h] * (dng - hats[h] * jnp.mean(dng * hats[h], axis=-1, keepdims=True)))
        do_ref[...] = jnp.concatenate(d_os, axis=1)
        _accumulate(dhg_ref, dhg, pl.program_id(0))

    row = lambda n, col=0: pl.BlockSpec((tm, n), lambda i: (i, col))
    return pl.pallas_call(
        body, name=name,
        out_shape=(jax.ShapeDtypeStruct((rows, d), BF16), jax.ShapeDtypeStruct((rows, d), F32),
                   jax.ShapeDtypeStruct((1, GLA_DV), F32)),
        grid=(rows // tm,),
        in_specs=[row(d), _full((d, d)), row(d), row(d), row(d, r_block), _full((1, GLA_DV))],
        out_specs=(row(d), row(d), _full((1, GLA_DV))),
        compiler_params=_params("arbitrary"),
    )(dy, w_out, o_f, o_b, proj, head_gain)


def local_step(x, target, w, reduce_first=None, reduce_second=None, late_weights=None):
    g = {}
    proj_e, h0 = norm_matmul(x, w["even_norm_pre"], w["even_w_in"], "even_in_proj")
    h_dir = [rglru_fwd(proj_e, w["rg_conv_w"], w["rg_conv_b"], w["rg_gate_w"][d], w["rg_gate_b"][d],
                       w["rg_lambda"][d], d == 1, "rglru_fwd_%d" % d) for d in range(2)]
    ycat = even_mix_fwd(proj_e, h_dir[0], h_dir[1], w["sc_conv_w"], "even_mix_fwd")
    if late_weights is not None:
        w = dict(w, **late_weights(ycat))
    x1, y_e = even_out_fwd(ycat, w["even_w_out"], w["even_norm_post"], x, "even_out_fwd")
    proj_o, h1 = norm_matmul(x1, w["odd_norm_pre"], w["odd_w_in"], "odd_in_proj")
    o_dir, st_dir = [], []
    for d in range(2):
        o, st = gla_fwd(proj_o, w["gla_wg_pad"], w["gla_b_gate"], d == 1, "gla_fwd_%d" % d)
        o_dir.append(o)
        st_dir.append(st)
    y2, dy_o, dx2, loss, g["odd_norm_post"] = odd_out_fwd(
        o_dir[0], o_dir[1], proj_o, w["gla_norm_g"], w["odd_w_out"], w["odd_norm_post"], x1, target, "odd_out_fwd")
    g["odd_w_out"] = matmul_dw(y2, dy_o, D_MODEL, "odd_w_out_grad")[0]
    dr, d_o, g["gla_norm_g"] = odd_out_bwd(dy_o, w["odd_w_out"], o_dir[0], o_dir[1], proj_o, w["gla_norm_g"],
                                           "odd_out_bwd")
    dq, dk, dv, dz_f = gla_bwd(proj_o, w["gla_wg_pad"], w["gla_b_gate"], d_o, st_dir[0], None, False, "gla_bwd_0")
    dq, dk, dv, dz_b = gla_bwd(proj_o, w["gla_wg_pad"], w["gla_b_gate"], d_o, st_dir[1], (dq, dk, dv), True,
                               "gla_bwd_1")
    dlr, g["gla_w_gate_lr"], g["gla_b_gate"] = gla_gate_bwd(proj_o, dz_f, dz_b, w["gla_wg_pad"], "gla_gate_bwd")
    dproj_o = [dq, dk, dv, dr, dlr]
    g["odd_w_in"] = jnp.concatenate(matmul_dw_pieces(h1, dproj_o, "odd_w_in_grad"), axis=1)[:, :ODD_IN]
    dx1, g["odd_norm_pre"] = inproj_bwd_pieces(dproj_o, w["odd_w_in"], x1, w["odd_norm_pre"], dx2, "odd_in_proj_bwd")
    dy_e, dycat, g["even_norm_post"] = even_out_bwd(dx1, y_e, w["even_norm_post"], w["even_w_out"], "even_out_bwd")
    g["even_w_out"] = matmul_dw(ycat, dy_e, D_MODEL, "even_w_out_grad")[0]
    conv_b = w["rg_conv_b"] if reduce_first is None else w["rg_conv_b"] + reduce_first(g)
    dua, dgw, dgb, dlam = [], [], [], []
    for d in range(2):
        a, b, c, e = rglru_bwd(proj_e, dycat, h_dir[d], w["rg_conv_w"], conv_b, w["rg_gate_w"][d],
                               w["rg_gate_b"][d], w["rg_lambda"][d], d == 1, "rglru_bwd_%d" % d)
        dua.append(a)
        dgw.append(b)
        dgb.append(c)
        dlam.append(e)
    dproj_e, g["rg_conv_w"], g["rg_conv_b"], g["sc_conv_w"] = even_mix_bwd(
        proj_e, dycat, h_dir[0], h_dir[1], dua[0], dua[1], w["rg_conv_w"], w["sc_conv_w"], "even_mix_bwd")
    dgw = jnp.stack(dgw).reshape(2, RG_HEADS, RG_HEAD_DIM, 2, RG_HEAD_DIM)
    g["rg_gate_w"] = jnp.transpose(dgw, (0, 3, 1, 2, 4))
    g["rg_gate_b"] = jnp.stack(dgb).reshape(2, 2, RG_HEADS, RG_HEAD_DIM)
    g["rg_lambda"] = jnp.concatenate(dlam, axis=0)
    g["even_w_in"] = matmul_dw(h0, dproj_e, EVEN_IN // 4, "even_w_in_grad")
    gain = w["even_norm_pre"] if reduce_second is None else w["even_norm_pre"] + reduce_second(g)
    grad_x, g["even_norm_pre"] = inproj_bwd(dproj_e, w["even_w_in"], x, gain, dx1, "even_in_proj_bwd")
    return loss, grad_x, g


def _prepare_weights(full):
    w = {}
    for name in ("even_norm_pre", "even_norm_post", "rg_conv_b", "odd_norm_pre", "odd_norm_post", "gla_norm_g"):
        if name in full:
            w[name] = full[name].reshape(1, -1)
    for name in ("rg_conv_w", "sc_conv_w"):
        if name in full:
            w[name] = full[name]
    for name in ("even_w_out", "odd_w_out"):
        if name in full:
            w[name] = full[name].astype(BF16)
    if "even_w_in" in full:
        w["even_w_in"] = full["even_w_in"].astype(BF16)
        if w["even_w_in"].ndim == 2:
            w["even_w_in"] = jnp.transpose(w["even_w_in"].reshape(D_MODEL, 4, EVEN_IN // 4), (1, 0, 2))
    if "rg_gate_w" in full:
        gw = jnp.transpose(full["rg_gate_w"].astype(BF16), (0, 2, 3, 1, 4))
        w["rg_gate_w"] = gw.reshape(2, RG_HEADS, RG_HEAD_DIM, 2 * RG_HEAD_DIM)
        w["rg_gate_b"] = full["rg_gate_b"].reshape(2, 2, D_MODEL)
        w["rg_lambda"] = full["rg_lambda"].reshape(2, 1, D_MODEL)
    if "odd_w_in" in full:
        w_in = jnp.pad(full["odd_w_in"].astype(BF16), ((0, 0), (0, ODD_IN_PAD - ODD_IN)))
        w["odd_w_in"] = w_in.reshape(1, D_MODEL, ODD_IN_PAD)
    if "gla_w_gate_lr" in full:
        wg = full["gla_w_gate_lr"].astype(BF16)
        w["gla_wg_pad"] = jnp.stack([jnp.pad(wg[d], ((d * GLA_RANK, LANES - (d + 1) * GLA_RANK), (0, 0)))
                                     for d in range(2)])
        w["gla_b_gate"] = full["gla_b_gate"].reshape(2, 1, GLA_HEADS * GLA_DK)
    return w


SHARDED_SMALL = (("rg_conv_w", (4, 256)), ("rg_lambda", (2, 256)), ("sc_conv_w", (3, 256)),
                 ("odd_norm_pre", (256,)), ("odd_norm_post", (256,)), ("gla_w_gate_lr", (2, 16, 128)),
                 ("gla_b_gate", (2, 128)), ("gla_norm_g", (64,)))
SHARDED_ROWS = 96
REPLICATED = (("even_norm_post", (1024,)), ("rg_conv_b", (1024,)),
              ("rg_gate_b", (2, 2, 8, 128)), ("rg_gate_w", (2, 2, 8, 128, 128)))
LAST_REPLICATED = (("even_norm_pre", (1024,)),)
LAST_ROWS = 8
REPLICATED_ROWS = 4160
REP_PART = REPLICATED_ROWS // 8
HALF_SHARDED = SHARDED_ROWS // 2
PACK_HALF = HALF_SHARDED + REP_PART


def _seg_rows(shape):
    n = 1
    for s in shape:
        n *= s
    return -(-n // (SUBLANES * LANES)) * SUBLANES


def _pack(arrays, spec, total_rows, lead=()):
    parts = []
    for name, shape in spec:
        flat = arrays[name].reshape(lead + (-1,))
        pad = _seg_rows(shape) * LANES - flat.shape[-1]
        if pad:
            flat = jnp.pad(flat, [(0, 0)] * len(lead) + [(0, pad)])
        parts.append(flat.reshape(lead + (-1, LANES)))
    rows = jnp.concatenate(parts, axis=len(lead))
    pad = total_rows - rows.shape[len(lead)]
    return jnp.pad(rows, [(0, 0)] * len(lead) + [(0, pad), (0, 0)])


def _unpack(rows, spec, lead=()):
    out, at = {}, 0
    for name, shape in spec:
        n = 1
        for s in shape:
            n *= s
        k = _seg_rows(shape)
        seg = lax.slice_in_dim(rows, at, at + k, axis=len(lead)).reshape(lead + (-1,))
        out[name] = lax.slice_in_dim(seg, 0, n, axis=len(lead)).reshape(lead + shape)
        at += k
    return out


def _split_owners(arr):
    a = arr.reshape(arr.shape[:-1] + (4, arr.shape[-1] // 4))
    return jnp.moveaxis(a, -2, 0)


def _merge_owners(arr):
    a = jnp.moveaxis(arr, 0, -2)
    return a.reshape(a.shape[:-2] + (-1,))


HBM_SPEC = pl.BlockSpec(memory_space=pltpu.HBM)


def _position():
    x, y, c = lax.axis_index("x"), lax.axis_index("y"), lax.axis_index("c")
    chips = [(1 - x, y), (x, 1 - y), (1 - x, 1 - y)]
    return x, y, c, chips


def _remote(src, dst, send_sem, recv_sem, device):
    return pltpu.make_async_remote_copy(src_ref=src, dst_ref=dst, send_sem=send_sem, recv_sem=recv_sem,
                                        device_id=device, device_id_type=MESH)


SEM_SPEC = pl.BlockSpec(memory_space=pltpu.SEMAPHORE)
SIDE_EFFECT = pltpu.SideEffectType.DATAFLOW_SIDE_EFFECTING


def _gather_copies(ins, lands, n_h, send_sems, recv_sems):
    x, y, c, chips = _position()
    me = 2 * x + y
    copies = []
    for a in range(len(ins)):
        for k, chip in enumerate(chips):
            src = ins[a].at[c] if a < n_h else ins[a]
            dst = lands[a].at[me, c] if a < n_h else lands[a].at[me]
            copies.append(_remote(src, dst, send_sems.at[3 * a + k], recv_sems.at[3 * a + k], (chip[0], chip[1], c)))
    return copies


def gather_start(halved, whole, name):
    arrays = list(halved) + list(whole)
    n, n_h = len(arrays), len(halved)
    lands = [lax.empty((4,) + a.shape, a.dtype) for a in arrays]

    def body(*refs):
        ins, lz, send_sems, recv_sems, token = refs[:n], refs[n:2 * n], refs[2 * n], refs[2 * n + 1], refs[-1]
        for cp in _gather_copies(ins, lz, n_h, send_sems, recv_sems):
            cp.start()
        token[...] = jnp.zeros_like(token)

    operands = [pltpu.with_memory_space_constraint(a, pltpu.HBM) for a in arrays + lands]
    return pl.pallas_call(
        body, name=name,
        out_shape=(pltpu.SemaphoreType.DMA((3 * n,)), pltpu.SemaphoreType.DMA((3 * n,)))
        + tuple(pltpu.HBM(a.shape, a.dtype) for a in operands) + (jax.ShapeDtypeStruct((SUBLANES, LANES), F32),),
        in_specs=[HBM_SPEC] * (2 * n),
        out_specs=(SEM_SPEC, SEM_SPEC) + (HBM_SPEC,) * (2 * n) + (pl.BlockSpec(memory_space=pltpu.VMEM),),
        input_output_aliases={i: 2 + i for i in range(2 * n)},
        compiler_params=pltpu.CompilerParams(has_side_effects=SIDE_EFFECT),
    )(*operands)


def gather_wait(started, n_h, after, name):
    send_sems, recv_sems = started[0], started[1]
    operands = list(started[2:-1])
    n = len(operands) // 2

    def body(*refs):
        ins, lz, send_ref, recv_ref = refs[:n], refs[n:2 * n], refs[2 * n], refs[2 * n + 1]
        for cp in _gather_copies(ins, lz, n_h, send_ref, recv_ref):
            cp.wait_send()
            cp.wait_recv()

    outs = pl.pallas_call(
        body, name=name,
        out_shape=tuple(pltpu.HBM(a.shape, a.dtype) for a in operands),
        in_specs=[HBM_SPEC] * (2 * n) + [SEM_SPEC, SEM_SPEC, pl.BlockSpec(memory_space=pl.ANY)],
        out_specs=(HBM_SPEC,) * (2 * n),
        input_output_aliases={i: i for i in range(2 * n)},
        compiler_params=pltpu.CompilerParams(has_side_effects=SIDE_EFFECT),
    )(*operands, send_sems, recv_sems, after)
    return outs[n:]


def pass_to_sibling(fulls, name):
    n = len(fulls)

    def body(*refs):
        bufs = refs[n:2 * n]
        send_sems, recv_sems = refs[2 * n:]
        x, y, c, chips = _position()
        sibling = (x, y, 1 - c)
        copies = []
        for a in range(n):
            for k, chip in enumerate(chips):
                q = 2 * chip[0] + chip[1]
                cp = _remote(bufs[a].at[q, c], bufs[a].at[q, c], send_sems.at[3 * a + k], recv_sems.at[3 * a + k],
                             sibling)
                cp.start()
                copies.append(cp)
        for a in range(n):
            for k, chip in enumerate(chips):
                q = 2 * chip[0] + chip[1]
                passed = bufs[a].at[q, 1 - c]
                _remote(passed, passed, send_sems.at[3 * a + k], recv_sems.at[3 * a + k], sibling).wait_recv()
        for cp in copies:
            cp.wait_send()

    return pl.pallas_call(
        body, name=name,
        out_shape=[jax.ShapeDtypeStruct(a.shape, a.dtype) for a in fulls],
        in_specs=[HBM_SPEC] * n, out_specs=[HBM_SPEC] * n,
        input_output_aliases={i: i for i in range(n)},
        scratch_shapes=[pltpu.SemaphoreType.DMA((3 * n,)), pltpu.SemaphoreType.DMA((3 * n,))],
    )(*fulls)


def place_own(full, own, chip, name):
    _, _, r, cols = full.shape
    tr = _row_tile(r, cols)

    def body(p_ref, own_ref, full_ref, o_ref):
        o_ref[0] = own_ref[...]

    return pl.pallas_call(
        body, name=name,
        out_shape=jax.ShapeDtypeStruct(full.shape, full.dtype),
        grid_spec=pltpu.PrefetchScalarGridSpec(
            num_scalar_prefetch=1, grid=(2, r // tr),
            in_specs=[pl.BlockSpec((1, tr, cols), lambda h, i, p_ref: (h, i, 0)), pl.BlockSpec(memory_space=pl.ANY)],
            out_specs=pl.BlockSpec((1, 1, tr, cols), lambda h, i, p_ref: (p_ref[0], h, i, 0))),
        input_output_aliases={2: 0},
        compiler_params=_params("parallel", "parallel"),
    )(chip, own, full)


def exchange_with_sibling(arrays, name):
    n = len(arrays)

    def body(*refs):
        ins, outs = refs[:n], refs[n:2 * n]
        send_sems, recv_sems = refs[2 * n:]
        x, y, c, _ = _position()
        copies = []
        for a in range(n):
            cp = _remote(ins[a].at[:, 1 - c], outs[a], send_sems.at[a], recv_sems.at[a], (x, y, 1 - c))
            cp.start()
            copies.append(cp)
        for cp in copies:
            cp.wait()

    return pl.pallas_call(
        body, name=name,
        out_shape=[jax.ShapeDtypeStruct((a.shape[0],) + a.shape[2:], a.dtype) for a in arrays],
        in_specs=[HBM_SPEC] * n, out_specs=[HBM_SPEC] * n,
        scratch_shapes=[pltpu.SemaphoreType.DMA((n,)), pltpu.SemaphoreType.DMA((n,))],
    )(*arrays)


def _chip_copies(ins, lands, send_sems, recv_sems):
    x, y, c, chips = _position()
    copies = []
    for a in range(len(ins)):
        for k, chip in enumerate(chips):
            q = 2 * chip[0] + chip[1]
            copies.append(_remote(ins[a].at[q], lands[a].at[k], send_sems.at[3 * a + k], recv_sems.at[3 * a + k],
                                  (chip[0], chip[1], c)))
    return copies


def exchange_with_chips_start(arrays, name):
    n = len(arrays)
    lands = [lax.empty((3,) + a.shape[1:], a.dtype) for a in arrays]

    def body(*refs):
        ins, lz, send_sems, recv_sems, token = refs[:n], refs[n:2 * n], refs[2 * n], refs[2 * n + 1], refs[-1]
        for cp in _chip_copies(ins, lz, send_sems, recv_sems):
            cp.start()
        token[...] = jnp.zeros_like(token)

    operands = [pltpu.with_memory_space_constraint(a, pltpu.HBM) for a in list(arrays) + lands]
    return pl.pallas_call(
        body, name=name,
        out_shape=(pltpu.SemaphoreType.DMA((3 * n,)), pltpu.SemaphoreType.DMA((3 * n,)))
        + tuple(pltpu.HBM(a.shape, a.dtype) for a in operands) + (jax.ShapeDtypeStruct((SUBLANES, LANES), F32),),
        in_specs=[HBM_SPEC] * (2 * n),
        out_specs=(SEM_SPEC, SEM_SPEC) + (HBM_SPEC,) * (2 * n) + (pl.BlockSpec(memory_space=pltpu.VMEM),),
        input_output_aliases={i: 2 + i for i in range(2 * n)},
        compiler_params=pltpu.CompilerParams(has_side_effects=SIDE_EFFECT),
    )(*operands)


def exchange_with_chips_wait(started, after, name):
    send_sems, recv_sems = started[0], started[1]
    operands = list(started[2:-1])
    n = len(operands) // 2

    def body(*refs):
        ins, lz, send_ref, recv_ref = refs[:n], refs[n:2 * n], refs[2 * n], refs[2 * n + 1]
        for cp in _chip_copies(ins, lz, send_ref, recv_ref):
            cp.wait_send()
            cp.wait_recv()

    outs = pl.pallas_call(
        body, name=name,
        out_shape=tuple(pltpu.HBM(a.shape, a.dtype) for a in operands),
        in_specs=[HBM_SPEC] * (2 * n) + [SEM_SPEC, SEM_SPEC, pl.BlockSpec(memory_space=pl.ANY)],
        out_specs=(HBM_SPEC,) * (2 * n),
        input_output_aliases={i: i for i in range(2 * n)},
        compiler_params=pltpu.CompilerParams(has_side_effects=SIDE_EFFECT),
    )(*operands, send_sems, recv_sems, after)
    return outs[:n], outs[n:]


def share_totals(totals, pack_total, last_part):
    arrays = list(totals) + [pack_total]
    n = len(arrays)

    def body(*refs):
        ins, last, outs, rep, last_all = refs[:n], refs[n], refs[n + 1:2 * n + 1], refs[2 * n + 1], refs[2 * n + 2]
        send_sems, recv_sems, rep_send, rep_recv, last_send, last_recv = refs[2 * n + 3:]
        x, y, c, chips = _position()
        sibling = (x, y, 1 - c)
        me = 4 * x + 2 * y + c
        sends = []
        for a in range(n):
            cp = _remote(ins[a], outs[a], send_sems.at[a], recv_sems.at[a], sibling)
            cp.start()
            sends.append(cp)
        mine = ins[n - 1].at[pl.ds(HALF_SHARDED, REP_PART)]
        peers = [sibling]
        for chip in chips:
            peers += [(chip[0], chip[1], c), (chip[0], chip[1], 1 - c)]
        for j, peer in enumerate(peers):
            for src, dst, s_sem, r_sem in ((mine, rep, rep_send, rep_recv), (last, last_all, last_send, last_recv)):
                cp = _remote(src, dst.at[me], s_sem.at[j], r_sem.at[j], peer)
                cp.start()
                sends.append(cp)
        for a in range(n):
            _remote(outs[a], outs[a], send_sems.at[a], recv_sems.at[a], sibling).wait_recv()
        for j, peer in enumerate(peers):
            it = 4 * peer[0] + 2 * peer[1] + peer[2]
            _remote(rep.at[it], rep.at[it], rep_send.at[j], rep_recv.at[j], peer).wait_recv()
            _remote(last_all.at[it], last_all.at[it], last_send.at[j], last_recv.at[j], peer).wait_recv()
        for cp in sends:
            cp.wait_send()

    outs = pl.pallas_call(
        body, name="grad_share_totals",
        out_shape=[jax.ShapeDtypeStruct(a.shape, a.dtype) for a in arrays]
        + [jax.ShapeDtypeStruct((8, REP_PART, LANES), F32), jax.ShapeDtypeStruct((8, LAST_ROWS, LANES), F32)],
        in_specs=[HBM_SPEC] * (n + 1), out_specs=[HBM_SPEC] * (n + 2),
        scratch_shapes=[pltpu.SemaphoreType.DMA((n,)), pltpu.SemaphoreType.DMA((n,))]
        + [pltpu.SemaphoreType.DMA((7,))] * 4,
    )(*arrays, last_part)
    return outs[:n], outs[n], outs[n + 1]


def sum_parts(parts, name):
    def body(p_ref, o_ref):
        total = p_ref[0]
        for k in range(1, parts.shape[0]):
            total = total + p_ref[k]
        o_ref[...] = total

    return pl.pallas_call(body, name=name, out_shape=jax.ShapeDtypeStruct(parts.shape[1:], parts.dtype))(parts)


TILE_BYTES = 1 << 20


def _row_tile(rows, cols):
    best = None
    for t in range(SUBLANES, rows + 1, SUBLANES):
        if rows % t == 0 and t * cols * 4 <= TILE_BYTES:
            best = t
    return best if best is not None else rows


def add_sibling(mine, received, core, out_dtype, name):
    _, _, r, cols = mine.shape
    tr = _row_tile(r, cols)

    def body(c_ref, a_ref, b_ref, o_ref):
        o_ref[...] = (a_ref[0] + b_ref[...]).astype(out_dtype)

    return pl.pallas_call(
        body, name=name,
        out_shape=jax.ShapeDtypeStruct((4, r, cols), out_dtype),
        grid_spec=pltpu.PrefetchScalarGridSpec(
            num_scalar_prefetch=1, grid=(4, r // tr),
            in_specs=[pl.BlockSpec((1, 1, tr, cols), lambda o, i, c_ref: (o, c_ref[0], i, 0)),
                      pl.BlockSpec((1, tr, cols), lambda o, i, c_ref: (o, i, 0))],
            out_specs=pl.BlockSpec((1, tr, cols), lambda o, i, c_ref: (o, i, 0))),
        compiler_params=_params("parallel", "parallel"),
    )(core, mine, received)


def add_chips(own, received, chip, name):
    _, r, cols = own.shape
    tr = _row_tile(r, cols)

    def body(p_ref, a_ref, b0, b1, b2, o_ref):
        o_ref[...] = ((a_ref[0].astype(F32) + b0[0].astype(F32)) + b1[0].astype(F32)) + b2[0].astype(F32)

    rb = lambda k: pl.BlockSpec((1, tr, cols), lambda i, p_ref: (k, i, 0))
    return pl.pallas_call(
        body, name=name,
        out_shape=jax.ShapeDtypeStruct((r, cols), F32),
        grid_spec=pltpu.PrefetchScalarGridSpec(
            num_scalar_prefetch=1, grid=(r // tr,),
            in_specs=[pl.BlockSpec((1, tr, cols), lambda i, p_ref: (p_ref[0], i, 0)), rb(0), rb(1), rb(2)],
            out_specs=pl.BlockSpec((tr, cols), lambda i, p_ref: (i, 0))),
        compiler_params=_params("parallel"),
    )(chip, own, received, received, received)


def _adamw_update(gv, w_ref, m_ref, v_ref, d_ref, nm_ref, nv_ref):
    nm = ADAM_B1 * m_ref[...] + (1.0 - ADAM_B1) * gv
    nv = ADAM_B2 * v_ref[...] + (1.0 - ADAM_B2) * (gv * gv)
    nm_ref[...] = nm
    nv_ref[...] = nv
    m_hat = nm / (1.0 - ADAM_B1 ** ADAM_STEP)
    v_hat = nv / (1.0 - ADAM_B2 ** ADAM_STEP)
    d_ref[...] = -ADAM_LR * (m_hat / (jnp.sqrt(v_hat) + ADAM_EPS) + ADAM_WD * w_ref[...])


def adamw_halves(w, own, received, m, v, core, name):
    rows, cols = w.shape
    r = rows // 2
    tr = _row_tile(r, cols)
    nr = r // tr

    def body(c_ref, w_ref, own_ref, rec_ref, m_ref, v_ref, g_ref, d_ref, nm_ref, nv_ref):
        gv = jnp.where(pl.program_id(0) == c_ref[0], own_ref[...], rec_ref[...])
        g_ref[...] = gv
        _adamw_update(gv, w_ref, m_ref, v_ref, d_ref, nm_ref, nv_ref)

    whole = pl.BlockSpec((tr, cols), lambda h, i, c_ref: (h * nr + i, 0))
    half = pl.BlockSpec((tr, cols), lambda h, i, c_ref: (i, 0))
    return pl.pallas_call(
        body, name=name,
        out_shape=(jax.ShapeDtypeStruct((rows, cols), F32),) * 4,
        grid_spec=pltpu.PrefetchScalarGridSpec(
            num_scalar_prefetch=1, grid=(2, nr),
            in_specs=[whole, half, half, whole, whole], out_specs=(whole,) * 4),
        compiler_params=_params("parallel", "parallel"),
    )(core, w, own, received, m, v)


def adamw(w, g, m, v, name):
    r, cols = w.shape
    tr = _row_tile(r, cols)

    def body(w_ref, g_ref, m_ref, v_ref, d_ref, nm_ref, nv_ref):
        _adamw_update(g_ref[...], w_ref, m_ref, v_ref, d_ref, nm_ref, nv_ref)

    blk = pl.BlockSpec((tr, cols), lambda i: (i, 0))
    return pl.pallas_call(
        body, name=name,
        out_shape=(jax.ShapeDtypeStruct((r, cols), F32),) * 3,
        grid=(r // tr,),
        in_specs=[blk] * 4, out_specs=(blk,) * 3,
        compiler_params=_params("parallel"),
    )(w, g, m, v)


WEIGHTS = ("even_norm_pre", "even_norm_post", "even_w_in", "rg_conv_w", "rg_conv_b", "rg_gate_w", "rg_gate_b",
           "rg_lambda", "sc_conv_w", "even_w_out", "odd_norm_pre", "odd_norm_post", "odd_w_in", "gla_w_gate_lr",
           "gla_b_gate", "gla_norm_g", "odd_w_out")
BIG = ("even_w_in", "even_w_out", "odd_w_in", "odd_w_out")


def _halves(a):
    return a.reshape((2, a.shape[0] // 2) + a.shape[1:])


def kernel(x, even_norm_pre, even_norm_post, even_w_in, rg_conv_w, rg_conv_b, rg_gate_w, rg_gate_b, rg_lambda, sc_conv_w, even_w_out, odd_norm_pre, odd_norm_post, odd_w_in, gla_w_gate_lr, gla_b_gate, gla_norm_g, odd_w_out, loss_target, m_even_norm_pre, m_even_norm_post, m_even_w_in, m_rg_conv_w, m_rg_conv_b, m_rg_gate_w, m_rg_gate_b, m_rg_lambda, m_sc_conv_w, m_even_w_out, m_odd_norm_pre, m_odd_norm_post, m_odd_w_in, m_gla_w_gate_lr, m_gla_b_gate, m_gla_norm_g, m_odd_w_out, v_even_norm_pre, v_even_norm_post, v_even_w_in, v_rg_conv_w, v_rg_conv_b, v_rg_gate_w, v_rg_gate_b, v_rg_lambda, v_sc_conv_w, v_even_w_out, v_odd_norm_pre, v_odd_norm_post, v_odd_w_in, v_gla_w_gate_lr, v_gla_b_gate, v_gla_norm_g, v_odd_w_out):
    given = dict(locals())
    shard = {n: given[n][0] for n in WEIGHTS}
    m_in = {n: given["m_" + n][0] for n in WEIGHTS}
    v_in = {n: given["v_" + n][0] for n in WEIGHTS}
    mx, my, mc = lax.axis_index("x"), lax.axis_index("y"), lax.axis_index("c")
    core = jnp.reshape(mc, (1,)).astype(jnp.int32)
    chip = jnp.reshape(2 * mx + my, (1,)).astype(jnp.int32)

    small_shard = _pack(shard, SHARDED_SMALL, SHARDED_ROWS)
    big_own = [_halves(shard[n].astype(BF16)) for n in BIG]
    started_a = gather_start(big_own[:1], [small_shard], "gather_start_a")
    started_b = gather_start(big_own[1:], [], "gather_start_b")
    even_w_in_full, small_full = gather_wait(started_a, 1, started_b[-1], "gather_wait_a")
    (even_w_in_full,) = pass_to_sibling([even_w_in_full], "gather_pass_a")
    even_w_in_full = place_own(even_w_in_full, big_own[0], chip, "place_even_w_in")
    small_full = lax.dynamic_update_slice(small_full, small_shard[None], (chip[0], 0, 0))
    full = {n: shard[n] for n, _ in REPLICATED + LAST_REPLICATED}
    full.update({n: _merge_owners(a) for n, a in _unpack(small_full, SHARDED_SMALL, lead=(4,)).items()})
    full["even_w_in"] = even_w_in_full.reshape(4, D_MODEL, EVEN_IN // 4)

    def late_weights(after):
        lands = pass_to_sibling(list(gather_wait(started_b, 3, after, "gather_wait_b")), "gather_pass_b")
        lands = [place_own(a, b, chip, "place_" + n) for a, b, n in zip(lands, big_own[1:], BIG[1:])]
        odd_w_in = jnp.transpose(lands[1].reshape(4, D_MODEL, ODD_IN // 4), (1, 0, 2)).reshape(D_MODEL, ODD_IN)
        return _prepare_weights({"even_w_out": lands[0].reshape(2 * D_MODEL, D_MODEL), "odd_w_in": odd_w_in,
                                 "odd_w_out": lands[2].reshape(D_MODEL, D_MODEL)})

    pending = {}

    def slab(a):
        return a.reshape((4, 2, a.shape[1] // 2) + a.shape[2:])

    def begin(tag, slabs, dtypes):
        got = exchange_with_sibling(slabs, "grad_sibling_" + tag)
        sums = [add_sibling(a, b, core, dt, "grad_add_sibling_%s%d" % (tag, i))
                for i, (a, b, dt) in enumerate(zip(slabs, got, dtypes))]
        pending[tag] = exchange_with_chips_start(sums, "grad_chips_start_" + tag)
        return pending[tag][-1][0, 0]

    def finish(tag, after):
        sums, got = exchange_with_chips_wait(pending[tag], after, "grad_chips_wait_" + tag)
        return [add_chips(a, b, chip, "grad_add_chips_%s%d" % (tag, i)) for i, (a, b) in enumerate(zip(sums, got))]

    def reduce_first(g):
        return begin("a", [slab(jnp.transpose(g["odd_w_in"].reshape(D_MODEL, 4, ODD_IN // 4), (1, 0, 2))),
                           slab(g["odd_w_out"].reshape(4, D_MODEL // 4, D_MODEL)),
                           slab(g["even_w_out"].reshape(4, D_MODEL // 2, D_MODEL))], [BF16] * 3)

    def reduce_second(g):
        pending["totals_a"] = finish("a", g["even_w_in"])
        rep_rows = _pack(g, REPLICATED, REPLICATED_ROWS).reshape(4, 2, REP_PART, LANES)
        sh_rows = _pack({n: _split_owners(g[n]) for n, _ in SHARDED_SMALL}, SHARDED_SMALL, SHARDED_ROWS, lead=(4,))
        pack = jnp.concatenate([sh_rows.reshape(4, 2, HALF_SHARDED, LANES), rep_rows], axis=2)
        return begin("b", [slab(g["even_w_in"]), pack], [BF16, F32])

    loss, grad_x, g = local_step(x[0], loss_target[0], _prepare_weights(full), reduce_first, reduce_second,
                                 late_weights)
    loss = lax.psum(loss[0, 0], ("x", "y", "c"))
    odd_w_in_t, odd_w_out_t, even_w_out_t = pending["totals_a"]
    even_w_in_t, pack_t = finish("b", grad_x)
    totals = [even_w_in_t, even_w_out_t, odd_w_in_t, odd_w_out_t]
    last_part = _pack(g, LAST_REPLICATED, LAST_ROWS)
    from_core, rep_all, last_all = share_totals(totals, pack_t, last_part)
    me = 2 * chip[0] + core[0]
    mine, theirs = pack_t[:HALF_SHARDED], from_core[4][:HALF_SHARDED]
    sh_total = jnp.where(mc == 0, jnp.concatenate([mine, theirs]), jnp.concatenate([theirs, mine]))
    rep_all = lax.dynamic_update_slice(rep_all, pack_t[None, HALF_SHARDED:], (me, 0, 0))
    rep_total = rep_all.reshape(REPLICATED_ROWS, LANES)
    last_total = sum_parts(lax.dynamic_update_slice(last_all, last_part[None], (me, 0, 0)), "grad_sum_last")
    grads = {}
    grads.update(_unpack(sh_total, SHARDED_SMALL))
    grads.update(_unpack(rep_total, REPLICATED))
    grads.update(_unpack(last_total, LAST_REPLICATED))

    delta, new_m, new_v = {}, {}, {}
    for i, n in enumerate(BIG):
        grads[n], delta[n], new_m[n], new_v[n] = adamw_halves(shard[n], totals[i], from_core[i], m_in[n], v_in[n],
                                                              core, "adamw_" + n)
    small = ((SHARDED_SMALL, SHARDED_ROWS), (REPLICATED, REPLICATED_ROWS), (LAST_REPLICATED, LAST_ROWS))
    packed = [jnp.concatenate([_pack(src, spec, rows) for spec, rows in small]) for src in (shard, m_in, v_in)]
    small_g = jnp.concatenate([sh_total, rep_total, last_total], axis=0)
    outs = adamw(packed[0], small_g, packed[1], packed[2], "adamw_small")
    for dst, packed_rows in zip((delta, new_m, new_v), outs):
        at = 0
        for spec, rows in small:
            dst.update(_unpack(packed_rows[at:at + rows], spec))
            at += rows
    result = [loss, grad_x[None]]
    for group in (grads, delta, new_m, new_v):
        result += [group[n].reshape(given[n].shape) for n in WEIGHTS]
    return tuple(result)
```

```python
import functools

import jax
import jax.numpy as jnp
from jax import lax
from jax.experimental import pallas as pl
from jax.experimental.pallas import tpu as pltpu

F32 = jnp.float32
BF16 = jnp.bfloat16
MESH = pl.DeviceIdType.MESH

D_MODEL = 1024
NORM_EPS = 1e-6
RG_HEADS = 8
RG_HEAD_DIM = 128
RG_C = 8.0
EVEN_IN = 6144
ODD_IN = 3104
ODD_IN_PAD = 3200
GLA_HEADS = 4
GLA_DK = 128
GLA_DV = 256
GLA_RANK = 16
GLA_NORMALIZER = 16.0
GLA_CHUNK = 128
LR_COL = 3072

ADAM_LR = 0.001
ADAM_B1 = 0.9
ADAM_B2 = 0.999
ADAM_EPS = 1e-08
ADAM_WD = 0.01
ADAM_STEP = 10

SUBLANES = 8
LANES = 128
VMEM_LIMIT = 56 * 2 ** 20

ROW_TILE = 512
SCAN_TILE = 256
GLA_BLOCK = 1024
MIX_TILE = 128


def _params(*sem):
    return pltpu.CompilerParams(dimension_semantics=sem, vmem_limit_bytes=VMEM_LIMIT)


def _full(shape):
    n = len(shape)
    return pl.BlockSpec(shape, lambda *_: (0,) * n)


def _sigmoid(x):
    return 0.5 + 0.5 * jnp.tanh(0.5 * x)


def _softplus(x):
    return jnp.maximum(x, 0.0) + jnp.log(1.0 + jnp.exp(-jnp.abs(x)))


def _dot(a, b):
    return jnp.dot(a, b, preferred_element_type=F32)


def _dot_nt(a, b):
    return lax.dot_general(a, b, (((1,), (1,)), ((), ())), preferred_element_type=F32)


def _dot_tn(a, b):
    return lax.dot_general(a, b, (((0,), (0,)), ((), ())), preferred_element_type=F32)


def _bdot(a, b, ca, cb):
    return lax.dot_general(a, b, (((ca,), (cb,)), ((0,), (0,))), preferred_element_type=F32)


def _halo_specs(rows, cols, col_block, n_row_tiles, tix):
    per = rows // SUBLANES
    last = n_row_tiles * per - 1

    def split(args):
        if len(args) == 2:
            return tix(args[1]), col_block + args[0]
        return tix(args[0]), col_block

    def prev(*args):
        t, c = split(args)
        return (jnp.maximum(t * per - 1, 0), c)

    def main(*args):
        return split(args)

    def nxt(*args):
        t, c = split(args)
        return (jnp.minimum((t + 1) * per, last), c)

    return [pl.BlockSpec((SUBLANES, cols), prev), pl.BlockSpec((rows, cols), main),
            pl.BlockSpec((SUBLANES, cols), nxt)]


def _extend(prev_ref, main_ref, next_ref, is_first, is_last):
    p = jnp.where(is_first, 0.0, prev_ref[...])
    n = jnp.where(is_last, 0.0, next_ref[...])
    return jnp.concatenate([p, main_ref[...], n], axis=0)


def _shifted(ext, offset, rows):
    if offset == 0:
        return ext[SUBLANES:SUBLANES + rows]
    n = ext.shape[0]
    return pltpu.roll(ext, (-offset) % n, 0)[SUBLANES:SUBLANES + rows]


def _conv(ext, w, left, rows):
    out = None
    for k in range(w.shape[0]):
        term = _shifted(ext, k - left, rows) * w[k:k + 1]
        out = term if out is None else out + term
    return out


def _conv_transpose(ext, w, left, rows):
    out = None
    for k in range(w.shape[0]):
        term = _shifted(ext, left - k, rows) * w[k:k + 1]
        out = term if out is None else out + term
    return out


def _colsum(x):
    return jnp.sum(x, axis=0, keepdims=True)


def _accumulate(ref, value, step):
    @pl.when(step == 0)
    def _():
        ref[...] = value

    @pl.when(step > 0)
    def _():
        ref[...] += value


PROJ_TILE_BYTES = 7 * 2 ** 20


def _proj_row_tile(rows, width):
    tm = min(ROW_TILE, rows)
    while tm * width * 4 > PROJ_TILE_BYTES and tm % (2 * SUBLANES) == 0:
        tm //= 2
    return tm


def norm_matmul(x, gain, w, name):
    rows, d = x.shape
    n_col_tiles, _, tn = w.shape
    tm = _proj_row_tile(rows, n_col_tiles * tn)

    def body(x_ref, g_ref, w_ref, proj_ref, h_ref):
        xv = x_ref[...]
        rstd = lax.rsqrt(jnp.mean(xv * xv, axis=-1, keepdims=True) + NORM_EPS)
        hv = (xv * rstd * g_ref[...]).astype(BF16)
        h_ref[...] = hv
        for j in range(n_col_tiles):
            proj_ref[:, j * tn:(j + 1) * tn] = _dot(hv, w_ref[j])

    row = lambda cols: pl.BlockSpec((tm, cols), lambda i: (i, 0))
    return pl.pallas_call(
        body, name=name,
        out_shape=(jax.ShapeDtypeStruct((rows, n_col_tiles * tn), F32), jax.ShapeDtypeStruct((rows, d), BF16)),
        grid=(rows // tm,),
        in_specs=[row(d), _full((1, d)), _full(w.shape)],
        out_specs=(row(n_col_tiles * tn), row(d)),
        compiler_params=_params("parallel"),
    )(x, gain, w)


def inproj_bwd(dproj, w, x, gain, dres, name):
    rows, d = x.shape
    n_col_tiles, _, tn = w.shape
    tm = _proj_row_tile(rows, n_col_tiles * tn)

    def body(dp_ref, w_ref, x_ref, g_ref, dres_ref, dx_ref, dg_ref):
        dh = None
        for j in range(n_col_tiles):
            part = _dot_nt(dp_ref[:, j * tn:(j + 1) * tn], w_ref[j])
            dh = part if dh is None else dh + part
        _inproj_finish(dh, x_ref, g_ref, dres_ref, dx_ref, dg_ref, pl.program_id(0))

    row = lambda cols: pl.BlockSpec((tm, cols), lambda i: (i, 0))
    return pl.pallas_call(
        body, name=name,
        out_shape=(jax.ShapeDtypeStruct((rows, d), F32), jax.ShapeDtypeStruct((1, d), F32)),
        grid=(rows // tm,),
        in_specs=[row(n_col_tiles * tn), _full(w.shape), row(d), _full((1, d)), row(d)],
        out_specs=(row(d), _full((1, d))),
        compiler_params=_params("arbitrary"),
    )(dproj, w, x, gain, dres)


def _inproj_finish(dh, x_ref, g_ref, dres_ref, dx_ref, dg_ref, step):
    xv = x_ref[...]
    rstd = lax.rsqrt(jnp.mean(xv * xv, axis=-1, keepdims=True) + NORM_EPS)
    xhat = xv * rstd
    dxn = dh * g_ref[...]
    dx_ref[...] = dres_ref[...] + rstd * (dxn - xhat * jnp.mean(dxn * xhat, axis=-1, keepdims=True))
    _accumulate(dg_ref, _colsum(dh * xhat), step)


def inproj_bwd_pieces(pieces, w, x, gain, dres, name):
    rows, d = x.shape
    tm = min(ROW_TILE, rows)
    n = len(pieces)
    widths = [p.shape[1] for p in pieces]
    starts = [sum(widths[:k]) for k in range(n)]
    assert sum(widths) == w.shape[2]

    def body(*refs):
        w_ref, x_ref, g_ref, dres_ref, dx_ref, dg_ref = refs[n:]
        dh = None
        for k in range(n):
            part = _dot_nt(refs[k][...], w_ref[0, :, starts[k]:starts[k] + widths[k]])
            dh = part if dh is None else dh + part
        _inproj_finish(dh, x_ref, g_ref, dres_ref, dx_ref, dg_ref, pl.program_id(0))

    row = lambda cols: pl.BlockSpec((tm, cols), lambda i: (i, 0))
    return pl.pallas_call(
        body, name=name,
        out_shape=(jax.ShapeDtypeStruct((rows, d), F32), jax.ShapeDtypeStruct((1, d), F32)),
        grid=(rows // tm,),
        in_specs=[row(wd) for wd in widths] + [_full(w.shape), row(d), _full((1, d)), row(d)],
        out_specs=(row(d), _full((1, d))),
        compiler_params=_params("arbitrary"),
    )(*pieces, w, x, gain, dres)


def matmul_dw_pieces(a, pieces, name):
    rows, m = a.shape
    tk = min(2 * ROW_TILE, rows)
    n = len(pieces)

    def body(*refs):
        a_ref, ins, outs = refs[0], refs[1:1 + n], refs[1 + n:]
        av = a_ref[...]
        for k in range(n):
            _accumulate(outs[k], _dot_tn(av, ins[k][...]), pl.program_id(0))

    return pl.pallas_call(
        body, name=name,
        out_shape=[jax.ShapeDtypeStruct((m, p.shape[1]), F32) for p in pieces],
        grid=(rows // tk,),
        in_specs=[pl.BlockSpec((tk, m), lambda k: (k, 0))]
        + [pl.BlockSpec((tk, p.shape[1]), lambda k: (k, 0)) for p in pieces],
        out_specs=[_full((m, p.shape[1])) for p in pieces],
        compiler_params=_params("arbitrary"),
    )(a, *pieces)


def matmul_dw(a, b, bn, name):
    rows, m = a.shape
    n = b.shape[1]
    tk = min(2 * ROW_TILE, rows)
    steps = rows // tk

    def body(a_ref, b_ref, o_ref):
        part = _dot_tn(a_ref[...], b_ref[...])

        @pl.when(pl.program_id(1) == 0)
        def _():
            o_ref[0] = part

        @pl.when(pl.program_id(1) > 0)
        def _():
            o_ref[0] += part

    return pl.pallas_call(
        body, name=name,
        out_shape=jax.ShapeDtypeStruct((n // bn, m, bn), F32),
        grid=(n // bn, steps),
        in_specs=[pl.BlockSpec((tk, m), lambda j, k: (k, 0)), pl.BlockSpec((tk, bn), lambda j, k: (k, j))],
        out_specs=pl.BlockSpec((1, m, bn), lambda j, k: (j, 0, 0)),
        compiler_params=_params("parallel", "arbitrary"),
    )(a, b)


def _scan(a, b, carry, reverse):
    n = a.shape[0]
    pos = lax.broadcasted_iota(jnp.int32, a.shape, 0) % SUBLANES
    s = 1
    while s < SUBLANES:
        if reverse:
            a_s, b_s, valid = pltpu.roll(a, n - s, 0), pltpu.roll(b, n - s, 0), pos < SUBLANES - s
        else:
            a_s, b_s, valid = pltpu.roll(a, s, 0), pltpu.roll(b, s, 0), pos >= s
        b = jnp.where(valid, a * b_s + b, b)
        a = jnp.where(valid, a * a_s, a)
        s *= 2
    blocks = n // SUBLANES
    out = [None] * blocks
    for k in (range(blocks - 1, -1, -1) if reverse else range(blocks)):
        rows = slice(k * SUBLANES, (k + 1) * SUBLANES)
        h = a[rows] * carry + b[rows]
        out[k] = h
        carry = h[0:1] if reverse else h[SUBLANES - 1:SUBLANES]
    return jnp.concatenate(out, axis=0)


def _rg_gates(ua, gw_ref, gb, lam):
    ub = ua.astype(BF16)
    pre_r, pre_i = [], []
    for h in range(RG_HEADS):
        z = _dot(ub[:, h * RG_HEAD_DIM:(h + 1) * RG_HEAD_DIM], gw_ref[h])
        pre_r.append(z[:, :RG_HEAD_DIM])
        pre_i.append(z[:, RG_HEAD_DIM:])
    r = _sigmoid(jnp.concatenate(pre_r, axis=1) + gb[0:1])
    i = _sigmoid(jnp.concatenate(pre_i, axis=1) + gb[1:2])
    sp = _softplus(-lam)
    log_a = -RG_C * r * sp
    a = jnp.exp(log_a)
    mult = jnp.sqrt(1.0 - a * a)
    return r, i, sp, a, mult


def _rg_weight_specs():
    return [_full((4, D_MODEL)), _full((1, D_MODEL)), _full((RG_HEADS, RG_HEAD_DIM, 2 * RG_HEAD_DIM)),
            _full((2, D_MODEL)), _full((1, D_MODEL))]


def rglru_fwd(proj, conv_w, conv_b, gate_w, gate_b, lam, reverse, name):
    rows_total = proj.shape[0]
    rows = min(SCAN_TILE, rows_total)
    n_tiles = rows_total // rows
    tix = (lambda i: n_tiles - 1 - i) if reverse else (lambda i: i)

    def body(xp, xm, xn, cw_ref, cb_ref, gw_ref, gb_ref, lam_ref, h_ref, carry):
        i = pl.program_id(0)
        t = tix(i)
        ext = _extend(xp, xm, xn, t == 0, t == n_tiles - 1)
        ua = _conv(ext, cw_ref[...], 2, rows) + cb_ref[...]
        _, gi, _, a, mult = _rg_gates(ua, gw_ref, gb_ref[...], lam_ref[...])
        b = mult * (gi * ua)

        @pl.when(i == 0)
        def _():
            carry[...] = jnp.zeros_like(carry)

        h = _scan(a, b, carry[0:1], reverse)
        h_ref[...] = h
        edge = h[0:1] if reverse else h[rows - 1:rows]
        carry[...] = jnp.broadcast_to(edge, carry.shape)

    return pl.pallas_call(
        body, name=name,
        out_shape=jax.ShapeDtypeStruct((rows_total, D_MODEL), F32),
        grid=(n_tiles,),
        in_specs=_halo_specs(rows, D_MODEL, 0, n_tiles, tix) + _rg_weight_specs(),
        out_specs=pl.BlockSpec((rows, D_MODEL), lambda i: (tix(i), 0)),
        scratch_shapes=[pltpu.VMEM((SUBLANES, D_MODEL), F32)],
        compiler_params=_params("arbitrary"),
    )(proj, proj, proj, conv_w, conv_b, gate_w, gate_b, lam)


def rglru_bwd(proj, dycat, h_dir, conv_w, conv_b, gate_w, gate_b, lam, reverse, name):
    rows_total = proj.shape[0]
    rows = min(SCAN_TILE, rows_total)
    n_tiles = rows_total // rows
    tix = (lambda i: i) if reverse else (lambda i: n_tiles - 1 - i)
    za_block = 1

    def body(xp, xm, xn, za_ref, dya_ref, hp, hm, hn, cw_ref, cb_ref, gw_ref, gb_ref, lam_ref,
             dua_ref, dgw_ref, dgb_ref, dlam_ref, carry):
        step = pl.program_id(0)
        t = tix(step)
        first, last = t == 0, t == n_tiles - 1
        ext = _extend(xp, xm, xn, first, last)
        ua = _conv(ext, cw_ref[...], 2, rows) + cb_ref[...]
        lam_v = lam_ref[...]
        r, gi, sp, a, mult = _rg_gates(ua, gw_ref, gb_ref[...], lam_v)
        za = za_ref[...]
        dh = dya_ref[...] * (za * _sigmoid(za))

        @pl.when(step == 0)
        def _():
            carry[...] = jnp.zeros_like(carry)

        old = carry[0:1]
        mu = _scan(a, a * dh, old, not reverse)
        row = lax.broadcasted_iota(jnp.int32, mu.shape, 0)
        if reverse:
            mu_next = jnp.where(row == 0, old, pltpu.roll(mu, 1, 0))
            carry[...] = jnp.broadcast_to(mu[rows - 1:rows], carry.shape)
            h_ext = _extend(hp, hm, hn, first, last)
            h_prev = _shifted(h_ext, 1, rows)
        else:
            mu_next = jnp.where(row == rows - 1, old, pltpu.roll(mu, rows - 1, 0))
            carry[...] = jnp.broadcast_to(mu[0:1], carry.shape)
            h_ext = _extend(hp, hm, hn, first, last)
            h_prev = _shifted(h_ext, -1, rows)
        db = dh + mu_next
        da = db * h_prev
        d_mult = db * (gi * ua)
        di = db * (mult * ua)
        dua = db * (mult * gi)
        dlog_a = da * a - d_mult * (a * a) / mult
        dr = dlog_a * (-RG_C * sp)
        dlam = _colsum(dlog_a * (-RG_C * r)) * (-_sigmoid(-lam_v))
        dpr = dr * (r * (1.0 - r))
        dpi = di * (gi * (1.0 - gi))
        dgb = jnp.concatenate([_colsum(dpr), _colsum(dpi)], axis=0)
        ub = ua.astype(BF16)
        dua_heads, dgw_heads = [], []
        for h in range(RG_HEADS):
            cols = slice(h * RG_HEAD_DIM, (h + 1) * RG_HEAD_DIM)
            dz = jnp.concatenate([dpr[:, cols], dpi[:, cols]], axis=1).astype(BF16)
            dgw_heads.append(_dot_tn(ub[:, cols], dz))
            dua_heads.append(_dot_nt(dz, gw_ref[h]))
        dua_ref[...] = dua + jnp.concatenate(dua_heads, axis=1)

        @pl.when(step == 0)
        def _():
            for h in range(RG_HEADS):
                dgw_ref[h] = dgw_heads[h]
            dgb_ref[...] = dgb
            dlam_ref[...] = dlam

        @pl.when(step > 0)
        def _():
            for h in range(RG_HEADS):
                dgw_ref[h] += dgw_heads[h]
            dgb_ref[...] += dgb
            dlam_ref[...] += dlam

    row_spec = lambda col: pl.BlockSpec((rows, D_MODEL), lambda i: (tix(i), col))
    return pl.pallas_call(
        body, name=name,
        out_shape=(jax.ShapeDtypeStruct((rows_total, D_MODEL), F32),
                   jax.ShapeDtypeStruct((RG_HEADS, RG_HEAD_DIM, 2 * RG_HEAD_DIM), F32),
                   jax.ShapeDtypeStruct((2, D_MODEL), F32), jax.ShapeDtypeStruct((1, D_MODEL), F32)),
        grid=(n_tiles,),
        in_specs=(_halo_specs(rows, D_MODEL, 0, n_tiles, tix) + [row_spec(za_block), row_spec(0)]
                  + _halo_specs(rows, D_MODEL, 0, n_tiles, tix) + _rg_weight_specs()),
        out_specs=(row_spec(0), _full((RG_HEADS, RG_HEAD_DIM, 2 * RG_HEAD_DIM)), _full((2, D_MODEL)),
                   _full((1, D_MODEL))),
        scratch_shapes=[pltpu.VMEM((SUBLANES, D_MODEL), F32)],
        compiler_params=_params("arbitrary"),
    )(proj, proj, proj, proj, dycat, h_dir, h_dir, h_dir, conv_w, conv_b, gate_w, gate_b, lam)


def even_mix_fwd(proj, h_f, h_b, sc_w, name):
    rows_total = proj.shape[0]
    rows = min(2 * MIX_TILE, rows_total)
    n_tiles = rows_total // rows
    cb = D_MODEL
    n_cb = 1
    ident = lambda i: i

    def body(za_ref, hf_ref, hb_ref, xbp, xbm, xbn, gcp, gcm, gcn, gb_ref, zb_ref, w_ref, y_ref):
        t = pl.program_id(1)
        first, last = t == 0, t == n_tiles - 1
        za = za_ref[...]
        y_ref[:, 0:cb] = ((hf_ref[...] + hb_ref[...]) * (za * _sigmoid(za))).astype(BF16)
        p_ext = _extend(xbp, xbm, xbn, first, last) * _extend(gcp, gcm, gcn, first, last)
        cv = _conv(p_ext, w_ref[...], 1, rows)
        zb = zb_ref[...]
        y_ref[:, cb:2 * cb] = (gb_ref[...] * cv * (zb * _sigmoid(zb))).astype(BF16)

    blk = lambda col: pl.BlockSpec((rows, cb), lambda c, i: (i, col * n_cb + c))
    own = pl.BlockSpec((rows, cb), lambda c, i: (i, c))
    return pl.pallas_call(
        body, name=name,
        out_shape=jax.ShapeDtypeStruct((rows_total, 2 * D_MODEL), BF16),
        grid=(n_cb, n_tiles),
        in_specs=([blk(1), own, own] + _halo_specs(rows, cb, 2 * n_cb, n_tiles, ident)
                  + _halo_specs(rows, cb, 4 * n_cb, n_tiles, ident)
                  + [blk(3), blk(5), pl.BlockSpec((3, cb), lambda c, i: (0, c))]),
        out_specs=pl.BlockSpec((rows, 2 * cb), lambda c, i: (i, 0)),
        compiler_params=_params("parallel", "arbitrary"),
    )(proj, h_f, h_b, proj, proj, proj, proj, proj, proj, proj, proj, sc_w)


def even_mix_bwd(proj, dycat, h_f, h_b, dua_f, dua_b, conv_w, sc_w, name):
    rows_total = proj.shape[0]
    rows = min(MIX_TILE, rows_total)
    n_tiles = rows_total // rows
    cb = D_MODEL
    n_cb = 1
    ident = lambda i: i

    def body(xap, xam, xan, za_ref, xbp, xbm, xbn, gbp, gbm, gbn, gcp, gcm, gcn, zbp, zbm, zbn,
             dya_ref, dybp, dybm, dybn, hf_ref, hb_ref, dfp, dfm, dfn, dbp, dbm, dbn, cw_ref, sw_ref,
             dp_ref, dcw_ref, dcb_ref, dsw_ref):
        def put(k, value):
            dp_ref[:, k * cb:(k + 1) * cb] = value.astype(BF16)

        t = pl.program_id(1)
        first, last = t == 0, t == n_tiles - 1
        za = za_ref[...]
        sa = _sigmoid(za)
        put(1, dya_ref[...] * (hf_ref[...] + hb_ref[...]) * (sa * (1.0 + za * (1.0 - sa))))
        dua_ext = _extend(dfp, dfm, dfn, first, last) + _extend(dbp, dbm, dbn, first, last)
        cw = cw_ref[...]
        put(0, _conv_transpose(dua_ext, cw, 2, rows))
        dua = dua_ext[SUBLANES:SUBLANES + rows]
        xa_ext = _extend(xap, xam, xan, first, last)
        dcw = jnp.concatenate([_colsum(dua * _shifted(xa_ext, k - 2, rows)) for k in range(4)], axis=0)
        dcb = _colsum(dua)
        xb_ext = _extend(xbp, xbm, xbn, first, last)
        gc_ext = _extend(gcp, gcm, gcn, first, last)
        p_ext = xb_ext * gc_ext
        zb_ext = _extend(zbp, zbm, zbn, first, last)
        sb_ext = _sigmoid(zb_ext)
        dyb_ext = _extend(dybp, dybm, dybn, first, last)
        gb_ext = _extend(gbp, gbm, gbn, first, last)
        dcv_ext = dyb_ext * gb_ext * (zb_ext * sb_ext)
        sw = sw_ref[...]
        cv = _conv(p_ext, sw, 1, rows)
        mid = slice(SUBLANES, SUBLANES + rows)
        zb, sb, dyb, gb = zb_ext[mid], sb_ext[mid], dyb_ext[mid], gb_ext[mid]
        put(3, dyb * cv * (zb * sb))
        put(5, dyb * gb * cv * (sb * (1.0 + zb * (1.0 - sb))))
        dp = _conv_transpose(dcv_ext, sw, 1, rows)
        put(4, dp * xb_ext[mid])
        put(2, dp * gc_ext[mid])
        dcv = dcv_ext[mid]
        dsw = jnp.concatenate([_colsum(dcv * _shifted(p_ext, k - 1, rows)) for k in range(3)], axis=0)

        @pl.when(t == 0)
        def _():
            dcw_ref[...] = dcw
            dcb_ref[...] = dcb
            dsw_ref[...] = dsw

        @pl.when(t > 0)
        def _():
            dcw_ref[...] += dcw
            dcb_ref[...] += dcb
            dsw_ref[...] += dsw

    blk = lambda col: pl.BlockSpec((rows, cb), lambda c, i: (i, col * n_cb + c))
    halo = lambda col: _halo_specs(rows, cb, col * n_cb, n_tiles, ident)
    own = pl.BlockSpec((rows, cb), lambda c, i: (i, c))
    wspec = lambda k: pl.BlockSpec((k, cb), lambda c, i: (0, c))
    return pl.pallas_call(
        body, name=name,
        out_shape=(jax.ShapeDtypeStruct((rows_total, 6 * D_MODEL), BF16),
                   jax.ShapeDtypeStruct((4, D_MODEL), F32), jax.ShapeDtypeStruct((1, D_MODEL), F32),
                   jax.ShapeDtypeStruct((3, D_MODEL), F32)),
        grid=(n_cb, n_tiles),
        in_specs=(halo(0) + [blk(1)] + halo(2) + halo(3) + halo(4) + halo(5) + [blk(0)] + halo(1)
                  + [own, own] + halo(0) + halo(0) + [wspec(4), wspec(3)]),
        out_specs=(pl.BlockSpec((rows, 6 * cb), lambda c, i: (i, 0)), wspec(4), wspec(1), wspec(3)),
        compiler_params=_params("parallel", "arbitrary"),
    )(proj, proj, proj, proj, proj, proj, proj, proj, proj, proj, proj, proj, proj, proj, proj, proj,
      dycat, dycat, dycat, dycat, h_f, h_b, dua_f, dua_f, dua_f, dua_b, dua_b, dua_b, conv_w, sc_w)


def even_out_fwd(ycat, w_out, gain, x, name):
    rows, d = x.shape
    k = ycat.shape[1]
    tm = min(ROW_TILE, rows)

    def body(yc_ref, w_ref, g_ref, x_ref, x1_ref, y_ref):
        y = _dot(yc_ref[...], w_ref[...])
        y_ref[...] = y
        rstd = lax.rsqrt(jnp.mean(y * y, axis=-1, keepdims=True) + NORM_EPS)
        x1_ref[...] = x_ref[...] + y * rstd * g_ref[...]

    row = lambda n: pl.BlockSpec((tm, n), lambda i: (i, 0))
    return pl.pallas_call(
        body, name=name,
        out_shape=(jax.ShapeDtypeStruct((rows, d), F32),) * 2,
        grid=(rows // tm,),
        in_specs=[row(k), _full((k, d)), _full((1, d)), row(d)],
        out_specs=(row(d), row(d)),
        compiler_params=_params("parallel"),
    )(ycat, w_out, gain, x)


def _rmsnorm_bwd(dout, y, gain):
    rstd = lax.rsqrt(jnp.mean(y * y, axis=-1, keepdims=True) + NORM_EPS)
    yhat = y * rstd
    dyn = dout * gain
    dy = rstd * (dyn - yhat * jnp.mean(dyn * yhat, axis=-1, keepdims=True))
    return dy, dout * yhat


def even_out_bwd(dx1, y, gain, w_out, name):
    rows, d = y.shape
    k = w_out.shape[0]
    tm = min(ROW_TILE, rows)

    def body(dx_ref, y_ref, g_ref, w_ref, dy_ref, dyc_ref, dg_ref):
        dy, dg_rows = _rmsnorm_bwd(dx_ref[...], y_ref[...], g_ref[...])
        dyb = dy.astype(BF16)
        dy_ref[...] = dyb
        dyc_ref[...] = _dot_nt(dyb, w_ref[...])
        _accumulate(dg_ref, _colsum(dg_rows), pl.program_id(0))

    row = lambda n: pl.BlockSpec((tm, n), lambda i: (i, 0))
    return pl.pallas_call(
        body, name=name,
        out_shape=(jax.ShapeDtypeStruct((rows, d), BF16), jax.ShapeDtypeStruct((rows, k), F32),
                   jax.ShapeDtypeStruct((1, d), F32)),
        grid=(rows // tm,),
        in_specs=[row(d), row(d), _full((1, d)), _full((k, d))],
        out_specs=(row(d), row(k), _full((1, d))),
        compiler_params=_params("arbitrary"),
    )(dx1, y, gain, w_out)


def _chunk_cumsum(g, reverse):
    n = g.shape[0]
    pos = lax.broadcasted_iota(jnp.int32, g.shape, 0) % GLA_CHUNK
    s = 1
    while s < GLA_CHUNK:
        if reverse:
            g = g + jnp.where(pos < GLA_CHUNK - s, pltpu.roll(g, n - s, 0), 0.0)
        else:
            g = g + jnp.where(pos >= s, pltpu.roll(g, s, 0), 0.0)
        s *= 2
    return g


def _gla_prepare(q_ref, k_ref, lr_ref, wg_ref, bg_ref, reverse, n_chunks):
    z = _dot(lr_ref[...].astype(BF16), wg_ref[0]) + bg_ref[0]
    g = -_softplus(-z) * (1.0 / GLA_NORMALIZER)
    bcum = _chunk_cumsum(g, reverse).reshape(n_chunks, GLA_CHUNK, GLA_DK)
    edge = 0 if reverse else GLA_CHUNK - 1
    btot = bcum[:, edge:edge + 1, :]
    e_pos = jnp.exp(bcum)
    e_neg = jnp.exp(-bcum)
    e_st = jnp.exp(btot - bcum)
    q3 = q_ref[...].reshape(n_chunks, GLA_CHUNK, GLA_DK)
    k3 = k_ref[...].reshape(n_chunks, GLA_CHUNK, GLA_DK)
    scale = GLA_DK ** -0.5
    q_in = q3 * scale * e_pos
    k_in = k3 * e_neg
    k_st = k3 * e_st
    dec = jnp.exp(btot)
    return z, q_in, k_in, k_st, dec, (scale * e_pos, e_neg, e_st)


def _gla_mask(reverse):
    i = lax.broadcasted_iota(jnp.int32, (GLA_CHUNK, GLA_CHUNK), 0)
    j = lax.broadcasted_iota(jnp.int32, (GLA_CHUNK, GLA_CHUNK), 1)
    return (j >= i) if reverse else (j <= i)


def _gla_specs(rows, n_blocks, reverse):
    tix = (lambda s: n_blocks - 1 - s) if reverse else (lambda s: s)
    d = 1 if reverse else 0
    lr_block = LR_COL // LANES
    specs = [pl.BlockSpec((rows, GLA_DK), lambda h, s: (tix(s), h)),
             pl.BlockSpec((rows, GLA_DK), lambda h, s: (tix(s), GLA_HEADS + h)),
             pl.BlockSpec((rows, GLA_DV), lambda h, s: (tix(s), GLA_HEADS + h)),
             pl.BlockSpec((rows, LANES), lambda h, s: (tix(s), lr_block)),
             pl.BlockSpec((1, LANES, GLA_DK), lambda h, s: (d, 0, h)),
             pl.BlockSpec((1, 1, GLA_DK), lambda h, s: (d, 0, h))]
    return specs, tix


def gla_fwd(proj, wg_pad, bg, reverse, name):
    rows_total = proj.shape[0]
    rows = min(GLA_BLOCK, rows_total)
    n_blocks = rows_total // rows
    n_chunks = rows // GLA_CHUNK
    specs, tix = _gla_specs(rows, n_blocks, reverse)

    def body(q_ref, k_ref, v_ref, lr_ref, wg_ref, bg_ref, o_ref, st_ref, state, kv_scr, dec_scr):
        _, q_in, k_in, k_st, dec, _ = _gla_prepare(q_ref, k_ref, lr_ref, wg_ref, bg_ref, reverse, n_chunks)
        vb = v_ref[...].reshape(n_chunks, GLA_CHUNK, GLA_DV).astype(BF16)
        qb = q_in.astype(BF16)
        p = jnp.where(_gla_mask(reverse), _bdot(qb, k_in.astype(BF16), 2, 2), 0.0)
        o = _bdot(p.astype(BF16), vb, 2, 1)
        kv_scr[...] = _bdot(vb, k_st.astype(BF16), 1, 1)
        dec_scr[...] = jnp.broadcast_to(dec, dec_scr.shape)

        @pl.when(pl.program_id(1) == 0)
        def _():
            state[...] = jnp.zeros_like(state)

        for c in range(n_chunks):
            cc = n_chunks - 1 - c if reverse else c
            st_ref[0, cc] = state[...]
            state[...] = state[...] * dec_scr[cc, 0:1] + kv_scr[cc]
        o = o + _bdot(qb, st_ref[0].astype(BF16), 2, 2)
        o_ref[...] = o.reshape(rows, GLA_DV)

    return pl.pallas_call(
        body, name=name,
        out_shape=(jax.ShapeDtypeStruct((rows_total, GLA_HEADS * GLA_DV), F32),
                   jax.ShapeDtypeStruct((GLA_HEADS, rows_total // GLA_CHUNK, GLA_DV, GLA_DK), F32)),
        grid=(GLA_HEADS, n_blocks),
        in_specs=specs,
        out_specs=(pl.BlockSpec((rows, GLA_DV), lambda h, s: (tix(s), h)),
                   pl.BlockSpec((1, n_chunks, GLA_DV, GLA_DK), lambda h, s: (h, tix(s), 0, 0))),
        scratch_shapes=[pltpu.VMEM((GLA_DV, GLA_DK), F32), pltpu.VMEM((n_chunks, GLA_DV, GLA_DK), F32),
                        pltpu.VMEM((n_chunks, SUBLANES, GLA_DK), F32)],
        compiler_params=_params("parallel", "arbitrary"),
    )(proj, proj, proj, proj, wg_pad, bg)


def gla_bwd(proj, wg_pad, bg, d_o, states, dqkv_in, reverse, name):
    rows_total = proj.shape[0]
    rows = min(GLA_BLOCK, rows_total)
    n_blocks = rows_total // rows
    n_chunks = rows // GLA_CHUNK
    specs, tix = _gla_specs(rows, n_blocks, not reverse)
    d = 1 if reverse else 0
    specs[4] = pl.BlockSpec((1, LANES, GLA_DK), lambda h, s: (d, 0, h))
    specs[5] = pl.BlockSpec((1, 1, GLA_DK), lambda h, s: (d, 0, h))
    add = dqkv_in is not None

    def body(*refs):
        q_ref, k_ref, v_ref, lr_ref, wg_ref, bg_ref, do_ref, st_ref = refs[:8]
        refs = refs[8:]
        if add:
            aq_ref, ak_ref, av_ref = refs[:3]
            refs = refs[3:]
        dq_ref, dk_ref, dv_ref, dz_ref, dstate, g_scr, dec_scr, dsn_scr = refs
        z, q_in, k_in, k_st, dec, (f_q, f_k, f_s) = _gla_prepare(q_ref, k_ref, lr_ref, wg_ref, bg_ref, reverse,
                                                                 n_chunks)
        mask = _gla_mask(reverse)
        vb = v_ref[...].reshape(n_chunks, GLA_CHUNK, GLA_DV).astype(BF16)
        dob = do_ref[...].reshape(n_chunks, GLA_CHUNK, GLA_DV).astype(BF16)
        qb, kb, ksb = q_in.astype(BF16), k_in.astype(BF16), k_st.astype(BF16)
        st = st_ref[0]
        stb = st.astype(BF16)
        pb = jnp.where(mask, _bdot(qb, kb, 2, 2), 0.0).astype(BF16)
        dpb = jnp.where(mask, _bdot(dob, vb, 2, 2), 0.0).astype(BF16)
        d_qin = _bdot(dpb, kb, 2, 1) + _bdot(dob, stb, 2, 1)
        d_kin = _bdot(dpb, qb, 1, 1)
        dv = _bdot(pb, dob, 1, 1)
        g_scr[...] = _bdot(dob, qb, 1, 1)
        dec_scr[...] = jnp.broadcast_to(dec, dec_scr.shape)

        @pl.when(pl.program_id(1) == 0)
        def _():
            dstate[...] = jnp.zeros_like(dstate)

        for c in range(n_chunks):
            cc = c if reverse else n_chunks - 1 - c
            dsn_scr[cc] = dstate[...]
            dstate[...] = dstate[...] * dec_scr[cc, 0:1] + g_scr[cc]
        dsn = dsn_scr[...]
        dsnb = dsn.astype(BF16)
        dv = dv + _bdot(ksb, dsnb, 2, 2)
        d_kst = _bdot(vb, dsnb, 2, 1)
        d_dec = jnp.sum(dsn * st, axis=1, keepdims=True)
        ks_term = d_kst * k_st
        d_btot = d_dec * dec + jnp.sum(ks_term, axis=1, keepdims=True)
        d_b = d_qin * q_in - d_kin * k_in - ks_term
        pos = lax.broadcasted_iota(jnp.int32, d_b.shape, 1)
        edge = 0 if reverse else GLA_CHUNK - 1
        d_b = d_b + jnp.where(pos == edge, d_btot, 0.0)
        dg = _chunk_cumsum(d_b.reshape(rows, GLA_DK), not reverse)
        dz_ref[...] = dg * (1.0 / GLA_NORMALIZER) * _sigmoid(-z)
        dq = (d_qin * f_q).reshape(rows, GLA_DK)
        dk = (d_kin * f_k + d_kst * f_s).reshape(rows, GLA_DK)
        dv = dv.reshape(rows, GLA_DV)
        if add:
            dq_ref[...] = (dq + aq_ref[...]).astype(BF16)
            dk_ref[...] = (dk + ak_ref[...]).astype(BF16)
            dv_ref[...] = (dv + av_ref[...]).astype(BF16)
        else:
            dq_ref[...] = dq
            dk_ref[...] = dk
            dv_ref[...] = dv

    qkv_specs = [pl.BlockSpec((rows, GLA_DK), lambda h, s: (tix(s), h)),
                 pl.BlockSpec((rows, GLA_DK), lambda h, s: (tix(s), h)),
                 pl.BlockSpec((rows, GLA_DV), lambda h, s: (tix(s), h))]
    in_specs = specs + [pl.BlockSpec((rows, GLA_DV), lambda h, s: (tix(s), h)),
                        pl.BlockSpec((1, n_chunks, GLA_DV, GLA_DK), lambda h, s: (h, tix(s), 0, 0))]
    args = [proj, proj, proj, proj, wg_pad, bg, d_o, states]
    out_dtype = F32
    if add:
        in_specs += qkv_specs
        args += list(dqkv_in)
        out_dtype = BF16
    return pl.pallas_call(
        body, name=name,
        out_shape=(jax.ShapeDtypeStruct((rows_total, GLA_HEADS * GLA_DK), out_dtype),
                   jax.ShapeDtypeStruct((rows_total, GLA_HEADS * GLA_DK), out_dtype),
                   jax.ShapeDtypeStruct((rows_total, GLA_HEADS * GLA_DV), out_dtype),
                   jax.ShapeDtypeStruct((rows_total, GLA_HEADS * GLA_DK), F32)),
        grid=(GLA_HEADS, n_blocks),
        in_specs=in_specs,
        out_specs=(pl.BlockSpec((rows, GLA_DK), lambda h, s: (tix(s), h)),
                   pl.BlockSpec((rows, GLA_DK), lambda h, s: (tix(s), h)),
                   pl.BlockSpec((rows, GLA_DV), lambda h, s: (tix(s), h)),
                   pl.BlockSpec((rows, GLA_DK), lambda h, s: (tix(s), h))),
        scratch_shapes=[pltpu.VMEM((GLA_DV, GLA_DK), F32), pltpu.VMEM((n_chunks, GLA_DV, GLA_DK), F32),
                        pltpu.VMEM((n_chunks, SUBLANES, GLA_DK), F32),
                        pltpu.VMEM((n_chunks, GLA_DV, GLA_DK), F32)],
        compiler_params=_params("parallel", "arbitrary"),
    )(*args)


def gla_gate_bwd(proj, dz_f, dz_b, wg_pad, name):
    rows_total = proj.shape[0]
    tm = min(ROW_TILE, rows_total)
    n_key = GLA_HEADS * GLA_DK

    def body(lr_ref, dzf_ref, dzb_ref, wg_ref, dlr_ref, dwg_ref, dbg_ref):
        step = pl.program_id(0)
        lr_t = jnp.transpose(lr_ref[...])
        dzf, dzb = dzf_ref[...], dzb_ref[...]
        dzf16, dzb16 = dzf.astype(BF16), dzb.astype(BF16)
        dlr_ref[...] = (_dot_nt(dzf16, wg_ref[0]) + _dot_nt(dzb16, wg_ref[1])).astype(BF16)
        dwf = _dot(lr_t[0:GLA_RANK].astype(BF16), dzf16)
        dwb = _dot(lr_t[GLA_RANK:2 * GLA_RANK].astype(BF16), dzb16)
        dbg = jnp.concatenate([_colsum(dzf), _colsum(dzb)], axis=0)

        @pl.when(step == 0)
        def _():
            dwg_ref[0] = dwf
            dwg_ref[1] = dwb
            dbg_ref[...] = dbg

        @pl.when(step > 0)
        def _():
            dwg_ref[0] += dwf
            dwg_ref[1] += dwb
            dbg_ref[...] += dbg

    return pl.pallas_call(
        body, name=name,
        out_shape=(jax.ShapeDtypeStruct((rows_total, LANES), BF16), jax.ShapeDtypeStruct((2, GLA_RANK, n_key), F32),
                   jax.ShapeDtypeStruct((2, n_key), F32)),
        grid=(rows_total // tm,),
        in_specs=[pl.BlockSpec((tm, LANES), lambda i: (i, LR_COL // LANES)),
                  pl.BlockSpec((tm, n_key), lambda i: (i, 0)), pl.BlockSpec((tm, n_key), lambda i: (i, 0)),
                  _full((2, LANES, n_key))],
        out_specs=(pl.BlockSpec((tm, LANES), lambda i: (i, 0)), _full((2, GLA_RANK, n_key)), _full((2, n_key))),
        compiler_params=_params("arbitrary"),
    )(proj, dz_f, dz_b, wg_pad)


def _head_norm(o, gain):
    outs, hats, rstds = [], [], []
    for h in range(GLA_HEADS):
        oh = o[:, h * GLA_DV:(h + 1) * GLA_DV]
        rstd = lax.rsqrt(jnp.mean(oh * oh, axis=-1, keepdims=True) + NORM_EPS)
        hat = oh * rstd
        outs.append(hat * gain)
        hats.append(hat)
        rstds.append(rstd)
    return outs, hats, rstds


def odd_out_fwd(o_f, o_b, proj, head_gain, w_out, gain, x1, target, name):
    rows, d = x1.shape
    tm = min(ROW_TILE, rows)
    r_block = (2 * GLA_HEADS * GLA_DK + GLA_HEADS * GLA_DV) // d

    def body(of_ref, ob_ref, r_ref, hg_ref, w_ref, g_ref, x1_ref, tgt_ref, y2_ref, dy_ref, dx2_ref, loss_ref,
             dg_ref):
        step = pl.program_id(0)
        on, _, _ = _head_norm(of_ref[...] + ob_ref[...], hg_ref[...])
        r = r_ref[...]
        y2 = (jnp.concatenate(on, axis=1) * (r * _sigmoid(r))).astype(BF16)
        y2_ref[...] = y2
        y = _dot(y2, w_ref[...])
        gain_v = g_ref[...]
        rstd = lax.rsqrt(jnp.mean(y * y, axis=-1, keepdims=True) + NORM_EPS)
        x2 = x1_ref[...] + y * rstd * gain_v
        diff = x2 - tgt_ref[...]
        loss = 0.5 * jnp.sum(jnp.mean(diff * diff, axis=-1, keepdims=True), axis=0, keepdims=True)
        dx2 = diff * (1.0 / d)
        dx2_ref[...] = dx2
        dy, dg_rows = _rmsnorm_bwd(dx2, y, gain_v)
        dy_ref[...] = dy.astype(BF16)
        _accumulate(loss_ref, jnp.broadcast_to(loss, loss_ref.shape), step)
        _accumulate(dg_ref, _colsum(dg_rows), step)

    row = lambda n, col=0: pl.BlockSpec((tm, n), lambda i: (i, col))
    return pl.pallas_call(
        body, name=name,
        out_shape=(jax.ShapeDtypeStruct((rows, d), BF16), jax.ShapeDtypeStruct((rows, d), BF16),
                   jax.ShapeDtypeStruct((rows, d), F32), jax.ShapeDtypeStruct((SUBLANES, LANES), F32),
                   jax.ShapeDtypeStruct((1, d), F32)),
        grid=(rows // tm,),
        in_specs=[row(d), row(d), row(d, r_block), _full((1, GLA_DV)), _full((d, d)), _full((1, d)), row(d), row(d)],
        out_specs=(row(d), row(d), row(d), _full((SUBLANES, LANES)), _full((1, d))),
        compiler_params=_params("arbitrary"),
    )(o_f, o_b, proj, head_gain, w_out, gain, x1, target)


def odd_out_bwd(dy, w_out, o_f, o_b, proj, head_gain, name):
    rows, d = dy.shape
    tm = min(ROW_TILE, rows)
    r_block = (2 * GLA_HEADS * GLA_DK + GLA_HEADS * GLA_DV) // d

    def body(dy_ref, w_ref, of_ref, ob_ref, r_ref, hg_ref, dr_ref, do_ref, dhg_ref):
        dy2 = _dot_nt(dy_ref[...], w_ref[...])
        hg = hg_ref[...]
        on, hats, rstds = _head_norm(of_ref[...] + ob_ref[...], hg)
        r = r_ref[...]
        sr = _sigmoid(r)
        dr_ref[...] = (dy2 * jnp.concatenate(on, axis=1) * (sr * (1.0 + r * (1.0 - sr)))).astype(BF16)
        d_on = dy2 * (r * sr)
        d_os, dhg = [], None
        for h in range(GLA_HEADS):
            dn = d_on[:, h * GLA_DV:(h + 1) * GLA_DV]
            part = _colsum(dn * hats[h])
            dhg = part if dhg is None else dhg + part
            dng = dn * hg
            d_os.append(rstds[h] * (dng - hats[h] * jnp.mean(dng * hats[h], axis=-1, keepdims=True)))
        do_ref[...] = jnp.concatenate(d_os, axis=1)
        _accumulate(dhg_ref, dhg, pl.program_id(0))

    row = lambda n, col=0: pl.BlockSpec((tm, n), lambda i: (i, col))
    return pl.pallas_call(
        body, name=name,
        out_shape=(jax.ShapeDtypeStruct((rows, d), BF16), jax.ShapeDtypeStruct((rows, d), F32),
                   jax.ShapeDtypeStruct((1, GLA_DV), F32)),
        grid=(rows // tm,),
        in_specs=[row(d), _full((d, d)), row(d), row(d), row(d, r_block), _full((1, GLA_DV))],
        out_specs=(row(d), row(d), _full((1, GLA_DV))),
        compiler_params=_params("arbitrary"),
    )(dy, w_out, o_f, o_b, proj, head_gain)


def local_step(x, target, w, reduce_first=None, reduce_second=None, late_weights=None):
    g = {}
    proj_e, h0 = norm_matmul(x, w["even_norm_pre"], w["even_w_in"], "even_in_proj")
    h_dir = [rglru_fwd(proj_e, w["rg_conv_w"], w["rg_conv_b"], w["rg_gate_w"][d], w["rg_gate_b"][d],
                       w["rg_lambda"][d], d == 1, "rglru_fwd_%d" % d) for d in range(2)]
    ycat = even_mix_fwd(proj_e, h_dir[0], h_dir[1], w["sc_conv_w"], "even_mix_fwd")
    if late_weights is not None:
        w = dict(w, **late_weights(ycat))
    x1, y_e = even_out_fwd(ycat, w["even_w_out"], w["even_norm_post"], x, "even_out_fwd")
    proj_o, h1 = norm_matmul(x1, w["odd_norm_pre"], w["odd_w_in"], "odd_in_proj")
    o_dir, st_dir = [], []
    for d in range(2):
        o, st = gla_fwd(proj_o, w["gla_wg_pad"], w["gla_b_gate"], d == 1, "gla_fwd_%d" % d)
        o_dir.append(o)
        st_dir.append(st)
    y2, dy_o, dx2, loss, g["odd_norm_post"] = odd_out_fwd(
        o_dir[0], o_dir[1], proj_o, w["gla_norm_g"], w["odd_w_out"], w["odd_norm_post"], x1, target, "odd_out_fwd")
    g["odd_w_out"] = matmul_dw(y2, dy_o, D_MODEL, "odd_w_out_grad")[0]
    dr, d_o, g["gla_norm_g"] = odd_out_bwd(dy_o, w["odd_w_out"], o_dir[0], o_dir[1], proj_o, w["gla_norm_g"],
                                           "odd_out_bwd")
    dq, dk, dv, dz_f = gla_bwd(proj_o, w["gla_wg_pad"], w["gla_b_gate"], d_o, st_dir[0], None, False, "gla_bwd_0")
    dq, dk, dv, dz_b = gla_bwd(proj_o, w["gla_wg_pad"], w["gla_b_gate"], d_o, st_dir[1], (dq, dk, dv), True,
                               "gla_bwd_1")
    dlr, g["gla_w_gate_lr"], g["gla_b_gate"] = gla_gate_bwd(proj_o, dz_f, dz_b, w["gla_wg_pad"], "gla_gate_bwd")
    dproj_o = [dq, dk, dv, dr, dlr]
    g["odd_w_in"] = jnp.concatenate(matmul_dw_pieces(h1, dproj_o, "odd_w_in_grad"), axis=1)[:, :ODD_IN]
    dx1, g["odd_norm_pre"] = inproj_bwd_pieces(dproj_o, w["odd_w_in"], x1, w["odd_norm_pre"], dx2, "odd_in_proj_bwd")
    dy_e, dycat, g["even_norm_post"] = even_out_bwd(dx1, y_e, w["even_norm_post"], w["even_w_out"], "even_out_bwd")
    g["even_w_out"] = matmul_dw(ycat, dy_e, D_MODEL, "even_w_out_grad")[0]
    conv_b = w["rg_conv_b"] if reduce_first is None else w["rg_conv_b"] + reduce_first(g)
    dua, dgw, dgb, dlam = [], [], [], []
    for d in range(2):
        a, b, c, e = rglru_bwd(proj_e, dycat, h_dir[d], w["rg_conv_w"], conv_b, w["rg_gate_w"][d],
                               w["rg_gate_b"][d], w["rg_lambda"][d], d == 1, "rglru_bwd_%d" % d)
        dua.append(a)
        dgw.append(b)
        dgb.append(c)
        dlam.append(e)
    dproj_e, g["rg_conv_w"], g["rg_conv_b"], g["sc_conv_w"] = even_mix_bwd(
        proj_e, dycat, h_dir[0], h_dir[1], dua[0], dua[1], w["rg_conv_w"], w["sc_conv_w"], "even_mix_bwd")
    dgw = jnp.stack(dgw).reshape(2, RG_HEADS, RG_HEAD_DIM, 2, RG_HEAD_DIM)
    g["rg_gate_w"] = jnp.transpose(dgw, (0, 3, 1, 2, 4))
    g["rg_gate_b"] = jnp.stack(dgb).reshape(2, 2, RG_HEADS, RG_HEAD_DIM)
    g["rg_lambda"] = jnp.concatenate(dlam, axis=0)
    g["even_w_in"] = matmul_dw(h0, dproj_e, EVEN_IN // 4, "even_w_in_grad")
    gain = w["even_norm_pre"] if reduce_second is None else w["even_norm_pre"] + reduce_second(g)
    grad_x, g["even_norm_pre"] = inproj_bwd(dproj_e, w["even_w_in"], x, gain, dx1, "even_in_proj_bwd")
    return loss, grad_x, g


def _prepare_weights(full):
    w = {}
    for name in ("even_norm_pre", "even_norm_post", "rg_conv_b", "odd_norm_pre", "odd_norm_post", "gla_norm_g"):
        if name in full:
            w[name] = full[name].reshape(1, -1)
    for name in ("rg_conv_w", "sc_conv_w"):
        if name in full:
            w[name] = full[name]
    for name in ("even_w_out", "odd_w_out"):
        if name in full:
            w[name] = full[name].astype(BF16)
    if "even_w_in" in full:
        w["even_w_in"] = full["even_w_in"].astype(BF16)
        if w["even_w_in"].ndim == 2:
            w["even_w_in"] = jnp.transpose(w["even_w_in"].reshape(D_MODEL, 4, EVEN_IN // 4), (1, 0, 2))
    if "rg_gate_w" in full:
        gw = jnp.transpose(full["rg_gate_w"].astype(BF16), (0, 2, 3, 1, 4))
        w["rg_gate_w"] = gw.reshape(2, RG_HEADS, RG_HEAD_DIM, 2 * RG_HEAD_DIM)
        w["rg_gate_b"] = full["rg_gate_b"].reshape(2, 2, D_MODEL)
        w["rg_lambda"] = full["rg_lambda"].reshape(2, 1, D_MODEL)
    if "odd_w_in" in full:
        w_in = jnp.pad(full["odd_w_in"].astype(BF16), ((0, 0), (0, ODD_IN_PAD - ODD_IN)))
        w["odd_w_in"] = w_in.reshape(1, D_MODEL, ODD_IN_PAD)
    if "gla_w_gate_lr" in full:
        wg = full["gla_w_gate_lr"].astype(BF16)
        w["gla_wg_pad"] = jnp.stack([jnp.pad(wg[d], ((d * GLA_RANK, LANES - (d + 1) * GLA_RANK), (0, 0)))
                                     for d in range(2)])
        w["gla_b_gate"] = full["gla_b_gate"].reshape(2, 1, GLA_HEADS * GLA_DK)
    return w


SHARDED_SMALL = (("rg_conv_w", (4, 256)), ("rg_lambda", (2, 256)), ("sc_conv_w", (3, 256)),
                 ("odd_norm_pre", (256,)), ("odd_norm_post", (256,)), ("gla_w_gate_lr", (2, 16, 128)),
                 ("gla_b_gate", (2, 128)), ("gla_norm_g", (64,)))
SHARDED_ROWS = 96
REPLICATED = (("even_norm_post", (1024,)), ("rg_conv_b", (1024,)),
              ("rg_gate_b", (2, 2, 8, 128)), ("rg_gate_w", (2, 2, 8, 128, 128)))
LAST_REPLICATED = (("even_norm_pre", (1024,)),)
LAST_ROWS = 8
REPLICATED_ROWS = 4160
REP_PART = REPLICATED_ROWS // 8
HALF_SHARDED = SHARDED_ROWS // 2
PACK_HALF = HALF_SHARDED + REP_PART


def _seg_rows(shape):
    n = 1
    for s in shape:
        n *= s
    return -(-n // (SUBLANES * LANES)) * SUBLANES


def _pack(arrays, spec, total_rows, lead=()):
    parts = []
    for name, shape in spec:
        flat = arrays[name].reshape(lead + (-1,))
        pad = _seg_rows(shape) * LANES - flat.shape[-1]
        if pad:
            flat = jnp.pad(flat, [(0, 0)] * len(lead) + [(0, pad)])
        parts.append(flat.reshape(lead + (-1, LANES)))
    rows = jnp.concatenate(parts, axis=len(lead))
    pad = total_rows - rows.shape[len(lead)]
    return jnp.pad(rows, [(0, 0)] * len(lead) + [(0, pad), (0, 0)])


def _unpack(rows, spec, lead=()):
    out, at = {}, 0
    for name, shape in spec:
        n = 1
        for s in shape:
            n *= s
        k = _seg_rows(shape)
        seg = lax.slice_in_dim(rows, at, at + k, axis=len(lead)).reshape(lead + (-1,))
        out[name] = lax.slice_in_dim(seg, 0, n, axis=len(lead)).reshape(lead + shape)
        at += k
    return out


def _split_owners(arr):
    a = arr.reshape(arr.shape[:-1] + (4, arr.shape[-1] // 4))
    return jnp.moveaxis(a, -2, 0)


def _merge_owners(arr):
    a = jnp.moveaxis(arr, 0, -2)
    return a.reshape(a.shape[:-2] + (-1,))


HBM_SPEC = pl.BlockSpec(memory_space=pltpu.HBM)


def _position():
    x, y, c = lax.axis_index("x"), lax.axis_index("y"), lax.axis_index("c")
    chips = [(1 - x, y), (x, 1 - y), (1 - x, 1 - y)]
    return x, y, c, chips


def _remote(src, dst, send_sem, recv_sem, device):
    return pltpu.make_async_remote_copy(src_ref=src, dst_ref=dst, send_sem=send_sem, recv_sem=recv_sem,
                                        device_id=device, device_id_type=MESH)


SEM_SPEC = pl.BlockSpec(memory_space=pltpu.SEMAPHORE)
SIDE_EFFECT = pltpu.SideEffectType.DATAFLOW_SIDE_EFFECTING


def _gather_copies(ins, lands, n_h, send_sems, recv_sems):
    x, y, c, chips = _position()
    me = 2 * x + y
    copies = []
    for a in range(len(ins)):
        for k, chip in enumerate(chips):
            src = ins[a].at[c] if a < n_h else ins[a]
            dst = lands[a].at[me, c] if a < n_h else lands[a].at[me]
            copies.append(_remote(src, dst, send_sems.at[3 * a + k], recv_sems.at[3 * a + k], (chip[0], chip[1], c)))
    return copies


def gather_start(halved, whole, name):
    arrays = list(halved) + list(whole)
    n, n_h = len(arrays), len(halved)
    lands = [lax.empty((4,) + a.shape, a.dtype) for a in arrays]

    def body(*refs):
        ins, lz, send_sems, recv_sems, token = refs[:n], refs[n:2 * n], refs[2 * n], refs[2 * n + 1], refs[-1]
        for cp in _gather_copies(ins, lz, n_h, send_sems, recv_sems):
            cp.start()
        token[...] = jnp.zeros_like(token)

    operands = [pltpu.with_memory_space_constraint(a, pltpu.HBM) for a in arrays + lands]
    return pl.pallas_call(
        body, name=name,
        out_shape=(pltpu.SemaphoreType.DMA((3 * n,)), pltpu.SemaphoreType.DMA((3 * n,)))
        + tuple(pltpu.HBM(a.shape, a.dtype) for a in operands) + (jax.ShapeDtypeStruct((SUBLANES, LANES), F32),),
        in_specs=[HBM_SPEC] * (2 * n),
        out_specs=(SEM_SPEC, SEM_SPEC) + (HBM_SPEC,) * (2 * n) + (pl.BlockSpec(memory_space=pltpu.VMEM),),
        input_output_aliases={i: 2 + i for i in range(2 * n)},
        compiler_params=pltpu.CompilerParams(has_side_effects=SIDE_EFFECT),
    )(*operands)


def gather_wait(started, n_h, after, name):
    send_sems, recv_sems = started[0], started[1]
    operands = list(started[2:-1])
    n = len(operands) // 2

    def body(*refs):
        ins, lz, send_ref, recv_ref = refs[:n], refs[n:2 * n], refs[2 * n], refs[2 * n + 1]
        for cp in _gather_copies(ins, lz, n_h, send_ref, recv_ref):
            cp.wait_send()
            cp.wait_recv()

    outs = pl.pallas_call(
        body, name=name,
        out_shape=tuple(pltpu.HBM(a.shape, a.dtype) for a in operands),
        in_specs=[HBM_SPEC] * (2 * n) + [SEM_SPEC, SEM_SPEC, pl.BlockSpec(memory_space=pl.ANY)],
        out_specs=(HBM_SPEC,) * (2 * n),
        input_output_aliases={i: i for i in range(2 * n)},
        compiler_params=pltpu.CompilerParams(has_side_effects=SIDE_EFFECT),
    )(*operands, send_sems, recv_sems, after)
    return outs[n:]


def pass_to_sibling(fulls, name):
    n = len(fulls)

    def body(*refs):
        bufs = refs[n:2 * n]
        send_sems, recv_sems = refs[2 * n:]
        x, y, c, chips = _position()
        sibling = (x, y, 1 - c)
        copies = []
        for a in range(n):
            for k, chip in enumerate(chips):
                q = 2 * chip[0] + chip[1]
                cp = _remote(bufs[a].at[q, c], bufs[a].at[q, c], send_sems.at[3 * a + k], recv_sems.at[3 * a + k],
                             sibling)
                cp.start()
                copies.append(cp)
        for a in range(n):
            for k, chip in enumerate(chips):
                q = 2 * chip[0] + chip[1]
                passed = bufs[a].at[q, 1 - c]
                _remote(passed, passed, send_sems.at[3 * a + k], recv_sems.at[3 * a + k], sibling).wait_recv()
        for cp in copies:
            cp.wait_send()

    return pl.pallas_call(
        body, name=name,
        out_shape=[jax.ShapeDtypeStruct(a.shape, a.dtype) for a in fulls],
        in_specs=[HBM_SPEC] * n, out_specs=[HBM_SPEC] * n,
        input_output_aliases={i: i for i in range(n)},
        scratch_shapes=[pltpu.SemaphoreType.DMA((3 * n,)), pltpu.SemaphoreType.DMA((3 * n,))],
    )(*fulls)


def place_own(full, own, chip, name):
    _, _, r, cols = full.shape
    tr = _row_tile(r, cols)

    def body(p_ref, own_ref, full_ref, o_ref):
        o_ref[0] = own_ref[...]

    return pl.pallas_call(
        body, name=name,
        out_shape=jax.ShapeDtypeStruct(full.shape, full.dtype),
        grid_spec=pltpu.PrefetchScalarGridSpec(
            num_scalar_prefetch=1, grid=(2, r // tr),
            in_specs=[pl.BlockSpec((1, tr, cols), lambda h, i, p_ref: (h, i, 0)), pl.BlockSpec(memory_space=pl.ANY)],
            out_specs=pl.BlockSpec((1, 1, tr, cols), lambda h, i, p_ref: (p_ref[0], h, i, 0))),
        input_output_aliases={2: 0},
        compiler_params=_params("parallel", "parallel"),
    )(chip, own, full)


def exchange_with_sibling(arrays, name):
    n = len(arrays)

    def body(*refs):
        ins, outs = refs[:n], refs[n:2 * n]
        send_sems, recv_sems = refs[2 * n:]
        x, y, c, _ = _position()
        copies = []
        for a in range(n):
            cp = _remote(ins[a].at[:, 1 - c], outs[a], send_sems.at[a], recv_sems.at[a], (x, y, 1 - c))
            cp.start()
            copies.append(cp)
        for cp in copies:
            cp.wait()

    return pl.pallas_call(
        body, name=name,
        out_shape=[jax.ShapeDtypeStruct((a.shape[0],) + a.shape[2:], a.dtype) for a in arrays],
        in_specs=[HBM_SPEC] * n, out_specs=[HBM_SPEC] * n,
        scratch_shapes=[pltpu.SemaphoreType.DMA((n,)), pltpu.SemaphoreType.DMA((n,))],
    )(*arrays)


def _chip_copies(ins, lands, send_sems, recv_sems):
    x, y, c, chips = _position()
    copies = []
    for a in range(len(ins)):
        for k, chip in enumerate(chips):
            q = 2 * chip[0] + chip[1]
            copies.append(_remote(ins[a].at[q], lands[a].at[k], send_sems.at[3 * a + k], recv_sems.at[3 * a + k],
                                  (chip[0], chip[1], c)))
    return copies


def exchange_with_chips_start(arrays, name):
    n = len(arrays)
    lands = [lax.empty((3,) + a.shape[1:], a.dtype) for a in arrays]

    def body(*refs):
        ins, lz, send_sems, recv_sems, token = refs[:n], refs[n:2 * n], refs[2 * n], refs[2 * n + 1], refs[-1]
        for cp in _chip_copies(ins, lz, send_sems, recv_sems):
            cp.start()
        token[...] = jnp.zeros_like(token)

    operands = [pltpu.with_memory_space_constraint(a, pltpu.HBM) for a in list(arrays) + lands]
    return pl.pallas_call(
        body, name=name,
        out_shape=(pltpu.SemaphoreType.DMA((3 * n,)), pltpu.SemaphoreType.DMA((3 * n,)))
        + tuple(pltpu.HBM(a.shape, a.dtype) for a in operands) + (jax.ShapeDtypeStruct((SUBLANES, LANES), F32),),
        in_specs=[HBM_SPEC] * (2 * n),
        out_specs=(SEM_SPEC, SEM_SPEC) + (HBM_SPEC,) * (2 * n) + (pl.BlockSpec(memory_space=pltpu.VMEM),),
        input_output_aliases={i: 2 + i for i in range(2 * n)},
        compiler_params=pltpu.CompilerParams(has_side_effects=SIDE_EFFECT),
    )(*operands)


def exchange_with_chips_wait(started, after, name):
    send_sems, recv_sems = started[0], started[1]
    operands = list(started[2:-1])
    n = len(operands) // 2

    def body(*refs):
        ins, lz, send_ref, recv_ref = refs[:n], refs[n:2 * n], refs[2 * n], refs[2 * n + 1]
        for cp in _chip_copies(ins, lz, send_ref, recv_ref):
            cp.wait_send()
            cp.wait_recv()

    outs = pl.pallas_call(
        body, name=name,
        out_shape=tuple(pltpu.HBM(a.shape, a.dtype) for a in operands),
        in_specs=[HBM_SPEC] * (2 * n) + [SEM_SPEC, SEM_SPEC, pl.BlockSpec(memory_space=pl.ANY)],
        out_specs=(HBM_SPEC,) * (2 * n),
        input_output_aliases={i: i for i in range(2 * n)},
        compiler_params=pltpu.CompilerParams(has_side_effects=SIDE_EFFECT),
    )(*operands, send_sems, recv_sems, after)
    return outs[:n], outs[n:]


def share_totals(totals, pack_total, last_part):
    arrays = list(totals) + [pack_total]
    n = len(arrays)

    def body(*refs):
        ins, last, outs, rep, last_all = refs[:n], refs[n], refs[n + 1:2 * n + 1], refs[2 * n + 1], refs[2 * n + 2]
        send_sems, recv_sems, rep_send, rep_recv, last_send, last_recv = refs[2 * n + 3:]
        x, y, c, chips = _position()
        sibling = (x, y, 1 - c)
        me = 4 * x + 2 * y + c
        sends = []
        for a in range(n):
            cp = _remote(ins[a], outs[a], send_sems.at[a], recv_sems.at[a], sibling)
            cp.start()
            sends.append(cp)
        mine = ins[n - 1].at[pl.ds(HALF_SHARDED, REP_PART)]
        peers = [sibling]
        for chip in chips:
            peers += [(chip[0], chip[1], c), (chip[0], chip[1], 1 - c)]
        for j, peer in enumerate(peers):
            for src, dst, s_sem, r_sem in ((mine, rep, rep_send, rep_recv), (last, last_all, last_send, last_recv)):
                cp = _remote(src, dst.at[me], s_sem.at[j], r_sem.at[j], peer)
                cp.start()
                sends.append(cp)
        for a in range(n):
            _remote(outs[a], outs[a], send_sems.at[a], recv_sems.at[a], sibling).wait_recv()
        for j, peer in enumerate(peers):
            it = 4 * peer[0] + 2 * peer[1] + peer[2]
            _remote(rep.at[it], rep.at[it], rep_send.at[j], rep_recv.at[j], peer).wait_recv()
            _remote(last_all.at[it], last_all.at[it], last_send.at[j], last_recv.at[j], peer).wait_recv()
        for cp in sends:
            cp.wait_send()

    outs = pl.pallas_call(
        body, name="grad_share_totals",
        out_shape=[jax.ShapeDtypeStruct(a.shape, a.dtype) for a in arrays]
        + [jax.ShapeDtypeStruct((8, REP_PART, LANES), F32), jax.ShapeDtypeStruct((8,) + last_part.shape, F32)],
        in_specs=[HBM_SPEC] * (n + 1), out_specs=[HBM_SPEC] * (n + 2),
        scratch_shapes=[pltpu.SemaphoreType.DMA((n,)), pltpu.SemaphoreType.DMA((n,))]
        + [pltpu.SemaphoreType.DMA((7,))] * 4,
    )(*arrays, last_part)
    return outs[:n], outs[n], outs[n + 1]


def sum_parts(parts, name):
    def body(p_ref, o_ref):
        total = p_ref[0]
        for k in range(1, parts.shape[0]):
            total = total + p_ref[k]
        o_ref[...] = total

    return pl.pallas_call(body, name=name, out_shape=jax.ShapeDtypeStruct(parts.shape[1:], parts.dtype))(parts)


TILE_BYTES = 1 << 20


def _row_tile(rows, cols):
    best = None
    for t in range(SUBLANES, rows + 1, SUBLANES):
        if rows % t == 0 and t * cols * 4 <= TILE_BYTES:
            best = t
    return best if best is not None else rows


def add_sibling(mine, received, core, out_dtype, name):
    _, _, r, cols = mine.shape
    tr = _row_tile(r, cols)

    def body(c_ref, a_ref, b_ref, o_ref):
        o_ref[...] = (a_ref[0] + b_ref[...]).astype(out_dtype)

    return pl.pallas_call(
        body, name=name,
        out_shape=jax.ShapeDtypeStruct((4, r, cols), out_dtype),
        grid_spec=pltpu.PrefetchScalarGridSpec(
            num_scalar_prefetch=1, grid=(4, r // tr),
            in_specs=[pl.BlockSpec((1, 1, tr, cols), lambda o, i, c_ref: (o, c_ref[0], i, 0)),
                      pl.BlockSpec((1, tr, cols), lambda o, i, c_ref: (o, i, 0))],
            out_specs=pl.BlockSpec((1, tr, cols), lambda o, i, c_ref: (o, i, 0))),
        compiler_params=_params("parallel", "parallel"),
    )(core, mine, received)


def add_chips(own, received, chip, name):
    _, r, cols = own.shape
    tr = _row_tile(r, cols)

    def body(p_ref, a_ref, b0, b1, b2, o_ref):
        o_ref[...] = ((a_ref[0].astype(F32) + b0[0].astype(F32)) + b1[0].astype(F32)) + b2[0].astype(F32)

    rb = lambda k: pl.BlockSpec((1, tr, cols), lambda i, p_ref: (k, i, 0))
    return pl.pallas_call(
        body, name=name,
        out_shape=jax.ShapeDtypeStruct((r, cols), F32),
        grid_spec=pltpu.PrefetchScalarGridSpec(
            num_scalar_prefetch=1, grid=(r // tr,),
            in_specs=[pl.BlockSpec((1, tr, cols), lambda i, p_ref: (p_ref[0], i, 0)), rb(0), rb(1), rb(2)],
            out_specs=pl.BlockSpec((tr, cols), lambda i, p_ref: (i, 0))),
        compiler_params=_params("parallel"),
    )(chip, own, received, received, received)


def _adamw_update(gv, w_ref, m_ref, v_ref, d_ref, nm_ref, nv_ref):
    nm = ADAM_B1 * m_ref[...] + (1.0 - ADAM_B1) * gv
    nv = ADAM_B2 * v_ref[...] + (1.0 - ADAM_B2) * (gv * gv)
    nm_ref[...] = nm
    nv_ref[...] = nv
    m_hat = nm / (1.0 - ADAM_B1 ** ADAM_STEP)
    v_hat = nv / (1.0 - ADAM_B2 ** ADAM_STEP)
    d_ref[...] = -ADAM_LR * (m_hat / (jnp.sqrt(v_hat) + ADAM_EPS) + ADAM_WD * w_ref[...])


def adamw_halves(w, own, received, m, v, core, name):
    rows, cols = w.shape
    r = rows // 2
    tr = _row_tile(r, cols)
    nr = r // tr

    def body(c_ref, w_ref, own_ref, rec_ref, m_ref, v_ref, g_ref, d_ref, nm_ref, nv_ref):
        gv = jnp.where(pl.program_id(0) == c_ref[0], own_ref[...], rec_ref[...])
        g_ref[...] = gv
        _adamw_update(gv, w_ref, m_ref, v_ref, d_ref, nm_ref, nv_ref)

    whole = pl.BlockSpec((tr, cols), lambda h, i, c_ref: (h * nr + i, 0))
    half = pl.BlockSpec((tr, cols), lambda h, i, c_ref: (i, 0))
    return pl.pallas_call(
        body, name=name,
        out_shape=(jax.ShapeDtypeStruct((rows, cols), F32),) * 4,
        grid_spec=pltpu.PrefetchScalarGridSpec(
            num_scalar_prefetch=1, grid=(2, nr),
            in_specs=[whole, half, half, whole, whole], out_specs=(whole,) * 4),
        compiler_params=_params("parallel", "parallel"),
    )(core, w, own, received, m, v)


def adamw(w, g, m, v, name):
    r, cols = w.shape
    tr = _row_tile(r, cols)

    def body(w_ref, g_ref, m_ref, v_ref, d_ref, nm_ref, nv_ref):
        _adamw_update(g_ref[...], w_ref, m_ref, v_ref, d_ref, nm_ref, nv_ref)

    blk = pl.BlockSpec((tr, cols), lambda i: (i, 0))
    return pl.pallas_call(
        body, name=name,
        out_shape=(jax.ShapeDtypeStruct((r, cols), F32),) * 3,
        grid=(r // tr,),
        in_specs=[blk] * 4, out_specs=(blk,) * 3,
        compiler_params=_params("parallel"),
    )(w, g, m, v)


WEIGHTS = ("even_norm_pre", "even_norm_post", "even_w_in", "rg_conv_w", "rg_conv_b", "rg_gate_w", "rg_gate_b",
           "rg_lambda", "sc_conv_w", "even_w_out", "odd_norm_pre", "odd_norm_post", "odd_w_in", "gla_w_gate_lr",
           "gla_b_gate", "gla_norm_g", "odd_w_out")
BIG = ("even_w_in", "even_w_out", "odd_w_in", "odd_w_out")


def _halves(a):
    return a.reshape((2, a.shape[0] // 2) + a.shape[1:])


def kernel(x, even_norm_pre, even_norm_post, even_w_in, rg_conv_w, rg_conv_b, rg_gate_w, rg_gate_b, rg_lambda, sc_conv_w, even_w_out, odd_norm_pre, odd_norm_post, odd_w_in, gla_w_gate_lr, gla_b_gate, gla_norm_g, odd_w_out, loss_target, m_even_norm_pre, m_even_norm_post, m_even_w_in, m_rg_conv_w, m_rg_conv_b, m_rg_gate_w, m_rg_gate_b, m_rg_lambda, m_sc_conv_w, m_even_w_out, m_odd_norm_pre, m_odd_norm_post, m_odd_w_in, m_gla_w_gate_lr, m_gla_b_gate, m_gla_norm_g, m_odd_w_out, v_even_norm_pre, v_even_norm_post, v_even_w_in, v_rg_conv_w, v_rg_conv_b, v_rg_gate_w, v_rg_gate_b, v_rg_lambda, v_sc_conv_w, v_even_w_out, v_odd_norm_pre, v_odd_norm_post, v_odd_w_in, v_gla_w_gate_lr, v_gla_b_gate, v_gla_norm_g, v_odd_w_out):
    given = dict(locals())
    shard = {n: given[n][0] for n in WEIGHTS}
    m_in = {n: given["m_" + n][0] for n in WEIGHTS}
    v_in = {n: given["v_" + n][0] for n in WEIGHTS}
    mx, my, mc = lax.axis_index("x"), lax.axis_index("y"), lax.axis_index("c")
    core = jnp.reshape(mc, (1,)).astype(jnp.int32)
    chip = jnp.reshape(2 * mx + my, (1,)).astype(jnp.int32)

    small_shard = _pack(shard, SHARDED_SMALL, SHARDED_ROWS)
    big_own = [_halves(shard[n].astype(BF16)) for n in BIG]
    started_a = gather_start(big_own[:1], [small_shard], "gather_start_a")
    started_b = gather_start(big_own[1:], [], "gather_start_b")
    even_w_in_full, small_full = gather_wait(started_a, 1, started_b[-1], "gather_wait_a")
    (even_w_in_full,) = pass_to_sibling([even_w_in_full], "gather_pass_a")
    even_w_in_full = place_own(even_w_in_full, big_own[0], chip, "place_even_w_in")
    small_full = lax.dynamic_update_slice(small_full, small_shard[None], (chip[0], 0, 0))
    full = {n: shard[n] for n, _ in REPLICATED + LAST_REPLICATED}
    full.update({n: _merge_owners(a) for n, a in _unpack(small_full, SHARDED_SMALL, lead=(4,)).items()})
    full["even_w_in"] = even_w_in_full.reshape(4, D_MODEL, EVEN_IN // 4)

    def late_weights(after):
        lands = pass_to_sibling(list(gather_wait(started_b, 3, after, "gather_wait_b")), "gather_pass_b")
        lands = [place_own(a, b, chip, "place_" + n) for a, b, n in zip(lands, big_own[1:], BIG[1:])]
        odd_w_in = jnp.transpose(lands[1].reshape(4, D_MODEL, ODD_IN // 4), (1, 0, 2)).reshape(D_MODEL, ODD_IN)
        return _prepare_weights({"even_w_out": lands[0].reshape(2 * D_MODEL, D_MODEL), "odd_w_in": odd_w_in,
                                 "odd_w_out": lands[2].reshape(D_MODEL, D_MODEL)})

    pending = {}

    def slab(a):
        return a.reshape((4, 2, a.shape[1] // 2) + a.shape[2:])

    def begin(tag, slabs, dtypes):
        got = exchange_with_sibling(slabs, "grad_sibling_" + tag)
        sums = [add_sibling(a, b, core, dt, "grad_add_sibling_%s%d" % (tag, i))
                for i, (a, b, dt) in enumerate(zip(slabs, got, dtypes))]
        pending[tag] = exchange_with_chips_start(sums, "grad_chips_start_" + tag)
        return pending[tag][-1][0, 0]

    def finish(tag, after):
        sums, got = exchange_with_chips_wait(pending[tag], after, "grad_chips_wait_" + tag)
        return [add_chips(a, b, chip, "grad_add_chips_%s%d" % (tag, i)) for i, (a, b) in enumerate(zip(sums, got))]

    def reduce_first(g):
        return begin("a", [slab(jnp.transpose(g["odd_w_in"].reshape(D_MODEL, 4, ODD_IN // 4), (1, 0, 2))),
                           slab(g["odd_w_out"].reshape(4, D_MODEL // 4, D_MODEL)),
                           slab(g["even_w_out"].reshape(4, D_MODEL // 2, D_MODEL))], [BF16] * 3)

    def reduce_second(g):
        pending["totals_a"] = finish("a", g["even_w_in"])
        rep_rows = _pack(g, REPLICATED, REPLICATED_ROWS).reshape(4, 2, REP_PART, LANES)
        sh_rows = _pack({n: _split_owners(g[n]) for n, _ in SHARDED_SMALL}, SHARDED_SMALL, SHARDED_ROWS, lead=(4,))
        pack = jnp.concatenate([sh_rows.reshape(4, 2, HALF_SHARDED, LANES), rep_rows], axis=2)
        return begin("b", [slab(g["even_w_in"]), pack], [BF16, F32])

    loss, grad_x, g = local_step(x[0], loss_target[0], _prepare_weights(full), reduce_first, reduce_second,
                                 late_weights)
    odd_w_in_t, odd_w_out_t, even_w_out_t = pending["totals_a"]
    even_w_in_t, pack_t = finish("b", grad_x)
    totals = [even_w_in_t, even_w_out_t, odd_w_in_t, odd_w_out_t]
    last_part = jnp.concatenate([_pack(g, LAST_REPLICATED, LAST_ROWS), loss])
    from_core, rep_all, last_all = share_totals(totals, pack_t, last_part)
    me = 2 * chip[0] + core[0]
    mine, theirs = pack_t[:HALF_SHARDED], from_core[4][:HALF_SHARDED]
    sh_total = jnp.where(mc == 0, jnp.concatenate([mine, theirs]), jnp.concatenate([theirs, mine]))
    rep_all = lax.dynamic_update_slice(rep_all, pack_t[None, HALF_SHARDED:], (me, 0, 0))
    rep_total = rep_all.reshape(REPLICATED_ROWS, LANES)
    last_total = sum_parts(lax.dynamic_update_slice(last_all, last_part[None], (me, 0, 0)), "grad_sum_last")
    last_total, loss = last_total[:LAST_ROWS], last_total[LAST_ROWS, 0]
    grads = {}
    grads.update(_unpack(sh_total, SHARDED_SMALL))
    grads.update(_unpack(rep_total, REPLICATED))
    grads.update(_unpack(last_total, LAST_REPLICATED))

    delta, new_m, new_v = {}, {}, {}
    for i, n in enumerate(BIG):
        grads[n], delta[n], new_m[n], new_v[n] = adamw_halves(shard[n], totals[i], from_core[i], m_in[n], v_in[n],
                                                              core, "adamw_" + n)
    small = ((SHARDED_SMALL, SHARDED_ROWS), (REPLICATED, REPLICATED_ROWS), (LAST_REPLICATED, LAST_ROWS))
    packed = [jnp.concatenate([_pack(src, spec, rows) for spec, rows in small]) for src in (shard, m_in, v_in)]
    small_g = jnp.concatenate([sh_total, rep_total, last_total], axis=0)
    outs = adamw(packed[0], small_g, packed[1], packed[2], "adamw_small")
    for dst, packed_rows in zip((delta, new_m, new_v), outs):
        at = 0
        for spec, rows in small:
            dst.update(_unpack(packed_rows[at:at + rows], spec))
            at += rows
    result = [loss, grad_x[None]]
    for group in (grads, delta, new_m, new_v):
        result += [group[n].reshape(given[n].shape) for n in WEIGHTS]
    return tuple(result)
```

```python
import functools

import jax
import jax.numpy as jnp
from jax import lax
from jax.experimental import pallas as pl
from jax.experimental.pallas import tpu as pltpu

F32 = jnp.float32
BF16 = jnp.bfloat16
MESH = pl.DeviceIdType.MESH

D_MODEL = 1024
NORM_EPS = 1e-6
RG_HEADS = 8
RG_HEAD_DIM = 128
RG_C = 8.0
EVEN_IN = 6144
ODD_IN = 3104
ODD_IN_PAD = 3200
GLA_HEADS = 4
GLA_DK = 128
GLA_DV = 256
GLA_RANK = 16
GLA_NORMALIZER = 16.0
GLA_CHUNK = 128
LR_COL = 3072

ADAM_LR = 0.001
ADAM_B1 = 0.9
ADAM_B2 = 0.999
ADAM_EPS = 1e-08
ADAM_WD = 0.01
ADAM_STEP = 10

SUBLANES = 8
LANES = 128
VMEM_LIMIT = 56 * 2 ** 20

ROW_TILE = 512
SCAN_TILE = 256
GLA_BLOCK = 1024
MIX_TILE = 128


def _params(*sem):
    return pltpu.CompilerParams(dimension_semantics=sem, vmem_limit_bytes=VMEM_LIMIT)


def _full(shape):
    n = len(shape)
    return pl.BlockSpec(shape, lambda *_: (0,) * n)


def _sigmoid(x):
    return 0.5 + 0.5 * jnp.tanh(0.5 * x)


def _softplus(x):
    return jnp.maximum(x, 0.0) + jnp.log(1.0 + jnp.exp(-jnp.abs(x)))


def _dot(a, b):
    return jnp.dot(a, b, preferred_element_type=F32)


def _dot_nt(a, b):
    return lax.dot_general(a, b, (((1,), (1,)), ((), ())), preferred_element_type=F32)


def _dot_tn(a, b):
    return lax.dot_general(a, b, (((0,), (0,)), ((), ())), preferred_element_type=F32)


def _bdot(a, b, ca, cb):
    return lax.dot_general(a, b, (((ca,), (cb,)), ((0,), (0,))), preferred_element_type=F32)


def _halo_specs(rows, cols, col_block, n_row_tiles, tix):
    per = rows // SUBLANES
    last = n_row_tiles * per - 1

    def split(args):
        if len(args) == 2:
            return tix(args[1]), col_block + args[0]
        return tix(args[0]), col_block

    def prev(*args):
        t, c = split(args)
        return (jnp.maximum(t * per - 1, 0), c)

    def main(*args):
        return split(args)

    def nxt(*args):
        t, c = split(args)
        return (jnp.minimum((t + 1) * per, last), c)

    return [pl.BlockSpec((SUBLANES, cols), prev), pl.BlockSpec((rows, cols), main),
            pl.BlockSpec((SUBLANES, cols), nxt)]


def _extend(prev_ref, main_ref, next_ref, is_first, is_last):
    p = jnp.where(is_first, 0.0, prev_ref[...])
    n = jnp.where(is_last, 0.0, next_ref[...])
    return jnp.concatenate([p, main_ref[...], n], axis=0)


def _shifted(ext, offset, rows):
    if offset == 0:
        return ext[SUBLANES:SUBLANES + rows]
    n = ext.shape[0]
    return pltpu.roll(ext, (-offset) % n, 0)[SUBLANES:SUBLANES + rows]


def _conv(ext, w, left, rows):
    out = None
    for k in range(w.shape[0]):
        term = _shifted(ext, k - left, rows) * w[k:k + 1]
        out = term if out is None else out + term
    return out


def _conv_transpose(ext, w, left, rows):
    out = None
    for k in range(w.shape[0]):
        term = _shifted(ext, left - k, rows) * w[k:k + 1]
        out = term if out is None else out + term
    return out


def _colsum(x):
    return jnp.sum(x, axis=0, keepdims=True)


def _accumulate(ref, value, step):
    @pl.when(step == 0)
    def _():
        ref[...] = value

    @pl.when(step > 0)
    def _():
        ref[...] += value


PROJ_TILE_BYTES = 7 * 2 ** 20


def _proj_row_tile(rows, width):
    tm = min(ROW_TILE, rows)
    while tm * width * 4 > PROJ_TILE_BYTES and tm % (2 * SUBLANES) == 0:
        tm //= 2
    return tm


def norm_matmul(x, gain, w, name):
    rows, d = x.shape
    n_col_tiles, _, tn = w.shape
    tm = _proj_row_tile(rows, n_col_tiles * tn)

    def body(x_ref, g_ref, w_ref, proj_ref, h_ref):
        xv = x_ref[...]
        rstd = lax.rsqrt(jnp.mean(xv * xv, axis=-1, keepdims=True) + NORM_EPS)
        hv = (xv * rstd * g_ref[...]).astype(BF16)
        h_ref[...] = hv
        for j in range(n_col_tiles):
            proj_ref[:, j * tn:(j + 1) * tn] = _dot(hv, w_ref[j])

    row = lambda cols: pl.BlockSpec((tm, cols), lambda i: (i, 0))
    return pl.pallas_call(
        body, name=name,
        out_shape=(jax.ShapeDtypeStruct((rows, n_col_tiles * tn), F32), jax.ShapeDtypeStruct((rows, d), BF16)),
        grid=(rows // tm,),
        in_specs=[row(d), _full((1, d)), _full(w.shape)],
        out_specs=(row(n_col_tiles * tn), row(d)),
        compiler_params=_params("parallel"),
    )(x, gain, w)


def inproj_bwd(dproj, w, x, gain, dres, name):
    rows, d = x.shape
    n_col_tiles, _, tn = w.shape
    tm = _proj_row_tile(rows, n_col_tiles * tn)

    def body(dp_ref, w_ref, x_ref, g_ref, dres_ref, dx_ref, dg_ref):
        dh = None
        for j in range(n_col_tiles):
            part = _dot_nt(dp_ref[:, j * tn:(j + 1) * tn], w_ref[j])
            dh = part if dh is None else dh + part
        _inproj_finish(dh, x_ref, g_ref, dres_ref, dx_ref, dg_ref, pl.program_id(0))

    row = lambda cols: pl.BlockSpec((tm, cols), lambda i: (i, 0))
    return pl.pallas_call(
        body, name=name,
        out_shape=(jax.ShapeDtypeStruct((rows, d), F32), jax.ShapeDtypeStruct((1, d), F32)),
        grid=(rows // tm,),
        in_specs=[row(n_col_tiles * tn), _full(w.shape), row(d), _full((1, d)), row(d)],
        out_specs=(row(d), _full((1, d))),
        compiler_params=_params("arbitrary"),
    )(dproj, w, x, gain, dres)


def _inproj_finish(dh, x_ref, g_ref, dres_ref, dx_ref, dg_ref, step):
    xv = x_ref[...]
    rstd = lax.rsqrt(jnp.mean(xv * xv, axis=-1, keepdims=True) + NORM_EPS)
    xhat = xv * rstd
    dxn = dh * g_ref[...]
    dx_ref[...] = dres_ref[...] + rstd * (dxn - xhat * jnp.mean(dxn * xhat, axis=-1, keepdims=True))
    _accumulate(dg_ref, _colsum(dh * xhat), step)


def inproj_bwd_pieces(pieces, w, x, gain, dres, name):
    rows, d = x.shape
    tm = min(ROW_TILE, rows)
    n = len(pieces)
    widths = [p.shape[1] for p in pieces]
    starts = [sum(widths[:k]) for k in range(n)]
    assert sum(widths) == w.shape[2]

    def body(*refs):
        w_ref, x_ref, g_ref, dres_ref, dx_ref, dg_ref = refs[n:]
        dh = None
        for k in range(n):
            part = _dot_nt(refs[k][...], w_ref[0, :, starts[k]:starts[k] + widths[k]])
            dh = part if dh is None else dh + part
        _inproj_finish(dh, x_ref, g_ref, dres_ref, dx_ref, dg_ref, pl.program_id(0))

    row = lambda cols: pl.BlockSpec((tm, cols), lambda i: (i, 0))
    return pl.pallas_call(
        body, name=name,
        out_shape=(jax.ShapeDtypeStruct((rows, d), F32), jax.ShapeDtypeStruct((1, d), F32)),
        grid=(rows // tm,),
        in_specs=[row(wd) for wd in widths] + [_full(w.shape), row(d), _full((1, d)), row(d)],
        out_specs=(row(d), _full((1, d))),
        compiler_params=_params("arbitrary"),
    )(*pieces, w, x, gain, dres)


def matmul_dw_pieces(a, pieces, name):
    rows, m = a.shape
    tk = min(2 * ROW_TILE, rows)
    n = len(pieces)

    def body(*refs):
        a_ref, ins, outs = refs[0], refs[1:1 + n], refs[1 + n:]
        av = a_ref[...]
        for k in range(n):
            _accumulate(outs[k], _dot_tn(av, ins[k][...]), pl.program_id(0))

    return pl.pallas_call(
        body, name=name,
        out_shape=[jax.ShapeDtypeStruct((m, p.shape[1]), F32) for p in pieces],
        grid=(rows // tk,),
        in_specs=[pl.BlockSpec((tk, m), lambda k: (k, 0))]
        + [pl.BlockSpec((tk, p.shape[1]), lambda k: (k, 0)) for p in pieces],
        out_specs=[_full((m, p.shape[1])) for p in pieces],
        compiler_params=_params("arbitrary"),
    )(a, *pieces)


def matmul_dw(a, b, bn, name):
    rows, m = a.shape
    n = b.shape[1]
    tk = min(4 * ROW_TILE, rows)
    steps = rows // tk

    def body(a_ref, b_ref, o_ref):
        part = _dot_tn(a_ref[...], b_ref[...])

        @pl.when(pl.program_id(1) == 0)
        def _():
            o_ref[0] = part

        @pl.when(pl.program_id(1) > 0)
        def _():
            o_ref[0] += part

    return pl.pallas_call(
        body, name=name,
        out_shape=jax.ShapeDtypeStruct((n // bn, m, bn), F32),
        grid=(n // bn, steps),
        in_specs=[pl.BlockSpec((tk, m), lambda j, k: (k, 0)), pl.BlockSpec((tk, bn), lambda j, k: (k, j))],
        out_specs=pl.BlockSpec((1, m, bn), lambda j, k: (j, 0, 0)),
        compiler_params=_params("parallel", "arbitrary"),
    )(a, b)


def _scan(a, b, carry, reverse):
    n = a.shape[0]
    pos = lax.broadcasted_iota(jnp.int32, a.shape, 0) % SUBLANES
    s = 1
    while s < SUBLANES:
        if reverse:
            a_s, b_s, valid = pltpu.roll(a, n - s, 0), pltpu.roll(b, n - s, 0), pos < SUBLANES - s
        else:
            a_s, b_s, valid = pltpu.roll(a, s, 0), pltpu.roll(b, s, 0), pos >= s
        b = jnp.where(valid, a * b_s + b, b)
        a = jnp.where(valid, a * a_s, a)
        s *= 2
    blocks = n // SUBLANES
    out = [None] * blocks
    for k in (range(blocks - 1, -1, -1) if reverse else range(blocks)):
        rows = slice(k * SUBLANES, (k + 1) * SUBLANES)
        h = a[rows] * carry + b[rows]
        out[k] = h
        carry = h[0:1] if reverse else h[SUBLANES - 1:SUBLANES]
    return jnp.concatenate(out, axis=0)


def _rg_gates(ua, gw_ref, gb, lam):
    ub = ua.astype(BF16)
    pre_r, pre_i = [], []
    for h in range(RG_HEADS):
        z = _dot(ub[:, h * RG_HEAD_DIM:(h + 1) * RG_HEAD_DIM], gw_ref[h])
        pre_r.append(z[:, :RG_HEAD_DIM])
        pre_i.append(z[:, RG_HEAD_DIM:])
    r = _sigmoid(jnp.concatenate(pre_r, axis=1) + gb[0:1])
    i = _sigmoid(jnp.concatenate(pre_i, axis=1) + gb[1:2])
    sp = _softplus(-lam)
    log_a = -RG_C * r * sp
    a = jnp.exp(log_a)
    mult = jnp.sqrt(1.0 - a * a)
    return r, i, sp, a, mult


def _rg_weight_specs():
    return [_full((4, D_MODEL)), _full((1, D_MODEL)), _full((RG_HEADS, RG_HEAD_DIM, 2 * RG_HEAD_DIM)),
            _full((2, D_MODEL)), _full((1, D_MODEL))]


def rglru_fwd(proj, conv_w, conv_b, gate_w, gate_b, lam, reverse, name):
    rows_total = proj.shape[0]
    rows = min(SCAN_TILE, rows_total)
    n_tiles = rows_total // rows
    tix = (lambda i: n_tiles - 1 - i) if reverse else (lambda i: i)

    def body(xp, xm, xn, cw_ref, cb_ref, gw_ref, gb_ref, lam_ref, h_ref, acts_ref, carry):
        i = pl.program_id(0)
        t = tix(i)
        ext = _extend(xp, xm, xn, t == 0, t == n_tiles - 1)
        ua = _conv(ext, cw_ref[...], 2, rows) + cb_ref[...]
        r, gi, _, a, mult = _rg_gates(ua, gw_ref, gb_ref[...], lam_ref[...])
        for k, saved in enumerate((ua, r, gi, a, mult)):
            acts_ref[k] = saved
        b = mult * (gi * ua)

        @pl.when(i == 0)
        def _():
            carry[...] = jnp.zeros_like(carry)

        h = _scan(a, b, carry[0:1], reverse)
        h_ref[...] = h
        edge = h[0:1] if reverse else h[rows - 1:rows]
        carry[...] = jnp.broadcast_to(edge, carry.shape)

    return pl.pallas_call(
        body, name=name,
        out_shape=(jax.ShapeDtypeStruct((rows_total, D_MODEL), F32),
                   jax.ShapeDtypeStruct((5, rows_total, D_MODEL), F32)),
        grid=(n_tiles,),
        in_specs=_halo_specs(rows, D_MODEL, 0, n_tiles, tix) + _rg_weight_specs(),
        out_specs=(pl.BlockSpec((rows, D_MODEL), lambda i: (tix(i), 0)),
                   pl.BlockSpec((5, rows, D_MODEL), lambda i: (0, tix(i), 0))),
        scratch_shapes=[pltpu.VMEM((SUBLANES, D_MODEL), F32)],
        compiler_params=_params("arbitrary"),
    )(proj, proj, proj, conv_w, conv_b, gate_w, gate_b, lam)


def rglru_bwd(proj, dycat, h_dir, acts, gate_w, lam, reverse, name):
    rows_total = proj.shape[0]
    rows = min(SCAN_TILE, rows_total)
    n_tiles = rows_total // rows
    tix = (lambda i: i) if reverse else (lambda i: n_tiles - 1 - i)
    za_block = 1

    def body(acts_ref, za_ref, dya_ref, hp, hm, hn, gw_ref, lam_ref, dua_ref, dgw_ref, dgb_ref, dlam_ref, carry):
        step = pl.program_id(0)
        t = tix(step)
        first, last = t == 0, t == n_tiles - 1
        ua, r, gi, a, mult = (acts_ref[k] for k in range(5))
        lam_v = lam_ref[...]
        sp = _softplus(-lam_v)
        za = za_ref[...]
        dh = dya_ref[...] * (za * _sigmoid(za))

        @pl.when(step == 0)
        def _():
            carry[...] = jnp.zeros_like(carry)

        old = carry[0:1]
        mu = _scan(a, a * dh, old, not reverse)
        row = lax.broadcasted_iota(jnp.int32, mu.shape, 0)
        if reverse:
            mu_next = jnp.where(row == 0, old, pltpu.roll(mu, 1, 0))
            carry[...] = jnp.broadcast_to(mu[rows - 1:rows], carry.shape)
            h_ext = _extend(hp, hm, hn, first, last)
            h_prev = _shifted(h_ext, 1, rows)
        else:
            mu_next = jnp.where(row == rows - 1, old, pltpu.roll(mu, rows - 1, 0))
            carry[...] = jnp.broadcast_to(mu[0:1], carry.shape)
            h_ext = _extend(hp, hm, hn, first, last)
            h_prev = _shifted(h_ext, -1, rows)
        db = dh + mu_next
        da = db * h_prev
        d_mult = db * (gi * ua)
        di = db * (mult * ua)
        dua = db * (mult * gi)
        dlog_a = da * a - d_mult * (a * a) / mult
        dr = dlog_a * (-RG_C * sp)
        dlam = _colsum(dlog_a * (-RG_C * r)) * (-_sigmoid(-lam_v))
        dpr = dr * (r * (1.0 - r))
        dpi = di * (gi * (1.0 - gi))
        dgb = jnp.concatenate([_colsum(dpr), _colsum(dpi)], axis=0)
        ub = ua.astype(BF16)
        dua_heads, dgw_heads = [], []
        for h in range(RG_HEADS):
            cols = slice(h * RG_HEAD_DIM, (h + 1) * RG_HEAD_DIM)
            dz = jnp.concatenate([dpr[:, cols], dpi[:, cols]], axis=1).astype(BF16)
            dgw_heads.append(_dot_tn(ub[:, cols], dz))
            dua_heads.append(_dot_nt(dz, gw_ref[h]))
        dua_ref[...] = dua + jnp.concatenate(dua_heads, axis=1)

        @pl.when(step == 0)
        def _():
            for h in range(RG_HEADS):
                dgw_ref[h] = dgw_heads[h]
            dgb_ref[...] = dgb
            dlam_ref[...] = dlam

        @pl.when(step > 0)
        def _():
            for h in range(RG_HEADS):
                dgw_ref[h] += dgw_heads[h]
            dgb_ref[...] += dgb
            dlam_ref[...] += dlam

    row_spec = lambda col: pl.BlockSpec((rows, D_MODEL), lambda i: (tix(i), col))
    return pl.pallas_call(
        body, name=name,
        out_shape=(jax.ShapeDtypeStruct((rows_total, D_MODEL), F32),
                   jax.ShapeDtypeStruct((RG_HEADS, RG_HEAD_DIM, 2 * RG_HEAD_DIM), F32),
                   jax.ShapeDtypeStruct((2, D_MODEL), F32), jax.ShapeDtypeStruct((1, D_MODEL), F32)),
        grid=(n_tiles,),
        in_specs=([pl.BlockSpec((5, rows, D_MODEL), lambda i: (0, tix(i), 0)), row_spec(za_block), row_spec(0)]
                  + _halo_specs(rows, D_MODEL, 0, n_tiles, tix)
                  + [_full((RG_HEADS, RG_HEAD_DIM, 2 * RG_HEAD_DIM)), _full((1, D_MODEL))]),
        out_specs=(row_spec(0), _full((RG_HEADS, RG_HEAD_DIM, 2 * RG_HEAD_DIM)), _full((2, D_MODEL)),
                   _full((1, D_MODEL))),
        scratch_shapes=[pltpu.VMEM((SUBLANES, D_MODEL), F32)],
        compiler_params=_params("arbitrary"),
    )(acts, proj, dycat, h_dir, h_dir, h_dir, gate_w, lam)


def even_mix_fwd(proj, h_f, h_b, sc_w, name):
    rows_total = proj.shape[0]
    rows = min(2 * MIX_TILE, rows_total)
    n_tiles = rows_total // rows
    cb = D_MODEL
    n_cb = 1
    ident = lambda i: i

    def body(za_ref, hf_ref, hb_ref, xbp, xbm, xbn, gcp, gcm, gcn, gb_ref, zb_ref, w_ref, y_ref):
        t = pl.program_id(1)
        first, last = t == 0, t == n_tiles - 1
        za = za_ref[...]
        y_ref[:, 0:cb] = ((hf_ref[...] + hb_ref[...]) * (za * _sigmoid(za))).astype(BF16)
        p_ext = _extend(xbp, xbm, xbn, first, last) * _extend(gcp, gcm, gcn, first, last)
        cv = _conv(p_ext, w_ref[...], 1, rows)
        zb = zb_ref[...]
        y_ref[:, cb:2 * cb] = (gb_ref[...] * cv * (zb * _sigmoid(zb))).astype(BF16)

    blk = lambda col: pl.BlockSpec((rows, cb), lambda c, i: (i, col * n_cb + c))
    own = pl.BlockSpec((rows, cb), lambda c, i: (i, c))
    return pl.pallas_call(
        body, name=name,
        out_shape=jax.ShapeDtypeStruct((rows_total, 2 * D_MODEL), BF16),
        grid=(n_cb, n_tiles),
        in_specs=([blk(1), own, own] + _halo_specs(rows, cb, 2 * n_cb, n_tiles, ident)
                  + _halo_specs(rows, cb, 4 * n_cb, n_tiles, ident)
                  + [blk(3), blk(5), pl.BlockSpec((3, cb), lambda c, i: (0, c))]),
        out_specs=pl.BlockSpec((rows, 2 * cb), lambda c, i: (i, 0)),
        compiler_params=_params("parallel", "arbitrary"),
    )(proj, h_f, h_b, proj, proj, proj, proj, proj, proj, proj, proj, sc_w)


def even_mix_bwd(proj, dycat, h_f, h_b, dua_f, dua_b, conv_w, sc_w, name):
    rows_total = proj.shape[0]
    rows = min(MIX_TILE, rows_total)
    n_tiles = rows_total // rows
    cb = D_MODEL
    n_cb = 1
    ident = lambda i: i

    def body(xap, xam, xan, za_ref, xbp, xbm, xbn, gbp, gbm, gbn, gcp, gcm, gcn, zbp, zbm, zbn,
             dya_ref, dybp, dybm, dybn, hf_ref, hb_ref, dfp, dfm, dfn, dbp, dbm, dbn, cw_ref, sw_ref,
             dp_ref, dcw_ref, dcb_ref, dsw_ref):
        def put(k, value):
            dp_ref[:, k * cb:(k + 1) * cb] = value.astype(BF16)

        t = pl.program_id(1)
        first, last = t == 0, t == n_tiles - 1
        za = za_ref[...]
        sa = _sigmoid(za)
        put(1, dya_ref[...] * (hf_ref[...] + hb_ref[...]) * (sa * (1.0 + za * (1.0 - sa))))
        dua_ext = _extend(dfp, dfm, dfn, first, last) + _extend(dbp, dbm, dbn, first, last)
        cw = cw_ref[...]
        put(0, _conv_transpose(dua_ext, cw, 2, rows))
        dua = dua_ext[SUBLANES:SUBLANES + rows]
        xa_ext = _extend(xap, xam, xan, first, last)
        dcw = jnp.concatenate([_colsum(dua * _shifted(xa_ext, k - 2, rows)) for k in range(4)], axis=0)
        dcb = _colsum(dua)
        xb_ext = _extend(xbp, xbm, xbn, first, last)
        gc_ext = _extend(gcp, gcm, gcn, first, last)
        p_ext = xb_ext * gc_ext
        zb_ext = _extend(zbp, zbm, zbn, first, last)
        sb_ext = _sigmoid(zb_ext)
        dyb_ext = _extend(dybp, dybm, dybn, first, last)
        gb_ext = _extend(gbp, gbm, gbn, first, last)
        dcv_ext = dyb_ext * gb_ext * (zb_ext * sb_ext)
        sw = sw_ref[...]
        cv = _conv(p_ext, sw, 1, rows)
        mid = slice(SUBLANES, SUBLANES + rows)
        zb, sb, dyb, gb = zb_ext[mid], sb_ext[mid], dyb_ext[mid], gb_ext[mid]
        put(3, dyb * cv * (zb * sb))
        put(5, dyb * gb * cv * (sb * (1.0 + zb * (1.0 - sb))))
        dp = _conv_transpose(dcv_ext, sw, 1, rows)
        put(4, dp * xb_ext[mid])
        put(2, dp * gc_ext[mid])
        dcv = dcv_ext[mid]
        dsw = jnp.concatenate([_colsum(dcv * _shifted(p_ext, k - 1, rows)) for k in range(3)], axis=0)

        @pl.when(t == 0)
        def _():
            dcw_ref[...] = dcw
            dcb_ref[...] = dcb
            dsw_ref[...] = dsw

        @pl.when(t > 0)
        def _():
            dcw_ref[...] += dcw
            dcb_ref[...] += dcb
            dsw_ref[...] += dsw

    blk = lambda col: pl.BlockSpec((rows, cb), lambda c, i: (i, col * n_cb + c))
    halo = lambda col: _halo_specs(rows, cb, col * n_cb, n_tiles, ident)
    own = pl.BlockSpec((rows, cb), lambda c, i: (i, c))
    wspec = lambda k: pl.BlockSpec((k, cb), lambda c, i: (0, c))
    return pl.pallas_call(
        body, name=name,
        out_shape=(jax.ShapeDtypeStruct((rows_total, 6 * D_MODEL), BF16),
                   jax.ShapeDtypeStruct((4, D_MODEL), F32), jax.ShapeDtypeStruct((1, D_MODEL), F32),
                   jax.ShapeDtypeStruct((3, D_MODEL), F32)),
        grid=(n_cb, n_tiles),
        in_specs=(halo(0) + [blk(1)] + halo(2) + halo(3) + halo(4) + halo(5) + [blk(0)] + halo(1)
                  + [own, own] + halo(0) + halo(0) + [wspec(4), wspec(3)]),
        out_specs=(pl.BlockSpec((rows, 6 * cb), lambda c, i: (i, 0)), wspec(4), wspec(1), wspec(3)),
        compiler_params=_params("parallel", "arbitrary"),
    )(proj, proj, proj, proj, proj, proj, proj, proj, proj, proj, proj, proj, proj, proj, proj, proj,
      dycat, dycat, dycat, dycat, h_f, h_b, dua_f, dua_f, dua_f, dua_b, dua_b, dua_b, conv_w, sc_w)


def even_out_fwd(ycat, w_out, gain, x, name):
    rows, d = x.shape
    k = ycat.shape[1]
    tm = min(ROW_TILE, rows)

    def body(yc_ref, w_ref, g_ref, x_ref, x1_ref, y_ref):
        y = _dot(yc_ref[...], w_ref[...])
        y_ref[...] = y
        rstd = lax.rsqrt(jnp.mean(y * y, axis=-1, keepdims=True) + NORM_EPS)
        x1_ref[...] = x_ref[...] + y * rstd * g_ref[...]

    row = lambda n: pl.BlockSpec((tm, n), lambda i: (i, 0))
    return pl.pallas_call(
        body, name=name,
        out_shape=(jax.ShapeDtypeStruct((rows, d), F32),) * 2,
        grid=(rows // tm,),
        in_specs=[row(k), _full((k, d)), _full((1, d)), row(d)],
        out_specs=(row(d), row(d)),
        compiler_params=_params("parallel"),
    )(ycat, w_out, gain, x)


def _rmsnorm_bwd(dout, y, gain):
    rstd = lax.rsqrt(jnp.mean(y * y, axis=-1, keepdims=True) + NORM_EPS)
    yhat = y * rstd
    dyn = dout * gain
    dy = rstd * (dyn - yhat * jnp.mean(dyn * yhat, axis=-1, keepdims=True))
    return dy, dout * yhat


def even_out_bwd(dx1, y, gain, w_out, name):
    rows, d = y.shape
    k = w_out.shape[0]
    tm = min(ROW_TILE, rows)

    def body(dx_ref, y_ref, g_ref, w_ref, dy_ref, dyc_ref, dg_ref):
        dy, dg_rows = _rmsnorm_bwd(dx_ref[...], y_ref[...], g_ref[...])
        dyb = dy.astype(BF16)
        dy_ref[...] = dyb
        dyc_ref[...] = _dot_nt(dyb, w_ref[...])
        _accumulate(dg_ref, _colsum(dg_rows), pl.program_id(0))

    row = lambda n: pl.BlockSpec((tm, n), lambda i: (i, 0))
    return pl.pallas_call(
        body, name=name,
        out_shape=(jax.ShapeDtypeStruct((rows, d), BF16), jax.ShapeDtypeStruct((rows, k), F32),
                   jax.ShapeDtypeStruct((1, d), F32)),
        grid=(rows // tm,),
        in_specs=[row(d), row(d), _full((1, d)), _full((k, d))],
        out_specs=(row(d), row(k), _full((1, d))),
        compiler_params=_params("arbitrary"),
    )(dx1, y, gain, w_out)


def _chunk_cumsum(g, reverse):
    n = g.shape[0]
    pos = lax.broadcasted_iota(jnp.int32, g.shape, 0) % GLA_CHUNK
    s = 1
    while s < GLA_CHUNK:
        if reverse:
            g = g + jnp.where(pos < GLA_CHUNK - s, pltpu.roll(g, n - s, 0), 0.0)
        else:
            g = g + jnp.where(pos >= s, pltpu.roll(g, s, 0), 0.0)
        s *= 2
    return g


def _gla_prepare(q_ref, k_ref, lr_ref, wg_ref, bg_ref, reverse, n_chunks):
    z = _dot(lr_ref[...].astype(BF16), wg_ref[0]) + bg_ref[0]
    g = -_softplus(-z) * (1.0 / GLA_NORMALIZER)
    bcum = _chunk_cumsum(g, reverse).reshape(n_chunks, GLA_CHUNK, GLA_DK)
    edge = 0 if reverse else GLA_CHUNK - 1
    btot = bcum[:, edge:edge + 1, :]
    e_pos = jnp.exp(bcum)
    e_neg = jnp.exp(-bcum)
    e_st = jnp.exp(btot - bcum)
    q3 = q_ref[...].reshape(n_chunks, GLA_CHUNK, GLA_DK)
    k3 = k_ref[...].reshape(n_chunks, GLA_CHUNK, GLA_DK)
    scale = GLA_DK ** -0.5
    q_in = q3 * scale * e_pos
    k_in = k3 * e_neg
    k_st = k3 * e_st
    dec = jnp.exp(btot)
    return z, q_in, k_in, k_st, dec, (scale * e_pos, e_neg, e_st)


def _gla_mask(reverse):
    i = lax.broadcasted_iota(jnp.int32, (GLA_CHUNK, GLA_CHUNK), 0)
    j = lax.broadcasted_iota(jnp.int32, (GLA_CHUNK, GLA_CHUNK), 1)
    return (j >= i) if reverse else (j <= i)


def _gla_specs(rows, n_blocks, reverse):
    tix = (lambda s: n_blocks - 1 - s) if reverse else (lambda s: s)
    d = 1 if reverse else 0
    lr_block = LR_COL // LANES
    specs = [pl.BlockSpec((rows, GLA_DK), lambda h, s: (tix(s), h)),
             pl.BlockSpec((rows, GLA_DK), lambda h, s: (tix(s), GLA_HEADS + h)),
             pl.BlockSpec((rows, GLA_DV), lambda h, s: (tix(s), GLA_HEADS + h)),
             pl.BlockSpec((rows, LANES), lambda h, s: (tix(s), lr_block)),
             pl.BlockSpec((1, LANES, GLA_DK), lambda h, s: (d, 0, h)),
             pl.BlockSpec((1, 1, GLA_DK), lambda h, s: (d, 0, h))]
    return specs, tix


def gla_fwd(proj, wg_pad, bg, reverse, name):
    rows_total = proj.shape[0]
    rows = min(GLA_BLOCK, rows_total)
    n_blocks = rows_total // rows
    n_chunks = rows // GLA_CHUNK
    specs, tix = _gla_specs(rows, n_blocks, reverse)

    def body(q_ref, k_ref, v_ref, lr_ref, wg_ref, bg_ref, o_ref, st_ref, state, kv_scr, dec_scr):
        _, q_in, k_in, k_st, dec, _ = _gla_prepare(q_ref, k_ref, lr_ref, wg_ref, bg_ref, reverse, n_chunks)
        vb = v_ref[...].reshape(n_chunks, GLA_CHUNK, GLA_DV).astype(BF16)
        qb = q_in.astype(BF16)
        p = jnp.where(_gla_mask(reverse), _bdot(qb, k_in.astype(BF16), 2, 2), 0.0)
        o = _bdot(p.astype(BF16), vb, 2, 1)
        kv_scr[...] = _bdot(vb, k_st.astype(BF16), 1, 1)
        dec_scr[...] = jnp.broadcast_to(dec, dec_scr.shape)

        @pl.when(pl.program_id(1) == 0)
        def _():
            state[...] = jnp.zeros_like(state)

        for c in range(n_chunks):
            cc = n_chunks - 1 - c if reverse else c
            st_ref[0, cc] = state[...]
            state[...] = state[...] * dec_scr[cc, 0:1] + kv_scr[cc]
        o = o + _bdot(qb, st_ref[0].astype(BF16), 2, 2)
        o_ref[...] = o.reshape(rows, GLA_DV)

    return pl.pallas_call(
        body, name=name,
        out_shape=(jax.ShapeDtypeStruct((rows_total, GLA_HEADS * GLA_DV), F32),
                   jax.ShapeDtypeStruct((GLA_HEADS, rows_total // GLA_CHUNK, GLA_DV, GLA_DK), F32)),
        grid=(GLA_HEADS, n_blocks),
        in_specs=specs,
        out_specs=(pl.BlockSpec((rows, GLA_DV), lambda h, s: (tix(s), h)),
                   pl.BlockSpec((1, n_chunks, GLA_DV, GLA_DK), lambda h, s: (h, tix(s), 0, 0))),
        scratch_shapes=[pltpu.VMEM((GLA_DV, GLA_DK), F32), pltpu.VMEM((n_chunks, GLA_DV, GLA_DK), F32),
                        pltpu.VMEM((n_chunks, SUBLANES, GLA_DK), F32)],
        compiler_params=_params("parallel", "arbitrary"),
    )(proj, proj, proj, proj, wg_pad, bg)


def gla_bwd(proj, wg_pad, bg, d_o, states, dqkv_in, reverse, name):
    rows_total = proj.shape[0]
    rows = min(GLA_BLOCK, rows_total)
    n_blocks = rows_total // rows
    n_chunks = rows // GLA_CHUNK
    specs, tix = _gla_specs(rows, n_blocks, not reverse)
    d = 1 if reverse else 0
    specs[4] = pl.BlockSpec((1, LANES, GLA_DK), lambda h, s: (d, 0, h))
    specs[5] = pl.BlockSpec((1, 1, GLA_DK), lambda h, s: (d, 0, h))
    add = dqkv_in is not None

    def body(*refs):
        q_ref, k_ref, v_ref, lr_ref, wg_ref, bg_ref, do_ref, st_ref = refs[:8]
        refs = refs[8:]
        if add:
            aq_ref, ak_ref, av_ref = refs[:3]
            refs = refs[3:]
        dq_ref, dk_ref, dv_ref, dz_ref, dstate, g_scr, dec_scr, dsn_scr = refs
        z, q_in, k_in, k_st, dec, (f_q, f_k, f_s) = _gla_prepare(q_ref, k_ref, lr_ref, wg_ref, bg_ref, reverse,
                                                                 n_chunks)
        mask = _gla_mask(reverse)
        vb = v_ref[...].reshape(n_chunks, GLA_CHUNK, GLA_DV).astype(BF16)
        dob = do_ref[...].reshape(n_chunks, GLA_CHUNK, GLA_DV).astype(BF16)
        qb, kb, ksb = q_in.astype(BF16), k_in.astype(BF16), k_st.astype(BF16)
        st = st_ref[0]
        stb = st.astype(BF16)
        pb = jnp.where(mask, _bdot(qb, kb, 2, 2), 0.0).astype(BF16)
        dpb = jnp.where(mask, _bdot(dob, vb, 2, 2), 0.0).astype(BF16)
        d_qin = _bdot(dpb, kb, 2, 1) + _bdot(dob, stb, 2, 1)
        d_kin = _bdot(dpb, qb, 1, 1)
        dv = _bdot(pb, dob, 1, 1)
        g_scr[...] = _bdot(dob, qb, 1, 1)
        dec_scr[...] = jnp.broadcast_to(dec, dec_scr.shape)

        @pl.when(pl.program_id(1) == 0)
        def _():
            dstate[...] = jnp.zeros_like(dstate)

        for c in range(n_chunks):
            cc = c if reverse else n_chunks - 1 - c
            dsn_scr[cc] = dstate[...]
            dstate[...] = dstate[...] * dec_scr[cc, 0:1] + g_scr[cc]
        dsn = dsn_scr[...]
        dsnb = dsn.astype(BF16)
        dv = dv + _bdot(ksb, dsnb, 2, 2)
        d_kst = _bdot(vb, dsnb, 2, 1)
        d_dec = jnp.sum(dsn * st, axis=1, keepdims=True)
        ks_term = d_kst * k_st
        d_btot = d_dec * dec + jnp.sum(ks_term, axis=1, keepdims=True)
        d_b = d_qin * q_in - d_kin * k_in - ks_term
        pos = lax.broadcasted_iota(jnp.int32, d_b.shape, 1)
        edge = 0 if reverse else GLA_CHUNK - 1
        d_b = d_b + jnp.where(pos == edge, d_btot, 0.0)
        dg = _chunk_cumsum(d_b.reshape(rows, GLA_DK), not reverse)
        dz_ref[...] = dg * (1.0 / GLA_NORMALIZER) * _sigmoid(-z)
        dq = (d_qin * f_q).reshape(rows, GLA_DK)
        dk = (d_kin * f_k + d_kst * f_s).reshape(rows, GLA_DK)
        dv = dv.reshape(rows, GLA_DV)
        if add:
            dq_ref[...] = (dq + aq_ref[...]).astype(BF16)
            dk_ref[...] = (dk + ak_ref[...]).astype(BF16)
            dv_ref[...] = (dv + av_ref[...]).astype(BF16)
        else:
            dq_ref[...] = dq
            dk_ref[...] = dk
            dv_ref[...] = dv

    qkv_specs = [pl.BlockSpec((rows, GLA_DK), lambda h, s: (tix(s), h)),
                 pl.BlockSpec((rows, GLA_DK), lambda h, s: (tix(s), h)),
                 pl.BlockSpec((rows, GLA_DV), lambda h, s: (tix(s), h))]
    in_specs = specs + [pl.BlockSpec((rows, GLA_DV), lambda h, s: (tix(s), h)),
                        pl.BlockSpec((1, n_chunks, GLA_DV, GLA_DK), lambda h, s: (h, tix(s), 0, 0))]
    args = [proj, proj, proj, proj, wg_pad, bg, d_o, states]
    out_dtype = F32
    if add:
        in_specs += qkv_specs
        args += list(dqkv_in)
        out_dtype = BF16
    return pl.pallas_call(
        body, name=name,
        out_shape=(jax.ShapeDtypeStruct((rows_total, GLA_HEADS * GLA_DK), out_dtype),
                   jax.ShapeDtypeStruct((rows_total, GLA_HEADS * GLA_DK), out_dtype),
                   jax.ShapeDtypeStruct((rows_total, GLA_HEADS * GLA_DV), out_dtype),
                   jax.ShapeDtypeStruct((rows_total, GLA_HEADS * GLA_DK), F32)),
        grid=(GLA_HEADS, n_blocks),
        in_specs=in_specs,
        out_specs=(pl.BlockSpec((rows, GLA_DK), lambda h, s: (tix(s), h)),
                   pl.BlockSpec((rows, GLA_DK), lambda h, s: (tix(s), h)),
                   pl.BlockSpec((rows, GLA_DV), lambda h, s: (tix(s), h)),
                   pl.BlockSpec((rows, GLA_DK), lambda h, s: (tix(s), h))),
        scratch_shapes=[pltpu.VMEM((GLA_DV, GLA_DK), F32), pltpu.VMEM((n_chunks, GLA_DV, GLA_DK), F32),
                        pltpu.VMEM((n_chunks, SUBLANES, GLA_DK), F32),
                        pltpu.VMEM((n_chunks, GLA_DV, GLA_DK), F32)],
        compiler_params=_params("parallel", "arbitrary"),
    )(*args)


def gla_gate_bwd(proj, dz_f, dz_b, wg_pad, name):
    rows_total = proj.shape[0]
    tm = min(ROW_TILE, rows_total)
    n_key = GLA_HEADS * GLA_DK

    def body(lr_ref, dzf_ref, dzb_ref, wg_ref, dlr_ref, dwg_ref, dbg_ref):
        step = pl.program_id(0)
        lr_t = jnp.transpose(lr_ref[...])
        dzf, dzb = dzf_ref[...], dzb_ref[...]
        dzf16, dzb16 = dzf.astype(BF16), dzb.astype(BF16)
        dlr_ref[...] = (_dot_nt(dzf16, wg_ref[0]) + _dot_nt(dzb16, wg_ref[1])).astype(BF16)
        dwf = _dot(lr_t[0:GLA_RANK].astype(BF16), dzf16)
        dwb = _dot(lr_t[GLA_RANK:2 * GLA_RANK].astype(BF16), dzb16)
        dbg = jnp.concatenate([_colsum(dzf), _colsum(dzb)], axis=0)

        @pl.when(step == 0)
        def _():
            dwg_ref[0] = dwf
            dwg_ref[1] = dwb
            dbg_ref[...] = dbg

        @pl.when(step > 0)
        def _():
            dwg_ref[0] += dwf
            dwg_ref[1] += dwb
            dbg_ref[...] += dbg

    return pl.pallas_call(
        body, name=name,
        out_shape=(jax.ShapeDtypeStruct((rows_total, LANES), BF16), jax.ShapeDtypeStruct((2, GLA_RANK, n_key), F32),
                   jax.ShapeDtypeStruct((2, n_key), F32)),
        grid=(rows_total // tm,),
        in_specs=[pl.BlockSpec((tm, LANES), lambda i: (i, LR_COL // LANES)),
                  pl.BlockSpec((tm, n_key), lambda i: (i, 0)), pl.BlockSpec((tm, n_key), lambda i: (i, 0)),
                  _full((2, LANES, n_key))],
        out_specs=(pl.BlockSpec((tm, LANES), lambda i: (i, 0)), _full((2, GLA_RANK, n_key)), _full((2, n_key))),
        compiler_params=_params("arbitrary"),
    )(proj, dz_f, dz_b, wg_pad)


def _head_norm(o, gain):
    outs, hats, rstds = [], [], []
    for h in range(GLA_HEADS):
        oh = o[:, h * GLA_DV:(h + 1) * GLA_DV]
        rstd = lax.rsqrt(jnp.mean(oh * oh, axis=-1, keepdims=True) + NORM_EPS)
        hat = oh * rstd
        outs.append(hat * gain)
        hats.append(hat)
        rstds.append(rstd)
    return outs, hats, rstds


def odd_out_fwd(o_f, o_b, proj, head_gain, w_out, gain, x1, target, name):
    rows, d = x1.shape
    tm = min(ROW_TILE, rows)
    r_block = (2 * GLA_HEADS * GLA_DK + GLA_HEADS * GLA_DV) // d

    def body(of_ref, ob_ref, r_ref, hg_ref, w_ref, g_ref, x1_ref, tgt_ref, y2_ref, dy_ref, dx2_ref, loss_ref,
             dg_ref):
        step = pl.program_id(0)
        on, _, _ = _head_norm(of_ref[...] + ob_ref[...], hg_ref[...])
        r = r_ref[...]
        y2 = (jnp.concatenate(on, axis=1) * (r * _sigmoid(r))).astype(BF16)
        y2_ref[...] = y2
        y = _dot(y2, w_ref[...])
        gain_v = g_ref[...]
        rstd = lax.rsqrt(jnp.mean(y * y, axis=-1, keepdims=True) + NORM_EPS)
        x2 = x1_ref[...] + y * rstd * gain_v
        diff = x2 - tgt_ref[...]
        loss = 0.5 * jnp.sum(jnp.mean(diff * diff, axis=-1, keepdims=True), axis=0, keepdims=True)
        dx2 = diff * (1.0 / d)
        dx2_ref[...] = dx2
        dy, dg_rows = _rmsnorm_bwd(dx2, y, gain_v)
        dy_ref[...] = dy.astype(BF16)
        _accumulate(loss_ref, jnp.broadcast_to(loss, loss_ref.shape), step)
        _accumulate(dg_ref, _colsum(dg_rows), step)

    row = lambda n, col=0: pl.BlockSpec((tm, n), lambda i: (i, col))
    return pl.pallas_call(
        body, name=name,
        out_shape=(jax.ShapeDtypeStruct((rows, d), BF16), jax.ShapeDtypeStruct((rows, d), BF16),
                   jax.ShapeDtypeStruct((rows, d), F32), jax.ShapeDtypeStruct((SUBLANES, LANES), F32),
                   jax.ShapeDtypeStruct((1, d), F32)),
        grid=(rows // tm,),
        in_specs=[row(d), row(d), row(d, r_block), _full((1, GLA_DV)), _full((d, d)), _full((1, d)), row(d), row(d)],
        out_specs=(row(d), row(d), row(d), _full((SUBLANES, LANES)), _full((1, d))),
        compiler_params=_params("arbitrary"),
    )(o_f, o_b, proj, head_gain, w_out, gain, x1, target)


def odd_out_bwd(dy, w_out, o_f, o_b, proj, head_gain, name):
    rows, d = dy.shape
    tm = min(ROW_TILE, rows)
    r_block = (2 * GLA_HEADS * GLA_DK + GLA_HEADS * GLA_DV) // d

    def body(dy_ref, w_ref, of_ref, ob_ref, r_ref, hg_ref, dr_ref, do_ref, dhg_ref):
        dy2 = _dot_nt(dy_ref[...], w_ref[...])
        hg = hg_ref[...]
        on, hats, rstds = _head_norm(of_ref[...] + ob_ref[...], hg)
        r = r_ref[...]
        sr = _sigmoid(r)
        dr_ref[...] = (dy2 * jnp.concatenate(on, axis=1) * (sr * (1.0 + r * (1.0 - sr)))).astype(BF16)
        d_on = dy2 * (r * sr)
        d_os, dhg = [], None
        for h in range(GLA_HEADS):
            dn = d_on[:, h * GLA_DV:(h + 1) * GLA_DV]
            part = _colsum(dn * hats[h])
            dhg = part if dhg is None else dhg + part
            dng = dn * hg
            d_os.append(rstds[h] * (dng - hats[h] * jnp.mean(dng * hats[h], axis=-1, keepdims=True)))
        do_ref[...] = jnp.concatenate(d_os, axis=1)
        _accumulate(dhg_ref, dhg, pl.program_id(0))

    row = lambda n, col=0: pl.BlockSpec((tm, n), lambda i: (i, col))
    return pl.pallas_call(
        body, name=name,
        out_shape=(jax.ShapeDtypeStruct((rows, d), BF16), jax.ShapeDtypeStruct((rows, d), F32),
                   jax.ShapeDtypeStruct((1, GLA_DV), F32)),
        grid=(rows // tm,),
        in_specs=[row(d), _full((d, d)), row(d), row(d), row(d, r_block), _full((1, GLA_DV))],
        out_specs=(row(d), row(d), _full((1, GLA_DV))),
        compiler_params=_params("arbitrary"),
    )(dy, w_out, o_f, o_b, proj, head_gain)


def local_step(x, target, w, reduce_first=None, reduce_second=None, late_weights=None):
    g = {}
    proj_e, h0 = norm_matmul(x, w["even_norm_pre"], w["even_w_in"], "even_in_proj")
    h_dir, acts = zip(*[rglru_fwd(proj_e, w["rg_conv_w"], w["rg_conv_b"], w["rg_gate_w"][d], w["rg_gate_b"][d],
                                  w["rg_lambda"][d], d == 1, "rglru_fwd_%d" % d) for d in range(2)])
    ycat = even_mix_fwd(proj_e, h_dir[0], h_dir[1], w["sc_conv_w"], "even_mix_fwd")
    if late_weights is not None:
        w = dict(w, **late_weights(ycat))
    x1, y_e = even_out_fwd(ycat, w["even_w_out"], w["even_norm_post"], x, "even_out_fwd")
    proj_o, h1 = norm_matmul(x1, w["odd_norm_pre"], w["odd_w_in"], "odd_in_proj")
    o_dir, st_dir = [], []
    for d in range(2):
        o, st = gla_fwd(proj_o, w["gla_wg_pad"], w["gla_b_gate"], d == 1, "gla_fwd_%d" % d)
        o_dir.append(o)
        st_dir.append(st)
    y2, dy_o, dx2, loss, g["odd_norm_post"] = odd_out_fwd(
        o_dir[0], o_dir[1], proj_o, w["gla_norm_g"], w["odd_w_out"], w["odd_norm_post"], x1, target, "odd_out_fwd")
    g["odd_w_out"] = matmul_dw(y2, dy_o, D_MODEL, "odd_w_out_grad")[0]
    dr, d_o, g["gla_norm_g"] = odd_out_bwd(dy_o, w["odd_w_out"], o_dir[0], o_dir[1], proj_o, w["gla_norm_g"],
                                           "odd_out_bwd")
    dq, dk, dv, dz_f = gla_bwd(proj_o, w["gla_wg_pad"], w["gla_b_gate"], d_o, st_dir[0], None, False, "gla_bwd_0")
    dq, dk, dv, dz_b = gla_bwd(proj_o, w["gla_wg_pad"], w["gla_b_gate"], d_o, st_dir[1], (dq, dk, dv), True,
                               "gla_bwd_1")
    dlr, g["gla_w_gate_lr"], g["gla_b_gate"] = gla_gate_bwd(proj_o, dz_f, dz_b, w["gla_wg_pad"], "gla_gate_bwd")
    dproj_o = [dq, dk, dv, dr, dlr]
    g["odd_w_in"] = jnp.concatenate(matmul_dw_pieces(h1, dproj_o, "odd_w_in_grad"), axis=1)[:, :ODD_IN]
    dx1, g["odd_norm_pre"] = inproj_bwd_pieces(dproj_o, w["odd_w_in"], x1, w["odd_norm_pre"], dx2, "odd_in_proj_bwd")
    dy_e, dycat, g["even_norm_post"] = even_out_bwd(dx1, y_e, w["even_norm_post"], w["even_w_out"], "even_out_bwd")
    g["even_w_out"] = matmul_dw(ycat, dy_e, D_MODEL, "even_w_out_grad")[0]
    lam = w["rg_lambda"] if reduce_first is None else w["rg_lambda"] + reduce_first(g)
    dua, dgw, dgb, dlam = [], [], [], []
    for d in range(2):
        a, b, c, e = rglru_bwd(proj_e, dycat, h_dir[d], acts[d], w["rg_gate_w"][d], lam[d], d == 1,
                               "rglru_bwd_%d" % d)
        dua.append(a)
        dgw.append(b)
        dgb.append(c)
        dlam.append(e)
    dproj_e, g["rg_conv_w"], g["rg_conv_b"], g["sc_conv_w"] = even_mix_bwd(
        proj_e, dycat, h_dir[0], h_dir[1], dua[0], dua[1], w["rg_conv_w"], w["sc_conv_w"], "even_mix_bwd")
    dgw = jnp.stack(dgw).reshape(2, RG_HEADS, RG_HEAD_DIM, 2, RG_HEAD_DIM)
    g["rg_gate_w"] = jnp.transpose(dgw, (0, 3, 1, 2, 4))
    g["rg_gate_b"] = jnp.stack(dgb).reshape(2, 2, RG_HEADS, RG_HEAD_DIM)
    g["rg_lambda"] = jnp.concatenate(dlam, axis=0)
    g["even_w_in"] = matmul_dw(h0, dproj_e, EVEN_IN // 4, "even_w_in_grad")
    gain = w["even_norm_pre"] if reduce_second is None else w["even_norm_pre"] + reduce_second(g)
    grad_x, g["even_norm_pre"] = inproj_bwd(dproj_e, w["even_w_in"], x, gain, dx1, "even_in_proj_bwd")
    return loss, grad_x, g


def _prepare_weights(full):
    w = {}
    for name in ("even_norm_pre", "even_norm_post", "rg_conv_b", "odd_norm_pre", "odd_norm_post", "gla_norm_g"):
        if name in full:
            w[name] = full[name].reshape(1, -1)
    for name in ("rg_conv_w", "sc_conv_w"):
        if name in full:
            w[name] = full[name]
    for name in ("even_w_out", "odd_w_out"):
        if name in full:
            w[name] = full[name].astype(BF16)
    if "even_w_in" in full:
        w["even_w_in"] = full["even_w_in"].astype(BF16)
        if w["even_w_in"].ndim == 2:
            w["even_w_in"] = jnp.transpose(w["even_w_in"].reshape(D_MODEL, 4, EVEN_IN // 4), (1, 0, 2))
    if "rg_gate_w" in full:
        gw = jnp.transpose(full["rg_gate_w"].astype(BF16), (0, 2, 3, 1, 4))
        w["rg_gate_w"] = gw.reshape(2, RG_HEADS, RG_HEAD_DIM, 2 * RG_HEAD_DIM)
        w["rg_gate_b"] = full["rg_gate_b"].reshape(2, 2, D_MODEL)
        w["rg_lambda"] = full["rg_lambda"].reshape(2, 1, D_MODEL)
    if "odd_w_in" in full:
        w_in = jnp.pad(full["odd_w_in"].astype(BF16), ((0, 0), (0, ODD_IN_PAD - ODD_IN)))
        w["odd_w_in"] = w_in.reshape(1, D_MODEL, ODD_IN_PAD)
    if "gla_w_gate_lr" in full:
        wg = full["gla_w_gate_lr"].astype(BF16)
        w["gla_wg_pad"] = jnp.stack([jnp.pad(wg[d], ((d * GLA_RANK, LANES - (d + 1) * GLA_RANK), (0, 0)))
                                     for d in range(2)])
        w["gla_b_gate"] = full["gla_b_gate"].reshape(2, 1, GLA_HEADS * GLA_DK)
    return w


SHARDED_SMALL = (("rg_conv_w", (4, 256)), ("rg_lambda", (2, 256)), ("sc_conv_w", (3, 256)),
                 ("odd_norm_pre", (256,)), ("odd_norm_post", (256,)), ("gla_w_gate_lr", (2, 16, 128)),
                 ("gla_b_gate", (2, 128)), ("gla_norm_g", (64,)))
SHARDED_ROWS = 96
REPLICATED = (("even_norm_post", (1024,)), ("rg_conv_b", (1024,)),
              ("rg_gate_b", (2, 2, 8, 128)), ("rg_gate_w", (2, 2, 8, 128, 128)))
LAST_REPLICATED = (("even_norm_pre", (1024,)),)
LAST_ROWS = 8
REPLICATED_ROWS = 4160
REP_PART = REPLICATED_ROWS // 8
HALF_SHARDED = SHARDED_ROWS // 2
PACK_HALF = HALF_SHARDED + REP_PART


def _seg_rows(shape):
    n = 1
    for s in shape:
        n *= s
    return -(-n // (SUBLANES * LANES)) * SUBLANES


def _pack(arrays, spec, total_rows, lead=()):
    parts = []
    for name, shape in spec:
        flat = arrays[name].reshape(lead + (-1,))
        pad = _seg_rows(shape) * LANES - flat.shape[-1]
        if pad:
            flat = jnp.pad(flat, [(0, 0)] * len(lead) + [(0, pad)])
        parts.append(flat.reshape(lead + (-1, LANES)))
    rows = jnp.concatenate(parts, axis=len(lead))
    pad = total_rows - rows.shape[len(lead)]
    return jnp.pad(rows, [(0, 0)] * len(lead) + [(0, pad), (0, 0)])


def _unpack(rows, spec, lead=()):
    out, at = {}, 0
    for name, shape in spec:
        n = 1
        for s in shape:
            n *= s
        k = _seg_rows(shape)
        seg = lax.slice_in_dim(rows, at, at + k, axis=len(lead)).reshape(lead + (-1,))
        out[name] = lax.slice_in_dim(seg, 0, n, axis=len(lead)).reshape(lead + shape)
        at += k
    return out


def _split_owners(arr):
    a = arr.reshape(arr.shape[:-1] + (4, arr.shape[-1] // 4))
    return jnp.moveaxis(a, -2, 0)


def _merge_owners(arr):
    a = jnp.moveaxis(arr, 0, -2)
    return a.reshape(a.shape[:-2] + (-1,))


HBM_SPEC = pl.BlockSpec(memory_space=pltpu.HBM)


def _position():
    x, y, c = lax.axis_index("x"), lax.axis_index("y"), lax.axis_index("c")
    chips = [(1 - x, y), (x, 1 - y), (1 - x, 1 - y)]
    return x, y, c, chips


def _remote(src, dst, send_sem, recv_sem, device):
    return pltpu.make_async_remote_copy(src_ref=src, dst_ref=dst, send_sem=send_sem, recv_sem=recv_sem,
                                        device_id=device, device_id_type=MESH)


SEM_SPEC = pl.BlockSpec(memory_space=pltpu.SEMAPHORE)
SIDE_EFFECT = pltpu.SideEffectType.DATAFLOW_SIDE_EFFECTING


def _gather_copies(ins, lands, n_h, send_sems, recv_sems):
    x, y, c, chips = _position()
    me = 2 * x + y
    copies = []
    for a in range(len(ins)):
        for k, chip in enumerate(chips):
            src = ins[a].at[c] if a < n_h else ins[a]
            dst = lands[a].at[me, c] if a < n_h else lands[a].at[me]
            copies.append(_remote(src, dst, send_sems.at[3 * a + k], recv_sems.at[3 * a + k], (chip[0], chip[1], c)))
    return copies


def gather_start(halved, whole, name):
    arrays = list(halved) + list(whole)
    n, n_h = len(arrays), len(halved)
    lands = [lax.empty((4,) + a.shape, a.dtype) for a in arrays]

    def body(*refs):
        ins, lz, send_sems, recv_sems, token = refs[:n], refs[n:2 * n], refs[2 * n], refs[2 * n + 1], refs[-1]
        for cp in _gather_copies(ins, lz, n_h, send_sems, recv_sems):
            cp.start()
        token[...] = jnp.zeros_like(token)

    operands = [pltpu.with_memory_space_constraint(a, pltpu.HBM) for a in arrays + lands]
    return pl.pallas_call(
        body, name=name,
        out_shape=(pltpu.SemaphoreType.DMA((3 * n,)), pltpu.SemaphoreType.DMA((3 * n,)))
        + tuple(pltpu.HBM(a.shape, a.dtype) for a in operands) + (jax.ShapeDtypeStruct((SUBLANES, LANES), F32),),
        in_specs=[HBM_SPEC] * (2 * n),
        out_specs=(SEM_SPEC, SEM_SPEC) + (HBM_SPEC,) * (2 * n) + (pl.BlockSpec(memory_space=pltpu.VMEM),),
        input_output_aliases={i: 2 + i for i in range(2 * n)},
        compiler_params=pltpu.CompilerParams(has_side_effects=SIDE_EFFECT),
    )(*operands)


def gather_wait(started, n_h, after, name):
    send_sems, recv_sems = started[0], started[1]
    operands = list(started[2:-1])
    n = len(operands) // 2

    def body(*refs):
        ins, lz, send_ref, recv_ref = refs[:n], refs[n:2 * n], refs[2 * n], refs[2 * n + 1]
        for cp in _gather_copies(ins, lz, n_h, send_ref, recv_ref):
            cp.wait_send()
            cp.wait_recv()

    outs = pl.pallas_call(
        body, name=name,
        out_shape=tuple(pltpu.HBM(a.shape, a.dtype) for a in operands),
        in_specs=[HBM_SPEC] * (2 * n) + [SEM_SPEC, SEM_SPEC, pl.BlockSpec(memory_space=pl.ANY)],
        out_specs=(HBM_SPEC,) * (2 * n),
        input_output_aliases={i: i for i in range(2 * n)},
        compiler_params=pltpu.CompilerParams(has_side_effects=SIDE_EFFECT),
    )(*operands, send_sems, recv_sems, after)
    return outs[n:]


def pass_to_sibling(fulls, name):
    n = len(fulls)

    def body(*refs):
        bufs = refs[n:2 * n]
        send_sems, recv_sems = refs[2 * n:]
        x, y, c, chips = _position()
        sibling = (x, y, 1 - c)
        copies = []
        for a in range(n):
            for k, chip in enumerate(chips):
                q = 2 * chip[0] + chip[1]
                cp = _remote(bufs[a].at[q, c], bufs[a].at[q, c], send_sems.at[3 * a + k], recv_sems.at[3 * a + k],
                             sibling)
                cp.start()
                copies.append(cp)
        for a in range(n):
            for k, chip in enumerate(chips):
                q = 2 * chip[0] + chip[1]
                passed = bufs[a].at[q, 1 - c]
                _remote(passed, passed, send_sems.at[3 * a + k], recv_sems.at[3 * a + k], sibling).wait_recv()
        for cp in copies:
            cp.wait_send()

    return pl.pallas_call(
        body, name=name,
        out_shape=[jax.ShapeDtypeStruct(a.shape, a.dtype) for a in fulls],
        in_specs=[HBM_SPEC] * n, out_specs=[HBM_SPEC] * n,
        input_output_aliases={i: i for i in range(n)},
        scratch_shapes=[pltpu.SemaphoreType.DMA((3 * n,)), pltpu.SemaphoreType.DMA((3 * n,))],
    )(*fulls)


def place_own(full, own, chip, name):
    _, _, r, cols = full.shape
    tr = _row_tile(r, cols)

    def body(p_ref, own_ref, full_ref, o_ref):
        o_ref[0] = own_ref[...]

    return pl.pallas_call(
        body, name=name,
        out_shape=jax.ShapeDtypeStruct(full.shape, full.dtype),
        grid_spec=pltpu.PrefetchScalarGridSpec(
            num_scalar_prefetch=1, grid=(2, r // tr),
            in_specs=[pl.BlockSpec((1, tr, cols), lambda h, i, p_ref: (h, i, 0)), pl.BlockSpec(memory_space=pl.ANY)],
            out_specs=pl.BlockSpec((1, 1, tr, cols), lambda h, i, p_ref: (p_ref[0], h, i, 0))),
        input_output_aliases={2: 0},
        compiler_params=_params("parallel", "parallel"),
    )(chip, own, full)


def exchange_with_sibling(arrays, name):
    n = len(arrays)

    def body(*refs):
        ins, outs = refs[:n], refs[n:2 * n]
        send_sems, recv_sems = refs[2 * n:]
        x, y, c, _ = _position()
        copies = []
        for a in range(n):
            cp = _remote(ins[a].at[:, 1 - c], outs[a], send_sems.at[a], recv_sems.at[a], (x, y, 1 - c))
            cp.start()
            copies.append(cp)
        for cp in copies:
            cp.wait()

    return pl.pallas_call(
        body, name=name,
        out_shape=[jax.ShapeDtypeStruct((a.shape[0],) + a.shape[2:], a.dtype) for a in arrays],
        in_specs=[HBM_SPEC] * n, out_specs=[HBM_SPEC] * n,
        scratch_shapes=[pltpu.SemaphoreType.DMA((n,)), pltpu.SemaphoreType.DMA((n,))],
    )(*arrays)


def _chip_copies(ins, lands, send_sems, recv_sems):
    x, y, c, chips = _position()
    copies = []
    for a in range(len(ins)):
        for k, chip in enumerate(chips):
            q = 2 * chip[0] + chip[1]
            copies.append(_remote(ins[a].at[q], lands[a].at[k], send_sems.at[3 * a + k], recv_sems.at[3 * a + k],
                                  (chip[0], chip[1], c)))
    return copies


def exchange_with_chips_start(arrays, name):
    n = len(arrays)
    lands = [lax.empty((3,) + a.shape[1:], a.dtype) for a in arrays]

    def body(*refs):
        ins, lz, send_sems, recv_sems, token = refs[:n], refs[n:2 * n], refs[2 * n], refs[2 * n + 1], refs[-1]
        for cp in _chip_copies(ins, lz, send_sems, recv_sems):
            cp.start()
        token[...] = jnp.zeros_like(token)

    operands = [pltpu.with_memory_space_constraint(a, pltpu.HBM) for a in list(arrays) + lands]
    return pl.pallas_call(
        body, name=name,
        out_shape=(pltpu.SemaphoreType.DMA((3 * n,)), pltpu.SemaphoreType.DMA((3 * n,)))
        + tuple(pltpu.HBM(a.shape, a.dtype) for a in operands) + (jax.ShapeDtypeStruct((SUBLANES, LANES), F32),),
        in_specs=[HBM_SPEC] * (2 * n),
        out_specs=(SEM_SPEC, SEM_SPEC) + (HBM_SPEC,) * (2 * n) + (pl.BlockSpec(memory_space=pltpu.VMEM),),
        input_output_aliases={i: 2 + i for i in range(2 * n)},
        compiler_params=pltpu.CompilerParams(has_side_effects=SIDE_EFFECT),
    )(*operands)


def exchange_with_chips_wait(started, after, name):
    send_sems, recv_sems = started[0], started[1]
    operands = list(started[2:-1])
    n = len(operands) // 2

    def body(*refs):
        ins, lz, send_ref, recv_ref = refs[:n], refs[n:2 * n], refs[2 * n], refs[2 * n + 1]
        for cp in _chip_copies(ins, lz, send_ref, recv_ref):
            cp.wait_send()
            cp.wait_recv()

    outs = pl.pallas_call(
        body, name=name,
        out_shape=tuple(pltpu.HBM(a.shape, a.dtype) for a in operands),
        in_specs=[HBM_SPEC] * (2 * n) + [SEM_SPEC, SEM_SPEC, pl.BlockSpec(memory_space=pl.ANY)],
        out_specs=(HBM_SPEC,) * (2 * n),
        input_output_aliases={i: i for i in range(2 * n)},
        compiler_params=pltpu.CompilerParams(has_side_effects=SIDE_EFFECT),
    )(*operands, send_sems, recv_sems, after)
    return outs[:n], outs[n:]


def share_totals(totals, pack_total, last_part):
    arrays = list(totals) + [pack_total]
    n = len(arrays)

    def body(*refs):
        ins, last, outs, rep, last_all = refs[:n], refs[n], refs[n + 1:2 * n + 1], refs[2 * n + 1], refs[2 * n + 2]
        send_sems, recv_sems, rep_send, rep_recv, last_send, last_recv = refs[2 * n + 3:]
        x, y, c, chips = _position()
        sibling = (x, y, 1 - c)
        me = 4 * x + 2 * y + c
        sends = []
        for a in range(n):
            cp = _remote(ins[a], outs[a], send_sems.at[a], recv_sems.at[a], sibling)
            cp.start()
            sends.append(cp)
        mine = ins[n - 1].at[pl.ds(HALF_SHARDED, REP_PART)]
        peers = [sibling]
        for chip in chips:
            peers += [(chip[0], chip[1], c), (chip[0], chip[1], 1 - c)]
        for j, peer in enumerate(peers):
            for src, dst, s_sem, r_sem in ((mine, rep, rep_send, rep_recv), (last, last_all, last_send, last_recv)):
                cp = _remote(src, dst.at[me], s_sem.at[j], r_sem.at[j], peer)
                cp.start()
                sends.append(cp)
        for a in range(n):
            _remote(outs[a], outs[a], send_sems.at[a], recv_sems.at[a], sibling).wait_recv()
        for j, peer in enumerate(peers):
            it = 4 * peer[0] + 2 * peer[1] + peer[2]
            _remote(rep.at[it], rep.at[it], rep_send.at[j], rep_recv.at[j], peer).wait_recv()
            _remote(last_all.at[it], last_all.at[it], last_send.at[j], last_recv.at[j], peer).wait_recv()
        for cp in sends:
            cp.wait_send()

    outs = pl.pallas_call(
        body, name="grad_share_totals",
        out_shape=[jax.ShapeDtypeStruct(a.shape, a.dtype) for a in arrays]
        + [jax.ShapeDtypeStruct((8, REP_PART, LANES), F32), jax.ShapeDtypeStruct((8,) + last_part.shape, F32)],
        in_specs=[HBM_SPEC] * (n + 1), out_specs=[HBM_SPEC] * (n + 2),
        scratch_shapes=[pltpu.SemaphoreType.DMA((n,)), pltpu.SemaphoreType.DMA((n,))]
        + [pltpu.SemaphoreType.DMA((7,))] * 4,
    )(*arrays, last_part)
    return outs[:n], outs[n], outs[n + 1]


def sum_parts(parts, name):
    def body(p_ref, o_ref):
        total = p_ref[0]
        for k in range(1, parts.shape[0]):
            total = total + p_ref[k]
        o_ref[...] = total

    return pl.pallas_call(body, name=name, out_shape=jax.ShapeDtypeStruct(parts.shape[1:], parts.dtype))(parts)


TILE_BYTES = 2 << 20


def _row_tile(rows, cols):
    best = None
    for t in range(SUBLANES, rows + 1, SUBLANES):
        if rows % t == 0 and t * cols * 4 <= TILE_BYTES:
            best = t
    return best if best is not None else rows


def add_sibling(mine, received, core, out_dtype, name):
    _, _, r, cols = mine.shape
    tr = _row_tile(r, cols)

    def body(c_ref, a_ref, b_ref, o_ref):
        o_ref[...] = (a_ref[0] + b_ref[...]).astype(out_dtype)

    return pl.pallas_call(
        body, name=name,
        out_shape=jax.ShapeDtypeStruct((4, r, cols), out_dtype),
        grid_spec=pltpu.PrefetchScalarGridSpec(
            num_scalar_prefetch=1, grid=(4, r // tr),
            in_specs=[pl.BlockSpec((1, 1, tr, cols), lambda o, i, c_ref: (o, c_ref[0], i, 0)),
                      pl.BlockSpec((1, tr, cols), lambda o, i, c_ref: (o, i, 0))],
            out_specs=pl.BlockSpec((1, tr, cols), lambda o, i, c_ref: (o, i, 0))),
        compiler_params=_params("parallel", "parallel"),
    )(core, mine, received)


def add_chips(own, received, chip, name):
    _, r, cols = own.shape
    tr = _row_tile(r, cols)

    def body(p_ref, a_ref, b0, b1, b2, o_ref):
        o_ref[...] = ((a_ref[0].astype(F32) + b0[0].astype(F32)) + b1[0].astype(F32)) + b2[0].astype(F32)

    rb = lambda k: pl.BlockSpec((1, tr, cols), lambda i, p_ref: (k, i, 0))
    return pl.pallas_call(
        body, name=name,
        out_shape=jax.ShapeDtypeStruct((r, cols), F32),
        grid_spec=pltpu.PrefetchScalarGridSpec(
            num_scalar_prefetch=1, grid=(r // tr,),
            in_specs=[pl.BlockSpec((1, tr, cols), lambda i, p_ref: (p_ref[0], i, 0)), rb(0), rb(1), rb(2)],
            out_specs=pl.BlockSpec((tr, cols), lambda i, p_ref: (i, 0))),
        compiler_params=_params("parallel"),
    )(chip, own, received, received, received)


def _adamw_update(gv, w_ref, m_ref, v_ref, d_ref, nm_ref, nv_ref):
    nm = ADAM_B1 * m_ref[...] + (1.0 - ADAM_B1) * gv
    nv = ADAM_B2 * v_ref[...] + (1.0 - ADAM_B2) * (gv * gv)
    nm_ref[...] = nm
    nv_ref[...] = nv
    m_hat = nm / (1.0 - ADAM_B1 ** ADAM_STEP)
    v_hat = nv / (1.0 - ADAM_B2 ** ADAM_STEP)
    d_ref[...] = -ADAM_LR * (m_hat / (jnp.sqrt(v_hat) + ADAM_EPS) + ADAM_WD * w_ref[...])


def adamw_halves(w, own, received, m, v, core, name):
    rows, cols = w.shape
    r = rows // 2
    tr = _row_tile(r, cols)
    nr = r // tr

    def body(c_ref, w_ref, own_ref, rec_ref, m_ref, v_ref, g_ref, d_ref, nm_ref, nv_ref):
        gv = jnp.where(pl.program_id(0) == c_ref[0], own_ref[...], rec_ref[...])
        g_ref[...] = gv
        _adamw_update(gv, w_ref, m_ref, v_ref, d_ref, nm_ref, nv_ref)

    whole = pl.BlockSpec((tr, cols), lambda h, i, c_ref: (h * nr + i, 0))
    half = pl.BlockSpec((tr, cols), lambda h, i, c_ref: (i, 0))
    return pl.pallas_call(
        body, name=name,
        out_shape=(jax.ShapeDtypeStruct((rows, cols), F32),) * 4,
        grid_spec=pltpu.PrefetchScalarGridSpec(
            num_scalar_prefetch=1, grid=(2, nr),
            in_specs=[whole, half, half, whole, whole], out_specs=(whole,) * 4),
        compiler_params=_params("parallel", "parallel"),
    )(core, w, own, received, m, v)


def adamw(w, g, m, v, name):
    r, cols = w.shape
    tr = _row_tile(r, cols)

    def body(w_ref, g_ref, m_ref, v_ref, d_ref, nm_ref, nv_ref):
        _adamw_update(g_ref[...], w_ref, m_ref, v_ref, d_ref, nm_ref, nv_ref)

    blk = pl.BlockSpec((tr, cols), lambda i: (i, 0))
    return pl.pallas_call(
        body, name=name,
        out_shape=(jax.ShapeDtypeStruct((r, cols), F32),) * 3,
        grid=(r // tr,),
        in_specs=[blk] * 4, out_specs=(blk,) * 3,
        compiler_params=_params("parallel"),
    )(w, g, m, v)


WEIGHTS = ("even_norm_pre", "even_norm_post", "even_w_in", "rg_conv_w", "rg_conv_b", "rg_gate_w", "rg_gate_b",
           "rg_lambda", "sc_conv_w", "even_w_out", "odd_norm_pre", "odd_norm_post", "odd_w_in", "gla_w_gate_lr",
           "gla_b_gate", "gla_norm_g", "odd_w_out")
BIG = ("even_w_in", "even_w_out", "odd_w_in", "odd_w_out")


def _halves(a):
    return a.reshape((2, a.shape[0] // 2) + a.shape[1:])


def kernel(x, even_norm_pre, even_norm_post, even_w_in, rg_conv_w, rg_conv_b, rg_gate_w, rg_gate_b, rg_lambda, sc_conv_w, even_w_out, odd_norm_pre, odd_norm_post, odd_w_in, gla_w_gate_lr, gla_b_gate, gla_norm_g, odd_w_out, loss_target, m_even_norm_pre, m_even_norm_post, m_even_w_in, m_rg_conv_w, m_rg_conv_b, m_rg_gate_w, m_rg_gate_b, m_rg_lambda, m_sc_conv_w, m_even_w_out, m_odd_norm_pre, m_odd_norm_post, m_odd_w_in, m_gla_w_gate_lr, m_gla_b_gate, m_gla_norm_g, m_odd_w_out, v_even_norm_pre, v_even_norm_post, v_even_w_in, v_rg_conv_w, v_rg_conv_b, v_rg_gate_w, v_rg_gate_b, v_rg_lambda, v_sc_conv_w, v_even_w_out, v_odd_norm_pre, v_odd_norm_post, v_odd_w_in, v_gla_w_gate_lr, v_gla_b_gate, v_gla_norm_g, v_odd_w_out):
    given = dict(locals())
    shard = {n: given[n][0] for n in WEIGHTS}
    m_in = {n: given["m_" + n][0] for n in WEIGHTS}
    v_in = {n: given["v_" + n][0] for n in WEIGHTS}
    mx, my, mc = lax.axis_index("x"), lax.axis_index("y"), lax.axis_index("c")
    core = jnp.reshape(mc, (1,)).astype(jnp.int32)
    chip = jnp.reshape(2 * mx + my, (1,)).astype(jnp.int32)

    small_shard = _pack(shard, SHARDED_SMALL, SHARDED_ROWS)
    big_own = [_halves(shard[n].astype(BF16)) for n in BIG]
    started_a = gather_start(big_own[:1], [small_shard], "gather_start_a")
    started_b = gather_start(big_own[1:], [], "gather_start_b")
    even_w_in_full, small_full = gather_wait(started_a, 1, started_b[-1], "gather_wait_a")
    (even_w_in_full,) = pass_to_sibling([even_w_in_full], "gather_pass_a")
    even_w_in_full = place_own(even_w_in_full, big_own[0], chip, "place_even_w_in")
    small_full = lax.dynamic_update_slice(small_full, small_shard[None], (chip[0], 0, 0))
    full = {n: shard[n] for n, _ in REPLICATED + LAST_REPLICATED}
    full.update({n: _merge_owners(a) for n, a in _unpack(small_full, SHARDED_SMALL, lead=(4,)).items()})
    full["even_w_in"] = even_w_in_full.reshape(4, D_MODEL, EVEN_IN // 4)

    def late_weights(after):
        lands = pass_to_sibling(list(gather_wait(started_b, 3, after, "gather_wait_b")), "gather_pass_b")
        lands = [place_own(a, b, chip, "place_" + n) for a, b, n in zip(lands, big_own[1:], BIG[1:])]
        odd_w_in = jnp.transpose(lands[1].reshape(4, D_MODEL, ODD_IN // 4), (1, 0, 2)).reshape(D_MODEL, ODD_IN)
        return _prepare_weights({"even_w_out": lands[0].reshape(2 * D_MODEL, D_MODEL), "odd_w_in": odd_w_in,
                                 "odd_w_out": lands[2].reshape(D_MODEL, D_MODEL)})

    pending = {}

    def slab(a):
        return a.reshape((4, 2, a.shape[1] // 2) + a.shape[2:])

    def begin(tag, slabs, dtypes):
        got = exchange_with_sibling(slabs, "grad_sibling_" + tag)
        sums = [add_sibling(a, b, core, dt, "grad_add_sibling_%s%d" % (tag, i))
                for i, (a, b, dt) in enumerate(zip(slabs, got, dtypes))]
        pending[tag] = exchange_with_chips_start(sums, "grad_chips_start_" + tag)
        return pending[tag][-1][0, 0]

    def finish(tag, after):
        sums, got = exchange_with_chips_wait(pending[tag], after, "grad_chips_wait_" + tag)
        return [add_chips(a, b, chip, "grad_add_chips_%s%d" % (tag, i)) for i, (a, b) in enumerate(zip(sums, got))]

    def reduce_first(g):
        return begin("a", [slab(jnp.transpose(g["odd_w_in"].reshape(D_MODEL, 4, ODD_IN // 4), (1, 0, 2))),
                           slab(g["odd_w_out"].reshape(4, D_MODEL // 4, D_MODEL)),
                           slab(g["even_w_out"].reshape(4, D_MODEL // 2, D_MODEL))], [BF16] * 3)

    def reduce_second(g):
        pending["totals_a"] = finish("a", g["even_w_in"])
        rep_rows = _pack(g, REPLICATED, REPLICATED_ROWS).reshape(4, 2, REP_PART, LANES)
        sh_rows = _pack({n: _split_owners(g[n]) for n, _ in SHARDED_SMALL}, SHARDED_SMALL, SHARDED_ROWS, lead=(4,))
        pack = jnp.concatenate([sh_rows.reshape(4, 2, HALF_SHARDED, LANES), rep_rows], axis=2)
        return begin("b", [slab(g["even_w_in"]), pack], [BF16, F32])

    loss, grad_x, g = local_step(x[0], loss_target[0], _prepare_weights(full), reduce_first, reduce_second,
                                 late_weights)
    odd_w_in_t, odd_w_out_t, even_w_out_t = pending["totals_a"]
    even_w_in_t, pack_t = finish("b", grad_x)
    totals = [even_w_in_t, even_w_out_t, odd_w_in_t, odd_w_out_t]
    last_part = jnp.concatenate([_pack(g, LAST_REPLICATED, LAST_ROWS), loss])
    from_core, rep_all, last_all = share_totals(totals, pack_t, last_part)
    me = 2 * chip[0] + core[0]
    mine, theirs = pack_t[:HALF_SHARDED], from_core[4][:HALF_SHARDED]
    sh_total = jnp.where(mc == 0, jnp.concatenate([mine, theirs]), jnp.concatenate([theirs, mine]))
    rep_all = lax.dynamic_update_slice(rep_all, pack_t[None, HALF_SHARDED:], (me, 0, 0))
    rep_total = rep_all.reshape(REPLICATED_ROWS, LANES)
    last_total = sum_parts(lax.dynamic_update_slice(last_all, last_part[None], (me, 0, 0)), "grad_sum_last")
    last_total, loss = last_total[:LAST_ROWS], last_total[LAST_ROWS, 0]
    grads = {}
    grads.update(_unpack(sh_total, SHARDED_SMALL))
    grads.update(_unpack(rep_total, REPLICATED))
    grads.update(_unpack(last_total, LAST_REPLICATED))

    delta, new_m, new_v = {}, {}, {}
    for i, n in enumerate(BIG):
        grads[n], delta[n], new_m[n], new_v[n] = adamw_halves(shard[n], totals[i], from_core[i], m_in[n], v_in[n],
                                                              core, "adamw_" + n)
    small = ((SHARDED_SMALL, SHARDED_ROWS), (REPLICATED, REPLICATED_ROWS), (LAST_REPLICATED, LAST_ROWS))
    packed = [jnp.concatenate([_pack(src, spec, rows) for spec, rows in small]) for src in (shard, m_in, v_in)]
    small_g = jnp.concatenate([sh_total, rep_total, last_total], axis=0)
    outs = adamw(packed[0], small_g, packed[1], packed[2], "adamw_small")
    for dst, packed_rows in zip((delta, new_m, new_v), outs):
        at = 0
        for spec, rows in small:
            dst.update(_unpack(packed_rows[at:at + rows], spec))
            at += rows
    result = [loss, grad_x[None]]
    for group in (grads, delta, new_m, new_v):
        result += [group[n].reshape(given[n].shape) for n in WEIGHTS]
    return tuple(result)
```

```python
import functools

import jax
import jax.numpy as jnp
from jax import lax
from jax.experimental import pallas as pl
from jax.experimental.pallas import tpu as pltpu

F32 = jnp.float32
BF16 = jnp.bfloat16
MESH = pl.DeviceIdType.MESH

D_MODEL = 1024
NORM_EPS = 1e-6
RG_HEADS = 8
RG_HEAD_DIM = 128
RG_C = 8.0
EVEN_IN = 6144
ODD_IN = 3104
ODD_IN_PAD = 3200
GLA_HEADS = 4
GLA_DK = 128
GLA_DV = 256
GLA_RANK = 16
GLA_NORMALIZER = 16.0
GLA_CHUNK = 128
LR_COL = 3072

ADAM_LR = 0.001
ADAM_B1 = 0.9
ADAM_B2 = 0.999
ADAM_EPS = 1e-08
ADAM_WD = 0.01
ADAM_STEP = 10

SUBLANES = 8
LANES = 128
VMEM_LIMIT = 56 * 2 ** 20

ROW_TILE = 512
SCAN_TILE = 256
GLA_BLOCK = 1024
MIX_TILE = 128


def _params(*sem):
    return pltpu.CompilerParams(dimension_semantics=sem, vmem_limit_bytes=VMEM_LIMIT)


def _full(shape):
    n = len(shape)
    return pl.BlockSpec(shape, lambda *_: (0,) * n)


def _sigmoid(x):
    return 0.5 + 0.5 * jnp.tanh(0.5 * x)


def _softplus(x):
    return jnp.maximum(x, 0.0) + jnp.log(1.0 + jnp.exp(-jnp.abs(x)))


def _dot(a, b):
    return jnp.dot(a, b, preferred_element_type=F32)


def _dot_nt(a, b):
    return lax.dot_general(a, b, (((1,), (1,)), ((), ())), preferred_element_type=F32)


def _dot_tn(a, b):
    return lax.dot_general(a, b, (((0,), (0,)), ((), ())), preferred_element_type=F32)


def _bdot(a, b, ca, cb):
    return lax.dot_general(a, b, (((ca,), (cb,)), ((0,), (0,))), preferred_element_type=F32)


def _halo_specs(rows, cols, col_block, n_row_tiles, tix):
    per = rows // SUBLANES
    last = n_row_tiles * per - 1

    def split(args):
        if len(args) == 2:
            return tix(args[1]), col_block + args[0]
        return tix(args[0]), col_block

    def prev(*args):
        t, c = split(args)
        return (jnp.maximum(t * per - 1, 0), c)

    def main(*args):
        return split(args)

    def nxt(*args):
        t, c = split(args)
        return (jnp.minimum((t + 1) * per, last), c)

    return [pl.BlockSpec((SUBLANES, cols), prev), pl.BlockSpec((rows, cols), main),
            pl.BlockSpec((SUBLANES, cols), nxt)]


def _extend(prev_ref, main_ref, next_ref, is_first, is_last):
    p = jnp.where(is_first, 0.0, prev_ref[...])
    n = jnp.where(is_last, 0.0, next_ref[...])
    return jnp.concatenate([p, main_ref[...], n], axis=0)


def _shifted(ext, offset, rows):
    if offset == 0:
        return ext[SUBLANES:SUBLANES + rows]
    n = ext.shape[0]
    return pltpu.roll(ext, (-offset) % n, 0)[SUBLANES:SUBLANES + rows]


def _conv(ext, w, left, rows):
    out = None
    for k in range(w.shape[0]):
        term = _shifted(ext, k - left, rows) * w[k:k + 1]
        out = term if out is None else out + term
    return out


def _conv_transpose(ext, w, left, rows):
    out = None
    for k in range(w.shape[0]):
        term = _shifted(ext, left - k, rows) * w[k:k + 1]
        out = term if out is None else out + term
    return out


def _colsum(x):
    return jnp.sum(x, axis=0, keepdims=True)


def _accumulate(ref, value, step):
    @pl.when(step == 0)
    def _():
        ref[...] = value

    @pl.when(step > 0)
    def _():
        ref[...] += value


PROJ_TILE_BYTES = 7 * 2 ** 20


def _proj_row_tile(rows, width):
    tm = min(ROW_TILE, rows)
    while tm * width * 4 > PROJ_TILE_BYTES and tm % (2 * SUBLANES) == 0:
        tm //= 2
    return tm


def norm_matmul(x, gain, w, name):
    rows, d = x.shape
    n_col_tiles, _, tn = w.shape
    tm = _proj_row_tile(rows, n_col_tiles * tn)

    def body(x_ref, g_ref, w_ref, proj_ref, h_ref):
        xv = x_ref[...]
        rstd = lax.rsqrt(jnp.mean(xv * xv, axis=-1, keepdims=True) + NORM_EPS)
        hv = (xv * rstd * g_ref[...]).astype(BF16)
        h_ref[...] = hv
        for j in range(n_col_tiles):
            proj_ref[:, j * tn:(j + 1) * tn] = _dot(hv, w_ref[j])

    row = lambda cols: pl.BlockSpec((tm, cols), lambda i: (i, 0))
    return pl.pallas_call(
        body, name=name,
        out_shape=(jax.ShapeDtypeStruct((rows, n_col_tiles * tn), F32), jax.ShapeDtypeStruct((rows, d), BF16)),
        grid=(rows // tm,),
        in_specs=[row(d), _full((1, d)), _full(w.shape)],
        out_specs=(row(n_col_tiles * tn), row(d)),
        compiler_params=_params("parallel"),
    )(x, gain, w)


def inproj_bwd(dproj, w, x, gain, dres, name):
    rows, d = x.shape
    n_col_tiles, _, tn = w.shape
    tm = _proj_row_tile(rows, n_col_tiles * tn)

    def body(dp_ref, w_ref, x_ref, g_ref, dres_ref, dx_ref, dg_ref):
        dh = None
        for j in range(n_col_tiles):
            part = _dot_nt(dp_ref[:, j * tn:(j + 1) * tn], w_ref[j])
            dh = part if dh is None else dh + part
        _inproj_finish(dh, x_ref, g_ref, dres_ref, dx_ref, dg_ref, pl.program_id(0))

    row = lambda cols: pl.BlockSpec((tm, cols), lambda i: (i, 0))
    return pl.pallas_call(
        body, name=name,
        out_shape=(jax.ShapeDtypeStruct((rows, d), F32), jax.ShapeDtypeStruct((1, d), F32)),
        grid=(rows // tm,),
        in_specs=[row(n_col_tiles * tn), _full(w.shape), row(d), _full((1, d)), row(d)],
        out_specs=(row(d), _full((1, d))),
        compiler_params=_params("arbitrary"),
    )(dproj, w, x, gain, dres)


def _inproj_finish(dh, x_ref, g_ref, dres_ref, dx_ref, dg_ref, step):
    xv = x_ref[...]
    rstd = lax.rsqrt(jnp.mean(xv * xv, axis=-1, keepdims=True) + NORM_EPS)
    xhat = xv * rstd
    dxn = dh * g_ref[...]
    dx_ref[...] = dres_ref[...] + rstd * (dxn - xhat * jnp.mean(dxn * xhat, axis=-1, keepdims=True))
    _accumulate(dg_ref, _colsum(dh * xhat), step)


def inproj_bwd_pieces(pieces, w, x, gain, dres, name):
    rows, d = x.shape
    tm = min(ROW_TILE, rows)
    n = len(pieces)
    widths = [p.shape[1] for p in pieces]
    starts = [sum(widths[:k]) for k in range(n)]
    assert sum(widths) == w.shape[2]

    def body(*refs):
        w_ref, x_ref, g_ref, dres_ref, dx_ref, dg_ref = refs[n:]
        dh = None
        for k in range(n):
            part = _dot_nt(refs[k][...], w_ref[0, :, starts[k]:starts[k] + widths[k]])
            dh = part if dh is None else dh + part
        _inproj_finish(dh, x_ref, g_ref, dres_ref, dx_ref, dg_ref, pl.program_id(0))

    row = lambda cols: pl.BlockSpec((tm, cols), lambda i: (i, 0))
    return pl.pallas_call(
        body, name=name,
        out_shape=(jax.ShapeDtypeStruct((rows, d), F32), jax.ShapeDtypeStruct((1, d), F32)),
        grid=(rows // tm,),
        in_specs=[row(wd) for wd in widths] + [_full(w.shape), row(d), _full((1, d)), row(d)],
        out_specs=(row(d), _full((1, d))),
        compiler_params=_params("arbitrary"),
    )(*pieces, w, x, gain, dres)


def matmul_dw_pieces(a, pieces, name):
    rows, m = a.shape
    tk = min(2 * ROW_TILE, rows)
    n = len(pieces)

    def body(*refs):
        a_ref, ins, outs = refs[0], refs[1:1 + n], refs[1 + n:]
        av = a_ref[...]
        for k in range(n):
            _accumulate(outs[k], _dot_tn(av, ins[k][...]), pl.program_id(0))

    return pl.pallas_call(
        body, name=name,
        out_shape=[jax.ShapeDtypeStruct((m, p.shape[1]), F32) for p in pieces],
        grid=(rows // tk,),
        in_specs=[pl.BlockSpec((tk, m), lambda k: (k, 0))]
        + [pl.BlockSpec((tk, p.shape[1]), lambda k: (k, 0)) for p in pieces],
        out_specs=[_full((m, p.shape[1])) for p in pieces],
        compiler_params=_params("arbitrary"),
    )(a, *pieces)


def matmul_dw(a, b, bn, name):
    rows, m = a.shape
    n = b.shape[1]
    tk = min(4 * ROW_TILE, rows)
    steps = rows // tk

    def body(a_ref, b_ref, o_ref):
        part = _dot_tn(a_ref[...], b_ref[...])

        @pl.when(pl.program_id(1) == 0)
        def _():
            o_ref[0] = part

        @pl.when(pl.program_id(1) > 0)
        def _():
            o_ref[0] += part

    return pl.pallas_call(
        body, name=name,
        out_shape=jax.ShapeDtypeStruct((n // bn, m, bn), F32),
        grid=(n // bn, steps),
        in_specs=[pl.BlockSpec((tk, m), lambda j, k: (k, 0)), pl.BlockSpec((tk, bn), lambda j, k: (k, j))],
        out_specs=pl.BlockSpec((1, m, bn), lambda j, k: (j, 0, 0)),
        compiler_params=_params("parallel", "arbitrary"),
    )(a, b)


def _scan(a, b, carry, reverse):
    n, c = a.shape
    blocks = n // SUBLANES
    a = a.reshape(blocks, SUBLANES, c)
    b = b.reshape(blocks, SUBLANES, c)
    pos = lax.broadcasted_iota(jnp.int32, (1, SUBLANES, c), 1)
    s = 1
    while s < SUBLANES:
        shift, valid = (SUBLANES - s, pos < SUBLANES - s) if reverse else (s, pos >= s)
        a_s, b_s = pltpu.roll(a, shift, 1), pltpu.roll(b, shift, 1)
        b = jnp.where(valid, a * b_s + b, b)
        a = jnp.where(valid, a * a_s, a)
        s *= 2
    out = [None] * blocks
    for k in (range(blocks - 1, -1, -1) if reverse else range(blocks)):
        h = a[k] * carry + b[k]
        out[k] = h
        carry = h[0:1] if reverse else h[SUBLANES - 1:SUBLANES]
    return jnp.concatenate(out, axis=0)


def _rg_gates(ua, gw_ref, gb, lam):
    ub = ua.astype(BF16)
    pre_r, pre_i = [], []
    for h in range(RG_HEADS):
        z = _dot(ub[:, h * RG_HEAD_DIM:(h + 1) * RG_HEAD_DIM], gw_ref[h])
        pre_r.append(z[:, :RG_HEAD_DIM])
        pre_i.append(z[:, RG_HEAD_DIM:])
    r = _sigmoid(jnp.concatenate(pre_r, axis=1) + gb[0:1])
    i = _sigmoid(jnp.concatenate(pre_i, axis=1) + gb[1:2])
    sp = _softplus(-lam)
    log_a = -RG_C * r * sp
    a = jnp.exp(log_a)
    mult = jnp.sqrt(1.0 - a * a)
    return r, i, sp, a, mult


def _rg_weight_specs():
    return [_full((4, D_MODEL)), _full((1, D_MODEL)), _full((RG_HEADS, RG_HEAD_DIM, 2 * RG_HEAD_DIM)),
            _full((2, D_MODEL)), _full((1, D_MODEL))]


def rglru_fwd(proj, conv_w, conv_b, gate_w, gate_b, lam, reverse, name):
    rows_total = proj.shape[0]
    rows = min(SCAN_TILE, rows_total)
    n_tiles = rows_total // rows
    tix = (lambda i: n_tiles - 1 - i) if reverse else (lambda i: i)

    def body(xp, xm, xn, cw_ref, cb_ref, gw_ref, gb_ref, lam_ref, h_ref, acts_ref, carry):
        i = pl.program_id(0)
        t = tix(i)
        ext = _extend(xp, xm, xn, t == 0, t == n_tiles - 1)
        ua = _conv(ext, cw_ref[...], 2, rows) + cb_ref[...]
        r, gi, _, a, mult = _rg_gates(ua, gw_ref, gb_ref[...], lam_ref[...])
        for k, saved in enumerate((ua, r, gi, a, mult)):
            acts_ref[k] = saved
        b = mult * (gi * ua)

        @pl.when(i == 0)
        def _():
            carry[...] = jnp.zeros_like(carry)

        h = _scan(a, b, carry[0:1], reverse)
        h_ref[...] = h
        edge = h[0:1] if reverse else h[rows - 1:rows]
        carry[...] = jnp.broadcast_to(edge, carry.shape)

    return pl.pallas_call(
        body, name=name,
        out_shape=(jax.ShapeDtypeStruct((rows_total, D_MODEL), F32),
                   jax.ShapeDtypeStruct((5, rows_total, D_MODEL), F32)),
        grid=(n_tiles,),
        in_specs=_halo_specs(rows, D_MODEL, 0, n_tiles, tix) + _rg_weight_specs(),
        out_specs=(pl.BlockSpec((rows, D_MODEL), lambda i: (tix(i), 0)),
                   pl.BlockSpec((5, rows, D_MODEL), lambda i: (0, tix(i), 0))),
        scratch_shapes=[pltpu.VMEM((SUBLANES, D_MODEL), F32)],
        compiler_params=_params("arbitrary"),
    )(proj, proj, proj, conv_w, conv_b, gate_w, gate_b, lam)


def rglru_bwd(proj, dycat, h_dir, acts, gate_w, lam, reverse, name):
    rows_total = proj.shape[0]
    rows = min(SCAN_TILE, rows_total)
    n_tiles = rows_total // rows
    tix = (lambda i: i) if reverse else (lambda i: n_tiles - 1 - i)
    za_block = 1

    def body(acts_ref, za_ref, dya_ref, hp, hm, hn, gw_ref, lam_ref, dua_ref, dgw_ref, dgb_ref, dlam_ref, carry):
        step = pl.program_id(0)
        t = tix(step)
        first, last = t == 0, t == n_tiles - 1
        ua, r, gi, a, mult = (acts_ref[k] for k in range(5))
        lam_v = lam_ref[...]
        sp = _softplus(-lam_v)
        za = za_ref[...]
        dh = dya_ref[...] * (za * _sigmoid(za))

        @pl.when(step == 0)
        def _():
            carry[...] = jnp.zeros_like(carry)

        old = carry[0:1]
        mu = _scan(a, a * dh, old, not reverse)
        row = lax.broadcasted_iota(jnp.int32, mu.shape, 0)
        if reverse:
            mu_next = jnp.where(row == 0, old, pltpu.roll(mu, 1, 0))
            carry[...] = jnp.broadcast_to(mu[rows - 1:rows], carry.shape)
            h_ext = _extend(hp, hm, hn, first, last)
            h_prev = _shifted(h_ext, 1, rows)
        else:
            mu_next = jnp.where(row == rows - 1, old, pltpu.roll(mu, rows - 1, 0))
            carry[...] = jnp.broadcast_to(mu[0:1], carry.shape)
            h_ext = _extend(hp, hm, hn, first, last)
            h_prev = _shifted(h_ext, -1, rows)
        db = dh + mu_next
        da = db * h_prev
        d_mult = db * (gi * ua)
        di = db * (mult * ua)
        dua = db * (mult * gi)
        dlog_a = da * a - d_mult * (a * a) / mult
        dr = dlog_a * (-RG_C * sp)
        dlam = _colsum(dlog_a * (-RG_C * r)) * (-_sigmoid(-lam_v))
        dpr = dr * (r * (1.0 - r))
        dpi = di * (gi * (1.0 - gi))
        dgb = jnp.concatenate([_colsum(dpr), _colsum(dpi)], axis=0)
        ub = ua.astype(BF16)
        dua_heads, dgw_heads = [], []
        for h in range(RG_HEADS):
            cols = slice(h * RG_HEAD_DIM, (h + 1) * RG_HEAD_DIM)
            dz = jnp.concatenate([dpr[:, cols], dpi[:, cols]], axis=1).astype(BF16)
            dgw_heads.append(_dot_tn(ub[:, cols], dz))
            dua_heads.append(_dot_nt(dz, gw_ref[h]))
        dua_ref[...] = dua + jnp.concatenate(dua_heads, axis=1)

        @pl.when(step == 0)
        def _():
            for h in range(RG_HEADS):
                dgw_ref[h] = dgw_heads[h]
            dgb_ref[...] = dgb
            dlam_ref[...] = dlam

        @pl.when(step > 0)
        def _():
            for h in range(RG_HEADS):
                dgw_ref[h] += dgw_heads[h]
            dgb_ref[...] += dgb
            dlam_ref[...] += dlam

    row_spec = lambda col: pl.BlockSpec((rows, D_MODEL), lambda i: (tix(i), col))
    return pl.pallas_call(
        body, name=name,
        out_shape=(jax.ShapeDtypeStruct((rows_total, D_MODEL), F32),
                   jax.ShapeDtypeStruct((RG_HEADS, RG_HEAD_DIM, 2 * RG_HEAD_DIM), F32),
                   jax.ShapeDtypeStruct((2, D_MODEL), F32), jax.ShapeDtypeStruct((1, D_MODEL), F32)),
        grid=(n_tiles,),
        in_specs=([pl.BlockSpec((5, rows, D_MODEL), lambda i: (0, tix(i), 0)), row_spec(za_block), row_spec(0)]
                  + _halo_specs(rows, D_MODEL, 0, n_tiles, tix)
                  + [_full((RG_HEADS, RG_HEAD_DIM, 2 * RG_HEAD_DIM)), _full((1, D_MODEL))]),
        out_specs=(row_spec(0), _full((RG_HEADS, RG_HEAD_DIM, 2 * RG_HEAD_DIM)), _full((2, D_MODEL)),
                   _full((1, D_MODEL))),
        scratch_shapes=[pltpu.VMEM((SUBLANES, D_MODEL), F32)],
        compiler_params=_params("arbitrary"),
    )(acts, proj, dycat, h_dir, h_dir, h_dir, gate_w, lam)


def even_mix_fwd(proj, h_f, h_b, sc_w, name):
    rows_total = proj.shape[0]
    rows = min(2 * MIX_TILE, rows_total)
    n_tiles = rows_total // rows
    cb = D_MODEL
    n_cb = 1
    ident = lambda i: i

    def body(za_ref, hf_ref, hb_ref, xbp, xbm, xbn, gcp, gcm, gcn, gb_ref, zb_ref, w_ref, y_ref):
        t = pl.program_id(1)
        first, last = t == 0, t == n_tiles - 1
        za = za_ref[...]
        y_ref[:, 0:cb] = ((hf_ref[...] + hb_ref[...]) * (za * _sigmoid(za))).astype(BF16)
        p_ext = _extend(xbp, xbm, xbn, first, last) * _extend(gcp, gcm, gcn, first, last)
        cv = _conv(p_ext, w_ref[...], 1, rows)
        zb = zb_ref[...]
        y_ref[:, cb:2 * cb] = (gb_ref[...] * cv * (zb * _sigmoid(zb))).astype(BF16)

    blk = lambda col: pl.BlockSpec((rows, cb), lambda c, i: (i, col * n_cb + c))
    own = pl.BlockSpec((rows, cb), lambda c, i: (i, c))
    return pl.pallas_call(
        body, name=name,
        out_shape=jax.ShapeDtypeStruct((rows_total, 2 * D_MODEL), BF16),
        grid=(n_cb, n_tiles),
        in_specs=([blk(1), own, own] + _halo_specs(rows, cb, 2 * n_cb, n_tiles, ident)
                  + _halo_specs(rows, cb, 4 * n_cb, n_tiles, ident)
                  + [blk(3), blk(5), pl.BlockSpec((3, cb), lambda c, i: (0, c))]),
        out_specs=pl.BlockSpec((rows, 2 * cb), lambda c, i: (i, 0)),
        compiler_params=_params("parallel", "arbitrary"),
    )(proj, h_f, h_b, proj, proj, proj, proj, proj, proj, proj, proj, sc_w)


def even_mix_bwd(proj, dycat, h_f, h_b, dua_f, dua_b, conv_w, sc_w, name):
    rows_total = proj.shape[0]
    rows = min(MIX_TILE, rows_total)
    n_tiles = rows_total // rows
    cb = D_MODEL
    n_cb = 1
    ident = lambda i: i

    def body(xap, xam, xan, za_ref, xbp, xbm, xbn, gbp, gbm, gbn, gcp, gcm, gcn, zbp, zbm, zbn,
             dya_ref, dybp, dybm, dybn, hf_ref, hb_ref, dfp, dfm, dfn, dbp, dbm, dbn, cw_ref, sw_ref,
             dp_ref, dcw_ref, dcb_ref, dsw_ref):
        def put(k, value):
            dp_ref[:, k * cb:(k + 1) * cb] = value.astype(BF16)

        t = pl.program_id(1)
        first, last = t == 0, t == n_tiles - 1
        za = za_ref[...]
        sa = _sigmoid(za)
        put(1, dya_ref[...] * (hf_ref[...] + hb_ref[...]) * (sa * (1.0 + za * (1.0 - sa))))
        dua_ext = _extend(dfp, dfm, dfn, first, last) + _extend(dbp, dbm, dbn, first, last)
        cw = cw_ref[...]
        put(0, _conv_transpose(dua_ext, cw, 2, rows))
        dua = dua_ext[SUBLANES:SUBLANES + rows]
        xa_ext = _extend(xap, xam, xan, first, last)
        dcw = jnp.concatenate([_colsum(dua * _shifted(xa_ext, k - 2, rows)) for k in range(4)], axis=0)
        dcb = _colsum(dua)
        xb_ext = _extend(xbp, xbm, xbn, first, last)
        gc_ext = _extend(gcp, gcm, gcn, first, last)
        p_ext = xb_ext * gc_ext
        zb_ext = _extend(zbp, zbm, zbn, first, last)
        sb_ext = _sigmoid(zb_ext)
        dyb_ext = _extend(dybp, dybm, dybn, first, last)
        gb_ext = _extend(gbp, gbm, gbn, first, last)
        dcv_ext = dyb_ext * gb_ext * (zb_ext * sb_ext)
        sw = sw_ref[...]
        cv = _conv(p_ext, sw, 1, rows)
        mid = slice(SUBLANES, SUBLANES + rows)
        zb, sb, dyb, gb = zb_ext[mid], sb_ext[mid], dyb_ext[mid], gb_ext[mid]
        put(3, dyb * cv * (zb * sb))
        put(5, dyb * gb * cv * (sb * (1.0 + zb * (1.0 - sb))))
        dp = _conv_transpose(dcv_ext, sw, 1, rows)
        put(4, dp * xb_ext[mid])
        put(2, dp * gc_ext[mid])
        dcv = dcv_ext[mid]
        dsw = jnp.concatenate([_colsum(dcv * _shifted(p_ext, k - 1, rows)) for k in range(3)], axis=0)

        @pl.when(t == 0)
        def _():
            dcw_ref[...] = dcw
            dcb_ref[...] = dcb
            dsw_ref[...] = dsw

        @pl.when(t > 0)
        def _():
            dcw_ref[...] += dcw
            dcb_ref[...] += dcb
            dsw_ref[...] += dsw

    blk = lambda col: pl.BlockSpec((rows, cb), lambda c, i: (i, col * n_cb + c))
    halo = lambda col: _halo_specs(rows, cb, col * n_cb, n_tiles, ident)
    own = pl.BlockSpec((rows, cb), lambda c, i: (i, c))
    wspec = lambda k: pl.BlockSpec((k, cb), lambda c, i: (0, c))
    return pl.pallas_call(
        body, name=name,
        out_shape=(jax.ShapeDtypeStruct((rows_total, 6 * D_MODEL), BF16),
                   jax.ShapeDtypeStruct((4, D_MODEL), F32), jax.ShapeDtypeStruct((1, D_MODEL), F32),
                   jax.ShapeDtypeStruct((3, D_MODEL), F32)),
        grid=(n_cb, n_tiles),
        in_specs=(halo(0) + [blk(1)] + halo(2) + halo(3) + halo(4) + halo(5) + [blk(0)] + halo(1)
                  + [own, own] + halo(0) + halo(0) + [wspec(4), wspec(3)]),
        out_specs=(pl.BlockSpec((rows, 6 * cb), lambda c, i: (i, 0)), wspec(4), wspec(1), wspec(3)),
        compiler_params=_params("parallel", "arbitrary"),
    )(proj, proj, proj, proj, proj, proj, proj, proj, proj, proj, proj, proj, proj, proj, proj, proj,
      dycat, dycat, dycat, dycat, h_f, h_b, dua_f, dua_f, dua_f, dua_b, dua_b, dua_b, conv_w, sc_w)


def even_out_fwd(ycat, w_out, gain, x, name):
    rows, d = x.shape
    k = ycat.shape[1]
    tm = min(ROW_TILE, rows)

    def body(yc_ref, w_ref, g_ref, x_ref, x1_ref, y_ref):
        y = _dot(yc_ref[...], w_ref[...])
        y_ref[...] = y
        rstd = lax.rsqrt(jnp.mean(y * y, axis=-1, keepdims=True) + NORM_EPS)
        x1_ref[...] = x_ref[...] + y * rstd * g_ref[...]

    row = lambda n: pl.BlockSpec((tm, n), lambda i: (i, 0))
    return pl.pallas_call(
        body, name=name,
        out_shape=(jax.ShapeDtypeStruct((rows, d), F32),) * 2,
        grid=(rows // tm,),
        in_specs=[row(k), _full((k, d)), _full((1, d)), row(d)],
        out_specs=(row(d), row(d)),
        compiler_params=_params("parallel"),
    )(ycat, w_out, gain, x)


def _rmsnorm_bwd(dout, y, gain):
    rstd = lax.rsqrt(jnp.mean(y * y, axis=-1, keepdims=True) + NORM_EPS)
    yhat = y * rstd
    dyn = dout * gain
    dy = rstd * (dyn - yhat * jnp.mean(dyn * yhat, axis=-1, keepdims=True))
    return dy, dout * yhat


def even_out_bwd(dx1, y, gain, w_out, name):
    rows, d = y.shape
    k = w_out.shape[0]
    tm = min(ROW_TILE, rows)

    def body(dx_ref, y_ref, g_ref, w_ref, dy_ref, dyc_ref, dg_ref):
        dy, dg_rows = _rmsnorm_bwd(dx_ref[...], y_ref[...], g_ref[...])
        dyb = dy.astype(BF16)
        dy_ref[...] = dyb
        dyc_ref[...] = _dot_nt(dyb, w_ref[...])
        _accumulate(dg_ref, _colsum(dg_rows), pl.program_id(0))

    row = lambda n: pl.BlockSpec((tm, n), lambda i: (i, 0))
    return pl.pallas_call(
        body, name=name,
        out_shape=(jax.ShapeDtypeStruct((rows, d), BF16), jax.ShapeDtypeStruct((rows, k), F32),
                   jax.ShapeDtypeStruct((1, d), F32)),
        grid=(rows // tm,),
        in_specs=[row(d), row(d), _full((1, d)), _full((k, d))],
        out_specs=(row(d), row(k), _full((1, d))),
        compiler_params=_params("arbitrary"),
    )(dx1, y, gain, w_out)


def _chunk_cumsum(g, reverse):
    n, c = g.shape
    g = g.reshape(n // GLA_CHUNK, GLA_CHUNK, c)
    pos = lax.broadcasted_iota(jnp.int32, (1, GLA_CHUNK, c), 1)
    s = 1
    while s < GLA_CHUNK:
        if reverse:
            g = g + jnp.where(pos < GLA_CHUNK - s, pltpu.roll(g, GLA_CHUNK - s, 1), 0.0)
        else:
            g = g + jnp.where(pos >= s, pltpu.roll(g, s, 1), 0.0)
        s *= 2
    return g.reshape(n, c)


def _gla_prepare(q_ref, k_ref, lr_ref, wg_ref, bg_ref, reverse, n_chunks):
    z = _dot(lr_ref[...].astype(BF16), wg_ref[0]) + bg_ref[0]
    g = -_softplus(-z) * (1.0 / GLA_NORMALIZER)
    bcum = _chunk_cumsum(g, reverse).reshape(n_chunks, GLA_CHUNK, GLA_DK)
    edge = 0 if reverse else GLA_CHUNK - 1
    btot = bcum[:, edge:edge + 1, :]
    e_pos = jnp.exp(bcum)
    e_neg = jnp.exp(-bcum)
    e_st = jnp.exp(btot - bcum)
    q3 = q_ref[...].reshape(n_chunks, GLA_CHUNK, GLA_DK)
    k3 = k_ref[...].reshape(n_chunks, GLA_CHUNK, GLA_DK)
    scale = GLA_DK ** -0.5
    q_in = q3 * scale * e_pos
    k_in = k3 * e_neg
    k_st = k3 * e_st
    dec = jnp.exp(btot)
    return z, q_in, k_in, k_st, dec, (scale * e_pos, e_neg, e_st)


def _gla_mask(reverse):
    i = lax.broadcasted_iota(jnp.int32, (GLA_CHUNK, GLA_CHUNK), 0)
    j = lax.broadcasted_iota(jnp.int32, (GLA_CHUNK, GLA_CHUNK), 1)
    return (j >= i) if reverse else (j <= i)


def _gla_specs(rows, n_blocks, reverse):
    tix = (lambda s: n_blocks - 1 - s) if reverse else (lambda s: s)
    d = 1 if reverse else 0
    lr_block = LR_COL // LANES
    specs = [pl.BlockSpec((rows, GLA_DK), lambda h, s: (tix(s), h)),
             pl.BlockSpec((rows, GLA_DK), lambda h, s: (tix(s), GLA_HEADS + h)),
             pl.BlockSpec((rows, GLA_DV), lambda h, s: (tix(s), GLA_HEADS + h)),
             pl.BlockSpec((rows, LANES), lambda h, s: (tix(s), lr_block)),
             pl.BlockSpec((1, LANES, GLA_DK), lambda h, s: (d, 0, h)),
             pl.BlockSpec((1, 1, GLA_DK), lambda h, s: (d, 0, h))]
    return specs, tix


def gla_fwd(proj, wg_pad, bg, reverse, name):
    rows_total = proj.shape[0]
    rows = min(GLA_BLOCK, rows_total)
    n_blocks = rows_total // rows
    n_chunks = rows // GLA_CHUNK
    specs, tix = _gla_specs(rows, n_blocks, reverse)

    def body(q_ref, k_ref, v_ref, lr_ref, wg_ref, bg_ref, o_ref, st_ref, state, kv_scr, dec_scr):
        _, q_in, k_in, k_st, dec, _ = _gla_prepare(q_ref, k_ref, lr_ref, wg_ref, bg_ref, reverse, n_chunks)
        vb = v_ref[...].reshape(n_chunks, GLA_CHUNK, GLA_DV).astype(BF16)
        qb = q_in.astype(BF16)
        p = jnp.where(_gla_mask(reverse), _bdot(qb, k_in.astype(BF16), 2, 2), 0.0)
        o = _bdot(p.astype(BF16), vb, 2, 1)
        kv_scr[...] = _bdot(vb, k_st.astype(BF16), 1, 1)
        dec_scr[...] = jnp.broadcast_to(dec, dec_scr.shape)

        @pl.when(pl.program_id(1) == 0)
        def _():
            state[...] = jnp.zeros_like(state)

        for c in range(n_chunks):
            cc = n_chunks - 1 - c if reverse else c
            st_ref[0, cc] = state[...]
            state[...] = state[...] * dec_scr[cc, 0:1] + kv_scr[cc]
        o = o + _bdot(qb, st_ref[0].astype(BF16), 2, 2)
        o_ref[...] = o.reshape(rows, GLA_DV)

    return pl.pallas_call(
        body, name=name,
        out_shape=(jax.ShapeDtypeStruct((rows_total, GLA_HEADS * GLA_DV), F32),
                   jax.ShapeDtypeStruct((GLA_HEADS, rows_total // GLA_CHUNK, GLA_DV, GLA_DK), F32)),
        grid=(GLA_HEADS, n_blocks),
        in_specs=specs,
        out_specs=(pl.BlockSpec((rows, GLA_DV), lambda h, s: (tix(s), h)),
                   pl.BlockSpec((1, n_chunks, GLA_DV, GLA_DK), lambda h, s: (h, tix(s), 0, 0))),
        scratch_shapes=[pltpu.VMEM((GLA_DV, GLA_DK), F32), pltpu.VMEM((n_chunks, GLA_DV, GLA_DK), F32),
                        pltpu.VMEM((n_chunks, SUBLANES, GLA_DK), F32)],
        compiler_params=_params("parallel", "arbitrary"),
    )(proj, proj, proj, proj, wg_pad, bg)


def gla_bwd(proj, wg_pad, bg, d_o, states, dqkv_in, reverse, name):
    rows_total = proj.shape[0]
    rows = min(GLA_BLOCK, rows_total)
    n_blocks = rows_total // rows
    n_chunks = rows // GLA_CHUNK
    specs, tix = _gla_specs(rows, n_blocks, not reverse)
    d = 1 if reverse else 0
    specs[4] = pl.BlockSpec((1, LANES, GLA_DK), lambda h, s: (d, 0, h))
    specs[5] = pl.BlockSpec((1, 1, GLA_DK), lambda h, s: (d, 0, h))
    add = dqkv_in is not None

    def body(*refs):
        q_ref, k_ref, v_ref, lr_ref, wg_ref, bg_ref, do_ref, st_ref = refs[:8]
        refs = refs[8:]
        if add:
            aq_ref, ak_ref, av_ref = refs[:3]
            refs = refs[3:]
        dq_ref, dk_ref, dv_ref, dz_ref, dstate, g_scr, dec_scr, dsn_scr = refs
        z, q_in, k_in, k_st, dec, (f_q, f_k, f_s) = _gla_prepare(q_ref, k_ref, lr_ref, wg_ref, bg_ref, reverse,
                                                                 n_chunks)
        mask = _gla_mask(reverse)
        vb = v_ref[...].reshape(n_chunks, GLA_CHUNK, GLA_DV).astype(BF16)
        dob = do_ref[...].reshape(n_chunks, GLA_CHUNK, GLA_DV).astype(BF16)
        qb, kb, ksb = q_in.astype(BF16), k_in.astype(BF16), k_st.astype(BF16)
        st = st_ref[0]
        stb = st.astype(BF16)
        pb = jnp.where(mask, _bdot(qb, kb, 2, 2), 0.0).astype(BF16)
        dpb = jnp.where(mask, _bdot(dob, vb, 2, 2), 0.0).astype(BF16)
        d_qin = _bdot(dpb, kb, 2, 1) + _bdot(dob, stb, 2, 1)
        d_kin = _bdot(dpb, qb, 1, 1)
        dv = _bdot(pb, dob, 1, 1)
        g_scr[...] = _bdot(dob, qb, 1, 1)
        dec_scr[...] = jnp.broadcast_to(dec, dec_scr.shape)

        @pl.when(pl.program_id(1) == 0)
        def _():
            dstate[...] = jnp.zeros_like(dstate)

        for c in range(n_chunks):
            cc = c if reverse else n_chunks - 1 - c
            dsn_scr[cc] = dstate[...]
            dstate[...] = dstate[...] * dec_scr[cc, 0:1] + g_scr[cc]
        dsn = dsn_scr[...]
        dsnb = dsn.astype(BF16)
        dv = dv + _bdot(ksb, dsnb, 2, 2)
        d_kst = _bdot(vb, dsnb, 2, 1)
        d_dec = jnp.sum(dsn * st, axis=1, keepdims=True)
        ks_term = d_kst * k_st
        d_btot = d_dec * dec + jnp.sum(ks_term, axis=1, keepdims=True)
        d_b = d_qin * q_in - d_kin * k_in - ks_term
        pos = lax.broadcasted_iota(jnp.int32, d_b.shape, 1)
        edge = 0 if reverse else GLA_CHUNK - 1
        d_b = d_b + jnp.where(pos == edge, d_btot, 0.0)
        dg = _chunk_cumsum(d_b.reshape(rows, GLA_DK), not reverse)
        dz_ref[...] = dg * (1.0 / GLA_NORMALIZER) * _sigmoid(-z)
        dq = (d_qin * f_q).reshape(rows, GLA_DK)
        dk = (d_kin * f_k + d_kst * f_s).reshape(rows, GLA_DK)
        dv = dv.reshape(rows, GLA_DV)
        if add:
            dq_ref[...] = (dq + aq_ref[...]).astype(BF16)
            dk_ref[...] = (dk + ak_ref[...]).astype(BF16)
            dv_ref[...] = (dv + av_ref[...]).astype(BF16)
        else:
            dq_ref[...] = dq
            dk_ref[...] = dk
            dv_ref[...] = dv

    qkv_specs = [pl.BlockSpec((rows, GLA_DK), lambda h, s: (tix(s), h)),
                 pl.BlockSpec((rows, GLA_DK), lambda h, s: (tix(s), h)),
                 pl.BlockSpec((rows, GLA_DV), lambda h, s: (tix(s), h))]
    in_specs = specs + [pl.BlockSpec((rows, GLA_DV), lambda h, s: (tix(s), h)),
                        pl.BlockSpec((1, n_chunks, GLA_DV, GLA_DK), lambda h, s: (h, tix(s), 0, 0))]
    args = [proj, proj, proj, proj, wg_pad, bg, d_o, states]
    out_dtype = F32
    if add:
        in_specs += qkv_specs
        args += list(dqkv_in)
        out_dtype = BF16
    return pl.pallas_call(
        body, name=name,
        out_shape=(jax.ShapeDtypeStruct((rows_total, GLA_HEADS * GLA_DK), out_dtype),
                   jax.ShapeDtypeStruct((rows_total, GLA_HEADS * GLA_DK), out_dtype),
                   jax.ShapeDtypeStruct((rows_total, GLA_HEADS * GLA_DV), out_dtype),
                   jax.ShapeDtypeStruct((rows_total, GLA_HEADS * GLA_DK), F32)),
        grid=(GLA_HEADS, n_blocks),
        in_specs=in_specs,
        out_specs=(pl.BlockSpec((rows, GLA_DK), lambda h, s: (tix(s), h)),
                   pl.BlockSpec((rows, GLA_DK), lambda h, s: (tix(s), h)),
                   pl.BlockSpec((rows, GLA_DV), lambda h, s: (tix(s), h)),
                   pl.BlockSpec((rows, GLA_DK), lambda h, s: (tix(s), h))),
        scratch_shapes=[pltpu.VMEM((GLA_DV, GLA_DK), F32), pltpu.VMEM((n_chunks, GLA_DV, GLA_DK), F32),
                        pltpu.VMEM((n_chunks, SUBLANES, GLA_DK), F32),
                        pltpu.VMEM((n_chunks, GLA_DV, GLA_DK), F32)],
        compiler_params=_params("parallel", "arbitrary"),
    )(*args)


def gla_gate_bwd(proj, dz_f, dz_b, wg_pad, name):
    rows_total = proj.shape[0]
    tm = min(ROW_TILE, rows_total)
    n_key = GLA_HEADS * GLA_DK

    def body(lr_ref, dzf_ref, dzb_ref, wg_ref, dlr_ref, dwg_ref, dbg_ref):
        step = pl.program_id(0)
        lr_t = jnp.transpose(lr_ref[...])
        dzf, dzb = dzf_ref[...], dzb_ref[...]
        dzf16, dzb16 = dzf.astype(BF16), dzb.astype(BF16)
        dlr_ref[...] = (_dot_nt(dzf16, wg_ref[0]) + _dot_nt(dzb16, wg_ref[1])).astype(BF16)
        dwf = _dot(lr_t[0:GLA_RANK].astype(BF16), dzf16)
        dwb = _dot(lr_t[GLA_RANK:2 * GLA_RANK].astype(BF16), dzb16)
        dbg = jnp.concatenate([_colsum(dzf), _colsum(dzb)], axis=0)

        @pl.when(step == 0)
        def _():
            dwg_ref[0] = dwf
            dwg_ref[1] = dwb
            dbg_ref[...] = dbg

        @pl.when(step > 0)
        def _():
            dwg_ref[0] += dwf
            dwg_ref[1] += dwb
            dbg_ref[...] += dbg

    return pl.pallas_call(
        body, name=name,
        out_shape=(jax.ShapeDtypeStruct((rows_total, LANES), BF16), jax.ShapeDtypeStruct((2, GLA_RANK, n_key), F32),
                   jax.ShapeDtypeStruct((2, n_key), F32)),
        grid=(rows_total // tm,),
        in_specs=[pl.BlockSpec((tm, LANES), lambda i: (i, LR_COL // LANES)),
                  pl.BlockSpec((tm, n_key), lambda i: (i, 0)), pl.BlockSpec((tm, n_key), lambda i: (i, 0)),
                  _full((2, LANES, n_key))],
        out_specs=(pl.BlockSpec((tm, LANES), lambda i: (i, 0)), _full((2, GLA_RANK, n_key)), _full((2, n_key))),
        compiler_params=_params("arbitrary"),
    )(proj, dz_f, dz_b, wg_pad)


def _head_norm(o, gain):
    outs, hats, rstds = [], [], []
    for h in range(GLA_HEADS):
        oh = o[:, h * GLA_DV:(h + 1) * GLA_DV]
        rstd = lax.rsqrt(jnp.mean(oh * oh, axis=-1, keepdims=True) + NORM_EPS)
        hat = oh * rstd
        outs.append(hat * gain)
        hats.append(hat)
        rstds.append(rstd)
    return outs, hats, rstds


def odd_out_fwd(o_f, o_b, proj, head_gain, w_out, gain, x1, target, name):
    rows, d = x1.shape
    tm = min(ROW_TILE, rows)
    r_block = (2 * GLA_HEADS * GLA_DK + GLA_HEADS * GLA_DV) // d

    def body(of_ref, ob_ref, r_ref, hg_ref, w_ref, g_ref, x1_ref, tgt_ref, y2_ref, dy_ref, dx2_ref, loss_ref,
             dg_ref):
        step = pl.program_id(0)
        on, _, _ = _head_norm(of_ref[...] + ob_ref[...], hg_ref[...])
        r = r_ref[...]
        y2 = (jnp.concatenate(on, axis=1) * (r * _sigmoid(r))).astype(BF16)
        y2_ref[...] = y2
        y = _dot(y2, w_ref[...])
        gain_v = g_ref[...]
        rstd = lax.rsqrt(jnp.mean(y * y, axis=-1, keepdims=True) + NORM_EPS)
        x2 = x1_ref[...] + y * rstd * gain_v
        diff = x2 - tgt_ref[...]
        loss = 0.5 * jnp.sum(jnp.mean(diff * diff, axis=-1, keepdims=True), axis=0, keepdims=True)
        dx2 = diff * (1.0 / d)
        dx2_ref[...] = dx2
        dy, dg_rows = _rmsnorm_bwd(dx2, y, gain_v)
        dy_ref[...] = dy.astype(BF16)
        _accumulate(loss_ref, jnp.broadcast_to(loss, loss_ref.shape), step)
        _accumulate(dg_ref, _colsum(dg_rows), step)

    row = lambda n, col=0: pl.BlockSpec((tm, n), lambda i: (i, col))
    return pl.pallas_call(
        body, name=name,
        out_shape=(jax.ShapeDtypeStruct((rows, d), BF16), jax.ShapeDtypeStruct((rows, d), BF16),
                   jax.ShapeDtypeStruct((rows, d), F32), jax.ShapeDtypeStruct((SUBLANES, LANES), F32),
                   jax.ShapeDtypeStruct((1, d), F32)),
        grid=(rows // tm,),
        in_specs=[row(d), row(d), row(d, r_block), _full((1, GLA_DV)), _full((d, d)), _full((1, d)), row(d), row(d)],
        out_specs=(row(d), row(d), row(d), _full((SUBLANES, LANES)), _full((1, d))),
        compiler_params=_params("arbitrary"),
    )(o_f, o_b, proj, head_gain, w_out, gain, x1, target)


def odd_out_bwd(dy, w_out, o_f, o_b, proj, head_gain, name):
    rows, d = dy.shape
    tm = min(ROW_TILE, rows)
    r_block = (2 * GLA_HEADS * GLA_DK + GLA_HEADS * GLA_DV) // d

    def body(dy_ref, w_ref, of_ref, ob_ref, r_ref, hg_ref, dr_ref, do_ref, dhg_ref):
        dy2 = _dot_nt(dy_ref[...], w_ref[...])
        hg = hg_ref[...]
        on, hats, rstds = _head_norm(of_ref[...] + ob_ref[...], hg)
        r = r_ref[...]
        sr = _sigmoid(r)
        dr_ref[...] = (dy2 * jnp.concatenate(on, axis=1) * (sr * (1.0 + r * (1.0 - sr)))).astype(BF16)
        d_on = dy2 * (r * sr)
        d_os, dhg = [], None
        for h in range(GLA_HEADS):
            dn = d_on[:, h * GLA_DV:(h + 1) * GLA_DV]
            part = _colsum(dn * hats[h])
            dhg = part if dhg is None else dhg + part
            dng = dn * hg
            d_os.append(rstds[h] * (dng - hats[h] * jnp.mean(dng * hats[h], axis=-1, keepdims=True)))
        do_ref[...] = jnp.concatenate(d_os, axis=1)
        _accumulate(dhg_ref, dhg, pl.program_id(0))

    row = lambda n, col=0: pl.BlockSpec((tm, n), lambda i: (i, col))
    return pl.pallas_call(
        body, name=name,
        out_shape=(jax.ShapeDtypeStruct((rows, d), BF16), jax.ShapeDtypeStruct((rows, d), F32),
                   jax.ShapeDtypeStruct((1, GLA_DV), F32)),
        grid=(rows // tm,),
        in_specs=[row(d), _full((d, d)), row(d), row(d), row(d, r_block), _full((1, GLA_DV))],
        out_specs=(row(d), row(d), _full((1, GLA_DV))),
        compiler_params=_params("arbitrary"),
    )(dy, w_out, o_f, o_b, proj, head_gain)


def local_step(x, target, w, reduce_first=None, reduce_second=None, late_weights=None):
    g = {}
    proj_e, h0 = norm_matmul(x, w["even_norm_pre"], w["even_w_in"], "even_in_proj")
    h_dir, acts = zip(*[rglru_fwd(proj_e, w["rg_conv_w"], w["rg_conv_b"], w["rg_gate_w"][d], w["rg_gate_b"][d],
                                  w["rg_lambda"][d], d == 1, "rglru_fwd_%d" % d) for d in range(2)])
    ycat = even_mix_fwd(proj_e, h_dir[0], h_dir[1], w["sc_conv_w"], "even_mix_fwd")
    if late_weights is not None:
        w = dict(w, **late_weights(ycat))
    x1, y_e = even_out_fwd(ycat, w["even_w_out"], w["even_norm_post"], x, "even_out_fwd")
    proj_o, h1 = norm_matmul(x1, w["odd_norm_pre"], w["odd_w_in"], "odd_in_proj")
    o_dir, st_dir = [], []
    for d in range(2):
        o, st = gla_fwd(proj_o, w["gla_wg_pad"], w["gla_b_gate"], d == 1, "gla_fwd_%d" % d)
        o_dir.append(o)
        st_dir.append(st)
    y2, dy_o, dx2, loss, g["odd_norm_post"] = odd_out_fwd(
        o_dir[0], o_dir[1], proj_o, w["gla_norm_g"], w["odd_w_out"], w["odd_norm_post"], x1, target, "odd_out_fwd")
    g["odd_w_out"] = matmul_dw(y2, dy_o, D_MODEL, "odd_w_out_grad")[0]
    dr, d_o, g["gla_norm_g"] = odd_out_bwd(dy_o, w["odd_w_out"], o_dir[0], o_dir[1], proj_o, w["gla_norm_g"],
                                           "odd_out_bwd")
    dq, dk, dv, dz_f = gla_bwd(proj_o, w["gla_wg_pad"], w["gla_b_gate"], d_o, st_dir[0], None, False, "gla_bwd_0")
    dq, dk, dv, dz_b = gla_bwd(proj_o, w["gla_wg_pad"], w["gla_b_gate"], d_o, st_dir[1], (dq, dk, dv), True,
                               "gla_bwd_1")
    dlr, g["gla_w_gate_lr"], g["gla_b_gate"] = gla_gate_bwd(proj_o, dz_f, dz_b, w["gla_wg_pad"], "gla_gate_bwd")
    dproj_o = [dq, dk, dv, dr, dlr]
    g["odd_w_in"] = jnp.concatenate(matmul_dw_pieces(h1, dproj_o, "odd_w_in_grad"), axis=1)[:, :ODD_IN]
    dx1, g["odd_norm_pre"] = inproj_bwd_pieces(dproj_o, w["odd_w_in"], x1, w["odd_norm_pre"], dx2, "odd_in_proj_bwd")
    dy_e, dycat, g["even_norm_post"] = even_out_bwd(dx1, y_e, w["even_norm_post"], w["even_w_out"], "even_out_bwd")
    g["even_w_out"] = matmul_dw(ycat, dy_e, D_MODEL, "even_w_out_grad")[0]
    lam = w["rg_lambda"] if reduce_first is None else w["rg_lambda"] + reduce_first(g)
    dua, dgw, dgb, dlam = [], [], [], []
    for d in range(2):
        a, b, c, e = rglru_bwd(proj_e, dycat, h_dir[d], acts[d], w["rg_gate_w"][d], lam[d], d == 1,
                               "rglru_bwd_%d" % d)
        dua.append(a)
        dgw.append(b)
        dgb.append(c)
        dlam.append(e)
    dproj_e, g["rg_conv_w"], g["rg_conv_b"], g["sc_conv_w"] = even_mix_bwd(
        proj_e, dycat, h_dir[0], h_dir[1], dua[0], dua[1], w["rg_conv_w"], w["sc_conv_w"], "even_mix_bwd")
    dgw = jnp.stack(dgw).reshape(2, RG_HEADS, RG_HEAD_DIM, 2, RG_HEAD_DIM)
    g["rg_gate_w"] = jnp.transpose(dgw, (0, 3, 1, 2, 4))
    g["rg_gate_b"] = jnp.stack(dgb).reshape(2, 2, RG_HEADS, RG_HEAD_DIM)
    g["rg_lambda"] = jnp.concatenate(dlam, axis=0)
    g["even_w_in"] = matmul_dw(h0, dproj_e, EVEN_IN // 4, "even_w_in_grad")
    gain = w["even_norm_pre"] if reduce_second is None else w["even_norm_pre"] + reduce_second(g)
    grad_x, g["even_norm_pre"] = inproj_bwd(dproj_e, w["even_w_in"], x, gain, dx1, "even_in_proj_bwd")
    return loss, grad_x, g


def _prepare_weights(full):
    w = {}
    for name in ("even_norm_pre", "even_norm_post", "rg_conv_b", "odd_norm_pre", "odd_norm_post", "gla_norm_g"):
        if name in full:
            w[name] = full[name].reshape(1, -1)
    for name in ("rg_conv_w", "sc_conv_w"):
        if name in full:
            w[name] = full[name]
    for name in ("even_w_out", "odd_w_out"):
        if name in full:
            w[name] = full[name].astype(BF16)
    if "even_w_in" in full:
        w["even_w_in"] = full["even_w_in"].astype(BF16)
        if w["even_w_in"].ndim == 2:
            w["even_w_in"] = jnp.transpose(w["even_w_in"].reshape(D_MODEL, 4, EVEN_IN // 4), (1, 0, 2))
    if "rg_gate_w" in full:
        gw = jnp.transpose(full["rg_gate_w"].astype(BF16), (0, 2, 3, 1, 4))
        w["rg_gate_w"] = gw.reshape(2, RG_HEADS, RG_HEAD_DIM, 2 * RG_HEAD_DIM)
        w["rg_gate_b"] = full["rg_gate_b"].reshape(2, 2, D_MODEL)
        w["rg_lambda"] = full["rg_lambda"].reshape(2, 1, D_MODEL)
    if "odd_w_in" in full:
        w_in = jnp.pad(full["odd_w_in"].astype(BF16), ((0, 0), (0, ODD_IN_PAD - ODD_IN)))
        w["odd_w_in"] = w_in.reshape(1, D_MODEL, ODD_IN_PAD)
    if "gla_w_gate_lr" in full:
        wg = full["gla_w_gate_lr"].astype(BF16)
        w["gla_wg_pad"] = jnp.stack([jnp.pad(wg[d], ((d * GLA_RANK, LANES - (d + 1) * GLA_RANK), (0, 0)))
                                     for d in range(2)])
        w["gla_b_gate"] = full["gla_b_gate"].reshape(2, 1, GLA_HEADS * GLA_DK)
    return w


SHARDED_SMALL = (("rg_conv_w", (4, 256)), ("rg_lambda", (2, 256)), ("sc_conv_w", (3, 256)),
                 ("odd_norm_pre", (256,)), ("odd_norm_post", (256,)), ("gla_w_gate_lr", (2, 16, 128)),
                 ("gla_b_gate", (2, 128)), ("gla_norm_g", (64,)))
SHARDED_ROWS = 96
REPLICATED = (("even_norm_post", (1024,)), ("rg_conv_b", (1024,)),
              ("rg_gate_b", (2, 2, 8, 128)), ("rg_gate_w", (2, 2, 8, 128, 128)))
LAST_REPLICATED = (("even_norm_pre", (1024,)),)
LAST_ROWS = 8
REPLICATED_ROWS = 4160
REP_PART = REPLICATED_ROWS // 8
HALF_SHARDED = SHARDED_ROWS // 2
PACK_HALF = HALF_SHARDED + REP_PART


def _seg_rows(shape):
    n = 1
    for s in shape:
        n *= s
    return -(-n // (SUBLANES * LANES)) * SUBLANES


def _pack(arrays, spec, total_rows, lead=()):
    parts = []
    for name, shape in spec:
        flat = arrays[name].reshape(lead + (-1,))
        pad = _seg_rows(shape) * LANES - flat.shape[-1]
        if pad:
            flat = jnp.pad(flat, [(0, 0)] * len(lead) + [(0, pad)])
        parts.append(flat.reshape(lead + (-1, LANES)))
    rows = jnp.concatenate(parts, axis=len(lead))
    pad = total_rows - rows.shape[len(lead)]
    return jnp.pad(rows, [(0, 0)] * len(lead) + [(0, pad), (0, 0)])


def _unpack(rows, spec, lead=()):
    out, at = {}, 0
    for name, shape in spec:
        n = 1
        for s in shape:
            n *= s
        k = _seg_rows(shape)
        seg = lax.slice_in_dim(rows, at, at + k, axis=len(lead)).reshape(lead + (-1,))
        out[name] = lax.slice_in_dim(seg, 0, n, axis=len(lead)).reshape(lead + shape)
        at += k
    return out


def _split_owners(arr):
    a = arr.reshape(arr.shape[:-1] + (4, arr.shape[-1] // 4))
    return jnp.moveaxis(a, -2, 0)


def _merge_owners(arr):
    a = jnp.moveaxis(arr, 0, -2)
    return a.reshape(a.shape[:-2] + (-1,))


HBM_SPEC = pl.BlockSpec(memory_space=pltpu.HBM)


def _position():
    x, y, c = lax.axis_index("x"), lax.axis_index("y"), lax.axis_index("c")
    chips = [(1 - x, y), (x, 1 - y), (1 - x, 1 - y)]
    return x, y, c, chips


def _remote(src, dst, send_sem, recv_sem, device):
    return pltpu.make_async_remote_copy(src_ref=src, dst_ref=dst, send_sem=send_sem, recv_sem=recv_sem,
                                        device_id=device, device_id_type=MESH)


SEM_SPEC = pl.BlockSpec(memory_space=pltpu.SEMAPHORE)
SIDE_EFFECT = pltpu.SideEffectType.DATAFLOW_SIDE_EFFECTING


def _gather_copies(ins, lands, n_h, send_sems, recv_sems):
    x, y, c, chips = _position()
    me = 2 * x + y
    copies = []
    for a in range(len(ins)):
        for k, chip in enumerate(chips):
            src = ins[a].at[c] if a < n_h else ins[a]
            dst = lands[a].at[me, c] if a < n_h else lands[a].at[me]
            copies.append(_remote(src, dst, send_sems.at[3 * a + k], recv_sems.at[3 * a + k], (chip[0], chip[1], c)))
    return copies


def gather_start(halved, whole, name):
    arrays = list(halved) + list(whole)
    n, n_h = len(arrays), len(halved)
    lands = [lax.empty((4,) + a.shape, a.dtype) for a in arrays]

    def body(*refs):
        ins, lz, send_sems, recv_sems, token = refs[:n], refs[n:2 * n], refs[2 * n], refs[2 * n + 1], refs[-1]
        for cp in _gather_copies(ins, lz, n_h, send_sems, recv_sems):
            cp.start()
        token[...] = jnp.zeros_like(token)

    operands = [pltpu.with_memory_space_constraint(a, pltpu.HBM) for a in arrays + lands]
    return pl.pallas_call(
        body, name=name,
        out_shape=(pltpu.SemaphoreType.DMA((3 * n,)), pltpu.SemaphoreType.DMA((3 * n,)))
        + tuple(pltpu.HBM(a.shape, a.dtype) for a in operands) + (jax.ShapeDtypeStruct((SUBLANES, LANES), F32),),
        in_specs=[HBM_SPEC] * (2 * n),
        out_specs=(SEM_SPEC, SEM_SPEC) + (HBM_SPEC,) * (2 * n) + (pl.BlockSpec(memory_space=pltpu.VMEM),),
        input_output_aliases={i: 2 + i for i in range(2 * n)},
        compiler_params=pltpu.CompilerParams(has_side_effects=SIDE_EFFECT),
    )(*operands)


def gather_wait(started, n_h, after, name):
    send_sems, recv_sems = started[0], started[1]
    operands = list(started[2:-1])
    n = len(operands) // 2

    def body(*refs):
        ins, lz, send_ref, recv_ref = refs[:n], refs[n:2 * n], refs[2 * n], refs[2 * n + 1]
        for cp in _gather_copies(ins, lz, n_h, send_ref, recv_ref):
            cp.wait_send()
            cp.wait_recv()

    outs = pl.pallas_call(
        body, name=name,
        out_shape=tuple(pltpu.HBM(a.shape, a.dtype) for a in operands),
        in_specs=[HBM_SPEC] * (2 * n) + [SEM_SPEC, SEM_SPEC, pl.BlockSpec(memory_space=pl.ANY)],
        out_specs=(HBM_SPEC,) * (2 * n),
        input_output_aliases={i: i for i in range(2 * n)},
        compiler_params=pltpu.CompilerParams(has_side_effects=SIDE_EFFECT),
    )(*operands, send_sems, recv_sems, after)
    return outs[n:]


def pass_to_sibling(fulls, name):
    n = len(fulls)

    def body(*refs):
        bufs = refs[n:2 * n]
        send_sems, recv_sems = refs[2 * n:]
        x, y, c, chips = _position()
        sibling = (x, y, 1 - c)
        copies = []
        for a in range(n):
            for k, chip in enumerate(chips):
                q = 2 * chip[0] + chip[1]
                cp = _remote(bufs[a].at[q, c], bufs[a].at[q, c], send_sems.at[3 * a + k], recv_sems.at[3 * a + k],
                             sibling)
                cp.start()
                copies.append(cp)
        for a in range(n):
            for k, chip in enumerate(chips):
                q = 2 * chip[0] + chip[1]
                passed = bufs[a].at[q, 1 - c]
                _remote(passed, passed, send_sems.at[3 * a + k], recv_sems.at[3 * a + k], sibling).wait_recv()
        for cp in copies:
            cp.wait_send()

    return pl.pallas_call(
        body, name=name,
        out_shape=[jax.ShapeDtypeStruct(a.shape, a.dtype) for a in fulls],
        in_specs=[HBM_SPEC] * n, out_specs=[HBM_SPEC] * n,
        input_output_aliases={i: i for i in range(n)},
        scratch_shapes=[pltpu.SemaphoreType.DMA((3 * n,)), pltpu.SemaphoreType.DMA((3 * n,))],
    )(*fulls)


def place_own(full, own, chip, name):
    _, _, r, cols = full.shape
    tr = _row_tile(r, cols)

    def body(p_ref, own_ref, full_ref, o_ref):
        o_ref[0] = own_ref[...]

    return pl.pallas_call(
        body, name=name,
        out_shape=jax.ShapeDtypeStruct(full.shape, full.dtype),
        grid_spec=pltpu.PrefetchScalarGridSpec(
            num_scalar_prefetch=1, grid=(2, r // tr),
            in_specs=[pl.BlockSpec((1, tr, cols), lambda h, i, p_ref: (h, i, 0)), pl.BlockSpec(memory_space=pl.ANY)],
            out_specs=pl.BlockSpec((1, 1, tr, cols), lambda h, i, p_ref: (p_ref[0], h, i, 0))),
        input_output_aliases={2: 0},
        compiler_params=_params("parallel", "parallel"),
    )(chip, own, full)


def exchange_with_sibling(arrays, name):
    n = len(arrays)

    def body(*refs):
        ins, outs = refs[:n], refs[n:2 * n]
        send_sems, recv_sems = refs[2 * n:]
        x, y, c, _ = _position()
        copies = []
        for a in range(n):
            cp = _remote(ins[a].at[:, 1 - c], outs[a], send_sems.at[a], recv_sems.at[a], (x, y, 1 - c))
            cp.start()
            copies.append(cp)
        for cp in copies:
            cp.wait()

    return pl.pallas_call(
        body, name=name,
        out_shape=[jax.ShapeDtypeStruct((a.shape[0],) + a.shape[2:], a.dtype) for a in arrays],
        in_specs=[HBM_SPEC] * n, out_specs=[HBM_SPEC] * n,
        scratch_shapes=[pltpu.SemaphoreType.DMA((n,)), pltpu.SemaphoreType.DMA((n,))],
    )(*arrays)


def _chip_copies(ins, lands, send_sems, recv_sems):
    x, y, c, chips = _position()
    copies = []
    for a in range(len(ins)):
        for k, chip in enumerate(chips):
            q = 2 * chip[0] + chip[1]
            copies.append(_remote(ins[a].at[q], lands[a].at[k], send_sems.at[3 * a + k], recv_sems.at[3 * a + k],
                                  (chip[0], chip[1], c)))
    return copies


def exchange_with_chips_start(arrays, name):
    n = len(arrays)
    lands = [lax.empty((3,) + a.shape[1:], a.dtype) for a in arrays]

    def body(*refs):
        ins, lz, send_sems, recv_sems, token = refs[:n], refs[n:2 * n], refs[2 * n], refs[2 * n + 1], refs[-1]
        for cp in _chip_copies(ins, lz, send_sems, recv_sems):
            cp.start()
        token[...] = jnp.zeros_like(token)

    operands = [pltpu.with_memory_space_constraint(a, pltpu.HBM) for a in list(arrays) + lands]
    return pl.pallas_call(
        body, name=name,
        out_shape=(pltpu.SemaphoreType.DMA((3 * n,)), pltpu.SemaphoreType.DMA((3 * n,)))
        + tuple(pltpu.HBM(a.shape, a.dtype) for a in operands) + (jax.ShapeDtypeStruct((SUBLANES, LANES), F32),),
        in_specs=[HBM_SPEC] * (2 * n),
        out_specs=(SEM_SPEC, SEM_SPEC) + (HBM_SPEC,) * (2 * n) + (pl.BlockSpec(memory_space=pltpu.VMEM),),
        input_output_aliases={i: 2 + i for i in range(2 * n)},
        compiler_params=pltpu.CompilerParams(has_side_effects=SIDE_EFFECT),
    )(*operands)


def exchange_with_chips_wait(started, after, name):
    send_sems, recv_sems = started[0], started[1]
    operands = list(started[2:-1])
    n = len(operands) // 2

    def body(*refs):
        ins, lz, send_ref, recv_ref = refs[:n], refs[n:2 * n], refs[2 * n], refs[2 * n + 1]
        for cp in _chip_copies(ins, lz, send_ref, recv_ref):
            cp.wait_send()
            cp.wait_recv()

    outs = pl.pallas_call(
        body, name=name,
        out_shape=tuple(pltpu.HBM(a.shape, a.dtype) for a in operands),
        in_specs=[HBM_SPEC] * (2 * n) + [SEM_SPEC, SEM_SPEC, pl.BlockSpec(memory_space=pl.ANY)],
        out_specs=(HBM_SPEC,) * (2 * n),
        input_output_aliases={i: i for i in range(2 * n)},
        compiler_params=pltpu.CompilerParams(has_side_effects=SIDE_EFFECT),
    )(*operands, send_sems, recv_sems, after)
    return outs[:n], outs[n:]


def share_totals(totals, pack_total, last_part):
    arrays = list(totals) + [pack_total]
    n = len(arrays)

    def body(*refs):
        ins, last, outs, rep, last_all = refs[:n], refs[n], refs[n + 1:2 * n + 1], refs[2 * n + 1], refs[2 * n + 2]
        send_sems, recv_sems, rep_send, rep_recv, last_send, last_recv = refs[2 * n + 3:]
        x, y, c, chips = _position()
        sibling = (x, y, 1 - c)
        me = 4 * x + 2 * y + c
        sends = []
        for a in range(n):
            cp = _remote(ins[a], outs[a], send_sems.at[a], recv_sems.at[a], sibling)
            cp.start()
            sends.append(cp)
        mine = ins[n - 1].at[pl.ds(HALF_SHARDED, REP_PART)]
        peers = [sibling]
        for chip in chips:
            peers += [(chip[0], chip[1], c), (chip[0], chip[1], 1 - c)]
        for j, peer in enumerate(peers):
            for src, dst, s_sem, r_sem in ((mine, rep, rep_send, rep_recv), (last, last_all, last_send, last_recv)):
                cp = _remote(src, dst.at[me], s_sem.at[j], r_sem.at[j], peer)
                cp.start()
                sends.append(cp)
        for a in range(n):
            _remote(outs[a], outs[a], send_sems.at[a], recv_sems.at[a], sibling).wait_recv()
        for j, peer in enumerate(peers):
            it = 4 * peer[0] + 2 * peer[1] + peer[2]
            _remote(rep.at[it], rep.at[it], rep_send.at[j], rep_recv.at[j], peer).wait_recv()
            _remote(last_all.at[it], last_all.at[it], last_send.at[j], last_recv.at[j], peer).wait_recv()
        for cp in sends:
            cp.wait_send()

    outs = pl.pallas_call(
        body, name="grad_share_totals",
        out_shape=[jax.ShapeDtypeStruct(a.shape, a.dtype) for a in arrays]
        + [jax.ShapeDtypeStruct((8, REP_PART, LANES), F32), jax.ShapeDtypeStruct((8,) + last_part.shape, F32)],
        in_specs=[HBM_SPEC] * (n + 1), out_specs=[HBM_SPEC] * (n + 2),
        scratch_shapes=[pltpu.SemaphoreType.DMA((n,)), pltpu.SemaphoreType.DMA((n,))]
        + [pltpu.SemaphoreType.DMA((7,))] * 4,
    )(*arrays, last_part)
    return outs[:n], outs[n], outs[n + 1]


def sum_parts(parts, name):
    def body(p_ref, o_ref):
        total = p_ref[0]
        for k in range(1, parts.shape[0]):
            total = total + p_ref[k]
        o_ref[...] = total

    return pl.pallas_call(body, name=name, out_shape=jax.ShapeDtypeStruct(parts.shape[1:], parts.dtype))(parts)


TILE_BYTES = 2 << 20


def _row_tile(rows, cols):
    best = None
    for t in range(SUBLANES, rows + 1, SUBLANES):
        if rows % t == 0 and t * cols * 4 <= TILE_BYTES:
            best = t
    return best if best is not None else rows


def add_sibling(mine, received, core, out_dtype, name):
    _, _, r, cols = mine.shape
    tr = _row_tile(r, cols)

    def body(c_ref, a_ref, b_ref, o_ref):
        o_ref[...] = (a_ref[0] + b_ref[...]).astype(out_dtype)

    return pl.pallas_call(
        body, name=name,
        out_shape=jax.ShapeDtypeStruct((4, r, cols), out_dtype),
        grid_spec=pltpu.PrefetchScalarGridSpec(
            num_scalar_prefetch=1, grid=(4, r // tr),
            in_specs=[pl.BlockSpec((1, 1, tr, cols), lambda o, i, c_ref: (o, c_ref[0], i, 0)),
                      pl.BlockSpec((1, tr, cols), lambda o, i, c_ref: (o, i, 0))],
            out_specs=pl.BlockSpec((1, tr, cols), lambda o, i, c_ref: (o, i, 0))),
        compiler_params=_params("parallel", "parallel"),
    )(core, mine, received)


def add_chips(own, received, chip, name):
    _, r, cols = own.shape
    tr = _row_tile(r, cols)

    def body(p_ref, a_ref, b0, b1, b2, o_ref):
        o_ref[...] = ((a_ref[0].astype(F32) + b0[0].astype(F32)) + b1[0].astype(F32)) + b2[0].astype(F32)

    rb = lambda k: pl.BlockSpec((1, tr, cols), lambda i, p_ref: (k, i, 0))
    return pl.pallas_call(
        body, name=name,
        out_shape=jax.ShapeDtypeStruct((r, cols), F32),
        grid_spec=pltpu.PrefetchScalarGridSpec(
            num_scalar_prefetch=1, grid=(r // tr,),
            in_specs=[pl.BlockSpec((1, tr, cols), lambda i, p_ref: (p_ref[0], i, 0)), rb(0), rb(1), rb(2)],
            out_specs=pl.BlockSpec((tr, cols), lambda i, p_ref: (i, 0))),
        compiler_params=_params("parallel"),
    )(chip, own, received, received, received)


def _adamw_update(gv, w_ref, m_ref, v_ref, d_ref, nm_ref, nv_ref):
    nm = ADAM_B1 * m_ref[...] + (1.0 - ADAM_B1) * gv
    nv = ADAM_B2 * v_ref[...] + (1.0 - ADAM_B2) * (gv * gv)
    nm_ref[...] = nm
    nv_ref[...] = nv
    m_hat = nm / (1.0 - ADAM_B1 ** ADAM_STEP)
    v_hat = nv / (1.0 - ADAM_B2 ** ADAM_STEP)
    d_ref[...] = -ADAM_LR * (m_hat / (jnp.sqrt(v_hat) + ADAM_EPS) + ADAM_WD * w_ref[...])


def adamw_halves(w, own, received, m, v, core, name):
    rows, cols = w.shape
    r = rows // 2
    tr = _row_tile(r, cols)
    nr = r // tr

    def body(c_ref, w_ref, own_ref, rec_ref, m_ref, v_ref, g_ref, d_ref, nm_ref, nv_ref):
        gv = jnp.where(pl.program_id(0) == c_ref[0], own_ref[...], rec_ref[...])
        g_ref[...] = gv
        _adamw_update(gv, w_ref, m_ref, v_ref, d_ref, nm_ref, nv_ref)

    whole = pl.BlockSpec((tr, cols), lambda h, i, c_ref: (h * nr + i, 0))
    half = pl.BlockSpec((tr, cols), lambda h, i, c_ref: (i, 0))
    return pl.pallas_call(
        body, name=name,
        out_shape=(jax.ShapeDtypeStruct((rows, cols), F32),) * 4,
        grid_spec=pltpu.PrefetchScalarGridSpec(
            num_scalar_prefetch=1, grid=(2, nr),
            in_specs=[whole, half, half, whole, whole], out_specs=(whole,) * 4),
        compiler_params=_params("parallel", "parallel"),
    )(core, w, own, received, m, v)


def adamw(w, g, m, v, name):
    r, cols = w.shape
    tr = _row_tile(r, cols)

    def body(w_ref, g_ref, m_ref, v_ref, d_ref, nm_ref, nv_ref):
        _adamw_update(g_ref[...], w_ref, m_ref, v_ref, d_ref, nm_ref, nv_ref)

    blk = pl.BlockSpec((tr, cols), lambda i: (i, 0))
    return pl.pallas_call(
        body, name=name,
        out_shape=(jax.ShapeDtypeStruct((r, cols), F32),) * 3,
        grid=(r // tr,),
        in_specs=[blk] * 4, out_specs=(blk,) * 3,
        compiler_params=_params("parallel"),
    )(w, g, m, v)


WEIGHTS = ("even_norm_pre", "even_norm_post", "even_w_in", "rg_conv_w", "rg_conv_b", "rg_gate_w", "rg_gate_b",
           "rg_lambda", "sc_conv_w", "even_w_out", "odd_norm_pre", "odd_norm_post", "odd_w_in", "gla_w_gate_lr",
           "gla_b_gate", "gla_norm_g", "odd_w_out")
BIG = ("even_w_in", "even_w_out", "odd_w_in", "odd_w_out")


def _halves(a):
    return a.reshape((2, a.shape[0] // 2) + a.shape[1:])


def kernel(x, even_norm_pre, even_norm_post, even_w_in, rg_conv_w, rg_conv_b, rg_gate_w, rg_gate_b, rg_lambda, sc_conv_w, even_w_out, odd_norm_pre, odd_norm_post, odd_w_in, gla_w_gate_lr, gla_b_gate, gla_norm_g, odd_w_out, loss_target, m_even_norm_pre, m_even_norm_post, m_even_w_in, m_rg_conv_w, m_rg_conv_b, m_rg_gate_w, m_rg_gate_b, m_rg_lambda, m_sc_conv_w, m_even_w_out, m_odd_norm_pre, m_odd_norm_post, m_odd_w_in, m_gla_w_gate_lr, m_gla_b_gate, m_gla_norm_g, m_odd_w_out, v_even_norm_pre, v_even_norm_post, v_even_w_in, v_rg_conv_w, v_rg_conv_b, v_rg_gate_w, v_rg_gate_b, v_rg_lambda, v_sc_conv_w, v_even_w_out, v_odd_norm_pre, v_odd_norm_post, v_odd_w_in, v_gla_w_gate_lr, v_gla_b_gate, v_gla_norm_g, v_odd_w_out):
    given = dict(locals())
    shard = {n: given[n][0] for n in WEIGHTS}
    m_in = {n: given["m_" + n][0] for n in WEIGHTS}
    v_in = {n: given["v_" + n][0] for n in WEIGHTS}
    mx, my, mc = lax.axis_index("x"), lax.axis_index("y"), lax.axis_index("c")
    core = jnp.reshape(mc, (1,)).astype(jnp.int32)
    chip = jnp.reshape(2 * mx + my, (1,)).astype(jnp.int32)

    small_shard = _pack(shard, SHARDED_SMALL, SHARDED_ROWS)
    big_own = [_halves(shard[n].astype(BF16)) for n in BIG]
    started_a = gather_start(big_own[:1], [small_shard], "gather_start_a")
    started_b = gather_start(big_own[1:], [], "gather_start_b")
    even_w_in_full, small_full = gather_wait(started_a, 1, started_b[-1], "gather_wait_a")
    (even_w_in_full,) = pass_to_sibling([even_w_in_full], "gather_pass_a")
    even_w_in_full = place_own(even_w_in_full, big_own[0], chip, "place_even_w_in")
    small_full = lax.dynamic_update_slice(small_full, small_shard[None], (chip[0], 0, 0))
    full = {n: shard[n] for n, _ in REPLICATED + LAST_REPLICATED}
    full.update({n: _merge_owners(a) for n, a in _unpack(small_full, SHARDED_SMALL, lead=(4,)).items()})
    full["even_w_in"] = even_w_in_full.reshape(4, D_MODEL, EVEN_IN // 4)

    def late_weights(after):
        lands = pass_to_sibling(list(gather_wait(started_b, 3, after, "gather_wait_b")), "gather_pass_b")
        lands = [place_own(a, b, chip, "place_" + n) for a, b, n in zip(lands, big_own[1:], BIG[1:])]
        odd_w_in = jnp.transpose(lands[1].reshape(4, D_MODEL, ODD_IN // 4), (1, 0, 2)).reshape(D_MODEL, ODD_IN)
        return _prepare_weights({"even_w_out": lands[0].reshape(2 * D_MODEL, D_MODEL), "odd_w_in": odd_w_in,
                                 "odd_w_out": lands[2].reshape(D_MODEL, D_MODEL)})

    pending = {}

    def slab(a):
        return a.reshape((4, 2, a.shape[1] // 2) + a.shape[2:])

    def begin(tag, slabs, dtypes):
        got = exchange_with_sibling(slabs, "grad_sibling_" + tag)
        sums = [add_sibling(a, b, core, dt, "grad_add_sibling_%s%d" % (tag, i))
                for i, (a, b, dt) in enumerate(zip(slabs, got, dtypes))]
        pending[tag] = exchange_with_chips_start(sums, "grad_chips_start_" + tag)
        return pending[tag][-1][0, 0]

    def finish(tag, after):
        sums, got = exchange_with_chips_wait(pending[tag], after, "grad_chips_wait_" + tag)
        return [add_chips(a, b, chip, "grad_add_chips_%s%d" % (tag, i)) for i, (a, b) in enumerate(zip(sums, got))]

    def reduce_first(g):
        return begin("a", [slab(jnp.transpose(g["odd_w_in"].reshape(D_MODEL, 4, ODD_IN // 4), (1, 0, 2))),
                           slab(g["odd_w_out"].reshape(4, D_MODEL // 4, D_MODEL)),
                           slab(g["even_w_out"].reshape(4, D_MODEL // 2, D_MODEL))], [BF16] * 3)

    def reduce_second(g):
        pending["totals_a"] = finish("a", g["even_w_in"])
        rep_rows = _pack(g, REPLICATED, REPLICATED_ROWS).reshape(4, 2, REP_PART, LANES)
        sh_rows = _pack({n: _split_owners(g[n]) for n, _ in SHARDED_SMALL}, SHARDED_SMALL, SHARDED_ROWS, lead=(4,))
        pack = jnp.concatenate([sh_rows.reshape(4, 2, HALF_SHARDED, LANES), rep_rows], axis=2)
        return begin("b", [slab(g["even_w_in"]), pack], [BF16, F32])

    loss, grad_x, g = local_step(x[0], loss_target[0], _prepare_weights(full), reduce_first, reduce_second,
                                 late_weights)
    odd_w_in_t, odd_w_out_t, even_w_out_t = pending["totals_a"]
    even_w_in_t, pack_t = finish("b", grad_x)
    totals = [even_w_in_t, even_w_out_t, odd_w_in_t, odd_w_out_t]
    last_part = jnp.concatenate([_pack(g, LAST_REPLICATED, LAST_ROWS), loss])
    from_core, rep_all, last_all = share_totals(totals, pack_t, last_part)
    me = 2 * chip[0] + core[0]
    mine, theirs = pack_t[:HALF_SHARDED], from_core[4][:HALF_SHARDED]
    sh_total = jnp.where(mc == 0, jnp.concatenate([mine, theirs]), jnp.concatenate([theirs, mine]))
    rep_all = lax.dynamic_update_slice(rep_all, pack_t[None, HALF_SHARDED:], (me, 0, 0))
    rep_total = rep_all.reshape(REPLICATED_ROWS, LANES)
    last_total = sum_parts(lax.dynamic_update_slice(last_all, last_part[None], (me, 0, 0)), "grad_sum_last")
    last_total, loss = last_total[:LAST_ROWS], last_total[LAST_ROWS, 0]
    grads = {}
    grads.update(_unpack(sh_total, SHARDED_SMALL))
    grads.update(_unpack(rep_total, REPLICATED))
    grads.update(_unpack(last_total, LAST_REPLICATED))

    delta, new_m, new_v = {}, {}, {}
    for i, n in enumerate(BIG):
        grads[n], delta[n], new_m[n], new_v[n] = adamw_halves(shard[n], totals[i], from_core[i], m_in[n], v_in[n],
                                                              core, "adamw_" + n)
    small = ((SHARDED_SMALL, SHARDED_ROWS), (REPLICATED, REPLICATED_ROWS), (LAST_REPLICATED, LAST_ROWS))
    packed = [jnp.concatenate([_pack(src, spec, rows) for spec, rows in small]) for src in (shard, m_in, v_in)]
    small_g = jnp.concatenate([sh_total, rep_total, last_total], axis=0)
    outs = adamw(packed[0], small_g, packed[1], packed[2], "adamw_small")
    for dst, packed_rows in zip((delta, new_m, new_v), outs):
        at = 0
        for spec, rows in small:
            dst.update(_unpack(packed_rows[at:at + rows], spec))
            at += rows
    result = [loss, grad_x[None]]
    for group in (grads, delta, new_m, new_v):
        result += [group[n].reshape(given[n].shape) for n in WEIGHTS]
    return tuple(result)
```

```python
import functools

import jax
import jax.numpy as jnp
from jax import lax
from jax.experimental import pallas as pl
from jax.experimental.pallas import tpu as pltpu

F32 = jnp.float32
BF16 = jnp.bfloat16
MESH = pl.DeviceIdType.MESH

D_MODEL = 1024
NORM_EPS = 1e-6
RG_HEADS = 8
RG_HEAD_DIM = 128
RG_C = 8.0
EVEN_IN = 6144
ODD_IN = 3104
ODD_IN_PAD = 3200
GLA_HEADS = 4
GLA_DK = 128
GLA_DV = 256
GLA_RANK = 16
GLA_NORMALIZER = 16.0
GLA_CHUNK = 128
LR_COL = 3072

ADAM_LR = 0.001
ADAM_B1 = 0.9
ADAM_B2 = 0.999
ADAM_EPS = 1e-08
ADAM_WD = 0.01
ADAM_STEP = 10

SUBLANES = 8
LANES = 128
VMEM_LIMIT = 56 * 2 ** 20

ROW_TILE = 512
SCAN_TILE = 256
GLA_BLOCK = 1024
MIX_TILE = 128


def _params(*sem):
    return pltpu.CompilerParams(dimension_semantics=sem, vmem_limit_bytes=VMEM_LIMIT)


def _full(shape):
    n = len(shape)
    return pl.BlockSpec(shape, lambda *_: (0,) * n)


def _sigmoid(x):
    return 0.5 + 0.5 * jnp.tanh(0.5 * x)


def _softplus(x):
    return jnp.maximum(x, 0.0) + jnp.log(1.0 + jnp.exp(-jnp.abs(x)))


def _dot(a, b):
    return jnp.dot(a, b, preferred_element_type=F32)


def _dot_nt(a, b):
    return lax.dot_general(a, b, (((1,), (1,)), ((), ())), preferred_element_type=F32)


def _dot_tn(a, b):
    return lax.dot_general(a, b, (((0,), (0,)), ((), ())), preferred_element_type=F32)


def _bdot(a, b, ca, cb):
    return lax.dot_general(a, b, (((ca,), (cb,)), ((0,), (0,))), preferred_element_type=F32)


def _halo_specs(rows, cols, col_block, n_row_tiles, tix):
    per = rows // SUBLANES
    last = n_row_tiles * per - 1

    def split(args):
        if len(args) == 2:
            return tix(args[1]), col_block + args[0]
        return tix(args[0]), col_block

    def prev(*args):
        t, c = split(args)
        return (jnp.maximum(t * per - 1, 0), c)

    def main(*args):
        return split(args)

    def nxt(*args):
        t, c = split(args)
        return (jnp.minimum((t + 1) * per, last), c)

    return [pl.BlockSpec((SUBLANES, cols), prev), pl.BlockSpec((rows, cols), main),
            pl.BlockSpec((SUBLANES, cols), nxt)]


def _extend(prev_ref, main_ref, next_ref, is_first, is_last):
    p = jnp.where(is_first, 0.0, prev_ref[...])
    n = jnp.where(is_last, 0.0, next_ref[...])
    return jnp.concatenate([p, main_ref[...], n], axis=0)


def _shifted(ext, offset, rows):
    if offset == 0:
        return ext[SUBLANES:SUBLANES + rows]
    n = ext.shape[0]
    return pltpu.roll(ext, (-offset) % n, 0)[SUBLANES:SUBLANES + rows]


def _conv(ext, w, left, rows):
    out = None
    for k in range(w.shape[0]):
        term = _shifted(ext, k - left, rows) * w[k:k + 1]
        out = term if out is None else out + term
    return out


def _conv_transpose(ext, w, left, rows):
    out = None
    for k in range(w.shape[0]):
        term = _shifted(ext, left - k, rows) * w[k:k + 1]
        out = term if out is None else out + term
    return out


def _colsum(x):
    return jnp.sum(x, axis=0, keepdims=True)


def _accumulate(ref, value, step):
    @pl.when(step == 0)
    def _():
        ref[...] = value

    @pl.when(step > 0)
    def _():
        ref[...] += value


PROJ_TILE_BYTES = 7 * 2 ** 20


def _proj_row_tile(rows, width):
    tm = min(ROW_TILE, rows)
    while tm * width * 4 > PROJ_TILE_BYTES and tm % (2 * SUBLANES) == 0:
        tm //= 2
    return tm


def norm_matmul(x, gain, w, name):
    rows, d = x.shape
    n_col_tiles, _, tn = w.shape
    tm = _proj_row_tile(rows, n_col_tiles * tn)

    def body(x_ref, g_ref, w_ref, proj_ref, h_ref):
        xv = x_ref[...]
        rstd = lax.rsqrt(jnp.mean(xv * xv, axis=-1, keepdims=True) + NORM_EPS)
        hv = (xv * rstd * g_ref[...]).astype(BF16)
        h_ref[...] = hv
        for j in range(n_col_tiles):
            proj_ref[:, j * tn:(j + 1) * tn] = _dot(hv, w_ref[j])

    row = lambda cols: pl.BlockSpec((tm, cols), lambda i: (i, 0))
    return pl.pallas_call(
        body, name=name,
        out_shape=(jax.ShapeDtypeStruct((rows, n_col_tiles * tn), F32), jax.ShapeDtypeStruct((rows, d), BF16)),
        grid=(rows // tm,),
        in_specs=[row(d), _full((1, d)), _full(w.shape)],
        out_specs=(row(n_col_tiles * tn), row(d)),
        compiler_params=_params("parallel"),
    )(x, gain, w)


def inproj_bwd(dproj, w, x, gain, dres, name):
    rows, d = x.shape
    n_col_tiles, _, tn = w.shape
    tm = _proj_row_tile(rows, n_col_tiles * tn)

    def body(dp_ref, w_ref, x_ref, g_ref, dres_ref, dx_ref, dg_ref):
        dh = None
        for j in range(n_col_tiles):
            part = _dot_nt(dp_ref[:, j * tn:(j + 1) * tn], w_ref[j])
            dh = part if dh is None else dh + part
        _inproj_finish(dh, x_ref, g_ref, dres_ref, dx_ref, dg_ref, pl.program_id(0))

    row = lambda cols: pl.BlockSpec((tm, cols), lambda i: (i, 0))
    return pl.pallas_call(
        body, name=name,
        out_shape=(jax.ShapeDtypeStruct((rows, d), F32), jax.ShapeDtypeStruct((1, d), F32)),
        grid=(rows // tm,),
        in_specs=[row(n_col_tiles * tn), _full(w.shape), row(d), _full((1, d)), row(d)],
        out_specs=(row(d), _full((1, d))),
        compiler_params=_params("arbitrary"),
    )(dproj, w, x, gain, dres)


def _inproj_finish(dh, x_ref, g_ref, dres_ref, dx_ref, dg_ref, step):
    xv = x_ref[...]
    rstd = lax.rsqrt(jnp.mean(xv * xv, axis=-1, keepdims=True) + NORM_EPS)
    xhat = xv * rstd
    dxn = dh * g_ref[...]
    dx_ref[...] = dres_ref[...] + rstd * (dxn - xhat * jnp.mean(dxn * xhat, axis=-1, keepdims=True))
    _accumulate(dg_ref, _colsum(dh * xhat), step)


def inproj_bwd_pieces(pieces, w, x, gain, dres, name):
    rows, d = x.shape
    tm = min(ROW_TILE, rows)
    n = len(pieces)
    widths = [p.shape[1] for p in pieces]
    starts = [sum(widths[:k]) for k in range(n)]
    assert sum(widths) == w.shape[2]

    def body(*refs):
        w_ref, x_ref, g_ref, dres_ref, dx_ref, dg_ref = refs[n:]
        dh = None
        for k in range(n):
            part = _dot_nt(refs[k][...], w_ref[0, :, starts[k]:starts[k] + widths[k]])
            dh = part if dh is None else dh + part
        _inproj_finish(dh, x_ref, g_ref, dres_ref, dx_ref, dg_ref, pl.program_id(0))

    row = lambda cols: pl.BlockSpec((tm, cols), lambda i: (i, 0))
    return pl.pallas_call(
        body, name=name,
        out_shape=(jax.ShapeDtypeStruct((rows, d), F32), jax.ShapeDtypeStruct((1, d), F32)),
        grid=(rows // tm,),
        in_specs=[row(wd) for wd in widths] + [_full(w.shape), row(d), _full((1, d)), row(d)],
        out_specs=(row(d), _full((1, d))),
        compiler_params=_params("arbitrary"),
    )(*pieces, w, x, gain, dres)


def matmul_dw_pieces(a, pieces, name):
    rows, m = a.shape
    tk = min(2 * ROW_TILE, rows)
    n = len(pieces)

    def body(*refs):
        a_ref, ins, outs = refs[0], refs[1:1 + n], refs[1 + n:]
        av = a_ref[...]
        for k in range(n):
            _accumulate(outs[k], _dot_tn(av, ins[k][...]), pl.program_id(0))

    return pl.pallas_call(
        body, name=name,
        out_shape=[jax.ShapeDtypeStruct((m, p.shape[1]), F32) for p in pieces],
        grid=(rows // tk,),
        in_specs=[pl.BlockSpec((tk, m), lambda k: (k, 0))]
        + [pl.BlockSpec((tk, p.shape[1]), lambda k: (k, 0)) for p in pieces],
        out_specs=[_full((m, p.shape[1])) for p in pieces],
        compiler_params=_params("arbitrary"),
    )(a, *pieces)


def matmul_dw(a, b, bn, name):
    rows, m = a.shape
    n = b.shape[1]
    tk = min(4 * ROW_TILE, rows)
    steps = rows // tk

    def body(a_ref, b_ref, o_ref):
        part = _dot_tn(a_ref[...], b_ref[...])

        @pl.when(pl.program_id(1) == 0)
        def _():
            o_ref[0] = part

        @pl.when(pl.program_id(1) > 0)
        def _():
            o_ref[0] += part

    return pl.pallas_call(
        body, name=name,
        out_shape=jax.ShapeDtypeStruct((n // bn, m, bn), F32),
        grid=(n // bn, steps),
        in_specs=[pl.BlockSpec((tk, m), lambda j, k: (k, 0)), pl.BlockSpec((tk, bn), lambda j, k: (k, j))],
        out_specs=pl.BlockSpec((1, m, bn), lambda j, k: (j, 0, 0)),
        compiler_params=_params("parallel", "arbitrary"),
    )(a, b)


def _scan(a, b, carry, reverse):
    n, c = a.shape
    blocks = n // SUBLANES
    a = a.reshape(blocks, SUBLANES, c)
    b = b.reshape(blocks, SUBLANES, c)
    pos = lax.broadcasted_iota(jnp.int32, (1, SUBLANES, c), 1)
    s = 1
    while s < SUBLANES:
        shift, valid = (SUBLANES - s, pos < SUBLANES - s) if reverse else (s, pos >= s)
        a_s, b_s = pltpu.roll(a, shift, 1), pltpu.roll(b, shift, 1)
        b = jnp.where(valid, a * b_s + b, b)
        a = jnp.where(valid, a * a_s, a)
        s *= 2
    out = [None] * blocks
    for k in (range(blocks - 1, -1, -1) if reverse else range(blocks)):
        h = a[k] * carry + b[k]
        out[k] = h
        carry = h[0:1] if reverse else h[SUBLANES - 1:SUBLANES]
    return jnp.concatenate(out, axis=0)


def _rg_gates(ua, gw_ref, gb, lam):
    ub = ua.astype(BF16)
    pre_r, pre_i = [], []
    for h in range(RG_HEADS):
        z = _dot(ub[:, h * RG_HEAD_DIM:(h + 1) * RG_HEAD_DIM], gw_ref[h])
        pre_r.append(z[:, :RG_HEAD_DIM])
        pre_i.append(z[:, RG_HEAD_DIM:])
    r = _sigmoid(jnp.concatenate(pre_r, axis=1) + gb[0:1])
    i = _sigmoid(jnp.concatenate(pre_i, axis=1) + gb[1:2])
    sp = _softplus(-lam)
    log_a = -RG_C * r * sp
    a = jnp.exp(log_a)
    mult = jnp.sqrt(1.0 - a * a)
    return r, i, sp, a, mult


def _rg_weight_specs():
    return [_full((4, D_MODEL)), _full((1, D_MODEL)), _full((RG_HEADS, RG_HEAD_DIM, 2 * RG_HEAD_DIM)),
            _full((2, D_MODEL)), _full((1, D_MODEL))]


def rglru_fwd(proj, conv_w, conv_b, gate_w, gate_b, lam, reverse, name):
    rows_total = proj.shape[0]
    rows = min(SCAN_TILE, rows_total)
    n_tiles = rows_total // rows
    tix = (lambda i: n_tiles - 1 - i) if reverse else (lambda i: i)

    def body(xp, xm, xn, cw_ref, cb_ref, gw_ref, gb_ref, lam_ref, h_ref, acts_ref, carry):
        i = pl.program_id(0)
        t = tix(i)
        ext = _extend(xp, xm, xn, t == 0, t == n_tiles - 1)
        ua = _conv(ext, cw_ref[...], 2, rows) + cb_ref[...]
        r, gi, _, a, mult = _rg_gates(ua, gw_ref, gb_ref[...], lam_ref[...])
        for k, saved in enumerate((ua, r, gi, a, mult)):
            acts_ref[k] = saved
        b = mult * (gi * ua)

        @pl.when(i == 0)
        def _():
            carry[...] = jnp.zeros_like(carry)

        h = _scan(a, b, carry[0:1], reverse)
        h_ref[...] = h
        edge = h[0:1] if reverse else h[rows - 1:rows]
        carry[...] = jnp.broadcast_to(edge, carry.shape)

    return pl.pallas_call(
        body, name=name,
        out_shape=(jax.ShapeDtypeStruct((rows_total, D_MODEL), F32),
                   jax.ShapeDtypeStruct((5, rows_total, D_MODEL), F32)),
        grid=(n_tiles,),
        in_specs=_halo_specs(rows, D_MODEL, 0, n_tiles, tix) + _rg_weight_specs(),
        out_specs=(pl.BlockSpec((rows, D_MODEL), lambda i: (tix(i), 0)),
                   pl.BlockSpec((5, rows, D_MODEL), lambda i: (0, tix(i), 0))),
        scratch_shapes=[pltpu.VMEM((SUBLANES, D_MODEL), F32)],
        compiler_params=_params("arbitrary"),
    )(proj, proj, proj, conv_w, conv_b, gate_w, gate_b, lam)


def rglru_bwd(proj, dycat, h_dir, acts, gate_w, lam, add_dua, reverse, name):
    rows_total = proj.shape[0]
    rows = min(SCAN_TILE, rows_total)
    n_tiles = rows_total // rows
    tix = (lambda i: i) if reverse else (lambda i: n_tiles - 1 - i)
    za_block = 1

    def body(acts_ref, za_ref, dya_ref, hp, hm, hn, gw_ref, lam_ref, *rest):
        other = rest[0][...] if add_dua is not None else 0.0
        dua_ref, dgw_ref, dgb_ref, dlam_ref, carry = rest[-5:]
        step = pl.program_id(0)
        t = tix(step)
        first, last = t == 0, t == n_tiles - 1
        ua, r, gi, a, mult = (acts_ref[k] for k in range(5))
        lam_v = lam_ref[...]
        sp = _softplus(-lam_v)
        za = za_ref[...]
        dh = dya_ref[...] * (za * _sigmoid(za))

        @pl.when(step == 0)
        def _():
            carry[...] = jnp.zeros_like(carry)

        old = carry[0:1]
        mu = _scan(a, a * dh, old, not reverse)
        row = lax.broadcasted_iota(jnp.int32, mu.shape, 0)
        if reverse:
            mu_next = jnp.where(row == 0, old, pltpu.roll(mu, 1, 0))
            carry[...] = jnp.broadcast_to(mu[rows - 1:rows], carry.shape)
            h_ext = _extend(hp, hm, hn, first, last)
            h_prev = _shifted(h_ext, 1, rows)
        else:
            mu_next = jnp.where(row == rows - 1, old, pltpu.roll(mu, rows - 1, 0))
            carry[...] = jnp.broadcast_to(mu[0:1], carry.shape)
            h_ext = _extend(hp, hm, hn, first, last)
            h_prev = _shifted(h_ext, -1, rows)
        db = dh + mu_next
        da = db * h_prev
        d_mult = db * (gi * ua)
        di = db * (mult * ua)
        dua = db * (mult * gi)
        dlog_a = da * a - d_mult * (a * a) / mult
        dr = dlog_a * (-RG_C * sp)
        dlam = _colsum(dlog_a * (-RG_C * r)) * (-_sigmoid(-lam_v))
        dpr = dr * (r * (1.0 - r))
        dpi = di * (gi * (1.0 - gi))
        dgb = jnp.concatenate([_colsum(dpr), _colsum(dpi)], axis=0)
        ub = ua.astype(BF16)
        dua_heads, dgw_heads = [], []
        for h in range(RG_HEADS):
            cols = slice(h * RG_HEAD_DIM, (h + 1) * RG_HEAD_DIM)
            dz = jnp.concatenate([dpr[:, cols], dpi[:, cols]], axis=1).astype(BF16)
            dgw_heads.append(_dot_tn(ub[:, cols], dz))
            dua_heads.append(_dot_nt(dz, gw_ref[h]))
        dua_ref[...] = dua + jnp.concatenate(dua_heads, axis=1) + other

        @pl.when(step == 0)
        def _():
            for h in range(RG_HEADS):
                dgw_ref[h] = dgw_heads[h]
            dgb_ref[...] = dgb
            dlam_ref[...] = dlam

        @pl.when(step > 0)
        def _():
            for h in range(RG_HEADS):
                dgw_ref[h] += dgw_heads[h]
            dgb_ref[...] += dgb
            dlam_ref[...] += dlam

    row_spec = lambda col: pl.BlockSpec((rows, D_MODEL), lambda i: (tix(i), col))
    return pl.pallas_call(
        body, name=name,
        out_shape=(jax.ShapeDtypeStruct((rows_total, D_MODEL), F32),
                   jax.ShapeDtypeStruct((RG_HEADS, RG_HEAD_DIM, 2 * RG_HEAD_DIM), F32),
                   jax.ShapeDtypeStruct((2, D_MODEL), F32), jax.ShapeDtypeStruct((1, D_MODEL), F32)),
        grid=(n_tiles,),
        in_specs=([pl.BlockSpec((5, rows, D_MODEL), lambda i: (0, tix(i), 0)), row_spec(za_block), row_spec(0)]
                  + _halo_specs(rows, D_MODEL, 0, n_tiles, tix)
                  + [_full((RG_HEADS, RG_HEAD_DIM, 2 * RG_HEAD_DIM)), _full((1, D_MODEL))]
                  + ([] if add_dua is None else [row_spec(0)])),
        out_specs=(row_spec(0), _full((RG_HEADS, RG_HEAD_DIM, 2 * RG_HEAD_DIM)), _full((2, D_MODEL)),
                   _full((1, D_MODEL))),
        scratch_shapes=[pltpu.VMEM((SUBLANES, D_MODEL), F32)],
        compiler_params=_params("arbitrary"),
    )(acts, proj, dycat, h_dir, h_dir, h_dir, gate_w, lam, *([] if add_dua is None else [add_dua]))


def even_mix_fwd(proj, h_f, h_b, sc_w, name):
    rows_total = proj.shape[0]
    rows = min(2 * MIX_TILE, rows_total)
    n_tiles = rows_total // rows
    cb = D_MODEL
    n_cb = 1
    ident = lambda i: i

    def body(za_ref, hf_ref, hb_ref, xbp, xbm, xbn, gcp, gcm, gcn, gb_ref, zb_ref, w_ref, y_ref):
        t = pl.program_id(1)
        first, last = t == 0, t == n_tiles - 1
        za = za_ref[...]
        y_ref[:, 0:cb] = ((hf_ref[...] + hb_ref[...]) * (za * _sigmoid(za))).astype(BF16)
        p_ext = _extend(xbp, xbm, xbn, first, last) * _extend(gcp, gcm, gcn, first, last)
        cv = _conv(p_ext, w_ref[...], 1, rows)
        zb = zb_ref[...]
        y_ref[:, cb:2 * cb] = (gb_ref[...] * cv * (zb * _sigmoid(zb))).astype(BF16)

    blk = lambda col: pl.BlockSpec((rows, cb), lambda c, i: (i, col * n_cb + c))
    own = pl.BlockSpec((rows, cb), lambda c, i: (i, c))
    return pl.pallas_call(
        body, name=name,
        out_shape=jax.ShapeDtypeStruct((rows_total, 2 * D_MODEL), BF16),
        grid=(n_cb, n_tiles),
        in_specs=([blk(1), own, own] + _halo_specs(rows, cb, 2 * n_cb, n_tiles, ident)
                  + _halo_specs(rows, cb, 4 * n_cb, n_tiles, ident)
                  + [blk(3), blk(5), pl.BlockSpec((3, cb), lambda c, i: (0, c))]),
        out_specs=pl.BlockSpec((rows, 2 * cb), lambda c, i: (i, 0)),
        compiler_params=_params("parallel", "arbitrary"),
    )(proj, h_f, h_b, proj, proj, proj, proj, proj, proj, proj, proj, sc_w)


def even_mix_bwd(proj, dycat, h_f, h_b, dua, conv_w, sc_w, name):
    rows_total = proj.shape[0]
    rows = min(MIX_TILE, rows_total)
    n_tiles = rows_total // rows
    cb = D_MODEL
    n_cb = 1
    ident = lambda i: i

    def body(xap, xam, xan, za_ref, xbp, xbm, xbn, gbp, gbm, gbn, gcp, gcm, gcn, zbp, zbm, zbn,
             dya_ref, dybp, dybm, dybn, hf_ref, hb_ref, dup, dum, dun, cw_ref, sw_ref,
             dp_ref, dcw_ref, dcb_ref, dsw_ref):
        def put(k, value):
            dp_ref[:, k * cb:(k + 1) * cb] = value.astype(BF16)

        t = pl.program_id(1)
        first, last = t == 0, t == n_tiles - 1
        za = za_ref[...]
        sa = _sigmoid(za)
        put(1, dya_ref[...] * (hf_ref[...] + hb_ref[...]) * (sa * (1.0 + za * (1.0 - sa))))
        dua_ext = _extend(dup, dum, dun, first, last)
        cw = cw_ref[...]
        put(0, _conv_transpose(dua_ext, cw, 2, rows))
        dua = dua_ext[SUBLANES:SUBLANES + rows]
        xa_ext = _extend(xap, xam, xan, first, last)
        dcw = jnp.concatenate([_colsum(dua * _shifted(xa_ext, k - 2, rows)) for k in range(4)], axis=0)
        dcb = _colsum(dua)
        xb_ext = _extend(xbp, xbm, xbn, first, last)
        gc_ext = _extend(gcp, gcm, gcn, first, last)
        p_ext = xb_ext * gc_ext
        zb_ext = _extend(zbp, zbm, zbn, first, last)
        sb_ext = _sigmoid(zb_ext)
        dyb_ext = _extend(dybp, dybm, dybn, first, last)
        gb_ext = _extend(gbp, gbm, gbn, first, last)
        dcv_ext = dyb_ext * gb_ext * (zb_ext * sb_ext)
        sw = sw_ref[...]
        p_at = [_shifted(p_ext, k - 1, rows) for k in range(3)]
        cv = (p_at[0] * sw[0:1] + p_at[1] * sw[1:2]) + p_at[2] * sw[2:3]
        mid = slice(SUBLANES, SUBLANES + rows)
        zb, sb, dyb, gb = zb_ext[mid], sb_ext[mid], dyb_ext[mid], gb_ext[mid]
        put(3, dyb * cv * (zb * sb))
        put(5, dyb * gb * cv * (sb * (1.0 + zb * (1.0 - sb))))
        dp = _conv_transpose(dcv_ext, sw, 1, rows)
        put(4, dp * xb_ext[mid])
        put(2, dp * gc_ext[mid])
        dcv = dcv_ext[mid]
        dsw = jnp.concatenate([_colsum(dcv * p_at[k]) for k in range(3)], axis=0)

        @pl.when(t == 0)
        def _():
            dcw_ref[...] = dcw
            dcb_ref[...] = dcb
            dsw_ref[...] = dsw

        @pl.when(t > 0)
        def _():
            dcw_ref[...] += dcw
            dcb_ref[...] += dcb
            dsw_ref[...] += dsw

    blk = lambda col: pl.BlockSpec((rows, cb), lambda c, i: (i, col * n_cb + c))
    halo = lambda col: _halo_specs(rows, cb, col * n_cb, n_tiles, ident)
    own = pl.BlockSpec((rows, cb), lambda c, i: (i, c))
    wspec = lambda k: pl.BlockSpec((k, cb), lambda c, i: (0, c))
    return pl.pallas_call(
        body, name=name,
        out_shape=(jax.ShapeDtypeStruct((rows_total, 6 * D_MODEL), BF16),
                   jax.ShapeDtypeStruct((4, D_MODEL), F32), jax.ShapeDtypeStruct((1, D_MODEL), F32),
                   jax.ShapeDtypeStruct((3, D_MODEL), F32)),
        grid=(n_cb, n_tiles),
        in_specs=(halo(0) + [blk(1)] + halo(2) + halo(3) + halo(4) + halo(5) + [blk(0)] + halo(1)
                  + [own, own] + halo(0) + [wspec(4), wspec(3)]),
        out_specs=(pl.BlockSpec((rows, 6 * cb), lambda c, i: (i, 0)), wspec(4), wspec(1), wspec(3)),
        compiler_params=_params("parallel", "arbitrary"),
    )(proj, proj, proj, proj, proj, proj, proj, proj, proj, proj, proj, proj, proj, proj, proj, proj,
      dycat, dycat, dycat, dycat, h_f, h_b, dua, dua, dua, conv_w, sc_w)


def even_out_fwd(ycat, w_out, gain, x, name):
    rows, d = x.shape
    k = ycat.shape[1]
    tm = min(ROW_TILE, rows)

    def body(yc_ref, w_ref, g_ref, x_ref, x1_ref, y_ref):
        y = _dot(yc_ref[...], w_ref[...])
        y_ref[...] = y
        rstd = lax.rsqrt(jnp.mean(y * y, axis=-1, keepdims=True) + NORM_EPS)
        x1_ref[...] = x_ref[...] + y * rstd * g_ref[...]

    row = lambda n: pl.BlockSpec((tm, n), lambda i: (i, 0))
    return pl.pallas_call(
        body, name=name,
        out_shape=(jax.ShapeDtypeStruct((rows, d), F32),) * 2,
        grid=(rows // tm,),
        in_specs=[row(k), _full((k, d)), _full((1, d)), row(d)],
        out_specs=(row(d), row(d)),
        compiler_params=_params("parallel"),
    )(ycat, w_out, gain, x)


def _rmsnorm_bwd(dout, y, gain):
    rstd = lax.rsqrt(jnp.mean(y * y, axis=-1, keepdims=True) + NORM_EPS)
    yhat = y * rstd
    dyn = dout * gain
    dy = rstd * (dyn - yhat * jnp.mean(dyn * yhat, axis=-1, keepdims=True))
    return dy, dout * yhat


def even_out_bwd(dx1, y, gain, w_out, name):
    rows, d = y.shape
    k = w_out.shape[0]
    tm = min(ROW_TILE, rows)

    def body(dx_ref, y_ref, g_ref, w_ref, dy_ref, dyc_ref, dg_ref):
        dy, dg_rows = _rmsnorm_bwd(dx_ref[...], y_ref[...], g_ref[...])
        dyb = dy.astype(BF16)
        dy_ref[...] = dyb
        dyc_ref[...] = _dot_nt(dyb, w_ref[...])
        _accumulate(dg_ref, _colsum(dg_rows), pl.program_id(0))

    row = lambda n: pl.BlockSpec((tm, n), lambda i: (i, 0))
    return pl.pallas_call(
        body, name=name,
        out_shape=(jax.ShapeDtypeStruct((rows, d), BF16), jax.ShapeDtypeStruct((rows, k), F32),
                   jax.ShapeDtypeStruct((1, d), F32)),
        grid=(rows // tm,),
        in_specs=[row(d), row(d), _full((1, d)), _full((k, d))],
        out_specs=(row(d), row(k), _full((1, d))),
        compiler_params=_params("arbitrary"),
    )(dx1, y, gain, w_out)


def _chunk_cumsum(g, reverse):
    n, c = g.shape
    g = g.reshape(n // GLA_CHUNK, GLA_CHUNK, c)
    pos = lax.broadcasted_iota(jnp.int32, (1, GLA_CHUNK, c), 1)
    s = 1
    while s < GLA_CHUNK:
        if reverse:
            g = g + jnp.where(pos < GLA_CHUNK - s, pltpu.roll(g, GLA_CHUNK - s, 1), 0.0)
        else:
            g = g + jnp.where(pos >= s, pltpu.roll(g, s, 1), 0.0)
        s *= 2
    return g.reshape(n, c)


def _gla_prepare(q_ref, k_ref, lr_ref, wg_ref, bg_ref, reverse, n_chunks):
    z = _dot(lr_ref[...].astype(BF16), wg_ref[0]) + bg_ref[0]
    g = -_softplus(-z) * (1.0 / GLA_NORMALIZER)
    bcum = _chunk_cumsum(g, reverse).reshape(n_chunks, GLA_CHUNK, GLA_DK)
    edge = 0 if reverse else GLA_CHUNK - 1
    btot = bcum[:, edge:edge + 1, :]
    e_pos = jnp.exp(bcum)
    e_neg = jnp.exp(-bcum)
    e_st = jnp.exp(btot - bcum)
    q3 = q_ref[...].reshape(n_chunks, GLA_CHUNK, GLA_DK)
    k3 = k_ref[...].reshape(n_chunks, GLA_CHUNK, GLA_DK)
    scale = GLA_DK ** -0.5
    q_in = q3 * scale * e_pos
    k_in = k3 * e_neg
    k_st = k3 * e_st
    dec = jnp.exp(btot)
    return z, q_in, k_in, k_st, dec, (scale * e_pos, e_neg, e_st)


def _gla_mask(reverse):
    i = lax.broadcasted_iota(jnp.int32, (GLA_CHUNK, GLA_CHUNK), 0)
    j = lax.broadcasted_iota(jnp.int32, (GLA_CHUNK, GLA_CHUNK), 1)
    return (j >= i) if reverse else (j <= i)


def _gla_specs(rows, n_blocks, reverse):
    tix = (lambda s: n_blocks - 1 - s) if reverse else (lambda s: s)
    d = 1 if reverse else 0
    lr_block = LR_COL // LANES
    specs = [pl.BlockSpec((rows, GLA_DK), lambda h, s: (tix(s), h)),
             pl.BlockSpec((rows, GLA_DK), lambda h, s: (tix(s), GLA_HEADS + h)),
             pl.BlockSpec((rows, GLA_DV), lambda h, s: (tix(s), GLA_HEADS + h)),
             pl.BlockSpec((rows, LANES), lambda h, s: (tix(s), lr_block)),
             pl.BlockSpec((1, LANES, GLA_DK), lambda h, s: (d, 0, h)),
             pl.BlockSpec((1, 1, GLA_DK), lambda h, s: (d, 0, h))]
    return specs, tix


def gla_fwd(proj, wg_pad, bg, reverse, name):
    rows_total = proj.shape[0]
    rows = min(GLA_BLOCK, rows_total)
    n_blocks = rows_total // rows
    n_chunks = rows // GLA_CHUNK
    specs, tix = _gla_specs(rows, n_blocks, reverse)

    def body(q_ref, k_ref, v_ref, lr_ref, wg_ref, bg_ref, o_ref, st_ref, state, kv_scr, dec_scr):
        _, q_in, k_in, k_st, dec, _ = _gla_prepare(q_ref, k_ref, lr_ref, wg_ref, bg_ref, reverse, n_chunks)
        vb = v_ref[...].reshape(n_chunks, GLA_CHUNK, GLA_DV).astype(BF16)
        qb = q_in.astype(BF16)
        p = jnp.where(_gla_mask(reverse), _bdot(qb, k_in.astype(BF16), 2, 2), 0.0)
        o = _bdot(p.astype(BF16), vb, 2, 1)
        kv_scr[...] = _bdot(vb, k_st.astype(BF16), 1, 1)
        dec_scr[...] = jnp.broadcast_to(dec, dec_scr.shape)

        @pl.when(pl.program_id(1) == 0)
        def _():
            state[...] = jnp.zeros_like(state)

        for c in range(n_chunks):
            cc = n_chunks - 1 - c if reverse else c
            st_ref[0, cc] = state[...]
            state[...] = state[...] * dec_scr[cc, 0:1] + kv_scr[cc]
        o = o + _bdot(qb, st_ref[0].astype(BF16), 2, 2)
        o_ref[...] = o.reshape(rows, GLA_DV)

    return pl.pallas_call(
        body, name=name,
        out_shape=(jax.ShapeDtypeStruct((rows_total, GLA_HEADS * GLA_DV), F32),
                   jax.ShapeDtypeStruct((GLA_HEADS, rows_total // GLA_CHUNK, GLA_DV, GLA_DK), F32)),
        grid=(GLA_HEADS, n_blocks),
        in_specs=specs,
        out_specs=(pl.BlockSpec((rows, GLA_DV), lambda h, s: (tix(s), h)),
                   pl.BlockSpec((1, n_chunks, GLA_DV, GLA_DK), lambda h, s: (h, tix(s), 0, 0))),
        scratch_shapes=[pltpu.VMEM((GLA_DV, GLA_DK), F32), pltpu.VMEM((n_chunks, GLA_DV, GLA_DK), F32),
                        pltpu.VMEM((n_chunks, SUBLANES, GLA_DK), F32)],
        compiler_params=_params("parallel", "arbitrary"),
    )(proj, proj, proj, proj, wg_pad, bg)


def gla_bwd(proj, wg_pad, bg, d_o, states, dqkv_in, reverse, name):
    rows_total = proj.shape[0]
    rows = min(GLA_BLOCK, rows_total)
    n_blocks = rows_total // rows
    n_chunks = rows // GLA_CHUNK
    specs, tix = _gla_specs(rows, n_blocks, not reverse)
    d = 1 if reverse else 0
    specs[4] = pl.BlockSpec((1, LANES, GLA_DK), lambda h, s: (d, 0, h))
    specs[5] = pl.BlockSpec((1, 1, GLA_DK), lambda h, s: (d, 0, h))
    add = dqkv_in is not None

    def body(*refs):
        q_ref, k_ref, v_ref, lr_ref, wg_ref, bg_ref, do_ref, st_ref = refs[:8]
        refs = refs[8:]
        if add:
            aq_ref, ak_ref, av_ref = refs[:3]
            refs = refs[3:]
        dq_ref, dk_ref, dv_ref, dz_ref, dstate, g_scr, dec_scr, dsn_scr = refs
        z, q_in, k_in, k_st, dec, (f_q, f_k, f_s) = _gla_prepare(q_ref, k_ref, lr_ref, wg_ref, bg_ref, reverse,
                                                                 n_chunks)
        mask = _gla_mask(reverse)
        vb = v_ref[...].reshape(n_chunks, GLA_CHUNK, GLA_DV).astype(BF16)
        dob = do_ref[...].reshape(n_chunks, GLA_CHUNK, GLA_DV).astype(BF16)
        qb, kb, ksb = q_in.astype(BF16), k_in.astype(BF16), k_st.astype(BF16)
        st = st_ref[0]
        stb = st.astype(BF16)
        pb = jnp.where(mask, _bdot(qb, kb, 2, 2), 0.0).astype(BF16)
        dpb = jnp.where(mask, _bdot(dob, vb, 2, 2), 0.0).astype(BF16)
        d_qin = _bdot(dpb, kb, 2, 1) + _bdot(dob, stb, 2, 1)
        d_kin = _bdot(dpb, qb, 1, 1)
        dv = _bdot(pb, dob, 1, 1)
        g_scr[...] = _bdot(dob, qb, 1, 1)
        dec_scr[...] = jnp.broadcast_to(dec, dec_scr.shape)

        @pl.when(pl.program_id(1) == 0)
        def _():
            dstate[...] = jnp.zeros_like(dstate)

        for c in range(n_chunks):
            cc = c if reverse else n_chunks - 1 - c
            dsn_scr[cc] = dstate[...]
            dstate[...] = dstate[...] * dec_scr[cc, 0:1] + g_scr[cc]
        dsn = dsn_scr[...]
        dsnb = dsn.astype(BF16)
        dv = dv + _bdot(ksb, dsnb, 2, 2)
        d_kst = _bdot(vb, dsnb, 2, 1)
        d_dec = jnp.sum(dsn * st, axis=1, keepdims=True)
        ks_term = d_kst * k_st
        d_btot = d_dec * dec + jnp.sum(ks_term, axis=1, keepdims=True)
        d_b = d_qin * q_in - d_kin * k_in - ks_term
        pos = lax.broadcasted_iota(jnp.int32, d_b.shape, 1)
        edge = 0 if reverse else GLA_CHUNK - 1
        d_b = d_b + jnp.where(pos == edge, d_btot, 0.0)
        dg = _chunk_cumsum(d_b.reshape(rows, GLA_DK), not reverse)
        dz_ref[...] = dg * (1.0 / GLA_NORMALIZER) * _sigmoid(-z)
        dq = (d_qin * f_q).reshape(rows, GLA_DK)
        dk = (d_kin * f_k + d_kst * f_s).reshape(rows, GLA_DK)
        dv = dv.reshape(rows, GLA_DV)
        if add:
            dq_ref[...] = (dq + aq_ref[...]).astype(BF16)
            dk_ref[...] = (dk + ak_ref[...]).astype(BF16)
            dv_ref[...] = (dv + av_ref[...]).astype(BF16)
        else:
            dq_ref[...] = dq
            dk_ref[...] = dk
            dv_ref[...] = dv

    qkv_specs = [pl.BlockSpec((rows, GLA_DK), lambda h, s: (tix(s), h)),
                 pl.BlockSpec((rows, GLA_DK), lambda h, s: (tix(s), h)),
                 pl.BlockSpec((rows, GLA_DV), lambda h, s: (tix(s), h))]
    in_specs = specs + [pl.BlockSpec((rows, GLA_DV), lambda h, s: (tix(s), h)),
                        pl.BlockSpec((1, n_chunks, GLA_DV, GLA_DK), lambda h, s: (h, tix(s), 0, 0))]
    args = [proj, proj, proj, proj, wg_pad, bg, d_o, states]
    out_dtype = F32
    if add:
        in_specs += qkv_specs
        args += list(dqkv_in)
        out_dtype = BF16
    return pl.pallas_call(
        body, name=name,
        out_shape=(jax.ShapeDtypeStruct((rows_total, GLA_HEADS * GLA_DK), out_dtype),
                   jax.ShapeDtypeStruct((rows_total, GLA_HEADS * GLA_DK), out_dtype),
                   jax.ShapeDtypeStruct((rows_total, GLA_HEADS * GLA_DV), out_dtype),
                   jax.ShapeDtypeStruct((rows_total, GLA_HEADS * GLA_DK), F32)),
        grid=(GLA_HEADS, n_blocks),
        in_specs=in_specs,
        out_specs=(pl.BlockSpec((rows, GLA_DK), lambda h, s: (tix(s), h)),
                   pl.BlockSpec((rows, GLA_DK), lambda h, s: (tix(s), h)),
                   pl.BlockSpec((rows, GLA_DV), lambda h, s: (tix(s), h)),
                   pl.BlockSpec((rows, GLA_DK), lambda h, s: (tix(s), h))),
        scratch_shapes=[pltpu.VMEM((GLA_DV, GLA_DK), F32), pltpu.VMEM((n_chunks, GLA_DV, GLA_DK), F32),
                        pltpu.VMEM((n_chunks, SUBLANES, GLA_DK), F32),
                        pltpu.VMEM((n_chunks, GLA_DV, GLA_DK), F32)],
        compiler_params=_params("parallel", "arbitrary"),
    )(*args)


def gla_gate_bwd(proj, dz_f, dz_b, wg_pad, name):
    rows_total = proj.shape[0]
    tm = min(ROW_TILE, rows_total)
    n_key = GLA_HEADS * GLA_DK

    def body(lr_ref, dzf_ref, dzb_ref, wg_ref, dlr_ref, dwg_ref, dbg_ref):
        step = pl.program_id(0)
        lr_t = jnp.transpose(lr_ref[...])
        dzf, dzb = dzf_ref[...], dzb_ref[...]
        dzf16, dzb16 = dzf.astype(BF16), dzb.astype(BF16)
        dlr_ref[...] = (_dot_nt(dzf16, wg_ref[0]) + _dot_nt(dzb16, wg_ref[1])).astype(BF16)
        dwf = _dot(lr_t[0:GLA_RANK].astype(BF16), dzf16)
        dwb = _dot(lr_t[GLA_RANK:2 * GLA_RANK].astype(BF16), dzb16)
        dbg = jnp.concatenate([_colsum(dzf), _colsum(dzb)], axis=0)

        @pl.when(step == 0)
        def _():
            dwg_ref[0] = dwf
            dwg_ref[1] = dwb
            dbg_ref[...] = dbg

        @pl.when(step > 0)
        def _():
            dwg_ref[0] += dwf
            dwg_ref[1] += dwb
            dbg_ref[...] += dbg

    return pl.pallas_call(
        body, name=name,
        out_shape=(jax.ShapeDtypeStruct((rows_total, LANES), BF16), jax.ShapeDtypeStruct((2, GLA_RANK, n_key), F32),
                   jax.ShapeDtypeStruct((2, n_key), F32)),
        grid=(rows_total // tm,),
        in_specs=[pl.BlockSpec((tm, LANES), lambda i: (i, LR_COL // LANES)),
                  pl.BlockSpec((tm, n_key), lambda i: (i, 0)), pl.BlockSpec((tm, n_key), lambda i: (i, 0)),
                  _full((2, LANES, n_key))],
        out_specs=(pl.BlockSpec((tm, LANES), lambda i: (i, 0)), _full((2, GLA_RANK, n_key)), _full((2, n_key))),
        compiler_params=_params("arbitrary"),
    )(proj, dz_f, dz_b, wg_pad)


def _head_norm(o, gain):
    outs, hats, rstds = [], [], []
    for h in range(GLA_HEADS):
        oh = o[:, h * GLA_DV:(h + 1) * GLA_DV]
        rstd = lax.rsqrt(jnp.mean(oh * oh, axis=-1, keepdims=True) + NORM_EPS)
        hat = oh * rstd
        outs.append(hat * gain)
        hats.append(hat)
        rstds.append(rstd)
    return outs, hats, rstds


def odd_out_fwd(o_f, o_b, proj, head_gain, w_out, gain, x1, target, name):
    rows, d = x1.shape
    tm = min(ROW_TILE, rows)
    r_block = (2 * GLA_HEADS * GLA_DK + GLA_HEADS * GLA_DV) // d

    def body(of_ref, ob_ref, r_ref, hg_ref, w_ref, g_ref, x1_ref, tgt_ref, y2_ref, dy_ref, dx2_ref, loss_ref,
             dg_ref):
        step = pl.program_id(0)
        on, _, _ = _head_norm(of_ref[...] + ob_ref[...], hg_ref[...])
        r = r_ref[...]
        y2 = (jnp.concatenate(on, axis=1) * (r * _sigmoid(r))).astype(BF16)
        y2_ref[...] = y2
        y = _dot(y2, w_ref[...])
        gain_v = g_ref[...]
        rstd = lax.rsqrt(jnp.mean(y * y, axis=-1, keepdims=True) + NORM_EPS)
        x2 = x1_ref[...] + y * rstd * gain_v
        diff = x2 - tgt_ref[...]
        loss = 0.5 * jnp.sum(jnp.mean(diff * diff, axis=-1, keepdims=True), axis=0, keepdims=True)
        dx2 = diff * (1.0 / d)
        dx2_ref[...] = dx2
        dy, dg_rows = _rmsnorm_bwd(dx2, y, gain_v)
        dy_ref[...] = dy.astype(BF16)
        _accumulate(loss_ref, jnp.broadcast_to(loss, loss_ref.shape), step)
        _accumulate(dg_ref, _colsum(dg_rows), step)

    row = lambda n, col=0: pl.BlockSpec((tm, n), lambda i: (i, col))
    return pl.pallas_call(
        body, name=name,
        out_shape=(jax.ShapeDtypeStruct((rows, d), BF16), jax.ShapeDtypeStruct((rows, d), BF16),
                   jax.ShapeDtypeStruct((rows, d), F32), jax.ShapeDtypeStruct((SUBLANES, LANES), F32),
                   jax.ShapeDtypeStruct((1, d), F32)),
        grid=(rows // tm,),
        in_specs=[row(d), row(d), row(d, r_block), _full((1, GLA_DV)), _full((d, d)), _full((1, d)), row(d), row(d)],
        out_specs=(row(d), row(d), row(d), _full((SUBLANES, LANES)), _full((1, d))),
        compiler_params=_params("arbitrary"),
    )(o_f, o_b, proj, head_gain, w_out, gain, x1, target)


def odd_out_bwd(dy, w_out, o_f, o_b, proj, head_gain, name):
    rows, d = dy.shape
    tm = min(ROW_TILE, rows)
    r_block = (2 * GLA_HEADS * GLA_DK + GLA_HEADS * GLA_DV) // d

    def body(dy_ref, w_ref, of_ref, ob_ref, r_ref, hg_ref, dr_ref, do_ref, dhg_ref):
        dy2 = _dot_nt(dy_ref[...], w_ref[...])
        hg = hg_ref[...]
        on, hats, rstds = _head_norm(of_ref[...] + ob_ref[...], hg)
        r = r_ref[...]
        sr = _sigmoid(r)
        dr_ref[...] = (dy2 * jnp.concatenate(on, axis=1) * (sr * (1.0 + r * (1.0 - sr)))).astype(BF16)
        d_on = dy2 * (r * sr)
        d_os, dhg = [], None
        for h in range(GLA_HEADS):
            dn = d_on[:, h * GLA_DV:(h + 1) * GLA_DV]
            part = _colsum(dn * hats[h])
            dhg = part if dhg is None else dhg + part
            dng = dn * hg
            d_os.append(rstds[h] * (dng - hats[h] * jnp.mean(dng * hats[h], axis=-1, keepdims=True)))
        do_ref[...] = jnp.concatenate(d_os, axis=1)
        _accumulate(dhg_ref, dhg, pl.program_id(0))

    row = lambda n, col=0: pl.BlockSpec((tm, n), lambda i: (i, col))
    return pl.pallas_call(
        body, name=name,
        out_shape=(jax.ShapeDtypeStruct((rows, d), BF16), jax.ShapeDtypeStruct((rows, d), F32),
                   jax.ShapeDtypeStruct((1, GLA_DV), F32)),
        grid=(rows // tm,),
        in_specs=[row(d), _full((d, d)), row(d), row(d), row(d, r_block), _full((1, GLA_DV))],
        out_specs=(row(d), row(d), _full((1, GLA_DV))),
        compiler_params=_params("arbitrary"),
    )(dy, w_out, o_f, o_b, proj, head_gain)


def local_step(x, target, w, reduce_first=None, reduce_second=None, late_weights=None):
    g = {}
    proj_e, h0 = norm_matmul(x, w["even_norm_pre"], w["even_w_in"], "even_in_proj")
    h_dir, acts = zip(*[rglru_fwd(proj_e, w["rg_conv_w"], w["rg_conv_b"], w["rg_gate_w"][d], w["rg_gate_b"][d],
                                  w["rg_lambda"][d], d == 1, "rglru_fwd_%d" % d) for d in range(2)])
    ycat = even_mix_fwd(proj_e, h_dir[0], h_dir[1], w["sc_conv_w"], "even_mix_fwd")
    if late_weights is not None:
        w = dict(w, **late_weights(ycat))
    x1, y_e = even_out_fwd(ycat, w["even_w_out"], w["even_norm_post"], x, "even_out_fwd")
    proj_o, h1 = norm_matmul(x1, w["odd_norm_pre"], w["odd_w_in"], "odd_in_proj")
    o_dir, st_dir = [], []
    for d in range(2):
        o, st = gla_fwd(proj_o, w["gla_wg_pad"], w["gla_b_gate"], d == 1, "gla_fwd_%d" % d)
        o_dir.append(o)
        st_dir.append(st)
    y2, dy_o, dx2, loss, g["odd_norm_post"] = odd_out_fwd(
        o_dir[0], o_dir[1], proj_o, w["gla_norm_g"], w["odd_w_out"], w["odd_norm_post"], x1, target, "odd_out_fwd")
    g["odd_w_out"] = matmul_dw(y2, dy_o, D_MODEL, "odd_w_out_grad")[0]
    dr, d_o, g["gla_norm_g"] = odd_out_bwd(dy_o, w["odd_w_out"], o_dir[0], o_dir[1], proj_o, w["gla_norm_g"],
                                           "odd_out_bwd")
    dq, dk, dv, dz_f = gla_bwd(proj_o, w["gla_wg_pad"], w["gla_b_gate"], d_o, st_dir[0], None, False, "gla_bwd_0")
    dq, dk, dv, dz_b = gla_bwd(proj_o, w["gla_wg_pad"], w["gla_b_gate"], d_o, st_dir[1], (dq, dk, dv), True,
                               "gla_bwd_1")
    dlr, g["gla_w_gate_lr"], g["gla_b_gate"] = gla_gate_bwd(proj_o, dz_f, dz_b, w["gla_wg_pad"], "gla_gate_bwd")
    dproj_o = [dq, dk, dv, dr, dlr]
    g["odd_w_in"] = jnp.concatenate(matmul_dw_pieces(h1, dproj_o, "odd_w_in_grad"), axis=1)[:, :ODD_IN]
    dx1, g["odd_norm_pre"] = inproj_bwd_pieces(dproj_o, w["odd_w_in"], x1, w["odd_norm_pre"], dx2, "odd_in_proj_bwd")
    dy_e, dycat, g["even_norm_post"] = even_out_bwd(dx1, y_e, w["even_norm_post"], w["even_w_out"], "even_out_bwd")
    g["even_w_out"] = matmul_dw(ycat, dy_e, D_MODEL, "even_w_out_grad")[0]
    lam = w["rg_lambda"] if reduce_first is None else w["rg_lambda"] + reduce_first(g)
    dua, dgw, dgb, dlam = None, [], [], []
    for d in range(2):
        a, b, c, e = rglru_bwd(proj_e, dycat, h_dir[d], acts[d], w["rg_gate_w"][d], lam[d], dua, d == 1,
                               "rglru_bwd_%d" % d)
        dua = a
        dgw.append(b)
        dgb.append(c)
        dlam.append(e)
    dproj_e, g["rg_conv_w"], g["rg_conv_b"], g["sc_conv_w"] = even_mix_bwd(
        proj_e, dycat, h_dir[0], h_dir[1], dua, w["rg_conv_w"], w["sc_conv_w"], "even_mix_bwd")
    dgw = jnp.stack(dgw).reshape(2, RG_HEADS, RG_HEAD_DIM, 2, RG_HEAD_DIM)
    g["rg_gate_w"] = jnp.transpose(dgw, (0, 3, 1, 2, 4))
    g["rg_gate_b"] = jnp.stack(dgb).reshape(2, 2, RG_HEADS, RG_HEAD_DIM)
    g["rg_lambda"] = jnp.concatenate(dlam, axis=0)
    g["even_w_in"] = matmul_dw(h0, dproj_e, EVEN_IN // 4, "even_w_in_grad")
    gain = w["even_norm_pre"] if reduce_second is None else w["even_norm_pre"] + reduce_second(g)
    grad_x, g["even_norm_pre"] = inproj_bwd(dproj_e, w["even_w_in"], x, gain, dx1, "even_in_proj_bwd")
    return loss, grad_x, g


def _prepare_weights(full):
    w = {}
    for name in ("even_norm_pre", "even_norm_post", "rg_conv_b", "odd_norm_pre", "odd_norm_post", "gla_norm_g"):
        if name in full:
            w[name] = full[name].reshape(1, -1)
    for name in ("rg_conv_w", "sc_conv_w"):
        if name in full:
            w[name] = full[name]
    for name in ("even_w_out", "odd_w_out"):
        if name in full:
            w[name] = full[name].astype(BF16)
    if "even_w_in" in full:
        w["even_w_in"] = full["even_w_in"].astype(BF16)
        if w["even_w_in"].ndim == 2:
            w["even_w_in"] = jnp.transpose(w["even_w_in"].reshape(D_MODEL, 4, EVEN_IN // 4), (1, 0, 2))
    if "rg_gate_w" in full:
        gw = jnp.transpose(full["rg_gate_w"].astype(BF16), (0, 2, 3, 1, 4))
        w["rg_gate_w"] = gw.reshape(2, RG_HEADS, RG_HEAD_DIM, 2 * RG_HEAD_DIM)
        w["rg_gate_b"] = full["rg_gate_b"].reshape(2, 2, D_MODEL)
        w["rg_lambda"] = full["rg_lambda"].reshape(2, 1, D_MODEL)
    if "odd_w_in" in full:
        w_in = jnp.pad(full["odd_w_in"].astype(BF16), ((0, 0), (0, ODD_IN_PAD - ODD_IN)))
        w["odd_w_in"] = w_in.reshape(1, D_MODEL, ODD_IN_PAD)
    if "gla_w_gate_lr" in full:
        wg = full["gla_w_gate_lr"].astype(BF16)
        w["gla_wg_pad"] = jnp.stack([jnp.pad(wg[d], ((d * GLA_RANK, LANES - (d + 1) * GLA_RANK), (0, 0)))
                                     for d in range(2)])
        w["gla_b_gate"] = full["gla_b_gate"].reshape(2, 1, GLA_HEADS * GLA_DK)
    return w


SHARDED_SMALL = (("rg_conv_w", (4, 256)), ("rg_lambda", (2, 256)), ("sc_conv_w", (3, 256)),
                 ("odd_norm_pre", (256,)), ("odd_norm_post", (256,)), ("gla_w_gate_lr", (2, 16, 128)),
                 ("gla_b_gate", (2, 128)), ("gla_norm_g", (64,)))
SHARDED_ROWS = 96
REPLICATED = (("rg_gate_w", (2, 2, 8, 128, 128)), ("even_norm_post", (1024,)), ("rg_conv_b", (1024,)),
              ("rg_gate_b", (2, 2, 8, 128)))
GATE_ROWS = 4096
LAST_REPLICATED = (("even_norm_pre", (1024,)),)
LAST_ROWS = 8
REPLICATED_ROWS = 4160
REP_PART = REPLICATED_ROWS // 8
HALF_SHARDED = SHARDED_ROWS // 2
PACK_HALF = HALF_SHARDED + REP_PART


def _seg_rows(shape):
    n = 1
    for s in shape:
        n *= s
    return -(-n // (SUBLANES * LANES)) * SUBLANES


def _pack(arrays, spec, total_rows, lead=()):
    parts = []
    for name, shape in spec:
        flat = arrays[name].reshape(lead + (-1,))
        pad = _seg_rows(shape) * LANES - flat.shape[-1]
        if pad:
            flat = jnp.pad(flat, [(0, 0)] * len(lead) + [(0, pad)])
        parts.append(flat.reshape(lead + (-1, LANES)))
    rows = jnp.concatenate(parts, axis=len(lead))
    pad = total_rows - rows.shape[len(lead)]
    return jnp.pad(rows, [(0, 0)] * len(lead) + [(0, pad), (0, 0)])


def _unpack(rows, spec, lead=()):
    out, at = {}, 0
    for name, shape in spec:
        n = 1
        for s in shape:
            n *= s
        k = _seg_rows(shape)
        seg = lax.slice_in_dim(rows, at, at + k, axis=len(lead)).reshape(lead + (-1,))
        out[name] = lax.slice_in_dim(seg, 0, n, axis=len(lead)).reshape(lead + shape)
        at += k
    return out


def _split_owners(arr):
    a = arr.reshape(arr.shape[:-1] + (4, arr.shape[-1] // 4))
    return jnp.moveaxis(a, -2, 0)


def _merge_owners(arr):
    a = jnp.moveaxis(arr, 0, -2)
    return a.reshape(a.shape[:-2] + (-1,))


HBM_SPEC = pl.BlockSpec(memory_space=pltpu.HBM)


def _position():
    x, y, c = lax.axis_index("x"), lax.axis_index("y"), lax.axis_index("c")
    chips = [(1 - x, y), (x, 1 - y), (1 - x, 1 - y)]
    return x, y, c, chips


def _remote(src, dst, send_sem, recv_sem, device):
    return pltpu.make_async_remote_copy(src_ref=src, dst_ref=dst, send_sem=send_sem, recv_sem=recv_sem,
                                        device_id=device, device_id_type=MESH)


SEM_SPEC = pl.BlockSpec(memory_space=pltpu.SEMAPHORE)
SIDE_EFFECT = pltpu.SideEffectType.DATAFLOW_SIDE_EFFECTING


def _gather_copies(ins, lands, n_h, send_sems, recv_sems):
    x, y, c, chips = _position()
    me = 2 * x + y
    copies = []
    for a in range(len(ins)):
        for k, chip in enumerate(chips):
            src = ins[a].at[c] if a < n_h else ins[a]
            dst = lands[a].at[me, c] if a < n_h else lands[a].at[me]
            copies.append(_remote(src, dst, send_sems.at[3 * a + k], recv_sems.at[3 * a + k], (chip[0], chip[1], c)))
    return copies


def gather_start(halved, whole, name):
    arrays = list(halved) + list(whole)
    n, n_h = len(arrays), len(halved)
    lands = [lax.empty((4,) + a.shape, a.dtype) for a in arrays]

    def body(*refs):
        ins, lz, send_sems, recv_sems, token = refs[:n], refs[n:2 * n], refs[2 * n], refs[2 * n + 1], refs[-1]
        for cp in _gather_copies(ins, lz, n_h, send_sems, recv_sems):
            cp.start()
        token[...] = jnp.zeros_like(token)

    operands = [pltpu.with_memory_space_constraint(a, pltpu.HBM) for a in arrays + lands]
    return pl.pallas_call(
        body, name=name,
        out_shape=(pltpu.SemaphoreType.DMA((3 * n,)), pltpu.SemaphoreType.DMA((3 * n,)))
        + tuple(pltpu.HBM(a.shape, a.dtype) for a in operands) + (jax.ShapeDtypeStruct((SUBLANES, LANES), F32),),
        in_specs=[HBM_SPEC] * (2 * n),
        out_specs=(SEM_SPEC, SEM_SPEC) + (HBM_SPEC,) * (2 * n) + (pl.BlockSpec(memory_space=pltpu.VMEM),),
        input_output_aliases={i: 2 + i for i in range(2 * n)},
        compiler_params=pltpu.CompilerParams(has_side_effects=SIDE_EFFECT),
    )(*operands)


def gather_wait(started, n_h, after, name):
    send_sems, recv_sems = started[0], started[1]
    operands = list(started[2:-1])
    n = len(operands) // 2

    def body(*refs):
        ins, lz, send_ref, recv_ref = refs[:n], refs[n:2 * n], refs[2 * n], refs[2 * n + 1]
        for cp in _gather_copies(ins, lz, n_h, send_ref, recv_ref):
            cp.wait_send()
            cp.wait_recv()

    outs = pl.pallas_call(
        body, name=name,
        out_shape=tuple(pltpu.HBM(a.shape, a.dtype) for a in operands),
        in_specs=[HBM_SPEC] * (2 * n) + [SEM_SPEC, SEM_SPEC, pl.BlockSpec(memory_space=pl.ANY)],
        out_specs=(HBM_SPEC,) * (2 * n),
        input_output_aliases={i: i for i in range(2 * n)},
        compiler_params=pltpu.CompilerParams(has_side_effects=SIDE_EFFECT),
    )(*operands, send_sems, recv_sems, after)
    return outs[n:]


def pass_to_sibling(fulls, name):
    n = len(fulls)

    def body(*refs):
        bufs = refs[n:2 * n]
        send_sems, recv_sems = refs[2 * n:]
        x, y, c, chips = _position()
        sibling = (x, y, 1 - c)
        copies = []
        for a in range(n):
            for k, chip in enumerate(chips):
                q = 2 * chip[0] + chip[1]
                cp = _remote(bufs[a].at[q, c], bufs[a].at[q, c], send_sems.at[3 * a + k], recv_sems.at[3 * a + k],
                             sibling)
                cp.start()
                copies.append(cp)
        for a in range(n):
            for k, chip in enumerate(chips):
                q = 2 * chip[0] + chip[1]
                passed = bufs[a].at[q, 1 - c]
                _remote(passed, passed, send_sems.at[3 * a + k], recv_sems.at[3 * a + k], sibling).wait_recv()
        for cp in copies:
            cp.wait_send()

    return pl.pallas_call(
        body, name=name,
        out_shape=[jax.ShapeDtypeStruct(a.shape, a.dtype) for a in fulls],
        in_specs=[HBM_SPEC] * n, out_specs=[HBM_SPEC] * n,
        input_output_aliases={i: i for i in range(n)},
        scratch_shapes=[pltpu.SemaphoreType.DMA((3 * n,)), pltpu.SemaphoreType.DMA((3 * n,))],
    )(*fulls)


def place_own(full, own, chip, name):
    _, _, r, cols = full.shape
    tr = _row_tile(r, cols)

    def body(p_ref, own_ref, full_ref, o_ref):
        o_ref[0] = own_ref[...]

    return pl.pallas_call(
        body, name=name,
        out_shape=jax.ShapeDtypeStruct(full.shape, full.dtype),
        grid_spec=pltpu.PrefetchScalarGridSpec(
            num_scalar_prefetch=1, grid=(2, r // tr),
            in_specs=[pl.BlockSpec((1, tr, cols), lambda h, i, p_ref: (h, i, 0)), pl.BlockSpec(memory_space=pl.ANY)],
            out_specs=pl.BlockSpec((1, 1, tr, cols), lambda h, i, p_ref: (p_ref[0], h, i, 0))),
        input_output_aliases={2: 0},
        compiler_params=_params("parallel", "parallel"),
    )(chip, own, full)


def exchange_with_sibling(arrays, name):
    n = len(arrays)

    def body(*refs):
        ins, outs = refs[:n], refs[n:2 * n]
        send_sems, recv_sems = refs[2 * n:]
        x, y, c, _ = _position()
        copies = []
        for a in range(n):
            cp = _remote(ins[a].at[:, 1 - c], outs[a], send_sems.at[a], recv_sems.at[a], (x, y, 1 - c))
            cp.start()
            copies.append(cp)
        for cp in copies:
            cp.wait()

    return pl.pallas_call(
        body, name=name,
        out_shape=[jax.ShapeDtypeStruct((a.shape[0],) + a.shape[2:], a.dtype) for a in arrays],
        in_specs=[HBM_SPEC] * n, out_specs=[HBM_SPEC] * n,
        scratch_shapes=[pltpu.SemaphoreType.DMA((n,)), pltpu.SemaphoreType.DMA((n,))],
    )(*arrays)


def _chip_copies(ins, lands, send_sems, recv_sems):
    x, y, c, chips = _position()
    copies = []
    for a in range(len(ins)):
        for k, chip in enumerate(chips):
            q = 2 * chip[0] + chip[1]
            copies.append(_remote(ins[a].at[q], lands[a].at[k], send_sems.at[3 * a + k], recv_sems.at[3 * a + k],
                                  (chip[0], chip[1], c)))
    return copies


def exchange_with_chips_start(arrays, name):
    n = len(arrays)
    lands = [lax.empty((3,) + a.shape[1:], a.dtype) for a in arrays]

    def body(*refs):
        ins, lz, send_sems, recv_sems, token = refs[:n], refs[n:2 * n], refs[2 * n], refs[2 * n + 1], refs[-1]
        for cp in _chip_copies(ins, lz, send_sems, recv_sems):
            cp.start()
        token[...] = jnp.zeros_like(token)

    operands = [pltpu.with_memory_space_constraint(a, pltpu.HBM) for a in list(arrays) + lands]
    return pl.pallas_call(
        body, name=name,
        out_shape=(pltpu.SemaphoreType.DMA((3 * n,)), pltpu.SemaphoreType.DMA((3 * n,)))
        + tuple(pltpu.HBM(a.shape, a.dtype) for a in operands) + (jax.ShapeDtypeStruct((SUBLANES, LANES), F32),),
        in_specs=[HBM_SPEC] * (2 * n),
        out_specs=(SEM_SPEC, SEM_SPEC) + (HBM_SPEC,) * (2 * n) + (pl.BlockSpec(memory_space=pltpu.VMEM),),
        input_output_aliases={i: 2 + i for i in range(2 * n)},
        compiler_params=pltpu.CompilerParams(has_side_effects=SIDE_EFFECT),
    )(*operands)


def exchange_with_chips_wait(started, after, name):
    send_sems, recv_sems = started[0], started[1]
    operands = list(started[2:-1])
    n = len(operands) // 2

    def body(*refs):
        ins, lz, send_ref, recv_ref = refs[:n], refs[n:2 * n], refs[2 * n], refs[2 * n + 1]
        for cp in _chip_copies(ins, lz, send_ref, recv_ref):
            cp.wait_send()
            cp.wait_recv()

    outs = pl.pallas_call(
        body, name=name,
        out_shape=tuple(pltpu.HBM(a.shape, a.dtype) for a in operands),
        in_specs=[HBM_SPEC] * (2 * n) + [SEM_SPEC, SEM_SPEC, pl.BlockSpec(memory_space=pl.ANY)],
        out_specs=(HBM_SPEC,) * (2 * n),
        input_output_aliases={i: i for i in range(2 * n)},
        compiler_params=pltpu.CompilerParams(has_side_effects=SIDE_EFFECT),
    )(*operands, send_sems, recv_sems, after)
    return outs[:n], outs[n:]


def share_totals(totals, pack_total, last_part):
    arrays = list(totals) + [pack_total]
    n = len(arrays)

    def body(*refs):
        ins, last, outs, rep, last_all = refs[:n], refs[n], refs[n + 1:2 * n + 1], refs[2 * n + 1], refs[2 * n + 2]
        send_sems, recv_sems, rep_send, rep_recv, last_send, last_recv = refs[2 * n + 3:]
        x, y, c, chips = _position()
        sibling = (x, y, 1 - c)
        me = 4 * x + 2 * y + c
        sends = []
        for a in range(n):
            cp = _remote(ins[a], outs[a], send_sems.at[a], recv_sems.at[a], sibling)
            cp.start()
            sends.append(cp)
        mine = ins[n - 1].at[pl.ds(HALF_SHARDED, REP_PART)]
        peers = [sibling]
        for chip in chips:
            peers += [(chip[0], chip[1], c), (chip[0], chip[1], 1 - c)]
        for j, peer in enumerate(peers):
            for src, dst, s_sem, r_sem in ((mine, rep, rep_send, rep_recv), (last, last_all, last_send, last_recv)):
                cp = _remote(src, dst.at[me], s_sem.at[j], r_sem.at[j], peer)
                cp.start()
                sends.append(cp)
        for a in range(n):
            _remote(outs[a], outs[a], send_sems.at[a], recv_sems.at[a], sibling).wait_recv()
        for j, peer in enumerate(peers):
            it = 4 * peer[0] + 2 * peer[1] + peer[2]
            _remote(rep.at[it], rep.at[it], rep_send.at[j], rep_recv.at[j], peer).wait_recv()
            _remote(last_all.at[it], last_all.at[it], last_send.at[j], last_recv.at[j], peer).wait_recv()
        for cp in sends:
            cp.wait_send()

    outs = pl.pallas_call(
        body, name="grad_share_totals",
        out_shape=[jax.ShapeDtypeStruct(a.shape, a.dtype) for a in arrays]
        + [jax.ShapeDtypeStruct((8, REP_PART, LANES), F32), jax.ShapeDtypeStruct((8,) + last_part.shape, F32)],
        in_specs=[HBM_SPEC] * (n + 1), out_specs=[HBM_SPEC] * (n + 2),
        scratch_shapes=[pltpu.SemaphoreType.DMA((n,)), pltpu.SemaphoreType.DMA((n,))]
        + [pltpu.SemaphoreType.DMA((7,))] * 4,
    )(*arrays, last_part)
    return outs[:n], outs[n], outs[n + 1]


def sum_parts(parts, name):
    def body(p_ref, o_ref):
        total = p_ref[0]
        for k in range(1, parts.shape[0]):
            total = total + p_ref[k]
        o_ref[...] = total

    return pl.pallas_call(body, name=name, out_shape=jax.ShapeDtypeStruct(parts.shape[1:], parts.dtype))(parts)


TILE_BYTES = 2 << 20


def _row_tile(rows, cols):
    best = None
    for t in range(SUBLANES, rows + 1, SUBLANES):
        if rows % t == 0 and t * cols * 4 <= TILE_BYTES:
            best = t
    return best if best is not None else rows


def add_sibling(mine, received, core, out_dtype, name):
    _, _, r, cols = mine.shape
    tr = _row_tile(r, cols)

    def body(c_ref, a_ref, b_ref, o_ref):
        o_ref[...] = (a_ref[0] + b_ref[...]).astype(out_dtype)

    return pl.pallas_call(
        body, name=name,
        out_shape=jax.ShapeDtypeStruct((4, r, cols), out_dtype),
        grid_spec=pltpu.PrefetchScalarGridSpec(
            num_scalar_prefetch=1, grid=(4, r // tr),
            in_specs=[pl.BlockSpec((1, 1, tr, cols), lambda o, i, c_ref: (o, c_ref[0], i, 0)),
                      pl.BlockSpec((1, tr, cols), lambda o, i, c_ref: (o, i, 0))],
            out_specs=pl.BlockSpec((1, tr, cols), lambda o, i, c_ref: (o, i, 0))),
        compiler_params=_params("parallel", "parallel"),
    )(core, mine, received)


def add_chips(own, received, chip, name):
    _, r, cols = own.shape
    tr = _row_tile(r, cols)

    def body(p_ref, a_ref, b0, b1, b2, o_ref):
        o_ref[...] = ((a_ref[0].astype(F32) + b0[0].astype(F32)) + b1[0].astype(F32)) + b2[0].astype(F32)

    rb = lambda k: pl.BlockSpec((1, tr, cols), lambda i, p_ref: (k, i, 0))
    return pl.pallas_call(
        body, name=name,
        out_shape=jax.ShapeDtypeStruct((r, cols), F32),
        grid_spec=pltpu.PrefetchScalarGridSpec(
            num_scalar_prefetch=1, grid=(r // tr,),
            in_specs=[pl.BlockSpec((1, tr, cols), lambda i, p_ref: (p_ref[0], i, 0)), rb(0), rb(1), rb(2)],
            out_specs=pl.BlockSpec((tr, cols), lambda i, p_ref: (i, 0))),
        compiler_params=_params("parallel"),
    )(chip, own, received, received, received)


def _adamw_update(gv, w_ref, m_ref, v_ref, d_ref, nm_ref, nv_ref):
    nm = ADAM_B1 * m_ref[...] + (1.0 - ADAM_B1) * gv
    nv = ADAM_B2 * v_ref[...] + (1.0 - ADAM_B2) * (gv * gv)
    nm_ref[...] = nm
    nv_ref[...] = nv
    m_hat = nm / (1.0 - ADAM_B1 ** ADAM_STEP)
    v_hat = nv / (1.0 - ADAM_B2 ** ADAM_STEP)
    d_ref[...] = -ADAM_LR * (m_hat / (jnp.sqrt(v_hat) + ADAM_EPS) + ADAM_WD * w_ref[...])


def adamw_halves(w, own, received, m, v, core, name):
    rows, cols = w.shape
    r = rows // 2
    tr = _row_tile(r, cols)
    nr = r // tr

    def body(c_ref, w_ref, own_ref, rec_ref, m_ref, v_ref, g_ref, d_ref, nm_ref, nv_ref):
        gv = jnp.where(pl.program_id(0) == c_ref[0], own_ref[...], rec_ref[...])
        g_ref[...] = gv
        _adamw_update(gv, w_ref, m_ref, v_ref, d_ref, nm_ref, nv_ref)

    whole = pl.BlockSpec((tr, cols), lambda h, i, c_ref: (h * nr + i, 0))
    half = pl.BlockSpec((tr, cols), lambda h, i, c_ref: (i, 0))
    return pl.pallas_call(
        body, name=name,
        out_shape=(jax.ShapeDtypeStruct((rows, cols), F32),) * 4,
        grid_spec=pltpu.PrefetchScalarGridSpec(
            num_scalar_prefetch=1, grid=(2, nr),
            in_specs=[whole, half, half, whole, whole], out_specs=(whole,) * 4),
        compiler_params=_params("parallel", "parallel"),
    )(core, w, own, received, m, v)


def adamw(w, g, m, v, name):
    r, cols = w.shape
    tr = _row_tile(r, cols)

    def body(w_ref, g_ref, m_ref, v_ref, g_out, d_ref, nm_ref, nv_ref):
        gv = g_ref[...]
        g_out[...] = gv
        _adamw_update(gv, w_ref, m_ref, v_ref, d_ref, nm_ref, nv_ref)

    blk = pl.BlockSpec((tr, cols), lambda i: (i, 0))
    return pl.pallas_call(
        body, name=name,
        out_shape=(jax.ShapeDtypeStruct((r, cols), F32),) * 4,
        grid=(r // tr,),
        in_specs=[blk] * 4, out_specs=(blk,) * 4,
        compiler_params=_params("parallel"),
    )(w, g, m, v)


WEIGHTS = ("even_norm_pre", "even_norm_post", "even_w_in", "rg_conv_w", "rg_conv_b", "rg_gate_w", "rg_gate_b",
           "rg_lambda", "sc_conv_w", "even_w_out", "odd_norm_pre", "odd_norm_post", "odd_w_in", "gla_w_gate_lr",
           "gla_b_gate", "gla_norm_g", "odd_w_out")
BIG = ("even_w_in", "even_w_out", "odd_w_in", "odd_w_out")


def _halves(a):
    return a.reshape((2, a.shape[0] // 2) + a.shape[1:])


def kernel(x, even_norm_pre, even_norm_post, even_w_in, rg_conv_w, rg_conv_b, rg_gate_w, rg_gate_b, rg_lambda, sc_conv_w, even_w_out, odd_norm_pre, odd_norm_post, odd_w_in, gla_w_gate_lr, gla_b_gate, gla_norm_g, odd_w_out, loss_target, m_even_norm_pre, m_even_norm_post, m_even_w_in, m_rg_conv_w, m_rg_conv_b, m_rg_gate_w, m_rg_gate_b, m_rg_lambda, m_sc_conv_w, m_even_w_out, m_odd_norm_pre, m_odd_norm_post, m_odd_w_in, m_gla_w_gate_lr, m_gla_b_gate, m_gla_norm_g, m_odd_w_out, v_even_norm_pre, v_even_norm_post, v_even_w_in, v_rg_conv_w, v_rg_conv_b, v_rg_gate_w, v_rg_gate_b, v_rg_lambda, v_sc_conv_w, v_even_w_out, v_odd_norm_pre, v_odd_norm_post, v_odd_w_in, v_gla_w_gate_lr, v_gla_b_gate, v_gla_norm_g, v_odd_w_out):
    given = dict(locals())
    shard = {n: given[n][0] for n in WEIGHTS}
    m_in = {n: given["m_" + n][0] for n in WEIGHTS}
    v_in = {n: given["v_" + n][0] for n in WEIGHTS}
    mx, my, mc = lax.axis_index("x"), lax.axis_index("y"), lax.axis_index("c")
    core = jnp.reshape(mc, (1,)).astype(jnp.int32)
    chip = jnp.reshape(2 * mx + my, (1,)).astype(jnp.int32)

    small_shard = _pack(shard, SHARDED_SMALL, SHARDED_ROWS)
    big_own = [_halves(shard[n].astype(BF16)) for n in BIG]
    started_a = gather_start(big_own[:1], [small_shard], "gather_start_a")
    started_b = gather_start(big_own[1:], [], "gather_start_b")
    even_w_in_full, small_full = gather_wait(started_a, 1, started_b[-1], "gather_wait_a")
    (even_w_in_full,) = pass_to_sibling([even_w_in_full], "gather_pass_a")
    even_w_in_full = place_own(even_w_in_full, big_own[0], chip, "place_even_w_in")
    small_full = lax.dynamic_update_slice(small_full, small_shard[None], (chip[0], 0, 0))
    full = {n: shard[n] for n, _ in REPLICATED + LAST_REPLICATED}
    full.update({n: _merge_owners(a) for n, a in _unpack(small_full, SHARDED_SMALL, lead=(4,)).items()})
    full["even_w_in"] = even_w_in_full.reshape(4, D_MODEL, EVEN_IN // 4)

    def late_weights(after):
        lands = pass_to_sibling(list(gather_wait(started_b, 3, after, "gather_wait_b")), "gather_pass_b")
        lands = [place_own(a, b, chip, "place_" + n) for a, b, n in zip(lands, big_own[1:], BIG[1:])]
        odd_w_in = jnp.transpose(lands[1].reshape(4, D_MODEL, ODD_IN // 4), (1, 0, 2)).reshape(D_MODEL, ODD_IN)
        return _prepare_weights({"even_w_out": lands[0].reshape(2 * D_MODEL, D_MODEL), "odd_w_in": odd_w_in,
                                 "odd_w_out": lands[2].reshape(D_MODEL, D_MODEL)})

    pending = {}

    def slab(a):
        return a.reshape((4, 2, a.shape[1] // 2) + a.shape[2:])

    def begin(tag, slabs, dtypes):
        got = exchange_with_sibling(slabs, "grad_sibling_" + tag)
        sums = [add_sibling(a, b, core, dt, "grad_add_sibling_%s%d" % (tag, i))
                for i, (a, b, dt) in enumerate(zip(slabs, got, dtypes))]
        pending[tag] = exchange_with_chips_start(sums, "grad_chips_start_" + tag)
        return pending[tag][-1][0, 0]

    def finish(tag, after):
        sums, got = exchange_with_chips_wait(pending[tag], after, "grad_chips_wait_" + tag)
        return [add_chips(a, b, chip, "grad_add_chips_%s%d" % (tag, i)) for i, (a, b) in enumerate(zip(sums, got))]

    def reduce_first(g):
        return begin("a", [slab(jnp.transpose(g["odd_w_in"].reshape(D_MODEL, 4, ODD_IN // 4), (1, 0, 2))),
                           slab(g["odd_w_out"].reshape(4, D_MODEL // 4, D_MODEL)),
                           slab(g["even_w_out"].reshape(4, D_MODEL // 2, D_MODEL))], [BF16] * 3)

    def reduce_second(g):
        pending["totals_a"] = finish("a", g["even_w_in"])
        rep_rows = _pack(g, REPLICATED, REPLICATED_ROWS).reshape(4, 2, REP_PART, LANES)
        sh_rows = _pack({n: _split_owners(g[n]) for n, _ in SHARDED_SMALL}, SHARDED_SMALL, SHARDED_ROWS, lead=(4,))
        pack = jnp.concatenate([sh_rows.reshape(4, 2, HALF_SHARDED, LANES), rep_rows], axis=2)
        return begin("b", [slab(g["even_w_in"]), pack], [BF16, F32])

    loss, grad_x, g = local_step(x[0], loss_target[0], _prepare_weights(full), reduce_first, reduce_second,
                                 late_weights)
    odd_w_in_t, odd_w_out_t, even_w_out_t = pending["totals_a"]
    even_w_in_t, pack_t = finish("b", grad_x)
    totals = [even_w_in_t, even_w_out_t, odd_w_in_t, odd_w_out_t]
    last_part = jnp.concatenate([_pack(g, LAST_REPLICATED, LAST_ROWS), loss])
    from_core, rep_all, last_all = share_totals(totals, pack_t, last_part)
    me = 2 * chip[0] + core[0]
    mine, theirs = pack_t[:HALF_SHARDED], from_core[4][:HALF_SHARDED]
    sh_total = jnp.where(mc == 0, jnp.concatenate([mine, theirs]), jnp.concatenate([theirs, mine]))
    rep_all = lax.dynamic_update_slice(rep_all, pack_t[None, HALF_SHARDED:], (me, 0, 0))
    rep_total = rep_all.reshape(REPLICATED_ROWS, LANES)
    last_total = sum_parts(lax.dynamic_update_slice(last_all, last_part[None], (me, 0, 0)), "grad_sum_last")
    last_total, loss = last_total[:LAST_ROWS], last_total[LAST_ROWS, 0]
    grads = {}

    delta, new_m, new_v = {}, {}, {}
    for i, n in enumerate(BIG):
        grads[n], delta[n], new_m[n], new_v[n] = adamw_halves(shard[n], totals[i], from_core[i], m_in[n], v_in[n],
                                                              core, "adamw_" + n)
    gate = [src["rg_gate_w"].reshape(GATE_ROWS, LANES) for src in (shard, m_in, v_in)]
    grads["rg_gate_w"], delta["rg_gate_w"], new_m["rg_gate_w"], new_v["rg_gate_w"] = adamw(
        gate[0], rep_total, gate[1], gate[2], "adamw_rg_gate_w")
    rest = REPLICATED[1:]
    rest_rows = sum(_seg_rows(shape) for _, shape in rest)
    small = ((SHARDED_SMALL, SHARDED_ROWS), (rest, rest_rows), (LAST_REPLICATED, LAST_ROWS))
    packed = [jnp.concatenate([_pack(src, spec, rows) for spec, rows in small]) for src in (shard, m_in, v_in)]
    small_g = jnp.concatenate([sh_total, rep_total[GATE_ROWS:GATE_ROWS + rest_rows], last_total], axis=0)
    outs = adamw(packed[0], small_g, packed[1], packed[2], "adamw_small")
    for dst, packed_rows in zip((grads, delta, new_m, new_v), outs):
        at = 0
        for spec, rows in small:
            dst.update(_unpack(packed_rows[at:at + rows], spec))
            at += rows
    result = [loss, grad_x[None]]
    for group in (grads, delta, new_m, new_v):
        result += [group[n].reshape(given[n].shape) for n in WEIGHTS]
    return tuple(result)
```

```python
import functools

import jax
import jax.numpy as jnp
from jax import lax
from jax.experimental import pallas as pl
from jax.experimental.pallas import tpu as pltpu

F32 = jnp.float32
BF16 = jnp.bfloat16
MESH = pl.DeviceIdType.MESH

D_MODEL = 1024
NORM_EPS = 1e-6
RG_HEADS = 8
RG_HEAD_DIM = 128
RG_C = 8.0
EVEN_IN = 6144
ODD_IN = 3104
ODD_IN_PAD = 3200
GLA_HEADS = 4
GLA_DK = 128
GLA_DV = 256
GLA_RANK = 16
GLA_NORMALIZER = 16.0
GLA_CHUNK = 128
LR_COL = 3072

ADAM_LR = 0.001
ADAM_B1 = 0.9
ADAM_B2 = 0.999
ADAM_EPS = 1e-08
ADAM_WD = 0.01
ADAM_STEP = 10

SUBLANES = 8
LANES = 128
VMEM_LIMIT = 56 * 2 ** 20

ROW_TILE = 512
SCAN_TILE = 256
GLA_BLOCK = 1024
MIX_TILE = 128


def _params(*sem):
    return pltpu.CompilerParams(dimension_semantics=sem, vmem_limit_bytes=VMEM_LIMIT)


def _full(shape):
    n = len(shape)
    return pl.BlockSpec(shape, lambda *_: (0,) * n)


def _sigmoid(x):
    return 0.5 + 0.5 * jnp.tanh(0.5 * x)


def _softplus(x):
    return jnp.maximum(x, 0.0) + jnp.log(1.0 + jnp.exp(-jnp.abs(x)))


def _dot(a, b):
    return jnp.dot(a, b, preferred_element_type=F32)


def _dot_nt(a, b):
    return lax.dot_general(a, b, (((1,), (1,)), ((), ())), preferred_element_type=F32)


def _dot_tn(a, b):
    return lax.dot_general(a, b, (((0,), (0,)), ((), ())), preferred_element_type=F32)


def _bdot(a, b, ca, cb):
    return lax.dot_general(a, b, (((ca,), (cb,)), ((0,), (0,))), preferred_element_type=F32)


def _halo_specs(rows, cols, col_block, n_row_tiles, tix):
    per = rows // SUBLANES
    last = n_row_tiles * per - 1

    def split(args):
        if len(args) == 2:
            return tix(args[1]), col_block + args[0]
        return tix(args[0]), col_block

    def prev(*args):
        t, c = split(args)
        return (jnp.maximum(t * per - 1, 0), c)

    def main(*args):
        return split(args)

    def nxt(*args):
        t, c = split(args)
        return (jnp.minimum((t + 1) * per, last), c)

    return [pl.BlockSpec((SUBLANES, cols), prev), pl.BlockSpec((rows, cols), main),
            pl.BlockSpec((SUBLANES, cols), nxt)]


def _extend(prev_ref, main_ref, next_ref, is_first, is_last):
    p = jnp.where(is_first, 0.0, prev_ref[...])
    n = jnp.where(is_last, 0.0, next_ref[...])
    return jnp.concatenate([p, main_ref[...], n], axis=0)


def _shifted(ext, offset, rows):
    if offset == 0:
        return ext[SUBLANES:SUBLANES + rows]
    n = ext.shape[0]
    return pltpu.roll(ext, (-offset) % n, 0)[SUBLANES:SUBLANES + rows]


def _conv(ext, w, left, rows):
    out = None
    for k in range(w.shape[0]):
        term = _shifted(ext, k - left, rows) * w[k:k + 1]
        out = term if out is None else out + term
    return out


def _conv_transpose(ext, w, left, rows):
    out = None
    for k in range(w.shape[0]):
        term = _shifted(ext, left - k, rows) * w[k:k + 1]
        out = term if out is None else out + term
    return out


def _colsum(x):
    return jnp.sum(x, axis=0, keepdims=True)


def _accumulate(ref, value, step):
    @pl.when(step == 0)
    def _():
        ref[...] = value

    @pl.when(step > 0)
    def _():
        ref[...] += value


PROJ_TILE_BYTES = 7 * 2 ** 20


def _proj_row_tile(rows, width):
    tm = min(ROW_TILE, rows)
    while tm * width * 4 > PROJ_TILE_BYTES and tm % (2 * SUBLANES) == 0:
        tm //= 2
    return tm


def norm_matmul(x, gain, w, name):
    rows, d = x.shape
    n_col_tiles, _, tn = w.shape
    tm = _proj_row_tile(rows, n_col_tiles * tn)

    def body(x_ref, g_ref, w_ref, proj_ref, h_ref):
        xv = x_ref[...]
        rstd = lax.rsqrt(jnp.mean(xv * xv, axis=-1, keepdims=True) + NORM_EPS)
        hv = (xv * rstd * g_ref[...]).astype(BF16)
        h_ref[...] = hv
        for j in range(n_col_tiles):
            proj_ref[:, j * tn:(j + 1) * tn] = _dot(hv, w_ref[j])

    row = lambda cols: pl.BlockSpec((tm, cols), lambda i: (i, 0))
    return pl.pallas_call(
        body, name=name,
        out_shape=(jax.ShapeDtypeStruct((rows, n_col_tiles * tn), F32), jax.ShapeDtypeStruct((rows, d), BF16)),
        grid=(rows // tm,),
        in_specs=[row(d), _full((1, d)), _full(w.shape)],
        out_specs=(row(n_col_tiles * tn), row(d)),
        compiler_params=_params("parallel"),
    )(x, gain, w)


def inproj_bwd(dproj, w, x, gain, dres, name):
    rows, d = x.shape
    n_col_tiles, _, tn = w.shape
    tm = _proj_row_tile(rows, n_col_tiles * tn)

    def body(dp_ref, w_ref, x_ref, g_ref, dres_ref, dx_ref, dg_ref):
        dh = None
        for j in range(n_col_tiles):
            part = _dot_nt(dp_ref[:, j * tn:(j + 1) * tn], w_ref[j])
            dh = part if dh is None else dh + part
        _inproj_finish(dh, x_ref, g_ref, dres_ref, dx_ref, dg_ref, pl.program_id(0))

    row = lambda cols: pl.BlockSpec((tm, cols), lambda i: (i, 0))
    return pl.pallas_call(
        body, name=name,
        out_shape=(jax.ShapeDtypeStruct((rows, d), F32), jax.ShapeDtypeStruct((1, d), F32)),
        grid=(rows // tm,),
        in_specs=[row(n_col_tiles * tn), _full(w.shape), row(d), _full((1, d)), row(d)],
        out_specs=(row(d), _full((1, d))),
        compiler_params=_params("arbitrary"),
    )(dproj, w, x, gain, dres)


def _inproj_finish(dh, x_ref, g_ref, dres_ref, dx_ref, dg_ref, step):
    xv = x_ref[...]
    rstd = lax.rsqrt(jnp.mean(xv * xv, axis=-1, keepdims=True) + NORM_EPS)
    xhat = xv * rstd
    dxn = dh * g_ref[...]
    dx_ref[...] = dres_ref[...] + rstd * (dxn - xhat * jnp.mean(dxn * xhat, axis=-1, keepdims=True))
    _accumulate(dg_ref, _colsum(dh * xhat), step)


def inproj_bwd_pieces(pieces, w, x, gain, dres, name):
    rows, d = x.shape
    tm = min(ROW_TILE, rows)
    n = len(pieces)
    widths = [p.shape[1] for p in pieces]
    starts = [sum(widths[:k]) for k in range(n)]
    assert sum(widths) == w.shape[2]

    def body(*refs):
        w_ref, x_ref, g_ref, dres_ref, dx_ref, dg_ref = refs[n:]
        dh = None
        for k in range(n):
            part = _dot_nt(refs[k][...], w_ref[0, :, starts[k]:starts[k] + widths[k]])
            dh = part if dh is None else dh + part
        _inproj_finish(dh, x_ref, g_ref, dres_ref, dx_ref, dg_ref, pl.program_id(0))

    row = lambda cols: pl.BlockSpec((tm, cols), lambda i: (i, 0))
    return pl.pallas_call(
        body, name=name,
        out_shape=(jax.ShapeDtypeStruct((rows, d), F32), jax.ShapeDtypeStruct((1, d), F32)),
        grid=(rows // tm,),
        in_specs=[row(wd) for wd in widths] + [_full(w.shape), row(d), _full((1, d)), row(d)],
        out_specs=(row(d), _full((1, d))),
        compiler_params=_params("arbitrary"),
    )(*pieces, w, x, gain, dres)


def matmul_dw_pieces(a, pieces, name):
    rows, m = a.shape
    tk = min(2 * ROW_TILE, rows)
    n = len(pieces)

    def body(*refs):
        a_ref, ins, outs = refs[0], refs[1:1 + n], refs[1 + n:]
        av = a_ref[...]
        for k in range(n):
            _accumulate(outs[k], _dot_tn(av, ins[k][...]), pl.program_id(0))

    return pl.pallas_call(
        body, name=name,
        out_shape=[jax.ShapeDtypeStruct((m, p.shape[1]), F32) for p in pieces],
        grid=(rows // tk,),
        in_specs=[pl.BlockSpec((tk, m), lambda k: (k, 0))]
        + [pl.BlockSpec((tk, p.shape[1]), lambda k: (k, 0)) for p in pieces],
        out_specs=[_full((m, p.shape[1])) for p in pieces],
        compiler_params=_params("arbitrary"),
    )(a, *pieces)


def matmul_dw(a, b, bn, name):
    rows, m = a.shape
    n = b.shape[1]
    tk = min(4 * ROW_TILE, rows)
    steps = rows // tk

    def body(a_ref, b_ref, o_ref):
        part = _dot_tn(a_ref[...], b_ref[...])

        @pl.when(pl.program_id(1) == 0)
        def _():
            o_ref[0] = part

        @pl.when(pl.program_id(1) > 0)
        def _():
            o_ref[0] += part

    return pl.pallas_call(
        body, name=name,
        out_shape=jax.ShapeDtypeStruct((n // bn, m, bn), F32),
        grid=(n // bn, steps),
        in_specs=[pl.BlockSpec((tk, m), lambda j, k: (k, 0)), pl.BlockSpec((tk, bn), lambda j, k: (k, j))],
        out_specs=pl.BlockSpec((1, m, bn), lambda j, k: (j, 0, 0)),
        compiler_params=_params("parallel", "arbitrary"),
    )(a, b)


def _scan(a, b, carry, reverse):
    n, c = a.shape
    blocks = n // SUBLANES
    a = a.reshape(blocks, SUBLANES, c)
    b = b.reshape(blocks, SUBLANES, c)
    pos = lax.broadcasted_iota(jnp.int32, (1, SUBLANES, c), 1)
    s = 1
    while s < SUBLANES:
        shift, valid = (SUBLANES - s, pos < SUBLANES - s) if reverse else (s, pos >= s)
        a_s, b_s = pltpu.roll(a, shift, 1), pltpu.roll(b, shift, 1)
        b = jnp.where(valid, a * b_s + b, b)
        a = jnp.where(valid, a * a_s, a)
        s *= 2
    out = [None] * blocks
    for k in (range(blocks - 1, -1, -1) if reverse else range(blocks)):
        h = a[k] * carry + b[k]
        out[k] = h
        carry = h[0:1] if reverse else h[SUBLANES - 1:SUBLANES]
    return jnp.concatenate(out, axis=0)


def _rg_gates(ua, gw_ref, gb, lam):
    ub = ua.astype(BF16)
    pre_r, pre_i = [], []
    for h in range(RG_HEADS):
        z = _dot(ub[:, h * RG_HEAD_DIM:(h + 1) * RG_HEAD_DIM], gw_ref[h])
        pre_r.append(z[:, :RG_HEAD_DIM])
        pre_i.append(z[:, RG_HEAD_DIM:])
    r = _sigmoid(jnp.concatenate(pre_r, axis=1) + gb[0:1])
    i = _sigmoid(jnp.concatenate(pre_i, axis=1) + gb[1:2])
    sp = _softplus(-lam)
    log_a = -RG_C * r * sp
    a = jnp.exp(log_a)
    mult = jnp.sqrt(1.0 - a * a)
    return r, i, sp, a, mult


def _rg_weight_specs():
    return [_full((4, D_MODEL)), _full((1, D_MODEL)), _full((RG_HEADS, RG_HEAD_DIM, 2 * RG_HEAD_DIM)),
            _full((2, D_MODEL)), _full((1, D_MODEL))]


def rglru_fwd(proj, conv_w, conv_b, gate_w, gate_b, lam, reverse, name):
    rows_total = proj.shape[0]
    rows = min(SCAN_TILE, rows_total)
    n_tiles = rows_total // rows
    tix = (lambda i: n_tiles - 1 - i) if reverse else (lambda i: i)

    def body(xp, xm, xn, cw_ref, cb_ref, gw_ref, gb_ref, lam_ref, h_ref, acts_ref, carry):
        i = pl.program_id(0)
        t = tix(i)
        ext = _extend(xp, xm, xn, t == 0, t == n_tiles - 1)
        ua = _conv(ext, cw_ref[...], 2, rows) + cb_ref[...]
        r, gi, _, a, mult = _rg_gates(ua, gw_ref, gb_ref[...], lam_ref[...])
        for k, saved in enumerate((ua, r, gi, a, mult)):
            acts_ref[k] = saved
        b = mult * (gi * ua)

        @pl.when(i == 0)
        def _():
            carry[...] = jnp.zeros_like(carry)

        h = _scan(a, b, carry[0:1], reverse)
        h_ref[...] = h
        edge = h[0:1] if reverse else h[rows - 1:rows]
        carry[...] = jnp.broadcast_to(edge, carry.shape)

    return pl.pallas_call(
        body, name=name,
        out_shape=(jax.ShapeDtypeStruct((rows_total, D_MODEL), F32),
                   jax.ShapeDtypeStruct((5, rows_total, D_MODEL), F32)),
        grid=(n_tiles,),
        in_specs=_halo_specs(rows, D_MODEL, 0, n_tiles, tix) + _rg_weight_specs(),
        out_specs=(pl.BlockSpec((rows, D_MODEL), lambda i: (tix(i), 0)),
                   pl.BlockSpec((5, rows, D_MODEL), lambda i: (0, tix(i), 0))),
        scratch_shapes=[pltpu.VMEM((SUBLANES, D_MODEL), F32)],
        compiler_params=_params("arbitrary"),
    )(proj, proj, proj, conv_w, conv_b, gate_w, gate_b, lam)


def rglru_bwd(proj, dycat, h_dir, acts, gate_w, lam, add_dua, reverse, name):
    rows_total = proj.shape[0]
    rows = min(SCAN_TILE, rows_total)
    n_tiles = rows_total // rows
    tix = (lambda i: i) if reverse else (lambda i: n_tiles - 1 - i)
    za_block = 1

    def body(acts_ref, za_ref, dya_ref, hp, hm, hn, gw_ref, lam_ref, *rest):
        other = rest[0][...] if add_dua is not None else 0.0
        dua_ref, dgw_ref, dgb_ref, dlam_ref, carry = rest[-5:]
        step = pl.program_id(0)
        t = tix(step)
        first, last = t == 0, t == n_tiles - 1
        ua, r, gi, a, mult = (acts_ref[k] for k in range(5))
        lam_v = lam_ref[...]
        sp = _softplus(-lam_v)
        za = za_ref[...]
        dh = dya_ref[...] * (za * _sigmoid(za))

        @pl.when(step == 0)
        def _():
            carry[...] = jnp.zeros_like(carry)

        old = carry[0:1]
        mu = _scan(a, a * dh, old, not reverse)
        row = lax.broadcasted_iota(jnp.int32, mu.shape, 0)
        if reverse:
            mu_next = jnp.where(row == 0, old, pltpu.roll(mu, 1, 0))
            carry[...] = jnp.broadcast_to(mu[rows - 1:rows], carry.shape)
            h_ext = _extend(hp, hm, hn, first, last)
            h_prev = _shifted(h_ext, 1, rows)
        else:
            mu_next = jnp.where(row == rows - 1, old, pltpu.roll(mu, rows - 1, 0))
            carry[...] = jnp.broadcast_to(mu[0:1], carry.shape)
            h_ext = _extend(hp, hm, hn, first, last)
            h_prev = _shifted(h_ext, -1, rows)
        db = dh + mu_next
        da = db * h_prev
        d_mult = db * (gi * ua)
        di = db * (mult * ua)
        dua = db * (mult * gi)
        dlog_a = da * a - d_mult * (a * a) / mult
        dr = dlog_a * (-RG_C * sp)
        dlam = _colsum(dlog_a * (-RG_C * r)) * (-_sigmoid(-lam_v))
        dpr = dr * (r * (1.0 - r))
        dpi = di * (gi * (1.0 - gi))
        dgb = jnp.concatenate([_colsum(dpr), _colsum(dpi)], axis=0)
        ub = ua.astype(BF16)
        dua_heads, dgw_heads = [], []
        for h in range(RG_HEADS):
            cols = slice(h * RG_HEAD_DIM, (h + 1) * RG_HEAD_DIM)
            dz = jnp.concatenate([dpr[:, cols], dpi[:, cols]], axis=1).astype(BF16)
            dgw_heads.append(_dot_tn(ub[:, cols], dz))
            dua_heads.append(_dot_nt(dz, gw_ref[h]))
        dua_ref[...] = dua + jnp.concatenate(dua_heads, axis=1) + other

        @pl.when(step == 0)
        def _():
            for h in range(RG_HEADS):
                dgw_ref[h] = dgw_heads[h]
            dgb_ref[...] = dgb
            dlam_ref[...] = dlam

        @pl.when(step > 0)
        def _():
            for h in range(RG_HEADS):
                dgw_ref[h] += dgw_heads[h]
            dgb_ref[...] += dgb
            dlam_ref[...] += dlam

    row_spec = lambda col: pl.BlockSpec((rows, D_MODEL), lambda i: (tix(i), col))
    return pl.pallas_call(
        body, name=name,
        out_shape=(jax.ShapeDtypeStruct((rows_total, D_MODEL), F32),
                   jax.ShapeDtypeStruct((RG_HEADS, RG_HEAD_DIM, 2 * RG_HEAD_DIM), F32),
                   jax.ShapeDtypeStruct((2, D_MODEL), F32), jax.ShapeDtypeStruct((1, D_MODEL), F32)),
        grid=(n_tiles,),
        in_specs=([pl.BlockSpec((5, rows, D_MODEL), lambda i: (0, tix(i), 0)), row_spec(za_block), row_spec(0)]
                  + _halo_specs(rows, D_MODEL, 0, n_tiles, tix)
                  + [_full((RG_HEADS, RG_HEAD_DIM, 2 * RG_HEAD_DIM)), _full((1, D_MODEL))]
                  + ([] if add_dua is None else [row_spec(0)])),
        out_specs=(row_spec(0), _full((RG_HEADS, RG_HEAD_DIM, 2 * RG_HEAD_DIM)), _full((2, D_MODEL)),
                   _full((1, D_MODEL))),
        scratch_shapes=[pltpu.VMEM((SUBLANES, D_MODEL), F32)],
        compiler_params=_params("arbitrary"),
    )(acts, proj, dycat, h_dir, h_dir, h_dir, gate_w, lam, *([] if add_dua is None else [add_dua]))


def even_mix_fwd(proj, h_f, h_b, sc_w, name):
    rows_total = proj.shape[0]
    rows = min(2 * MIX_TILE, rows_total)
    n_tiles = rows_total // rows
    cb = D_MODEL
    n_cb = 1
    ident = lambda i: i

    def body(za_ref, hf_ref, hb_ref, xbp, xbm, xbn, gcp, gcm, gcn, gb_ref, zb_ref, w_ref, y_ref):
        t = pl.program_id(1)
        first, last = t == 0, t == n_tiles - 1
        za = za_ref[...]
        y_ref[:, 0:cb] = ((hf_ref[...] + hb_ref[...]) * (za * _sigmoid(za))).astype(BF16)
        p_ext = _extend(xbp, xbm, xbn, first, last) * _extend(gcp, gcm, gcn, first, last)
        cv = _conv(p_ext, w_ref[...], 1, rows)
        zb = zb_ref[...]
        y_ref[:, cb:2 * cb] = (gb_ref[...] * cv * (zb * _sigmoid(zb))).astype(BF16)

    blk = lambda col: pl.BlockSpec((rows, cb), lambda c, i: (i, col * n_cb + c))
    own = pl.BlockSpec((rows, cb), lambda c, i: (i, c))
    return pl.pallas_call(
        body, name=name,
        out_shape=jax.ShapeDtypeStruct((rows_total, 2 * D_MODEL), BF16),
        grid=(n_cb, n_tiles),
        in_specs=([blk(1), own, own] + _halo_specs(rows, cb, 2 * n_cb, n_tiles, ident)
                  + _halo_specs(rows, cb, 4 * n_cb, n_tiles, ident)
                  + [blk(3), blk(5), pl.BlockSpec((3, cb), lambda c, i: (0, c))]),
        out_specs=pl.BlockSpec((rows, 2 * cb), lambda c, i: (i, 0)),
        compiler_params=_params("parallel", "arbitrary"),
    )(proj, h_f, h_b, proj, proj, proj, proj, proj, proj, proj, proj, sc_w)


def even_mix_bwd(proj, dycat, h_f, h_b, dua, conv_w, sc_w, name):
    rows_total = proj.shape[0]
    rows = min(MIX_TILE, rows_total)
    n_tiles = rows_total // rows
    cb = D_MODEL
    n_cb = 1
    ident = lambda i: i

    def body(xap, xam, xan, za_ref, xbp, xbm, xbn, gbp, gbm, gbn, gcp, gcm, gcn, zbp, zbm, zbn,
             dya_ref, dybp, dybm, dybn, hf_ref, hb_ref, dup, dum, dun, cw_ref, sw_ref,
             dp_ref, dcw_ref, dcb_ref, dsw_ref):
        def put(k, value):
            dp_ref[:, k * cb:(k + 1) * cb] = value.astype(BF16)

        t = pl.program_id(1)
        first, last = t == 0, t == n_tiles - 1
        za = za_ref[...]
        sa = _sigmoid(za)
        put(1, dya_ref[...] * (hf_ref[...] + hb_ref[...]) * (sa * (1.0 + za * (1.0 - sa))))
        dua_ext = _extend(dup, dum, dun, first, last)
        cw = cw_ref[...]
        put(0, _conv_transpose(dua_ext, cw, 2, rows))
        dua = dua_ext[SUBLANES:SUBLANES + rows]
        xa_ext = _extend(xap, xam, xan, first, last)
        dcw = jnp.concatenate([_colsum(dua * _shifted(xa_ext, k - 2, rows)) for k in range(4)], axis=0)
        dcb = _colsum(dua)
        xb_ext = _extend(xbp, xbm, xbn, first, last)
        gc_ext = _extend(gcp, gcm, gcn, first, last)
        p_ext = xb_ext * gc_ext
        zb_ext = _extend(zbp, zbm, zbn, first, last)
        sb_ext = _sigmoid(zb_ext)
        dyb_ext = _extend(dybp, dybm, dybn, first, last)
        gb_ext = _extend(gbp, gbm, gbn, first, last)
        dcv_ext = dyb_ext * gb_ext * (zb_ext * sb_ext)
        sw = sw_ref[...]
        p_at = [_shifted(p_ext, k - 1, rows) for k in range(3)]
        cv = (p_at[0] * sw[0:1] + p_at[1] * sw[1:2]) + p_at[2] * sw[2:3]
        mid = slice(SUBLANES, SUBLANES + rows)
        zb, sb, dyb, gb = zb_ext[mid], sb_ext[mid], dyb_ext[mid], gb_ext[mid]
        put(3, dyb * cv * (zb * sb))
        put(5, dyb * gb * cv * (sb * (1.0 + zb * (1.0 - sb))))
        dp = _conv_transpose(dcv_ext, sw, 1, rows)
        put(4, dp * xb_ext[mid])
        put(2, dp * gc_ext[mid])
        dcv = dcv_ext[mid]
        dsw = jnp.concatenate([_colsum(dcv * p_at[k]) for k in range(3)], axis=0)

        @pl.when(t == 0)
        def _():
            dcw_ref[...] = dcw
            dcb_ref[...] = dcb
            dsw_ref[...] = dsw

        @pl.when(t > 0)
        def _():
            dcw_ref[...] += dcw
            dcb_ref[...] += dcb
            dsw_ref[...] += dsw

    blk = lambda col: pl.BlockSpec((rows, cb), lambda c, i: (i, col * n_cb + c))
    halo = lambda col: _halo_specs(rows, cb, col * n_cb, n_tiles, ident)
    own = pl.BlockSpec((rows, cb), lambda c, i: (i, c))
    wspec = lambda k: pl.BlockSpec((k, cb), lambda c, i: (0, c))
    return pl.pallas_call(
        body, name=name,
        out_shape=(jax.ShapeDtypeStruct((rows_total, 6 * D_MODEL), BF16),
                   jax.ShapeDtypeStruct((4, D_MODEL), F32), jax.ShapeDtypeStruct((1, D_MODEL), F32),
                   jax.ShapeDtypeStruct((3, D_MODEL), F32)),
        grid=(n_cb, n_tiles),
        in_specs=(halo(0) + [blk(1)] + halo(2) + halo(3) + halo(4) + halo(5) + [blk(0)] + halo(1)
                  + [own, own] + halo(0) + [wspec(4), wspec(3)]),
        out_specs=(pl.BlockSpec((rows, 6 * cb), lambda c, i: (i, 0)), wspec(4), wspec(1), wspec(3)),
        compiler_params=_params("parallel", "arbitrary"),
    )(proj, proj, proj, proj, proj, proj, proj, proj, proj, proj, proj, proj, proj, proj, proj, proj,
      dycat, dycat, dycat, dycat, h_f, h_b, dua, dua, dua, conv_w, sc_w)


def even_out_fwd(ycat, w_out, gain, x, name):
    rows, d = x.shape
    k = ycat.shape[1]
    tm = min(ROW_TILE, rows)

    def body(yc_ref, w_ref, g_ref, x_ref, x1_ref, y_ref):
        y = _dot(yc_ref[...], w_ref[...])
        y_ref[...] = y
        rstd = lax.rsqrt(jnp.mean(y * y, axis=-1, keepdims=True) + NORM_EPS)
        x1_ref[...] = x_ref[...] + y * rstd * g_ref[...]

    row = lambda n: pl.BlockSpec((tm, n), lambda i: (i, 0))
    return pl.pallas_call(
        body, name=name,
        out_shape=(jax.ShapeDtypeStruct((rows, d), F32),) * 2,
        grid=(rows // tm,),
        in_specs=[row(k), _full((k, d)), _full((1, d)), row(d)],
        out_specs=(row(d), row(d)),
        compiler_params=_params("parallel"),
    )(ycat, w_out, gain, x)


def _rmsnorm_bwd(dout, y, gain):
    rstd = lax.rsqrt(jnp.mean(y * y, axis=-1, keepdims=True) + NORM_EPS)
    yhat = y * rstd
    dyn = dout * gain
    dy = rstd * (dyn - yhat * jnp.mean(dyn * yhat, axis=-1, keepdims=True))
    return dy, dout * yhat


def even_out_bwd(dx1, y, gain, w_out, name):
    rows, d = y.shape
    k = w_out.shape[0]
    tm = min(ROW_TILE, rows)

    def body(dx_ref, y_ref, g_ref, w_ref, dy_ref, dyc_ref, dg_ref):
        dy, dg_rows = _rmsnorm_bwd(dx_ref[...], y_ref[...], g_ref[...])
        dyb = dy.astype(BF16)
        dy_ref[...] = dyb
        dyc_ref[...] = _dot_nt(dyb, w_ref[...])
        _accumulate(dg_ref, _colsum(dg_rows), pl.program_id(0))

    row = lambda n: pl.BlockSpec((tm, n), lambda i: (i, 0))
    return pl.pallas_call(
        body, name=name,
        out_shape=(jax.ShapeDtypeStruct((rows, d), BF16), jax.ShapeDtypeStruct((rows, k), F32),
                   jax.ShapeDtypeStruct((1, d), F32)),
        grid=(rows // tm,),
        in_specs=[row(d), row(d), _full((1, d)), _full((k, d))],
        out_specs=(row(d), row(k), _full((1, d))),
        compiler_params=_params("arbitrary"),
    )(dx1, y, gain, w_out)


def _chunk_cumsum(g, reverse):
    n, c = g.shape
    g = g.reshape(n // GLA_CHUNK, GLA_CHUNK, c)
    pos = lax.broadcasted_iota(jnp.int32, (1, GLA_CHUNK, c), 1)
    s = 1
    while s < GLA_CHUNK:
        if reverse:
            g = g + jnp.where(pos < GLA_CHUNK - s, pltpu.roll(g, GLA_CHUNK - s, 1), 0.0)
        else:
            g = g + jnp.where(pos >= s, pltpu.roll(g, s, 1), 0.0)
        s *= 2
    return g.reshape(n, c)


def _gla_prepare(q_ref, k_ref, lr_ref, wg_ref, bg_ref, reverse, n_chunks):
    z = _dot(lr_ref[...].astype(BF16), wg_ref[0]) + bg_ref[0]
    g = -_softplus(-z) * (1.0 / GLA_NORMALIZER)
    bcum = _chunk_cumsum(g, reverse).reshape(n_chunks, GLA_CHUNK, GLA_DK)
    edge = 0 if reverse else GLA_CHUNK - 1
    btot = bcum[:, edge:edge + 1, :]
    e_pos = jnp.exp(bcum)
    e_neg = jnp.exp(-bcum)
    e_st = jnp.exp(btot - bcum)
    q3 = q_ref[...].reshape(n_chunks, GLA_CHUNK, GLA_DK)
    k3 = k_ref[...].reshape(n_chunks, GLA_CHUNK, GLA_DK)
    scale = GLA_DK ** -0.5
    q_in = q3 * scale * e_pos
    k_in = k3 * e_neg
    k_st = k3 * e_st
    dec = jnp.exp(btot)
    return z, q_in, k_in, k_st, dec, (scale * e_pos, e_neg, e_st)


def _gla_mask(reverse):
    i = lax.broadcasted_iota(jnp.int32, (GLA_CHUNK, GLA_CHUNK), 0)
    j = lax.broadcasted_iota(jnp.int32, (GLA_CHUNK, GLA_CHUNK), 1)
    return (j >= i) if reverse else (j <= i)


def _gla_specs(rows, n_blocks, reverse):
    tix = (lambda s: n_blocks - 1 - s) if reverse else (lambda s: s)
    d = 1 if reverse else 0
    lr_block = LR_COL // LANES
    specs = [pl.BlockSpec((rows, GLA_DK), lambda h, s: (tix(s), h)),
             pl.BlockSpec((rows, GLA_DK), lambda h, s: (tix(s), GLA_HEADS + h)),
             pl.BlockSpec((rows, GLA_DV), lambda h, s: (tix(s), GLA_HEADS + h)),
             pl.BlockSpec((rows, LANES), lambda h, s: (tix(s), lr_block)),
             pl.BlockSpec((1, LANES, GLA_DK), lambda h, s: (d, 0, h)),
             pl.BlockSpec((1, 1, GLA_DK), lambda h, s: (d, 0, h))]
    return specs, tix


def gla_fwd(proj, wg_pad, bg, reverse, name):
    rows_total = proj.shape[0]
    rows = min(GLA_BLOCK, rows_total)
    n_blocks = rows_total // rows
    n_chunks = rows // GLA_CHUNK
    specs, tix = _gla_specs(rows, n_blocks, reverse)

    def body(q_ref, k_ref, v_ref, lr_ref, wg_ref, bg_ref, o_ref, st_ref, state, kv_scr, dec_scr):
        _, q_in, k_in, k_st, dec, _ = _gla_prepare(q_ref, k_ref, lr_ref, wg_ref, bg_ref, reverse, n_chunks)
        vb = v_ref[...].reshape(n_chunks, GLA_CHUNK, GLA_DV).astype(BF16)
        qb = q_in.astype(BF16)
        p = jnp.where(_gla_mask(reverse), _bdot(qb, k_in.astype(BF16), 2, 2), 0.0)
        o = _bdot(p.astype(BF16), vb, 2, 1)
        kv_scr[...] = _bdot(vb, k_st.astype(BF16), 1, 1)
        dec_scr[...] = jnp.broadcast_to(dec, dec_scr.shape)

        @pl.when(pl.program_id(1) == 0)
        def _():
            state[...] = jnp.zeros_like(state)

        for c in range(n_chunks):
            cc = n_chunks - 1 - c if reverse else c
            st_ref[0, cc] = state[...]
            state[...] = state[...] * dec_scr[cc, 0:1] + kv_scr[cc]
        o = o + _bdot(qb, st_ref[0].astype(BF16), 2, 2)
        o_ref[...] = o.reshape(rows, GLA_DV)

    return pl.pallas_call(
        body, name=name,
        out_shape=(jax.ShapeDtypeStruct((rows_total, GLA_HEADS * GLA_DV), F32),
                   jax.ShapeDtypeStruct((GLA_HEADS, rows_total // GLA_CHUNK, GLA_DV, GLA_DK), F32)),
        grid=(GLA_HEADS, n_blocks),
        in_specs=specs,
        out_specs=(pl.BlockSpec((rows, GLA_DV), lambda h, s: (tix(s), h)),
                   pl.BlockSpec((1, n_chunks, GLA_DV, GLA_DK), lambda h, s: (h, tix(s), 0, 0))),
        scratch_shapes=[pltpu.VMEM((GLA_DV, GLA_DK), F32), pltpu.VMEM((n_chunks, GLA_DV, GLA_DK), F32),
                        pltpu.VMEM((n_chunks, SUBLANES, GLA_DK), F32)],
        compiler_params=_params("parallel", "arbitrary"),
    )(proj, proj, proj, proj, wg_pad, bg)


def gla_bwd(proj, wg_pad, bg, d_o, states, dqkv_in, reverse, name):
    rows_total = proj.shape[0]
    rows = min(GLA_BLOCK, rows_total)
    n_blocks = rows_total // rows
    n_chunks = rows // GLA_CHUNK
    specs, tix = _gla_specs(rows, n_blocks, not reverse)
    d = 1 if reverse else 0
    specs[4] = pl.BlockSpec((1, LANES, GLA_DK), lambda h, s: (d, 0, h))
    specs[5] = pl.BlockSpec((1, 1, GLA_DK), lambda h, s: (d, 0, h))
    add = dqkv_in is not None

    def body(*refs):
        q_ref, k_ref, v_ref, lr_ref, wg_ref, bg_ref, do_ref, st_ref = refs[:8]
        refs = refs[8:]
        if add:
            aq_ref, ak_ref, av_ref = refs[:3]
            refs = refs[3:]
        dq_ref, dk_ref, dv_ref, dz_ref, dstate, g_scr, dec_scr, dsn_scr = refs
        z, q_in, k_in, k_st, dec, (f_q, f_k, f_s) = _gla_prepare(q_ref, k_ref, lr_ref, wg_ref, bg_ref, reverse,
                                                                 n_chunks)
        mask = _gla_mask(reverse)
        vb = v_ref[...].reshape(n_chunks, GLA_CHUNK, GLA_DV).astype(BF16)
        dob = do_ref[...].reshape(n_chunks, GLA_CHUNK, GLA_DV).astype(BF16)
        qb, kb, ksb = q_in.astype(BF16), k_in.astype(BF16), k_st.astype(BF16)
        st = st_ref[0]
        stb = st.astype(BF16)
        pb = jnp.where(mask, _bdot(qb, kb, 2, 2), 0.0).astype(BF16)
        dpb = jnp.where(mask, _bdot(dob, vb, 2, 2), 0.0).astype(BF16)
        d_qin = _bdot(dpb, kb, 2, 1) + _bdot(dob, stb, 2, 1)
        d_kin = _bdot(dpb, qb, 1, 1)
        dv = _bdot(pb, dob, 1, 1)
        g_scr[...] = _bdot(dob, qb, 1, 1)
        dec_scr[...] = jnp.broadcast_to(dec, dec_scr.shape)

        @pl.when(pl.program_id(1) == 0)
        def _():
            dstate[...] = jnp.zeros_like(dstate)

        for c in range(n_chunks):
            cc = c if reverse else n_chunks - 1 - c
            dsn_scr[cc] = dstate[...]
            dstate[...] = dstate[...] * dec_scr[cc, 0:1] + g_scr[cc]
        dsn = dsn_scr[...]
        dsnb = dsn.astype(BF16)
        dv = dv + _bdot(ksb, dsnb, 2, 2)
        d_kst = _bdot(vb, dsnb, 2, 1)
        d_dec = jnp.sum(dsn * st, axis=1, keepdims=True)
        ks_term = d_kst * k_st
        d_btot = d_dec * dec + jnp.sum(ks_term, axis=1, keepdims=True)
        d_b = d_qin * q_in - d_kin * k_in - ks_term
        pos = lax.broadcasted_iota(jnp.int32, d_b.shape, 1)
        edge = 0 if reverse else GLA_CHUNK - 1
        d_b = d_b + jnp.where(pos == edge, d_btot, 0.0)
        dg = _chunk_cumsum(d_b.reshape(rows, GLA_DK), not reverse)
        dz_ref[...] = dg * (1.0 / GLA_NORMALIZER) * _sigmoid(-z)
        dq = (d_qin * f_q).reshape(rows, GLA_DK)
        dk = (d_kin * f_k + d_kst * f_s).reshape(rows, GLA_DK)
        dv = dv.reshape(rows, GLA_DV)
        if add:
            dq_ref[...] = (dq + aq_ref[...]).astype(BF16)
            dk_ref[...] = (dk + ak_ref[...]).astype(BF16)
            dv_ref[...] = (dv + av_ref[...]).astype(BF16)
        else:
            dq_ref[...] = dq
            dk_ref[...] = dk
            dv_ref[...] = dv

    qkv_specs = [pl.BlockSpec((rows, GLA_DK), lambda h, s: (tix(s), h)),
                 pl.BlockSpec((rows, GLA_DK), lambda h, s: (tix(s), h)),
                 pl.BlockSpec((rows, GLA_DV), lambda h, s: (tix(s), h))]
    in_specs = specs + [pl.BlockSpec((rows, GLA_DV), lambda h, s: (tix(s), h)),
                        pl.BlockSpec((1, n_chunks, GLA_DV, GLA_DK), lambda h, s: (h, tix(s), 0, 0))]
    args = [proj, proj, proj, proj, wg_pad, bg, d_o, states]
    out_dtype = F32
    if add:
        in_specs += qkv_specs
        args += list(dqkv_in)
        out_dtype = BF16
    return pl.pallas_call(
        body, name=name,
        out_shape=(jax.ShapeDtypeStruct((rows_total, GLA_HEADS * GLA_DK), out_dtype),
                   jax.ShapeDtypeStruct((rows_total, GLA_HEADS * GLA_DK), out_dtype),
                   jax.ShapeDtypeStruct((rows_total, GLA_HEADS * GLA_DV), out_dtype),
                   jax.ShapeDtypeStruct((rows_total, GLA_HEADS * GLA_DK), F32)),
        grid=(GLA_HEADS, n_blocks),
        in_specs=in_specs,
        out_specs=(pl.BlockSpec((rows, GLA_DK), lambda h, s: (tix(s), h)),
                   pl.BlockSpec((rows, GLA_DK), lambda h, s: (tix(s), h)),
                   pl.BlockSpec((rows, GLA_DV), lambda h, s: (tix(s), h)),
                   pl.BlockSpec((rows, GLA_DK), lambda h, s: (tix(s), h))),
        scratch_shapes=[pltpu.VMEM((GLA_DV, GLA_DK), F32), pltpu.VMEM((n_chunks, GLA_DV, GLA_DK), F32),
                        pltpu.VMEM((n_chunks, SUBLANES, GLA_DK), F32),
                        pltpu.VMEM((n_chunks, GLA_DV, GLA_DK), F32)],
        compiler_params=_params("parallel", "arbitrary"),
    )(*args)


def gla_gate_bwd(proj, dz_f, dz_b, wg_pad, name):
    rows_total = proj.shape[0]
    tm = min(ROW_TILE, rows_total)
    n_key = GLA_HEADS * GLA_DK

    def body(lr_ref, dzf_ref, dzb_ref, wg_ref, dlr_ref, dwg_ref, dbg_ref):
        step = pl.program_id(0)
        lr_t = jnp.transpose(lr_ref[...])
        dzf, dzb = dzf_ref[...], dzb_ref[...]
        dzf16, dzb16 = dzf.astype(BF16), dzb.astype(BF16)
        dlr_ref[...] = (_dot_nt(dzf16, wg_ref[0]) + _dot_nt(dzb16, wg_ref[1])).astype(BF16)
        dwf = _dot(lr_t[0:GLA_RANK].astype(BF16), dzf16)
        dwb = _dot(lr_t[GLA_RANK:2 * GLA_RANK].astype(BF16), dzb16)
        dbg = jnp.concatenate([_colsum(dzf), _colsum(dzb)], axis=0)

        @pl.when(step == 0)
        def _():
            dwg_ref[0] = dwf
            dwg_ref[1] = dwb
            dbg_ref[...] = dbg

        @pl.when(step > 0)
        def _():
            dwg_ref[0] += dwf
            dwg_ref[1] += dwb
            dbg_ref[...] += dbg

    return pl.pallas_call(
        body, name=name,
        out_shape=(jax.ShapeDtypeStruct((rows_total, LANES), BF16), jax.ShapeDtypeStruct((2, GLA_RANK, n_key), F32),
                   jax.ShapeDtypeStruct((2, n_key), F32)),
        grid=(rows_total // tm,),
        in_specs=[pl.BlockSpec((tm, LANES), lambda i: (i, LR_COL // LANES)),
                  pl.BlockSpec((tm, n_key), lambda i: (i, 0)), pl.BlockSpec((tm, n_key), lambda i: (i, 0)),
                  _full((2, LANES, n_key))],
        out_specs=(pl.BlockSpec((tm, LANES), lambda i: (i, 0)), _full((2, GLA_RANK, n_key)), _full((2, n_key))),
        compiler_params=_params("arbitrary"),
    )(proj, dz_f, dz_b, wg_pad)


def _head_norm(o, gain):
    outs, hats, rstds = [], [], []
    for h in range(GLA_HEADS):
        oh = o[:, h * GLA_DV:(h + 1) * GLA_DV]
        rstd = lax.rsqrt(jnp.mean(oh * oh, axis=-1, keepdims=True) + NORM_EPS)
        hat = oh * rstd
        outs.append(hat * gain)
        hats.append(hat)
        rstds.append(rstd)
    return outs, hats, rstds


def odd_out_fwd(o_f, o_b, proj, head_gain, w_out, gain, x1, target, name):
    rows, d = x1.shape
    tm = min(ROW_TILE, rows)
    r_block = (2 * GLA_HEADS * GLA_DK + GLA_HEADS * GLA_DV) // d

    def body(of_ref, ob_ref, r_ref, hg_ref, w_ref, g_ref, x1_ref, tgt_ref, y2_ref, dy_ref, dx2_ref, loss_ref,
             dg_ref):
        step = pl.program_id(0)
        on, _, _ = _head_norm(of_ref[...] + ob_ref[...], hg_ref[...])
        r = r_ref[...]
        y2 = (jnp.concatenate(on, axis=1) * (r * _sigmoid(r))).astype(BF16)
        y2_ref[...] = y2
        y = _dot(y2, w_ref[...])
        gain_v = g_ref[...]
        rstd = lax.rsqrt(jnp.mean(y * y, axis=-1, keepdims=True) + NORM_EPS)
        x2 = x1_ref[...] + y * rstd * gain_v
        diff = x2 - tgt_ref[...]
        loss = 0.5 * jnp.sum(jnp.mean(diff * diff, axis=-1, keepdims=True), axis=0, keepdims=True)
        dx2 = diff * (1.0 / d)
        dx2_ref[...] = dx2
        dy, dg_rows = _rmsnorm_bwd(dx2, y, gain_v)
        dy_ref[...] = dy.astype(BF16)
        _accumulate(loss_ref, jnp.broadcast_to(loss, loss_ref.shape), step)
        _accumulate(dg_ref, _colsum(dg_rows), step)

    row = lambda n, col=0: pl.BlockSpec((tm, n), lambda i: (i, col))
    return pl.pallas_call(
        body, name=name,
        out_shape=(jax.ShapeDtypeStruct((rows, d), BF16), jax.ShapeDtypeStruct((rows, d), BF16),
                   jax.ShapeDtypeStruct((rows, d), F32), jax.ShapeDtypeStruct((SUBLANES, LANES), F32),
                   jax.ShapeDtypeStruct((1, d), F32)),
        grid=(rows // tm,),
        in_specs=[row(d), row(d), row(d, r_block), _full((1, GLA_DV)), _full((d, d)), _full((1, d)), row(d), row(d)],
        out_specs=(row(d), row(d), row(d), _full((SUBLANES, LANES)), _full((1, d))),
        compiler_params=_params("arbitrary"),
    )(o_f, o_b, proj, head_gain, w_out, gain, x1, target)


def odd_out_bwd(dy, w_out, o_f, o_b, proj, head_gain, name):
    rows, d = dy.shape
    tm = min(ROW_TILE, rows)
    r_block = (2 * GLA_HEADS * GLA_DK + GLA_HEADS * GLA_DV) // d

    def body(dy_ref, w_ref, of_ref, ob_ref, r_ref, hg_ref, dr_ref, do_ref, dhg_ref):
        dy2 = _dot_nt(dy_ref[...], w_ref[...])
        hg = hg_ref[...]
        on, hats, rstds = _head_norm(of_ref[...] + ob_ref[...], hg)
        r = r_ref[...]
        sr = _sigmoid(r)
        dr_ref[...] = (dy2 * jnp.concatenate(on, axis=1) * (sr * (1.0 + r * (1.0 - sr)))).astype(BF16)
        d_on = dy2 * (r * sr)
        d_os, dhg = [], None
        for h in range(GLA_HEADS):
            dn = d_on[:, h * GLA_DV:(h + 1) * GLA_DV]
            part = _colsum(dn * hats[h])
            dhg = part if dhg is None else dhg + part
            dng = dn * hg
            d_os.append(rstds[h] * (dng - hats[h] * jnp.mean(dng * hats[h], axis=-1, keepdims=True)))
        do_ref[...] = jnp.concatenate(d_os, axis=1)
        _accumulate(dhg_ref, dhg, pl.program_id(0))

    row = lambda n, col=0: pl.BlockSpec((tm, n), lambda i: (i, col))
    return pl.pallas_call(
        body, name=name,
        out_shape=(jax.ShapeDtypeStruct((rows, d), BF16), jax.ShapeDtypeStruct((rows, d), F32),
                   jax.ShapeDtypeStruct((1, GLA_DV), F32)),
        grid=(rows // tm,),
        in_specs=[row(d), _full((d, d)), row(d), row(d), row(d, r_block), _full((1, GLA_DV))],
        out_specs=(row(d), row(d), _full((1, GLA_DV))),
        compiler_params=_params("arbitrary"),
    )(dy, w_out, o_f, o_b, proj, head_gain)


def local_step(x, target, w, reduce_first=None, reduce_second=None, late_weights=None):
    g = {}
    proj_e, h0 = norm_matmul(x, w["even_norm_pre"], w["even_w_in"], "even_in_proj")
    h_dir, acts = zip(*[rglru_fwd(proj_e, w["rg_conv_w"], w["rg_conv_b"], w["rg_gate_w"][d], w["rg_gate_b"][d],
                                  w["rg_lambda"][d], d == 1, "rglru_fwd_%d" % d) for d in range(2)])
    ycat = even_mix_fwd(proj_e, h_dir[0], h_dir[1], w["sc_conv_w"], "even_mix_fwd")
    if late_weights is not None:
        w = dict(w, **late_weights(ycat))
    x1, y_e = even_out_fwd(ycat, w["even_w_out"], w["even_norm_post"], x, "even_out_fwd")
    proj_o, h1 = norm_matmul(x1, w["odd_norm_pre"], w["odd_w_in"], "odd_in_proj")
    o_dir, st_dir = [], []
    for d in range(2):
        o, st = gla_fwd(proj_o, w["gla_wg_pad"], w["gla_b_gate"], d == 1, "gla_fwd_%d" % d)
        o_dir.append(o)
        st_dir.append(st)
    y2, dy_o, dx2, loss, g["odd_norm_post"] = odd_out_fwd(
        o_dir[0], o_dir[1], proj_o, w["gla_norm_g"], w["odd_w_out"], w["odd_norm_post"], x1, target, "odd_out_fwd")
    g["odd_w_out"] = matmul_dw(y2, dy_o, D_MODEL, "odd_w_out_grad")[0]
    dr, d_o, g["gla_norm_g"] = odd_out_bwd(dy_o, w["odd_w_out"], o_dir[0], o_dir[1], proj_o, w["gla_norm_g"],
                                           "odd_out_bwd")
    dq, dk, dv, dz_f = gla_bwd(proj_o, w["gla_wg_pad"], w["gla_b_gate"], d_o, st_dir[0], None, False, "gla_bwd_0")
    dq, dk, dv, dz_b = gla_bwd(proj_o, w["gla_wg_pad"], w["gla_b_gate"], d_o, st_dir[1], (dq, dk, dv), True,
                               "gla_bwd_1")
    dlr, g["gla_w_gate_lr"], g["gla_b_gate"] = gla_gate_bwd(proj_o, dz_f, dz_b, w["gla_wg_pad"], "gla_gate_bwd")
    dproj_o = [dq, dk, dv, dr, dlr]
    g["odd_w_in"] = jnp.concatenate(matmul_dw_pieces(h1, dproj_o, "odd_w_in_grad"), axis=1)[:, :ODD_IN]
    dx1, g["odd_norm_pre"] = inproj_bwd_pieces(dproj_o, w["odd_w_in"], x1, w["odd_norm_pre"], dx2, "odd_in_proj_bwd")
    dy_e, dycat, g["even_norm_post"] = even_out_bwd(dx1, y_e, w["even_norm_post"], w["even_w_out"], "even_out_bwd")
    g["even_w_out"] = matmul_dw(ycat, dy_e, D_MODEL, "even_w_out_grad")[0]
    lam = w["rg_lambda"] if reduce_first is None else w["rg_lambda"] + reduce_first(g)
    dua, dgw, dgb, dlam = None, [], [], []
    for d in range(2):
        a, b, c, e = rglru_bwd(proj_e, dycat, h_dir[d], acts[d], w["rg_gate_w"][d], lam[d], dua, d == 1,
                               "rglru_bwd_%d" % d)
        dua = a
        dgw.append(b)
        dgb.append(c)
        dlam.append(e)
    dproj_e, g["rg_conv_w"], g["rg_conv_b"], g["sc_conv_w"] = even_mix_bwd(
        proj_e, dycat, h_dir[0], h_dir[1], dua, w["rg_conv_w"], w["sc_conv_w"], "even_mix_bwd")
    dgw = jnp.stack(dgw).reshape(2, RG_HEADS, RG_HEAD_DIM, 2, RG_HEAD_DIM)
    g["rg_gate_w"] = jnp.transpose(dgw, (0, 3, 1, 2, 4))
    g["rg_gate_b"] = jnp.stack(dgb).reshape(2, 2, RG_HEADS, RG_HEAD_DIM)
    g["rg_lambda"] = jnp.concatenate(dlam, axis=0)
    g["even_w_in"] = matmul_dw(h0, dproj_e, EVEN_IN // 4, "even_w_in_grad")
    gain = w["even_norm_pre"] if reduce_second is None else w["even_norm_pre"] + reduce_second(g)
    grad_x, g["even_norm_pre"] = inproj_bwd(dproj_e, w["even_w_in"], x, gain, dx1, "even_in_proj_bwd")
    return loss, grad_x, g


def _prepare_weights(full):
    w = {}
    for name in ("even_norm_pre", "even_norm_post", "rg_conv_b", "odd_norm_pre", "odd_norm_post", "gla_norm_g"):
        if name in full:
            w[name] = full[name].reshape(1, -1)
    for name in ("rg_conv_w", "sc_conv_w"):
        if name in full:
            w[name] = full[name]
    for name in ("even_w_out", "odd_w_out"):
        if name in full:
            w[name] = full[name].astype(BF16)
    if "even_w_in" in full:
        w["even_w_in"] = full["even_w_in"].astype(BF16)
        if w["even_w_in"].ndim == 2:
            w["even_w_in"] = jnp.transpose(w["even_w_in"].reshape(D_MODEL, 4, EVEN_IN // 4), (1, 0, 2))
    if "rg_gate_w" in full:
        gw = jnp.transpose(full["rg_gate_w"].astype(BF16), (0, 2, 3, 1, 4))
        w["rg_gate_w"] = gw.reshape(2, RG_HEADS, RG_HEAD_DIM, 2 * RG_HEAD_DIM)
        w["rg_gate_b"] = full["rg_gate_b"].reshape(2, 2, D_MODEL)
        w["rg_lambda"] = full["rg_lambda"].reshape(2, 1, D_MODEL)
    if "odd_w_in" in full:
        w_in = jnp.pad(full["odd_w_in"].astype(BF16), ((0, 0), (0, ODD_IN_PAD - ODD_IN)))
        w["odd_w_in"] = w_in.reshape(1, D_MODEL, ODD_IN_PAD)
    if "gla_w_gate_lr" in full:
        wg = full["gla_w_gate_lr"].astype(BF16)
        w["gla_wg_pad"] = jnp.stack([jnp.pad(wg[d], ((d * GLA_RANK, LANES - (d + 1) * GLA_RANK), (0, 0)))
                                     for d in range(2)])
        w["gla_b_gate"] = full["gla_b_gate"].reshape(2, 1, GLA_HEADS * GLA_DK)
    return w


SHARDED_SMALL = (("rg_conv_w", (4, 256)), ("rg_lambda", (2, 256)), ("sc_conv_w", (3, 256)),
                 ("odd_norm_pre", (256,)), ("odd_norm_post", (256,)), ("gla_w_gate_lr", (2, 16, 128)),
                 ("gla_b_gate", (2, 128)), ("gla_norm_g", (64,)))
SHARDED_ROWS = 96
REPLICATED = (("rg_gate_w", (2, 2, 8, 128, 128)), ("even_norm_post", (1024,)), ("rg_conv_b", (1024,)),
              ("rg_gate_b", (2, 2, 8, 128)))
GATE_ROWS = 4096
LAST_REPLICATED = (("even_norm_pre", (1024,)),)
LAST_ROWS = 8
REPLICATED_ROWS = 4160
REP_PART = REPLICATED_ROWS // 8
HALF_SHARDED = SHARDED_ROWS // 2
PACK_HALF = HALF_SHARDED + REP_PART


def _seg_rows(shape):
    n = 1
    for s in shape:
        n *= s
    return -(-n // (SUBLANES * LANES)) * SUBLANES


def _pack(arrays, spec, total_rows, lead=()):
    parts = []
    for name, shape in spec:
        flat = arrays[name].reshape(lead + (-1,))
        pad = _seg_rows(shape) * LANES - flat.shape[-1]
        if pad:
            flat = jnp.pad(flat, [(0, 0)] * len(lead) + [(0, pad)])
        parts.append(flat.reshape(lead + (-1, LANES)))
    rows = jnp.concatenate(parts, axis=len(lead))
    pad = total_rows - rows.shape[len(lead)]
    return jnp.pad(rows, [(0, 0)] * len(lead) + [(0, pad), (0, 0)])


def _unpack(rows, spec, lead=()):
    out, at = {}, 0
    for name, shape in spec:
        n = 1
        for s in shape:
            n *= s
        k = _seg_rows(shape)
        seg = lax.slice_in_dim(rows, at, at + k, axis=len(lead)).reshape(lead + (-1,))
        out[name] = lax.slice_in_dim(seg, 0, n, axis=len(lead)).reshape(lead + shape)
        at += k
    return out


def _split_owners(arr):
    a = arr.reshape(arr.shape[:-1] + (4, arr.shape[-1] // 4))
    return jnp.moveaxis(a, -2, 0)


def _merge_owners(arr):
    a = jnp.moveaxis(arr, 0, -2)
    return a.reshape(a.shape[:-2] + (-1,))


HBM_SPEC = pl.BlockSpec(memory_space=pltpu.HBM)


def _position():
    x, y, c = lax.axis_index("x"), lax.axis_index("y"), lax.axis_index("c")
    chips = [(1 - x, y), (x, 1 - y), (1 - x, 1 - y)]
    return x, y, c, chips


def _remote(src, dst, send_sem, recv_sem, device):
    return pltpu.make_async_remote_copy(src_ref=src, dst_ref=dst, send_sem=send_sem, recv_sem=recv_sem,
                                        device_id=device, device_id_type=MESH)


SEM_SPEC = pl.BlockSpec(memory_space=pltpu.SEMAPHORE)
SIDE_EFFECT = pltpu.SideEffectType.DATAFLOW_SIDE_EFFECTING


def _gather_copies(ins, lands, n_h, send_sems, recv_sems):
    x, y, c, chips = _position()
    me = 2 * x + y
    copies = []
    for a in range(len(ins)):
        for k, chip in enumerate(chips):
            src = ins[a].at[c] if a < n_h else ins[a]
            dst = lands[a].at[me, c] if a < n_h else lands[a].at[me]
            copies.append(_remote(src, dst, send_sems.at[3 * a + k], recv_sems.at[3 * a + k], (chip[0], chip[1], c)))
    return copies


def gather_start(halved, whole, name):
    arrays = list(halved) + list(whole)
    n, n_h = len(arrays), len(halved)
    lands = [lax.empty((4,) + a.shape, a.dtype) for a in arrays]

    def body(*refs):
        ins, lz, send_sems, recv_sems, token = refs[:n], refs[n:2 * n], refs[2 * n], refs[2 * n + 1], refs[-1]
        for cp in _gather_copies(ins, lz, n_h, send_sems, recv_sems):
            cp.start()
        token[...] = jnp.zeros_like(token)

    operands = [pltpu.with_memory_space_constraint(a, pltpu.HBM) for a in arrays + lands]
    return pl.pallas_call(
        body, name=name,
        out_shape=(pltpu.SemaphoreType.DMA((3 * n,)), pltpu.SemaphoreType.DMA((3 * n,)))
        + tuple(pltpu.HBM(a.shape, a.dtype) for a in operands) + (jax.ShapeDtypeStruct((SUBLANES, LANES), F32),),
        in_specs=[HBM_SPEC] * (2 * n),
        out_specs=(SEM_SPEC, SEM_SPEC) + (HBM_SPEC,) * (2 * n) + (pl.BlockSpec(memory_space=pltpu.VMEM),),
        input_output_aliases={i: 2 + i for i in range(2 * n)},
        compiler_params=pltpu.CompilerParams(has_side_effects=SIDE_EFFECT),
    )(*operands)


def gather_wait(started, n_h, after, name):
    send_sems, recv_sems = started[0], started[1]
    operands = list(started[2:-1])
    n = len(operands) // 2

    def body(*refs):
        ins, lz, send_ref, recv_ref = refs[:n], refs[n:2 * n], refs[2 * n], refs[2 * n + 1]
        for cp in _gather_copies(ins, lz, n_h, send_ref, recv_ref):
            cp.wait_send()
            cp.wait_recv()

    outs = pl.pallas_call(
        body, name=name,
        out_shape=tuple(pltpu.HBM(a.shape, a.dtype) for a in operands),
        in_specs=[HBM_SPEC] * (2 * n) + [SEM_SPEC, SEM_SPEC, pl.BlockSpec(memory_space=pl.ANY)],
        out_specs=(HBM_SPEC,) * (2 * n),
        input_output_aliases={i: i for i in range(2 * n)},
        compiler_params=pltpu.CompilerParams(has_side_effects=SIDE_EFFECT),
    )(*operands, send_sems, recv_sems, after)
    return outs[n:]


def pass_to_sibling(fulls, name):
    n = len(fulls)

    def body(*refs):
        bufs = refs[n:2 * n]
        send_sems, recv_sems = refs[2 * n:]
        x, y, c, chips = _position()
        sibling = (x, y, 1 - c)
        copies = []
        for a in range(n):
            for k, chip in enumerate(chips):
                q = 2 * chip[0] + chip[1]
                cp = _remote(bufs[a].at[q, c], bufs[a].at[q, c], send_sems.at[3 * a + k], recv_sems.at[3 * a + k],
                             sibling)
                cp.start()
                copies.append(cp)
        for a in range(n):
            for k, chip in enumerate(chips):
                q = 2 * chip[0] + chip[1]
                passed = bufs[a].at[q, 1 - c]
                _remote(passed, passed, send_sems.at[3 * a + k], recv_sems.at[3 * a + k], sibling).wait_recv()
        for cp in copies:
            cp.wait_send()

    return pl.pallas_call(
        body, name=name,
        out_shape=[jax.ShapeDtypeStruct(a.shape, a.dtype) for a in fulls],
        in_specs=[HBM_SPEC] * n, out_specs=[HBM_SPEC] * n,
        input_output_aliases={i: i for i in range(n)},
        scratch_shapes=[pltpu.SemaphoreType.DMA((3 * n,)), pltpu.SemaphoreType.DMA((3 * n,))],
    )(*fulls)


def place_own(full, own, chip, name):
    _, _, r, cols = full.shape
    tr = _row_tile(r, cols)

    def body(p_ref, own_ref, full_ref, o_ref):
        o_ref[0] = own_ref[...]

    return pl.pallas_call(
        body, name=name,
        out_shape=jax.ShapeDtypeStruct(full.shape, full.dtype),
        grid_spec=pltpu.PrefetchScalarGridSpec(
            num_scalar_prefetch=1, grid=(2, r // tr),
            in_specs=[pl.BlockSpec((1, tr, cols), lambda h, i, p_ref: (h, i, 0)), pl.BlockSpec(memory_space=pl.ANY)],
            out_specs=pl.BlockSpec((1, 1, tr, cols), lambda h, i, p_ref: (p_ref[0], h, i, 0))),
        input_output_aliases={2: 0},
        compiler_params=_params("parallel", "parallel"),
    )(chip, own, full)


def exchange_with_sibling(arrays, name):
    n = len(arrays)

    def body(*refs):
        ins, outs = refs[:n], refs[n:2 * n]
        send_sems, recv_sems = refs[2 * n:]
        x, y, c, _ = _position()
        copies = []
        for a in range(n):
            cp = _remote(ins[a].at[:, 1 - c], outs[a], send_sems.at[a], recv_sems.at[a], (x, y, 1 - c))
            cp.start()
            copies.append(cp)
        for cp in copies:
            cp.wait()

    return pl.pallas_call(
        body, name=name,
        out_shape=[jax.ShapeDtypeStruct((a.shape[0],) + a.shape[2:], a.dtype) for a in arrays],
        in_specs=[HBM_SPEC] * n, out_specs=[HBM_SPEC] * n,
        scratch_shapes=[pltpu.SemaphoreType.DMA((n,)), pltpu.SemaphoreType.DMA((n,))],
    )(*arrays)


def _chip_copies(ins, lands, send_sems, recv_sems):
    x, y, c, chips = _position()
    copies = []
    for a in range(len(ins)):
        for k, chip in enumerate(chips):
            q = 2 * chip[0] + chip[1]
            copies.append(_remote(ins[a].at[q], lands[a].at[k], send_sems.at[3 * a + k], recv_sems.at[3 * a + k],
                                  (chip[0], chip[1], c)))
    return copies


def exchange_with_chips_start(arrays, name):
    n = len(arrays)
    lands = [lax.empty((3,) + a.shape[1:], a.dtype) for a in arrays]

    def body(*refs):
        ins, lz, send_sems, recv_sems, token = refs[:n], refs[n:2 * n], refs[2 * n], refs[2 * n + 1], refs[-1]
        for cp in _chip_copies(ins, lz, send_sems, recv_sems):
            cp.start()
        token[...] = jnp.zeros_like(token)

    operands = [pltpu.with_memory_space_constraint(a, pltpu.HBM) for a in list(arrays) + lands]
    return pl.pallas_call(
        body, name=name,
        out_shape=(pltpu.SemaphoreType.DMA((3 * n,)), pltpu.SemaphoreType.DMA((3 * n,)))
        + tuple(pltpu.HBM(a.shape, a.dtype) for a in operands) + (jax.ShapeDtypeStruct((SUBLANES, LANES), F32),),
        in_specs=[HBM_SPEC] * (2 * n),
        out_specs=(SEM_SPEC, SEM_SPEC) + (HBM_SPEC,) * (2 * n) + (pl.BlockSpec(memory_space=pltpu.VMEM),),
        input_output_aliases={i: 2 + i for i in range(2 * n)},
        compiler_params=pltpu.CompilerParams(has_side_effects=SIDE_EFFECT),
    )(*operands)


def exchange_with_chips_wait(started, after, name):
    send_sems, recv_sems = started[0], started[1]
    operands = list(started[2:-1])
    n = len(operands) // 2

    def body(*refs):
        ins, lz, send_ref, recv_ref = refs[:n], refs[n:2 * n], refs[2 * n], refs[2 * n + 1]
        for cp in _chip_copies(ins, lz, send_ref, recv_ref):
            cp.wait_send()
            cp.wait_recv()

    outs = pl.pallas_call(
        body, name=name,
        out_shape=tuple(pltpu.HBM(a.shape, a.dtype) for a in operands),
        in_specs=[HBM_SPEC] * (2 * n) + [SEM_SPEC, SEM_SPEC, pl.BlockSpec(memory_space=pl.ANY)],
        out_specs=(HBM_SPEC,) * (2 * n),
        input_output_aliases={i: i for i in range(2 * n)},
        compiler_params=pltpu.CompilerParams(has_side_effects=SIDE_EFFECT),
    )(*operands, send_sems, recv_sems, after)
    return outs[:n], outs[n:]


def share_totals(totals, pack_total, last_part):
    arrays = list(totals) + [pack_total]
    n = len(arrays)

    def body(*refs):
        ins, last, outs, rep, last_all = refs[:n], refs[n], refs[n + 1:2 * n + 1], refs[2 * n + 1], refs[2 * n + 2]
        send_sems, recv_sems, rep_send, rep_recv, last_send, last_recv = refs[2 * n + 3:]
        x, y, c, chips = _position()
        sibling = (x, y, 1 - c)
        me = 4 * x + 2 * y + c
        sends = []
        for a in range(n):
            cp = _remote(ins[a], outs[a], send_sems.at[a], recv_sems.at[a], sibling)
            cp.start()
            sends.append(cp)
        mine = ins[n - 1].at[pl.ds(HALF_SHARDED, REP_PART)]
        peers = [sibling]
        for chip in chips:
            peers += [(chip[0], chip[1], c), (chip[0], chip[1], 1 - c)]
        for j, peer in enumerate(peers):
            for src, dst, s_sem, r_sem in ((mine, rep, rep_send, rep_recv), (last, last_all, last_send, last_recv)):
                cp = _remote(src, dst.at[me], s_sem.at[j], r_sem.at[j], peer)
                cp.start()
                sends.append(cp)
        for a in range(n):
            _remote(outs[a], outs[a], send_sems.at[a], recv_sems.at[a], sibling).wait_recv()
        for j, peer in enumerate(peers):
            it = 4 * peer[0] + 2 * peer[1] + peer[2]
            _remote(rep.at[it], rep.at[it], rep_send.at[j], rep_recv.at[j], peer).wait_recv()
            _remote(last_all.at[it], last_all.at[it], last_send.at[j], last_recv.at[j], peer).wait_recv()
        for cp in sends:
            cp.wait_send()

    outs = pl.pallas_call(
        body, name="grad_share_totals",
        out_shape=[jax.ShapeDtypeStruct(a.shape, a.dtype) for a in arrays]
        + [jax.ShapeDtypeStruct((8, REP_PART, LANES), F32), jax.ShapeDtypeStruct((8,) + last_part.shape, F32)],
        in_specs=[HBM_SPEC] * (n + 1), out_specs=[HBM_SPEC] * (n + 2),
        scratch_shapes=[pltpu.SemaphoreType.DMA((n,)), pltpu.SemaphoreType.DMA((n,))]
        + [pltpu.SemaphoreType.DMA((7,))] * 4,
    )(*arrays, last_part)
    return outs[:n], outs[n], outs[n + 1]


def sum_parts(parts, name):
    def body(p_ref, o_ref):
        total = p_ref[0]
        for k in range(1, parts.shape[0]):
            total = total + p_ref[k]
        o_ref[...] = total

    return pl.pallas_call(body, name=name, out_shape=jax.ShapeDtypeStruct(parts.shape[1:], parts.dtype))(parts)


TILE_BYTES = 2 << 20


def _row_tile(rows, cols):
    best = None
    for t in range(SUBLANES, rows + 1, SUBLANES):
        if rows % t == 0 and t * cols * 4 <= TILE_BYTES:
            best = t
    return best if best is not None else rows


def add_sibling(mine, received, core, out_dtype, name):
    _, _, r, cols = mine.shape
    tr = _row_tile(r, cols)

    def body(c_ref, a_ref, b_ref, o_ref):
        o_ref[...] = (a_ref[0] + b_ref[...]).astype(out_dtype)

    return pl.pallas_call(
        body, name=name,
        out_shape=jax.ShapeDtypeStruct((4, r, cols), out_dtype),
        grid_spec=pltpu.PrefetchScalarGridSpec(
            num_scalar_prefetch=1, grid=(4, r // tr),
            in_specs=[pl.BlockSpec((1, 1, tr, cols), lambda o, i, c_ref: (o, c_ref[0], i, 0)),
                      pl.BlockSpec((1, tr, cols), lambda o, i, c_ref: (o, i, 0))],
            out_specs=pl.BlockSpec((1, tr, cols), lambda o, i, c_ref: (o, i, 0))),
        compiler_params=_params("parallel", "parallel"),
    )(core, mine, received)


def add_chips(own, received, chip, name):
    _, r, cols = own.shape
    tr = _row_tile(r, cols)

    def body(p_ref, a_ref, b0, b1, b2, o_ref):
        o_ref[...] = ((a_ref[0].astype(F32) + b0[0].astype(F32)) + b1[0].astype(F32)) + b2[0].astype(F32)

    rb = lambda k: pl.BlockSpec((1, tr, cols), lambda i, p_ref: (k, i, 0))
    return pl.pallas_call(
        body, name=name,
        out_shape=jax.ShapeDtypeStruct((r, cols), F32),
        grid_spec=pltpu.PrefetchScalarGridSpec(
            num_scalar_prefetch=1, grid=(r // tr,),
            in_specs=[pl.BlockSpec((1, tr, cols), lambda i, p_ref: (p_ref[0], i, 0)), rb(0), rb(1), rb(2)],
            out_specs=pl.BlockSpec((tr, cols), lambda i, p_ref: (i, 0))),
        compiler_params=_params("parallel"),
    )(chip, own, received, received, received)


def _adamw_update(gv, w_ref, m_ref, v_ref, d_ref, nm_ref, nv_ref):
    nm = ADAM_B1 * m_ref[...] + (1.0 - ADAM_B1) * gv
    nv = ADAM_B2 * v_ref[...] + (1.0 - ADAM_B2) * (gv * gv)
    nm_ref[...] = nm
    nv_ref[...] = nv
    m_hat = nm / (1.0 - ADAM_B1 ** ADAM_STEP)
    v_hat = nv / (1.0 - ADAM_B2 ** ADAM_STEP)
    d_ref[...] = -ADAM_LR * (m_hat / (jnp.sqrt(v_hat) + ADAM_EPS) + ADAM_WD * w_ref[...])


def adamw_halves(w, own, received, m, v, core, name, by_columns=False):
    rows, cols = w.shape

    def body(c_ref, w_ref, own_ref, rec_ref, m_ref, v_ref, g_ref, d_ref, nm_ref, nv_ref):
        gv = jnp.where(pl.program_id(0) == c_ref[0], own_ref[...], rec_ref[...])
        g_ref[...] = gv
        _adamw_update(gv, w_ref, m_ref, v_ref, d_ref, nm_ref, nv_ref)

    if by_columns:
        nr = 1
        whole = pl.BlockSpec((rows, cols // 2), lambda h, i, c_ref: (0, h))
        half = pl.BlockSpec((rows, cols // 2), lambda h, i, c_ref: (0, 0))
    else:
        r = rows // 2
        tr = _row_tile(r, cols)
        nr = r // tr
        whole = pl.BlockSpec((tr, cols), lambda h, i, c_ref: (h * nr + i, 0))
        half = pl.BlockSpec((tr, cols), lambda h, i, c_ref: (i, 0))
    return pl.pallas_call(
        body, name=name,
        out_shape=(jax.ShapeDtypeStruct((rows, cols), F32),) * 4,
        grid_spec=pltpu.PrefetchScalarGridSpec(
            num_scalar_prefetch=1, grid=(2, nr),
            in_specs=[whole, half, half, whole, whole], out_specs=(whole,) * 4),
        compiler_params=_params("parallel", "parallel"),
    )(core, w, own, received, m, v)


def adamw(w, g, m, v, name):
    r, cols = w.shape
    tr = _row_tile(r, cols)

    def body(w_ref, g_ref, m_ref, v_ref, g_out, d_ref, nm_ref, nv_ref):
        gv = g_ref[...]
        g_out[...] = gv
        _adamw_update(gv, w_ref, m_ref, v_ref, d_ref, nm_ref, nv_ref)

    blk = pl.BlockSpec((tr, cols), lambda i: (i, 0))
    return pl.pallas_call(
        body, name=name,
        out_shape=(jax.ShapeDtypeStruct((r, cols), F32),) * 4,
        grid=(r // tr,),
        in_specs=[blk] * 4, out_specs=(blk,) * 4,
        compiler_params=_params("parallel"),
    )(w, g, m, v)


WEIGHTS = ("even_norm_pre", "even_norm_post", "even_w_in", "rg_conv_w", "rg_conv_b", "rg_gate_w", "rg_gate_b",
           "rg_lambda", "sc_conv_w", "even_w_out", "odd_norm_pre", "odd_norm_post", "odd_w_in", "gla_w_gate_lr",
           "gla_b_gate", "gla_norm_g", "odd_w_out")
BIG = ("even_w_in", "even_w_out", "odd_w_in", "odd_w_out")


def _halves(a):
    return a.reshape((2, a.shape[0] // 2) + a.shape[1:])


def kernel(x, even_norm_pre, even_norm_post, even_w_in, rg_conv_w, rg_conv_b, rg_gate_w, rg_gate_b, rg_lambda, sc_conv_w, even_w_out, odd_norm_pre, odd_norm_post, odd_w_in, gla_w_gate_lr, gla_b_gate, gla_norm_g, odd_w_out, loss_target, m_even_norm_pre, m_even_norm_post, m_even_w_in, m_rg_conv_w, m_rg_conv_b, m_rg_gate_w, m_rg_gate_b, m_rg_lambda, m_sc_conv_w, m_even_w_out, m_odd_norm_pre, m_odd_norm_post, m_odd_w_in, m_gla_w_gate_lr, m_gla_b_gate, m_gla_norm_g, m_odd_w_out, v_even_norm_pre, v_even_norm_post, v_even_w_in, v_rg_conv_w, v_rg_conv_b, v_rg_gate_w, v_rg_gate_b, v_rg_lambda, v_sc_conv_w, v_even_w_out, v_odd_norm_pre, v_odd_norm_post, v_odd_w_in, v_gla_w_gate_lr, v_gla_b_gate, v_gla_norm_g, v_odd_w_out):
    given = dict(locals())
    shard = {n: given[n][0] for n in WEIGHTS}
    m_in = {n: given["m_" + n][0] for n in WEIGHTS}
    v_in = {n: given["v_" + n][0] for n in WEIGHTS}
    mx, my, mc = lax.axis_index("x"), lax.axis_index("y"), lax.axis_index("c")
    core = jnp.reshape(mc, (1,)).astype(jnp.int32)
    chip = jnp.reshape(2 * mx + my, (1,)).astype(jnp.int32)

    small_shard = _pack(shard, SHARDED_SMALL, SHARDED_ROWS)
    big_own = [_halves(shard[n].astype(BF16)) for n in BIG]
    started_a = gather_start(big_own[:1], [small_shard], "gather_start_a")
    started_b = gather_start(big_own[1:], [], "gather_start_b")
    even_w_in_full, small_full = gather_wait(started_a, 1, started_b[-1], "gather_wait_a")
    (even_w_in_full,) = pass_to_sibling([even_w_in_full], "gather_pass_a")
    even_w_in_full = place_own(even_w_in_full, big_own[0], chip, "place_even_w_in")
    small_full = lax.dynamic_update_slice(small_full, small_shard[None], (chip[0], 0, 0))
    full = {n: shard[n] for n, _ in REPLICATED + LAST_REPLICATED}
    full.update({n: _merge_owners(a) for n, a in _unpack(small_full, SHARDED_SMALL, lead=(4,)).items()})
    full["even_w_in"] = even_w_in_full.reshape(4, D_MODEL, EVEN_IN // 4)

    def late_weights(after):
        lands = pass_to_sibling(list(gather_wait(started_b, 3, after, "gather_wait_b")), "gather_pass_b")
        lands = [place_own(a, b, chip, "place_" + n) for a, b, n in zip(lands, big_own[1:], BIG[1:])]
        odd_w_in = jnp.transpose(lands[1].reshape(4, D_MODEL, ODD_IN // 4), (1, 0, 2)).reshape(D_MODEL, ODD_IN)
        return _prepare_weights({"even_w_out": lands[0].reshape(2 * D_MODEL, D_MODEL), "odd_w_in": odd_w_in,
                                 "odd_w_out": lands[2].reshape(D_MODEL, D_MODEL)})

    pending = {}

    def slab(a):
        return a.reshape((4, 2, a.shape[1] // 2) + a.shape[2:])

    def begin(tag, slabs, dtypes):
        got = exchange_with_sibling(slabs, "grad_sibling_" + tag)
        sums = [add_sibling(a, b, core, dt, "grad_add_sibling_%s%d" % (tag, i))
                for i, (a, b, dt) in enumerate(zip(slabs, got, dtypes))]
        pending[tag] = exchange_with_chips_start(sums, "grad_chips_start_" + tag)
        return pending[tag][-1][0, 0]

    def finish(tag, after):
        sums, got = exchange_with_chips_wait(pending[tag], after, "grad_chips_wait_" + tag)
        return [add_chips(a, b, chip, "grad_add_chips_%s%d" % (tag, i)) for i, (a, b) in enumerate(zip(sums, got))]

    def reduce_first(g):
        return begin("a", [slab(jnp.transpose(g["odd_w_in"].reshape(D_MODEL, 4, ODD_IN // 4), (1, 0, 2))),
                           slab(g["odd_w_out"].reshape(4, D_MODEL // 4, D_MODEL)),
                           slab(g["even_w_out"].reshape(4, D_MODEL // 2, D_MODEL))], [BF16] * 3)

    def reduce_second(g):
        pending["totals_a"] = finish("a", g["even_w_in"])
        rep_rows = _pack(g, REPLICATED, REPLICATED_ROWS).reshape(4, 2, REP_PART, LANES)
        sh_rows = _pack({n: _split_owners(g[n]) for n, _ in SHARDED_SMALL}, SHARDED_SMALL, SHARDED_ROWS, lead=(4,))
        pack = jnp.concatenate([sh_rows.reshape(4, 2, HALF_SHARDED, LANES), rep_rows], axis=2)
        return begin("b", [slab(g["even_w_in"]), pack], [BF16, F32])

    loss, grad_x, g = local_step(x[0], loss_target[0], _prepare_weights(full), reduce_first, reduce_second,
                                 late_weights)
    odd_w_in_t, odd_w_out_t, even_w_out_t = pending["totals_a"]
    even_w_in_t, pack_t = finish("b", grad_x)
    totals = [even_w_in_t, even_w_out_t, odd_w_in_t, odd_w_out_t]
    last_part = jnp.concatenate([_pack(g, LAST_REPLICATED, LAST_ROWS), loss])
    from_core, rep_all, last_all = share_totals(totals, pack_t, last_part)
    me = 2 * chip[0] + core[0]
    mine, theirs = pack_t[:HALF_SHARDED], from_core[4][:HALF_SHARDED]
    sh_total = jnp.where(mc == 0, jnp.concatenate([mine, theirs]), jnp.concatenate([theirs, mine]))
    rep_all = lax.dynamic_update_slice(rep_all, pack_t[None, HALF_SHARDED:], (me, 0, 0))
    rep_total = rep_all.reshape(REPLICATED_ROWS, LANES)
    last_total = sum_parts(lax.dynamic_update_slice(last_all, last_part[None], (me, 0, 0)), "grad_sum_last")
    last_total, loss = last_total[:LAST_ROWS], last_total[LAST_ROWS, 0]
    grads = {}

    delta, new_m, new_v = {}, {}, {}
    for i, n in enumerate(BIG):
        if shard[n].shape[1] % LANES:
            outs = adamw_halves(shard[n].T, totals[i].T, from_core[i].T, m_in[n].T, v_in[n].T, core, "adamw_" + n,
                                by_columns=True)
            grads[n], delta[n], new_m[n], new_v[n] = [o.T for o in outs]
        else:
            grads[n], delta[n], new_m[n], new_v[n] = adamw_halves(shard[n], totals[i], from_core[i], m_in[n],
                                                                  v_in[n], core, "adamw_" + n)
    gate = [src["rg_gate_w"].reshape(GATE_ROWS, LANES) for src in (shard, m_in, v_in)]
    grads["rg_gate_w"], delta["rg_gate_w"], new_m["rg_gate_w"], new_v["rg_gate_w"] = adamw(
        gate[0], rep_total, gate[1], gate[2], "adamw_rg_gate_w")
    rest = REPLICATED[1:]
    rest_rows = sum(_seg_rows(shape) for _, shape in rest)
    small = ((SHARDED_SMALL, SHARDED_ROWS), (rest, rest_rows), (LAST_REPLICATED, LAST_ROWS))
    packed = [jnp.concatenate([_pack(src, spec, rows) for spec, rows in small]) for src in (shard, m_in, v_in)]
    small_g = jnp.concatenate([sh_total, rep_total[GATE_ROWS:GATE_ROWS + rest_rows], last_total], axis=0)
    outs = adamw(packed[0], small_g, packed[1], packed[2], "adamw_small")
    for dst, packed_rows in zip((grads, delta, new_m, new_v), outs):
        at = 0
        for spec, rows in small:
            dst.update(_unpack(packed_rows[at:at + rows], spec))
            at += rows
    result = [loss, grad_x[None]]
    for group in (grads, delta, new_m, new_v):
        result += [group[n].reshape(given[n].shape) for n in WEIGHTS]
    return tuple(result)
```

```python
import functools

import jax
import jax.numpy as jnp
from jax import lax
from jax.experimental import pallas as pl
from jax.experimental.pallas import tpu as pltpu

F32 = jnp.float32
BF16 = jnp.bfloat16
MESH = pl.DeviceIdType.MESH

D_MODEL = 1024
NORM_EPS = 1e-6
RG_HEADS = 8
RG_HEAD_DIM = 128
RG_C = 8.0
EVEN_IN = 6144
ODD_IN = 3104
ODD_IN_PAD = 3200
GLA_HEADS = 4
GLA_DK = 128
GLA_DV = 256
GLA_RANK = 16
GLA_NORMALIZER = 16.0
GLA_CHUNK = 128
LR_COL = 3072

ADAM_LR = 0.001
ADAM_B1 = 0.9
ADAM_B2 = 0.999
ADAM_EPS = 1e-08
ADAM_WD = 0.01
ADAM_STEP = 10

SUBLANES = 8
LANES = 128
VMEM_LIMIT = 56 * 2 ** 20

ROW_TILE = 512
SCAN_TILE = 256
GLA_BLOCK = 1024
MIX_TILE = 128


def _params(*sem):
    return pltpu.CompilerParams(dimension_semantics=sem, vmem_limit_bytes=VMEM_LIMIT)


def _full(shape):
    n = len(shape)
    return pl.BlockSpec(shape, lambda *_: (0,) * n)


def _sigmoid(x):
    return 0.5 + 0.5 * jnp.tanh(0.5 * x)


def _softplus(x):
    return jnp.maximum(x, 0.0) + jnp.log(1.0 + jnp.exp(-jnp.abs(x)))


def _dot(a, b):
    return jnp.dot(a, b, preferred_element_type=F32)


def _dot_nt(a, b):
    return lax.dot_general(a, b, (((1,), (1,)), ((), ())), preferred_element_type=F32)


def _dot_tn(a, b):
    return lax.dot_general(a, b, (((0,), (0,)), ((), ())), preferred_element_type=F32)


def _bdot(a, b, ca, cb):
    return lax.dot_general(a, b, (((ca,), (cb,)), ((0,), (0,))), preferred_element_type=F32)


def _halo_specs(rows, cols, col_block, n_row_tiles, tix):
    per = rows // SUBLANES
    last = n_row_tiles * per - 1

    def split(args):
        if len(args) == 2:
            return tix(args[1]), col_block + args[0]
        return tix(args[0]), col_block

    def prev(*args):
        t, c = split(args)
        return (jnp.maximum(t * per - 1, 0), c)

    def main(*args):
        return split(args)

    def nxt(*args):
        t, c = split(args)
        return (jnp.minimum((t + 1) * per, last), c)

    return [pl.BlockSpec((SUBLANES, cols), prev), pl.BlockSpec((rows, cols), main),
            pl.BlockSpec((SUBLANES, cols), nxt)]


def _extend(prev_ref, main_ref, next_ref, is_first, is_last):
    p = jnp.where(is_first, 0.0, prev_ref[...])
    n = jnp.where(is_last, 0.0, next_ref[...])
    return jnp.concatenate([p, main_ref[...], n], axis=0)


def _shifted(ext, offset, rows):
    if offset == 0:
        return ext[SUBLANES:SUBLANES + rows]
    n = ext.shape[0]
    return pltpu.roll(ext, (-offset) % n, 0)[SUBLANES:SUBLANES + rows]


def _conv(ext, w, left, rows):
    out = None
    for k in range(w.shape[0]):
        term = _shifted(ext, k - left, rows) * w[k:k + 1]
        out = term if out is None else out + term
    return out


def _conv_transpose(ext, w, left, rows):
    out = None
    for k in range(w.shape[0]):
        term = _shifted(ext, left - k, rows) * w[k:k + 1]
        out = term if out is None else out + term
    return out


def _colsum(x):
    return jnp.sum(x, axis=0, keepdims=True)


def _accumulate(ref, value, step):
    @pl.when(step == 0)
    def _():
        ref[...] = value

    @pl.when(step > 0)
    def _():
        ref[...] += value


PROJ_TILE_BYTES = 7 * 2 ** 20


def _proj_row_tile(rows, width):
    tm = min(ROW_TILE, rows)
    while tm * width * 4 > PROJ_TILE_BYTES and tm % (2 * SUBLANES) == 0:
        tm //= 2
    return tm


def norm_matmul(x, gain, w, name):
    rows, d = x.shape
    n_col_tiles, _, tn = w.shape
    tm = _proj_row_tile(rows, n_col_tiles * tn)

    def body(x_ref, g_ref, w_ref, proj_ref, h_ref):
        xv = x_ref[...]
        rstd = lax.rsqrt(jnp.mean(xv * xv, axis=-1, keepdims=True) + NORM_EPS)
        hv = (xv * rstd * g_ref[...]).astype(BF16)
        h_ref[...] = hv
        for j in range(n_col_tiles):
            proj_ref[:, j * tn:(j + 1) * tn] = _dot(hv, w_ref[j])

    row = lambda cols: pl.BlockSpec((tm, cols), lambda i: (i, 0))
    return pl.pallas_call(
        body, name=name,
        out_shape=(jax.ShapeDtypeStruct((rows, n_col_tiles * tn), F32), jax.ShapeDtypeStruct((rows, d), BF16)),
        grid=(rows // tm,),
        in_specs=[row(d), _full((1, d)), _full(w.shape)],
        out_specs=(row(n_col_tiles * tn), row(d)),
        compiler_params=_params("parallel"),
    )(x, gain, w)


def inproj_bwd(dproj, w, x, gain, dres, name):
    rows, d = x.shape
    n_col_tiles, _, tn = w.shape
    tm = _proj_row_tile(rows, n_col_tiles * tn)

    def body(dp_ref, w_ref, x_ref, g_ref, dres_ref, dx_ref, dg_ref):
        dh = None
        for j in range(n_col_tiles):
            part = _dot_nt(dp_ref[:, j * tn:(j + 1) * tn], w_ref[j])
            dh = part if dh is None else dh + part
        _inproj_finish(dh, x_ref, g_ref, dres_ref, dx_ref, dg_ref, pl.program_id(0))

    row = lambda cols: pl.BlockSpec((tm, cols), lambda i: (i, 0))
    return pl.pallas_call(
        body, name=name,
        out_shape=(jax.ShapeDtypeStruct((rows, d), F32), jax.ShapeDtypeStruct((1, d), F32)),
        grid=(rows // tm,),
        in_specs=[row(n_col_tiles * tn), _full(w.shape), row(d), _full((1, d)), row(d)],
        out_specs=(row(d), _full((1, d))),
        compiler_params=_params("arbitrary"),
    )(dproj, w, x, gain, dres)


def _inproj_finish(dh, x_ref, g_ref, dres_ref, dx_ref, dg_ref, step):
    xv = x_ref[...]
    rstd = lax.rsqrt(jnp.mean(xv * xv, axis=-1, keepdims=True) + NORM_EPS)
    xhat = xv * rstd
    dxn = dh * g_ref[...]
    dx_ref[...] = dres_ref[...] + rstd * (dxn - xhat * jnp.mean(dxn * xhat, axis=-1, keepdims=True))
    _accumulate(dg_ref, _colsum(dh * xhat), step)


def inproj_bwd_pieces(pieces, w, x, gain, dres, name):
    rows, d = x.shape
    tm = min(ROW_TILE, rows)
    n = len(pieces)
    widths = [p.shape[1] for p in pieces]
    starts = [sum(widths[:k]) for k in range(n)]
    assert sum(widths) == w.shape[2]

    def body(*refs):
        w_ref, x_ref, g_ref, dres_ref, dx_ref, dg_ref = refs[n:]
        dh = None
        for k in range(n):
            part = _dot_nt(refs[k][...], w_ref[0, :, starts[k]:starts[k] + widths[k]])
            dh = part if dh is None else dh + part
        _inproj_finish(dh, x_ref, g_ref, dres_ref, dx_ref, dg_ref, pl.program_id(0))

    row = lambda cols: pl.BlockSpec((tm, cols), lambda i: (i, 0))
    return pl.pallas_call(
        body, name=name,
        out_shape=(jax.ShapeDtypeStruct((rows, d), F32), jax.ShapeDtypeStruct((1, d), F32)),
        grid=(rows // tm,),
        in_specs=[row(wd) for wd in widths] + [_full(w.shape), row(d), _full((1, d)), row(d)],
        out_specs=(row(d), _full((1, d))),
        compiler_params=_params("arbitrary"),
    )(*pieces, w, x, gain, dres)


def matmul_dw_pieces(a, pieces, name):
    rows, m = a.shape
    tk = min(2 * ROW_TILE, rows)
    n = len(pieces)

    def body(*refs):
        a_ref, ins, outs = refs[0], refs[1:1 + n], refs[1 + n:]
        av = a_ref[...]
        for k in range(n):
            _accumulate(outs[k], _dot_tn(av, ins[k][...]), pl.program_id(0))

    return pl.pallas_call(
        body, name=name,
        out_shape=[jax.ShapeDtypeStruct((m, p.shape[1]), F32) for p in pieces],
        grid=(rows // tk,),
        in_specs=[pl.BlockSpec((tk, m), lambda k: (k, 0))]
        + [pl.BlockSpec((tk, p.shape[1]), lambda k: (k, 0)) for p in pieces],
        out_specs=[_full((m, p.shape[1])) for p in pieces],
        compiler_params=_params("arbitrary"),
    )(a, *pieces)


def matmul_dw(a, b, bn, name):
    rows, m = a.shape
    n = b.shape[1]
    tk = min(4 * ROW_TILE, rows)
    steps = rows // tk

    def body(a_ref, b_ref, o_ref):
        part = _dot_tn(a_ref[...], b_ref[...])

        @pl.when(pl.program_id(1) == 0)
        def _():
            o_ref[0] = part

        @pl.when(pl.program_id(1) > 0)
        def _():
            o_ref[0] += part

    return pl.pallas_call(
        body, name=name,
        out_shape=jax.ShapeDtypeStruct((n // bn, m, bn), F32),
        grid=(n // bn, steps),
        in_specs=[pl.BlockSpec((tk, m), lambda j, k: (k, 0)), pl.BlockSpec((tk, bn), lambda j, k: (k, j))],
        out_specs=pl.BlockSpec((1, m, bn), lambda j, k: (j, 0, 0)),
        compiler_params=_params("parallel", "arbitrary"),
    )(a, b)


def _scan(a, b, carry, reverse):
    n, c = a.shape
    blocks = n // SUBLANES
    a = a.reshape(blocks, SUBLANES, c)
    b = b.reshape(blocks, SUBLANES, c)
    pos = lax.broadcasted_iota(jnp.int32, (1, SUBLANES, c), 1)
    s = 1
    while s < SUBLANES:
        shift, valid = (SUBLANES - s, pos < SUBLANES - s) if reverse else (s, pos >= s)
        a_s, b_s = pltpu.roll(a, shift, 1), pltpu.roll(b, shift, 1)
        b = jnp.where(valid, a * b_s + b, b)
        a = jnp.where(valid, a * a_s, a)
        s *= 2
    out = [None] * blocks
    for k in (range(blocks - 1, -1, -1) if reverse else range(blocks)):
        h = a[k] * carry + b[k]
        out[k] = h
        carry = h[0:1] if reverse else h[SUBLANES - 1:SUBLANES]
    return jnp.concatenate(out, axis=0)


def _rg_gates(ua, gw_ref, gb, lam):
    ub = ua.astype(BF16)
    pre_r, pre_i = [], []
    for h in range(RG_HEADS):
        z = _dot(ub[:, h * RG_HEAD_DIM:(h + 1) * RG_HEAD_DIM], gw_ref[h])
        pre_r.append(z[:, :RG_HEAD_DIM])
        pre_i.append(z[:, RG_HEAD_DIM:])
    r = _sigmoid(jnp.concatenate(pre_r, axis=1) + gb[0:1])
    i = _sigmoid(jnp.concatenate(pre_i, axis=1) + gb[1:2])
    sp = _softplus(-lam)
    log_a = -RG_C * r * sp
    a = jnp.exp(log_a)
    mult = jnp.sqrt(1.0 - a * a)
    return r, i, sp, a, mult


def _rg_weight_specs():
    return [_full((4, D_MODEL)), _full((1, D_MODEL)), _full((RG_HEADS, RG_HEAD_DIM, 2 * RG_HEAD_DIM)),
            _full((2, D_MODEL)), _full((1, D_MODEL))]


def rglru_fwd(proj, conv_w, conv_b, gate_w, gate_b, lam, reverse, name):
    rows_total = proj.shape[0]
    rows = min(SCAN_TILE, rows_total)
    n_tiles = rows_total // rows
    tix = (lambda i: n_tiles - 1 - i) if reverse else (lambda i: i)

    def body(xp, xm, xn, cw_ref, cb_ref, gw_ref, gb_ref, lam_ref, h_ref, acts_ref, carry):
        i = pl.program_id(0)
        t = tix(i)
        ext = _extend(xp, xm, xn, t == 0, t == n_tiles - 1)
        ua = _conv(ext, cw_ref[...], 2, rows) + cb_ref[...]
        r, gi, _, a, mult = _rg_gates(ua, gw_ref, gb_ref[...], lam_ref[...])
        for k, saved in enumerate((ua, r, gi, a, mult)):
            acts_ref[k] = saved
        b = mult * (gi * ua)

        @pl.when(i == 0)
        def _():
            carry[...] = jnp.zeros_like(carry)

        h = _scan(a, b, carry[0:1], reverse)
        h_ref[...] = h
        edge = h[0:1] if reverse else h[rows - 1:rows]
        carry[...] = jnp.broadcast_to(edge, carry.shape)

    return pl.pallas_call(
        body, name=name,
        out_shape=(jax.ShapeDtypeStruct((rows_total, D_MODEL), F32),
                   jax.ShapeDtypeStruct((5, rows_total, D_MODEL), F32)),
        grid=(n_tiles,),
        in_specs=_halo_specs(rows, D_MODEL, 0, n_tiles, tix) + _rg_weight_specs(),
        out_specs=(pl.BlockSpec((rows, D_MODEL), lambda i: (tix(i), 0)),
                   pl.BlockSpec((5, rows, D_MODEL), lambda i: (0, tix(i), 0))),
        scratch_shapes=[pltpu.VMEM((SUBLANES, D_MODEL), F32)],
        compiler_params=_params("arbitrary"),
    )(proj, proj, proj, conv_w, conv_b, gate_w, gate_b, lam)


def rglru_bwd(proj, dycat, h_dir, acts, gate_w, lam, add_dua, reverse, name):
    rows_total = proj.shape[0]
    rows = min(SCAN_TILE, rows_total)
    n_tiles = rows_total // rows
    tix = (lambda i: i) if reverse else (lambda i: n_tiles - 1 - i)
    za_block = 1

    def body(acts_ref, za_ref, dya_ref, hp, hm, hn, gw_ref, lam_ref, *rest):
        other = rest[0][...] if add_dua is not None else 0.0
        dua_ref, dgw_ref, dgb_ref, dlam_ref, carry = rest[-5:]
        step = pl.program_id(0)
        t = tix(step)
        first, last = t == 0, t == n_tiles - 1
        ua, r, gi, a, mult = (acts_ref[k] for k in range(5))
        lam_v = lam_ref[...]
        sp = _softplus(-lam_v)
        za = za_ref[...]
        dh = dya_ref[...] * (za * _sigmoid(za))

        @pl.when(step == 0)
        def _():
            carry[...] = jnp.zeros_like(carry)

        old = carry[0:1]
        mu = _scan(a, a * dh, old, not reverse)
        row = lax.broadcasted_iota(jnp.int32, mu.shape, 0)
        if reverse:
            mu_next = jnp.where(row == 0, old, pltpu.roll(mu, 1, 0))
            carry[...] = jnp.broadcast_to(mu[rows - 1:rows], carry.shape)
            h_ext = _extend(hp, hm, hn, first, last)
            h_prev = _shifted(h_ext, 1, rows)
        else:
            mu_next = jnp.where(row == rows - 1, old, pltpu.roll(mu, rows - 1, 0))
            carry[...] = jnp.broadcast_to(mu[0:1], carry.shape)
            h_ext = _extend(hp, hm, hn, first, last)
            h_prev = _shifted(h_ext, -1, rows)
        db = dh + mu_next
        da = db * h_prev
        d_mult = db * (gi * ua)
        di = db * (mult * ua)
        dua = db * (mult * gi)
        dlog_a = da * a - d_mult * (a * a) / mult
        dr = dlog_a * (-RG_C * sp)
        dlam = _colsum(dlog_a * (-RG_C * r)) * (-_sigmoid(-lam_v))
        dpr = dr * (r * (1.0 - r))
        dpi = di * (gi * (1.0 - gi))
        dgb = jnp.concatenate([_colsum(dpr), _colsum(dpi)], axis=0)
        ub = ua.astype(BF16)
        dua_heads, dgw_heads = [], []
        for h in range(RG_HEADS):
            cols = slice(h * RG_HEAD_DIM, (h + 1) * RG_HEAD_DIM)
            dz = jnp.concatenate([dpr[:, cols], dpi[:, cols]], axis=1).astype(BF16)
            dgw_heads.append(_dot_tn(ub[:, cols], dz))
            dua_heads.append(_dot_nt(dz, gw_ref[h]))
        dua_ref[...] = dua + jnp.concatenate(dua_heads, axis=1) + other

        @pl.when(step == 0)
        def _():
            for h in range(RG_HEADS):
                dgw_ref[h] = dgw_heads[h]
            dgb_ref[...] = dgb
            dlam_ref[...] = dlam

        @pl.when(step > 0)
        def _():
            for h in range(RG_HEADS):
                dgw_ref[h] += dgw_heads[h]
            dgb_ref[...] += dgb
            dlam_ref[...] += dlam

    row_spec = lambda col: pl.BlockSpec((rows, D_MODEL), lambda i: (tix(i), col))
    return pl.pallas_call(
        body, name=name,
        out_shape=(jax.ShapeDtypeStruct((rows_total, D_MODEL), F32),
                   jax.ShapeDtypeStruct((RG_HEADS, RG_HEAD_DIM, 2 * RG_HEAD_DIM), F32),
                   jax.ShapeDtypeStruct((2, D_MODEL), F32), jax.ShapeDtypeStruct((1, D_MODEL), F32)),
        grid=(n_tiles,),
        in_specs=([pl.BlockSpec((5, rows, D_MODEL), lambda i: (0, tix(i), 0)), row_spec(za_block), row_spec(0)]
                  + _halo_specs(rows, D_MODEL, 0, n_tiles, tix)
                  + [_full((RG_HEADS, RG_HEAD_DIM, 2 * RG_HEAD_DIM)), _full((1, D_MODEL))]
                  + ([] if add_dua is None else [row_spec(0)])),
        out_specs=(row_spec(0), _full((RG_HEADS, RG_HEAD_DIM, 2 * RG_HEAD_DIM)), _full((2, D_MODEL)),
                   _full((1, D_MODEL))),
        scratch_shapes=[pltpu.VMEM((SUBLANES, D_MODEL), F32)],
        compiler_params=_params("arbitrary"),
    )(acts, proj, dycat, h_dir, h_dir, h_dir, gate_w, lam, *([] if add_dua is None else [add_dua]))


def even_mix_fwd(proj, h_f, h_b, sc_w, name):
    rows_total = proj.shape[0]
    rows = min(2 * MIX_TILE, rows_total)
    n_tiles = rows_total // rows
    cb = D_MODEL
    n_cb = 1
    ident = lambda i: i

    def body(za_ref, hf_ref, hb_ref, xbp, xbm, xbn, gcp, gcm, gcn, gb_ref, zb_ref, w_ref, y_ref):
        t = pl.program_id(1)
        first, last = t == 0, t == n_tiles - 1
        za = za_ref[...]
        y_ref[:, 0:cb] = ((hf_ref[...] + hb_ref[...]) * (za * _sigmoid(za))).astype(BF16)
        p_ext = _extend(xbp, xbm, xbn, first, last) * _extend(gcp, gcm, gcn, first, last)
        cv = _conv(p_ext, w_ref[...], 1, rows)
        zb = zb_ref[...]
        y_ref[:, cb:2 * cb] = (gb_ref[...] * cv * (zb * _sigmoid(zb))).astype(BF16)

    blk = lambda col: pl.BlockSpec((rows, cb), lambda c, i: (i, col * n_cb + c))
    own = pl.BlockSpec((rows, cb), lambda c, i: (i, c))
    return pl.pallas_call(
        body, name=name,
        out_shape=jax.ShapeDtypeStruct((rows_total, 2 * D_MODEL), BF16),
        grid=(n_cb, n_tiles),
        in_specs=([blk(1), own, own] + _halo_specs(rows, cb, 2 * n_cb, n_tiles, ident)
                  + _halo_specs(rows, cb, 4 * n_cb, n_tiles, ident)
                  + [blk(3), blk(5), pl.BlockSpec((3, cb), lambda c, i: (0, c))]),
        out_specs=pl.BlockSpec((rows, 2 * cb), lambda c, i: (i, 0)),
        compiler_params=_params("parallel", "arbitrary"),
    )(proj, h_f, h_b, proj, proj, proj, proj, proj, proj, proj, proj, sc_w)


def even_mix_bwd(proj, dycat, h_f, h_b, dua, conv_w, sc_w, name):
    rows_total = proj.shape[0]
    rows = min(MIX_TILE, rows_total)
    n_tiles = rows_total // rows
    cb = D_MODEL
    n_cb = 1
    ident = lambda i: i

    def body(xap, xam, xan, za_ref, xbp, xbm, xbn, gbp, gbm, gbn, gcp, gcm, gcn, zbp, zbm, zbn,
             dya_ref, dybp, dybm, dybn, hf_ref, hb_ref, dup, dum, dun, cw_ref, sw_ref,
             dp_ref, dcw_ref, dcb_ref, dsw_ref):
        def put(k, value):
            dp_ref[:, k * cb:(k + 1) * cb] = value.astype(BF16)

        t = pl.program_id(1)
        first, last = t == 0, t == n_tiles - 1
        za = za_ref[...]
        sa = _sigmoid(za)
        put(1, dya_ref[...] * (hf_ref[...] + hb_ref[...]) * (sa * (1.0 + za * (1.0 - sa))))
        dua_ext = _extend(dup, dum, dun, first, last)
        cw = cw_ref[...]
        put(0, _conv_transpose(dua_ext, cw, 2, rows))
        dua = dua_ext[SUBLANES:SUBLANES + rows]
        xa_ext = _extend(xap, xam, xan, first, last)
        dcw = jnp.concatenate([_colsum(dua * _shifted(xa_ext, k - 2, rows)) for k in range(4)], axis=0)
        dcb = _colsum(dua)
        xb_ext = _extend(xbp, xbm, xbn, first, last)
        gc_ext = _extend(gcp, gcm, gcn, first, last)
        p_ext = xb_ext * gc_ext
        zb_ext = _extend(zbp, zbm, zbn, first, last)
        sb_ext = _sigmoid(zb_ext)
        dyb_ext = _extend(dybp, dybm, dybn, first, last)
        gb_ext = _extend(gbp, gbm, gbn, first, last)
        dcv_ext = dyb_ext * gb_ext * (zb_ext * sb_ext)
        sw = sw_ref[...]
        p_at = [_shifted(p_ext, k - 1, rows) for k in range(3)]
        cv = (p_at[0] * sw[0:1] + p_at[1] * sw[1:2]) + p_at[2] * sw[2:3]
        mid = slice(SUBLANES, SUBLANES + rows)
        zb, sb, dyb, gb = zb_ext[mid], sb_ext[mid], dyb_ext[mid], gb_ext[mid]
        put(3, dyb * cv * (zb * sb))
        put(5, dyb * gb * cv * (sb * (1.0 + zb * (1.0 - sb))))
        dp = _conv_transpose(dcv_ext, sw, 1, rows)
        put(4, dp * xb_ext[mid])
        put(2, dp * gc_ext[mid])
        dcv = dcv_ext[mid]
        dsw = jnp.concatenate([_colsum(dcv * p_at[k]) for k in range(3)], axis=0)

        @pl.when(t == 0)
        def _():
            dcw_ref[...] = dcw
            dcb_ref[...] = dcb
            dsw_ref[...] = dsw

        @pl.when(t > 0)
        def _():
            dcw_ref[...] += dcw
            dcb_ref[...] += dcb
            dsw_ref[...] += dsw

    blk = lambda col: pl.BlockSpec((rows, cb), lambda c, i: (i, col * n_cb + c))
    halo = lambda col: _halo_specs(rows, cb, col * n_cb, n_tiles, ident)
    own = pl.BlockSpec((rows, cb), lambda c, i: (i, c))
    wspec = lambda k: pl.BlockSpec((k, cb), lambda c, i: (0, c))
    return pl.pallas_call(
        body, name=name,
        out_shape=(jax.ShapeDtypeStruct((rows_total, 6 * D_MODEL), BF16),
                   jax.ShapeDtypeStruct((4, D_MODEL), F32), jax.ShapeDtypeStruct((1, D_MODEL), F32),
                   jax.ShapeDtypeStruct((3, D_MODEL), F32)),
        grid=(n_cb, n_tiles),
        in_specs=(halo(0) + [blk(1)] + halo(2) + halo(3) + halo(4) + halo(5) + [blk(0)] + halo(1)
                  + [own, own] + halo(0) + [wspec(4), wspec(3)]),
        out_specs=(pl.BlockSpec((rows, 6 * cb), lambda c, i: (i, 0)), wspec(4), wspec(1), wspec(3)),
        compiler_params=_params("parallel", "arbitrary"),
    )(proj, proj, proj, proj, proj, proj, proj, proj, proj, proj, proj, proj, proj, proj, proj, proj,
      dycat, dycat, dycat, dycat, h_f, h_b, dua, dua, dua, conv_w, sc_w)


def even_out_fwd(ycat, w_out, gain, x, name):
    rows, d = x.shape
    k = ycat.shape[1]
    tm = min(ROW_TILE, rows)

    def body(yc_ref, w_ref, g_ref, x_ref, x1_ref, y_ref):
        y = _dot(yc_ref[...], w_ref[...])
        y_ref[...] = y
        rstd = lax.rsqrt(jnp.mean(y * y, axis=-1, keepdims=True) + NORM_EPS)
        x1_ref[...] = x_ref[...] + y * rstd * g_ref[...]

    row = lambda n: pl.BlockSpec((tm, n), lambda i: (i, 0))
    return pl.pallas_call(
        body, name=name,
        out_shape=(jax.ShapeDtypeStruct((rows, d), F32),) * 2,
        grid=(rows // tm,),
        in_specs=[row(k), _full((k, d)), _full((1, d)), row(d)],
        out_specs=(row(d), row(d)),
        compiler_params=_params("parallel"),
    )(ycat, w_out, gain, x)


def _rmsnorm_bwd(dout, y, gain):
    rstd = lax.rsqrt(jnp.mean(y * y, axis=-1, keepdims=True) + NORM_EPS)
    yhat = y * rstd
    dyn = dout * gain
    dy = rstd * (dyn - yhat * jnp.mean(dyn * yhat, axis=-1, keepdims=True))
    return dy, dout * yhat


def even_out_bwd(dx1, y, gain, w_out, name):
    rows, d = y.shape
    k = w_out.shape[0]
    tm = min(ROW_TILE, rows)

    def body(dx_ref, y_ref, g_ref, w_ref, dy_ref, dyc_ref, dg_ref):
        dy, dg_rows = _rmsnorm_bwd(dx_ref[...], y_ref[...], g_ref[...])
        dyb = dy.astype(BF16)
        dy_ref[...] = dyb
        dyc_ref[...] = _dot_nt(dyb, w_ref[...])
        _accumulate(dg_ref, _colsum(dg_rows), pl.program_id(0))

    row = lambda n: pl.BlockSpec((tm, n), lambda i: (i, 0))
    return pl.pallas_call(
        body, name=name,
        out_shape=(jax.ShapeDtypeStruct((rows, d), BF16), jax.ShapeDtypeStruct((rows, k), F32),
                   jax.ShapeDtypeStruct((1, d), F32)),
        grid=(rows // tm,),
        in_specs=[row(d), row(d), _full((1, d)), _full((k, d))],
        out_specs=(row(d), row(k), _full((1, d))),
        compiler_params=_params("arbitrary"),
    )(dx1, y, gain, w_out)


def _chunk_cumsum(g, reverse):
    n, c = g.shape
    chunks, per = n // GLA_CHUNK, GLA_CHUNK // SUBLANES
    g = g.reshape(n // SUBLANES, SUBLANES, c)
    pos = lax.broadcasted_iota(jnp.int32, (1, SUBLANES, c), 1)
    s = 1
    while s < SUBLANES:
        if reverse:
            g = g + jnp.where(pos < SUBLANES - s, pltpu.roll(g, SUBLANES - s, 1), 0.0)
        else:
            g = g + jnp.where(pos >= s, pltpu.roll(g, s, 1), 0.0)
        s *= 2
    g = g.reshape(chunks, per, SUBLANES, c)
    out, carry = [None] * per, None
    for k in (range(per - 1, -1, -1) if reverse else range(per)):
        out[k] = g[:, k] if carry is None else g[:, k] + carry
        carry = out[k][:, 0:1] if reverse else out[k][:, SUBLANES - 1:SUBLANES]
    return jnp.stack(out, axis=1).reshape(n, c)


def _gla_prepare(q_ref, k_ref, lr_ref, wg_ref, bg_ref, reverse, n_chunks):
    z = _dot(lr_ref[...].astype(BF16), wg_ref[0]) + bg_ref[0]
    g = -_softplus(-z) * (1.0 / GLA_NORMALIZER)
    bcum = _chunk_cumsum(g, reverse).reshape(n_chunks, GLA_CHUNK, GLA_DK)
    edge = 0 if reverse else GLA_CHUNK - 1
    btot = bcum[:, edge:edge + 1, :]
    e_pos = jnp.exp(bcum)
    e_neg = jnp.exp(-bcum)
    e_st = jnp.exp(btot - bcum)
    q3 = q_ref[...].reshape(n_chunks, GLA_CHUNK, GLA_DK)
    k3 = k_ref[...].reshape(n_chunks, GLA_CHUNK, GLA_DK)
    scale = GLA_DK ** -0.5
    q_in = q3 * scale * e_pos
    k_in = k3 * e_neg
    k_st = k3 * e_st
    dec = jnp.exp(btot)
    return z, q_in, k_in, k_st, dec, (scale * e_pos, e_neg, e_st)


def _gla_mask(reverse):
    i = lax.broadcasted_iota(jnp.int32, (GLA_CHUNK, GLA_CHUNK), 0)
    j = lax.broadcasted_iota(jnp.int32, (GLA_CHUNK, GLA_CHUNK), 1)
    return (j >= i) if reverse else (j <= i)


def _gla_specs(rows, n_blocks, reverse):
    tix = (lambda s: n_blocks - 1 - s) if reverse else (lambda s: s)
    d = 1 if reverse else 0
    lr_block = LR_COL // LANES
    specs = [pl.BlockSpec((rows, GLA_DK), lambda h, s: (tix(s), h)),
             pl.BlockSpec((rows, GLA_DK), lambda h, s: (tix(s), GLA_HEADS + h)),
             pl.BlockSpec((rows, GLA_DV), lambda h, s: (tix(s), GLA_HEADS + h)),
             pl.BlockSpec((rows, LANES), lambda h, s: (tix(s), lr_block)),
             pl.BlockSpec((1, LANES, GLA_DK), lambda h, s: (d, 0, h)),
             pl.BlockSpec((1, 1, GLA_DK), lambda h, s: (d, 0, h))]
    return specs, tix


def gla_fwd(proj, wg_pad, bg, reverse, name):
    rows_total = proj.shape[0]
    rows = min(GLA_BLOCK, rows_total)
    n_blocks = rows_total // rows
    n_chunks = rows // GLA_CHUNK
    specs, tix = _gla_specs(rows, n_blocks, reverse)

    def body(q_ref, k_ref, v_ref, lr_ref, wg_ref, bg_ref, o_ref, st_ref, state, kv_scr, dec_scr):
        _, q_in, k_in, k_st, dec, _ = _gla_prepare(q_ref, k_ref, lr_ref, wg_ref, bg_ref, reverse, n_chunks)
        vb = v_ref[...].reshape(n_chunks, GLA_CHUNK, GLA_DV).astype(BF16)
        qb = q_in.astype(BF16)
        p = jnp.where(_gla_mask(reverse), _bdot(qb, k_in.astype(BF16), 2, 2), 0.0)
        o = _bdot(p.astype(BF16), vb, 2, 1)
        kv_scr[...] = _bdot(vb, k_st.astype(BF16), 1, 1)
        dec_scr[...] = jnp.broadcast_to(dec, dec_scr.shape)

        @pl.when(pl.program_id(1) == 0)
        def _():
            state[...] = jnp.zeros_like(state)

        for c in range(n_chunks):
            cc = n_chunks - 1 - c if reverse else c
            st_ref[0, cc] = state[...]
            state[...] = state[...] * dec_scr[cc, 0:1] + kv_scr[cc]
        o = o + _bdot(qb, st_ref[0].astype(BF16), 2, 2)
        o_ref[...] = o.reshape(rows, GLA_DV)

    return pl.pallas_call(
        body, name=name,
        out_shape=(jax.ShapeDtypeStruct((rows_total, GLA_HEADS * GLA_DV), F32),
                   jax.ShapeDtypeStruct((GLA_HEADS, rows_total // GLA_CHUNK, GLA_DV, GLA_DK), F32)),
        grid=(GLA_HEADS, n_blocks),
        in_specs=specs,
        out_specs=(pl.BlockSpec((rows, GLA_DV), lambda h, s: (tix(s), h)),
                   pl.BlockSpec((1, n_chunks, GLA_DV, GLA_DK), lambda h, s: (h, tix(s), 0, 0))),
        scratch_shapes=[pltpu.VMEM((GLA_DV, GLA_DK), F32), pltpu.VMEM((n_chunks, GLA_DV, GLA_DK), F32),
                        pltpu.VMEM((n_chunks, SUBLANES, GLA_DK), F32)],
        compiler_params=_params("parallel", "arbitrary"),
    )(proj, proj, proj, proj, wg_pad, bg)


def gla_bwd(proj, wg_pad, bg, d_o, states, dqkv_in, reverse, name):
    rows_total = proj.shape[0]
    rows = min(GLA_BLOCK, rows_total)
    n_blocks = rows_total // rows
    n_chunks = rows // GLA_CHUNK
    specs, tix = _gla_specs(rows, n_blocks, not reverse)
    d = 1 if reverse else 0
    specs[4] = pl.BlockSpec((1, LANES, GLA_DK), lambda h, s: (d, 0, h))
    specs[5] = pl.BlockSpec((1, 1, GLA_DK), lambda h, s: (d, 0, h))
    add = dqkv_in is not None

    def body(*refs):
        q_ref, k_ref, v_ref, lr_ref, wg_ref, bg_ref, do_ref, st_ref = refs[:8]
        refs = refs[8:]
        if add:
            aq_ref, ak_ref, av_ref = refs[:3]
            refs = refs[3:]
        dq_ref, dk_ref, dv_ref, dz_ref, dstate, g_scr, dec_scr, dsn_scr = refs
        z, q_in, k_in, k_st, dec, (f_q, f_k, f_s) = _gla_prepare(q_ref, k_ref, lr_ref, wg_ref, bg_ref, reverse,
                                                                 n_chunks)
        mask = _gla_mask(reverse)
        vb = v_ref[...].reshape(n_chunks, GLA_CHUNK, GLA_DV).astype(BF16)
        dob = do_ref[...].reshape(n_chunks, GLA_CHUNK, GLA_DV).astype(BF16)
        qb, kb, ksb = q_in.astype(BF16), k_in.astype(BF16), k_st.astype(BF16)
        st = st_ref[0]
        stb = st.astype(BF16)
        pb = jnp.where(mask, _bdot(qb, kb, 2, 2), 0.0).astype(BF16)
        dpb = jnp.where(mask, _bdot(dob, vb, 2, 2), 0.0).astype(BF16)
        d_qin = _bdot(dpb, kb, 2, 1) + _bdot(dob, stb, 2, 1)
        d_kin = _bdot(dpb, qb, 1, 1)
        dv = _bdot(pb, dob, 1, 1)
        g_scr[...] = _bdot(dob, qb, 1, 1)
        dec_scr[...] = jnp.broadcast_to(dec, dec_scr.shape)

        @pl.when(pl.program_id(1) == 0)
        def _():
            dstate[...] = jnp.zeros_like(dstate)

        for c in range(n_chunks):
            cc = c if reverse else n_chunks - 1 - c
            dsn_scr[cc] = dstate[...]
            dstate[...] = dstate[...] * dec_scr[cc, 0:1] + g_scr[cc]
        dsn = dsn_scr[...]
        dsnb = dsn.astype(BF16)
        dv = dv + _bdot(ksb, dsnb, 2, 2)
        d_kst = _bdot(vb, dsnb, 2, 1)
        d_dec = jnp.sum(dsn * st, axis=1, keepdims=True)
        ks_term = d_kst * k_st
        d_btot = d_dec * dec + jnp.sum(ks_term, axis=1, keepdims=True)
        d_b = d_qin * q_in - d_kin * k_in - ks_term
        pos = lax.broadcasted_iota(jnp.int32, d_b.shape, 1)
        edge = 0 if reverse else GLA_CHUNK - 1
        d_b = d_b + jnp.where(pos == edge, d_btot, 0.0)
        dg = _chunk_cumsum(d_b.reshape(rows, GLA_DK), not reverse)
        dz_ref[...] = dg * (1.0 / GLA_NORMALIZER) * _sigmoid(-z)
        dq = (d_qin * f_q).reshape(rows, GLA_DK)
        dk = (d_kin * f_k + d_kst * f_s).reshape(rows, GLA_DK)
        dv = dv.reshape(rows, GLA_DV)
        if add:
            dq_ref[...] = (dq + aq_ref[...]).astype(BF16)
            dk_ref[...] = (dk + ak_ref[...]).astype(BF16)
            dv_ref[...] = (dv + av_ref[...]).astype(BF16)
        else:
            dq_ref[...] = dq
            dk_ref[...] = dk
            dv_ref[...] = dv

    qkv_specs = [pl.BlockSpec((rows, GLA_DK), lambda h, s: (tix(s), h)),
                 pl.BlockSpec((rows, GLA_DK), lambda h, s: (tix(s), h)),
                 pl.BlockSpec((rows, GLA_DV), lambda h, s: (tix(s), h))]
    in_specs = specs + [pl.BlockSpec((rows, GLA_DV), lambda h, s: (tix(s), h)),
                        pl.BlockSpec((1, n_chunks, GLA_DV, GLA_DK), lambda h, s: (h, tix(s), 0, 0))]
    args = [proj, proj, proj, proj, wg_pad, bg, d_o, states]
    out_dtype = F32
    if add:
        in_specs += qkv_specs
        args += list(dqkv_in)
        out_dtype = BF16
    return pl.pallas_call(
        body, name=name,
        out_shape=(jax.ShapeDtypeStruct((rows_total, GLA_HEADS * GLA_DK), out_dtype),
                   jax.ShapeDtypeStruct((rows_total, GLA_HEADS * GLA_DK), out_dtype),
                   jax.ShapeDtypeStruct((rows_total, GLA_HEADS * GLA_DV), out_dtype),
                   jax.ShapeDtypeStruct((rows_total, GLA_HEADS * GLA_DK), F32)),
        grid=(GLA_HEADS, n_blocks),
        in_specs=in_specs,
        out_specs=(pl.BlockSpec((rows, GLA_DK), lambda h, s: (tix(s), h)),
                   pl.BlockSpec((rows, GLA_DK), lambda h, s: (tix(s), h)),
                   pl.BlockSpec((rows, GLA_DV), lambda h, s: (tix(s), h)),
                   pl.BlockSpec((rows, GLA_DK), lambda h, s: (tix(s), h))),
        scratch_shapes=[pltpu.VMEM((GLA_DV, GLA_DK), F32), pltpu.VMEM((n_chunks, GLA_DV, GLA_DK), F32),
                        pltpu.VMEM((n_chunks, SUBLANES, GLA_DK), F32),
                        pltpu.VMEM((n_chunks, GLA_DV, GLA_DK), F32)],
        compiler_params=_params("parallel", "arbitrary"),
    )(*args)


def gla_gate_bwd(proj, dz_f, dz_b, wg_pad, name):
    rows_total = proj.shape[0]
    tm = min(ROW_TILE, rows_total)
    n_key = GLA_HEADS * GLA_DK

    def body(lr_ref, dzf_ref, dzb_ref, wg_ref, dlr_ref, dwg_ref, dbg_ref):
        step = pl.program_id(0)
        lr_t = jnp.transpose(lr_ref[...])
        dzf, dzb = dzf_ref[...], dzb_ref[...]
        dzf16, dzb16 = dzf.astype(BF16), dzb.astype(BF16)
        dlr_ref[...] = (_dot_nt(dzf16, wg_ref[0]) + _dot_nt(dzb16, wg_ref[1])).astype(BF16)
        dwf = _dot(lr_t[0:GLA_RANK].astype(BF16), dzf16)
        dwb = _dot(lr_t[GLA_RANK:2 * GLA_RANK].astype(BF16), dzb16)
        dbg = jnp.concatenate([_colsum(dzf), _colsum(dzb)], axis=0)

        @pl.when(step == 0)
        def _():
            dwg_ref[0] = dwf
            dwg_ref[1] = dwb
            dbg_ref[...] = dbg

        @pl.when(step > 0)
        def _():
            dwg_ref[0] += dwf
            dwg_ref[1] += dwb
            dbg_ref[...] += dbg

    return pl.pallas_call(
        body, name=name,
        out_shape=(jax.ShapeDtypeStruct((rows_total, LANES), BF16), jax.ShapeDtypeStruct((2, GLA_RANK, n_key), F32),
                   jax.ShapeDtypeStruct((2, n_key), F32)),
        grid=(rows_total // tm,),
        in_specs=[pl.BlockSpec((tm, LANES), lambda i: (i, LR_COL // LANES)),
                  pl.BlockSpec((tm, n_key), lambda i: (i, 0)), pl.BlockSpec((tm, n_key), lambda i: (i, 0)),
                  _full((2, LANES, n_key))],
        out_specs=(pl.BlockSpec((tm, LANES), lambda i: (i, 0)), _full((2, GLA_RANK, n_key)), _full((2, n_key))),
        compiler_params=_params("arbitrary"),
    )(proj, dz_f, dz_b, wg_pad)


def _head_norm(o, gain):
    outs, hats, rstds = [], [], []
    for h in range(GLA_HEADS):
        oh = o[:, h * GLA_DV:(h + 1) * GLA_DV]
        rstd = lax.rsqrt(jnp.mean(oh * oh, axis=-1, keepdims=True) + NORM_EPS)
        hat = oh * rstd
        outs.append(hat * gain)
        hats.append(hat)
        rstds.append(rstd)
    return outs, hats, rstds


def odd_out_fwd(o_f, o_b, proj, head_gain, w_out, gain, x1, target, name):
    rows, d = x1.shape
    tm = min(ROW_TILE, rows)
    r_block = (2 * GLA_HEADS * GLA_DK + GLA_HEADS * GLA_DV) // d

    def body(of_ref, ob_ref, r_ref, hg_ref, w_ref, g_ref, x1_ref, tgt_ref, y2_ref, dy_ref, dx2_ref, loss_ref,
             dg_ref):
        step = pl.program_id(0)
        on, _, _ = _head_norm(of_ref[...] + ob_ref[...], hg_ref[...])
        r = r_ref[...]
        y2 = (jnp.concatenate(on, axis=1) * (r * _sigmoid(r))).astype(BF16)
        y2_ref[...] = y2
        y = _dot(y2, w_ref[...])
        gain_v = g_ref[...]
        rstd = lax.rsqrt(jnp.mean(y * y, axis=-1, keepdims=True) + NORM_EPS)
        x2 = x1_ref[...] + y * rstd * gain_v
        diff = x2 - tgt_ref[...]
        loss = 0.5 * jnp.sum(jnp.mean(diff * diff, axis=-1, keepdims=True), axis=0, keepdims=True)
        dx2 = diff * (1.0 / d)
        dx2_ref[...] = dx2
        dy, dg_rows = _rmsnorm_bwd(dx2, y, gain_v)
        dy_ref[...] = dy.astype(BF16)
        _accumulate(loss_ref, jnp.broadcast_to(loss, loss_ref.shape), step)
        _accumulate(dg_ref, _colsum(dg_rows), step)

    row = lambda n, col=0: pl.BlockSpec((tm, n), lambda i: (i, col))
    return pl.pallas_call(
        body, name=name,
        out_shape=(jax.ShapeDtypeStruct((rows, d), BF16), jax.ShapeDtypeStruct((rows, d), BF16),
                   jax.ShapeDtypeStruct((rows, d), F32), jax.ShapeDtypeStruct((SUBLANES, LANES), F32),
                   jax.ShapeDtypeStruct((1, d), F32)),
        grid=(rows // tm,),
        in_specs=[row(d), row(d), row(d, r_block), _full((1, GLA_DV)), _full((d, d)), _full((1, d)), row(d), row(d)],
        out_specs=(row(d), row(d), row(d), _full((SUBLANES, LANES)), _full((1, d))),
        compiler_params=_params("arbitrary"),
    )(o_f, o_b, proj, head_gain, w_out, gain, x1, target)


def odd_out_bwd(dy, w_out, o_f, o_b, proj, head_gain, name):
    rows, d = dy.shape
    tm = min(ROW_TILE, rows)
    r_block = (2 * GLA_HEADS * GLA_DK + GLA_HEADS * GLA_DV) // d

    def body(dy_ref, w_ref, of_ref, ob_ref, r_ref, hg_ref, dr_ref, do_ref, dhg_ref):
        dy2 = _dot_nt(dy_ref[...], w_ref[...])
        hg = hg_ref[...]
        on, hats, rstds = _head_norm(of_ref[...] + ob_ref[...], hg)
        r = r_ref[...]
        sr = _sigmoid(r)
        dr_ref[...] = (dy2 * jnp.concatenate(on, axis=1) * (sr * (1.0 + r * (1.0 - sr)))).astype(BF16)
        d_on = dy2 * (r * sr)
        d_os, dhg = [], None
        for h in range(GLA_HEADS):
            dn = d_on[:, h * GLA_DV:(h + 1) * GLA_DV]
            part = _colsum(dn * hats[h])
            dhg = part if dhg is None else dhg + part
            dng = dn * hg
            d_os.append(rstds[h] * (dng - hats[h] * jnp.mean(dng * hats[h], axis=-1, keepdims=True)))
        do_ref[...] = jnp.concatenate(d_os, axis=1)
        _accumulate(dhg_ref, dhg, pl.program_id(0))

    row = lambda n, col=0: pl.BlockSpec((tm, n), lambda i: (i, col))
    return pl.pallas_call(
        body, name=name,
        out_shape=(jax.ShapeDtypeStruct((rows, d), BF16), jax.ShapeDtypeStruct((rows, d), F32),
                   jax.ShapeDtypeStruct((1, GLA_DV), F32)),
        grid=(rows // tm,),
        in_specs=[row(d), _full((d, d)), row(d), row(d), row(d, r_block), _full((1, GLA_DV))],
        out_specs=(row(d), row(d), _full((1, GLA_DV))),
        compiler_params=_params("arbitrary"),
    )(dy, w_out, o_f, o_b, proj, head_gain)


def local_step(x, target, w, reduce_first=None, reduce_second=None, late_weights=None):
    g = {}
    proj_e, h0 = norm_matmul(x, w["even_norm_pre"], w["even_w_in"], "even_in_proj")
    h_dir, acts = zip(*[rglru_fwd(proj_e, w["rg_conv_w"], w["rg_conv_b"], w["rg_gate_w"][d], w["rg_gate_b"][d],
                                  w["rg_lambda"][d], d == 1, "rglru_fwd_%d" % d) for d in range(2)])
    ycat = even_mix_fwd(proj_e, h_dir[0], h_dir[1], w["sc_conv_w"], "even_mix_fwd")
    if late_weights is not None:
        w = dict(w, **late_weights(ycat))
    x1, y_e = even_out_fwd(ycat, w["even_w_out"], w["even_norm_post"], x, "even_out_fwd")
    proj_o, h1 = norm_matmul(x1, w["odd_norm_pre"], w["odd_w_in"], "odd_in_proj")
    o_dir, st_dir = [], []
    for d in range(2):
        o, st = gla_fwd(proj_o, w["gla_wg_pad"], w["gla_b_gate"], d == 1, "gla_fwd_%d" % d)
        o_dir.append(o)
        st_dir.append(st)
    y2, dy_o, dx2, loss, g["odd_norm_post"] = odd_out_fwd(
        o_dir[0], o_dir[1], proj_o, w["gla_norm_g"], w["odd_w_out"], w["odd_norm_post"], x1, target, "odd_out_fwd")
    g["odd_w_out"] = matmul_dw(y2, dy_o, D_MODEL, "odd_w_out_grad")[0]
    dr, d_o, g["gla_norm_g"] = odd_out_bwd(dy_o, w["odd_w_out"], o_dir[0], o_dir[1], proj_o, w["gla_norm_g"],
                                           "odd_out_bwd")
    dq, dk, dv, dz_f = gla_bwd(proj_o, w["gla_wg_pad"], w["gla_b_gate"], d_o, st_dir[0], None, False, "gla_bwd_0")
    dq, dk, dv, dz_b = gla_bwd(proj_o, w["gla_wg_pad"], w["gla_b_gate"], d_o, st_dir[1], (dq, dk, dv), True,
                               "gla_bwd_1")
    dlr, g["gla_w_gate_lr"], g["gla_b_gate"] = gla_gate_bwd(proj_o, dz_f, dz_b, w["gla_wg_pad"], "gla_gate_bwd")
    dproj_o = [dq, dk, dv, dr, dlr]
    g["odd_w_in"] = jnp.concatenate(matmul_dw_pieces(h1, dproj_o, "odd_w_in_grad"), axis=1)[:, :ODD_IN]
    dx1, g["odd_norm_pre"] = inproj_bwd_pieces(dproj_o, w["odd_w_in"], x1, w["odd_norm_pre"], dx2, "odd_in_proj_bwd")
    dy_e, dycat, g["even_norm_post"] = even_out_bwd(dx1, y_e, w["even_norm_post"], w["even_w_out"], "even_out_bwd")
    g["even_w_out"] = matmul_dw(ycat, dy_e, D_MODEL, "even_w_out_grad")[0]
    lam = w["rg_lambda"] if reduce_first is None else w["rg_lambda"] + reduce_first(g)
    dua, dgw, dgb, dlam = None, [], [], []
    for d in range(2):
        a, b, c, e = rglru_bwd(proj_e, dycat, h_dir[d], acts[d], w["rg_gate_w"][d], lam[d], dua, d == 1,
                               "rglru_bwd_%d" % d)
        dua = a
        dgw.append(b)
        dgb.append(c)
        dlam.append(e)
    dproj_e, g["rg_conv_w"], g["rg_conv_b"], g["sc_conv_w"] = even_mix_bwd(
        proj_e, dycat, h_dir[0], h_dir[1], dua, w["rg_conv_w"], w["sc_conv_w"], "even_mix_bwd")
    dgw = jnp.stack(dgw).reshape(2, RG_HEADS, RG_HEAD_DIM, 2, RG_HEAD_DIM)
    g["rg_gate_w"] = jnp.transpose(dgw, (0, 3, 1, 2, 4))
    g["rg_gate_b"] = jnp.stack(dgb).reshape(2, 2, RG_HEADS, RG_HEAD_DIM)
    g["rg_lambda"] = jnp.concatenate(dlam, axis=0)
    g["even_w_in"] = matmul_dw(h0, dproj_e, EVEN_IN // 4, "even_w_in_grad")
    gain = w["even_norm_pre"] if reduce_second is None else w["even_norm_pre"] + reduce_second(g)
    grad_x, g["even_norm_pre"] = inproj_bwd(dproj_e, w["even_w_in"], x, gain, dx1, "even_in_proj_bwd")
    return loss, grad_x, g


def _prepare_weights(full):
    w = {}
    for name in ("even_norm_pre", "even_norm_post", "rg_conv_b", "odd_norm_pre", "odd_norm_post", "gla_norm_g"):
        if name in full:
            w[name] = full[name].reshape(1, -1)
    for name in ("rg_conv_w", "sc_conv_w"):
        if name in full:
            w[name] = full[name]
    for name in ("even_w_out", "odd_w_out"):
        if name in full:
            w[name] = full[name].astype(BF16)
    if "even_w_in" in full:
        w["even_w_in"] = full["even_w_in"].astype(BF16)
        if w["even_w_in"].ndim == 2:
            w["even_w_in"] = jnp.transpose(w["even_w_in"].reshape(D_MODEL, 4, EVEN_IN // 4), (1, 0, 2))
    if "rg_gate_w" in full:
        gw = jnp.transpose(full["rg_gate_w"].astype(BF16), (0, 2, 3, 1, 4))
        w["rg_gate_w"] = gw.reshape(2, RG_HEADS, RG_HEAD_DIM, 2 * RG_HEAD_DIM)
        w["rg_gate_b"] = full["rg_gate_b"].reshape(2, 2, D_MODEL)
        w["rg_lambda"] = full["rg_lambda"].reshape(2, 1, D_MODEL)
    if "odd_w_in" in full:
        w_in = jnp.pad(full["odd_w_in"].astype(BF16), ((0, 0), (0, ODD_IN_PAD - ODD_IN)))
        w["odd_w_in"] = w_in.reshape(1, D_MODEL, ODD_IN_PAD)
    if "gla_w_gate_lr" in full:
        wg = full["gla_w_gate_lr"].astype(BF16)
        w["gla_wg_pad"] = jnp.stack([jnp.pad(wg[d], ((d * GLA_RANK, LANES - (d + 1) * GLA_RANK), (0, 0)))
                                     for d in range(2)])
        w["gla_b_gate"] = full["gla_b_gate"].reshape(2, 1, GLA_HEADS * GLA_DK)
    return w


SHARDED_SMALL = (("rg_conv_w", (4, 256)), ("rg_lambda", (2, 256)), ("sc_conv_w", (3, 256)),
                 ("odd_norm_pre", (256,)), ("odd_norm_post", (256,)), ("gla_w_gate_lr", (2, 16, 128)),
                 ("gla_b_gate", (2, 128)), ("gla_norm_g", (64,)))
SHARDED_ROWS = 96
REPLICATED = (("rg_gate_w", (2, 2, 8, 128, 128)), ("even_norm_post", (1024,)), ("rg_conv_b", (1024,)),
              ("rg_gate_b", (2, 2, 8, 128)))
GATE_ROWS = 4096
LAST_REPLICATED = (("even_norm_pre", (1024,)),)
LAST_ROWS = 8
REPLICATED_ROWS = 4160
REP_PART = REPLICATED_ROWS // 8
HALF_SHARDED = SHARDED_ROWS // 2
PACK_HALF = HALF_SHARDED + REP_PART


def _seg_rows(shape):
    n = 1
    for s in shape:
        n *= s
    return -(-n // (SUBLANES * LANES)) * SUBLANES


def _pack(arrays, spec, total_rows, lead=()):
    parts = []
    for name, shape in spec:
        flat = arrays[name].reshape(lead + (-1,))
        pad = _seg_rows(shape) * LANES - flat.shape[-1]
        if pad:
            flat = jnp.pad(flat, [(0, 0)] * len(lead) + [(0, pad)])
        parts.append(flat.reshape(lead + (-1, LANES)))
    rows = jnp.concatenate(parts, axis=len(lead))
    pad = total_rows - rows.shape[len(lead)]
    return jnp.pad(rows, [(0, 0)] * len(lead) + [(0, pad), (0, 0)])


def _unpack(rows, spec, lead=()):
    out, at = {}, 0
    for name, shape in spec:
        n = 1
        for s in shape:
            n *= s
        k = _seg_rows(shape)
        seg = lax.slice_in_dim(rows, at, at + k, axis=len(lead)).reshape(lead + (-1,))
        out[name] = lax.slice_in_dim(seg, 0, n, axis=len(lead)).reshape(lead + shape)
        at += k
    return out


def _split_owners(arr):
    a = arr.reshape(arr.shape[:-1] + (4, arr.shape[-1] // 4))
    return jnp.moveaxis(a, -2, 0)


def _merge_owners(arr):
    a = jnp.moveaxis(arr, 0, -2)
    return a.reshape(a.shape[:-2] + (-1,))


HBM_SPEC = pl.BlockSpec(memory_space=pltpu.HBM)


def _position():
    x, y, c = lax.axis_index("x"), lax.axis_index("y"), lax.axis_index("c")
    chips = [(1 - x, y), (x, 1 - y), (1 - x, 1 - y)]
    return x, y, c, chips


def _remote(src, dst, send_sem, recv_sem, device):
    return pltpu.make_async_remote_copy(src_ref=src, dst_ref=dst, send_sem=send_sem, recv_sem=recv_sem,
                                        device_id=device, device_id_type=MESH)


SEM_SPEC = pl.BlockSpec(memory_space=pltpu.SEMAPHORE)
SIDE_EFFECT = pltpu.SideEffectType.DATAFLOW_SIDE_EFFECTING


def _gather_copies(ins, lands, n_h, send_sems, recv_sems):
    x, y, c, chips = _position()
    me = 2 * x + y
    copies = []
    for a in range(len(ins)):
        for k, chip in enumerate(chips):
            src = ins[a].at[c] if a < n_h else ins[a]
            dst = lands[a].at[me, c] if a < n_h else lands[a].at[me]
            copies.append(_remote(src, dst, send_sems.at[3 * a + k], recv_sems.at[3 * a + k], (chip[0], chip[1], c)))
    return copies


def gather_start(halved, whole, name):
    arrays = list(halved) + list(whole)
    n, n_h = len(arrays), len(halved)
    lands = [lax.empty((4,) + a.shape, a.dtype) for a in arrays]

    def body(*refs):
        ins, lz, send_sems, recv_sems, token = refs[:n], refs[n:2 * n], refs[2 * n], refs[2 * n + 1], refs[-1]
        for cp in _gather_copies(ins, lz, n_h, send_sems, recv_sems):
            cp.start()
        token[...] = jnp.zeros_like(token)

    operands = [pltpu.with_memory_space_constraint(a, pltpu.HBM) for a in arrays + lands]
    return pl.pallas_call(
        body, name=name,
        out_shape=(pltpu.SemaphoreType.DMA((3 * n,)), pltpu.SemaphoreType.DMA((3 * n,)))
        + tuple(pltpu.HBM(a.shape, a.dtype) for a in operands) + (jax.ShapeDtypeStruct((SUBLANES, LANES), F32),),
        in_specs=[HBM_SPEC] * (2 * n),
        out_specs=(SEM_SPEC, SEM_SPEC) + (HBM_SPEC,) * (2 * n) + (pl.BlockSpec(memory_space=pltpu.VMEM),),
        input_output_aliases={i: 2 + i for i in range(2 * n)},
        compiler_params=pltpu.CompilerParams(has_side_effects=SIDE_EFFECT),
    )(*operands)


def gather_wait(started, n_h, after, name):
    send_sems, recv_sems = started[0], started[1]
    operands = list(started[2:-1])
    n = len(operands) // 2

    def body(*refs):
        ins, lz, send_ref, recv_ref = refs[:n], refs[n:2 * n], refs[2 * n], refs[2 * n + 1]
        for cp in _gather_copies(ins, lz, n_h, send_ref, recv_ref):
            cp.wait_send()
            cp.wait_recv()

    outs = pl.pallas_call(
        body, name=name,
        out_shape=tuple(pltpu.HBM(a.shape, a.dtype) for a in operands),
        in_specs=[HBM_SPEC] * (2 * n) + [SEM_SPEC, SEM_SPEC, pl.BlockSpec(memory_space=pl.ANY)],
        out_specs=(HBM_SPEC,) * (2 * n),
        input_output_aliases={i: i for i in range(2 * n)},
        compiler_params=pltpu.CompilerParams(has_side_effects=SIDE_EFFECT),
    )(*operands, send_sems, recv_sems, after)
    return outs[n:]


def pass_to_sibling(fulls, name):
    n = len(fulls)

    def body(*refs):
        bufs = refs[n:2 * n]
        send_sems, recv_sems = refs[2 * n:]
        x, y, c, chips = _position()
        sibling = (x, y, 1 - c)
        copies = []
        for a in range(n):
            for k, chip in enumerate(chips):
                q = 2 * chip[0] + chip[1]
                cp = _remote(bufs[a].at[q, c], bufs[a].at[q, c], send_sems.at[3 * a + k], recv_sems.at[3 * a + k],
                             sibling)
                cp.start()
                copies.append(cp)
        for a in range(n):
            for k, chip in enumerate(chips):
                q = 2 * chip[0] + chip[1]
                passed = bufs[a].at[q, 1 - c]
                _remote(passed, passed, send_sems.at[3 * a + k], recv_sems.at[3 * a + k], sibling).wait_recv()
        for cp in copies:
            cp.wait_send()

    return pl.pallas_call(
        body, name=name,
        out_shape=[jax.ShapeDtypeStruct(a.shape, a.dtype) for a in fulls],
        in_specs=[HBM_SPEC] * n, out_specs=[HBM_SPEC] * n,
        input_output_aliases={i: i for i in range(n)},
        scratch_shapes=[pltpu.SemaphoreType.DMA((3 * n,)), pltpu.SemaphoreType.DMA((3 * n,))],
    )(*fulls)


def place_own(full, own, chip, name):
    _, _, r, cols = full.shape
    tr = _row_tile(r, cols)

    def body(p_ref, own_ref, full_ref, o_ref):
        o_ref[0] = own_ref[...]

    return pl.pallas_call(
        body, name=name,
        out_shape=jax.ShapeDtypeStruct(full.shape, full.dtype),
        grid_spec=pltpu.PrefetchScalarGridSpec(
            num_scalar_prefetch=1, grid=(2, r // tr),
            in_specs=[pl.BlockSpec((1, tr, cols), lambda h, i, p_ref: (h, i, 0)), pl.BlockSpec(memory_space=pl.ANY)],
            out_specs=pl.BlockSpec((1, 1, tr, cols), lambda h, i, p_ref: (p_ref[0], h, i, 0))),
        input_output_aliases={2: 0},
        compiler_params=_params("parallel", "parallel"),
    )(chip, own, full)


def exchange_with_sibling(arrays, name):
    n = len(arrays)

    def body(*refs):
        ins, outs = refs[:n], refs[n:2 * n]
        send_sems, recv_sems = refs[2 * n:]
        x, y, c, _ = _position()
        copies = []
        for a in range(n):
            cp = _remote(ins[a].at[:, 1 - c], outs[a], send_sems.at[a], recv_sems.at[a], (x, y, 1 - c))
            cp.start()
            copies.append(cp)
        for cp in copies:
            cp.wait()

    return pl.pallas_call(
        body, name=name,
        out_shape=[jax.ShapeDtypeStruct((a.shape[0],) + a.shape[2:], a.dtype) for a in arrays],
        in_specs=[HBM_SPEC] * n, out_specs=[HBM_SPEC] * n,
        scratch_shapes=[pltpu.SemaphoreType.DMA((n,)), pltpu.SemaphoreType.DMA((n,))],
    )(*arrays)


def _chip_copies(ins, lands, send_sems, recv_sems):
    x, y, c, chips = _position()
    copies = []
    for a in range(len(ins)):
        for k, chip in enumerate(chips):
            q = 2 * chip[0] + chip[1]
            copies.append(_remote(ins[a].at[q], lands[a].at[k], send_sems.at[3 * a + k], recv_sems.at[3 * a + k],
                                  (chip[0], chip[1], c)))
    return copies


def exchange_with_chips_start(arrays, name):
    n = len(arrays)
    lands = [lax.empty((3,) + a.shape[1:], a.dtype) for a in arrays]

    def body(*refs):
        ins, lz, send_sems, recv_sems, token = refs[:n], refs[n:2 * n], refs[2 * n], refs[2 * n + 1], refs[-1]
        for cp in _chip_copies(ins, lz, send_sems, recv_sems):
            cp.start()
        token[...] = jnp.zeros_like(token)

    operands = [pltpu.with_memory_space_constraint(a, pltpu.HBM) for a in list(arrays) + lands]
    return pl.pallas_call(
        body, name=name,
        out_shape=(pltpu.SemaphoreType.DMA((3 * n,)), pltpu.SemaphoreType.DMA((3 * n,)))
        + tuple(pltpu.HBM(a.shape, a.dtype) for a in operands) + (jax.ShapeDtypeStruct((SUBLANES, LANES), F32),),
        in_specs=[HBM_SPEC] * (2 * n),
        out_specs=(SEM_SPEC, SEM_SPEC) + (HBM_SPEC,) * (2 * n) + (pl.BlockSpec(memory_space=pltpu.VMEM),),
        input_output_aliases={i: 2 + i for i in range(2 * n)},
        compiler_params=pltpu.CompilerParams(has_side_effects=SIDE_EFFECT),
    )(*operands)


def exchange_with_chips_wait(started, after, name):
    send_sems, recv_sems = started[0], started[1]
    operands = list(started[2:-1])
    n = len(operands) // 2

    def body(*refs):
        ins, lz, send_ref, recv_ref = refs[:n], refs[n:2 * n], refs[2 * n], refs[2 * n + 1]
        for cp in _chip_copies(ins, lz, send_ref, recv_ref):
            cp.wait_send()
            cp.wait_recv()

    outs = pl.pallas_call(
        body, name=name,
        out_shape=tuple(pltpu.HBM(a.shape, a.dtype) for a in operands),
        in_specs=[HBM_SPEC] * (2 * n) + [SEM_SPEC, SEM_SPEC, pl.BlockSpec(memory_space=pl.ANY)],
        out_specs=(HBM_SPEC,) * (2 * n),
        input_output_aliases={i: i for i in range(2 * n)},
        compiler_params=pltpu.CompilerParams(has_side_effects=SIDE_EFFECT),
    )(*operands, send_sems, recv_sems, after)
    return outs[:n], outs[n:]


def share_totals(totals, pack_total, last_part):
    arrays = list(totals) + [pack_total]
    n = len(arrays)

    def body(*refs):
        ins, last, outs, rep, last_all = refs[:n], refs[n], refs[n + 1:2 * n + 1], refs[2 * n + 1], refs[2 * n + 2]
        send_sems, recv_sems, rep_send, rep_recv, last_send, last_recv = refs[2 * n + 3:]
        x, y, c, chips = _position()
        sibling = (x, y, 1 - c)
        me = 4 * x + 2 * y + c
        sends = []
        for a in range(n):
            cp = _remote(ins[a], outs[a], send_sems.at[a], recv_sems.at[a], sibling)
            cp.start()
            sends.append(cp)
        mine = ins[n - 1].at[pl.ds(HALF_SHARDED, REP_PART)]
        peers = [sibling]
        for chip in chips:
            peers += [(chip[0], chip[1], c), (chip[0], chip[1], 1 - c)]
        for j, peer in enumerate(peers):
            for src, dst, s_sem, r_sem in ((mine, rep, rep_send, rep_recv), (last, last_all, last_send, last_recv)):
                cp = _remote(src, dst.at[me], s_sem.at[j], r_sem.at[j], peer)
                cp.start()
                sends.append(cp)
        for a in range(n):
            _remote(outs[a], outs[a], send_sems.at[a], recv_sems.at[a], sibling).wait_recv()
        for j, peer in enumerate(peers):
            it = 4 * peer[0] + 2 * peer[1] + peer[2]
            _remote(rep.at[it], rep.at[it], rep_send.at[j], rep_recv.at[j], peer).wait_recv()
            _remote(last_all.at[it], last_all.at[it], last_send.at[j], last_recv.at[j], peer).wait_recv()
        for cp in sends:
            cp.wait_send()

    outs = pl.pallas_call(
        body, name="grad_share_totals",
        out_shape=[jax.ShapeDtypeStruct(a.shape, a.dtype) for a in arrays]
        + [jax.ShapeDtypeStruct((8, REP_PART, LANES), F32), jax.ShapeDtypeStruct((8,) + last_part.shape, F32)],
        in_specs=[HBM_SPEC] * (n + 1), out_specs=[HBM_SPEC] * (n + 2),
        scratch_shapes=[pltpu.SemaphoreType.DMA((n,)), pltpu.SemaphoreType.DMA((n,))]
        + [pltpu.SemaphoreType.DMA((7,))] * 4,
    )(*arrays, last_part)
    return outs[:n], outs[n], outs[n + 1]


def sum_parts(parts, name):
    def body(p_ref, o_ref):
        total = p_ref[0]
        for k in range(1, parts.shape[0]):
            total = total + p_ref[k]
        o_ref[...] = total

    return pl.pallas_call(body, name=name, out_shape=jax.ShapeDtypeStruct(parts.shape[1:], parts.dtype))(parts)


TILE_BYTES = 2 << 20


def _row_tile(rows, cols):
    best = None
    for t in range(SUBLANES, rows + 1, SUBLANES):
        if rows % t == 0 and t * cols * 4 <= TILE_BYTES:
            best = t
    return best if best is not None else rows


def add_sibling(mine, received, core, out_dtype, name):
    _, _, r, cols = mine.shape
    tr = _row_tile(r, cols)

    def body(c_ref, a_ref, b_ref, o_ref):
        o_ref[...] = (a_ref[0] + b_ref[...]).astype(out_dtype)

    return pl.pallas_call(
        body, name=name,
        out_shape=jax.ShapeDtypeStruct((4, r, cols), out_dtype),
        grid_spec=pltpu.PrefetchScalarGridSpec(
            num_scalar_prefetch=1, grid=(4, r // tr),
            in_specs=[pl.BlockSpec((1, 1, tr, cols), lambda o, i, c_ref: (o, c_ref[0], i, 0)),
                      pl.BlockSpec((1, tr, cols), lambda o, i, c_ref: (o, i, 0))],
            out_specs=pl.BlockSpec((1, tr, cols), lambda o, i, c_ref: (o, i, 0))),
        compiler_params=_params("parallel", "parallel"),
    )(core, mine, received)


def add_chips(own, received, chip, name):
    _, r, cols = own.shape
    tr = _row_tile(r, cols)

    def body(p_ref, a_ref, b0, b1, b2, o_ref):
        o_ref[...] = ((a_ref[0].astype(F32) + b0[0].astype(F32)) + b1[0].astype(F32)) + b2[0].astype(F32)

    rb = lambda k: pl.BlockSpec((1, tr, cols), lambda i, p_ref: (k, i, 0))
    return pl.pallas_call(
        body, name=name,
        out_shape=jax.ShapeDtypeStruct((r, cols), F32),
        grid_spec=pltpu.PrefetchScalarGridSpec(
            num_scalar_prefetch=1, grid=(r // tr,),
            in_specs=[pl.BlockSpec((1, tr, cols), lambda i, p_ref: (p_ref[0], i, 0)), rb(0), rb(1), rb(2)],
            out_specs=pl.BlockSpec((tr, cols), lambda i, p_ref: (i, 0))),
        compiler_params=_params("parallel"),
    )(chip, own, received, received, received)


def _adamw_update(gv, w_ref, m_ref, v_ref, d_ref, nm_ref, nv_ref):
    nm = ADAM_B1 * m_ref[...] + (1.0 - ADAM_B1) * gv
    nv = ADAM_B2 * v_ref[...] + (1.0 - ADAM_B2) * (gv * gv)
    nm_ref[...] = nm
    nv_ref[...] = nv
    m_hat = nm / (1.0 - ADAM_B1 ** ADAM_STEP)
    v_hat = nv / (1.0 - ADAM_B2 ** ADAM_STEP)
    d_ref[...] = -ADAM_LR * (m_hat / (jnp.sqrt(v_hat) + ADAM_EPS) + ADAM_WD * w_ref[...])


def adamw_halves(w, own, received, m, v, core, name, by_columns=False):
    rows, cols = w.shape

    def body(c_ref, w_ref, own_ref, rec_ref, m_ref, v_ref, g_ref, d_ref, nm_ref, nv_ref):
        gv = jnp.where(pl.program_id(0) == c_ref[0], own_ref[...], rec_ref[...])
        g_ref[...] = gv
        _adamw_update(gv, w_ref, m_ref, v_ref, d_ref, nm_ref, nv_ref)

    if by_columns:
        nr = 1
        whole = pl.BlockSpec((rows, cols // 2), lambda h, i, c_ref: (0, h))
        half = pl.BlockSpec((rows, cols // 2), lambda h, i, c_ref: (0, 0))
    else:
        r = rows // 2
        tr = _row_tile(r, cols)
        nr = r // tr
        whole = pl.BlockSpec((tr, cols), lambda h, i, c_ref: (h * nr + i, 0))
        half = pl.BlockSpec((tr, cols), lambda h, i, c_ref: (i, 0))
    return pl.pallas_call(
        body, name=name,
        out_shape=(jax.ShapeDtypeStruct((rows, cols), F32),) * 4,
        grid_spec=pltpu.PrefetchScalarGridSpec(
            num_scalar_prefetch=1, grid=(2, nr),
            in_specs=[whole, half, half, whole, whole], out_specs=(whole,) * 4),
        compiler_params=_params("parallel", "parallel"),
    )(core, w, own, received, m, v)


def adamw(w, g, m, v, name):
    r, cols = w.shape
    tr = _row_tile(r, cols)

    def body(w_ref, g_ref, m_ref, v_ref, g_out, d_ref, nm_ref, nv_ref):
        gv = g_ref[...]
        g_out[...] = gv
        _adamw_update(gv, w_ref, m_ref, v_ref, d_ref, nm_ref, nv_ref)

    blk = pl.BlockSpec((tr, cols), lambda i: (i, 0))
    return pl.pallas_call(
        body, name=name,
        out_shape=(jax.ShapeDtypeStruct((r, cols), F32),) * 4,
        grid=(r // tr,),
        in_specs=[blk] * 4, out_specs=(blk,) * 4,
        compiler_params=_params("parallel"),
    )(w, g, m, v)


WEIGHTS = ("even_norm_pre", "even_norm_post", "even_w_in", "rg_conv_w", "rg_conv_b", "rg_gate_w", "rg_gate_b",
           "rg_lambda", "sc_conv_w", "even_w_out", "odd_norm_pre", "odd_norm_post", "odd_w_in", "gla_w_gate_lr",
           "gla_b_gate", "gla_norm_g", "odd_w_out")
BIG = ("even_w_in", "even_w_out", "odd_w_in", "odd_w_out")


def _halves(a):
    return a.reshape((2, a.shape[0] // 2) + a.shape[1:])


def kernel(x, even_norm_pre, even_norm_post, even_w_in, rg_conv_w, rg_conv_b, rg_gate_w, rg_gate_b, rg_lambda, sc_conv_w, even_w_out, odd_norm_pre, odd_norm_post, odd_w_in, gla_w_gate_lr, gla_b_gate, gla_norm_g, odd_w_out, loss_target, m_even_norm_pre, m_even_norm_post, m_even_w_in, m_rg_conv_w, m_rg_conv_b, m_rg_gate_w, m_rg_gate_b, m_rg_lambda, m_sc_conv_w, m_even_w_out, m_odd_norm_pre, m_odd_norm_post, m_odd_w_in, m_gla_w_gate_lr, m_gla_b_gate, m_gla_norm_g, m_odd_w_out, v_even_norm_pre, v_even_norm_post, v_even_w_in, v_rg_conv_w, v_rg_conv_b, v_rg_gate_w, v_rg_gate_b, v_rg_lambda, v_sc_conv_w, v_even_w_out, v_odd_norm_pre, v_odd_norm_post, v_odd_w_in, v_gla_w_gate_lr, v_gla_b_gate, v_gla_norm_g, v_odd_w_out):
    given = dict(locals())
    shard = {n: given[n][0] for n in WEIGHTS}
    m_in = {n: given["m_" + n][0] for n in WEIGHTS}
    v_in = {n: given["v_" + n][0] for n in WEIGHTS}
    mx, my, mc = lax.axis_index("x"), lax.axis_index("y"), lax.axis_index("c")
    core = jnp.reshape(mc, (1,)).astype(jnp.int32)
    chip = jnp.reshape(2 * mx + my, (1,)).astype(jnp.int32)

    small_shard = _pack(shard, SHARDED_SMALL, SHARDED_ROWS)
    big_own = [_halves(shard[n].astype(BF16)) for n in BIG]
    started_a = gather_start(big_own[:1], [small_shard], "gather_start_a")
    started_b = gather_start(big_own[1:], [], "gather_start_b")
    even_w_in_full, small_full = gather_wait(started_a, 1, started_b[-1], "gather_wait_a")
    (even_w_in_full,) = pass_to_sibling([even_w_in_full], "gather_pass_a")
    even_w_in_full = place_own(even_w_in_full, big_own[0], chip, "place_even_w_in")
    small_full = lax.dynamic_update_slice(small_full, small_shard[None], (chip[0], 0, 0))
    full = {n: shard[n] for n, _ in REPLICATED + LAST_REPLICATED}
    full.update({n: _merge_owners(a) for n, a in _unpack(small_full, SHARDED_SMALL, lead=(4,)).items()})
    full["even_w_in"] = even_w_in_full.reshape(4, D_MODEL, EVEN_IN // 4)

    def late_weights(after):
        lands = pass_to_sibling(list(gather_wait(started_b, 3, after, "gather_wait_b")), "gather_pass_b")
        lands = [place_own(a, b, chip, "place_" + n) for a, b, n in zip(lands, big_own[1:], BIG[1:])]
        odd_w_in = jnp.transpose(lands[1].reshape(4, D_MODEL, ODD_IN // 4), (1, 0, 2)).reshape(D_MODEL, ODD_IN)
        return _prepare_weights({"even_w_out": lands[0].reshape(2 * D_MODEL, D_MODEL), "odd_w_in": odd_w_in,
                                 "odd_w_out": lands[2].reshape(D_MODEL, D_MODEL)})

    pending = {}

    def slab(a):
        return a.reshape((4, 2, a.shape[1] // 2) + a.shape[2:])

    def begin(tag, slabs, dtypes):
        got = exchange_with_sibling(slabs, "grad_sibling_" + tag)
        sums = [add_sibling(a, b, core, dt, "grad_add_sibling_%s%d" % (tag, i))
                for i, (a, b, dt) in enumerate(zip(slabs, got, dtypes))]
        pending[tag] = exchange_with_chips_start(sums, "grad_chips_start_" + tag)
        return pending[tag][-1][0, 0]

    def finish(tag, after):
        sums, got = exchange_with_chips_wait(pending[tag], after, "grad_chips_wait_" + tag)
        return [add_chips(a, b, chip, "grad_add_chips_%s%d" % (tag, i)) for i, (a, b) in enumerate(zip(sums, got))]

    def reduce_first(g):
        return begin("a", [slab(jnp.transpose(g["odd_w_in"].reshape(D_MODEL, 4, ODD_IN // 4), (1, 0, 2))),
                           slab(g["odd_w_out"].reshape(4, D_MODEL // 4, D_MODEL)),
                           slab(g["even_w_out"].reshape(4, D_MODEL // 2, D_MODEL))], [BF16] * 3)

    def reduce_second(g):
        pending["totals_a"] = finish("a", g["even_w_in"])
        rep_rows = _pack(g, REPLICATED, REPLICATED_ROWS).reshape(4, 2, REP_PART, LANES)
        sh_rows = _pack({n: _split_owners(g[n]) for n, _ in SHARDED_SMALL}, SHARDED_SMALL, SHARDED_ROWS, lead=(4,))
        pack = jnp.concatenate([sh_rows.reshape(4, 2, HALF_SHARDED, LANES), rep_rows], axis=2)
        return begin("b", [slab(g["even_w_in"]), pack], [BF16, F32])

    loss, grad_x, g = local_step(x[0], loss_target[0], _prepare_weights(full), reduce_first, reduce_second,
                                 late_weights)
    odd_w_in_t, odd_w_out_t, even_w_out_t = pending["totals_a"]
    even_w_in_t, pack_t = finish("b", grad_x)
    totals = [even_w_in_t, even_w_out_t, odd_w_in_t, odd_w_out_t]
    last_part = jnp.concatenate([_pack(g, LAST_REPLICATED, LAST_ROWS), loss])
    from_core, rep_all, last_all = share_totals(totals, pack_t, last_part)
    me = 2 * chip[0] + core[0]
    mine, theirs = pack_t[:HALF_SHARDED], from_core[4][:HALF_SHARDED]
    sh_total = jnp.where(mc == 0, jnp.concatenate([mine, theirs]), jnp.concatenate([theirs, mine]))
    rep_all = lax.dynamic_update_slice(rep_all, pack_t[None, HALF_SHARDED:], (me, 0, 0))
    rep_total = rep_all.reshape(REPLICATED_ROWS, LANES)
    last_total = sum_parts(lax.dynamic_update_slice(last_all, last_part[None], (me, 0, 0)), "grad_sum_last")
    last_total, loss = last_total[:LAST_ROWS], last_total[LAST_ROWS, 0]
    grads = {}

    delta, new_m, new_v = {}, {}, {}
    for i, n in enumerate(BIG):
        if shard[n].shape[1] % LANES:
            outs = adamw_halves(shard[n].T, totals[i].T, from_core[i].T, m_in[n].T, v_in[n].T, core, "adamw_" + n,
                                by_columns=True)
            grads[n], delta[n], new_m[n], new_v[n] = [o.T for o in outs]
        else:
            grads[n], delta[n], new_m[n], new_v[n] = adamw_halves(shard[n], totals[i], from_core[i], m_in[n],
                                                                  v_in[n], core, "adamw_" + n)
    gate = [src["rg_gate_w"].reshape(GATE_ROWS, LANES) for src in (shard, m_in, v_in)]
    grads["rg_gate_w"], delta["rg_gate_w"], new_m["rg_gate_w"], new_v["rg_gate_w"] = adamw(
        gate[0], rep_total, gate[1], gate[2], "adamw_rg_gate_w")
    rest = REPLICATED[1:]
    rest_rows = sum(_seg_rows(shape) for _, shape in rest)
    small = ((SHARDED_SMALL, SHARDED_ROWS), (rest, rest_rows), (LAST_REPLICATED, LAST_ROWS))
    packed = [jnp.concatenate([_pack(src, spec, rows) for spec, rows in small]) for src in (shard, m_in, v_in)]
    small_g = jnp.concatenate([sh_total, rep_total[GATE_ROWS:GATE_ROWS + rest_rows], last_total], axis=0)
    outs = adamw(packed[0], small_g, packed[1], packed[2], "adamw_small")
    for dst, packed_rows in zip((grads, delta, new_m, new_v), outs):
        at = 0
        for spec, rows in small:
            dst.update(_unpack(packed_rows[at:at + rows], spec))
            at += rows
    result = [loss, grad_x[None]]
    for group in (grads, delta, new_m, new_v):
        result += [group[n].reshape(given[n].shape) for n in WEIGHTS]
    return tuple(result)
```

```python
import functools

import jax
import jax.numpy as jnp
from jax import lax
from jax.experimental import pallas as pl
from jax.experimental.pallas import tpu as pltpu

F32 = jnp.float32
BF16 = jnp.bfloat16
MESH = pl.DeviceIdType.MESH

D_MODEL = 1024
NORM_EPS = 1e-6
RG_HEADS = 8
RG_HEAD_DIM = 128
RG_C = 8.0
EVEN_IN = 6144
ODD_IN = 3104
ODD_IN_PAD = 3200
GLA_HEADS = 4
GLA_DK = 128
GLA_DV = 256
GLA_RANK = 16
GLA_NORMALIZER = 16.0
GLA_CHUNK = 128
LR_COL = 3072

ADAM_LR = 0.001
ADAM_B1 = 0.9
ADAM_B2 = 0.999
ADAM_EPS = 1e-08
ADAM_WD = 0.01
ADAM_STEP = 10

SUBLANES = 8
LANES = 128
VMEM_LIMIT = 56 * 2 ** 20

ROW_TILE = 512
SCAN_TILE = 256
GLA_BLOCK = 1024
MIX_TILE = 128


def _params(*sem):
    return pltpu.CompilerParams(dimension_semantics=sem, vmem_limit_bytes=VMEM_LIMIT)


def _full(shape):
    n = len(shape)
    return pl.BlockSpec(shape, lambda *_: (0,) * n)


def _sigmoid(x):
    return 0.5 + 0.5 * jnp.tanh(0.5 * x)


def _softplus(x):
    return jnp.maximum(x, 0.0) + jnp.log(1.0 + jnp.exp(-jnp.abs(x)))


def _dot(a, b):
    return jnp.dot(a, b, preferred_element_type=F32)


def _dot_nt(a, b):
    return lax.dot_general(a, b, (((1,), (1,)), ((), ())), preferred_element_type=F32)


def _dot_tn(a, b):
    return lax.dot_general(a, b, (((0,), (0,)), ((), ())), preferred_element_type=F32)


def _bdot(a, b, ca, cb):
    return lax.dot_general(a, b, (((ca,), (cb,)), ((0,), (0,))), preferred_element_type=F32)


def _halo_specs(rows, cols, col_block, n_row_tiles, tix):
    per = rows // SUBLANES
    last = n_row_tiles * per - 1

    def split(args):
        if len(args) == 2:
            return tix(args[1]), col_block + args[0]
        return tix(args[0]), col_block

    def prev(*args):
        t, c = split(args)
        return (jnp.maximum(t * per - 1, 0), c)

    def main(*args):
        return split(args)

    def nxt(*args):
        t, c = split(args)
        return (jnp.minimum((t + 1) * per, last), c)

    return [pl.BlockSpec((SUBLANES, cols), prev), pl.BlockSpec((rows, cols), main),
            pl.BlockSpec((SUBLANES, cols), nxt)]


def _extend(prev_ref, main_ref, next_ref, is_first, is_last):
    p = jnp.where(is_first, 0.0, prev_ref[...])
    n = jnp.where(is_last, 0.0, next_ref[...])
    return jnp.concatenate([p, main_ref[...], n], axis=0)


def _shifted(ext, offset, rows):
    if offset == 0:
        return ext[SUBLANES:SUBLANES + rows]
    n = ext.shape[0]
    return pltpu.roll(ext, (-offset) % n, 0)[SUBLANES:SUBLANES + rows]


def _conv(ext, w, left, rows):
    out = None
    for k in range(w.shape[0]):
        term = _shifted(ext, k - left, rows) * w[k:k + 1]
        out = term if out is None else out + term
    return out


def _conv_transpose(ext, w, left, rows):
    out = None
    for k in range(w.shape[0]):
        term = _shifted(ext, left - k, rows) * w[k:k + 1]
        out = term if out is None else out + term
    return out


def _colsum(x):
    return jnp.sum(x, axis=0, keepdims=True)


def _accumulate(ref, value, step):
    @pl.when(step == 0)
    def _():
        ref[...] = value

    @pl.when(step > 0)
    def _():
        ref[...] += value


PROJ_TILE_BYTES = 7 * 2 ** 20


def _proj_row_tile(rows, width):
    tm = min(ROW_TILE, rows)
    while tm * width * 4 > PROJ_TILE_BYTES and tm % (2 * SUBLANES) == 0:
        tm //= 2
    return tm


def norm_matmul(x, gain, w, name):
    rows, d = x.shape
    n_col_tiles, _, tn = w.shape
    tm = _proj_row_tile(rows, n_col_tiles * tn)

    def body(x_ref, g_ref, w_ref, proj_ref, h_ref):
        xv = x_ref[...]
        rstd = lax.rsqrt(jnp.mean(xv * xv, axis=-1, keepdims=True) + NORM_EPS)
        hv = (xv * rstd * g_ref[...]).astype(BF16)
        h_ref[...] = hv
        for j in range(n_col_tiles):
            proj_ref[:, j * tn:(j + 1) * tn] = _dot(hv, w_ref[j])

    row = lambda cols: pl.BlockSpec((tm, cols), lambda i: (i, 0))
    return pl.pallas_call(
        body, name=name,
        out_shape=(jax.ShapeDtypeStruct((rows, n_col_tiles * tn), F32), jax.ShapeDtypeStruct((rows, d), BF16)),
        grid=(rows // tm,),
        in_specs=[row(d), _full((1, d)), _full(w.shape)],
        out_specs=(row(n_col_tiles * tn), row(d)),
        compiler_params=_params("parallel"),
    )(x, gain, w)


def inproj_bwd(dproj, w, x, gain, dres, name):
    rows, d = x.shape
    n_col_tiles, _, tn = w.shape
    tm = _proj_row_tile(rows, n_col_tiles * tn)

    def body(dp_ref, w_ref, x_ref, g_ref, dres_ref, dx_ref, dg_ref):
        dh = None
        for j in range(n_col_tiles):
            part = _dot_nt(dp_ref[:, j * tn:(j + 1) * tn], w_ref[j])
            dh = part if dh is None else dh + part
        _inproj_finish(dh, x_ref, g_ref, dres_ref, dx_ref, dg_ref, pl.program_id(0))

    row = lambda cols: pl.BlockSpec((tm, cols), lambda i: (i, 0))
    return pl.pallas_call(
        body, name=name,
        out_shape=(jax.ShapeDtypeStruct((rows, d), F32), jax.ShapeDtypeStruct((1, d), F32)),
        grid=(rows // tm,),
        in_specs=[row(n_col_tiles * tn), _full(w.shape), row(d), _full((1, d)), row(d)],
        out_specs=(row(d), _full((1, d))),
        compiler_params=_params("arbitrary"),
    )(dproj, w, x, gain, dres)


def _inproj_finish(dh, x_ref, g_ref, dres_ref, dx_ref, dg_ref, step):
    xv = x_ref[...]
    rstd = lax.rsqrt(jnp.mean(xv * xv, axis=-1, keepdims=True) + NORM_EPS)
    xhat = xv * rstd
    dxn = dh * g_ref[...]
    dx_ref[...] = dres_ref[...] + rstd * (dxn - xhat * jnp.mean(dxn * xhat, axis=-1, keepdims=True))
    _accumulate(dg_ref, _colsum(dh * xhat), step)


def inproj_bwd_pieces(pieces, w, x, gain, dres, name):
    rows, d = x.shape
    tm = min(ROW_TILE, rows)
    n = len(pieces)
    widths = [p.shape[1] for p in pieces]
    starts = [sum(widths[:k]) for k in range(n)]
    assert sum(widths) == w.shape[2]

    def body(*refs):
        w_ref, x_ref, g_ref, dres_ref, dx_ref, dg_ref = refs[n:]
        dh = None
        for k in range(n):
            part = _dot_nt(refs[k][...], w_ref[0, :, starts[k]:starts[k] + widths[k]])
            dh = part if dh is None else dh + part
        _inproj_finish(dh, x_ref, g_ref, dres_ref, dx_ref, dg_ref, pl.program_id(0))

    row = lambda cols: pl.BlockSpec((tm, cols), lambda i: (i, 0))
    return pl.pallas_call(
        body, name=name,
        out_shape=(jax.ShapeDtypeStruct((rows, d), F32), jax.ShapeDtypeStruct((1, d), F32)),
        grid=(rows // tm,),
        in_specs=[row(wd) for wd in widths] + [_full(w.shape), row(d), _full((1, d)), row(d)],
        out_specs=(row(d), _full((1, d))),
        compiler_params=_params("arbitrary"),
    )(*pieces, w, x, gain, dres)


def matmul_dw_pieces(a, pieces, name):
    rows, m = a.shape
    tk = min(2 * ROW_TILE, rows)
    n = len(pieces)

    def body(*refs):
        a_ref, ins, outs = refs[0], refs[1:1 + n], refs[1 + n:]
        av = a_ref[...]
        for k in range(n):
            _accumulate(outs[k], _dot_tn(av, ins[k][...]), pl.program_id(0))

    return pl.pallas_call(
        body, name=name,
        out_shape=[jax.ShapeDtypeStruct((m, p.shape[1]), F32) for p in pieces],
        grid=(rows // tk,),
        in_specs=[pl.BlockSpec((tk, m), lambda k: (k, 0))]
        + [pl.BlockSpec((tk, p.shape[1]), lambda k: (k, 0)) for p in pieces],
        out_specs=[_full((m, p.shape[1])) for p in pieces],
        compiler_params=_params("arbitrary"),
    )(a, *pieces)


def matmul_dw(a, b, bn, name):
    rows, m = a.shape
    n = b.shape[1]
    tk = min(4 * ROW_TILE, rows)
    steps = rows // tk

    def body(a_ref, b_ref, o_ref):
        part = _dot_tn(a_ref[...], b_ref[...])

        @pl.when(pl.program_id(1) == 0)
        def _():
            o_ref[0] = part

        @pl.when(pl.program_id(1) > 0)
        def _():
            o_ref[0] += part

    return pl.pallas_call(
        body, name=name,
        out_shape=jax.ShapeDtypeStruct((n // bn, m, bn), F32),
        grid=(n // bn, steps),
        in_specs=[pl.BlockSpec((tk, m), lambda j, k: (k, 0)), pl.BlockSpec((tk, bn), lambda j, k: (k, j))],
        out_specs=pl.BlockSpec((1, m, bn), lambda j, k: (j, 0, 0)),
        compiler_params=_params("parallel", "arbitrary"),
    )(a, b)


def _scan(a, b, carry, reverse):
    n, c = a.shape
    blocks = n // SUBLANES
    a = a.reshape(blocks, SUBLANES, c)
    b = b.reshape(blocks, SUBLANES, c)
    pos = lax.broadcasted_iota(jnp.int32, (1, SUBLANES, c), 1)
    s = 1
    while s < SUBLANES:
        shift, valid = (SUBLANES - s, pos < SUBLANES - s) if reverse else (s, pos >= s)
        a_s, b_s = pltpu.roll(a, shift, 1), pltpu.roll(b, shift, 1)
        b = jnp.where(valid, a * b_s + b, b)
        a = jnp.where(valid, a * a_s, a)
        s *= 2
    out = [None] * blocks
    for k in (range(blocks - 1, -1, -1) if reverse else range(blocks)):
        h = a[k] * carry + b[k]
        out[k] = h
        carry = h[0:1] if reverse else h[SUBLANES - 1:SUBLANES]
    return jnp.concatenate(out, axis=0)


def _rg_gates(ua, gw_ref, gb, lam):
    ub = ua.astype(BF16)
    pre_r, pre_i = [], []
    for h in range(RG_HEADS):
        z = _dot(ub[:, h * RG_HEAD_DIM:(h + 1) * RG_HEAD_DIM], gw_ref[h])
        pre_r.append(z[:, :RG_HEAD_DIM])
        pre_i.append(z[:, RG_HEAD_DIM:])
    r = _sigmoid(jnp.concatenate(pre_r, axis=1) + gb[0:1])
    i = _sigmoid(jnp.concatenate(pre_i, axis=1) + gb[1:2])
    sp = _softplus(-lam)
    log_a = -RG_C * r * sp
    a = jnp.exp(log_a)
    mult = jnp.sqrt(1.0 - a * a)
    return r, i, sp, a, mult


def _rg_weight_specs():
    return [_full((4, D_MODEL)), _full((1, D_MODEL)), _full((RG_HEADS, RG_HEAD_DIM, 2 * RG_HEAD_DIM)),
            _full((2, D_MODEL)), _full((1, D_MODEL))]


def rglru_fwd(proj, conv_w, conv_b, gate_w, gate_b, lam, reverse, name):
    rows_total = proj.shape[0]
    rows = min(SCAN_TILE, rows_total)
    n_tiles = rows_total // rows
    tix = (lambda i: n_tiles - 1 - i) if reverse else (lambda i: i)

    def body(xp, xm, xn, cw_ref, cb_ref, gw_ref, gb_ref, lam_ref, h_ref, acts_ref, carry):
        i = pl.program_id(0)
        t = tix(i)
        ext = _extend(xp, xm, xn, t == 0, t == n_tiles - 1)
        ua = _conv(ext, cw_ref[...], 2, rows) + cb_ref[...]
        r, gi, _, a, mult = _rg_gates(ua, gw_ref, gb_ref[...], lam_ref[...])
        for k, saved in enumerate((ua, r, gi, a, mult)):
            acts_ref[k] = saved
        b = mult * (gi * ua)

        @pl.when(i == 0)
        def _():
            carry[...] = jnp.zeros_like(carry)

        h = _scan(a, b, carry[0:1], reverse)
        h_ref[...] = h
        edge = h[0:1] if reverse else h[rows - 1:rows]
        carry[...] = jnp.broadcast_to(edge, carry.shape)

    return pl.pallas_call(
        body, name=name,
        out_shape=(jax.ShapeDtypeStruct((rows_total, D_MODEL), F32),
                   jax.ShapeDtypeStruct((5, rows_total, D_MODEL), F32)),
        grid=(n_tiles,),
        in_specs=_halo_specs(rows, D_MODEL, 0, n_tiles, tix) + _rg_weight_specs(),
        out_specs=(pl.BlockSpec((rows, D_MODEL), lambda i: (tix(i), 0)),
                   pl.BlockSpec((5, rows, D_MODEL), lambda i: (0, tix(i), 0))),
        scratch_shapes=[pltpu.VMEM((SUBLANES, D_MODEL), F32)],
        compiler_params=_params("arbitrary"),
    )(proj, proj, proj, conv_w, conv_b, gate_w, gate_b, lam)


def rglru_bwd(proj, dycat, h_dir, acts, gate_w, lam, add_dua, reverse, name):
    rows_total = proj.shape[0]
    rows = min(SCAN_TILE, rows_total)
    n_tiles = rows_total // rows
    tix = (lambda i: i) if reverse else (lambda i: n_tiles - 1 - i)
    za_block = 1

    def body(acts_ref, za_ref, dya_ref, hp, hm, hn, gw_ref, lam_ref, *rest):
        other = rest[0][...] if add_dua is not None else 0.0
        dua_ref, dgw_ref, dgb_ref, dlam_ref, carry = rest[-5:]
        step = pl.program_id(0)
        t = tix(step)
        first, last = t == 0, t == n_tiles - 1
        ua, r, gi, a, mult = (acts_ref[k] for k in range(5))
        lam_v = lam_ref[...]
        sp = _softplus(-lam_v)
        za = za_ref[...]
        dh = dya_ref[...] * (za * _sigmoid(za))

        @pl.when(step == 0)
        def _():
            carry[...] = jnp.zeros_like(carry)

        old = carry[0:1]
        mu = _scan(a, a * dh, old, not reverse)
        row = lax.broadcasted_iota(jnp.int32, mu.shape, 0)
        if reverse:
            mu_next = jnp.where(row == 0, old, pltpu.roll(mu, 1, 0))
            carry[...] = jnp.broadcast_to(mu[rows - 1:rows], carry.shape)
            h_ext = _extend(hp, hm, hn, first, last)
            h_prev = _shifted(h_ext, 1, rows)
        else:
            mu_next = jnp.where(row == rows - 1, old, pltpu.roll(mu, rows - 1, 0))
            carry[...] = jnp.broadcast_to(mu[0:1], carry.shape)
            h_ext = _extend(hp, hm, hn, first, last)
            h_prev = _shifted(h_ext, -1, rows)
        db = dh + mu_next
        da = db * h_prev
        d_mult = db * (gi * ua)
        di = db * (mult * ua)
        dua = db * (mult * gi)
        dlog_a = da * a - d_mult * (a * a) / mult
        dr = dlog_a * (-RG_C * sp)
        dlam = _colsum(dlog_a * (-RG_C * r)) * (-_sigmoid(-lam_v))
        dpr = dr * (r * (1.0 - r))
        dpi = di * (gi * (1.0 - gi))
        dgb = jnp.concatenate([_colsum(dpr), _colsum(dpi)], axis=0)
        ub = ua.astype(BF16)
        dua_heads, dgw_heads = [], []
        for h in range(RG_HEADS):
            cols = slice(h * RG_HEAD_DIM, (h + 1) * RG_HEAD_DIM)
            dz = jnp.concatenate([dpr[:, cols], dpi[:, cols]], axis=1).astype(BF16)
            dgw_heads.append(_dot_tn(ub[:, cols], dz))
            dua_heads.append(_dot_nt(dz, gw_ref[h]))
        dua_ref[...] = dua + jnp.concatenate(dua_heads, axis=1) + other

        @pl.when(step == 0)
        def _():
            for h in range(RG_HEADS):
                dgw_ref[h] = dgw_heads[h]
            dgb_ref[...] = dgb
            dlam_ref[...] = dlam

        @pl.when(step > 0)
        def _():
            for h in range(RG_HEADS):
                dgw_ref[h] += dgw_heads[h]
            dgb_ref[...] += dgb
            dlam_ref[...] += dlam

    row_spec = lambda col: pl.BlockSpec((rows, D_MODEL), lambda i: (tix(i), col))
    return pl.pallas_call(
        body, name=name,
        out_shape=(jax.ShapeDtypeStruct((rows_total, D_MODEL), F32),
                   jax.ShapeDtypeStruct((RG_HEADS, RG_HEAD_DIM, 2 * RG_HEAD_DIM), F32),
                   jax.ShapeDtypeStruct((2, D_MODEL), F32), jax.ShapeDtypeStruct((1, D_MODEL), F32)),
        grid=(n_tiles,),
        in_specs=([pl.BlockSpec((5, rows, D_MODEL), lambda i: (0, tix(i), 0)), row_spec(za_block), row_spec(0)]
                  + _halo_specs(rows, D_MODEL, 0, n_tiles, tix)
                  + [_full((RG_HEADS, RG_HEAD_DIM, 2 * RG_HEAD_DIM)), _full((1, D_MODEL))]
                  + ([] if add_dua is None else [row_spec(0)])),
        out_specs=(row_spec(0), _full((RG_HEADS, RG_HEAD_DIM, 2 * RG_HEAD_DIM)), _full((2, D_MODEL)),
                   _full((1, D_MODEL))),
        scratch_shapes=[pltpu.VMEM((SUBLANES, D_MODEL), F32)],
        compiler_params=_params("arbitrary"),
    )(acts, proj, dycat, h_dir, h_dir, h_dir, gate_w, lam, *([] if add_dua is None else [add_dua]))


def even_mix_fwd(proj, h_f, h_b, sc_w, name):
    rows_total = proj.shape[0]
    rows = min(2 * MIX_TILE, rows_total)
    n_tiles = rows_total // rows
    cb = D_MODEL
    n_cb = 1
    ident = lambda i: i

    def body(za_ref, hf_ref, hb_ref, xbp, xbm, xbn, gcp, gcm, gcn, gb_ref, zb_ref, w_ref, y_ref):
        t = pl.program_id(1)
        first, last = t == 0, t == n_tiles - 1
        za = za_ref[...]
        y_ref[:, 0:cb] = ((hf_ref[...] + hb_ref[...]) * (za * _sigmoid(za))).astype(BF16)
        p_ext = _extend(xbp, xbm, xbn, first, last) * _extend(gcp, gcm, gcn, first, last)
        cv = _conv(p_ext, w_ref[...], 1, rows)
        zb = zb_ref[...]
        y_ref[:, cb:2 * cb] = (gb_ref[...] * cv * (zb * _sigmoid(zb))).astype(BF16)

    blk = lambda col: pl.BlockSpec((rows, cb), lambda c, i: (i, col * n_cb + c))
    own = pl.BlockSpec((rows, cb), lambda c, i: (i, c))
    return pl.pallas_call(
        body, name=name,
        out_shape=jax.ShapeDtypeStruct((rows_total, 2 * D_MODEL), BF16),
        grid=(n_cb, n_tiles),
        in_specs=([blk(1), own, own] + _halo_specs(rows, cb, 2 * n_cb, n_tiles, ident)
                  + _halo_specs(rows, cb, 4 * n_cb, n_tiles, ident)
                  + [blk(3), blk(5), pl.BlockSpec((3, cb), lambda c, i: (0, c))]),
        out_specs=pl.BlockSpec((rows, 2 * cb), lambda c, i: (i, 0)),
        compiler_params=_params("parallel", "arbitrary"),
    )(proj, h_f, h_b, proj, proj, proj, proj, proj, proj, proj, proj, sc_w)


def even_mix_bwd(proj, dycat, h_f, h_b, dua, conv_w, sc_w, name):
    rows_total = proj.shape[0]
    rows = min(MIX_TILE, rows_total)
    n_tiles = rows_total // rows
    cb = D_MODEL
    n_cb = 1
    ident = lambda i: i

    def body(xap, xam, xan, za_ref, xbp, xbm, xbn, gbp, gbm, gbn, gcp, gcm, gcn, zbp, zbm, zbn,
             dya_ref, dybp, dybm, dybn, hf_ref, hb_ref, dup, dum, dun, cw_ref, sw_ref,
             dp_ref, dcw_ref, dcb_ref, dsw_ref):
        def put(k, value):
            dp_ref[:, k * cb:(k + 1) * cb] = value.astype(BF16)

        t = pl.program_id(1)
        first, last = t == 0, t == n_tiles - 1
        za = za_ref[...]
        sa = _sigmoid(za)
        put(1, dya_ref[...] * (hf_ref[...] + hb_ref[...]) * (sa * (1.0 + za * (1.0 - sa))))
        dua_ext = _extend(dup, dum, dun, first, last)
        cw = cw_ref[...]
        put(0, _conv_transpose(dua_ext, cw, 2, rows))
        dua = dua_ext[SUBLANES:SUBLANES + rows]
        xa_ext = _extend(xap, xam, xan, first, last)
        dcw = jnp.concatenate([_colsum(dua * _shifted(xa_ext, k - 2, rows)) for k in range(4)], axis=0)
        dcb = _colsum(dua)
        xb_ext = _extend(xbp, xbm, xbn, first, last)
        gc_ext = _extend(gcp, gcm, gcn, first, last)
        p_ext = xb_ext * gc_ext
        zb_ext = _extend(zbp, zbm, zbn, first, last)
        sb_ext = _sigmoid(zb_ext)
        dyb_ext = _extend(dybp, dybm, dybn, first, last)
        gb_ext = _extend(gbp, gbm, gbn, first, last)
        dcv_ext = dyb_ext * gb_ext * (zb_ext * sb_ext)
        sw = sw_ref[...]
        p_at = [_shifted(p_ext, k - 1, rows) for k in range(3)]
        cv = (p_at[0] * sw[0:1] + p_at[1] * sw[1:2]) + p_at[2] * sw[2:3]
        mid = slice(SUBLANES, SUBLANES + rows)
        zb, sb, dyb, gb = zb_ext[mid], sb_ext[mid], dyb_ext[mid], gb_ext[mid]
        put(3, dyb * cv * (zb * sb))
        put(5, dyb * gb * cv * (sb * (1.0 + zb * (1.0 - sb))))
        dp = _conv_transpose(dcv_ext, sw, 1, rows)
        put(4, dp * xb_ext[mid])
        put(2, dp * gc_ext[mid])
        dcv = dcv_ext[mid]
        dsw = jnp.concatenate([_colsum(dcv * p_at[k]) for k in range(3)], axis=0)

        @pl.when(t == 0)
        def _():
            dcw_ref[...] = dcw
            dcb_ref[...] = dcb
            dsw_ref[...] = dsw

        @pl.when(t > 0)
        def _():
            dcw_ref[...] += dcw
            dcb_ref[...] += dcb
            dsw_ref[...] += dsw

    blk = lambda col: pl.BlockSpec((rows, cb), lambda c, i: (i, col * n_cb + c))
    halo = lambda col: _halo_specs(rows, cb, col * n_cb, n_tiles, ident)
    own = pl.BlockSpec((rows, cb), lambda c, i: (i, c))
    wspec = lambda k: pl.BlockSpec((k, cb), lambda c, i: (0, c))
    return pl.pallas_call(
        body, name=name,
        out_shape=(jax.ShapeDtypeStruct((rows_total, 6 * D_MODEL), BF16),
                   jax.ShapeDtypeStruct((4, D_MODEL), F32), jax.ShapeDtypeStruct((1, D_MODEL), F32),
                   jax.ShapeDtypeStruct((3, D_MODEL), F32)),
        grid=(n_cb, n_tiles),
        in_specs=(halo(0) + [blk(1)] + halo(2) + halo(3) + halo(4) + halo(5) + [blk(0)] + halo(1)
                  + [own, own] + halo(0) + [wspec(4), wspec(3)]),
        out_specs=(pl.BlockSpec((rows, 6 * cb), lambda c, i: (i, 0)), wspec(4), wspec(1), wspec(3)),
        compiler_params=_params("parallel", "arbitrary"),
    )(proj, proj, proj, proj, proj, proj, proj, proj, proj, proj, proj, proj, proj, proj, proj, proj,
      dycat, dycat, dycat, dycat, h_f, h_b, dua, dua, dua, conv_w, sc_w)


def even_out_fwd(ycat, w_out, gain, x, name):
    rows, d = x.shape
    k = ycat.shape[1]
    tm = min(ROW_TILE, rows)

    def body(yc_ref, w_ref, g_ref, x_ref, x1_ref, y_ref):
        y = _dot(yc_ref[...], w_ref[...])
        y_ref[...] = y
        rstd = lax.rsqrt(jnp.mean(y * y, axis=-1, keepdims=True) + NORM_EPS)
        x1_ref[...] = x_ref[...] + y * rstd * g_ref[...]

    row = lambda n: pl.BlockSpec((tm, n), lambda i: (i, 0))
    return pl.pallas_call(
        body, name=name,
        out_shape=(jax.ShapeDtypeStruct((rows, d), F32),) * 2,
        grid=(rows // tm,),
        in_specs=[row(k), _full((k, d)), _full((1, d)), row(d)],
        out_specs=(row(d), row(d)),
        compiler_params=_params("parallel"),
    )(ycat, w_out, gain, x)


def _rmsnorm_bwd(dout, y, gain):
    rstd = lax.rsqrt(jnp.mean(y * y, axis=-1, keepdims=True) + NORM_EPS)
    yhat = y * rstd
    dyn = dout * gain
    dy = rstd * (dyn - yhat * jnp.mean(dyn * yhat, axis=-1, keepdims=True))
    return dy, dout * yhat


def even_out_bwd(dx1, y, gain, w_out, name):
    rows, d = y.shape
    k = w_out.shape[0]
    tm = min(ROW_TILE, rows)

    def body(dx_ref, y_ref, g_ref, w_ref, dy_ref, dyc_ref, dg_ref):
        dy, dg_rows = _rmsnorm_bwd(dx_ref[...], y_ref[...], g_ref[...])
        dyb = dy.astype(BF16)
        dy_ref[...] = dyb
        dyc_ref[...] = _dot_nt(dyb, w_ref[...])
        _accumulate(dg_ref, _colsum(dg_rows), pl.program_id(0))

    row = lambda n: pl.BlockSpec((tm, n), lambda i: (i, 0))
    return pl.pallas_call(
        body, name=name,
        out_shape=(jax.ShapeDtypeStruct((rows, d), BF16), jax.ShapeDtypeStruct((rows, k), F32),
                   jax.ShapeDtypeStruct((1, d), F32)),
        grid=(rows // tm,),
        in_specs=[row(d), row(d), _full((1, d)), _full((k, d))],
        out_specs=(row(d), row(k), _full((1, d))),
        compiler_params=_params("arbitrary"),
    )(dx1, y, gain, w_out)


def _chunk_cumsum(g, reverse):
    n, c = g.shape
    chunks, per = n // GLA_CHUNK, GLA_CHUNK // SUBLANES
    g = g.reshape(n // SUBLANES, SUBLANES, c)
    pos = lax.broadcasted_iota(jnp.int32, (1, SUBLANES, c), 1)
    s = 1
    while s < SUBLANES:
        if reverse:
            g = g + jnp.where(pos < SUBLANES - s, pltpu.roll(g, SUBLANES - s, 1), 0.0)
        else:
            g = g + jnp.where(pos >= s, pltpu.roll(g, s, 1), 0.0)
        s *= 2
    g = g.reshape(chunks, per, SUBLANES, c)
    out, carry = [None] * per, None
    for k in (range(per - 1, -1, -1) if reverse else range(per)):
        out[k] = g[:, k] if carry is None else g[:, k] + carry
        carry = out[k][:, 0:1] if reverse else out[k][:, SUBLANES - 1:SUBLANES]
    return jnp.stack(out, axis=1).reshape(n, c)


def _gla_prepare(q_ref, k_ref, lr_ref, wg_ref, bg_ref, reverse, n_chunks):
    z = _dot(lr_ref[...].astype(BF16), wg_ref[0]) + bg_ref[0]
    g = -_softplus(-z) * (1.0 / GLA_NORMALIZER)
    bcum = _chunk_cumsum(g, reverse).reshape(n_chunks, GLA_CHUNK, GLA_DK)
    edge = 0 if reverse else GLA_CHUNK - 1
    btot = bcum[:, edge:edge + 1, :]
    e_pos = jnp.exp(bcum)
    e_neg = jnp.exp(-bcum)
    e_st = jnp.exp(btot - bcum)
    q3 = q_ref[...].reshape(n_chunks, GLA_CHUNK, GLA_DK)
    k3 = k_ref[...].reshape(n_chunks, GLA_CHUNK, GLA_DK)
    scale = GLA_DK ** -0.5
    q_in = q3 * scale * e_pos
    k_in = k3 * e_neg
    k_st = k3 * e_st
    dec = jnp.exp(btot)
    return z, q_in, k_in, k_st, dec, (scale * e_pos, e_neg, e_st)


def _gla_mask(reverse):
    i = lax.broadcasted_iota(jnp.int32, (GLA_CHUNK, GLA_CHUNK), 0)
    j = lax.broadcasted_iota(jnp.int32, (GLA_CHUNK, GLA_CHUNK), 1)
    return (j >= i) if reverse else (j <= i)


def _gla_specs(rows, n_blocks, reverse):
    tix = (lambda s: n_blocks - 1 - s) if reverse else (lambda s: s)
    d = 1 if reverse else 0
    lr_block = LR_COL // LANES
    specs = [pl.BlockSpec((rows, GLA_DK), lambda h, s: (tix(s), h)),
             pl.BlockSpec((rows, GLA_DK), lambda h, s: (tix(s), GLA_HEADS + h)),
             pl.BlockSpec((rows, GLA_DV), lambda h, s: (tix(s), GLA_HEADS + h)),
             pl.BlockSpec((rows, LANES), lambda h, s: (tix(s), lr_block)),
             pl.BlockSpec((1, LANES, GLA_DK), lambda h, s: (d, 0, h)),
             pl.BlockSpec((1, 1, GLA_DK), lambda h, s: (d, 0, h))]
    return specs, tix


def gla_fwd(proj, wg_pad, bg, reverse, name):
    rows_total = proj.shape[0]
    rows = min(GLA_BLOCK, rows_total)
    n_blocks = rows_total // rows
    n_chunks = rows // GLA_CHUNK
    specs, tix = _gla_specs(rows, n_blocks, reverse)

    def body(q_ref, k_ref, v_ref, lr_ref, wg_ref, bg_ref, o_ref, st_ref, state, kv_scr, dec_scr):
        _, q_in, k_in, k_st, dec, _ = _gla_prepare(q_ref, k_ref, lr_ref, wg_ref, bg_ref, reverse, n_chunks)
        vb = v_ref[...].reshape(n_chunks, GLA_CHUNK, GLA_DV).astype(BF16)
        qb = q_in.astype(BF16)
        p = jnp.where(_gla_mask(reverse), _bdot(qb, k_in.astype(BF16), 2, 2), 0.0)
        o = _bdot(p.astype(BF16), vb, 2, 1)
        kv_scr[...] = _bdot(vb, k_st.astype(BF16), 1, 1)
        dec_scr[...] = jnp.broadcast_to(dec, dec_scr.shape)

        @pl.when(pl.program_id(1) == 0)
        def _():
            state[...] = jnp.zeros_like(state)

        for c in range(n_chunks):
            cc = n_chunks - 1 - c if reverse else c
            st_ref[0, cc] = state[...]
            state[...] = state[...] * dec_scr[cc, 0:1] + kv_scr[cc]
        o = o + _bdot(qb, st_ref[0].astype(BF16), 2, 2)
        o_ref[...] = o.reshape(rows, GLA_DV)

    return pl.pallas_call(
        body, name=name,
        out_shape=(jax.ShapeDtypeStruct((rows_total, GLA_HEADS * GLA_DV), F32),
                   jax.ShapeDtypeStruct((GLA_HEADS, rows_total // GLA_CHUNK, GLA_DV, GLA_DK), F32)),
        grid=(GLA_HEADS, n_blocks),
        in_specs=specs,
        out_specs=(pl.BlockSpec((rows, GLA_DV), lambda h, s: (tix(s), h)),
                   pl.BlockSpec((1, n_chunks, GLA_DV, GLA_DK), lambda h, s: (h, tix(s), 0, 0))),
        scratch_shapes=[pltpu.VMEM((GLA_DV, GLA_DK), F32), pltpu.VMEM((n_chunks, GLA_DV, GLA_DK), F32),
                        pltpu.VMEM((n_chunks, SUBLANES, GLA_DK), F32)],
        compiler_params=_params("parallel", "arbitrary"),
    )(proj, proj, proj, proj, wg_pad, bg)


def gla_bwd(proj, wg_pad, bg, d_o, states, dqkv_in, reverse, name):
    rows_total = proj.shape[0]
    rows = min(GLA_BLOCK, rows_total)
    n_blocks = rows_total // rows
    n_chunks = rows // GLA_CHUNK
    specs, tix = _gla_specs(rows, n_blocks, not reverse)
    d = 1 if reverse else 0
    specs[4] = pl.BlockSpec((1, LANES, GLA_DK), lambda h, s: (d, 0, h))
    specs[5] = pl.BlockSpec((1, 1, GLA_DK), lambda h, s: (d, 0, h))
    add = dqkv_in is not None

    def body(*refs):
        q_ref, k_ref, v_ref, lr_ref, wg_ref, bg_ref, do_ref, st_ref = refs[:8]
        refs = refs[8:]
        if add:
            aq_ref, ak_ref, av_ref = refs[:3]
            refs = refs[3:]
        dq_ref, dk_ref, dv_ref, dz_ref, dstate, g_scr, dec_scr, dsn_scr = refs
        z, q_in, k_in, k_st, dec, (f_q, f_k, f_s) = _gla_prepare(q_ref, k_ref, lr_ref, wg_ref, bg_ref, reverse,
                                                                 n_chunks)
        mask = _gla_mask(reverse)
        vb = v_ref[...].reshape(n_chunks, GLA_CHUNK, GLA_DV).astype(BF16)
        dob = do_ref[...].reshape(n_chunks, GLA_CHUNK, GLA_DV).astype(BF16)
        qb, kb, ksb = q_in.astype(BF16), k_in.astype(BF16), k_st.astype(BF16)
        st = st_ref[0]
        stb = st.astype(BF16)
        pb = jnp.where(mask, _bdot(qb, kb, 2, 2), 0.0).astype(BF16)
        dpb = jnp.where(mask, _bdot(dob, vb, 2, 2), 0.0).astype(BF16)
        d_qin = _bdot(dpb, kb, 2, 1) + _bdot(dob, stb, 2, 1)
        d_kin = _bdot(dpb, qb, 1, 1)
        dv = _bdot(pb, dob, 1, 1)
        g_scr[...] = _bdot(dob, qb, 1, 1)
        dec_scr[...] = jnp.broadcast_to(dec, dec_scr.shape)

        @pl.when(pl.program_id(1) == 0)
        def _():
            dstate[...] = jnp.zeros_like(dstate)

        for c in range(n_chunks):
            cc = c if reverse else n_chunks - 1 - c
            dsn_scr[cc] = dstate[...]
            dstate[...] = dstate[...] * dec_scr[cc, 0:1] + g_scr[cc]
        dsn = dsn_scr[...]
        dsnb = dsn.astype(BF16)
        dv = dv + _bdot(ksb, dsnb, 2, 2)
        d_kst = _bdot(vb, dsnb, 2, 1)
        d_dec = jnp.sum(dsn * st, axis=1, keepdims=True)
        ks_term = d_kst * k_st
        d_btot = d_dec * dec + jnp.sum(ks_term, axis=1, keepdims=True)
        d_b = d_qin * q_in - d_kin * k_in - ks_term
        pos = lax.broadcasted_iota(jnp.int32, d_b.shape, 1)
        edge = 0 if reverse else GLA_CHUNK - 1
        d_b = d_b + jnp.where(pos == edge, d_btot, 0.0)
        dg = _chunk_cumsum(d_b.reshape(rows, GLA_DK), not reverse)
        dz_ref[...] = dg * (1.0 / GLA_NORMALIZER) * _sigmoid(-z)
        dq = (d_qin * f_q).reshape(rows, GLA_DK)
        dk = (d_kin * f_k + d_kst * f_s).reshape(rows, GLA_DK)
        dv = dv.reshape(rows, GLA_DV)
        if add:
            dq_ref[...] = (dq + aq_ref[...]).astype(BF16)
            dk_ref[...] = (dk + ak_ref[...]).astype(BF16)
            dv_ref[...] = (dv + av_ref[...]).astype(BF16)
        else:
            dq_ref[...] = dq
            dk_ref[...] = dk
            dv_ref[...] = dv

    qkv_specs = [pl.BlockSpec((rows, GLA_DK), lambda h, s: (tix(s), h)),
                 pl.BlockSpec((rows, GLA_DK), lambda h, s: (tix(s), h)),
                 pl.BlockSpec((rows, GLA_DV), lambda h, s: (tix(s), h))]
    in_specs = specs + [pl.BlockSpec((rows, GLA_DV), lambda h, s: (tix(s), h)),
                        pl.BlockSpec((1, n_chunks, GLA_DV, GLA_DK), lambda h, s: (h, tix(s), 0, 0))]
    args = [proj, proj, proj, proj, wg_pad, bg, d_o, states]
    out_dtype = F32
    if add:
        in_specs += qkv_specs
        args += list(dqkv_in)
        out_dtype = BF16
    return pl.pallas_call(
        body, name=name,
        out_shape=(jax.ShapeDtypeStruct((rows_total, GLA_HEADS * GLA_DK), out_dtype),
                   jax.ShapeDtypeStruct((rows_total, GLA_HEADS * GLA_DK), out_dtype),
                   jax.ShapeDtypeStruct((rows_total, GLA_HEADS * GLA_DV), out_dtype),
                   jax.ShapeDtypeStruct((rows_total, GLA_HEADS * GLA_DK), F32)),
        grid=(GLA_HEADS, n_blocks),
        in_specs=in_specs,
        out_specs=(pl.BlockSpec((rows, GLA_DK), lambda h, s: (tix(s), h)),
                   pl.BlockSpec((rows, GLA_DK), lambda h, s: (tix(s), h)),
                   pl.BlockSpec((rows, GLA_DV), lambda h, s: (tix(s), h)),
                   pl.BlockSpec((rows, GLA_DK), lambda h, s: (tix(s), h))),
        scratch_shapes=[pltpu.VMEM((GLA_DV, GLA_DK), F32), pltpu.VMEM((n_chunks, GLA_DV, GLA_DK), F32),
                        pltpu.VMEM((n_chunks, SUBLANES, GLA_DK), F32),
                        pltpu.VMEM((n_chunks, GLA_DV, GLA_DK), F32)],
        compiler_params=_params("parallel", "arbitrary"),
    )(*args)


def gla_gate_bwd(proj, dz_f, dz_b, wg_pad, name):
    rows_total = proj.shape[0]
    tm = min(ROW_TILE, rows_total)
    n_key = GLA_HEADS * GLA_DK

    def body(lr_ref, dzf_ref, dzb_ref, wg_ref, dlr_ref, dwg_ref, dbg_ref):
        step = pl.program_id(0)
        lr_t = jnp.transpose(lr_ref[...])
        dzf, dzb = dzf_ref[...], dzb_ref[...]
        dzf16, dzb16 = dzf.astype(BF16), dzb.astype(BF16)
        dlr_ref[...] = (_dot_nt(dzf16, wg_ref[0]) + _dot_nt(dzb16, wg_ref[1])).astype(BF16)
        dwf = _dot(lr_t[0:GLA_RANK].astype(BF16), dzf16)
        dwb = _dot(lr_t[GLA_RANK:2 * GLA_RANK].astype(BF16), dzb16)
        dbg = jnp.concatenate([_colsum(dzf), _colsum(dzb)], axis=0)

        @pl.when(step == 0)
        def _():
            dwg_ref[0] = dwf
            dwg_ref[1] = dwb
            dbg_ref[...] = dbg

        @pl.when(step > 0)
        def _():
            dwg_ref[0] += dwf
            dwg_ref[1] += dwb
            dbg_ref[...] += dbg

    return pl.pallas_call(
        body, name=name,
        out_shape=(jax.ShapeDtypeStruct((rows_total, LANES), BF16), jax.ShapeDtypeStruct((2, GLA_RANK, n_key), F32),
                   jax.ShapeDtypeStruct((2, n_key), F32)),
        grid=(rows_total // tm,),
        in_specs=[pl.BlockSpec((tm, LANES), lambda i: (i, LR_COL // LANES)),
                  pl.BlockSpec((tm, n_key), lambda i: (i, 0)), pl.BlockSpec((tm, n_key), lambda i: (i, 0)),
                  _full((2, LANES, n_key))],
        out_specs=(pl.BlockSpec((tm, LANES), lambda i: (i, 0)), _full((2, GLA_RANK, n_key)), _full((2, n_key))),
        compiler_params=_params("arbitrary"),
    )(proj, dz_f, dz_b, wg_pad)


def _head_norm(o, gain):
    outs, hats, rstds = [], [], []
    for h in range(GLA_HEADS):
        oh = o[:, h * GLA_DV:(h + 1) * GLA_DV]
        rstd = lax.rsqrt(jnp.mean(oh * oh, axis=-1, keepdims=True) + NORM_EPS)
        hat = oh * rstd
        outs.append(hat * gain)
        hats.append(hat)
        rstds.append(rstd)
    return outs, hats, rstds


def odd_out_fwd(o_f, o_b, proj, head_gain, w_out, gain, x1, target, name):
    rows, d = x1.shape
    tm = min(ROW_TILE, rows)
    r_block = (2 * GLA_HEADS * GLA_DK + GLA_HEADS * GLA_DV) // d

    def body(of_ref, ob_ref, r_ref, hg_ref, w_ref, g_ref, x1_ref, tgt_ref, y2_ref, dy_ref, dx2_ref, loss_ref,
             dg_ref):
        step = pl.program_id(0)
        on, _, _ = _head_norm(of_ref[...] + ob_ref[...], hg_ref[...])
        r = r_ref[...]
        y2 = (jnp.concatenate(on, axis=1) * (r * _sigmoid(r))).astype(BF16)
        y2_ref[...] = y2
        y = _dot(y2, w_ref[...])
        gain_v = g_ref[...]
        rstd = lax.rsqrt(jnp.mean(y * y, axis=-1, keepdims=True) + NORM_EPS)
        x2 = x1_ref[...] + y * rstd * gain_v
        diff = x2 - tgt_ref[...]
        loss = 0.5 * jnp.sum(jnp.mean(diff * diff, axis=-1, keepdims=True), axis=0, keepdims=True)
        dx2 = diff * (1.0 / d)
        dx2_ref[...] = dx2
        dy, dg_rows = _rmsnorm_bwd(dx2, y, gain_v)
        dy_ref[...] = dy.astype(BF16)
        _accumulate(loss_ref, jnp.broadcast_to(loss, loss_ref.shape), step)
        _accumulate(dg_ref, _colsum(dg_rows), step)

    row = lambda n, col=0: pl.BlockSpec((tm, n), lambda i: (i, col))
    return pl.pallas_call(
        body, name=name,
        out_shape=(jax.ShapeDtypeStruct((rows, d), BF16), jax.ShapeDtypeStruct((rows, d), BF16),
                   jax.ShapeDtypeStruct((rows, d), F32), jax.ShapeDtypeStruct((SUBLANES, LANES), F32),
                   jax.ShapeDtypeStruct((1, d), F32)),
        grid=(rows // tm,),
        in_specs=[row(d), row(d), row(d, r_block), _full((1, GLA_DV)), _full((d, d)), _full((1, d)), row(d), row(d)],
        out_specs=(row(d), row(d), row(d), _full((SUBLANES, LANES)), _full((1, d))),
        compiler_params=_params("arbitrary"),
    )(o_f, o_b, proj, head_gain, w_out, gain, x1, target)


def odd_out_bwd(dy, w_out, o_f, o_b, proj, head_gain, name):
    rows, d = dy.shape
    tm = min(ROW_TILE, rows)
    r_block = (2 * GLA_HEADS * GLA_DK + GLA_HEADS * GLA_DV) // d

    def body(dy_ref, w_ref, of_ref, ob_ref, r_ref, hg_ref, dr_ref, do_ref, dhg_ref):
        dy2 = _dot_nt(dy_ref[...], w_ref[...])
        hg = hg_ref[...]
        on, hats, rstds = _head_norm(of_ref[...] + ob_ref[...], hg)
        r = r_ref[...]
        sr = _sigmoid(r)
        dr_ref[...] = (dy2 * jnp.concatenate(on, axis=1) * (sr * (1.0 + r * (1.0 - sr)))).astype(BF16)
        d_on = dy2 * (r * sr)
        d_os, dhg = [], None
        for h in range(GLA_HEADS):
            dn = d_on[:, h * GLA_DV:(h + 1) * GLA_DV]
            part = _colsum(dn * hats[h])
            dhg = part if dhg is None else dhg + part
            dng = dn * hg
            d_os.append(rstds[h] * (dng - hats[h] * jnp.mean(dng * hats[h], axis=-1, keepdims=True)))
        do_ref[...] = jnp.concatenate(d_os, axis=1)
        _accumulate(dhg_ref, dhg, pl.program_id(0))

    row = lambda n, col=0: pl.BlockSpec((tm, n), lambda i: (i, col))
    return pl.pallas_call(
        body, name=name,
        out_shape=(jax.ShapeDtypeStruct((rows, d), BF16), jax.ShapeDtypeStruct((rows, d), F32),
                   jax.ShapeDtypeStruct((1, GLA_DV), F32)),
        grid=(rows // tm,),
        in_specs=[row(d), _full((d, d)), row(d), row(d), row(d, r_block), _full((1, GLA_DV))],
        out_specs=(row(d), row(d), _full((1, GLA_DV))),
        compiler_params=_params("arbitrary"),
    )(dy, w_out, o_f, o_b, proj, head_gain)


def local_step(x, target, w, reduce_first=None, reduce_second=None, late_weights=None):
    g = {}
    proj_e, h0 = norm_matmul(x, w["even_norm_pre"], w["even_w_in"], "even_in_proj")
    h_dir, acts = zip(*[rglru_fwd(proj_e, w["rg_conv_w"], w["rg_conv_b"], w["rg_gate_w"][d], w["rg_gate_b"][d],
                                  w["rg_lambda"][d], d == 1, "rglru_fwd_%d" % d) for d in range(2)])
    ycat = even_mix_fwd(proj_e, h_dir[0], h_dir[1], w["sc_conv_w"], "even_mix_fwd")
    if late_weights is not None:
        w = dict(w, **late_weights(ycat))
    x1, y_e = even_out_fwd(ycat, w["even_w_out"], w["even_norm_post"], x, "even_out_fwd")
    proj_o, h1 = norm_matmul(x1, w["odd_norm_pre"], w["odd_w_in"], "odd_in_proj")
    o_dir, st_dir = [], []
    for d in range(2):
        o, st = gla_fwd(proj_o, w["gla_wg_pad"], w["gla_b_gate"], d == 1, "gla_fwd_%d" % d)
        o_dir.append(o)
        st_dir.append(st)
    y2, dy_o, dx2, loss, g["odd_norm_post"] = odd_out_fwd(
        o_dir[0], o_dir[1], proj_o, w["gla_norm_g"], w["odd_w_out"], w["odd_norm_post"], x1, target, "odd_out_fwd")
    g["odd_w_out"] = matmul_dw(y2, dy_o, D_MODEL, "odd_w_out_grad")[0]
    dr, d_o, g["gla_norm_g"] = odd_out_bwd(dy_o, w["odd_w_out"], o_dir[0], o_dir[1], proj_o, w["gla_norm_g"],
                                           "odd_out_bwd")
    dq, dk, dv, dz_f = gla_bwd(proj_o, w["gla_wg_pad"], w["gla_b_gate"], d_o, st_dir[0], None, False, "gla_bwd_0")
    dq, dk, dv, dz_b = gla_bwd(proj_o, w["gla_wg_pad"], w["gla_b_gate"], d_o, st_dir[1], (dq, dk, dv), True,
                               "gla_bwd_1")
    dlr, g["gla_w_gate_lr"], g["gla_b_gate"] = gla_gate_bwd(proj_o, dz_f, dz_b, w["gla_wg_pad"], "gla_gate_bwd")
    dproj_o = [dq, dk, dv, dr, dlr]
    g["odd_w_in"] = jnp.concatenate(matmul_dw_pieces(h1, dproj_o, "odd_w_in_grad"), axis=1)[:, :ODD_IN]
    dx1, g["odd_norm_pre"] = inproj_bwd_pieces(dproj_o, w["odd_w_in"], x1, w["odd_norm_pre"], dx2, "odd_in_proj_bwd")
    dy_e, dycat, g["even_norm_post"] = even_out_bwd(dx1, y_e, w["even_norm_post"], w["even_w_out"], "even_out_bwd")
    g["even_w_out"] = matmul_dw(ycat, dy_e, D_MODEL, "even_w_out_grad")[0]
    lam = w["rg_lambda"] if reduce_first is None else w["rg_lambda"] + reduce_first(g)
    dua, dgw, dgb, dlam = None, [], [], []
    for d in range(2):
        a, b, c, e = rglru_bwd(proj_e, dycat, h_dir[d], acts[d], w["rg_gate_w"][d], lam[d], dua, d == 1,
                               "rglru_bwd_%d" % d)
        dua = a
        dgw.append(b)
        dgb.append(c)
        dlam.append(e)
    dproj_e, g["rg_conv_w"], g["rg_conv_b"], g["sc_conv_w"] = even_mix_bwd(
        proj_e, dycat, h_dir[0], h_dir[1], dua, w["rg_conv_w"], w["sc_conv_w"], "even_mix_bwd")
    dgw = jnp.stack(dgw).reshape(2, RG_HEADS, RG_HEAD_DIM, 2, RG_HEAD_DIM)
    g["rg_gate_w"] = jnp.transpose(dgw, (0, 3, 1, 2, 4))
    g["rg_gate_b"] = jnp.stack(dgb).reshape(2, 2, RG_HEADS, RG_HEAD_DIM)
    g["rg_lambda"] = jnp.concatenate(dlam, axis=0)
    g["even_w_in"] = matmul_dw(h0, dproj_e, EVEN_IN // 4, "even_w_in_grad")
    gain = w["even_norm_pre"] if reduce_second is None else w["even_norm_pre"] + reduce_second(g)
    grad_x, g["even_norm_pre"] = inproj_bwd(dproj_e, w["even_w_in"], x, gain, dx1, "even_in_proj_bwd")
    return loss, grad_x, g


def _prepare_weights(full):
    w = {}
    for name in ("even_norm_pre", "even_norm_post", "rg_conv_b", "odd_norm_pre", "odd_norm_post", "gla_norm_g"):
        if name in full:
            w[name] = full[name].reshape(1, -1)
    for name in ("rg_conv_w", "sc_conv_w"):
        if name in full:
            w[name] = full[name]
    for name in ("even_w_out", "odd_w_out"):
        if name in full:
            w[name] = full[name].astype(BF16)
    if "even_w_in" in full:
        w["even_w_in"] = full["even_w_in"].astype(BF16)
        if w["even_w_in"].ndim == 2:
            w["even_w_in"] = jnp.transpose(w["even_w_in"].reshape(D_MODEL, 4, EVEN_IN // 4), (1, 0, 2))
    if "rg_gate_w" in full:
        gw = jnp.transpose(full["rg_gate_w"].astype(BF16), (0, 2, 3, 1, 4))
        w["rg_gate_w"] = gw.reshape(2, RG_HEADS, RG_HEAD_DIM, 2 * RG_HEAD_DIM)
        w["rg_gate_b"] = full["rg_gate_b"].reshape(2, 2, D_MODEL)
        w["rg_lambda"] = full["rg_lambda"].reshape(2, 1, D_MODEL)
    if "odd_w_in" in full:
        w_in = jnp.pad(full["odd_w_in"].astype(BF16), ((0, 0), (0, ODD_IN_PAD - ODD_IN)))
        w["odd_w_in"] = w_in.reshape(1, D_MODEL, ODD_IN_PAD)
    if "gla_w_gate_lr" in full:
        wg = full["gla_w_gate_lr"].astype(BF16)
        w["gla_wg_pad"] = jnp.stack([jnp.pad(wg[d], ((d * GLA_RANK, LANES - (d + 1) * GLA_RANK), (0, 0)))
                                     for d in range(2)])
        w["gla_b_gate"] = full["gla_b_gate"].reshape(2, 1, GLA_HEADS * GLA_DK)
    return w


SHARDED_SMALL = (("rg_conv_w", (4, 256)), ("rg_lambda", (2, 256)), ("sc_conv_w", (3, 256)),
                 ("odd_norm_pre", (256,)), ("odd_norm_post", (256,)), ("gla_w_gate_lr", (2, 16, 128)),
                 ("gla_b_gate", (2, 128)), ("gla_norm_g", (64,)))
SHARDED_ROWS = 96
REPLICATED = (("rg_gate_w", (2, 2, 8, 128, 128)), ("even_norm_post", (1024,)), ("rg_conv_b", (1024,)),
              ("rg_gate_b", (2, 2, 8, 128)))
GATE_ROWS = 4096
LAST_REPLICATED = (("even_norm_pre", (1024,)),)
LAST_ROWS = 8
REPLICATED_ROWS = 4160
REP_PART = REPLICATED_ROWS // 8
HALF_SHARDED = SHARDED_ROWS // 2
PACK_HALF = HALF_SHARDED + REP_PART


def _seg_rows(shape):
    n = 1
    for s in shape:
        n *= s
    return -(-n // (SUBLANES * LANES)) * SUBLANES


def _pack(arrays, spec, total_rows, lead=()):
    parts = []
    for name, shape in spec:
        flat = arrays[name].reshape(lead + (-1,))
        pad = _seg_rows(shape) * LANES - flat.shape[-1]
        if pad:
            flat = jnp.pad(flat, [(0, 0)] * len(lead) + [(0, pad)])
        parts.append(flat.reshape(lead + (-1, LANES)))
    rows = jnp.concatenate(parts, axis=len(lead))
    pad = total_rows - rows.shape[len(lead)]
    return jnp.pad(rows, [(0, 0)] * len(lead) + [(0, pad), (0, 0)])


def _unpack(rows, spec, lead=()):
    out, at = {}, 0
    for name, shape in spec:
        n = 1
        for s in shape:
            n *= s
        k = _seg_rows(shape)
        seg = lax.slice_in_dim(rows, at, at + k, axis=len(lead)).reshape(lead + (-1,))
        out[name] = lax.slice_in_dim(seg, 0, n, axis=len(lead)).reshape(lead + shape)
        at += k
    return out


def _split_owners(arr):
    a = arr.reshape(arr.shape[:-1] + (4, arr.shape[-1] // 4))
    return jnp.moveaxis(a, -2, 0)


def _merge_owners(arr):
    a = jnp.moveaxis(arr, 0, -2)
    return a.reshape(a.shape[:-2] + (-1,))


HBM_SPEC = pl.BlockSpec(memory_space=pltpu.HBM)


def _position():
    x, y, c = lax.axis_index("x"), lax.axis_index("y"), lax.axis_index("c")
    chips = [(1 - x, y), (x, 1 - y), (1 - x, 1 - y)]
    return x, y, c, chips


def _remote(src, dst, send_sem, recv_sem, device):
    return pltpu.make_async_remote_copy(src_ref=src, dst_ref=dst, send_sem=send_sem, recv_sem=recv_sem,
                                        device_id=device, device_id_type=MESH)


SEM_SPEC = pl.BlockSpec(memory_space=pltpu.SEMAPHORE)
SIDE_EFFECT = pltpu.SideEffectType.DATAFLOW_SIDE_EFFECTING


def _gather_copies(ins, lands, n_h, send_sems, recv_sems):
    x, y, c, chips = _position()
    me = 2 * x + y
    copies = []
    for a in range(len(ins)):
        for k, chip in enumerate(chips):
            src = ins[a].at[c] if a < n_h else ins[a]
            dst = lands[a].at[me, c] if a < n_h else lands[a].at[me]
            copies.append(_remote(src, dst, send_sems.at[3 * a + k], recv_sems.at[3 * a + k], (chip[0], chip[1], c)))
    return copies


def gather_start(halved, whole, name):
    arrays = list(halved) + list(whole)
    n, n_h = len(arrays), len(halved)
    lands = [lax.empty((4,) + a.shape, a.dtype) for a in arrays]

    def body(*refs):
        ins, lz, send_sems, recv_sems, token = refs[:n], refs[n:2 * n], refs[2 * n], refs[2 * n + 1], refs[-1]
        for cp in _gather_copies(ins, lz, n_h, send_sems, recv_sems):
            cp.start()
        token[...] = jnp.zeros_like(token)

    operands = [pltpu.with_memory_space_constraint(a, pltpu.HBM) for a in arrays + lands]
    return pl.pallas_call(
        body, name=name,
        out_shape=(pltpu.SemaphoreType.DMA((3 * n,)), pltpu.SemaphoreType.DMA((3 * n,)))
        + tuple(pltpu.HBM(a.shape, a.dtype) for a in operands) + (jax.ShapeDtypeStruct((SUBLANES, LANES), F32),),
        in_specs=[HBM_SPEC] * (2 * n),
        out_specs=(SEM_SPEC, SEM_SPEC) + (HBM_SPEC,) * (2 * n) + (pl.BlockSpec(memory_space=pltpu.VMEM),),
        input_output_aliases={i: 2 + i for i in range(2 * n)},
        compiler_params=pltpu.CompilerParams(has_side_effects=SIDE_EFFECT),
    )(*operands)


def gather_wait(started, n_h, after, name):
    send_sems, recv_sems = started[0], started[1]
    operands = list(started[2:-1])
    n = len(operands) // 2

    def body(*refs):
        ins, lz, send_ref, recv_ref = refs[:n], refs[n:2 * n], refs[2 * n], refs[2 * n + 1]
        for cp in _gather_copies(ins, lz, n_h, send_ref, recv_ref):
            cp.wait_send()
            cp.wait_recv()

    outs = pl.pallas_call(
        body, name=name,
        out_shape=tuple(pltpu.HBM(a.shape, a.dtype) for a in operands),
        in_specs=[HBM_SPEC] * (2 * n) + [SEM_SPEC, SEM_SPEC, pl.BlockSpec(memory_space=pl.ANY)],
        out_specs=(HBM_SPEC,) * (2 * n),
        input_output_aliases={i: i for i in range(2 * n)},
        compiler_params=pltpu.CompilerParams(has_side_effects=SIDE_EFFECT),
    )(*operands, send_sems, recv_sems, after)
    return outs[n:]


def pass_to_sibling(fulls, name):
    n = len(fulls)

    def body(*refs):
        bufs = refs[n:2 * n]
        send_sems, recv_sems = refs[2 * n:]
        x, y, c, chips = _position()
        sibling = (x, y, 1 - c)
        copies = []
        for a in range(n):
            for k, chip in enumerate(chips):
                q = 2 * chip[0] + chip[1]
                cp = _remote(bufs[a].at[q, c], bufs[a].at[q, c], send_sems.at[3 * a + k], recv_sems.at[3 * a + k],
                             sibling)
                cp.start()
                copies.append(cp)
        for a in range(n):
            for k, chip in enumerate(chips):
                q = 2 * chip[0] + chip[1]
                passed = bufs[a].at[q, 1 - c]
                _remote(passed, passed, send_sems.at[3 * a + k], recv_sems.at[3 * a + k], sibling).wait_recv()
        for cp in copies:
            cp.wait_send()

    return pl.pallas_call(
        body, name=name,
        out_shape=[jax.ShapeDtypeStruct(a.shape, a.dtype) for a in fulls],
        in_specs=[HBM_SPEC] * n, out_specs=[HBM_SPEC] * n,
        input_output_aliases={i: i for i in range(n)},
        scratch_shapes=[pltpu.SemaphoreType.DMA((3 * n,)), pltpu.SemaphoreType.DMA((3 * n,))],
    )(*fulls)


def place_own(full, own, chip, name):
    _, _, r, cols = full.shape
    tr = _row_tile(r, cols)

    def body(p_ref, own_ref, full_ref, o_ref):
        o_ref[0] = own_ref[...]

    return pl.pallas_call(
        body, name=name,
        out_shape=jax.ShapeDtypeStruct(full.shape, full.dtype),
        grid_spec=pltpu.PrefetchScalarGridSpec(
            num_scalar_prefetch=1, grid=(2, r // tr),
            in_specs=[pl.BlockSpec((1, tr, cols), lambda h, i, p_ref: (h, i, 0)), pl.BlockSpec(memory_space=pl.ANY)],
            out_specs=pl.BlockSpec((1, 1, tr, cols), lambda h, i, p_ref: (p_ref[0], h, i, 0))),
        input_output_aliases={2: 0},
        compiler_params=_params("parallel", "parallel"),
    )(chip, own, full)


def exchange_with_sibling(arrays, name):
    n = len(arrays)

    def body(*refs):
        ins, outs = refs[:n], refs[n:2 * n]
        send_sems, recv_sems = refs[2 * n:]
        x, y, c, _ = _position()
        copies = []
        for a in range(n):
            cp = _remote(ins[a].at[:, 1 - c], outs[a], send_sems.at[a], recv_sems.at[a], (x, y, 1 - c))
            cp.start()
            copies.append(cp)
        for cp in copies:
            cp.wait()

    return pl.pallas_call(
        body, name=name,
        out_shape=[jax.ShapeDtypeStruct((a.shape[0],) + a.shape[2:], a.dtype) for a in arrays],
        in_specs=[HBM_SPEC] * n, out_specs=[HBM_SPEC] * n,
        scratch_shapes=[pltpu.SemaphoreType.DMA((n,)), pltpu.SemaphoreType.DMA((n,))],
    )(*arrays)


def _chip_copies(ins, lands, send_sems, recv_sems):
    x, y, c, chips = _position()
    copies = []
    for a in range(len(ins)):
        for k, chip in enumerate(chips):
            q = 2 * chip[0] + chip[1]
            copies.append(_remote(ins[a].at[q], lands[a].at[k], send_sems.at[3 * a + k], recv_sems.at[3 * a + k],
                                  (chip[0], chip[1], c)))
    return copies


def exchange_with_chips_start(arrays, name):
    n = len(arrays)
    lands = [lax.empty((3,) + a.shape[1:], a.dtype) for a in arrays]

    def body(*refs):
        ins, lz, send_sems, recv_sems, token = refs[:n], refs[n:2 * n], refs[2 * n], refs[2 * n + 1], refs[-1]
        for cp in _chip_copies(ins, lz, send_sems, recv_sems):
            cp.start()
        token[...] = jnp.zeros_like(token)

    operands = [pltpu.with_memory_space_constraint(a, pltpu.HBM) for a in list(arrays) + lands]
    return pl.pallas_call(
        body, name=name,
        out_shape=(pltpu.SemaphoreType.DMA((3 * n,)), pltpu.SemaphoreType.DMA((3 * n,)))
        + tuple(pltpu.HBM(a.shape, a.dtype) for a in operands) + (jax.ShapeDtypeStruct((SUBLANES, LANES), F32),),
        in_specs=[HBM_SPEC] * (2 * n),
        out_specs=(SEM_SPEC, SEM_SPEC) + (HBM_SPEC,) * (2 * n) + (pl.BlockSpec(memory_space=pltpu.VMEM),),
        input_output_aliases={i: 2 + i for i in range(2 * n)},
        compiler_params=pltpu.CompilerParams(has_side_effects=SIDE_EFFECT),
    )(*operands)


def exchange_with_chips_wait(started, after, name):
    send_sems, recv_sems = started[0], started[1]
    operands = list(started[2:-1])
    n = len(operands) // 2

    def body(*refs):
        ins, lz, send_ref, recv_ref = refs[:n], refs[n:2 * n], refs[2 * n], refs[2 * n + 1]
        for cp in _chip_copies(ins, lz, send_ref, recv_ref):
            cp.wait_send()
            cp.wait_recv()

    outs = pl.pallas_call(
        body, name=name,
        out_shape=tuple(pltpu.HBM(a.shape, a.dtype) for a in operands),
        in_specs=[HBM_SPEC] * (2 * n) + [SEM_SPEC, SEM_SPEC, pl.BlockSpec(memory_space=pl.ANY)],
        out_specs=(HBM_SPEC,) * (2 * n),
        input_output_aliases={i: i for i in range(2 * n)},
        compiler_params=pltpu.CompilerParams(has_side_effects=SIDE_EFFECT),
    )(*operands, send_sems, recv_sems, after)
    return outs[:n], outs[n:]


def share_totals(totals, pack_total, last_part):
    arrays = list(totals) + [pack_total]
    n = len(arrays)

    def body(*refs):
        ins, last, outs, rep, last_all = refs[:n], refs[n], refs[n + 1:2 * n + 1], refs[2 * n + 1], refs[2 * n + 2]
        send_sems, recv_sems, rep_send, rep_recv, last_send, last_recv = refs[2 * n + 3:]
        x, y, c, chips = _position()
        sibling = (x, y, 1 - c)
        me = 4 * x + 2 * y + c
        sends = []
        for a in range(n):
            cp = _remote(ins[a], outs[a], send_sems.at[a], recv_sems.at[a], sibling)
            cp.start()
            sends.append(cp)
        mine = ins[n - 1].at[pl.ds(HALF_SHARDED, REP_PART)]
        peers = [sibling]
        for chip in chips:
            peers += [(chip[0], chip[1], c), (chip[0], chip[1], 1 - c)]
        for j, peer in enumerate(peers):
            for src, dst, s_sem, r_sem in ((mine, rep, rep_send, rep_recv), (last, last_all, last_send, last_recv)):
                cp = _remote(src, dst.at[me], s_sem.at[j], r_sem.at[j], peer)
                cp.start()
                sends.append(cp)
        for a in range(n):
            _remote(outs[a], outs[a], send_sems.at[a], recv_sems.at[a], sibling).wait_recv()
        for j, peer in enumerate(peers):
            it = 4 * peer[0] + 2 * peer[1] + peer[2]
            _remote(rep.at[it], rep.at[it], rep_send.at[j], rep_recv.at[j], peer).wait_recv()
            _remote(last_all.at[it], last_all.at[it], last_send.at[j], last_recv.at[j], peer).wait_recv()
        for cp in sends:
            cp.wait_send()

    outs = pl.pallas_call(
        body, name="grad_share_totals",
        out_shape=[jax.ShapeDtypeStruct(a.shape, a.dtype) for a in arrays]
        + [jax.ShapeDtypeStruct((8, REP_PART, LANES), F32), jax.ShapeDtypeStruct((8,) + last_part.shape, F32)],
        in_specs=[HBM_SPEC] * (n + 1), out_specs=[HBM_SPEC] * (n + 2),
        scratch_shapes=[pltpu.SemaphoreType.DMA((n,)), pltpu.SemaphoreType.DMA((n,))]
        + [pltpu.SemaphoreType.DMA((7,))] * 4,
    )(*arrays, last_part)
    return outs[:n], outs[n], outs[n + 1]


def sum_parts(parts, name):
    def body(p_ref, o_ref):
        total = p_ref[0]
        for k in range(1, parts.shape[0]):
            total = total + p_ref[k]
        o_ref[...] = total

    return pl.pallas_call(body, name=name, out_shape=jax.ShapeDtypeStruct(parts.shape[1:], parts.dtype))(parts)


TILE_BYTES = 2 << 20


def _row_tile(rows, cols):
    best = None
    for t in range(SUBLANES, rows + 1, SUBLANES):
        if rows % t == 0 and t * cols * 4 <= TILE_BYTES:
            best = t
    return best if best is not None else rows


def add_sibling(mine, received, core, out_dtype, name):
    _, _, r, cols = mine.shape
    tr = _row_tile(r, cols)

    def body(c_ref, a_ref, b_ref, o_ref):
        o_ref[...] = (a_ref[0] + b_ref[...]).astype(out_dtype)

    return pl.pallas_call(
        body, name=name,
        out_shape=jax.ShapeDtypeStruct((4, r, cols), out_dtype),
        grid_spec=pltpu.PrefetchScalarGridSpec(
            num_scalar_prefetch=1, grid=(4, r // tr),
            in_specs=[pl.BlockSpec((1, 1, tr, cols), lambda o, i, c_ref: (o, c_ref[0], i, 0)),
                      pl.BlockSpec((1, tr, cols), lambda o, i, c_ref: (o, i, 0))],
            out_specs=pl.BlockSpec((1, tr, cols), lambda o, i, c_ref: (o, i, 0))),
        compiler_params=_params("parallel", "parallel"),
    )(core, mine, received)


def add_chips(own, received, chip, name):
    _, r, cols = own.shape
    tr = _row_tile(r, cols)

    def body(p_ref, a_ref, b0, b1, b2, o_ref):
        o_ref[...] = ((a_ref[0].astype(F32) + b0[0].astype(F32)) + b1[0].astype(F32)) + b2[0].astype(F32)

    rb = lambda k: pl.BlockSpec((1, tr, cols), lambda i, p_ref: (k, i, 0))
    return pl.pallas_call(
        body, name=name,
        out_shape=jax.ShapeDtypeStruct((r, cols), F32),
        grid_spec=pltpu.PrefetchScalarGridSpec(
            num_scalar_prefetch=1, grid=(r // tr,),
            in_specs=[pl.BlockSpec((1, tr, cols), lambda i, p_ref: (p_ref[0], i, 0)), rb(0), rb(1), rb(2)],
            out_specs=pl.BlockSpec((tr, cols), lambda i, p_ref: (i, 0))),
        compiler_params=_params("parallel"),
    )(chip, own, received, received, received)


def _adamw_update(gv, w_ref, m_ref, v_ref, d_ref, nm_ref, nv_ref):
    nm = ADAM_B1 * m_ref[...] + (1.0 - ADAM_B1) * gv
    nv = ADAM_B2 * v_ref[...] + (1.0 - ADAM_B2) * (gv * gv)
    nm_ref[...] = nm
    nv_ref[...] = nv
    m_hat = nm / (1.0 - ADAM_B1 ** ADAM_STEP)
    v_hat = nv / (1.0 - ADAM_B2 ** ADAM_STEP)
    d_ref[...] = -ADAM_LR * (m_hat / (jnp.sqrt(v_hat) + ADAM_EPS) + ADAM_WD * w_ref[...])


def adamw_halves(w, own, received, m, v, core, name, by_columns=False):
    rows, cols = w.shape

    def body(c_ref, w_ref, own_ref, rec_ref, m_ref, v_ref, g_ref, d_ref, nm_ref, nv_ref):
        gv = jnp.where(pl.program_id(0) == c_ref[0], own_ref[...], rec_ref[...])
        g_ref[...] = gv
        _adamw_update(gv, w_ref, m_ref, v_ref, d_ref, nm_ref, nv_ref)

    if by_columns:
        nr = 1
        whole = pl.BlockSpec((rows, cols // 2), lambda h, i, c_ref: (0, h))
        half = pl.BlockSpec((rows, cols // 2), lambda h, i, c_ref: (0, 0))
    else:
        r = rows // 2
        tr = _row_tile(r, cols)
        nr = r // tr
        whole = pl.BlockSpec((tr, cols), lambda h, i, c_ref: (h * nr + i, 0))
        half = pl.BlockSpec((tr, cols), lambda h, i, c_ref: (i, 0))
    return pl.pallas_call(
        body, name=name,
        out_shape=(jax.ShapeDtypeStruct((rows, cols), F32),) * 4,
        grid_spec=pltpu.PrefetchScalarGridSpec(
            num_scalar_prefetch=1, grid=(2, nr),
            in_specs=[whole, half, half, whole, whole], out_specs=(whole,) * 4),
        compiler_params=_params("parallel", "parallel"),
    )(core, w, own, received, m, v)


def adamw_many(ws, gs, ms, vs, name):
    n = len(ws)

    def body(*refs):
        ins, outs = refs[:4 * n], refs[4 * n:]
        for k in range(n):
            w_ref, g_ref, m_ref, v_ref = (ins[j * n + k] for j in range(4))
            d_ref, nm_ref, nv_ref = outs[3 * k:3 * k + 3]
            _adamw_update(g_ref[...], w_ref, m_ref, v_ref, d_ref, nm_ref, nv_ref)

    flat = pl.pallas_call(
        body, name=name,
        out_shape=[jax.ShapeDtypeStruct(w.shape, F32) for w in ws for _ in range(3)],
    )(*ws, *gs, *ms, *vs)
    return [tuple(flat[3 * k:3 * k + 3]) for k in range(n)]


def adamw(w, g, m, v, name):
    r, cols = w.shape
    tr = _row_tile(r, cols)

    def body(w_ref, g_ref, m_ref, v_ref, g_out, d_ref, nm_ref, nv_ref):
        gv = g_ref[...]
        g_out[...] = gv
        _adamw_update(gv, w_ref, m_ref, v_ref, d_ref, nm_ref, nv_ref)

    blk = pl.BlockSpec((tr, cols), lambda i: (i, 0))
    return pl.pallas_call(
        body, name=name,
        out_shape=(jax.ShapeDtypeStruct((r, cols), F32),) * 4,
        grid=(r // tr,),
        in_specs=[blk] * 4, out_specs=(blk,) * 4,
        compiler_params=_params("parallel"),
    )(w, g, m, v)


WEIGHTS = ("even_norm_pre", "even_norm_post", "even_w_in", "rg_conv_w", "rg_conv_b", "rg_gate_w", "rg_gate_b",
           "rg_lambda", "sc_conv_w", "even_w_out", "odd_norm_pre", "odd_norm_post", "odd_w_in", "gla_w_gate_lr",
           "gla_b_gate", "gla_norm_g", "odd_w_out")
BIG = ("even_w_in", "even_w_out", "odd_w_in", "odd_w_out")


def _halves(a):
    return a.reshape((2, a.shape[0] // 2) + a.shape[1:])


def kernel(x, even_norm_pre, even_norm_post, even_w_in, rg_conv_w, rg_conv_b, rg_gate_w, rg_gate_b, rg_lambda, sc_conv_w, even_w_out, odd_norm_pre, odd_norm_post, odd_w_in, gla_w_gate_lr, gla_b_gate, gla_norm_g, odd_w_out, loss_target, m_even_norm_pre, m_even_norm_post, m_even_w_in, m_rg_conv_w, m_rg_conv_b, m_rg_gate_w, m_rg_gate_b, m_rg_lambda, m_sc_conv_w, m_even_w_out, m_odd_norm_pre, m_odd_norm_post, m_odd_w_in, m_gla_w_gate_lr, m_gla_b_gate, m_gla_norm_g, m_odd_w_out, v_even_norm_pre, v_even_norm_post, v_even_w_in, v_rg_conv_w, v_rg_conv_b, v_rg_gate_w, v_rg_gate_b, v_rg_lambda, v_sc_conv_w, v_even_w_out, v_odd_norm_pre, v_odd_norm_post, v_odd_w_in, v_gla_w_gate_lr, v_gla_b_gate, v_gla_norm_g, v_odd_w_out):
    given = dict(locals())
    shard = {n: given[n][0] for n in WEIGHTS}
    m_in = {n: given["m_" + n][0] for n in WEIGHTS}
    v_in = {n: given["v_" + n][0] for n in WEIGHTS}
    mx, my, mc = lax.axis_index("x"), lax.axis_index("y"), lax.axis_index("c")
    core = jnp.reshape(mc, (1,)).astype(jnp.int32)
    chip = jnp.reshape(2 * mx + my, (1,)).astype(jnp.int32)

    small_shard = _pack(shard, SHARDED_SMALL, SHARDED_ROWS)
    big_own = [_halves(shard[n].astype(BF16)) for n in BIG]
    started_a = gather_start(big_own[:1], [small_shard], "gather_start_a")
    started_b = gather_start(big_own[1:], [], "gather_start_b")
    even_w_in_full, small_full = gather_wait(started_a, 1, started_b[-1], "gather_wait_a")
    (even_w_in_full,) = pass_to_sibling([even_w_in_full], "gather_pass_a")
    even_w_in_full = place_own(even_w_in_full, big_own[0], chip, "place_even_w_in")
    small_full = lax.dynamic_update_slice(small_full, small_shard[None], (chip[0], 0, 0))
    full = {n: shard[n] for n, _ in REPLICATED + LAST_REPLICATED}
    full.update({n: _merge_owners(a) for n, a in _unpack(small_full, SHARDED_SMALL, lead=(4,)).items()})
    full["even_w_in"] = even_w_in_full.reshape(4, D_MODEL, EVEN_IN // 4)

    def late_weights(after):
        lands = pass_to_sibling(list(gather_wait(started_b, 3, after, "gather_wait_b")), "gather_pass_b")
        lands = [place_own(a, b, chip, "place_" + n) for a, b, n in zip(lands, big_own[1:], BIG[1:])]
        odd_w_in = jnp.transpose(lands[1].reshape(4, D_MODEL, ODD_IN // 4), (1, 0, 2)).reshape(D_MODEL, ODD_IN)
        return _prepare_weights({"even_w_out": lands[0].reshape(2 * D_MODEL, D_MODEL), "odd_w_in": odd_w_in,
                                 "odd_w_out": lands[2].reshape(D_MODEL, D_MODEL)})

    pending = {}

    def slab(a):
        return a.reshape((4, 2, a.shape[1] // 2) + a.shape[2:])

    def begin(tag, slabs, dtypes):
        got = exchange_with_sibling(slabs, "grad_sibling_" + tag)
        sums = [add_sibling(a, b, core, dt, "grad_add_sibling_%s%d" % (tag, i))
                for i, (a, b, dt) in enumerate(zip(slabs, got, dtypes))]
        pending[tag] = exchange_with_chips_start(sums, "grad_chips_start_" + tag)
        return pending[tag][-1][0, 0]

    def finish(tag, after):
        sums, got = exchange_with_chips_wait(pending[tag], after, "grad_chips_wait_" + tag)
        return [add_chips(a, b, chip, "grad_add_chips_%s%d" % (tag, i)) for i, (a, b) in enumerate(zip(sums, got))]

    def reduce_first(g):
        return begin("a", [slab(jnp.transpose(g["odd_w_in"].reshape(D_MODEL, 4, ODD_IN // 4), (1, 0, 2))),
                           slab(g["odd_w_out"].reshape(4, D_MODEL // 4, D_MODEL)),
                           slab(g["even_w_out"].reshape(4, D_MODEL // 2, D_MODEL))], [BF16] * 3)

    def reduce_second(g):
        pending["totals_a"] = finish("a", g["even_w_in"])
        rep_rows = _pack(g, REPLICATED, REPLICATED_ROWS).reshape(4, 2, REP_PART, LANES)
        sh_rows = _pack({n: _split_owners(g[n]) for n, _ in SHARDED_SMALL}, SHARDED_SMALL, SHARDED_ROWS, lead=(4,))
        pack = jnp.concatenate([sh_rows.reshape(4, 2, HALF_SHARDED, LANES), rep_rows], axis=2)
        return begin("b", [slab(g["even_w_in"]), pack], [BF16, F32])

    loss, grad_x, g = local_step(x[0], loss_target[0], _prepare_weights(full), reduce_first, reduce_second,
                                 late_weights)
    odd_w_in_t, odd_w_out_t, even_w_out_t = pending["totals_a"]
    even_w_in_t, pack_t = finish("b", grad_x)
    totals = [even_w_in_t, even_w_out_t, odd_w_in_t, odd_w_out_t]
    last_part = jnp.concatenate([_pack(g, LAST_REPLICATED, LAST_ROWS), loss])
    from_core, rep_all, last_all = share_totals(totals, pack_t, last_part)
    me = 2 * chip[0] + core[0]
    mine, theirs = pack_t[:HALF_SHARDED], from_core[4][:HALF_SHARDED]
    sh_total = jnp.where(mc == 0, jnp.concatenate([mine, theirs]), jnp.concatenate([theirs, mine]))
    rep_all = lax.dynamic_update_slice(rep_all, pack_t[None, HALF_SHARDED:], (me, 0, 0))
    rep_total = rep_all.reshape(REPLICATED_ROWS, LANES)
    last_total = sum_parts(lax.dynamic_update_slice(last_all, last_part[None], (me, 0, 0)), "grad_sum_last")
    last_total, loss = last_total[:LAST_ROWS], last_total[LAST_ROWS, 0]
    grads = {}

    delta, new_m, new_v = {}, {}, {}
    for i, n in enumerate(BIG):
        if shard[n].shape[1] % LANES:
            outs = adamw_halves(shard[n].T, totals[i].T, from_core[i].T, m_in[n].T, v_in[n].T, core, "adamw_" + n,
                                by_columns=True)
            grads[n], delta[n], new_m[n], new_v[n] = [o.T for o in outs]
        else:
            grads[n], delta[n], new_m[n], new_v[n] = adamw_halves(shard[n], totals[i], from_core[i], m_in[n],
                                                                  v_in[n], core, "adamw_" + n)
    gate = [src["rg_gate_w"].reshape(GATE_ROWS, LANES) for src in (shard, m_in, v_in)]
    grads["rg_gate_w"], delta["rg_gate_w"], new_m["rg_gate_w"], new_v["rg_gate_w"] = adamw(
        gate[0], rep_total, gate[1], gate[2], "adamw_rg_gate_w")
    rest = REPLICATED[1:]
    rest_rows = sum(_seg_rows(shape) for _, shape in rest)
    grads.update(_unpack(sh_total, SHARDED_SMALL))
    grads.update(_unpack(rep_total[GATE_ROWS:GATE_ROWS + rest_rows], rest))
    grads.update(_unpack(last_total, LAST_REPLICATED))
    names = [n for n, _ in SHARDED_SMALL + rest + LAST_REPLICATED]
    rows_of = lambda a, n: a.reshape(-1, given[n].shape[-1])
    outs = adamw_many([rows_of(given[n], n) for n in names], [rows_of(grads[n], n) for n in names],
                      [rows_of(given["m_" + n], n) for n in names], [rows_of(given["v_" + n], n) for n in names],
                      "adamw_small")
    for n, (d, nm, nv) in zip(names, outs):
        delta[n], new_m[n], new_v[n] = d, nm, nv
    result = [loss, grad_x[None]]
    for group in (grads, delta, new_m, new_v):
        result += [group[n].reshape(given[n].shape) for n in WEIGHTS]
    return tuple(result)
```

```python
import functools

import jax
import jax.numpy as jnp
from jax import lax
from jax.experimental import pallas as pl
from jax.experimental.pallas import tpu as pltpu

F32 = jnp.float32
BF16 = jnp.bfloat16
MESH = pl.DeviceIdType.MESH

D_MODEL = 1024
NORM_EPS = 1e-6
RG_HEADS = 8
RG_HEAD_DIM = 128
RG_C = 8.0
EVEN_IN = 6144
ODD_IN = 3104
ODD_IN_PAD = 3200
GLA_HEADS = 4
GLA_DK = 128
GLA_DV = 256
GLA_RANK = 16
GLA_NORMALIZER = 16.0
GLA_CHUNK = 128
LR_COL = 3072

ADAM_LR = 0.001
ADAM_B1 = 0.9
ADAM_B2 = 0.999
ADAM_EPS = 1e-08
ADAM_WD = 0.01
ADAM_STEP = 10

SUBLANES = 8
LANES = 128
VMEM_LIMIT = 56 * 2 ** 20

ROW_TILE = 512
SCAN_TILE = 256
GLA_BLOCK = 1024
MIX_TILE = 128


def _params(*sem):
    return pltpu.CompilerParams(dimension_semantics=sem, vmem_limit_bytes=VMEM_LIMIT)


def _full(shape):
    n = len(shape)
    return pl.BlockSpec(shape, lambda *_: (0,) * n)


def _sigmoid(x):
    return 0.5 + 0.5 * jnp.tanh(0.5 * x)


def _softplus(x):
    return jnp.maximum(x, 0.0) + jnp.log(1.0 + jnp.exp(-jnp.abs(x)))


def _dot(a, b):
    return jnp.dot(a, b, preferred_element_type=F32)


def _dot_nt(a, b):
    return lax.dot_general(a, b, (((1,), (1,)), ((), ())), preferred_element_type=F32)


def _dot_tn(a, b):
    return lax.dot_general(a, b, (((0,), (0,)), ((), ())), preferred_element_type=F32)


def _bdot(a, b, ca, cb):
    return lax.dot_general(a, b, (((ca,), (cb,)), ((0,), (0,))), preferred_element_type=F32)


def _halo_specs(rows, cols, col_block, n_row_tiles, tix):
    per = rows // SUBLANES
    last = n_row_tiles * per - 1

    def split(args):
        if len(args) == 2:
            return tix(args[1]), col_block + args[0]
        return tix(args[0]), col_block

    def prev(*args):
        t, c = split(args)
        return (jnp.maximum(t * per - 1, 0), c)

    def main(*args):
        return split(args)

    def nxt(*args):
        t, c = split(args)
        return (jnp.minimum((t + 1) * per, last), c)

    return [pl.BlockSpec((SUBLANES, cols), prev), pl.BlockSpec((rows, cols), main),
            pl.BlockSpec((SUBLANES, cols), nxt)]


def _extend(prev_ref, main_ref, next_ref, is_first, is_last):
    p = jnp.where(is_first, 0.0, prev_ref[...])
    n = jnp.where(is_last, 0.0, next_ref[...])
    return jnp.concatenate([p, main_ref[...], n], axis=0)


def _shifted(ext, offset, rows):
    if offset == 0:
        return ext[SUBLANES:SUBLANES + rows]
    n = ext.shape[0]
    return pltpu.roll(ext, (-offset) % n, 0)[SUBLANES:SUBLANES + rows]


def _conv(ext, w, left, rows):
    out = None
    for k in range(w.shape[0]):
        term = _shifted(ext, k - left, rows) * w[k:k + 1]
        out = term if out is None else out + term
    return out


def _conv_transpose(ext, w, left, rows):
    out = None
    for k in range(w.shape[0]):
        term = _shifted(ext, left - k, rows) * w[k:k + 1]
        out = term if out is None else out + term
    return out


def _colsum(x):
    return jnp.sum(x, axis=0, keepdims=True)


def _accumulate(ref, value, step):
    @pl.when(step == 0)
    def _():
        ref[...] = value

    @pl.when(step > 0)
    def _():
        ref[...] += value


PROJ_TILE_BYTES = 7 * 2 ** 20


def _proj_row_tile(rows, width):
    tm = min(ROW_TILE, rows)
    while tm * width * 4 > PROJ_TILE_BYTES and tm % (2 * SUBLANES) == 0:
        tm //= 2
    return tm


def norm_matmul(x, gain, w, name):
    rows, d = x.shape
    n_col_tiles, _, tn = w.shape
    tm = _proj_row_tile(rows, n_col_tiles * tn)

    def body(x_ref, g_ref, w_ref, proj_ref, h_ref):
        xv = x_ref[...]
        rstd = lax.rsqrt(jnp.mean(xv * xv, axis=-1, keepdims=True) + NORM_EPS)
        hv = (xv * rstd * g_ref[...]).astype(BF16)
        h_ref[...] = hv
        for j in range(n_col_tiles):
            proj_ref[:, j * tn:(j + 1) * tn] = _dot(hv, w_ref[j])

    row = lambda cols: pl.BlockSpec((tm, cols), lambda i: (i, 0))
    return pl.pallas_call(
        body, name=name,
        out_shape=(jax.ShapeDtypeStruct((rows, n_col_tiles * tn), F32), jax.ShapeDtypeStruct((rows, d), BF16)),
        grid=(rows // tm,),
        in_specs=[row(d), _full((1, d)), _full(w.shape)],
        out_specs=(row(n_col_tiles * tn), row(d)),
        compiler_params=_params("parallel"),
    )(x, gain, w)


def inproj_bwd(dproj, w, x, gain, dres, name):
    rows, d = x.shape
    n_col_tiles, _, tn = w.shape
    tm = _proj_row_tile(rows, n_col_tiles * tn)

    def body(dp_ref, w_ref, x_ref, g_ref, dres_ref, dx_ref, dg_ref):
        dh = None
        for j in range(n_col_tiles):
            part = _dot_nt(dp_ref[:, j * tn:(j + 1) * tn], w_ref[j])
            dh = part if dh is None else dh + part
        _inproj_finish(dh, x_ref, g_ref, dres_ref, dx_ref, dg_ref, pl.program_id(0))

    row = lambda cols: pl.BlockSpec((tm, cols), lambda i: (i, 0))
    return pl.pallas_call(
        body, name=name,
        out_shape=(jax.ShapeDtypeStruct((rows, d), F32), jax.ShapeDtypeStruct((1, d), F32)),
        grid=(rows // tm,),
        in_specs=[row(n_col_tiles * tn), _full(w.shape), row(d), _full((1, d)), row(d)],
        out_specs=(row(d), _full((1, d))),
        compiler_params=_params("arbitrary"),
    )(dproj, w, x, gain, dres)


def _inproj_finish(dh, x_ref, g_ref, dres_ref, dx_ref, dg_ref, step):
    xv = x_ref[...]
    rstd = lax.rsqrt(jnp.mean(xv * xv, axis=-1, keepdims=True) + NORM_EPS)
    xhat = xv * rstd
    dxn = dh * g_ref[...]
    dx_ref[...] = dres_ref[...] + rstd * (dxn - xhat * jnp.mean(dxn * xhat, axis=-1, keepdims=True))
    _accumulate(dg_ref, _colsum(dh * xhat), step)


def inproj_bwd_pieces(pieces, w, x, gain, dres, name):
    rows, d = x.shape
    tm = min(ROW_TILE, rows)
    n = len(pieces)
    widths = [p.shape[1] for p in pieces]
    starts = [sum(widths[:k]) for k in range(n)]
    assert sum(widths) == w.shape[2]

    def body(*refs):
        w_ref, x_ref, g_ref, dres_ref, dx_ref, dg_ref = refs[n:]
        dh = None
        for k in range(n):
            part = _dot_nt(refs[k][...], w_ref[0, :, starts[k]:starts[k] + widths[k]])
            dh = part if dh is None else dh + part
        _inproj_finish(dh, x_ref, g_ref, dres_ref, dx_ref, dg_ref, pl.program_id(0))

    row = lambda cols: pl.BlockSpec((tm, cols), lambda i: (i, 0))
    return pl.pallas_call(
        body, name=name,
        out_shape=(jax.ShapeDtypeStruct((rows, d), F32), jax.ShapeDtypeStruct((1, d), F32)),
        grid=(rows // tm,),
        in_specs=[row(wd) for wd in widths] + [_full(w.shape), row(d), _full((1, d)), row(d)],
        out_specs=(row(d), _full((1, d))),
        compiler_params=_params("arbitrary"),
    )(*pieces, w, x, gain, dres)


def matmul_dw_pieces(a, pieces, name):
    rows, m = a.shape
    tk = min(2 * ROW_TILE, rows)
    n = len(pieces)

    def body(*refs):
        a_ref, ins, outs = refs[0], refs[1:1 + n], refs[1 + n:]
        av = a_ref[...]
        for k in range(n):
            _accumulate(outs[k], _dot_tn(av, ins[k][...]), pl.program_id(0))

    return pl.pallas_call(
        body, name=name,
        out_shape=[jax.ShapeDtypeStruct((m, p.shape[1]), F32) for p in pieces],
        grid=(rows // tk,),
        in_specs=[pl.BlockSpec((tk, m), lambda k: (k, 0))]
        + [pl.BlockSpec((tk, p.shape[1]), lambda k: (k, 0)) for p in pieces],
        out_specs=[_full((m, p.shape[1])) for p in pieces],
        compiler_params=_params("arbitrary"),
    )(a, *pieces)


def matmul_dw(a, b, bn, name):
    rows, m = a.shape
    n = b.shape[1]
    tk = min((4 if n > bn else 2) * ROW_TILE, rows)
    steps = rows // tk

    def body(a_ref, b_ref, o_ref, o16_ref):
        part = _dot_tn(a_ref[...], b_ref[...])

        @pl.when(pl.program_id(1) == 0)
        def _():
            o_ref[0] = part

        @pl.when(pl.program_id(1) > 0)
        def _():
            o_ref[0] += part

        @pl.when(pl.program_id(1) == steps - 1)
        def _():
            o16_ref[0] = o_ref[0].astype(BF16)

    out = pl.BlockSpec((1, m, bn), lambda j, k: (j, 0, 0))
    return pl.pallas_call(
        body, name=name,
        out_shape=(jax.ShapeDtypeStruct((n // bn, m, bn), F32), jax.ShapeDtypeStruct((n // bn, m, bn), BF16)),
        grid=(n // bn, steps),
        in_specs=[pl.BlockSpec((tk, m), lambda j, k: (k, 0)), pl.BlockSpec((tk, bn), lambda j, k: (k, j))],
        out_specs=(out, out),
        compiler_params=_params("parallel", "arbitrary"),
    )(a, b)


def _scan(a, b, carry, reverse):
    n, c = a.shape
    blocks = n // SUBLANES
    a = a.reshape(blocks, SUBLANES, c)
    b = b.reshape(blocks, SUBLANES, c)
    pos = lax.broadcasted_iota(jnp.int32, (1, SUBLANES, c), 1)
    s = 1
    while s < SUBLANES:
        shift, valid = (SUBLANES - s, pos < SUBLANES - s) if reverse else (s, pos >= s)
        a_s, b_s = pltpu.roll(a, shift, 1), pltpu.roll(b, shift, 1)
        b = jnp.where(valid, a * b_s + b, b)
        a = jnp.where(valid, a * a_s, a)
        s *= 2
    out = [None] * blocks
    for k in (range(blocks - 1, -1, -1) if reverse else range(blocks)):
        h = a[k] * carry + b[k]
        out[k] = h
        carry = h[0:1] if reverse else h[SUBLANES - 1:SUBLANES]
    return jnp.concatenate(out, axis=0)


def _rg_gates(ua, gw_ref, gb, lam):
    ub = ua.astype(BF16)
    pre_r, pre_i = [], []
    for h in range(RG_HEADS):
        z = _dot(ub[:, h * RG_HEAD_DIM:(h + 1) * RG_HEAD_DIM], gw_ref[h])
        pre_r.append(z[:, :RG_HEAD_DIM])
        pre_i.append(z[:, RG_HEAD_DIM:])
    r = _sigmoid(jnp.concatenate(pre_r, axis=1) + gb[0:1])
    i = _sigmoid(jnp.concatenate(pre_i, axis=1) + gb[1:2])
    sp = _softplus(-lam)
    log_a = -RG_C * r * sp
    a = jnp.exp(log_a)
    mult = jnp.sqrt(1.0 - a * a)
    return r, i, sp, a, mult


def _rg_weight_specs():
    return [_full((4, D_MODEL)), _full((1, D_MODEL)), _full((RG_HEADS, RG_HEAD_DIM, 2 * RG_HEAD_DIM)),
            _full((2, D_MODEL)), _full((1, D_MODEL))]


def rglru_fwd(proj, conv_w, conv_b, gate_w, gate_b, lam, reverse, name):
    rows_total = proj.shape[0]
    rows = min(SCAN_TILE, rows_total)
    n_tiles = rows_total // rows
    tix = (lambda i: n_tiles - 1 - i) if reverse else (lambda i: i)

    def body(xp, xm, xn, cw_ref, cb_ref, gw_ref, gb_ref, lam_ref, h_ref, acts_ref, carry):
        i = pl.program_id(0)
        t = tix(i)
        ext = _extend(xp, xm, xn, t == 0, t == n_tiles - 1)
        ua = _conv(ext, cw_ref[...], 2, rows) + cb_ref[...]
        r, gi, _, a, mult = _rg_gates(ua, gw_ref, gb_ref[...], lam_ref[...])
        for k, saved in enumerate((ua, r, gi, a, mult)):
            acts_ref[k] = saved
        b = mult * (gi * ua)

        @pl.when(i == 0)
        def _():
            carry[...] = jnp.zeros_like(carry)

        h = _scan(a, b, carry[0:1], reverse)
        h_ref[...] = h
        edge = h[0:1] if reverse else h[rows - 1:rows]
        carry[...] = jnp.broadcast_to(edge, carry.shape)

    return pl.pallas_call(
        body, name=name,
        out_shape=(jax.ShapeDtypeStruct((rows_total, D_MODEL), F32),
                   jax.ShapeDtypeStruct((5, rows_total, D_MODEL), F32)),
        grid=(n_tiles,),
        in_specs=_halo_specs(rows, D_MODEL, 0, n_tiles, tix) + _rg_weight_specs(),
        out_specs=(pl.BlockSpec((rows, D_MODEL), lambda i: (tix(i), 0)),
                   pl.BlockSpec((5, rows, D_MODEL), lambda i: (0, tix(i), 0))),
        scratch_shapes=[pltpu.VMEM((SUBLANES, D_MODEL), F32)],
        compiler_params=_params("arbitrary"),
    )(proj, proj, proj, conv_w, conv_b, gate_w, gate_b, lam)


def rglru_bwd(proj, dycat, h_dir, acts, gate_w, lam, add_dua, reverse, name):
    rows_total = proj.shape[0]
    rows = min(SCAN_TILE, rows_total)
    n_tiles = rows_total // rows
    tix = (lambda i: i) if reverse else (lambda i: n_tiles - 1 - i)
    za_block = 1

    def body(acts_ref, za_ref, dya_ref, hp, hm, hn, gw_ref, lam_ref, *rest):
        other = rest[0][...] if add_dua is not None else 0.0
        dua_ref, dgw_ref, dgb_ref, dlam_ref, carry = rest[-5:]
        step = pl.program_id(0)
        t = tix(step)
        first, last = t == 0, t == n_tiles - 1
        ua, r, gi, a, mult = (acts_ref[k] for k in range(5))
        lam_v = lam_ref[...]
        sp = _softplus(-lam_v)
        za = za_ref[...]
        dh = dya_ref[...] * (za * _sigmoid(za))

        @pl.when(step == 0)
        def _():
            carry[...] = jnp.zeros_like(carry)

        old = carry[0:1]
        mu = _scan(a, a * dh, old, not reverse)
        row = lax.broadcasted_iota(jnp.int32, mu.shape, 0)
        if reverse:
            mu_next = jnp.where(row == 0, old, pltpu.roll(mu, 1, 0))
            carry[...] = jnp.broadcast_to(mu[rows - 1:rows], carry.shape)
            h_ext = _extend(hp, hm, hn, first, last)
            h_prev = _shifted(h_ext, 1, rows)
        else:
            mu_next = jnp.where(row == rows - 1, old, pltpu.roll(mu, rows - 1, 0))
            carry[...] = jnp.broadcast_to(mu[0:1], carry.shape)
            h_ext = _extend(hp, hm, hn, first, last)
            h_prev = _shifted(h_ext, -1, rows)
        db = dh + mu_next
        da = db * h_prev
        d_mult = db * (gi * ua)
        di = db * (mult * ua)
        dua = db * (mult * gi)
        dlog_a = da * a - d_mult * (a * a) / mult
        dr = dlog_a * (-RG_C * sp)
        dlam = _colsum(dlog_a * (-RG_C * r)) * (-_sigmoid(-lam_v))
        dpr = dr * (r * (1.0 - r))
        dpi = di * (gi * (1.0 - gi))
        dgb = jnp.concatenate([_colsum(dpr), _colsum(dpi)], axis=0)
        ub = ua.astype(BF16)
        dua_heads, dgw_heads = [], []
        for h in range(RG_HEADS):
            cols = slice(h * RG_HEAD_DIM, (h + 1) * RG_HEAD_DIM)
            dz = jnp.concatenate([dpr[:, cols], dpi[:, cols]], axis=1).astype(BF16)
            dgw_heads.append(_dot_tn(ub[:, cols], dz))
            dua_heads.append(_dot_nt(dz, gw_ref[h]))
        dua_ref[...] = dua + jnp.concatenate(dua_heads, axis=1) + other

        @pl.when(step == 0)
        def _():
            for h in range(RG_HEADS):
                dgw_ref[h] = dgw_heads[h]
            dgb_ref[...] = dgb
            dlam_ref[...] = dlam

        @pl.when(step > 0)
        def _():
            for h in range(RG_HEADS):
                dgw_ref[h] += dgw_heads[h]
            dgb_ref[...] += dgb
            dlam_ref[...] += dlam

    row_spec = lambda col: pl.BlockSpec((rows, D_MODEL), lambda i: (tix(i), col))
    return pl.pallas_call(
        body, name=name,
        out_shape=(jax.ShapeDtypeStruct((rows_total, D_MODEL), F32),
                   jax.ShapeDtypeStruct((RG_HEADS, RG_HEAD_DIM, 2 * RG_HEAD_DIM), F32),
                   jax.ShapeDtypeStruct((2, D_MODEL), F32), jax.ShapeDtypeStruct((1, D_MODEL), F32)),
        grid=(n_tiles,),
        in_specs=([pl.BlockSpec((5, rows, D_MODEL), lambda i: (0, tix(i), 0)), row_spec(za_block), row_spec(0)]
                  + _halo_specs(rows, D_MODEL, 0, n_tiles, tix)
                  + [_full((RG_HEADS, RG_HEAD_DIM, 2 * RG_HEAD_DIM)), _full((1, D_MODEL))]
                  + ([] if add_dua is None else [row_spec(0)])),
        out_specs=(row_spec(0), _full((RG_HEADS, RG_HEAD_DIM, 2 * RG_HEAD_DIM)), _full((2, D_MODEL)),
                   _full((1, D_MODEL))),
        scratch_shapes=[pltpu.VMEM((SUBLANES, D_MODEL), F32)],
        compiler_params=_params("arbitrary"),
    )(acts, proj, dycat, h_dir, h_dir, h_dir, gate_w, lam, *([] if add_dua is None else [add_dua]))


def even_mix_fwd(proj, h_f, h_b, sc_w, name):
    rows_total = proj.shape[0]
    rows = min(2 * MIX_TILE, rows_total)
    n_tiles = rows_total // rows
    cb = D_MODEL
    n_cb = 1
    ident = lambda i: i

    def body(za_ref, hf_ref, hb_ref, xbp, xbm, xbn, gcp, gcm, gcn, gb_ref, zb_ref, w_ref, y_ref):
        t = pl.program_id(1)
        first, last = t == 0, t == n_tiles - 1
        za = za_ref[...]
        y_ref[:, 0:cb] = ((hf_ref[...] + hb_ref[...]) * (za * _sigmoid(za))).astype(BF16)
        p_ext = _extend(xbp, xbm, xbn, first, last) * _extend(gcp, gcm, gcn, first, last)
        cv = _conv(p_ext, w_ref[...], 1, rows)
        zb = zb_ref[...]
        y_ref[:, cb:2 * cb] = (gb_ref[...] * cv * (zb * _sigmoid(zb))).astype(BF16)

    blk = lambda col: pl.BlockSpec((rows, cb), lambda c, i: (i, col * n_cb + c))
    own = pl.BlockSpec((rows, cb), lambda c, i: (i, c))
    return pl.pallas_call(
        body, name=name,
        out_shape=jax.ShapeDtypeStruct((rows_total, 2 * D_MODEL), BF16),
        grid=(n_cb, n_tiles),
        in_specs=([blk(1), own, own] + _halo_specs(rows, cb, 2 * n_cb, n_tiles, ident)
                  + _halo_specs(rows, cb, 4 * n_cb, n_tiles, ident)
                  + [blk(3), blk(5), pl.BlockSpec((3, cb), lambda c, i: (0, c))]),
        out_specs=pl.BlockSpec((rows, 2 * cb), lambda c, i: (i, 0)),
        compiler_params=_params("parallel", "arbitrary"),
    )(proj, h_f, h_b, proj, proj, proj, proj, proj, proj, proj, proj, sc_w)


def even_mix_bwd(proj, dycat, h_f, h_b, dua, conv_w, sc_w, name):
    rows_total = proj.shape[0]
    rows = min(MIX_TILE, rows_total)
    n_tiles = rows_total // rows
    cb = D_MODEL
    n_cb = 1
    ident = lambda i: i

    def body(xap, xam, xan, za_ref, xbp, xbm, xbn, gbp, gbm, gbn, gcp, gcm, gcn, zbp, zbm, zbn,
             dya_ref, dybp, dybm, dybn, hf_ref, hb_ref, dup, dum, dun, cw_ref, sw_ref,
             dp_ref, dcw_ref, dcb_ref, dsw_ref):
        def put(k, value):
            dp_ref[:, k * cb:(k + 1) * cb] = value.astype(BF16)

        t = pl.program_id(1)
        first, last = t == 0, t == n_tiles - 1
        za = za_ref[...]
        sa = _sigmoid(za)
        put(1, dya_ref[...] * (hf_ref[...] + hb_ref[...]) * (sa * (1.0 + za * (1.0 - sa))))
        dua_ext = _extend(dup, dum, dun, first, last)
        cw = cw_ref[...]
        put(0, _conv_transpose(dua_ext, cw, 2, rows))
        dua = dua_ext[SUBLANES:SUBLANES + rows]
        xa_ext = _extend(xap, xam, xan, first, last)
        dcw = jnp.concatenate([_colsum(dua * _shifted(xa_ext, k - 2, rows)) for k in range(4)], axis=0)
        dcb = _colsum(dua)
        xb_ext = _extend(xbp, xbm, xbn, first, last)
        gc_ext = _extend(gcp, gcm, gcn, first, last)
        p_ext = xb_ext * gc_ext
        zb_ext = _extend(zbp, zbm, zbn, first, last)
        sb_ext = _sigmoid(zb_ext)
        dyb_ext = _extend(dybp, dybm, dybn, first, last)
        gb_ext = _extend(gbp, gbm, gbn, first, last)
        dcv_ext = dyb_ext * gb_ext * (zb_ext * sb_ext)
        sw = sw_ref[...]
        p_at = [_shifted(p_ext, k - 1, rows) for k in range(3)]
        cv = (p_at[0] * sw[0:1] + p_at[1] * sw[1:2]) + p_at[2] * sw[2:3]
        mid = slice(SUBLANES, SUBLANES + rows)
        zb, sb, dyb, gb = zb_ext[mid], sb_ext[mid], dyb_ext[mid], gb_ext[mid]
        put(3, dyb * cv * (zb * sb))
        put(5, dyb * gb * cv * (sb * (1.0 + zb * (1.0 - sb))))
        dp = _conv_transpose(dcv_ext, sw, 1, rows)
        put(4, dp * xb_ext[mid])
        put(2, dp * gc_ext[mid])
        dcv = dcv_ext[mid]
        dsw = jnp.concatenate([_colsum(dcv * p_at[k]) for k in range(3)], axis=0)

        @pl.when(t == 0)
        def _():
            dcw_ref[...] = dcw
            dcb_ref[...] = dcb
            dsw_ref[...] = dsw

        @pl.when(t > 0)
        def _():
            dcw_ref[...] += dcw
            dcb_ref[...] += dcb
            dsw_ref[...] += dsw

    blk = lambda col: pl.BlockSpec((rows, cb), lambda c, i: (i, col * n_cb + c))
    halo = lambda col: _halo_specs(rows, cb, col * n_cb, n_tiles, ident)
    own = pl.BlockSpec((rows, cb), lambda c, i: (i, c))
    wspec = lambda k: pl.BlockSpec((k, cb), lambda c, i: (0, c))
    return pl.pallas_call(
        body, name=name,
        out_shape=(jax.ShapeDtypeStruct((rows_total, 6 * D_MODEL), BF16),
                   jax.ShapeDtypeStruct((4, D_MODEL), F32), jax.ShapeDtypeStruct((1, D_MODEL), F32),
                   jax.ShapeDtypeStruct((3, D_MODEL), F32)),
        grid=(n_cb, n_tiles),
        in_specs=(halo(0) + [blk(1)] + halo(2) + halo(3) + halo(4) + halo(5) + [blk(0)] + halo(1)
                  + [own, own] + halo(0) + [wspec(4), wspec(3)]),
        out_specs=(pl.BlockSpec((rows, 6 * cb), lambda c, i: (i, 0)), wspec(4), wspec(1), wspec(3)),
        compiler_params=_params("parallel", "arbitrary"),
    )(proj, proj, proj, proj, proj, proj, proj, proj, proj, proj, proj, proj, proj, proj, proj, proj,
      dycat, dycat, dycat, dycat, h_f, h_b, dua, dua, dua, conv_w, sc_w)


def even_out_fwd(ycat, w_out, gain, x, name):
    rows, d = x.shape
    k = ycat.shape[1]
    tm = min(ROW_TILE, rows)

    def body(yc_ref, w_ref, g_ref, x_ref, x1_ref, y_ref):
        y = _dot(yc_ref[...], w_ref[...])
        y_ref[...] = y
        rstd = lax.rsqrt(jnp.mean(y * y, axis=-1, keepdims=True) + NORM_EPS)
        x1_ref[...] = x_ref[...] + y * rstd * g_ref[...]

    row = lambda n: pl.BlockSpec((tm, n), lambda i: (i, 0))
    return pl.pallas_call(
        body, name=name,
        out_shape=(jax.ShapeDtypeStruct((rows, d), F32),) * 2,
        grid=(rows // tm,),
        in_specs=[row(k), _full((k, d)), _full((1, d)), row(d)],
        out_specs=(row(d), row(d)),
        compiler_params=_params("parallel"),
    )(ycat, w_out, gain, x)


def _rmsnorm_bwd(dout, y, gain):
    rstd = lax.rsqrt(jnp.mean(y * y, axis=-1, keepdims=True) + NORM_EPS)
    yhat = y * rstd
    dyn = dout * gain
    dy = rstd * (dyn - yhat * jnp.mean(dyn * yhat, axis=-1, keepdims=True))
    return dy, dout * yhat


def even_out_bwd(dx1, y, gain, w_out, name):
    rows, d = y.shape
    k = w_out.shape[0]
    tm = min(ROW_TILE, rows)

    def body(dx_ref, y_ref, g_ref, w_ref, dy_ref, dyc_ref, dg_ref):
        dy, dg_rows = _rmsnorm_bwd(dx_ref[...], y_ref[...], g_ref[...])
        dyb = dy.astype(BF16)
        dy_ref[...] = dyb
        dyc_ref[...] = _dot_nt(dyb, w_ref[...])
        _accumulate(dg_ref, _colsum(dg_rows), pl.program_id(0))

    row = lambda n: pl.BlockSpec((tm, n), lambda i: (i, 0))
    return pl.pallas_call(
        body, name=name,
        out_shape=(jax.ShapeDtypeStruct((rows, d), BF16), jax.ShapeDtypeStruct((rows, k), F32),
                   jax.ShapeDtypeStruct((1, d), F32)),
        grid=(rows // tm,),
        in_specs=[row(d), row(d), _full((1, d)), _full((k, d))],
        out_specs=(row(d), row(k), _full((1, d))),
        compiler_params=_params("arbitrary"),
    )(dx1, y, gain, w_out)


def _chunk_cumsum(g, reverse):
    n, c = g.shape
    chunks, per = n // GLA_CHUNK, GLA_CHUNK // SUBLANES
    g = g.reshape(n // SUBLANES, SUBLANES, c)
    pos = lax.broadcasted_iota(jnp.int32, (1, SUBLANES, c), 1)
    s = 1
    while s < SUBLANES:
        if reverse:
            g = g + jnp.where(pos < SUBLANES - s, pltpu.roll(g, SUBLANES - s, 1), 0.0)
        else:
            g = g + jnp.where(pos >= s, pltpu.roll(g, s, 1), 0.0)
        s *= 2
    g = g.reshape(chunks, per, SUBLANES, c)
    out, carry = [None] * per, None
    for k in (range(per - 1, -1, -1) if reverse else range(per)):
        out[k] = g[:, k] if carry is None else g[:, k] + carry
        carry = out[k][:, 0:1] if reverse else out[k][:, SUBLANES - 1:SUBLANES]
    return jnp.stack(out, axis=1).reshape(n, c)


def _gla_prepare(q_ref, k_ref, lr_ref, wg_ref, bg_ref, reverse, n_chunks):
    z = _dot(lr_ref[...].astype(BF16), wg_ref[0]) + bg_ref[0]
    g = -_softplus(-z) * (1.0 / GLA_NORMALIZER)
    bcum = _chunk_cumsum(g, reverse).reshape(n_chunks, GLA_CHUNK, GLA_DK)
    edge = 0 if reverse else GLA_CHUNK - 1
    btot = bcum[:, edge:edge + 1, :]
    e_pos = jnp.exp(bcum)
    e_neg = jnp.exp(-bcum)
    e_st = jnp.exp(btot - bcum)
    q3 = q_ref[...].reshape(n_chunks, GLA_CHUNK, GLA_DK)
    k3 = k_ref[...].reshape(n_chunks, GLA_CHUNK, GLA_DK)
    scale = GLA_DK ** -0.5
    q_in = q3 * scale * e_pos
    k_in = k3 * e_neg
    k_st = k3 * e_st
    dec = jnp.exp(btot)
    return z, q_in, k_in, k_st, dec, (scale * e_pos, e_neg, e_st)


def _gla_mask(reverse):
    i = lax.broadcasted_iota(jnp.int32, (GLA_CHUNK, GLA_CHUNK), 0)
    j = lax.broadcasted_iota(jnp.int32, (GLA_CHUNK, GLA_CHUNK), 1)
    return (j >= i) if reverse else (j <= i)


def _gla_specs(rows, n_blocks, reverse):
    tix = (lambda s: n_blocks - 1 - s) if reverse else (lambda s: s)
    d = 1 if reverse else 0
    lr_block = LR_COL // LANES
    specs = [pl.BlockSpec((rows, GLA_DK), lambda h, s: (tix(s), h)),
             pl.BlockSpec((rows, GLA_DK), lambda h, s: (tix(s), GLA_HEADS + h)),
             pl.BlockSpec((rows, GLA_DV), lambda h, s: (tix(s), GLA_HEADS + h)),
             pl.BlockSpec((rows, LANES), lambda h, s: (tix(s), lr_block)),
             pl.BlockSpec((1, LANES, GLA_DK), lambda h, s: (d, 0, h)),
             pl.BlockSpec((1, 1, GLA_DK), lambda h, s: (d, 0, h))]
    return specs, tix


def gla_fwd(proj, wg_pad, bg, reverse, name):
    rows_total = proj.shape[0]
    rows = min(GLA_BLOCK, rows_total)
    n_blocks = rows_total // rows
    n_chunks = rows // GLA_CHUNK
    specs, tix = _gla_specs(rows, n_blocks, reverse)

    def body(q_ref, k_ref, v_ref, lr_ref, wg_ref, bg_ref, o_ref, st_ref, state, kv_scr, dec_scr):
        _, q_in, k_in, k_st, dec, _ = _gla_prepare(q_ref, k_ref, lr_ref, wg_ref, bg_ref, reverse, n_chunks)
        vb = v_ref[...].reshape(n_chunks, GLA_CHUNK, GLA_DV).astype(BF16)
        qb = q_in.astype(BF16)
        p = jnp.where(_gla_mask(reverse), _bdot(qb, k_in.astype(BF16), 2, 2), 0.0)
        o = _bdot(p.astype(BF16), vb, 2, 1)
        kv_scr[...] = _bdot(vb, k_st.astype(BF16), 1, 1)
        dec_scr[...] = jnp.broadcast_to(dec, dec_scr.shape)

        @pl.when(pl.program_id(1) == 0)
        def _():
            state[...] = jnp.zeros_like(state)

        for c in range(n_chunks):
            cc = n_chunks - 1 - c if reverse else c
            st_ref[0, cc] = state[...]
            state[...] = state[...] * dec_scr[cc, 0:1] + kv_scr[cc]
        o = o + _bdot(qb, st_ref[0].astype(BF16), 2, 2)
        o_ref[...] = o.reshape(rows, GLA_DV)

    return pl.pallas_call(
        body, name=name,
        out_shape=(jax.ShapeDtypeStruct((rows_total, GLA_HEADS * GLA_DV), F32),
                   jax.ShapeDtypeStruct((GLA_HEADS, rows_total // GLA_CHUNK, GLA_DV, GLA_DK), F32)),
        grid=(GLA_HEADS, n_blocks),
        in_specs=specs,
        out_specs=(pl.BlockSpec((rows, GLA_DV), lambda h, s: (tix(s), h)),
                   pl.BlockSpec((1, n_chunks, GLA_DV, GLA_DK), lambda h, s: (h, tix(s), 0, 0))),
        scratch_shapes=[pltpu.VMEM((GLA_DV, GLA_DK), F32), pltpu.VMEM((n_chunks, GLA_DV, GLA_DK), F32),
                        pltpu.VMEM((n_chunks, SUBLANES, GLA_DK), F32)],
        compiler_params=_params("parallel", "arbitrary"),
    )(proj, proj, proj, proj, wg_pad, bg)


def gla_bwd(proj, wg_pad, bg, d_o, states, dqkv_in, reverse, name):
    rows_total = proj.shape[0]
    rows = min(GLA_BLOCK, rows_total)
    n_blocks = rows_total // rows
    n_chunks = rows // GLA_CHUNK
    specs, tix = _gla_specs(rows, n_blocks, not reverse)
    d = 1 if reverse else 0
    specs[4] = pl.BlockSpec((1, LANES, GLA_DK), lambda h, s: (d, 0, h))
    specs[5] = pl.BlockSpec((1, 1, GLA_DK), lambda h, s: (d, 0, h))
    add = dqkv_in is not None

    def body(*refs):
        q_ref, k_ref, v_ref, lr_ref, wg_ref, bg_ref, do_ref, st_ref = refs[:8]
        refs = refs[8:]
        if add:
            aq_ref, ak_ref, av_ref = refs[:3]
            refs = refs[3:]
        dq_ref, dk_ref, dv_ref, dz_ref, dstate, g_scr, dec_scr, dsn_scr = refs
        z, q_in, k_in, k_st, dec, (f_q, f_k, f_s) = _gla_prepare(q_ref, k_ref, lr_ref, wg_ref, bg_ref, reverse,
                                                                 n_chunks)
        mask = _gla_mask(reverse)
        vb = v_ref[...].reshape(n_chunks, GLA_CHUNK, GLA_DV).astype(BF16)
        dob = do_ref[...].reshape(n_chunks, GLA_CHUNK, GLA_DV).astype(BF16)
        qb, kb, ksb = q_in.astype(BF16), k_in.astype(BF16), k_st.astype(BF16)
        st = st_ref[0]
        stb = st.astype(BF16)
        pb = jnp.where(mask, _bdot(qb, kb, 2, 2), 0.0).astype(BF16)
        dpb = jnp.where(mask, _bdot(dob, vb, 2, 2), 0.0).astype(BF16)
        d_qin = _bdot(dpb, kb, 2, 1) + _bdot(dob, stb, 2, 1)
        d_kin = _bdot(dpb, qb, 1, 1)
        dv = _bdot(pb, dob, 1, 1)
        g_scr[...] = _bdot(dob, qb, 1, 1)
        dec_scr[...] = jnp.broadcast_to(dec, dec_scr.shape)

        @pl.when(pl.program_id(1) == 0)
        def _():
            dstate[...] = jnp.zeros_like(dstate)

        for c in range(n_chunks):
            cc = c if reverse else n_chunks - 1 - c
            dsn_scr[cc] = dstate[...]
            dstate[...] = dstate[...] * dec_scr[cc, 0:1] + g_scr[cc]
        dsn = dsn_scr[...]
        dsnb = dsn.astype(BF16)
        dv = dv + _bdot(ksb, dsnb, 2, 2)
        d_kst = _bdot(vb, dsnb, 2, 1)
        d_dec = jnp.sum(dsn * st, axis=1, keepdims=True)
        ks_term = d_kst * k_st
        d_btot = d_dec * dec + jnp.sum(ks_term, axis=1, keepdims=True)
        d_b = d_qin * q_in - d_kin * k_in - ks_term
        pos = lax.broadcasted_iota(jnp.int32, d_b.shape, 1)
        edge = 0 if reverse else GLA_CHUNK - 1
        d_b = d_b + jnp.where(pos == edge, d_btot, 0.0)
        dg = _chunk_cumsum(d_b.reshape(rows, GLA_DK), not reverse)
        dz_ref[...] = dg * (1.0 / GLA_NORMALIZER) * _sigmoid(-z)
        dq = (d_qin * f_q).reshape(rows, GLA_DK)
        dk = (d_kin * f_k + d_kst * f_s).reshape(rows, GLA_DK)
        dv = dv.reshape(rows, GLA_DV)
        if add:
            dq_ref[...] = (dq + aq_ref[...]).astype(BF16)
            dk_ref[...] = (dk + ak_ref[...]).astype(BF16)
            dv_ref[...] = (dv + av_ref[...]).astype(BF16)
        else:
            dq_ref[...] = dq
            dk_ref[...] = dk
            dv_ref[...] = dv

    qkv_specs = [pl.BlockSpec((rows, GLA_DK), lambda h, s: (tix(s), h)),
                 pl.BlockSpec((rows, GLA_DK), lambda h, s: (tix(s), h)),
                 pl.BlockSpec((rows, GLA_DV), lambda h, s: (tix(s), h))]
    in_specs = specs + [pl.BlockSpec((rows, GLA_DV), lambda h, s: (tix(s), h)),
                        pl.BlockSpec((1, n_chunks, GLA_DV, GLA_DK), lambda h, s: (h, tix(s), 0, 0))]
    args = [proj, proj, proj, proj, wg_pad, bg, d_o, states]
    out_dtype = F32
    if add:
        in_specs += qkv_specs
        args += list(dqkv_in)
        out_dtype = BF16
    return pl.pallas_call(
        body, name=name,
        out_shape=(jax.ShapeDtypeStruct((rows_total, GLA_HEADS * GLA_DK), out_dtype),
                   jax.ShapeDtypeStruct((rows_total, GLA_HEADS * GLA_DK), out_dtype),
                   jax.ShapeDtypeStruct((rows_total, GLA_HEADS * GLA_DV), out_dtype),
                   jax.ShapeDtypeStruct((rows_total, GLA_HEADS * GLA_DK), F32)),
        grid=(GLA_HEADS, n_blocks),
        in_specs=in_specs,
        out_specs=(pl.BlockSpec((rows, GLA_DK), lambda h, s: (tix(s), h)),
                   pl.BlockSpec((rows, GLA_DK), lambda h, s: (tix(s), h)),
                   pl.BlockSpec((rows, GLA_DV), lambda h, s: (tix(s), h)),
                   pl.BlockSpec((rows, GLA_DK), lambda h, s: (tix(s), h))),
        scratch_shapes=[pltpu.VMEM((GLA_DV, GLA_DK), F32), pltpu.VMEM((n_chunks, GLA_DV, GLA_DK), F32),
                        pltpu.VMEM((n_chunks, SUBLANES, GLA_DK), F32),
                        pltpu.VMEM((n_chunks, GLA_DV, GLA_DK), F32)],
        compiler_params=_params("parallel", "arbitrary"),
    )(*args)


def gla_gate_bwd(proj, dz_f, dz_b, wg_pad, name):
    rows_total = proj.shape[0]
    tm = min(ROW_TILE, rows_total)
    n_key = GLA_HEADS * GLA_DK

    def body(lr_ref, dzf_ref, dzb_ref, wg_ref, dlr_ref, dwg_ref, dbg_ref):
        step = pl.program_id(0)
        lr_t = jnp.transpose(lr_ref[...])
        dzf, dzb = dzf_ref[...], dzb_ref[...]
        dzf16, dzb16 = dzf.astype(BF16), dzb.astype(BF16)
        dlr_ref[...] = (_dot_nt(dzf16, wg_ref[0]) + _dot_nt(dzb16, wg_ref[1])).astype(BF16)
        dwf = _dot(lr_t[0:GLA_RANK].astype(BF16), dzf16)
        dwb = _dot(lr_t[GLA_RANK:2 * GLA_RANK].astype(BF16), dzb16)
        dbg = jnp.concatenate([_colsum(dzf), _colsum(dzb)], axis=0)

        @pl.when(step == 0)
        def _():
            dwg_ref[0] = dwf
            dwg_ref[1] = dwb
            dbg_ref[...] = dbg

        @pl.when(step > 0)
        def _():
            dwg_ref[0] += dwf
            dwg_ref[1] += dwb
            dbg_ref[...] += dbg

    return pl.pallas_call(
        body, name=name,
        out_shape=(jax.ShapeDtypeStruct((rows_total, LANES), BF16), jax.ShapeDtypeStruct((2, GLA_RANK, n_key), F32),
                   jax.ShapeDtypeStruct((2, n_key), F32)),
        grid=(rows_total // tm,),
        in_specs=[pl.BlockSpec((tm, LANES), lambda i: (i, LR_COL // LANES)),
                  pl.BlockSpec((tm, n_key), lambda i: (i, 0)), pl.BlockSpec((tm, n_key), lambda i: (i, 0)),
                  _full((2, LANES, n_key))],
        out_specs=(pl.BlockSpec((tm, LANES), lambda i: (i, 0)), _full((2, GLA_RANK, n_key)), _full((2, n_key))),
        compiler_params=_params("arbitrary"),
    )(proj, dz_f, dz_b, wg_pad)


def _head_norm(o, gain):
    outs, hats, rstds = [], [], []
    for h in range(GLA_HEADS):
        oh = o[:, h * GLA_DV:(h + 1) * GLA_DV]
        rstd = lax.rsqrt(jnp.mean(oh * oh, axis=-1, keepdims=True) + NORM_EPS)
        hat = oh * rstd
        outs.append(hat * gain)
        hats.append(hat)
        rstds.append(rstd)
    return outs, hats, rstds


def odd_out_fwd(o_f, o_b, proj, head_gain, w_out, gain, x1, target, name):
    rows, d = x1.shape
    tm = min(ROW_TILE, rows)
    r_block = (2 * GLA_HEADS * GLA_DK + GLA_HEADS * GLA_DV) // d

    def body(of_ref, ob_ref, r_ref, hg_ref, w_ref, g_ref, x1_ref, tgt_ref, y2_ref, dy_ref, dx2_ref, loss_ref,
             dg_ref):
        step = pl.program_id(0)
        on, _, _ = _head_norm(of_ref[...] + ob_ref[...], hg_ref[...])
        r = r_ref[...]
        y2 = (jnp.concatenate(on, axis=1) * (r * _sigmoid(r))).astype(BF16)
        y2_ref[...] = y2
        y = _dot(y2, w_ref[...])
        gain_v = g_ref[...]
        rstd = lax.rsqrt(jnp.mean(y * y, axis=-1, keepdims=True) + NORM_EPS)
        x2 = x1_ref[...] + y * rstd * gain_v
        diff = x2 - tgt_ref[...]
        loss = 0.5 * jnp.sum(jnp.mean(diff * diff, axis=-1, keepdims=True), axis=0, keepdims=True)
        dx2 = diff * (1.0 / d)
        dx2_ref[...] = dx2
        dy, dg_rows = _rmsnorm_bwd(dx2, y, gain_v)
        dy_ref[...] = dy.astype(BF16)
        _accumulate(loss_ref, jnp.broadcast_to(loss, loss_ref.shape), step)
        _accumulate(dg_ref, _colsum(dg_rows), step)

    row = lambda n, col=0: pl.BlockSpec((tm, n), lambda i: (i, col))
    return pl.pallas_call(
        body, name=name,
        out_shape=(jax.ShapeDtypeStruct((rows, d), BF16), jax.ShapeDtypeStruct((rows, d), BF16),
                   jax.ShapeDtypeStruct((rows, d), F32), jax.ShapeDtypeStruct((SUBLANES, LANES), F32),
                   jax.ShapeDtypeStruct((1, d), F32)),
        grid=(rows // tm,),
        in_specs=[row(d), row(d), row(d, r_block), _full((1, GLA_DV)), _full((d, d)), _full((1, d)), row(d), row(d)],
        out_specs=(row(d), row(d), row(d), _full((SUBLANES, LANES)), _full((1, d))),
        compiler_params=_params("arbitrary"),
    )(o_f, o_b, proj, head_gain, w_out, gain, x1, target)


def odd_out_bwd(dy, w_out, o_f, o_b, proj, head_gain, name):
    rows, d = dy.shape
    tm = min(ROW_TILE, rows)
    r_block = (2 * GLA_HEADS * GLA_DK + GLA_HEADS * GLA_DV) // d

    def body(dy_ref, w_ref, of_ref, ob_ref, r_ref, hg_ref, dr_ref, do_ref, dhg_ref):
        dy2 = _dot_nt(dy_ref[...], w_ref[...])
        hg = hg_ref[...]
        on, hats, rstds = _head_norm(of_ref[...] + ob_ref[...], hg)
        r = r_ref[...]
        sr = _sigmoid(r)
        dr_ref[...] = (dy2 * jnp.concatenate(on, axis=1) * (sr * (1.0 + r * (1.0 - sr)))).astype(BF16)
        d_on = dy2 * (r * sr)
        d_os, dhg = [], None
        for h in range(GLA_HEADS):
            dn = d_on[:, h * GLA_DV:(h + 1) * GLA_DV]
            part = _colsum(dn * hats[h])
            dhg = part if dhg is None else dhg + part
            dng = dn * hg
            d_os.append(rstds[h] * (dng - hats[h] * jnp.mean(dng * hats[h], axis=-1, keepdims=True)))
        do_ref[...] = jnp.concatenate(d_os, axis=1)
        _accumulate(dhg_ref, dhg, pl.program_id(0))

    row = lambda n, col=0: pl.BlockSpec((tm, n), lambda i: (i, col))
    return pl.pallas_call(
        body, name=name,
        out_shape=(jax.ShapeDtypeStruct((rows, d), BF16), jax.ShapeDtypeStruct((rows, d), F32),
                   jax.ShapeDtypeStruct((1, GLA_DV), F32)),
        grid=(rows // tm,),
        in_specs=[row(d), _full((d, d)), row(d), row(d), row(d, r_block), _full((1, GLA_DV))],
        out_specs=(row(d), row(d), _full((1, GLA_DV))),
        compiler_params=_params("arbitrary"),
    )(dy, w_out, o_f, o_b, proj, head_gain)


def local_step(x, target, w, reduce_first=None, reduce_second=None, late_weights=None):
    g, g16 = {}, {}
    proj_e, h0 = norm_matmul(x, w["even_norm_pre"], w["even_w_in"], "even_in_proj")
    h_dir, acts = zip(*[rglru_fwd(proj_e, w["rg_conv_w"], w["rg_conv_b"], w["rg_gate_w"][d], w["rg_gate_b"][d],
                                  w["rg_lambda"][d], d == 1, "rglru_fwd_%d" % d) for d in range(2)])
    ycat = even_mix_fwd(proj_e, h_dir[0], h_dir[1], w["sc_conv_w"], "even_mix_fwd")
    if late_weights is not None:
        w = dict(w, **late_weights(ycat))
    x1, y_e = even_out_fwd(ycat, w["even_w_out"], w["even_norm_post"], x, "even_out_fwd")
    proj_o, h1 = norm_matmul(x1, w["odd_norm_pre"], w["odd_w_in"], "odd_in_proj")
    o_dir, st_dir = [], []
    for d in range(2):
        o, st = gla_fwd(proj_o, w["gla_wg_pad"], w["gla_b_gate"], d == 1, "gla_fwd_%d" % d)
        o_dir.append(o)
        st_dir.append(st)
    y2, dy_o, dx2, loss, g["odd_norm_post"] = odd_out_fwd(
        o_dir[0], o_dir[1], proj_o, w["gla_norm_g"], w["odd_w_out"], w["odd_norm_post"], x1, target, "odd_out_fwd")
    g["odd_w_out"], g16["odd_w_out"] = (a[0] for a in matmul_dw(y2, dy_o, D_MODEL, "odd_w_out_grad"))
    dr, d_o, g["gla_norm_g"] = odd_out_bwd(dy_o, w["odd_w_out"], o_dir[0], o_dir[1], proj_o, w["gla_norm_g"],
                                           "odd_out_bwd")
    dq, dk, dv, dz_f = gla_bwd(proj_o, w["gla_wg_pad"], w["gla_b_gate"], d_o, st_dir[0], None, False, "gla_bwd_0")
    dq, dk, dv, dz_b = gla_bwd(proj_o, w["gla_wg_pad"], w["gla_b_gate"], d_o, st_dir[1], (dq, dk, dv), True,
                               "gla_bwd_1")
    dlr, g["gla_w_gate_lr"], g["gla_b_gate"] = gla_gate_bwd(proj_o, dz_f, dz_b, w["gla_wg_pad"], "gla_gate_bwd")
    dproj_o = [dq, dk, dv, dr, dlr]
    g["odd_w_in"] = jnp.concatenate(matmul_dw_pieces(h1, dproj_o, "odd_w_in_grad"), axis=1)[:, :ODD_IN]
    dx1, g["odd_norm_pre"] = inproj_bwd_pieces(dproj_o, w["odd_w_in"], x1, w["odd_norm_pre"], dx2, "odd_in_proj_bwd")
    dy_e, dycat, g["even_norm_post"] = even_out_bwd(dx1, y_e, w["even_norm_post"], w["even_w_out"], "even_out_bwd")
    g["even_w_out"], g16["even_w_out"] = (a[0] for a in matmul_dw(ycat, dy_e, D_MODEL, "even_w_out_grad"))
    lam = w["rg_lambda"] if reduce_first is None else w["rg_lambda"] + reduce_first(g, g16)
    dua, dgw, dgb, dlam = None, [], [], []
    for d in range(2):
        a, b, c, e = rglru_bwd(proj_e, dycat, h_dir[d], acts[d], w["rg_gate_w"][d], lam[d], dua, d == 1,
                               "rglru_bwd_%d" % d)
        dua = a
        dgw.append(b)
        dgb.append(c)
        dlam.append(e)
    dproj_e, g["rg_conv_w"], g["rg_conv_b"], g["sc_conv_w"] = even_mix_bwd(
        proj_e, dycat, h_dir[0], h_dir[1], dua, w["rg_conv_w"], w["sc_conv_w"], "even_mix_bwd")
    dgw = jnp.stack(dgw).reshape(2, RG_HEADS, RG_HEAD_DIM, 2, RG_HEAD_DIM)
    g["rg_gate_w"] = jnp.transpose(dgw, (0, 3, 1, 2, 4))
    g["rg_gate_b"] = jnp.stack(dgb).reshape(2, 2, RG_HEADS, RG_HEAD_DIM)
    g["rg_lambda"] = jnp.concatenate(dlam, axis=0)
    g["even_w_in"], g16["even_w_in"] = matmul_dw(h0, dproj_e, EVEN_IN // 4, "even_w_in_grad")
    gain = w["even_norm_pre"] if reduce_second is None else w["even_norm_pre"] + reduce_second(g, g16)
    grad_x, g["even_norm_pre"] = inproj_bwd(dproj_e, w["even_w_in"], x, gain, dx1, "even_in_proj_bwd")
    return loss, grad_x, g


def _prepare_weights(full):
    w = {}
    for name in ("even_norm_pre", "even_norm_post", "rg_conv_b", "odd_norm_pre", "odd_norm_post", "gla_norm_g"):
        if name in full:
            w[name] = full[name].reshape(1, -1)
    for name in ("rg_conv_w", "sc_conv_w"):
        if name in full:
            w[name] = full[name]
    for name in ("even_w_out", "odd_w_out"):
        if name in full:
            w[name] = full[name].astype(BF16)
    if "even_w_in" in full:
        w["even_w_in"] = full["even_w_in"].astype(BF16)
        if w["even_w_in"].ndim == 2:
            w["even_w_in"] = jnp.transpose(w["even_w_in"].reshape(D_MODEL, 4, EVEN_IN // 4), (1, 0, 2))
    if "rg_gate_w" in full:
        gw = jnp.transpose(full["rg_gate_w"].astype(BF16), (0, 2, 3, 1, 4))
        w["rg_gate_w"] = gw.reshape(2, RG_HEADS, RG_HEAD_DIM, 2 * RG_HEAD_DIM)
        w["rg_gate_b"] = full["rg_gate_b"].reshape(2, 2, D_MODEL)
        w["rg_lambda"] = full["rg_lambda"].reshape(2, 1, D_MODEL)
    if "odd_w_in" in full:
        w_in = jnp.pad(full["odd_w_in"].astype(BF16), ((0, 0), (0, ODD_IN_PAD - ODD_IN)))
        w["odd_w_in"] = w_in.reshape(1, D_MODEL, ODD_IN_PAD)
    if "gla_w_gate_lr" in full:
        wg = full["gla_w_gate_lr"].astype(BF16)
        w["gla_wg_pad"] = jnp.stack([jnp.pad(wg[d], ((d * GLA_RANK, LANES - (d + 1) * GLA_RANK), (0, 0)))
                                     for d in range(2)])
        w["gla_b_gate"] = full["gla_b_gate"].reshape(2, 1, GLA_HEADS * GLA_DK)
    return w


SHARDED_SMALL = (("rg_conv_w", (4, 256)), ("rg_lambda", (2, 256)), ("sc_conv_w", (3, 256)),
                 ("odd_norm_pre", (256,)), ("odd_norm_post", (256,)), ("gla_w_gate_lr", (2, 16, 128)),
                 ("gla_b_gate", (2, 128)), ("gla_norm_g", (64,)))
SHARDED_ROWS = 96
REPLICATED = (("rg_gate_w", (2, 2, 8, 128, 128)), ("even_norm_post", (1024,)), ("rg_conv_b", (1024,)),
              ("rg_gate_b", (2, 2, 8, 128)))
GATE_ROWS = 4096
LAST_REPLICATED = (("even_norm_pre", (1024,)),)
LAST_ROWS = 8
REPLICATED_ROWS = 4160
REP_PART = REPLICATED_ROWS // 8
HALF_SHARDED = SHARDED_ROWS // 2
PACK_HALF = HALF_SHARDED + REP_PART


def _seg_rows(shape):
    n = 1
    for s in shape:
        n *= s
    return -(-n // (SUBLANES * LANES)) * SUBLANES


def _pack(arrays, spec, total_rows, lead=()):
    parts = []
    for name, shape in spec:
        flat = arrays[name].reshape(lead + (-1,))
        pad = _seg_rows(shape) * LANES - flat.shape[-1]
        if pad:
            flat = jnp.pad(flat, [(0, 0)] * len(lead) + [(0, pad)])
        parts.append(flat.reshape(lead + (-1, LANES)))
    rows = jnp.concatenate(parts, axis=len(lead))
    pad = total_rows - rows.shape[len(lead)]
    return jnp.pad(rows, [(0, 0)] * len(lead) + [(0, pad), (0, 0)])


def _unpack(rows, spec, lead=()):
    out, at = {}, 0
    for name, shape in spec:
        n = 1
        for s in shape:
            n *= s
        k = _seg_rows(shape)
        seg = lax.slice_in_dim(rows, at, at + k, axis=len(lead)).reshape(lead + (-1,))
        out[name] = lax.slice_in_dim(seg, 0, n, axis=len(lead)).reshape(lead + shape)
        at += k
    return out


def _split_owners(arr):
    a = arr.reshape(arr.shape[:-1] + (4, arr.shape[-1] // 4))
    return jnp.moveaxis(a, -2, 0)


def _merge_owners(arr):
    a = jnp.moveaxis(arr, 0, -2)
    return a.reshape(a.shape[:-2] + (-1,))


HBM_SPEC = pl.BlockSpec(memory_space=pltpu.HBM)


def _position():
    x, y, c = lax.axis_index("x"), lax.axis_index("y"), lax.axis_index("c")
    chips = [(1 - x, y), (x, 1 - y), (1 - x, 1 - y)]
    return x, y, c, chips


def _remote(src, dst, send_sem, recv_sem, device):
    return pltpu.make_async_remote_copy(src_ref=src, dst_ref=dst, send_sem=send_sem, recv_sem=recv_sem,
                                        device_id=device, device_id_type=MESH)


SEM_SPEC = pl.BlockSpec(memory_space=pltpu.SEMAPHORE)
SIDE_EFFECT = pltpu.SideEffectType.DATAFLOW_SIDE_EFFECTING


def _gather_copies(ins, lands, n_h, send_sems, recv_sems):
    x, y, c, chips = _position()
    me = 2 * x + y
    copies = []
    for a in range(len(ins)):
        for k, chip in enumerate(chips):
            src = ins[a].at[c] if a < n_h else ins[a]
            dst = lands[a].at[me, c] if a < n_h else lands[a].at[me]
            copies.append(_remote(src, dst, send_sems.at[3 * a + k], recv_sems.at[3 * a + k], (chip[0], chip[1], c)))
    return copies


def gather_start(halved, whole, name):
    arrays = list(halved) + list(whole)
    n, n_h = len(arrays), len(halved)
    lands = [lax.empty((4,) + a.shape, a.dtype) for a in arrays]

    def body(*refs):
        ins, lz, send_sems, recv_sems, token = refs[:n], refs[n:2 * n], refs[2 * n], refs[2 * n + 1], refs[-1]
        for cp in _gather_copies(ins, lz, n_h, send_sems, recv_sems):
            cp.start()
        token[...] = jnp.zeros_like(token)

    operands = [pltpu.with_memory_space_constraint(a, pltpu.HBM) for a in arrays + lands]
    return pl.pallas_call(
        body, name=name,
        out_shape=(pltpu.SemaphoreType.DMA((3 * n,)), pltpu.SemaphoreType.DMA((3 * n,)))
        + tuple(pltpu.HBM(a.shape, a.dtype) for a in operands) + (jax.ShapeDtypeStruct((SUBLANES, LANES), F32),),
        in_specs=[HBM_SPEC] * (2 * n),
        out_specs=(SEM_SPEC, SEM_SPEC) + (HBM_SPEC,) * (2 * n) + (pl.BlockSpec(memory_space=pltpu.VMEM),),
        input_output_aliases={i: 2 + i for i in range(2 * n)},
        compiler_params=pltpu.CompilerParams(has_side_effects=SIDE_EFFECT),
    )(*operands)


def gather_wait(started, n_h, after, name):
    send_sems, recv_sems = started[0], started[1]
    operands = list(started[2:-1])
    n = len(operands) // 2

    def body(*refs):
        ins, lz, send_ref, recv_ref = refs[:n], refs[n:2 * n], refs[2 * n], refs[2 * n + 1]
        for cp in _gather_copies(ins, lz, n_h, send_ref, recv_ref):
            cp.wait_send()
            cp.wait_recv()

    outs = pl.pallas_call(
        body, name=name,
        out_shape=tuple(pltpu.HBM(a.shape, a.dtype) for a in operands),
        in_specs=[HBM_SPEC] * (2 * n) + [SEM_SPEC, SEM_SPEC, pl.BlockSpec(memory_space=pl.ANY)],
        out_specs=(HBM_SPEC,) * (2 * n),
        input_output_aliases={i: i for i in range(2 * n)},
        compiler_params=pltpu.CompilerParams(has_side_effects=SIDE_EFFECT),
    )(*operands, send_sems, recv_sems, after)
    return outs[n:]


def pass_to_sibling(fulls, name):
    n = len(fulls)

    def body(*refs):
        bufs = refs[n:2 * n]
        send_sems, recv_sems = refs[2 * n:]
        x, y, c, chips = _position()
        sibling = (x, y, 1 - c)
        copies = []
        for a in range(n):
            for k, chip in enumerate(chips):
                q = 2 * chip[0] + chip[1]
                cp = _remote(bufs[a].at[q, c], bufs[a].at[q, c], send_sems.at[3 * a + k], recv_sems.at[3 * a + k],
                             sibling)
                cp.start()
                copies.append(cp)
        for a in range(n):
            for k, chip in enumerate(chips):
                q = 2 * chip[0] + chip[1]
                passed = bufs[a].at[q, 1 - c]
                _remote(passed, passed, send_sems.at[3 * a + k], recv_sems.at[3 * a + k], sibling).wait_recv()
        for cp in copies:
            cp.wait_send()

    return pl.pallas_call(
        body, name=name,
        out_shape=[jax.ShapeDtypeStruct(a.shape, a.dtype) for a in fulls],
        in_specs=[HBM_SPEC] * n, out_specs=[HBM_SPEC] * n,
        input_output_aliases={i: i for i in range(n)},
        scratch_shapes=[pltpu.SemaphoreType.DMA((3 * n,)), pltpu.SemaphoreType.DMA((3 * n,))],
    )(*fulls)


def place_own(full, own, chip, name):
    _, _, r, cols = full.shape
    tr = _row_tile(r, cols)

    def body(p_ref, own_ref, full_ref, o_ref):
        o_ref[0] = own_ref[...]

    return pl.pallas_call(
        body, name=name,
        out_shape=jax.ShapeDtypeStruct(full.shape, full.dtype),
        grid_spec=pltpu.PrefetchScalarGridSpec(
            num_scalar_prefetch=1, grid=(2, r // tr),
            in_specs=[pl.BlockSpec((1, tr, cols), lambda h, i, p_ref: (h, i, 0)), pl.BlockSpec(memory_space=pl.ANY)],
            out_specs=pl.BlockSpec((1, 1, tr, cols), lambda h, i, p_ref: (p_ref[0], h, i, 0))),
        input_output_aliases={2: 0},
        compiler_params=_params("parallel", "parallel"),
    )(chip, own, full)


def exchange_with_sibling(arrays, name):
    n = len(arrays)

    def body(*refs):
        ins, outs = refs[:n], refs[n:2 * n]
        send_sems, recv_sems = refs[2 * n:]
        x, y, c, _ = _position()
        copies = []
        for a in range(n):
            cp = _remote(ins[a].at[:, 1 - c], outs[a], send_sems.at[a], recv_sems.at[a], (x, y, 1 - c))
            cp.start()
            copies.append(cp)
        for cp in copies:
            cp.wait()

    return pl.pallas_call(
        body, name=name,
        out_shape=[jax.ShapeDtypeStruct((a.shape[0],) + a.shape[2:], a.dtype) for a in arrays],
        in_specs=[HBM_SPEC] * n, out_specs=[HBM_SPEC] * n,
        scratch_shapes=[pltpu.SemaphoreType.DMA((n,)), pltpu.SemaphoreType.DMA((n,))],
    )(*arrays)


def _chip_copies(ins, lands, send_sems, recv_sems):
    x, y, c, chips = _position()
    copies = []
    for a in range(len(ins)):
        for k, chip in enumerate(chips):
            q = 2 * chip[0] + chip[1]
            copies.append(_remote(ins[a].at[q], lands[a].at[k], send_sems.at[3 * a + k], recv_sems.at[3 * a + k],
                                  (chip[0], chip[1], c)))
    return copies


def exchange_with_chips_start(arrays, name):
    n = len(arrays)
    lands = [lax.empty((3,) + a.shape[1:], a.dtype) for a in arrays]

    def body(*refs):
        ins, lz, send_sems, recv_sems, token = refs[:n], refs[n:2 * n], refs[2 * n], refs[2 * n + 1], refs[-1]
        for cp in _chip_copies(ins, lz, send_sems, recv_sems):
            cp.start()
        token[...] = jnp.zeros_like(token)

    operands = [pltpu.with_memory_space_constraint(a, pltpu.HBM) for a in list(arrays) + lands]
    return pl.pallas_call(
        body, name=name,
        out_shape=(pltpu.SemaphoreType.DMA((3 * n,)), pltpu.SemaphoreType.DMA((3 * n,)))
        + tuple(pltpu.HBM(a.shape, a.dtype) for a in operands) + (jax.ShapeDtypeStruct((SUBLANES, LANES), F32),),
        in_specs=[HBM_SPEC] * (2 * n),
        out_specs=(SEM_SPEC, SEM_SPEC) + (HBM_SPEC,) * (2 * n) + (pl.BlockSpec(memory_space=pltpu.VMEM),),
        input_output_aliases={i: 2 + i for i in range(2 * n)},
        compiler_params=pltpu.CompilerParams(has_side_effects=SIDE_EFFECT),
    )(*operands)


def exchange_with_chips_wait(started, after, name):
    send_sems, recv_sems = started[0], started[1]
    operands = list(started[2:-1])
    n = len(operands) // 2

    def body(*refs):
        ins, lz, send_ref, recv_ref = refs[:n], refs[n:2 * n], refs[2 * n], refs[2 * n + 1]
        for cp in _chip_copies(ins, lz, send_ref, recv_ref):
            cp.wait_send()
            cp.wait_recv()

    outs = pl.pallas_call(
        body, name=name,
        out_shape=tuple(pltpu.HBM(a.shape, a.dtype) for a in operands),
        in_specs=[HBM_SPEC] * (2 * n) + [SEM_SPEC, SEM_SPEC, pl.BlockSpec(memory_space=pl.ANY)],
        out_specs=(HBM_SPEC,) * (2 * n),
        input_output_aliases={i: i for i in range(2 * n)},
        compiler_params=pltpu.CompilerParams(has_side_effects=SIDE_EFFECT),
    )(*operands, send_sems, recv_sems, after)
    return outs[:n], outs[n:]


def share_totals(totals, pack_total, last_part):
    arrays = list(totals) + [pack_total]
    n = len(arrays)

    def body(*refs):
        ins, last, outs, rep, last_all = refs[:n], refs[n], refs[n + 1:2 * n + 1], refs[2 * n + 1], refs[2 * n + 2]
        send_sems, recv_sems, rep_send, rep_recv, last_send, last_recv = refs[2 * n + 3:]
        x, y, c, chips = _position()
        sibling = (x, y, 1 - c)
        me = 4 * x + 2 * y + c
        sends = []
        for a in range(n):
            cp = _remote(ins[a], outs[a], send_sems.at[a], recv_sems.at[a], sibling)
            cp.start()
            sends.append(cp)
        mine = ins[n - 1].at[pl.ds(HALF_SHARDED, REP_PART)]
        peers = [sibling]
        for chip in chips:
            peers += [(chip[0], chip[1], c), (chip[0], chip[1], 1 - c)]
        for j, peer in enumerate(peers):
            for src, dst, s_sem, r_sem in ((mine, rep, rep_send, rep_recv), (last, last_all, last_send, last_recv)):
                cp = _remote(src, dst.at[me], s_sem.at[j], r_sem.at[j], peer)
                cp.start()
                sends.append(cp)
        for a in range(n):
            _remote(outs[a], outs[a], send_sems.at[a], recv_sems.at[a], sibling).wait_recv()
        for j, peer in enumerate(peers):
            it = 4 * peer[0] + 2 * peer[1] + peer[2]
            _remote(rep.at[it], rep.at[it], rep_send.at[j], rep_recv.at[j], peer).wait_recv()
            _remote(last_all.at[it], last_all.at[it], last_send.at[j], last_recv.at[j], peer).wait_recv()
        for cp in sends:
            cp.wait_send()

    outs = pl.pallas_call(
        body, name="grad_share_totals",
        out_shape=[jax.ShapeDtypeStruct(a.shape, a.dtype) for a in arrays]
        + [jax.ShapeDtypeStruct((8, REP_PART, LANES), F32), jax.ShapeDtypeStruct((8,) + last_part.shape, F32)],
        in_specs=[HBM_SPEC] * (n + 1), out_specs=[HBM_SPEC] * (n + 2),
        scratch_shapes=[pltpu.SemaphoreType.DMA((n,)), pltpu.SemaphoreType.DMA((n,))]
        + [pltpu.SemaphoreType.DMA((7,))] * 4,
    )(*arrays, last_part)
    return outs[:n], outs[n], outs[n + 1]


def sum_parts(parts, name):
    def body(p_ref, o_ref):
        total = p_ref[0]
        for k in range(1, parts.shape[0]):
            total = total + p_ref[k]
        o_ref[...] = total

    return pl.pallas_call(body, name=name, out_shape=jax.ShapeDtypeStruct(parts.shape[1:], parts.dtype))(parts)


TILE_BYTES = 2 << 20


def _row_tile(rows, cols):
    best = None
    for t in range(SUBLANES, rows + 1, SUBLANES):
        if rows % t == 0 and t * cols * 4 <= TILE_BYTES:
            best = t
    return best if best is not None else rows


def add_sibling(mine, received, core, out_dtype, name):
    _, _, r, cols = mine.shape
    tr = _row_tile(r, cols)

    def body(c_ref, a_ref, b_ref, o_ref):
        o_ref[...] = (a_ref[0] + b_ref[...].astype(F32)).astype(out_dtype)

    return pl.pallas_call(
        body, name=name,
        out_shape=jax.ShapeDtypeStruct((4, r, cols), out_dtype),
        grid_spec=pltpu.PrefetchScalarGridSpec(
            num_scalar_prefetch=1, grid=(4, r // tr),
            in_specs=[pl.BlockSpec((1, 1, tr, cols), lambda o, i, c_ref: (o, c_ref[0], i, 0)),
                      pl.BlockSpec((1, tr, cols), lambda o, i, c_ref: (o, i, 0))],
            out_specs=pl.BlockSpec((1, tr, cols), lambda o, i, c_ref: (o, i, 0))),
        compiler_params=_params("parallel", "parallel"),
    )(core, mine, received)


def add_chips(own, received, chip, name):
    _, r, cols = own.shape
    tr = _row_tile(r, cols)

    def body(p_ref, a_ref, b0, b1, b2, o_ref):
        o_ref[...] = ((a_ref[0].astype(F32) + b0[0].astype(F32)) + b1[0].astype(F32)) + b2[0].astype(F32)

    rb = lambda k: pl.BlockSpec((1, tr, cols), lambda i, p_ref: (k, i, 0))
    return pl.pallas_call(
        body, name=name,
        out_shape=jax.ShapeDtypeStruct((r, cols), F32),
        grid_spec=pltpu.PrefetchScalarGridSpec(
            num_scalar_prefetch=1, grid=(r // tr,),
            in_specs=[pl.BlockSpec((1, tr, cols), lambda i, p_ref: (p_ref[0], i, 0)), rb(0), rb(1), rb(2)],
            out_specs=pl.BlockSpec((tr, cols), lambda i, p_ref: (i, 0))),
        compiler_params=_params("parallel"),
    )(chip, own, received, received, received)


def _adamw_update(gv, w_ref, m_ref, v_ref, d_ref, nm_ref, nv_ref):
    nm = ADAM_B1 * m_ref[...] + (1.0 - ADAM_B1) * gv
    nv = ADAM_B2 * v_ref[...] + (1.0 - ADAM_B2) * (gv * gv)
    nm_ref[...] = nm
    nv_ref[...] = nv
    m_hat = nm / (1.0 - ADAM_B1 ** ADAM_STEP)
    v_hat = nv / (1.0 - ADAM_B2 ** ADAM_STEP)
    d_ref[...] = -ADAM_LR * (m_hat / (jnp.sqrt(v_hat) + ADAM_EPS) + ADAM_WD * w_ref[...])


def adamw_halves(w, own, received, m, v, core, name, by_columns=False):
    rows, cols = w.shape

    def body(c_ref, w_ref, own_ref, rec_ref, m_ref, v_ref, g_ref, d_ref, nm_ref, nv_ref):
        gv = jnp.where(pl.program_id(0) == c_ref[0], own_ref[...], rec_ref[...])
        g_ref[...] = gv
        _adamw_update(gv, w_ref, m_ref, v_ref, d_ref, nm_ref, nv_ref)

    if by_columns:
        nr = 1
        whole = pl.BlockSpec((rows, cols // 2), lambda h, i, c_ref: (0, h))
        half = pl.BlockSpec((rows, cols // 2), lambda h, i, c_ref: (0, 0))
    else:
        r = rows // 2
        tr = _row_tile(r, cols)
        nr = r // tr
        whole = pl.BlockSpec((tr, cols), lambda h, i, c_ref: (h * nr + i, 0))
        half = pl.BlockSpec((tr, cols), lambda h, i, c_ref: (i, 0))
    return pl.pallas_call(
        body, name=name,
        out_shape=(jax.ShapeDtypeStruct((rows, cols), F32),) * 4,
        grid_spec=pltpu.PrefetchScalarGridSpec(
            num_scalar_prefetch=1, grid=(2, nr),
            in_specs=[whole, half, half, whole, whole], out_specs=(whole,) * 4),
        compiler_params=_params("parallel", "parallel"),
    )(core, w, own, received, m, v)


def adamw_many(ws, gs, ms, vs, name):
    n = len(ws)

    def body(*refs):
        ins, outs = refs[:4 * n], refs[4 * n:]
        for k in range(n):
            w_ref, g_ref, m_ref, v_ref = (ins[j * n + k] for j in range(4))
            d_ref, nm_ref, nv_ref = outs[3 * k:3 * k + 3]
            _adamw_update(g_ref[...], w_ref, m_ref, v_ref, d_ref, nm_ref, nv_ref)

    flat = pl.pallas_call(
        body, name=name,
        out_shape=[jax.ShapeDtypeStruct(w.shape, F32) for w in ws for _ in range(3)],
    )(*ws, *gs, *ms, *vs)
    return [tuple(flat[3 * k:3 * k + 3]) for k in range(n)]


def adamw(w, g, m, v, name):
    r, cols = w.shape
    tr = _row_tile(r, cols)

    def body(w_ref, g_ref, m_ref, v_ref, g_out, d_ref, nm_ref, nv_ref):
        gv = g_ref[...]
        g_out[...] = gv
        _adamw_update(gv, w_ref, m_ref, v_ref, d_ref, nm_ref, nv_ref)

    blk = pl.BlockSpec((tr, cols), lambda i: (i, 0))
    return pl.pallas_call(
        body, name=name,
        out_shape=(jax.ShapeDtypeStruct((r, cols), F32),) * 4,
        grid=(r // tr,),
        in_specs=[blk] * 4, out_specs=(blk,) * 4,
        compiler_params=_params("parallel"),
    )(w, g, m, v)


WEIGHTS = ("even_norm_pre", "even_norm_post", "even_w_in", "rg_conv_w", "rg_conv_b", "rg_gate_w", "rg_gate_b",
           "rg_lambda", "sc_conv_w", "even_w_out", "odd_norm_pre", "odd_norm_post", "odd_w_in", "gla_w_gate_lr",
           "gla_b_gate", "gla_norm_g", "odd_w_out")
BIG = ("even_w_in", "even_w_out", "odd_w_in", "odd_w_out")


def _halves(a):
    return a.reshape((2, a.shape[0] // 2) + a.shape[1:])


def kernel(x, even_norm_pre, even_norm_post, even_w_in, rg_conv_w, rg_conv_b, rg_gate_w, rg_gate_b, rg_lambda, sc_conv_w, even_w_out, odd_norm_pre, odd_norm_post, odd_w_in, gla_w_gate_lr, gla_b_gate, gla_norm_g, odd_w_out, loss_target, m_even_norm_pre, m_even_norm_post, m_even_w_in, m_rg_conv_w, m_rg_conv_b, m_rg_gate_w, m_rg_gate_b, m_rg_lambda, m_sc_conv_w, m_even_w_out, m_odd_norm_pre, m_odd_norm_post, m_odd_w_in, m_gla_w_gate_lr, m_gla_b_gate, m_gla_norm_g, m_odd_w_out, v_even_norm_pre, v_even_norm_post, v_even_w_in, v_rg_conv_w, v_rg_conv_b, v_rg_gate_w, v_rg_gate_b, v_rg_lambda, v_sc_conv_w, v_even_w_out, v_odd_norm_pre, v_odd_norm_post, v_odd_w_in, v_gla_w_gate_lr, v_gla_b_gate, v_gla_norm_g, v_odd_w_out):
    given = dict(locals())
    shard = {n: given[n][0] for n in WEIGHTS}
    m_in = {n: given["m_" + n][0] for n in WEIGHTS}
    v_in = {n: given["v_" + n][0] for n in WEIGHTS}
    mx, my, mc = lax.axis_index("x"), lax.axis_index("y"), lax.axis_index("c")
    core = jnp.reshape(mc, (1,)).astype(jnp.int32)
    chip = jnp.reshape(2 * mx + my, (1,)).astype(jnp.int32)

    small_shard = _pack(shard, SHARDED_SMALL, SHARDED_ROWS)
    big_own = [_halves(shard[n].astype(BF16)) for n in BIG]
    started_a = gather_start(big_own[:1], [small_shard], "gather_start_a")
    started_b = gather_start(big_own[1:], [], "gather_start_b")
    even_w_in_full, small_full = gather_wait(started_a, 1, started_b[-1], "gather_wait_a")
    (even_w_in_full,) = pass_to_sibling([even_w_in_full], "gather_pass_a")
    even_w_in_full = place_own(even_w_in_full, big_own[0], chip, "place_even_w_in")
    small_full = lax.dynamic_update_slice(small_full, small_shard[None], (chip[0], 0, 0))
    full = {n: shard[n] for n, _ in REPLICATED + LAST_REPLICATED}
    full.update({n: _merge_owners(a) for n, a in _unpack(small_full, SHARDED_SMALL, lead=(4,)).items()})
    full["even_w_in"] = even_w_in_full.reshape(4, D_MODEL, EVEN_IN // 4)

    def late_weights(after):
        lands = pass_to_sibling(list(gather_wait(started_b, 3, after, "gather_wait_b")), "gather_pass_b")
        lands = [place_own(a, b, chip, "place_" + n) for a, b, n in zip(lands, big_own[1:], BIG[1:])]
        odd_w_in = jnp.transpose(lands[1].reshape(4, D_MODEL, ODD_IN // 4), (1, 0, 2)).reshape(D_MODEL, ODD_IN)
        return _prepare_weights({"even_w_out": lands[0].reshape(2 * D_MODEL, D_MODEL), "odd_w_in": odd_w_in,
                                 "odd_w_out": lands[2].reshape(D_MODEL, D_MODEL)})

    pending = {}

    def slab(a):
        return a.reshape((4, 2, a.shape[1] // 2) + a.shape[2:])

    def begin(tag, slabs, to_send, dtypes):
        got = exchange_with_sibling(to_send, "grad_sibling_" + tag)
        sums = [add_sibling(a, b, core, dt, "grad_add_sibling_%s%d" % (tag, i))
                for i, (a, b, dt) in enumerate(zip(slabs, got, dtypes))]
        pending[tag] = exchange_with_chips_start(sums, "grad_chips_start_" + tag)
        return pending[tag][-1][0, 0]

    def finish(tag, after):
        sums, got = exchange_with_chips_wait(pending[tag], after, "grad_chips_wait_" + tag)
        return [add_chips(a, b, chip, "grad_add_chips_%s%d" % (tag, i)) for i, (a, b) in enumerate(zip(sums, got))]

    def reduce_first(g, g16):
        odd_w_in = slab(jnp.transpose(g["odd_w_in"].reshape(D_MODEL, 4, ODD_IN // 4), (1, 0, 2)))
        slabs = [odd_w_in] + [slab(g[n].reshape(4, -1, D_MODEL)) for n in ("odd_w_out", "even_w_out")]
        to_send = [odd_w_in.astype(BF16)] + [slab(g16[n].reshape(4, -1, D_MODEL)) for n in ("odd_w_out", "even_w_out")]
        return begin("a", slabs, to_send, [BF16] * 3)

    def reduce_second(g, g16):
        pending["totals_a"] = finish("a", g["even_w_in"])
        rep_rows = _pack(g, REPLICATED, REPLICATED_ROWS).reshape(4, 2, REP_PART, LANES)
        sh_rows = _pack({n: _split_owners(g[n]) for n, _ in SHARDED_SMALL}, SHARDED_SMALL, SHARDED_ROWS, lead=(4,))
        pack = jnp.concatenate([sh_rows.reshape(4, 2, HALF_SHARDED, LANES), rep_rows], axis=2)
        return begin("b", [slab(g["even_w_in"]), pack], [slab(g16["even_w_in"]), pack], [BF16, F32])

    loss, grad_x, g = local_step(x[0], loss_target[0], _prepare_weights(full), reduce_first, reduce_second,
                                 late_weights)
    odd_w_in_t, odd_w_out_t, even_w_out_t = pending["totals_a"]
    even_w_in_t, pack_t = finish("b", grad_x)
    totals = [even_w_in_t, even_w_out_t, odd_w_in_t, odd_w_out_t]
    last_part = jnp.concatenate([_pack(g, LAST_REPLICATED, LAST_ROWS), loss])
    from_core, rep_all, last_all = share_totals(totals, pack_t, last_part)
    me = 2 * chip[0] + core[0]
    mine, theirs = pack_t[:HALF_SHARDED], from_core[4][:HALF_SHARDED]
    sh_total = jnp.where(mc == 0, jnp.concatenate([mine, theirs]), jnp.concatenate([theirs, mine]))
    rep_all = lax.dynamic_update_slice(rep_all, pack_t[None, HALF_SHARDED:], (me, 0, 0))
    rep_total = rep_all.reshape(REPLICATED_ROWS, LANES)
    last_total = sum_parts(lax.dynamic_update_slice(last_all, last_part[None], (me, 0, 0)), "grad_sum_last")
    last_total, loss = last_total[:LAST_ROWS], last_total[LAST_ROWS, 0]
    grads = {}

    delta, new_m, new_v = {}, {}, {}
    for i, n in enumerate(BIG):
        if shard[n].shape[1] % LANES:
            outs = adamw_halves(shard[n].T, totals[i].T, from_core[i].T, m_in[n].T, v_in[n].T, core, "adamw_" + n,
                                by_columns=True)
            grads[n], delta[n], new_m[n], new_v[n] = [o.T for o in outs]
        else:
            grads[n], delta[n], new_m[n], new_v[n] = adamw_halves(shard[n], totals[i], from_core[i], m_in[n],
                                                                  v_in[n], core, "adamw_" + n)
    gate = [src["rg_gate_w"].reshape(GATE_ROWS, LANES) for src in (shard, m_in, v_in)]
    grads["rg_gate_w"], delta["rg_gate_w"], new_m["rg_gate_w"], new_v["rg_gate_w"] = adamw(
        gate[0], rep_total, gate[1], gate[2], "adamw_rg_gate_w")
    rest = REPLICATED[1:]
    rest_rows = sum(_seg_rows(shape) for _, shape in rest)
    grads.update(_unpack(sh_total, SHARDED_SMALL))
    grads.update(_unpack(rep_total[GATE_ROWS:GATE_ROWS + rest_rows], rest))
    grads.update(_unpack(last_total, LAST_REPLICATED))
    names = [n for n, _ in SHARDED_SMALL + rest + LAST_REPLICATED]
    rows_of = lambda a, n: a.reshape(-1, given[n].shape[-1])
    outs = adamw_many([rows_of(given[n], n) for n in names], [rows_of(grads[n], n) for n in names],
                      [rows_of(given["m_" + n], n) for n in names], [rows_of(given["v_" + n], n) for n in names],
                      "adamw_small")
    for n, (d, nm, nv) in zip(names, outs):
        delta[n], new_m[n], new_v[n] = d, nm, nv
    result = [loss, grad_x[None]]
    for group in (grads, delta, new_m, new_v):
        result += [group[n].reshape(given[n].shape) for n in WEIGHTS]
    return tuple(result)
```

```python
import functools

import jax
import jax.numpy as jnp
from jax import lax
from jax.experimental import pallas as pl
from jax.experimental.pallas import tpu as pltpu

F32 = jnp.float32
BF16 = jnp.bfloat16
MESH = pl.DeviceIdType.MESH

D_MODEL = 1024
NORM_EPS = 1e-6
RG_HEADS = 8
RG_HEAD_DIM = 128
RG_C = 8.0
EVEN_IN = 6144
ODD_IN = 3104
ODD_IN_PAD = 3200
GLA_HEADS = 4
GLA_DK = 128
GLA_DV = 256
GLA_RANK = 16
GLA_NORMALIZER = 16.0
GLA_CHUNK = 128
LR_COL = 3072

ADAM_LR = 0.001
ADAM_B1 = 0.9
ADAM_B2 = 0.999
ADAM_EPS = 1e-08
ADAM_WD = 0.01
ADAM_STEP = 10

SUBLANES = 8
LANES = 128
VMEM_LIMIT = 56 * 2 ** 20

ROW_TILE = 512
SCAN_TILE = 256
GLA_BLOCK = 1024
MIX_TILE = 128


def _params(*sem):
    return pltpu.CompilerParams(dimension_semantics=sem, vmem_limit_bytes=VMEM_LIMIT)


def _full(shape):
    n = len(shape)
    return pl.BlockSpec(shape, lambda *_: (0,) * n)


def _sigmoid(x):
    return 0.5 + 0.5 * jnp.tanh(0.5 * x)


def _softplus(x):
    return jnp.maximum(x, 0.0) + jnp.log(1.0 + jnp.exp(-jnp.abs(x)))


def _dot(a, b):
    return jnp.dot(a, b, preferred_element_type=F32)


def _dot_nt(a, b):
    return lax.dot_general(a, b, (((1,), (1,)), ((), ())), preferred_element_type=F32)


def _dot_tn(a, b):
    return lax.dot_general(a, b, (((0,), (0,)), ((), ())), preferred_element_type=F32)


def _bdot(a, b, ca, cb):
    return lax.dot_general(a, b, (((ca,), (cb,)), ((0,), (0,))), preferred_element_type=F32)


def _halo_specs(rows, cols, col_block, n_row_tiles, tix):
    per = rows // SUBLANES
    last = n_row_tiles * per - 1

    def split(args):
        if len(args) == 2:
            return tix(args[1]), col_block + args[0]
        return tix(args[0]), col_block

    def prev(*args):
        t, c = split(args)
        return (jnp.maximum(t * per - 1, 0), c)

    def main(*args):
        return split(args)

    def nxt(*args):
        t, c = split(args)
        return (jnp.minimum((t + 1) * per, last), c)

    return [pl.BlockSpec((SUBLANES, cols), prev), pl.BlockSpec((rows, cols), main),
            pl.BlockSpec((SUBLANES, cols), nxt)]


def _extend(prev_ref, main_ref, next_ref, is_first, is_last):
    p = jnp.where(is_first, 0.0, prev_ref[...])
    n = jnp.where(is_last, 0.0, next_ref[...])
    return jnp.concatenate([p, main_ref[...], n], axis=0)


def _shifted(ext, offset, rows):
    if offset == 0:
        return ext[SUBLANES:SUBLANES + rows]
    n = ext.shape[0]
    return pltpu.roll(ext, (-offset) % n, 0)[SUBLANES:SUBLANES + rows]


def _conv(ext, w, left, rows):
    out = None
    for k in range(w.shape[0]):
        term = _shifted(ext, k - left, rows) * w[k:k + 1]
        out = term if out is None else out + term
    return out


def _conv_transpose(ext, w, left, rows):
    out = None
    for k in range(w.shape[0]):
        term = _shifted(ext, left - k, rows) * w[k:k + 1]
        out = term if out is None else out + term
    return out


def _colsum(x):
    return jnp.sum(x, axis=0, keepdims=True)


def _accumulate(ref, value, step):
    @pl.when(step == 0)
    def _():
        ref[...] = value

    @pl.when(step > 0)
    def _():
        ref[...] += value


PROJ_TILE_BYTES = 7 * 2 ** 20


def _proj_row_tile(rows, width):
    tm = min(ROW_TILE, rows)
    while tm * width * 4 > PROJ_TILE_BYTES and tm % (2 * SUBLANES) == 0:
        tm //= 2
    return tm


def norm_matmul(x, gain, w, name):
    rows, d = x.shape
    n_col_tiles, _, tn = w.shape
    tm = _proj_row_tile(rows, n_col_tiles * tn)

    def body(x_ref, g_ref, w_ref, proj_ref, h_ref):
        xv = x_ref[...]
        rstd = lax.rsqrt(jnp.mean(xv * xv, axis=-1, keepdims=True) + NORM_EPS)
        hv = (xv * rstd * g_ref[...]).astype(BF16)
        h_ref[...] = hv
        for j in range(n_col_tiles):
            proj_ref[:, j * tn:(j + 1) * tn] = _dot(hv, w_ref[j])

    row = lambda cols: pl.BlockSpec((tm, cols), lambda i: (i, 0))
    return pl.pallas_call(
        body, name=name,
        out_shape=(jax.ShapeDtypeStruct((rows, n_col_tiles * tn), F32), jax.ShapeDtypeStruct((rows, d), BF16)),
        grid=(rows // tm,),
        in_specs=[row(d), _full((1, d)), _full(w.shape)],
        out_specs=(row(n_col_tiles * tn), row(d)),
        compiler_params=_params("parallel"),
    )(x, gain, w)


def inproj_bwd(dproj, w, x, gain, dres, name):
    rows, d = x.shape
    n_col_tiles, _, tn = w.shape
    tm = _proj_row_tile(rows, n_col_tiles * tn)

    def body(dp_ref, w_ref, x_ref, g_ref, dres_ref, dx_ref, dg_ref):
        dh = None
        for j in range(n_col_tiles):
            part = _dot_nt(dp_ref[:, j * tn:(j + 1) * tn], w_ref[j])
            dh = part if dh is None else dh + part
        _inproj_finish(dh, x_ref, g_ref, dres_ref, dx_ref, dg_ref, pl.program_id(0))

    row = lambda cols: pl.BlockSpec((tm, cols), lambda i: (i, 0))
    return pl.pallas_call(
        body, name=name,
        out_shape=(jax.ShapeDtypeStruct((rows, d), F32), jax.ShapeDtypeStruct((1, d), F32)),
        grid=(rows // tm,),
        in_specs=[row(n_col_tiles * tn), _full(w.shape), row(d), _full((1, d)), row(d)],
        out_specs=(row(d), _full((1, d))),
        compiler_params=_params("arbitrary"),
    )(dproj, w, x, gain, dres)


def _inproj_finish(dh, x_ref, g_ref, dres_ref, dx_ref, dg_ref, step):
    xv = x_ref[...]
    rstd = lax.rsqrt(jnp.mean(xv * xv, axis=-1, keepdims=True) + NORM_EPS)
    xhat = xv * rstd
    dxn = dh * g_ref[...]
    dx_ref[...] = dres_ref[...] + rstd * (dxn - xhat * jnp.mean(dxn * xhat, axis=-1, keepdims=True))
    _accumulate(dg_ref, _colsum(dh * xhat), step)


def inproj_bwd_pieces(pieces, w, x, gain, dres, name):
    rows, d = x.shape
    tm = min(ROW_TILE, rows)
    n = len(pieces)
    widths = [p.shape[1] for p in pieces]
    starts = [sum(widths[:k]) for k in range(n)]
    assert sum(widths) == w.shape[2]

    def body(*refs):
        w_ref, x_ref, g_ref, dres_ref, dx_ref, dg_ref = refs[n:]
        dh = None
        for k in range(n):
            part = _dot_nt(refs[k][...], w_ref[0, :, starts[k]:starts[k] + widths[k]])
            dh = part if dh is None else dh + part
        _inproj_finish(dh, x_ref, g_ref, dres_ref, dx_ref, dg_ref, pl.program_id(0))

    row = lambda cols: pl.BlockSpec((tm, cols), lambda i: (i, 0))
    return pl.pallas_call(
        body, name=name,
        out_shape=(jax.ShapeDtypeStruct((rows, d), F32), jax.ShapeDtypeStruct((1, d), F32)),
        grid=(rows // tm,),
        in_specs=[row(wd) for wd in widths] + [_full(w.shape), row(d), _full((1, d)), row(d)],
        out_specs=(row(d), _full((1, d))),
        compiler_params=_params("arbitrary"),
    )(*pieces, w, x, gain, dres)


def matmul_dw_pieces(a, pieces, name):
    rows, m = a.shape
    tk = min(2 * ROW_TILE, rows)
    n = len(pieces)

    def body(*refs):
        a_ref, ins, outs = refs[0], refs[1:1 + n], refs[1 + n:]
        av = a_ref[...]
        for k in range(n):
            _accumulate(outs[k], _dot_tn(av, ins[k][...]), pl.program_id(0))

    return pl.pallas_call(
        body, name=name,
        out_shape=[jax.ShapeDtypeStruct((m, p.shape[1]), F32) for p in pieces],
        grid=(rows // tk,),
        in_specs=[pl.BlockSpec((tk, m), lambda k: (k, 0))]
        + [pl.BlockSpec((tk, p.shape[1]), lambda k: (k, 0)) for p in pieces],
        out_specs=[_full((m, p.shape[1])) for p in pieces],
        compiler_params=_params("arbitrary"),
    )(a, *pieces)


def matmul_dw(a, b, bn, name):
    rows, m = a.shape
    n = b.shape[1]
    tk = min((4 if n > bn else 2) * ROW_TILE, rows)
    steps = rows // tk

    def body(a_ref, b_ref, o_ref, o16_ref):
        part = _dot_tn(a_ref[...], b_ref[...])

        @pl.when(pl.program_id(1) == 0)
        def _():
            o_ref[0] = part

        @pl.when(pl.program_id(1) > 0)
        def _():
            o_ref[0] += part

        @pl.when(pl.program_id(1) == steps - 1)
        def _():
            o16_ref[0] = o_ref[0].astype(BF16)

    out = pl.BlockSpec((1, m, bn), lambda j, k: (j, 0, 0))
    return pl.pallas_call(
        body, name=name,
        out_shape=(jax.ShapeDtypeStruct((n // bn, m, bn), F32), jax.ShapeDtypeStruct((n // bn, m, bn), BF16)),
        grid=(n // bn, steps),
        in_specs=[pl.BlockSpec((tk, m), lambda j, k: (k, 0)), pl.BlockSpec((tk, bn), lambda j, k: (k, j))],
        out_specs=(out, out),
        compiler_params=_params("parallel", "arbitrary"),
    )(a, b)


def _scan(a, b, carry, reverse):
    n, c = a.shape
    blocks = n // SUBLANES
    a = a.reshape(blocks, SUBLANES, c)
    b = b.reshape(blocks, SUBLANES, c)
    pos = lax.broadcasted_iota(jnp.int32, (1, SUBLANES, c), 1)
    s = 1
    while s < SUBLANES:
        shift, valid = (SUBLANES - s, pos < SUBLANES - s) if reverse else (s, pos >= s)
        a_s, b_s = pltpu.roll(a, shift, 1), pltpu.roll(b, shift, 1)
        b = jnp.where(valid, a * b_s + b, b)
        a = jnp.where(valid, a * a_s, a)
        s *= 2
    out = [None] * blocks
    for k in (range(blocks - 1, -1, -1) if reverse else range(blocks)):
        h = a[k] * carry + b[k]
        out[k] = h
        carry = h[0:1] if reverse else h[SUBLANES - 1:SUBLANES]
    return jnp.concatenate(out, axis=0)


def _rg_gates(ua, gw_ref, gb, lam):
    ub = ua.astype(BF16)
    pre_r, pre_i = [], []
    for h in range(RG_HEADS):
        z = _dot(ub[:, h * RG_HEAD_DIM:(h + 1) * RG_HEAD_DIM], gw_ref[h])
        pre_r.append(z[:, :RG_HEAD_DIM])
        pre_i.append(z[:, RG_HEAD_DIM:])
    r = _sigmoid(jnp.concatenate(pre_r, axis=1) + gb[0:1])
    i = _sigmoid(jnp.concatenate(pre_i, axis=1) + gb[1:2])
    sp = _softplus(-lam)
    log_a = -RG_C * r * sp
    a = jnp.exp(log_a)
    mult = jnp.sqrt(1.0 - a * a)
    return r, i, sp, a, mult


def _rg_weight_specs():
    return [_full((4, D_MODEL)), _full((1, D_MODEL)), _full((RG_HEADS, RG_HEAD_DIM, 2 * RG_HEAD_DIM)),
            _full((2, D_MODEL)), _full((1, D_MODEL))]


def rglru_fwd(proj, conv_w, conv_b, gate_w, gate_b, lam, reverse, name):
    rows_total = proj.shape[0]
    rows = min(SCAN_TILE, rows_total)
    n_tiles = rows_total // rows
    tix = (lambda i: n_tiles - 1 - i) if reverse else (lambda i: i)

    def body(xp, xm, xn, cw_ref, cb_ref, gw_ref, gb_ref, lam_ref, h_ref, acts_ref, carry):
        i = pl.program_id(0)
        t = tix(i)
        ext = _extend(xp, xm, xn, t == 0, t == n_tiles - 1)
        ua = _conv(ext, cw_ref[...], 2, rows) + cb_ref[...]
        r, gi, _, a, mult = _rg_gates(ua, gw_ref, gb_ref[...], lam_ref[...])
        for k, saved in enumerate((ua, r, gi, a, mult)):
            acts_ref[k] = saved
        b = mult * (gi * ua)

        @pl.when(i == 0)
        def _():
            carry[...] = jnp.zeros_like(carry)

        h = _scan(a, b, carry[0:1], reverse)
        h_ref[...] = h
        edge = h[0:1] if reverse else h[rows - 1:rows]
        carry[...] = jnp.broadcast_to(edge, carry.shape)

    return pl.pallas_call(
        body, name=name,
        out_shape=(jax.ShapeDtypeStruct((rows_total, D_MODEL), F32),
                   jax.ShapeDtypeStruct((5, rows_total, D_MODEL), F32)),
        grid=(n_tiles,),
        in_specs=_halo_specs(rows, D_MODEL, 0, n_tiles, tix) + _rg_weight_specs(),
        out_specs=(pl.BlockSpec((rows, D_MODEL), lambda i: (tix(i), 0)),
                   pl.BlockSpec((5, rows, D_MODEL), lambda i: (0, tix(i), 0))),
        scratch_shapes=[pltpu.VMEM((SUBLANES, D_MODEL), F32)],
        compiler_params=_params("arbitrary"),
    )(proj, proj, proj, conv_w, conv_b, gate_w, gate_b, lam)


def rglru_bwd(proj, dycat, h_dir, acts, gate_w, lam, add_dua, reverse, name):
    rows_total = proj.shape[0]
    rows = min(SCAN_TILE, rows_total)
    n_tiles = rows_total // rows
    tix = (lambda i: i) if reverse else (lambda i: n_tiles - 1 - i)
    za_block = 1

    def body(acts_ref, za_ref, dya_ref, hp, hm, hn, gw_ref, lam_ref, *rest):
        other = rest[0][...] if add_dua is not None else 0.0
        dua_ref, dgw_ref, dgb_ref, dlam_ref, carry = rest[-5:]
        step = pl.program_id(0)
        t = tix(step)
        first, last = t == 0, t == n_tiles - 1
        ua, r, gi, a, mult = (acts_ref[k] for k in range(5))
        lam_v = lam_ref[...]
        sp = _softplus(-lam_v)
        za = za_ref[...]
        dh = dya_ref[...] * (za * _sigmoid(za))

        @pl.when(step == 0)
        def _():
            carry[...] = jnp.zeros_like(carry)

        old = carry[0:1]
        mu = _scan(a, a * dh, old, not reverse)
        row = lax.broadcasted_iota(jnp.int32, mu.shape, 0)
        if reverse:
            mu_next = jnp.where(row == 0, old, pltpu.roll(mu, 1, 0))
            carry[...] = jnp.broadcast_to(mu[rows - 1:rows], carry.shape)
            h_ext = _extend(hp, hm, hn, first, last)
            h_prev = _shifted(h_ext, 1, rows)
        else:
            mu_next = jnp.where(row == rows - 1, old, pltpu.roll(mu, rows - 1, 0))
            carry[...] = jnp.broadcast_to(mu[0:1], carry.shape)
            h_ext = _extend(hp, hm, hn, first, last)
            h_prev = _shifted(h_ext, -1, rows)
        db = dh + mu_next
        da = db * h_prev
        d_mult = db * (gi * ua)
        di = db * (mult * ua)
        dua = db * (mult * gi)
        dlog_a = da * a - d_mult * (a * a) / mult
        dr = dlog_a * (-RG_C * sp)
        dlam = _colsum(dlog_a * (-RG_C * r)) * (-_sigmoid(-lam_v))
        dpr = dr * (r * (1.0 - r))
        dpi = di * (gi * (1.0 - gi))
        dgb = jnp.concatenate([_colsum(dpr), _colsum(dpi)], axis=0)
        ub = ua.astype(BF16)
        dua_heads, dgw_heads = [], []
        for h in range(RG_HEADS):
            cols = slice(h * RG_HEAD_DIM, (h + 1) * RG_HEAD_DIM)
            dz = jnp.concatenate([dpr[:, cols], dpi[:, cols]], axis=1).astype(BF16)
            dgw_heads.append(_dot_tn(ub[:, cols], dz))
            dua_heads.append(_dot_nt(dz, gw_ref[h]))
        dua_ref[...] = dua + jnp.concatenate(dua_heads, axis=1) + other

        @pl.when(step == 0)
        def _():
            for h in range(RG_HEADS):
                dgw_ref[h] = dgw_heads[h]
            dgb_ref[...] = dgb
            dlam_ref[...] = dlam

        @pl.when(step > 0)
        def _():
            for h in range(RG_HEADS):
                dgw_ref[h] += dgw_heads[h]
            dgb_ref[...] += dgb
            dlam_ref[...] += dlam

    row_spec = lambda col: pl.BlockSpec((rows, D_MODEL), lambda i: (tix(i), col))
    return pl.pallas_call(
        body, name=name,
        out_shape=(jax.ShapeDtypeStruct((rows_total, D_MODEL), F32),
                   jax.ShapeDtypeStruct((RG_HEADS, RG_HEAD_DIM, 2 * RG_HEAD_DIM), F32),
                   jax.ShapeDtypeStruct((2, D_MODEL), F32), jax.ShapeDtypeStruct((1, D_MODEL), F32)),
        grid=(n_tiles,),
        in_specs=([pl.BlockSpec((5, rows, D_MODEL), lambda i: (0, tix(i), 0)), row_spec(za_block), row_spec(0)]
                  + _halo_specs(rows, D_MODEL, 0, n_tiles, tix)
                  + [_full((RG_HEADS, RG_HEAD_DIM, 2 * RG_HEAD_DIM)), _full((1, D_MODEL))]
                  + ([] if add_dua is None else [row_spec(0)])),
        out_specs=(row_spec(0), _full((RG_HEADS, RG_HEAD_DIM, 2 * RG_HEAD_DIM)), _full((2, D_MODEL)),
                   _full((1, D_MODEL))),
        scratch_shapes=[pltpu.VMEM((SUBLANES, D_MODEL), F32)],
        compiler_params=_params("arbitrary"),
    )(acts, proj, dycat, h_dir, h_dir, h_dir, gate_w, lam, *([] if add_dua is None else [add_dua]))


def _extend_cols(refs, block, is_first, is_last):
    cols = slice(block * D_MODEL, (block + 1) * D_MODEL)
    prev_ref, main_ref, next_ref = refs
    p = jnp.where(is_first, 0.0, prev_ref[:, cols])
    n = jnp.where(is_last, 0.0, next_ref[:, cols])
    return jnp.concatenate([p, main_ref[:, cols], n], axis=0)


def even_mix_fwd(proj, h_f, h_b, sc_w, name):
    rows_total, width = proj.shape
    rows = min(2 * MIX_TILE, rows_total)
    n_tiles = rows_total // rows
    ident = lambda i: i

    def body(pp, pm, pn, hf_ref, hb_ref, w_ref, y_ref):
        t = pl.program_id(0)
        first, last = t == 0, t == n_tiles - 1
        col = lambda k: pm[:, k * D_MODEL:(k + 1) * D_MODEL]
        za = col(1)
        y_ref[:, 0:D_MODEL] = ((hf_ref[...] + hb_ref[...]) * (za * _sigmoid(za))).astype(BF16)
        p_ext = _extend_cols((pp, pm, pn), 2, first, last) * _extend_cols((pp, pm, pn), 4, first, last)
        cv = _conv(p_ext, w_ref[...], 1, rows)
        zb = col(5)
        y_ref[:, D_MODEL:2 * D_MODEL] = (col(3) * cv * (zb * _sigmoid(zb))).astype(BF16)

    own = pl.BlockSpec((rows, D_MODEL), lambda i: (i, 0))
    return pl.pallas_call(
        body, name=name,
        out_shape=jax.ShapeDtypeStruct((rows_total, 2 * D_MODEL), BF16),
        grid=(n_tiles,),
        in_specs=_halo_specs(rows, width, 0, n_tiles, ident) + [own, own, _full((3, D_MODEL))],
        out_specs=pl.BlockSpec((rows, 2 * D_MODEL), lambda i: (i, 0)),
        compiler_params=_params("parallel"),
    )(proj, proj, proj, h_f, h_b, sc_w)


def even_mix_bwd(proj, dycat, h_f, h_b, dua, conv_w, sc_w, name):
    rows_total, width = proj.shape
    rows = min(MIX_TILE, rows_total)
    n_tiles = rows_total // rows
    ident = lambda i: i

    def body(pp, pm, pn, dyp, dym, dyn, hf_ref, hb_ref, dup, dum, dun, cw_ref, sw_ref,
             dp_ref, dcw_ref, dcb_ref, dsw_ref):
        def put(k, value):
            dp_ref[:, k * D_MODEL:(k + 1) * D_MODEL] = value.astype(BF16)

        t = pl.program_id(0)
        first, last = t == 0, t == n_tiles - 1
        proj_ext = lambda k: _extend_cols((pp, pm, pn), k, first, last)
        mid = slice(SUBLANES, SUBLANES + rows)
        za = pm[:, D_MODEL:2 * D_MODEL]
        sa = _sigmoid(za)
        put(1, dym[:, 0:D_MODEL] * (hf_ref[...] + hb_ref[...]) * (sa * (1.0 + za * (1.0 - sa))))
        dua_ext = _extend(dup, dum, dun, first, last)
        cw = cw_ref[...]
        put(0, _conv_transpose(dua_ext, cw, 2, rows))
        dua_mid = dua_ext[mid]
        xa_ext = proj_ext(0)
        dcw = jnp.concatenate([_colsum(dua_mid * _shifted(xa_ext, k - 2, rows)) for k in range(4)], axis=0)
        dcb = _colsum(dua_mid)
        xb_ext, gb_ext, gc_ext, zb_ext = proj_ext(2), proj_ext(3), proj_ext(4), proj_ext(5)
        p_ext = xb_ext * gc_ext
        sb_ext = _sigmoid(zb_ext)
        dyb_ext = _extend_cols((dyp, dym, dyn), 1, first, last)
        dcv_ext = dyb_ext * gb_ext * (zb_ext * sb_ext)
        sw = sw_ref[...]
        p_at = [_shifted(p_ext, k - 1, rows) for k in range(3)]
        cv = (p_at[0] * sw[0:1] + p_at[1] * sw[1:2]) + p_at[2] * sw[2:3]
        zb, sb, dyb, gb = zb_ext[mid], sb_ext[mid], dyb_ext[mid], gb_ext[mid]
        put(3, dyb * cv * (zb * sb))
        put(5, dyb * gb * cv * (sb * (1.0 + zb * (1.0 - sb))))
        dp = _conv_transpose(dcv_ext, sw, 1, rows)
        put(4, dp * xb_ext[mid])
        put(2, dp * gc_ext[mid])
        dcv = dcv_ext[mid]
        dsw = jnp.concatenate([_colsum(dcv * p_at[k]) for k in range(3)], axis=0)
        _accumulate(dcw_ref, dcw, t)
        _accumulate(dcb_ref, dcb, t)
        _accumulate(dsw_ref, dsw, t)

    own = pl.BlockSpec((rows, D_MODEL), lambda i: (i, 0))
    return pl.pallas_call(
        body, name=name,
        out_shape=(jax.ShapeDtypeStruct((rows_total, 6 * D_MODEL), BF16),
                   jax.ShapeDtypeStruct((4, D_MODEL), F32), jax.ShapeDtypeStruct((1, D_MODEL), F32),
                   jax.ShapeDtypeStruct((3, D_MODEL), F32)),
        grid=(n_tiles,),
        in_specs=(_halo_specs(rows, width, 0, n_tiles, ident) + _halo_specs(rows, 2 * D_MODEL, 0, n_tiles, ident)
                  + [own, own] + _halo_specs(rows, D_MODEL, 0, n_tiles, ident)
                  + [_full((4, D_MODEL)), _full((3, D_MODEL))]),
        out_specs=(pl.BlockSpec((rows, 6 * D_MODEL), lambda i: (i, 0)), _full((4, D_MODEL)), _full((1, D_MODEL)),
                   _full((3, D_MODEL))),
        compiler_params=_params("arbitrary"),
    )(proj, proj, proj, dycat, dycat, dycat, h_f, h_b, dua, dua, dua, conv_w, sc_w)


def even_out_fwd(ycat, w_out, gain, x, name):
    rows, d = x.shape
    k = ycat.shape[1]
    tm = min(ROW_TILE, rows)

    def body(yc_ref, w_ref, g_ref, x_ref, x1_ref, y_ref):
        y = _dot(yc_ref[...], w_ref[...])
        y_ref[...] = y
        rstd = lax.rsqrt(jnp.mean(y * y, axis=-1, keepdims=True) + NORM_EPS)
        x1_ref[...] = x_ref[...] + y * rstd * g_ref[...]

    row = lambda n: pl.BlockSpec((tm, n), lambda i: (i, 0))
    return pl.pallas_call(
        body, name=name,
        out_shape=(jax.ShapeDtypeStruct((rows, d), F32),) * 2,
        grid=(rows // tm,),
        in_specs=[row(k), _full((k, d)), _full((1, d)), row(d)],
        out_specs=(row(d), row(d)),
        compiler_params=_params("parallel"),
    )(ycat, w_out, gain, x)


def _rmsnorm_bwd(dout, y, gain):
    rstd = lax.rsqrt(jnp.mean(y * y, axis=-1, keepdims=True) + NORM_EPS)
    yhat = y * rstd
    dyn = dout * gain
    dy = rstd * (dyn - yhat * jnp.mean(dyn * yhat, axis=-1, keepdims=True))
    return dy, dout * yhat


def even_out_bwd(dx1, y, gain, w_out, name):
    rows, d = y.shape
    k = w_out.shape[0]
    tm = min(ROW_TILE, rows)

    def body(dx_ref, y_ref, g_ref, w_ref, dy_ref, dyc_ref, dg_ref):
        dy, dg_rows = _rmsnorm_bwd(dx_ref[...], y_ref[...], g_ref[...])
        dyb = dy.astype(BF16)
        dy_ref[...] = dyb
        dyc_ref[...] = _dot_nt(dyb, w_ref[...])
        _accumulate(dg_ref, _colsum(dg_rows), pl.program_id(0))

    row = lambda n: pl.BlockSpec((tm, n), lambda i: (i, 0))
    return pl.pallas_call(
        body, name=name,
        out_shape=(jax.ShapeDtypeStruct((rows, d), BF16), jax.ShapeDtypeStruct((rows, k), F32),
                   jax.ShapeDtypeStruct((1, d), F32)),
        grid=(rows // tm,),
        in_specs=[row(d), row(d), _full((1, d)), _full((k, d))],
        out_specs=(row(d), row(k), _full((1, d))),
        compiler_params=_params("arbitrary"),
    )(dx1, y, gain, w_out)


def _chunk_cumsum(g, reverse):
    n, c = g.shape
    chunks, per = n // GLA_CHUNK, GLA_CHUNK // SUBLANES
    g = g.reshape(n // SUBLANES, SUBLANES, c)
    pos = lax.broadcasted_iota(jnp.int32, (1, SUBLANES, c), 1)
    s = 1
    while s < SUBLANES:
        if reverse:
            g = g + jnp.where(pos < SUBLANES - s, pltpu.roll(g, SUBLANES - s, 1), 0.0)
        else:
            g = g + jnp.where(pos >= s, pltpu.roll(g, s, 1), 0.0)
        s *= 2
    g = g.reshape(chunks, per, SUBLANES, c)
    out, carry = [None] * per, None
    for k in (range(per - 1, -1, -1) if reverse else range(per)):
        out[k] = g[:, k] if carry is None else g[:, k] + carry
        carry = out[k][:, 0:1] if reverse else out[k][:, SUBLANES - 1:SUBLANES]
    return jnp.stack(out, axis=1).reshape(n, c)


def _gla_prepare(q_ref, k_ref, lr_ref, wg_ref, bg_ref, reverse, n_chunks):
    z = _dot(lr_ref[...].astype(BF16), wg_ref[0]) + bg_ref[0]
    g = -_softplus(-z) * (1.0 / GLA_NORMALIZER)
    bcum = _chunk_cumsum(g, reverse).reshape(n_chunks, GLA_CHUNK, GLA_DK)
    edge = 0 if reverse else GLA_CHUNK - 1
    btot = bcum[:, edge:edge + 1, :]
    e_pos = jnp.exp(bcum)
    e_neg = jnp.exp(-bcum)
    e_st = jnp.exp(btot - bcum)
    q3 = q_ref[...].reshape(n_chunks, GLA_CHUNK, GLA_DK)
    k3 = k_ref[...].reshape(n_chunks, GLA_CHUNK, GLA_DK)
    scale = GLA_DK ** -0.5
    q_in = q3 * scale * e_pos
    k_in = k3 * e_neg
    k_st = k3 * e_st
    dec = jnp.exp(btot)
    return z, q_in, k_in, k_st, dec, (scale * e_pos, e_neg, e_st)


def _gla_mask(reverse):
    i = lax.broadcasted_iota(jnp.int32, (GLA_CHUNK, GLA_CHUNK), 0)
    j = lax.broadcasted_iota(jnp.int32, (GLA_CHUNK, GLA_CHUNK), 1)
    return (j >= i) if reverse else (j <= i)


def _gla_specs(rows, n_blocks, reverse):
    tix = (lambda s: n_blocks - 1 - s) if reverse else (lambda s: s)
    d = 1 if reverse else 0
    lr_block = LR_COL // LANES
    specs = [pl.BlockSpec((rows, GLA_DK), lambda h, s: (tix(s), h)),
             pl.BlockSpec((rows, GLA_DK), lambda h, s: (tix(s), GLA_HEADS + h)),
             pl.BlockSpec((rows, GLA_DV), lambda h, s: (tix(s), GLA_HEADS + h)),
             pl.BlockSpec((rows, LANES), lambda h, s: (tix(s), lr_block)),
             pl.BlockSpec((1, LANES, GLA_DK), lambda h, s: (d, 0, h)),
             pl.BlockSpec((1, 1, GLA_DK), lambda h, s: (d, 0, h))]
    return specs, tix


def gla_fwd(proj, wg_pad, bg, reverse, name):
    rows_total = proj.shape[0]
    rows = min(GLA_BLOCK, rows_total)
    n_blocks = rows_total // rows
    n_chunks = rows // GLA_CHUNK
    specs, tix = _gla_specs(rows, n_blocks, reverse)

    def body(q_ref, k_ref, v_ref, lr_ref, wg_ref, bg_ref, o_ref, st_ref, state, kv_scr, dec_scr):
        _, q_in, k_in, k_st, dec, _ = _gla_prepare(q_ref, k_ref, lr_ref, wg_ref, bg_ref, reverse, n_chunks)
        vb = v_ref[...].reshape(n_chunks, GLA_CHUNK, GLA_DV).astype(BF16)
        qb = q_in.astype(BF16)
        p = jnp.where(_gla_mask(reverse), _bdot(qb, k_in.astype(BF16), 2, 2), 0.0)
        o = _bdot(p.astype(BF16), vb, 2, 1)
        kv_scr[...] = _bdot(vb, k_st.astype(BF16), 1, 1)
        dec_scr[...] = jnp.broadcast_to(dec, dec_scr.shape)

        @pl.when(pl.program_id(1) == 0)
        def _():
            state[...] = jnp.zeros_like(state)

        for c in range(n_chunks):
            cc = n_chunks - 1 - c if reverse else c
            st_ref[0, cc] = state[...]
            state[...] = state[...] * dec_scr[cc, 0:1] + kv_scr[cc]
        o = o + _bdot(qb, st_ref[0].astype(BF16), 2, 2)
        o_ref[...] = o.reshape(rows, GLA_DV)

    return pl.pallas_call(
        body, name=name,
        out_shape=(jax.ShapeDtypeStruct((rows_total, GLA_HEADS * GLA_DV), F32),
                   jax.ShapeDtypeStruct((GLA_HEADS, rows_total // GLA_CHUNK, GLA_DV, GLA_DK), F32)),
        grid=(GLA_HEADS, n_blocks),
        in_specs=specs,
        out_specs=(pl.BlockSpec((rows, GLA_DV), lambda h, s: (tix(s), h)),
                   pl.BlockSpec((1, n_chunks, GLA_DV, GLA_DK), lambda h, s: (h, tix(s), 0, 0))),
        scratch_shapes=[pltpu.VMEM((GLA_DV, GLA_DK), F32), pltpu.VMEM((n_chunks, GLA_DV, GLA_DK), F32),
                        pltpu.VMEM((n_chunks, SUBLANES, GLA_DK), F32)],
        compiler_params=_params("parallel", "arbitrary"),
    )(proj, proj, proj, proj, wg_pad, bg)


def gla_bwd(proj, wg_pad, bg, d_o, states, dqkv_in, reverse, name):
    rows_total = proj.shape[0]
    rows = min(GLA_BLOCK, rows_total)
    n_blocks = rows_total // rows
    n_chunks = rows // GLA_CHUNK
    specs, tix = _gla_specs(rows, n_blocks, not reverse)
    d = 1 if reverse else 0
    specs[4] = pl.BlockSpec((1, LANES, GLA_DK), lambda h, s: (d, 0, h))
    specs[5] = pl.BlockSpec((1, 1, GLA_DK), lambda h, s: (d, 0, h))
    add = dqkv_in is not None

    def body(*refs):
        q_ref, k_ref, v_ref, lr_ref, wg_ref, bg_ref, do_ref, st_ref = refs[:8]
        refs = refs[8:]
        if add:
            aq_ref, ak_ref, av_ref = refs[:3]
            refs = refs[3:]
        dq_ref, dk_ref, dv_ref, dz_ref, dstate, g_scr, dec_scr, dsn_scr = refs
        z, q_in, k_in, k_st, dec, (f_q, f_k, f_s) = _gla_prepare(q_ref, k_ref, lr_ref, wg_ref, bg_ref, reverse,
                                                                 n_chunks)
        mask = _gla_mask(reverse)
        vb = v_ref[...].reshape(n_chunks, GLA_CHUNK, GLA_DV).astype(BF16)
        dob = do_ref[...].reshape(n_chunks, GLA_CHUNK, GLA_DV).astype(BF16)
        qb, kb, ksb = q_in.astype(BF16), k_in.astype(BF16), k_st.astype(BF16)
        st = st_ref[0]
        stb = st.astype(BF16)
        pb = jnp.where(mask, _bdot(qb, kb, 2, 2), 0.0).astype(BF16)
        dpb = jnp.where(mask, _bdot(dob, vb, 2, 2), 0.0).astype(BF16)
        d_qin = _bdot(dpb, kb, 2, 1) + _bdot(dob, stb, 2, 1)
        d_kin = _bdot(dpb, qb, 1, 1)
        dv = _bdot(pb, dob, 1, 1)
        g_scr[...] = _bdot(dob, qb, 1, 1)
        dec_scr[...] = jnp.broadcast_to(dec, dec_scr.shape)

        @pl.when(pl.program_id(1) == 0)
        def _():
            dstate[...] = jnp.zeros_like(dstate)

        for c in range(n_chunks):
            cc = c if reverse else n_chunks - 1 - c
            dsn_scr[cc] = dstate[...]
            dstate[...] = dstate[...] * dec_scr[cc, 0:1] + g_scr[cc]
        dsn = dsn_scr[...]
        dsnb = dsn.astype(BF16)
        dv = dv + _bdot(ksb, dsnb, 2, 2)
        d_kst = _bdot(vb, dsnb, 2, 1)
        d_dec = jnp.sum(dsn * st, axis=1, keepdims=True)
        ks_term = d_kst * k_st
        d_btot = d_dec * dec + jnp.sum(ks_term, axis=1, keepdims=True)
        d_b = d_qin * q_in - d_kin * k_in - ks_term
        pos = lax.broadcasted_iota(jnp.int32, d_b.shape, 1)
        edge = 0 if reverse else GLA_CHUNK - 1
        d_b = d_b + jnp.where(pos == edge, d_btot, 0.0)
        dg = _chunk_cumsum(d_b.reshape(rows, GLA_DK), not reverse)
        dz_ref[...] = dg * (1.0 / GLA_NORMALIZER) * _sigmoid(-z)
        dq = (d_qin * f_q).reshape(rows, GLA_DK)
        dk = (d_kin * f_k + d_kst * f_s).reshape(rows, GLA_DK)
        dv = dv.reshape(rows, GLA_DV)
        if add:
            dq_ref[...] = (dq + aq_ref[...]).astype(BF16)
            dk_ref[...] = (dk + ak_ref[...]).astype(BF16)
            dv_ref[...] = (dv + av_ref[...]).astype(BF16)
        else:
            dq_ref[...] = dq
            dk_ref[...] = dk
            dv_ref[...] = dv

    qkv_specs = [pl.BlockSpec((rows, GLA_DK), lambda h, s: (tix(s), h)),
                 pl.BlockSpec((rows, GLA_DK), lambda h, s: (tix(s), h)),
                 pl.BlockSpec((rows, GLA_DV), lambda h, s: (tix(s), h))]
    in_specs = specs + [pl.BlockSpec((rows, GLA_DV), lambda h, s: (tix(s), h)),
                        pl.BlockSpec((1, n_chunks, GLA_DV, GLA_DK), lambda h, s: (h, tix(s), 0, 0))]
    args = [proj, proj, proj, proj, wg_pad, bg, d_o, states]
    out_dtype = F32
    if add:
        in_specs += qkv_specs
        args += list(dqkv_in)
        out_dtype = BF16
    return pl.pallas_call(
        body, name=name,
        out_shape=(jax.ShapeDtypeStruct((rows_total, GLA_HEADS * GLA_DK), out_dtype),
                   jax.ShapeDtypeStruct((rows_total, GLA_HEADS * GLA_DK), out_dtype),
                   jax.ShapeDtypeStruct((rows_total, GLA_HEADS * GLA_DV), out_dtype),
                   jax.ShapeDtypeStruct((rows_total, GLA_HEADS * GLA_DK), F32)),
        grid=(GLA_HEADS, n_blocks),
        in_specs=in_specs,
        out_specs=(pl.BlockSpec((rows, GLA_DK), lambda h, s: (tix(s), h)),
                   pl.BlockSpec((rows, GLA_DK), lambda h, s: (tix(s), h)),
                   pl.BlockSpec((rows, GLA_DV), lambda h, s: (tix(s), h)),
                   pl.BlockSpec((rows, GLA_DK), lambda h, s: (tix(s), h))),
        scratch_shapes=[pltpu.VMEM((GLA_DV, GLA_DK), F32), pltpu.VMEM((n_chunks, GLA_DV, GLA_DK), F32),
                        pltpu.VMEM((n_chunks, SUBLANES, GLA_DK), F32),
                        pltpu.VMEM((n_chunks, GLA_DV, GLA_DK), F32)],
        compiler_params=_params("parallel", "arbitrary"),
    )(*args)


def gla_gate_bwd(proj, dz_f, dz_b, wg_pad, name):
    rows_total = proj.shape[0]
    tm = min(ROW_TILE, rows_total)
    n_key = GLA_HEADS * GLA_DK

    def body(lr_ref, dzf_ref, dzb_ref, wg_ref, dlr_ref, dwg_ref, dbg_ref):
        step = pl.program_id(0)
        lr_t = jnp.transpose(lr_ref[...])
        dzf, dzb = dzf_ref[...], dzb_ref[...]
        dzf16, dzb16 = dzf.astype(BF16), dzb.astype(BF16)
        dlr_ref[...] = (_dot_nt(dzf16, wg_ref[0]) + _dot_nt(dzb16, wg_ref[1])).astype(BF16)
        dwf = _dot(lr_t[0:GLA_RANK].astype(BF16), dzf16)
        dwb = _dot(lr_t[GLA_RANK:2 * GLA_RANK].astype(BF16), dzb16)
        dbg = jnp.concatenate([_colsum(dzf), _colsum(dzb)], axis=0)

        @pl.when(step == 0)
        def _():
            dwg_ref[0] = dwf
            dwg_ref[1] = dwb
            dbg_ref[...] = dbg

        @pl.when(step > 0)
        def _():
            dwg_ref[0] += dwf
            dwg_ref[1] += dwb
            dbg_ref[...] += dbg

    return pl.pallas_call(
        body, name=name,
        out_shape=(jax.ShapeDtypeStruct((rows_total, LANES), BF16), jax.ShapeDtypeStruct((2, GLA_RANK, n_key), F32),
                   jax.ShapeDtypeStruct((2, n_key), F32)),
        grid=(rows_total // tm,),
        in_specs=[pl.BlockSpec((tm, LANES), lambda i: (i, LR_COL // LANES)),
                  pl.BlockSpec((tm, n_key), lambda i: (i, 0)), pl.BlockSpec((tm, n_key), lambda i: (i, 0)),
                  _full((2, LANES, n_key))],
        out_specs=(pl.BlockSpec((tm, LANES), lambda i: (i, 0)), _full((2, GLA_RANK, n_key)), _full((2, n_key))),
        compiler_params=_params("arbitrary"),
    )(proj, dz_f, dz_b, wg_pad)


def _head_norm(o, gain):
    outs, hats, rstds = [], [], []
    for h in range(GLA_HEADS):
        oh = o[:, h * GLA_DV:(h + 1) * GLA_DV]
        rstd = lax.rsqrt(jnp.mean(oh * oh, axis=-1, keepdims=True) + NORM_EPS)
        hat = oh * rstd
        outs.append(hat * gain)
        hats.append(hat)
        rstds.append(rstd)
    return outs, hats, rstds


def odd_out_fwd(o_f, o_b, proj, head_gain, w_out, gain, x1, target, name):
    rows, d = x1.shape
    tm = min(ROW_TILE, rows)
    r_block = (2 * GLA_HEADS * GLA_DK + GLA_HEADS * GLA_DV) // d

    def body(of_ref, ob_ref, r_ref, hg_ref, w_ref, g_ref, x1_ref, tgt_ref, y2_ref, dy_ref, dx2_ref, loss_ref,
             dg_ref):
        step = pl.program_id(0)
        on, _, _ = _head_norm(of_ref[...] + ob_ref[...], hg_ref[...])
        r = r_ref[...]
        y2 = (jnp.concatenate(on, axis=1) * (r * _sigmoid(r))).astype(BF16)
        y2_ref[...] = y2
        y = _dot(y2, w_ref[...])
        gain_v = g_ref[...]
        rstd = lax.rsqrt(jnp.mean(y * y, axis=-1, keepdims=True) + NORM_EPS)
        x2 = x1_ref[...] + y * rstd * gain_v
        diff = x2 - tgt_ref[...]
        loss = 0.5 * jnp.sum(jnp.mean(diff * diff, axis=-1, keepdims=True), axis=0, keepdims=True)
        dx2 = diff * (1.0 / d)
        dx2_ref[...] = dx2
        dy, dg_rows = _rmsnorm_bwd(dx2, y, gain_v)
        dy_ref[...] = dy.astype(BF16)
        _accumulate(loss_ref, jnp.broadcast_to(loss, loss_ref.shape), step)
        _accumulate(dg_ref, _colsum(dg_rows), step)

    row = lambda n, col=0: pl.BlockSpec((tm, n), lambda i: (i, col))
    return pl.pallas_call(
        body, name=name,
        out_shape=(jax.ShapeDtypeStruct((rows, d), BF16), jax.ShapeDtypeStruct((rows, d), BF16),
                   jax.ShapeDtypeStruct((rows, d), F32), jax.ShapeDtypeStruct((SUBLANES, LANES), F32),
                   jax.ShapeDtypeStruct((1, d), F32)),
        grid=(rows // tm,),
        in_specs=[row(d), row(d), row(d, r_block), _full((1, GLA_DV)), _full((d, d)), _full((1, d)), row(d), row(d)],
        out_specs=(row(d), row(d), row(d), _full((SUBLANES, LANES)), _full((1, d))),
        compiler_params=_params("arbitrary"),
    )(o_f, o_b, proj, head_gain, w_out, gain, x1, target)


def odd_out_bwd(dy, w_out, o_f, o_b, proj, head_gain, name):
    rows, d = dy.shape
    tm = min(ROW_TILE, rows)
    r_block = (2 * GLA_HEADS * GLA_DK + GLA_HEADS * GLA_DV) // d

    def body(dy_ref, w_ref, of_ref, ob_ref, r_ref, hg_ref, dr_ref, do_ref, dhg_ref):
        dy2 = _dot_nt(dy_ref[...], w_ref[...])
        hg = hg_ref[...]
        on, hats, rstds = _head_norm(of_ref[...] + ob_ref[...], hg)
        r = r_ref[...]
        sr = _sigmoid(r)
        dr_ref[...] = (dy2 * jnp.concatenate(on, axis=1) * (sr * (1.0 + r * (1.0 - sr)))).astype(BF16)
        d_on = dy2 * (r * sr)
        d_os, dhg = [], None
        for h in range(GLA_HEADS):
            dn = d_on[:, h * GLA_DV:(h + 1) * GLA_DV]
            part = _colsum(dn * hats[h])
            dhg = part if dhg is None else dhg + part
            dng = dn * hg
            d_os.append(rstds[h] * (dng - hats[h] * jnp.mean(dng * hats[h], axis=-1, keepdims=True)))
        do_ref[...] = jnp.concatenate(d_os, axis=1)
        _accumulate(dhg_ref, dhg, pl.program_id(0))

    row = lambda n, col=0: pl.BlockSpec((tm, n), lambda i: (i, col))
    return pl.pallas_call(
        body, name=name,
        out_shape=(jax.ShapeDtypeStruct((rows, d), BF16), jax.ShapeDtypeStruct((rows, d), F32),
                   jax.ShapeDtypeStruct((1, GLA_DV), F32)),
        grid=(rows // tm,),
        in_specs=[row(d), _full((d, d)), row(d), row(d), row(d, r_block), _full((1, GLA_DV))],
        out_specs=(row(d), row(d), _full((1, GLA_DV))),
        compiler_params=_params("arbitrary"),
    )(dy, w_out, o_f, o_b, proj, head_gain)


def local_step(x, target, w, reduce_first=None, reduce_second=None, late_weights=None):
    g, g16 = {}, {}
    proj_e, h0 = norm_matmul(x, w["even_norm_pre"], w["even_w_in"], "even_in_proj")
    h_dir, acts = zip(*[rglru_fwd(proj_e, w["rg_conv_w"], w["rg_conv_b"], w["rg_gate_w"][d], w["rg_gate_b"][d],
                                  w["rg_lambda"][d], d == 1, "rglru_fwd_%d" % d) for d in range(2)])
    ycat = even_mix_fwd(proj_e, h_dir[0], h_dir[1], w["sc_conv_w"], "even_mix_fwd")
    if late_weights is not None:
        w = dict(w, **late_weights(ycat))
    x1, y_e = even_out_fwd(ycat, w["even_w_out"], w["even_norm_post"], x, "even_out_fwd")
    proj_o, h1 = norm_matmul(x1, w["odd_norm_pre"], w["odd_w_in"], "odd_in_proj")
    o_dir, st_dir = [], []
    for d in range(2):
        o, st = gla_fwd(proj_o, w["gla_wg_pad"], w["gla_b_gate"], d == 1, "gla_fwd_%d" % d)
        o_dir.append(o)
        st_dir.append(st)
    y2, dy_o, dx2, loss, g["odd_norm_post"] = odd_out_fwd(
        o_dir[0], o_dir[1], proj_o, w["gla_norm_g"], w["odd_w_out"], w["odd_norm_post"], x1, target, "odd_out_fwd")
    g["odd_w_out"], g16["odd_w_out"] = (a[0] for a in matmul_dw(y2, dy_o, D_MODEL, "odd_w_out_grad"))
    dr, d_o, g["gla_norm_g"] = odd_out_bwd(dy_o, w["odd_w_out"], o_dir[0], o_dir[1], proj_o, w["gla_norm_g"],
                                           "odd_out_bwd")
    dq, dk, dv, dz_f = gla_bwd(proj_o, w["gla_wg_pad"], w["gla_b_gate"], d_o, st_dir[0], None, False, "gla_bwd_0")
    dq, dk, dv, dz_b = gla_bwd(proj_o, w["gla_wg_pad"], w["gla_b_gate"], d_o, st_dir[1], (dq, dk, dv), True,
                               "gla_bwd_1")
    dlr, g["gla_w_gate_lr"], g["gla_b_gate"] = gla_gate_bwd(proj_o, dz_f, dz_b, w["gla_wg_pad"], "gla_gate_bwd")
    dproj_o = [dq, dk, dv, dr, dlr]
    g["odd_w_in"] = jnp.concatenate(matmul_dw_pieces(h1, dproj_o, "odd_w_in_grad"), axis=1)[:, :ODD_IN]
    dx1, g["odd_norm_pre"] = inproj_bwd_pieces(dproj_o, w["odd_w_in"], x1, w["odd_norm_pre"], dx2, "odd_in_proj_bwd")
    dy_e, dycat, g["even_norm_post"] = even_out_bwd(dx1, y_e, w["even_norm_post"], w["even_w_out"], "even_out_bwd")
    g["even_w_out"], g16["even_w_out"] = (a[0] for a in matmul_dw(ycat, dy_e, D_MODEL, "even_w_out_grad"))
    lam = w["rg_lambda"] if reduce_first is None else w["rg_lambda"] + reduce_first(g, g16)
    dua, dgw, dgb, dlam = None, [], [], []
    for d in range(2):
        a, b, c, e = rglru_bwd(proj_e, dycat, h_dir[d], acts[d], w["rg_gate_w"][d], lam[d], dua, d == 1,
                               "rglru_bwd_%d" % d)
        dua = a
        dgw.append(b)
        dgb.append(c)
        dlam.append(e)
    dproj_e, g["rg_conv_w"], g["rg_conv_b"], g["sc_conv_w"] = even_mix_bwd(
        proj_e, dycat, h_dir[0], h_dir[1], dua, w["rg_conv_w"], w["sc_conv_w"], "even_mix_bwd")
    dgw = jnp.stack(dgw).reshape(2, RG_HEADS, RG_HEAD_DIM, 2, RG_HEAD_DIM)
    g["rg_gate_w"] = jnp.transpose(dgw, (0, 3, 1, 2, 4))
    g["rg_gate_b"] = jnp.stack(dgb).reshape(2, 2, RG_HEADS, RG_HEAD_DIM)
    g["rg_lambda"] = jnp.concatenate(dlam, axis=0)
    g["even_w_in"], g16["even_w_in"] = matmul_dw(h0, dproj_e, EVEN_IN // 4, "even_w_in_grad")
    gain = w["even_norm_pre"] if reduce_second is None else w["even_norm_pre"] + reduce_second(g, g16)
    grad_x, g["even_norm_pre"] = inproj_bwd(dproj_e, w["even_w_in"], x, gain, dx1, "even_in_proj_bwd")
    return loss, grad_x, g


def _prepare_weights(full):
    w = {}
    for name in ("even_norm_pre", "even_norm_post", "rg_conv_b", "odd_norm_pre", "odd_norm_post", "gla_norm_g"):
        if name in full:
            w[name] = full[name].reshape(1, -1)
    for name in ("rg_conv_w", "sc_conv_w"):
        if name in full:
            w[name] = full[name]
    for name in ("even_w_out", "odd_w_out"):
        if name in full:
            w[name] = full[name].astype(BF16)
    if "even_w_in" in full:
        w["even_w_in"] = full["even_w_in"].astype(BF16)
        if w["even_w_in"].ndim == 2:
            w["even_w_in"] = jnp.transpose(w["even_w_in"].reshape(D_MODEL, 4, EVEN_IN // 4), (1, 0, 2))
    if "rg_gate_w" in full:
        gw = jnp.transpose(full["rg_gate_w"].astype(BF16), (0, 2, 3, 1, 4))
        w["rg_gate_w"] = gw.reshape(2, RG_HEADS, RG_HEAD_DIM, 2 * RG_HEAD_DIM)
        w["rg_gate_b"] = full["rg_gate_b"].reshape(2, 2, D_MODEL)
        w["rg_lambda"] = full["rg_lambda"].reshape(2, 1, D_MODEL)
    if "odd_w_in" in full:
        w_in = jnp.pad(full["odd_w_in"].astype(BF16), ((0, 0), (0, ODD_IN_PAD - ODD_IN)))
        w["odd_w_in"] = w_in.reshape(1, D_MODEL, ODD_IN_PAD)
    if "gla_w_gate_lr" in full:
        wg = full["gla_w_gate_lr"].astype(BF16)
        w["gla_wg_pad"] = jnp.stack([jnp.pad(wg[d], ((d * GLA_RANK, LANES - (d + 1) * GLA_RANK), (0, 0)))
                                     for d in range(2)])
        w["gla_b_gate"] = full["gla_b_gate"].reshape(2, 1, GLA_HEADS * GLA_DK)
    return w


SHARDED_SMALL = (("rg_conv_w", (4, 256)), ("rg_lambda", (2, 256)), ("sc_conv_w", (3, 256)),
                 ("odd_norm_pre", (256,)), ("odd_norm_post", (256,)), ("gla_w_gate_lr", (2, 16, 128)),
                 ("gla_b_gate", (2, 128)), ("gla_norm_g", (64,)))
SHARDED_ROWS = 96
REPLICATED = (("rg_gate_w", (2, 2, 8, 128, 128)), ("even_norm_post", (1024,)), ("rg_conv_b", (1024,)),
              ("rg_gate_b", (2, 2, 8, 128)))
GATE_ROWS = 4096
LAST_REPLICATED = (("even_norm_pre", (1024,)),)
LAST_ROWS = 8
REPLICATED_ROWS = 4160
REP_PART = REPLICATED_ROWS // 8
HALF_SHARDED = SHARDED_ROWS // 2
PACK_HALF = HALF_SHARDED + REP_PART


def _seg_rows(shape):
    n = 1
    for s in shape:
        n *= s
    return -(-n // (SUBLANES * LANES)) * SUBLANES


def _pack(arrays, spec, total_rows, lead=()):
    parts = []
    for name, shape in spec:
        flat = arrays[name].reshape(lead + (-1,))
        pad = _seg_rows(shape) * LANES - flat.shape[-1]
        if pad:
            flat = jnp.pad(flat, [(0, 0)] * len(lead) + [(0, pad)])
        parts.append(flat.reshape(lead + (-1, LANES)))
    rows = jnp.concatenate(parts, axis=len(lead))
    pad = total_rows - rows.shape[len(lead)]
    return jnp.pad(rows, [(0, 0)] * len(lead) + [(0, pad), (0, 0)])


def _unpack(rows, spec, lead=()):
    out, at = {}, 0
    for name, shape in spec:
        n = 1
        for s in shape:
            n *= s
        k = _seg_rows(shape)
        seg = lax.slice_in_dim(rows, at, at + k, axis=len(lead)).reshape(lead + (-1,))
        out[name] = lax.slice_in_dim(seg, 0, n, axis=len(lead)).reshape(lead + shape)
        at += k
    return out


def _split_owners(arr):
    a = arr.reshape(arr.shape[:-1] + (4, arr.shape[-1] // 4))
    return jnp.moveaxis(a, -2, 0)


def _merge_owners(arr):
    a = jnp.moveaxis(arr, 0, -2)
    return a.reshape(a.shape[:-2] + (-1,))


HBM_SPEC = pl.BlockSpec(memory_space=pltpu.HBM)


def _position():
    x, y, c = lax.axis_index("x"), lax.axis_index("y"), lax.axis_index("c")
    chips = [(1 - x, y), (x, 1 - y), (1 - x, 1 - y)]
    return x, y, c, chips


def _remote(src, dst, send_sem, recv_sem, device):
    return pltpu.make_async_remote_copy(src_ref=src, dst_ref=dst, send_sem=send_sem, recv_sem=recv_sem,
                                        device_id=device, device_id_type=MESH)


SEM_SPEC = pl.BlockSpec(memory_space=pltpu.SEMAPHORE)
SIDE_EFFECT = pltpu.SideEffectType.DATAFLOW_SIDE_EFFECTING


def _gather_copies(ins, lands, n_h, send_sems, recv_sems):
    x, y, c, chips = _position()
    me = 2 * x + y
    copies = []
    for a in range(len(ins)):
        for k, chip in enumerate(chips):
            src = ins[a].at[c] if a < n_h else ins[a]
            dst = lands[a].at[me, c] if a < n_h else lands[a].at[me]
            copies.append(_remote(src, dst, send_sems.at[3 * a + k], recv_sems.at[3 * a + k], (chip[0], chip[1], c)))
    return copies


def gather_start(halved, whole, name):
    arrays = list(halved) + list(whole)
    n, n_h = len(arrays), len(halved)
    lands = [lax.empty((4,) + a.shape, a.dtype) for a in arrays]

    def body(*refs):
        ins, lz, send_sems, recv_sems, token = refs[:n], refs[n:2 * n], refs[2 * n], refs[2 * n + 1], refs[-1]
        for cp in _gather_copies(ins, lz, n_h, send_sems, recv_sems):
            cp.start()
        token[...] = jnp.zeros_like(token)

    operands = [pltpu.with_memory_space_constraint(a, pltpu.HBM) for a in arrays + lands]
    return pl.pallas_call(
        body, name=name,
        out_shape=(pltpu.SemaphoreType.DMA((3 * n,)), pltpu.SemaphoreType.DMA((3 * n,)))
        + tuple(pltpu.HBM(a.shape, a.dtype) for a in operands) + (jax.ShapeDtypeStruct((SUBLANES, LANES), F32),),
        in_specs=[HBM_SPEC] * (2 * n),
        out_specs=(SEM_SPEC, SEM_SPEC) + (HBM_SPEC,) * (2 * n) + (pl.BlockSpec(memory_space=pltpu.VMEM),),
        input_output_aliases={i: 2 + i for i in range(2 * n)},
        compiler_params=pltpu.CompilerParams(has_side_effects=SIDE_EFFECT),
    )(*operands)


def gather_wait(started, n_h, after, name):
    send_sems, recv_sems = started[0], started[1]
    operands = list(started[2:-1])
    n = len(operands) // 2

    def body(*refs):
        ins, lz, send_ref, recv_ref = refs[:n], refs[n:2 * n], refs[2 * n], refs[2 * n + 1]
        for cp in _gather_copies(ins, lz, n_h, send_ref, recv_ref):
            cp.wait_send()
            cp.wait_recv()

    outs = pl.pallas_call(
        body, name=name,
        out_shape=tuple(pltpu.HBM(a.shape, a.dtype) for a in operands),
        in_specs=[HBM_SPEC] * (2 * n) + [SEM_SPEC, SEM_SPEC, pl.BlockSpec(memory_space=pl.ANY)],
        out_specs=(HBM_SPEC,) * (2 * n),
        input_output_aliases={i: i for i in range(2 * n)},
        compiler_params=pltpu.CompilerParams(has_side_effects=SIDE_EFFECT),
    )(*operands, send_sems, recv_sems, after)
    return outs[n:]


def pass_to_sibling(fulls, name):
    n = len(fulls)

    def body(*refs):
        bufs = refs[n:2 * n]
        send_sems, recv_sems = refs[2 * n:]
        x, y, c, chips = _position()
        sibling = (x, y, 1 - c)
        copies = []
        for a in range(n):
            for k, chip in enumerate(chips):
                q = 2 * chip[0] + chip[1]
                cp = _remote(bufs[a].at[q, c], bufs[a].at[q, c], send_sems.at[3 * a + k], recv_sems.at[3 * a + k],
                             sibling)
                cp.start()
                copies.append(cp)
        for a in range(n):
            for k, chip in enumerate(chips):
                q = 2 * chip[0] + chip[1]
                passed = bufs[a].at[q, 1 - c]
                _remote(passed, passed, send_sems.at[3 * a + k], recv_sems.at[3 * a + k], sibling).wait_recv()
        for cp in copies:
            cp.wait_send()

    return pl.pallas_call(
        body, name=name,
        out_shape=[jax.ShapeDtypeStruct(a.shape, a.dtype) for a in fulls],
        in_specs=[HBM_SPEC] * n, out_specs=[HBM_SPEC] * n,
        input_output_aliases={i: i for i in range(n)},
        scratch_shapes=[pltpu.SemaphoreType.DMA((3 * n,)), pltpu.SemaphoreType.DMA((3 * n,))],
    )(*fulls)


def place_own(full, own, chip, name):
    _, _, r, cols = full.shape
    tr = _row_tile(r, cols)

    def body(p_ref, own_ref, full_ref, o_ref):
        o_ref[0] = own_ref[...]

    return pl.pallas_call(
        body, name=name,
        out_shape=jax.ShapeDtypeStruct(full.shape, full.dtype),
        grid_spec=pltpu.PrefetchScalarGridSpec(
            num_scalar_prefetch=1, grid=(2, r // tr),
            in_specs=[pl.BlockSpec((1, tr, cols), lambda h, i, p_ref: (h, i, 0)), pl.BlockSpec(memory_space=pl.ANY)],
            out_specs=pl.BlockSpec((1, 1, tr, cols), lambda h, i, p_ref: (p_ref[0], h, i, 0))),
        input_output_aliases={2: 0},
        compiler_params=_params("parallel", "parallel"),
    )(chip, own, full)


def exchange_with_sibling(arrays, name):
    n = len(arrays)

    def body(*refs):
        ins, outs = refs[:n], refs[n:2 * n]
        send_sems, recv_sems = refs[2 * n:]
        x, y, c, _ = _position()
        copies = []
        for a in range(n):
            cp = _remote(ins[a].at[:, 1 - c], outs[a], send_sems.at[a], recv_sems.at[a], (x, y, 1 - c))
            cp.start()
            copies.append(cp)
        for cp in copies:
            cp.wait()

    return pl.pallas_call(
        body, name=name,
        out_shape=[jax.ShapeDtypeStruct((a.shape[0],) + a.shape[2:], a.dtype) for a in arrays],
        in_specs=[HBM_SPEC] * n, out_specs=[HBM_SPEC] * n,
        scratch_shapes=[pltpu.SemaphoreType.DMA((n,)), pltpu.SemaphoreType.DMA((n,))],
    )(*arrays)


def _chip_copies(ins, lands, send_sems, recv_sems):
    x, y, c, chips = _position()
    copies = []
    for a in range(len(ins)):
        for k, chip in enumerate(chips):
            q = 2 * chip[0] + chip[1]
            copies.append(_remote(ins[a].at[q], lands[a].at[k], send_sems.at[3 * a + k], recv_sems.at[3 * a + k],
                                  (chip[0], chip[1], c)))
    return copies


def exchange_with_chips_start(arrays, name):
    n = len(arrays)
    lands = [lax.empty((3,) + a.shape[1:], a.dtype) for a in arrays]

    def body(*refs):
        ins, lz, send_sems, recv_sems, token = refs[:n], refs[n:2 * n], refs[2 * n], refs[2 * n + 1], refs[-1]
        for cp in _chip_copies(ins, lz, send_sems, recv_sems):
            cp.start()
        token[...] = jnp.zeros_like(token)

    operands = [pltpu.with_memory_space_constraint(a, pltpu.HBM) for a in list(arrays) + lands]
    return pl.pallas_call(
        body, name=name,
        out_shape=(pltpu.SemaphoreType.DMA((3 * n,)), pltpu.SemaphoreType.DMA((3 * n,)))
        + tuple(pltpu.HBM(a.shape, a.dtype) for a in operands) + (jax.ShapeDtypeStruct((SUBLANES, LANES), F32),),
        in_specs=[HBM_SPEC] * (2 * n),
        out_specs=(SEM_SPEC, SEM_SPEC) + (HBM_SPEC,) * (2 * n) + (pl.BlockSpec(memory_space=pltpu.VMEM),),
        input_output_aliases={i: 2 + i for i in range(2 * n)},
        compiler_params=pltpu.CompilerParams(has_side_effects=SIDE_EFFECT),
    )(*operands)


def exchange_with_chips_wait(started, after, name):
    send_sems, recv_sems = started[0], started[1]
    operands = list(started[2:-1])
    n = len(operands) // 2

    def body(*refs):
        ins, lz, send_ref, recv_ref = refs[:n], refs[n:2 * n], refs[2 * n], refs[2 * n + 1]
        for cp in _chip_copies(ins, lz, send_ref, recv_ref):
            cp.wait_send()
            cp.wait_recv()

    outs = pl.pallas_call(
        body, name=name,
        out_shape=tuple(pltpu.HBM(a.shape, a.dtype) for a in operands),
        in_specs=[HBM_SPEC] * (2 * n) + [SEM_SPEC, SEM_SPEC, pl.BlockSpec(memory_space=pl.ANY)],
        out_specs=(HBM_SPEC,) * (2 * n),
        input_output_aliases={i: i for i in range(2 * n)},
        compiler_params=pltpu.CompilerParams(has_side_effects=SIDE_EFFECT),
    )(*operands, send_sems, recv_sems, after)
    return outs[:n], outs[n:]


def share_totals(totals, pack_total, last_part):
    arrays = list(totals) + [pack_total]
    n = len(arrays)

    def body(*refs):
        ins, last, outs, rep, last_all = refs[:n], refs[n], refs[n + 1:2 * n + 1], refs[2 * n + 1], refs[2 * n + 2]
        send_sems, recv_sems, rep_send, rep_recv, last_send, last_recv = refs[2 * n + 3:]
        x, y, c, chips = _position()
        sibling = (x, y, 1 - c)
        me = 4 * x + 2 * y + c
        sends = []
        for a in range(n):
            cp = _remote(ins[a], outs[a], send_sems.at[a], recv_sems.at[a], sibling)
            cp.start()
            sends.append(cp)
        mine = ins[n - 1].at[pl.ds(HALF_SHARDED, REP_PART)]
        peers = [sibling]
        for chip in chips:
            peers += [(chip[0], chip[1], c), (chip[0], chip[1], 1 - c)]
        for j, peer in enumerate(peers):
            for src, dst, s_sem, r_sem in ((mine, rep, rep_send, rep_recv), (last, last_all, last_send, last_recv)):
                cp = _remote(src, dst.at[me], s_sem.at[j], r_sem.at[j], peer)
                cp.start()
                sends.append(cp)
        for a in range(n):
            _remote(outs[a], outs[a], send_sems.at[a], recv_sems.at[a], sibling).wait_recv()
        for j, peer in enumerate(peers):
            it = 4 * peer[0] + 2 * peer[1] + peer[2]
            _remote(rep.at[it], rep.at[it], rep_send.at[j], rep_recv.at[j], peer).wait_recv()
            _remote(last_all.at[it], last_all.at[it], last_send.at[j], last_recv.at[j], peer).wait_recv()
        for cp in sends:
            cp.wait_send()

    outs = pl.pallas_call(
        body, name="grad_share_totals",
        out_shape=[jax.ShapeDtypeStruct(a.shape, a.dtype) for a in arrays]
        + [jax.ShapeDtypeStruct((8, REP_PART, LANES), F32), jax.ShapeDtypeStruct((8,) + last_part.shape, F32)],
        in_specs=[HBM_SPEC] * (n + 1), out_specs=[HBM_SPEC] * (n + 2),
        scratch_shapes=[pltpu.SemaphoreType.DMA((n,)), pltpu.SemaphoreType.DMA((n,))]
        + [pltpu.SemaphoreType.DMA((7,))] * 4,
    )(*arrays, last_part)
    return outs[:n], outs[n], outs[n + 1]


def sum_parts(parts, name):
    def body(p_ref, o_ref):
        total = p_ref[0]
        for k in range(1, parts.shape[0]):
            total = total + p_ref[k]
        o_ref[...] = total

    return pl.pallas_call(body, name=name, out_shape=jax.ShapeDtypeStruct(parts.shape[1:], parts.dtype))(parts)


TILE_BYTES = 2 << 20


def _row_tile(rows, cols):
    best = None
    for t in range(SUBLANES, rows + 1, SUBLANES):
        if rows % t == 0 and t * cols * 4 <= TILE_BYTES:
            best = t
    return best if best is not None else rows


def add_sibling(mine, received, core, out_dtype, name):
    _, _, r, cols = mine.shape
    tr = _row_tile(r, cols)

    def body(c_ref, a_ref, b_ref, o_ref):
        o_ref[...] = (a_ref[0] + b_ref[...].astype(F32)).astype(out_dtype)

    return pl.pallas_call(
        body, name=name,
        out_shape=jax.ShapeDtypeStruct((4, r, cols), out_dtype),
        grid_spec=pltpu.PrefetchScalarGridSpec(
            num_scalar_prefetch=1, grid=(4, r // tr),
            in_specs=[pl.BlockSpec((1, 1, tr, cols), lambda o, i, c_ref: (o, c_ref[0], i, 0)),
                      pl.BlockSpec((1, tr, cols), lambda o, i, c_ref: (o, i, 0))],
            out_specs=pl.BlockSpec((1, tr, cols), lambda o, i, c_ref: (o, i, 0))),
        compiler_params=_params("parallel", "parallel"),
    )(core, mine, received)


def add_chips(own, received, chip, name):
    _, r, cols = own.shape
    tr = _row_tile(r, cols)

    def body(p_ref, a_ref, b0, b1, b2, o_ref):
        o_ref[...] = ((a_ref[0].astype(F32) + b0[0].astype(F32)) + b1[0].astype(F32)) + b2[0].astype(F32)

    rb = lambda k: pl.BlockSpec((1, tr, cols), lambda i, p_ref: (k, i, 0))
    return pl.pallas_call(
        body, name=name,
        out_shape=jax.ShapeDtypeStruct((r, cols), F32),
        grid_spec=pltpu.PrefetchScalarGridSpec(
            num_scalar_prefetch=1, grid=(r // tr,),
            in_specs=[pl.BlockSpec((1, tr, cols), lambda i, p_ref: (p_ref[0], i, 0)), rb(0), rb(1), rb(2)],
            out_specs=pl.BlockSpec((tr, cols), lambda i, p_ref: (i, 0))),
        compiler_params=_params("parallel"),
    )(chip, own, received, received, received)


def _adamw_update(gv, w_ref, m_ref, v_ref, d_ref, nm_ref, nv_ref):
    nm = ADAM_B1 * m_ref[...] + (1.0 - ADAM_B1) * gv
    nv = ADAM_B2 * v_ref[...] + (1.0 - ADAM_B2) * (gv * gv)
    nm_ref[...] = nm
    nv_ref[...] = nv
    m_hat = nm / (1.0 - ADAM_B1 ** ADAM_STEP)
    v_hat = nv / (1.0 - ADAM_B2 ** ADAM_STEP)
    d_ref[...] = -ADAM_LR * (m_hat / (jnp.sqrt(v_hat) + ADAM_EPS) + ADAM_WD * w_ref[...])


def adamw_halves(w, own, received, m, v, core, name, by_columns=False):
    rows, cols = w.shape

    def body(c_ref, w_ref, own_ref, rec_ref, m_ref, v_ref, g_ref, d_ref, nm_ref, nv_ref):
        gv = jnp.where(pl.program_id(0) == c_ref[0], own_ref[...], rec_ref[...])
        g_ref[...] = gv
        _adamw_update(gv, w_ref, m_ref, v_ref, d_ref, nm_ref, nv_ref)

    if by_columns:
        nr = 1
        whole = pl.BlockSpec((rows, cols // 2), lambda h, i, c_ref: (0, h))
        half = pl.BlockSpec((rows, cols // 2), lambda h, i, c_ref: (0, 0))
    else:
        r = rows // 2
        tr = _row_tile(r, cols)
        nr = r // tr
        whole = pl.BlockSpec((tr, cols), lambda h, i, c_ref: (h * nr + i, 0))
        half = pl.BlockSpec((tr, cols), lambda h, i, c_ref: (i, 0))
    return pl.pallas_call(
        body, name=name,
        out_shape=(jax.ShapeDtypeStruct((rows, cols), F32),) * 4,
        grid_spec=pltpu.PrefetchScalarGridSpec(
            num_scalar_prefetch=1, grid=(2, nr),
            in_specs=[whole, half, half, whole, whole], out_specs=(whole,) * 4),
        compiler_params=_params("parallel", "parallel"),
    )(core, w, own, received, m, v)


def adamw_many(ws, gs, ms, vs, name):
    n = len(ws)

    def body(*refs):
        ins, outs = refs[:4 * n], refs[4 * n:]
        for k in range(n):
            w_ref, g_ref, m_ref, v_ref = (ins[j * n + k] for j in range(4))
            d_ref, nm_ref, nv_ref = outs[3 * k:3 * k + 3]
            _adamw_update(g_ref[...], w_ref, m_ref, v_ref, d_ref, nm_ref, nv_ref)

    flat = pl.pallas_call(
        body, name=name,
        out_shape=[jax.ShapeDtypeStruct(w.shape, F32) for w in ws for _ in range(3)],
    )(*ws, *gs, *ms, *vs)
    return [tuple(flat[3 * k:3 * k + 3]) for k in range(n)]


def adamw(w, g, m, v, name):
    r, cols = w.shape
    tr = _row_tile(r, cols)

    def body(w_ref, g_ref, m_ref, v_ref, g_out, d_ref, nm_ref, nv_ref):
        gv = g_ref[...]
        g_out[...] = gv
        _adamw_update(gv, w_ref, m_ref, v_ref, d_ref, nm_ref, nv_ref)

    blk = pl.BlockSpec((tr, cols), lambda i: (i, 0))
    return pl.pallas_call(
        body, name=name,
        out_shape=(jax.ShapeDtypeStruct((r, cols), F32),) * 4,
        grid=(r // tr,),
        in_specs=[blk] * 4, out_specs=(blk,) * 4,
        compiler_params=_params("parallel"),
    )(w, g, m, v)


WEIGHTS = ("even_norm_pre", "even_norm_post", "even_w_in", "rg_conv_w", "rg_conv_b", "rg_gate_w", "rg_gate_b",
           "rg_lambda", "sc_conv_w", "even_w_out", "odd_norm_pre", "odd_norm_post", "odd_w_in", "gla_w_gate_lr",
           "gla_b_gate", "gla_norm_g", "odd_w_out")
BIG = ("even_w_in", "even_w_out", "odd_w_in", "odd_w_out")


def _halves(a):
    return a.reshape((2, a.shape[0] // 2) + a.shape[1:])


def kernel(x, even_norm_pre, even_norm_post, even_w_in, rg_conv_w, rg_conv_b, rg_gate_w, rg_gate_b, rg_lambda, sc_conv_w, even_w_out, odd_norm_pre, odd_norm_post, odd_w_in, gla_w_gate_lr, gla_b_gate, gla_norm_g, odd_w_out, loss_target, m_even_norm_pre, m_even_norm_post, m_even_w_in, m_rg_conv_w, m_rg_conv_b, m_rg_gate_w, m_rg_gate_b, m_rg_lambda, m_sc_conv_w, m_even_w_out, m_odd_norm_pre, m_odd_norm_post, m_odd_w_in, m_gla_w_gate_lr, m_gla_b_gate, m_gla_norm_g, m_odd_w_out, v_even_norm_pre, v_even_norm_post, v_even_w_in, v_rg_conv_w, v_rg_conv_b, v_rg_gate_w, v_rg_gate_b, v_rg_lambda, v_sc_conv_w, v_even_w_out, v_odd_norm_pre, v_odd_norm_post, v_odd_w_in, v_gla_w_gate_lr, v_gla_b_gate, v_gla_norm_g, v_odd_w_out):
    given = dict(locals())
    shard = {n: given[n][0] for n in WEIGHTS}
    m_in = {n: given["m_" + n][0] for n in WEIGHTS}
    v_in = {n: given["v_" + n][0] for n in WEIGHTS}
    mx, my, mc = lax.axis_index("x"), lax.axis_index("y"), lax.axis_index("c")
    core = jnp.reshape(mc, (1,)).astype(jnp.int32)
    chip = jnp.reshape(2 * mx + my, (1,)).astype(jnp.int32)

    small_shard = _pack(shard, SHARDED_SMALL, SHARDED_ROWS)
    big_own = [_halves(shard[n].astype(BF16)) for n in BIG]
    started_a = gather_start(big_own[:1], [small_shard], "gather_start_a")
    started_b = gather_start(big_own[1:], [], "gather_start_b")
    even_w_in_full, small_full = gather_wait(started_a, 1, started_b[-1], "gather_wait_a")
    (even_w_in_full,) = pass_to_sibling([even_w_in_full], "gather_pass_a")
    even_w_in_full = place_own(even_w_in_full, big_own[0], chip, "place_even_w_in")
    small_full = lax.dynamic_update_slice(small_full, small_shard[None], (chip[0], 0, 0))
    full = {n: shard[n] for n, _ in REPLICATED + LAST_REPLICATED}
    full.update({n: _merge_owners(a) for n, a in _unpack(small_full, SHARDED_SMALL, lead=(4,)).items()})
    full["even_w_in"] = even_w_in_full.reshape(4, D_MODEL, EVEN_IN // 4)

    def late_weights(after):
        lands = pass_to_sibling(list(gather_wait(started_b, 3, after, "gather_wait_b")), "gather_pass_b")
        lands = [place_own(a, b, chip, "place_" + n) for a, b, n in zip(lands, big_own[1:], BIG[1:])]
        odd_w_in = jnp.transpose(lands[1].reshape(4, D_MODEL, ODD_IN // 4), (1, 0, 2)).reshape(D_MODEL, ODD_IN)
        return _prepare_weights({"even_w_out": lands[0].reshape(2 * D_MODEL, D_MODEL), "odd_w_in": odd_w_in,
                                 "odd_w_out": lands[2].reshape(D_MODEL, D_MODEL)})

    pending = {}

    def slab(a):
        return a.reshape((4, 2, a.shape[1] // 2) + a.shape[2:])

    def begin(tag, slabs, to_send, dtypes):
        got = exchange_with_sibling(to_send, "grad_sibling_" + tag)
        sums = [add_sibling(a, b, core, dt, "grad_add_sibling_%s%d" % (tag, i))
                for i, (a, b, dt) in enumerate(zip(slabs, got, dtypes))]
        pending[tag] = exchange_with_chips_start(sums, "grad_chips_start_" + tag)
        return pending[tag][-1][0, 0]

    def finish(tag, after):
        sums, got = exchange_with_chips_wait(pending[tag], after, "grad_chips_wait_" + tag)
        return [add_chips(a, b, chip, "grad_add_chips_%s%d" % (tag, i)) for i, (a, b) in enumerate(zip(sums, got))]

    def reduce_first(g, g16):
        odd_w_in = slab(jnp.transpose(g["odd_w_in"].reshape(D_MODEL, 4, ODD_IN // 4), (1, 0, 2)))
        slabs = [odd_w_in] + [slab(g[n].reshape(4, -1, D_MODEL)) for n in ("odd_w_out", "even_w_out")]
        to_send = [odd_w_in.astype(BF16)] + [slab(g16[n].reshape(4, -1, D_MODEL)) for n in ("odd_w_out", "even_w_out")]
        return begin("a", slabs, to_send, [BF16] * 3)

    def reduce_second(g, g16):
        pending["totals_a"] = finish("a", g["even_w_in"])
        rep_rows = _pack(g, REPLICATED, REPLICATED_ROWS).reshape(4, 2, REP_PART, LANES)
        sh_rows = _pack({n: _split_owners(g[n]) for n, _ in SHARDED_SMALL}, SHARDED_SMALL, SHARDED_ROWS, lead=(4,))
        pack = jnp.concatenate([sh_rows.reshape(4, 2, HALF_SHARDED, LANES), rep_rows], axis=2)
        return begin("b", [slab(g["even_w_in"]), pack], [slab(g16["even_w_in"]), pack], [BF16, F32])

    loss, grad_x, g = local_step(x[0], loss_target[0], _prepare_weights(full), reduce_first, reduce_second,
                                 late_weights)
    odd_w_in_t, odd_w_out_t, even_w_out_t = pending["totals_a"]
    even_w_in_t, pack_t = finish("b", grad_x)
    totals = [even_w_in_t, even_w_out_t, odd_w_in_t, odd_w_out_t]
    last_part = jnp.concatenate([_pack(g, LAST_REPLICATED, LAST_ROWS), loss])
    from_core, rep_all, last_all = share_totals(totals, pack_t, last_part)
    me = 2 * chip[0] + core[0]
    mine, theirs = pack_t[:HALF_SHARDED], from_core[4][:HALF_SHARDED]
    sh_total = jnp.where(mc == 0, jnp.concatenate([mine, theirs]), jnp.concatenate([theirs, mine]))
    rep_all = lax.dynamic_update_slice(rep_all, pack_t[None, HALF_SHARDED:], (me, 0, 0))
    rep_total = rep_all.reshape(REPLICATED_ROWS, LANES)
    last_total = sum_parts(lax.dynamic_update_slice(last_all, last_part[None], (me, 0, 0)), "grad_sum_last")
    last_total, loss = last_total[:LAST_ROWS], last_total[LAST_ROWS, 0]
    grads = {}

    delta, new_m, new_v = {}, {}, {}
    for i, n in enumerate(BIG):
        if shard[n].shape[1] % LANES:
            outs = adamw_halves(shard[n].T, totals[i].T, from_core[i].T, m_in[n].T, v_in[n].T, core, "adamw_" + n,
                                by_columns=True)
            grads[n], delta[n], new_m[n], new_v[n] = [o.T for o in outs]
        else:
            grads[n], delta[n], new_m[n], new_v[n] = adamw_halves(shard[n], totals[i], from_core[i], m_in[n],
                                                                  v_in[n], core, "adamw_" + n)
    gate = [src["rg_gate_w"].reshape(GATE_ROWS, LANES) for src in (shard, m_in, v_in)]
    grads["rg_gate_w"], delta["rg_gate_w"], new_m["rg_gate_w"], new_v["rg_gate_w"] = adamw(
        gate[0], rep_total, gate[1], gate[2], "adamw_rg_gate_w")
    rest = REPLICATED[1:]
    rest_rows = sum(_seg_rows(shape) for _, shape in rest)
    grads.update(_unpack(sh_total, SHARDED_SMALL))
    grads.update(_unpack(rep_total[GATE_ROWS:GATE_ROWS + rest_rows], rest))
    grads.update(_unpack(last_total, LAST_REPLICATED))
    names = [n for n, _ in SHARDED_SMALL + rest + LAST_REPLICATED]
    rows_of = lambda a, n: a.reshape(-1, given[n].shape[-1])
    outs = adamw_many([rows_of(given[n], n) for n in names], [rows_of(grads[n], n) for n in names],
                      [rows_of(given["m_" + n], n) for n in names], [rows_of(given["v_" + n], n) for n in names],
                      "adamw_small")
    for n, (d, nm, nv) in zip(names, outs):
        delta[n], new_m[n], new_v[n] = d, nm, nv
    result = [loss, grad_x[None]]
    for group in (grads, delta, new_m, new_v):
        result += [group[n].reshape(given[n].shape) for n in WEIGHTS]
    return tuple(result)
```

```python
import functools

import jax
import jax.numpy as jnp
from jax import lax
from jax.experimental import pallas as pl
from jax.experimental.pallas import tpu as pltpu

F32 = jnp.float32
BF16 = jnp.bfloat16
MESH = pl.DeviceIdType.MESH

D_MODEL = 1024
NORM_EPS = 1e-6
RG_HEADS = 8
RG_HEAD_DIM = 128
RG_C = 8.0
EVEN_IN = 6144
ODD_IN = 3104
ODD_IN_PAD = 3200
GLA_HEADS = 4
GLA_DK = 128
GLA_DV = 256
GLA_RANK = 16
GLA_NORMALIZER = 16.0
GLA_CHUNK = 128
LR_COL = 3072

ADAM_LR = 0.001
ADAM_B1 = 0.9
ADAM_B2 = 0.999
ADAM_EPS = 1e-08
ADAM_WD = 0.01
ADAM_STEP = 10

SUBLANES = 8
LANES = 128
VMEM_LIMIT = 56 * 2 ** 20

ROW_TILE = 512
SCAN_TILE = 256
GLA_BLOCK = 1024
MIX_TILE = 128


def _params(*sem):
    return pltpu.CompilerParams(dimension_semantics=sem, vmem_limit_bytes=VMEM_LIMIT)


def _full(shape):
    n = len(shape)
    return pl.BlockSpec(shape, lambda *_: (0,) * n)


def _sigmoid(x):
    return 0.5 + 0.5 * jnp.tanh(0.5 * x)


def _softplus(x):
    return jnp.maximum(x, 0.0) + jnp.log(1.0 + jnp.exp(-jnp.abs(x)))


def _dot(a, b):
    return jnp.dot(a, b, preferred_element_type=F32)


def _dot_nt(a, b):
    return lax.dot_general(a, b, (((1,), (1,)), ((), ())), preferred_element_type=F32)


def _dot_tn(a, b):
    return lax.dot_general(a, b, (((0,), (0,)), ((), ())), preferred_element_type=F32)


def _bdot(a, b, ca, cb):
    return lax.dot_general(a, b, (((ca,), (cb,)), ((0,), (0,))), preferred_element_type=F32)


def _halo_specs(rows, cols, col_block, n_row_tiles, tix):
    per = rows // SUBLANES
    last = n_row_tiles * per - 1

    def split(args):
        if len(args) == 2:
            return tix(args[1]), col_block + args[0]
        return tix(args[0]), col_block

    def prev(*args):
        t, c = split(args)
        return (jnp.maximum(t * per - 1, 0), c)

    def main(*args):
        return split(args)

    def nxt(*args):
        t, c = split(args)
        return (jnp.minimum((t + 1) * per, last), c)

    return [pl.BlockSpec((SUBLANES, cols), prev), pl.BlockSpec((rows, cols), main),
            pl.BlockSpec((SUBLANES, cols), nxt)]


def _extend(prev_ref, main_ref, next_ref, is_first, is_last):
    p = jnp.where(is_first, 0.0, prev_ref[...])
    n = jnp.where(is_last, 0.0, next_ref[...])
    return jnp.concatenate([p, main_ref[...], n], axis=0)


def _shifted(ext, offset, rows):
    if offset == 0:
        return ext[SUBLANES:SUBLANES + rows]
    n = ext.shape[0]
    return pltpu.roll(ext, (-offset) % n, 0)[SUBLANES:SUBLANES + rows]


def _conv(ext, w, left, rows):
    out = None
    for k in range(w.shape[0]):
        term = _shifted(ext, k - left, rows) * w[k:k + 1]
        out = term if out is None else out + term
    return out


def _conv_transpose(ext, w, left, rows):
    out = None
    for k in range(w.shape[0]):
        term = _shifted(ext, left - k, rows) * w[k:k + 1]
        out = term if out is None else out + term
    return out


def _colsum(x):
    return jnp.sum(x, axis=0, keepdims=True)


def _accumulate(ref, value, step):
    @pl.when(step == 0)
    def _():
        ref[...] = value

    @pl.when(step > 0)
    def _():
        ref[...] += value


PROJ_TILE_BYTES = 7 * 2 ** 20


def _proj_row_tile(rows, width):
    tm = min(ROW_TILE, rows)
    while tm * width * 4 > PROJ_TILE_BYTES and tm % (2 * SUBLANES) == 0:
        tm //= 2
    return tm


def norm_matmul(x, gain, w, name):
    rows, d = x.shape
    n_col_tiles, _, tn = w.shape
    tm = _proj_row_tile(rows, n_col_tiles * tn)

    def body(x_ref, g_ref, w_ref, proj_ref, h_ref):
        xv = x_ref[...]
        rstd = lax.rsqrt(jnp.mean(xv * xv, axis=-1, keepdims=True) + NORM_EPS)
        hv = (xv * rstd * g_ref[...]).astype(BF16)
        h_ref[...] = hv
        for j in range(n_col_tiles):
            proj_ref[:, j * tn:(j + 1) * tn] = _dot(hv, w_ref[j])

    row = lambda cols: pl.BlockSpec((tm, cols), lambda i: (i, 0))
    return pl.pallas_call(
        body, name=name,
        out_shape=(jax.ShapeDtypeStruct((rows, n_col_tiles * tn), F32), jax.ShapeDtypeStruct((rows, d), BF16)),
        grid=(rows // tm,),
        in_specs=[row(d), _full((1, d)), _full(w.shape)],
        out_specs=(row(n_col_tiles * tn), row(d)),
        compiler_params=_params("parallel"),
    )(x, gain, w)


def inproj_bwd(dproj, w, x, gain, dres, name):
    rows, d = x.shape
    n_col_tiles, _, tn = w.shape
    tm = _proj_row_tile(rows, n_col_tiles * tn)

    def body(dp_ref, w_ref, x_ref, g_ref, dres_ref, dx_ref, dg_ref):
        dh = None
        for j in range(n_col_tiles):
            part = _dot_nt(dp_ref[:, j * tn:(j + 1) * tn], w_ref[j])
            dh = part if dh is None else dh + part
        _inproj_finish(dh, x_ref, g_ref, dres_ref, dx_ref, dg_ref, pl.program_id(0))

    row = lambda cols: pl.BlockSpec((tm, cols), lambda i: (i, 0))
    return pl.pallas_call(
        body, name=name,
        out_shape=(jax.ShapeDtypeStruct((rows, d), F32), jax.ShapeDtypeStruct((1, d), F32)),
        grid=(rows // tm,),
        in_specs=[row(n_col_tiles * tn), _full(w.shape), row(d), _full((1, d)), row(d)],
        out_specs=(row(d), _full((1, d))),
        compiler_params=_params("arbitrary"),
    )(dproj, w, x, gain, dres)


def _inproj_finish(dh, x_ref, g_ref, dres_ref, dx_ref, dg_ref, step):
    xv = x_ref[...]
    rstd = lax.rsqrt(jnp.mean(xv * xv, axis=-1, keepdims=True) + NORM_EPS)
    xhat = xv * rstd
    dxn = dh * g_ref[...]
    dx_ref[...] = dres_ref[...] + rstd * (dxn - xhat * jnp.mean(dxn * xhat, axis=-1, keepdims=True))
    _accumulate(dg_ref, _colsum(dh * xhat), step)


def inproj_bwd_pieces(pieces, w, x, gain, dres, name):
    rows, d = x.shape
    tm = min(ROW_TILE, rows)
    n = len(pieces)
    widths = [p.shape[1] for p in pieces]
    starts = [sum(widths[:k]) for k in range(n)]
    assert sum(widths) == w.shape[2]

    def body(*refs):
        w_ref, x_ref, g_ref, dres_ref, dx_ref, dg_ref = refs[n:]
        dh = None
        for k in range(n):
            part = _dot_nt(refs[k][...], w_ref[0, :, starts[k]:starts[k] + widths[k]])
            dh = part if dh is None else dh + part
        _inproj_finish(dh, x_ref, g_ref, dres_ref, dx_ref, dg_ref, pl.program_id(0))

    row = lambda cols: pl.BlockSpec((tm, cols), lambda i: (i, 0))
    return pl.pallas_call(
        body, name=name,
        out_shape=(jax.ShapeDtypeStruct((rows, d), F32), jax.ShapeDtypeStruct((1, d), F32)),
        grid=(rows // tm,),
        in_specs=[row(wd) for wd in widths] + [_full(w.shape), row(d), _full((1, d)), row(d)],
        out_specs=(row(d), _full((1, d))),
        compiler_params=_params("arbitrary"),
    )(*pieces, w, x, gain, dres)


def matmul_dw_pieces(a, pieces, name):
    rows, m = a.shape
    tk = min(2 * ROW_TILE, rows)
    n = len(pieces)

    def body(*refs):
        a_ref, ins, outs = refs[0], refs[1:1 + n], refs[1 + n:]
        av = a_ref[...]
        for k in range(n):
            _accumulate(outs[k], _dot_tn(av, ins[k][...]), pl.program_id(0))

    return pl.pallas_call(
        body, name=name,
        out_shape=[jax.ShapeDtypeStruct((m, p.shape[1]), F32) for p in pieces],
        grid=(rows // tk,),
        in_specs=[pl.BlockSpec((tk, m), lambda k: (k, 0))]
        + [pl.BlockSpec((tk, p.shape[1]), lambda k: (k, 0)) for p in pieces],
        out_specs=[_full((m, p.shape[1])) for p in pieces],
        compiler_params=_params("arbitrary"),
    )(a, *pieces)


def matmul_dw(a, b, bn, name):
    rows, m = a.shape
    n = b.shape[1]
    tk = min((4 if n > bn else 2) * ROW_TILE, rows)
    steps = rows // tk

    def body(a_ref, b_ref, o_ref, o16_ref):
        part = _dot_tn(a_ref[...], b_ref[...])

        @pl.when(pl.program_id(1) == 0)
        def _():
            o_ref[0] = part

        @pl.when(pl.program_id(1) > 0)
        def _():
            o_ref[0] += part

        @pl.when(pl.program_id(1) == steps - 1)
        def _():
            o16_ref[0] = o_ref[0].astype(BF16)

    out = pl.BlockSpec((1, m, bn), lambda j, k: (j, 0, 0))
    return pl.pallas_call(
        body, name=name,
        out_shape=(jax.ShapeDtypeStruct((n // bn, m, bn), F32), jax.ShapeDtypeStruct((n // bn, m, bn), BF16)),
        grid=(n // bn, steps),
        in_specs=[pl.BlockSpec((tk, m), lambda j, k: (k, 0)), pl.BlockSpec((tk, bn), lambda j, k: (k, j))],
        out_specs=(out, out),
        compiler_params=_params("parallel", "arbitrary"),
    )(a, b)


def _scan(a, b, carry, reverse):
    n, c = a.shape
    blocks = n // SUBLANES
    a = a.reshape(blocks, SUBLANES, c)
    b = b.reshape(blocks, SUBLANES, c)
    pos = lax.broadcasted_iota(jnp.int32, (1, SUBLANES, c), 1)
    s = 1
    while s < SUBLANES:
        shift, valid = (SUBLANES - s, pos < SUBLANES - s) if reverse else (s, pos >= s)
        a_s, b_s = pltpu.roll(a, shift, 1), pltpu.roll(b, shift, 1)
        b = jnp.where(valid, a * b_s + b, b)
        a = jnp.where(valid, a * a_s, a)
        s *= 2
    out = [None] * blocks
    for k in (range(blocks - 1, -1, -1) if reverse else range(blocks)):
        h = a[k] * carry + b[k]
        out[k] = h
        carry = h[0:1] if reverse else h[SUBLANES - 1:SUBLANES]
    return jnp.concatenate(out, axis=0)


def _rg_gates(ua, gw_ref, gb, lam):
    ub = ua.astype(BF16)
    pre_r, pre_i = [], []
    for h in range(RG_HEADS):
        z = _dot(ub[:, h * RG_HEAD_DIM:(h + 1) * RG_HEAD_DIM], gw_ref[h])
        pre_r.append(z[:, :RG_HEAD_DIM])
        pre_i.append(z[:, RG_HEAD_DIM:])
    r = _sigmoid(jnp.concatenate(pre_r, axis=1) + gb[0:1])
    i = _sigmoid(jnp.concatenate(pre_i, axis=1) + gb[1:2])
    sp = _softplus(-lam)
    log_a = -RG_C * r * sp
    a = jnp.exp(log_a)
    mult = jnp.sqrt(1.0 - a * a)
    return r, i, sp, a, mult


def _rg_weight_specs():
    return [_full((4, D_MODEL)), _full((1, D_MODEL)), _full((RG_HEADS, RG_HEAD_DIM, 2 * RG_HEAD_DIM)),
            _full((2, D_MODEL)), _full((1, D_MODEL))]


def rglru_fwd(proj, conv_w, conv_b, gate_w, gate_b, lam, reverse, name):
    rows_total = proj.shape[0]
    rows = min(SCAN_TILE, rows_total)
    n_tiles = rows_total // rows
    tix = (lambda i: n_tiles - 1 - i) if reverse else (lambda i: i)

    def body(xp, xm, xn, cw_ref, cb_ref, gw_ref, gb_ref, lam_ref, h_ref, acts_ref, carry):
        i = pl.program_id(0)
        t = tix(i)
        ext = _extend(xp, xm, xn, t == 0, t == n_tiles - 1)
        ua = _conv(ext, cw_ref[...], 2, rows) + cb_ref[...]
        r, gi, _, a, mult = _rg_gates(ua, gw_ref, gb_ref[...], lam_ref[...])
        for k, saved in enumerate((ua, r, gi, a, mult)):
            acts_ref[k] = saved
        b = mult * (gi * ua)

        @pl.when(i == 0)
        def _():
            carry[...] = jnp.zeros_like(carry)

        h = _scan(a, b, carry[0:1], reverse)
        h_ref[...] = h
        edge = h[0:1] if reverse else h[rows - 1:rows]
        carry[...] = jnp.broadcast_to(edge, carry.shape)

    return pl.pallas_call(
        body, name=name,
        out_shape=(jax.ShapeDtypeStruct((rows_total, D_MODEL), F32),
                   jax.ShapeDtypeStruct((5, rows_total, D_MODEL), F32)),
        grid=(n_tiles,),
        in_specs=_halo_specs(rows, D_MODEL, 0, n_tiles, tix) + _rg_weight_specs(),
        out_specs=(pl.BlockSpec((rows, D_MODEL), lambda i: (tix(i), 0)),
                   pl.BlockSpec((5, rows, D_MODEL), lambda i: (0, tix(i), 0))),
        scratch_shapes=[pltpu.VMEM((SUBLANES, D_MODEL), F32)],
        compiler_params=_params("arbitrary"),
    )(proj, proj, proj, conv_w, conv_b, gate_w, gate_b, lam)


def rglru_bwd(proj, dycat, h_dir, acts, gate_w, lam, add_dua, reverse, name):
    rows_total = proj.shape[0]
    rows = min(SCAN_TILE, rows_total)
    n_tiles = rows_total // rows
    tix = (lambda i: i) if reverse else (lambda i: n_tiles - 1 - i)
    za_block = 1

    def body(acts_ref, za_ref, dya_ref, hp, hm, hn, gw_ref, lam_ref, *rest):
        other = rest[0][...] if add_dua is not None else 0.0
        dua_ref, dgw_ref, dgb_ref, dlam_ref, carry = rest[-5:]
        step = pl.program_id(0)
        t = tix(step)
        first, last = t == 0, t == n_tiles - 1
        ua, r, gi, a, mult = (acts_ref[k] for k in range(5))
        lam_v = lam_ref[...]
        sp = _softplus(-lam_v)
        za = za_ref[...]
        dh = dya_ref[...] * (za * _sigmoid(za))

        @pl.when(step == 0)
        def _():
            carry[...] = jnp.zeros_like(carry)

        old = carry[0:1]
        mu = _scan(a, a * dh, old, not reverse)
        row = lax.broadcasted_iota(jnp.int32, mu.shape, 0)
        if reverse:
            mu_next = jnp.where(row == 0, old, pltpu.roll(mu, 1, 0))
            carry[...] = jnp.broadcast_to(mu[rows - 1:rows], carry.shape)
            h_ext = _extend(hp, hm, hn, first, last)
            h_prev = _shifted(h_ext, 1, rows)
        else:
            mu_next = jnp.where(row == rows - 1, old, pltpu.roll(mu, rows - 1, 0))
            carry[...] = jnp.broadcast_to(mu[0:1], carry.shape)
            h_ext = _extend(hp, hm, hn, first, last)
            h_prev = _shifted(h_ext, -1, rows)
        db = dh + mu_next
        da = db * h_prev
        d_mult = db * (gi * ua)
        di = db * (mult * ua)
        dua = db * (mult * gi)
        dlog_a = da * a - d_mult * (a * a) / mult
        dr = dlog_a * (-RG_C * sp)
        dlam = _colsum(dlog_a * (-RG_C * r)) * (-_sigmoid(-lam_v))
        dpr = dr * (r * (1.0 - r))
        dpi = di * (gi * (1.0 - gi))
        dgb = jnp.concatenate([_colsum(dpr), _colsum(dpi)], axis=0)
        ub = ua.astype(BF16)
        dua_heads, dgw_heads = [], []
        for h in range(RG_HEADS):
            cols = slice(h * RG_HEAD_DIM, (h + 1) * RG_HEAD_DIM)
            dz = jnp.concatenate([dpr[:, cols], dpi[:, cols]], axis=1).astype(BF16)
            dgw_heads.append(_dot_tn(ub[:, cols], dz))
            dua_heads.append(_dot_nt(dz, gw_ref[h]))
        dua_ref[...] = dua + jnp.concatenate(dua_heads, axis=1) + other

        @pl.when(step == 0)
        def _():
            for h in range(RG_HEADS):
                dgw_ref[h] = dgw_heads[h]
            dgb_ref[...] = dgb
            dlam_ref[...] = dlam

        @pl.when(step > 0)
        def _():
            for h in range(RG_HEADS):
                dgw_ref[h] += dgw_heads[h]
            dgb_ref[...] += dgb
            dlam_ref[...] += dlam

    row_spec = lambda col: pl.BlockSpec((rows, D_MODEL), lambda i: (tix(i), col))
    return pl.pallas_call(
        body, name=name,
        out_shape=(jax.ShapeDtypeStruct((rows_total, D_MODEL), F32),
                   jax.ShapeDtypeStruct((RG_HEADS, RG_HEAD_DIM, 2 * RG_HEAD_DIM), F32),
                   jax.ShapeDtypeStruct((2, D_MODEL), F32), jax.ShapeDtypeStruct((1, D_MODEL), F32)),
        grid=(n_tiles,),
        in_specs=([pl.BlockSpec((5, rows, D_MODEL), lambda i: (0, tix(i), 0)), row_spec(za_block), row_spec(0)]
                  + _halo_specs(rows, D_MODEL, 0, n_tiles, tix)
                  + [_full((RG_HEADS, RG_HEAD_DIM, 2 * RG_HEAD_DIM)), _full((1, D_MODEL))]
                  + ([] if add_dua is None else [row_spec(0)])),
        out_specs=(row_spec(0), _full((RG_HEADS, RG_HEAD_DIM, 2 * RG_HEAD_DIM)), _full((2, D_MODEL)),
                   _full((1, D_MODEL))),
        scratch_shapes=[pltpu.VMEM((SUBLANES, D_MODEL), F32)],
        compiler_params=_params("arbitrary"),
    )(acts, proj, dycat, h_dir, h_dir, h_dir, gate_w, lam, *([] if add_dua is None else [add_dua]))


def _extend_cols(refs, block, is_first, is_last):
    cols = slice(block * D_MODEL, (block + 1) * D_MODEL)
    prev_ref, main_ref, next_ref = refs
    p = jnp.where(is_first, 0.0, prev_ref[:, cols])
    n = jnp.where(is_last, 0.0, next_ref[:, cols])
    return jnp.concatenate([p, main_ref[:, cols], n], axis=0)


def even_mix_fwd(proj, h_f, h_b, sc_w, name):
    rows_total = proj.shape[0]
    rows = min(2 * MIX_TILE, rows_total)
    n_tiles = rows_total // rows
    ident = lambda i: i

    def body(za_ref, hf_ref, hb_ref, xbp, xbm, xbn, gcp, gcm, gcn, gb_ref, zb_ref, w_ref, y_ref):
        t = pl.program_id(0)
        first, last = t == 0, t == n_tiles - 1
        za = za_ref[...]
        y_ref[:, 0:D_MODEL] = ((hf_ref[...] + hb_ref[...]) * (za * _sigmoid(za))).astype(BF16)
        p_ext = _extend(xbp, xbm, xbn, first, last) * _extend(gcp, gcm, gcn, first, last)
        cv = _conv(p_ext, w_ref[...], 1, rows)
        zb = zb_ref[...]
        y_ref[:, D_MODEL:2 * D_MODEL] = (gb_ref[...] * cv * (zb * _sigmoid(zb))).astype(BF16)

    blk = lambda col: pl.BlockSpec((rows, D_MODEL), lambda i: (i, col))
    return pl.pallas_call(
        body, name=name,
        out_shape=jax.ShapeDtypeStruct((rows_total, 2 * D_MODEL), BF16),
        grid=(n_tiles,),
        in_specs=([blk(1), blk(0), blk(0)] + _halo_specs(rows, D_MODEL, 2, n_tiles, ident)
                  + _halo_specs(rows, D_MODEL, 4, n_tiles, ident) + [blk(3), blk(5), _full((3, D_MODEL))]),
        out_specs=pl.BlockSpec((rows, 2 * D_MODEL), lambda i: (i, 0)),
        compiler_params=_params("parallel"),
    )(proj, h_f, h_b, proj, proj, proj, proj, proj, proj, proj, proj, sc_w)


def even_mix_bwd(proj, dycat, h_f, h_b, dua, conv_w, sc_w, name):
    rows_total, width = proj.shape
    rows = min(MIX_TILE, rows_total)
    n_tiles = rows_total // rows
    ident = lambda i: i

    def body(pp, pm, pn, dyp, dym, dyn, hf_ref, hb_ref, dup, dum, dun, cw_ref, sw_ref,
             dp_ref, dcw_ref, dcb_ref, dsw_ref):
        def put(k, value):
            dp_ref[:, k * D_MODEL:(k + 1) * D_MODEL] = value.astype(BF16)

        t = pl.program_id(0)
        first, last = t == 0, t == n_tiles - 1
        proj_ext = lambda k: _extend_cols((pp, pm, pn), k, first, last)
        mid = slice(SUBLANES, SUBLANES + rows)
        za = pm[:, D_MODEL:2 * D_MODEL]
        sa = _sigmoid(za)
        put(1, dym[:, 0:D_MODEL] * (hf_ref[...] + hb_ref[...]) * (sa * (1.0 + za * (1.0 - sa))))
        dua_ext = _extend(dup, dum, dun, first, last)
        cw = cw_ref[...]
        put(0, _conv_transpose(dua_ext, cw, 2, rows))
        dua_mid = dua_ext[mid]
        xa_ext = proj_ext(0)
        dcw = jnp.concatenate([_colsum(dua_mid * _shifted(xa_ext, k - 2, rows)) for k in range(4)], axis=0)
        dcb = _colsum(dua_mid)
        xb_ext, gb_ext, gc_ext, zb_ext = proj_ext(2), proj_ext(3), proj_ext(4), proj_ext(5)
        p_ext = xb_ext * gc_ext
        sb_ext = _sigmoid(zb_ext)
        dyb_ext = _extend_cols((dyp, dym, dyn), 1, first, last)
        dcv_ext = dyb_ext * gb_ext * (zb_ext * sb_ext)
        sw = sw_ref[...]
        p_at = [_shifted(p_ext, k - 1, rows) for k in range(3)]
        cv = (p_at[0] * sw[0:1] + p_at[1] * sw[1:2]) + p_at[2] * sw[2:3]
        zb, sb, dyb, gb = zb_ext[mid], sb_ext[mid], dyb_ext[mid], gb_ext[mid]
        put(3, dyb * cv * (zb * sb))
        put(5, dyb * gb * cv * (sb * (1.0 + zb * (1.0 - sb))))
        dp = _conv_transpose(dcv_ext, sw, 1, rows)
        put(4, dp * xb_ext[mid])
        put(2, dp * gc_ext[mid])
        dcv = dcv_ext[mid]
        dsw = jnp.concatenate([_colsum(dcv * p_at[k]) for k in range(3)], axis=0)
        _accumulate(dcw_ref, dcw, t)
        _accumulate(dcb_ref, dcb, t)
        _accumulate(dsw_ref, dsw, t)

    own = pl.BlockSpec((rows, D_MODEL), lambda i: (i, 0))
    return pl.pallas_call(
        body, name=name,
        out_shape=(jax.ShapeDtypeStruct((rows_total, 6 * D_MODEL), BF16),
                   jax.ShapeDtypeStruct((4, D_MODEL), F32), jax.ShapeDtypeStruct((1, D_MODEL), F32),
                   jax.ShapeDtypeStruct((3, D_MODEL), F32)),
        grid=(n_tiles,),
        in_specs=(_halo_specs(rows, width, 0, n_tiles, ident) + _halo_specs(rows, 2 * D_MODEL, 0, n_tiles, ident)
                  + [own, own] + _halo_specs(rows, D_MODEL, 0, n_tiles, ident)
                  + [_full((4, D_MODEL)), _full((3, D_MODEL))]),
        out_specs=(pl.BlockSpec((rows, 6 * D_MODEL), lambda i: (i, 0)), _full((4, D_MODEL)), _full((1, D_MODEL)),
                   _full((3, D_MODEL))),
        compiler_params=_params("arbitrary"),
    )(proj, proj, proj, dycat, dycat, dycat, h_f, h_b, dua, dua, dua, conv_w, sc_w)


def even_out_fwd(ycat, w_out, gain, x, name):
    rows, d = x.shape
    k = ycat.shape[1]
    tm = min(ROW_TILE, rows)

    def body(yc_ref, w_ref, g_ref, x_ref, x1_ref, y_ref):
        y = _dot(yc_ref[...], w_ref[...])
        y_ref[...] = y
        rstd = lax.rsqrt(jnp.mean(y * y, axis=-1, keepdims=True) + NORM_EPS)
        x1_ref[...] = x_ref[...] + y * rstd * g_ref[...]

    row = lambda n: pl.BlockSpec((tm, n), lambda i: (i, 0))
    return pl.pallas_call(
        body, name=name,
        out_shape=(jax.ShapeDtypeStruct((rows, d), F32),) * 2,
        grid=(rows // tm,),
        in_specs=[row(k), _full((k, d)), _full((1, d)), row(d)],
        out_specs=(row(d), row(d)),
        compiler_params=_params("parallel"),
    )(ycat, w_out, gain, x)


def _rmsnorm_bwd(dout, y, gain):
    rstd = lax.rsqrt(jnp.mean(y * y, axis=-1, keepdims=True) + NORM_EPS)
    yhat = y * rstd
    dyn = dout * gain
    dy = rstd * (dyn - yhat * jnp.mean(dyn * yhat, axis=-1, keepdims=True))
    return dy, dout * yhat


def even_out_bwd(dx1, y, gain, w_out, name):
    rows, d = y.shape
    k = w_out.shape[0]
    tm = min(ROW_TILE, rows)

    def body(dx_ref, y_ref, g_ref, w_ref, dy_ref, dyc_ref, dg_ref):
        dy, dg_rows = _rmsnorm_bwd(dx_ref[...], y_ref[...], g_ref[...])
        dyb = dy.astype(BF16)
        dy_ref[...] = dyb
        dyc_ref[...] = _dot_nt(dyb, w_ref[...])
        _accumulate(dg_ref, _colsum(dg_rows), pl.program_id(0))

    row = lambda n: pl.BlockSpec((tm, n), lambda i: (i, 0))
    return pl.pallas_call(
        body, name=name,
        out_shape=(jax.ShapeDtypeStruct((rows, d), BF16), jax.ShapeDtypeStruct((rows, k), F32),
                   jax.ShapeDtypeStruct((1, d), F32)),
        grid=(rows // tm,),
        in_specs=[row(d), row(d), _full((1, d)), _full((k, d))],
        out_specs=(row(d), row(k), _full((1, d))),
        compiler_params=_params("arbitrary"),
    )(dx1, y, gain, w_out)


def _chunk_cumsum(g, reverse):
    n, c = g.shape
    chunks, per = n // GLA_CHUNK, GLA_CHUNK // SUBLANES
    g = g.reshape(n // SUBLANES, SUBLANES, c)
    pos = lax.broadcasted_iota(jnp.int32, (1, SUBLANES, c), 1)
    s = 1
    while s < SUBLANES:
        if reverse:
            g = g + jnp.where(pos < SUBLANES - s, pltpu.roll(g, SUBLANES - s, 1), 0.0)
        else:
            g = g + jnp.where(pos >= s, pltpu.roll(g, s, 1), 0.0)
        s *= 2
    g = g.reshape(chunks, per, SUBLANES, c)
    out, carry = [None] * per, None
    for k in (range(per - 1, -1, -1) if reverse else range(per)):
        out[k] = g[:, k] if carry is None else g[:, k] + carry
        carry = out[k][:, 0:1] if reverse else out[k][:, SUBLANES - 1:SUBLANES]
    return jnp.stack(out, axis=1).reshape(n, c)


def _gla_prepare(q_ref, k_ref, lr_ref, wg_ref, bg_ref, reverse, n_chunks):
    z = _dot(lr_ref[...].astype(BF16), wg_ref[0]) + bg_ref[0]
    g = -_softplus(-z) * (1.0 / GLA_NORMALIZER)
    bcum = _chunk_cumsum(g, reverse).reshape(n_chunks, GLA_CHUNK, GLA_DK)
    edge = 0 if reverse else GLA_CHUNK - 1
    btot = bcum[:, edge:edge + 1, :]
    e_pos = jnp.exp(bcum)
    e_neg = jnp.exp(-bcum)
    e_st = jnp.exp(btot - bcum)
    q3 = q_ref[...].reshape(n_chunks, GLA_CHUNK, GLA_DK)
    k3 = k_ref[...].reshape(n_chunks, GLA_CHUNK, GLA_DK)
    scale = GLA_DK ** -0.5
    q_in = q3 * scale * e_pos
    k_in = k3 * e_neg
    k_st = k3 * e_st
    dec = jnp.exp(btot)
    return z, q_in, k_in, k_st, dec, (scale * e_pos, e_neg, e_st)


def _gla_mask(reverse):
    i = lax.broadcasted_iota(jnp.int32, (GLA_CHUNK, GLA_CHUNK), 0)
    j = lax.broadcasted_iota(jnp.int32, (GLA_CHUNK, GLA_CHUNK), 1)
    return (j >= i) if reverse else (j <= i)


def _gla_specs(rows, n_blocks, reverse):
    tix = (lambda s: n_blocks - 1 - s) if reverse else (lambda s: s)
    d = 1 if reverse else 0
    lr_block = LR_COL // LANES
    specs = [pl.BlockSpec((rows, GLA_DK), lambda h, s: (tix(s), h)),
             pl.BlockSpec((rows, GLA_DK), lambda h, s: (tix(s), GLA_HEADS + h)),
             pl.BlockSpec((rows, GLA_DV), lambda h, s: (tix(s), GLA_HEADS + h)),
             pl.BlockSpec((rows, LANES), lambda h, s: (tix(s), lr_block)),
             pl.BlockSpec((1, LANES, GLA_DK), lambda h, s: (d, 0, h)),
             pl.BlockSpec((1, 1, GLA_DK), lambda h, s: (d, 0, h))]
    return specs, tix


def gla_fwd(proj, wg_pad, bg, reverse, name):
    rows_total = proj.shape[0]
    rows = min(GLA_BLOCK, rows_total)
    n_blocks = rows_total // rows
    n_chunks = rows // GLA_CHUNK
    specs, tix = _gla_specs(rows, n_blocks, reverse)

    def body(q_ref, k_ref, v_ref, lr_ref, wg_ref, bg_ref, o_ref, st_ref, state, kv_scr, dec_scr):
        _, q_in, k_in, k_st, dec, _ = _gla_prepare(q_ref, k_ref, lr_ref, wg_ref, bg_ref, reverse, n_chunks)
        vb = v_ref[...].reshape(n_chunks, GLA_CHUNK, GLA_DV).astype(BF16)
        qb = q_in.astype(BF16)
        p = jnp.where(_gla_mask(reverse), _bdot(qb, k_in.astype(BF16), 2, 2), 0.0)
        o = _bdot(p.astype(BF16), vb, 2, 1)
        kv_scr[...] = _bdot(vb, k_st.astype(BF16), 1, 1)
        dec_scr[...] = jnp.broadcast_to(dec, dec_scr.shape)

        @pl.when(pl.program_id(1) == 0)
        def _():
            state[...] = jnp.zeros_like(state)

        for c in range(n_chunks):
            cc = n_chunks - 1 - c if reverse else c
            st_ref[0, cc] = state[...]
            state[...] = state[...] * dec_scr[cc, 0:1] + kv_scr[cc]
        o = o + _bdot(qb, st_ref[0].astype(BF16), 2, 2)
        o_ref[...] = o.reshape(rows, GLA_DV)

    return pl.pallas_call(
        body, name=name,
        out_shape=(jax.ShapeDtypeStruct((rows_total, GLA_HEADS * GLA_DV), F32),
                   jax.ShapeDtypeStruct((GLA_HEADS, rows_total // GLA_CHUNK, GLA_DV, GLA_DK), F32)),
        grid=(GLA_HEADS, n_blocks),
        in_specs=specs,
        out_specs=(pl.BlockSpec((rows, GLA_DV), lambda h, s: (tix(s), h)),
                   pl.BlockSpec((1, n_chunks, GLA_DV, GLA_DK), lambda h, s: (h, tix(s), 0, 0))),
        scratch_shapes=[pltpu.VMEM((GLA_DV, GLA_DK), F32), pltpu.VMEM((n_chunks, GLA_DV, GLA_DK), F32),
                        pltpu.VMEM((n_chunks, SUBLANES, GLA_DK), F32)],
        compiler_params=_params("parallel", "arbitrary"),
    )(proj, proj, proj, proj, wg_pad, bg)


def gla_bwd(proj, wg_pad, bg, d_o, states, dqkv_in, reverse, name):
    rows_total = proj.shape[0]
    rows = min(GLA_BLOCK, rows_total)
    n_blocks = rows_total // rows
    n_chunks = rows // GLA_CHUNK
    specs, tix = _gla_specs(rows, n_blocks, not reverse)
    d = 1 if reverse else 0
    specs[4] = pl.BlockSpec((1, LANES, GLA_DK), lambda h, s: (d, 0, h))
    specs[5] = pl.BlockSpec((1, 1, GLA_DK), lambda h, s: (d, 0, h))
    add = dqkv_in is not None

    def body(*refs):
        q_ref, k_ref, v_ref, lr_ref, wg_ref, bg_ref, do_ref, st_ref = refs[:8]
        refs = refs[8:]
        if add:
            aq_ref, ak_ref, av_ref = refs[:3]
            refs = refs[3:]
        dq_ref, dk_ref, dv_ref, dz_ref, dstate, g_scr, dec_scr, dsn_scr = refs
        z, q_in, k_in, k_st, dec, (f_q, f_k, f_s) = _gla_prepare(q_ref, k_ref, lr_ref, wg_ref, bg_ref, reverse,
                                                                 n_chunks)
        mask = _gla_mask(reverse)
        vb = v_ref[...].reshape(n_chunks, GLA_CHUNK, GLA_DV).astype(BF16)
        dob = do_ref[...].reshape(n_chunks, GLA_CHUNK, GLA_DV).astype(BF16)
        qb, kb, ksb = q_in.astype(BF16), k_in.astype(BF16), k_st.astype(BF16)
        st = st_ref[0]
        stb = st.astype(BF16)
        pb = jnp.where(mask, _bdot(qb, kb, 2, 2), 0.0).astype(BF16)
        dpb = jnp.where(mask, _bdot(dob, vb, 2, 2), 0.0).astype(BF16)
        d_qin = _bdot(dpb, kb, 2, 1) + _bdot(dob, stb, 2, 1)
        d_kin = _bdot(dpb, qb, 1, 1)
        dv = _bdot(pb, dob, 1, 1)
        g_scr[...] = _bdot(dob, qb, 1, 1)
        dec_scr[...] = jnp.broadcast_to(dec, dec_scr.shape)

        @pl.when(pl.program_id(1) == 0)
        def _():
            dstate[...] = jnp.zeros_like(dstate)

        for c in range(n_chunks):
            cc = c if reverse else n_chunks - 1 - c
            dsn_scr[cc] = dstate[...]
            dstate[...] = dstate[...] * dec_scr[cc, 0:1] + g_scr[cc]
        dsn = dsn_scr[...]
        dsnb = dsn.astype(BF16)
        dv = dv + _bdot(ksb, dsnb, 2, 2)
        d_kst = _bdot(vb, dsnb, 2, 1)
        d_dec = jnp.sum(dsn * st, axis=1, keepdims=True)
        ks_term = d_kst * k_st
        d_btot = d_dec * dec + jnp.sum(ks_term, axis=1, keepdims=True)
        d_b = d_qin * q_in - d_kin * k_in - ks_term
        pos = lax.broadcasted_iota(jnp.int32, d_b.shape, 1)
        edge = 0 if reverse else GLA_CHUNK - 1
        d_b = d_b + jnp.where(pos == edge, d_btot, 0.0)
        dg = _chunk_cumsum(d_b.reshape(rows, GLA_DK), not reverse)
        dz_ref[...] = dg * (1.0 / GLA_NORMALIZER) * _sigmoid(-z)
        dq = (d_qin * f_q).reshape(rows, GLA_DK)
        dk = (d_kin * f_k + d_kst * f_s).reshape(rows, GLA_DK)
        dv = dv.reshape(rows, GLA_DV)
        if add:
            dq_ref[...] = (dq + aq_ref[...]).astype(BF16)
            dk_ref[...] = (dk + ak_ref[...]).astype(BF16)
            dv_ref[...] = (dv + av_ref[...]).astype(BF16)
        else:
            dq_ref[...] = dq
            dk_ref[...] = dk
            dv_ref[...] = dv

    qkv_specs = [pl.BlockSpec((rows, GLA_DK), lambda h, s: (tix(s), h)),
                 pl.BlockSpec((rows, GLA_DK), lambda h, s: (tix(s), h)),
                 pl.BlockSpec((rows, GLA_DV), lambda h, s: (tix(s), h))]
    in_specs = specs + [pl.BlockSpec((rows, GLA_DV), lambda h, s: (tix(s), h)),
                        pl.BlockSpec((1, n_chunks, GLA_DV, GLA_DK), lambda h, s: (h, tix(s), 0, 0))]
    args = [proj, proj, proj, proj, wg_pad, bg, d_o, states]
    out_dtype = F32
    if add:
        in_specs += qkv_specs
        args += list(dqkv_in)
        out_dtype = BF16
    return pl.pallas_call(
        body, name=name,
        out_shape=(jax.ShapeDtypeStruct((rows_total, GLA_HEADS * GLA_DK), out_dtype),
                   jax.ShapeDtypeStruct((rows_total, GLA_HEADS * GLA_DK), out_dtype),
                   jax.ShapeDtypeStruct((rows_total, GLA_HEADS * GLA_DV), out_dtype),
                   jax.ShapeDtypeStruct((rows_total, GLA_HEADS * GLA_DK), F32)),
        grid=(GLA_HEADS, n_blocks),
        in_specs=in_specs,
        out_specs=(pl.BlockSpec((rows, GLA_DK), lambda h, s: (tix(s), h)),
                   pl.BlockSpec((rows, GLA_DK), lambda h, s: (tix(s), h)),
                   pl.BlockSpec((rows, GLA_DV), lambda h, s: (tix(s), h)),
                   pl.BlockSpec((rows, GLA_DK), lambda h, s: (tix(s), h))),
        scratch_shapes=[pltpu.VMEM((GLA_DV, GLA_DK), F32), pltpu.VMEM((n_chunks, GLA_DV, GLA_DK), F32),
                        pltpu.VMEM((n_chunks, SUBLANES, GLA_DK), F32),
                        pltpu.VMEM((n_chunks, GLA_DV, GLA_DK), F32)],
        compiler_params=_params("parallel", "arbitrary"),
    )(*args)


def gla_gate_bwd(proj, dz_f, dz_b, wg_pad, name):
    rows_total = proj.shape[0]
    tm = min(ROW_TILE, rows_total)
    n_key = GLA_HEADS * GLA_DK

    def body(lr_ref, dzf_ref, dzb_ref, wg_ref, dlr_ref, dwg_ref, dbg_ref):
        step = pl.program_id(0)
        lr_t = jnp.transpose(lr_ref[...])
        dzf, dzb = dzf_ref[...], dzb_ref[...]
        dzf16, dzb16 = dzf.astype(BF16), dzb.astype(BF16)
        dlr_ref[...] = (_dot_nt(dzf16, wg_ref[0]) + _dot_nt(dzb16, wg_ref[1])).astype(BF16)
        dwf = _dot(lr_t[0:GLA_RANK].astype(BF16), dzf16)
        dwb = _dot(lr_t[GLA_RANK:2 * GLA_RANK].astype(BF16), dzb16)
        dbg = jnp.concatenate([_colsum(dzf), _colsum(dzb)], axis=0)

        @pl.when(step == 0)
        def _():
            dwg_ref[0] = dwf
            dwg_ref[1] = dwb
            dbg_ref[...] = dbg

        @pl.when(step > 0)
        def _():
            dwg_ref[0] += dwf
            dwg_ref[1] += dwb
            dbg_ref[...] += dbg

    return pl.pallas_call(
        body, name=name,
        out_shape=(jax.ShapeDtypeStruct((rows_total, LANES), BF16), jax.ShapeDtypeStruct((2, GLA_RANK, n_key), F32),
                   jax.ShapeDtypeStruct((2, n_key), F32)),
        grid=(rows_total // tm,),
        in_specs=[pl.BlockSpec((tm, LANES), lambda i: (i, LR_COL // LANES)),
                  pl.BlockSpec((tm, n_key), lambda i: (i, 0)), pl.BlockSpec((tm, n_key), lambda i: (i, 0)),
                  _full((2, LANES, n_key))],
        out_specs=(pl.BlockSpec((tm, LANES), lambda i: (i, 0)), _full((2, GLA_RANK, n_key)), _full((2, n_key))),
        compiler_params=_params("arbitrary"),
    )(proj, dz_f, dz_b, wg_pad)


def _head_norm(o, gain):
    outs, hats, rstds = [], [], []
    for h in range(GLA_HEADS):
        oh = o[:, h * GLA_DV:(h + 1) * GLA_DV]
        rstd = lax.rsqrt(jnp.mean(oh * oh, axis=-1, keepdims=True) + NORM_EPS)
        hat = oh * rstd
        outs.append(hat * gain)
        hats.append(hat)
        rstds.append(rstd)
    return outs, hats, rstds


def odd_out_fwd(o_f, o_b, proj, head_gain, w_out, gain, x1, target, name):
    rows, d = x1.shape
    tm = min(ROW_TILE, rows)
    r_block = (2 * GLA_HEADS * GLA_DK + GLA_HEADS * GLA_DV) // d

    def body(of_ref, ob_ref, r_ref, hg_ref, w_ref, g_ref, x1_ref, tgt_ref, y2_ref, dy_ref, dx2_ref, loss_ref,
             dg_ref):
        step = pl.program_id(0)
        on, _, _ = _head_norm(of_ref[...] + ob_ref[...], hg_ref[...])
        r = r_ref[...]
        y2 = (jnp.concatenate(on, axis=1) * (r * _sigmoid(r))).astype(BF16)
        y2_ref[...] = y2
        y = _dot(y2, w_ref[...])
        gain_v = g_ref[...]
        rstd = lax.rsqrt(jnp.mean(y * y, axis=-1, keepdims=True) + NORM_EPS)
        x2 = x1_ref[...] + y * rstd * gain_v
        diff = x2 - tgt_ref[...]
        loss = 0.5 * jnp.sum(jnp.mean(diff * diff, axis=-1, keepdims=True), axis=0, keepdims=True)
        dx2 = diff * (1.0 / d)
        dx2_ref[...] = dx2
        dy, dg_rows = _rmsnorm_bwd(dx2, y, gain_v)
        dy_ref[...] = dy.astype(BF16)
        _accumulate(loss_ref, jnp.broadcast_to(loss, loss_ref.shape), step)
        _accumulate(dg_ref, _colsum(dg_rows), step)

    row = lambda n, col=0: pl.BlockSpec((tm, n), lambda i: (i, col))
    return pl.pallas_call(
        body, name=name,
        out_shape=(jax.ShapeDtypeStruct((rows, d), BF16), jax.ShapeDtypeStruct((rows, d), BF16),
                   jax.ShapeDtypeStruct((rows, d), F32), jax.ShapeDtypeStruct((SUBLANES, LANES), F32),
                   jax.ShapeDtypeStruct((1, d), F32)),
        grid=(rows // tm,),
        in_specs=[row(d), row(d), row(d, r_block), _full((1, GLA_DV)), _full((d, d)), _full((1, d)), row(d), row(d)],
        out_specs=(row(d), row(d), row(d), _full((SUBLANES, LANES)), _full((1, d))),
        compiler_params=_params("arbitrary"),
    )(o_f, o_b, proj, head_gain, w_out, gain, x1, target)


def odd_out_bwd(dy, w_out, o_f, o_b, proj, head_gain, name):
    rows, d = dy.shape
    tm = min(ROW_TILE, rows)
    r_block = (2 * GLA_HEADS * GLA_DK + GLA_HEADS * GLA_DV) // d

    def body(dy_ref, w_ref, of_ref, ob_ref, r_ref, hg_ref, dr_ref, do_ref, dhg_ref):
        dy2 = _dot_nt(dy_ref[...], w_ref[...])
        hg = hg_ref[...]
        on, hats, rstds = _head_norm(of_ref[...] + ob_ref[...], hg)
        r = r_ref[...]
        sr = _sigmoid(r)
        dr_ref[...] = (dy2 * jnp.concatenate(on, axis=1) * (sr * (1.0 + r * (1.0 - sr)))).astype(BF16)
        d_on = dy2 * (r * sr)
        d_os, dhg = [], None
        for h in range(GLA_HEADS):
            dn = d_on[:, h * GLA_DV:(h + 1) * GLA_DV]
            part = _colsum(dn * hats[h])
            dhg = part if dhg is None else dhg + part
            dng = dn * hg
            d_os.append(rstds[h] * (dng - hats[h] * jnp.mean(dng * hats[h], axis=-1, keepdims=True)))
        do_ref[...] = jnp.concatenate(d_os, axis=1)
        _accumulate(dhg_ref, dhg, pl.program_id(0))

    row = lambda n, col=0: pl.BlockSpec((tm, n), lambda i: (i, col))
    return pl.pallas_call(
        body, name=name,
        out_shape=(jax.ShapeDtypeStruct((rows, d), BF16), jax.ShapeDtypeStruct((rows, d), F32),
                   jax.ShapeDtypeStruct((1, GLA_DV), F32)),
        grid=(rows // tm,),
        in_specs=[row(d), _full((d, d)), row(d), row(d), row(d, r_block), _full((1, GLA_DV))],
        out_specs=(row(d), row(d), _full((1, GLA_DV))),
        compiler_params=_params("arbitrary"),
    )(dy, w_out, o_f, o_b, proj, head_gain)


def local_step(x, target, w, reduce_first=None, reduce_second=None, late_weights=None):
    g, g16 = {}, {}
    proj_e, h0 = norm_matmul(x, w["even_norm_pre"], w["even_w_in"], "even_in_proj")
    h_dir, acts = zip(*[rglru_fwd(proj_e, w["rg_conv_w"], w["rg_conv_b"], w["rg_gate_w"][d], w["rg_gate_b"][d],
                                  w["rg_lambda"][d], d == 1, "rglru_fwd_%d" % d) for d in range(2)])
    ycat = even_mix_fwd(proj_e, h_dir[0], h_dir[1], w["sc_conv_w"], "even_mix_fwd")
    if late_weights is not None:
        w = dict(w, **late_weights(ycat))
    x1, y_e = even_out_fwd(ycat, w["even_w_out"], w["even_norm_post"], x, "even_out_fwd")
    proj_o, h1 = norm_matmul(x1, w["odd_norm_pre"], w["odd_w_in"], "odd_in_proj")
    o_dir, st_dir = [], []
    for d in range(2):
        o, st = gla_fwd(proj_o, w["gla_wg_pad"], w["gla_b_gate"], d == 1, "gla_fwd_%d" % d)
        o_dir.append(o)
        st_dir.append(st)
    y2, dy_o, dx2, loss, g["odd_norm_post"] = odd_out_fwd(
        o_dir[0], o_dir[1], proj_o, w["gla_norm_g"], w["odd_w_out"], w["odd_norm_post"], x1, target, "odd_out_fwd")
    g["odd_w_out"], g16["odd_w_out"] = (a[0] for a in matmul_dw(y2, dy_o, D_MODEL, "odd_w_out_grad"))
    dr, d_o, g["gla_norm_g"] = odd_out_bwd(dy_o, w["odd_w_out"], o_dir[0], o_dir[1], proj_o, w["gla_norm_g"],
                                           "odd_out_bwd")
    dq, dk, dv, dz_f = gla_bwd(proj_o, w["gla_wg_pad"], w["gla_b_gate"], d_o, st_dir[0], None, False, "gla_bwd_0")
    dq, dk, dv, dz_b = gla_bwd(proj_o, w["gla_wg_pad"], w["gla_b_gate"], d_o, st_dir[1], (dq, dk, dv), True,
                               "gla_bwd_1")
    dlr, g["gla_w_gate_lr"], g["gla_b_gate"] = gla_gate_bwd(proj_o, dz_f, dz_b, w["gla_wg_pad"], "gla_gate_bwd")
    dproj_o = [dq, dk, dv, dr, dlr]
    g["odd_w_in"] = jnp.concatenate(matmul_dw_pieces(h1, dproj_o, "odd_w_in_grad"), axis=1)[:, :ODD_IN]
    dx1, g["odd_norm_pre"] = inproj_bwd_pieces(dproj_o, w["odd_w_in"], x1, w["odd_norm_pre"], dx2, "odd_in_proj_bwd")
    dy_e, dycat, g["even_norm_post"] = even_out_bwd(dx1, y_e, w["even_norm_post"], w["even_w_out"], "even_out_bwd")
    g["even_w_out"], g16["even_w_out"] = (a[0] for a in matmul_dw(ycat, dy_e, D_MODEL, "even_w_out_grad"))
    lam = w["rg_lambda"] if reduce_first is None else w["rg_lambda"] + reduce_first(g, g16)
    dua, dgw, dgb, dlam = None, [], [], []
    for d in range(2):
        a, b, c, e = rglru_bwd(proj_e, dycat, h_dir[d], acts[d], w["rg_gate_w"][d], lam[d], dua, d == 1,
                               "rglru_bwd_%d" % d)
        dua = a
        dgw.append(b)
        dgb.append(c)
        dlam.append(e)
    dproj_e, g["rg_conv_w"], g["rg_conv_b"], g["sc_conv_w"] = even_mix_bwd(
        proj_e, dycat, h_dir[0], h_dir[1], dua, w["rg_conv_w"], w["sc_conv_w"], "even_mix_bwd")
    dgw = jnp.stack(dgw).reshape(2, RG_HEADS, RG_HEAD_DIM, 2, RG_HEAD_DIM)
    g["rg_gate_w"] = jnp.transpose(dgw, (0, 3, 1, 2, 4))
    g["rg_gate_b"] = jnp.stack(dgb).reshape(2, 2, RG_HEADS, RG_HEAD_DIM)
    g["rg_lambda"] = jnp.concatenate(dlam, axis=0)
    g["even_w_in"], g16["even_w_in"] = matmul_dw(h0, dproj_e, EVEN_IN // 4, "even_w_in_grad")
    gain = w["even_norm_pre"] if reduce_second is None else w["even_norm_pre"] + reduce_second(g, g16)
    grad_x, g["even_norm_pre"] = inproj_bwd(dproj_e, w["even_w_in"], x, gain, dx1, "even_in_proj_bwd")
    return loss, grad_x, g


def _prepare_weights(full):
    w = {}
    for name in ("even_norm_pre", "even_norm_post", "rg_conv_b", "odd_norm_pre", "odd_norm_post", "gla_norm_g"):
        if name in full:
            w[name] = full[name].reshape(1, -1)
    for name in ("rg_conv_w", "sc_conv_w"):
        if name in full:
            w[name] = full[name]
    for name in ("even_w_out", "odd_w_out"):
        if name in full:
            w[name] = full[name].astype(BF16)
    if "even_w_in" in full:
        w["even_w_in"] = full["even_w_in"].astype(BF16)
        if w["even_w_in"].ndim == 2:
            w["even_w_in"] = jnp.transpose(w["even_w_in"].reshape(D_MODEL, 4, EVEN_IN // 4), (1, 0, 2))
    if "rg_gate_w" in full:
        gw = jnp.transpose(full["rg_gate_w"].astype(BF16), (0, 2, 3, 1, 4))
        w["rg_gate_w"] = gw.reshape(2, RG_HEADS, RG_HEAD_DIM, 2 * RG_HEAD_DIM)
        w["rg_gate_b"] = full["rg_gate_b"].reshape(2, 2, D_MODEL)
        w["rg_lambda"] = full["rg_lambda"].reshape(2, 1, D_MODEL)
    if "odd_w_in" in full:
        w_in = jnp.pad(full["odd_w_in"].astype(BF16), ((0, 0), (0, ODD_IN_PAD - ODD_IN)))
        w["odd_w_in"] = w_in.reshape(1, D_MODEL, ODD_IN_PAD)
    if "gla_w_gate_lr" in full:
        wg = full["gla_w_gate_lr"].astype(BF16)
        w["gla_wg_pad"] = jnp.stack([jnp.pad(wg[d], ((d * GLA_RANK, LANES - (d + 1) * GLA_RANK), (0, 0)))
                                     for d in range(2)])
        w["gla_b_gate"] = full["gla_b_gate"].reshape(2, 1, GLA_HEADS * GLA_DK)
    return w


SHARDED_SMALL = (("rg_conv_w", (4, 256)), ("rg_lambda", (2, 256)), ("sc_conv_w", (3, 256)),
                 ("odd_norm_pre", (256,)), ("odd_norm_post", (256,)), ("gla_w_gate_lr", (2, 16, 128)),
                 ("gla_b_gate", (2, 128)), ("gla_norm_g", (64,)))
SHARDED_ROWS = 96
REPLICATED = (("rg_gate_w", (2, 2, 8, 128, 128)), ("even_norm_post", (1024,)), ("rg_conv_b", (1024,)),
              ("rg_gate_b", (2, 2, 8, 128)))
GATE_ROWS = 4096
LAST_REPLICATED = (("even_norm_pre", (1024,)),)
LAST_ROWS = 8
REPLICATED_ROWS = 4160
REP_PART = REPLICATED_ROWS // 8
HALF_SHARDED = SHARDED_ROWS // 2
PACK_HALF = HALF_SHARDED + REP_PART


def _seg_rows(shape):
    n = 1
    for s in shape:
        n *= s
    return -(-n // (SUBLANES * LANES)) * SUBLANES


def _pack(arrays, spec, total_rows, lead=()):
    parts = []
    for name, shape in spec:
        flat = arrays[name].reshape(lead + (-1,))
        pad = _seg_rows(shape) * LANES - flat.shape[-1]
        if pad:
            flat = jnp.pad(flat, [(0, 0)] * len(lead) + [(0, pad)])
        parts.append(flat.reshape(lead + (-1, LANES)))
    rows = jnp.concatenate(parts, axis=len(lead))
    pad = total_rows - rows.shape[len(lead)]
    return jnp.pad(rows, [(0, 0)] * len(lead) + [(0, pad), (0, 0)])


def _unpack(rows, spec, lead=()):
    out, at = {}, 0
    for name, shape in spec:
        n = 1
        for s in shape:
            n *= s
        k = _seg_rows(shape)
        seg = lax.slice_in_dim(rows, at, at + k, axis=len(lead)).reshape(lead + (-1,))
        out[name] = lax.slice_in_dim(seg, 0, n, axis=len(lead)).reshape(lead + shape)
        at += k
    return out


def _split_owners(arr):
    a = arr.reshape(arr.shape[:-1] + (4, arr.shape[-1] // 4))
    return jnp.moveaxis(a, -2, 0)


def _merge_owners(arr):
    a = jnp.moveaxis(arr, 0, -2)
    return a.reshape(a.shape[:-2] + (-1,))


HBM_SPEC = pl.BlockSpec(memory_space=pltpu.HBM)


def _position():
    x, y, c = lax.axis_index("x"), lax.axis_index("y"), lax.axis_index("c")
    chips = [(1 - x, y), (x, 1 - y), (1 - x, 1 - y)]
    return x, y, c, chips


def _remote(src, dst, send_sem, recv_sem, device):
    return pltpu.make_async_remote_copy(src_ref=src, dst_ref=dst, send_sem=send_sem, recv_sem=recv_sem,
                                        device_id=device, device_id_type=MESH)


SEM_SPEC = pl.BlockSpec(memory_space=pltpu.SEMAPHORE)
SIDE_EFFECT = pltpu.SideEffectType.DATAFLOW_SIDE_EFFECTING


def _gather_copies(ins, lands, n_h, send_sems, recv_sems):
    x, y, c, chips = _position()
    me = 2 * x + y
    copies = []
    for a in range(len(ins)):
        for k, chip in enumerate(chips):
            src = ins[a].at[c] if a < n_h else ins[a]
            dst = lands[a].at[me, c] if a < n_h else lands[a].at[me]
            copies.append(_remote(src, dst, send_sems.at[3 * a + k], recv_sems.at[3 * a + k], (chip[0], chip[1], c)))
    return copies


def gather_start(halved, whole, name):
    arrays = list(halved) + list(whole)
    n, n_h = len(arrays), len(halved)
    lands = [lax.empty((4,) + a.shape, a.dtype) for a in arrays]

    def body(*refs):
        ins, lz, send_sems, recv_sems, token = refs[:n], refs[n:2 * n], refs[2 * n], refs[2 * n + 1], refs[-1]
        for cp in _gather_copies(ins, lz, n_h, send_sems, recv_sems):
            cp.start()
        token[...] = jnp.zeros_like(token)

    operands = [pltpu.with_memory_space_constraint(a, pltpu.HBM) for a in arrays + lands]
    return pl.pallas_call(
        body, name=name,
        out_shape=(pltpu.SemaphoreType.DMA((3 * n,)), pltpu.SemaphoreType.DMA((3 * n,)))
        + tuple(pltpu.HBM(a.shape, a.dtype) for a in operands) + (jax.ShapeDtypeStruct((SUBLANES, LANES), F32),),
        in_specs=[HBM_SPEC] * (2 * n),
        out_specs=(SEM_SPEC, SEM_SPEC) + (HBM_SPEC,) * (2 * n) + (pl.BlockSpec(memory_space=pltpu.VMEM),),
        input_output_aliases={i: 2 + i for i in range(2 * n)},
        compiler_params=pltpu.CompilerParams(has_side_effects=SIDE_EFFECT),
    )(*operands)


def gather_wait(started, n_h, after, name):
    send_sems, recv_sems = started[0], started[1]
    operands = list(started[2:-1])
    n = len(operands) // 2

    def body(*refs):
        ins, lz, send_ref, recv_ref = refs[:n], refs[n:2 * n], refs[2 * n], refs[2 * n + 1]
        for cp in _gather_copies(ins, lz, n_h, send_ref, recv_ref):
            cp.wait_send()
            cp.wait_recv()

    outs = pl.pallas_call(
        body, name=name,
        out_shape=tuple(pltpu.HBM(a.shape, a.dtype) for a in operands),
        in_specs=[HBM_SPEC] * (2 * n) + [SEM_SPEC, SEM_SPEC, pl.BlockSpec(memory_space=pl.ANY)],
        out_specs=(HBM_SPEC,) * (2 * n),
        input_output_aliases={i: i for i in range(2 * n)},
        compiler_params=pltpu.CompilerParams(has_side_effects=SIDE_EFFECT),
    )(*operands, send_sems, recv_sems, after)
    return outs[n:]


def pass_to_sibling(fulls, name):
    n = len(fulls)

    def body(*refs):
        bufs = refs[n:2 * n]
        send_sems, recv_sems = refs[2 * n:]
        x, y, c, chips = _position()
        sibling = (x, y, 1 - c)
        copies = []
        for a in range(n):
            for k, chip in enumerate(chips):
                q = 2 * chip[0] + chip[1]
                cp = _remote(bufs[a].at[q, c], bufs[a].at[q, c], send_sems.at[3 * a + k], recv_sems.at[3 * a + k],
                             sibling)
                cp.start()
                copies.append(cp)
        for a in range(n):
            for k, chip in enumerate(chips):
                q = 2 * chip[0] + chip[1]
                passed = bufs[a].at[q, 1 - c]
                _remote(passed, passed, send_sems.at[3 * a + k], recv_sems.at[3 * a + k], sibling).wait_recv()
        for cp in copies:
            cp.wait_send()

    return pl.pallas_call(
        body, name=name,
        out_shape=[jax.ShapeDtypeStruct(a.shape, a.dtype) for a in fulls],
        in_specs=[HBM_SPEC] * n, out_specs=[HBM_SPEC] * n,
        input_output_aliases={i: i for i in range(n)},
        scratch_shapes=[pltpu.SemaphoreType.DMA((3 * n,)), pltpu.SemaphoreType.DMA((3 * n,))],
    )(*fulls)


def place_own(full, own, chip, name):
    _, _, r, cols = full.shape
    tr = _row_tile(r, cols)

    def body(p_ref, own_ref, full_ref, o_ref):
        o_ref[0] = own_ref[...]

    return pl.pallas_call(
        body, name=name,
        out_shape=jax.ShapeDtypeStruct(full.shape, full.dtype),
        grid_spec=pltpu.PrefetchScalarGridSpec(
            num_scalar_prefetch=1, grid=(2, r // tr),
            in_specs=[pl.BlockSpec((1, tr, cols), lambda h, i, p_ref: (h, i, 0)), pl.BlockSpec(memory_space=pl.ANY)],
            out_specs=pl.BlockSpec((1, 1, tr, cols), lambda h, i, p_ref: (p_ref[0], h, i, 0))),
        input_output_aliases={2: 0},
        compiler_params=_params("parallel", "parallel"),
    )(chip, own, full)


def exchange_with_sibling(arrays, name):
    n = len(arrays)

    def body(*refs):
        ins, outs = refs[:n], refs[n:2 * n]
        send_sems, recv_sems = refs[2 * n:]
        x, y, c, _ = _position()
        copies = []
        for a in range(n):
            cp = _remote(ins[a].at[:, 1 - c], outs[a], send_sems.at[a], recv_sems.at[a], (x, y, 1 - c))
            cp.start()
            copies.append(cp)
        for cp in copies:
            cp.wait()

    return pl.pallas_call(
        body, name=name,
        out_shape=[jax.ShapeDtypeStruct((a.shape[0],) + a.shape[2:], a.dtype) for a in arrays],
        in_specs=[HBM_SPEC] * n, out_specs=[HBM_SPEC] * n,
        scratch_shapes=[pltpu.SemaphoreType.DMA((n,)), pltpu.SemaphoreType.DMA((n,))],
    )(*arrays)


def _chip_copies(ins, lands, send_sems, recv_sems):
    x, y, c, chips = _position()
    copies = []
    for a in range(len(ins)):
        for k, chip in enumerate(chips):
            q = 2 * chip[0] + chip[1]
            copies.append(_remote(ins[a].at[q], lands[a].at[k], send_sems.at[3 * a + k], recv_sems.at[3 * a + k],
                                  (chip[0], chip[1], c)))
    return copies


def exchange_with_chips_start(arrays, name):
    n = len(arrays)
    lands = [lax.empty((3,) + a.shape[1:], a.dtype) for a in arrays]

    def body(*refs):
        ins, lz, send_sems, recv_sems, token = refs[:n], refs[n:2 * n], refs[2 * n], refs[2 * n + 1], refs[-1]
        for cp in _chip_copies(ins, lz, send_sems, recv_sems):
            cp.start()
        token[...] = jnp.zeros_like(token)

    operands = [pltpu.with_memory_space_constraint(a, pltpu.HBM) for a in list(arrays) + lands]
    return pl.pallas_call(
        body, name=name,
        out_shape=(pltpu.SemaphoreType.DMA((3 * n,)), pltpu.SemaphoreType.DMA((3 * n,)))
        + tuple(pltpu.HBM(a.shape, a.dtype) for a in operands) + (jax.ShapeDtypeStruct((SUBLANES, LANES), F32),),
        in_specs=[HBM_SPEC] * (2 * n),
        out_specs=(SEM_SPEC, SEM_SPEC) + (HBM_SPEC,) * (2 * n) + (pl.BlockSpec(memory_space=pltpu.VMEM),),
        input_output_aliases={i: 2 + i for i in range(2 * n)},
        compiler_params=pltpu.CompilerParams(has_side_effects=SIDE_EFFECT),
    )(*operands)


def exchange_with_chips_wait(started, after, name):
    send_sems, recv_sems = started[0], started[1]
    operands = list(started[2:-1])
    n = len(operands) // 2

    def body(*refs):
        ins, lz, send_ref, recv_ref = refs[:n], refs[n:2 * n], refs[2 * n], refs[2 * n + 1]
        for cp in _chip_copies(ins, lz, send_ref, recv_ref):
            cp.wait_send()
            cp.wait_recv()

    outs = pl.pallas_call(
        body, name=name,
        out_shape=tuple(pltpu.HBM(a.shape, a.dtype) for a in operands),
        in_specs=[HBM_SPEC] * (2 * n) + [SEM_SPEC, SEM_SPEC, pl.BlockSpec(memory_space=pl.ANY)],
        out_specs=(HBM_SPEC,) * (2 * n),
        input_output_aliases={i: i for i in range(2 * n)},
        compiler_params=pltpu.CompilerParams(has_side_effects=SIDE_EFFECT),
    )(*operands, send_sems, recv_sems, after)
    return outs[:n], outs[n:]


def share_totals(totals, pack_total, last_part):
    arrays = list(totals) + [pack_total]
    n = len(arrays)

    def body(*refs):
        ins, last, outs, rep, last_all = refs[:n], refs[n], refs[n + 1:2 * n + 1], refs[2 * n + 1], refs[2 * n + 2]
        send_sems, recv_sems, rep_send, rep_recv, last_send, last_recv = refs[2 * n + 3:]
        x, y, c, chips = _position()
        sibling = (x, y, 1 - c)
        me = 4 * x + 2 * y + c
        sends = []
        for a in range(n):
            cp = _remote(ins[a], outs[a], send_sems.at[a], recv_sems.at[a], sibling)
            cp.start()
            sends.append(cp)
        mine = ins[n - 1].at[pl.ds(HALF_SHARDED, REP_PART)]
        peers = [sibling]
        for chip in chips:
            peers += [(chip[0], chip[1], c), (chip[0], chip[1], 1 - c)]
        for j, peer in enumerate(peers):
            for src, dst, s_sem, r_sem in ((mine, rep, rep_send, rep_recv), (last, last_all, last_send, last_recv)):
                cp = _remote(src, dst.at[me], s_sem.at[j], r_sem.at[j], peer)
                cp.start()
                sends.append(cp)
        for a in range(n):
            _remote(outs[a], outs[a], send_sems.at[a], recv_sems.at[a], sibling).wait_recv()
        for j, peer in enumerate(peers):
            it = 4 * peer[0] + 2 * peer[1] + peer[2]
            _remote(rep.at[it], rep.at[it], rep_send.at[j], rep_recv.at[j], peer).wait_recv()
            _remote(last_all.at[it], last_all.at[it], last_send.at[j], last_recv.at[j], peer).wait_recv()
        for cp in sends:
            cp.wait_send()

    outs = pl.pallas_call(
        body, name="grad_share_totals",
        out_shape=[jax.ShapeDtypeStruct(a.shape, a.dtype) for a in arrays]
        + [jax.ShapeDtypeStruct((8, REP_PART, LANES), F32), jax.ShapeDtypeStruct((8,) + last_part.shape, F32)],
        in_specs=[HBM_SPEC] * (n + 1), out_specs=[HBM_SPEC] * (n + 2),
        scratch_shapes=[pltpu.SemaphoreType.DMA((n,)), pltpu.SemaphoreType.DMA((n,))]
        + [pltpu.SemaphoreType.DMA((7,))] * 4,
    )(*arrays, last_part)
    return outs[:n], outs[n], outs[n + 1]


def sum_parts(parts, name):
    def body(p_ref, o_ref):
        total = p_ref[0]
        for k in range(1, parts.shape[0]):
            total = total + p_ref[k]
        o_ref[...] = total

    return pl.pallas_call(body, name=name, out_shape=jax.ShapeDtypeStruct(parts.shape[1:], parts.dtype))(parts)


TILE_BYTES = 2 << 20


def _row_tile(rows, cols):
    best = None
    for t in range(SUBLANES, rows + 1, SUBLANES):
        if rows % t == 0 and t * cols * 4 <= TILE_BYTES:
            best = t
    return best if best is not None else rows


def add_sibling(mine, received, core, out_dtype, name):
    _, _, r, cols = mine.shape
    tr = _row_tile(r, cols)

    def body(c_ref, a_ref, b_ref, o_ref):
        o_ref[...] = (a_ref[0] + b_ref[...].astype(F32)).astype(out_dtype)

    return pl.pallas_call(
        body, name=name,
        out_shape=jax.ShapeDtypeStruct((4, r, cols), out_dtype),
        grid_spec=pltpu.PrefetchScalarGridSpec(
            num_scalar_prefetch=1, grid=(4, r // tr),
            in_specs=[pl.BlockSpec((1, 1, tr, cols), lambda o, i, c_ref: (o, c_ref[0], i, 0)),
                      pl.BlockSpec((1, tr, cols), lambda o, i, c_ref: (o, i, 0))],
            out_specs=pl.BlockSpec((1, tr, cols), lambda o, i, c_ref: (o, i, 0))),
        compiler_params=_params("parallel", "parallel"),
    )(core, mine, received)


def add_chips(own, received, chip, name):
    _, r, cols = own.shape
    tr = _row_tile(r, cols)

    def body(p_ref, a_ref, b0, b1, b2, o_ref):
        o_ref[...] = ((a_ref[0].astype(F32) + b0[0].astype(F32)) + b1[0].astype(F32)) + b2[0].astype(F32)

    rb = lambda k: pl.BlockSpec((1, tr, cols), lambda i, p_ref: (k, i, 0))
    return pl.pallas_call(
        body, name=name,
        out_shape=jax.ShapeDtypeStruct((r, cols), F32),
        grid_spec=pltpu.PrefetchScalarGridSpec(
            num_scalar_prefetch=1, grid=(r // tr,),
            in_specs=[pl.BlockSpec((1, tr, cols), lambda i, p_ref: (p_ref[0], i, 0)), rb(0), rb(1), rb(2)],
            out_specs=pl.BlockSpec((tr, cols), lambda i, p_ref: (i, 0))),
        compiler_params=_params("parallel"),
    )(chip, own, received, received, received)


def _adamw_update(gv, w_ref, m_ref, v_ref, d_ref, nm_ref, nv_ref):
    nm = ADAM_B1 * m_ref[...] + (1.0 - ADAM_B1) * gv
    nv = ADAM_B2 * v_ref[...] + (1.0 - ADAM_B2) * (gv * gv)
    nm_ref[...] = nm
    nv_ref[...] = nv
    m_hat = nm / (1.0 - ADAM_B1 ** ADAM_STEP)
    v_hat = nv / (1.0 - ADAM_B2 ** ADAM_STEP)
    d_ref[...] = -ADAM_LR * (m_hat / (jnp.sqrt(v_hat) + ADAM_EPS) + ADAM_WD * w_ref[...])


def adamw_halves(w, own, received, m, v, core, name, by_columns=False):
    rows, cols = w.shape

    def body(c_ref, w_ref, own_ref, rec_ref, m_ref, v_ref, g_ref, d_ref, nm_ref, nv_ref):
        gv = jnp.where(pl.program_id(0) == c_ref[0], own_ref[...], rec_ref[...])
        g_ref[...] = gv
        _adamw_update(gv, w_ref, m_ref, v_ref, d_ref, nm_ref, nv_ref)

    if by_columns:
        nr = 1
        whole = pl.BlockSpec((rows, cols // 2), lambda h, i, c_ref: (0, h))
        half = pl.BlockSpec((rows, cols // 2), lambda h, i, c_ref: (0, 0))
    else:
        r = rows // 2
        tr = _row_tile(r, cols)
        nr = r // tr
        whole = pl.BlockSpec((tr, cols), lambda h, i, c_ref: (h * nr + i, 0))
        half = pl.BlockSpec((tr, cols), lambda h, i, c_ref: (i, 0))
    return pl.pallas_call(
        body, name=name,
        out_shape=(jax.ShapeDtypeStruct((rows, cols), F32),) * 4,
        grid_spec=pltpu.PrefetchScalarGridSpec(
            num_scalar_prefetch=1, grid=(2, nr),
            in_specs=[whole, half, half, whole, whole], out_specs=(whole,) * 4),
        compiler_params=_params("parallel", "parallel"),
    )(core, w, own, received, m, v)


def adamw_many(ws, gs, ms, vs, name):
    n = len(ws)

    def body(*refs):
        ins, outs = refs[:4 * n], refs[4 * n:]
        for k in range(n):
            w_ref, g_ref, m_ref, v_ref = (ins[j * n + k] for j in range(4))
            d_ref, nm_ref, nv_ref = outs[3 * k:3 * k + 3]
            _adamw_update(g_ref[...], w_ref, m_ref, v_ref, d_ref, nm_ref, nv_ref)

    flat = pl.pallas_call(
        body, name=name,
        out_shape=[jax.ShapeDtypeStruct(w.shape, F32) for w in ws for _ in range(3)],
    )(*ws, *gs, *ms, *vs)
    return [tuple(flat[3 * k:3 * k + 3]) for k in range(n)]


def adamw(w, g, m, v, name):
    r, cols = w.shape
    tr = _row_tile(r, cols)

    def body(w_ref, g_ref, m_ref, v_ref, g_out, d_ref, nm_ref, nv_ref):
        gv = g_ref[...]
        g_out[...] = gv
        _adamw_update(gv, w_ref, m_ref, v_ref, d_ref, nm_ref, nv_ref)

    blk = pl.BlockSpec((tr, cols), lambda i: (i, 0))
    return pl.pallas_call(
        body, name=name,
        out_shape=(jax.ShapeDtypeStruct((r, cols), F32),) * 4,
        grid=(r // tr,),
        in_specs=[blk] * 4, out_specs=(blk,) * 4,
        compiler_params=_params("parallel"),
    )(w, g, m, v)


WEIGHTS = ("even_norm_pre", "even_norm_post", "even_w_in", "rg_conv_w", "rg_conv_b", "rg_gate_w", "rg_gate_b",
           "rg_lambda", "sc_conv_w", "even_w_out", "odd_norm_pre", "odd_norm_post", "odd_w_in", "gla_w_gate_lr",
           "gla_b_gate", "gla_norm_g", "odd_w_out")
BIG = ("even_w_in", "even_w_out", "odd_w_in", "odd_w_out")


def _halves(a):
    return a.reshape((2, a.shape[0] // 2) + a.shape[1:])


def kernel(x, even_norm_pre, even_norm_post, even_w_in, rg_conv_w, rg_conv_b, rg_gate_w, rg_gate_b, rg_lambda, sc_conv_w, even_w_out, odd_norm_pre, odd_norm_post, odd_w_in, gla_w_gate_lr, gla_b_gate, gla_norm_g, odd_w_out, loss_target, m_even_norm_pre, m_even_norm_post, m_even_w_in, m_rg_conv_w, m_rg_conv_b, m_rg_gate_w, m_rg_gate_b, m_rg_lambda, m_sc_conv_w, m_even_w_out, m_odd_norm_pre, m_odd_norm_post, m_odd_w_in, m_gla_w_gate_lr, m_gla_b_gate, m_gla_norm_g, m_odd_w_out, v_even_norm_pre, v_even_norm_post, v_even_w_in, v_rg_conv_w, v_rg_conv_b, v_rg_gate_w, v_rg_gate_b, v_rg_lambda, v_sc_conv_w, v_even_w_out, v_odd_norm_pre, v_odd_norm_post, v_odd_w_in, v_gla_w_gate_lr, v_gla_b_gate, v_gla_norm_g, v_odd_w_out):
    given = dict(locals())
    shard = {n: given[n][0] for n in WEIGHTS}
    m_in = {n: given["m_" + n][0] for n in WEIGHTS}
    v_in = {n: given["v_" + n][0] for n in WEIGHTS}
    mx, my, mc = lax.axis_index("x"), lax.axis_index("y"), lax.axis_index("c")
    core = jnp.reshape(mc, (1,)).astype(jnp.int32)
    chip = jnp.reshape(2 * mx + my, (1,)).astype(jnp.int32)

    small_shard = _pack(shard, SHARDED_SMALL, SHARDED_ROWS)
    big_own = [_halves(shard[n].astype(BF16)) for n in BIG]
    started_a = gather_start(big_own[:1], [small_shard], "gather_start_a")
    started_b = gather_start(big_own[1:], [], "gather_start_b")
    even_w_in_full, small_full = gather_wait(started_a, 1, started_b[-1], "gather_wait_a")
    (even_w_in_full,) = pass_to_sibling([even_w_in_full], "gather_pass_a")
    even_w_in_full = place_own(even_w_in_full, big_own[0], chip, "place_even_w_in")
    small_full = lax.dynamic_update_slice(small_full, small_shard[None], (chip[0], 0, 0))
    full = {n: shard[n] for n, _ in REPLICATED + LAST_REPLICATED}
    full.update({n: _merge_owners(a) for n, a in _unpack(small_full, SHARDED_SMALL, lead=(4,)).items()})
    full["even_w_in"] = even_w_in_full.reshape(4, D_MODEL, EVEN_IN // 4)

    def late_weights(after):
        lands = pass_to_sibling(list(gather_wait(started_b, 3, after, "gather_wait_b")), "gather_pass_b")
        lands = [place_own(a, b, chip, "place_" + n) for a, b, n in zip(lands, big_own[1:], BIG[1:])]
        odd_w_in = jnp.transpose(lands[1].reshape(4, D_MODEL, ODD_IN // 4), (1, 0, 2)).reshape(D_MODEL, ODD_IN)
        return _prepare_weights({"even_w_out": lands[0].reshape(2 * D_MODEL, D_MODEL), "odd_w_in": odd_w_in,
                                 "odd_w_out": lands[2].reshape(D_MODEL, D_MODEL)})

    pending = {}

    def slab(a):
        return a.reshape((4, 2, a.shape[1] // 2) + a.shape[2:])

    def begin(tag, slabs, to_send, dtypes):
        got = exchange_with_sibling(to_send, "grad_sibling_" + tag)
        sums = [add_sibling(a, b, core, dt, "grad_add_sibling_%s%d" % (tag, i))
                for i, (a, b, dt) in enumerate(zip(slabs, got, dtypes))]
        pending[tag] = exchange_with_chips_start(sums, "grad_chips_start_" + tag)
        return pending[tag][-1][0, 0]

    def finish(tag, after):
        sums, got = exchange_with_chips_wait(pending[tag], after, "grad_chips_wait_" + tag)
        return [add_chips(a, b, chip, "grad_add_chips_%s%d" % (tag, i)) for i, (a, b) in enumerate(zip(sums, got))]

    def reduce_first(g, g16):
        odd_w_in = slab(jnp.transpose(g["odd_w_in"].reshape(D_MODEL, 4, ODD_IN // 4), (1, 0, 2)))
        slabs = [odd_w_in] + [slab(g[n].reshape(4, -1, D_MODEL)) for n in ("odd_w_out", "even_w_out")]
        to_send = [odd_w_in.astype(BF16)] + [slab(g16[n].reshape(4, -1, D_MODEL)) for n in ("odd_w_out", "even_w_out")]
        return begin("a", slabs, to_send, [BF16] * 3)

    def reduce_second(g, g16):
        pending["totals_a"] = finish("a", g["even_w_in"])
        rep_rows = _pack(g, REPLICATED, REPLICATED_ROWS).reshape(4, 2, REP_PART, LANES)
        sh_rows = _pack({n: _split_owners(g[n]) for n, _ in SHARDED_SMALL}, SHARDED_SMALL, SHARDED_ROWS, lead=(4,))
        pack = jnp.concatenate([sh_rows.reshape(4, 2, HALF_SHARDED, LANES), rep_rows], axis=2)
        return begin("b", [slab(g["even_w_in"]), pack], [slab(g16["even_w_in"]), pack], [BF16, F32])

    loss, grad_x, g = local_step(x[0], loss_target[0], _prepare_weights(full), reduce_first, reduce_second,
                                 late_weights)
    odd_w_in_t, odd_w_out_t, even_w_out_t = pending["totals_a"]
    even_w_in_t, pack_t = finish("b", grad_x)
    totals = [even_w_in_t, even_w_out_t, odd_w_in_t, odd_w_out_t]
    last_part = jnp.concatenate([_pack(g, LAST_REPLICATED, LAST_ROWS), loss])
    from_core, rep_all, last_all = share_totals(totals, pack_t, last_part)
    me = 2 * chip[0] + core[0]
    mine, theirs = pack_t[:HALF_SHARDED], from_core[4][:HALF_SHARDED]
    sh_total = jnp.where(mc == 0, jnp.concatenate([mine, theirs]), jnp.concatenate([theirs, mine]))
    rep_all = lax.dynamic_update_slice(rep_all, pack_t[None, HALF_SHARDED:], (me, 0, 0))
    rep_total = rep_all.reshape(REPLICATED_ROWS, LANES)
    last_total = sum_parts(lax.dynamic_update_slice(last_all, last_part[None], (me, 0, 0)), "grad_sum_last")
    last_total, loss = last_total[:LAST_ROWS], last_total[LAST_ROWS, 0]
    grads = {}

    delta, new_m, new_v = {}, {}, {}
    for i, n in enumerate(BIG):
        if shard[n].shape[1] % LANES:
            outs = adamw_halves(shard[n].T, totals[i].T, from_core[i].T, m_in[n].T, v_in[n].T, core, "adamw_" + n,
                                by_columns=True)
            grads[n], delta[n], new_m[n], new_v[n] = [o.T for o in outs]
        else:
            grads[n], delta[n], new_m[n], new_v[n] = adamw_halves(shard[n], totals[i], from_core[i], m_in[n],
                                                                  v_in[n], core, "adamw_" + n)
    gate = [src["rg_gate_w"].reshape(GATE_ROWS, LANES) for src in (shard, m_in, v_in)]
    grads["rg_gate_w"], delta["rg_gate_w"], new_m["rg_gate_w"], new_v["rg_gate_w"] = adamw(
        gate[0], rep_total, gate[1], gate[2], "adamw_rg_gate_w")
    rest = REPLICATED[1:]
    rest_rows = sum(_seg_rows(shape) for _, shape in rest)
    grads.update(_unpack(sh_total, SHARDED_SMALL))
    grads.update(_unpack(rep_total[GATE_ROWS:GATE_ROWS + rest_rows], rest))
    grads.update(_unpack(last_total, LAST_REPLICATED))
    names = [n for n, _ in SHARDED_SMALL + rest + LAST_REPLICATED]
    rows_of = lambda a, n: a.reshape(-1, given[n].shape[-1])
    outs = adamw_many([rows_of(given[n], n) for n in names], [rows_of(grads[n], n) for n in names],
                      [rows_of(given["m_" + n], n) for n in names], [rows_of(given["v_" + n], n) for n in names],
                      "adamw_small")
    for n, (d, nm, nv) in zip(names, outs):
        delta[n], new_m[n], new_v[n] = d, nm, nv
    result = [loss, grad_x[None]]
    for group in (grads, delta, new_m, new_v):
        result += [group[n].reshape(given[n].shape) for n in WEIGHTS]
    return tuple(result)
```

```python
import functools

import jax
import jax.numpy as jnp
from jax import lax
from jax.experimental import pallas as pl
from jax.experimental.pallas import tpu as pltpu

F32 = jnp.float32
BF16 = jnp.bfloat16
MESH = pl.DeviceIdType.MESH

D_MODEL = 1024
NORM_EPS = 1e-6
RG_HEADS = 8
RG_HEAD_DIM = 128
RG_C = 8.0
EVEN_IN = 6144
ODD_IN = 3104
ODD_IN_PAD = 3200
GLA_HEADS = 4
GLA_DK = 128
GLA_DV = 256
GLA_RANK = 16
GLA_NORMALIZER = 16.0
GLA_CHUNK = 128
LR_COL = 3072

ADAM_LR = 0.001
ADAM_B1 = 0.9
ADAM_B2 = 0.999
ADAM_EPS = 1e-08
ADAM_WD = 0.01
ADAM_STEP = 10

SUBLANES = 8
HALO = 16
LANES = 128
VMEM_LIMIT = 56 * 2 ** 20

ROW_TILE = 512
SCAN_TILE = 256
GLA_BLOCK = 1024
MIX_TILE = 128


def _params(*sem):
    return pltpu.CompilerParams(dimension_semantics=sem, vmem_limit_bytes=VMEM_LIMIT)


def _full(shape):
    n = len(shape)
    return pl.BlockSpec(shape, lambda *_: (0,) * n)


def _sigmoid(x):
    return 0.5 + 0.5 * jnp.tanh(0.5 * x)


def _softplus(x):
    return jnp.maximum(x, 0.0) + jnp.log(1.0 + jnp.exp(-jnp.abs(x)))


def _dot(a, b):
    return jnp.dot(a, b, preferred_element_type=F32)


def _dot_nt(a, b):
    return lax.dot_general(a, b, (((1,), (1,)), ((), ())), preferred_element_type=F32)


def _dot_tn(a, b):
    return lax.dot_general(a, b, (((0,), (0,)), ((), ())), preferred_element_type=F32)


def _bdot(a, b, ca, cb):
    return lax.dot_general(a, b, (((ca,), (cb,)), ((0,), (0,))), preferred_element_type=F32)


def _halo_specs(rows, cols, col_block, n_row_tiles, tix):
    per = rows // HALO
    last = n_row_tiles * per - 1

    def split(args):
        if len(args) == 2:
            return tix(args[1]), col_block + args[0]
        return tix(args[0]), col_block

    def prev(*args):
        t, c = split(args)
        return (jnp.maximum(t * per - 1, 0), c)

    def main(*args):
        return split(args)

    def nxt(*args):
        t, c = split(args)
        return (jnp.minimum((t + 1) * per, last), c)

    return [pl.BlockSpec((HALO, cols), prev), pl.BlockSpec((rows, cols), main),
            pl.BlockSpec((HALO, cols), nxt)]


def _extend(prev_ref, main_ref, next_ref, is_first, is_last):
    p = jnp.where(is_first, 0.0, prev_ref[...].astype(F32))
    n = jnp.where(is_last, 0.0, next_ref[...].astype(F32))
    return jnp.concatenate([p, main_ref[...].astype(F32), n], axis=0)


def _shifted(ext, offset, rows):
    if offset == 0:
        return ext[HALO:HALO + rows]
    n = ext.shape[0]
    return pltpu.roll(ext, (-offset) % n, 0)[HALO:HALO + rows]


def _conv(ext, w, left, rows):
    out = None
    for k in range(w.shape[0]):
        term = _shifted(ext, k - left, rows) * w[k:k + 1]
        out = term if out is None else out + term
    return out


def _conv_transpose(ext, w, left, rows):
    out = None
    for k in range(w.shape[0]):
        term = _shifted(ext, left - k, rows) * w[k:k + 1]
        out = term if out is None else out + term
    return out


def _colsum(x):
    return jnp.sum(x, axis=0, keepdims=True)


def _accumulate(ref, value, step):
    @pl.when(step == 0)
    def _():
        ref[...] = value

    @pl.when(step > 0)
    def _():
        ref[...] += value


PROJ_TILE_BYTES = 7 * 2 ** 20


def _proj_row_tile(rows, width):
    tm = min(ROW_TILE, rows)
    while tm * width * 4 > PROJ_TILE_BYTES and tm % (2 * SUBLANES) == 0:
        tm //= 2
    return tm


def norm_matmul(x, gain, w, out_dtype, name):
    rows, d = x.shape
    n_col_tiles, _, tn = w.shape
    tm = _proj_row_tile(rows, n_col_tiles * tn)

    def body(x_ref, g_ref, w_ref, proj_ref, h_ref):
        xv = x_ref[...]
        rstd = lax.rsqrt(jnp.mean(xv * xv, axis=-1, keepdims=True) + NORM_EPS)
        hv = (xv * rstd * g_ref[...]).astype(BF16)
        h_ref[...] = hv
        for j in range(n_col_tiles):
            proj_ref[:, j * tn:(j + 1) * tn] = _dot(hv, w_ref[j]).astype(out_dtype)

    row = lambda cols: pl.BlockSpec((tm, cols), lambda i: (i, 0))
    return pl.pallas_call(
        body, name=name,
        out_shape=(jax.ShapeDtypeStruct((rows, n_col_tiles * tn), out_dtype), jax.ShapeDtypeStruct((rows, d), BF16)),
        grid=(rows // tm,),
        in_specs=[row(d), _full((1, d)), _full(w.shape)],
        out_specs=(row(n_col_tiles * tn), row(d)),
        compiler_params=_params("parallel"),
    )(x, gain, w)


def inproj_bwd(dproj, w, x, gain, dres, name):
    rows, d = x.shape
    n_col_tiles, _, tn = w.shape
    tm = _proj_row_tile(rows, n_col_tiles * tn)

    def body(dp_ref, w_ref, x_ref, g_ref, dres_ref, dx_ref, dg_ref):
        dh = None
        for j in range(n_col_tiles):
            part = _dot_nt(dp_ref[:, j * tn:(j + 1) * tn], w_ref[j])
            dh = part if dh is None else dh + part
        _inproj_finish(dh, x_ref, g_ref, dres_ref, dx_ref, dg_ref, pl.program_id(0))

    row = lambda cols: pl.BlockSpec((tm, cols), lambda i: (i, 0))
    return pl.pallas_call(
        body, name=name,
        out_shape=(jax.ShapeDtypeStruct((rows, d), F32), jax.ShapeDtypeStruct((1, d), F32)),
        grid=(rows // tm,),
        in_specs=[row(n_col_tiles * tn), _full(w.shape), row(d), _full((1, d)), row(d)],
        out_specs=(row(d), _full((1, d))),
        compiler_params=_params("arbitrary"),
    )(dproj, w, x, gain, dres)


def _inproj_finish(dh, x_ref, g_ref, dres_ref, dx_ref, dg_ref, step):
    xv = x_ref[...]
    rstd = lax.rsqrt(jnp.mean(xv * xv, axis=-1, keepdims=True) + NORM_EPS)
    xhat = xv * rstd
    dxn = dh * g_ref[...]
    dx_ref[...] = dres_ref[...] + rstd * (dxn - xhat * jnp.mean(dxn * xhat, axis=-1, keepdims=True))
    _accumulate(dg_ref, _colsum(dh * xhat), step)


def inproj_bwd_pieces(pieces, w, x, gain, dres, name):
    rows, d = x.shape
    tm = min(ROW_TILE, rows)
    n = len(pieces)
    widths = [p.shape[1] for p in pieces]
    starts = [sum(widths[:k]) for k in range(n)]
    assert sum(widths) == w.shape[2]

    def body(*refs):
        w_ref, x_ref, g_ref, dres_ref, dx_ref, dg_ref = refs[n:]
        dh = None
        for k in range(n):
            part = _dot_nt(refs[k][...], w_ref[0, :, starts[k]:starts[k] + widths[k]])
            dh = part if dh is None else dh + part
        _inproj_finish(dh, x_ref, g_ref, dres_ref, dx_ref, dg_ref, pl.program_id(0))

    row = lambda cols: pl.BlockSpec((tm, cols), lambda i: (i, 0))
    return pl.pallas_call(
        body, name=name,
        out_shape=(jax.ShapeDtypeStruct((rows, d), F32), jax.ShapeDtypeStruct((1, d), F32)),
        grid=(rows // tm,),
        in_specs=[row(wd) for wd in widths] + [_full(w.shape), row(d), _full((1, d)), row(d)],
        out_specs=(row(d), _full((1, d))),
        compiler_params=_params("arbitrary"),
    )(*pieces, w, x, gain, dres)


def matmul_dw_pieces(a, pieces, name):
    rows, m = a.shape
    tk = min(2 * ROW_TILE, rows)
    n = len(pieces)

    def body(*refs):
        a_ref, ins, outs = refs[0], refs[1:1 + n], refs[1 + n:]
        av = a_ref[...]
        for k in range(n):
            _accumulate(outs[k], _dot_tn(av, ins[k][...]), pl.program_id(0))

    return pl.pallas_call(
        body, name=name,
        out_shape=[jax.ShapeDtypeStruct((m, p.shape[1]), F32) for p in pieces],
        grid=(rows // tk,),
        in_specs=[pl.BlockSpec((tk, m), lambda k: (k, 0))]
        + [pl.BlockSpec((tk, p.shape[1]), lambda k: (k, 0)) for p in pieces],
        out_specs=[_full((m, p.shape[1])) for p in pieces],
        compiler_params=_params("arbitrary"),
    )(a, *pieces)


def matmul_dw(a, b, bn, name):
    rows, m = a.shape
    n = b.shape[1]
    tk = min((4 if n > bn else 2) * ROW_TILE, rows)
    steps = rows // tk

    def body(a_ref, b_ref, o_ref, o16_ref):
        part = _dot_tn(a_ref[...], b_ref[...])

        @pl.when(pl.program_id(1) == 0)
        def _():
            o_ref[0] = part

        @pl.when(pl.program_id(1) > 0)
        def _():
            o_ref[0] += part

        @pl.when(pl.program_id(1) == steps - 1)
        def _():
            o16_ref[0] = o_ref[0].astype(BF16)

    out = pl.BlockSpec((1, m, bn), lambda j, k: (j, 0, 0))
    return pl.pallas_call(
        body, name=name,
        out_shape=(jax.ShapeDtypeStruct((n // bn, m, bn), F32), jax.ShapeDtypeStruct((n // bn, m, bn), BF16)),
        grid=(n // bn, steps),
        in_specs=[pl.BlockSpec((tk, m), lambda j, k: (k, 0)), pl.BlockSpec((tk, bn), lambda j, k: (k, j))],
        out_specs=(out, out),
        compiler_params=_params("parallel", "arbitrary"),
    )(a, b)


def _scan(a, b, carry, reverse):
    n, c = a.shape
    blocks = n // SUBLANES
    a = a.reshape(blocks, SUBLANES, c)
    b = b.reshape(blocks, SUBLANES, c)
    pos = lax.broadcasted_iota(jnp.int32, (1, SUBLANES, c), 1)
    s = 1
    while s < SUBLANES:
        shift, valid = (SUBLANES - s, pos < SUBLANES - s) if reverse else (s, pos >= s)
        a_s, b_s = pltpu.roll(a, shift, 1), pltpu.roll(b, shift, 1)
        b = jnp.where(valid, a * b_s + b, b)
        a = jnp.where(valid, a * a_s, a)
        s *= 2
    out = [None] * blocks
    for k in (range(blocks - 1, -1, -1) if reverse else range(blocks)):
        h = a[k] * carry + b[k]
        out[k] = h
        carry = h[0:1] if reverse else h[SUBLANES - 1:SUBLANES]
    return jnp.concatenate(out, axis=0)


def _rg_gates(ua, gw_ref, gb, lam):
    ub = ua.astype(BF16)
    pre_r, pre_i = [], []
    for h in range(RG_HEADS):
        z = _dot(ub[:, h * RG_HEAD_DIM:(h + 1) * RG_HEAD_DIM], gw_ref[h])
        pre_r.append(z[:, :RG_HEAD_DIM])
        pre_i.append(z[:, RG_HEAD_DIM:])
    r = _sigmoid(jnp.concatenate(pre_r, axis=1) + gb[0:1])
    i = _sigmoid(jnp.concatenate(pre_i, axis=1) + gb[1:2])
    sp = _softplus(-lam)
    log_a = -RG_C * r * sp
    a = jnp.exp(log_a)
    mult = jnp.sqrt(1.0 - a * a)
    return r, i, sp, a, mult


def _rg_weight_specs():
    return [_full((4, D_MODEL)), _full((1, D_MODEL)), _full((RG_HEADS, RG_HEAD_DIM, 2 * RG_HEAD_DIM)),
            _full((2, D_MODEL)), _full((1, D_MODEL))]


def rglru_fwd(proj, conv_w, conv_b, gate_w, gate_b, lam, reverse, name):
    rows_total = proj.shape[0]
    rows = min(SCAN_TILE, rows_total)
    n_tiles = rows_total // rows
    tix = (lambda i: n_tiles - 1 - i) if reverse else (lambda i: i)

    def body(xp, xm, xn, cw_ref, cb_ref, gw_ref, gb_ref, lam_ref, h_ref, acts_ref, carry):
        i = pl.program_id(0)
        t = tix(i)
        ext = _extend(xp, xm, xn, t == 0, t == n_tiles - 1)
        ua = _conv(ext, cw_ref[...], 2, rows) + cb_ref[...]
        r, gi, _, a, mult = _rg_gates(ua, gw_ref, gb_ref[...], lam_ref[...])
        for k, saved in enumerate((ua, r, gi, a, mult)):
            acts_ref[k] = saved
        b = mult * (gi * ua)

        @pl.when(i == 0)
        def _():
            carry[...] = jnp.zeros_like(carry)

        h = _scan(a, b, carry[0:1], reverse)
        h_ref[...] = h
        edge = h[0:1] if reverse else h[rows - 1:rows]
        carry[...] = jnp.broadcast_to(edge, carry.shape)

    return pl.pallas_call(
        body, name=name,
        out_shape=(jax.ShapeDtypeStruct((rows_total, D_MODEL), F32),
                   jax.ShapeDtypeStruct((5, rows_total, D_MODEL), F32)),
        grid=(n_tiles,),
        in_specs=_halo_specs(rows, D_MODEL, 0, n_tiles, tix) + _rg_weight_specs(),
        out_specs=(pl.BlockSpec((rows, D_MODEL), lambda i: (tix(i), 0)),
                   pl.BlockSpec((5, rows, D_MODEL), lambda i: (0, tix(i), 0))),
        scratch_shapes=[pltpu.VMEM((SUBLANES, D_MODEL), F32)],
        compiler_params=_params("arbitrary"),
    )(proj, proj, proj, conv_w, conv_b, gate_w, gate_b, lam)


def rglru_bwd(proj, dycat, h_dir, acts, gate_w, lam, add_dua, reverse, name):
    rows_total = proj.shape[0]
    rows = min(SCAN_TILE, rows_total)
    n_tiles = rows_total // rows
    tix = (lambda i: i) if reverse else (lambda i: n_tiles - 1 - i)
    za_block = 1

    def body(acts_ref, za_ref, dya_ref, hp, hm, hn, gw_ref, lam_ref, *rest):
        other = rest[0][...] if add_dua is not None else 0.0
        dua_ref, dgw_ref, dgb_ref, dlam_ref, carry = rest[-5:]
        step = pl.program_id(0)
        t = tix(step)
        first, last = t == 0, t == n_tiles - 1
        ua, r, gi, a, mult = (acts_ref[k] for k in range(5))
        lam_v = lam_ref[...]
        sp = _softplus(-lam_v)
        za = za_ref[...].astype(F32)
        dh = dya_ref[...] * (za * _sigmoid(za))

        @pl.when(step == 0)
        def _():
            carry[...] = jnp.zeros_like(carry)

        old = carry[0:1]
        mu = _scan(a, a * dh, old, not reverse)
        row = lax.broadcasted_iota(jnp.int32, mu.shape, 0)
        if reverse:
            mu_next = jnp.where(row == 0, old, pltpu.roll(mu, 1, 0))
            carry[...] = jnp.broadcast_to(mu[rows - 1:rows], carry.shape)
            h_ext = _extend(hp, hm, hn, first, last)
            h_prev = _shifted(h_ext, 1, rows)
        else:
            mu_next = jnp.where(row == rows - 1, old, pltpu.roll(mu, rows - 1, 0))
            carry[...] = jnp.broadcast_to(mu[0:1], carry.shape)
            h_ext = _extend(hp, hm, hn, first, last)
            h_prev = _shifted(h_ext, -1, rows)
        db = dh + mu_next
        da = db * h_prev
        d_mult = db * (gi * ua)
        di = db * (mult * ua)
        dua = db * (mult * gi)
        dlog_a = da * a - d_mult * (a * a) / mult
        dr = dlog_a * (-RG_C * sp)
        dlam = _colsum(dlog_a * (-RG_C * r)) * (-_sigmoid(-lam_v))
        dpr = dr * (r * (1.0 - r))
        dpi = di * (gi * (1.0 - gi))
        dgb = jnp.concatenate([_colsum(dpr), _colsum(dpi)], axis=0)
        ub = ua.astype(BF16)
        dua_heads, dgw_heads = [], []
        for h in range(RG_HEADS):
            cols = slice(h * RG_HEAD_DIM, (h + 1) * RG_HEAD_DIM)
            dz = jnp.concatenate([dpr[:, cols], dpi[:, cols]], axis=1).astype(BF16)
            dgw_heads.append(_dot_tn(ub[:, cols], dz))
            dua_heads.append(_dot_nt(dz, gw_ref[h]))
        dua_ref[...] = dua + jnp.concatenate(dua_heads, axis=1) + other

        @pl.when(step == 0)
        def _():
            for h in range(RG_HEADS):
                dgw_ref[h] = dgw_heads[h]
            dgb_ref[...] = dgb
            dlam_ref[...] = dlam

        @pl.when(step > 0)
        def _():
            for h in range(RG_HEADS):
                dgw_ref[h] += dgw_heads[h]
            dgb_ref[...] += dgb
            dlam_ref[...] += dlam

    row_spec = lambda col: pl.BlockSpec((rows, D_MODEL), lambda i: (tix(i), col))
    return pl.pallas_call(
        body, name=name,
        out_shape=(jax.ShapeDtypeStruct((rows_total, D_MODEL), F32),
                   jax.ShapeDtypeStruct((RG_HEADS, RG_HEAD_DIM, 2 * RG_HEAD_DIM), F32),
                   jax.ShapeDtypeStruct((2, D_MODEL), F32), jax.ShapeDtypeStruct((1, D_MODEL), F32)),
        grid=(n_tiles,),
        in_specs=([pl.BlockSpec((5, rows, D_MODEL), lambda i: (0, tix(i), 0)), row_spec(za_block), row_spec(0)]
                  + _halo_specs(rows, D_MODEL, 0, n_tiles, tix)
                  + [_full((RG_HEADS, RG_HEAD_DIM, 2 * RG_HEAD_DIM)), _full((1, D_MODEL))]
                  + ([] if add_dua is None else [row_spec(0)])),
        out_specs=(row_spec(0), _full((RG_HEADS, RG_HEAD_DIM, 2 * RG_HEAD_DIM)), _full((2, D_MODEL)),
                   _full((1, D_MODEL))),
        scratch_shapes=[pltpu.VMEM((SUBLANES, D_MODEL), F32)],
        compiler_params=_params("arbitrary"),
    )(acts, proj, dycat, h_dir, h_dir, h_dir, gate_w, lam, *([] if add_dua is None else [add_dua]))


def _extend_cols(refs, block, is_first, is_last):
    cols = slice(block * D_MODEL, (block + 1) * D_MODEL)
    prev_ref, main_ref, next_ref = refs
    p = jnp.where(is_first, 0.0, prev_ref[:, cols].astype(F32))
    n = jnp.where(is_last, 0.0, next_ref[:, cols].astype(F32))
    return jnp.concatenate([p, main_ref[:, cols].astype(F32), n], axis=0)


def even_mix_fwd(proj, h_f, h_b, sc_w, name):
    rows_total = proj.shape[0]
    rows = min(2 * MIX_TILE, rows_total)
    n_tiles = rows_total // rows
    ident = lambda i: i

    def body(za_ref, hf_ref, hb_ref, xbp, xbm, xbn, gcp, gcm, gcn, gb_ref, zb_ref, w_ref, y_ref):
        t = pl.program_id(0)
        first, last = t == 0, t == n_tiles - 1
        za = za_ref[...].astype(F32)
        y_ref[:, 0:D_MODEL] = ((hf_ref[...] + hb_ref[...]) * (za * _sigmoid(za))).astype(BF16)
        p_ext = _extend(xbp, xbm, xbn, first, last) * _extend(gcp, gcm, gcn, first, last)
        cv = _conv(p_ext, w_ref[...], 1, rows)
        zb = zb_ref[...].astype(F32)
        y_ref[:, D_MODEL:2 * D_MODEL] = (gb_ref[...].astype(F32) * cv * (zb * _sigmoid(zb))).astype(BF16)

    blk = lambda col: pl.BlockSpec((rows, D_MODEL), lambda i: (i, col))
    return pl.pallas_call(
        body, name=name,
        out_shape=jax.ShapeDtypeStruct((rows_total, 2 * D_MODEL), BF16),
        grid=(n_tiles,),
        in_specs=([blk(1), blk(0), blk(0)] + _halo_specs(rows, D_MODEL, 2, n_tiles, ident)
                  + _halo_specs(rows, D_MODEL, 4, n_tiles, ident) + [blk(3), blk(5), _full((3, D_MODEL))]),
        out_specs=pl.BlockSpec((rows, 2 * D_MODEL), lambda i: (i, 0)),
        compiler_params=_params("parallel"),
    )(proj, h_f, h_b, proj, proj, proj, proj, proj, proj, proj, proj, sc_w)


def even_mix_bwd(proj, dycat, h_f, h_b, dua, conv_w, sc_w, name):
    rows_total, width = proj.shape
    rows = min(MIX_TILE, rows_total)
    n_tiles = rows_total // rows
    ident = lambda i: i

    def body(pp, pm, pn, dyp, dym, dyn, hf_ref, hb_ref, dup, dum, dun, cw_ref, sw_ref,
             dp_ref, dcw_ref, dcb_ref, dsw_ref):
        def put(k, value):
            dp_ref[:, k * D_MODEL:(k + 1) * D_MODEL] = value.astype(BF16)

        t = pl.program_id(0)
        first, last = t == 0, t == n_tiles - 1
        proj_ext = lambda k: _extend_cols((pp, pm, pn), k, first, last)
        mid = slice(HALO, HALO + rows)
        za = pm[:, D_MODEL:2 * D_MODEL].astype(F32)
        sa = _sigmoid(za)
        put(1, dym[:, 0:D_MODEL] * (hf_ref[...] + hb_ref[...]) * (sa * (1.0 + za * (1.0 - sa))))
        dua_ext = _extend(dup, dum, dun, first, last)
        cw = cw_ref[...]
        put(0, _conv_transpose(dua_ext, cw, 2, rows))
        dua_mid = dua_ext[mid]
        xa_ext = proj_ext(0)
        dcw = jnp.concatenate([_colsum(dua_mid * _shifted(xa_ext, k - 2, rows)) for k in range(4)], axis=0)
        dcb = _colsum(dua_mid)
        xb_ext, gb_ext, gc_ext, zb_ext = proj_ext(2), proj_ext(3), proj_ext(4), proj_ext(5)
        p_ext = xb_ext * gc_ext
        sb_ext = _sigmoid(zb_ext)
        dyb_ext = _extend_cols((dyp, dym, dyn), 1, first, last)
        dcv_ext = dyb_ext * gb_ext * (zb_ext * sb_ext)
        sw = sw_ref[...]
        p_at = [_shifted(p_ext, k - 1, rows) for k in range(3)]
        cv = (p_at[0] * sw[0:1] + p_at[1] * sw[1:2]) + p_at[2] * sw[2:3]
        zb, sb, dyb, gb = zb_ext[mid], sb_ext[mid], dyb_ext[mid], gb_ext[mid]
        put(3, dyb * cv * (zb * sb))
        put(5, dyb * gb * cv * (sb * (1.0 + zb * (1.0 - sb))))
        dp = _conv_transpose(dcv_ext, sw, 1, rows)
        put(4, dp * xb_ext[mid])
        put(2, dp * gc_ext[mid])
        dcv = dcv_ext[mid]
        dsw = jnp.concatenate([_colsum(dcv * p_at[k]) for k in range(3)], axis=0)
        _accumulate(dcw_ref, dcw, t)
        _accumulate(dcb_ref, dcb, t)
        _accumulate(dsw_ref, dsw, t)

    own = pl.BlockSpec((rows, D_MODEL), lambda i: (i, 0))
    return pl.pallas_call(
        body, name=name,
        out_shape=(jax.ShapeDtypeStruct((rows_total, 6 * D_MODEL), BF16),
                   jax.ShapeDtypeStruct((4, D_MODEL), F32), jax.ShapeDtypeStruct((1, D_MODEL), F32),
                   jax.ShapeDtypeStruct((3, D_MODEL), F32)),
        grid=(n_tiles,),
        in_specs=(_halo_specs(rows, width, 0, n_tiles, ident) + _halo_specs(rows, 2 * D_MODEL, 0, n_tiles, ident)
                  + [own, own] + _halo_specs(rows, D_MODEL, 0, n_tiles, ident)
                  + [_full((4, D_MODEL)), _full((3, D_MODEL))]),
        out_specs=(pl.BlockSpec((rows, 6 * D_MODEL), lambda i: (i, 0)), _full((4, D_MODEL)), _full((1, D_MODEL)),
                   _full((3, D_MODEL))),
        compiler_params=_params("arbitrary"),
    )(proj, proj, proj, dycat, dycat, dycat, h_f, h_b, dua, dua, dua, conv_w, sc_w)


def even_out_fwd(ycat, w_out, gain, x, name):
    rows, d = x.shape
    k = ycat.shape[1]
    tm = min(ROW_TILE, rows)

    def body(yc_ref, w_ref, g_ref, x_ref, x1_ref, y_ref):
        y = _dot(yc_ref[...], w_ref[...])
        y_ref[...] = y
        rstd = lax.rsqrt(jnp.mean(y * y, axis=-1, keepdims=True) + NORM_EPS)
        x1_ref[...] = x_ref[...] + y * rstd * g_ref[...]

    row = lambda n: pl.BlockSpec((tm, n), lambda i: (i, 0))
    return pl.pallas_call(
        body, name=name,
        out_shape=(jax.ShapeDtypeStruct((rows, d), F32),) * 2,
        grid=(rows // tm,),
        in_specs=[row(k), _full((k, d)), _full((1, d)), row(d)],
        out_specs=(row(d), row(d)),
        compiler_params=_params("parallel"),
    )(ycat, w_out, gain, x)


def _rmsnorm_bwd(dout, y, gain):
    rstd = lax.rsqrt(jnp.mean(y * y, axis=-1, keepdims=True) + NORM_EPS)
    yhat = y * rstd
    dyn = dout * gain
    dy = rstd * (dyn - yhat * jnp.mean(dyn * yhat, axis=-1, keepdims=True))
    return dy, dout * yhat


def even_out_bwd(dx1, y, gain, w_out, name):
    rows, d = y.shape
    k = w_out.shape[0]
    tm = min(ROW_TILE, rows)

    def body(dx_ref, y_ref, g_ref, w_ref, dy_ref, dyc_ref, dg_ref):
        dy, dg_rows = _rmsnorm_bwd(dx_ref[...], y_ref[...], g_ref[...])
        dyb = dy.astype(BF16)
        dy_ref[...] = dyb
        dyc_ref[...] = _dot_nt(dyb, w_ref[...])
        _accumulate(dg_ref, _colsum(dg_rows), pl.program_id(0))

    row = lambda n: pl.BlockSpec((tm, n), lambda i: (i, 0))
    return pl.pallas_call(
        body, name=name,
        out_shape=(jax.ShapeDtypeStruct((rows, d), BF16), jax.ShapeDtypeStruct((rows, k), F32),
                   jax.ShapeDtypeStruct((1, d), F32)),
        grid=(rows // tm,),
        in_specs=[row(d), row(d), _full((1, d)), _full((k, d))],
        out_specs=(row(d), row(k), _full((1, d))),
        compiler_params=_params("arbitrary"),
    )(dx1, y, gain, w_out)


def _chunk_cumsum(g, reverse):
    n, c = g.shape
    chunks, per = n // GLA_CHUNK, GLA_CHUNK // SUBLANES
    g = g.reshape(n // SUBLANES, SUBLANES, c)
    pos = lax.broadcasted_iota(jnp.int32, (1, SUBLANES, c), 1)
    s = 1
    while s < SUBLANES:
        if reverse:
            g = g + jnp.where(pos < SUBLANES - s, pltpu.roll(g, SUBLANES - s, 1), 0.0)
        else:
            g = g + jnp.where(pos >= s, pltpu.roll(g, s, 1), 0.0)
        s *= 2
    g = g.reshape(chunks, per, SUBLANES, c)
    out, carry = [None] * per, None
    for k in (range(per - 1, -1, -1) if reverse else range(per)):
        out[k] = g[:, k] if carry is None else g[:, k] + carry
        carry = out[k][:, 0:1] if reverse else out[k][:, SUBLANES - 1:SUBLANES]
    return jnp.stack(out, axis=1).reshape(n, c)


def _gla_prepare(q_ref, k_ref, lr_ref, wg_ref, bg_ref, reverse, n_chunks):
    z = _dot(lr_ref[...].astype(BF16), wg_ref[0]) + bg_ref[0]
    g = -_softplus(-z) * (1.0 / GLA_NORMALIZER)
    bcum = _chunk_cumsum(g, reverse).reshape(n_chunks, GLA_CHUNK, GLA_DK)
    edge = 0 if reverse else GLA_CHUNK - 1
    btot = bcum[:, edge:edge + 1, :]
    e_pos = jnp.exp(bcum)
    e_neg = jnp.exp(-bcum)
    e_st = jnp.exp(btot - bcum)
    q3 = q_ref[...].reshape(n_chunks, GLA_CHUNK, GLA_DK)
    k3 = k_ref[...].reshape(n_chunks, GLA_CHUNK, GLA_DK)
    scale = GLA_DK ** -0.5
    q_in = q3 * scale * e_pos
    k_in = k3 * e_neg
    k_st = k3 * e_st
    dec = jnp.exp(btot)
    return z, q_in, k_in, k_st, dec, (scale * e_pos, e_neg, e_st)


def _gla_mask(reverse):
    i = lax.broadcasted_iota(jnp.int32, (GLA_CHUNK, GLA_CHUNK), 0)
    j = lax.broadcasted_iota(jnp.int32, (GLA_CHUNK, GLA_CHUNK), 1)
    return (j >= i) if reverse else (j <= i)


def _gla_specs(rows, n_blocks, reverse):
    tix = (lambda s: n_blocks - 1 - s) if reverse else (lambda s: s)
    d = 1 if reverse else 0
    lr_block = LR_COL // LANES
    specs = [pl.BlockSpec((rows, GLA_DK), lambda h, s: (tix(s), h)),
             pl.BlockSpec((rows, GLA_DK), lambda h, s: (tix(s), GLA_HEADS + h)),
             pl.BlockSpec((rows, GLA_DV), lambda h, s: (tix(s), GLA_HEADS + h)),
             pl.BlockSpec((rows, LANES), lambda h, s: (tix(s), lr_block)),
             pl.BlockSpec((1, LANES, GLA_DK), lambda h, s: (d, 0, h)),
             pl.BlockSpec((1, 1, GLA_DK), lambda h, s: (d, 0, h))]
    return specs, tix


def gla_fwd(proj, wg_pad, bg, reverse, name):
    rows_total = proj.shape[0]
    rows = min(GLA_BLOCK, rows_total)
    n_blocks = rows_total // rows
    n_chunks = rows // GLA_CHUNK
    specs, tix = _gla_specs(rows, n_blocks, reverse)

    def body(q_ref, k_ref, v_ref, lr_ref, wg_ref, bg_ref, o_ref, st_ref, state, kv_scr, dec_scr):
        _, q_in, k_in, k_st, dec, _ = _gla_prepare(q_ref, k_ref, lr_ref, wg_ref, bg_ref, reverse, n_chunks)
        vb = v_ref[...].reshape(n_chunks, GLA_CHUNK, GLA_DV).astype(BF16)
        qb = q_in.astype(BF16)
        p = jnp.where(_gla_mask(reverse), _bdot(qb, k_in.astype(BF16), 2, 2), 0.0)
        o = _bdot(p.astype(BF16), vb, 2, 1)
        kv_scr[...] = _bdot(vb, k_st.astype(BF16), 1, 1)
        dec_scr[...] = jnp.broadcast_to(dec, dec_scr.shape)

        @pl.when(pl.program_id(1) == 0)
        def _():
            state[...] = jnp.zeros_like(state)

        for c in range(n_chunks):
            cc = n_chunks - 1 - c if reverse else c
            st_ref[0, cc] = state[...]
            state[...] = state[...] * dec_scr[cc, 0:1] + kv_scr[cc]
        o = o + _bdot(qb, st_ref[0].astype(BF16), 2, 2)
        o_ref[...] = o.reshape(rows, GLA_DV)

    return pl.pallas_call(
        body, name=name,
        out_shape=(jax.ShapeDtypeStruct((rows_total, GLA_HEADS * GLA_DV), F32),
                   jax.ShapeDtypeStruct((GLA_HEADS, rows_total // GLA_CHUNK, GLA_DV, GLA_DK), F32)),
        grid=(GLA_HEADS, n_blocks),
        in_specs=specs,
        out_specs=(pl.BlockSpec((rows, GLA_DV), lambda h, s: (tix(s), h)),
                   pl.BlockSpec((1, n_chunks, GLA_DV, GLA_DK), lambda h, s: (h, tix(s), 0, 0))),
        scratch_shapes=[pltpu.VMEM((GLA_DV, GLA_DK), F32), pltpu.VMEM((n_chunks, GLA_DV, GLA_DK), F32),
                        pltpu.VMEM((n_chunks, SUBLANES, GLA_DK), F32)],
        compiler_params=_params("parallel", "arbitrary"),
    )(proj, proj, proj, proj, wg_pad, bg)


def gla_bwd(proj, wg_pad, bg, d_o, states, dqkv_in, reverse, name):
    rows_total = proj.shape[0]
    rows = min(GLA_BLOCK, rows_total)
    n_blocks = rows_total // rows
    n_chunks = rows // GLA_CHUNK
    specs, tix = _gla_specs(rows, n_blocks, not reverse)
    d = 1 if reverse else 0
    specs[4] = pl.BlockSpec((1, LANES, GLA_DK), lambda h, s: (d, 0, h))
    specs[5] = pl.BlockSpec((1, 1, GLA_DK), lambda h, s: (d, 0, h))
    add = dqkv_in is not None

    def body(*refs):
        q_ref, k_ref, v_ref, lr_ref, wg_ref, bg_ref, do_ref, st_ref = refs[:8]
        refs = refs[8:]
        if add:
            aq_ref, ak_ref, av_ref = refs[:3]
            refs = refs[3:]
        dq_ref, dk_ref, dv_ref, dz_ref, dstate, g_scr, dec_scr, dsn_scr = refs
        z, q_in, k_in, k_st, dec, (f_q, f_k, f_s) = _gla_prepare(q_ref, k_ref, lr_ref, wg_ref, bg_ref, reverse,
                                                                 n_chunks)
        mask = _gla_mask(reverse)
        vb = v_ref[...].reshape(n_chunks, GLA_CHUNK, GLA_DV).astype(BF16)
        dob = do_ref[...].reshape(n_chunks, GLA_CHUNK, GLA_DV).astype(BF16)
        qb, kb, ksb = q_in.astype(BF16), k_in.astype(BF16), k_st.astype(BF16)
        st = st_ref[0]
        stb = st.astype(BF16)
        pb = jnp.where(mask, _bdot(qb, kb, 2, 2), 0.0).astype(BF16)
        dpb = jnp.where(mask, _bdot(dob, vb, 2, 2), 0.0).astype(BF16)
        d_qin = _bdot(dpb, kb, 2, 1) + _bdot(dob, stb, 2, 1)
        d_kin = _bdot(dpb, qb, 1, 1)
        dv = _bdot(pb, dob, 1, 1)
        g_scr[...] = _bdot(dob, qb, 1, 1)
        dec_scr[...] = jnp.broadcast_to(dec, dec_scr.shape)

        @pl.when(pl.program_id(1) == 0)
        def _():
            dstate[...] = jnp.zeros_like(dstate)

        for c in range(n_chunks):
            cc = c if reverse else n_chunks - 1 - c
            dsn_scr[cc] = dstate[...]
            dstate[...] = dstate[...] * dec_scr[cc, 0:1] + g_scr[cc]
        dsn = dsn_scr[...]
        dsnb = dsn.astype(BF16)
        dv = dv + _bdot(ksb, dsnb, 2, 2)
        d_kst = _bdot(vb, dsnb, 2, 1)
        d_dec = jnp.sum(dsn * st, axis=1, keepdims=True)
        ks_term = d_kst * k_st
        d_btot = d_dec * dec + jnp.sum(ks_term, axis=1, keepdims=True)
        d_b = d_qin * q_in - d_kin * k_in - ks_term
        pos = lax.broadcasted_iota(jnp.int32, d_b.shape, 1)
        edge = 0 if reverse else GLA_CHUNK - 1
        d_b = d_b + jnp.where(pos == edge, d_btot, 0.0)
        dg = _chunk_cumsum(d_b.reshape(rows, GLA_DK), not reverse)
        dz_ref[...] = dg * (1.0 / GLA_NORMALIZER) * _sigmoid(-z)
        dq = (d_qin * f_q).reshape(rows, GLA_DK)
        dk = (d_kin * f_k + d_kst * f_s).reshape(rows, GLA_DK)
        dv = dv.reshape(rows, GLA_DV)
        if add:
            dq_ref[...] = (dq + aq_ref[...]).astype(BF16)
            dk_ref[...] = (dk + ak_ref[...]).astype(BF16)
            dv_ref[...] = (dv + av_ref[...]).astype(BF16)
        else:
            dq_ref[...] = dq
            dk_ref[...] = dk
            dv_ref[...] = dv

    qkv_specs = [pl.BlockSpec((rows, GLA_DK), lambda h, s: (tix(s), h)),
                 pl.BlockSpec((rows, GLA_DK), lambda h, s: (tix(s), h)),
                 pl.BlockSpec((rows, GLA_DV), lambda h, s: (tix(s), h))]
    in_specs = specs + [pl.BlockSpec((rows, GLA_DV), lambda h, s: (tix(s), h)),
                        pl.BlockSpec((1, n_chunks, GLA_DV, GLA_DK), lambda h, s: (h, tix(s), 0, 0))]
    args = [proj, proj, proj, proj, wg_pad, bg, d_o, states]
    out_dtype = F32
    if add:
        in_specs += qkv_specs
        args += list(dqkv_in)
        out_dtype = BF16
    return pl.pallas_call(
        body, name=name,
        out_shape=(jax.ShapeDtypeStruct((rows_total, GLA_HEADS * GLA_DK), out_dtype),
                   jax.ShapeDtypeStruct((rows_total, GLA_HEADS * GLA_DK), out_dtype),
                   jax.ShapeDtypeStruct((rows_total, GLA_HEADS * GLA_DV), out_dtype),
                   jax.ShapeDtypeStruct((rows_total, GLA_HEADS * GLA_DK), F32)),
        grid=(GLA_HEADS, n_blocks),
        in_specs=in_specs,
        out_specs=(pl.BlockSpec((rows, GLA_DK), lambda h, s: (tix(s), h)),
                   pl.BlockSpec((rows, GLA_DK), lambda h, s: (tix(s), h)),
                   pl.BlockSpec((rows, GLA_DV), lambda h, s: (tix(s), h)),
                   pl.BlockSpec((rows, GLA_DK), lambda h, s: (tix(s), h))),
        scratch_shapes=[pltpu.VMEM((GLA_DV, GLA_DK), F32), pltpu.VMEM((n_chunks, GLA_DV, GLA_DK), F32),
                        pltpu.VMEM((n_chunks, SUBLANES, GLA_DK), F32),
                        pltpu.VMEM((n_chunks, GLA_DV, GLA_DK), F32)],
        compiler_params=_params("parallel", "arbitrary"),
    )(*args)


def gla_gate_bwd(proj, dz_f, dz_b, wg_pad, name):
    rows_total = proj.shape[0]
    tm = min(ROW_TILE, rows_total)
    n_key = GLA_HEADS * GLA_DK

    def body(lr_ref, dzf_ref, dzb_ref, wg_ref, dlr_ref, dwg_ref, dbg_ref):
        step = pl.program_id(0)
        lr_t = jnp.transpose(lr_ref[...])
        dzf, dzb = dzf_ref[...], dzb_ref[...]
        dzf16, dzb16 = dzf.astype(BF16), dzb.astype(BF16)
        dlr_ref[...] = (_dot_nt(dzf16, wg_ref[0]) + _dot_nt(dzb16, wg_ref[1])).astype(BF16)
        dwf = _dot(lr_t[0:GLA_RANK].astype(BF16), dzf16)
        dwb = _dot(lr_t[GLA_RANK:2 * GLA_RANK].astype(BF16), dzb16)
        dbg = jnp.concatenate([_colsum(dzf), _colsum(dzb)], axis=0)

        @pl.when(step == 0)
        def _():
            dwg_ref[0] = dwf
            dwg_ref[1] = dwb
            dbg_ref[...] = dbg

        @pl.when(step > 0)
        def _():
            dwg_ref[0] += dwf
            dwg_ref[1] += dwb
            dbg_ref[...] += dbg

    return pl.pallas_call(
        body, name=name,
        out_shape=(jax.ShapeDtypeStruct((rows_total, LANES), BF16), jax.ShapeDtypeStruct((2, GLA_RANK, n_key), F32),
                   jax.ShapeDtypeStruct((2, n_key), F32)),
        grid=(rows_total // tm,),
        in_specs=[pl.BlockSpec((tm, LANES), lambda i: (i, LR_COL // LANES)),
                  pl.BlockSpec((tm, n_key), lambda i: (i, 0)), pl.BlockSpec((tm, n_key), lambda i: (i, 0)),
                  _full((2, LANES, n_key))],
        out_specs=(pl.BlockSpec((tm, LANES), lambda i: (i, 0)), _full((2, GLA_RANK, n_key)), _full((2, n_key))),
        compiler_params=_params("arbitrary"),
    )(proj, dz_f, dz_b, wg_pad)


def _head_norm(o, gain):
    outs, hats, rstds = [], [], []
    for h in range(GLA_HEADS):
        oh = o[:, h * GLA_DV:(h + 1) * GLA_DV]
        rstd = lax.rsqrt(jnp.mean(oh * oh, axis=-1, keepdims=True) + NORM_EPS)
        hat = oh * rstd
        outs.append(hat * gain)
        hats.append(hat)
        rstds.append(rstd)
    return outs, hats, rstds


def odd_out_fwd(o_f, o_b, proj, head_gain, w_out, gain, x1, target, name):
    rows, d = x1.shape
    tm = min(ROW_TILE, rows)
    r_block = (2 * GLA_HEADS * GLA_DK + GLA_HEADS * GLA_DV) // d

    def body(of_ref, ob_ref, r_ref, hg_ref, w_ref, g_ref, x1_ref, tgt_ref, y2_ref, dy_ref, dx2_ref, loss_ref,
             dg_ref):
        step = pl.program_id(0)
        on, _, _ = _head_norm(of_ref[...] + ob_ref[...], hg_ref[...])
        r = r_ref[...]
        y2 = (jnp.concatenate(on, axis=1) * (r * _sigmoid(r))).astype(BF16)
        y2_ref[...] = y2
        y = _dot(y2, w_ref[...])
        gain_v = g_ref[...]
        rstd = lax.rsqrt(jnp.mean(y * y, axis=-1, keepdims=True) + NORM_EPS)
        x2 = x1_ref[...] + y * rstd * gain_v
        diff = x2 - tgt_ref[...]
        loss = 0.5 * jnp.sum(jnp.mean(diff * diff, axis=-1, keepdims=True), axis=0, keepdims=True)
        dx2 = diff * (1.0 / d)
        dx2_ref[...] = dx2
        dy, dg_rows = _rmsnorm_bwd(dx2, y, gain_v)
        dy_ref[...] = dy.astype(BF16)
        _accumulate(loss_ref, jnp.broadcast_to(loss, loss_ref.shape), step)
        _accumulate(dg_ref, _colsum(dg_rows), step)

    row = lambda n, col=0: pl.BlockSpec((tm, n), lambda i: (i, col))
    return pl.pallas_call(
        body, name=name,
        out_shape=(jax.ShapeDtypeStruct((rows, d), BF16), jax.ShapeDtypeStruct((rows, d), BF16),
                   jax.ShapeDtypeStruct((rows, d), F32), jax.ShapeDtypeStruct((SUBLANES, LANES), F32),
                   jax.ShapeDtypeStruct((1, d), F32)),
        grid=(rows // tm,),
        in_specs=[row(d), row(d), row(d, r_block), _full((1, GLA_DV)), _full((d, d)), _full((1, d)), row(d), row(d)],
        out_specs=(row(d), row(d), row(d), _full((SUBLANES, LANES)), _full((1, d))),
        compiler_params=_params("arbitrary"),
    )(o_f, o_b, proj, head_gain, w_out, gain, x1, target)


def odd_out_bwd(dy, w_out, o_f, o_b, proj, head_gain, name):
    rows, d = dy.shape
    tm = min(ROW_TILE, rows)
    r_block = (2 * GLA_HEADS * GLA_DK + GLA_HEADS * GLA_DV) // d

    def body(dy_ref, w_ref, of_ref, ob_ref, r_ref, hg_ref, dr_ref, do_ref, dhg_ref):
        dy2 = _dot_nt(dy_ref[...], w_ref[...])
        hg = hg_ref[...]
        on, hats, rstds = _head_norm(of_ref[...] + ob_ref[...], hg)
        r = r_ref[...]
        sr = _sigmoid(r)
        dr_ref[...] = (dy2 * jnp.concatenate(on, axis=1) * (sr * (1.0 + r * (1.0 - sr)))).astype(BF16)
        d_on = dy2 * (r * sr)
        d_os, dhg = [], None
        for h in range(GLA_HEADS):
            dn = d_on[:, h * GLA_DV:(h + 1) * GLA_DV]
            part = _colsum(dn * hats[h])
            dhg = part if dhg is None else dhg + part
            dng = dn * hg
            d_os.append(rstds[h] * (dng - hats[h] * jnp.mean(dng * hats[h], axis=-1, keepdims=True)))
        do_ref[...] = jnp.concatenate(d_os, axis=1)
        _accumulate(dhg_ref, dhg, pl.program_id(0))

    row = lambda n, col=0: pl.BlockSpec((tm, n), lambda i: (i, col))
    return pl.pallas_call(
        body, name=name,
        out_shape=(jax.ShapeDtypeStruct((rows, d), BF16), jax.ShapeDtypeStruct((rows, d), F32),
                   jax.ShapeDtypeStruct((1, GLA_DV), F32)),
        grid=(rows // tm,),
        in_specs=[row(d), _full((d, d)), row(d), row(d), row(d, r_block), _full((1, GLA_DV))],
        out_specs=(row(d), row(d), _full((1, GLA_DV))),
        compiler_params=_params("arbitrary"),
    )(dy, w_out, o_f, o_b, proj, head_gain)


def local_step(x, target, w, reduce_first=None, reduce_second=None, late_weights=None):
    g, g16 = {}, {}
    proj_e, h0 = norm_matmul(x, w["even_norm_pre"], w["even_w_in"], BF16, "even_in_proj")
    h_dir, acts = zip(*[rglru_fwd(proj_e, w["rg_conv_w"], w["rg_conv_b"], w["rg_gate_w"][d], w["rg_gate_b"][d],
                                  w["rg_lambda"][d], d == 1, "rglru_fwd_%d" % d) for d in range(2)])
    ycat = even_mix_fwd(proj_e, h_dir[0], h_dir[1], w["sc_conv_w"], "even_mix_fwd")
    if late_weights is not None:
        w = dict(w, **late_weights(ycat))
    x1, y_e = even_out_fwd(ycat, w["even_w_out"], w["even_norm_post"], x, "even_out_fwd")
    proj_o, h1 = norm_matmul(x1, w["odd_norm_pre"], w["odd_w_in"], F32, "odd_in_proj")
    o_dir, st_dir = [], []
    for d in range(2):
        o, st = gla_fwd(proj_o, w["gla_wg_pad"], w["gla_b_gate"], d == 1, "gla_fwd_%d" % d)
        o_dir.append(o)
        st_dir.append(st)
    y2, dy_o, dx2, loss, g["odd_norm_post"] = odd_out_fwd(
        o_dir[0], o_dir[1], proj_o, w["gla_norm_g"], w["odd_w_out"], w["odd_norm_post"], x1, target, "odd_out_fwd")
    g["odd_w_out"], g16["odd_w_out"] = (a[0] for a in matmul_dw(y2, dy_o, D_MODEL, "odd_w_out_grad"))
    dr, d_o, g["gla_norm_g"] = odd_out_bwd(dy_o, w["odd_w_out"], o_dir[0], o_dir[1], proj_o, w["gla_norm_g"],
                                           "odd_out_bwd")
    dq, dk, dv, dz_f = gla_bwd(proj_o, w["gla_wg_pad"], w["gla_b_gate"], d_o, st_dir[0], None, False, "gla_bwd_0")
    dq, dk, dv, dz_b = gla_bwd(proj_o, w["gla_wg_pad"], w["gla_b_gate"], d_o, st_dir[1], (dq, dk, dv), True,
                               "gla_bwd_1")
    dlr, g["gla_w_gate_lr"], g["gla_b_gate"] = gla_gate_bwd(proj_o, dz_f, dz_b, w["gla_wg_pad"], "gla_gate_bwd")
    dproj_o = [dq, dk, dv, dr, dlr]
    g["odd_w_in"] = jnp.concatenate(matmul_dw_pieces(h1, dproj_o, "odd_w_in_grad"), axis=1)[:, :ODD_IN]
    dx1, g["odd_norm_pre"] = inproj_bwd_pieces(dproj_o, w["odd_w_in"], x1, w["odd_norm_pre"], dx2, "odd_in_proj_bwd")
    dy_e, dycat, g["even_norm_post"] = even_out_bwd(dx1, y_e, w["even_norm_post"], w["even_w_out"], "even_out_bwd")
    g["even_w_out"], g16["even_w_out"] = (a[0] for a in matmul_dw(ycat, dy_e, D_MODEL, "even_w_out_grad"))
    lam = w["rg_lambda"] if reduce_first is None else w["rg_lambda"] + reduce_first(g, g16)
    dua, dgw, dgb, dlam = None, [], [], []
    for d in range(2):
        a, b, c, e = rglru_bwd(proj_e, dycat, h_dir[d], acts[d], w["rg_gate_w"][d], lam[d], dua, d == 1,
                               "rglru_bwd_%d" % d)
        dua = a
        dgw.append(b)
        dgb.append(c)
        dlam.append(e)
    dproj_e, g["rg_conv_w"], g["rg_conv_b"], g["sc_conv_w"] = even_mix_bwd(
        proj_e, dycat, h_dir[0], h_dir[1], dua, w["rg_conv_w"], w["sc_conv_w"], "even_mix_bwd")
    dgw = jnp.stack(dgw).reshape(2, RG_HEADS, RG_HEAD_DIM, 2, RG_HEAD_DIM)
    g["rg_gate_w"] = jnp.transpose(dgw, (0, 3, 1, 2, 4))
    g["rg_gate_b"] = jnp.stack(dgb).reshape(2, 2, RG_HEADS, RG_HEAD_DIM)
    g["rg_lambda"] = jnp.concatenate(dlam, axis=0)
    g["even_w_in"], g16["even_w_in"] = matmul_dw(h0, dproj_e, EVEN_IN // 4, "even_w_in_grad")
    gain = w["even_norm_pre"] if reduce_second is None else w["even_norm_pre"] + reduce_second(g, g16)
    grad_x, g["even_norm_pre"] = inproj_bwd(dproj_e, w["even_w_in"], x, gain, dx1, "even_in_proj_bwd")
    return loss, grad_x, g


def _prepare_weights(full):
    w = {}
    for name in ("even_norm_pre", "even_norm_post", "rg_conv_b", "odd_norm_pre", "odd_norm_post", "gla_norm_g"):
        if name in full:
            w[name] = full[name].reshape(1, -1)
    for name in ("rg_conv_w", "sc_conv_w"):
        if name in full:
            w[name] = full[name]
    for name in ("even_w_out", "odd_w_out"):
        if name in full:
            w[name] = full[name].astype(BF16)
    if "even_w_in" in full:
        w["even_w_in"] = full["even_w_in"].astype(BF16)
        if w["even_w_in"].ndim == 2:
            w["even_w_in"] = jnp.transpose(w["even_w_in"].reshape(D_MODEL, 4, EVEN_IN // 4), (1, 0, 2))
    if "rg_gate_w" in full:
        gw = jnp.transpose(full["rg_gate_w"].astype(BF16), (0, 2, 3, 1, 4))
        w["rg_gate_w"] = gw.reshape(2, RG_HEADS, RG_HEAD_DIM, 2 * RG_HEAD_DIM)
        w["rg_gate_b"] = full["rg_gate_b"].reshape(2, 2, D_MODEL)
        w["rg_lambda"] = full["rg_lambda"].reshape(2, 1, D_MODEL)
    if "odd_w_in" in full:
        w_in = jnp.pad(full["odd_w_in"].astype(BF16), ((0, 0), (0, ODD_IN_PAD - ODD_IN)))
        w["odd_w_in"] = w_in.reshape(1, D_MODEL, ODD_IN_PAD)
    if "gla_w_gate_lr" in full:
        wg = full["gla_w_gate_lr"].astype(BF16)
        w["gla_wg_pad"] = jnp.stack([jnp.pad(wg[d], ((d * GLA_RANK, LANES - (d + 1) * GLA_RANK), (0, 0)))
                                     for d in range(2)])
        w["gla_b_gate"] = full["gla_b_gate"].reshape(2, 1, GLA_HEADS * GLA_DK)
    return w


SHARDED_SMALL = (("rg_conv_w", (4, 256)), ("rg_lambda", (2, 256)), ("sc_conv_w", (3, 256)),
                 ("odd_norm_pre", (256,)), ("odd_norm_post", (256,)), ("gla_w_gate_lr", (2, 16, 128)),
                 ("gla_b_gate", (2, 128)), ("gla_norm_g", (64,)))
SHARDED_ROWS = 96
REPLICATED = (("rg_gate_w", (2, 2, 8, 128, 128)), ("even_norm_post", (1024,)), ("rg_conv_b", (1024,)),
              ("rg_gate_b", (2, 2, 8, 128)))
GATE_ROWS = 4096
LAST_REPLICATED = (("even_norm_pre", (1024,)),)
LAST_ROWS = 8
REPLICATED_ROWS = 4160
REP_PART = REPLICATED_ROWS // 8
HALF_SHARDED = SHARDED_ROWS // 2
PACK_HALF = HALF_SHARDED + REP_PART


def _seg_rows(shape):
    n = 1
    for s in shape:
        n *= s
    return -(-n // (SUBLANES * LANES)) * SUBLANES


def _pack(arrays, spec, total_rows, lead=()):
    parts = []
    for name, shape in spec:
        flat = arrays[name].reshape(lead + (-1,))
        pad = _seg_rows(shape) * LANES - flat.shape[-1]
        if pad:
            flat = jnp.pad(flat, [(0, 0)] * len(lead) + [(0, pad)])
        parts.append(flat.reshape(lead + (-1, LANES)))
    rows = jnp.concatenate(parts, axis=len(lead))
    pad = total_rows - rows.shape[len(lead)]
    return jnp.pad(rows, [(0, 0)] * len(lead) + [(0, pad), (0, 0)])


def _unpack(rows, spec, lead=()):
    out, at = {}, 0
    for name, shape in spec:
        n = 1
        for s in shape:
            n *= s
        k = _seg_rows(shape)
        seg = lax.slice_in_dim(rows, at, at + k, axis=len(lead)).reshape(lead + (-1,))
        out[name] = lax.slice_in_dim(seg, 0, n, axis=len(lead)).reshape(lead + shape)
        at += k
    return out


def _split_owners(arr):
    a = arr.reshape(arr.shape[:-1] + (4, arr.shape[-1] // 4))
    return jnp.moveaxis(a, -2, 0)


def _merge_owners(arr):
    a = jnp.moveaxis(arr, 0, -2)
    return a.reshape(a.shape[:-2] + (-1,))


HBM_SPEC = pl.BlockSpec(memory_space=pltpu.HBM)


def _position():
    x, y, c = lax.axis_index("x"), lax.axis_index("y"), lax.axis_index("c")
    chips = [(1 - x, y), (x, 1 - y), (1 - x, 1 - y)]
    return x, y, c, chips


def _remote(src, dst, send_sem, recv_sem, device):
    return pltpu.make_async_remote_copy(src_ref=src, dst_ref=dst, send_sem=send_sem, recv_sem=recv_sem,
                                        device_id=device, device_id_type=MESH)


SEM_SPEC = pl.BlockSpec(memory_space=pltpu.SEMAPHORE)
SIDE_EFFECT = pltpu.SideEffectType.DATAFLOW_SIDE_EFFECTING


def _gather_copies(ins, lands, n_h, send_sems, recv_sems):
    x, y, c, chips = _position()
    me = 2 * x + y
    copies = []
    for a in range(len(ins)):
        for k, chip in enumerate(chips):
            src = ins[a].at[c] if a < n_h else ins[a]
            dst = lands[a].at[me, c] if a < n_h else lands[a].at[me]
            copies.append(_remote(src, dst, send_sems.at[3 * a + k], recv_sems.at[3 * a + k], (chip[0], chip[1], c)))
    return copies


def gather_start(halved, whole, name):
    arrays = list(halved) + list(whole)
    n, n_h = len(arrays), len(halved)
    lands = [lax.empty((4,) + a.shape, a.dtype) for a in arrays]

    def body(*refs):
        ins, lz, send_sems, recv_sems, token = refs[:n], refs[n:2 * n], refs[2 * n], refs[2 * n + 1], refs[-1]
        for cp in _gather_copies(ins, lz, n_h, send_sems, recv_sems):
            cp.start()
        token[...] = jnp.zeros_like(token)

    operands = [pltpu.with_memory_space_constraint(a, pltpu.HBM) for a in arrays + lands]
    return pl.pallas_call(
        body, name=name,
        out_shape=(pltpu.SemaphoreType.DMA((3 * n,)), pltpu.SemaphoreType.DMA((3 * n,)))
        + tuple(pltpu.HBM(a.shape, a.dtype) for a in operands) + (jax.ShapeDtypeStruct((SUBLANES, LANES), F32),),
        in_specs=[HBM_SPEC] * (2 * n),
        out_specs=(SEM_SPEC, SEM_SPEC) + (HBM_SPEC,) * (2 * n) + (pl.BlockSpec(memory_space=pltpu.VMEM),),
        input_output_aliases={i: 2 + i for i in range(2 * n)},
        compiler_params=pltpu.CompilerParams(has_side_effects=SIDE_EFFECT),
    )(*operands)


def gather_wait(started, n_h, after, name):
    send_sems, recv_sems = started[0], started[1]
    operands = list(started[2:-1])
    n = len(operands) // 2

    def body(*refs):
        ins, lz, send_ref, recv_ref = refs[:n], refs[n:2 * n], refs[2 * n], refs[2 * n + 1]
        for cp in _gather_copies(ins, lz, n_h, send_ref, recv_ref):
            cp.wait_send()
            cp.wait_recv()

    outs = pl.pallas_call(
        body, name=name,
        out_shape=tuple(pltpu.HBM(a.shape, a.dtype) for a in operands),
        in_specs=[HBM_SPEC] * (2 * n) + [SEM_SPEC, SEM_SPEC, pl.BlockSpec(memory_space=pl.ANY)],
        out_specs=(HBM_SPEC,) * (2 * n),
        input_output_aliases={i: i for i in range(2 * n)},
        compiler_params=pltpu.CompilerParams(has_side_effects=SIDE_EFFECT),
    )(*operands, send_sems, recv_sems, after)
    return outs[n:]


def pass_to_sibling(fulls, name):
    n = len(fulls)

    def body(*refs):
        bufs = refs[n:2 * n]
        send_sems, recv_sems = refs[2 * n:]
        x, y, c, chips = _position()
        sibling = (x, y, 1 - c)
        copies = []
        for a in range(n):
            for k, chip in enumerate(chips):
                q = 2 * chip[0] + chip[1]
                cp = _remote(bufs[a].at[q, c], bufs[a].at[q, c], send_sems.at[3 * a + k], recv_sems.at[3 * a + k],
                             sibling)
                cp.start()
                copies.append(cp)
        for a in range(n):
            for k, chip in enumerate(chips):
                q = 2 * chip[0] + chip[1]
                passed = bufs[a].at[q, 1 - c]
                _remote(passed, passed, send_sems.at[3 * a + k], recv_sems.at[3 * a + k], sibling).wait_recv()
        for cp in copies:
            cp.wait_send()

    return pl.pallas_call(
        body, name=name,
        out_shape=[jax.ShapeDtypeStruct(a.shape, a.dtype) for a in fulls],
        in_specs=[HBM_SPEC] * n, out_specs=[HBM_SPEC] * n,
        input_output_aliases={i: i for i in range(n)},
        scratch_shapes=[pltpu.SemaphoreType.DMA((3 * n,)), pltpu.SemaphoreType.DMA((3 * n,))],
    )(*fulls)


def place_own(full, own, chip, name):
    _, _, r, cols = full.shape
    tr = _row_tile(r, cols)

    def body(p_ref, own_ref, full_ref, o_ref):
        o_ref[0] = own_ref[...]

    return pl.pallas_call(
        body, name=name,
        out_shape=jax.ShapeDtypeStruct(full.shape, full.dtype),
        grid_spec=pltpu.PrefetchScalarGridSpec(
            num_scalar_prefetch=1, grid=(2, r // tr),
            in_specs=[pl.BlockSpec((1, tr, cols), lambda h, i, p_ref: (h, i, 0)), pl.BlockSpec(memory_space=pl.ANY)],
            out_specs=pl.BlockSpec((1, 1, tr, cols), lambda h, i, p_ref: (p_ref[0], h, i, 0))),
        input_output_aliases={2: 0},
        compiler_params=_params("parallel", "parallel"),
    )(chip, own, full)


def exchange_with_sibling(arrays, name):
    n = len(arrays)

    def body(*refs):
        ins, outs = refs[:n], refs[n:2 * n]
        send_sems, recv_sems = refs[2 * n:]
        x, y, c, _ = _position()
        copies = []
        for a in range(n):
            cp = _remote(ins[a].at[:, 1 - c], outs[a], send_sems.at[a], recv_sems.at[a], (x, y, 1 - c))
            cp.start()
            copies.append(cp)
        for cp in copies:
            cp.wait()

    return pl.pallas_call(
        body, name=name,
        out_shape=[jax.ShapeDtypeStruct((a.shape[0],) + a.shape[2:], a.dtype) for a in arrays],
        in_specs=[HBM_SPEC] * n, out_specs=[HBM_SPEC] * n,
        scratch_shapes=[pltpu.SemaphoreType.DMA((n,)), pltpu.SemaphoreType.DMA((n,))],
    )(*arrays)


def _chip_copies(ins, lands, send_sems, recv_sems):
    x, y, c, chips = _position()
    copies = []
    for a in range(len(ins)):
        for k, chip in enumerate(chips):
            q = 2 * chip[0] + chip[1]
            copies.append(_remote(ins[a].at[q], lands[a].at[k], send_sems.at[3 * a + k], recv_sems.at[3 * a + k],
                                  (chip[0], chip[1], c)))
    return copies


def exchange_with_chips_start(arrays, name):
    n = len(arrays)
    lands = [lax.empty((3,) + a.shape[1:], a.dtype) for a in arrays]

    def body(*refs):
        ins, lz, send_sems, recv_sems, token = refs[:n], refs[n:2 * n], refs[2 * n], refs[2 * n + 1], refs[-1]
        for cp in _chip_copies(ins, lz, send_sems, recv_sems):
            cp.start()
        token[...] = jnp.zeros_like(token)

    operands = [pltpu.with_memory_space_constraint(a, pltpu.HBM) for a in list(arrays) + lands]
    return pl.pallas_call(
        body, name=name,
        out_shape=(pltpu.SemaphoreType.DMA((3 * n,)), pltpu.SemaphoreType.DMA((3 * n,)))
        + tuple(pltpu.HBM(a.shape, a.dtype) for a in operands) + (jax.ShapeDtypeStruct((SUBLANES, LANES), F32),),
        in_specs=[HBM_SPEC] * (2 * n),
        out_specs=(SEM_SPEC, SEM_SPEC) + (HBM_SPEC,) * (2 * n) + (pl.BlockSpec(memory_space=pltpu.VMEM),),
        input_output_aliases={i: 2 + i for i in range(2 * n)},
        compiler_params=pltpu.CompilerParams(has_side_effects=SIDE_EFFECT),
    )(*operands)


def exchange_with_chips_wait(started, after, name):
    send_sems, recv_sems = started[0], started[1]
    operands = list(started[2:-1])
    n = len(operands) // 2

    def body(*refs):
        ins, lz, send_ref, recv_ref = refs[:n], refs[n:2 * n], refs[2 * n], refs[2 * n + 1]
        for cp in _chip_copies(ins, lz, send_ref, recv_ref):
            cp.wait_send()
            cp.wait_recv()

    outs = pl.pallas_call(
        body, name=name,
        out_shape=tuple(pltpu.HBM(a.shape, a.dtype) for a in operands),
        in_specs=[HBM_SPEC] * (2 * n) + [SEM_SPEC, SEM_SPEC, pl.BlockSpec(memory_space=pl.ANY)],
        out_specs=(HBM_SPEC,) * (2 * n),
        input_output_aliases={i: i for i in range(2 * n)},
        compiler_params=pltpu.CompilerParams(has_side_effects=SIDE_EFFECT),
    )(*operands, send_sems, recv_sems, after)
    return outs[:n], outs[n:]


def share_totals(totals, pack_total, last_part):
    arrays = list(totals) + [pack_total]
    n = len(arrays)

    def body(*refs):
        ins, last, outs, rep, last_all = refs[:n], refs[n], refs[n + 1:2 * n + 1], refs[2 * n + 1], refs[2 * n + 2]
        send_sems, recv_sems, rep_send, rep_recv, last_send, last_recv = refs[2 * n + 3:]
        x, y, c, chips = _position()
        sibling = (x, y, 1 - c)
        me = 4 * x + 2 * y + c
        sends = []
        for a in range(n):
            cp = _remote(ins[a], outs[a], send_sems.at[a], recv_sems.at[a], sibling)
            cp.start()
            sends.append(cp)
        mine = ins[n - 1].at[pl.ds(HALF_SHARDED, REP_PART)]
        peers = [sibling]
        for chip in chips:
            peers += [(chip[0], chip[1], c), (chip[0], chip[1], 1 - c)]
        for j, peer in enumerate(peers):
            for src, dst, s_sem, r_sem in ((mine, rep, rep_send, rep_recv), (last, last_all, last_send, last_recv)):
                cp = _remote(src, dst.at[me], s_sem.at[j], r_sem.at[j], peer)
                cp.start()
                sends.append(cp)
        for a in range(n):
            _remote(outs[a], outs[a], send_sems.at[a], recv_sems.at[a], sibling).wait_recv()
        for j, peer in enumerate(peers):
            it = 4 * peer[0] + 2 * peer[1] + peer[2]
            _remote(rep.at[it], rep.at[it], rep_send.at[j], rep_recv.at[j], peer).wait_recv()
            _remote(last_all.at[it], last_all.at[it], last_send.at[j], last_recv.at[j], peer).wait_recv()
        for cp in sends:
            cp.wait_send()

    outs = pl.pallas_call(
        body, name="grad_share_totals",
        out_shape=[jax.ShapeDtypeStruct(a.shape, a.dtype) for a in arrays]
        + [jax.ShapeDtypeStruct((8, REP_PART, LANES), F32), jax.ShapeDtypeStruct((8,) + last_part.shape, F32)],
        in_specs=[HBM_SPEC] * (n + 1), out_specs=[HBM_SPEC] * (n + 2),
        scratch_shapes=[pltpu.SemaphoreType.DMA((n,)), pltpu.SemaphoreType.DMA((n,))]
        + [pltpu.SemaphoreType.DMA((7,))] * 4,
    )(*arrays, last_part)
    return outs[:n], outs[n], outs[n + 1]


def sum_parts(parts, name):
    def body(p_ref, o_ref):
        total = p_ref[0]
        for k in range(1, parts.shape[0]):
            total = total + p_ref[k]
        o_ref[...] = total

    return pl.pallas_call(body, name=name, out_shape=jax.ShapeDtypeStruct(parts.shape[1:], parts.dtype))(parts)


TILE_BYTES = 2 << 20


def _row_tile(rows, cols):
    best = None
    for t in range(SUBLANES, rows + 1, SUBLANES):
        if rows % t == 0 and t * cols * 4 <= TILE_BYTES:
            best = t
    return best if best is not None else rows


def add_sibling(mine, received, core, out_dtype, name):
    _, _, r, cols = mine.shape
    tr = _row_tile(r, cols)

    def body(c_ref, a_ref, b_ref, o_ref):
        o_ref[...] = (a_ref[0] + b_ref[...].astype(F32)).astype(out_dtype)

    return pl.pallas_call(
        body, name=name,
        out_shape=jax.ShapeDtypeStruct((4, r, cols), out_dtype),
        grid_spec=pltpu.PrefetchScalarGridSpec(
            num_scalar_prefetch=1, grid=(4, r // tr),
            in_specs=[pl.BlockSpec((1, 1, tr, cols), lambda o, i, c_ref: (o, c_ref[0], i, 0)),
                      pl.BlockSpec((1, tr, cols), lambda o, i, c_ref: (o, i, 0))],
            out_specs=pl.BlockSpec((1, tr, cols), lambda o, i, c_ref: (o, i, 0))),
        compiler_params=_params("parallel", "parallel"),
    )(core, mine, received)


def add_chips(own, received, chip, name):
    _, r, cols = own.shape
    tr = _row_tile(r, cols)

    def body(p_ref, a_ref, b0, b1, b2, o_ref):
        o_ref[...] = ((a_ref[0].astype(F32) + b0[0].astype(F32)) + b1[0].astype(F32)) + b2[0].astype(F32)

    rb = lambda k: pl.BlockSpec((1, tr, cols), lambda i, p_ref: (k, i, 0))
    return pl.pallas_call(
        body, name=name,
        out_shape=jax.ShapeDtypeStruct((r, cols), F32),
        grid_spec=pltpu.PrefetchScalarGridSpec(
            num_scalar_prefetch=1, grid=(r // tr,),
            in_specs=[pl.BlockSpec((1, tr, cols), lambda i, p_ref: (p_ref[0], i, 0)), rb(0), rb(1), rb(2)],
            out_specs=pl.BlockSpec((tr, cols), lambda i, p_ref: (i, 0))),
        compiler_params=_params("parallel"),
    )(chip, own, received, received, received)


def _adamw_update(gv, w_ref, m_ref, v_ref, d_ref, nm_ref, nv_ref):
    nm = ADAM_B1 * m_ref[...] + (1.0 - ADAM_B1) * gv
    nv = ADAM_B2 * v_ref[...] + (1.0 - ADAM_B2) * (gv * gv)
    nm_ref[...] = nm
    nv_ref[...] = nv
    m_hat = nm / (1.0 - ADAM_B1 ** ADAM_STEP)
    v_hat = nv / (1.0 - ADAM_B2 ** ADAM_STEP)
    d_ref[...] = -ADAM_LR * (m_hat / (jnp.sqrt(v_hat) + ADAM_EPS) + ADAM_WD * w_ref[...])


def adamw_halves(w, own, received, m, v, core, name, by_columns=False):
    rows, cols = w.shape

    def body(c_ref, w_ref, own_ref, rec_ref, m_ref, v_ref, g_ref, d_ref, nm_ref, nv_ref):
        gv = jnp.where(pl.program_id(0) == c_ref[0], own_ref[...], rec_ref[...])
        g_ref[...] = gv
        _adamw_update(gv, w_ref, m_ref, v_ref, d_ref, nm_ref, nv_ref)

    if by_columns:
        nr = 1
        whole = pl.BlockSpec((rows, cols // 2), lambda h, i, c_ref: (0, h))
        half = pl.BlockSpec((rows, cols // 2), lambda h, i, c_ref: (0, 0))
    else:
        r = rows // 2
        tr = _row_tile(r, cols)
        nr = r // tr
        whole = pl.BlockSpec((tr, cols), lambda h, i, c_ref: (h * nr + i, 0))
        half = pl.BlockSpec((tr, cols), lambda h, i, c_ref: (i, 0))
    return pl.pallas_call(
        body, name=name,
        out_shape=(jax.ShapeDtypeStruct((rows, cols), F32),) * 4,
        grid_spec=pltpu.PrefetchScalarGridSpec(
            num_scalar_prefetch=1, grid=(2, nr),
            in_specs=[whole, half, half, whole, whole], out_specs=(whole,) * 4),
        compiler_params=_params("parallel", "parallel"),
    )(core, w, own, received, m, v)


def adamw_many(ws, gs, ms, vs, name):
    n = len(ws)

    def body(*refs):
        ins, outs = refs[:4 * n], refs[4 * n:]
        for k in range(n):
            w_ref, g_ref, m_ref, v_ref = (ins[j * n + k] for j in range(4))
            d_ref, nm_ref, nv_ref = outs[3 * k:3 * k + 3]
            _adamw_update(g_ref[...], w_ref, m_ref, v_ref, d_ref, nm_ref, nv_ref)

    flat = pl.pallas_call(
        body, name=name,
        out_shape=[jax.ShapeDtypeStruct(w.shape, F32) for w in ws for _ in range(3)],
    )(*ws, *gs, *ms, *vs)
    return [tuple(flat[3 * k:3 * k + 3]) for k in range(n)]


def adamw(w, g, m, v, name):
    r, cols = w.shape
    tr = _row_tile(r, cols)

    def body(w_ref, g_ref, m_ref, v_ref, g_out, d_ref, nm_ref, nv_ref):
        gv = g_ref[...]
        g_out[...] = gv
        _adamw_update(gv, w_ref, m_ref, v_ref, d_ref, nm_ref, nv_ref)

    blk = pl.BlockSpec((tr, cols), lambda i: (i, 0))
    return pl.pallas_call(
        body, name=name,
        out_shape=(jax.ShapeDtypeStruct((r, cols), F32),) * 4,
        grid=(r // tr,),
        in_specs=[blk] * 4, out_specs=(blk,) * 4,
        compiler_params=_params("parallel"),
    )(w, g, m, v)


WEIGHTS = ("even_norm_pre", "even_norm_post", "even_w_in", "rg_conv_w", "rg_conv_b", "rg_gate_w", "rg_gate_b",
           "rg_lambda", "sc_conv_w", "even_w_out", "odd_norm_pre", "odd_norm_post", "odd_w_in", "gla_w_gate_lr",
           "gla_b_gate", "gla_norm_g", "odd_w_out")
BIG = ("even_w_in", "even_w_out", "odd_w_in", "odd_w_out")


def _halves(a):
    return a.reshape((2, a.shape[0] // 2) + a.shape[1:])


def kernel(x, even_norm_pre, even_norm_post, even_w_in, rg_conv_w, rg_conv_b, rg_gate_w, rg_gate_b, rg_lambda, sc_conv_w, even_w_out, odd_norm_pre, odd_norm_post, odd_w_in, gla_w_gate_lr, gla_b_gate, gla_norm_g, odd_w_out, loss_target, m_even_norm_pre, m_even_norm_post, m_even_w_in, m_rg_conv_w, m_rg_conv_b, m_rg_gate_w, m_rg_gate_b, m_rg_lambda, m_sc_conv_w, m_even_w_out, m_odd_norm_pre, m_odd_norm_post, m_odd_w_in, m_gla_w_gate_lr, m_gla_b_gate, m_gla_norm_g, m_odd_w_out, v_even_norm_pre, v_even_norm_post, v_even_w_in, v_rg_conv_w, v_rg_conv_b, v_rg_gate_w, v_rg_gate_b, v_rg_lambda, v_sc_conv_w, v_even_w_out, v_odd_norm_pre, v_odd_norm_post, v_odd_w_in, v_gla_w_gate_lr, v_gla_b_gate, v_gla_norm_g, v_odd_w_out):
    given = dict(locals())
    shard = {n: given[n][0] for n in WEIGHTS}
    m_in = {n: given["m_" + n][0] for n in WEIGHTS}
    v_in = {n: given["v_" + n][0] for n in WEIGHTS}
    mx, my, mc = lax.axis_index("x"), lax.axis_index("y"), lax.axis_index("c")
    core = jnp.reshape(mc, (1,)).astype(jnp.int32)
    chip = jnp.reshape(2 * mx + my, (1,)).astype(jnp.int32)

    small_shard = _pack(shard, SHARDED_SMALL, SHARDED_ROWS)
    big_own = [_halves(shard[n].astype(BF16)) for n in BIG]
    started_a = gather_start(big_own[:1], [small_shard], "gather_start_a")
    started_b = gather_start(big_own[1:], [], "gather_start_b")
    even_w_in_full, small_full = gather_wait(started_a, 1, started_b[-1], "gather_wait_a")
    (even_w_in_full,) = pass_to_sibling([even_w_in_full], "gather_pass_a")
    even_w_in_full = place_own(even_w_in_full, big_own[0], chip, "place_even_w_in")
    small_full = lax.dynamic_update_slice(small_full, small_shard[None], (chip[0], 0, 0))
    full = {n: shard[n] for n, _ in REPLICATED + LAST_REPLICATED}
    full.update({n: _merge_owners(a) for n, a in _unpack(small_full, SHARDED_SMALL, lead=(4,)).items()})
    full["even_w_in"] = even_w_in_full.reshape(4, D_MODEL, EVEN_IN // 4)

    def late_weights(after):
        lands = pass_to_sibling(list(gather_wait(started_b, 3, after, "gather_wait_b")), "gather_pass_b")
        lands = [place_own(a, b, chip, "place_" + n) for a, b, n in zip(lands, big_own[1:], BIG[1:])]
        odd_w_in = jnp.transpose(lands[1].reshape(4, D_MODEL, ODD_IN // 4), (1, 0, 2)).reshape(D_MODEL, ODD_IN)
        return _prepare_weights({"even_w_out": lands[0].reshape(2 * D_MODEL, D_MODEL), "odd_w_in": odd_w_in,
                                 "odd_w_out": lands[2].reshape(D_MODEL, D_MODEL)})

    pending = {}

    def slab(a):
        return a.reshape((4, 2, a.shape[1] // 2) + a.shape[2:])

    def begin(tag, slabs, to_send, dtypes):
        got = exchange_with_sibling(to_send, "grad_sibling_" + tag)
        sums = [add_sibling(a, b, core, dt, "grad_add_sibling_%s%d" % (tag, i))
                for i, (a, b, dt) in enumerate(zip(slabs, got, dtypes))]
        pending[tag] = exchange_with_chips_start(sums, "grad_chips_start_" + tag)
        return pending[tag][-1][0, 0]

    def finish(tag, after):
        sums, got = exchange_with_chips_wait(pending[tag], after, "grad_chips_wait_" + tag)
        return [add_chips(a, b, chip, "grad_add_chips_%s%d" % (tag, i)) for i, (a, b) in enumerate(zip(sums, got))]

    def reduce_first(g, g16):
        odd_w_in = slab(jnp.transpose(g["odd_w_in"].reshape(D_MODEL, 4, ODD_IN // 4), (1, 0, 2)))
        slabs = [odd_w_in] + [slab(g[n].reshape(4, -1, D_MODEL)) for n in ("odd_w_out", "even_w_out")]
        to_send = [odd_w_in.astype(BF16)] + [slab(g16[n].reshape(4, -1, D_MODEL)) for n in ("odd_w_out", "even_w_out")]
        return begin("a", slabs, to_send, [BF16] * 3)

    def reduce_second(g, g16):
        pending["totals_a"] = finish("a", g["even_w_in"])
        rep_rows = _pack(g, REPLICATED, REPLICATED_ROWS).reshape(4, 2, REP_PART, LANES)
        sh_rows = _pack({n: _split_owners(g[n]) for n, _ in SHARDED_SMALL}, SHARDED_SMALL, SHARDED_ROWS, lead=(4,))
        pack = jnp.concatenate([sh_rows.reshape(4, 2, HALF_SHARDED, LANES), rep_rows], axis=2)
        return begin("b", [slab(g["even_w_in"]), pack], [slab(g16["even_w_in"]), pack], [BF16, F32])

    loss, grad_x, g = local_step(x[0], loss_target[0], _prepare_weights(full), reduce_first, reduce_second,
                                 late_weights)
    odd_w_in_t, odd_w_out_t, even_w_out_t = pending["totals_a"]
    even_w_in_t, pack_t = finish("b", grad_x)
    totals = [even_w_in_t, even_w_out_t, odd_w_in_t, odd_w_out_t]
    last_part = jnp.concatenate([_pack(g, LAST_REPLICATED, LAST_ROWS), loss])
    from_core, rep_all, last_all = share_totals(totals, pack_t, last_part)
    me = 2 * chip[0] + core[0]
    mine, theirs = pack_t[:HALF_SHARDED], from_core[4][:HALF_SHARDED]
    sh_total = jnp.where(mc == 0, jnp.concatenate([mine, theirs]), jnp.concatenate([theirs, mine]))
    rep_all = lax.dynamic_update_slice(rep_all, pack_t[None, HALF_SHARDED:], (me, 0, 0))
    rep_total = rep_all.reshape(REPLICATED_ROWS, LANES)
    last_total = sum_parts(lax.dynamic_update_slice(last_all, last_part[None], (me, 0, 0)), "grad_sum_last")
    last_total, loss = last_total[:LAST_ROWS], last_total[LAST_ROWS, 0]
    grads = {}

    delta, new_m, new_v = {}, {}, {}
    for i, n in enumerate(BIG):
        if shard[n].shape[1] % LANES:
            outs = adamw_halves(shard[n].T, totals[i].T, from_core[i].T, m_in[n].T, v_in[n].T, core, "adamw_" + n,
                                by_columns=True)
            grads[n], delta[n], new_m[n], new_v[n] = [o.T for o in outs]
        else:
            grads[n], delta[n], new_m[n], new_v[n] = adamw_halves(shard[n], totals[i], from_core[i], m_in[n],
                                                                  v_in[n], core, "adamw_" + n)
    gate = [src["rg_gate_w"].reshape(GATE_ROWS, LANES) for src in (shard, m_in, v_in)]
    grads["rg_gate_w"], delta["rg_gate_w"], new_m["rg_gate_w"], new_v["rg_gate_w"] = adamw(
        gate[0], rep_total, gate[1], gate[2], "adamw_rg_gate_w")
    rest = REPLICATED[1:]
    rest_rows = sum(_seg_rows(shape) for _, shape in rest)
    grads.update(_unpack(sh_total, SHARDED_SMALL))
    grads.update(_unpack(rep_total[GATE_ROWS:GATE_ROWS + rest_rows], rest))
    grads.update(_unpack(last_total, LAST_REPLICATED))
    names = [n for n, _ in SHARDED_SMALL + rest + LAST_REPLICATED]
    rows_of = lambda a, n: a.reshape(-1, given[n].shape[-1])
    outs = adamw_many([rows_of(given[n], n) for n in names], [rows_of(grads[n], n) for n in names],
                      [rows_of(given["m_" + n], n) for n in names], [rows_of(given["v_" + n], n) for n in names],
                      "adamw_small")
    for n, (d, nm, nv) in zip(names, outs):
        delta[n], new_m[n], new_v[n] = d, nm, nv
    result = [loss, grad_x[None]]
    for group in (grads, delta, new_m, new_v):
        result += [group[n].reshape(given[n].shape) for n in WEIGHTS]
    return tuple(result)
```

```python
import functools

import jax
import jax.numpy as jnp
from jax import lax
from jax.experimental import pallas as pl
from jax.experimental.pallas import tpu as pltpu

F32 = jnp.float32
BF16 = jnp.bfloat16
MESH = pl.DeviceIdType.MESH

D_MODEL = 1024
NORM_EPS = 1e-6
RG_HEADS = 8
RG_HEAD_DIM = 128
RG_C = 8.0
EVEN_IN = 6144
ODD_IN = 3104
ODD_IN_PAD = 3200
GLA_HEADS = 4
GLA_DK = 128
GLA_DV = 256
GLA_RANK = 16
GLA_NORMALIZER = 16.0
GLA_CHUNK = 128
LR_COL = 3072

ADAM_LR = 0.001
ADAM_B1 = 0.9
ADAM_B2 = 0.999
ADAM_EPS = 1e-08
ADAM_WD = 0.01
ADAM_STEP = 10

SUBLANES = 8
HALO = 16
LANES = 128
VMEM_LIMIT = 56 * 2 ** 20

ROW_TILE = 512
SCAN_TILE = 256
GLA_BLOCK = 1024
MIX_TILE = 128


def _params(*sem):
    return pltpu.CompilerParams(dimension_semantics=sem, vmem_limit_bytes=VMEM_LIMIT)


def _full(shape):
    n = len(shape)
    return pl.BlockSpec(shape, lambda *_: (0,) * n)


def _sigmoid(x):
    return 0.5 + 0.5 * jnp.tanh(0.5 * x)


def _softplus(x):
    return jnp.maximum(x, 0.0) + jnp.log(1.0 + jnp.exp(-jnp.abs(x)))


def _dot(a, b):
    return jnp.dot(a, b, preferred_element_type=F32)


def _dot_nt(a, b):
    return lax.dot_general(a, b, (((1,), (1,)), ((), ())), preferred_element_type=F32)


def _dot_tn(a, b):
    return lax.dot_general(a, b, (((0,), (0,)), ((), ())), preferred_element_type=F32)


def _bdot(a, b, ca, cb):
    return lax.dot_general(a, b, (((ca,), (cb,)), ((0,), (0,))), preferred_element_type=F32)


def _halo_specs(rows, cols, col_block, n_row_tiles, tix):
    per = rows // HALO
    last = n_row_tiles * per - 1

    def split(args):
        if len(args) == 2:
            return tix(args[1]), col_block + args[0]
        return tix(args[0]), col_block

    def prev(*args):
        t, c = split(args)
        return (jnp.maximum(t * per - 1, 0), c)

    def main(*args):
        return split(args)

    def nxt(*args):
        t, c = split(args)
        return (jnp.minimum((t + 1) * per, last), c)

    return [pl.BlockSpec((HALO, cols), prev), pl.BlockSpec((rows, cols), main),
            pl.BlockSpec((HALO, cols), nxt)]


def _extend(prev_ref, main_ref, next_ref, is_first, is_last):
    p = jnp.where(is_first, 0.0, prev_ref[...].astype(F32))
    n = jnp.where(is_last, 0.0, next_ref[...].astype(F32))
    return jnp.concatenate([p, main_ref[...].astype(F32), n], axis=0)


def _shifted(ext, offset, rows):
    if offset == 0:
        return ext[HALO:HALO + rows]
    n = ext.shape[0]
    return pltpu.roll(ext, (-offset) % n, 0)[HALO:HALO + rows]


def _conv(ext, w, left, rows):
    out = None
    for k in range(w.shape[0]):
        term = _shifted(ext, k - left, rows) * w[k:k + 1]
        out = term if out is None else out + term
    return out


def _conv_transpose(ext, w, left, rows):
    out = None
    for k in range(w.shape[0]):
        term = _shifted(ext, left - k, rows) * w[k:k + 1]
        out = term if out is None else out + term
    return out


def _colsum(x):
    return jnp.sum(x, axis=0, keepdims=True)


def _accumulate(ref, value, step):
    @pl.when(step == 0)
    def _():
        ref[...] = value

    @pl.when(step > 0)
    def _():
        ref[...] += value


PROJ_TILE_BYTES = 7 * 2 ** 20


def _proj_row_tile(rows, width):
    tm = min(ROW_TILE, rows)
    while tm * width * 4 > PROJ_TILE_BYTES and tm % (2 * SUBLANES) == 0:
        tm //= 2
    return tm


def norm_matmul(x, gain, w, out_dtype, name):
    rows, d = x.shape
    n_col_tiles, _, tn = w.shape
    tm = _proj_row_tile(rows, n_col_tiles * tn)

    def body(x_ref, g_ref, w_ref, proj_ref, h_ref):
        xv = x_ref[...]
        rstd = lax.rsqrt(jnp.mean(xv * xv, axis=-1, keepdims=True) + NORM_EPS)
        hv = (xv * rstd * g_ref[...]).astype(BF16)
        h_ref[...] = hv
        for j in range(n_col_tiles):
            proj_ref[:, j * tn:(j + 1) * tn] = _dot(hv, w_ref[j]).astype(out_dtype)

    row = lambda cols: pl.BlockSpec((tm, cols), lambda i: (i, 0))
    return pl.pallas_call(
        body, name=name,
        out_shape=(jax.ShapeDtypeStruct((rows, n_col_tiles * tn), out_dtype), jax.ShapeDtypeStruct((rows, d), BF16)),
        grid=(rows // tm,),
        in_specs=[row(d), _full((1, d)), _full(w.shape)],
        out_specs=(row(n_col_tiles * tn), row(d)),
        compiler_params=_params("parallel"),
    )(x, gain, w)


def inproj_bwd(dproj, w, x, gain, dres, name):
    rows, d = x.shape
    n_col_tiles, _, tn = w.shape
    tm = _proj_row_tile(rows, n_col_tiles * tn)

    def body(dp_ref, w_ref, x_ref, g_ref, dres_ref, dx_ref, dg_ref):
        dh = None
        for j in range(n_col_tiles):
            part = _dot_nt(dp_ref[:, j * tn:(j + 1) * tn], w_ref[j])
            dh = part if dh is None else dh + part
        _inproj_finish(dh, x_ref, g_ref, dres_ref, dx_ref, dg_ref, pl.program_id(0))

    row = lambda cols: pl.BlockSpec((tm, cols), lambda i: (i, 0))
    return pl.pallas_call(
        body, name=name,
        out_shape=(jax.ShapeDtypeStruct((rows, d), F32), jax.ShapeDtypeStruct((1, d), F32)),
        grid=(rows // tm,),
        in_specs=[row(n_col_tiles * tn), _full(w.shape), row(d), _full((1, d)), row(d)],
        out_specs=(row(d), _full((1, d))),
        compiler_params=_params("arbitrary"),
    )(dproj, w, x, gain, dres)


def _inproj_finish(dh, x_ref, g_ref, dres_ref, dx_ref, dg_ref, step):
    xv = x_ref[...]
    rstd = lax.rsqrt(jnp.mean(xv * xv, axis=-1, keepdims=True) + NORM_EPS)
    xhat = xv * rstd
    dxn = dh * g_ref[...]
    dx_ref[...] = dres_ref[...] + rstd * (dxn - xhat * jnp.mean(dxn * xhat, axis=-1, keepdims=True))
    _accumulate(dg_ref, _colsum(dh * xhat), step)


def inproj_bwd_pieces(pieces, w, x, gain, dres, name):
    rows, d = x.shape
    tm = min(ROW_TILE, rows)
    n = len(pieces)
    widths = [p.shape[1] for p in pieces]
    starts = [sum(widths[:k]) for k in range(n)]
    assert sum(widths) == w.shape[2]

    def body(*refs):
        w_ref, x_ref, g_ref, dres_ref, dx_ref, dg_ref = refs[n:]
        dh = None
        for k in range(n):
            part = _dot_nt(refs[k][...], w_ref[0, :, starts[k]:starts[k] + widths[k]])
            dh = part if dh is None else dh + part
        _inproj_finish(dh, x_ref, g_ref, dres_ref, dx_ref, dg_ref, pl.program_id(0))

    row = lambda cols: pl.BlockSpec((tm, cols), lambda i: (i, 0))
    return pl.pallas_call(
        body, name=name,
        out_shape=(jax.ShapeDtypeStruct((rows, d), F32), jax.ShapeDtypeStruct((1, d), F32)),
        grid=(rows // tm,),
        in_specs=[row(wd) for wd in widths] + [_full(w.shape), row(d), _full((1, d)), row(d)],
        out_specs=(row(d), _full((1, d))),
        compiler_params=_params("arbitrary"),
    )(*pieces, w, x, gain, dres)


def matmul_dw_pieces(a, pieces, name):
    rows, m = a.shape
    tk = min(2 * ROW_TILE, rows)
    n = len(pieces)

    def body(*refs):
        a_ref, ins, outs = refs[0], refs[1:1 + n], refs[1 + n:]
        av = a_ref[...]
        for k in range(n):
            _accumulate(outs[k], _dot_tn(av, ins[k][...]), pl.program_id(0))

    return pl.pallas_call(
        body, name=name,
        out_shape=[jax.ShapeDtypeStruct((m, p.shape[1]), F32) for p in pieces],
        grid=(rows // tk,),
        in_specs=[pl.BlockSpec((tk, m), lambda k: (k, 0))]
        + [pl.BlockSpec((tk, p.shape[1]), lambda k: (k, 0)) for p in pieces],
        out_specs=[_full((m, p.shape[1])) for p in pieces],
        compiler_params=_params("arbitrary"),
    )(a, *pieces)


def matmul_dw(a, b, bn, name):
    rows, m = a.shape
    n = b.shape[1]
    tk = min((4 if n > bn else 2) * ROW_TILE, rows)
    steps = rows // tk

    def body(a_ref, b_ref, o_ref, o16_ref):
        part = _dot_tn(a_ref[...], b_ref[...])

        @pl.when(pl.program_id(1) == 0)
        def _():
            o_ref[0] = part

        @pl.when(pl.program_id(1) > 0)
        def _():
            o_ref[0] += part

        @pl.when(pl.program_id(1) == steps - 1)
        def _():
            o16_ref[0] = o_ref[0].astype(BF16)

    out = pl.BlockSpec((1, m, bn), lambda j, k: (j, 0, 0))
    return pl.pallas_call(
        body, name=name,
        out_shape=(jax.ShapeDtypeStruct((n // bn, m, bn), F32), jax.ShapeDtypeStruct((n // bn, m, bn), BF16)),
        grid=(n // bn, steps),
        in_specs=[pl.BlockSpec((tk, m), lambda j, k: (k, 0)), pl.BlockSpec((tk, bn), lambda j, k: (k, j))],
        out_specs=(out, out),
        compiler_params=_params("parallel", "arbitrary"),
    )(a, b)


def _scan(a, b, carry, reverse):
    n, c = a.shape
    blocks = n // SUBLANES
    a = a.reshape(blocks, SUBLANES, c)
    b = b.reshape(blocks, SUBLANES, c)
    pos = lax.broadcasted_iota(jnp.int32, (1, SUBLANES, c), 1)
    s = 1
    while s < SUBLANES:
        shift, valid = (SUBLANES - s, pos < SUBLANES - s) if reverse else (s, pos >= s)
        a_s, b_s = pltpu.roll(a, shift, 1), pltpu.roll(b, shift, 1)
        b = jnp.where(valid, a * b_s + b, b)
        a = jnp.where(valid, a * a_s, a)
        s *= 2
    out = [None] * blocks
    for k in (range(blocks - 1, -1, -1) if reverse else range(blocks)):
        h = a[k] * carry + b[k]
        out[k] = h
        carry = h[0:1] if reverse else h[SUBLANES - 1:SUBLANES]
    return jnp.concatenate(out, axis=0)


def _rg_gates(ua, gw_ref, gb, lam):
    ub = ua.astype(BF16)
    pre_r, pre_i = [], []
    for h in range(RG_HEADS):
        z = _dot(ub[:, h * RG_HEAD_DIM:(h + 1) * RG_HEAD_DIM], gw_ref[h])
        pre_r.append(z[:, :RG_HEAD_DIM])
        pre_i.append(z[:, RG_HEAD_DIM:])
    r = _sigmoid(jnp.concatenate(pre_r, axis=1) + gb[0:1])
    i = _sigmoid(jnp.concatenate(pre_i, axis=1) + gb[1:2])
    sp = _softplus(-lam)
    log_a = -RG_C * r * sp
    a = jnp.exp(log_a)
    mult = jnp.sqrt(1.0 - a * a)
    return r, i, sp, a, mult


def _rg_weight_specs():
    return [_full((4, D_MODEL)), _full((1, D_MODEL)), _full((RG_HEADS, RG_HEAD_DIM, 2 * RG_HEAD_DIM)),
            _full((2, D_MODEL)), _full((1, D_MODEL))]


def rglru_fwd(proj, conv_w, conv_b, gate_w, gate_b, lam, reverse, name):
    rows_total = proj.shape[0]
    rows = min(SCAN_TILE, rows_total)
    n_tiles = rows_total // rows
    tix = (lambda i: n_tiles - 1 - i) if reverse else (lambda i: i)

    def body(xp, xm, xn, cw_ref, cb_ref, gw_ref, gb_ref, lam_ref, h_ref, acts_ref, carry):
        i = pl.program_id(0)
        t = tix(i)
        ext = _extend(xp, xm, xn, t == 0, t == n_tiles - 1)
        ua = _conv(ext, cw_ref[...], 2, rows) + cb_ref[...]
        r, gi, _, a, mult = _rg_gates(ua, gw_ref, gb_ref[...], lam_ref[...])
        for k, saved in enumerate((ua, r, gi, a, mult)):
            acts_ref[k] = saved
        b = mult * (gi * ua)

        @pl.when(i == 0)
        def _():
            carry[...] = jnp.zeros_like(carry)

        h = _scan(a, b, carry[0:1], reverse)
        h_ref[...] = h
        edge = h[0:1] if reverse else h[rows - 1:rows]
        carry[...] = jnp.broadcast_to(edge, carry.shape)

    return pl.pallas_call(
        body, name=name,
        out_shape=(jax.ShapeDtypeStruct((rows_total, D_MODEL), F32),
                   jax.ShapeDtypeStruct((5, rows_total, D_MODEL), F32)),
        grid=(n_tiles,),
        in_specs=_halo_specs(rows, D_MODEL, 0, n_tiles, tix) + _rg_weight_specs(),
        out_specs=(pl.BlockSpec((rows, D_MODEL), lambda i: (tix(i), 0)),
                   pl.BlockSpec((5, rows, D_MODEL), lambda i: (0, tix(i), 0))),
        scratch_shapes=[pltpu.VMEM((SUBLANES, D_MODEL), F32)],
        compiler_params=_params("arbitrary"),
    )(proj, proj, proj, conv_w, conv_b, gate_w, gate_b, lam)


def rglru_bwd(proj, dycat, h_dir, acts, gate_w, lam, add_dua, reverse, name):
    rows_total = proj.shape[0]
    rows = min(SCAN_TILE, rows_total)
    n_tiles = rows_total // rows
    tix = (lambda i: i) if reverse else (lambda i: n_tiles - 1 - i)
    za_block = 1

    def body(acts_ref, za_ref, dya_ref, hp, hm, hn, gw_ref, lam_ref, *rest):
        other = rest[0][...] if add_dua is not None else 0.0
        dua_ref, dgw_ref, dgb_ref, dlam_ref, carry = rest[-5:]
        step = pl.program_id(0)
        t = tix(step)
        first, last = t == 0, t == n_tiles - 1
        ua, r, gi, a, mult = (acts_ref[k] for k in range(5))
        lam_v = lam_ref[...]
        sp = _softplus(-lam_v)
        za = za_ref[...].astype(F32)
        dh = dya_ref[...] * (za * _sigmoid(za))

        @pl.when(step == 0)
        def _():
            carry[...] = jnp.zeros_like(carry)

        old = carry[0:1]
        mu = _scan(a, a * dh, old, not reverse)
        row = lax.broadcasted_iota(jnp.int32, mu.shape, 0)
        if reverse:
            mu_next = jnp.where(row == 0, old, pltpu.roll(mu, 1, 0))
            carry[...] = jnp.broadcast_to(mu[rows - 1:rows], carry.shape)
            h_ext = _extend(hp, hm, hn, first, last)
            h_prev = _shifted(h_ext, 1, rows)
        else:
            mu_next = jnp.where(row == rows - 1, old, pltpu.roll(mu, rows - 1, 0))
            carry[...] = jnp.broadcast_to(mu[0:1], carry.shape)
            h_ext = _extend(hp, hm, hn, first, last)
            h_prev = _shifted(h_ext, -1, rows)
        db = dh + mu_next
        da = db * h_prev
        d_mult = db * (gi * ua)
        di = db * (mult * ua)
        dua = db * (mult * gi)
        dlog_a = da * a - d_mult * (a * a) / mult
        dr = dlog_a * (-RG_C * sp)
        dlam = _colsum(dlog_a * (-RG_C * r)) * (-_sigmoid(-lam_v))
        dpr = dr * (r * (1.0 - r))
        dpi = di * (gi * (1.0 - gi))
        dgb = jnp.concatenate([_colsum(dpr), _colsum(dpi)], axis=0)
        ub = ua.astype(BF16)
        dua_heads, dgw_heads = [], []
        for h in range(RG_HEADS):
            cols = slice(h * RG_HEAD_DIM, (h + 1) * RG_HEAD_DIM)
            dz = jnp.concatenate([dpr[:, cols], dpi[:, cols]], axis=1).astype(BF16)
            dgw_heads.append(_dot_tn(ub[:, cols], dz))
            dua_heads.append(_dot_nt(dz, gw_ref[h]))
        dua_ref[...] = dua + jnp.concatenate(dua_heads, axis=1) + other

        @pl.when(step == 0)
        def _():
            for h in range(RG_HEADS):
                dgw_ref[h] = dgw_heads[h]
            dgb_ref[...] = dgb
            dlam_ref[...] = dlam

        @pl.when(step > 0)
        def _():
            for h in range(RG_HEADS):
                dgw_ref[h] += dgw_heads[h]
            dgb_ref[...] += dgb
            dlam_ref[...] += dlam

    row_spec = lambda col: pl.BlockSpec((rows, D_MODEL), lambda i: (tix(i), col))
    return pl.pallas_call(
        body, name=name,
        out_shape=(jax.ShapeDtypeStruct((rows_total, D_MODEL), F32),
                   jax.ShapeDtypeStruct((RG_HEADS, RG_HEAD_DIM, 2 * RG_HEAD_DIM), F32),
                   jax.ShapeDtypeStruct((2, D_MODEL), F32), jax.ShapeDtypeStruct((1, D_MODEL), F32)),
        grid=(n_tiles,),
        in_specs=([pl.BlockSpec((5, rows, D_MODEL), lambda i: (0, tix(i), 0)), row_spec(za_block), row_spec(0)]
                  + _halo_specs(rows, D_MODEL, 0, n_tiles, tix)
                  + [_full((RG_HEADS, RG_HEAD_DIM, 2 * RG_HEAD_DIM)), _full((1, D_MODEL))]
                  + ([] if add_dua is None else [row_spec(0)])),
        out_specs=(row_spec(0), _full((RG_HEADS, RG_HEAD_DIM, 2 * RG_HEAD_DIM)), _full((2, D_MODEL)),
                   _full((1, D_MODEL))),
        scratch_shapes=[pltpu.VMEM((SUBLANES, D_MODEL), F32)],
        compiler_params=_params("arbitrary"),
    )(acts, proj, dycat, h_dir, h_dir, h_dir, gate_w, lam, *([] if add_dua is None else [add_dua]))


def _extend_cols(refs, block, is_first, is_last):
    cols = slice(block * D_MODEL, (block + 1) * D_MODEL)
    prev_ref, main_ref, next_ref = refs
    p = jnp.where(is_first, 0.0, prev_ref[:, cols].astype(F32))
    n = jnp.where(is_last, 0.0, next_ref[:, cols].astype(F32))
    return jnp.concatenate([p, main_ref[:, cols].astype(F32), n], axis=0)


def even_mix_fwd(proj, h_f, h_b, sc_w, name):
    rows_total = proj.shape[0]
    rows = min(2 * MIX_TILE, rows_total)
    n_tiles = rows_total // rows
    ident = lambda i: i

    def body(za_ref, hf_ref, hb_ref, xbp, xbm, xbn, gcp, gcm, gcn, gb_ref, zb_ref, w_ref, y_ref):
        t = pl.program_id(0)
        first, last = t == 0, t == n_tiles - 1
        za = za_ref[...].astype(F32)
        y_ref[:, 0:D_MODEL] = ((hf_ref[...] + hb_ref[...]) * (za * _sigmoid(za))).astype(BF16)
        p_ext = _extend(xbp, xbm, xbn, first, last) * _extend(gcp, gcm, gcn, first, last)
        cv = _conv(p_ext, w_ref[...], 1, rows)
        zb = zb_ref[...].astype(F32)
        y_ref[:, D_MODEL:2 * D_MODEL] = (gb_ref[...].astype(F32) * cv * (zb * _sigmoid(zb))).astype(BF16)

    blk = lambda col: pl.BlockSpec((rows, D_MODEL), lambda i: (i, col))
    return pl.pallas_call(
        body, name=name,
        out_shape=jax.ShapeDtypeStruct((rows_total, 2 * D_MODEL), BF16),
        grid=(n_tiles,),
        in_specs=([blk(1), blk(0), blk(0)] + _halo_specs(rows, D_MODEL, 2, n_tiles, ident)
                  + _halo_specs(rows, D_MODEL, 4, n_tiles, ident) + [blk(3), blk(5), _full((3, D_MODEL))]),
        out_specs=pl.BlockSpec((rows, 2 * D_MODEL), lambda i: (i, 0)),
        compiler_params=_params("parallel"),
    )(proj, h_f, h_b, proj, proj, proj, proj, proj, proj, proj, proj, sc_w)


def even_mix_bwd(proj, dycat, h_f, h_b, dua, conv_w, sc_w, name):
    rows_total, width = proj.shape
    rows = min(MIX_TILE, rows_total)
    n_tiles = rows_total // rows
    ident = lambda i: i

    def body(pp, pm, pn, dyp, dym, dyn, hf_ref, hb_ref, dup, dum, dun, cw_ref, sw_ref,
             dp_ref, dcw_ref, dcb_ref, dsw_ref):
        def put(k, value):
            dp_ref[:, k * D_MODEL:(k + 1) * D_MODEL] = value.astype(BF16)

        t = pl.program_id(0)
        first, last = t == 0, t == n_tiles - 1
        proj_ext = lambda k: _extend_cols((pp, pm, pn), k, first, last)
        mid = slice(HALO, HALO + rows)
        za = pm[:, D_MODEL:2 * D_MODEL].astype(F32)
        sa = _sigmoid(za)
        put(1, dym[:, 0:D_MODEL] * (hf_ref[...] + hb_ref[...]) * (sa * (1.0 + za * (1.0 - sa))))
        dua_ext = _extend(dup, dum, dun, first, last)
        cw = cw_ref[...]
        put(0, _conv_transpose(dua_ext, cw, 2, rows))
        dua_mid = dua_ext[mid]
        xa_ext = proj_ext(0)
        dcw = jnp.concatenate([_colsum(dua_mid * _shifted(xa_ext, k - 2, rows)) for k in range(4)], axis=0)
        dcb = _colsum(dua_mid)
        xb_ext, gb_ext, gc_ext, zb_ext = proj_ext(2), proj_ext(3), proj_ext(4), proj_ext(5)
        p_ext = xb_ext * gc_ext
        sb_ext = _sigmoid(zb_ext)
        dyb_ext = _extend_cols((dyp, dym, dyn), 1, first, last)
        dcv_ext = dyb_ext * gb_ext * (zb_ext * sb_ext)
        sw = sw_ref[...]
        p_at = [_shifted(p_ext, k - 1, rows) for k in range(3)]
        cv = (p_at[0] * sw[0:1] + p_at[1] * sw[1:2]) + p_at[2] * sw[2:3]
        zb, sb, dyb, gb = zb_ext[mid], sb_ext[mid], dyb_ext[mid], gb_ext[mid]
        put(3, dyb * cv * (zb * sb))
        put(5, dyb * gb * cv * (sb * (1.0 + zb * (1.0 - sb))))
        dp = _conv_transpose(dcv_ext, sw, 1, rows)
        put(4, dp * xb_ext[mid])
        put(2, dp * gc_ext[mid])
        dcv = dcv_ext[mid]
        dsw = jnp.concatenate([_colsum(dcv * p_at[k]) for k in range(3)], axis=0)
        _accumulate(dcw_ref, dcw, t)
        _accumulate(dcb_ref, dcb, t)
        _accumulate(dsw_ref, dsw, t)

    own = pl.BlockSpec((rows, D_MODEL), lambda i: (i, 0))
    return pl.pallas_call(
        body, name=name,
        out_shape=(jax.ShapeDtypeStruct((rows_total, 6 * D_MODEL), BF16),
                   jax.ShapeDtypeStruct((4, D_MODEL), F32), jax.ShapeDtypeStruct((1, D_MODEL), F32),
                   jax.ShapeDtypeStruct((3, D_MODEL), F32)),
        grid=(n_tiles,),
        in_specs=(_halo_specs(rows, width, 0, n_tiles, ident) + _halo_specs(rows, 2 * D_MODEL, 0, n_tiles, ident)
                  + [own, own] + _halo_specs(rows, D_MODEL, 0, n_tiles, ident)
                  + [_full((4, D_MODEL)), _full((3, D_MODEL))]),
        out_specs=(pl.BlockSpec((rows, 6 * D_MODEL), lambda i: (i, 0)), _full((4, D_MODEL)), _full((1, D_MODEL)),
                   _full((3, D_MODEL))),
        compiler_params=_params("arbitrary"),
    )(proj, proj, proj, dycat, dycat, dycat, h_f, h_b, dua, dua, dua, conv_w, sc_w)


def even_out_fwd(ycat, w_out, gain, x, name):
    rows, d = x.shape
    k = ycat.shape[1]
    tm = min(ROW_TILE, rows)

    def body(yc_ref, w_ref, g_ref, x_ref, x1_ref, y_ref):
        y = _dot(yc_ref[...], w_ref[...])
        y_ref[...] = y
        rstd = lax.rsqrt(jnp.mean(y * y, axis=-1, keepdims=True) + NORM_EPS)
        x1_ref[...] = x_ref[...] + y * rstd * g_ref[...]

    row = lambda n: pl.BlockSpec((tm, n), lambda i: (i, 0))
    return pl.pallas_call(
        body, name=name,
        out_shape=(jax.ShapeDtypeStruct((rows, d), F32),) * 2,
        grid=(rows // tm,),
        in_specs=[row(k), _full((k, d)), _full((1, d)), row(d)],
        out_specs=(row(d), row(d)),
        compiler_params=_params("parallel"),
    )(ycat, w_out, gain, x)


def _rmsnorm_bwd(dout, y, gain):
    rstd = lax.rsqrt(jnp.mean(y * y, axis=-1, keepdims=True) + NORM_EPS)
    yhat = y * rstd
    dyn = dout * gain
    dy = rstd * (dyn - yhat * jnp.mean(dyn * yhat, axis=-1, keepdims=True))
    return dy, dout * yhat


def even_out_bwd(dx1, y, gain, w_out, name):
    rows, d = y.shape
    k = w_out.shape[0]
    tm = min(ROW_TILE, rows)

    def body(dx_ref, y_ref, g_ref, w_ref, dy_ref, dyc_ref, dg_ref):
        dy, dg_rows = _rmsnorm_bwd(dx_ref[...], y_ref[...], g_ref[...])
        dyb = dy.astype(BF16)
        dy_ref[...] = dyb
        dyc_ref[...] = _dot_nt(dyb, w_ref[...])
        _accumulate(dg_ref, _colsum(dg_rows), pl.program_id(0))

    row = lambda n: pl.BlockSpec((tm, n), lambda i: (i, 0))
    return pl.pallas_call(
        body, name=name,
        out_shape=(jax.ShapeDtypeStruct((rows, d), BF16), jax.ShapeDtypeStruct((rows, k), F32),
                   jax.ShapeDtypeStruct((1, d), F32)),
        grid=(rows // tm,),
        in_specs=[row(d), row(d), _full((1, d)), _full((k, d))],
        out_specs=(row(d), row(k), _full((1, d))),
        compiler_params=_params("arbitrary"),
    )(dx1, y, gain, w_out)


def _chunk_cumsum(g, reverse):
    n, c = g.shape
    chunks, per = n // GLA_CHUNK, GLA_CHUNK // SUBLANES
    g = g.reshape(n // SUBLANES, SUBLANES, c)
    pos = lax.broadcasted_iota(jnp.int32, (1, SUBLANES, c), 1)
    s = 1
    while s < SUBLANES:
        if reverse:
            g = g + jnp.where(pos < SUBLANES - s, pltpu.roll(g, SUBLANES - s, 1), 0.0)
        else:
            g = g + jnp.where(pos >= s, pltpu.roll(g, s, 1), 0.0)
        s *= 2
    g = g.reshape(chunks, per, SUBLANES, c)
    out, carry = [None] * per, None
    for k in (range(per - 1, -1, -1) if reverse else range(per)):
        out[k] = g[:, k] if carry is None else g[:, k] + carry
        carry = out[k][:, 0:1] if reverse else out[k][:, SUBLANES - 1:SUBLANES]
    return jnp.stack(out, axis=1).reshape(n, c)


def _gla_prepare(q_ref, k_ref, lr_ref, wg_ref, bg_ref, reverse, n_chunks):
    z = _dot(lr_ref[...].astype(BF16), wg_ref[0]) + bg_ref[0]
    g = -_softplus(-z) * (1.0 / GLA_NORMALIZER)
    bcum = _chunk_cumsum(g, reverse).reshape(n_chunks, GLA_CHUNK, GLA_DK)
    edge = 0 if reverse else GLA_CHUNK - 1
    btot = bcum[:, edge:edge + 1, :]
    e_pos = jnp.exp(bcum)
    e_neg = jnp.exp(-bcum)
    e_st = jnp.exp(btot - bcum)
    q3 = q_ref[...].reshape(n_chunks, GLA_CHUNK, GLA_DK)
    k3 = k_ref[...].reshape(n_chunks, GLA_CHUNK, GLA_DK)
    scale = GLA_DK ** -0.5
    q_in = q3 * scale * e_pos
    k_in = k3 * e_neg
    k_st = k3 * e_st
    dec = jnp.exp(btot)
    return z, q_in, k_in, k_st, dec, (scale * e_pos, e_neg, e_st)


def _gla_mask(reverse):
    i = lax.broadcasted_iota(jnp.int32, (GLA_CHUNK, GLA_CHUNK), 0)
    j = lax.broadcasted_iota(jnp.int32, (GLA_CHUNK, GLA_CHUNK), 1)
    return (j >= i) if reverse else (j <= i)


def _gla_specs(rows, n_blocks, reverse):
    tix = (lambda s: n_blocks - 1 - s) if reverse else (lambda s: s)
    d = 1 if reverse else 0
    lr_block = LR_COL // LANES
    specs = [pl.BlockSpec((rows, GLA_DK), lambda h, s: (tix(s), h)),
             pl.BlockSpec((rows, GLA_DK), lambda h, s: (tix(s), GLA_HEADS + h)),
             pl.BlockSpec((rows, GLA_DV), lambda h, s: (tix(s), GLA_HEADS + h)),
             pl.BlockSpec((rows, LANES), lambda h, s: (tix(s), lr_block)),
             pl.BlockSpec((1, LANES, GLA_DK), lambda h, s: (d, 0, h)),
             pl.BlockSpec((1, 1, GLA_DK), lambda h, s: (d, 0, h))]
    return specs, tix


def gla_fwd(proj, wg_pad, bg, add_o, reverse, name):
    rows_total = proj.shape[0]
    rows = min(GLA_BLOCK, rows_total)
    n_blocks = rows_total // rows
    n_chunks = rows // GLA_CHUNK
    specs, tix = _gla_specs(rows, n_blocks, reverse)

    def body(q_ref, k_ref, v_ref, lr_ref, wg_ref, bg_ref, *rest):
        o_ref, st_ref, state, kv_scr, dec_scr = rest[-5:]
        _, q_in, k_in, k_st, dec, _ = _gla_prepare(q_ref, k_ref, lr_ref, wg_ref, bg_ref, reverse, n_chunks)
        vb = v_ref[...].reshape(n_chunks, GLA_CHUNK, GLA_DV).astype(BF16)
        qb = q_in.astype(BF16)
        p = jnp.where(_gla_mask(reverse), _bdot(qb, k_in.astype(BF16), 2, 2), 0.0)
        o = _bdot(p.astype(BF16), vb, 2, 1)
        kv_scr[...] = _bdot(vb, k_st.astype(BF16), 1, 1)
        dec_scr[...] = jnp.broadcast_to(dec, dec_scr.shape)

        @pl.when(pl.program_id(1) == 0)
        def _():
            state[...] = jnp.zeros_like(state)

        for c in range(n_chunks):
            cc = n_chunks - 1 - c if reverse else c
            st_ref[0, cc] = state[...]
            state[...] = state[...] * dec_scr[cc, 0:1] + kv_scr[cc]
        o = o + _bdot(qb, st_ref[0].astype(BF16), 2, 2)
        o = o.reshape(rows, GLA_DV)
        o_ref[...] = o if add_o is None else o + rest[0][...]

    o_spec = pl.BlockSpec((rows, GLA_DV), lambda h, s: (tix(s), h))
    return pl.pallas_call(
        body, name=name,
        out_shape=(jax.ShapeDtypeStruct((rows_total, GLA_HEADS * GLA_DV), F32),
                   jax.ShapeDtypeStruct((GLA_HEADS, rows_total // GLA_CHUNK, GLA_DV, GLA_DK), F32)),
        grid=(GLA_HEADS, n_blocks),
        in_specs=specs + ([] if add_o is None else [o_spec]),
        out_specs=(o_spec,
                   pl.BlockSpec((1, n_chunks, GLA_DV, GLA_DK), lambda h, s: (h, tix(s), 0, 0))),
        scratch_shapes=[pltpu.VMEM((GLA_DV, GLA_DK), F32), pltpu.VMEM((n_chunks, GLA_DV, GLA_DK), F32),
                        pltpu.VMEM((n_chunks, SUBLANES, GLA_DK), F32)],
        compiler_params=_params("parallel", "arbitrary"),
    )(proj, proj, proj, proj, wg_pad, bg, *([] if add_o is None else [add_o]))


def gla_bwd(proj, wg_pad, bg, d_o, states, dqkv_in, reverse, name):
    rows_total = proj.shape[0]
    rows = min(GLA_BLOCK, rows_total)
    n_blocks = rows_total // rows
    n_chunks = rows // GLA_CHUNK
    specs, tix = _gla_specs(rows, n_blocks, not reverse)
    d = 1 if reverse else 0
    specs[4] = pl.BlockSpec((1, LANES, GLA_DK), lambda h, s: (d, 0, h))
    specs[5] = pl.BlockSpec((1, 1, GLA_DK), lambda h, s: (d, 0, h))
    add = dqkv_in is not None

    def body(*refs):
        q_ref, k_ref, v_ref, lr_ref, wg_ref, bg_ref, do_ref, st_ref = refs[:8]
        refs = refs[8:]
        if add:
            aq_ref, ak_ref, av_ref = refs[:3]
            refs = refs[3:]
        dq_ref, dk_ref, dv_ref, dz_ref, dstate, g_scr, dec_scr, dsn_scr = refs
        z, q_in, k_in, k_st, dec, (f_q, f_k, f_s) = _gla_prepare(q_ref, k_ref, lr_ref, wg_ref, bg_ref, reverse,
                                                                 n_chunks)
        mask = _gla_mask(reverse)
        vb = v_ref[...].reshape(n_chunks, GLA_CHUNK, GLA_DV).astype(BF16)
        dob = do_ref[...].reshape(n_chunks, GLA_CHUNK, GLA_DV).astype(BF16)
        qb, kb, ksb = q_in.astype(BF16), k_in.astype(BF16), k_st.astype(BF16)
        st = st_ref[0]
        stb = st.astype(BF16)
        pb = jnp.where(mask, _bdot(qb, kb, 2, 2), 0.0).astype(BF16)
        dpb = jnp.where(mask, _bdot(dob, vb, 2, 2), 0.0).astype(BF16)
        d_qin = _bdot(dpb, kb, 2, 1) + _bdot(dob, stb, 2, 1)
        d_kin = _bdot(dpb, qb, 1, 1)
        dv = _bdot(pb, dob, 1, 1)
        g_scr[...] = _bdot(dob, qb, 1, 1)
        dec_scr[...] = jnp.broadcast_to(dec, dec_scr.shape)

        @pl.when(pl.program_id(1) == 0)
        def _():
            dstate[...] = jnp.zeros_like(dstate)

        for c in range(n_chunks):
            cc = c if reverse else n_chunks - 1 - c
            dsn_scr[cc] = dstate[...]
            dstate[...] = dstate[...] * dec_scr[cc, 0:1] + g_scr[cc]
        dsn = dsn_scr[...]
        dsnb = dsn.astype(BF16)
        dv = dv + _bdot(ksb, dsnb, 2, 2)
        d_kst = _bdot(vb, dsnb, 2, 1)
        d_dec = jnp.sum(dsn * st, axis=1, keepdims=True)
        ks_term = d_kst * k_st
        d_btot = d_dec * dec + jnp.sum(ks_term, axis=1, keepdims=True)
        d_b = d_qin * q_in - d_kin * k_in - ks_term
        pos = lax.broadcasted_iota(jnp.int32, d_b.shape, 1)
        edge = 0 if reverse else GLA_CHUNK - 1
        d_b = d_b + jnp.where(pos == edge, d_btot, 0.0)
        dg = _chunk_cumsum(d_b.reshape(rows, GLA_DK), not reverse)
        dz_ref[...] = dg * (1.0 / GLA_NORMALIZER) * _sigmoid(-z)
        dq = (d_qin * f_q).reshape(rows, GLA_DK)
        dk = (d_kin * f_k + d_kst * f_s).reshape(rows, GLA_DK)
        dv = dv.reshape(rows, GLA_DV)
        if add:
            dq_ref[...] = (dq + aq_ref[...]).astype(BF16)
            dk_ref[...] = (dk + ak_ref[...]).astype(BF16)
            dv_ref[...] = (dv + av_ref[...]).astype(BF16)
        else:
            dq_ref[...] = dq
            dk_ref[...] = dk
            dv_ref[...] = dv

    qkv_specs = [pl.BlockSpec((rows, GLA_DK), lambda h, s: (tix(s), h)),
                 pl.BlockSpec((rows, GLA_DK), lambda h, s: (tix(s), h)),
                 pl.BlockSpec((rows, GLA_DV), lambda h, s: (tix(s), h))]
    in_specs = specs + [pl.BlockSpec((rows, GLA_DV), lambda h, s: (tix(s), h)),
                        pl.BlockSpec((1, n_chunks, GLA_DV, GLA_DK), lambda h, s: (h, tix(s), 0, 0))]
    args = [proj, proj, proj, proj, wg_pad, bg, d_o, states]
    out_dtype = F32
    if add:
        in_specs += qkv_specs
        args += list(dqkv_in)
        out_dtype = BF16
    return pl.pallas_call(
        body, name=name,
        out_shape=(jax.ShapeDtypeStruct((rows_total, GLA_HEADS * GLA_DK), out_dtype),
                   jax.ShapeDtypeStruct((rows_total, GLA_HEADS * GLA_DK), out_dtype),
                   jax.ShapeDtypeStruct((rows_total, GLA_HEADS * GLA_DV), out_dtype),
                   jax.ShapeDtypeStruct((rows_total, GLA_HEADS * GLA_DK), F32)),
        grid=(GLA_HEADS, n_blocks),
        in_specs=in_specs,
        out_specs=(pl.BlockSpec((rows, GLA_DK), lambda h, s: (tix(s), h)),
                   pl.BlockSpec((rows, GLA_DK), lambda h, s: (tix(s), h)),
                   pl.BlockSpec((rows, GLA_DV), lambda h, s: (tix(s), h)),
                   pl.BlockSpec((rows, GLA_DK), lambda h, s: (tix(s), h))),
        scratch_shapes=[pltpu.VMEM((GLA_DV, GLA_DK), F32), pltpu.VMEM((n_chunks, GLA_DV, GLA_DK), F32),
                        pltpu.VMEM((n_chunks, SUBLANES, GLA_DK), F32),
                        pltpu.VMEM((n_chunks, GLA_DV, GLA_DK), F32)],
        compiler_params=_params("parallel", "arbitrary"),
    )(*args)


def gla_gate_bwd(proj, dz_f, dz_b, wg_pad, name):
    rows_total = proj.shape[0]
    tm = min(ROW_TILE, rows_total)
    n_key = GLA_HEADS * GLA_DK

    def body(lr_ref, dzf_ref, dzb_ref, wg_ref, dlr_ref, dwg_ref, dbg_ref):
        step = pl.program_id(0)
        lr_t = jnp.transpose(lr_ref[...])
        dzf, dzb = dzf_ref[...], dzb_ref[...]
        dzf16, dzb16 = dzf.astype(BF16), dzb.astype(BF16)
        dlr_ref[...] = (_dot_nt(dzf16, wg_ref[0]) + _dot_nt(dzb16, wg_ref[1])).astype(BF16)
        dwf = _dot(lr_t[0:GLA_RANK].astype(BF16), dzf16)
        dwb = _dot(lr_t[GLA_RANK:2 * GLA_RANK].astype(BF16), dzb16)
        dbg = jnp.concatenate([_colsum(dzf), _colsum(dzb)], axis=0)

        @pl.when(step == 0)
        def _():
            dwg_ref[0] = dwf
            dwg_ref[1] = dwb
            dbg_ref[...] = dbg

        @pl.when(step > 0)
        def _():
            dwg_ref[0] += dwf
            dwg_ref[1] += dwb
            dbg_ref[...] += dbg

    return pl.pallas_call(
        body, name=name,
        out_shape=(jax.ShapeDtypeStruct((rows_total, LANES), BF16), jax.ShapeDtypeStruct((2, GLA_RANK, n_key), F32),
                   jax.ShapeDtypeStruct((2, n_key), F32)),
        grid=(rows_total // tm,),
        in_specs=[pl.BlockSpec((tm, LANES), lambda i: (i, LR_COL // LANES)),
                  pl.BlockSpec((tm, n_key), lambda i: (i, 0)), pl.BlockSpec((tm, n_key), lambda i: (i, 0)),
                  _full((2, LANES, n_key))],
        out_specs=(pl.BlockSpec((tm, LANES), lambda i: (i, 0)), _full((2, GLA_RANK, n_key)), _full((2, n_key))),
        compiler_params=_params("arbitrary"),
    )(proj, dz_f, dz_b, wg_pad)


def _head_norm(o, gain):
    outs, hats, rstds = [], [], []
    for h in range(GLA_HEADS):
        oh = o[:, h * GLA_DV:(h + 1) * GLA_DV]
        rstd = lax.rsqrt(jnp.mean(oh * oh, axis=-1, keepdims=True) + NORM_EPS)
        hat = oh * rstd
        outs.append(hat * gain)
        hats.append(hat)
        rstds.append(rstd)
    return outs, hats, rstds


def odd_out_fwd(o, proj, head_gain, w_out, gain, x1, target, name):
    rows, d = x1.shape
    tm = min(ROW_TILE, rows)
    r_block = (2 * GLA_HEADS * GLA_DK + GLA_HEADS * GLA_DV) // d

    def body(o_ref, r_ref, hg_ref, w_ref, g_ref, x1_ref, tgt_ref, y2_ref, dy_ref, dx2_ref, loss_ref, dg_ref):
        step = pl.program_id(0)
        on, _, _ = _head_norm(o_ref[...], hg_ref[...])
        r = r_ref[...]
        y2 = (jnp.concatenate(on, axis=1) * (r * _sigmoid(r))).astype(BF16)
        y2_ref[...] = y2
        y = _dot(y2, w_ref[...])
        gain_v = g_ref[...]
        rstd = lax.rsqrt(jnp.mean(y * y, axis=-1, keepdims=True) + NORM_EPS)
        x2 = x1_ref[...] + y * rstd * gain_v
        diff = x2 - tgt_ref[...]
        loss = 0.5 * jnp.sum(jnp.mean(diff * diff, axis=-1, keepdims=True), axis=0, keepdims=True)
        dx2 = diff * (1.0 / d)
        dx2_ref[...] = dx2
        dy, dg_rows = _rmsnorm_bwd(dx2, y, gain_v)
        dy_ref[...] = dy.astype(BF16)
        _accumulate(loss_ref, jnp.broadcast_to(loss, loss_ref.shape), step)
        _accumulate(dg_ref, _colsum(dg_rows), step)

    row = lambda n, col=0: pl.BlockSpec((tm, n), lambda i: (i, col))
    return pl.pallas_call(
        body, name=name,
        out_shape=(jax.ShapeDtypeStruct((rows, d), BF16), jax.ShapeDtypeStruct((rows, d), BF16),
                   jax.ShapeDtypeStruct((rows, d), F32), jax.ShapeDtypeStruct((SUBLANES, LANES), F32),
                   jax.ShapeDtypeStruct((1, d), F32)),
        grid=(rows // tm,),
        in_specs=[row(d), row(d, r_block), _full((1, GLA_DV)), _full((d, d)), _full((1, d)), row(d), row(d)],
        out_specs=(row(d), row(d), row(d), _full((SUBLANES, LANES)), _full((1, d))),
        compiler_params=_params("arbitrary"),
    )(o, proj, head_gain, w_out, gain, x1, target)


def odd_out_bwd(dy, w_out, o, proj, head_gain, name):
    rows, d = dy.shape
    tm = min(ROW_TILE, rows)
    r_block = (2 * GLA_HEADS * GLA_DK + GLA_HEADS * GLA_DV) // d

    def body(dy_ref, w_ref, o_ref, r_ref, hg_ref, dr_ref, do_ref, dhg_ref):
        dy2 = _dot_nt(dy_ref[...], w_ref[...])
        hg = hg_ref[...]
        on, hats, rstds = _head_norm(o_ref[...], hg)
        r = r_ref[...]
        sr = _sigmoid(r)
        dr_ref[...] = (dy2 * jnp.concatenate(on, axis=1) * (sr * (1.0 + r * (1.0 - sr)))).astype(BF16)
        d_on = dy2 * (r * sr)
        d_os, dhg = [], None
        for h in range(GLA_HEADS):
            dn = d_on[:, h * GLA_DV:(h + 1) * GLA_DV]
            part = _colsum(dn * hats[h])
            dhg = part if dhg is None else dhg + part
            dng = dn * hg
            d_os.append(rstds[h] * (dng - hats[h] * jnp.mean(dng * hats[h], axis=-1, keepdims=True)))
        do_ref[...] = jnp.concatenate(d_os, axis=1)
        _accumulate(dhg_ref, dhg, pl.program_id(0))

    row = lambda n, col=0: pl.BlockSpec((tm, n), lambda i: (i, col))
    return pl.pallas_call(
        body, name=name,
        out_shape=(jax.ShapeDtypeStruct((rows, d), BF16), jax.ShapeDtypeStruct((rows, d), F32),
                   jax.ShapeDtypeStruct((1, GLA_DV), F32)),
        grid=(rows // tm,),
        in_specs=[row(d), _full((d, d)), row(d), row(d, r_block), _full((1, GLA_DV))],
        out_specs=(row(d), row(d), _full((1, GLA_DV))),
        compiler_params=_params("arbitrary"),
    )(dy, w_out, o, proj, head_gain)


def local_step(x, target, w, reduce_first=None, reduce_second=None, late_weights=None):
    g, g16 = {}, {}
    proj_e, h0 = norm_matmul(x, w["even_norm_pre"], w["even_w_in"], BF16, "even_in_proj")
    h_dir, acts = zip(*[rglru_fwd(proj_e, w["rg_conv_w"], w["rg_conv_b"], w["rg_gate_w"][d], w["rg_gate_b"][d],
                                  w["rg_lambda"][d], d == 1, "rglru_fwd_%d" % d) for d in range(2)])
    ycat = even_mix_fwd(proj_e, h_dir[0], h_dir[1], w["sc_conv_w"], "even_mix_fwd")
    if late_weights is not None:
        w = dict(w, **late_weights(ycat))
    x1, y_e = even_out_fwd(ycat, w["even_w_out"], w["even_norm_post"], x, "even_out_fwd")
    proj_o, h1 = norm_matmul(x1, w["odd_norm_pre"], w["odd_w_in"], F32, "odd_in_proj")
    o, st_dir = None, []
    for d in range(2):
        o, st = gla_fwd(proj_o, w["gla_wg_pad"], w["gla_b_gate"], o, d == 1, "gla_fwd_%d" % d)
        st_dir.append(st)
    y2, dy_o, dx2, loss, g["odd_norm_post"] = odd_out_fwd(
        o, proj_o, w["gla_norm_g"], w["odd_w_out"], w["odd_norm_post"], x1, target, "odd_out_fwd")
    g["odd_w_out"], g16["odd_w_out"] = (a[0] for a in matmul_dw(y2, dy_o, D_MODEL, "odd_w_out_grad"))
    dr, d_o, g["gla_norm_g"] = odd_out_bwd(dy_o, w["odd_w_out"], o, proj_o, w["gla_norm_g"], "odd_out_bwd")
    dq, dk, dv, dz_f = gla_bwd(proj_o, w["gla_wg_pad"], w["gla_b_gate"], d_o, st_dir[0], None, False, "gla_bwd_0")
    dq, dk, dv, dz_b = gla_bwd(proj_o, w["gla_wg_pad"], w["gla_b_gate"], d_o, st_dir[1], (dq, dk, dv), True,
                               "gla_bwd_1")
    dlr, g["gla_w_gate_lr"], g["gla_b_gate"] = gla_gate_bwd(proj_o, dz_f, dz_b, w["gla_wg_pad"], "gla_gate_bwd")
    dproj_o = [dq, dk, dv, dr, dlr]
    g["odd_w_in"] = jnp.concatenate(matmul_dw_pieces(h1, dproj_o, "odd_w_in_grad"), axis=1)[:, :ODD_IN]
    dx1, g["odd_norm_pre"] = inproj_bwd_pieces(dproj_o, w["odd_w_in"], x1, w["odd_norm_pre"], dx2, "odd_in_proj_bwd")
    dy_e, dycat, g["even_norm_post"] = even_out_bwd(dx1, y_e, w["even_norm_post"], w["even_w_out"], "even_out_bwd")
    g["even_w_out"], g16["even_w_out"] = (a[0] for a in matmul_dw(ycat, dy_e, D_MODEL, "even_w_out_grad"))
    lam = w["rg_lambda"] if reduce_first is None else w["rg_lambda"] + reduce_first(g, g16)
    dua, dgw, dgb, dlam = None, [], [], []
    for d in range(2):
        a, b, c, e = rglru_bwd(proj_e, dycat, h_dir[d], acts[d], w["rg_gate_w"][d], lam[d], dua, d == 1,
                               "rglru_bwd_%d" % d)
        dua = a
        dgw.append(b)
        dgb.append(c)
        dlam.append(e)
    dproj_e, g["rg_conv_w"], g["rg_conv_b"], g["sc_conv_w"] = even_mix_bwd(
        proj_e, dycat, h_dir[0], h_dir[1], dua, w["rg_conv_w"], w["sc_conv_w"], "even_mix_bwd")
    dgw = jnp.stack(dgw).reshape(2, RG_HEADS, RG_HEAD_DIM, 2, RG_HEAD_DIM)
    g["rg_gate_w"] = jnp.transpose(dgw, (0, 3, 1, 2, 4))
    g["rg_gate_b"] = jnp.stack(dgb).reshape(2, 2, RG_HEADS, RG_HEAD_DIM)
    g["rg_lambda"] = jnp.concatenate(dlam, axis=0)
    g["even_w_in"], g16["even_w_in"] = matmul_dw(h0, dproj_e, EVEN_IN // 4, "even_w_in_grad")
    gain = w["even_norm_pre"] if reduce_second is None else w["even_norm_pre"] + reduce_second(g, g16)
    grad_x, g["even_norm_pre"] = inproj_bwd(dproj_e, w["even_w_in"], x, gain, dx1, "even_in_proj_bwd")
    return loss, grad_x, g


def _prepare_weights(full):
    w = {}
    for name in ("even_norm_pre", "even_norm_post", "rg_conv_b", "odd_norm_pre", "odd_norm_post", "gla_norm_g"):
        if name in full:
            w[name] = full[name].reshape(1, -1)
    for name in ("rg_conv_w", "sc_conv_w"):
        if name in full:
            w[name] = full[name]
    for name in ("even_w_out", "odd_w_out"):
        if name in full:
            w[name] = full[name].astype(BF16)
    if "even_w_in" in full:
        w["even_w_in"] = full["even_w_in"].astype(BF16)
        if w["even_w_in"].ndim == 2:
            w["even_w_in"] = jnp.transpose(w["even_w_in"].reshape(D_MODEL, 4, EVEN_IN // 4), (1, 0, 2))
    if "rg_gate_w" in full:
        gw = jnp.transpose(full["rg_gate_w"].astype(BF16), (0, 2, 3, 1, 4))
        w["rg_gate_w"] = gw.reshape(2, RG_HEADS, RG_HEAD_DIM, 2 * RG_HEAD_DIM)
        w["rg_gate_b"] = full["rg_gate_b"].reshape(2, 2, D_MODEL)
        w["rg_lambda"] = full["rg_lambda"].reshape(2, 1, D_MODEL)
    if "odd_w_in" in full:
        w_in = jnp.pad(full["odd_w_in"].astype(BF16), ((0, 0), (0, ODD_IN_PAD - ODD_IN)))
        w["odd_w_in"] = w_in.reshape(1, D_MODEL, ODD_IN_PAD)
    if "gla_w_gate_lr" in full:
        wg = full["gla_w_gate_lr"].astype(BF16)
        w["gla_wg_pad"] = jnp.stack([jnp.pad(wg[d], ((d * GLA_RANK, LANES - (d + 1) * GLA_RANK), (0, 0)))
                                     for d in range(2)])
        w["gla_b_gate"] = full["gla_b_gate"].reshape(2, 1, GLA_HEADS * GLA_DK)
    return w


SHARDED_SMALL = (("rg_conv_w", (4, 256)), ("rg_lambda", (2, 256)), ("sc_conv_w", (3, 256)),
                 ("odd_norm_pre", (256,)), ("odd_norm_post", (256,)), ("gla_w_gate_lr", (2, 16, 128)),
                 ("gla_b_gate", (2, 128)), ("gla_norm_g", (64,)))
SHARDED_ROWS = 96
REPLICATED = (("rg_gate_w", (2, 2, 8, 128, 128)), ("even_norm_post", (1024,)), ("rg_conv_b", (1024,)),
              ("rg_gate_b", (2, 2, 8, 128)))
GATE_ROWS = 4096
LAST_REPLICATED = (("even_norm_pre", (1024,)),)
LAST_ROWS = 8
REPLICATED_ROWS = 4160
REP_PART = REPLICATED_ROWS // 8
HALF_SHARDED = SHARDED_ROWS // 2
PACK_HALF = HALF_SHARDED + REP_PART


def _seg_rows(shape):
    n = 1
    for s in shape:
        n *= s
    return -(-n // (SUBLANES * LANES)) * SUBLANES


def _pack(arrays, spec, total_rows, lead=()):
    parts = []
    for name, shape in spec:
        flat = arrays[name].reshape(lead + (-1,))
        pad = _seg_rows(shape) * LANES - flat.shape[-1]
        if pad:
            flat = jnp.pad(flat, [(0, 0)] * len(lead) + [(0, pad)])
        parts.append(flat.reshape(lead + (-1, LANES)))
    rows = jnp.concatenate(parts, axis=len(lead))
    pad = total_rows - rows.shape[len(lead)]
    return jnp.pad(rows, [(0, 0)] * len(lead) + [(0, pad), (0, 0)])


def _unpack(rows, spec, lead=()):
    out, at = {}, 0
    for name, shape in spec:
        n = 1
        for s in shape:
            n *= s
        k = _seg_rows(shape)
        seg = lax.slice_in_dim(rows, at, at + k, axis=len(lead)).reshape(lead + (-1,))
        out[name] = lax.slice_in_dim(seg, 0, n, axis=len(lead)).reshape(lead + shape)
        at += k
    return out


def _split_owners(arr):
    a = arr.reshape(arr.shape[:-1] + (4, arr.shape[-1] // 4))
    return jnp.moveaxis(a, -2, 0)


def _merge_owners(arr):
    a = jnp.moveaxis(arr, 0, -2)
    return a.reshape(a.shape[:-2] + (-1,))


HBM_SPEC = pl.BlockSpec(memory_space=pltpu.HBM)


def _position():
    x, y, c = lax.axis_index("x"), lax.axis_index("y"), lax.axis_index("c")
    chips = [(1 - x, y), (x, 1 - y), (1 - x, 1 - y)]
    return x, y, c, chips


def _remote(src, dst, send_sem, recv_sem, device):
    return pltpu.make_async_remote_copy(src_ref=src, dst_ref=dst, send_sem=send_sem, recv_sem=recv_sem,
                                        device_id=device, device_id_type=MESH)


SEM_SPEC = pl.BlockSpec(memory_space=pltpu.SEMAPHORE)
SIDE_EFFECT = pltpu.SideEffectType.DATAFLOW_SIDE_EFFECTING


def _gather_copies(ins, lands, n_h, send_sems, recv_sems):
    x, y, c, chips = _position()
    me = 2 * x + y
    copies = []
    for a in range(len(ins)):
        for k, chip in enumerate(chips):
            src = ins[a].at[c] if a < n_h else ins[a]
            dst = lands[a].at[me, c] if a < n_h else lands[a].at[me]
            copies.append(_remote(src, dst, send_sems.at[3 * a + k], recv_sems.at[3 * a + k], (chip[0], chip[1], c)))
    return copies


def gather_start(halved, whole, name):
    arrays = list(halved) + list(whole)
    n, n_h = len(arrays), len(halved)
    lands = [lax.empty((4,) + a.shape, a.dtype) for a in arrays]

    def body(*refs):
        ins, lz, send_sems, recv_sems, token = refs[:n], refs[n:2 * n], refs[2 * n], refs[2 * n + 1], refs[-1]
        for cp in _gather_copies(ins, lz, n_h, send_sems, recv_sems):
            cp.start()
        token[...] = jnp.zeros_like(token)

    operands = [pltpu.with_memory_space_constraint(a, pltpu.HBM) for a in arrays + lands]
    return pl.pallas_call(
        body, name=name,
        out_shape=(pltpu.SemaphoreType.DMA((3 * n,)), pltpu.SemaphoreType.DMA((3 * n,)))
        + tuple(pltpu.HBM(a.shape, a.dtype) for a in operands) + (jax.ShapeDtypeStruct((SUBLANES, LANES), F32),),
        in_specs=[HBM_SPEC] * (2 * n),
        out_specs=(SEM_SPEC, SEM_SPEC) + (HBM_SPEC,) * (2 * n) + (pl.BlockSpec(memory_space=pltpu.VMEM),),
        input_output_aliases={i: 2 + i for i in range(2 * n)},
        compiler_params=pltpu.CompilerParams(has_side_effects=SIDE_EFFECT),
    )(*operands)


def gather_wait(started, n_h, after, name):
    send_sems, recv_sems = started[0], started[1]
    operands = list(started[2:-1])
    n = len(operands) // 2

    def body(*refs):
        ins, lz, send_ref, recv_ref = refs[:n], refs[n:2 * n], refs[2 * n], refs[2 * n + 1]
        for cp in _gather_copies(ins, lz, n_h, send_ref, recv_ref):
            cp.wait_send()
            cp.wait_recv()

    outs = pl.pallas_call(
        body, name=name,
        out_shape=tuple(pltpu.HBM(a.shape, a.dtype) for a in operands),
        in_specs=[HBM_SPEC] * (2 * n) + [SEM_SPEC, SEM_SPEC, pl.BlockSpec(memory_space=pl.ANY)],
        out_specs=(HBM_SPEC,) * (2 * n),
        input_output_aliases={i: i for i in range(2 * n)},
        compiler_params=pltpu.CompilerParams(has_side_effects=SIDE_EFFECT),
    )(*operands, send_sems, recv_sems, after)
    return outs[n:]


def pass_to_sibling(fulls, name):
    n = len(fulls)

    def body(*refs):
        bufs = refs[n:2 * n]
        send_sems, recv_sems = refs[2 * n:]
        x, y, c, chips = _position()
        sibling = (x, y, 1 - c)
        copies = []
        for a in range(n):
            for k, chip in enumerate(chips):
                q = 2 * chip[0] + chip[1]
                cp = _remote(bufs[a].at[q, c], bufs[a].at[q, c], send_sems.at[3 * a + k], recv_sems.at[3 * a + k],
                             sibling)
                cp.start()
                copies.append(cp)
        for a in range(n):
            for k, chip in enumerate(chips):
                q = 2 * chip[0] + chip[1]
                passed = bufs[a].at[q, 1 - c]
                _remote(passed, passed, send_sems.at[3 * a + k], recv_sems.at[3 * a + k], sibling).wait_recv()
        for cp in copies:
            cp.wait_send()

    return pl.pallas_call(
        body, name=name,
        out_shape=[jax.ShapeDtypeStruct(a.shape, a.dtype) for a in fulls],
        in_specs=[HBM_SPEC] * n, out_specs=[HBM_SPEC] * n,
        input_output_aliases={i: i for i in range(n)},
        scratch_shapes=[pltpu.SemaphoreType.DMA((3 * n,)), pltpu.SemaphoreType.DMA((3 * n,))],
    )(*fulls)


def place_own(full, own, chip, name):
    _, _, r, cols = full.shape
    tr = _row_tile(r, cols)

    def body(p_ref, own_ref, full_ref, o_ref):
        o_ref[0] = own_ref[...]

    return pl.pallas_call(
        body, name=name,
        out_shape=jax.ShapeDtypeStruct(full.shape, full.dtype),
        grid_spec=pltpu.PrefetchScalarGridSpec(
            num_scalar_prefetch=1, grid=(2, r // tr),
            in_specs=[pl.BlockSpec((1, tr, cols), lambda h, i, p_ref: (h, i, 0)), pl.BlockSpec(memory_space=pl.ANY)],
            out_specs=pl.BlockSpec((1, 1, tr, cols), lambda h, i, p_ref: (p_ref[0], h, i, 0))),
        input_output_aliases={2: 0},
        compiler_params=_params("parallel", "parallel"),
    )(chip, own, full)


def exchange_with_sibling(arrays, name):
    n = len(arrays)

    def body(*refs):
        ins, outs = refs[:n], refs[n:2 * n]
        send_sems, recv_sems = refs[2 * n:]
        x, y, c, _ = _position()
        copies = []
        for a in range(n):
            cp = _remote(ins[a].at[:, 1 - c], outs[a], send_sems.at[a], recv_sems.at[a], (x, y, 1 - c))
            cp.start()
            copies.append(cp)
        for cp in copies:
            cp.wait()

    return pl.pallas_call(
        body, name=name,
        out_shape=[jax.ShapeDtypeStruct((a.shape[0],) + a.shape[2:], a.dtype) for a in arrays],
        in_specs=[HBM_SPEC] * n, out_specs=[HBM_SPEC] * n,
        scratch_shapes=[pltpu.SemaphoreType.DMA((n,)), pltpu.SemaphoreType.DMA((n,))],
    )(*arrays)


def _chip_copies(ins, lands, send_sems, recv_sems):
    x, y, c, chips = _position()
    copies = []
    for a in range(len(ins)):
        for k, chip in enumerate(chips):
            q = 2 * chip[0] + chip[1]
            copies.append(_remote(ins[a].at[q], lands[a].at[k], send_sems.at[3 * a + k], recv_sems.at[3 * a + k],
                                  (chip[0], chip[1], c)))
    return copies


def exchange_with_chips_start(arrays, name):
    n = len(arrays)
    lands = [lax.empty((3,) + a.shape[1:], a.dtype) for a in arrays]

    def body(*refs):
        ins, lz, send_sems, recv_sems, token = refs[:n], refs[n:2 * n], refs[2 * n], refs[2 * n + 1], refs[-1]
        for cp in _chip_copies(ins, lz, send_sems, recv_sems):
            cp.start()
        token[...] = jnp.zeros_like(token)

    operands = [pltpu.with_memory_space_constraint(a, pltpu.HBM) for a in list(arrays) + lands]
    return pl.pallas_call(
        body, name=name,
        out_shape=(pltpu.SemaphoreType.DMA((3 * n,)), pltpu.SemaphoreType.DMA((3 * n,)))
        + tuple(pltpu.HBM(a.shape, a.dtype) for a in operands) + (jax.ShapeDtypeStruct((SUBLANES, LANES), F32),),
        in_specs=[HBM_SPEC] * (2 * n),
        out_specs=(SEM_SPEC, SEM_SPEC) + (HBM_SPEC,) * (2 * n) + (pl.BlockSpec(memory_space=pltpu.VMEM),),
        input_output_aliases={i: 2 + i for i in range(2 * n)},
        compiler_params=pltpu.CompilerParams(has_side_effects=SIDE_EFFECT),
    )(*operands)


def exchange_with_chips_wait(started, after, name):
    send_sems, recv_sems = started[0], started[1]
    operands = list(started[2:-1])
    n = len(operands) // 2

    def body(*refs):
        ins, lz, send_ref, recv_ref = refs[:n], refs[n:2 * n], refs[2 * n], refs[2 * n + 1]
        for cp in _chip_copies(ins, lz, send_ref, recv_ref):
            cp.wait_send()
            cp.wait_recv()

    outs = pl.pallas_call(
        body, name=name,
        out_shape=tuple(pltpu.HBM(a.shape, a.dtype) for a in operands),
        in_specs=[HBM_SPEC] * (2 * n) + [SEM_SPEC, SEM_SPEC, pl.BlockSpec(memory_space=pl.ANY)],
        out_specs=(HBM_SPEC,) * (2 * n),
        input_output_aliases={i: i for i in range(2 * n)},
        compiler_params=pltpu.CompilerParams(has_side_effects=SIDE_EFFECT),
    )(*operands, send_sems, recv_sems, after)
    return outs[:n], outs[n:]


def share_totals(totals, pack_total, last_part):
    arrays = list(totals) + [pack_total]
    n = len(arrays)

    def body(*refs):
        ins, last, outs, rep, last_all = refs[:n], refs[n], refs[n + 1:2 * n + 1], refs[2 * n + 1], refs[2 * n + 2]
        send_sems, recv_sems, rep_send, rep_recv, last_send, last_recv = refs[2 * n + 3:]
        x, y, c, chips = _position()
        sibling = (x, y, 1 - c)
        me = 4 * x + 2 * y + c
        sends = []
        for a in range(n):
            cp = _remote(ins[a], outs[a], send_sems.at[a], recv_sems.at[a], sibling)
            cp.start()
            sends.append(cp)
        mine = ins[n - 1].at[pl.ds(HALF_SHARDED, REP_PART)]
        peers = [sibling]
        for chip in chips:
            peers += [(chip[0], chip[1], c), (chip[0], chip[1], 1 - c)]
        for j, peer in enumerate(peers):
            for src, dst, s_sem, r_sem in ((mine, rep, rep_send, rep_recv), (last, last_all, last_send, last_recv)):
                cp = _remote(src, dst.at[me], s_sem.at[j], r_sem.at[j], peer)
                cp.start()
                sends.append(cp)
        for a in range(n):
            _remote(outs[a], outs[a], send_sems.at[a], recv_sems.at[a], sibling).wait_recv()
        for j, peer in enumerate(peers):
            it = 4 * peer[0] + 2 * peer[1] + peer[2]
            _remote(rep.at[it], rep.at[it], rep_send.at[j], rep_recv.at[j], peer).wait_recv()
            _remote(last_all.at[it], last_all.at[it], last_send.at[j], last_recv.at[j], peer).wait_recv()
        for cp in sends:
            cp.wait_send()

    outs = pl.pallas_call(
        body, name="grad_share_totals",
        out_shape=[jax.ShapeDtypeStruct(a.shape, a.dtype) for a in arrays]
        + [jax.ShapeDtypeStruct((8, REP_PART, LANES), F32), jax.ShapeDtypeStruct((8,) + last_part.shape, F32)],
        in_specs=[HBM_SPEC] * (n + 1), out_specs=[HBM_SPEC] * (n + 2),
        scratch_shapes=[pltpu.SemaphoreType.DMA((n,)), pltpu.SemaphoreType.DMA((n,))]
        + [pltpu.SemaphoreType.DMA((7,))] * 4,
    )(*arrays, last_part)
    return outs[:n], outs[n], outs[n + 1]


def sum_parts(parts, name):
    def body(p_ref, o_ref):
        total = p_ref[0]
        for k in range(1, parts.shape[0]):
            total = total + p_ref[k]
        o_ref[...] = total

    return pl.pallas_call(body, name=name, out_shape=jax.ShapeDtypeStruct(parts.shape[1:], parts.dtype))(parts)


TILE_BYTES = 2 << 20


def _row_tile(rows, cols):
    best = None
    for t in range(SUBLANES, rows + 1, SUBLANES):
        if rows % t == 0 and t * cols * 4 <= TILE_BYTES:
            best = t
    return best if best is not None else rows


def add_sibling(mine, received, core, out_dtype, name):
    _, _, r, cols = mine.shape
    tr = _row_tile(r, cols)

    def body(c_ref, a_ref, b_ref, o_ref):
        o_ref[...] = (a_ref[0] + b_ref[...].astype(F32)).astype(out_dtype)

    return pl.pallas_call(
        body, name=name,
        out_shape=jax.ShapeDtypeStruct((4, r, cols), out_dtype),
        grid_spec=pltpu.PrefetchScalarGridSpec(
            num_scalar_prefetch=1, grid=(4, r // tr),
            in_specs=[pl.BlockSpec((1, 1, tr, cols), lambda o, i, c_ref: (o, c_ref[0], i, 0)),
                      pl.BlockSpec((1, tr, cols), lambda o, i, c_ref: (o, i, 0))],
            out_specs=pl.BlockSpec((1, tr, cols), lambda o, i, c_ref: (o, i, 0))),
        compiler_params=_params("parallel", "parallel"),
    )(core, mine, received)


def add_chips(own, received, chip, name):
    _, r, cols = own.shape
    tr = _row_tile(r, cols)

    def body(p_ref, a_ref, b0, b1, b2, o_ref):
        o_ref[...] = ((a_ref[0].astype(F32) + b0[0].astype(F32)) + b1[0].astype(F32)) + b2[0].astype(F32)

    rb = lambda k: pl.BlockSpec((1, tr, cols), lambda i, p_ref: (k, i, 0))
    return pl.pallas_call(
        body, name=name,
        out_shape=jax.ShapeDtypeStruct((r, cols), F32),
        grid_spec=pltpu.PrefetchScalarGridSpec(
            num_scalar_prefetch=1, grid=(r // tr,),
            in_specs=[pl.BlockSpec((1, tr, cols), lambda i, p_ref: (p_ref[0], i, 0)), rb(0), rb(1), rb(2)],
            out_specs=pl.BlockSpec((tr, cols), lambda i, p_ref: (i, 0))),
        compiler_params=_params("parallel"),
    )(chip, own, received, received, received)


def _adamw_update(gv, w_ref, m_ref, v_ref, d_ref, nm_ref, nv_ref):
    nm = ADAM_B1 * m_ref[...] + (1.0 - ADAM_B1) * gv
    nv = ADAM_B2 * v_ref[...] + (1.0 - ADAM_B2) * (gv * gv)
    nm_ref[...] = nm
    nv_ref[...] = nv
    m_hat = nm / (1.0 - ADAM_B1 ** ADAM_STEP)
    v_hat = nv / (1.0 - ADAM_B2 ** ADAM_STEP)
    d_ref[...] = -ADAM_LR * (m_hat / (jnp.sqrt(v_hat) + ADAM_EPS) + ADAM_WD * w_ref[...])


def adamw_halves(w, own, received, m, v, core, name, by_columns=False):
    rows, cols = w.shape

    def body(c_ref, w_ref, own_ref, rec_ref, m_ref, v_ref, g_ref, d_ref, nm_ref, nv_ref):
        gv = jnp.where(pl.program_id(0) == c_ref[0], own_ref[...], rec_ref[...])
        g_ref[...] = gv
        _adamw_update(gv, w_ref, m_ref, v_ref, d_ref, nm_ref, nv_ref)

    if by_columns:
        nr = 1
        whole = pl.BlockSpec((rows, cols // 2), lambda h, i, c_ref: (0, h))
        half = pl.BlockSpec((rows, cols // 2), lambda h, i, c_ref: (0, 0))
    else:
        r = rows // 2
        tr = _row_tile(r, cols)
        nr = r // tr
        whole = pl.BlockSpec((tr, cols), lambda h, i, c_ref: (h * nr + i, 0))
        half = pl.BlockSpec((tr, cols), lambda h, i, c_ref: (i, 0))
    return pl.pallas_call(
        body, name=name,
        out_shape=(jax.ShapeDtypeStruct((rows, cols), F32),) * 4,
        grid_spec=pltpu.PrefetchScalarGridSpec(
            num_scalar_prefetch=1, grid=(2, nr),
            in_specs=[whole, half, half, whole, whole], out_specs=(whole,) * 4),
        compiler_params=_params("parallel", "parallel"),
    )(core, w, own, received, m, v)


def adamw_many(ws, gs, ms, vs, name):
    n = len(ws)

    def body(*refs):
        ins, outs = refs[:4 * n], refs[4 * n:]
        for k in range(n):
            w_ref, g_ref, m_ref, v_ref = (ins[j * n + k] for j in range(4))
            d_ref, nm_ref, nv_ref = outs[3 * k:3 * k + 3]
            _adamw_update(g_ref[...], w_ref, m_ref, v_ref, d_ref, nm_ref, nv_ref)

    flat = pl.pallas_call(
        body, name=name,
        out_shape=[jax.ShapeDtypeStruct(w.shape, F32) for w in ws for _ in range(3)],
    )(*ws, *gs, *ms, *vs)
    return [tuple(flat[3 * k:3 * k + 3]) for k in range(n)]


def adamw(w, g, m, v, name):
    r, cols = w.shape
    tr = _row_tile(r, cols)

    def body(w_ref, g_ref, m_ref, v_ref, g_out, d_ref, nm_ref, nv_ref):
        gv = g_ref[...]
        g_out[...] = gv
        _adamw_update(gv, w_ref, m_ref, v_ref, d_ref, nm_ref, nv_ref)

    blk = pl.BlockSpec((tr, cols), lambda i: (i, 0))
    return pl.pallas_call(
        body, name=name,
        out_shape=(jax.ShapeDtypeStruct((r, cols), F32),) * 4,
        grid=(r // tr,),
        in_specs=[blk] * 4, out_specs=(blk,) * 4,
        compiler_params=_params("parallel"),
    )(w, g, m, v)


WEIGHTS = ("even_norm_pre", "even_norm_post", "even_w_in", "rg_conv_w", "rg_conv_b", "rg_gate_w", "rg_gate_b",
           "rg_lambda", "sc_conv_w", "even_w_out", "odd_norm_pre", "odd_norm_post", "odd_w_in", "gla_w_gate_lr",
           "gla_b_gate", "gla_norm_g", "odd_w_out")
BIG = ("even_w_in", "even_w_out", "odd_w_in", "odd_w_out")


def _halves(a):
    return a.reshape((2, a.shape[0] // 2) + a.shape[1:])


def kernel(x, even_norm_pre, even_norm_post, even_w_in, rg_conv_w, rg_conv_b, rg_gate_w, rg_gate_b, rg_lambda, sc_conv_w, even_w_out, odd_norm_pre, odd_norm_post, odd_w_in, gla_w_gate_lr, gla_b_gate, gla_norm_g, odd_w_out, loss_target, m_even_norm_pre, m_even_norm_post, m_even_w_in, m_rg_conv_w, m_rg_conv_b, m_rg_gate_w, m_rg_gate_b, m_rg_lambda, m_sc_conv_w, m_even_w_out, m_odd_norm_pre, m_odd_norm_post, m_odd_w_in, m_gla_w_gate_lr, m_gla_b_gate, m_gla_norm_g, m_odd_w_out, v_even_norm_pre, v_even_norm_post, v_even_w_in, v_rg_conv_w, v_rg_conv_b, v_rg_gate_w, v_rg_gate_b, v_rg_lambda, v_sc_conv_w, v_even_w_out, v_odd_norm_pre, v_odd_norm_post, v_odd_w_in, v_gla_w_gate_lr, v_gla_b_gate, v_gla_norm_g, v_odd_w_out):
    given = dict(locals())
    shard = {n: given[n][0] for n in WEIGHTS}
    m_in = {n: given["m_" + n][0] for n in WEIGHTS}
    v_in = {n: given["v_" + n][0] for n in WEIGHTS}
    mx, my, mc = lax.axis_index("x"), lax.axis_index("y"), lax.axis_index("c")
    core = jnp.reshape(mc, (1,)).astype(jnp.int32)
    chip = jnp.reshape(2 * mx + my, (1,)).astype(jnp.int32)

    small_shard = _pack(shard, SHARDED_SMALL, SHARDED_ROWS)
    big_own = [_halves(shard[n].astype(BF16)) for n in BIG]
    started_a = gather_start(big_own[:1], [small_shard], "gather_start_a")
    started_b = gather_start(big_own[1:], [], "gather_start_b")
    even_w_in_full, small_full = gather_wait(started_a, 1, started_b[-1], "gather_wait_a")
    (even_w_in_full,) = pass_to_sibling([even_w_in_full], "gather_pass_a")
    even_w_in_full = place_own(even_w_in_full, big_own[0], chip, "place_even_w_in")
    small_full = lax.dynamic_update_slice(small_full, small_shard[None], (chip[0], 0, 0))
    full = {n: shard[n] for n, _ in REPLICATED + LAST_REPLICATED}
    full.update({n: _merge_owners(a) for n, a in _unpack(small_full, SHARDED_SMALL, lead=(4,)).items()})
    full["even_w_in"] = even_w_in_full.reshape(4, D_MODEL, EVEN_IN // 4)

    def late_weights(after):
        lands = pass_to_sibling(list(gather_wait(started_b, 3, after, "gather_wait_b")), "gather_pass_b")
        lands = [place_own(a, b, chip, "place_" + n) for a, b, n in zip(lands, big_own[1:], BIG[1:])]
        odd_w_in = jnp.transpose(lands[1].reshape(4, D_MODEL, ODD_IN // 4), (1, 0, 2)).reshape(D_MODEL, ODD_IN)
        return _prepare_weights({"even_w_out": lands[0].reshape(2 * D_MODEL, D_MODEL), "odd_w_in": odd_w_in,
                                 "odd_w_out": lands[2].reshape(D_MODEL, D_MODEL)})

    pending = {}

    def slab(a):
        return a.reshape((4, 2, a.shape[1] // 2) + a.shape[2:])

    def begin(tag, slabs, to_send, dtypes):
        got = exchange_with_sibling(to_send, "grad_sibling_" + tag)
        sums = [add_sibling(a, b, core, dt, "grad_add_sibling_%s%d" % (tag, i))
                for i, (a, b, dt) in enumerate(zip(slabs, got, dtypes))]
        pending[tag] = exchange_with_chips_start(sums, "grad_chips_start_" + tag)
        return pending[tag][-1][0, 0]

    def finish(tag, after):
        sums, got = exchange_with_chips_wait(pending[tag], after, "grad_chips_wait_" + tag)
        return [add_chips(a, b, chip, "grad_add_chips_%s%d" % (tag, i)) for i, (a, b) in enumerate(zip(sums, got))]

    def reduce_first(g, g16):
        odd_w_in = slab(jnp.transpose(g["odd_w_in"].reshape(D_MODEL, 4, ODD_IN // 4), (1, 0, 2)))
        slabs = [odd_w_in] + [slab(g[n].reshape(4, -1, D_MODEL)) for n in ("odd_w_out", "even_w_out")]
        to_send = [odd_w_in.astype(BF16)] + [slab(g16[n].reshape(4, -1, D_MODEL)) for n in ("odd_w_out", "even_w_out")]
        return begin("a", slabs, to_send, [BF16] * 3)

    def reduce_second(g, g16):
        pending["totals_a"] = finish("a", g["even_w_in"])
        rep_rows = _pack(g, REPLICATED, REPLICATED_ROWS).reshape(4, 2, REP_PART, LANES)
        sh_rows = _pack({n: _split_owners(g[n]) for n, _ in SHARDED_SMALL}, SHARDED_SMALL, SHARDED_ROWS, lead=(4,))
        pack = jnp.concatenate([sh_rows.reshape(4, 2, HALF_SHARDED, LANES), rep_rows], axis=2)
        return begin("b", [slab(g["even_w_in"]), pack], [slab(g16["even_w_in"]), pack], [BF16, F32])

    loss, grad_x, g = local_step(x[0], loss_target[0], _prepare_weights(full), reduce_first, reduce_second,
                                 late_weights)
    odd_w_in_t, odd_w_out_t, even_w_out_t = pending["totals_a"]
    even_w_in_t, pack_t = finish("b", grad_x)
    totals = [even_w_in_t, even_w_out_t, odd_w_in_t, odd_w_out_t]
    last_part = jnp.concatenate([_pack(g, LAST_REPLICATED, LAST_ROWS), loss])
    from_core, rep_all, last_all = share_totals(totals, pack_t, last_part)
    me = 2 * chip[0] + core[0]
    mine, theirs = pack_t[:HALF_SHARDED], from_core[4][:HALF_SHARDED]
    sh_total = jnp.where(mc == 0, jnp.concatenate([mine, theirs]), jnp.concatenate([theirs, mine]))
    rep_all = lax.dynamic_update_slice(rep_all, pack_t[None, HALF_SHARDED:], (me, 0, 0))
    rep_total = rep_all.reshape(REPLICATED_ROWS, LANES)
    last_total = sum_parts(lax.dynamic_update_slice(last_all, last_part[None], (me, 0, 0)), "grad_sum_last")
    last_total, loss = last_total[:LAST_ROWS], last_total[LAST_ROWS, 0]
    grads = {}

    delta, new_m, new_v = {}, {}, {}
    for i, n in enumerate(BIG):
        if shard[n].shape[1] % LANES:
            outs = adamw_halves(shard[n].T, totals[i].T, from_core[i].T, m_in[n].T, v_in[n].T, core, "adamw_" + n,
                                by_columns=True)
            grads[n], delta[n], new_m[n], new_v[n] = [o.T for o in outs]
        else:
            grads[n], delta[n], new_m[n], new_v[n] = adamw_halves(shard[n], totals[i], from_core[i], m_in[n],
                                                                  v_in[n], core, "adamw_" + n)
    gate = [src["rg_gate_w"].reshape(GATE_ROWS, LANES) for src in (shard, m_in, v_in)]
    grads["rg_gate_w"], delta["rg_gate_w"], new_m["rg_gate_w"], new_v["rg_gate_w"] = adamw(
        gate[0], rep_total, gate[1], gate[2], "adamw_rg_gate_w")
    rest = REPLICATED[1:]
    rest_rows = sum(_seg_rows(shape) for _, shape in rest)
    grads.update(_unpack(sh_total, SHARDED_SMALL))
    grads.update(_unpack(rep_total[GATE_ROWS:GATE_ROWS + rest_rows], rest))
    grads.update(_unpack(last_total, LAST_REPLICATED))
    names = [n for n, _ in SHARDED_SMALL + rest + LAST_REPLICATED]
    rows_of = lambda a, n: a.reshape(-1, given[n].shape[-1])
    outs = adamw_many([rows_of(given[n], n) for n in names], [rows_of(grads[n], n) for n in names],
                      [rows_of(given["m_" + n], n) for n in names], [rows_of(given["v_" + n], n) for n in names],
                      "adamw_small")
    for n, (d, nm, nv) in zip(names, outs):
        delta[n], new_m[n], new_v[n] = d, nm, nv
    result = [loss, grad_x[None]]
    for group in (grads, delta, new_m, new_v):
        result += [group[n].reshape(given[n].shape) for n in WEIGHTS]
    return tuple(result)
```

```python
import jax
import jax.numpy as jnp
from jax import lax
from jax.experimental import pallas as pl
from jax.experimental.pallas import tpu as pltpu

F32 = jnp.float32
BF16 = jnp.bfloat16
MESH = pl.DeviceIdType.MESH

D_MODEL = 1024
NORM_EPS = 1e-6
RG_HEADS = 8
RG_HEAD_DIM = 128
RG_C = 8.0
EVEN_IN = 6144
ODD_IN = 3104
ODD_IN_PAD = 3200
GLA_HEADS = 4
GLA_DK = 128
GLA_DV = 256
GLA_RANK = 16
GLA_NORMALIZER = 16.0
GLA_CHUNK = 128
LR_COL = 3072

ADAM_LR = 0.001
ADAM_B1 = 0.9
ADAM_B2 = 0.999
ADAM_EPS = 1e-08
ADAM_WD = 0.01
ADAM_STEP = 10

SUBLANES = 8
HALO = 16
LANES = 128
VMEM_LIMIT = 56 * 2 ** 20

ROW_TILE = 512
SCAN_TILE = 256
GLA_BLOCK = 1024
MIX_TILE = 128


def _params(*sem):
    return pltpu.CompilerParams(dimension_semantics=sem, vmem_limit_bytes=VMEM_LIMIT)


def _full(shape):
    n = len(shape)
    return pl.BlockSpec(shape, lambda *_: (0,) * n)


def _sigmoid(x):
    return 0.5 + 0.5 * jnp.tanh(0.5 * x)


def _softplus(x):
    return jnp.maximum(x, 0.0) + jnp.log(1.0 + jnp.exp(-jnp.abs(x)))


def _dot(a, b):
    return jnp.dot(a, b, preferred_element_type=F32)


def _dot_nt(a, b):
    return lax.dot_general(a, b, (((1,), (1,)), ((), ())), preferred_element_type=F32)


def _dot_tn(a, b):
    return lax.dot_general(a, b, (((0,), (0,)), ((), ())), preferred_element_type=F32)


def _bdot(a, b, ca, cb):
    return lax.dot_general(a, b, (((ca,), (cb,)), ((0,), (0,))), preferred_element_type=F32)


def _halo_specs(rows, cols, col_block, n_row_tiles, tix):
    per = rows // HALO
    last = n_row_tiles * per - 1

    def split(args):
        if len(args) == 2:
            return tix(args[1]), col_block + args[0]
        return tix(args[0]), col_block

    def prev(*args):
        t, c = split(args)
        return (jnp.maximum(t * per - 1, 0), c)

    def main(*args):
        return split(args)

    def nxt(*args):
        t, c = split(args)
        return (jnp.minimum((t + 1) * per, last), c)

    return [pl.BlockSpec((HALO, cols), prev), pl.BlockSpec((rows, cols), main),
            pl.BlockSpec((HALO, cols), nxt)]


def _extend(prev_ref, main_ref, next_ref, is_first, is_last):
    p = jnp.where(is_first, 0.0, prev_ref[...].astype(F32))
    n = jnp.where(is_last, 0.0, next_ref[...].astype(F32))
    return jnp.concatenate([p, main_ref[...].astype(F32), n], axis=0)


def _shifted(ext, offset, rows):
    if offset == 0:
        return ext[HALO:HALO + rows]
    n = ext.shape[0]
    return pltpu.roll(ext, (-offset) % n, 0)[HALO:HALO + rows]


def _conv(ext, w, left, rows):
    out = None
    for k in range(w.shape[0]):
        term = _shifted(ext, k - left, rows) * w[k:k + 1]
        out = term if out is None else out + term
    return out


def _conv_transpose(ext, w, left, rows):
    out = None
    for k in range(w.shape[0]):
        term = _shifted(ext, left - k, rows) * w[k:k + 1]
        out = term if out is None else out + term
    return out


def _colsum(x):
    return jnp.sum(x, axis=0, keepdims=True)


def _accumulate(ref, value, step):
    @pl.when(step == 0)
    def _():
        ref[...] = value

    @pl.when(step > 0)
    def _():
        ref[...] += value


PROJ_TILE_BYTES = 7 * 2 ** 20


def _proj_row_tile(rows, width, dtype):
    tm = min(ROW_TILE, rows)
    while tm * width * jnp.dtype(dtype).itemsize > PROJ_TILE_BYTES and tm % (2 * HALO) == 0:
        tm //= 2
    return tm


def norm_matmul(x, gain, w, out_dtype, name):
    rows, d = x.shape
    n_col_tiles, _, tn = w.shape
    tm = _proj_row_tile(rows, n_col_tiles * tn, out_dtype)

    def body(x_ref, g_ref, w_ref, proj_ref, h_ref):
        xv = x_ref[...]
        rstd = lax.rsqrt(jnp.mean(xv * xv, axis=-1, keepdims=True) + NORM_EPS)
        hv = (xv * rstd * g_ref[...]).astype(BF16)
        h_ref[...] = hv
        for j in range(n_col_tiles):
            proj_ref[:, j * tn:(j + 1) * tn] = _dot(hv, w_ref[j]).astype(out_dtype)

    row = lambda cols: pl.BlockSpec((tm, cols), lambda i: (i, 0))
    return pl.pallas_call(
        body, name=name,
        out_shape=(jax.ShapeDtypeStruct((rows, n_col_tiles * tn), out_dtype), jax.ShapeDtypeStruct((rows, d), BF16)),
        grid=(rows // tm,),
        in_specs=[row(d), _full((1, d)), _full(w.shape)],
        out_specs=(row(n_col_tiles * tn), row(d)),
        compiler_params=_params("parallel"),
    )(x, gain, w)


def inproj_bwd(dproj, w, x, gain, dres, name):
    rows, d = x.shape
    n_col_tiles, _, tn = w.shape
    tm = _proj_row_tile(rows, n_col_tiles * tn, dproj.dtype)

    def body(dp_ref, w_ref, x_ref, g_ref, dres_ref, dx_ref, dg_ref):
        dh = None
        for j in range(n_col_tiles):
            part = _dot_nt(dp_ref[:, j * tn:(j + 1) * tn], w_ref[j])
            dh = part if dh is None else dh + part
        _inproj_finish(dh, x_ref, g_ref, dres_ref, dx_ref, dg_ref, pl.program_id(0))

    row = lambda cols: pl.BlockSpec((tm, cols), lambda i: (i, 0))
    return pl.pallas_call(
        body, name=name,
        out_shape=(jax.ShapeDtypeStruct((rows, d), F32), jax.ShapeDtypeStruct((1, d), F32)),
        grid=(rows // tm,),
        in_specs=[row(n_col_tiles * tn), _full(w.shape), row(d), _full((1, d)), row(d)],
        out_specs=(row(d), _full((1, d))),
        compiler_params=_params("arbitrary"),
    )(dproj, w, x, gain, dres)


def _inproj_finish(dh, x_ref, g_ref, dres_ref, dx_ref, dg_ref, step):
    xv = x_ref[...]
    rstd = lax.rsqrt(jnp.mean(xv * xv, axis=-1, keepdims=True) + NORM_EPS)
    xhat = xv * rstd
    dxn = dh * g_ref[...]
    dx_ref[...] = dres_ref[...] + rstd * (dxn - xhat * jnp.mean(dxn * xhat, axis=-1, keepdims=True))
    _accumulate(dg_ref, _colsum(dh * xhat), step)


def inproj_bwd_pieces(pieces, w, x, gain, dres, name):
    rows, d = x.shape
    tm = min(ROW_TILE, rows)
    n = len(pieces)
    widths = [p.shape[1] for p in pieces]
    starts = [sum(widths[:k]) for k in range(n)]
    assert sum(widths) == w.shape[2]

    def body(*refs):
        w_ref, x_ref, g_ref, dres_ref, dx_ref, dg_ref = refs[n:]
        dh = None
        for k in range(n):
            part = _dot_nt(refs[k][...], w_ref[0, :, starts[k]:starts[k] + widths[k]])
            dh = part if dh is None else dh + part
        _inproj_finish(dh, x_ref, g_ref, dres_ref, dx_ref, dg_ref, pl.program_id(0))

    row = lambda cols: pl.BlockSpec((tm, cols), lambda i: (i, 0))
    return pl.pallas_call(
        body, name=name,
        out_shape=(jax.ShapeDtypeStruct((rows, d), F32), jax.ShapeDtypeStruct((1, d), F32)),
        grid=(rows // tm,),
        in_specs=[row(wd) for wd in widths] + [_full(w.shape), row(d), _full((1, d)), row(d)],
        out_specs=(row(d), _full((1, d))),
        compiler_params=_params("arbitrary"),
    )(*pieces, w, x, gain, dres)


def matmul_dw_pieces(a, pieces, name):
    rows, m = a.shape
    tk = min(2 * ROW_TILE, rows)
    n = len(pieces)

    def body(*refs):
        a_ref, ins, outs = refs[0], refs[1:1 + n], refs[1 + n:]
        av = a_ref[...]
        for k in range(n):
            _accumulate(outs[k], _dot_tn(av, ins[k][...]), pl.program_id(0))

    return pl.pallas_call(
        body, name=name,
        out_shape=[jax.ShapeDtypeStruct((m, p.shape[1]), F32) for p in pieces],
        grid=(rows // tk,),
        in_specs=[pl.BlockSpec((tk, m), lambda k: (k, 0))]
        + [pl.BlockSpec((tk, p.shape[1]), lambda k: (k, 0)) for p in pieces],
        out_specs=[_full((m, p.shape[1])) for p in pieces],
        compiler_params=_params("arbitrary"),
    )(a, *pieces)


def matmul_dw(a, b, bn, name):
    rows, m = a.shape
    n = b.shape[1]
    tk = min((4 if n > bn else 2) * ROW_TILE, rows)
    steps = rows // tk

    def body(a_ref, b_ref, o_ref, o16_ref):
        part = _dot_tn(a_ref[...], b_ref[...])

        @pl.when(pl.program_id(1) == 0)
        def _():
            o_ref[0] = part

        @pl.when(pl.program_id(1) > 0)
        def _():
            o_ref[0] += part

        @pl.when(pl.program_id(1) == steps - 1)
        def _():
            o16_ref[0] = o_ref[0].astype(BF16)

    out = pl.BlockSpec((1, m, bn), lambda j, k: (j, 0, 0))
    return pl.pallas_call(
        body, name=name,
        out_shape=(jax.ShapeDtypeStruct((n // bn, m, bn), F32), jax.ShapeDtypeStruct((n // bn, m, bn), BF16)),
        grid=(n // bn, steps),
        in_specs=[pl.BlockSpec((tk, m), lambda j, k: (k, 0)), pl.BlockSpec((tk, bn), lambda j, k: (k, j))],
        out_specs=(out, out),
        compiler_params=_params("parallel", "arbitrary"),
    )(a, b)


def _scan(a, b, carry, reverse):
    n, c = a.shape
    blocks = n // SUBLANES
    a = a.reshape(blocks, SUBLANES, c)
    b = b.reshape(blocks, SUBLANES, c)
    pos = lax.broadcasted_iota(jnp.int32, (1, SUBLANES, c), 1)
    s = 1
    while s < SUBLANES:
        shift, valid = (SUBLANES - s, pos < SUBLANES - s) if reverse else (s, pos >= s)
        a_s, b_s = pltpu.roll(a, shift, 1), pltpu.roll(b, shift, 1)
        b = jnp.where(valid, a * b_s + b, b)
        a = jnp.where(valid, a * a_s, a)
        s *= 2
    out = [None] * blocks
    for k in (range(blocks - 1, -1, -1) if reverse else range(blocks)):
        h = a[k] * carry + b[k]
        out[k] = h
        carry = h[0:1] if reverse else h[SUBLANES - 1:SUBLANES]
    return jnp.concatenate(out, axis=0)


def _rg_gates(ua, gw_ref, gb, lam):
    ub = ua.astype(BF16)
    pre_r, pre_i = [], []
    for h in range(RG_HEADS):
        z = _dot(ub[:, h * RG_HEAD_DIM:(h + 1) * RG_HEAD_DIM], gw_ref[h])
        pre_r.append(z[:, :RG_HEAD_DIM])
        pre_i.append(z[:, RG_HEAD_DIM:])
    r = _sigmoid(jnp.concatenate(pre_r, axis=1) + gb[0:1])
    i = _sigmoid(jnp.concatenate(pre_i, axis=1) + gb[1:2])
    sp = _softplus(-lam)
    log_a = -RG_C * r * sp
    a = jnp.exp(log_a)
    mult = jnp.sqrt(1.0 - a * a)
    return r, i, sp, a, mult


def _rg_weight_specs():
    return [_full((4, D_MODEL)), _full((1, D_MODEL)), _full((RG_HEADS, RG_HEAD_DIM, 2 * RG_HEAD_DIM)),
            _full((2, D_MODEL)), _full((1, D_MODEL))]


def rglru_fwd(proj, conv_w, conv_b, gate_w, gate_b, lam, reverse, name):
    rows_total = proj.shape[0]
    rows = min(SCAN_TILE, rows_total)
    n_tiles = rows_total // rows
    tix = (lambda i: n_tiles - 1 - i) if reverse else (lambda i: i)

    def body(xp, xm, xn, cw_ref, cb_ref, gw_ref, gb_ref, lam_ref, h_ref, acts_ref, carry):
        i = pl.program_id(0)
        t = tix(i)
        ext = _extend(xp, xm, xn, t == 0, t == n_tiles - 1)
        ua = _conv(ext, cw_ref[...], 2, rows) + cb_ref[...]
        r, gi, _, a, mult = _rg_gates(ua, gw_ref, gb_ref[...], lam_ref[...])
        for k, saved in enumerate((ua, r, gi, a, mult)):
            acts_ref[k] = saved
        b = mult * (gi * ua)

        @pl.when(i == 0)
        def _():
            carry[...] = jnp.zeros_like(carry)

        h = _scan(a, b, carry[0:1], reverse)
        h_ref[...] = h
        edge = h[0:1] if reverse else h[rows - 1:rows]
        carry[...] = jnp.broadcast_to(edge, carry.shape)

    return pl.pallas_call(
        body, name=name,
        out_shape=(jax.ShapeDtypeStruct((rows_total, D_MODEL), F32),
                   jax.ShapeDtypeStruct((5, rows_total, D_MODEL), F32)),
        grid=(n_tiles,),
        in_specs=_halo_specs(rows, D_MODEL, 0, n_tiles, tix) + _rg_weight_specs(),
        out_specs=(pl.BlockSpec((rows, D_MODEL), lambda i: (tix(i), 0)),
                   pl.BlockSpec((5, rows, D_MODEL), lambda i: (0, tix(i), 0))),
        scratch_shapes=[pltpu.VMEM((SUBLANES, D_MODEL), F32)],
        compiler_params=_params("arbitrary"),
    )(proj, proj, proj, conv_w, conv_b, gate_w, gate_b, lam)


def rglru_bwd(proj, dycat, h_dir, acts, gate_w, lam, add_dua, reverse, name):
    rows_total = proj.shape[0]
    rows = min(SCAN_TILE, rows_total)
    n_tiles = rows_total // rows
    tix = (lambda i: i) if reverse else (lambda i: n_tiles - 1 - i)
    za_block = 1

    def body(acts_ref, za_ref, dya_ref, hp, hm, hn, gw_ref, lam_ref, *rest):
        other = rest[0][...] if add_dua is not None else 0.0
        dua_ref, dgw_ref, dgb_ref, dlam_ref, carry = rest[-5:]
        step = pl.program_id(0)
        t = tix(step)
        first, last = t == 0, t == n_tiles - 1
        ua, r, gi, a, mult = (acts_ref[k] for k in range(5))
        lam_v = lam_ref[...]
        sp = _softplus(-lam_v)
        za = za_ref[...].astype(F32)
        dh = dya_ref[...] * (za * _sigmoid(za))

        @pl.when(step == 0)
        def _():
            carry[...] = jnp.zeros_like(carry)

        old = carry[0:1]
        mu = _scan(a, a * dh, old, not reverse)
        row = lax.broadcasted_iota(jnp.int32, mu.shape, 0)
        if reverse:
            mu_next = jnp.where(row == 0, old, pltpu.roll(mu, 1, 0))
            carry[...] = jnp.broadcast_to(mu[rows - 1:rows], carry.shape)
            h_ext = _extend(hp, hm, hn, first, last)
            h_prev = _shifted(h_ext, 1, rows)
        else:
            mu_next = jnp.where(row == rows - 1, old, pltpu.roll(mu, rows - 1, 0))
            carry[...] = jnp.broadcast_to(mu[0:1], carry.shape)
            h_ext = _extend(hp, hm, hn, first, last)
            h_prev = _shifted(h_ext, -1, rows)
        db = dh + mu_next
        da = db * h_prev
        d_mult = db * (gi * ua)
        di = db * (mult * ua)
        dua = db * (mult * gi)
        dlog_a = da * a - d_mult * (a * a) / mult
        dr = dlog_a * (-RG_C * sp)
        dlam = _colsum(dlog_a * (-RG_C * r)) * (-_sigmoid(-lam_v))
        dpr = dr * (r * (1.0 - r))
        dpi = di * (gi * (1.0 - gi))
        dgb = jnp.concatenate([_colsum(dpr), _colsum(dpi)], axis=0)
        ub = ua.astype(BF16)
        dua_heads, dgw_heads = [], []
        for h in range(RG_HEADS):
            cols = slice(h * RG_HEAD_DIM, (h + 1) * RG_HEAD_DIM)
            dz = jnp.concatenate([dpr[:, cols], dpi[:, cols]], axis=1).astype(BF16)
            dgw_heads.append(_dot_tn(ub[:, cols], dz))
            dua_heads.append(_dot_nt(dz, gw_ref[h]))
        dua_ref[...] = dua + jnp.concatenate(dua_heads, axis=1) + other

        @pl.when(step == 0)
        def _():
            for h in range(RG_HEADS):
                dgw_ref[h] = dgw_heads[h]
            dgb_ref[...] = dgb
            dlam_ref[...] = dlam

        @pl.when(step > 0)
        def _():
            for h in range(RG_HEADS):
                dgw_ref[h] += dgw_heads[h]
            dgb_ref[...] += dgb
            dlam_ref[...] += dlam

    row_spec = lambda col: pl.BlockSpec((rows, D_MODEL), lambda i: (tix(i), col))
    return pl.pallas_call(
        body, name=name,
        out_shape=(jax.ShapeDtypeStruct((rows_total, D_MODEL), F32),
                   jax.ShapeDtypeStruct((RG_HEADS, RG_HEAD_DIM, 2 * RG_HEAD_DIM), F32),
                   jax.ShapeDtypeStruct((2, D_MODEL), F32), jax.ShapeDtypeStruct((1, D_MODEL), F32)),
        grid=(n_tiles,),
        in_specs=([pl.BlockSpec((5, rows, D_MODEL), lambda i: (0, tix(i), 0)), row_spec(za_block), row_spec(0)]
                  + _halo_specs(rows, D_MODEL, 0, n_tiles, tix)
                  + [_full((RG_HEADS, RG_HEAD_DIM, 2 * RG_HEAD_DIM)), _full((1, D_MODEL))]
                  + ([] if add_dua is None else [row_spec(0)])),
        out_specs=(row_spec(0), _full((RG_HEADS, RG_HEAD_DIM, 2 * RG_HEAD_DIM)), _full((2, D_MODEL)),
                   _full((1, D_MODEL))),
        scratch_shapes=[pltpu.VMEM((SUBLANES, D_MODEL), F32)],
        compiler_params=_params("arbitrary"),
    )(acts, proj, dycat, h_dir, h_dir, h_dir, gate_w, lam, *([] if add_dua is None else [add_dua]))


def _extend_cols(refs, block, is_first, is_last):
    cols = slice(block * D_MODEL, (block + 1) * D_MODEL)
    prev_ref, main_ref, next_ref = refs
    p = jnp.where(is_first, 0.0, prev_ref[:, cols].astype(F32))
    n = jnp.where(is_last, 0.0, next_ref[:, cols].astype(F32))
    return jnp.concatenate([p, main_ref[:, cols].astype(F32), n], axis=0)


def even_mix_fwd(proj, h_f, h_b, sc_w, name):
    rows_total = proj.shape[0]
    rows = min(2 * MIX_TILE, rows_total)
    n_tiles = rows_total // rows
    ident = lambda i: i

    def body(za_ref, hf_ref, hb_ref, xbp, xbm, xbn, gcp, gcm, gcn, gb_ref, zb_ref, w_ref, y_ref):
        t = pl.program_id(0)
        first, last = t == 0, t == n_tiles - 1
        za = za_ref[...].astype(F32)
        y_ref[:, 0:D_MODEL] = ((hf_ref[...] + hb_ref[...]) * (za * _sigmoid(za))).astype(BF16)
        p_ext = _extend(xbp, xbm, xbn, first, last) * _extend(gcp, gcm, gcn, first, last)
        cv = _conv(p_ext, w_ref[...], 1, rows)
        zb = zb_ref[...].astype(F32)
        y_ref[:, D_MODEL:2 * D_MODEL] = (gb_ref[...].astype(F32) * cv * (zb * _sigmoid(zb))).astype(BF16)

    blk = lambda col: pl.BlockSpec((rows, D_MODEL), lambda i: (i, col))
    return pl.pallas_call(
        body, name=name,
        out_shape=jax.ShapeDtypeStruct((rows_total, 2 * D_MODEL), BF16),
        grid=(n_tiles,),
        in_specs=([blk(1), blk(0), blk(0)] + _halo_specs(rows, D_MODEL, 2, n_tiles, ident)
                  + _halo_specs(rows, D_MODEL, 4, n_tiles, ident) + [blk(3), blk(5), _full((3, D_MODEL))]),
        out_specs=pl.BlockSpec((rows, 2 * D_MODEL), lambda i: (i, 0)),
        compiler_params=_params("parallel"),
    )(proj, h_f, h_b, proj, proj, proj, proj, proj, proj, proj, proj, sc_w)


def even_mix_bwd(proj, dycat, h_f, h_b, dua, conv_w, sc_w, name):
    rows_total, width = proj.shape
    rows = min(MIX_TILE, rows_total)
    n_tiles = rows_total // rows
    ident = lambda i: i

    def body(pp, pm, pn, dyp, dym, dyn, hf_ref, hb_ref, dup, dum, dun, cw_ref, sw_ref,
             dp_ref, dcw_ref, dcb_ref, dsw_ref):
        def put(k, value):
            dp_ref[:, k * D_MODEL:(k + 1) * D_MODEL] = value.astype(BF16)

        t = pl.program_id(0)
        first, last = t == 0, t == n_tiles - 1
        proj_ext = lambda k: _extend_cols((pp, pm, pn), k, first, last)
        mid = slice(HALO, HALO + rows)
        za = pm[:, D_MODEL:2 * D_MODEL].astype(F32)
        sa = _sigmoid(za)
        put(1, dym[:, 0:D_MODEL] * (hf_ref[...] + hb_ref[...]) * (sa * (1.0 + za * (1.0 - sa))))
        dua_ext = _extend(dup, dum, dun, first, last)
        cw = cw_ref[...]
        put(0, _conv_transpose(dua_ext, cw, 2, rows))
        dua_mid = dua_ext[mid]
        xa_ext = proj_ext(0)
        dcw = jnp.concatenate([_colsum(dua_mid * _shifted(xa_ext, k - 2, rows)) for k in range(4)], axis=0)
        dcb = _colsum(dua_mid)
        xb_ext, gb_ext, gc_ext, zb_ext = proj_ext(2), proj_ext(3), proj_ext(4), proj_ext(5)
        p_ext = xb_ext * gc_ext
        sb_ext = _sigmoid(zb_ext)
        dyb_ext = _extend_cols((dyp, dym, dyn), 1, first, last)
        dcv_ext = dyb_ext * gb_ext * (zb_ext * sb_ext)
        sw = sw_ref[...]
        p_at = [_shifted(p_ext, k - 1, rows) for k in range(3)]
        cv = (p_at[0] * sw[0:1] + p_at[1] * sw[1:2]) + p_at[2] * sw[2:3]
        zb, sb, dyb, gb = zb_ext[mid], sb_ext[mid], dyb_ext[mid], gb_ext[mid]
        put(3, dyb * cv * (zb * sb))
        put(5, dyb * gb * cv * (sb * (1.0 + zb * (1.0 - sb))))
        dp = _conv_transpose(dcv_ext, sw, 1, rows)
        put(4, dp * xb_ext[mid])
        put(2, dp * gc_ext[mid])
        dcv = dcv_ext[mid]
        dsw = jnp.concatenate([_colsum(dcv * p_at[k]) for k in range(3)], axis=0)
        _accumulate(dcw_ref, dcw, t)
        _accumulate(dcb_ref, dcb, t)
        _accumulate(dsw_ref, dsw, t)

    own = pl.BlockSpec((rows, D_MODEL), lambda i: (i, 0))
    return pl.pallas_call(
        body, name=name,
        out_shape=(jax.ShapeDtypeStruct((rows_total, 6 * D_MODEL), BF16),
                   jax.ShapeDtypeStruct((4, D_MODEL), F32), jax.ShapeDtypeStruct((1, D_MODEL), F32),
                   jax.ShapeDtypeStruct((3, D_MODEL), F32)),
        grid=(n_tiles,),
        in_specs=(_halo_specs(rows, width, 0, n_tiles, ident) + _halo_specs(rows, 2 * D_MODEL, 0, n_tiles, ident)
                  + [own, own] + _halo_specs(rows, D_MODEL, 0, n_tiles, ident)
                  + [_full((4, D_MODEL)), _full((3, D_MODEL))]),
        out_specs=(pl.BlockSpec((rows, 6 * D_MODEL), lambda i: (i, 0)), _full((4, D_MODEL)), _full((1, D_MODEL)),
                   _full((3, D_MODEL))),
        compiler_params=_params("arbitrary"),
    )(proj, proj, proj, dycat, dycat, dycat, h_f, h_b, dua, dua, dua, conv_w, sc_w)


def even_out_fwd(ycat, w_out, gain, x, name):
    rows, d = x.shape
    k = ycat.shape[1]
    tm = min(ROW_TILE, rows)

    def body(yc_ref, w_ref, g_ref, x_ref, x1_ref, y_ref):
        y = _dot(yc_ref[...], w_ref[...])
        y_ref[...] = y
        rstd = lax.rsqrt(jnp.mean(y * y, axis=-1, keepdims=True) + NORM_EPS)
        x1_ref[...] = x_ref[...] + y * rstd * g_ref[...]

    row = lambda n: pl.BlockSpec((tm, n), lambda i: (i, 0))
    return pl.pallas_call(
        body, name=name,
        out_shape=(jax.ShapeDtypeStruct((rows, d), F32),) * 2,
        grid=(rows // tm,),
        in_specs=[row(k), _full((k, d)), _full((1, d)), row(d)],
        out_specs=(row(d), row(d)),
        compiler_params=_params("parallel"),
    )(ycat, w_out, gain, x)


def _rmsnorm_bwd(dout, y, gain):
    rstd = lax.rsqrt(jnp.mean(y * y, axis=-1, keepdims=True) + NORM_EPS)
    yhat = y * rstd
    dyn = dout * gain
    dy = rstd * (dyn - yhat * jnp.mean(dyn * yhat, axis=-1, keepdims=True))
    return dy, dout * yhat


def even_out_bwd(dx1, y, gain, w_out, name):
    rows, d = y.shape
    k = w_out.shape[0]
    tm = min(ROW_TILE, rows)

    def body(dx_ref, y_ref, g_ref, w_ref, dy_ref, dyc_ref, dg_ref):
        dy, dg_rows = _rmsnorm_bwd(dx_ref[...], y_ref[...], g_ref[...])
        dyb = dy.astype(BF16)
        dy_ref[...] = dyb
        dyc_ref[...] = _dot_nt(dyb, w_ref[...])
        _accumulate(dg_ref, _colsum(dg_rows), pl.program_id(0))

    row = lambda n: pl.BlockSpec((tm, n), lambda i: (i, 0))
    return pl.pallas_call(
        body, name=name,
        out_shape=(jax.ShapeDtypeStruct((rows, d), BF16), jax.ShapeDtypeStruct((rows, k), F32),
                   jax.ShapeDtypeStruct((1, d), F32)),
        grid=(rows // tm,),
        in_specs=[row(d), row(d), _full((1, d)), _full((k, d))],
        out_specs=(row(d), row(k), _full((1, d))),
        compiler_params=_params("arbitrary"),
    )(dx1, y, gain, w_out)


def _chunk_cumsum(g, reverse):
    n, c = g.shape
    chunks, per = n // GLA_CHUNK, GLA_CHUNK // SUBLANES
    g = g.reshape(n // SUBLANES, SUBLANES, c)
    pos = lax.broadcasted_iota(jnp.int32, (1, SUBLANES, c), 1)
    s = 1
    while s < SUBLANES:
        if reverse:
            g = g + jnp.where(pos < SUBLANES - s, pltpu.roll(g, SUBLANES - s, 1), 0.0)
        else:
            g = g + jnp.where(pos >= s, pltpu.roll(g, s, 1), 0.0)
        s *= 2
    g = g.reshape(chunks, per, SUBLANES, c)
    out, carry = [None] * per, None
    for k in (range(per - 1, -1, -1) if reverse else range(per)):
        out[k] = g[:, k] if carry is None else g[:, k] + carry
        carry = out[k][:, 0:1] if reverse else out[k][:, SUBLANES - 1:SUBLANES]
    return jnp.stack(out, axis=1).reshape(n, c)


def _gla_prepare(q_ref, k_ref, lr_ref, wg_ref, bg_ref, reverse, n_chunks):
    z = _dot(lr_ref[...].astype(BF16), wg_ref[0]) + bg_ref[0]
    g = -_softplus(-z) * (1.0 / GLA_NORMALIZER)
    bcum = _chunk_cumsum(g, reverse).reshape(n_chunks, GLA_CHUNK, GLA_DK)
    edge = 0 if reverse else GLA_CHUNK - 1
    btot = bcum[:, edge:edge + 1, :]
    e_pos = jnp.exp(bcum)
    e_neg = jnp.exp(-bcum)
    e_st = jnp.exp(btot - bcum)
    q3 = q_ref[...].reshape(n_chunks, GLA_CHUNK, GLA_DK)
    k3 = k_ref[...].reshape(n_chunks, GLA_CHUNK, GLA_DK)
    scale = GLA_DK ** -0.5
    q_in = q3 * scale * e_pos
    k_in = k3 * e_neg
    k_st = k3 * e_st
    dec = jnp.exp(btot)
    return z, q_in, k_in, k_st, dec, (scale * e_pos, e_neg, e_st)


def _gla_mask(reverse):
    i = lax.broadcasted_iota(jnp.int32, (GLA_CHUNK, GLA_CHUNK), 0)
    j = lax.broadcasted_iota(jnp.int32, (GLA_CHUNK, GLA_CHUNK), 1)
    return (j >= i) if reverse else (j <= i)


def _gla_specs(rows, n_blocks, reverse):
    tix = (lambda s: n_blocks - 1 - s) if reverse else (lambda s: s)
    d = 1 if reverse else 0
    lr_block = LR_COL // LANES
    specs = [pl.BlockSpec((rows, GLA_DK), lambda h, s: (tix(s), h)),
             pl.BlockSpec((rows, GLA_DK), lambda h, s: (tix(s), GLA_HEADS + h)),
             pl.BlockSpec((rows, GLA_DV), lambda h, s: (tix(s), GLA_HEADS + h)),
             pl.BlockSpec((rows, LANES), lambda h, s: (tix(s), lr_block)),
             pl.BlockSpec((1, LANES, GLA_DK), lambda h, s: (d, 0, h)),
             pl.BlockSpec((1, 1, GLA_DK), lambda h, s: (d, 0, h))]
    return specs, tix


def gla_fwd(proj, wg_pad, bg, add_o, reverse, name):
    rows_total = proj.shape[0]
    rows = min(GLA_BLOCK, rows_total)
    n_blocks = rows_total // rows
    n_chunks = rows // GLA_CHUNK
    specs, tix = _gla_specs(rows, n_blocks, reverse)

    def body(q_ref, k_ref, v_ref, lr_ref, wg_ref, bg_ref, *rest):
        o_ref, st_ref, state, kv_scr, dec_scr = rest[-5:]
        _, q_in, k_in, k_st, dec, _ = _gla_prepare(q_ref, k_ref, lr_ref, wg_ref, bg_ref, reverse, n_chunks)
        vb = v_ref[...].reshape(n_chunks, GLA_CHUNK, GLA_DV).astype(BF16)
        qb = q_in.astype(BF16)
        p = jnp.where(_gla_mask(reverse), _bdot(qb, k_in.astype(BF16), 2, 2), 0.0)
        o = _bdot(p.astype(BF16), vb, 2, 1)
        kv_scr[...] = _bdot(vb, k_st.astype(BF16), 1, 1)
        dec_scr[...] = jnp.broadcast_to(dec, dec_scr.shape)

        @pl.when(pl.program_id(1) == 0)
        def _():
            state[...] = jnp.zeros_like(state)

        for c in range(n_chunks):
            cc = n_chunks - 1 - c if reverse else c
            st_ref[0, cc] = state[...]
            state[...] = state[...] * dec_scr[cc, 0:1] + kv_scr[cc]
        o = o + _bdot(qb, st_ref[0].astype(BF16), 2, 2)
        o = o.reshape(rows, GLA_DV)
        o_ref[...] = o if add_o is None else o + rest[0][...]

    o_spec = pl.BlockSpec((rows, GLA_DV), lambda h, s: (tix(s), h))
    return pl.pallas_call(
        body, name=name,
        out_shape=(jax.ShapeDtypeStruct((rows_total, GLA_HEADS * GLA_DV), F32),
                   jax.ShapeDtypeStruct((GLA_HEADS, rows_total // GLA_CHUNK, GLA_DV, GLA_DK), F32)),
        grid=(GLA_HEADS, n_blocks),
        in_specs=specs + ([] if add_o is None else [o_spec]),
        out_specs=(o_spec,
                   pl.BlockSpec((1, n_chunks, GLA_DV, GLA_DK), lambda h, s: (h, tix(s), 0, 0))),
        scratch_shapes=[pltpu.VMEM((GLA_DV, GLA_DK), F32), pltpu.VMEM((n_chunks, GLA_DV, GLA_DK), F32),
                        pltpu.VMEM((n_chunks, SUBLANES, GLA_DK), F32)],
        compiler_params=_params("parallel", "arbitrary"),
    )(proj, proj, proj, proj, wg_pad, bg, *([] if add_o is None else [add_o]))


def gla_bwd(proj, wg_pad, bg, d_o, states, dqkv_in, reverse, name):
    rows_total = proj.shape[0]
    rows = min(GLA_BLOCK, rows_total)
    n_blocks = rows_total // rows
    n_chunks = rows // GLA_CHUNK
    specs, tix = _gla_specs(rows, n_blocks, not reverse)
    d = 1 if reverse else 0
    specs[4] = pl.BlockSpec((1, LANES, GLA_DK), lambda h, s: (d, 0, h))
    specs[5] = pl.BlockSpec((1, 1, GLA_DK), lambda h, s: (d, 0, h))
    add = dqkv_in is not None

    def body(*refs):
        q_ref, k_ref, v_ref, lr_ref, wg_ref, bg_ref, do_ref, st_ref = refs[:8]
        refs = refs[8:]
        if add:
            aq_ref, ak_ref, av_ref = refs[:3]
            refs = refs[3:]
        dq_ref, dk_ref, dv_ref, dz_ref, dstate, g_scr, dec_scr, dsn_scr = refs
        z, q_in, k_in, k_st, dec, (f_q, f_k, f_s) = _gla_prepare(q_ref, k_ref, lr_ref, wg_ref, bg_ref, reverse,
                                                                 n_chunks)
        mask = _gla_mask(reverse)
        vb = v_ref[...].reshape(n_chunks, GLA_CHUNK, GLA_DV).astype(BF16)
        dob = do_ref[...].reshape(n_chunks, GLA_CHUNK, GLA_DV).astype(BF16)
        qb, kb, ksb = q_in.astype(BF16), k_in.astype(BF16), k_st.astype(BF16)
        st = st_ref[0]
        stb = st.astype(BF16)
        pb = jnp.where(mask, _bdot(qb, kb, 2, 2), 0.0).astype(BF16)
        dpb = jnp.where(mask, _bdot(dob, vb, 2, 2), 0.0).astype(BF16)
        d_qin = _bdot(dpb, kb, 2, 1) + _bdot(dob, stb, 2, 1)
        d_kin = _bdot(dpb, qb, 1, 1)
        dv = _bdot(pb, dob, 1, 1)
        g_scr[...] = _bdot(dob, qb, 1, 1)
        dec_scr[...] = jnp.broadcast_to(dec, dec_scr.shape)

        @pl.when(pl.program_id(1) == 0)
        def _():
            dstate[...] = jnp.zeros_like(dstate)

        for c in range(n_chunks):
            cc = c if reverse else n_chunks - 1 - c
            dsn_scr[cc] = dstate[...]
            dstate[...] = dstate[...] * dec_scr[cc, 0:1] + g_scr[cc]
        dsn = dsn_scr[...]
        dsnb = dsn.astype(BF16)
        dv = dv + _bdot(ksb, dsnb, 2, 2)
        d_kst = _bdot(vb, dsnb, 2, 1)
        d_dec = jnp.sum(dsn * st, axis=1, keepdims=True)
        ks_term = d_kst * k_st
        d_btot = d_dec * dec + jnp.sum(ks_term, axis=1, keepdims=True)
        d_b = d_qin * q_in - d_kin * k_in - ks_term
        pos = lax.broadcasted_iota(jnp.int32, d_b.shape, 1)
        edge = 0 if reverse else GLA_CHUNK - 1
        d_b = d_b + jnp.where(pos == edge, d_btot, 0.0)
        dg = _chunk_cumsum(d_b.reshape(rows, GLA_DK), not reverse)
        dz_ref[...] = dg * (1.0 / GLA_NORMALIZER) * _sigmoid(-z)
        dq = (d_qin * f_q).reshape(rows, GLA_DK)
        dk = (d_kin * f_k + d_kst * f_s).reshape(rows, GLA_DK)
        dv = dv.reshape(rows, GLA_DV)
        if add:
            dq_ref[...] = (dq + aq_ref[...]).astype(BF16)
            dk_ref[...] = (dk + ak_ref[...]).astype(BF16)
            dv_ref[...] = (dv + av_ref[...]).astype(BF16)
        else:
            dq_ref[...] = dq
            dk_ref[...] = dk
            dv_ref[...] = dv

    qkv_specs = [pl.BlockSpec((rows, GLA_DK), lambda h, s: (tix(s), h)),
                 pl.BlockSpec((rows, GLA_DK), lambda h, s: (tix(s), h)),
                 pl.BlockSpec((rows, GLA_DV), lambda h, s: (tix(s), h))]
    in_specs = specs + [pl.BlockSpec((rows, GLA_DV), lambda h, s: (tix(s), h)),
                        pl.BlockSpec((1, n_chunks, GLA_DV, GLA_DK), lambda h, s: (h, tix(s), 0, 0))]
    args = [proj, proj, proj, proj, wg_pad, bg, d_o, states]
    out_dtype = F32
    if add:
        in_specs += qkv_specs
        args += list(dqkv_in)
        out_dtype = BF16
    return pl.pallas_call(
        body, name=name,
        out_shape=(jax.ShapeDtypeStruct((rows_total, GLA_HEADS * GLA_DK), out_dtype),
                   jax.ShapeDtypeStruct((rows_total, GLA_HEADS * GLA_DK), out_dtype),
                   jax.ShapeDtypeStruct((rows_total, GLA_HEADS * GLA_DV), out_dtype),
                   jax.ShapeDtypeStruct((rows_total, GLA_HEADS * GLA_DK), F32)),
        grid=(GLA_HEADS, n_blocks),
        in_specs=in_specs,
        out_specs=(pl.BlockSpec((rows, GLA_DK), lambda h, s: (tix(s), h)),
                   pl.BlockSpec((rows, GLA_DK), lambda h, s: (tix(s), h)),
                   pl.BlockSpec((rows, GLA_DV), lambda h, s: (tix(s), h)),
                   pl.BlockSpec((rows, GLA_DK), lambda h, s: (tix(s), h))),
        scratch_shapes=[pltpu.VMEM((GLA_DV, GLA_DK), F32), pltpu.VMEM((n_chunks, GLA_DV, GLA_DK), F32),
                        pltpu.VMEM((n_chunks, SUBLANES, GLA_DK), F32),
                        pltpu.VMEM((n_chunks, GLA_DV, GLA_DK), F32)],
        compiler_params=_params("parallel", "arbitrary"),
    )(*args)


def gla_gate_bwd(proj, dz_f, dz_b, wg_pad, name):
    rows_total = proj.shape[0]
    tm = min(ROW_TILE, rows_total)
    n_key = GLA_HEADS * GLA_DK

    def body(lr_ref, dzf_ref, dzb_ref, wg_ref, dlr_ref, dwg_ref, dbg_ref):
        step = pl.program_id(0)
        lr_t = jnp.transpose(lr_ref[...])
        dzf, dzb = dzf_ref[...], dzb_ref[...]
        dzf16, dzb16 = dzf.astype(BF16), dzb.astype(BF16)
        dlr_ref[...] = (_dot_nt(dzf16, wg_ref[0]) + _dot_nt(dzb16, wg_ref[1])).astype(BF16)
        dwf = _dot(lr_t[0:GLA_RANK].astype(BF16), dzf16)
        dwb = _dot(lr_t[GLA_RANK:2 * GLA_RANK].astype(BF16), dzb16)
        dbg = jnp.concatenate([_colsum(dzf), _colsum(dzb)], axis=0)

        @pl.when(step == 0)
        def _():
            dwg_ref[0] = dwf
            dwg_ref[1] = dwb
            dbg_ref[...] = dbg

        @pl.when(step > 0)
        def _():
            dwg_ref[0] += dwf
            dwg_ref[1] += dwb
            dbg_ref[...] += dbg

    return pl.pallas_call(
        body, name=name,
        out_shape=(jax.ShapeDtypeStruct((rows_total, LANES), BF16), jax.ShapeDtypeStruct((2, GLA_RANK, n_key), F32),
                   jax.ShapeDtypeStruct((2, n_key), F32)),
        grid=(rows_total // tm,),
        in_specs=[pl.BlockSpec((tm, LANES), lambda i: (i, LR_COL // LANES)),
                  pl.BlockSpec((tm, n_key), lambda i: (i, 0)), pl.BlockSpec((tm, n_key), lambda i: (i, 0)),
                  _full((2, LANES, n_key))],
        out_specs=(pl.BlockSpec((tm, LANES), lambda i: (i, 0)), _full((2, GLA_RANK, n_key)), _full((2, n_key))),
        compiler_params=_params("arbitrary"),
    )(proj, dz_f, dz_b, wg_pad)


def _head_norm(o, gain):
    outs, hats, rstds = [], [], []
    for h in range(GLA_HEADS):
        oh = o[:, h * GLA_DV:(h + 1) * GLA_DV]
        rstd = lax.rsqrt(jnp.mean(oh * oh, axis=-1, keepdims=True) + NORM_EPS)
        hat = oh * rstd
        outs.append(hat * gain)
        hats.append(hat)
        rstds.append(rstd)
    return outs, hats, rstds


def odd_out_fwd(o, proj, head_gain, w_out, gain, x1, target, name):
    rows, d = x1.shape
    tm = min(ROW_TILE, rows)
    r_block = (2 * GLA_HEADS * GLA_DK + GLA_HEADS * GLA_DV) // d

    def body(o_ref, r_ref, hg_ref, w_ref, g_ref, x1_ref, tgt_ref, y2_ref, dy_ref, dx2_ref, loss_ref, dg_ref):
        step = pl.program_id(0)
        on, _, _ = _head_norm(o_ref[...], hg_ref[...])
        r = r_ref[...]
        y2 = (jnp.concatenate(on, axis=1) * (r * _sigmoid(r))).astype(BF16)
        y2_ref[...] = y2
        y = _dot(y2, w_ref[...])
        gain_v = g_ref[...]
        rstd = lax.rsqrt(jnp.mean(y * y, axis=-1, keepdims=True) + NORM_EPS)
        x2 = x1_ref[...] + y * rstd * gain_v
        diff = x2 - tgt_ref[...]
        loss = 0.5 * jnp.sum(jnp.mean(diff * diff, axis=-1, keepdims=True), axis=0, keepdims=True)
        dx2 = diff * (1.0 / d)
        dx2_ref[...] = dx2
        dy, dg_rows = _rmsnorm_bwd(dx2, y, gain_v)
        dy_ref[...] = dy.astype(BF16)
        _accumulate(loss_ref, jnp.broadcast_to(loss, loss_ref.shape), step)
        _accumulate(dg_ref, _colsum(dg_rows), step)

    row = lambda n, col=0: pl.BlockSpec((tm, n), lambda i: (i, col))
    return pl.pallas_call(
        body, name=name,
        out_shape=(jax.ShapeDtypeStruct((rows, d), BF16), jax.ShapeDtypeStruct((rows, d), BF16),
                   jax.ShapeDtypeStruct((rows, d), F32), jax.ShapeDtypeStruct((SUBLANES, LANES), F32),
                   jax.ShapeDtypeStruct((1, d), F32)),
        grid=(rows // tm,),
        in_specs=[row(d), row(d, r_block), _full((1, GLA_DV)), _full((d, d)), _full((1, d)), row(d), row(d)],
        out_specs=(row(d), row(d), row(d), _full((SUBLANES, LANES)), _full((1, d))),
        compiler_params=_params("arbitrary"),
    )(o, proj, head_gain, w_out, gain, x1, target)


def odd_out_bwd(dy, w_out, o, proj, head_gain, name):
    rows, d = dy.shape
    tm = min(ROW_TILE, rows)
    r_block = (2 * GLA_HEADS * GLA_DK + GLA_HEADS * GLA_DV) // d

    def body(dy_ref, w_ref, o_ref, r_ref, hg_ref, dr_ref, do_ref, dhg_ref):
        dy2 = _dot_nt(dy_ref[...], w_ref[...])
        hg = hg_ref[...]
        on, hats, rstds = _head_norm(o_ref[...], hg)
        r = r_ref[...]
        sr = _sigmoid(r)
        dr_ref[...] = (dy2 * jnp.concatenate(on, axis=1) * (sr * (1.0 + r * (1.0 - sr)))).astype(BF16)
        d_on = dy2 * (r * sr)
        d_os, dhg = [], None
        for h in range(GLA_HEADS):
            dn = d_on[:, h * GLA_DV:(h + 1) * GLA_DV]
            part = _colsum(dn * hats[h])
            dhg = part if dhg is None else dhg + part
            dng = dn * hg
            d_os.append(rstds[h] * (dng - hats[h] * jnp.mean(dng * hats[h], axis=-1, keepdims=True)))
        do_ref[...] = jnp.concatenate(d_os, axis=1)
        _accumulate(dhg_ref, dhg, pl.program_id(0))

    row = lambda n, col=0: pl.BlockSpec((tm, n), lambda i: (i, col))
    return pl.pallas_call(
        body, name=name,
        out_shape=(jax.ShapeDtypeStruct((rows, d), BF16), jax.ShapeDtypeStruct((rows, d), F32),
                   jax.ShapeDtypeStruct((1, GLA_DV), F32)),
        grid=(rows // tm,),
        in_specs=[row(d), _full((d, d)), row(d), row(d, r_block), _full((1, GLA_DV))],
        out_specs=(row(d), row(d), _full((1, GLA_DV))),
        compiler_params=_params("arbitrary"),
    )(dy, w_out, o, proj, head_gain)


def local_step(x, target, w, reduce_first=None, reduce_second=None, late_weights=None):
    g, g16 = {}, {}
    proj_e, h0 = norm_matmul(x, w["even_norm_pre"], w["even_w_in"], BF16, "even_in_proj")
    h_dir, acts = zip(*[rglru_fwd(proj_e, w["rg_conv_w"], w["rg_conv_b"], w["rg_gate_w"][d], w["rg_gate_b"][d],
                                  w["rg_lambda"][d], d == 1, "rglru_fwd_%d" % d) for d in range(2)])
    ycat = even_mix_fwd(proj_e, h_dir[0], h_dir[1], w["sc_conv_w"], "even_mix_fwd")
    if late_weights is not None:
        w = dict(w, **late_weights(ycat))
    x1, y_e = even_out_fwd(ycat, w["even_w_out"], w["even_norm_post"], x, "even_out_fwd")
    proj_o, h1 = norm_matmul(x1, w["odd_norm_pre"], w["odd_w_in"], F32, "odd_in_proj")
    o, st_dir = None, []
    for d in range(2):
        o, st = gla_fwd(proj_o, w["gla_wg_pad"], w["gla_b_gate"], o, d == 1, "gla_fwd_%d" % d)
        st_dir.append(st)
    y2, dy_o, dx2, loss, g["odd_norm_post"] = odd_out_fwd(
        o, proj_o, w["gla_norm_g"], w["odd_w_out"], w["odd_norm_post"], x1, target, "odd_out_fwd")
    g["odd_w_out"], g16["odd_w_out"] = (a[0] for a in matmul_dw(y2, dy_o, D_MODEL, "odd_w_out_grad"))
    dr, d_o, g["gla_norm_g"] = odd_out_bwd(dy_o, w["odd_w_out"], o, proj_o, w["gla_norm_g"], "odd_out_bwd")
    dq, dk, dv, dz_f = gla_bwd(proj_o, w["gla_wg_pad"], w["gla_b_gate"], d_o, st_dir[0], None, False, "gla_bwd_0")
    dq, dk, dv, dz_b = gla_bwd(proj_o, w["gla_wg_pad"], w["gla_b_gate"], d_o, st_dir[1], (dq, dk, dv), True,
                               "gla_bwd_1")
    dlr, g["gla_w_gate_lr"], g["gla_b_gate"] = gla_gate_bwd(proj_o, dz_f, dz_b, w["gla_wg_pad"], "gla_gate_bwd")
    dproj_o = [dq, dk, dv, dr, dlr]
    g["odd_w_in"] = jnp.concatenate(matmul_dw_pieces(h1, dproj_o, "odd_w_in_grad"), axis=1)[:, :ODD_IN]
    dx1, g["odd_norm_pre"] = inproj_bwd_pieces(dproj_o, w["odd_w_in"], x1, w["odd_norm_pre"], dx2, "odd_in_proj_bwd")
    dy_e, dycat, g["even_norm_post"] = even_out_bwd(dx1, y_e, w["even_norm_post"], w["even_w_out"], "even_out_bwd")
    g["even_w_out"], g16["even_w_out"] = (a[0] for a in matmul_dw(ycat, dy_e, D_MODEL, "even_w_out_grad"))
    lam = w["rg_lambda"] if reduce_first is None else w["rg_lambda"] + reduce_first(g, g16)
    dua, dgw, dgb, dlam = None, [], [], []
    for d in range(2):
        a, b, c, e = rglru_bwd(proj_e, dycat, h_dir[d], acts[d], w["rg_gate_w"][d], lam[d], dua, d == 1,
                               "rglru_bwd_%d" % d)
        dua = a
        dgw.append(b)
        dgb.append(c)
        dlam.append(e)
    dproj_e, g["rg_conv_w"], g["rg_conv_b"], g["sc_conv_w"] = even_mix_bwd(
        proj_e, dycat, h_dir[0], h_dir[1], dua, w["rg_conv_w"], w["sc_conv_w"], "even_mix_bwd")
    dgw = jnp.stack(dgw).reshape(2, RG_HEADS, RG_HEAD_DIM, 2, RG_HEAD_DIM)
    g["rg_gate_w"] = jnp.transpose(dgw, (0, 3, 1, 2, 4))
    g["rg_gate_b"] = jnp.stack(dgb).reshape(2, 2, RG_HEADS, RG_HEAD_DIM)
    g["rg_lambda"] = jnp.concatenate(dlam, axis=0)
    g["even_w_in"], g16["even_w_in"] = matmul_dw(h0, dproj_e, EVEN_IN // 4, "even_w_in_grad")
    gain = w["even_norm_pre"] if reduce_second is None else w["even_norm_pre"] + reduce_second(g, g16)
    grad_x, g["even_norm_pre"] = inproj_bwd(dproj_e, w["even_w_in"], x, gain, dx1, "even_in_proj_bwd")
    return loss, grad_x, g


def _prepare_weights(full):
    w = {}
    for name in ("even_norm_pre", "even_norm_post", "rg_conv_b", "odd_norm_pre", "odd_norm_post", "gla_norm_g"):
        if name in full:
            w[name] = full[name].reshape(1, -1)
    for name in ("rg_conv_w", "sc_conv_w"):
        if name in full:
            w[name] = full[name]
    for name in ("even_w_out", "odd_w_out"):
        if name in full:
            w[name] = full[name].astype(BF16)
    if "even_w_in" in full:
        w["even_w_in"] = full["even_w_in"].astype(BF16)
        if w["even_w_in"].ndim == 2:
            w["even_w_in"] = jnp.transpose(w["even_w_in"].reshape(D_MODEL, 4, EVEN_IN // 4), (1, 0, 2))
    if "rg_gate_w" in full:
        gw = jnp.transpose(full["rg_gate_w"].astype(BF16), (0, 2, 3, 1, 4))
        w["rg_gate_w"] = gw.reshape(2, RG_HEADS, RG_HEAD_DIM, 2 * RG_HEAD_DIM)
        w["rg_gate_b"] = full["rg_gate_b"].reshape(2, 2, D_MODEL)
        w["rg_lambda"] = full["rg_lambda"].reshape(2, 1, D_MODEL)
    if "odd_w_in" in full:
        w_in = jnp.pad(full["odd_w_in"].astype(BF16), ((0, 0), (0, ODD_IN_PAD - ODD_IN)))
        w["odd_w_in"] = w_in.reshape(1, D_MODEL, ODD_IN_PAD)
    if "gla_w_gate_lr" in full:
        wg = full["gla_w_gate_lr"].astype(BF16)
        w["gla_wg_pad"] = jnp.stack([jnp.pad(wg[d], ((d * GLA_RANK, LANES - (d + 1) * GLA_RANK), (0, 0)))
                                     for d in range(2)])
        w["gla_b_gate"] = full["gla_b_gate"].reshape(2, 1, GLA_HEADS * GLA_DK)
    return w


SHARDED_SMALL = (("rg_conv_w", (4, 256)), ("rg_lambda", (2, 256)), ("sc_conv_w", (3, 256)),
                 ("odd_norm_pre", (256,)), ("odd_norm_post", (256,)), ("gla_w_gate_lr", (2, 16, 128)),
                 ("gla_b_gate", (2, 128)), ("gla_norm_g", (64,)))
SHARDED_ROWS = 96
REPLICATED = (("rg_gate_w", (2, 2, 8, 128, 128)), ("even_norm_post", (1024,)), ("rg_conv_b", (1024,)),
              ("rg_gate_b", (2, 2, 8, 128)))
GATE_ROWS = 4096
LAST_REPLICATED = (("even_norm_pre", (1024,)),)
LAST_ROWS = 8
REPLICATED_ROWS = 4160
REP_PART = REPLICATED_ROWS // 8
HALF_SHARDED = SHARDED_ROWS // 2
PACK_HALF = HALF_SHARDED + REP_PART


def _seg_rows(shape):
    n = 1
    for s in shape:
        n *= s
    return -(-n // (SUBLANES * LANES)) * SUBLANES


def _pack(arrays, spec, total_rows, lead=()):
    parts = []
    for name, shape in spec:
        flat = arrays[name].reshape(lead + (-1,))
        pad = _seg_rows(shape) * LANES - flat.shape[-1]
        if pad:
            flat = jnp.pad(flat, [(0, 0)] * len(lead) + [(0, pad)])
        parts.append(flat.reshape(lead + (-1, LANES)))
    rows = jnp.concatenate(parts, axis=len(lead))
    pad = total_rows - rows.shape[len(lead)]
    return jnp.pad(rows, [(0, 0)] * len(lead) + [(0, pad), (0, 0)])


def _unpack(rows, spec, lead=()):
    out, at = {}, 0
    for name, shape in spec:
        n = 1
        for s in shape:
            n *= s
        k = _seg_rows(shape)
        seg = lax.slice_in_dim(rows, at, at + k, axis=len(lead)).reshape(lead + (-1,))
        out[name] = lax.slice_in_dim(seg, 0, n, axis=len(lead)).reshape(lead + shape)
        at += k
    return out


def _split_owners(arr):
    a = arr.reshape(arr.shape[:-1] + (4, arr.shape[-1] // 4))
    return jnp.moveaxis(a, -2, 0)


def _merge_owners(arr):
    a = jnp.moveaxis(arr, 0, -2)
    return a.reshape(a.shape[:-2] + (-1,))


HBM_SPEC = pl.BlockSpec(memory_space=pltpu.HBM)


def _position():
    x, y, c = lax.axis_index("x"), lax.axis_index("y"), lax.axis_index("c")
    chips = [(1 - x, y), (x, 1 - y), (1 - x, 1 - y)]
    return x, y, c, chips


def _remote(src, dst, send_sem, recv_sem, device):
    return pltpu.make_async_remote_copy(src_ref=src, dst_ref=dst, send_sem=send_sem, recv_sem=recv_sem,
                                        device_id=device, device_id_type=MESH)


SEM_SPEC = pl.BlockSpec(memory_space=pltpu.SEMAPHORE)
SIDE_EFFECT = pltpu.SideEffectType.DATAFLOW_SIDE_EFFECTING


def _gather_copies(ins, lands, n_h, send_sems, recv_sems):
    x, y, c, chips = _position()
    me = 2 * x + y
    copies = []
    for a in range(len(ins)):
        for k, chip in enumerate(chips):
            src = ins[a].at[c] if a < n_h else ins[a]
            dst = lands[a].at[me, c] if a < n_h else lands[a].at[me]
            copies.append(_remote(src, dst, send_sems.at[3 * a + k], recv_sems.at[3 * a + k], (chip[0], chip[1], c)))
    return copies


def gather_start(halved, whole, name):
    arrays = list(halved) + list(whole)
    n, n_h = len(arrays), len(halved)
    lands = [lax.empty((4,) + a.shape, a.dtype) for a in arrays]

    def body(*refs):
        ins, lz, send_sems, recv_sems, token = refs[:n], refs[n:2 * n], refs[2 * n], refs[2 * n + 1], refs[-1]
        for cp in _gather_copies(ins, lz, n_h, send_sems, recv_sems):
            cp.start()
        token[...] = jnp.zeros_like(token)

    operands = [pltpu.with_memory_space_constraint(a, pltpu.HBM) for a in arrays + lands]
    return pl.pallas_call(
        body, name=name,
        out_shape=(pltpu.SemaphoreType.DMA((3 * n,)), pltpu.SemaphoreType.DMA((3 * n,)))
        + tuple(pltpu.HBM(a.shape, a.dtype) for a in operands) + (jax.ShapeDtypeStruct((SUBLANES, LANES), F32),),
        in_specs=[HBM_SPEC] * (2 * n),
        out_specs=(SEM_SPEC, SEM_SPEC) + (HBM_SPEC,) * (2 * n) + (pl.BlockSpec(memory_space=pltpu.VMEM),),
        input_output_aliases={i: 2 + i for i in range(2 * n)},
        compiler_params=pltpu.CompilerParams(has_side_effects=SIDE_EFFECT),
    )(*operands)


def gather_wait(started, n_h, after, name):
    send_sems, recv_sems = started[0], started[1]
    operands = list(started[2:-1])
    n = len(operands) // 2

    def body(*refs):
        ins, lz, send_ref, recv_ref = refs[:n], refs[n:2 * n], refs[2 * n], refs[2 * n + 1]
        for cp in _gather_copies(ins, lz, n_h, send_ref, recv_ref):
            cp.wait_send()
            cp.wait_recv()

    outs = pl.pallas_call(
        body, name=name,
        out_shape=tuple(pltpu.HBM(a.shape, a.dtype) for a in operands),
        in_specs=[HBM_SPEC] * (2 * n) + [SEM_SPEC, SEM_SPEC, pl.BlockSpec(memory_space=pl.ANY)],
        out_specs=(HBM_SPEC,) * (2 * n),
        input_output_aliases={i: i for i in range(2 * n)},
        compiler_params=pltpu.CompilerParams(has_side_effects=SIDE_EFFECT),
    )(*operands, send_sems, recv_sems, after)
    return outs[n:]


def pass_to_sibling(fulls, name):
    n = len(fulls)

    def body(*refs):
        bufs = refs[n:2 * n]
        send_sems, recv_sems = refs[2 * n:]
        x, y, c, chips = _position()
        sibling = (x, y, 1 - c)
        copies = []
        for a in range(n):
            for k, chip in enumerate(chips):
                q = 2 * chip[0] + chip[1]
                cp = _remote(bufs[a].at[q, c], bufs[a].at[q, c], send_sems.at[3 * a + k], recv_sems.at[3 * a + k],
                             sibling)
                cp.start()
                copies.append(cp)
        for a in range(n):
            for k, chip in enumerate(chips):
                q = 2 * chip[0] + chip[1]
                passed = bufs[a].at[q, 1 - c]
                _remote(passed, passed, send_sems.at[3 * a + k], recv_sems.at[3 * a + k], sibling).wait_recv()
        for cp in copies:
            cp.wait_send()

    return pl.pallas_call(
        body, name=name,
        out_shape=[jax.ShapeDtypeStruct(a.shape, a.dtype) for a in fulls],
        in_specs=[HBM_SPEC] * n, out_specs=[HBM_SPEC] * n,
        input_output_aliases={i: i for i in range(n)},
        scratch_shapes=[pltpu.SemaphoreType.DMA((3 * n,)), pltpu.SemaphoreType.DMA((3 * n,))],
    )(*fulls)


def place_own(full, own, chip, name):
    _, _, r, cols = full.shape
    tr = _row_tile(r, cols)

    def body(p_ref, own_ref, full_ref, o_ref):
        o_ref[0] = own_ref[...]

    return pl.pallas_call(
        body, name=name,
        out_shape=jax.ShapeDtypeStruct(full.shape, full.dtype),
        grid_spec=pltpu.PrefetchScalarGridSpec(
            num_scalar_prefetch=1, grid=(2, r // tr),
            in_specs=[pl.BlockSpec((1, tr, cols), lambda h, i, p_ref: (h, i, 0)), pl.BlockSpec(memory_space=pl.ANY)],
            out_specs=pl.BlockSpec((1, 1, tr, cols), lambda h, i, p_ref: (p_ref[0], h, i, 0))),
        input_output_aliases={2: 0},
        compiler_params=_params("parallel", "parallel"),
    )(chip, own, full)


def exchange_with_sibling(arrays, name):
    n = len(arrays)

    def body(*refs):
        ins, outs = refs[:n], refs[n:2 * n]
        send_sems, recv_sems = refs[2 * n:]
        x, y, c, _ = _position()
        copies = []
        for a in range(n):
            cp = _remote(ins[a].at[:, 1 - c], outs[a], send_sems.at[a], recv_sems.at[a], (x, y, 1 - c))
            cp.start()
            copies.append(cp)
        for cp in copies:
            cp.wait()

    return pl.pallas_call(
        body, name=name,
        out_shape=[jax.ShapeDtypeStruct((a.shape[0],) + a.shape[2:], a.dtype) for a in arrays],
        in_specs=[HBM_SPEC] * n, out_specs=[HBM_SPEC] * n,
        scratch_shapes=[pltpu.SemaphoreType.DMA((n,)), pltpu.SemaphoreType.DMA((n,))],
    )(*arrays)


def _chip_copies(ins, lands, send_sems, recv_sems):
    x, y, c, chips = _position()
    copies = []
    for a in range(len(ins)):
        for k, chip in enumerate(chips):
            q = 2 * chip[0] + chip[1]
            copies.append(_remote(ins[a].at[q], lands[a].at[k], send_sems.at[3 * a + k], recv_sems.at[3 * a + k],
                                  (chip[0], chip[1], c)))
    return copies


def exchange_with_chips_start(arrays, name):
    n = len(arrays)
    lands = [lax.empty((3,) + a.shape[1:], a.dtype) for a in arrays]

    def body(*refs):
        ins, lz, send_sems, recv_sems, token = refs[:n], refs[n:2 * n], refs[2 * n], refs[2 * n + 1], refs[-1]
        for cp in _chip_copies(ins, lz, send_sems, recv_sems):
            cp.start()
        token[...] = jnp.zeros_like(token)

    operands = [pltpu.with_memory_space_constraint(a, pltpu.HBM) for a in list(arrays) + lands]
    return pl.pallas_call(
        body, name=name,
        out_shape=(pltpu.SemaphoreType.DMA((3 * n,)), pltpu.SemaphoreType.DMA((3 * n,)))
        + tuple(pltpu.HBM(a.shape, a.dtype) for a in operands) + (jax.ShapeDtypeStruct((SUBLANES, LANES), F32),),
        in_specs=[HBM_SPEC] * (2 * n),
        out_specs=(SEM_SPEC, SEM_SPEC) + (HBM_SPEC,) * (2 * n) + (pl.BlockSpec(memory_space=pltpu.VMEM),),
        input_output_aliases={i: 2 + i for i in range(2 * n)},
        compiler_params=pltpu.CompilerParams(has_side_effects=SIDE_EFFECT),
    )(*operands)


def exchange_with_chips_wait(started, after, name):
    send_sems, recv_sems = started[0], started[1]
    operands = list(started[2:-1])
    n = len(operands) // 2

    def body(*refs):
        ins, lz, send_ref, recv_ref = refs[:n], refs[n:2 * n], refs[2 * n], refs[2 * n + 1]
        for cp in _chip_copies(ins, lz, send_ref, recv_ref):
            cp.wait_send()
            cp.wait_recv()

    outs = pl.pallas_call(
        body, name=name,
        out_shape=tuple(pltpu.HBM(a.shape, a.dtype) for a in operands),
        in_specs=[HBM_SPEC] * (2 * n) + [SEM_SPEC, SEM_SPEC, pl.BlockSpec(memory_space=pl.ANY)],
        out_specs=(HBM_SPEC,) * (2 * n),
        input_output_aliases={i: i for i in range(2 * n)},
        compiler_params=pltpu.CompilerParams(has_side_effects=SIDE_EFFECT),
    )(*operands, send_sems, recv_sems, after)
    return outs[:n], outs[n:]


def share_totals(totals, pack_total, last_part):
    arrays = list(totals) + [pack_total]
    n = len(arrays)

    def body(*refs):
        ins, last, outs, rep, last_all = refs[:n], refs[n], refs[n + 1:2 * n + 1], refs[2 * n + 1], refs[2 * n + 2]
        send_sems, recv_sems, rep_send, rep_recv, last_send, last_recv = refs[2 * n + 3:]
        x, y, c, chips = _position()
        sibling = (x, y, 1 - c)
        me = 4 * x + 2 * y + c
        sends = []
        for a in range(n):
            cp = _remote(ins[a], outs[a], send_sems.at[a], recv_sems.at[a], sibling)
            cp.start()
            sends.append(cp)
        mine = ins[n - 1].at[pl.ds(HALF_SHARDED, REP_PART)]
        peers = [sibling]
        for chip in chips:
            peers += [(chip[0], chip[1], c), (chip[0], chip[1], 1 - c)]
        for j, peer in enumerate(peers):
            for src, dst, s_sem, r_sem in ((mine, rep, rep_send, rep_recv), (last, last_all, last_send, last_recv)):
                cp = _remote(src, dst.at[me], s_sem.at[j], r_sem.at[j], peer)
                cp.start()
                sends.append(cp)
        for a in range(n):
            _remote(outs[a], outs[a], send_sems.at[a], recv_sems.at[a], sibling).wait_recv()
        for j, peer in enumerate(peers):
            it = 4 * peer[0] + 2 * peer[1] + peer[2]
            _remote(rep.at[it], rep.at[it], rep_send.at[j], rep_recv.at[j], peer).wait_recv()
            _remote(last_all.at[it], last_all.at[it], last_send.at[j], last_recv.at[j], peer).wait_recv()
        for cp in sends:
            cp.wait_send()

    outs = pl.pallas_call(
        body, name="grad_share_totals",
        out_shape=[jax.ShapeDtypeStruct(a.shape, a.dtype) for a in arrays]
        + [jax.ShapeDtypeStruct((8, REP_PART, LANES), F32), jax.ShapeDtypeStruct((8,) + last_part.shape, F32)],
        in_specs=[HBM_SPEC] * (n + 1), out_specs=[HBM_SPEC] * (n + 2),
        scratch_shapes=[pltpu.SemaphoreType.DMA((n,)), pltpu.SemaphoreType.DMA((n,))]
        + [pltpu.SemaphoreType.DMA((7,))] * 4,
    )(*arrays, last_part)
    return outs[:n], outs[n], outs[n + 1]


def sum_parts(parts, name):
    def body(p_ref, o_ref):
        total = p_ref[0]
        for k in range(1, parts.shape[0]):
            total = total + p_ref[k]
        o_ref[...] = total

    return pl.pallas_call(body, name=name, out_shape=jax.ShapeDtypeStruct(parts.shape[1:], parts.dtype))(parts)


TILE_BYTES = 2 << 20


def _row_tile(rows, cols):
    best = None
    for t in range(SUBLANES, rows + 1, SUBLANES):
        if rows % t == 0 and t * cols * 4 <= TILE_BYTES:
            best = t
    return best if best is not None else rows


def add_sibling(mine, received, core, out_dtype, name):
    _, _, r, cols = mine.shape
    tr = _row_tile(r, cols)

    def body(c_ref, a_ref, b_ref, o_ref):
        o_ref[...] = (a_ref[0] + b_ref[...].astype(F32)).astype(out_dtype)

    return pl.pallas_call(
        body, name=name,
        out_shape=jax.ShapeDtypeStruct((4, r, cols), out_dtype),
        grid_spec=pltpu.PrefetchScalarGridSpec(
            num_scalar_prefetch=1, grid=(4, r // tr),
            in_specs=[pl.BlockSpec((1, 1, tr, cols), lambda o, i, c_ref: (o, c_ref[0], i, 0)),
                      pl.BlockSpec((1, tr, cols), lambda o, i, c_ref: (o, i, 0))],
            out_specs=pl.BlockSpec((1, tr, cols), lambda o, i, c_ref: (o, i, 0))),
        compiler_params=_params("parallel", "parallel"),
    )(core, mine, received)


def add_chips(own, received, chip, name):
    _, r, cols = own.shape
    tr = _row_tile(r, cols)

    def body(p_ref, a_ref, b0, b1, b2, o_ref):
        o_ref[...] = ((a_ref[0].astype(F32) + b0[0].astype(F32)) + b1[0].astype(F32)) + b2[0].astype(F32)

    rb = lambda k: pl.BlockSpec((1, tr, cols), lambda i, p_ref: (k, i, 0))
    return pl.pallas_call(
        body, name=name,
        out_shape=jax.ShapeDtypeStruct((r, cols), F32),
        grid_spec=pltpu.PrefetchScalarGridSpec(
            num_scalar_prefetch=1, grid=(r // tr,),
            in_specs=[pl.BlockSpec((1, tr, cols), lambda i, p_ref: (p_ref[0], i, 0)), rb(0), rb(1), rb(2)],
            out_specs=pl.BlockSpec((tr, cols), lambda i, p_ref: (i, 0))),
        compiler_params=_params("parallel"),
    )(chip, own, received, received, received)


def _adamw_update(gv, w_ref, m_ref, v_ref, d_ref, nm_ref, nv_ref):
    nm = ADAM_B1 * m_ref[...] + (1.0 - ADAM_B1) * gv
    nv = ADAM_B2 * v_ref[...] + (1.0 - ADAM_B2) * (gv * gv)
    nm_ref[...] = nm
    nv_ref[...] = nv
    m_hat = nm / (1.0 - ADAM_B1 ** ADAM_STEP)
    v_hat = nv / (1.0 - ADAM_B2 ** ADAM_STEP)
    d_ref[...] = -ADAM_LR * (m_hat / (jnp.sqrt(v_hat) + ADAM_EPS) + ADAM_WD * w_ref[...])


def adamw_halves(w, own, received, m, v, core, name, by_columns=False):
    rows, cols = w.shape

    def body(c_ref, w_ref, own_ref, rec_ref, m_ref, v_ref, g_ref, d_ref, nm_ref, nv_ref):
        gv = jnp.where(pl.program_id(0) == c_ref[0], own_ref[...], rec_ref[...])
        g_ref[...] = gv
        _adamw_update(gv, w_ref, m_ref, v_ref, d_ref, nm_ref, nv_ref)

    if by_columns:
        nr = 1
        whole = pl.BlockSpec((rows, cols // 2), lambda h, i, c_ref: (0, h))
        half = pl.BlockSpec((rows, cols // 2), lambda h, i, c_ref: (0, 0))
    else:
        r = rows // 2
        tr = _row_tile(r, cols)
        nr = r // tr
        whole = pl.BlockSpec((tr, cols), lambda h, i, c_ref: (h * nr + i, 0))
        half = pl.BlockSpec((tr, cols), lambda h, i, c_ref: (i, 0))
    return pl.pallas_call(
        body, name=name,
        out_shape=(jax.ShapeDtypeStruct((rows, cols), F32),) * 4,
        grid_spec=pltpu.PrefetchScalarGridSpec(
            num_scalar_prefetch=1, grid=(2, nr),
            in_specs=[whole, half, half, whole, whole], out_specs=(whole,) * 4),
        compiler_params=_params("parallel", "parallel"),
    )(core, w, own, received, m, v)


def adamw_many(ws, gs, ms, vs, name):
    n = len(ws)

    def body(*refs):
        ins, outs = refs[:4 * n], refs[4 * n:]
        for k in range(n):
            w_ref, g_ref, m_ref, v_ref = (ins[j * n + k] for j in range(4))
            d_ref, nm_ref, nv_ref = outs[3 * k:3 * k + 3]
            _adamw_update(g_ref[...], w_ref, m_ref, v_ref, d_ref, nm_ref, nv_ref)

    flat = pl.pallas_call(
        body, name=name,
        out_shape=[jax.ShapeDtypeStruct(w.shape, F32) for w in ws for _ in range(3)],
    )(*ws, *gs, *ms, *vs)
    return [tuple(flat[3 * k:3 * k + 3]) for k in range(n)]


def adamw(w, g, m, v, name):
    r, cols = w.shape
    tr = _row_tile(r, cols)

    def body(w_ref, g_ref, m_ref, v_ref, g_out, d_ref, nm_ref, nv_ref):
        gv = g_ref[...]
        g_out[...] = gv
        _adamw_update(gv, w_ref, m_ref, v_ref, d_ref, nm_ref, nv_ref)

    blk = pl.BlockSpec((tr, cols), lambda i: (i, 0))
    return pl.pallas_call(
        body, name=name,
        out_shape=(jax.ShapeDtypeStruct((r, cols), F32),) * 4,
        grid=(r // tr,),
        in_specs=[blk] * 4, out_specs=(blk,) * 4,
        compiler_params=_params("parallel"),
    )(w, g, m, v)


WEIGHTS = ("even_norm_pre", "even_norm_post", "even_w_in", "rg_conv_w", "rg_conv_b", "rg_gate_w", "rg_gate_b",
           "rg_lambda", "sc_conv_w", "even_w_out", "odd_norm_pre", "odd_norm_post", "odd_w_in", "gla_w_gate_lr",
           "gla_b_gate", "gla_norm_g", "odd_w_out")
BIG = ("even_w_in", "even_w_out", "odd_w_in", "odd_w_out")


def _halves(a):
    return a.reshape((2, a.shape[0] // 2) + a.shape[1:])


def kernel(x, even_norm_pre, even_norm_post, even_w_in, rg_conv_w, rg_conv_b, rg_gate_w, rg_gate_b, rg_lambda, sc_conv_w, even_w_out, odd_norm_pre, odd_norm_post, odd_w_in, gla_w_gate_lr, gla_b_gate, gla_norm_g, odd_w_out, loss_target, m_even_norm_pre, m_even_norm_post, m_even_w_in, m_rg_conv_w, m_rg_conv_b, m_rg_gate_w, m_rg_gate_b, m_rg_lambda, m_sc_conv_w, m_even_w_out, m_odd_norm_pre, m_odd_norm_post, m_odd_w_in, m_gla_w_gate_lr, m_gla_b_gate, m_gla_norm_g, m_odd_w_out, v_even_norm_pre, v_even_norm_post, v_even_w_in, v_rg_conv_w, v_rg_conv_b, v_rg_gate_w, v_rg_gate_b, v_rg_lambda, v_sc_conv_w, v_even_w_out, v_odd_norm_pre, v_odd_norm_post, v_odd_w_in, v_gla_w_gate_lr, v_gla_b_gate, v_gla_norm_g, v_odd_w_out):
    given = dict(locals())
    shard = {n: given[n][0] for n in WEIGHTS}
    m_in = {n: given["m_" + n][0] for n in WEIGHTS}
    v_in = {n: given["v_" + n][0] for n in WEIGHTS}
    mx, my, mc = lax.axis_index("x"), lax.axis_index("y"), lax.axis_index("c")
    core = jnp.reshape(mc, (1,)).astype(jnp.int32)
    chip = jnp.reshape(2 * mx + my, (1,)).astype(jnp.int32)

    small_shard = _pack(shard, SHARDED_SMALL, SHARDED_ROWS)
    big_own = [_halves(shard[n].astype(BF16)) for n in BIG]
    started_a = gather_start(big_own[:1], [small_shard], "gather_start_a")
    started_b = gather_start(big_own[1:], [], "gather_start_b")
    even_w_in_full, small_full = gather_wait(started_a, 1, started_b[-1], "gather_wait_a")
    (even_w_in_full,) = pass_to_sibling([even_w_in_full], "gather_pass_a")
    even_w_in_full = place_own(even_w_in_full, big_own[0], chip, "place_even_w_in")
    small_full = lax.dynamic_update_slice(small_full, small_shard[None], (chip[0], 0, 0))
    full = {n: shard[n] for n, _ in REPLICATED + LAST_REPLICATED}
    full.update({n: _merge_owners(a) for n, a in _unpack(small_full, SHARDED_SMALL, lead=(4,)).items()})
    full["even_w_in"] = even_w_in_full.reshape(4, D_MODEL, EVEN_IN // 4)

    def late_weights(after):
        lands = pass_to_sibling(list(gather_wait(started_b, 3, after, "gather_wait_b")), "gather_pass_b")
        lands = [place_own(a, b, chip, "place_" + n) for a, b, n in zip(lands, big_own[1:], BIG[1:])]
        odd_w_in = jnp.transpose(lands[1].reshape(4, D_MODEL, ODD_IN // 4), (1, 0, 2)).reshape(D_MODEL, ODD_IN)
        return _prepare_weights({"even_w_out": lands[0].reshape(2 * D_MODEL, D_MODEL), "odd_w_in": odd_w_in,
                                 "odd_w_out": lands[2].reshape(D_MODEL, D_MODEL)})

    pending = {}

    def slab(a):
        return a.reshape((4, 2, a.shape[1] // 2) + a.shape[2:])

    def begin(tag, slabs, to_send, dtypes):
        got = exchange_with_sibling(to_send, "grad_sibling_" + tag)
        sums = [add_sibling(a, b, core, dt, "grad_add_sibling_%s%d" % (tag, i))
                for i, (a, b, dt) in enumerate(zip(slabs, got, dtypes))]
        pending[tag] = exchange_with_chips_start(sums, "grad_chips_start_" + tag)
        return pending[tag][-1][0, 0]

    def finish(tag, after):
        sums, got = exchange_with_chips_wait(pending[tag], after, "grad_chips_wait_" + tag)
        return [add_chips(a, b, chip, "grad_add_chips_%s%d" % (tag, i)) for i, (a, b) in enumerate(zip(sums, got))]

    def reduce_first(g, g16):
        odd_w_in = slab(jnp.transpose(g["odd_w_in"].reshape(D_MODEL, 4, ODD_IN // 4), (1, 0, 2)))
        slabs = [odd_w_in] + [slab(g[n].reshape(4, -1, D_MODEL)) for n in ("odd_w_out", "even_w_out")]
        to_send = [odd_w_in.astype(BF16)] + [slab(g16[n].reshape(4, -1, D_MODEL)) for n in ("odd_w_out", "even_w_out")]
        return begin("a", slabs, to_send, [BF16] * 3)

    def reduce_second(g, g16):
        pending["totals_a"] = finish("a", g["even_w_in"])
        rep_rows = _pack(g, REPLICATED, REPLICATED_ROWS).reshape(4, 2, REP_PART, LANES)
        sh_rows = _pack({n: _split_owners(g[n]) for n, _ in SHARDED_SMALL}, SHARDED_SMALL, SHARDED_ROWS, lead=(4,))
        pack = jnp.concatenate([sh_rows.reshape(4, 2, HALF_SHARDED, LANES), rep_rows], axis=2)
        return begin("b", [slab(g["even_w_in"]), pack], [slab(g16["even_w_in"]), pack], [BF16, F32])

    loss, grad_x, g = local_step(x[0], loss_target[0], _prepare_weights(full), reduce_first, reduce_second,
                                 late_weights)
    odd_w_in_t, odd_w_out_t, even_w_out_t = pending["totals_a"]
    even_w_in_t, pack_t = finish("b", grad_x)
    totals = [even_w_in_t, even_w_out_t, odd_w_in_t, odd_w_out_t]
    last_part = jnp.concatenate([_pack(g, LAST_REPLICATED, LAST_ROWS), loss])
    from_core, rep_all, last_all = share_totals(totals, pack_t, last_part)
    me = 2 * chip[0] + core[0]
    mine, theirs = pack_t[:HALF_SHARDED], from_core[4][:HALF_SHARDED]
    sh_total = jnp.where(mc == 0, jnp.concatenate([mine, theirs]), jnp.concatenate([theirs, mine]))
    rep_all = lax.dynamic_update_slice(rep_all, pack_t[None, HALF_SHARDED:], (me, 0, 0))
    rep_total = rep_all.reshape(REPLICATED_ROWS, LANES)
    last_total = sum_parts(lax.dynamic_update_slice(last_all, last_part[None], (me, 0, 0)), "grad_sum_last")
    last_total, loss = last_total[:LAST_ROWS], last_total[LAST_ROWS, 0]
    grads = {}

    delta, new_m, new_v = {}, {}, {}
    for i, n in enumerate(BIG):
        if shard[n].shape[1] % LANES:
            outs = adamw_halves(shard[n].T, totals[i].T, from_core[i].T, m_in[n].T, v_in[n].T, core, "adamw_" + n,
                                by_columns=True)
            grads[n], delta[n], new_m[n], new_v[n] = [o.T for o in outs]
        else:
            grads[n], delta[n], new_m[n], new_v[n] = adamw_halves(shard[n], totals[i], from_core[i], m_in[n],
                                                                  v_in[n], core, "adamw_" + n)
    gate = [src["rg_gate_w"].reshape(GATE_ROWS, LANES) for src in (shard, m_in, v_in)]
    grads["rg_gate_w"], delta["rg_gate_w"], new_m["rg_gate_w"], new_v["rg_gate_w"] = adamw(
        gate[0], rep_total, gate[1], gate[2], "adamw_rg_gate_w")
    rest = REPLICATED[1:]
    rest_rows = sum(_seg_rows(shape) for _, shape in rest)
    grads.update(_unpack(sh_total, SHARDED_SMALL))
    grads.update(_unpack(rep_total[GATE_ROWS:GATE_ROWS + rest_rows], rest))
    grads.update(_unpack(last_total, LAST_REPLICATED))
    names = [n for n, _ in SHARDED_SMALL + rest + LAST_REPLICATED]
    rows_of = lambda a, n: a.reshape(-1, given[n].shape[-1])
    outs = adamw_many([rows_of(given[n], n) for n in names], [rows_of(grads[n], n) for n in names],
                      [rows_of(given["m_" + n], n) for n in names], [rows_of(given["v_" + n], n) for n in names],
                      "adamw_small")
    for n, (d, nm, nv) in zip(names, outs):
        delta[n], new_m[n], new_v[n] = d, nm, nv
    result = [loss, grad_x[None]]
    for group in (grads, delta, new_m, new_v):
        result += [group[n].reshape(given[n].shape) for n in WEIGHTS]
    return tuple(result)
```

```python
import jax
import jax.numpy as jnp
from jax import lax
from jax.experimental import pallas as pl
from jax.experimental.pallas import tpu as pltpu

F32 = jnp.float32
BF16 = jnp.bfloat16
MESH = pl.DeviceIdType.MESH

D_MODEL = 1024
NORM_EPS = 1e-6
RG_HEADS = 8
RG_HEAD_DIM = 128
RG_C = 8.0
EVEN_IN = 6144
ODD_IN = 3104
ODD_IN_PAD = 3200
GLA_HEADS = 4
GLA_DK = 128
GLA_DV = 256
GLA_RANK = 16
GLA_NORMALIZER = 16.0
GLA_CHUNK = 128
LR_COL = 3072

ADAM_LR = 0.001
ADAM_B1 = 0.9
ADAM_B2 = 0.999
ADAM_EPS = 1e-08
ADAM_WD = 0.01
ADAM_STEP = 10

SUBLANES = 8
HALO = 16
LANES = 128
VMEM_LIMIT = 56 * 2 ** 20

ROW_TILE = 512
SCAN_TILE = 256
GLA_BLOCK = 2048
MIX_TILE = 128


def _params(*sem):
    return pltpu.CompilerParams(dimension_semantics=sem, vmem_limit_bytes=VMEM_LIMIT)


def _full(shape):
    n = len(shape)
    return pl.BlockSpec(shape, lambda *_: (0,) * n)


def _sigmoid(x):
    return 0.5 + 0.5 * jnp.tanh(0.5 * x)


def _softplus(x):
    return jnp.maximum(x, 0.0) + jnp.log(1.0 + jnp.exp(-jnp.abs(x)))


def _dot(a, b):
    return jnp.dot(a, b, preferred_element_type=F32)


def _dot_nt(a, b):
    return lax.dot_general(a, b, (((1,), (1,)), ((), ())), preferred_element_type=F32)


def _dot_tn(a, b):
    return lax.dot_general(a, b, (((0,), (0,)), ((), ())), preferred_element_type=F32)


def _bdot(a, b, ca, cb):
    return lax.dot_general(a, b, (((ca,), (cb,)), ((0,), (0,))), preferred_element_type=F32)


def _halo_specs(rows, cols, col_block, n_row_tiles, tix):
    per = rows // HALO
    last = n_row_tiles * per - 1

    def split(args):
        if len(args) == 2:
            return tix(args[1]), col_block + args[0]
        return tix(args[0]), col_block

    def prev(*args):
        t, c = split(args)
        return (jnp.maximum(t * per - 1, 0), c)

    def main(*args):
        return split(args)

    def nxt(*args):
        t, c = split(args)
        return (jnp.minimum((t + 1) * per, last), c)

    return [pl.BlockSpec((HALO, cols), prev), pl.BlockSpec((rows, cols), main),
            pl.BlockSpec((HALO, cols), nxt)]


def _extend(prev_ref, main_ref, next_ref, is_first, is_last):
    p = jnp.where(is_first, 0.0, prev_ref[...].astype(F32))
    n = jnp.where(is_last, 0.0, next_ref[...].astype(F32))
    return jnp.concatenate([p, main_ref[...].astype(F32), n], axis=0)


def _shifted(ext, offset, rows):
    if offset == 0:
        return ext[HALO:HALO + rows]
    n = ext.shape[0]
    return pltpu.roll(ext, (-offset) % n, 0)[HALO:HALO + rows]


def _conv(ext, w, left, rows):
    out = None
    for k in range(w.shape[0]):
        term = _shifted(ext, k - left, rows) * w[k:k + 1]
        out = term if out is None else out + term
    return out


def _conv_transpose(ext, w, left, rows):
    out = None
    for k in range(w.shape[0]):
        term = _shifted(ext, left - k, rows) * w[k:k + 1]
        out = term if out is None else out + term
    return out


def _colsum(x):
    return jnp.sum(x, axis=0, keepdims=True)


def _accumulate(ref, value, step):
    @pl.when(step == 0)
    def _():
        ref[...] = value

    @pl.when(step > 0)
    def _():
        ref[...] += value


PROJ_TILE_BYTES = 7 * 2 ** 20


def _proj_row_tile(rows, width, dtype):
    tm = min(ROW_TILE, rows)
    while tm * width * jnp.dtype(dtype).itemsize > PROJ_TILE_BYTES and tm % (2 * HALO) == 0:
        tm //= 2
    return tm


def norm_matmul(x, gain, w, out_dtype, name):
    rows, d = x.shape
    n_col_tiles, _, tn = w.shape
    tm = _proj_row_tile(rows, n_col_tiles * tn, out_dtype)

    def body(x_ref, g_ref, w_ref, proj_ref, h_ref):
        xv = x_ref[...]
        rstd = lax.rsqrt(jnp.mean(xv * xv, axis=-1, keepdims=True) + NORM_EPS)
        hv = (xv * rstd * g_ref[...]).astype(BF16)
        h_ref[...] = hv
        for j in range(n_col_tiles):
            proj_ref[:, j * tn:(j + 1) * tn] = _dot(hv, w_ref[j]).astype(out_dtype)

    row = lambda cols: pl.BlockSpec((tm, cols), lambda i: (i, 0))
    return pl.pallas_call(
        body, name=name,
        out_shape=(jax.ShapeDtypeStruct((rows, n_col_tiles * tn), out_dtype), jax.ShapeDtypeStruct((rows, d), BF16)),
        grid=(rows // tm,),
        in_specs=[row(d), _full((1, d)), _full(w.shape)],
        out_specs=(row(n_col_tiles * tn), row(d)),
        compiler_params=_params("parallel"),
    )(x, gain, w)


def inproj_bwd(dproj, w, x, gain, dres, name):
    rows, d = x.shape
    n_col_tiles, _, tn = w.shape
    tm = _proj_row_tile(rows, n_col_tiles * tn, dproj.dtype)

    def body(dp_ref, w_ref, x_ref, g_ref, dres_ref, dx_ref, dg_ref):
        dh = None
        for j in range(n_col_tiles):
            part = _dot_nt(dp_ref[:, j * tn:(j + 1) * tn], w_ref[j])
            dh = part if dh is None else dh + part
        _inproj_finish(dh, x_ref, g_ref, dres_ref, dx_ref, dg_ref, pl.program_id(0))

    row = lambda cols: pl.BlockSpec((tm, cols), lambda i: (i, 0))
    return pl.pallas_call(
        body, name=name,
        out_shape=(jax.ShapeDtypeStruct((rows, d), F32), jax.ShapeDtypeStruct((1, d), F32)),
        grid=(rows // tm,),
        in_specs=[row(n_col_tiles * tn), _full(w.shape), row(d), _full((1, d)), row(d)],
        out_specs=(row(d), _full((1, d))),
        compiler_params=_params("arbitrary"),
    )(dproj, w, x, gain, dres)


def _inproj_finish(dh, x_ref, g_ref, dres_ref, dx_ref, dg_ref, step):
    xv = x_ref[...]
    rstd = lax.rsqrt(jnp.mean(xv * xv, axis=-1, keepdims=True) + NORM_EPS)
    xhat = xv * rstd
    dxn = dh * g_ref[...]
    dx_ref[...] = dres_ref[...] + rstd * (dxn - xhat * jnp.mean(dxn * xhat, axis=-1, keepdims=True))
    _accumulate(dg_ref, _colsum(dh * xhat), step)


def inproj_bwd_pieces(pieces, w, x, gain, dres, name):
    rows, d = x.shape
    tm = min(ROW_TILE, rows)
    n = len(pieces)
    widths = [p.shape[1] for p in pieces]
    starts = [sum(widths[:k]) for k in range(n)]
    assert sum(widths) == w.shape[2]

    def body(*refs):
        w_ref, x_ref, g_ref, dres_ref, dx_ref, dg_ref = refs[n:]
        dh = None
        for k in range(n):
            part = _dot_nt(refs[k][...], w_ref[0, :, starts[k]:starts[k] + widths[k]])
            dh = part if dh is None else dh + part
        _inproj_finish(dh, x_ref, g_ref, dres_ref, dx_ref, dg_ref, pl.program_id(0))

    row = lambda cols: pl.BlockSpec((tm, cols), lambda i: (i, 0))
    return pl.pallas_call(
        body, name=name,
        out_shape=(jax.ShapeDtypeStruct((rows, d), F32), jax.ShapeDtypeStruct((1, d), F32)),
        grid=(rows // tm,),
        in_specs=[row(wd) for wd in widths] + [_full(w.shape), row(d), _full((1, d)), row(d)],
        out_specs=(row(d), _full((1, d))),
        compiler_params=_params("arbitrary"),
    )(*pieces, w, x, gain, dres)


def matmul_dw_pieces(a, pieces, name):
    rows, m = a.shape
    tk = min(2 * ROW_TILE, rows)
    n = len(pieces)

    def body(*refs):
        a_ref, ins, outs = refs[0], refs[1:1 + n], refs[1 + n:]
        av = a_ref[...]
        for k in range(n):
            _accumulate(outs[k], _dot_tn(av, ins[k][...]), pl.program_id(0))

    return pl.pallas_call(
        body, name=name,
        out_shape=[jax.ShapeDtypeStruct((m, p.shape[1]), F32) for p in pieces],
        grid=(rows // tk,),
        in_specs=[pl.BlockSpec((tk, m), lambda k: (k, 0))]
        + [pl.BlockSpec((tk, p.shape[1]), lambda k: (k, 0)) for p in pieces],
        out_specs=[_full((m, p.shape[1])) for p in pieces],
        compiler_params=_params("arbitrary"),
    )(a, *pieces)


def matmul_dw(a, b, bn, name):
    rows, m = a.shape
    n = b.shape[1]
    tk = min((4 if n > bn else 2) * ROW_TILE, rows)
    steps = rows // tk

    def body(a_ref, b_ref, o_ref, o16_ref):
        part = _dot_tn(a_ref[...], b_ref[...])

        @pl.when(pl.program_id(1) == 0)
        def _():
            o_ref[0] = part

        @pl.when(pl.program_id(1) > 0)
        def _():
            o_ref[0] += part

        @pl.when(pl.program_id(1) == steps - 1)
        def _():
            o16_ref[0] = o_ref[0].astype(BF16)

    out = pl.BlockSpec((1, m, bn), lambda j, k: (j, 0, 0))
    return pl.pallas_call(
        body, name=name,
        out_shape=(jax.ShapeDtypeStruct((n // bn, m, bn), F32), jax.ShapeDtypeStruct((n // bn, m, bn), BF16)),
        grid=(n // bn, steps),
        in_specs=[pl.BlockSpec((tk, m), lambda j, k: (k, 0)), pl.BlockSpec((tk, bn), lambda j, k: (k, j))],
        out_specs=(out, out),
        compiler_params=_params("parallel", "arbitrary"),
    )(a, b)


def _scan(a, b, carry, reverse):
    n, c = a.shape
    blocks = n // SUBLANES
    a = a.reshape(blocks, SUBLANES, c)
    b = b.reshape(blocks, SUBLANES, c)
    pos = lax.broadcasted_iota(jnp.int32, (1, SUBLANES, c), 1)
    s = 1
    while s < SUBLANES:
        shift, valid = (SUBLANES - s, pos < SUBLANES - s) if reverse else (s, pos >= s)
        a_s, b_s = pltpu.roll(a, shift, 1), pltpu.roll(b, shift, 1)
        b = jnp.where(valid, a * b_s + b, b)
        a = jnp.where(valid, a * a_s, a)
        s *= 2
    out = [None] * blocks
    for k in (range(blocks - 1, -1, -1) if reverse else range(blocks)):
        h = a[k] * carry + b[k]
        out[k] = h
        carry = h[0:1] if reverse else h[SUBLANES - 1:SUBLANES]
    return jnp.concatenate(out, axis=0)


def _rg_gates(ua, gw_ref, gb, lam):
    ub = ua.astype(BF16)
    pre_r, pre_i = [], []
    for h in range(RG_HEADS):
        z = _dot(ub[:, h * RG_HEAD_DIM:(h + 1) * RG_HEAD_DIM], gw_ref[h])
        pre_r.append(z[:, :RG_HEAD_DIM])
        pre_i.append(z[:, RG_HEAD_DIM:])
    r = _sigmoid(jnp.concatenate(pre_r, axis=1) + gb[0:1])
    i = _sigmoid(jnp.concatenate(pre_i, axis=1) + gb[1:2])
    sp = _softplus(-lam)
    log_a = -RG_C * r * sp
    a = jnp.exp(log_a)
    mult = jnp.sqrt(1.0 - a * a)
    return r, i, sp, a, mult


def _rg_weight_specs():
    return [_full((4, D_MODEL)), _full((1, D_MODEL)), _full((RG_HEADS, RG_HEAD_DIM, 2 * RG_HEAD_DIM)),
            _full((2, D_MODEL)), _full((1, D_MODEL))]


def rglru_fwd(proj, conv_w, conv_b, gate_w, gate_b, lam, reverse, name):
    rows_total = proj.shape[0]
    rows = min(SCAN_TILE, rows_total)
    n_tiles = rows_total // rows
    tix = (lambda i: n_tiles - 1 - i) if reverse else (lambda i: i)

    def body(xp, xm, xn, cw_ref, cb_ref, gw_ref, gb_ref, lam_ref, h_ref, acts_ref, carry):
        i = pl.program_id(0)
        t = tix(i)
        ext = _extend(xp, xm, xn, t == 0, t == n_tiles - 1)
        ua = _conv(ext, cw_ref[...], 2, rows) + cb_ref[...]
        r, gi, _, a, mult = _rg_gates(ua, gw_ref, gb_ref[...], lam_ref[...])
        for k, saved in enumerate((ua, r, gi, a, mult)):
            acts_ref[k] = saved
        b = mult * (gi * ua)

        @pl.when(i == 0)
        def _():
            carry[...] = jnp.zeros_like(carry)

        h = _scan(a, b, carry[0:1], reverse)
        h_ref[...] = h
        edge = h[0:1] if reverse else h[rows - 1:rows]
        carry[...] = jnp.broadcast_to(edge, carry.shape)

    return pl.pallas_call(
        body, name=name,
        out_shape=(jax.ShapeDtypeStruct((rows_total, D_MODEL), F32),
                   jax.ShapeDtypeStruct((5, rows_total, D_MODEL), F32)),
        grid=(n_tiles,),
        in_specs=_halo_specs(rows, D_MODEL, 0, n_tiles, tix) + _rg_weight_specs(),
        out_specs=(pl.BlockSpec((rows, D_MODEL), lambda i: (tix(i), 0)),
                   pl.BlockSpec((5, rows, D_MODEL), lambda i: (0, tix(i), 0))),
        scratch_shapes=[pltpu.VMEM((SUBLANES, D_MODEL), F32)],
        compiler_params=_params("arbitrary"),
    )(proj, proj, proj, conv_w, conv_b, gate_w, gate_b, lam)


def rglru_bwd(proj, dycat, h_dir, acts, gate_w, lam, add_dua, reverse, name):
    rows_total = proj.shape[0]
    rows = min(SCAN_TILE, rows_total)
    n_tiles = rows_total // rows
    tix = (lambda i: i) if reverse else (lambda i: n_tiles - 1 - i)
    za_block = 1

    def body(acts_ref, za_ref, dya_ref, hp, hm, hn, gw_ref, lam_ref, *rest):
        other = rest[0][...] if add_dua is not None else 0.0
        dua_ref, dgw_ref, dgb_ref, dlam_ref, carry = rest[-5:]
        step = pl.program_id(0)
        t = tix(step)
        first, last = t == 0, t == n_tiles - 1
        ua, r, gi, a, mult = (acts_ref[k] for k in range(5))
        lam_v = lam_ref[...]
        sp = _softplus(-lam_v)
        za = za_ref[...].astype(F32)
        dh = dya_ref[...] * (za * _sigmoid(za))

        @pl.when(step == 0)
        def _():
            carry[...] = jnp.zeros_like(carry)

        old = carry[0:1]
        mu = _scan(a, a * dh, old, not reverse)
        row = lax.broadcasted_iota(jnp.int32, mu.shape, 0)
        if reverse:
            mu_next = jnp.where(row == 0, old, pltpu.roll(mu, 1, 0))
            carry[...] = jnp.broadcast_to(mu[rows - 1:rows], carry.shape)
            h_ext = _extend(hp, hm, hn, first, last)
            h_prev = _shifted(h_ext, 1, rows)
        else:
            mu_next = jnp.where(row == rows - 1, old, pltpu.roll(mu, rows - 1, 0))
            carry[...] = jnp.broadcast_to(mu[0:1], carry.shape)
            h_ext = _extend(hp, hm, hn, first, last)
            h_prev = _shifted(h_ext, -1, rows)
        db = dh + mu_next
        da = db * h_prev
        d_mult = db * (gi * ua)
        di = db * (mult * ua)
        dua = db * (mult * gi)
        dlog_a = da * a - d_mult * (a * a) / mult
        dr = dlog_a * (-RG_C * sp)
        dlam = _colsum(dlog_a * (-RG_C * r)) * (-_sigmoid(-lam_v))
        dpr = dr * (r * (1.0 - r))
        dpi = di * (gi * (1.0 - gi))
        dgb = jnp.concatenate([_colsum(dpr), _colsum(dpi)], axis=0)
        ub = ua.astype(BF16)
        dua_heads, dgw_heads = [], []
        for h in range(RG_HEADS):
            cols = slice(h * RG_HEAD_DIM, (h + 1) * RG_HEAD_DIM)
            dz = jnp.concatenate([dpr[:, cols], dpi[:, cols]], axis=1).astype(BF16)
            dgw_heads.append(_dot_tn(ub[:, cols], dz))
            dua_heads.append(_dot_nt(dz, gw_ref[h]))
        dua_ref[...] = dua + jnp.concatenate(dua_heads, axis=1) + other

        @pl.when(step == 0)
        def _():
            for h in range(RG_HEADS):
                dgw_ref[h] = dgw_heads[h]
            dgb_ref[...] = dgb
            dlam_ref[...] = dlam

        @pl.when(step > 0)
        def _():
            for h in range(RG_HEADS):
                dgw_ref[h] += dgw_heads[h]
            dgb_ref[...] += dgb
            dlam_ref[...] += dlam

    row_spec = lambda col: pl.BlockSpec((rows, D_MODEL), lambda i: (tix(i), col))
    return pl.pallas_call(
        body, name=name,
        out_shape=(jax.ShapeDtypeStruct((rows_total, D_MODEL), F32),
                   jax.ShapeDtypeStruct((RG_HEADS, RG_HEAD_DIM, 2 * RG_HEAD_DIM), F32),
                   jax.ShapeDtypeStruct((2, D_MODEL), F32), jax.ShapeDtypeStruct((1, D_MODEL), F32)),
        grid=(n_tiles,),
        in_specs=([pl.BlockSpec((5, rows, D_MODEL), lambda i: (0, tix(i), 0)), row_spec(za_block), row_spec(0)]
                  + _halo_specs(rows, D_MODEL, 0, n_tiles, tix)
                  + [_full((RG_HEADS, RG_HEAD_DIM, 2 * RG_HEAD_DIM)), _full((1, D_MODEL))]
                  + ([] if add_dua is None else [row_spec(0)])),
        out_specs=(row_spec(0), _full((RG_HEADS, RG_HEAD_DIM, 2 * RG_HEAD_DIM)), _full((2, D_MODEL)),
                   _full((1, D_MODEL))),
        scratch_shapes=[pltpu.VMEM((SUBLANES, D_MODEL), F32)],
        compiler_params=_params("arbitrary"),
    )(acts, proj, dycat, h_dir, h_dir, h_dir, gate_w, lam, *([] if add_dua is None else [add_dua]))


def _extend_cols(refs, block, is_first, is_last):
    cols = slice(block * D_MODEL, (block + 1) * D_MODEL)
    prev_ref, main_ref, next_ref = refs
    p = jnp.where(is_first, 0.0, prev_ref[:, cols].astype(F32))
    n = jnp.where(is_last, 0.0, next_ref[:, cols].astype(F32))
    return jnp.concatenate([p, main_ref[:, cols].astype(F32), n], axis=0)


def even_mix_fwd(proj, h_f, h_b, sc_w, name):
    rows_total = proj.shape[0]
    rows = min(2 * MIX_TILE, rows_total)
    n_tiles = rows_total // rows
    ident = lambda i: i

    def body(za_ref, hf_ref, hb_ref, xbp, xbm, xbn, gcp, gcm, gcn, gb_ref, zb_ref, w_ref, y_ref):
        t = pl.program_id(0)
        first, last = t == 0, t == n_tiles - 1
        za = za_ref[...].astype(F32)
        y_ref[:, 0:D_MODEL] = ((hf_ref[...] + hb_ref[...]) * (za * _sigmoid(za))).astype(BF16)
        p_ext = _extend(xbp, xbm, xbn, first, last) * _extend(gcp, gcm, gcn, first, last)
        cv = _conv(p_ext, w_ref[...], 1, rows)
        zb = zb_ref[...].astype(F32)
        y_ref[:, D_MODEL:2 * D_MODEL] = (gb_ref[...].astype(F32) * cv * (zb * _sigmoid(zb))).astype(BF16)

    blk = lambda col: pl.BlockSpec((rows, D_MODEL), lambda i: (i, col))
    return pl.pallas_call(
        body, name=name,
        out_shape=jax.ShapeDtypeStruct((rows_total, 2 * D_MODEL), BF16),
        grid=(n_tiles,),
        in_specs=([blk(1), blk(0), blk(0)] + _halo_specs(rows, D_MODEL, 2, n_tiles, ident)
                  + _halo_specs(rows, D_MODEL, 4, n_tiles, ident) + [blk(3), blk(5), _full((3, D_MODEL))]),
        out_specs=pl.BlockSpec((rows, 2 * D_MODEL), lambda i: (i, 0)),
        compiler_params=_params("parallel"),
    )(proj, h_f, h_b, proj, proj, proj, proj, proj, proj, proj, proj, sc_w)


def even_mix_bwd(proj, dycat, h_f, h_b, dua, conv_w, sc_w, name):
    rows_total, width = proj.shape
    rows = min(MIX_TILE, rows_total)
    n_tiles = rows_total // rows
    ident = lambda i: i

    def body(pp, pm, pn, dyp, dym, dyn, hf_ref, hb_ref, dup, dum, dun, cw_ref, sw_ref,
             dp_ref, dcw_ref, dcb_ref, dsw_ref):
        def put(k, value):
            dp_ref[:, k * D_MODEL:(k + 1) * D_MODEL] = value.astype(BF16)

        t = pl.program_id(0)
        first, last = t == 0, t == n_tiles - 1
        proj_ext = lambda k: _extend_cols((pp, pm, pn), k, first, last)
        mid = slice(HALO, HALO + rows)
        za = pm[:, D_MODEL:2 * D_MODEL].astype(F32)
        sa = _sigmoid(za)
        put(1, dym[:, 0:D_MODEL] * (hf_ref[...] + hb_ref[...]) * (sa * (1.0 + za * (1.0 - sa))))
        dua_ext = _extend(dup, dum, dun, first, last)
        cw = cw_ref[...]
        put(0, _conv_transpose(dua_ext, cw, 2, rows))
        dua_mid = dua_ext[mid]
        xa_ext = proj_ext(0)
        dcw = jnp.concatenate([_colsum(dua_mid * _shifted(xa_ext, k - 2, rows)) for k in range(4)], axis=0)
        dcb = _colsum(dua_mid)
        xb_ext, gb_ext, gc_ext, zb_ext = proj_ext(2), proj_ext(3), proj_ext(4), proj_ext(5)
        p_ext = xb_ext * gc_ext
        sb_ext = _sigmoid(zb_ext)
        dyb_ext = _extend_cols((dyp, dym, dyn), 1, first, last)
        dcv_ext = dyb_ext * gb_ext * (zb_ext * sb_ext)
        sw = sw_ref[...]
        p_at = [_shifted(p_ext, k - 1, rows) for k in range(3)]
        cv = (p_at[0] * sw[0:1] + p_at[1] * sw[1:2]) + p_at[2] * sw[2:3]
        zb, sb, dyb, gb = zb_ext[mid], sb_ext[mid], dyb_ext[mid], gb_ext[mid]
        put(3, dyb * cv * (zb * sb))
        put(5, dyb * gb * cv * (sb * (1.0 + zb * (1.0 - sb))))
        dp = _conv_transpose(dcv_ext, sw, 1, rows)
        put(4, dp * xb_ext[mid])
        put(2, dp * gc_ext[mid])
        dcv = dcv_ext[mid]
        dsw = jnp.concatenate([_colsum(dcv * p_at[k]) for k in range(3)], axis=0)
        _accumulate(dcw_ref, dcw, t)
        _accumulate(dcb_ref, dcb, t)
        _accumulate(dsw_ref, dsw, t)

    own = pl.BlockSpec((rows, D_MODEL), lambda i: (i, 0))
    return pl.pallas_call(
        body, name=name,
        out_shape=(jax.ShapeDtypeStruct((rows_total, 6 * D_MODEL), BF16),
                   jax.ShapeDtypeStruct((4, D_MODEL), F32), jax.ShapeDtypeStruct((1, D_MODEL), F32),
                   jax.ShapeDtypeStruct((3, D_MODEL), F32)),
        grid=(n_tiles,),
        in_specs=(_halo_specs(rows, width, 0, n_tiles, ident) + _halo_specs(rows, 2 * D_MODEL, 0, n_tiles, ident)
                  + [own, own] + _halo_specs(rows, D_MODEL, 0, n_tiles, ident)
                  + [_full((4, D_MODEL)), _full((3, D_MODEL))]),
        out_specs=(pl.BlockSpec((rows, 6 * D_MODEL), lambda i: (i, 0)), _full((4, D_MODEL)), _full((1, D_MODEL)),
                   _full((3, D_MODEL))),
        compiler_params=_params("arbitrary"),
    )(proj, proj, proj, dycat, dycat, dycat, h_f, h_b, dua, dua, dua, conv_w, sc_w)


def even_out_fwd(ycat, w_out, gain, x, name):
    rows, d = x.shape
    k = ycat.shape[1]
    tm = min(ROW_TILE, rows)

    def body(yc_ref, w_ref, g_ref, x_ref, x1_ref, y_ref):
        y = _dot(yc_ref[...], w_ref[...])
        y_ref[...] = y
        rstd = lax.rsqrt(jnp.mean(y * y, axis=-1, keepdims=True) + NORM_EPS)
        x1_ref[...] = x_ref[...] + y * rstd * g_ref[...]

    row = lambda n: pl.BlockSpec((tm, n), lambda i: (i, 0))
    return pl.pallas_call(
        body, name=name,
        out_shape=(jax.ShapeDtypeStruct((rows, d), F32),) * 2,
        grid=(rows // tm,),
        in_specs=[row(k), _full((k, d)), _full((1, d)), row(d)],
        out_specs=(row(d), row(d)),
        compiler_params=_params("parallel"),
    )(ycat, w_out, gain, x)


def _rmsnorm_bwd(dout, y, gain):
    rstd = lax.rsqrt(jnp.mean(y * y, axis=-1, keepdims=True) + NORM_EPS)
    yhat = y * rstd
    dyn = dout * gain
    dy = rstd * (dyn - yhat * jnp.mean(dyn * yhat, axis=-1, keepdims=True))
    return dy, dout * yhat


def even_out_bwd(dx1, y, gain, w_out, name):
    rows, d = y.shape
    k = w_out.shape[0]
    tm = min(ROW_TILE, rows)

    def body(dx_ref, y_ref, g_ref, w_ref, dy_ref, dyc_ref, dg_ref):
        dy, dg_rows = _rmsnorm_bwd(dx_ref[...], y_ref[...], g_ref[...])
        dyb = dy.astype(BF16)
        dy_ref[...] = dyb
        dyc_ref[...] = _dot_nt(dyb, w_ref[...])
        _accumulate(dg_ref, _colsum(dg_rows), pl.program_id(0))

    row = lambda n: pl.BlockSpec((tm, n), lambda i: (i, 0))
    return pl.pallas_call(
        body, name=name,
        out_shape=(jax.ShapeDtypeStruct((rows, d), BF16), jax.ShapeDtypeStruct((rows, k), F32),
                   jax.ShapeDtypeStruct((1, d), F32)),
        grid=(rows // tm,),
        in_specs=[row(d), row(d), _full((1, d)), _full((k, d))],
        out_specs=(row(d), row(k), _full((1, d))),
        compiler_params=_params("arbitrary"),
    )(dx1, y, gain, w_out)


def _chunk_cumsum(g, reverse):
    n, c = g.shape
    chunks, per = n // GLA_CHUNK, GLA_CHUNK // SUBLANES
    g = g.reshape(n // SUBLANES, SUBLANES, c)
    pos = lax.broadcasted_iota(jnp.int32, (1, SUBLANES, c), 1)
    s = 1
    while s < SUBLANES:
        if reverse:
            g = g + jnp.where(pos < SUBLANES - s, pltpu.roll(g, SUBLANES - s, 1), 0.0)
        else:
            g = g + jnp.where(pos >= s, pltpu.roll(g, s, 1), 0.0)
        s *= 2
    g = g.reshape(chunks, per, SUBLANES, c)
    out, carry = [None] * per, None
    for k in (range(per - 1, -1, -1) if reverse else range(per)):
        out[k] = g[:, k] if carry is None else g[:, k] + carry
        carry = out[k][:, 0:1] if reverse else out[k][:, SUBLANES - 1:SUBLANES]
    return jnp.stack(out, axis=1).reshape(n, c)


def _gla_prepare(q_ref, k_ref, lr_ref, wg_ref, bg_ref, reverse, n_chunks):
    z = _dot(lr_ref[...].astype(BF16), wg_ref[0]) + bg_ref[0]
    g = -_softplus(-z) * (1.0 / GLA_NORMALIZER)
    bcum = _chunk_cumsum(g, reverse).reshape(n_chunks, GLA_CHUNK, GLA_DK)
    edge = 0 if reverse else GLA_CHUNK - 1
    btot = bcum[:, edge:edge + 1, :]
    e_pos = jnp.exp(bcum)
    e_neg = jnp.exp(-bcum)
    e_st = jnp.exp(btot - bcum)
    q3 = q_ref[...].reshape(n_chunks, GLA_CHUNK, GLA_DK)
    k3 = k_ref[...].reshape(n_chunks, GLA_CHUNK, GLA_DK)
    scale = GLA_DK ** -0.5
    q_in = q3 * scale * e_pos
    k_in = k3 * e_neg
    k_st = k3 * e_st
    dec = jnp.exp(btot)
    return z, q_in, k_in, k_st, dec, (scale * e_pos, e_neg, e_st)


def _gla_mask(reverse):
    i = lax.broadcasted_iota(jnp.int32, (GLA_CHUNK, GLA_CHUNK), 0)
    j = lax.broadcasted_iota(jnp.int32, (GLA_CHUNK, GLA_CHUNK), 1)
    return (j >= i) if reverse else (j <= i)


def _gla_specs(rows, n_blocks, reverse):
    tix = (lambda s: n_blocks - 1 - s) if reverse else (lambda s: s)
    d = 1 if reverse else 0
    lr_block = LR_COL // LANES
    specs = [pl.BlockSpec((rows, GLA_DK), lambda h, s: (tix(s), h)),
             pl.BlockSpec((rows, GLA_DK), lambda h, s: (tix(s), GLA_HEADS + h)),
             pl.BlockSpec((rows, GLA_DV), lambda h, s: (tix(s), GLA_HEADS + h)),
             pl.BlockSpec((rows, LANES), lambda h, s: (tix(s), lr_block)),
             pl.BlockSpec((1, LANES, GLA_DK), lambda h, s: (d, 0, h)),
             pl.BlockSpec((1, 1, GLA_DK), lambda h, s: (d, 0, h))]
    return specs, tix


def gla_fwd(proj, wg_pad, bg, add_o, reverse, name):
    rows_total = proj.shape[0]
    rows = min(GLA_BLOCK, rows_total)
    n_blocks = rows_total // rows
    n_chunks = rows // GLA_CHUNK
    specs, tix = _gla_specs(rows, n_blocks, reverse)

    def body(q_ref, k_ref, v_ref, lr_ref, wg_ref, bg_ref, *rest):
        o_ref, st_ref, state, kv_scr, dec_scr = rest[-5:]
        _, q_in, k_in, k_st, dec, _ = _gla_prepare(q_ref, k_ref, lr_ref, wg_ref, bg_ref, reverse, n_chunks)
        vb = v_ref[...].reshape(n_chunks, GLA_CHUNK, GLA_DV).astype(BF16)
        qb = q_in.astype(BF16)
        p = jnp.where(_gla_mask(reverse), _bdot(qb, k_in.astype(BF16), 2, 2), 0.0)
        o = _bdot(p.astype(BF16), vb, 2, 1)
        kv_scr[...] = _bdot(vb, k_st.astype(BF16), 1, 1)
        dec_scr[...] = jnp.broadcast_to(dec, dec_scr.shape)

        @pl.when(pl.program_id(1) == 0)
        def _():
            state[...] = jnp.zeros_like(state)

        for c in range(n_chunks):
            cc = n_chunks - 1 - c if reverse else c
            st_ref[0, cc] = state[...]
            state[...] = state[...] * dec_scr[cc, 0:1] + kv_scr[cc]
        o = o + _bdot(qb, st_ref[0].astype(BF16), 2, 2)
        o = o.reshape(rows, GLA_DV)
        o_ref[...] = o if add_o is None else o + rest[0][...]

    o_spec = pl.BlockSpec((rows, GLA_DV), lambda h, s: (tix(s), h))
    return pl.pallas_call(
        body, name=name,
        out_shape=(jax.ShapeDtypeStruct((rows_total, GLA_HEADS * GLA_DV), F32),
                   jax.ShapeDtypeStruct((GLA_HEADS, rows_total // GLA_CHUNK, GLA_DV, GLA_DK), F32)),
        grid=(GLA_HEADS, n_blocks),
        in_specs=specs + ([] if add_o is None else [o_spec]),
        out_specs=(o_spec,
                   pl.BlockSpec((1, n_chunks, GLA_DV, GLA_DK), lambda h, s: (h, tix(s), 0, 0))),
        scratch_shapes=[pltpu.VMEM((GLA_DV, GLA_DK), F32), pltpu.VMEM((n_chunks, GLA_DV, GLA_DK), F32),
                        pltpu.VMEM((n_chunks, SUBLANES, GLA_DK), F32)],
        compiler_params=_params("parallel", "arbitrary"),
    )(proj, proj, proj, proj, wg_pad, bg, *([] if add_o is None else [add_o]))


def gla_bwd(proj, wg_pad, bg, d_o, states, dqkv_in, reverse, name):
    rows_total = proj.shape[0]
    rows = min(GLA_BLOCK, rows_total)
    n_blocks = rows_total // rows
    n_chunks = rows // GLA_CHUNK
    specs, tix = _gla_specs(rows, n_blocks, not reverse)
    d = 1 if reverse else 0
    specs[4] = pl.BlockSpec((1, LANES, GLA_DK), lambda h, s: (d, 0, h))
    specs[5] = pl.BlockSpec((1, 1, GLA_DK), lambda h, s: (d, 0, h))
    add = dqkv_in is not None

    def body(*refs):
        q_ref, k_ref, v_ref, lr_ref, wg_ref, bg_ref, do_ref, st_ref = refs[:8]
        refs = refs[8:]
        if add:
            aq_ref, ak_ref, av_ref = refs[:3]
            refs = refs[3:]
        dq_ref, dk_ref, dv_ref, dz_ref, dstate, g_scr, dec_scr, dsn_scr = refs
        z, q_in, k_in, k_st, dec, (f_q, f_k, f_s) = _gla_prepare(q_ref, k_ref, lr_ref, wg_ref, bg_ref, reverse,
                                                                 n_chunks)
        mask = _gla_mask(reverse)
        vb = v_ref[...].reshape(n_chunks, GLA_CHUNK, GLA_DV).astype(BF16)
        dob = do_ref[...].reshape(n_chunks, GLA_CHUNK, GLA_DV).astype(BF16)
        qb, kb, ksb = q_in.astype(BF16), k_in.astype(BF16), k_st.astype(BF16)
        st = st_ref[0]
        stb = st.astype(BF16)
        pb = jnp.where(mask, _bdot(qb, kb, 2, 2), 0.0).astype(BF16)
        dpb = jnp.where(mask, _bdot(dob, vb, 2, 2), 0.0).astype(BF16)
        d_qin = _bdot(dpb, kb, 2, 1) + _bdot(dob, stb, 2, 1)
        d_kin = _bdot(dpb, qb, 1, 1)
        dv = _bdot(pb, dob, 1, 1)
        g_scr[...] = _bdot(dob, qb, 1, 1)
        dec_scr[...] = jnp.broadcast_to(dec, dec_scr.shape)

        @pl.when(pl.program_id(1) == 0)
        def _():
            dstate[...] = jnp.zeros_like(dstate)

        for c in range(n_chunks):
            cc = c if reverse else n_chunks - 1 - c
            dsn_scr[cc] = dstate[...]
            dstate[...] = dstate[...] * dec_scr[cc, 0:1] + g_scr[cc]
        dsn = dsn_scr[...]
        dsnb = dsn.astype(BF16)
        dv = dv + _bdot(ksb, dsnb, 2, 2)
        d_kst = _bdot(vb, dsnb, 2, 1)
        d_dec = jnp.sum(dsn * st, axis=1, keepdims=True)
        ks_term = d_kst * k_st
        d_btot = d_dec * dec + jnp.sum(ks_term, axis=1, keepdims=True)
        d_b = d_qin * q_in - d_kin * k_in - ks_term
        pos = lax.broadcasted_iota(jnp.int32, d_b.shape, 1)
        edge = 0 if reverse else GLA_CHUNK - 1
        d_b = d_b + jnp.where(pos == edge, d_btot, 0.0)
        dg = _chunk_cumsum(d_b.reshape(rows, GLA_DK), not reverse)
        dz_ref[...] = dg * (1.0 / GLA_NORMALIZER) * _sigmoid(-z)
        dq = (d_qin * f_q).reshape(rows, GLA_DK)
        dk = (d_kin * f_k + d_kst * f_s).reshape(rows, GLA_DK)
        dv = dv.reshape(rows, GLA_DV)
        if add:
            dq_ref[...] = (dq + aq_ref[...]).astype(BF16)
            dk_ref[...] = (dk + ak_ref[...]).astype(BF16)
            dv_ref[...] = (dv + av_ref[...]).astype(BF16)
        else:
            dq_ref[...] = dq
            dk_ref[...] = dk
            dv_ref[...] = dv

    qkv_specs = [pl.BlockSpec((rows, GLA_DK), lambda h, s: (tix(s), h)),
                 pl.BlockSpec((rows, GLA_DK), lambda h, s: (tix(s), h)),
                 pl.BlockSpec((rows, GLA_DV), lambda h, s: (tix(s), h))]
    in_specs = specs + [pl.BlockSpec((rows, GLA_DV), lambda h, s: (tix(s), h)),
                        pl.BlockSpec((1, n_chunks, GLA_DV, GLA_DK), lambda h, s: (h, tix(s), 0, 0))]
    args = [proj, proj, proj, proj, wg_pad, bg, d_o, states]
    out_dtype = F32
    if add:
        in_specs += qkv_specs
        args += list(dqkv_in)
        out_dtype = BF16
    return pl.pallas_call(
        body, name=name,
        out_shape=(jax.ShapeDtypeStruct((rows_total, GLA_HEADS * GLA_DK), out_dtype),
                   jax.ShapeDtypeStruct((rows_total, GLA_HEADS * GLA_DK), out_dtype),
                   jax.ShapeDtypeStruct((rows_total, GLA_HEADS * GLA_DV), out_dtype),
                   jax.ShapeDtypeStruct((rows_total, GLA_HEADS * GLA_DK), F32)),
        grid=(GLA_HEADS, n_blocks),
        in_specs=in_specs,
        out_specs=(pl.BlockSpec((rows, GLA_DK), lambda h, s: (tix(s), h)),
                   pl.BlockSpec((rows, GLA_DK), lambda h, s: (tix(s), h)),
                   pl.BlockSpec((rows, GLA_DV), lambda h, s: (tix(s), h)),
                   pl.BlockSpec((rows, GLA_DK), lambda h, s: (tix(s), h))),
        scratch_shapes=[pltpu.VMEM((GLA_DV, GLA_DK), F32), pltpu.VMEM((n_chunks, GLA_DV, GLA_DK), F32),
                        pltpu.VMEM((n_chunks, SUBLANES, GLA_DK), F32),
                        pltpu.VMEM((n_chunks, GLA_DV, GLA_DK), F32)],
        compiler_params=_params("parallel", "arbitrary"),
    )(*args)


def gla_gate_bwd(proj, dz_f, dz_b, wg_pad, name):
    rows_total = proj.shape[0]
    tm = min(ROW_TILE, rows_total)
    n_key = GLA_HEADS * GLA_DK

    def body(lr_ref, dzf_ref, dzb_ref, wg_ref, dlr_ref, dwg_ref, dbg_ref):
        step = pl.program_id(0)
        lr_t = jnp.transpose(lr_ref[...])
        dzf, dzb = dzf_ref[...], dzb_ref[...]
        dzf16, dzb16 = dzf.astype(BF16), dzb.astype(BF16)
        dlr_ref[...] = (_dot_nt(dzf16, wg_ref[0]) + _dot_nt(dzb16, wg_ref[1])).astype(BF16)
        dwf = _dot(lr_t[0:GLA_RANK].astype(BF16), dzf16)
        dwb = _dot(lr_t[GLA_RANK:2 * GLA_RANK].astype(BF16), dzb16)
        dbg = jnp.concatenate([_colsum(dzf), _colsum(dzb)], axis=0)

        @pl.when(step == 0)
        def _():
            dwg_ref[0] = dwf
            dwg_ref[1] = dwb
            dbg_ref[...] = dbg

        @pl.when(step > 0)
        def _():
            dwg_ref[0] += dwf
            dwg_ref[1] += dwb
            dbg_ref[...] += dbg

    return pl.pallas_call(
        body, name=name,
        out_shape=(jax.ShapeDtypeStruct((rows_total, LANES), BF16), jax.ShapeDtypeStruct((2, GLA_RANK, n_key), F32),
                   jax.ShapeDtypeStruct((2, n_key), F32)),
        grid=(rows_total // tm,),
        in_specs=[pl.BlockSpec((tm, LANES), lambda i: (i, LR_COL // LANES)),
                  pl.BlockSpec((tm, n_key), lambda i: (i, 0)), pl.BlockSpec((tm, n_key), lambda i: (i, 0)),
                  _full((2, LANES, n_key))],
        out_specs=(pl.BlockSpec((tm, LANES), lambda i: (i, 0)), _full((2, GLA_RANK, n_key)), _full((2, n_key))),
        compiler_params=_params("arbitrary"),
    )(proj, dz_f, dz_b, wg_pad)


def _head_norm(o, gain):
    outs, hats, rstds = [], [], []
    for h in range(GLA_HEADS):
        oh = o[:, h * GLA_DV:(h + 1) * GLA_DV]
        rstd = lax.rsqrt(jnp.mean(oh * oh, axis=-1, keepdims=True) + NORM_EPS)
        hat = oh * rstd
        outs.append(hat * gain)
        hats.append(hat)
        rstds.append(rstd)
    return outs, hats, rstds


def odd_out_fwd(o, proj, head_gain, w_out, gain, x1, target, name):
    rows, d = x1.shape
    tm = min(ROW_TILE, rows)
    r_block = (2 * GLA_HEADS * GLA_DK + GLA_HEADS * GLA_DV) // d

    def body(o_ref, r_ref, hg_ref, w_ref, g_ref, x1_ref, tgt_ref, y2_ref, dy_ref, dx2_ref, loss_ref, dg_ref):
        step = pl.program_id(0)
        on, _, _ = _head_norm(o_ref[...], hg_ref[...])
        r = r_ref[...]
        y2 = (jnp.concatenate(on, axis=1) * (r * _sigmoid(r))).astype(BF16)
        y2_ref[...] = y2
        y = _dot(y2, w_ref[...])
        gain_v = g_ref[...]
        rstd = lax.rsqrt(jnp.mean(y * y, axis=-1, keepdims=True) + NORM_EPS)
        x2 = x1_ref[...] + y * rstd * gain_v
        diff = x2 - tgt_ref[...]
        loss = 0.5 * jnp.sum(jnp.mean(diff * diff, axis=-1, keepdims=True), axis=0, keepdims=True)
        dx2 = diff * (1.0 / d)
        dx2_ref[...] = dx2
        dy, dg_rows = _rmsnorm_bwd(dx2, y, gain_v)
        dy_ref[...] = dy.astype(BF16)
        _accumulate(loss_ref, jnp.broadcast_to(loss, loss_ref.shape), step)
        _accumulate(dg_ref, _colsum(dg_rows), step)

    row = lambda n, col=0: pl.BlockSpec((tm, n), lambda i: (i, col))
    return pl.pallas_call(
        body, name=name,
        out_shape=(jax.ShapeDtypeStruct((rows, d), BF16), jax.ShapeDtypeStruct((rows, d), BF16),
                   jax.ShapeDtypeStruct((rows, d), F32), jax.ShapeDtypeStruct((SUBLANES, LANES), F32),
                   jax.ShapeDtypeStruct((1, d), F32)),
        grid=(rows // tm,),
        in_specs=[row(d), row(d, r_block), _full((1, GLA_DV)), _full((d, d)), _full((1, d)), row(d), row(d)],
        out_specs=(row(d), row(d), row(d), _full((SUBLANES, LANES)), _full((1, d))),
        compiler_params=_params("arbitrary"),
    )(o, proj, head_gain, w_out, gain, x1, target)


def odd_out_bwd(dy, w_out, o, proj, head_gain, name):
    rows, d = dy.shape
    tm = min(ROW_TILE, rows)
    r_block = (2 * GLA_HEADS * GLA_DK + GLA_HEADS * GLA_DV) // d

    def body(dy_ref, w_ref, o_ref, r_ref, hg_ref, dr_ref, do_ref, dhg_ref):
        dy2 = _dot_nt(dy_ref[...], w_ref[...])
        hg = hg_ref[...]
        on, hats, rstds = _head_norm(o_ref[...], hg)
        r = r_ref[...]
        sr = _sigmoid(r)
        dr_ref[...] = (dy2 * jnp.concatenate(on, axis=1) * (sr * (1.0 + r * (1.0 - sr)))).astype(BF16)
        d_on = dy2 * (r * sr)
        d_os, dhg = [], None
        for h in range(GLA_HEADS):
            dn = d_on[:, h * GLA_DV:(h + 1) * GLA_DV]
            part = _colsum(dn * hats[h])
            dhg = part if dhg is None else dhg + part
            dng = dn * hg
            d_os.append(rstds[h] * (dng - hats[h] * jnp.mean(dng * hats[h], axis=-1, keepdims=True)))
        do_ref[...] = jnp.concatenate(d_os, axis=1)
        _accumulate(dhg_ref, dhg, pl.program_id(0))

    row = lambda n, col=0: pl.BlockSpec((tm, n), lambda i: (i, col))
    return pl.pallas_call(
        body, name=name,
        out_shape=(jax.ShapeDtypeStruct((rows, d), BF16), jax.ShapeDtypeStruct((rows, d), F32),
                   jax.ShapeDtypeStruct((1, GLA_DV), F32)),
        grid=(rows // tm,),
        in_specs=[row(d), _full((d, d)), row(d), row(d, r_block), _full((1, GLA_DV))],
        out_specs=(row(d), row(d), _full((1, GLA_DV))),
        compiler_params=_params("arbitrary"),
    )(dy, w_out, o, proj, head_gain)


def local_step(x, target, w, reduce_first=None, reduce_second=None, late_weights=None):
    g, g16 = {}, {}
    proj_e, h0 = norm_matmul(x, w["even_norm_pre"], w["even_w_in"], BF16, "even_in_proj")
    h_dir, acts = zip(*[rglru_fwd(proj_e, w["rg_conv_w"], w["rg_conv_b"], w["rg_gate_w"][d], w["rg_gate_b"][d],
                                  w["rg_lambda"][d], d == 1, "rglru_fwd_%d" % d) for d in range(2)])
    ycat = even_mix_fwd(proj_e, h_dir[0], h_dir[1], w["sc_conv_w"], "even_mix_fwd")
    if late_weights is not None:
        w = dict(w, **late_weights(ycat))
    x1, y_e = even_out_fwd(ycat, w["even_w_out"], w["even_norm_post"], x, "even_out_fwd")
    proj_o, h1 = norm_matmul(x1, w["odd_norm_pre"], w["odd_w_in"], F32, "odd_in_proj")
    o, st_dir = None, []
    for d in range(2):
        o, st = gla_fwd(proj_o, w["gla_wg_pad"], w["gla_b_gate"], o, d == 1, "gla_fwd_%d" % d)
        st_dir.append(st)
    y2, dy_o, dx2, loss, g["odd_norm_post"] = odd_out_fwd(
        o, proj_o, w["gla_norm_g"], w["odd_w_out"], w["odd_norm_post"], x1, target, "odd_out_fwd")
    g["odd_w_out"], g16["odd_w_out"] = (a[0] for a in matmul_dw(y2, dy_o, D_MODEL, "odd_w_out_grad"))
    dr, d_o, g["gla_norm_g"] = odd_out_bwd(dy_o, w["odd_w_out"], o, proj_o, w["gla_norm_g"], "odd_out_bwd")
    dq, dk, dv, dz_f = gla_bwd(proj_o, w["gla_wg_pad"], w["gla_b_gate"], d_o, st_dir[0], None, False, "gla_bwd_0")
    dq, dk, dv, dz_b = gla_bwd(proj_o, w["gla_wg_pad"], w["gla_b_gate"], d_o, st_dir[1], (dq, dk, dv), True,
                               "gla_bwd_1")
    dlr, g["gla_w_gate_lr"], g["gla_b_gate"] = gla_gate_bwd(proj_o, dz_f, dz_b, w["gla_wg_pad"], "gla_gate_bwd")
    dproj_o = [dq, dk, dv, dr, dlr]
    g["odd_w_in"] = jnp.concatenate(matmul_dw_pieces(h1, dproj_o, "odd_w_in_grad"), axis=1)[:, :ODD_IN]
    dx1, g["odd_norm_pre"] = inproj_bwd_pieces(dproj_o, w["odd_w_in"], x1, w["odd_norm_pre"], dx2, "odd_in_proj_bwd")
    dy_e, dycat, g["even_norm_post"] = even_out_bwd(dx1, y_e, w["even_norm_post"], w["even_w_out"], "even_out_bwd")
    g["even_w_out"], g16["even_w_out"] = (a[0] for a in matmul_dw(ycat, dy_e, D_MODEL, "even_w_out_grad"))
    lam = w["rg_lambda"] if reduce_first is None else w["rg_lambda"] + reduce_first(g, g16)
    dua, dgw, dgb, dlam = None, [], [], []
    for d in range(2):
        a, b, c, e = rglru_bwd(proj_e, dycat, h_dir[d], acts[d], w["rg_gate_w"][d], lam[d], dua, d == 1,
                               "rglru_bwd_%d" % d)
        dua = a
        dgw.append(b)
        dgb.append(c)
        dlam.append(e)
    dproj_e, g["rg_conv_w"], g["rg_conv_b"], g["sc_conv_w"] = even_mix_bwd(
        proj_e, dycat, h_dir[0], h_dir[1], dua, w["rg_conv_w"], w["sc_conv_w"], "even_mix_bwd")
    dgw = jnp.stack(dgw).reshape(2, RG_HEADS, RG_HEAD_DIM, 2, RG_HEAD_DIM)
    g["rg_gate_w"] = jnp.transpose(dgw, (0, 3, 1, 2, 4))
    g["rg_gate_b"] = jnp.stack(dgb).reshape(2, 2, RG_HEADS, RG_HEAD_DIM)
    g["rg_lambda"] = jnp.concatenate(dlam, axis=0)
    g["even_w_in"], g16["even_w_in"] = matmul_dw(h0, dproj_e, EVEN_IN // 4, "even_w_in_grad")
    gain = w["even_norm_pre"] if reduce_second is None else w["even_norm_pre"] + reduce_second(g, g16)
    grad_x, g["even_norm_pre"] = inproj_bwd(dproj_e, w["even_w_in"], x, gain, dx1, "even_in_proj_bwd")
    return loss, grad_x, g


def _prepare_weights(full):
    w = {}
    for name in ("even_norm_pre", "even_norm_post", "rg_conv_b", "odd_norm_pre", "odd_norm_post", "gla_norm_g"):
        if name in full:
            w[name] = full[name].reshape(1, -1)
    for name in ("rg_conv_w", "sc_conv_w"):
        if name in full:
            w[name] = full[name]
    for name in ("even_w_out", "odd_w_out"):
        if name in full:
            w[name] = full[name].astype(BF16)
    if "even_w_in" in full:
        w["even_w_in"] = full["even_w_in"].astype(BF16)
        if w["even_w_in"].ndim == 2:
            w["even_w_in"] = jnp.transpose(w["even_w_in"].reshape(D_MODEL, 4, EVEN_IN // 4), (1, 0, 2))
    if "rg_gate_w" in full:
        gw = jnp.transpose(full["rg_gate_w"].astype(BF16), (0, 2, 3, 1, 4))
        w["rg_gate_w"] = gw.reshape(2, RG_HEADS, RG_HEAD_DIM, 2 * RG_HEAD_DIM)
        w["rg_gate_b"] = full["rg_gate_b"].reshape(2, 2, D_MODEL)
        w["rg_lambda"] = full["rg_lambda"].reshape(2, 1, D_MODEL)
    if "odd_w_in" in full:
        w_in = jnp.pad(full["odd_w_in"].astype(BF16), ((0, 0), (0, ODD_IN_PAD - ODD_IN)))
        w["odd_w_in"] = w_in.reshape(1, D_MODEL, ODD_IN_PAD)
    if "gla_w_gate_lr" in full:
        wg = full["gla_w_gate_lr"].astype(BF16)
        w["gla_wg_pad"] = jnp.stack([jnp.pad(wg[d], ((d * GLA_RANK, LANES - (d + 1) * GLA_RANK), (0, 0)))
                                     for d in range(2)])
        w["gla_b_gate"] = full["gla_b_gate"].reshape(2, 1, GLA_HEADS * GLA_DK)
    return w


SHARDED_SMALL = (("rg_conv_w", (4, 256)), ("rg_lambda", (2, 256)), ("sc_conv_w", (3, 256)),
                 ("odd_norm_pre", (256,)), ("odd_norm_post", (256,)), ("gla_w_gate_lr", (2, 16, 128)),
                 ("gla_b_gate", (2, 128)), ("gla_norm_g", (64,)))
SHARDED_ROWS = 96
REPLICATED = (("rg_gate_w", (2, 2, 8, 128, 128)), ("even_norm_post", (1024,)), ("rg_conv_b", (1024,)),
              ("rg_gate_b", (2, 2, 8, 128)))
GATE_ROWS = 4096
LAST_REPLICATED = (("even_norm_pre", (1024,)),)
LAST_ROWS = 8
REPLICATED_ROWS = 4160
REP_PART = REPLICATED_ROWS // 8
HALF_SHARDED = SHARDED_ROWS // 2
PACK_HALF = HALF_SHARDED + REP_PART


def _seg_rows(shape):
    n = 1
    for s in shape:
        n *= s
    return -(-n // (SUBLANES * LANES)) * SUBLANES


def _pack(arrays, spec, total_rows, lead=()):
    parts = []
    for name, shape in spec:
        flat = arrays[name].reshape(lead + (-1,))
        pad = _seg_rows(shape) * LANES - flat.shape[-1]
        if pad:
            flat = jnp.pad(flat, [(0, 0)] * len(lead) + [(0, pad)])
        parts.append(flat.reshape(lead + (-1, LANES)))
    rows = jnp.concatenate(parts, axis=len(lead))
    pad = total_rows - rows.shape[len(lead)]
    return jnp.pad(rows, [(0, 0)] * len(lead) + [(0, pad), (0, 0)])


def _unpack(rows, spec, lead=()):
    out, at = {}, 0
    for name, shape in spec:
        n = 1
        for s in shape:
            n *= s
        k = _seg_rows(shape)
        seg = lax.slice_in_dim(rows, at, at + k, axis=len(lead)).reshape(lead + (-1,))
        out[name] = lax.slice_in_dim(seg, 0, n, axis=len(lead)).reshape(lead + shape)
        at += k
    return out


def _split_owners(arr):
    a = arr.reshape(arr.shape[:-1] + (4, arr.shape[-1] // 4))
    return jnp.moveaxis(a, -2, 0)


def _merge_owners(arr):
    a = jnp.moveaxis(arr, 0, -2)
    return a.reshape(a.shape[:-2] + (-1,))


HBM_SPEC = pl.BlockSpec(memory_space=pltpu.HBM)


def _position():
    x, y, c = lax.axis_index("x"), lax.axis_index("y"), lax.axis_index("c")
    chips = [(1 - x, y), (x, 1 - y), (1 - x, 1 - y)]
    return x, y, c, chips


def _remote(src, dst, send_sem, recv_sem, device):
    return pltpu.make_async_remote_copy(src_ref=src, dst_ref=dst, send_sem=send_sem, recv_sem=recv_sem,
                                        device_id=device, device_id_type=MESH)


SEM_SPEC = pl.BlockSpec(memory_space=pltpu.SEMAPHORE)
SIDE_EFFECT = pltpu.SideEffectType.DATAFLOW_SIDE_EFFECTING


def _gather_copies(ins, lands, n_h, send_sems, recv_sems):
    x, y, c, chips = _position()
    me = 2 * x + y
    copies = []
    for a in range(len(ins)):
        for k, chip in enumerate(chips):
            src = ins[a].at[c] if a < n_h else ins[a]
            dst = lands[a].at[me, c] if a < n_h else lands[a].at[me]
            copies.append(_remote(src, dst, send_sems.at[3 * a + k], recv_sems.at[3 * a + k], (chip[0], chip[1], c)))
    return copies


def gather_start(halved, whole, name):
    arrays = list(halved) + list(whole)
    n, n_h = len(arrays), len(halved)
    lands = [lax.empty((4,) + a.shape, a.dtype) for a in arrays]

    def body(*refs):
        ins, lz, send_sems, recv_sems, token = refs[:n], refs[n:2 * n], refs[2 * n], refs[2 * n + 1], refs[-1]
        for cp in _gather_copies(ins, lz, n_h, send_sems, recv_sems):
            cp.start()
        token[...] = jnp.zeros_like(token)

    operands = [pltpu.with_memory_space_constraint(a, pltpu.HBM) for a in arrays + lands]
    return pl.pallas_call(
        body, name=name,
        out_shape=(pltpu.SemaphoreType.DMA((3 * n,)), pltpu.SemaphoreType.DMA((3 * n,)))
        + tuple(pltpu.HBM(a.shape, a.dtype) for a in operands) + (jax.ShapeDtypeStruct((SUBLANES, LANES), F32),),
        in_specs=[HBM_SPEC] * (2 * n),
        out_specs=(SEM_SPEC, SEM_SPEC) + (HBM_SPEC,) * (2 * n) + (pl.BlockSpec(memory_space=pltpu.VMEM),),
        input_output_aliases={i: 2 + i for i in range(2 * n)},
        compiler_params=pltpu.CompilerParams(has_side_effects=SIDE_EFFECT),
    )(*operands)


def gather_wait(started, n_h, after, name):
    send_sems, recv_sems = started[0], started[1]
    operands = list(started[2:-1])
    n = len(operands) // 2

    def body(*refs):
        ins, lz, send_ref, recv_ref = refs[:n], refs[n:2 * n], refs[2 * n], refs[2 * n + 1]
        for cp in _gather_copies(ins, lz, n_h, send_ref, recv_ref):
            cp.wait_send()
            cp.wait_recv()

    outs = pl.pallas_call(
        body, name=name,
        out_shape=tuple(pltpu.HBM(a.shape, a.dtype) for a in operands),
        in_specs=[HBM_SPEC] * (2 * n) + [SEM_SPEC, SEM_SPEC, pl.BlockSpec(memory_space=pl.ANY)],
        out_specs=(HBM_SPEC,) * (2 * n),
        input_output_aliases={i: i for i in range(2 * n)},
        compiler_params=pltpu.CompilerParams(has_side_effects=SIDE_EFFECT),
    )(*operands, send_sems, recv_sems, after)
    return outs[n:]


def pass_to_sibling(fulls, name):
    n = len(fulls)

    def body(*refs):
        bufs = refs[n:2 * n]
        send_sems, recv_sems = refs[2 * n:]
        x, y, c, chips = _position()
        sibling = (x, y, 1 - c)
        copies = []
        for a in range(n):
            for k, chip in enumerate(chips):
                q = 2 * chip[0] + chip[1]
                cp = _remote(bufs[a].at[q, c], bufs[a].at[q, c], send_sems.at[3 * a + k], recv_sems.at[3 * a + k],
                             sibling)
                cp.start()
                copies.append(cp)
        for a in range(n):
            for k, chip in enumerate(chips):
                q = 2 * chip[0] + chip[1]
                passed = bufs[a].at[q, 1 - c]
                _remote(passed, passed, send_sems.at[3 * a + k], recv_sems.at[3 * a + k], sibling).wait_recv()
        for cp in copies:
            cp.wait_send()

    return pl.pallas_call(
        body, name=name,
        out_shape=[jax.ShapeDtypeStruct(a.shape, a.dtype) for a in fulls],
        in_specs=[HBM_SPEC] * n, out_specs=[HBM_SPEC] * n,
        input_output_aliases={i: i for i in range(n)},
        scratch_shapes=[pltpu.SemaphoreType.DMA((3 * n,)), pltpu.SemaphoreType.DMA((3 * n,))],
    )(*fulls)


def place_own(full, own, chip, name):
    _, _, r, cols = full.shape
    tr = _row_tile(r, cols)

    def body(p_ref, own_ref, full_ref, o_ref):
        o_ref[0] = own_ref[...]

    return pl.pallas_call(
        body, name=name,
        out_shape=jax.ShapeDtypeStruct(full.shape, full.dtype),
        grid_spec=pltpu.PrefetchScalarGridSpec(
            num_scalar_prefetch=1, grid=(2, r // tr),
            in_specs=[pl.BlockSpec((1, tr, cols), lambda h, i, p_ref: (h, i, 0)), pl.BlockSpec(memory_space=pl.ANY)],
            out_specs=pl.BlockSpec((1, 1, tr, cols), lambda h, i, p_ref: (p_ref[0], h, i, 0))),
        input_output_aliases={2: 0},
        compiler_params=_params("parallel", "parallel"),
    )(chip, own, full)


def exchange_with_sibling(arrays, name):
    n = len(arrays)

    def body(*refs):
        ins, outs = refs[:n], refs[n:2 * n]
        send_sems, recv_sems = refs[2 * n:]
        x, y, c, _ = _position()
        copies = []
        for a in range(n):
            cp = _remote(ins[a].at[:, 1 - c], outs[a], send_sems.at[a], recv_sems.at[a], (x, y, 1 - c))
            cp.start()
            copies.append(cp)
        for cp in copies:
            cp.wait()

    return pl.pallas_call(
        body, name=name,
        out_shape=[jax.ShapeDtypeStruct((a.shape[0],) + a.shape[2:], a.dtype) for a in arrays],
        in_specs=[HBM_SPEC] * n, out_specs=[HBM_SPEC] * n,
        scratch_shapes=[pltpu.SemaphoreType.DMA((n,)), pltpu.SemaphoreType.DMA((n,))],
    )(*arrays)


def _chip_copies(ins, lands, send_sems, recv_sems):
    x, y, c, chips = _position()
    copies = []
    for a in range(len(ins)):
        for k, chip in enumerate(chips):
            q = 2 * chip[0] + chip[1]
            copies.append(_remote(ins[a].at[q], lands[a].at[k], send_sems.at[3 * a + k], recv_sems.at[3 * a + k],
                                  (chip[0], chip[1], c)))
    return copies


def exchange_with_chips_start(arrays, name):
    n = len(arrays)
    lands = [lax.empty((3,) + a.shape[1:], a.dtype) for a in arrays]

    def body(*refs):
        ins, lz, send_sems, recv_sems, token = refs[:n], refs[n:2 * n], refs[2 * n], refs[2 * n + 1], refs[-1]
        for cp in _chip_copies(ins, lz, send_sems, recv_sems):
            cp.start()
        token[...] = jnp.zeros_like(token)

    operands = [pltpu.with_memory_space_constraint(a, pltpu.HBM) for a in list(arrays) + lands]
    return pl.pallas_call(
        body, name=name,
        out_shape=(pltpu.SemaphoreType.DMA((3 * n,)), pltpu.SemaphoreType.DMA((3 * n,)))
        + tuple(pltpu.HBM(a.shape, a.dtype) for a in operands) + (jax.ShapeDtypeStruct((SUBLANES, LANES), F32),),
        in_specs=[HBM_SPEC] * (2 * n),
        out_specs=(SEM_SPEC, SEM_SPEC) + (HBM_SPEC,) * (2 * n) + (pl.BlockSpec(memory_space=pltpu.VMEM),),
        input_output_aliases={i: 2 + i for i in range(2 * n)},
        compiler_params=pltpu.CompilerParams(has_side_effects=SIDE_EFFECT),
    )(*operands)


def exchange_with_chips_wait(started, after, name):
    send_sems, recv_sems = started[0], started[1]
    operands = list(started[2:-1])
    n = len(operands) // 2

    def body(*refs):
        ins, lz, send_ref, recv_ref = refs[:n], refs[n:2 * n], refs[2 * n], refs[2 * n + 1]
        for cp in _chip_copies(ins, lz, send_ref, recv_ref):
            cp.wait_send()
            cp.wait_recv()

    outs = pl.pallas_call(
        body, name=name,
        out_shape=tuple(pltpu.HBM(a.shape, a.dtype) for a in operands),
        in_specs=[HBM_SPEC] * (2 * n) + [SEM_SPEC, SEM_SPEC, pl.BlockSpec(memory_space=pl.ANY)],
        out_specs=(HBM_SPEC,) * (2 * n),
        input_output_aliases={i: i for i in range(2 * n)},
        compiler_params=pltpu.CompilerParams(has_side_effects=SIDE_EFFECT),
    )(*operands, send_sems, recv_sems, after)
    return outs[:n], outs[n:]


def share_totals(totals, pack_total, last_part):
    arrays = list(totals) + [pack_total]
    n = len(arrays)

    def body(*refs):
        ins, last, outs, rep, last_all = refs[:n], refs[n], refs[n + 1:2 * n + 1], refs[2 * n + 1], refs[2 * n + 2]
        send_sems, recv_sems, rep_send, rep_recv, last_send, last_recv = refs[2 * n + 3:]
        x, y, c, chips = _position()
        sibling = (x, y, 1 - c)
        me = 4 * x + 2 * y + c
        sends = []
        for a in range(n):
            cp = _remote(ins[a], outs[a], send_sems.at[a], recv_sems.at[a], sibling)
            cp.start()
            sends.append(cp)
        mine = ins[n - 1].at[pl.ds(HALF_SHARDED, REP_PART)]
        peers = [sibling]
        for chip in chips:
            peers += [(chip[0], chip[1], c), (chip[0], chip[1], 1 - c)]
        for j, peer in enumerate(peers):
            for src, dst, s_sem, r_sem in ((mine, rep, rep_send, rep_recv), (last, last_all, last_send, last_recv)):
                cp = _remote(src, dst.at[me], s_sem.at[j], r_sem.at[j], peer)
                cp.start()
                sends.append(cp)
        for a in range(n):
            _remote(outs[a], outs[a], send_sems.at[a], recv_sems.at[a], sibling).wait_recv()
        for j, peer in enumerate(peers):
            it = 4 * peer[0] + 2 * peer[1] + peer[2]
            _remote(rep.at[it], rep.at[it], rep_send.at[j], rep_recv.at[j], peer).wait_recv()
            _remote(last_all.at[it], last_all.at[it], last_send.at[j], last_recv.at[j], peer).wait_recv()
        for cp in sends:
            cp.wait_send()

    outs = pl.pallas_call(
        body, name="grad_share_totals",
        out_shape=[jax.ShapeDtypeStruct(a.shape, a.dtype) for a in arrays]
        + [jax.ShapeDtypeStruct((8, REP_PART, LANES), F32), jax.ShapeDtypeStruct((8,) + last_part.shape, F32)],
        in_specs=[HBM_SPEC] * (n + 1), out_specs=[HBM_SPEC] * (n + 2),
        scratch_shapes=[pltpu.SemaphoreType.DMA((n,)), pltpu.SemaphoreType.DMA((n,))]
        + [pltpu.SemaphoreType.DMA((7,))] * 4,
    )(*arrays, last_part)
    return outs[:n], outs[n], outs[n + 1]


def sum_parts(parts, name):
    def body(p_ref, o_ref):
        total = p_ref[0]
        for k in range(1, parts.shape[0]):
            total = total + p_ref[k]
        o_ref[...] = total

    return pl.pallas_call(body, name=name, out_shape=jax.ShapeDtypeStruct(parts.shape[1:], parts.dtype))(parts)


TILE_BYTES = 2 << 20


def _row_tile(rows, cols):
    best = None
    for t in range(SUBLANES, rows + 1, SUBLANES):
        if rows % t == 0 and t * cols * 4 <= TILE_BYTES:
            best = t
    return best if best is not None else rows


def add_sibling(mine, received, core, out_dtype, name):
    _, _, r, cols = mine.shape
    tr = _row_tile(r, cols)

    def body(c_ref, a_ref, b_ref, o_ref):
        o_ref[...] = (a_ref[0] + b_ref[...].astype(F32)).astype(out_dtype)

    return pl.pallas_call(
        body, name=name,
        out_shape=jax.ShapeDtypeStruct((4, r, cols), out_dtype),
        grid_spec=pltpu.PrefetchScalarGridSpec(
            num_scalar_prefetch=1, grid=(4, r // tr),
            in_specs=[pl.BlockSpec((1, 1, tr, cols), lambda o, i, c_ref: (o, c_ref[0], i, 0)),
                      pl.BlockSpec((1, tr, cols), lambda o, i, c_ref: (o, i, 0))],
            out_specs=pl.BlockSpec((1, tr, cols), lambda o, i, c_ref: (o, i, 0))),
        compiler_params=_params("parallel", "parallel"),
    )(core, mine, received)


def add_chips(own, received, chip, name):
    _, r, cols = own.shape
    tr = _row_tile(r, cols)

    def body(p_ref, a_ref, b0, b1, b2, o_ref):
        o_ref[...] = ((a_ref[0].astype(F32) + b0[0].astype(F32)) + b1[0].astype(F32)) + b2[0].astype(F32)

    rb = lambda k: pl.BlockSpec((1, tr, cols), lambda i, p_ref: (k, i, 0))
    return pl.pallas_call(
        body, name=name,
        out_shape=jax.ShapeDtypeStruct((r, cols), F32),
        grid_spec=pltpu.PrefetchScalarGridSpec(
            num_scalar_prefetch=1, grid=(r // tr,),
            in_specs=[pl.BlockSpec((1, tr, cols), lambda i, p_ref: (p_ref[0], i, 0)), rb(0), rb(1), rb(2)],
            out_specs=pl.BlockSpec((tr, cols), lambda i, p_ref: (i, 0))),
        compiler_params=_params("parallel"),
    )(chip, own, received, received, received)


def _adamw_update(gv, w_ref, m_ref, v_ref, d_ref, nm_ref, nv_ref):
    nm = ADAM_B1 * m_ref[...] + (1.0 - ADAM_B1) * gv
    nv = ADAM_B2 * v_ref[...] + (1.0 - ADAM_B2) * (gv * gv)
    nm_ref[...] = nm
    nv_ref[...] = nv
    m_hat = nm / (1.0 - ADAM_B1 ** ADAM_STEP)
    v_hat = nv / (1.0 - ADAM_B2 ** ADAM_STEP)
    d_ref[...] = -ADAM_LR * (m_hat / (jnp.sqrt(v_hat) + ADAM_EPS) + ADAM_WD * w_ref[...])


def adamw_halves(w, own, received, m, v, core, name, by_columns=False):
    rows, cols = w.shape

    def body(c_ref, w_ref, own_ref, rec_ref, m_ref, v_ref, g_ref, d_ref, nm_ref, nv_ref):
        gv = jnp.where(pl.program_id(0) == c_ref[0], own_ref[...], rec_ref[...])
        g_ref[...] = gv
        _adamw_update(gv, w_ref, m_ref, v_ref, d_ref, nm_ref, nv_ref)

    if by_columns:
        nr = 1
        whole = pl.BlockSpec((rows, cols // 2), lambda h, i, c_ref: (0, h))
        half = pl.BlockSpec((rows, cols // 2), lambda h, i, c_ref: (0, 0))
    else:
        r = rows // 2
        tr = _row_tile(r, cols)
        nr = r // tr
        whole = pl.BlockSpec((tr, cols), lambda h, i, c_ref: (h * nr + i, 0))
        half = pl.BlockSpec((tr, cols), lambda h, i, c_ref: (i, 0))
    return pl.pallas_call(
        body, name=name,
        out_shape=(jax.ShapeDtypeStruct((rows, cols), F32),) * 4,
        grid_spec=pltpu.PrefetchScalarGridSpec(
            num_scalar_prefetch=1, grid=(2, nr),
            in_specs=[whole, half, half, whole, whole], out_specs=(whole,) * 4),
        compiler_params=_params("parallel", "parallel"),
    )(core, w, own, received, m, v)


def adamw_many(ws, gs, ms, vs, name):
    n = len(ws)

    def body(*refs):
        ins, outs = refs[:4 * n], refs[4 * n:]
        for k in range(n):
            w_ref, g_ref, m_ref, v_ref = (ins[j * n + k] for j in range(4))
            d_ref, nm_ref, nv_ref = outs[3 * k:3 * k + 3]
            _adamw_update(g_ref[...], w_ref, m_ref, v_ref, d_ref, nm_ref, nv_ref)

    flat = pl.pallas_call(
        body, name=name,
        out_shape=[jax.ShapeDtypeStruct(w.shape, F32) for w in ws for _ in range(3)],
    )(*ws, *gs, *ms, *vs)
    return [tuple(flat[3 * k:3 * k + 3]) for k in range(n)]


def adamw(w, g, m, v, name):
    r, cols = w.shape
    tr = _row_tile(r, cols)

    def body(w_ref, g_ref, m_ref, v_ref, g_out, d_ref, nm_ref, nv_ref):
        gv = g_ref[...]
        g_out[...] = gv
        _adamw_update(gv, w_ref, m_ref, v_ref, d_ref, nm_ref, nv_ref)

    blk = pl.BlockSpec((tr, cols), lambda i: (i, 0))
    return pl.pallas_call(
        body, name=name,
        out_shape=(jax.ShapeDtypeStruct((r, cols), F32),) * 4,
        grid=(r // tr,),
        in_specs=[blk] * 4, out_specs=(blk,) * 4,
        compiler_params=_params("parallel"),
    )(w, g, m, v)


WEIGHTS = ("even_norm_pre", "even_norm_post", "even_w_in", "rg_conv_w", "rg_conv_b", "rg_gate_w", "rg_gate_b",
           "rg_lambda", "sc_conv_w", "even_w_out", "odd_norm_pre", "odd_norm_post", "odd_w_in", "gla_w_gate_lr",
           "gla_b_gate", "gla_norm_g", "odd_w_out")
BIG = ("even_w_in", "even_w_out", "odd_w_in", "odd_w_out")


def _halves(a):
    return a.reshape((2, a.shape[0] // 2) + a.shape[1:])


def kernel(x, even_norm_pre, even_norm_post, even_w_in, rg_conv_w, rg_conv_b, rg_gate_w, rg_gate_b, rg_lambda, sc_conv_w, even_w_out, odd_norm_pre, odd_norm_post, odd_w_in, gla_w_gate_lr, gla_b_gate, gla_norm_g, odd_w_out, loss_target, m_even_norm_pre, m_even_norm_post, m_even_w_in, m_rg_conv_w, m_rg_conv_b, m_rg_gate_w, m_rg_gate_b, m_rg_lambda, m_sc_conv_w, m_even_w_out, m_odd_norm_pre, m_odd_norm_post, m_odd_w_in, m_gla_w_gate_lr, m_gla_b_gate, m_gla_norm_g, m_odd_w_out, v_even_norm_pre, v_even_norm_post, v_even_w_in, v_rg_conv_w, v_rg_conv_b, v_rg_gate_w, v_rg_gate_b, v_rg_lambda, v_sc_conv_w, v_even_w_out, v_odd_norm_pre, v_odd_norm_post, v_odd_w_in, v_gla_w_gate_lr, v_gla_b_gate, v_gla_norm_g, v_odd_w_out):
    given = dict(locals())
    shard = {n: given[n][0] for n in WEIGHTS}
    m_in = {n: given["m_" + n][0] for n in WEIGHTS}
    v_in = {n: given["v_" + n][0] for n in WEIGHTS}
    mx, my, mc = lax.axis_index("x"), lax.axis_index("y"), lax.axis_index("c")
    core = jnp.reshape(mc, (1,)).astype(jnp.int32)
    chip = jnp.reshape(2 * mx + my, (1,)).astype(jnp.int32)

    small_shard = _pack(shard, SHARDED_SMALL, SHARDED_ROWS)
    big_own = [_halves(shard[n].astype(BF16)) for n in BIG]
    started_a = gather_start(big_own[:1], [small_shard], "gather_start_a")
    started_b = gather_start(big_own[1:], [], "gather_start_b")
    even_w_in_full, small_full = gather_wait(started_a, 1, started_b[-1], "gather_wait_a")
    (even_w_in_full,) = pass_to_sibling([even_w_in_full], "gather_pass_a")
    even_w_in_full = place_own(even_w_in_full, big_own[0], chip, "place_even_w_in")
    small_full = lax.dynamic_update_slice(small_full, small_shard[None], (chip[0], 0, 0))
    full = {n: shard[n] for n, _ in REPLICATED + LAST_REPLICATED}
    full.update({n: _merge_owners(a) for n, a in _unpack(small_full, SHARDED_SMALL, lead=(4,)).items()})
    full["even_w_in"] = even_w_in_full.reshape(4, D_MODEL, EVEN_IN // 4)

    def late_weights(after):
        lands = pass_to_sibling(list(gather_wait(started_b, 3, after, "gather_wait_b")), "gather_pass_b")
        lands = [place_own(a, b, chip, "place_" + n) for a, b, n in zip(lands, big_own[1:], BIG[1:])]
        odd_w_in = jnp.transpose(lands[1].reshape(4, D_MODEL, ODD_IN // 4), (1, 0, 2)).reshape(D_MODEL, ODD_IN)
        return _prepare_weights({"even_w_out": lands[0].reshape(2 * D_MODEL, D_MODEL), "odd_w_in": odd_w_in,
                                 "odd_w_out": lands[2].reshape(D_MODEL, D_MODEL)})

    pending = {}

    def slab(a):
        return a.reshape((4, 2, a.shape[1] // 2) + a.shape[2:])

    def begin(tag, slabs, to_send, dtypes):
        got = exchange_with_sibling(to_send, "grad_sibling_" + tag)
        sums = [add_sibling(a, b, core, dt, "grad_add_sibling_%s%d" % (tag, i))
                for i, (a, b, dt) in enumerate(zip(slabs, got, dtypes))]
        pending[tag] = exchange_with_chips_start(sums, "grad_chips_start_" + tag)
        return pending[tag][-1][0, 0]

    def finish(tag, after):
        sums, got = exchange_with_chips_wait(pending[tag], after, "grad_chips_wait_" + tag)
        return [add_chips(a, b, chip, "grad_add_chips_%s%d" % (tag, i)) for i, (a, b) in enumerate(zip(sums, got))]

    def reduce_first(g, g16):
        odd_w_in = slab(jnp.transpose(g["odd_w_in"].reshape(D_MODEL, 4, ODD_IN // 4), (1, 0, 2)))
        slabs = [odd_w_in] + [slab(g[n].reshape(4, -1, D_MODEL)) for n in ("odd_w_out", "even_w_out")]
        to_send = [odd_w_in.astype(BF16)] + [slab(g16[n].reshape(4, -1, D_MODEL)) for n in ("odd_w_out", "even_w_out")]
        return begin("a", slabs, to_send, [BF16] * 3)

    def reduce_second(g, g16):
        pending["totals_a"] = finish("a", g["even_w_in"])
        rep_rows = _pack(g, REPLICATED, REPLICATED_ROWS).reshape(4, 2, REP_PART, LANES)
        sh_rows = _pack({n: _split_owners(g[n]) for n, _ in SHARDED_SMALL}, SHARDED_SMALL, SHARDED_ROWS, lead=(4,))
        pack = jnp.concatenate([sh_rows.reshape(4, 2, HALF_SHARDED, LANES), rep_rows], axis=2)
        return begin("b", [slab(g["even_w_in"]), pack], [slab(g16["even_w_in"]), pack], [BF16, F32])

    loss, grad_x, g = local_step(x[0], loss_target[0], _prepare_weights(full), reduce_first, reduce_second,
                                 late_weights)
    odd_w_in_t, odd_w_out_t, even_w_out_t = pending["totals_a"]
    even_w_in_t, pack_t = finish("b", grad_x)
    totals = [even_w_in_t, even_w_out_t, odd_w_in_t, odd_w_out_t]
    last_part = jnp.concatenate([_pack(g, LAST_REPLICATED, LAST_ROWS), loss])
    from_core, rep_all, last_all = share_totals(totals, pack_t, last_part)
    me = 2 * chip[0] + core[0]
    mine, theirs = pack_t[:HALF_SHARDED], from_core[4][:HALF_SHARDED]
    sh_total = jnp.where(mc == 0, jnp.concatenate([mine, theirs]), jnp.concatenate([theirs, mine]))
    rep_all = lax.dynamic_update_slice(rep_all, pack_t[None, HALF_SHARDED:], (me, 0, 0))
    rep_total = rep_all.reshape(REPLICATED_ROWS, LANES)
    last_total = sum_parts(lax.dynamic_update_slice(last_all, last_part[None], (me, 0, 0)), "grad_sum_last")
    last_total, loss = last_total[:LAST_ROWS], last_total[LAST_ROWS, 0]
    grads = {}

    delta, new_m, new_v = {}, {}, {}
    for i, n in enumerate(BIG):
        if shard[n].shape[1] % LANES:
            outs = adamw_halves(shard[n].T, totals[i].T, from_core[i].T, m_in[n].T, v_in[n].T, core, "adamw_" + n,
                                by_columns=True)
            grads[n], delta[n], new_m[n], new_v[n] = [o.T for o in outs]
        else:
            grads[n], delta[n], new_m[n], new_v[n] = adamw_halves(shard[n], totals[i], from_core[i], m_in[n],
                                                                  v_in[n], core, "adamw_" + n)
    gate = [src["rg_gate_w"].reshape(GATE_ROWS, LANES) for src in (shard, m_in, v_in)]
    grads["rg_gate_w"], delta["rg_gate_w"], new_m["rg_gate_w"], new_v["rg_gate_w"] = adamw(
        gate[0], rep_total, gate[1], gate[2], "adamw_rg_gate_w")
    rest = REPLICATED[1:]
    rest_rows = sum(_seg_rows(shape) for _, shape in rest)
    grads.update(_unpack(sh_total, SHARDED_SMALL))
    grads.update(_unpack(rep_total[GATE_ROWS:GATE_ROWS + rest_rows], rest))
    grads.update(_unpack(last_total, LAST_REPLICATED))
    names = [n for n, _ in SHARDED_SMALL + rest + LAST_REPLICATED]
    rows_of = lambda a, n: a.reshape(-1, given[n].shape[-1])
    outs = adamw_many([rows_of(given[n], n) for n in names], [rows_of(grads[n], n) for n in names],
                      [rows_of(given["m_" + n], n) for n in names], [rows_of(given["v_" + n], n) for n in names],
                      "adamw_small")
    for n, (d, nm, nv) in zip(names, outs):
        delta[n], new_m[n], new_v[n] = d, nm, nv
    result = [loss, grad_x[None]]
    for group in (grads, delta, new_m, new_v):
        result += [group[n].reshape(given[n].shape) for n in WEIGHTS]
    return tuple(result)
```

```python
import jax
import jax.numpy as jnp
from jax import lax
from jax.experimental import pallas as pl
from jax.experimental.pallas import tpu as pltpu

F32 = jnp.float32
BF16 = jnp.bfloat16
MESH = pl.DeviceIdType.MESH

D_MODEL = 1024
NORM_EPS = 1e-6
RG_HEADS = 8
RG_HEAD_DIM = 128
RG_C = 8.0
EVEN_IN = 6144
ODD_IN = 3104
ODD_IN_PAD = 3200
GLA_HEADS = 4
GLA_DK = 128
GLA_DV = 256
GLA_RANK = 16
GLA_NORMALIZER = 16.0
GLA_CHUNK = 128
LR_COL = 3072

ADAM_LR = 0.001
ADAM_B1 = 0.9
ADAM_B2 = 0.999
ADAM_EPS = 1e-08
ADAM_WD = 0.01
ADAM_STEP = 10

SUBLANES = 8
HALO = 16
LANES = 128
VMEM_LIMIT = 56 * 2 ** 20

ROW_TILE = 512
SCAN_TILE = 256
GLA_BLOCK = 2048
MIX_TILE = 256


def _params(*sem):
    return pltpu.CompilerParams(dimension_semantics=sem, vmem_limit_bytes=VMEM_LIMIT)


def _full(shape):
    n = len(shape)
    return pl.BlockSpec(shape, lambda *_: (0,) * n)


def _sigmoid(x):
    return 0.5 + 0.5 * jnp.tanh(0.5 * x)


def _softplus(x):
    return jnp.maximum(x, 0.0) + jnp.log(1.0 + jnp.exp(-jnp.abs(x)))


def _dot(a, b):
    return jnp.dot(a, b, preferred_element_type=F32)


def _dot_nt(a, b):
    return lax.dot_general(a, b, (((1,), (1,)), ((), ())), preferred_element_type=F32)


def _dot_tn(a, b):
    return lax.dot_general(a, b, (((0,), (0,)), ((), ())), preferred_element_type=F32)


def _bdot(a, b, ca, cb):
    return lax.dot_general(a, b, (((ca,), (cb,)), ((0,), (0,))), preferred_element_type=F32)


def _halo_specs(rows, cols, col_block, n_row_tiles, tix):
    per = rows // HALO
    last = n_row_tiles * per - 1

    def split(args):
        if len(args) == 2:
            return tix(args[1]), col_block + args[0]
        return tix(args[0]), col_block

    def prev(*args):
        t, c = split(args)
        return (jnp.maximum(t * per - 1, 0), c)

    def main(*args):
        return split(args)

    def nxt(*args):
        t, c = split(args)
        return (jnp.minimum((t + 1) * per, last), c)

    return [pl.BlockSpec((HALO, cols), prev), pl.BlockSpec((rows, cols), main),
            pl.BlockSpec((HALO, cols), nxt)]


def _extend(prev_ref, main_ref, next_ref, is_first, is_last):
    p = jnp.where(is_first, 0.0, prev_ref[...].astype(F32))
    n = jnp.where(is_last, 0.0, next_ref[...].astype(F32))
    return jnp.concatenate([p, main_ref[...].astype(F32), n], axis=0)


def _shifted(ext, offset, rows):
    if offset == 0:
        return ext[HALO:HALO + rows]
    n = ext.shape[0]
    return pltpu.roll(ext, (-offset) % n, 0)[HALO:HALO + rows]


def _conv(ext, w, left, rows):
    out = None
    for k in range(w.shape[0]):
        term = _shifted(ext, k - left, rows) * w[k:k + 1]
        out = term if out is None else out + term
    return out


def _conv_transpose(ext, w, left, rows):
    out = None
    for k in range(w.shape[0]):
        term = _shifted(ext, left - k, rows) * w[k:k + 1]
        out = term if out is None else out + term
    return out


def _colsum(x):
    return jnp.sum(x, axis=0, keepdims=True)


def _accumulate(ref, value, step):
    @pl.when(step == 0)
    def _():
        ref[...] = value

    @pl.when(step > 0)
    def _():
        ref[...] += value


PROJ_TILE_BYTES = 7 * 2 ** 20


def _proj_row_tile(rows, width, dtype):
    tm = min(ROW_TILE, rows)
    while tm * width * jnp.dtype(dtype).itemsize > PROJ_TILE_BYTES and tm % (2 * HALO) == 0:
        tm //= 2
    return tm


def norm_matmul(x, gain, w, out_dtype, name):
    rows, d = x.shape
    n_col_tiles, _, tn = w.shape
    tm = _proj_row_tile(rows, n_col_tiles * tn, out_dtype)

    def body(x_ref, g_ref, w_ref, proj_ref, h_ref):
        xv = x_ref[...]
        rstd = lax.rsqrt(jnp.mean(xv * xv, axis=-1, keepdims=True) + NORM_EPS)
        hv = (xv * rstd * g_ref[...]).astype(BF16)
        h_ref[...] = hv
        for j in range(n_col_tiles):
            proj_ref[:, j * tn:(j + 1) * tn] = _dot(hv, w_ref[j]).astype(out_dtype)

    row = lambda cols: pl.BlockSpec((tm, cols), lambda i: (i, 0))
    return pl.pallas_call(
        body, name=name,
        out_shape=(jax.ShapeDtypeStruct((rows, n_col_tiles * tn), out_dtype), jax.ShapeDtypeStruct((rows, d), BF16)),
        grid=(rows // tm,),
        in_specs=[row(d), _full((1, d)), _full(w.shape)],
        out_specs=(row(n_col_tiles * tn), row(d)),
        compiler_params=_params("parallel"),
    )(x, gain, w)


def inproj_bwd(dproj, w, x, gain, dres, name):
    rows, d = x.shape
    n_col_tiles, _, tn = w.shape
    tm = _proj_row_tile(rows, n_col_tiles * tn, dproj.dtype)

    def body(dp_ref, w_ref, x_ref, g_ref, dres_ref, dx_ref, dg_ref):
        dh = None
        for j in range(n_col_tiles):
            part = _dot_nt(dp_ref[:, j * tn:(j + 1) * tn], w_ref[j])
            dh = part if dh is None else dh + part
        _inproj_finish(dh, x_ref, g_ref, dres_ref, dx_ref, dg_ref, pl.program_id(0))

    row = lambda cols: pl.BlockSpec((tm, cols), lambda i: (i, 0))
    return pl.pallas_call(
        body, name=name,
        out_shape=(jax.ShapeDtypeStruct((rows, d), F32), jax.ShapeDtypeStruct((1, d), F32)),
        grid=(rows // tm,),
        in_specs=[row(n_col_tiles * tn), _full(w.shape), row(d), _full((1, d)), row(d)],
        out_specs=(row(d), _full((1, d))),
        compiler_params=_params("arbitrary"),
    )(dproj, w, x, gain, dres)


def _inproj_finish(dh, x_ref, g_ref, dres_ref, dx_ref, dg_ref, step):
    xv = x_ref[...]
    rstd = lax.rsqrt(jnp.mean(xv * xv, axis=-1, keepdims=True) + NORM_EPS)
    xhat = xv * rstd
    dxn = dh * g_ref[...]
    dx_ref[...] = dres_ref[...] + rstd * (dxn - xhat * jnp.mean(dxn * xhat, axis=-1, keepdims=True))
    _accumulate(dg_ref, _colsum(dh * xhat), step)


def inproj_bwd_pieces(pieces, w, x, gain, dres, name):
    rows, d = x.shape
    tm = min(ROW_TILE, rows)
    n = len(pieces)
    widths = [p.shape[1] for p in pieces]
    starts = [sum(widths[:k]) for k in range(n)]
    assert sum(widths) == w.shape[2]

    def body(*refs):
        w_ref, x_ref, g_ref, dres_ref, dx_ref, dg_ref = refs[n:]
        dh = None
        for k in range(n):
            part = _dot_nt(refs[k][...], w_ref[0, :, starts[k]:starts[k] + widths[k]])
            dh = part if dh is None else dh + part
        _inproj_finish(dh, x_ref, g_ref, dres_ref, dx_ref, dg_ref, pl.program_id(0))

    row = lambda cols: pl.BlockSpec((tm, cols), lambda i: (i, 0))
    return pl.pallas_call(
        body, name=name,
        out_shape=(jax.ShapeDtypeStruct((rows, d), F32), jax.ShapeDtypeStruct((1, d), F32)),
        grid=(rows // tm,),
        in_specs=[row(wd) for wd in widths] + [_full(w.shape), row(d), _full((1, d)), row(d)],
        out_specs=(row(d), _full((1, d))),
        compiler_params=_params("arbitrary"),
    )(*pieces, w, x, gain, dres)


def matmul_dw_pieces(a, pieces, name):
    rows, m = a.shape
    tk = min(2 * ROW_TILE, rows)
    n = len(pieces)

    def body(*refs):
        a_ref, ins, outs = refs[0], refs[1:1 + n], refs[1 + n:]
        av = a_ref[...]
        for k in range(n):
            _accumulate(outs[k], _dot_tn(av, ins[k][...]), pl.program_id(0))

    return pl.pallas_call(
        body, name=name,
        out_shape=[jax.ShapeDtypeStruct((m, p.shape[1]), F32) for p in pieces],
        grid=(rows // tk,),
        in_specs=[pl.BlockSpec((tk, m), lambda k: (k, 0))]
        + [pl.BlockSpec((tk, p.shape[1]), lambda k: (k, 0)) for p in pieces],
        out_specs=[_full((m, p.shape[1])) for p in pieces],
        compiler_params=_params("arbitrary"),
    )(a, *pieces)


def matmul_dw(a, b, bn, name):
    rows, m = a.shape
    n = b.shape[1]
    tk = min((4 if n > bn else 2) * ROW_TILE, rows)
    steps = rows // tk

    def body(a_ref, b_ref, o_ref, o16_ref):
        part = _dot_tn(a_ref[...], b_ref[...])

        @pl.when(pl.program_id(1) == 0)
        def _():
            o_ref[0] = part

        @pl.when(pl.program_id(1) > 0)
        def _():
            o_ref[0] += part

        @pl.when(pl.program_id(1) == steps - 1)
        def _():
            o16_ref[0] = o_ref[0].astype(BF16)

    out = pl.BlockSpec((1, m, bn), lambda j, k: (j, 0, 0))
    return pl.pallas_call(
        body, name=name,
        out_shape=(jax.ShapeDtypeStruct((n // bn, m, bn), F32), jax.ShapeDtypeStruct((n // bn, m, bn), BF16)),
        grid=(n // bn, steps),
        in_specs=[pl.BlockSpec((tk, m), lambda j, k: (k, 0)), pl.BlockSpec((tk, bn), lambda j, k: (k, j))],
        out_specs=(out, out),
        compiler_params=_params("parallel", "arbitrary"),
    )(a, b)


def _scan(a, b, carry, reverse):
    n, c = a.shape
    blocks = n // SUBLANES
    a = a.reshape(blocks, SUBLANES, c)
    b = b.reshape(blocks, SUBLANES, c)
    pos = lax.broadcasted_iota(jnp.int32, (1, SUBLANES, c), 1)
    s = 1
    while s < SUBLANES:
        shift, valid = (SUBLANES - s, pos < SUBLANES - s) if reverse else (s, pos >= s)
        a_s, b_s = pltpu.roll(a, shift, 1), pltpu.roll(b, shift, 1)
        b = jnp.where(valid, a * b_s + b, b)
        a = jnp.where(valid, a * a_s, a)
        s *= 2
    out = [None] * blocks
    for k in (range(blocks - 1, -1, -1) if reverse else range(blocks)):
        h = a[k] * carry + b[k]
        out[k] = h
        carry = h[0:1] if reverse else h[SUBLANES - 1:SUBLANES]
    return jnp.concatenate(out, axis=0)


def _rg_gates(ua, gw_ref, gb, lam):
    ub = ua.astype(BF16)
    pre_r, pre_i = [], []
    for h in range(RG_HEADS):
        z = _dot(ub[:, h * RG_HEAD_DIM:(h + 1) * RG_HEAD_DIM], gw_ref[h])
        pre_r.append(z[:, :RG_HEAD_DIM])
        pre_i.append(z[:, RG_HEAD_DIM:])
    r = _sigmoid(jnp.concatenate(pre_r, axis=1) + gb[0:1])
    i = _sigmoid(jnp.concatenate(pre_i, axis=1) + gb[1:2])
    sp = _softplus(-lam)
    log_a = -RG_C * r * sp
    a = jnp.exp(log_a)
    mult = jnp.sqrt(1.0 - a * a)
    return r, i, sp, a, mult


def _rg_weight_specs():
    return [_full((4, D_MODEL)), _full((1, D_MODEL)), _full((RG_HEADS, RG_HEAD_DIM, 2 * RG_HEAD_DIM)),
            _full((2, D_MODEL)), _full((1, D_MODEL))]


def rglru_fwd(proj, conv_w, conv_b, gate_w, gate_b, lam, reverse, name):
    rows_total = proj.shape[0]
    rows = min(SCAN_TILE, rows_total)
    n_tiles = rows_total // rows
    tix = (lambda i: n_tiles - 1 - i) if reverse else (lambda i: i)

    def body(xp, xm, xn, cw_ref, cb_ref, gw_ref, gb_ref, lam_ref, h_ref, acts_ref, carry):
        i = pl.program_id(0)
        t = tix(i)
        ext = _extend(xp, xm, xn, t == 0, t == n_tiles - 1)
        ua = _conv(ext, cw_ref[...], 2, rows) + cb_ref[...]
        r, gi, _, a, mult = _rg_gates(ua, gw_ref, gb_ref[...], lam_ref[...])
        for k, saved in enumerate((ua, r, gi, a, mult)):
            acts_ref[k] = saved
        b = mult * (gi * ua)

        @pl.when(i == 0)
        def _():
            carry[...] = jnp.zeros_like(carry)

        h = _scan(a, b, carry[0:1], reverse)
        h_ref[...] = h
        edge = h[0:1] if reverse else h[rows - 1:rows]
        carry[...] = jnp.broadcast_to(edge, carry.shape)

    return pl.pallas_call(
        body, name=name,
        out_shape=(jax.ShapeDtypeStruct((rows_total, D_MODEL), F32),
                   jax.ShapeDtypeStruct((5, rows_total, D_MODEL), F32)),
        grid=(n_tiles,),
        in_specs=_halo_specs(rows, D_MODEL, 0, n_tiles, tix) + _rg_weight_specs(),
        out_specs=(pl.BlockSpec((rows, D_MODEL), lambda i: (tix(i), 0)),
                   pl.BlockSpec((5, rows, D_MODEL), lambda i: (0, tix(i), 0))),
        scratch_shapes=[pltpu.VMEM((SUBLANES, D_MODEL), F32)],
        compiler_params=_params("arbitrary"),
    )(proj, proj, proj, conv_w, conv_b, gate_w, gate_b, lam)


def rglru_bwd(proj, dycat, h_dir, acts, gate_w, lam, add_dua, reverse, name):
    rows_total = proj.shape[0]
    rows = min(SCAN_TILE, rows_total)
    n_tiles = rows_total // rows
    tix = (lambda i: i) if reverse else (lambda i: n_tiles - 1 - i)
    za_block = 1

    def body(acts_ref, za_ref, dya_ref, hp, hm, hn, gw_ref, lam_ref, *rest):
        other = rest[0][...] if add_dua is not None else 0.0
        dua_ref, dgw_ref, dgb_ref, dlam_ref, carry = rest[-5:]
        step = pl.program_id(0)
        t = tix(step)
        first, last = t == 0, t == n_tiles - 1
        ua, r, gi, a, mult = (acts_ref[k] for k in range(5))
        lam_v = lam_ref[...]
        sp = _softplus(-lam_v)
        za = za_ref[...].astype(F32)
        dh = dya_ref[...] * (za * _sigmoid(za))

        @pl.when(step == 0)
        def _():
            carry[...] = jnp.zeros_like(carry)

        old = carry[0:1]
        mu = _scan(a, a * dh, old, not reverse)
        row = lax.broadcasted_iota(jnp.int32, mu.shape, 0)
        if reverse:
            mu_next = jnp.where(row == 0, old, pltpu.roll(mu, 1, 0))
            carry[...] = jnp.broadcast_to(mu[rows - 1:rows], carry.shape)
            h_ext = _extend(hp, hm, hn, first, last)
            h_prev = _shifted(h_ext, 1, rows)
        else:
            mu_next = jnp.where(row == rows - 1, old, pltpu.roll(mu, rows - 1, 0))
            carry[...] = jnp.broadcast_to(mu[0:1], carry.shape)
            h_ext = _extend(hp, hm, hn, first, last)
            h_prev = _shifted(h_ext, -1, rows)
        db = dh + mu_next
        da = db * h_prev
        d_mult = db * (gi * ua)
        di = db * (mult * ua)
        dua = db * (mult * gi)
        dlog_a = da * a - d_mult * (a * a) / mult
        dr = dlog_a * (-RG_C * sp)
        dlam = _colsum(dlog_a * (-RG_C * r)) * (-_sigmoid(-lam_v))
        dpr = dr * (r * (1.0 - r))
        dpi = di * (gi * (1.0 - gi))
        dgb = jnp.concatenate([_colsum(dpr), _colsum(dpi)], axis=0)
        ub = ua.astype(BF16)
        dua_heads, dgw_heads = [], []
        for h in range(RG_HEADS):
            cols = slice(h * RG_HEAD_DIM, (h + 1) * RG_HEAD_DIM)
            dz = jnp.concatenate([dpr[:, cols], dpi[:, cols]], axis=1).astype(BF16)
            dgw_heads.append(_dot_tn(ub[:, cols], dz))
            dua_heads.append(_dot_nt(dz, gw_ref[h]))
        dua_ref[...] = dua + jnp.concatenate(dua_heads, axis=1) + other

        @pl.when(step == 0)
        def _():
            for h in range(RG_HEADS):
                dgw_ref[h] = dgw_heads[h]
            dgb_ref[...] = dgb
            dlam_ref[...] = dlam

        @pl.when(step > 0)
        def _():
            for h in range(RG_HEADS):
                dgw_ref[h] += dgw_heads[h]
            dgb_ref[...] += dgb
            dlam_ref[...] += dlam

    row_spec = lambda col: pl.BlockSpec((rows, D_MODEL), lambda i: (tix(i), col))
    return pl.pallas_call(
        body, name=name,
        out_shape=(jax.ShapeDtypeStruct((rows_total, D_MODEL), F32),
                   jax.ShapeDtypeStruct((RG_HEADS, RG_HEAD_DIM, 2 * RG_HEAD_DIM), F32),
                   jax.ShapeDtypeStruct((2, D_MODEL), F32), jax.ShapeDtypeStruct((1, D_MODEL), F32)),
        grid=(n_tiles,),
        in_specs=([pl.BlockSpec((5, rows, D_MODEL), lambda i: (0, tix(i), 0)), row_spec(za_block), row_spec(0)]
                  + _halo_specs(rows, D_MODEL, 0, n_tiles, tix)
                  + [_full((RG_HEADS, RG_HEAD_DIM, 2 * RG_HEAD_DIM)), _full((1, D_MODEL))]
                  + ([] if add_dua is None else [row_spec(0)])),
        out_specs=(row_spec(0), _full((RG_HEADS, RG_HEAD_DIM, 2 * RG_HEAD_DIM)), _full((2, D_MODEL)),
                   _full((1, D_MODEL))),
        scratch_shapes=[pltpu.VMEM((SUBLANES, D_MODEL), F32)],
        compiler_params=_params("arbitrary"),
    )(acts, proj, dycat, h_dir, h_dir, h_dir, gate_w, lam, *([] if add_dua is None else [add_dua]))


def _extend_cols(refs, block, is_first, is_last):
    cols = slice(block * D_MODEL, (block + 1) * D_MODEL)
    prev_ref, main_ref, next_ref = refs
    p = jnp.where(is_first, 0.0, prev_ref[:, cols].astype(F32))
    n = jnp.where(is_last, 0.0, next_ref[:, cols].astype(F32))
    return jnp.concatenate([p, main_ref[:, cols].astype(F32), n], axis=0)


def even_mix_fwd(proj, h_f, h_b, sc_w, name):
    rows_total = proj.shape[0]
    rows = min(2 * MIX_TILE, rows_total)
    n_tiles = rows_total // rows
    ident = lambda i: i

    def body(za_ref, hf_ref, hb_ref, xbp, xbm, xbn, gcp, gcm, gcn, gb_ref, zb_ref, w_ref, y_ref):
        t = pl.program_id(0)
        first, last = t == 0, t == n_tiles - 1
        za = za_ref[...].astype(F32)
        y_ref[:, 0:D_MODEL] = ((hf_ref[...] + hb_ref[...]) * (za * _sigmoid(za))).astype(BF16)
        p_ext = _extend(xbp, xbm, xbn, first, last) * _extend(gcp, gcm, gcn, first, last)
        cv = _conv(p_ext, w_ref[...], 1, rows)
        zb = zb_ref[...].astype(F32)
        y_ref[:, D_MODEL:2 * D_MODEL] = (gb_ref[...].astype(F32) * cv * (zb * _sigmoid(zb))).astype(BF16)

    blk = lambda col: pl.BlockSpec((rows, D_MODEL), lambda i: (i, col))
    return pl.pallas_call(
        body, name=name,
        out_shape=jax.ShapeDtypeStruct((rows_total, 2 * D_MODEL), BF16),
        grid=(n_tiles,),
        in_specs=([blk(1), blk(0), blk(0)] + _halo_specs(rows, D_MODEL, 2, n_tiles, ident)
                  + _halo_specs(rows, D_MODEL, 4, n_tiles, ident) + [blk(3), blk(5), _full((3, D_MODEL))]),
        out_specs=pl.BlockSpec((rows, 2 * D_MODEL), lambda i: (i, 0)),
        compiler_params=_params("parallel"),
    )(proj, h_f, h_b, proj, proj, proj, proj, proj, proj, proj, proj, sc_w)


def even_mix_bwd(proj, dycat, h_f, h_b, dua, conv_w, sc_w, name):
    rows_total, width = proj.shape
    rows = min(MIX_TILE, rows_total)
    n_tiles = rows_total // rows
    ident = lambda i: i

    def body(pp, pm, pn, dyp, dym, dyn, hf_ref, hb_ref, dup, dum, dun, cw_ref, sw_ref,
             dp_ref, dcw_ref, dcb_ref, dsw_ref):
        def put(k, value):
            dp_ref[:, k * D_MODEL:(k + 1) * D_MODEL] = value.astype(BF16)

        t = pl.program_id(0)
        first, last = t == 0, t == n_tiles - 1
        proj_ext = lambda k: _extend_cols((pp, pm, pn), k, first, last)
        mid = slice(HALO, HALO + rows)
        za = pm[:, D_MODEL:2 * D_MODEL].astype(F32)
        sa = _sigmoid(za)
        put(1, dym[:, 0:D_MODEL] * (hf_ref[...] + hb_ref[...]) * (sa * (1.0 + za * (1.0 - sa))))
        dua_ext = _extend(dup, dum, dun, first, last)
        cw = cw_ref[...]
        put(0, _conv_transpose(dua_ext, cw, 2, rows))
        dua_mid = dua_ext[mid]
        xa_ext = proj_ext(0)
        dcw = jnp.concatenate([_colsum(dua_mid * _shifted(xa_ext, k - 2, rows)) for k in range(4)], axis=0)
        dcb = _colsum(dua_mid)
        xb_ext, gb_ext, gc_ext, zb_ext = proj_ext(2), proj_ext(3), proj_ext(4), proj_ext(5)
        p_ext = xb_ext * gc_ext
        sb_ext = _sigmoid(zb_ext)
        dyb_ext = _extend_cols((dyp, dym, dyn), 1, first, last)
        dcv_ext = dyb_ext * gb_ext * (zb_ext * sb_ext)
        sw = sw_ref[...]
        p_at = [_shifted(p_ext, k - 1, rows) for k in range(3)]
        cv = (p_at[0] * sw[0:1] + p_at[1] * sw[1:2]) + p_at[2] * sw[2:3]
        zb, sb, dyb, gb = zb_ext[mid], sb_ext[mid], dyb_ext[mid], gb_ext[mid]
        put(3, dyb * cv * (zb * sb))
        put(5, dyb * gb * cv * (sb * (1.0 + zb * (1.0 - sb))))
        dp = _conv_transpose(dcv_ext, sw, 1, rows)
        put(4, dp * xb_ext[mid])
        put(2, dp * gc_ext[mid])
        dcv = dcv_ext[mid]
        dsw = jnp.concatenate([_colsum(dcv * p_at[k]) for k in range(3)], axis=0)
        _accumulate(dcw_ref, dcw, t)
        _accumulate(dcb_ref, dcb, t)
        _accumulate(dsw_ref, dsw, t)

    own = pl.BlockSpec((rows, D_MODEL), lambda i: (i, 0))
    return pl.pallas_call(
        body, name=name,
        out_shape=(jax.ShapeDtypeStruct((rows_total, 6 * D_MODEL), BF16),
                   jax.ShapeDtypeStruct((4, D_MODEL), F32), jax.ShapeDtypeStruct((1, D_MODEL), F32),
                   jax.ShapeDtypeStruct((3, D_MODEL), F32)),
        grid=(n_tiles,),
        in_specs=(_halo_specs(rows, width, 0, n_tiles, ident) + _halo_specs(rows, 2 * D_MODEL, 0, n_tiles, ident)
                  + [own, own] + _halo_specs(rows, D_MODEL, 0, n_tiles, ident)
                  + [_full((4, D_MODEL)), _full((3, D_MODEL))]),
        out_specs=(pl.BlockSpec((rows, 6 * D_MODEL), lambda i: (i, 0)), _full((4, D_MODEL)), _full((1, D_MODEL)),
                   _full((3, D_MODEL))),
        compiler_params=_params("arbitrary"),
    )(proj, proj, proj, dycat, dycat, dycat, h_f, h_b, dua, dua, dua, conv_w, sc_w)


def even_out_fwd(ycat, w_out, gain, x, name):
    rows, d = x.shape
    k = ycat.shape[1]
    tm = min(ROW_TILE, rows)

    def body(yc_ref, w_ref, g_ref, x_ref, x1_ref, y_ref):
        y = _dot(yc_ref[...], w_ref[...])
        y_ref[...] = y
        rstd = lax.rsqrt(jnp.mean(y * y, axis=-1, keepdims=True) + NORM_EPS)
        x1_ref[...] = x_ref[...] + y * rstd * g_ref[...]

    row = lambda n: pl.BlockSpec((tm, n), lambda i: (i, 0))
    return pl.pallas_call(
        body, name=name,
        out_shape=(jax.ShapeDtypeStruct((rows, d), F32),) * 2,
        grid=(rows // tm,),
        in_specs=[row(k), _full((k, d)), _full((1, d)), row(d)],
        out_specs=(row(d), row(d)),
        compiler_params=_params("parallel"),
    )(ycat, w_out, gain, x)


def _rmsnorm_bwd(dout, y, gain):
    rstd = lax.rsqrt(jnp.mean(y * y, axis=-1, keepdims=True) + NORM_EPS)
    yhat = y * rstd
    dyn = dout * gain
    dy = rstd * (dyn - yhat * jnp.mean(dyn * yhat, axis=-1, keepdims=True))
    return dy, dout * yhat


def even_out_bwd(dx1, y, gain, w_out, name):
    rows, d = y.shape
    k = w_out.shape[0]
    tm = min(ROW_TILE, rows)

    def body(dx_ref, y_ref, g_ref, w_ref, dy_ref, dyc_ref, dg_ref):
        dy, dg_rows = _rmsnorm_bwd(dx_ref[...], y_ref[...], g_ref[...])
        dyb = dy.astype(BF16)
        dy_ref[...] = dyb
        dyc_ref[...] = _dot_nt(dyb, w_ref[...])
        _accumulate(dg_ref, _colsum(dg_rows), pl.program_id(0))

    row = lambda n: pl.BlockSpec((tm, n), lambda i: (i, 0))
    return pl.pallas_call(
        body, name=name,
        out_shape=(jax.ShapeDtypeStruct((rows, d), BF16), jax.ShapeDtypeStruct((rows, k), F32),
                   jax.ShapeDtypeStruct((1, d), F32)),
        grid=(rows // tm,),
        in_specs=[row(d), row(d), _full((1, d)), _full((k, d))],
        out_specs=(row(d), row(k), _full((1, d))),
        compiler_params=_params("arbitrary"),
    )(dx1, y, gain, w_out)


def _chunk_cumsum(g, reverse):
    n, c = g.shape
    chunks, per = n // GLA_CHUNK, GLA_CHUNK // SUBLANES
    g = g.reshape(n // SUBLANES, SUBLANES, c)
    pos = lax.broadcasted_iota(jnp.int32, (1, SUBLANES, c), 1)
    s = 1
    while s < SUBLANES:
        if reverse:
            g = g + jnp.where(pos < SUBLANES - s, pltpu.roll(g, SUBLANES - s, 1), 0.0)
        else:
            g = g + jnp.where(pos >= s, pltpu.roll(g, s, 1), 0.0)
        s *= 2
    g = g.reshape(chunks, per, SUBLANES, c)
    out, carry = [None] * per, None
    for k in (range(per - 1, -1, -1) if reverse else range(per)):
        out[k] = g[:, k] if carry is None else g[:, k] + carry
        carry = out[k][:, 0:1] if reverse else out[k][:, SUBLANES - 1:SUBLANES]
    return jnp.stack(out, axis=1).reshape(n, c)


def _gla_prepare(q_ref, k_ref, lr_ref, wg_ref, bg_ref, reverse, n_chunks):
    z = _dot(lr_ref[...].astype(BF16), wg_ref[0]) + bg_ref[0]
    g = -_softplus(-z) * (1.0 / GLA_NORMALIZER)
    bcum = _chunk_cumsum(g, reverse).reshape(n_chunks, GLA_CHUNK, GLA_DK)
    edge = 0 if reverse else GLA_CHUNK - 1
    btot = bcum[:, edge:edge + 1, :]
    e_pos = jnp.exp(bcum)
    e_neg = jnp.exp(-bcum)
    e_st = jnp.exp(btot - bcum)
    q3 = q_ref[...].reshape(n_chunks, GLA_CHUNK, GLA_DK)
    k3 = k_ref[...].reshape(n_chunks, GLA_CHUNK, GLA_DK)
    scale = GLA_DK ** -0.5
    q_in = q3 * scale * e_pos
    k_in = k3 * e_neg
    k_st = k3 * e_st
    dec = jnp.exp(btot)
    return z, q_in, k_in, k_st, dec, (scale * e_pos, e_neg, e_st)


def _gla_mask(reverse):
    i = lax.broadcasted_iota(jnp.int32, (GLA_CHUNK, GLA_CHUNK), 0)
    j = lax.broadcasted_iota(jnp.int32, (GLA_CHUNK, GLA_CHUNK), 1)
    return (j >= i) if reverse else (j <= i)


def _gla_specs(rows, n_blocks, reverse):
    tix = (lambda s: n_blocks - 1 - s) if reverse else (lambda s: s)
    d = 1 if reverse else 0
    lr_block = LR_COL // LANES
    specs = [pl.BlockSpec((rows, GLA_DK), lambda h, s: (tix(s), h)),
             pl.BlockSpec((rows, GLA_DK), lambda h, s: (tix(s), GLA_HEADS + h)),
             pl.BlockSpec((rows, GLA_DV), lambda h, s: (tix(s), GLA_HEADS + h)),
             pl.BlockSpec((rows, LANES), lambda h, s: (tix(s), lr_block)),
             pl.BlockSpec((1, LANES, GLA_DK), lambda h, s: (d, 0, h)),
             pl.BlockSpec((1, 1, GLA_DK), lambda h, s: (d, 0, h))]
    return specs, tix


def gla_fwd(proj, wg_pad, bg, add_o, reverse, name):
    rows_total = proj.shape[0]
    rows = min(GLA_BLOCK, rows_total)
    n_blocks = rows_total // rows
    n_chunks = rows // GLA_CHUNK
    specs, tix = _gla_specs(rows, n_blocks, reverse)

    def body(q_ref, k_ref, v_ref, lr_ref, wg_ref, bg_ref, *rest):
        o_ref, st_ref, state, kv_scr, dec_scr = rest[-5:]
        _, q_in, k_in, k_st, dec, _ = _gla_prepare(q_ref, k_ref, lr_ref, wg_ref, bg_ref, reverse, n_chunks)
        vb = v_ref[...].reshape(n_chunks, GLA_CHUNK, GLA_DV).astype(BF16)
        qb = q_in.astype(BF16)
        p = jnp.where(_gla_mask(reverse), _bdot(qb, k_in.astype(BF16), 2, 2), 0.0)
        o = _bdot(p.astype(BF16), vb, 2, 1)
        kv_scr[...] = _bdot(vb, k_st.astype(BF16), 1, 1)
        dec_scr[...] = jnp.broadcast_to(dec, dec_scr.shape)

        @pl.when(pl.program_id(1) == 0)
        def _():
            state[...] = jnp.zeros_like(state)

        for c in range(n_chunks):
            cc = n_chunks - 1 - c if reverse else c
            st_ref[0, cc] = state[...]
            state[...] = state[...] * dec_scr[cc, 0:1] + kv_scr[cc]
        o = o + _bdot(qb, st_ref[0].astype(BF16), 2, 2)
        o = o.reshape(rows, GLA_DV)
        o_ref[...] = o if add_o is None else o + rest[0][...]

    o_spec = pl.BlockSpec((rows, GLA_DV), lambda h, s: (tix(s), h))
    return pl.pallas_call(
        body, name=name,
        out_shape=(jax.ShapeDtypeStruct((rows_total, GLA_HEADS * GLA_DV), F32),
                   jax.ShapeDtypeStruct((GLA_HEADS, rows_total // GLA_CHUNK, GLA_DV, GLA_DK), F32)),
        grid=(GLA_HEADS, n_blocks),
        in_specs=specs + ([] if add_o is None else [o_spec]),
        out_specs=(o_spec,
                   pl.BlockSpec((1, n_chunks, GLA_DV, GLA_DK), lambda h, s: (h, tix(s), 0, 0))),
        scratch_shapes=[pltpu.VMEM((GLA_DV, GLA_DK), F32), pltpu.VMEM((n_chunks, GLA_DV, GLA_DK), F32),
                        pltpu.VMEM((n_chunks, SUBLANES, GLA_DK), F32)],
        compiler_params=_params("parallel", "arbitrary"),
    )(proj, proj, proj, proj, wg_pad, bg, *([] if add_o is None else [add_o]))


def gla_bwd(proj, wg_pad, bg, d_o, states, dqkv_in, reverse, name):
    rows_total = proj.shape[0]
    rows = min(GLA_BLOCK, rows_total)
    n_blocks = rows_total // rows
    n_chunks = rows // GLA_CHUNK
    specs, tix = _gla_specs(rows, n_blocks, not reverse)
    d = 1 if reverse else 0
    specs[4] = pl.BlockSpec((1, LANES, GLA_DK), lambda h, s: (d, 0, h))
    specs[5] = pl.BlockSpec((1, 1, GLA_DK), lambda h, s: (d, 0, h))
    add = dqkv_in is not None

    def body(*refs):
        q_ref, k_ref, v_ref, lr_ref, wg_ref, bg_ref, do_ref, st_ref = refs[:8]
        refs = refs[8:]
        if add:
            aq_ref, ak_ref, av_ref = refs[:3]
            refs = refs[3:]
        dq_ref, dk_ref, dv_ref, dz_ref, dstate, g_scr, dec_scr, dsn_scr = refs
        z, q_in, k_in, k_st, dec, (f_q, f_k, f_s) = _gla_prepare(q_ref, k_ref, lr_ref, wg_ref, bg_ref, reverse,
                                                                 n_chunks)
        mask = _gla_mask(reverse)
        vb = v_ref[...].reshape(n_chunks, GLA_CHUNK, GLA_DV).astype(BF16)
        dob = do_ref[...].reshape(n_chunks, GLA_CHUNK, GLA_DV).astype(BF16)
        qb, kb, ksb = q_in.astype(BF16), k_in.astype(BF16), k_st.astype(BF16)
        st = st_ref[0]
        stb = st.astype(BF16)
        pb = jnp.where(mask, _bdot(qb, kb, 2, 2), 0.0).astype(BF16)
        dpb = jnp.where(mask, _bdot(dob, vb, 2, 2), 0.0).astype(BF16)
        d_qin = _bdot(dpb, kb, 2, 1) + _bdot(dob, stb, 2, 1)
        d_kin = _bdot(dpb, qb, 1, 1)
        dv = _bdot(pb, dob, 1, 1)
        g_scr[...] = _bdot(dob, qb, 1, 1)
        dec_scr[...] = jnp.broadcast_to(dec, dec_scr.shape)

        @pl.when(pl.program_id(1) == 0)
        def _():
            dstate[...] = jnp.zeros_like(dstate)

        for c in range(n_chunks):
            cc = c if reverse else n_chunks - 1 - c
            dsn_scr[cc] = dstate[...]
            dstate[...] = dstate[...] * dec_scr[cc, 0:1] + g_scr[cc]
        dsn = dsn_scr[...]
        dsnb = dsn.astype(BF16)
        dv = dv + _bdot(ksb, dsnb, 2, 2)
        d_kst = _bdot(vb, dsnb, 2, 1)
        d_dec = jnp.sum(dsn * st, axis=1, keepdims=True)
        ks_term = d_kst * k_st
        d_btot = d_dec * dec + jnp.sum(ks_term, axis=1, keepdims=True)
        d_b = d_qin * q_in - d_kin * k_in - ks_term
        pos = lax.broadcasted_iota(jnp.int32, d_b.shape, 1)
        edge = 0 if reverse else GLA_CHUNK - 1
        d_b = d_b + jnp.where(pos == edge, d_btot, 0.0)
        dg = _chunk_cumsum(d_b.reshape(rows, GLA_DK), not reverse)
        dz_ref[...] = dg * (1.0 / GLA_NORMALIZER) * _sigmoid(-z)
        dq = (d_qin * f_q).reshape(rows, GLA_DK)
        dk = (d_kin * f_k + d_kst * f_s).reshape(rows, GLA_DK)
        dv = dv.reshape(rows, GLA_DV)
        if add:
            dq_ref[...] = (dq + aq_ref[...]).astype(BF16)
            dk_ref[...] = (dk + ak_ref[...]).astype(BF16)
            dv_ref[...] = (dv + av_ref[...]).astype(BF16)
        else:
            dq_ref[...] = dq
            dk_ref[...] = dk
            dv_ref[...] = dv

    qkv_specs = [pl.BlockSpec((rows, GLA_DK), lambda h, s: (tix(s), h)),
                 pl.BlockSpec((rows, GLA_DK), lambda h, s: (tix(s), h)),
                 pl.BlockSpec((rows, GLA_DV), lambda h, s: (tix(s), h))]
    in_specs = specs + [pl.BlockSpec((rows, GLA_DV), lambda h, s: (tix(s), h)),
                        pl.BlockSpec((1, n_chunks, GLA_DV, GLA_DK), lambda h, s: (h, tix(s), 0, 0))]
    args = [proj, proj, proj, proj, wg_pad, bg, d_o, states]
    out_dtype = F32
    if add:
        in_specs += qkv_specs
        args += list(dqkv_in)
        out_dtype = BF16
    return pl.pallas_call(
        body, name=name,
        out_shape=(jax.ShapeDtypeStruct((rows_total, GLA_HEADS * GLA_DK), out_dtype),
                   jax.ShapeDtypeStruct((rows_total, GLA_HEADS * GLA_DK), out_dtype),
                   jax.ShapeDtypeStruct((rows_total, GLA_HEADS * GLA_DV), out_dtype),
                   jax.ShapeDtypeStruct((rows_total, GLA_HEADS * GLA_DK), F32)),
        grid=(GLA_HEADS, n_blocks),
        in_specs=in_specs,
        out_specs=(pl.BlockSpec((rows, GLA_DK), lambda h, s: (tix(s), h)),
                   pl.BlockSpec((rows, GLA_DK), lambda h, s: (tix(s), h)),
                   pl.BlockSpec((rows, GLA_DV), lambda h, s: (tix(s), h)),
                   pl.BlockSpec((rows, GLA_DK), lambda h, s: (tix(s), h))),
        scratch_shapes=[pltpu.VMEM((GLA_DV, GLA_DK), F32), pltpu.VMEM((n_chunks, GLA_DV, GLA_DK), F32),
                        pltpu.VMEM((n_chunks, SUBLANES, GLA_DK), F32),
                        pltpu.VMEM((n_chunks, GLA_DV, GLA_DK), F32)],
        compiler_params=_params("parallel", "arbitrary"),
    )(*args)


def gla_gate_bwd(proj, dz_f, dz_b, wg_pad, name):
    rows_total = proj.shape[0]
    tm = min(ROW_TILE, rows_total)
    n_key = GLA_HEADS * GLA_DK

    def body(lr_ref, dzf_ref, dzb_ref, wg_ref, dlr_ref, dwg_ref, dbg_ref):
        step = pl.program_id(0)
        lr_t = jnp.transpose(lr_ref[...])
        dzf, dzb = dzf_ref[...], dzb_ref[...]
        dzf16, dzb16 = dzf.astype(BF16), dzb.astype(BF16)
        dlr_ref[...] = (_dot_nt(dzf16, wg_ref[0]) + _dot_nt(dzb16, wg_ref[1])).astype(BF16)
        dwf = _dot(lr_t[0:GLA_RANK].astype(BF16), dzf16)
        dwb = _dot(lr_t[GLA_RANK:2 * GLA_RANK].astype(BF16), dzb16)
        dbg = jnp.concatenate([_colsum(dzf), _colsum(dzb)], axis=0)

        @pl.when(step == 0)
        def _():
            dwg_ref[0] = dwf
            dwg_ref[1] = dwb
            dbg_ref[...] = dbg

        @pl.when(step > 0)
        def _():
            dwg_ref[0] += dwf
            dwg_ref[1] += dwb
            dbg_ref[...] += dbg

    return pl.pallas_call(
        body, name=name,
        out_shape=(jax.ShapeDtypeStruct((rows_total, LANES), BF16), jax.ShapeDtypeStruct((2, GLA_RANK, n_key), F32),
                   jax.ShapeDtypeStruct((2, n_key), F32)),
        grid=(rows_total // tm,),
        in_specs=[pl.BlockSpec((tm, LANES), lambda i: (i, LR_COL // LANES)),
                  pl.BlockSpec((tm, n_key), lambda i: (i, 0)), pl.BlockSpec((tm, n_key), lambda i: (i, 0)),
                  _full((2, LANES, n_key))],
        out_specs=(pl.BlockSpec((tm, LANES), lambda i: (i, 0)), _full((2, GLA_RANK, n_key)), _full((2, n_key))),
        compiler_params=_params("arbitrary"),
    )(proj, dz_f, dz_b, wg_pad)


def _head_norm(o, gain):
    outs, hats, rstds = [], [], []
    for h in range(GLA_HEADS):
        oh = o[:, h * GLA_DV:(h + 1) * GLA_DV]
        rstd = lax.rsqrt(jnp.mean(oh * oh, axis=-1, keepdims=True) + NORM_EPS)
        hat = oh * rstd
        outs.append(hat * gain)
        hats.append(hat)
        rstds.append(rstd)
    return outs, hats, rstds


def odd_out_fwd(o, proj, head_gain, w_out, gain, x1, target, name):
    rows, d = x1.shape
    tm = min(ROW_TILE, rows)
    r_block = (2 * GLA_HEADS * GLA_DK + GLA_HEADS * GLA_DV) // d

    def body(o_ref, r_ref, hg_ref, w_ref, g_ref, x1_ref, tgt_ref, y2_ref, dy_ref, dx2_ref, loss_ref, dg_ref):
        step = pl.program_id(0)
        on, _, _ = _head_norm(o_ref[...], hg_ref[...])
        r = r_ref[...]
        y2 = (jnp.concatenate(on, axis=1) * (r * _sigmoid(r))).astype(BF16)
        y2_ref[...] = y2
        y = _dot(y2, w_ref[...])
        gain_v = g_ref[...]
        rstd = lax.rsqrt(jnp.mean(y * y, axis=-1, keepdims=True) + NORM_EPS)
        x2 = x1_ref[...] + y * rstd * gain_v
        diff = x2 - tgt_ref[...]
        loss = 0.5 * jnp.sum(jnp.mean(diff * diff, axis=-1, keepdims=True), axis=0, keepdims=True)
        dx2 = diff * (1.0 / d)
        dx2_ref[...] = dx2
        dy, dg_rows = _rmsnorm_bwd(dx2, y, gain_v)
        dy_ref[...] = dy.astype(BF16)
        _accumulate(loss_ref, jnp.broadcast_to(loss, loss_ref.shape), step)
        _accumulate(dg_ref, _colsum(dg_rows), step)

    row = lambda n, col=0: pl.BlockSpec((tm, n), lambda i: (i, col))
    return pl.pallas_call(
        body, name=name,
        out_shape=(jax.ShapeDtypeStruct((rows, d), BF16), jax.ShapeDtypeStruct((rows, d), BF16),
                   jax.ShapeDtypeStruct((rows, d), F32), jax.ShapeDtypeStruct((SUBLANES, LANES), F32),
                   jax.ShapeDtypeStruct((1, d), F32)),
        grid=(rows // tm,),
        in_specs=[row(d), row(d, r_block), _full((1, GLA_DV)), _full((d, d)), _full((1, d)), row(d), row(d)],
        out_specs=(row(d), row(d), row(d), _full((SUBLANES, LANES)), _full((1, d))),
        compiler_params=_params("arbitrary"),
    )(o, proj, head_gain, w_out, gain, x1, target)


def odd_out_bwd(dy, w_out, o, proj, head_gain, name):
    rows, d = dy.shape
    tm = min(ROW_TILE, rows)
    r_block = (2 * GLA_HEADS * GLA_DK + GLA_HEADS * GLA_DV) // d

    def body(dy_ref, w_ref, o_ref, r_ref, hg_ref, dr_ref, do_ref, dhg_ref):
        dy2 = _dot_nt(dy_ref[...], w_ref[...])
        hg = hg_ref[...]
        on, hats, rstds = _head_norm(o_ref[...], hg)
        r = r_ref[...]
        sr = _sigmoid(r)
        dr_ref[...] = (dy2 * jnp.concatenate(on, axis=1) * (sr * (1.0 + r * (1.0 - sr)))).astype(BF16)
        d_on = dy2 * (r * sr)
        d_os, dhg = [], None
        for h in range(GLA_HEADS):
            dn = d_on[:, h * GLA_DV:(h + 1) * GLA_DV]
            part = _colsum(dn * hats[h])
            dhg = part if dhg is None else dhg + part
            dng = dn * hg
            d_os.append(rstds[h] * (dng - hats[h] * jnp.mean(dng * hats[h], axis=-1, keepdims=True)))
        do_ref[...] = jnp.concatenate(d_os, axis=1)
        _accumulate(dhg_ref, dhg, pl.program_id(0))

    row = lambda n, col=0: pl.BlockSpec((tm, n), lambda i: (i, col))
    return pl.pallas_call(
        body, name=name,
        out_shape=(jax.ShapeDtypeStruct((rows, d), BF16), jax.ShapeDtypeStruct((rows, d), F32),
                   jax.ShapeDtypeStruct((1, GLA_DV), F32)),
        grid=(rows // tm,),
        in_specs=[row(d), _full((d, d)), row(d), row(d, r_block), _full((1, GLA_DV))],
        out_specs=(row(d), row(d), _full((1, GLA_DV))),
        compiler_params=_params("arbitrary"),
    )(dy, w_out, o, proj, head_gain)


def local_step(x, target, w, reduce_first=None, reduce_second=None, late_weights=None):
    g, g16 = {}, {}
    proj_e, h0 = norm_matmul(x, w["even_norm_pre"], w["even_w_in"], BF16, "even_in_proj")
    h_dir, acts = zip(*[rglru_fwd(proj_e, w["rg_conv_w"], w["rg_conv_b"], w["rg_gate_w"][d], w["rg_gate_b"][d],
                                  w["rg_lambda"][d], d == 1, "rglru_fwd_%d" % d) for d in range(2)])
    ycat = even_mix_fwd(proj_e, h_dir[0], h_dir[1], w["sc_conv_w"], "even_mix_fwd")
    if late_weights is not None:
        w = dict(w, **late_weights(ycat))
    x1, y_e = even_out_fwd(ycat, w["even_w_out"], w["even_norm_post"], x, "even_out_fwd")
    proj_o, h1 = norm_matmul(x1, w["odd_norm_pre"], w["odd_w_in"], F32, "odd_in_proj")
    o, st_dir = None, []
    for d in range(2):
        o, st = gla_fwd(proj_o, w["gla_wg_pad"], w["gla_b_gate"], o, d == 1, "gla_fwd_%d" % d)
        st_dir.append(st)
    y2, dy_o, dx2, loss, g["odd_norm_post"] = odd_out_fwd(
        o, proj_o, w["gla_norm_g"], w["odd_w_out"], w["odd_norm_post"], x1, target, "odd_out_fwd")
    g["odd_w_out"], g16["odd_w_out"] = (a[0] for a in matmul_dw(y2, dy_o, D_MODEL, "odd_w_out_grad"))
    dr, d_o, g["gla_norm_g"] = odd_out_bwd(dy_o, w["odd_w_out"], o, proj_o, w["gla_norm_g"], "odd_out_bwd")
    dq, dk, dv, dz_f = gla_bwd(proj_o, w["gla_wg_pad"], w["gla_b_gate"], d_o, st_dir[0], None, False, "gla_bwd_0")
    dq, dk, dv, dz_b = gla_bwd(proj_o, w["gla_wg_pad"], w["gla_b_gate"], d_o, st_dir[1], (dq, dk, dv), True,
                               "gla_bwd_1")
    dlr, g["gla_w_gate_lr"], g["gla_b_gate"] = gla_gate_bwd(proj_o, dz_f, dz_b, w["gla_wg_pad"], "gla_gate_bwd")
    dproj_o = [dq, dk, dv, dr, dlr]
    g["odd_w_in"] = jnp.concatenate(matmul_dw_pieces(h1, dproj_o, "odd_w_in_grad"), axis=1)[:, :ODD_IN]
    dx1, g["odd_norm_pre"] = inproj_bwd_pieces(dproj_o, w["odd_w_in"], x1, w["odd_norm_pre"], dx2, "odd_in_proj_bwd")
    dy_e, dycat, g["even_norm_post"] = even_out_bwd(dx1, y_e, w["even_norm_post"], w["even_w_out"], "even_out_bwd")
    g["even_w_out"], g16["even_w_out"] = (a[0] for a in matmul_dw(ycat, dy_e, D_MODEL, "even_w_out_grad"))
    lam = w["rg_lambda"] if reduce_first is None else w["rg_lambda"] + reduce_first(g, g16)
    dua, dgw, dgb, dlam = None, [], [], []
    for d in range(2):
        a, b, c, e = rglru_bwd(proj_e, dycat, h_dir[d], acts[d], w["rg_gate_w"][d], lam[d], dua, d == 1,
                               "rglru_bwd_%d" % d)
        dua = a
        dgw.append(b)
        dgb.append(c)
        dlam.append(e)
    dproj_e, g["rg_conv_w"], g["rg_conv_b"], g["sc_conv_w"] = even_mix_bwd(
        proj_e, dycat, h_dir[0], h_dir[1], dua, w["rg_conv_w"], w["sc_conv_w"], "even_mix_bwd")
    dgw = jnp.stack(dgw).reshape(2, RG_HEADS, RG_HEAD_DIM, 2, RG_HEAD_DIM)
    g["rg_gate_w"] = jnp.transpose(dgw, (0, 3, 1, 2, 4))
    g["rg_gate_b"] = jnp.stack(dgb).reshape(2, 2, RG_HEADS, RG_HEAD_DIM)
    g["rg_lambda"] = jnp.concatenate(dlam, axis=0)
    g["even_w_in"], g16["even_w_in"] = matmul_dw(h0, dproj_e, EVEN_IN // 4, "even_w_in_grad")
    gain = w["even_norm_pre"] if reduce_second is None else w["even_norm_pre"] + reduce_second(g, g16)
    grad_x, g["even_norm_pre"] = inproj_bwd(dproj_e, w["even_w_in"], x, gain, dx1, "even_in_proj_bwd")
    return loss, grad_x, g


def _prepare_weights(full):
    w = {}
    for name in ("even_norm_pre", "even_norm_post", "rg_conv_b", "odd_norm_pre", "odd_norm_post", "gla_norm_g"):
        if name in full:
            w[name] = full[name].reshape(1, -1)
    for name in ("rg_conv_w", "sc_conv_w"):
        if name in full:
            w[name] = full[name]
    for name in ("even_w_out", "odd_w_out"):
        if name in full:
            w[name] = full[name].astype(BF16)
    if "even_w_in" in full:
        w["even_w_in"] = full["even_w_in"].astype(BF16)
        if w["even_w_in"].ndim == 2:
            w["even_w_in"] = jnp.transpose(w["even_w_in"].reshape(D_MODEL, 4, EVEN_IN // 4), (1, 0, 2))
    if "rg_gate_w" in full:
        gw = jnp.transpose(full["rg_gate_w"].astype(BF16), (0, 2, 3, 1, 4))
        w["rg_gate_w"] = gw.reshape(2, RG_HEADS, RG_HEAD_DIM, 2 * RG_HEAD_DIM)
        w["rg_gate_b"] = full["rg_gate_b"].reshape(2, 2, D_MODEL)
        w["rg_lambda"] = full["rg_lambda"].reshape(2, 1, D_MODEL)
    if "odd_w_in" in full:
        w_in = jnp.pad(full["odd_w_in"].astype(BF16), ((0, 0), (0, ODD_IN_PAD - ODD_IN)))
        w["odd_w_in"] = w_in.reshape(1, D_MODEL, ODD_IN_PAD)
    if "gla_w_gate_lr" in full:
        wg = full["gla_w_gate_lr"].astype(BF16)
        w["gla_wg_pad"] = jnp.stack([jnp.pad(wg[d], ((d * GLA_RANK, LANES - (d + 1) * GLA_RANK), (0, 0)))
                                     for d in range(2)])
        w["gla_b_gate"] = full["gla_b_gate"].reshape(2, 1, GLA_HEADS * GLA_DK)
    return w


SHARDED_SMALL = (("rg_conv_w", (4, 256)), ("rg_lambda", (2, 256)), ("sc_conv_w", (3, 256)),
                 ("odd_norm_pre", (256,)), ("odd_norm_post", (256,)), ("gla_w_gate_lr", (2, 16, 128)),
                 ("gla_b_gate", (2, 128)), ("gla_norm_g", (64,)))
SHARDED_ROWS = 96
REPLICATED = (("rg_gate_w", (2, 2, 8, 128, 128)), ("even_norm_post", (1024,)), ("rg_conv_b", (1024,)),
              ("rg_gate_b", (2, 2, 8, 128)))
GATE_ROWS = 4096
LAST_REPLICATED = (("even_norm_pre", (1024,)),)
LAST_ROWS = 8
REPLICATED_ROWS = 4160
REP_PART = REPLICATED_ROWS // 8
HALF_SHARDED = SHARDED_ROWS // 2
PACK_HALF = HALF_SHARDED + REP_PART


def _seg_rows(shape):
    n = 1
    for s in shape:
        n *= s
    return -(-n // (SUBLANES * LANES)) * SUBLANES


def _pack(arrays, spec, total_rows, lead=()):
    parts = []
    for name, shape in spec:
        flat = arrays[name].reshape(lead + (-1,))
        pad = _seg_rows(shape) * LANES - flat.shape[-1]
        if pad:
            flat = jnp.pad(flat, [(0, 0)] * len(lead) + [(0, pad)])
        parts.append(flat.reshape(lead + (-1, LANES)))
    rows = jnp.concatenate(parts, axis=len(lead))
    pad = total_rows - rows.shape[len(lead)]
    return jnp.pad(rows, [(0, 0)] * len(lead) + [(0, pad), (0, 0)])


def _unpack(rows, spec, lead=()):
    out, at = {}, 0
    for name, shape in spec:
        n = 1
        for s in shape:
            n *= s
        k = _seg_rows(shape)
        seg = lax.slice_in_dim(rows, at, at + k, axis=len(lead)).reshape(lead + (-1,))
        out[name] = lax.slice_in_dim(seg, 0, n, axis=len(lead)).reshape(lead + shape)
        at += k
    return out


def _split_owners(arr):
    a = arr.reshape(arr.shape[:-1] + (4, arr.shape[-1] // 4))
    return jnp.moveaxis(a, -2, 0)


def _merge_owners(arr):
    a = jnp.moveaxis(arr, 0, -2)
    return a.reshape(a.shape[:-2] + (-1,))


HBM_SPEC = pl.BlockSpec(memory_space=pltpu.HBM)


def _position():
    x, y, c = lax.axis_index("x"), lax.axis_index("y"), lax.axis_index("c")
    chips = [(1 - x, y), (x, 1 - y), (1 - x, 1 - y)]
    return x, y, c, chips


def _remote(src, dst, send_sem, recv_sem, device):
    return pltpu.make_async_remote_copy(src_ref=src, dst_ref=dst, send_sem=send_sem, recv_sem=recv_sem,
                                        device_id=device, device_id_type=MESH)


SEM_SPEC = pl.BlockSpec(memory_space=pltpu.SEMAPHORE)
SIDE_EFFECT = pltpu.SideEffectType.DATAFLOW_SIDE_EFFECTING


def _gather_copies(ins, lands, n_h, send_sems, recv_sems):
    x, y, c, chips = _position()
    me = 2 * x + y
    copies = []
    for a in range(len(ins)):
        for k, chip in enumerate(chips):
            src = ins[a].at[c] if a < n_h else ins[a]
            dst = lands[a].at[me, c] if a < n_h else lands[a].at[me]
            copies.append(_remote(src, dst, send_sems.at[3 * a + k], recv_sems.at[3 * a + k], (chip[0], chip[1], c)))
    return copies


def gather_start(halved, whole, name):
    arrays = list(halved) + list(whole)
    n, n_h = len(arrays), len(halved)
    lands = [lax.empty((4,) + a.shape, a.dtype) for a in arrays]

    def body(*refs):
        ins, lz, send_sems, recv_sems, token = refs[:n], refs[n:2 * n], refs[2 * n], refs[2 * n + 1], refs[-1]
        for cp in _gather_copies(ins, lz, n_h, send_sems, recv_sems):
            cp.start()
        token[...] = jnp.zeros_like(token)

    operands = [pltpu.with_memory_space_constraint(a, pltpu.HBM) for a in arrays + lands]
    return pl.pallas_call(
        body, name=name,
        out_shape=(pltpu.SemaphoreType.DMA((3 * n,)), pltpu.SemaphoreType.DMA((3 * n,)))
        + tuple(pltpu.HBM(a.shape, a.dtype) for a in operands) + (jax.ShapeDtypeStruct((SUBLANES, LANES), F32),),
        in_specs=[HBM_SPEC] * (2 * n),
        out_specs=(SEM_SPEC, SEM_SPEC) + (HBM_SPEC,) * (2 * n) + (pl.BlockSpec(memory_space=pltpu.VMEM),),
        input_output_aliases={i: 2 + i for i in range(2 * n)},
        compiler_params=pltpu.CompilerParams(has_side_effects=SIDE_EFFECT),
    )(*operands)


def gather_wait(started, n_h, after, name):
    send_sems, recv_sems = started[0], started[1]
    operands = list(started[2:-1])
    n = len(operands) // 2

    def body(*refs):
        ins, lz, send_ref, recv_ref = refs[:n], refs[n:2 * n], refs[2 * n], refs[2 * n + 1]
        for cp in _gather_copies(ins, lz, n_h, send_ref, recv_ref):
            cp.wait_send()
            cp.wait_recv()

    outs = pl.pallas_call(
        body, name=name,
        out_shape=tuple(pltpu.HBM(a.shape, a.dtype) for a in operands),
        in_specs=[HBM_SPEC] * (2 * n) + [SEM_SPEC, SEM_SPEC, pl.BlockSpec(memory_space=pl.ANY)],
        out_specs=(HBM_SPEC,) * (2 * n),
        input_output_aliases={i: i for i in range(2 * n)},
        compiler_params=pltpu.CompilerParams(has_side_effects=SIDE_EFFECT),
    )(*operands, send_sems, recv_sems, after)
    return outs[n:]


def pass_to_sibling(fulls, name):
    n = len(fulls)

    def body(*refs):
        bufs = refs[n:2 * n]
        send_sems, recv_sems = refs[2 * n:]
        x, y, c, chips = _position()
        sibling = (x, y, 1 - c)
        copies = []
        for a in range(n):
            for k, chip in enumerate(chips):
                q = 2 * chip[0] + chip[1]
                cp = _remote(bufs[a].at[q, c], bufs[a].at[q, c], send_sems.at[3 * a + k], recv_sems.at[3 * a + k],
                             sibling)
                cp.start()
                copies.append(cp)
        for a in range(n):
            for k, chip in enumerate(chips):
                q = 2 * chip[0] + chip[1]
                passed = bufs[a].at[q, 1 - c]
                _remote(passed, passed, send_sems.at[3 * a + k], recv_sems.at[3 * a + k], sibling).wait_recv()
        for cp in copies:
            cp.wait_send()

    return pl.pallas_call(
        body, name=name,
        out_shape=[jax.ShapeDtypeStruct(a.shape, a.dtype) for a in fulls],
        in_specs=[HBM_SPEC] * n, out_specs=[HBM_SPEC] * n,
        input_output_aliases={i: i for i in range(n)},
        scratch_shapes=[pltpu.SemaphoreType.DMA((3 * n,)), pltpu.SemaphoreType.DMA((3 * n,))],
    )(*fulls)


def place_own(full, own, chip, name):
    _, _, r, cols = full.shape
    tr = _row_tile(r, cols)

    def body(p_ref, own_ref, full_ref, o_ref):
        o_ref[0] = own_ref[...]

    return pl.pallas_call(
        body, name=name,
        out_shape=jax.ShapeDtypeStruct(full.shape, full.dtype),
        grid_spec=pltpu.PrefetchScalarGridSpec(
            num_scalar_prefetch=1, grid=(2, r // tr),
            in_specs=[pl.BlockSpec((1, tr, cols), lambda h, i, p_ref: (h, i, 0)), pl.BlockSpec(memory_space=pl.ANY)],
            out_specs=pl.BlockSpec((1, 1, tr, cols), lambda h, i, p_ref: (p_ref[0], h, i, 0))),
        input_output_aliases={2: 0},
        compiler_params=_params("parallel", "parallel"),
    )(chip, own, full)


def exchange_with_sibling(arrays, name):
    n = len(arrays)

    def body(*refs):
        ins, outs = refs[:n], refs[n:2 * n]
        send_sems, recv_sems = refs[2 * n:]
        x, y, c, _ = _position()
        copies = []
        for a in range(n):
            cp = _remote(ins[a].at[:, 1 - c], outs[a], send_sems.at[a], recv_sems.at[a], (x, y, 1 - c))
            cp.start()
            copies.append(cp)
        for cp in copies:
            cp.wait()

    return pl.pallas_call(
        body, name=name,
        out_shape=[jax.ShapeDtypeStruct((a.shape[0],) + a.shape[2:], a.dtype) for a in arrays],
        in_specs=[HBM_SPEC] * n, out_specs=[HBM_SPEC] * n,
        scratch_shapes=[pltpu.SemaphoreType.DMA((n,)), pltpu.SemaphoreType.DMA((n,))],
    )(*arrays)


def _chip_copies(ins, lands, send_sems, recv_sems):
    x, y, c, chips = _position()
    copies = []
    for a in range(len(ins)):
        for k, chip in enumerate(chips):
            q = 2 * chip[0] + chip[1]
            copies.append(_remote(ins[a].at[q], lands[a].at[k], send_sems.at[3 * a + k], recv_sems.at[3 * a + k],
                                  (chip[0], chip[1], c)))
    return copies


def exchange_with_chips_start(arrays, name):
    n = len(arrays)
    lands = [lax.empty((3,) + a.shape[1:], a.dtype) for a in arrays]

    def body(*refs):
        ins, lz, send_sems, recv_sems, token = refs[:n], refs[n:2 * n], refs[2 * n], refs[2 * n + 1], refs[-1]
        for cp in _chip_copies(ins, lz, send_sems, recv_sems):
            cp.start()
        token[...] = jnp.zeros_like(token)

    operands = [pltpu.with_memory_space_constraint(a, pltpu.HBM) for a in list(arrays) + lands]
    return pl.pallas_call(
        body, name=name,
        out_shape=(pltpu.SemaphoreType.DMA((3 * n,)), pltpu.SemaphoreType.DMA((3 * n,)))
        + tuple(pltpu.HBM(a.shape, a.dtype) for a in operands) + (jax.ShapeDtypeStruct((SUBLANES, LANES), F32),),
        in_specs=[HBM_SPEC] * (2 * n),
        out_specs=(SEM_SPEC, SEM_SPEC) + (HBM_SPEC,) * (2 * n) + (pl.BlockSpec(memory_space=pltpu.VMEM),),
        input_output_aliases={i: 2 + i for i in range(2 * n)},
        compiler_params=pltpu.CompilerParams(has_side_effects=SIDE_EFFECT),
    )(*operands)


def exchange_with_chips_wait(started, after, name):
    send_sems, recv_sems = started[0], started[1]
    operands = list(started[2:-1])
    n = len(operands) // 2

    def body(*refs):
        ins, lz, send_ref, recv_ref = refs[:n], refs[n:2 * n], refs[2 * n], refs[2 * n + 1]
        for cp in _chip_copies(ins, lz, send_ref, recv_ref):
            cp.wait_send()
            cp.wait_recv()

    outs = pl.pallas_call(
        body, name=name,
        out_shape=tuple(pltpu.HBM(a.shape, a.dtype) for a in operands),
        in_specs=[HBM_SPEC] * (2 * n) + [SEM_SPEC, SEM_SPEC, pl.BlockSpec(memory_space=pl.ANY)],
        out_specs=(HBM_SPEC,) * (2 * n),
        input_output_aliases={i: i for i in range(2 * n)},
        compiler_params=pltpu.CompilerParams(has_side_effects=SIDE_EFFECT),
    )(*operands, send_sems, recv_sems, after)
    return outs[:n], outs[n:]


def share_totals(totals, pack_total, last_part):
    arrays = list(totals) + [pack_total]
    n = len(arrays)

    def body(*refs):
        ins, last, outs, rep, last_all = refs[:n], refs[n], refs[n + 1:2 * n + 1], refs[2 * n + 1], refs[2 * n + 2]
        send_sems, recv_sems, rep_send, rep_recv, last_send, last_recv = refs[2 * n + 3:]
        x, y, c, chips = _position()
        sibling = (x, y, 1 - c)
        me = 4 * x + 2 * y + c
        sends = []
        for a in range(n):
            cp = _remote(ins[a], outs[a], send_sems.at[a], recv_sems.at[a], sibling)
            cp.start()
            sends.append(cp)
        mine = ins[n - 1].at[pl.ds(HALF_SHARDED, REP_PART)]
        peers = [sibling]
        for chip in chips:
            peers += [(chip[0], chip[1], c), (chip[0], chip[1], 1 - c)]
        for j, peer in enumerate(peers):
            for src, dst, s_sem, r_sem in ((mine, rep, rep_send, rep_recv), (last, last_all, last_send, last_recv)):
                cp = _remote(src, dst.at[me], s_sem.at[j], r_sem.at[j], peer)
                cp.start()
                sends.append(cp)
        for a in range(n):
            _remote(outs[a], outs[a], send_sems.at[a], recv_sems.at[a], sibling).wait_recv()
        for j, peer in enumerate(peers):
            it = 4 * peer[0] + 2 * peer[1] + peer[2]
            _remote(rep.at[it], rep.at[it], rep_send.at[j], rep_recv.at[j], peer).wait_recv()
            _remote(last_all.at[it], last_all.at[it], last_send.at[j], last_recv.at[j], peer).wait_recv()
        for cp in sends:
            cp.wait_send()

    outs = pl.pallas_call(
        body, name="grad_share_totals",
        out_shape=[jax.ShapeDtypeStruct(a.shape, a.dtype) for a in arrays]
        + [jax.ShapeDtypeStruct((8, REP_PART, LANES), F32), jax.ShapeDtypeStruct((8,) + last_part.shape, F32)],
        in_specs=[HBM_SPEC] * (n + 1), out_specs=[HBM_SPEC] * (n + 2),
        scratch_shapes=[pltpu.SemaphoreType.DMA((n,)), pltpu.SemaphoreType.DMA((n,))]
        + [pltpu.SemaphoreType.DMA((7,))] * 4,
    )(*arrays, last_part)
    return outs[:n], outs[n], outs[n + 1]


def sum_parts(parts, name):
    def body(p_ref, o_ref):
        total = p_ref[0]
        for k in range(1, parts.shape[0]):
            total = total + p_ref[k]
        o_ref[...] = total

    return pl.pallas_call(body, name=name, out_shape=jax.ShapeDtypeStruct(parts.shape[1:], parts.dtype))(parts)


TILE_BYTES = 2 << 20


def _row_tile(rows, cols):
    best = None
    for t in range(SUBLANES, rows + 1, SUBLANES):
        if rows % t == 0 and t * cols * 4 <= TILE_BYTES:
            best = t
    return best if best is not None else rows


def add_sibling(mine, received, core, out_dtype, name):
    _, _, r, cols = mine.shape
    tr = _row_tile(r, cols)

    def body(c_ref, a_ref, b_ref, o_ref):
        o_ref[...] = (a_ref[0] + b_ref[...].astype(F32)).astype(out_dtype)

    return pl.pallas_call(
        body, name=name,
        out_shape=jax.ShapeDtypeStruct((4, r, cols), out_dtype),
        grid_spec=pltpu.PrefetchScalarGridSpec(
            num_scalar_prefetch=1, grid=(4, r // tr),
            in_specs=[pl.BlockSpec((1, 1, tr, cols), lambda o, i, c_ref: (o, c_ref[0], i, 0)),
                      pl.BlockSpec((1, tr, cols), lambda o, i, c_ref: (o, i, 0))],
            out_specs=pl.BlockSpec((1, tr, cols), lambda o, i, c_ref: (o, i, 0))),
        compiler_params=_params("parallel", "parallel"),
    )(core, mine, received)


def add_chips(own, received, chip, name):
    _, r, cols = own.shape
    tr = _row_tile(r, cols)

    def body(p_ref, a_ref, b0, b1, b2, o_ref):
        o_ref[...] = ((a_ref[0].astype(F32) + b0[0].astype(F32)) + b1[0].astype(F32)) + b2[0].astype(F32)

    rb = lambda k: pl.BlockSpec((1, tr, cols), lambda i, p_ref: (k, i, 0))
    return pl.pallas_call(
        body, name=name,
        out_shape=jax.ShapeDtypeStruct((r, cols), F32),
        grid_spec=pltpu.PrefetchScalarGridSpec(
            num_scalar_prefetch=1, grid=(r // tr,),
            in_specs=[pl.BlockSpec((1, tr, cols), lambda i, p_ref: (p_ref[0], i, 0)), rb(0), rb(1), rb(2)],
            out_specs=pl.BlockSpec((tr, cols), lambda i, p_ref: (i, 0))),
        compiler_params=_params("parallel"),
    )(chip, own, received, received, received)


def _adamw_update(gv, w_ref, m_ref, v_ref, d_ref, nm_ref, nv_ref):
    nm = ADAM_B1 * m_ref[...] + (1.0 - ADAM_B1) * gv
    nv = ADAM_B2 * v_ref[...] + (1.0 - ADAM_B2) * (gv * gv)
    nm_ref[...] = nm
    nv_ref[...] = nv
    m_hat = nm / (1.0 - ADAM_B1 ** ADAM_STEP)
    v_hat = nv / (1.0 - ADAM_B2 ** ADAM_STEP)
    d_ref[...] = -ADAM_LR * (m_hat / (jnp.sqrt(v_hat) + ADAM_EPS) + ADAM_WD * w_ref[...])


def adamw_halves(w, own, received, m, v, core, name, by_columns=False):
    rows, cols = w.shape

    def body(c_ref, w_ref, own_ref, rec_ref, m_ref, v_ref, g_ref, d_ref, nm_ref, nv_ref):
        gv = jnp.where(pl.program_id(0) == c_ref[0], own_ref[...], rec_ref[...])
        g_ref[...] = gv
        _adamw_update(gv, w_ref, m_ref, v_ref, d_ref, nm_ref, nv_ref)

    if by_columns:
        nr = 1
        whole = pl.BlockSpec((rows, cols // 2), lambda h, i, c_ref: (0, h))
        half = pl.BlockSpec((rows, cols // 2), lambda h, i, c_ref: (0, 0))
    else:
        r = rows // 2
        tr = _row_tile(r, cols)
        nr = r // tr
        whole = pl.BlockSpec((tr, cols), lambda h, i, c_ref: (h * nr + i, 0))
        half = pl.BlockSpec((tr, cols), lambda h, i, c_ref: (i, 0))
    return pl.pallas_call(
        body, name=name,
        out_shape=(jax.ShapeDtypeStruct((rows, cols), F32),) * 4,
        grid_spec=pltpu.PrefetchScalarGridSpec(
            num_scalar_prefetch=1, grid=(2, nr),
            in_specs=[whole, half, half, whole, whole], out_specs=(whole,) * 4),
        compiler_params=_params("parallel", "parallel"),
    )(core, w, own, received, m, v)


def adamw_many(ws, gs, ms, vs, name):
    n = len(ws)

    def body(*refs):
        ins, outs = refs[:4 * n], refs[4 * n:]
        for k in range(n):
            w_ref, g_ref, m_ref, v_ref = (ins[j * n + k] for j in range(4))
            d_ref, nm_ref, nv_ref = outs[3 * k:3 * k + 3]
            _adamw_update(g_ref[...], w_ref, m_ref, v_ref, d_ref, nm_ref, nv_ref)

    flat = pl.pallas_call(
        body, name=name,
        out_shape=[jax.ShapeDtypeStruct(w.shape, F32) for w in ws for _ in range(3)],
    )(*ws, *gs, *ms, *vs)
    return [tuple(flat[3 * k:3 * k + 3]) for k in range(n)]


def adamw(w, g, m, v, name):
    r, cols = w.shape
    tr = _row_tile(r, cols)

    def body(w_ref, g_ref, m_ref, v_ref, g_out, d_ref, nm_ref, nv_ref):
        gv = g_ref[...]
        g_out[...] = gv
        _adamw_update(gv, w_ref, m_ref, v_ref, d_ref, nm_ref, nv_ref)

    blk = pl.BlockSpec((tr, cols), lambda i: (i, 0))
    return pl.pallas_call(
        body, name=name,
        out_shape=(jax.ShapeDtypeStruct((r, cols), F32),) * 4,
        grid=(r // tr,),
        in_specs=[blk] * 4, out_specs=(blk,) * 4,
        compiler_params=_params("parallel"),
    )(w, g, m, v)


WEIGHTS = ("even_norm_pre", "even_norm_post", "even_w_in", "rg_conv_w", "rg_conv_b", "rg_gate_w", "rg_gate_b",
           "rg_lambda", "sc_conv_w", "even_w_out", "odd_norm_pre", "odd_norm_post", "odd_w_in", "gla_w_gate_lr",
           "gla_b_gate", "gla_norm_g", "odd_w_out")
BIG = ("even_w_in", "even_w_out", "odd_w_in", "odd_w_out")


def _halves(a):
    return a.reshape((2, a.shape[0] // 2) + a.shape[1:])


def kernel(x, even_norm_pre, even_norm_post, even_w_in, rg_conv_w, rg_conv_b, rg_gate_w, rg_gate_b, rg_lambda, sc_conv_w, even_w_out, odd_norm_pre, odd_norm_post, odd_w_in, gla_w_gate_lr, gla_b_gate, gla_norm_g, odd_w_out, loss_target, m_even_norm_pre, m_even_norm_post, m_even_w_in, m_rg_conv_w, m_rg_conv_b, m_rg_gate_w, m_rg_gate_b, m_rg_lambda, m_sc_conv_w, m_even_w_out, m_odd_norm_pre, m_odd_norm_post, m_odd_w_in, m_gla_w_gate_lr, m_gla_b_gate, m_gla_norm_g, m_odd_w_out, v_even_norm_pre, v_even_norm_post, v_even_w_in, v_rg_conv_w, v_rg_conv_b, v_rg_gate_w, v_rg_gate_b, v_rg_lambda, v_sc_conv_w, v_even_w_out, v_odd_norm_pre, v_odd_norm_post, v_odd_w_in, v_gla_w_gate_lr, v_gla_b_gate, v_gla_norm_g, v_odd_w_out):
    given = dict(locals())
    shard = {n: given[n][0] for n in WEIGHTS}
    m_in = {n: given["m_" + n][0] for n in WEIGHTS}
    v_in = {n: given["v_" + n][0] for n in WEIGHTS}
    mx, my, mc = lax.axis_index("x"), lax.axis_index("y"), lax.axis_index("c")
    core = jnp.reshape(mc, (1,)).astype(jnp.int32)
    chip = jnp.reshape(2 * mx + my, (1,)).astype(jnp.int32)

    small_shard = _pack(shard, SHARDED_SMALL, SHARDED_ROWS)
    big_own = [_halves(shard[n].astype(BF16)) for n in BIG]
    started_a = gather_start(big_own[:1], [small_shard], "gather_start_a")
    started_b = gather_start(big_own[1:], [], "gather_start_b")
    even_w_in_full, small_full = gather_wait(started_a, 1, started_b[-1], "gather_wait_a")
    (even_w_in_full,) = pass_to_sibling([even_w_in_full], "gather_pass_a")
    even_w_in_full = place_own(even_w_in_full, big_own[0], chip, "place_even_w_in")
    small_full = lax.dynamic_update_slice(small_full, small_shard[None], (chip[0], 0, 0))
    full = {n: shard[n] for n, _ in REPLICATED + LAST_REPLICATED}
    full.update({n: _merge_owners(a) for n, a in _unpack(small_full, SHARDED_SMALL, lead=(4,)).items()})
    full["even_w_in"] = even_w_in_full.reshape(4, D_MODEL, EVEN_IN // 4)

    def late_weights(after):
        lands = pass_to_sibling(list(gather_wait(started_b, 3, after, "gather_wait_b")), "gather_pass_b")
        lands = [place_own(a, b, chip, "place_" + n) for a, b, n in zip(lands, big_own[1:], BIG[1:])]
        odd_w_in = jnp.transpose(lands[1].reshape(4, D_MODEL, ODD_IN // 4), (1, 0, 2)).reshape(D_MODEL, ODD_IN)
        return _prepare_weights({"even_w_out": lands[0].reshape(2 * D_MODEL, D_MODEL), "odd_w_in": odd_w_in,
                                 "odd_w_out": lands[2].reshape(D_MODEL, D_MODEL)})

    pending = {}

    def slab(a):
        return a.reshape((4, 2, a.shape[1] // 2) + a.shape[2:])

    def begin(tag, slabs, to_send, dtypes):
        got = exchange_with_sibling(to_send, "grad_sibling_" + tag)
        sums = [add_sibling(a, b, core, dt, "grad_add_sibling_%s%d" % (tag, i))
                for i, (a, b, dt) in enumerate(zip(slabs, got, dtypes))]
        pending[tag] = exchange_with_chips_start(sums, "grad_chips_start_" + tag)
        return pending[tag][-1][0, 0]

    def finish(tag, after):
        sums, got = exchange_with_chips_wait(pending[tag], after, "grad_chips_wait_" + tag)
        return [add_chips(a, b, chip, "grad_add_chips_%s%d" % (tag, i)) for i, (a, b) in enumerate(zip(sums, got))]

    def reduce_first(g, g16):
        odd_w_in = slab(jnp.transpose(g["odd_w_in"].reshape(D_MODEL, 4, ODD_IN // 4), (1, 0, 2)))
        slabs = [odd_w_in] + [slab(g[n].reshape(4, -1, D_MODEL)) for n in ("odd_w_out", "even_w_out")]
        to_send = [odd_w_in.astype(BF16)] + [slab(g16[n].reshape(4, -1, D_MODEL)) for n in ("odd_w_out", "even_w_out")]
        return begin("a", slabs, to_send, [BF16] * 3)

    def reduce_second(g, g16):
        pending["totals_a"] = finish("a", g["even_w_in"])
        rep_rows = _pack(g, REPLICATED, REPLICATED_ROWS).reshape(4, 2, REP_PART, LANES)
        sh_rows = _pack({n: _split_owners(g[n]) for n, _ in SHARDED_SMALL}, SHARDED_SMALL, SHARDED_ROWS, lead=(4,))
        pack = jnp.concatenate([sh_rows.reshape(4, 2, HALF_SHARDED, LANES), rep_rows], axis=2)
        return begin("b", [slab(g["even_w_in"]), pack], [slab(g16["even_w_in"]), pack], [BF16, F32])

    loss, grad_x, g = local_step(x[0], loss_target[0], _prepare_weights(full), reduce_first, reduce_second,
                                 late_weights)
    odd_w_in_t, odd_w_out_t, even_w_out_t = pending["totals_a"]
    even_w_in_t, pack_t = finish("b", grad_x)
    totals = [even_w_in_t, even_w_out_t, odd_w_in_t, odd_w_out_t]
    last_part = jnp.concatenate([_pack(g, LAST_REPLICATED, LAST_ROWS), loss])
    from_core, rep_all, last_all = share_totals(totals, pack_t, last_part)
    me = 2 * chip[0] + core[0]
    mine, theirs = pack_t[:HALF_SHARDED], from_core[4][:HALF_SHARDED]
    sh_total = jnp.where(mc == 0, jnp.concatenate([mine, theirs]), jnp.concatenate([theirs, mine]))
    rep_all = lax.dynamic_update_slice(rep_all, pack_t[None, HALF_SHARDED:], (me, 0, 0))
    rep_total = rep_all.reshape(REPLICATED_ROWS, LANES)
    last_total = sum_parts(lax.dynamic_update_slice(last_all, last_part[None], (me, 0, 0)), "grad_sum_last")
    last_total, loss = last_total[:LAST_ROWS], last_total[LAST_ROWS, 0]
    grads = {}

    delta, new_m, new_v = {}, {}, {}
    for i, n in enumerate(BIG):
        if shard[n].shape[1] % LANES:
            outs = adamw_halves(shard[n].T, totals[i].T, from_core[i].T, m_in[n].T, v_in[n].T, core, "adamw_" + n,
                                by_columns=True)
            grads[n], delta[n], new_m[n], new_v[n] = [o.T for o in outs]
        else:
            grads[n], delta[n], new_m[n], new_v[n] = adamw_halves(shard[n], totals[i], from_core[i], m_in[n],
                                                                  v_in[n], core, "adamw_" + n)
    gate = [src["rg_gate_w"].reshape(GATE_ROWS, LANES) for src in (shard, m_in, v_in)]
    grads["rg_gate_w"], delta["rg_gate_w"], new_m["rg_gate_w"], new_v["rg_gate_w"] = adamw(
        gate[0], rep_total, gate[1], gate[2], "adamw_rg_gate_w")
    rest = REPLICATED[1:]
    rest_rows = sum(_seg_rows(shape) for _, shape in rest)
    grads.update(_unpack(sh_total, SHARDED_SMALL))
    grads.update(_unpack(rep_total[GATE_ROWS:GATE_ROWS + rest_rows], rest))
    grads.update(_unpack(last_total, LAST_REPLICATED))
    names = [n for n, _ in SHARDED_SMALL + rest + LAST_REPLICATED]
    rows_of = lambda a, n: a.reshape(-1, given[n].shape[-1])
    outs = adamw_many([rows_of(given[n], n) for n in names], [rows_of(grads[n], n) for n in names],
                      [rows_of(given["m_" + n], n) for n in names], [rows_of(given["v_" + n], n) for n in names],
                      "adamw_small")
    for n, (d, nm, nv) in zip(names, outs):
        delta[n], new_m[n], new_v[n] = d, nm, nv
    result = [loss, grad_x[None]]
    for group in (grads, delta, new_m, new_v):
        result += [group[n].reshape(given[n].shape) for n in WEIGHTS]
    return tuple(result)
```

```python
import jax
import jax.numpy as jnp
from jax import lax
from jax.experimental import pallas as pl
from jax.experimental.pallas import tpu as pltpu

F32 = jnp.float32
BF16 = jnp.bfloat16
MESH = pl.DeviceIdType.MESH

D_MODEL = 1024
NORM_EPS = 1e-6
RG_HEADS = 8
RG_HEAD_DIM = 128
RG_C = 8.0
EVEN_IN = 6144
ODD_IN = 3104
ODD_IN_PAD = 3200
GLA_HEADS = 4
GLA_DK = 128
GLA_DV = 256
GLA_RANK = 16
GLA_NORMALIZER = 16.0
GLA_CHUNK = 128
LR_COL = 3072

ADAM_LR = 0.001
ADAM_B1 = 0.9
ADAM_B2 = 0.999
ADAM_EPS = 1e-08
ADAM_WD = 0.01
ADAM_STEP = 10

SUBLANES = 8
HALO = 16
LANES = 128
VMEM_LIMIT = 56 * 2 ** 20

ROW_TILE = 512
SCAN_TILE = 512
GLA_BLOCK = 2048
MIX_TILE = 256


def _params(*sem):
    return pltpu.CompilerParams(dimension_semantics=sem, vmem_limit_bytes=VMEM_LIMIT)


def _full(shape):
    n = len(shape)
    return pl.BlockSpec(shape, lambda *_: (0,) * n)


def _sigmoid(x):
    return 0.5 + 0.5 * jnp.tanh(0.5 * x)


def _softplus(x):
    return jnp.maximum(x, 0.0) + jnp.log(1.0 + jnp.exp(-jnp.abs(x)))


def _dot(a, b):
    return jnp.dot(a, b, preferred_element_type=F32)


def _dot_nt(a, b):
    return lax.dot_general(a, b, (((1,), (1,)), ((), ())), preferred_element_type=F32)


def _dot_tn(a, b):
    return lax.dot_general(a, b, (((0,), (0,)), ((), ())), preferred_element_type=F32)


def _bdot(a, b, ca, cb):
    return lax.dot_general(a, b, (((ca,), (cb,)), ((0,), (0,))), preferred_element_type=F32)


def _halo_specs(rows, cols, col_block, n_row_tiles, tix):
    per = rows // HALO
    last = n_row_tiles * per - 1

    def split(args):
        if len(args) == 2:
            return tix(args[1]), col_block + args[0]
        return tix(args[0]), col_block

    def prev(*args):
        t, c = split(args)
        return (jnp.maximum(t * per - 1, 0), c)

    def main(*args):
        return split(args)

    def nxt(*args):
        t, c = split(args)
        return (jnp.minimum((t + 1) * per, last), c)

    return [pl.BlockSpec((HALO, cols), prev), pl.BlockSpec((rows, cols), main),
            pl.BlockSpec((HALO, cols), nxt)]


def _extend(prev_ref, main_ref, next_ref, is_first, is_last):
    p = jnp.where(is_first, 0.0, prev_ref[...].astype(F32))
    n = jnp.where(is_last, 0.0, next_ref[...].astype(F32))
    return jnp.concatenate([p, main_ref[...].astype(F32), n], axis=0)


def _shifted(ext, offset, rows):
    if offset == 0:
        return ext[HALO:HALO + rows]
    n = ext.shape[0]
    return pltpu.roll(ext, (-offset) % n, 0)[HALO:HALO + rows]


def _conv(ext, w, left, rows):
    out = None
    for k in range(w.shape[0]):
        term = _shifted(ext, k - left, rows) * w[k:k + 1]
        out = term if out is None else out + term
    return out


def _conv_transpose(ext, w, left, rows):
    out = None
    for k in range(w.shape[0]):
        term = _shifted(ext, left - k, rows) * w[k:k + 1]
        out = term if out is None else out + term
    return out


def _colsum(x):
    return jnp.sum(x, axis=0, keepdims=True)


def _accumulate(ref, value, step):
    @pl.when(step == 0)
    def _():
        ref[...] = value

    @pl.when(step > 0)
    def _():
        ref[...] += value


PROJ_TILE_BYTES = 7 * 2 ** 20


def _proj_row_tile(rows, width, dtype):
    tm = min(ROW_TILE, rows)
    while tm * width * jnp.dtype(dtype).itemsize > PROJ_TILE_BYTES and tm % (2 * HALO) == 0:
        tm //= 2
    return tm


def norm_matmul(x, gain, w, out_dtype, name):
    rows, d = x.shape
    n_col_tiles, _, tn = w.shape
    tm = _proj_row_tile(rows, n_col_tiles * tn, out_dtype)

    def body(x_ref, g_ref, w_ref, proj_ref, h_ref):
        xv = x_ref[...]
        rstd = lax.rsqrt(jnp.mean(xv * xv, axis=-1, keepdims=True) + NORM_EPS)
        hv = (xv * rstd * g_ref[...]).astype(BF16)
        h_ref[...] = hv
        for j in range(n_col_tiles):
            proj_ref[:, j * tn:(j + 1) * tn] = _dot(hv, w_ref[j]).astype(out_dtype)

    row = lambda cols: pl.BlockSpec((tm, cols), lambda i: (i, 0))
    return pl.pallas_call(
        body, name=name,
        out_shape=(jax.ShapeDtypeStruct((rows, n_col_tiles * tn), out_dtype), jax.ShapeDtypeStruct((rows, d), BF16)),
        grid=(rows // tm,),
        in_specs=[row(d), _full((1, d)), _full(w.shape)],
        out_specs=(row(n_col_tiles * tn), row(d)),
        compiler_params=_params("parallel"),
    )(x, gain, w)


def inproj_bwd(dproj, w, x, gain, dres, name):
    rows, d = x.shape
    n_col_tiles, _, tn = w.shape
    tm = _proj_row_tile(rows, n_col_tiles * tn, dproj.dtype)

    def body(dp_ref, w_ref, x_ref, g_ref, dres_ref, dx_ref, dg_ref):
        dh = None
        for j in range(n_col_tiles):
            part = _dot_nt(dp_ref[:, j * tn:(j + 1) * tn], w_ref[j])
            dh = part if dh is None else dh + part
        _inproj_finish(dh, x_ref, g_ref, dres_ref, dx_ref, dg_ref, pl.program_id(0))

    row = lambda cols: pl.BlockSpec((tm, cols), lambda i: (i, 0))
    return pl.pallas_call(
        body, name=name,
        out_shape=(jax.ShapeDtypeStruct((rows, d), F32), jax.ShapeDtypeStruct((1, d), F32)),
        grid=(rows // tm,),
        in_specs=[row(n_col_tiles * tn), _full(w.shape), row(d), _full((1, d)), row(d)],
        out_specs=(row(d), _full((1, d))),
        compiler_params=_params("arbitrary"),
    )(dproj, w, x, gain, dres)


def _inproj_finish(dh, x_ref, g_ref, dres_ref, dx_ref, dg_ref, step):
    xv = x_ref[...]
    rstd = lax.rsqrt(jnp.mean(xv * xv, axis=-1, keepdims=True) + NORM_EPS)
    xhat = xv * rstd
    dxn = dh * g_ref[...]
    dx_ref[...] = dres_ref[...] + rstd * (dxn - xhat * jnp.mean(dxn * xhat, axis=-1, keepdims=True))
    _accumulate(dg_ref, _colsum(dh * xhat), step)


def inproj_bwd_pieces(pieces, w, x, gain, dres, name):
    rows, d = x.shape
    tm = min(ROW_TILE, rows)
    n = len(pieces)
    widths = [p.shape[1] for p in pieces]
    starts = [sum(widths[:k]) for k in range(n)]
    assert sum(widths) == w.shape[2]

    def body(*refs):
        w_ref, x_ref, g_ref, dres_ref, dx_ref, dg_ref = refs[n:]
        dh = None
        for k in range(n):
            part = _dot_nt(refs[k][...], w_ref[0, :, starts[k]:starts[k] + widths[k]])
            dh = part if dh is None else dh + part
        _inproj_finish(dh, x_ref, g_ref, dres_ref, dx_ref, dg_ref, pl.program_id(0))

    row = lambda cols: pl.BlockSpec((tm, cols), lambda i: (i, 0))
    return pl.pallas_call(
        body, name=name,
        out_shape=(jax.ShapeDtypeStruct((rows, d), F32), jax.ShapeDtypeStruct((1, d), F32)),
        grid=(rows // tm,),
        in_specs=[row(wd) for wd in widths] + [_full(w.shape), row(d), _full((1, d)), row(d)],
        out_specs=(row(d), _full((1, d))),
        compiler_params=_params("arbitrary"),
    )(*pieces, w, x, gain, dres)


def matmul_dw_pieces(a, pieces, name):
    rows, m = a.shape
    tk = min(2 * ROW_TILE, rows)
    n = len(pieces)

    def body(*refs):
        a_ref, ins, outs = refs[0], refs[1:1 + n], refs[1 + n:]
        av = a_ref[...]
        for k in range(n):
            _accumulate(outs[k], _dot_tn(av, ins[k][...]), pl.program_id(0))

    return pl.pallas_call(
        body, name=name,
        out_shape=[jax.ShapeDtypeStruct((m, p.shape[1]), F32) for p in pieces],
        grid=(rows // tk,),
        in_specs=[pl.BlockSpec((tk, m), lambda k: (k, 0))]
        + [pl.BlockSpec((tk, p.shape[1]), lambda k: (k, 0)) for p in pieces],
        out_specs=[_full((m, p.shape[1])) for p in pieces],
        compiler_params=_params("arbitrary"),
    )(a, *pieces)


def matmul_dw(a, b, bn, name):
    rows, m = a.shape
    n = b.shape[1]
    tk = min((4 if n > bn else 2) * ROW_TILE, rows)
    steps = rows // tk

    def body(a_ref, b_ref, o_ref, o16_ref):
        part = _dot_tn(a_ref[...], b_ref[...])

        @pl.when(pl.program_id(1) == 0)
        def _():
            o_ref[0] = part

        @pl.when(pl.program_id(1) > 0)
        def _():
            o_ref[0] += part

        @pl.when(pl.program_id(1) == steps - 1)
        def _():
            o16_ref[0] = o_ref[0].astype(BF16)

    out = pl.BlockSpec((1, m, bn), lambda j, k: (j, 0, 0))
    return pl.pallas_call(
        body, name=name,
        out_shape=(jax.ShapeDtypeStruct((n // bn, m, bn), F32), jax.ShapeDtypeStruct((n // bn, m, bn), BF16)),
        grid=(n // bn, steps),
        in_specs=[pl.BlockSpec((tk, m), lambda j, k: (k, 0)), pl.BlockSpec((tk, bn), lambda j, k: (k, j))],
        out_specs=(out, out),
        compiler_params=_params("parallel", "arbitrary"),
    )(a, b)


def _scan(a, b, carry, reverse):
    n, c = a.shape
    blocks = n // SUBLANES
    a = a.reshape(blocks, SUBLANES, c)
    b = b.reshape(blocks, SUBLANES, c)
    pos = lax.broadcasted_iota(jnp.int32, (1, SUBLANES, c), 1)
    s = 1
    while s < SUBLANES:
        shift, valid = (SUBLANES - s, pos < SUBLANES - s) if reverse else (s, pos >= s)
        a_s, b_s = pltpu.roll(a, shift, 1), pltpu.roll(b, shift, 1)
        b = jnp.where(valid, a * b_s + b, b)
        a = jnp.where(valid, a * a_s, a)
        s *= 2
    out = [None] * blocks
    for k in (range(blocks - 1, -1, -1) if reverse else range(blocks)):
        h = a[k] * carry + b[k]
        out[k] = h
        carry = h[0:1] if reverse else h[SUBLANES - 1:SUBLANES]
    return jnp.concatenate(out, axis=0)


def _rg_gates(ua, gw_ref, gb, lam):
    ub = ua.astype(BF16)
    pre_r, pre_i = [], []
    for h in range(RG_HEADS):
        z = _dot(ub[:, h * RG_HEAD_DIM:(h + 1) * RG_HEAD_DIM], gw_ref[h])
        pre_r.append(z[:, :RG_HEAD_DIM])
        pre_i.append(z[:, RG_HEAD_DIM:])
    r = _sigmoid(jnp.concatenate(pre_r, axis=1) + gb[0:1])
    i = _sigmoid(jnp.concatenate(pre_i, axis=1) + gb[1:2])
    sp = _softplus(-lam)
    log_a = -RG_C * r * sp
    a = jnp.exp(log_a)
    mult = jnp.sqrt(1.0 - a * a)
    return r, i, sp, a, mult


def _rg_weight_specs():
    return [_full((4, D_MODEL)), _full((1, D_MODEL)), _full((RG_HEADS, RG_HEAD_DIM, 2 * RG_HEAD_DIM)),
            _full((2, D_MODEL)), _full((1, D_MODEL))]


def rglru_fwd(proj, conv_w, conv_b, gate_w, gate_b, lam, reverse, name):
    rows_total = proj.shape[0]
    rows = min(SCAN_TILE, rows_total)
    n_tiles = rows_total // rows
    tix = (lambda i: n_tiles - 1 - i) if reverse else (lambda i: i)

    def body(xp, xm, xn, cw_ref, cb_ref, gw_ref, gb_ref, lam_ref, h_ref, acts_ref, carry):
        i = pl.program_id(0)
        t = tix(i)
        ext = _extend(xp, xm, xn, t == 0, t == n_tiles - 1)
        ua = _conv(ext, cw_ref[...], 2, rows) + cb_ref[...]
        r, gi, _, a, mult = _rg_gates(ua, gw_ref, gb_ref[...], lam_ref[...])
        for k, saved in enumerate((ua, r, gi, a, mult)):
            acts_ref[k] = saved
        b = mult * (gi * ua)

        @pl.when(i == 0)
        def _():
            carry[...] = jnp.zeros_like(carry)

        h = _scan(a, b, carry[0:1], reverse)
        h_ref[...] = h
        edge = h[0:1] if reverse else h[rows - 1:rows]
        carry[...] = jnp.broadcast_to(edge, carry.shape)

    return pl.pallas_call(
        body, name=name,
        out_shape=(jax.ShapeDtypeStruct((rows_total, D_MODEL), F32),
                   jax.ShapeDtypeStruct((5, rows_total, D_MODEL), F32)),
        grid=(n_tiles,),
        in_specs=_halo_specs(rows, D_MODEL, 0, n_tiles, tix) + _rg_weight_specs(),
        out_specs=(pl.BlockSpec((rows, D_MODEL), lambda i: (tix(i), 0)),
                   pl.BlockSpec((5, rows, D_MODEL), lambda i: (0, tix(i), 0))),
        scratch_shapes=[pltpu.VMEM((SUBLANES, D_MODEL), F32)],
        compiler_params=_params("arbitrary"),
    )(proj, proj, proj, conv_w, conv_b, gate_w, gate_b, lam)


def rglru_bwd(proj, dycat, h_dir, acts, gate_w, lam, add_dua, reverse, name):
    rows_total = proj.shape[0]
    rows = min(SCAN_TILE, rows_total)
    n_tiles = rows_total // rows
    tix = (lambda i: i) if reverse else (lambda i: n_tiles - 1 - i)
    za_block = 1

    def body(acts_ref, za_ref, dya_ref, hp, hm, hn, gw_ref, lam_ref, *rest):
        other = rest[0][...] if add_dua is not None else 0.0
        dua_ref, dgw_ref, dgb_ref, dlam_ref, carry = rest[-5:]
        step = pl.program_id(0)
        t = tix(step)
        first, last = t == 0, t == n_tiles - 1
        ua, r, gi, a, mult = (acts_ref[k] for k in range(5))
        lam_v = lam_ref[...]
        sp = _softplus(-lam_v)
        za = za_ref[...].astype(F32)
        dh = dya_ref[...] * (za * _sigmoid(za))

        @pl.when(step == 0)
        def _():
            carry[...] = jnp.zeros_like(carry)

        old = carry[0:1]
        mu = _scan(a, a * dh, old, not reverse)
        row = lax.broadcasted_iota(jnp.int32, mu.shape, 0)
        if reverse:
            mu_next = jnp.where(row == 0, old, pltpu.roll(mu, 1, 0))
            carry[...] = jnp.broadcast_to(mu[rows - 1:rows], carry.shape)
            h_ext = _extend(hp, hm, hn, first, last)
            h_prev = _shifted(h_ext, 1, rows)
        else:
            mu_next = jnp.where(row == rows - 1, old, pltpu.roll(mu, rows - 1, 0))
            carry[...] = jnp.broadcast_to(mu[0:1], carry.shape)
            h_ext = _extend(hp, hm, hn, first, last)
            h_prev = _shifted(h_ext, -1, rows)
        db = dh + mu_next
        da = db * h_prev
        d_mult = db * (gi * ua)
        di = db * (mult * ua)
        dua = db * (mult * gi)
        dlog_a = da * a - d_mult * (a * a) / mult
        dr = dlog_a * (-RG_C * sp)
        dlam = _colsum(dlog_a * (-RG_C * r)) * (-_sigmoid(-lam_v))
        dpr = dr * (r * (1.0 - r))
        dpi = di * (gi * (1.0 - gi))
        dgb = jnp.concatenate([_colsum(dpr), _colsum(dpi)], axis=0)
        ub = ua.astype(BF16)
        dua_heads, dgw_heads = [], []
        for h in range(RG_HEADS):
            cols = slice(h * RG_HEAD_DIM, (h + 1) * RG_HEAD_DIM)
            dz = jnp.concatenate([dpr[:, cols], dpi[:, cols]], axis=1).astype(BF16)
            dgw_heads.append(_dot_tn(ub[:, cols], dz))
            dua_heads.append(_dot_nt(dz, gw_ref[h]))
        dua_ref[...] = dua + jnp.concatenate(dua_heads, axis=1) + other

        @pl.when(step == 0)
        def _():
            for h in range(RG_HEADS):
                dgw_ref[h] = dgw_heads[h]
            dgb_ref[...] = dgb
            dlam_ref[...] = dlam

        @pl.when(step > 0)
        def _():
            for h in range(RG_HEADS):
                dgw_ref[h] += dgw_heads[h]
            dgb_ref[...] += dgb
            dlam_ref[...] += dlam

    row_spec = lambda col: pl.BlockSpec((rows, D_MODEL), lambda i: (tix(i), col))
    return pl.pallas_call(
        body, name=name,
        out_shape=(jax.ShapeDtypeStruct((rows_total, D_MODEL), F32),
                   jax.ShapeDtypeStruct((RG_HEADS, RG_HEAD_DIM, 2 * RG_HEAD_DIM), F32),
                   jax.ShapeDtypeStruct((2, D_MODEL), F32), jax.ShapeDtypeStruct((1, D_MODEL), F32)),
        grid=(n_tiles,),
        in_specs=([pl.BlockSpec((5, rows, D_MODEL), lambda i: (0, tix(i), 0)), row_spec(za_block), row_spec(0)]
                  + _halo_specs(rows, D_MODEL, 0, n_tiles, tix)
                  + [_full((RG_HEADS, RG_HEAD_DIM, 2 * RG_HEAD_DIM)), _full((1, D_MODEL))]
                  + ([] if add_dua is None else [row_spec(0)])),
        out_specs=(row_spec(0), _full((RG_HEADS, RG_HEAD_DIM, 2 * RG_HEAD_DIM)), _full((2, D_MODEL)),
                   _full((1, D_MODEL))),
        scratch_shapes=[pltpu.VMEM((SUBLANES, D_MODEL), F32)],
        compiler_params=_params("arbitrary"),
    )(acts, proj, dycat, h_dir, h_dir, h_dir, gate_w, lam, *([] if add_dua is None else [add_dua]))


def _extend_cols(refs, block, is_first, is_last):
    cols = slice(block * D_MODEL, (block + 1) * D_MODEL)
    prev_ref, main_ref, next_ref = refs
    p = jnp.where(is_first, 0.0, prev_ref[:, cols].astype(F32))
    n = jnp.where(is_last, 0.0, next_ref[:, cols].astype(F32))
    return jnp.concatenate([p, main_ref[:, cols].astype(F32), n], axis=0)


def even_mix_fwd(proj, h_f, h_b, sc_w, name):
    rows_total = proj.shape[0]
    rows = min(2 * MIX_TILE, rows_total)
    n_tiles = rows_total // rows
    ident = lambda i: i

    def body(za_ref, hf_ref, hb_ref, xbp, xbm, xbn, gcp, gcm, gcn, gb_ref, zb_ref, w_ref, y_ref):
        t = pl.program_id(0)
        first, last = t == 0, t == n_tiles - 1
        za = za_ref[...].astype(F32)
        y_ref[:, 0:D_MODEL] = ((hf_ref[...] + hb_ref[...]) * (za * _sigmoid(za))).astype(BF16)
        p_ext = _extend(xbp, xbm, xbn, first, last) * _extend(gcp, gcm, gcn, first, last)
        cv = _conv(p_ext, w_ref[...], 1, rows)
        zb = zb_ref[...].astype(F32)
        y_ref[:, D_MODEL:2 * D_MODEL] = (gb_ref[...].astype(F32) * cv * (zb * _sigmoid(zb))).astype(BF16)

    blk = lambda col: pl.BlockSpec((rows, D_MODEL), lambda i: (i, col))
    return pl.pallas_call(
        body, name=name,
        out_shape=jax.ShapeDtypeStruct((rows_total, 2 * D_MODEL), BF16),
        grid=(n_tiles,),
        in_specs=([blk(1), blk(0), blk(0)] + _halo_specs(rows, D_MODEL, 2, n_tiles, ident)
                  + _halo_specs(rows, D_MODEL, 4, n_tiles, ident) + [blk(3), blk(5), _full((3, D_MODEL))]),
        out_specs=pl.BlockSpec((rows, 2 * D_MODEL), lambda i: (i, 0)),
        compiler_params=_params("parallel"),
    )(proj, h_f, h_b, proj, proj, proj, proj, proj, proj, proj, proj, sc_w)


def even_mix_bwd(proj, dycat, h_f, h_b, dua, conv_w, sc_w, name):
    rows_total, width = proj.shape
    rows = min(MIX_TILE, rows_total)
    n_tiles = rows_total // rows
    ident = lambda i: i

    def body(pp, pm, pn, dyp, dym, dyn, hf_ref, hb_ref, dup, dum, dun, cw_ref, sw_ref,
             dp_ref, dcw_ref, dcb_ref, dsw_ref):
        def put(k, value):
            dp_ref[:, k * D_MODEL:(k + 1) * D_MODEL] = value.astype(BF16)

        t = pl.program_id(0)
        first, last = t == 0, t == n_tiles - 1
        proj_ext = lambda k: _extend_cols((pp, pm, pn), k, first, last)
        mid = slice(HALO, HALO + rows)
        za = pm[:, D_MODEL:2 * D_MODEL].astype(F32)
        sa = _sigmoid(za)
        put(1, dym[:, 0:D_MODEL] * (hf_ref[...] + hb_ref[...]) * (sa * (1.0 + za * (1.0 - sa))))
        dua_ext = _extend(dup, dum, dun, first, last)
        cw = cw_ref[...]
        put(0, _conv_transpose(dua_ext, cw, 2, rows))
        dua_mid = dua_ext[mid]
        xa_ext = proj_ext(0)
        dcw = jnp.concatenate([_colsum(dua_mid * _shifted(xa_ext, k - 2, rows)) for k in range(4)], axis=0)
        dcb = _colsum(dua_mid)
        xb_ext, gb_ext, gc_ext, zb_ext = proj_ext(2), proj_ext(3), proj_ext(4), proj_ext(5)
        p_ext = xb_ext * gc_ext
        sb_ext = _sigmoid(zb_ext)
        dyb_ext = _extend_cols((dyp, dym, dyn), 1, first, last)
        dcv_ext = dyb_ext * gb_ext * (zb_ext * sb_ext)
        sw = sw_ref[...]
        p_at = [_shifted(p_ext, k - 1, rows) for k in range(3)]
        cv = (p_at[0] * sw[0:1] + p_at[1] * sw[1:2]) + p_at[2] * sw[2:3]
        zb, sb, dyb, gb = zb_ext[mid], sb_ext[mid], dyb_ext[mid], gb_ext[mid]
        put(3, dyb * cv * (zb * sb))
        put(5, dyb * gb * cv * (sb * (1.0 + zb * (1.0 - sb))))
        dp = _conv_transpose(dcv_ext, sw, 1, rows)
        put(4, dp * xb_ext[mid])
        put(2, dp * gc_ext[mid])
        dcv = dcv_ext[mid]
        dsw = jnp.concatenate([_colsum(dcv * p_at[k]) for k in range(3)], axis=0)
        _accumulate(dcw_ref, dcw, t)
        _accumulate(dcb_ref, dcb, t)
        _accumulate(dsw_ref, dsw, t)

    own = pl.BlockSpec((rows, D_MODEL), lambda i: (i, 0))
    return pl.pallas_call(
        body, name=name,
        out_shape=(jax.ShapeDtypeStruct((rows_total, 6 * D_MODEL), BF16),
                   jax.ShapeDtypeStruct((4, D_MODEL), F32), jax.ShapeDtypeStruct((1, D_MODEL), F32),
                   jax.ShapeDtypeStruct((3, D_MODEL), F32)),
        grid=(n_tiles,),
        in_specs=(_halo_specs(rows, width, 0, n_tiles, ident) + _halo_specs(rows, 2 * D_MODEL, 0, n_tiles, ident)
                  + [own, own] + _halo_specs(rows, D_MODEL, 0, n_tiles, ident)
                  + [_full((4, D_MODEL)), _full((3, D_MODEL))]),
        out_specs=(pl.BlockSpec((rows, 6 * D_MODEL), lambda i: (i, 0)), _full((4, D_MODEL)), _full((1, D_MODEL)),
                   _full((3, D_MODEL))),
        compiler_params=_params("arbitrary"),
    )(proj, proj, proj, dycat, dycat, dycat, h_f, h_b, dua, dua, dua, conv_w, sc_w)


def even_out_fwd(ycat, w_out, gain, x, name):
    rows, d = x.shape
    k = ycat.shape[1]
    tm = min(ROW_TILE, rows)

    def body(yc_ref, w_ref, g_ref, x_ref, x1_ref, y_ref):
        y = _dot(yc_ref[...], w_ref[...])
        y_ref[...] = y
        rstd = lax.rsqrt(jnp.mean(y * y, axis=-1, keepdims=True) + NORM_EPS)
        x1_ref[...] = x_ref[...] + y * rstd * g_ref[...]

    row = lambda n: pl.BlockSpec((tm, n), lambda i: (i, 0))
    return pl.pallas_call(
        body, name=name,
        out_shape=(jax.ShapeDtypeStruct((rows, d), F32),) * 2,
        grid=(rows // tm,),
        in_specs=[row(k), _full((k, d)), _full((1, d)), row(d)],
        out_specs=(row(d), row(d)),
        compiler_params=_params("parallel"),
    )(ycat, w_out, gain, x)


def _rmsnorm_bwd(dout, y, gain):
    rstd = lax.rsqrt(jnp.mean(y * y, axis=-1, keepdims=True) + NORM_EPS)
    yhat = y * rstd
    dyn = dout * gain
    dy = rstd * (dyn - yhat * jnp.mean(dyn * yhat, axis=-1, keepdims=True))
    return dy, dout * yhat


def even_out_bwd(dx1, y, gain, w_out, name):
    rows, d = y.shape
    k = w_out.shape[0]
    tm = min(ROW_TILE, rows)

    def body(dx_ref, y_ref, g_ref, w_ref, dy_ref, dyc_ref, dg_ref):
        dy, dg_rows = _rmsnorm_bwd(dx_ref[...], y_ref[...], g_ref[...])
        dyb = dy.astype(BF16)
        dy_ref[...] = dyb
        dyc_ref[...] = _dot_nt(dyb, w_ref[...])
        _accumulate(dg_ref, _colsum(dg_rows), pl.program_id(0))

    row = lambda n: pl.BlockSpec((tm, n), lambda i: (i, 0))
    return pl.pallas_call(
        body, name=name,
        out_shape=(jax.ShapeDtypeStruct((rows, d), BF16), jax.ShapeDtypeStruct((rows, k), F32),
                   jax.ShapeDtypeStruct((1, d), F32)),
        grid=(rows // tm,),
        in_specs=[row(d), row(d), _full((1, d)), _full((k, d))],
        out_specs=(row(d), row(k), _full((1, d))),
        compiler_params=_params("arbitrary"),
    )(dx1, y, gain, w_out)


def _chunk_cumsum(g, reverse):
    n, c = g.shape
    chunks, per = n // GLA_CHUNK, GLA_CHUNK // SUBLANES
    g = g.reshape(n // SUBLANES, SUBLANES, c)
    pos = lax.broadcasted_iota(jnp.int32, (1, SUBLANES, c), 1)
    s = 1
    while s < SUBLANES:
        if reverse:
            g = g + jnp.where(pos < SUBLANES - s, pltpu.roll(g, SUBLANES - s, 1), 0.0)
        else:
            g = g + jnp.where(pos >= s, pltpu.roll(g, s, 1), 0.0)
        s *= 2
    g = g.reshape(chunks, per, SUBLANES, c)
    out, carry = [None] * per, None
    for k in (range(per - 1, -1, -1) if reverse else range(per)):
        out[k] = g[:, k] if carry is None else g[:, k] + carry
        carry = out[k][:, 0:1] if reverse else out[k][:, SUBLANES - 1:SUBLANES]
    return jnp.stack(out, axis=1).reshape(n, c)


def _gla_prepare(q_ref, k_ref, lr_ref, wg_ref, bg_ref, reverse, n_chunks):
    z = _dot(lr_ref[...].astype(BF16), wg_ref[0]) + bg_ref[0]
    g = -_softplus(-z) * (1.0 / GLA_NORMALIZER)
    bcum = _chunk_cumsum(g, reverse).reshape(n_chunks, GLA_CHUNK, GLA_DK)
    edge = 0 if reverse else GLA_CHUNK - 1
    btot = bcum[:, edge:edge + 1, :]
    e_pos = jnp.exp(bcum)
    e_neg = jnp.exp(-bcum)
    e_st = jnp.exp(btot - bcum)
    q3 = q_ref[...].reshape(n_chunks, GLA_CHUNK, GLA_DK)
    k3 = k_ref[...].reshape(n_chunks, GLA_CHUNK, GLA_DK)
    scale = GLA_DK ** -0.5
    q_in = q3 * scale * e_pos
    k_in = k3 * e_neg
    k_st = k3 * e_st
    dec = jnp.exp(btot)
    return z, q_in, k_in, k_st, dec, (scale * e_pos, e_neg, e_st)


def _gla_mask(reverse):
    i = lax.broadcasted_iota(jnp.int32, (GLA_CHUNK, GLA_CHUNK), 0)
    j = lax.broadcasted_iota(jnp.int32, (GLA_CHUNK, GLA_CHUNK), 1)
    return (j >= i) if reverse else (j <= i)


def _gla_specs(rows, n_blocks, reverse):
    tix = (lambda s: n_blocks - 1 - s) if reverse else (lambda s: s)
    d = 1 if reverse else 0
    lr_block = LR_COL // LANES
    specs = [pl.BlockSpec((rows, GLA_DK), lambda h, s: (tix(s), h)),
             pl.BlockSpec((rows, GLA_DK), lambda h, s: (tix(s), GLA_HEADS + h)),
             pl.BlockSpec((rows, GLA_DV), lambda h, s: (tix(s), GLA_HEADS + h)),
             pl.BlockSpec((rows, LANES), lambda h, s: (tix(s), lr_block)),
             pl.BlockSpec((1, LANES, GLA_DK), lambda h, s: (d, 0, h)),
             pl.BlockSpec((1, 1, GLA_DK), lambda h, s: (d, 0, h))]
    return specs, tix


def gla_fwd(proj, wg_pad, bg, add_o, reverse, name):
    rows_total = proj.shape[0]
    rows = min(GLA_BLOCK, rows_total)
    n_blocks = rows_total // rows
    n_chunks = rows // GLA_CHUNK
    specs, tix = _gla_specs(rows, n_blocks, reverse)

    def body(q_ref, k_ref, v_ref, lr_ref, wg_ref, bg_ref, *rest):
        o_ref, st_ref, state, kv_scr, dec_scr = rest[-5:]
        _, q_in, k_in, k_st, dec, _ = _gla_prepare(q_ref, k_ref, lr_ref, wg_ref, bg_ref, reverse, n_chunks)
        vb = v_ref[...].reshape(n_chunks, GLA_CHUNK, GLA_DV).astype(BF16)
        qb = q_in.astype(BF16)
        p = jnp.where(_gla_mask(reverse), _bdot(qb, k_in.astype(BF16), 2, 2), 0.0)
        o = _bdot(p.astype(BF16), vb, 2, 1)
        kv_scr[...] = _bdot(vb, k_st.astype(BF16), 1, 1)
        dec_scr[...] = jnp.broadcast_to(dec, dec_scr.shape)

        @pl.when(pl.program_id(1) == 0)
        def _():
            state[...] = jnp.zeros_like(state)

        for c in range(n_chunks):
            cc = n_chunks - 1 - c if reverse else c
            st_ref[0, cc] = state[...]
            state[...] = state[...] * dec_scr[cc, 0:1] + kv_scr[cc]
        o = o + _bdot(qb, st_ref[0].astype(BF16), 2, 2)
        o = o.reshape(rows, GLA_DV)
        o_ref[...] = o if add_o is None else o + rest[0][...]

    o_spec = pl.BlockSpec((rows, GLA_DV), lambda h, s: (tix(s), h))
    return pl.pallas_call(
        body, name=name,
        out_shape=(jax.ShapeDtypeStruct((rows_total, GLA_HEADS * GLA_DV), F32),
                   jax.ShapeDtypeStruct((GLA_HEADS, rows_total // GLA_CHUNK, GLA_DV, GLA_DK), F32)),
        grid=(GLA_HEADS, n_blocks),
        in_specs=specs + ([] if add_o is None else [o_spec]),
        out_specs=(o_spec,
                   pl.BlockSpec((1, n_chunks, GLA_DV, GLA_DK), lambda h, s: (h, tix(s), 0, 0))),
        scratch_shapes=[pltpu.VMEM((GLA_DV, GLA_DK), F32), pltpu.VMEM((n_chunks, GLA_DV, GLA_DK), F32),
                        pltpu.VMEM((n_chunks, SUBLANES, GLA_DK), F32)],
        compiler_params=_params("parallel", "arbitrary"),
    )(proj, proj, proj, proj, wg_pad, bg, *([] if add_o is None else [add_o]))


def gla_bwd(proj, wg_pad, bg, d_o, states, dqkv_in, reverse, name):
    rows_total = proj.shape[0]
    rows = min(GLA_BLOCK, rows_total)
    n_blocks = rows_total // rows
    n_chunks = rows // GLA_CHUNK
    specs, tix = _gla_specs(rows, n_blocks, not reverse)
    d = 1 if reverse else 0
    specs[4] = pl.BlockSpec((1, LANES, GLA_DK), lambda h, s: (d, 0, h))
    specs[5] = pl.BlockSpec((1, 1, GLA_DK), lambda h, s: (d, 0, h))
    add = dqkv_in is not None

    def body(*refs):
        q_ref, k_ref, v_ref, lr_ref, wg_ref, bg_ref, do_ref, st_ref = refs[:8]
        refs = refs[8:]
        if add:
            aq_ref, ak_ref, av_ref = refs[:3]
            refs = refs[3:]
        dq_ref, dk_ref, dv_ref, dz_ref, dstate, g_scr, dec_scr, dsn_scr = refs
        z, q_in, k_in, k_st, dec, (f_q, f_k, f_s) = _gla_prepare(q_ref, k_ref, lr_ref, wg_ref, bg_ref, reverse,
                                                                 n_chunks)
        mask = _gla_mask(reverse)
        vb = v_ref[...].reshape(n_chunks, GLA_CHUNK, GLA_DV).astype(BF16)
        dob = do_ref[...].reshape(n_chunks, GLA_CHUNK, GLA_DV).astype(BF16)
        qb, kb, ksb = q_in.astype(BF16), k_in.astype(BF16), k_st.astype(BF16)
        st = st_ref[0]
        stb = st.astype(BF16)
        pb = jnp.where(mask, _bdot(qb, kb, 2, 2), 0.0).astype(BF16)
        dpb = jnp.where(mask, _bdot(dob, vb, 2, 2), 0.0).astype(BF16)
        d_qin = _bdot(dpb, kb, 2, 1) + _bdot(dob, stb, 2, 1)
        d_kin = _bdot(dpb, qb, 1, 1)
        dv = _bdot(pb, dob, 1, 1)
        g_scr[...] = _bdot(dob, qb, 1, 1)
        dec_scr[...] = jnp.broadcast_to(dec, dec_scr.shape)

        @pl.when(pl.program_id(1) == 0)
        def _():
            dstate[...] = jnp.zeros_like(dstate)

        for c in range(n_chunks):
            cc = c if reverse else n_chunks - 1 - c
            dsn_scr[cc] = dstate[...]
            dstate[...] = dstate[...] * dec_scr[cc, 0:1] + g_scr[cc]
        dsn = dsn_scr[...]
        dsnb = dsn.astype(BF16)
        dv = dv + _bdot(ksb, dsnb, 2, 2)
        d_kst = _bdot(vb, dsnb, 2, 1)
        d_dec = jnp.sum(dsn * st, axis=1, keepdims=True)
        ks_term = d_kst * k_st
        d_btot = d_dec * dec + jnp.sum(ks_term, axis=1, keepdims=True)
        d_b = d_qin * q_in - d_kin * k_in - ks_term
        pos = lax.broadcasted_iota(jnp.int32, d_b.shape, 1)
        edge = 0 if reverse else GLA_CHUNK - 1
        d_b = d_b + jnp.where(pos == edge, d_btot, 0.0)
        dg = _chunk_cumsum(d_b.reshape(rows, GLA_DK), not reverse)
        dz_ref[...] = dg * (1.0 / GLA_NORMALIZER) * _sigmoid(-z)
        dq = (d_qin * f_q).reshape(rows, GLA_DK)
        dk = (d_kin * f_k + d_kst * f_s).reshape(rows, GLA_DK)
        dv = dv.reshape(rows, GLA_DV)
        if add:
            dq_ref[...] = (dq + aq_ref[...]).astype(BF16)
            dk_ref[...] = (dk + ak_ref[...]).astype(BF16)
            dv_ref[...] = (dv + av_ref[...]).astype(BF16)
        else:
            dq_ref[...] = dq
            dk_ref[...] = dk
            dv_ref[...] = dv

    qkv_specs = [pl.BlockSpec((rows, GLA_DK), lambda h, s: (tix(s), h)),
                 pl.BlockSpec((rows, GLA_DK), lambda h, s: (tix(s), h)),
                 pl.BlockSpec((rows, GLA_DV), lambda h, s: (tix(s), h))]
    in_specs = specs + [pl.BlockSpec((rows, GLA_DV), lambda h, s: (tix(s), h)),
                        pl.BlockSpec((1, n_chunks, GLA_DV, GLA_DK), lambda h, s: (h, tix(s), 0, 0))]
    args = [proj, proj, proj, proj, wg_pad, bg, d_o, states]
    out_dtype = F32
    if add:
        in_specs += qkv_specs
        args += list(dqkv_in)
        out_dtype = BF16
    return pl.pallas_call(
        body, name=name,
        out_shape=(jax.ShapeDtypeStruct((rows_total, GLA_HEADS * GLA_DK), out_dtype),
                   jax.ShapeDtypeStruct((rows_total, GLA_HEADS * GLA_DK), out_dtype),
                   jax.ShapeDtypeStruct((rows_total, GLA_HEADS * GLA_DV), out_dtype),
                   jax.ShapeDtypeStruct((rows_total, GLA_HEADS * GLA_DK), F32)),
        grid=(GLA_HEADS, n_blocks),
        in_specs=in_specs,
        out_specs=(pl.BlockSpec((rows, GLA_DK), lambda h, s: (tix(s), h)),
                   pl.BlockSpec((rows, GLA_DK), lambda h, s: (tix(s), h)),
                   pl.BlockSpec((rows, GLA_DV), lambda h, s: (tix(s), h)),
                   pl.BlockSpec((rows, GLA_DK), lambda h, s: (tix(s), h))),
        scratch_shapes=[pltpu.VMEM((GLA_DV, GLA_DK), F32), pltpu.VMEM((n_chunks, GLA_DV, GLA_DK), F32),
                        pltpu.VMEM((n_chunks, SUBLANES, GLA_DK), F32),
                        pltpu.VMEM((n_chunks, GLA_DV, GLA_DK), F32)],
        compiler_params=_params("parallel", "arbitrary"),
    )(*args)


def gla_gate_bwd(proj, dz_f, dz_b, wg_pad, name):
    rows_total = proj.shape[0]
    tm = min(ROW_TILE, rows_total)
    n_key = GLA_HEADS * GLA_DK

    def body(lr_ref, dzf_ref, dzb_ref, wg_ref, dlr_ref, dwg_ref, dbg_ref):
        step = pl.program_id(0)
        lr_t = jnp.transpose(lr_ref[...])
        dzf, dzb = dzf_ref[...], dzb_ref[...]
        dzf16, dzb16 = dzf.astype(BF16), dzb.astype(BF16)
        dlr_ref[...] = (_dot_nt(dzf16, wg_ref[0]) + _dot_nt(dzb16, wg_ref[1])).astype(BF16)
        dwf = _dot(lr_t[0:GLA_RANK].astype(BF16), dzf16)
        dwb = _dot(lr_t[GLA_RANK:2 * GLA_RANK].astype(BF16), dzb16)
        dbg = jnp.concatenate([_colsum(dzf), _colsum(dzb)], axis=0)

        @pl.when(step == 0)
        def _():
            dwg_ref[0] = dwf
            dwg_ref[1] = dwb
            dbg_ref[...] = dbg

        @pl.when(step > 0)
        def _():
            dwg_ref[0] += dwf
            dwg_ref[1] += dwb
            dbg_ref[...] += dbg

    return pl.pallas_call(
        body, name=name,
        out_shape=(jax.ShapeDtypeStruct((rows_total, LANES), BF16), jax.ShapeDtypeStruct((2, GLA_RANK, n_key), F32),
                   jax.ShapeDtypeStruct((2, n_key), F32)),
        grid=(rows_total // tm,),
        in_specs=[pl.BlockSpec((tm, LANES), lambda i: (i, LR_COL // LANES)),
                  pl.BlockSpec((tm, n_key), lambda i: (i, 0)), pl.BlockSpec((tm, n_key), lambda i: (i, 0)),
                  _full((2, LANES, n_key))],
        out_specs=(pl.BlockSpec((tm, LANES), lambda i: (i, 0)), _full((2, GLA_RANK, n_key)), _full((2, n_key))),
        compiler_params=_params("arbitrary"),
    )(proj, dz_f, dz_b, wg_pad)


def _head_norm(o, gain):
    outs, hats, rstds = [], [], []
    for h in range(GLA_HEADS):
        oh = o[:, h * GLA_DV:(h + 1) * GLA_DV]
        rstd = lax.rsqrt(jnp.mean(oh * oh, axis=-1, keepdims=True) + NORM_EPS)
        hat = oh * rstd
        outs.append(hat * gain)
        hats.append(hat)
        rstds.append(rstd)
    return outs, hats, rstds


def odd_out_fwd(o, proj, head_gain, w_out, gain, x1, target, name):
    rows, d = x1.shape
    tm = min(ROW_TILE, rows)
    r_block = (2 * GLA_HEADS * GLA_DK + GLA_HEADS * GLA_DV) // d

    def body(o_ref, r_ref, hg_ref, w_ref, g_ref, x1_ref, tgt_ref, y2_ref, dy_ref, dx2_ref, loss_ref, dg_ref):
        step = pl.program_id(0)
        on, _, _ = _head_norm(o_ref[...], hg_ref[...])
        r = r_ref[...]
        y2 = (jnp.concatenate(on, axis=1) * (r * _sigmoid(r))).astype(BF16)
        y2_ref[...] = y2
        y = _dot(y2, w_ref[...])
        gain_v = g_ref[...]
        rstd = lax.rsqrt(jnp.mean(y * y, axis=-1, keepdims=True) + NORM_EPS)
        x2 = x1_ref[...] + y * rstd * gain_v
        diff = x2 - tgt_ref[...]
        loss = 0.5 * jnp.sum(jnp.mean(diff * diff, axis=-1, keepdims=True), axis=0, keepdims=True)
        dx2 = diff * (1.0 / d)
        dx2_ref[...] = dx2
        dy, dg_rows = _rmsnorm_bwd(dx2, y, gain_v)
        dy_ref[...] = dy.astype(BF16)
        _accumulate(loss_ref, jnp.broadcast_to(loss, loss_ref.shape), step)
        _accumulate(dg_ref, _colsum(dg_rows), step)

    row = lambda n, col=0: pl.BlockSpec((tm, n), lambda i: (i, col))
    return pl.pallas_call(
        body, name=name,
        out_shape=(jax.ShapeDtypeStruct((rows, d), BF16), jax.ShapeDtypeStruct((rows, d), BF16),
                   jax.ShapeDtypeStruct((rows, d), F32), jax.ShapeDtypeStruct((SUBLANES, LANES), F32),
                   jax.ShapeDtypeStruct((1, d), F32)),
        grid=(rows // tm,),
        in_specs=[row(d), row(d, r_block), _full((1, GLA_DV)), _full((d, d)), _full((1, d)), row(d), row(d)],
        out_specs=(row(d), row(d), row(d), _full((SUBLANES, LANES)), _full((1, d))),
        compiler_params=_params("arbitrary"),
    )(o, proj, head_gain, w_out, gain, x1, target)


def odd_out_bwd(dy, w_out, o, proj, head_gain, name):
    rows, d = dy.shape
    tm = min(ROW_TILE, rows)
    r_block = (2 * GLA_HEADS * GLA_DK + GLA_HEADS * GLA_DV) // d

    def body(dy_ref, w_ref, o_ref, r_ref, hg_ref, dr_ref, do_ref, dhg_ref):
        dy2 = _dot_nt(dy_ref[...], w_ref[...])
        hg = hg_ref[...]
        on, hats, rstds = _head_norm(o_ref[...], hg)
        r = r_ref[...]
        sr = _sigmoid(r)
        dr_ref[...] = (dy2 * jnp.concatenate(on, axis=1) * (sr * (1.0 + r * (1.0 - sr)))).astype(BF16)
        d_on = dy2 * (r * sr)
        d_os, dhg = [], None
        for h in range(GLA_HEADS):
            dn = d_on[:, h * GLA_DV:(h + 1) * GLA_DV]
            part = _colsum(dn * hats[h])
            dhg = part if dhg is None else dhg + part
            dng = dn * hg
            d_os.append(rstds[h] * (dng - hats[h] * jnp.mean(dng * hats[h], axis=-1, keepdims=True)))
        do_ref[...] = jnp.concatenate(d_os, axis=1)
        _accumulate(dhg_ref, dhg, pl.program_id(0))

    row = lambda n, col=0: pl.BlockSpec((tm, n), lambda i: (i, col))
    return pl.pallas_call(
        body, name=name,
        out_shape=(jax.ShapeDtypeStruct((rows, d), BF16), jax.ShapeDtypeStruct((rows, d), F32),
                   jax.ShapeDtypeStruct((1, GLA_DV), F32)),
        grid=(rows // tm,),
        in_specs=[row(d), _full((d, d)), row(d), row(d, r_block), _full((1, GLA_DV))],
        out_specs=(row(d), row(d), _full((1, GLA_DV))),
        compiler_params=_params("arbitrary"),
    )(dy, w_out, o, proj, head_gain)


def local_step(x, target, w, reduce_first=None, reduce_second=None, late_weights=None):
    g, g16 = {}, {}
    proj_e, h0 = norm_matmul(x, w["even_norm_pre"], w["even_w_in"], BF16, "even_in_proj")
    h_dir, acts = zip(*[rglru_fwd(proj_e, w["rg_conv_w"], w["rg_conv_b"], w["rg_gate_w"][d], w["rg_gate_b"][d],
                                  w["rg_lambda"][d], d == 1, "rglru_fwd_%d" % d) for d in range(2)])
    ycat = even_mix_fwd(proj_e, h_dir[0], h_dir[1], w["sc_conv_w"], "even_mix_fwd")
    if late_weights is not None:
        w = dict(w, **late_weights(ycat))
    x1, y_e = even_out_fwd(ycat, w["even_w_out"], w["even_norm_post"], x, "even_out_fwd")
    proj_o, h1 = norm_matmul(x1, w["odd_norm_pre"], w["odd_w_in"], F32, "odd_in_proj")
    o, st_dir = None, []
    for d in range(2):
        o, st = gla_fwd(proj_o, w["gla_wg_pad"], w["gla_b_gate"], o, d == 1, "gla_fwd_%d" % d)
        st_dir.append(st)
    y2, dy_o, dx2, loss, g["odd_norm_post"] = odd_out_fwd(
        o, proj_o, w["gla_norm_g"], w["odd_w_out"], w["odd_norm_post"], x1, target, "odd_out_fwd")
    g["odd_w_out"], g16["odd_w_out"] = (a[0] for a in matmul_dw(y2, dy_o, D_MODEL, "odd_w_out_grad"))
    dr, d_o, g["gla_norm_g"] = odd_out_bwd(dy_o, w["odd_w_out"], o, proj_o, w["gla_norm_g"], "odd_out_bwd")
    dq, dk, dv, dz_f = gla_bwd(proj_o, w["gla_wg_pad"], w["gla_b_gate"], d_o, st_dir[0], None, False, "gla_bwd_0")
    dq, dk, dv, dz_b = gla_bwd(proj_o, w["gla_wg_pad"], w["gla_b_gate"], d_o, st_dir[1], (dq, dk, dv), True,
                               "gla_bwd_1")
    dlr, g["gla_w_gate_lr"], g["gla_b_gate"] = gla_gate_bwd(proj_o, dz_f, dz_b, w["gla_wg_pad"], "gla_gate_bwd")
    dproj_o = [dq, dk, dv, dr, dlr]
    g["odd_w_in"] = jnp.concatenate(matmul_dw_pieces(h1, dproj_o, "odd_w_in_grad"), axis=1)[:, :ODD_IN]
    dx1, g["odd_norm_pre"] = inproj_bwd_pieces(dproj_o, w["odd_w_in"], x1, w["odd_norm_pre"], dx2, "odd_in_proj_bwd")
    dy_e, dycat, g["even_norm_post"] = even_out_bwd(dx1, y_e, w["even_norm_post"], w["even_w_out"], "even_out_bwd")
    g["even_w_out"], g16["even_w_out"] = (a[0] for a in matmul_dw(ycat, dy_e, D_MODEL, "even_w_out_grad"))
    lam = w["rg_lambda"] if reduce_first is None else w["rg_lambda"] + reduce_first(g, g16)
    dua, dgw, dgb, dlam = None, [], [], []
    for d in range(2):
        a, b, c, e = rglru_bwd(proj_e, dycat, h_dir[d], acts[d], w["rg_gate_w"][d], lam[d], dua, d == 1,
                               "rglru_bwd_%d" % d)
        dua = a
        dgw.append(b)
        dgb.append(c)
        dlam.append(e)
    dproj_e, g["rg_conv_w"], g["rg_conv_b"], g["sc_conv_w"] = even_mix_bwd(
        proj_e, dycat, h_dir[0], h_dir[1], dua, w["rg_conv_w"], w["sc_conv_w"], "even_mix_bwd")
    dgw = jnp.stack(dgw).reshape(2, RG_HEADS, RG_HEAD_DIM, 2, RG_HEAD_DIM)
    g["rg_gate_w"] = jnp.transpose(dgw, (0, 3, 1, 2, 4))
    g["rg_gate_b"] = jnp.stack(dgb).reshape(2, 2, RG_HEADS, RG_HEAD_DIM)
    g["rg_lambda"] = jnp.concatenate(dlam, axis=0)
    g["even_w_in"], g16["even_w_in"] = matmul_dw(h0, dproj_e, EVEN_IN // 4, "even_w_in_grad")
    gain = w["even_norm_pre"] if reduce_second is None else w["even_norm_pre"] + reduce_second(g, g16)
    grad_x, g["even_norm_pre"] = inproj_bwd(dproj_e, w["even_w_in"], x, gain, dx1, "even_in_proj_bwd")
    return loss, grad_x, g


def _prepare_weights(full):
    w = {}
    for name in ("even_norm_pre", "even_norm_post", "rg_conv_b", "odd_norm_pre", "odd_norm_post", "gla_norm_g"):
        if name in full:
            w[name] = full[name].reshape(1, -1)
    for name in ("rg_conv_w", "sc_conv_w"):
        if name in full:
            w[name] = full[name]
    for name in ("even_w_out", "odd_w_out"):
        if name in full:
            w[name] = full[name].astype(BF16)
    if "even_w_in" in full:
        w["even_w_in"] = full["even_w_in"].astype(BF16)
        if w["even_w_in"].ndim == 2:
            w["even_w_in"] = jnp.transpose(w["even_w_in"].reshape(D_MODEL, 4, EVEN_IN // 4), (1, 0, 2))
    if "rg_gate_w" in full:
        gw = jnp.transpose(full["rg_gate_w"].astype(BF16), (0, 2, 3, 1, 4))
        w["rg_gate_w"] = gw.reshape(2, RG_HEADS, RG_HEAD_DIM, 2 * RG_HEAD_DIM)
        w["rg_gate_b"] = full["rg_gate_b"].reshape(2, 2, D_MODEL)
        w["rg_lambda"] = full["rg_lambda"].reshape(2, 1, D_MODEL)
    if "odd_w_in" in full:
        w_in = jnp.pad(full["odd_w_in"].astype(BF16), ((0, 0), (0, ODD_IN_PAD - ODD_IN)))
        w["odd_w_in"] = w_in.reshape(1, D_MODEL, ODD_IN_PAD)
    if "gla_w_gate_lr" in full:
        wg = full["gla_w_gate_lr"].astype(BF16)
        w["gla_wg_pad"] = jnp.stack([jnp.pad(wg[d], ((d * GLA_RANK, LANES - (d + 1) * GLA_RANK), (0, 0)))
                                     for d in range(2)])
        w["gla_b_gate"] = full["gla_b_gate"].reshape(2, 1, GLA_HEADS * GLA_DK)
    return w


SHARDED_SMALL = (("rg_conv_w", (4, 256)), ("rg_lambda", (2, 256)), ("sc_conv_w", (3, 256)),
                 ("odd_norm_pre", (256,)), ("odd_norm_post", (256,)), ("gla_w_gate_lr", (2, 16, 128)),
                 ("gla_b_gate", (2, 128)), ("gla_norm_g", (64,)))
SHARDED_ROWS = 96
REPLICATED = (("rg_gate_w", (2, 2, 8, 128, 128)), ("even_norm_post", (1024,)), ("rg_conv_b", (1024,)),
              ("rg_gate_b", (2, 2, 8, 128)))
GATE_ROWS = 4096
LAST_REPLICATED = (("even_norm_pre", (1024,)),)
LAST_ROWS = 8
REPLICATED_ROWS = 4160
REP_PART = REPLICATED_ROWS // 8
HALF_SHARDED = SHARDED_ROWS // 2
PACK_HALF = HALF_SHARDED + REP_PART


def _seg_rows(shape):
    n = 1
    for s in shape:
        n *= s
    return -(-n // (SUBLANES * LANES)) * SUBLANES


def _pack(arrays, spec, total_rows, lead=()):
    parts = []
    for name, shape in spec:
        flat = arrays[name].reshape(lead + (-1,))
        pad = _seg_rows(shape) * LANES - flat.shape[-1]
        if pad:
            flat = jnp.pad(flat, [(0, 0)] * len(lead) + [(0, pad)])
        parts.append(flat.reshape(lead + (-1, LANES)))
    rows = jnp.concatenate(parts, axis=len(lead))
    pad = total_rows - rows.shape[len(lead)]
    return jnp.pad(rows, [(0, 0)] * len(lead) + [(0, pad), (0, 0)])


def _unpack(rows, spec, lead=()):
    out, at = {}, 0
    for name, shape in spec:
        n = 1
        for s in shape:
            n *= s
        k = _seg_rows(shape)
        seg = lax.slice_in_dim(rows, at, at + k, axis=len(lead)).reshape(lead + (-1,))
        out[name] = lax.slice_in_dim(seg, 0, n, axis=len(lead)).reshape(lead + shape)
        at += k
    return out


def _split_owners(arr):
    a = arr.reshape(arr.shape[:-1] + (4, arr.shape[-1] // 4))
    return jnp.moveaxis(a, -2, 0)


def _merge_owners(arr):
    a = jnp.moveaxis(arr, 0, -2)
    return a.reshape(a.shape[:-2] + (-1,))


HBM_SPEC = pl.BlockSpec(memory_space=pltpu.HBM)


def _position():
    x, y, c = lax.axis_index("x"), lax.axis_index("y"), lax.axis_index("c")
    chips = [(1 - x, y), (x, 1 - y), (1 - x, 1 - y)]
    return x, y, c, chips


def _remote(src, dst, send_sem, recv_sem, device):
    return pltpu.make_async_remote_copy(src_ref=src, dst_ref=dst, send_sem=send_sem, recv_sem=recv_sem,
                                        device_id=device, device_id_type=MESH)


SEM_SPEC = pl.BlockSpec(memory_space=pltpu.SEMAPHORE)
SIDE_EFFECT = pltpu.SideEffectType.DATAFLOW_SIDE_EFFECTING


def _gather_copies(ins, lands, n_h, send_sems, recv_sems):
    x, y, c, chips = _position()
    me = 2 * x + y
    copies = []
    for a in range(len(ins)):
        for k, chip in enumerate(chips):
            src = ins[a].at[c] if a < n_h else ins[a]
            dst = lands[a].at[me, c] if a < n_h else lands[a].at[me]
            copies.append(_remote(src, dst, send_sems.at[3 * a + k], recv_sems.at[3 * a + k], (chip[0], chip[1], c)))
    return copies


def gather_start(halved, whole, name):
    arrays = list(halved) + list(whole)
    n, n_h = len(arrays), len(halved)
    lands = [lax.empty((4,) + a.shape, a.dtype) for a in arrays]

    def body(*refs):
        ins, lz, send_sems, recv_sems, token = refs[:n], refs[n:2 * n], refs[2 * n], refs[2 * n + 1], refs[-1]
        for cp in _gather_copies(ins, lz, n_h, send_sems, recv_sems):
            cp.start()
        token[...] = jnp.zeros_like(token)

    operands = [pltpu.with_memory_space_constraint(a, pltpu.HBM) for a in arrays + lands]
    return pl.pallas_call(
        body, name=name,
        out_shape=(pltpu.SemaphoreType.DMA((3 * n,)), pltpu.SemaphoreType.DMA((3 * n,)))
        + tuple(pltpu.HBM(a.shape, a.dtype) for a in operands) + (jax.ShapeDtypeStruct((SUBLANES, LANES), F32),),
        in_specs=[HBM_SPEC] * (2 * n),
        out_specs=(SEM_SPEC, SEM_SPEC) + (HBM_SPEC,) * (2 * n) + (pl.BlockSpec(memory_space=pltpu.VMEM),),
        input_output_aliases={i: 2 + i for i in range(2 * n)},
        compiler_params=pltpu.CompilerParams(has_side_effects=SIDE_EFFECT),
    )(*operands)


def gather_wait(started, n_h, after, name):
    send_sems, recv_sems = started[0], started[1]
    operands = list(started[2:-1])
    n = len(operands) // 2

    def body(*refs):
        ins, lz, send_ref, recv_ref = refs[:n], refs[n:2 * n], refs[2 * n], refs[2 * n + 1]
        for cp in _gather_copies(ins, lz, n_h, send_ref, recv_ref):
            cp.wait_send()
            cp.wait_recv()

    outs = pl.pallas_call(
        body, name=name,
        out_shape=tuple(pltpu.HBM(a.shape, a.dtype) for a in operands),
        in_specs=[HBM_SPEC] * (2 * n) + [SEM_SPEC, SEM_SPEC, pl.BlockSpec(memory_space=pl.ANY)],
        out_specs=(HBM_SPEC,) * (2 * n),
        input_output_aliases={i: i for i in range(2 * n)},
        compiler_params=pltpu.CompilerParams(has_side_effects=SIDE_EFFECT),
    )(*operands, send_sems, recv_sems, after)
    return outs[n:]


def pass_to_sibling(fulls, name):
    n = len(fulls)

    def body(*refs):
        bufs = refs[n:2 * n]
        send_sems, recv_sems = refs[2 * n:]
        x, y, c, chips = _position()
        sibling = (x, y, 1 - c)
        copies = []
        for a in range(n):
            for k, chip in enumerate(chips):
                q = 2 * chip[0] + chip[1]
                cp = _remote(bufs[a].at[q, c], bufs[a].at[q, c], send_sems.at[3 * a + k], recv_sems.at[3 * a + k],
                             sibling)
                cp.start()
                copies.append(cp)
        for a in range(n):
            for k, chip in enumerate(chips):
                q = 2 * chip[0] + chip[1]
                passed = bufs[a].at[q, 1 - c]
                _remote(passed, passed, send_sems.at[3 * a + k], recv_sems.at[3 * a + k], sibling).wait_recv()
        for cp in copies:
            cp.wait_send()

    return pl.pallas_call(
        body, name=name,
        out_shape=[jax.ShapeDtypeStruct(a.shape, a.dtype) for a in fulls],
        in_specs=[HBM_SPEC] * n, out_specs=[HBM_SPEC] * n,
        input_output_aliases={i: i for i in range(n)},
        scratch_shapes=[pltpu.SemaphoreType.DMA((3 * n,)), pltpu.SemaphoreType.DMA((3 * n,))],
    )(*fulls)


def place_own(full, own, chip, name):
    _, _, r, cols = full.shape
    tr = _row_tile(r, cols)

    def body(p_ref, own_ref, full_ref, o_ref):
        o_ref[0] = own_ref[...]

    return pl.pallas_call(
        body, name=name,
        out_shape=jax.ShapeDtypeStruct(full.shape, full.dtype),
        grid_spec=pltpu.PrefetchScalarGridSpec(
            num_scalar_prefetch=1, grid=(2, r // tr),
            in_specs=[pl.BlockSpec((1, tr, cols), lambda h, i, p_ref: (h, i, 0)), pl.BlockSpec(memory_space=pl.ANY)],
            out_specs=pl.BlockSpec((1, 1, tr, cols), lambda h, i, p_ref: (p_ref[0], h, i, 0))),
        input_output_aliases={2: 0},
        compiler_params=_params("parallel", "parallel"),
    )(chip, own, full)


def exchange_with_sibling(arrays, name):
    n = len(arrays)

    def body(*refs):
        ins, outs = refs[:n], refs[n:2 * n]
        send_sems, recv_sems = refs[2 * n:]
        x, y, c, _ = _position()
        copies = []
        for a in range(n):
            cp = _remote(ins[a].at[:, 1 - c], outs[a], send_sems.at[a], recv_sems.at[a], (x, y, 1 - c))
            cp.start()
            copies.append(cp)
        for cp in copies:
            cp.wait()

    return pl.pallas_call(
        body, name=name,
        out_shape=[jax.ShapeDtypeStruct((a.shape[0],) + a.shape[2:], a.dtype) for a in arrays],
        in_specs=[HBM_SPEC] * n, out_specs=[HBM_SPEC] * n,
        scratch_shapes=[pltpu.SemaphoreType.DMA((n,)), pltpu.SemaphoreType.DMA((n,))],
    )(*arrays)


def _chip_copies(ins, lands, send_sems, recv_sems):
    x, y, c, chips = _position()
    copies = []
    for a in range(len(ins)):
        for k, chip in enumerate(chips):
            q = 2 * chip[0] + chip[1]
            copies.append(_remote(ins[a].at[q], lands[a].at[k], send_sems.at[3 * a + k], recv_sems.at[3 * a + k],
                                  (chip[0], chip[1], c)))
    return copies


def exchange_with_chips_start(arrays, name):
    n = len(arrays)
    lands = [lax.empty((3,) + a.shape[1:], a.dtype) for a in arrays]

    def body(*refs):
        ins, lz, send_sems, recv_sems, token = refs[:n], refs[n:2 * n], refs[2 * n], refs[2 * n + 1], refs[-1]
        for cp in _chip_copies(ins, lz, send_sems, recv_sems):
            cp.start()
        token[...] = jnp.zeros_like(token)

    operands = [pltpu.with_memory_space_constraint(a, pltpu.HBM) for a in list(arrays) + lands]
    return pl.pallas_call(
        body, name=name,
        out_shape=(pltpu.SemaphoreType.DMA((3 * n,)), pltpu.SemaphoreType.DMA((3 * n,)))
        + tuple(pltpu.HBM(a.shape, a.dtype) for a in operands) + (jax.ShapeDtypeStruct((SUBLANES, LANES), F32),),
        in_specs=[HBM_SPEC] * (2 * n),
        out_specs=(SEM_SPEC, SEM_SPEC) + (HBM_SPEC,) * (2 * n) + (pl.BlockSpec(memory_space=pltpu.VMEM),),
        input_output_aliases={i: 2 + i for i in range(2 * n)},
        compiler_params=pltpu.CompilerParams(has_side_effects=SIDE_EFFECT),
    )(*operands)


def exchange_with_chips_wait(started, after, name):
    send_sems, recv_sems = started[0], started[1]
    operands = list(started[2:-1])
    n = len(operands) // 2

    def body(*refs):
        ins, lz, send_ref, recv_ref = refs[:n], refs[n:2 * n], refs[2 * n], refs[2 * n + 1]
        for cp in _chip_copies(ins, lz, send_ref, recv_ref):
            cp.wait_send()
            cp.wait_recv()

    outs = pl.pallas_call(
        body, name=name,
        out_shape=tuple(pltpu.HBM(a.shape, a.dtype) for a in operands),
        in_specs=[HBM_SPEC] * (2 * n) + [SEM_SPEC, SEM_SPEC, pl.BlockSpec(memory_space=pl.ANY)],
        out_specs=(HBM_SPEC,) * (2 * n),
        input_output_aliases={i: i for i in range(2 * n)},
        compiler_params=pltpu.CompilerParams(has_side_effects=SIDE_EFFECT),
    )(*operands, send_sems, recv_sems, after)
    return outs[:n], outs[n:]


def share_totals(totals, pack_total, last_part):
    arrays = list(totals) + [pack_total]
    n = len(arrays)

    def body(*refs):
        ins, last, outs, rep, last_all = refs[:n], refs[n], refs[n + 1:2 * n + 1], refs[2 * n + 1], refs[2 * n + 2]
        send_sems, recv_sems, rep_send, rep_recv, last_send, last_recv = refs[2 * n + 3:]
        x, y, c, chips = _position()
        sibling = (x, y, 1 - c)
        me = 4 * x + 2 * y + c
        sends = []
        for a in range(n):
            cp = _remote(ins[a], outs[a], send_sems.at[a], recv_sems.at[a], sibling)
            cp.start()
            sends.append(cp)
        mine = ins[n - 1].at[pl.ds(HALF_SHARDED, REP_PART)]
        peers = [sibling]
        for chip in chips:
            peers += [(chip[0], chip[1], c), (chip[0], chip[1], 1 - c)]
        for j, peer in enumerate(peers):
            for src, dst, s_sem, r_sem in ((mine, rep, rep_send, rep_recv), (last, last_all, last_send, last_recv)):
                cp = _remote(src, dst.at[me], s_sem.at[j], r_sem.at[j], peer)
                cp.start()
                sends.append(cp)
        for a in range(n):
            _remote(outs[a], outs[a], send_sems.at[a], recv_sems.at[a], sibling).wait_recv()
        for j, peer in enumerate(peers):
            it = 4 * peer[0] + 2 * peer[1] + peer[2]
            _remote(rep.at[it], rep.at[it], rep_send.at[j], rep_recv.at[j], peer).wait_recv()
            _remote(last_all.at[it], last_all.at[it], last_send.at[j], last_recv.at[j], peer).wait_recv()
        for cp in sends:
            cp.wait_send()

    outs = pl.pallas_call(
        body, name="grad_share_totals",
        out_shape=[jax.ShapeDtypeStruct(a.shape, a.dtype) for a in arrays]
        + [jax.ShapeDtypeStruct((8, REP_PART, LANES), F32), jax.ShapeDtypeStruct((8,) + last_part.shape, F32)],
        in_specs=[HBM_SPEC] * (n + 1), out_specs=[HBM_SPEC] * (n + 2),
        scratch_shapes=[pltpu.SemaphoreType.DMA((n,)), pltpu.SemaphoreType.DMA((n,))]
        + [pltpu.SemaphoreType.DMA((7,))] * 4,
    )(*arrays, last_part)
    return outs[:n], outs[n], outs[n + 1]


def sum_parts(parts, name):
    def body(p_ref, o_ref):
        total = p_ref[0]
        for k in range(1, parts.shape[0]):
            total = total + p_ref[k]
        o_ref[...] = total

    return pl.pallas_call(body, name=name, out_shape=jax.ShapeDtypeStruct(parts.shape[1:], parts.dtype))(parts)


TILE_BYTES = 2 << 20


def _row_tile(rows, cols):
    best = None
    for t in range(SUBLANES, rows + 1, SUBLANES):
        if rows % t == 0 and t * cols * 4 <= TILE_BYTES:
            best = t
    return best if best is not None else rows


def add_sibling(mine, received, core, out_dtype, name):
    _, _, r, cols = mine.shape
    tr = _row_tile(r, cols)

    def body(c_ref, a_ref, b_ref, o_ref):
        o_ref[...] = (a_ref[0] + b_ref[...].astype(F32)).astype(out_dtype)

    return pl.pallas_call(
        body, name=name,
        out_shape=jax.ShapeDtypeStruct((4, r, cols), out_dtype),
        grid_spec=pltpu.PrefetchScalarGridSpec(
            num_scalar_prefetch=1, grid=(4, r // tr),
            in_specs=[pl.BlockSpec((1, 1, tr, cols), lambda o, i, c_ref: (o, c_ref[0], i, 0)),
                      pl.BlockSpec((1, tr, cols), lambda o, i, c_ref: (o, i, 0))],
            out_specs=pl.BlockSpec((1, tr, cols), lambda o, i, c_ref: (o, i, 0))),
        compiler_params=_params("parallel", "parallel"),
    )(core, mine, received)


def add_chips(own, received, chip, name):
    _, r, cols = own.shape
    tr = _row_tile(r, cols)

    def body(p_ref, a_ref, b0, b1, b2, o_ref):
        o_ref[...] = ((a_ref[0].astype(F32) + b0[0].astype(F32)) + b1[0].astype(F32)) + b2[0].astype(F32)

    rb = lambda k: pl.BlockSpec((1, tr, cols), lambda i, p_ref: (k, i, 0))
    return pl.pallas_call(
        body, name=name,
        out_shape=jax.ShapeDtypeStruct((r, cols), F32),
        grid_spec=pltpu.PrefetchScalarGridSpec(
            num_scalar_prefetch=1, grid=(r // tr,),
            in_specs=[pl.BlockSpec((1, tr, cols), lambda i, p_ref: (p_ref[0], i, 0)), rb(0), rb(1), rb(2)],
            out_specs=pl.BlockSpec((tr, cols), lambda i, p_ref: (i, 0))),
        compiler_params=_params("parallel"),
    )(chip, own, received, received, received)


def _adamw_update(gv, w_ref, m_ref, v_ref, d_ref, nm_ref, nv_ref):
    nm = ADAM_B1 * m_ref[...] + (1.0 - ADAM_B1) * gv
    nv = ADAM_B2 * v_ref[...] + (1.0 - ADAM_B2) * (gv * gv)
    nm_ref[...] = nm
    nv_ref[...] = nv
    m_hat = nm / (1.0 - ADAM_B1 ** ADAM_STEP)
    v_hat = nv / (1.0 - ADAM_B2 ** ADAM_STEP)
    d_ref[...] = -ADAM_LR * (m_hat / (jnp.sqrt(v_hat) + ADAM_EPS) + ADAM_WD * w_ref[...])


def adamw_halves(w, own, received, m, v, core, name, by_columns=False):
    rows, cols = w.shape

    def body(c_ref, w_ref, own_ref, rec_ref, m_ref, v_ref, g_ref, d_ref, nm_ref, nv_ref):
        gv = jnp.where(pl.program_id(0) == c_ref[0], own_ref[...], rec_ref[...])
        g_ref[...] = gv
        _adamw_update(gv, w_ref, m_ref, v_ref, d_ref, nm_ref, nv_ref)

    if by_columns:
        nr = 1
        whole = pl.BlockSpec((rows, cols // 2), lambda h, i, c_ref: (0, h))
        half = pl.BlockSpec((rows, cols // 2), lambda h, i, c_ref: (0, 0))
    else:
        r = rows // 2
        tr = _row_tile(r, cols)
        nr = r // tr
        whole = pl.BlockSpec((tr, cols), lambda h, i, c_ref: (h * nr + i, 0))
        half = pl.BlockSpec((tr, cols), lambda h, i, c_ref: (i, 0))
    return pl.pallas_call(
        body, name=name,
        out_shape=(jax.ShapeDtypeStruct((rows, cols), F32),) * 4,
        grid_spec=pltpu.PrefetchScalarGridSpec(
            num_scalar_prefetch=1, grid=(2, nr),
            in_specs=[whole, half, half, whole, whole], out_specs=(whole,) * 4),
        compiler_params=_params("parallel", "parallel"),
    )(core, w, own, received, m, v)


def adamw_many(ws, gs, ms, vs, name):
    n = len(ws)

    def body(*refs):
        ins, outs = refs[:4 * n], refs[4 * n:]
        for k in range(n):
            w_ref, g_ref, m_ref, v_ref = (ins[j * n + k] for j in range(4))
            d_ref, nm_ref, nv_ref = outs[3 * k:3 * k + 3]
            _adamw_update(g_ref[...], w_ref, m_ref, v_ref, d_ref, nm_ref, nv_ref)

    flat = pl.pallas_call(
        body, name=name,
        out_shape=[jax.ShapeDtypeStruct(w.shape, F32) for w in ws for _ in range(3)],
    )(*ws, *gs, *ms, *vs)
    return [tuple(flat[3 * k:3 * k + 3]) for k in range(n)]


def adamw(w, g, m, v, name):
    r, cols = w.shape
    tr = _row_tile(r, cols)

    def body(w_ref, g_ref, m_ref, v_ref, g_out, d_ref, nm_ref, nv_ref):
        gv = g_ref[...]
        g_out[...] = gv
        _adamw_update(gv, w_ref, m_ref, v_ref, d_ref, nm_ref, nv_ref)

    blk = pl.BlockSpec((tr, cols), lambda i: (i, 0))
    return pl.pallas_call(
        body, name=name,
        out_shape=(jax.ShapeDtypeStruct((r, cols), F32),) * 4,
        grid=(r // tr,),
        in_specs=[blk] * 4, out_specs=(blk,) * 4,
        compiler_params=_params("parallel"),
    )(w, g, m, v)


WEIGHTS = ("even_norm_pre", "even_norm_post", "even_w_in", "rg_conv_w", "rg_conv_b", "rg_gate_w", "rg_gate_b",
           "rg_lambda", "sc_conv_w", "even_w_out", "odd_norm_pre", "odd_norm_post", "odd_w_in", "gla_w_gate_lr",
           "gla_b_gate", "gla_norm_g", "odd_w_out")
BIG = ("even_w_in", "even_w_out", "odd_w_in", "odd_w_out")


def _halves(a):
    return a.reshape((2, a.shape[0] // 2) + a.shape[1:])


def kernel(x, even_norm_pre, even_norm_post, even_w_in, rg_conv_w, rg_conv_b, rg_gate_w, rg_gate_b, rg_lambda, sc_conv_w, even_w_out, odd_norm_pre, odd_norm_post, odd_w_in, gla_w_gate_lr, gla_b_gate, gla_norm_g, odd_w_out, loss_target, m_even_norm_pre, m_even_norm_post, m_even_w_in, m_rg_conv_w, m_rg_conv_b, m_rg_gate_w, m_rg_gate_b, m_rg_lambda, m_sc_conv_w, m_even_w_out, m_odd_norm_pre, m_odd_norm_post, m_odd_w_in, m_gla_w_gate_lr, m_gla_b_gate, m_gla_norm_g, m_odd_w_out, v_even_norm_pre, v_even_norm_post, v_even_w_in, v_rg_conv_w, v_rg_conv_b, v_rg_gate_w, v_rg_gate_b, v_rg_lambda, v_sc_conv_w, v_even_w_out, v_odd_norm_pre, v_odd_norm_post, v_odd_w_in, v_gla_w_gate_lr, v_gla_b_gate, v_gla_norm_g, v_odd_w_out):
    given = dict(locals())
    shard = {n: given[n][0] for n in WEIGHTS}
    m_in = {n: given["m_" + n][0] for n in WEIGHTS}
    v_in = {n: given["v_" + n][0] for n in WEIGHTS}
    mx, my, mc = lax.axis_index("x"), lax.axis_index("y"), lax.axis_index("c")
    core = jnp.reshape(mc, (1,)).astype(jnp.int32)
    chip = jnp.reshape(2 * mx + my, (1,)).astype(jnp.int32)

    small_shard = _pack(shard, SHARDED_SMALL, SHARDED_ROWS)
    big_own = [_halves(shard[n].astype(BF16)) for n in BIG]
    started_a = gather_start(big_own[:1], [small_shard], "gather_start_a")
    started_b = gather_start(big_own[1:], [], "gather_start_b")
    even_w_in_full, small_full = gather_wait(started_a, 1, started_b[-1], "gather_wait_a")
    (even_w_in_full,) = pass_to_sibling([even_w_in_full], "gather_pass_a")
    even_w_in_full = place_own(even_w_in_full, big_own[0], chip, "place_even_w_in")
    small_full = lax.dynamic_update_slice(small_full, small_shard[None], (chip[0], 0, 0))
    full = {n: shard[n] for n, _ in REPLICATED + LAST_REPLICATED}
    full.update({n: _merge_owners(a) for n, a in _unpack(small_full, SHARDED_SMALL, lead=(4,)).items()})
    full["even_w_in"] = even_w_in_full.reshape(4, D_MODEL, EVEN_IN // 4)

    def late_weights(after):
        lands = pass_to_sibling(list(gather_wait(started_b, 3, after, "gather_wait_b")), "gather_pass_b")
        lands = [place_own(a, b, chip, "place_" + n) for a, b, n in zip(lands, big_own[1:], BIG[1:])]
        odd_w_in = jnp.transpose(lands[1].reshape(4, D_MODEL, ODD_IN // 4), (1, 0, 2)).reshape(D_MODEL, ODD_IN)
        return _prepare_weights({"even_w_out": lands[0].reshape(2 * D_MODEL, D_MODEL), "odd_w_in": odd_w_in,
                                 "odd_w_out": lands[2].reshape(D_MODEL, D_MODEL)})

    pending = {}

    def slab(a):
        return a.reshape((4, 2, a.shape[1] // 2) + a.shape[2:])

    def begin(tag, slabs, to_send, dtypes):
        got = exchange_with_sibling(to_send, "grad_sibling_" + tag)
        sums = [add_sibling(a, b, core, dt, "grad_add_sibling_%s%d" % (tag, i))
                for i, (a, b, dt) in enumerate(zip(slabs, got, dtypes))]
        pending[tag] = exchange_with_chips_start(sums, "grad_chips_start_" + tag)
        return pending[tag][-1][0, 0]

    def finish(tag, after):
        sums, got = exchange_with_chips_wait(pending[tag], after, "grad_chips_wait_" + tag)
        return [add_chips(a, b, chip, "grad_add_chips_%s%d" % (tag, i)) for i, (a, b) in enumerate(zip(sums, got))]

    def reduce_first(g, g16):
        odd_w_in = slab(jnp.transpose(g["odd_w_in"].reshape(D_MODEL, 4, ODD_IN // 4), (1, 0, 2)))
        slabs = [odd_w_in] + [slab(g[n].reshape(4, -1, D_MODEL)) for n in ("odd_w_out", "even_w_out")]
        to_send = [odd_w_in.astype(BF16)] + [slab(g16[n].reshape(4, -1, D_MODEL)) for n in ("odd_w_out", "even_w_out")]
        return begin("a", slabs, to_send, [BF16] * 3)

    def reduce_second(g, g16):
        pending["totals_a"] = finish("a", g["even_w_in"])
        rep_rows = _pack(g, REPLICATED, REPLICATED_ROWS).reshape(4, 2, REP_PART, LANES)
        sh_rows = _pack({n: _split_owners(g[n]) for n, _ in SHARDED_SMALL}, SHARDED_SMALL, SHARDED_ROWS, lead=(4,))
        pack = jnp.concatenate([sh_rows.reshape(4, 2, HALF_SHARDED, LANES), rep_rows], axis=2)
        return begin("b", [slab(g["even_w_in"]), pack], [slab(g16["even_w_in"]), pack], [BF16, F32])

    loss, grad_x, g = local_step(x[0], loss_target[0], _prepare_weights(full), reduce_first, reduce_second,
                                 late_weights)
    odd_w_in_t, odd_w_out_t, even_w_out_t = pending["totals_a"]
    even_w_in_t, pack_t = finish("b", grad_x)
    totals = [even_w_in_t, even_w_out_t, odd_w_in_t, odd_w_out_t]
    last_part = jnp.concatenate([_pack(g, LAST_REPLICATED, LAST_ROWS), loss])
    from_core, rep_all, last_all = share_totals(totals, pack_t, last_part)
    me = 2 * chip[0] + core[0]
    mine, theirs = pack_t[:HALF_SHARDED], from_core[4][:HALF_SHARDED]
    sh_total = jnp.where(mc == 0, jnp.concatenate([mine, theirs]), jnp.concatenate([theirs, mine]))
    rep_all = lax.dynamic_update_slice(rep_all, pack_t[None, HALF_SHARDED:], (me, 0, 0))
    rep_total = rep_all.reshape(REPLICATED_ROWS, LANES)
    last_total = sum_parts(lax.dynamic_update_slice(last_all, last_part[None], (me, 0, 0)), "grad_sum_last")
    last_total, loss = last_total[:LAST_ROWS], last_total[LAST_ROWS, 0]
    grads = {}

    delta, new_m, new_v = {}, {}, {}
    for i, n in enumerate(BIG):
        if shard[n].shape[1] % LANES:
            outs = adamw_halves(shard[n].T, totals[i].T, from_core[i].T, m_in[n].T, v_in[n].T, core, "adamw_" + n,
                                by_columns=True)
            grads[n], delta[n], new_m[n], new_v[n] = [o.T for o in outs]
        else:
            grads[n], delta[n], new_m[n], new_v[n] = adamw_halves(shard[n], totals[i], from_core[i], m_in[n],
                                                                  v_in[n], core, "adamw_" + n)
    gate = [src["rg_gate_w"].reshape(GATE_ROWS, LANES) for src in (shard, m_in, v_in)]
    grads["rg_gate_w"], delta["rg_gate_w"], new_m["rg_gate_w"], new_v["rg_gate_w"] = adamw(
        gate[0], rep_total, gate[1], gate[2], "adamw_rg_gate_w")
    rest = REPLICATED[1:]
    rest_rows = sum(_seg_rows(shape) for _, shape in rest)
    grads.update(_unpack(sh_total, SHARDED_SMALL))
    grads.update(_unpack(rep_total[GATE_ROWS:GATE_ROWS + rest_rows], rest))
    grads.update(_unpack(last_total, LAST_REPLICATED))
    names = [n for n, _ in SHARDED_SMALL + rest + LAST_REPLICATED]
    rows_of = lambda a, n: a.reshape(-1, given[n].shape[-1])
    outs = adamw_many([rows_of(given[n], n) for n in names], [rows_of(grads[n], n) for n in names],
                      [rows_of(given["m_" + n], n) for n in names], [rows_of(given["v_" + n], n) for n in names],
                      "adamw_small")
    for n, (d, nm, nv) in zip(names, outs):
        delta[n], new_m[n], new_v[n] = d, nm, nv
    result = [loss, grad_x[None]]
    for group in (grads, delta, new_m, new_v):
        result += [group[n].reshape(given[n].shape) for n in WEIGHTS]
    return tuple(result)
```

```python
import jax
import jax.numpy as jnp
from jax import lax
from jax.experimental import pallas as pl
from jax.experimental.pallas import tpu as pltpu

F32 = jnp.float32
BF16 = jnp.bfloat16
MESH = pl.DeviceIdType.MESH

D_MODEL = 1024
NORM_EPS = 1e-6
RG_HEADS = 8
RG_HEAD_DIM = 128
RG_C = 8.0
EVEN_IN = 6144
ODD_IN = 3104
ODD_IN_PAD = 3200
GLA_HEADS = 4
GLA_DK = 128
GLA_DV = 256
GLA_RANK = 16
GLA_NORMALIZER = 16.0
GLA_CHUNK = 128
LR_COL = 3072

ADAM_LR = 0.001
ADAM_B1 = 0.9
ADAM_B2 = 0.999
ADAM_EPS = 1e-08
ADAM_WD = 0.01
ADAM_STEP = 10

SUBLANES = 8
HALO = 16
LANES = 128
VMEM_LIMIT = 56 * 2 ** 20

ROW_TILE = 512
SCAN_TILE = 256
GLA_BLOCK = 2048
MIX_TILE = 256


def _params(*sem):
    return pltpu.CompilerParams(dimension_semantics=sem, vmem_limit_bytes=VMEM_LIMIT)


def _full(shape):
    n = len(shape)
    return pl.BlockSpec(shape, lambda *_: (0,) * n)


def _sigmoid(x):
    return 0.5 + 0.5 * jnp.tanh(0.5 * x)


def _softplus(x):
    return jnp.maximum(x, 0.0) + jnp.log(1.0 + jnp.exp(-jnp.abs(x)))


def _dot(a, b):
    return jnp.dot(a, b, preferred_element_type=F32)


def _dot_nt(a, b):
    return lax.dot_general(a, b, (((1,), (1,)), ((), ())), preferred_element_type=F32)


def _dot_tn(a, b):
    return lax.dot_general(a, b, (((0,), (0,)), ((), ())), preferred_element_type=F32)


def _bdot(a, b, ca, cb):
    return lax.dot_general(a, b, (((ca,), (cb,)), ((0,), (0,))), preferred_element_type=F32)


def _halo_specs(rows, cols, col_block, n_row_tiles, tix):
    per = rows // HALO
    last = n_row_tiles * per - 1

    def split(args):
        if len(args) == 2:
            return tix(args[1]), col_block + args[0]
        return tix(args[0]), col_block

    def prev(*args):
        t, c = split(args)
        return (jnp.maximum(t * per - 1, 0), c)

    def main(*args):
        return split(args)

    def nxt(*args):
        t, c = split(args)
        return (jnp.minimum((t + 1) * per, last), c)

    return [pl.BlockSpec((HALO, cols), prev), pl.BlockSpec((rows, cols), main),
            pl.BlockSpec((HALO, cols), nxt)]


def _extend(prev_ref, main_ref, next_ref, is_first, is_last):
    p = jnp.where(is_first, 0.0, prev_ref[...].astype(F32))
    n = jnp.where(is_last, 0.0, next_ref[...].astype(F32))
    return jnp.concatenate([p, main_ref[...].astype(F32), n], axis=0)


def _shifted(ext, offset, rows):
    if offset == 0:
        return ext[HALO:HALO + rows]
    n = ext.shape[0]
    return pltpu.roll(ext, (-offset) % n, 0)[HALO:HALO + rows]


def _conv(ext, w, left, rows):
    out = None
    for k in range(w.shape[0]):
        term = _shifted(ext, k - left, rows) * w[k:k + 1]
        out = term if out is None else out + term
    return out


def _conv_transpose(ext, w, left, rows):
    out = None
    for k in range(w.shape[0]):
        term = _shifted(ext, left - k, rows) * w[k:k + 1]
        out = term if out is None else out + term
    return out


def _colsum(x):
    return jnp.sum(x, axis=0, keepdims=True)


def _accumulate(ref, value, step):
    @pl.when(step == 0)
    def _():
        ref[...] = value

    @pl.when(step > 0)
    def _():
        ref[...] += value


PROJ_TILE_BYTES = 7 * 2 ** 20


def _proj_row_tile(rows, width, dtype):
    tm = min(ROW_TILE, rows)
    while tm * width * jnp.dtype(dtype).itemsize > PROJ_TILE_BYTES and tm % (2 * HALO) == 0:
        tm //= 2
    return tm


def norm_matmul(x, gain, w, out_dtype, name):
    rows, d = x.shape
    n_col_tiles, _, tn = w.shape
    tm = _proj_row_tile(rows, n_col_tiles * tn, out_dtype)

    def body(x_ref, g_ref, w_ref, proj_ref, h_ref):
        xv = x_ref[...]
        rstd = lax.rsqrt(jnp.mean(xv * xv, axis=-1, keepdims=True) + NORM_EPS)
        hv = (xv * rstd * g_ref[...]).astype(BF16)
        h_ref[...] = hv
        for j in range(n_col_tiles):
            proj_ref[:, j * tn:(j + 1) * tn] = _dot(hv, w_ref[j]).astype(out_dtype)

    row = lambda cols: pl.BlockSpec((tm, cols), lambda i: (i, 0))
    return pl.pallas_call(
        body, name=name,
        out_shape=(jax.ShapeDtypeStruct((rows, n_col_tiles * tn), out_dtype), jax.ShapeDtypeStruct((rows, d), BF16)),
        grid=(rows // tm,),
        in_specs=[row(d), _full((1, d)), _full(w.shape)],
        out_specs=(row(n_col_tiles * tn), row(d)),
        compiler_params=_params("parallel"),
    )(x, gain, w)


def inproj_bwd(dproj, w, x, gain, dres, name):
    rows, d = x.shape
    n_col_tiles, _, tn = w.shape
    tm = _proj_row_tile(rows, n_col_tiles * tn, dproj.dtype)

    def body(dp_ref, w_ref, x_ref, g_ref, dres_ref, dx_ref, dg_ref):
        dh = None
        for j in range(n_col_tiles):
            part = _dot_nt(dp_ref[:, j * tn:(j + 1) * tn], w_ref[j])
            dh = part if dh is None else dh + part
        _inproj_finish(dh, x_ref, g_ref, dres_ref, dx_ref, dg_ref, pl.program_id(0))

    row = lambda cols: pl.BlockSpec((tm, cols), lambda i: (i, 0))
    return pl.pallas_call(
        body, name=name,
        out_shape=(jax.ShapeDtypeStruct((rows, d), F32), jax.ShapeDtypeStruct((1, d), F32)),
        grid=(rows // tm,),
        in_specs=[row(n_col_tiles * tn), _full(w.shape), row(d), _full((1, d)), row(d)],
        out_specs=(row(d), _full((1, d))),
        compiler_params=_params("arbitrary"),
    )(dproj, w, x, gain, dres)


def _inproj_finish(dh, x_ref, g_ref, dres_ref, dx_ref, dg_ref, step):
    xv = x_ref[...]
    rstd = lax.rsqrt(jnp.mean(xv * xv, axis=-1, keepdims=True) + NORM_EPS)
    xhat = xv * rstd
    dxn = dh * g_ref[...]
    dx_ref[...] = dres_ref[...] + rstd * (dxn - xhat * jnp.mean(dxn * xhat, axis=-1, keepdims=True))
    _accumulate(dg_ref, _colsum(dh * xhat), step)


def inproj_bwd_pieces(pieces, w, x, gain, dres, name):
    rows, d = x.shape
    tm = min(ROW_TILE, rows)
    n = len(pieces)
    widths = [p.shape[1] for p in pieces]
    assert sum(widths) == w.shape[2]

    def body(*refs):
        w_ref, x_ref, g_ref, dres_ref, dx_ref, dg_ref = refs[n:]
        dproj = jnp.concatenate([refs[k][...] for k in range(n)], axis=1)
        _inproj_finish(_dot_nt(dproj, w_ref[0]), x_ref, g_ref, dres_ref, dx_ref, dg_ref, pl.program_id(0))

    row = lambda cols: pl.BlockSpec((tm, cols), lambda i: (i, 0))
    return pl.pallas_call(
        body, name=name,
        out_shape=(jax.ShapeDtypeStruct((rows, d), F32), jax.ShapeDtypeStruct((1, d), F32)),
        grid=(rows // tm,),
        in_specs=[row(wd) for wd in widths] + [_full(w.shape), row(d), _full((1, d)), row(d)],
        out_specs=(row(d), _full((1, d))),
        compiler_params=_params("arbitrary"),
    )(*pieces, w, x, gain, dres)


def matmul_dw_pieces(a, pieces, name):
    rows, m = a.shape
    tk = min(2 * ROW_TILE, rows)
    n = len(pieces)
    width = sum(p.shape[1] for p in pieces)

    def body(*refs):
        a_ref, ins, out = refs[0], refs[1:1 + n], refs[1 + n]
        b = jnp.concatenate([ref[...] for ref in ins], axis=1)
        _accumulate(out, _dot_tn(a_ref[...], b), pl.program_id(0))

    return pl.pallas_call(
        body, name=name,
        out_shape=jax.ShapeDtypeStruct((m, width), F32),
        grid=(rows // tk,),
        in_specs=[pl.BlockSpec((tk, m), lambda k: (k, 0))]
        + [pl.BlockSpec((tk, p.shape[1]), lambda k: (k, 0)) for p in pieces],
        out_specs=_full((m, width)),
        compiler_params=_params("arbitrary"),
    )(a, *pieces)


def matmul_dw(a, b, bn, name):
    rows, m = a.shape
    n = b.shape[1]
    tk = min((4 if n > bn else 2) * ROW_TILE, rows)
    steps = rows // tk

    def body(a_ref, b_ref, o_ref, o16_ref):
        part = _dot_tn(a_ref[...], b_ref[...])

        @pl.when(pl.program_id(1) == 0)
        def _():
            o_ref[0] = part

        @pl.when(pl.program_id(1) > 0)
        def _():
            o_ref[0] += part

        @pl.when(pl.program_id(1) == steps - 1)
        def _():
            o16_ref[0] = o_ref[0].astype(BF16)

    out = pl.BlockSpec((1, m, bn), lambda j, k: (j, 0, 0))
    return pl.pallas_call(
        body, name=name,
        out_shape=(jax.ShapeDtypeStruct((n // bn, m, bn), F32), jax.ShapeDtypeStruct((n // bn, m, bn), BF16)),
        grid=(n // bn, steps),
        in_specs=[pl.BlockSpec((tk, m), lambda j, k: (k, 0)), pl.BlockSpec((tk, bn), lambda j, k: (k, j))],
        out_specs=(out, out),
        compiler_params=_params("parallel", "arbitrary"),
    )(a, b)


def _scan(a, b, carry, reverse):
    n, c = a.shape
    blocks = n // SUBLANES
    a = a.reshape(blocks, SUBLANES, c)
    b = b.reshape(blocks, SUBLANES, c)
    pos = lax.broadcasted_iota(jnp.int32, (1, SUBLANES, c), 1)
    s = 1
    while s < SUBLANES:
        shift, valid = (SUBLANES - s, pos < SUBLANES - s) if reverse else (s, pos >= s)
        a_s, b_s = pltpu.roll(a, shift, 1), pltpu.roll(b, shift, 1)
        b = jnp.where(valid, a * b_s + b, b)
        a = jnp.where(valid, a * a_s, a)
        s *= 2
    out = [None] * blocks
    for k in (range(blocks - 1, -1, -1) if reverse else range(blocks)):
        h = a[k] * carry + b[k]
        out[k] = h
        carry = h[0:1] if reverse else h[SUBLANES - 1:SUBLANES]
    return jnp.concatenate(out, axis=0)


def _rg_gates(ua, gw_ref, gb, lam):
    ub = ua.astype(BF16)
    pre_r, pre_i = [], []
    for h in range(RG_HEADS):
        z = _dot(ub[:, h * RG_HEAD_DIM:(h + 1) * RG_HEAD_DIM], gw_ref[h])
        pre_r.append(z[:, :RG_HEAD_DIM])
        pre_i.append(z[:, RG_HEAD_DIM:])
    r = _sigmoid(jnp.concatenate(pre_r, axis=1) + gb[0:1])
    i = _sigmoid(jnp.concatenate(pre_i, axis=1) + gb[1:2])
    sp = _softplus(-lam)
    log_a = -RG_C * r * sp
    a = jnp.exp(log_a)
    mult = jnp.sqrt(1.0 - a * a)
    return r, i, sp, a, mult


def _rg_weight_specs():
    return [_full((4, D_MODEL)), _full((1, D_MODEL)), _full((RG_HEADS, RG_HEAD_DIM, 2 * RG_HEAD_DIM)),
            _full((2, D_MODEL)), _full((1, D_MODEL))]


def rglru_fwd(proj, conv_w, conv_b, gate_w, gate_b, lam, reverse, name):
    rows_total = proj.shape[0]
    rows = min(SCAN_TILE, rows_total)
    n_tiles = rows_total // rows
    tix = (lambda i: n_tiles - 1 - i) if reverse else (lambda i: i)

    def body(xp, xm, xn, cw_ref, cb_ref, gw_ref, gb_ref, lam_ref, h_ref, acts_ref, carry):
        i = pl.program_id(0)
        t = tix(i)
        ext = _extend(xp, xm, xn, t == 0, t == n_tiles - 1)
        ua = _conv(ext, cw_ref[...], 2, rows) + cb_ref[...]
        r, gi, _, a, mult = _rg_gates(ua, gw_ref, gb_ref[...], lam_ref[...])
        for k, saved in enumerate((ua, r, gi, a, mult)):
            acts_ref[k] = saved
        b = mult * (gi * ua)

        @pl.when(i == 0)
        def _():
            carry[...] = jnp.zeros_like(carry)

        h = _scan(a, b, carry[0:1], reverse)
        h_ref[...] = h
        edge = h[0:1] if reverse else h[rows - 1:rows]
        carry[...] = jnp.broadcast_to(edge, carry.shape)

    return pl.pallas_call(
        body, name=name,
        out_shape=(jax.ShapeDtypeStruct((rows_total, D_MODEL), F32),
                   jax.ShapeDtypeStruct((5, rows_total, D_MODEL), F32)),
        grid=(n_tiles,),
        in_specs=_halo_specs(rows, D_MODEL, 0, n_tiles, tix) + _rg_weight_specs(),
        out_specs=(pl.BlockSpec((rows, D_MODEL), lambda i: (tix(i), 0)),
                   pl.BlockSpec((5, rows, D_MODEL), lambda i: (0, tix(i), 0))),
        scratch_shapes=[pltpu.VMEM((SUBLANES, D_MODEL), F32)],
        compiler_params=_params("arbitrary"),
    )(proj, proj, proj, conv_w, conv_b, gate_w, gate_b, lam)


def rglru_bwd(proj, dycat, h_dir, acts, gate_w, lam, add_dua, reverse, name):
    rows_total = proj.shape[0]
    rows = min(SCAN_TILE, rows_total)
    n_tiles = rows_total // rows
    tix = (lambda i: i) if reverse else (lambda i: n_tiles - 1 - i)
    za_block = 1

    def body(acts_ref, za_ref, dya_ref, hp, hm, hn, gw_ref, lam_ref, *rest):
        other = rest[0][...] if add_dua is not None else 0.0
        dua_ref, dgw_ref, dgb_ref, dlam_ref, carry = rest[-5:]
        step = pl.program_id(0)
        t = tix(step)
        first, last = t == 0, t == n_tiles - 1
        ua, r, gi, a, mult = (acts_ref[k] for k in range(5))
        lam_v = lam_ref[...]
        sp = _softplus(-lam_v)
        za = za_ref[...].astype(F32)
        dh = dya_ref[...] * (za * _sigmoid(za))

        @pl.when(step == 0)
        def _():
            carry[...] = jnp.zeros_like(carry)

        old = carry[0:1]
        mu = _scan(a, a * dh, old, not reverse)
        row = lax.broadcasted_iota(jnp.int32, mu.shape, 0)
        if reverse:
            mu_next = jnp.where(row == 0, old, pltpu.roll(mu, 1, 0))
            carry[...] = jnp.broadcast_to(mu[rows - 1:rows], carry.shape)
            h_ext = _extend(hp, hm, hn, first, last)
            h_prev = _shifted(h_ext, 1, rows)
        else:
            mu_next = jnp.where(row == rows - 1, old, pltpu.roll(mu, rows - 1, 0))
            carry[...] = jnp.broadcast_to(mu[0:1], carry.shape)
            h_ext = _extend(hp, hm, hn, first, last)
            h_prev = _shifted(h_ext, -1, rows)
        db = dh + mu_next
        da = db * h_prev
        d_mult = db * (gi * ua)
        di = db * (mult * ua)
        dua = db * (mult * gi)
        dlog_a = da * a - d_mult * (a * a) / mult
        dr = dlog_a * (-RG_C * sp)
        dlam = _colsum(dlog_a * (-RG_C * r)) * (-_sigmoid(-lam_v))
        dpr = dr * (r * (1.0 - r))
        dpi = di * (gi * (1.0 - gi))
        dgb = jnp.concatenate([_colsum(dpr), _colsum(dpi)], axis=0)
        ub = ua.astype(BF16)
        dua_heads, dgw_heads = [], []
        for h in range(RG_HEADS):
            cols = slice(h * RG_HEAD_DIM, (h + 1) * RG_HEAD_DIM)
            dz = jnp.concatenate([dpr[:, cols], dpi[:, cols]], axis=1).astype(BF16)
            dgw_heads.append(_dot_tn(ub[:, cols], dz))
            dua_heads.append(_dot_nt(dz, gw_ref[h]))
        dua_ref[...] = dua + jnp.concatenate(dua_heads, axis=1) + other

        @pl.when(step == 0)
        def _():
            for h in range(RG_HEADS):
                dgw_ref[h] = dgw_heads[h]
            dgb_ref[...] = dgb
            dlam_ref[...] = dlam

        @pl.when(step > 0)
        def _():
            for h in range(RG_HEADS):
                dgw_ref[h] += dgw_heads[h]
            dgb_ref[...] += dgb
            dlam_ref[...] += dlam

    row_spec = lambda col: pl.BlockSpec((rows, D_MODEL), lambda i: (tix(i), col))
    return pl.pallas_call(
        body, name=name,
        out_shape=(jax.ShapeDtypeStruct((rows_total, D_MODEL), F32),
                   jax.ShapeDtypeStruct((RG_HEADS, RG_HEAD_DIM, 2 * RG_HEAD_DIM), F32),
                   jax.ShapeDtypeStruct((2, D_MODEL), F32), jax.ShapeDtypeStruct((1, D_MODEL), F32)),
        grid=(n_tiles,),
        in_specs=([pl.BlockSpec((5, rows, D_MODEL), lambda i: (0, tix(i), 0)), row_spec(za_block), row_spec(0)]
                  + _halo_specs(rows, D_MODEL, 0, n_tiles, tix)
                  + [_full((RG_HEADS, RG_HEAD_DIM, 2 * RG_HEAD_DIM)), _full((1, D_MODEL))]
                  + ([] if add_dua is None else [row_spec(0)])),
        out_specs=(row_spec(0), _full((RG_HEADS, RG_HEAD_DIM, 2 * RG_HEAD_DIM)), _full((2, D_MODEL)),
                   _full((1, D_MODEL))),
        scratch_shapes=[pltpu.VMEM((SUBLANES, D_MODEL), F32)],
        compiler_params=_params("arbitrary"),
    )(acts, proj, dycat, h_dir, h_dir, h_dir, gate_w, lam, *([] if add_dua is None else [add_dua]))


def _extend_cols(refs, block, is_first, is_last):
    cols = slice(block * D_MODEL, (block + 1) * D_MODEL)
    prev_ref, main_ref, next_ref = refs
    p = jnp.where(is_first, 0.0, prev_ref[:, cols].astype(F32))
    n = jnp.where(is_last, 0.0, next_ref[:, cols].astype(F32))
    return jnp.concatenate([p, main_ref[:, cols].astype(F32), n], axis=0)


def even_mix_fwd(proj, h_f, h_b, sc_w, name):
    rows_total = proj.shape[0]
    rows = min(2 * MIX_TILE, rows_total)
    n_tiles = rows_total // rows
    ident = lambda i: i

    def body(za_ref, hf_ref, hb_ref, xbp, xbm, xbn, gcp, gcm, gcn, gb_ref, zb_ref, w_ref, y_ref):
        t = pl.program_id(0)
        first, last = t == 0, t == n_tiles - 1
        za = za_ref[...].astype(F32)
        y_ref[:, 0:D_MODEL] = ((hf_ref[...] + hb_ref[...]) * (za * _sigmoid(za))).astype(BF16)
        p_ext = _extend(xbp, xbm, xbn, first, last) * _extend(gcp, gcm, gcn, first, last)
        cv = _conv(p_ext, w_ref[...], 1, rows)
        zb = zb_ref[...].astype(F32)
        y_ref[:, D_MODEL:2 * D_MODEL] = (gb_ref[...].astype(F32) * cv * (zb * _sigmoid(zb))).astype(BF16)

    blk = lambda col: pl.BlockSpec((rows, D_MODEL), lambda i: (i, col))
    return pl.pallas_call(
        body, name=name,
        out_shape=jax.ShapeDtypeStruct((rows_total, 2 * D_MODEL), BF16),
        grid=(n_tiles,),
        in_specs=([blk(1), blk(0), blk(0)] + _halo_specs(rows, D_MODEL, 2, n_tiles, ident)
                  + _halo_specs(rows, D_MODEL, 4, n_tiles, ident) + [blk(3), blk(5), _full((3, D_MODEL))]),
        out_specs=pl.BlockSpec((rows, 2 * D_MODEL), lambda i: (i, 0)),
        compiler_params=_params("parallel"),
    )(proj, h_f, h_b, proj, proj, proj, proj, proj, proj, proj, proj, sc_w)


def even_mix_bwd(proj, dycat, h_f, h_b, dua, conv_w, sc_w, name):
    rows_total, width = proj.shape
    rows = min(MIX_TILE, rows_total)
    n_tiles = rows_total // rows
    ident = lambda i: i

    def body(pp, pm, pn, dyp, dym, dyn, hf_ref, hb_ref, dup, dum, dun, cw_ref, sw_ref,
             dp_ref, dcw_ref, dcb_ref, dsw_ref):
        def put(k, value):
            dp_ref[:, k * D_MODEL:(k + 1) * D_MODEL] = value.astype(BF16)

        t = pl.program_id(0)
        first, last = t == 0, t == n_tiles - 1
        proj_ext = lambda k: _extend_cols((pp, pm, pn), k, first, last)
        mid = slice(HALO, HALO + rows)
        za = pm[:, D_MODEL:2 * D_MODEL].astype(F32)
        sa = _sigmoid(za)
        put(1, dym[:, 0:D_MODEL] * (hf_ref[...] + hb_ref[...]) * (sa * (1.0 + za * (1.0 - sa))))
        dua_ext = _extend(dup, dum, dun, first, last)
        cw = cw_ref[...]
        put(0, _conv_transpose(dua_ext, cw, 2, rows))
        dua_mid = dua_ext[mid]
        xa_ext = proj_ext(0)
        dcw = jnp.concatenate([_colsum(dua_mid * _shifted(xa_ext, k - 2, rows)) for k in range(4)], axis=0)
        dcb = _colsum(dua_mid)
        xb_ext, gb_ext, gc_ext, zb_ext = proj_ext(2), proj_ext(3), proj_ext(4), proj_ext(5)
        p_ext = xb_ext * gc_ext
        sb_ext = _sigmoid(zb_ext)
        dyb_ext = _extend_cols((dyp, dym, dyn), 1, first, last)
        dcv_ext = dyb_ext * gb_ext * (zb_ext * sb_ext)
        sw = sw_ref[...]
        p_at = [_shifted(p_ext, k - 1, rows) for k in range(3)]
        cv = (p_at[0] * sw[0:1] + p_at[1] * sw[1:2]) + p_at[2] * sw[2:3]
        zb, sb, dyb, gb = zb_ext[mid], sb_ext[mid], dyb_ext[mid], gb_ext[mid]
        put(3, dyb * cv * (zb * sb))
        put(5, dyb * gb * cv * (sb * (1.0 + zb * (1.0 - sb))))
        dp = _conv_transpose(dcv_ext, sw, 1, rows)
        put(4, dp * xb_ext[mid])
        put(2, dp * gc_ext[mid])
        dcv = dcv_ext[mid]
        dsw = jnp.concatenate([_colsum(dcv * p_at[k]) for k in range(3)], axis=0)
        _accumulate(dcw_ref, dcw, t)
        _accumulate(dcb_ref, dcb, t)
        _accumulate(dsw_ref, dsw, t)

    own = pl.BlockSpec((rows, D_MODEL), lambda i: (i, 0))
    return pl.pallas_call(
        body, name=name,
        out_shape=(jax.ShapeDtypeStruct((rows_total, 6 * D_MODEL), BF16),
                   jax.ShapeDtypeStruct((4, D_MODEL), F32), jax.ShapeDtypeStruct((1, D_MODEL), F32),
                   jax.ShapeDtypeStruct((3, D_MODEL), F32)),
        grid=(n_tiles,),
        in_specs=(_halo_specs(rows, width, 0, n_tiles, ident) + _halo_specs(rows, 2 * D_MODEL, 0, n_tiles, ident)
                  + [own, own] + _halo_specs(rows, D_MODEL, 0, n_tiles, ident)
                  + [_full((4, D_MODEL)), _full((3, D_MODEL))]),
        out_specs=(pl.BlockSpec((rows, 6 * D_MODEL), lambda i: (i, 0)), _full((4, D_MODEL)), _full((1, D_MODEL)),
                   _full((3, D_MODEL))),
        compiler_params=_params("arbitrary"),
    )(proj, proj, proj, dycat, dycat, dycat, h_f, h_b, dua, dua, dua, conv_w, sc_w)


def even_out_fwd(ycat, w_out, gain, x, name):
    rows, d = x.shape
    k = ycat.shape[1]
    tm = min(ROW_TILE, rows)

    def body(yc_ref, w_ref, g_ref, x_ref, x1_ref, y_ref):
        y = _dot(yc_ref[...], w_ref[...])
        y_ref[...] = y
        rstd = lax.rsqrt(jnp.mean(y * y, axis=-1, keepdims=True) + NORM_EPS)
        x1_ref[...] = x_ref[...] + y * rstd * g_ref[...]

    row = lambda n: pl.BlockSpec((tm, n), lambda i: (i, 0))
    return pl.pallas_call(
        body, name=name,
        out_shape=(jax.ShapeDtypeStruct((rows, d), F32),) * 2,
        grid=(rows // tm,),
        in_specs=[row(k), _full((k, d)), _full((1, d)), row(d)],
        out_specs=(row(d), row(d)),
        compiler_params=_params("parallel"),
    )(ycat, w_out, gain, x)


def _rmsnorm_bwd(dout, y, gain):
    rstd = lax.rsqrt(jnp.mean(y * y, axis=-1, keepdims=True) + NORM_EPS)
    yhat = y * rstd
    dyn = dout * gain
    dy = rstd * (dyn - yhat * jnp.mean(dyn * yhat, axis=-1, keepdims=True))
    return dy, dout * yhat


def even_out_bwd(dx1, y, gain, w_out, name):
    rows, d = y.shape
    k = w_out.shape[0]
    tm = min(ROW_TILE, rows)

    def body(dx_ref, y_ref, g_ref, w_ref, dy_ref, dyc_ref, dg_ref):
        dy, dg_rows = _rmsnorm_bwd(dx_ref[...], y_ref[...], g_ref[...])
        dyb = dy.astype(BF16)
        dy_ref[...] = dyb
        dyc_ref[...] = _dot_nt(dyb, w_ref[...])
        _accumulate(dg_ref, _colsum(dg_rows), pl.program_id(0))

    row = lambda n: pl.BlockSpec((tm, n), lambda i: (i, 0))
    return pl.pallas_call(
        body, name=name,
        out_shape=(jax.ShapeDtypeStruct((rows, d), BF16), jax.ShapeDtypeStruct((rows, k), F32),
                   jax.ShapeDtypeStruct((1, d), F32)),
        grid=(rows // tm,),
        in_specs=[row(d), row(d), _full((1, d)), _full((k, d))],
        out_specs=(row(d), row(k), _full((1, d))),
        compiler_params=_params("arbitrary"),
    )(dx1, y, gain, w_out)


def _chunk_cumsum(g, reverse):
    n, c = g.shape
    chunks, per = n // GLA_CHUNK, GLA_CHUNK // SUBLANES
    g = g.reshape(n // SUBLANES, SUBLANES, c)
    pos = lax.broadcasted_iota(jnp.int32, (1, SUBLANES, c), 1)
    s = 1
    while s < SUBLANES:
        if reverse:
            g = g + jnp.where(pos < SUBLANES - s, pltpu.roll(g, SUBLANES - s, 1), 0.0)
        else:
            g = g + jnp.where(pos >= s, pltpu.roll(g, s, 1), 0.0)
        s *= 2
    g = g.reshape(chunks, per, SUBLANES, c)
    out, carry = [None] * per, None
    for k in (range(per - 1, -1, -1) if reverse else range(per)):
        out[k] = g[:, k] if carry is None else g[:, k] + carry
        carry = out[k][:, 0:1] if reverse else out[k][:, SUBLANES - 1:SUBLANES]
    return jnp.stack(out, axis=1).reshape(n, c)


def _gla_prepare(q_ref, k_ref, lr_ref, wg_ref, bg_ref, reverse, n_chunks):
    z = _dot(lr_ref[...].astype(BF16), wg_ref[0]) + bg_ref[0]
    g = -_softplus(-z) * (1.0 / GLA_NORMALIZER)
    bcum = _chunk_cumsum(g, reverse).reshape(n_chunks, GLA_CHUNK, GLA_DK)
    edge = 0 if reverse else GLA_CHUNK - 1
    btot = bcum[:, edge:edge + 1, :]
    e_pos = jnp.exp(bcum)
    e_neg = jnp.exp(-bcum)
    e_st = jnp.exp(btot - bcum)
    q3 = q_ref[...].reshape(n_chunks, GLA_CHUNK, GLA_DK)
    k3 = k_ref[...].reshape(n_chunks, GLA_CHUNK, GLA_DK)
    scale = GLA_DK ** -0.5
    q_in = q3 * scale * e_pos
    k_in = k3 * e_neg
    k_st = k3 * e_st
    dec = jnp.exp(btot)
    return z, q_in, k_in, k_st, dec, (scale * e_pos, e_neg, e_st)


def _gla_mask(reverse):
    i = lax.broadcasted_iota(jnp.int32, (GLA_CHUNK, GLA_CHUNK), 0)
    j = lax.broadcasted_iota(jnp.int32, (GLA_CHUNK, GLA_CHUNK), 1)
    return (j >= i) if reverse else (j <= i)


def _gla_specs(rows, n_blocks, reverse):
    tix = (lambda s: n_blocks - 1 - s) if reverse else (lambda s: s)
    d = 1 if reverse else 0
    lr_block = LR_COL // LANES
    specs = [pl.BlockSpec((rows, GLA_DK), lambda h, s: (tix(s), h)),
             pl.BlockSpec((rows, GLA_DK), lambda h, s: (tix(s), GLA_HEADS + h)),
             pl.BlockSpec((rows, GLA_DV), lambda h, s: (tix(s), GLA_HEADS + h)),
             pl.BlockSpec((rows, LANES), lambda h, s: (tix(s), lr_block)),
             pl.BlockSpec((1, LANES, GLA_DK), lambda h, s: (d, 0, h)),
             pl.BlockSpec((1, 1, GLA_DK), lambda h, s: (d, 0, h))]
    return specs, tix


def gla_fwd(proj, wg_pad, bg, add_o, reverse, name):
    rows_total = proj.shape[0]
    rows = min(GLA_BLOCK, rows_total)
    n_blocks = rows_total // rows
    n_chunks = rows // GLA_CHUNK
    specs, tix = _gla_specs(rows, n_blocks, reverse)

    def body(q_ref, k_ref, v_ref, lr_ref, wg_ref, bg_ref, *rest):
        o_ref, st_ref, state, kv_scr, dec_scr = rest[-5:]
        _, q_in, k_in, k_st, dec, _ = _gla_prepare(q_ref, k_ref, lr_ref, wg_ref, bg_ref, reverse, n_chunks)
        vb = v_ref[...].reshape(n_chunks, GLA_CHUNK, GLA_DV).astype(BF16)
        qb = q_in.astype(BF16)
        p = jnp.where(_gla_mask(reverse), _bdot(qb, k_in.astype(BF16), 2, 2), 0.0)
        o = _bdot(p.astype(BF16), vb, 2, 1)
        kv_scr[...] = _bdot(vb, k_st.astype(BF16), 1, 1)
        dec_scr[...] = jnp.broadcast_to(dec, dec_scr.shape)

        @pl.when(pl.program_id(1) == 0)
        def _():
            state[...] = jnp.zeros_like(state)

        for c in range(n_chunks):
            cc = n_chunks - 1 - c if reverse else c
            st_ref[0, cc] = state[...]
            state[...] = state[...] * dec_scr[cc, 0:1] + kv_scr[cc]
        o = o + _bdot(qb, st_ref[0].astype(BF16), 2, 2)
        o = o.reshape(rows, GLA_DV)
        o_ref[...] = o if add_o is None else o + rest[0][...]

    o_spec = pl.BlockSpec((rows, GLA_DV), lambda h, s: (tix(s), h))
    return pl.pallas_call(
        body, name=name,
        out_shape=(jax.ShapeDtypeStruct((rows_total, GLA_HEADS * GLA_DV), F32),
                   jax.ShapeDtypeStruct((GLA_HEADS, rows_total // GLA_CHUNK, GLA_DV, GLA_DK), F32)),
        grid=(GLA_HEADS, n_blocks),
        in_specs=specs + ([] if add_o is None else [o_spec]),
        out_specs=(o_spec,
                   pl.BlockSpec((1, n_chunks, GLA_DV, GLA_DK), lambda h, s: (h, tix(s), 0, 0))),
        scratch_shapes=[pltpu.VMEM((GLA_DV, GLA_DK), F32), pltpu.VMEM((n_chunks, GLA_DV, GLA_DK), F32),
                        pltpu.VMEM((n_chunks, SUBLANES, GLA_DK), F32)],
        compiler_params=_params("parallel", "arbitrary"),
    )(proj, proj, proj, proj, wg_pad, bg, *([] if add_o is None else [add_o]))


def gla_bwd(proj, wg_pad, bg, d_o, states, dqkv_in, reverse, name):
    rows_total = proj.shape[0]
    rows = min(GLA_BLOCK, rows_total)
    n_blocks = rows_total // rows
    n_chunks = rows // GLA_CHUNK
    specs, tix = _gla_specs(rows, n_blocks, not reverse)
    d = 1 if reverse else 0
    specs[4] = pl.BlockSpec((1, LANES, GLA_DK), lambda h, s: (d, 0, h))
    specs[5] = pl.BlockSpec((1, 1, GLA_DK), lambda h, s: (d, 0, h))
    add = dqkv_in is not None

    def body(*refs):
        q_ref, k_ref, v_ref, lr_ref, wg_ref, bg_ref, do_ref, st_ref = refs[:8]
        refs = refs[8:]
        if add:
            aq_ref, ak_ref, av_ref = refs[:3]
            refs = refs[3:]
        dq_ref, dk_ref, dv_ref, dz_ref, dstate, g_scr, dec_scr, dsn_scr = refs
        z, q_in, k_in, k_st, dec, (f_q, f_k, f_s) = _gla_prepare(q_ref, k_ref, lr_ref, wg_ref, bg_ref, reverse,
                                                                 n_chunks)
        mask = _gla_mask(reverse)
        vb = v_ref[...].reshape(n_chunks, GLA_CHUNK, GLA_DV).astype(BF16)
        dob = do_ref[...].reshape(n_chunks, GLA_CHUNK, GLA_DV).astype(BF16)
        qb, kb, ksb = q_in.astype(BF16), k_in.astype(BF16), k_st.astype(BF16)
        st = st_ref[0]
        stb = st.astype(BF16)
        pb = jnp.where(mask, _bdot(qb, kb, 2, 2), 0.0).astype(BF16)
        dpb = jnp.where(mask, _bdot(dob, vb, 2, 2), 0.0).astype(BF16)
        d_qin = _bdot(dpb, kb, 2, 1) + _bdot(dob, stb, 2, 1)
        d_kin = _bdot(dpb, qb, 1, 1)
        dv = _bdot(pb, dob, 1, 1)
        g_scr[...] = _bdot(dob, qb, 1, 1)
        dec_scr[...] = jnp.broadcast_to(dec, dec_scr.shape)

        @pl.when(pl.program_id(1) == 0)
        def _():
            dstate[...] = jnp.zeros_like(dstate)

        for c in range(n_chunks):
            cc = c if reverse else n_chunks - 1 - c
            dsn_scr[cc] = dstate[...]
            dstate[...] = dstate[...] * dec_scr[cc, 0:1] + g_scr[cc]
        dsn = dsn_scr[...]
        dsnb = dsn.astype(BF16)
        dv = dv + _bdot(ksb, dsnb, 2, 2)
        d_kst = _bdot(vb, dsnb, 2, 1)
        d_dec = jnp.sum(dsn * st, axis=1, keepdims=True)
        ks_term = d_kst * k_st
        d_btot = d_dec * dec + jnp.sum(ks_term, axis=1, keepdims=True)
        d_b = d_qin * q_in - d_kin * k_in - ks_term
        pos = lax.broadcasted_iota(jnp.int32, d_b.shape, 1)
        edge = 0 if reverse else GLA_CHUNK - 1
        d_b = d_b + jnp.where(pos == edge, d_btot, 0.0)
        dg = _chunk_cumsum(d_b.reshape(rows, GLA_DK), not reverse)
        dz_ref[...] = dg * (1.0 / GLA_NORMALIZER) * _sigmoid(-z)
        dq = (d_qin * f_q).reshape(rows, GLA_DK)
        dk = (d_kin * f_k + d_kst * f_s).reshape(rows, GLA_DK)
        dv = dv.reshape(rows, GLA_DV)
        if add:
            dq_ref[...] = (dq + aq_ref[...]).astype(BF16)
            dk_ref[...] = (dk + ak_ref[...]).astype(BF16)
            dv_ref[...] = (dv + av_ref[...]).astype(BF16)
        else:
            dq_ref[...] = dq
            dk_ref[...] = dk
            dv_ref[...] = dv

    qkv_specs = [pl.BlockSpec((rows, GLA_DK), lambda h, s: (tix(s), h)),
                 pl.BlockSpec((rows, GLA_DK), lambda h, s: (tix(s), h)),
                 pl.BlockSpec((rows, GLA_DV), lambda h, s: (tix(s), h))]
    in_specs = specs + [pl.BlockSpec((rows, GLA_DV), lambda h, s: (tix(s), h)),
                        pl.BlockSpec((1, n_chunks, GLA_DV, GLA_DK), lambda h, s: (h, tix(s), 0, 0))]
    args = [proj, proj, proj, proj, wg_pad, bg, d_o, states]
    out_dtype = F32
    if add:
        in_specs += qkv_specs
        args += list(dqkv_in)
        out_dtype = BF16
    return pl.pallas_call(
        body, name=name,
        out_shape=(jax.ShapeDtypeStruct((rows_total, GLA_HEADS * GLA_DK), out_dtype),
                   jax.ShapeDtypeStruct((rows_total, GLA_HEADS * GLA_DK), out_dtype),
                   jax.ShapeDtypeStruct((rows_total, GLA_HEADS * GLA_DV), out_dtype),
                   jax.ShapeDtypeStruct((rows_total, GLA_HEADS * GLA_DK), F32)),
        grid=(GLA_HEADS, n_blocks),
        in_specs=in_specs,
        out_specs=(pl.BlockSpec((rows, GLA_DK), lambda h, s: (tix(s), h)),
                   pl.BlockSpec((rows, GLA_DK), lambda h, s: (tix(s), h)),
                   pl.BlockSpec((rows, GLA_DV), lambda h, s: (tix(s), h)),
                   pl.BlockSpec((rows, GLA_DK), lambda h, s: (tix(s), h))),
        scratch_shapes=[pltpu.VMEM((GLA_DV, GLA_DK), F32), pltpu.VMEM((n_chunks, GLA_DV, GLA_DK), F32),
                        pltpu.VMEM((n_chunks, SUBLANES, GLA_DK), F32),
                        pltpu.VMEM((n_chunks, GLA_DV, GLA_DK), F32)],
        compiler_params=_params("parallel", "arbitrary"),
    )(*args)


def gla_gate_bwd(proj, dz_f, dz_b, wg_pad, name):
    rows_total = proj.shape[0]
    tm = min(ROW_TILE, rows_total)
    n_key = GLA_HEADS * GLA_DK

    def body(lr_ref, dzf_ref, dzb_ref, wg_ref, dlr_ref, dwg_ref, dbg_ref):
        step = pl.program_id(0)
        lr_t = jnp.transpose(lr_ref[...])
        dzf, dzb = dzf_ref[...], dzb_ref[...]
        dzf16, dzb16 = dzf.astype(BF16), dzb.astype(BF16)
        dlr_ref[...] = (_dot_nt(dzf16, wg_ref[0]) + _dot_nt(dzb16, wg_ref[1])).astype(BF16)
        dwf = _dot(lr_t[0:GLA_RANK].astype(BF16), dzf16)
        dwb = _dot(lr_t[GLA_RANK:2 * GLA_RANK].astype(BF16), dzb16)
        dbg = jnp.concatenate([_colsum(dzf), _colsum(dzb)], axis=0)

        @pl.when(step == 0)
        def _():
            dwg_ref[0] = dwf
            dwg_ref[1] = dwb
            dbg_ref[...] = dbg

        @pl.when(step > 0)
        def _():
            dwg_ref[0] += dwf
            dwg_ref[1] += dwb
            dbg_ref[...] += dbg

    return pl.pallas_call(
        body, name=name,
        out_shape=(jax.ShapeDtypeStruct((rows_total, LANES), BF16), jax.ShapeDtypeStruct((2, GLA_RANK, n_key), F32),
                   jax.ShapeDtypeStruct((2, n_key), F32)),
        grid=(rows_total // tm,),
        in_specs=[pl.BlockSpec((tm, LANES), lambda i: (i, LR_COL // LANES)),
                  pl.BlockSpec((tm, n_key), lambda i: (i, 0)), pl.BlockSpec((tm, n_key), lambda i: (i, 0)),
                  _full((2, LANES, n_key))],
        out_specs=(pl.BlockSpec((tm, LANES), lambda i: (i, 0)), _full((2, GLA_RANK, n_key)), _full((2, n_key))),
        compiler_params=_params("arbitrary"),
    )(proj, dz_f, dz_b, wg_pad)


def _head_norm(o, gain):
    outs, hats, rstds = [], [], []
    for h in range(GLA_HEADS):
        oh = o[:, h * GLA_DV:(h + 1) * GLA_DV]
        rstd = lax.rsqrt(jnp.mean(oh * oh, axis=-1, keepdims=True) + NORM_EPS)
        hat = oh * rstd
        outs.append(hat * gain)
        hats.append(hat)
        rstds.append(rstd)
    return outs, hats, rstds


def odd_out_fwd(o, proj, head_gain, w_out, gain, x1, target, name):
    rows, d = x1.shape
    tm = min(ROW_TILE, rows)
    r_block = (2 * GLA_HEADS * GLA_DK + GLA_HEADS * GLA_DV) // d

    def body(o_ref, r_ref, hg_ref, w_ref, g_ref, x1_ref, tgt_ref, y2_ref, dy_ref, dx2_ref, loss_ref, dg_ref):
        step = pl.program_id(0)
        on, _, _ = _head_norm(o_ref[...], hg_ref[...])
        r = r_ref[...]
        y2 = (jnp.concatenate(on, axis=1) * (r * _sigmoid(r))).astype(BF16)
        y2_ref[...] = y2
        y = _dot(y2, w_ref[...])
        gain_v = g_ref[...]
        rstd = lax.rsqrt(jnp.mean(y * y, axis=-1, keepdims=True) + NORM_EPS)
        x2 = x1_ref[...] + y * rstd * gain_v
        diff = x2 - tgt_ref[...]
        loss = 0.5 * jnp.sum(jnp.mean(diff * diff, axis=-1, keepdims=True), axis=0, keepdims=True)
        dx2 = diff * (1.0 / d)
        dx2_ref[...] = dx2
        dy, dg_rows = _rmsnorm_bwd(dx2, y, gain_v)
        dy_ref[...] = dy.astype(BF16)
        _accumulate(loss_ref, jnp.broadcast_to(loss, loss_ref.shape), step)
        _accumulate(dg_ref, _colsum(dg_rows), step)

    row = lambda n, col=0: pl.BlockSpec((tm, n), lambda i: (i, col))
    return pl.pallas_call(
        body, name=name,
        out_shape=(jax.ShapeDtypeStruct((rows, d), BF16), jax.ShapeDtypeStruct((rows, d), BF16),
                   jax.ShapeDtypeStruct((rows, d), F32), jax.ShapeDtypeStruct((SUBLANES, LANES), F32),
                   jax.ShapeDtypeStruct((1, d), F32)),
        grid=(rows // tm,),
        in_specs=[row(d), row(d, r_block), _full((1, GLA_DV)), _full((d, d)), _full((1, d)), row(d), row(d)],
        out_specs=(row(d), row(d), row(d), _full((SUBLANES, LANES)), _full((1, d))),
        compiler_params=_params("arbitrary"),
    )(o, proj, head_gain, w_out, gain, x1, target)


def odd_out_bwd(dy, w_out, o, proj, head_gain, name):
    rows, d = dy.shape
    tm = min(ROW_TILE, rows)
    r_block = (2 * GLA_HEADS * GLA_DK + GLA_HEADS * GLA_DV) // d

    def body(dy_ref, w_ref, o_ref, r_ref, hg_ref, dr_ref, do_ref, dhg_ref):
        dy2 = _dot_nt(dy_ref[...], w_ref[...])
        hg = hg_ref[...]
        on, hats, rstds = _head_norm(o_ref[...], hg)
        r = r_ref[...]
        sr = _sigmoid(r)
        dr_ref[...] = (dy2 * jnp.concatenate(on, axis=1) * (sr * (1.0 + r * (1.0 - sr)))).astype(BF16)
        d_on = dy2 * (r * sr)
        d_os, dhg = [], None
        for h in range(GLA_HEADS):
            dn = d_on[:, h * GLA_DV:(h + 1) * GLA_DV]
            part = _colsum(dn * hats[h])
            dhg = part if dhg is None else dhg + part
            dng = dn * hg
            d_os.append(rstds[h] * (dng - hats[h] * jnp.mean(dng * hats[h], axis=-1, keepdims=True)))
        do_ref[...] = jnp.concatenate(d_os, axis=1)
        _accumulate(dhg_ref, dhg, pl.program_id(0))

    row = lambda n, col=0: pl.BlockSpec((tm, n), lambda i: (i, col))
    return pl.pallas_call(
        body, name=name,
        out_shape=(jax.ShapeDtypeStruct((rows, d), BF16), jax.ShapeDtypeStruct((rows, d), F32),
                   jax.ShapeDtypeStruct((1, GLA_DV), F32)),
        grid=(rows // tm,),
        in_specs=[row(d), _full((d, d)), row(d), row(d, r_block), _full((1, GLA_DV))],
        out_specs=(row(d), row(d), _full((1, GLA_DV))),
        compiler_params=_params("arbitrary"),
    )(dy, w_out, o, proj, head_gain)


def local_step(x, target, w, reduce_first=None, reduce_second=None, late_weights=None):
    g, g16 = {}, {}
    proj_e, h0 = norm_matmul(x, w["even_norm_pre"], w["even_w_in"], BF16, "even_in_proj")
    h_dir, acts = zip(*[rglru_fwd(proj_e, w["rg_conv_w"], w["rg_conv_b"], w["rg_gate_w"][d], w["rg_gate_b"][d],
                                  w["rg_lambda"][d], d == 1, "rglru_fwd_%d" % d) for d in range(2)])
    ycat = even_mix_fwd(proj_e, h_dir[0], h_dir[1], w["sc_conv_w"], "even_mix_fwd")
    if late_weights is not None:
        w = dict(w, **late_weights(ycat))
    x1, y_e = even_out_fwd(ycat, w["even_w_out"], w["even_norm_post"], x, "even_out_fwd")
    proj_o, h1 = norm_matmul(x1, w["odd_norm_pre"], w["odd_w_in"], F32, "odd_in_proj")
    o, st_dir = None, []
    for d in range(2):
        o, st = gla_fwd(proj_o, w["gla_wg_pad"], w["gla_b_gate"], o, d == 1, "gla_fwd_%d" % d)
        st_dir.append(st)
    y2, dy_o, dx2, loss, g["odd_norm_post"] = odd_out_fwd(
        o, proj_o, w["gla_norm_g"], w["odd_w_out"], w["odd_norm_post"], x1, target, "odd_out_fwd")
    g["odd_w_out"], g16["odd_w_out"] = (a[0] for a in matmul_dw(y2, dy_o, D_MODEL, "odd_w_out_grad"))
    dr, d_o, g["gla_norm_g"] = odd_out_bwd(dy_o, w["odd_w_out"], o, proj_o, w["gla_norm_g"], "odd_out_bwd")
    dq, dk, dv, dz_f = gla_bwd(proj_o, w["gla_wg_pad"], w["gla_b_gate"], d_o, st_dir[0], None, False, "gla_bwd_0")
    dq, dk, dv, dz_b = gla_bwd(proj_o, w["gla_wg_pad"], w["gla_b_gate"], d_o, st_dir[1], (dq, dk, dv), True,
                               "gla_bwd_1")
    dlr, g["gla_w_gate_lr"], g["gla_b_gate"] = gla_gate_bwd(proj_o, dz_f, dz_b, w["gla_wg_pad"], "gla_gate_bwd")
    dproj_o = [dq, dk, dv, dr, dlr]
    g["odd_w_in"] = matmul_dw_pieces(h1, dproj_o, "odd_w_in_grad")[:, :ODD_IN]
    dx1, g["odd_norm_pre"] = inproj_bwd_pieces(dproj_o, w["odd_w_in"], x1, w["odd_norm_pre"], dx2, "odd_in_proj_bwd")
    dy_e, dycat, g["even_norm_post"] = even_out_bwd(dx1, y_e, w["even_norm_post"], w["even_w_out"], "even_out_bwd")
    g["even_w_out"], g16["even_w_out"] = (a[0] for a in matmul_dw(ycat, dy_e, D_MODEL, "even_w_out_grad"))
    lam = w["rg_lambda"] if reduce_first is None else w["rg_lambda"] + reduce_first(g, g16)
    dua, dgw, dgb, dlam = None, [], [], []
    for d in range(2):
        a, b, c, e = rglru_bwd(proj_e, dycat, h_dir[d], acts[d], w["rg_gate_w"][d], lam[d], dua, d == 1,
                               "rglru_bwd_%d" % d)
        dua = a
        dgw.append(b)
        dgb.append(c)
        dlam.append(e)
    dproj_e, g["rg_conv_w"], g["rg_conv_b"], g["sc_conv_w"] = even_mix_bwd(
        proj_e, dycat, h_dir[0], h_dir[1], dua, w["rg_conv_w"], w["sc_conv_w"], "even_mix_bwd")
    dgw = jnp.stack(dgw).reshape(2, RG_HEADS, RG_HEAD_DIM, 2, RG_HEAD_DIM)
    g["rg_gate_w"] = jnp.transpose(dgw, (0, 3, 1, 2, 4))
    g["rg_gate_b"] = jnp.stack(dgb).reshape(2, 2, RG_HEADS, RG_HEAD_DIM)
    g["rg_lambda"] = jnp.concatenate(dlam, axis=0)
    g["even_w_in"], g16["even_w_in"] = matmul_dw(h0, dproj_e, EVEN_IN // 4, "even_w_in_grad")
    gain = w["even_norm_pre"] if reduce_second is None else w["even_norm_pre"] + reduce_second(g, g16)
    grad_x, g["even_norm_pre"] = inproj_bwd(dproj_e, w["even_w_in"], x, gain, dx1, "even_in_proj_bwd")
    return loss, grad_x, g


def _prepare_weights(full):
    w = {}
    for name in ("even_norm_pre", "even_norm_post", "rg_conv_b", "odd_norm_pre", "odd_norm_post", "gla_norm_g"):
        if name in full:
            w[name] = full[name].reshape(1, -1)
    for name in ("rg_conv_w", "sc_conv_w"):
        if name in full:
            w[name] = full[name]
    for name in ("even_w_out", "odd_w_out"):
        if name in full:
            w[name] = full[name].astype(BF16)
    if "even_w_in" in full:
        w["even_w_in"] = full["even_w_in"].astype(BF16)
        if w["even_w_in"].ndim == 2:
            w["even_w_in"] = jnp.transpose(w["even_w_in"].reshape(D_MODEL, 4, EVEN_IN // 4), (1, 0, 2))
    if "rg_gate_w" in full:
        gw = jnp.transpose(full["rg_gate_w"].astype(BF16), (0, 2, 3, 1, 4))
        w["rg_gate_w"] = gw.reshape(2, RG_HEADS, RG_HEAD_DIM, 2 * RG_HEAD_DIM)
        w["rg_gate_b"] = full["rg_gate_b"].reshape(2, 2, D_MODEL)
        w["rg_lambda"] = full["rg_lambda"].reshape(2, 1, D_MODEL)
    if "odd_w_in" in full:
        w_in = jnp.pad(full["odd_w_in"].astype(BF16), ((0, 0), (0, ODD_IN_PAD - ODD_IN)))
        w["odd_w_in"] = w_in.reshape(1, D_MODEL, ODD_IN_PAD)
    if "gla_w_gate_lr" in full:
        wg = full["gla_w_gate_lr"].astype(BF16)
        w["gla_wg_pad"] = jnp.stack([jnp.pad(wg[d], ((d * GLA_RANK, LANES - (d + 1) * GLA_RANK), (0, 0)))
                                     for d in range(2)])
        w["gla_b_gate"] = full["gla_b_gate"].reshape(2, 1, GLA_HEADS * GLA_DK)
    return w


SHARDED_SMALL = (("rg_conv_w", (4, 256)), ("rg_lambda", (2, 256)), ("sc_conv_w", (3, 256)),
                 ("odd_norm_pre", (256,)), ("odd_norm_post", (256,)), ("gla_w_gate_lr", (2, 16, 128)),
                 ("gla_b_gate", (2, 128)), ("gla_norm_g", (64,)))
SHARDED_ROWS = 96
REPLICATED = (("rg_gate_w", (2, 2, 8, 128, 128)), ("even_norm_post", (1024,)), ("rg_conv_b", (1024,)),
              ("rg_gate_b", (2, 2, 8, 128)))
GATE_ROWS = 4096
LAST_REPLICATED = (("even_norm_pre", (1024,)),)
LAST_ROWS = 8
REPLICATED_ROWS = 4160
REP_PART = REPLICATED_ROWS // 8
HALF_SHARDED = SHARDED_ROWS // 2
PACK_HALF = HALF_SHARDED + REP_PART


def _seg_rows(shape):
    n = 1
    for s in shape:
        n *= s
    return -(-n // (SUBLANES * LANES)) * SUBLANES


def _pack(arrays, spec, total_rows, lead=()):
    parts = []
    for name, shape in spec:
        flat = arrays[name].reshape(lead + (-1,))
        pad = _seg_rows(shape) * LANES - flat.shape[-1]
        if pad:
            flat = jnp.pad(flat, [(0, 0)] * len(lead) + [(0, pad)])
        parts.append(flat.reshape(lead + (-1, LANES)))
    rows = jnp.concatenate(parts, axis=len(lead))
    pad = total_rows - rows.shape[len(lead)]
    return jnp.pad(rows, [(0, 0)] * len(lead) + [(0, pad), (0, 0)])


def _unpack(rows, spec, lead=()):
    out, at = {}, 0
    for name, shape in spec:
        n = 1
        for s in shape:
            n *= s
        k = _seg_rows(shape)
        seg = lax.slice_in_dim(rows, at, at + k, axis=len(lead)).reshape(lead + (-1,))
        out[name] = lax.slice_in_dim(seg, 0, n, axis=len(lead)).reshape(lead + shape)
        at += k
    return out


def _split_owners(arr):
    a = arr.reshape(arr.shape[:-1] + (4, arr.shape[-1] // 4))
    return jnp.moveaxis(a, -2, 0)


def _merge_owners(arr):
    a = jnp.moveaxis(arr, 0, -2)
    return a.reshape(a.shape[:-2] + (-1,))


HBM_SPEC = pl.BlockSpec(memory_space=pltpu.HBM)


def _position():
    x, y, c = lax.axis_index("x"), lax.axis_index("y"), lax.axis_index("c")
    chips = [(1 - x, y), (x, 1 - y), (1 - x, 1 - y)]
    return x, y, c, chips


def _remote(src, dst, send_sem, recv_sem, device):
    return pltpu.make_async_remote_copy(src_ref=src, dst_ref=dst, send_sem=send_sem, recv_sem=recv_sem,
                                        device_id=device, device_id_type=MESH)


SEM_SPEC = pl.BlockSpec(memory_space=pltpu.SEMAPHORE)
SIDE_EFFECT = pltpu.SideEffectType.DATAFLOW_SIDE_EFFECTING


def _gather_copies(ins, lands, n_h, send_sems, recv_sems):
    x, y, c, chips = _position()
    me = 2 * x + y
    copies = []
    for a in range(len(ins)):
        for k, chip in enumerate(chips):
            src = ins[a].at[c] if a < n_h else ins[a]
            dst = lands[a].at[me, c] if a < n_h else lands[a].at[me]
            copies.append(_remote(src, dst, send_sems.at[3 * a + k], recv_sems.at[3 * a + k], (chip[0], chip[1], c)))
    return copies


def gather_start(halved, whole, name):
    arrays = list(halved) + list(whole)
    n, n_h = len(arrays), len(halved)
    lands = [lax.empty((4,) + a.shape, a.dtype) for a in arrays]

    def body(*refs):
        ins, lz, send_sems, recv_sems, token = refs[:n], refs[n:2 * n], refs[2 * n], refs[2 * n + 1], refs[-1]
        for cp in _gather_copies(ins, lz, n_h, send_sems, recv_sems):
            cp.start()
        token[...] = jnp.zeros_like(token)

    operands = [pltpu.with_memory_space_constraint(a, pltpu.HBM) for a in arrays + lands]
    return pl.pallas_call(
        body, name=name,
        out_shape=(pltpu.SemaphoreType.DMA((3 * n,)), pltpu.SemaphoreType.DMA((3 * n,)))
        + tuple(pltpu.HBM(a.shape, a.dtype) for a in operands) + (jax.ShapeDtypeStruct((SUBLANES, LANES), F32),),
        in_specs=[HBM_SPEC] * (2 * n),
        out_specs=(SEM_SPEC, SEM_SPEC) + (HBM_SPEC,) * (2 * n) + (pl.BlockSpec(memory_space=pltpu.VMEM),),
        input_output_aliases={i: 2 + i for i in range(2 * n)},
        compiler_params=pltpu.CompilerParams(has_side_effects=SIDE_EFFECT),
    )(*operands)


def gather_wait(started, n_h, after, name):
    send_sems, recv_sems = started[0], started[1]
    operands = list(started[2:-1])
    n = len(operands) // 2

    def body(*refs):
        ins, lz, send_ref, recv_ref = refs[:n], refs[n:2 * n], refs[2 * n], refs[2 * n + 1]
        for cp in _gather_copies(ins, lz, n_h, send_ref, recv_ref):
            cp.wait_send()
            cp.wait_recv()

    outs = pl.pallas_call(
        body, name=name,
        out_shape=tuple(pltpu.HBM(a.shape, a.dtype) for a in operands),
        in_specs=[HBM_SPEC] * (2 * n) + [SEM_SPEC, SEM_SPEC, pl.BlockSpec(memory_space=pl.ANY)],
        out_specs=(HBM_SPEC,) * (2 * n),
        input_output_aliases={i: i for i in range(2 * n)},
        compiler_params=pltpu.CompilerParams(has_side_effects=SIDE_EFFECT),
    )(*operands, send_sems, recv_sems, after)
    return outs[n:]


def pass_to_sibling(fulls, name):
    n = len(fulls)

    def body(*refs):
        bufs = refs[n:2 * n]
        send_sems, recv_sems = refs[2 * n:]
        x, y, c, chips = _position()
        sibling = (x, y, 1 - c)
        copies = []
        for a in range(n):
            for k, chip in enumerate(chips):
                q = 2 * chip[0] + chip[1]
                cp = _remote(bufs[a].at[q, c], bufs[a].at[q, c], send_sems.at[3 * a + k], recv_sems.at[3 * a + k],
                             sibling)
                cp.start()
                copies.append(cp)
        for a in range(n):
            for k, chip in enumerate(chips):
                q = 2 * chip[0] + chip[1]
                passed = bufs[a].at[q, 1 - c]
                _remote(passed, passed, send_sems.at[3 * a + k], recv_sems.at[3 * a + k], sibling).wait_recv()
        for cp in copies:
            cp.wait_send()

    return pl.pallas_call(
        body, name=name,
        out_shape=[jax.ShapeDtypeStruct(a.shape, a.dtype) for a in fulls],
        in_specs=[HBM_SPEC] * n, out_specs=[HBM_SPEC] * n,
        input_output_aliases={i: i for i in range(n)},
        scratch_shapes=[pltpu.SemaphoreType.DMA((3 * n,)), pltpu.SemaphoreType.DMA((3 * n,))],
    )(*fulls)


def place_own(full, own, chip, name):
    _, _, r, cols = full.shape
    tr = _row_tile(r, cols)

    def body(p_ref, own_ref, full_ref, o_ref):
        o_ref[0] = own_ref[...]

    return pl.pallas_call(
        body, name=name,
        out_shape=jax.ShapeDtypeStruct(full.shape, full.dtype),
        grid_spec=pltpu.PrefetchScalarGridSpec(
            num_scalar_prefetch=1, grid=(2, r // tr),
            in_specs=[pl.BlockSpec((1, tr, cols), lambda h, i, p_ref: (h, i, 0)), pl.BlockSpec(memory_space=pl.ANY)],
            out_specs=pl.BlockSpec((1, 1, tr, cols), lambda h, i, p_ref: (p_ref[0], h, i, 0))),
        input_output_aliases={2: 0},
        compiler_params=_params("parallel", "parallel"),
    )(chip, own, full)


def exchange_with_sibling(arrays, name):
    n = len(arrays)

    def body(*refs):
        ins, outs = refs[:n], refs[n:2 * n]
        send_sems, recv_sems = refs[2 * n:]
        x, y, c, _ = _position()
        copies = []
        for a in range(n):
            cp = _remote(ins[a].at[:, 1 - c], outs[a], send_sems.at[a], recv_sems.at[a], (x, y, 1 - c))
            cp.start()
            copies.append(cp)
        for cp in copies:
            cp.wait()

    return pl.pallas_call(
        body, name=name,
        out_shape=[jax.ShapeDtypeStruct((a.shape[0],) + a.shape[2:], a.dtype) for a in arrays],
        in_specs=[HBM_SPEC] * n, out_specs=[HBM_SPEC] * n,
        scratch_shapes=[pltpu.SemaphoreType.DMA((n,)), pltpu.SemaphoreType.DMA((n,))],
    )(*arrays)


def _chip_copies(ins, lands, send_sems, recv_sems):
    x, y, c, chips = _position()
    copies = []
    for a in range(len(ins)):
        for k, chip in enumerate(chips):
            q = 2 * chip[0] + chip[1]
            copies.append(_remote(ins[a].at[q], lands[a].at[k], send_sems.at[3 * a + k], recv_sems.at[3 * a + k],
                                  (chip[0], chip[1], c)))
    return copies


def exchange_with_chips_start(arrays, name):
    n = len(arrays)
    lands = [lax.empty((3,) + a.shape[1:], a.dtype) for a in arrays]

    def body(*refs):
        ins, lz, send_sems, recv_sems, token = refs[:n], refs[n:2 * n], refs[2 * n], refs[2 * n + 1], refs[-1]
        for cp in _chip_copies(ins, lz, send_sems, recv_sems):
            cp.start()
        token[...] = jnp.zeros_like(token)

    operands = [pltpu.with_memory_space_constraint(a, pltpu.HBM) for a in list(arrays) + lands]
    return pl.pallas_call(
        body, name=name,
        out_shape=(pltpu.SemaphoreType.DMA((3 * n,)), pltpu.SemaphoreType.DMA((3 * n,)))
        + tuple(pltpu.HBM(a.shape, a.dtype) for a in operands) + (jax.ShapeDtypeStruct((SUBLANES, LANES), F32),),
        in_specs=[HBM_SPEC] * (2 * n),
        out_specs=(SEM_SPEC, SEM_SPEC) + (HBM_SPEC,) * (2 * n) + (pl.BlockSpec(memory_space=pltpu.VMEM),),
        input_output_aliases={i: 2 + i for i in range(2 * n)},
        compiler_params=pltpu.CompilerParams(has_side_effects=SIDE_EFFECT),
    )(*operands)


def exchange_with_chips_wait(started, after, name):
    send_sems, recv_sems = started[0], started[1]
    operands = list(started[2:-1])
    n = len(operands) // 2

    def body(*refs):
        ins, lz, send_ref, recv_ref = refs[:n], refs[n:2 * n], refs[2 * n], refs[2 * n + 1]
        for cp in _chip_copies(ins, lz, send_ref, recv_ref):
            cp.wait_send()
            cp.wait_recv()

    outs = pl.pallas_call(
        body, name=name,
        out_shape=tuple(pltpu.HBM(a.shape, a.dtype) for a in operands),
        in_specs=[HBM_SPEC] * (2 * n) + [SEM_SPEC, SEM_SPEC, pl.BlockSpec(memory_space=pl.ANY)],
        out_specs=(HBM_SPEC,) * (2 * n),
        input_output_aliases={i: i for i in range(2 * n)},
        compiler_params=pltpu.CompilerParams(has_side_effects=SIDE_EFFECT),
    )(*operands, send_sems, recv_sems, after)
    return outs[:n], outs[n:]


def share_totals(totals, pack_total, last_part):
    arrays = list(totals) + [pack_total]
    n = len(arrays)

    def body(*refs):
        ins, last, outs, rep, last_all = refs[:n], refs[n], refs[n + 1:2 * n + 1], refs[2 * n + 1], refs[2 * n + 2]
        send_sems, recv_sems, rep_send, rep_recv, last_send, last_recv = refs[2 * n + 3:]
        x, y, c, chips = _position()
        sibling = (x, y, 1 - c)
        me = 4 * x + 2 * y + c
        sends = []
        for a in range(n):
            cp = _remote(ins[a], outs[a], send_sems.at[a], recv_sems.at[a], sibling)
            cp.start()
            sends.append(cp)
        mine = ins[n - 1].at[pl.ds(HALF_SHARDED, REP_PART)]
        peers = [sibling]
        for chip in chips:
            peers += [(chip[0], chip[1], c), (chip[0], chip[1], 1 - c)]
        for j, peer in enumerate(peers):
            for src, dst, s_sem, r_sem in ((mine, rep, rep_send, rep_recv), (last, last_all, last_send, last_recv)):
                cp = _remote(src, dst.at[me], s_sem.at[j], r_sem.at[j], peer)
                cp.start()
                sends.append(cp)
        for a in range(n):
            _remote(outs[a], outs[a], send_sems.at[a], recv_sems.at[a], sibling).wait_recv()
        for j, peer in enumerate(peers):
            it = 4 * peer[0] + 2 * peer[1] + peer[2]
            _remote(rep.at[it], rep.at[it], rep_send.at[j], rep_recv.at[j], peer).wait_recv()
            _remote(last_all.at[it], last_all.at[it], last_send.at[j], last_recv.at[j], peer).wait_recv()
        for cp in sends:
            cp.wait_send()

    outs = pl.pallas_call(
        body, name="grad_share_totals",
        out_shape=[jax.ShapeDtypeStruct(a.shape, a.dtype) for a in arrays]
        + [jax.ShapeDtypeStruct((8, REP_PART, LANES), F32), jax.ShapeDtypeStruct((8,) + last_part.shape, F32)],
        in_specs=[HBM_SPEC] * (n + 1), out_specs=[HBM_SPEC] * (n + 2),
        scratch_shapes=[pltpu.SemaphoreType.DMA((n,)), pltpu.SemaphoreType.DMA((n,))]
        + [pltpu.SemaphoreType.DMA((7,))] * 4,
    )(*arrays, last_part)
    return outs[:n], outs[n], outs[n + 1]


def sum_parts(parts, name):
    def body(p_ref, o_ref):
        total = p_ref[0]
        for k in range(1, parts.shape[0]):
            total = total + p_ref[k]
        o_ref[...] = total

    return pl.pallas_call(body, name=name, out_shape=jax.ShapeDtypeStruct(parts.shape[1:], parts.dtype))(parts)


TILE_BYTES = 2 << 20


def _row_tile(rows, cols):
    best = None
    for t in range(SUBLANES, rows + 1, SUBLANES):
        if rows % t == 0 and t * cols * 4 <= TILE_BYTES:
            best = t
    return best if best is not None else rows


def add_sibling(mine, received, core, out_dtype, name):
    _, _, r, cols = mine.shape
    tr = _row_tile(r, cols)

    def body(c_ref, a_ref, b_ref, o_ref):
        o_ref[...] = (a_ref[0] + b_ref[...].astype(F32)).astype(out_dtype)

    return pl.pallas_call(
        body, name=name,
        out_shape=jax.ShapeDtypeStruct((4, r, cols), out_dtype),
        grid_spec=pltpu.PrefetchScalarGridSpec(
            num_scalar_prefetch=1, grid=(4, r // tr),
            in_specs=[pl.BlockSpec((1, 1, tr, cols), lambda o, i, c_ref: (o, c_ref[0], i, 0)),
                      pl.BlockSpec((1, tr, cols), lambda o, i, c_ref: (o, i, 0))],
            out_specs=pl.BlockSpec((1, tr, cols), lambda o, i, c_ref: (o, i, 0))),
        compiler_params=_params("parallel", "parallel"),
    )(core, mine, received)


def add_chips(own, received, chip, name):
    _, r, cols = own.shape
    tr = _row_tile(r, cols)

    def body(p_ref, a_ref, b0, b1, b2, o_ref):
        o_ref[...] = ((a_ref[0].astype(F32) + b0[0].astype(F32)) + b1[0].astype(F32)) + b2[0].astype(F32)

    rb = lambda k: pl.BlockSpec((1, tr, cols), lambda i, p_ref: (k, i, 0))
    return pl.pallas_call(
        body, name=name,
        out_shape=jax.ShapeDtypeStruct((r, cols), F32),
        grid_spec=pltpu.PrefetchScalarGridSpec(
            num_scalar_prefetch=1, grid=(r // tr,),
            in_specs=[pl.BlockSpec((1, tr, cols), lambda i, p_ref: (p_ref[0], i, 0)), rb(0), rb(1), rb(2)],
            out_specs=pl.BlockSpec((tr, cols), lambda i, p_ref: (i, 0))),
        compiler_params=_params("parallel"),
    )(chip, own, received, received, received)


def _adamw_update(gv, w_ref, m_ref, v_ref, d_ref, nm_ref, nv_ref):
    nm = ADAM_B1 * m_ref[...] + (1.0 - ADAM_B1) * gv
    nv = ADAM_B2 * v_ref[...] + (1.0 - ADAM_B2) * (gv * gv)
    nm_ref[...] = nm
    nv_ref[...] = nv
    m_hat = nm / (1.0 - ADAM_B1 ** ADAM_STEP)
    v_hat = nv / (1.0 - ADAM_B2 ** ADAM_STEP)
    d_ref[...] = -ADAM_LR * (m_hat / (jnp.sqrt(v_hat) + ADAM_EPS) + ADAM_WD * w_ref[...])


def adamw_halves(w, own, received, m, v, core, name, by_columns=False):
    rows, cols = w.shape

    def body(c_ref, w_ref, own_ref, rec_ref, m_ref, v_ref, g_ref, d_ref, nm_ref, nv_ref):
        gv = jnp.where(pl.program_id(0) == c_ref[0], own_ref[...], rec_ref[...])
        g_ref[...] = gv
        _adamw_update(gv, w_ref, m_ref, v_ref, d_ref, nm_ref, nv_ref)

    if by_columns:
        nr = 1
        whole = pl.BlockSpec((rows, cols // 2), lambda h, i, c_ref: (0, h))
        half = pl.BlockSpec((rows, cols // 2), lambda h, i, c_ref: (0, 0))
    else:
        r = rows // 2
        tr = _row_tile(r, cols)
        nr = r // tr
        whole = pl.BlockSpec((tr, cols), lambda h, i, c_ref: (h * nr + i, 0))
        half = pl.BlockSpec((tr, cols), lambda h, i, c_ref: (i, 0))
    return pl.pallas_call(
        body, name=name,
        out_shape=(jax.ShapeDtypeStruct((rows, cols), F32),) * 4,
        grid_spec=pltpu.PrefetchScalarGridSpec(
            num_scalar_prefetch=1, grid=(2, nr),
            in_specs=[whole, half, half, whole, whole], out_specs=(whole,) * 4),
        compiler_params=_params("parallel", "parallel"),
    )(core, w, own, received, m, v)


def adamw_many(ws, gs, ms, vs, name):
    n = len(ws)

    def body(*refs):
        ins, outs = refs[:4 * n], refs[4 * n:]
        for k in range(n):
            w_ref, g_ref, m_ref, v_ref = (ins[j * n + k] for j in range(4))
            d_ref, nm_ref, nv_ref = outs[3 * k:3 * k + 3]
            _adamw_update(g_ref[...], w_ref, m_ref, v_ref, d_ref, nm_ref, nv_ref)

    flat = pl.pallas_call(
        body, name=name,
        out_shape=[jax.ShapeDtypeStruct(w.shape, F32) for w in ws for _ in range(3)],
    )(*ws, *gs, *ms, *vs)
    return [tuple(flat[3 * k:3 * k + 3]) for k in range(n)]


def adamw(w, g, m, v, name):
    r, cols = w.shape
    tr = _row_tile(r, cols)

    def body(w_ref, g_ref, m_ref, v_ref, g_out, d_ref, nm_ref, nv_ref):
        gv = g_ref[...]
        g_out[...] = gv
        _adamw_update(gv, w_ref, m_ref, v_ref, d_ref, nm_ref, nv_ref)

    blk = pl.BlockSpec((tr, cols), lambda i: (i, 0))
    return pl.pallas_call(
        body, name=name,
        out_shape=(jax.ShapeDtypeStruct((r, cols), F32),) * 4,
        grid=(r // tr,),
        in_specs=[blk] * 4, out_specs=(blk,) * 4,
        compiler_params=_params("parallel"),
    )(w, g, m, v)


WEIGHTS = ("even_norm_pre", "even_norm_post", "even_w_in", "rg_conv_w", "rg_conv_b", "rg_gate_w", "rg_gate_b",
           "rg_lambda", "sc_conv_w", "even_w_out", "odd_norm_pre", "odd_norm_post", "odd_w_in", "gla_w_gate_lr",
           "gla_b_gate", "gla_norm_g", "odd_w_out")
BIG = ("even_w_in", "even_w_out", "odd_w_in", "odd_w_out")


def _halves(a):
    return a.reshape((2, a.shape[0] // 2) + a.shape[1:])


def kernel(x, even_norm_pre, even_norm_post, even_w_in, rg_conv_w, rg_conv_b, rg_gate_w, rg_gate_b, rg_lambda, sc_conv_w, even_w_out, odd_norm_pre, odd_norm_post, odd_w_in, gla_w_gate_lr, gla_b_gate, gla_norm_g, odd_w_out, loss_target, m_even_norm_pre, m_even_norm_post, m_even_w_in, m_rg_conv_w, m_rg_conv_b, m_rg_gate_w, m_rg_gate_b, m_rg_lambda, m_sc_conv_w, m_even_w_out, m_odd_norm_pre, m_odd_norm_post, m_odd_w_in, m_gla_w_gate_lr, m_gla_b_gate, m_gla_norm_g, m_odd_w_out, v_even_norm_pre, v_even_norm_post, v_even_w_in, v_rg_conv_w, v_rg_conv_b, v_rg_gate_w, v_rg_gate_b, v_rg_lambda, v_sc_conv_w, v_even_w_out, v_odd_norm_pre, v_odd_norm_post, v_odd_w_in, v_gla_w_gate_lr, v_gla_b_gate, v_gla_norm_g, v_odd_w_out):
    given = dict(locals())
    shard = {n: given[n][0] for n in WEIGHTS}
    m_in = {n: given["m_" + n][0] for n in WEIGHTS}
    v_in = {n: given["v_" + n][0] for n in WEIGHTS}
    mx, my, mc = lax.axis_index("x"), lax.axis_index("y"), lax.axis_index("c")
    core = jnp.reshape(mc, (1,)).astype(jnp.int32)
    chip = jnp.reshape(2 * mx + my, (1,)).astype(jnp.int32)

    small_shard = _pack(shard, SHARDED_SMALL, SHARDED_ROWS)
    big_own = [_halves(shard[n].astype(BF16)) for n in BIG]
    started_a = gather_start(big_own[:1], [small_shard], "gather_start_a")
    started_b = gather_start(big_own[1:], [], "gather_start_b")
    even_w_in_full, small_full = gather_wait(started_a, 1, started_b[-1], "gather_wait_a")
    (even_w_in_full,) = pass_to_sibling([even_w_in_full], "gather_pass_a")
    even_w_in_full = place_own(even_w_in_full, big_own[0], chip, "place_even_w_in")
    small_full = lax.dynamic_update_slice(small_full, small_shard[None], (chip[0], 0, 0))
    full = {n: shard[n] for n, _ in REPLICATED + LAST_REPLICATED}
    full.update({n: _merge_owners(a) for n, a in _unpack(small_full, SHARDED_SMALL, lead=(4,)).items()})
    full["even_w_in"] = even_w_in_full.reshape(4, D_MODEL, EVEN_IN // 4)

    def late_weights(after):
        lands = pass_to_sibling(list(gather_wait(started_b, 3, after, "gather_wait_b")), "gather_pass_b")
        lands = [place_own(a, b, chip, "place_" + n) for a, b, n in zip(lands, big_own[1:], BIG[1:])]
        odd_w_in = jnp.transpose(lands[1].reshape(4, D_MODEL, ODD_IN // 4), (1, 0, 2)).reshape(D_MODEL, ODD_IN)
        return _prepare_weights({"even_w_out": lands[0].reshape(2 * D_MODEL, D_MODEL), "odd_w_in": odd_w_in,
                                 "odd_w_out": lands[2].reshape(D_MODEL, D_MODEL)})

    pending = {}

    def slab(a):
        return a.reshape((4, 2, a.shape[1] // 2) + a.shape[2:])

    def begin(tag, slabs, to_send, dtypes):
        got = exchange_with_sibling(to_send, "grad_sibling_" + tag)
        sums = [add_sibling(a, b, core, dt, "grad_add_sibling_%s%d" % (tag, i))
                for i, (a, b, dt) in enumerate(zip(slabs, got, dtypes))]
        pending[tag] = exchange_with_chips_start(sums, "grad_chips_start_" + tag)
        return pending[tag][-1][0, 0]

    def finish(tag, after):
        sums, got = exchange_with_chips_wait(pending[tag], after, "grad_chips_wait_" + tag)
        return [add_chips(a, b, chip, "grad_add_chips_%s%d" % (tag, i)) for i, (a, b) in enumerate(zip(sums, got))]

    def reduce_first(g, g16):
        odd_w_in = slab(jnp.transpose(g["odd_w_in"].reshape(D_MODEL, 4, ODD_IN // 4), (1, 0, 2)))
        slabs = [odd_w_in] + [slab(g[n].reshape(4, -1, D_MODEL)) for n in ("odd_w_out", "even_w_out")]
        to_send = [odd_w_in.astype(BF16)] + [slab(g16[n].reshape(4, -1, D_MODEL)) for n in ("odd_w_out", "even_w_out")]
        return begin("a", slabs, to_send, [BF16] * 3)

    def reduce_second(g, g16):
        pending["totals_a"] = finish("a", g["even_w_in"])
        rep_rows = _pack(g, REPLICATED, REPLICATED_ROWS).reshape(4, 2, REP_PART, LANES)
        sh_rows = _pack({n: _split_owners(g[n]) for n, _ in SHARDED_SMALL}, SHARDED_SMALL, SHARDED_ROWS, lead=(4,))
        pack = jnp.concatenate([sh_rows.reshape(4, 2, HALF_SHARDED, LANES), rep_rows], axis=2)
        return begin("b", [slab(g["even_w_in"]), pack], [slab(g16["even_w_in"]), pack], [BF16, F32])

    loss, grad_x, g = local_step(x[0], loss_target[0], _prepare_weights(full), reduce_first, reduce_second,
                                 late_weights)
    odd_w_in_t, odd_w_out_t, even_w_out_t = pending["totals_a"]
    even_w_in_t, pack_t = finish("b", grad_x)
    totals = [even_w_in_t, even_w_out_t, odd_w_in_t, odd_w_out_t]
    last_part = jnp.concatenate([_pack(g, LAST_REPLICATED, LAST_ROWS), loss])
    from_core, rep_all, last_all = share_totals(totals, pack_t, last_part)
    me = 2 * chip[0] + core[0]
    mine, theirs = pack_t[:HALF_SHARDED], from_core[4][:HALF_SHARDED]
    sh_total = jnp.where(mc == 0, jnp.concatenate([mine, theirs]), jnp.concatenate([theirs, mine]))
    rep_all = lax.dynamic_update_slice(rep_all, pack_t[None, HALF_SHARDED:], (me, 0, 0))
    rep_total = rep_all.reshape(REPLICATED_ROWS, LANES)
    last_total = sum_parts(lax.dynamic_update_slice(last_all, last_part[None], (me, 0, 0)), "grad_sum_last")
    last_total, loss = last_total[:LAST_ROWS], last_total[LAST_ROWS, 0]
    grads = {}

    delta, new_m, new_v = {}, {}, {}
    for i, n in enumerate(BIG):
        if shard[n].shape[1] % LANES:
            outs = adamw_halves(shard[n].T, totals[i].T, from_core[i].T, m_in[n].T, v_in[n].T, core, "adamw_" + n,
                                by_columns=True)
            grads[n], delta[n], new_m[n], new_v[n] = [o.T for o in outs]
        else:
            grads[n], delta[n], new_m[n], new_v[n] = adamw_halves(shard[n], totals[i], from_core[i], m_in[n],
                                                                  v_in[n], core, "adamw_" + n)
    gate = [src["rg_gate_w"].reshape(GATE_ROWS, LANES) for src in (shard, m_in, v_in)]
    grads["rg_gate_w"], delta["rg_gate_w"], new_m["rg_gate_w"], new_v["rg_gate_w"] = adamw(
        gate[0], rep_total, gate[1], gate[2], "adamw_rg_gate_w")
    rest = REPLICATED[1:]
    rest_rows = sum(_seg_rows(shape) for _, shape in rest)
    grads.update(_unpack(sh_total, SHARDED_SMALL))
    grads.update(_unpack(rep_total[GATE_ROWS:GATE_ROWS + rest_rows], rest))
    grads.update(_unpack(last_total, LAST_REPLICATED))
    names = [n for n, _ in SHARDED_SMALL + rest + LAST_REPLICATED]
    rows_of = lambda a, n: a.reshape(-1, given[n].shape[-1])
    outs = adamw_many([rows_of(given[n], n) for n in names], [rows_of(grads[n], n) for n in names],
                      [rows_of(given["m_" + n], n) for n in names], [rows_of(given["v_" + n], n) for n in names],
                      "adamw_small")
    for n, (d, nm, nv) in zip(names, outs):
        delta[n], new_m[n], new_v[n] = d, nm, nv
    result = [loss, grad_x[None]]
    for group in (grads, delta, new_m, new_v):
        result += [group[n].reshape(given[n].shape) for n in WEIGHTS]
    return tuple(result)
```

```python
import jax
import jax.numpy as jnp
from jax import lax
from jax.experimental import pallas as pl
from jax.experimental.pallas import tpu as pltpu

F32 = jnp.float32
BF16 = jnp.bfloat16
MESH = pl.DeviceIdType.MESH

D_MODEL = 1024
NORM_EPS = 1e-6
RG_HEADS = 8
RG_HEAD_DIM = 128
RG_C = 8.0
EVEN_IN = 6144
ODD_IN = 3104
ODD_IN_PAD = 3200
GLA_HEADS = 4
GLA_DK = 128
GLA_DV = 256
GLA_RANK = 16
GLA_NORMALIZER = 16.0
GLA_CHUNK = 128
LR_COL = 3072

ADAM_LR = 0.001
ADAM_B1 = 0.9
ADAM_B2 = 0.999
ADAM_EPS = 1e-08
ADAM_WD = 0.01
ADAM_STEP = 10

SUBLANES = 8
HALO = 16
LANES = 128
VMEM_LIMIT = 56 * 2 ** 20

ROW_TILE = 512
SCAN_TILE = 256
GLA_BLOCK = 2048
MIX_TILE = 256


def _params(*sem):
    return pltpu.CompilerParams(dimension_semantics=sem, vmem_limit_bytes=VMEM_LIMIT)


def _full(shape):
    n = len(shape)
    return pl.BlockSpec(shape, lambda *_: (0,) * n)


def _sigmoid(x):
    return 0.5 + 0.5 * jnp.tanh(0.5 * x)


def _softplus(x):
    return jnp.maximum(x, 0.0) + jnp.log(1.0 + jnp.exp(-jnp.abs(x)))


def _dot(a, b):
    return jnp.dot(a, b, preferred_element_type=F32)


def _dot_nt(a, b):
    return lax.dot_general(a, b, (((1,), (1,)), ((), ())), preferred_element_type=F32)


def _dot_tn(a, b):
    return lax.dot_general(a, b, (((0,), (0,)), ((), ())), preferred_element_type=F32)


def _bdot(a, b, ca, cb):
    return lax.dot_general(a, b, (((ca,), (cb,)), ((0,), (0,))), preferred_element_type=F32)


def _halo_specs(rows, cols, col_block, n_row_tiles, tix):
    per = rows // HALO
    last = n_row_tiles * per - 1

    def split(args):
        if len(args) == 2:
            return tix(args[1]), col_block + args[0]
        return tix(args[0]), col_block

    def prev(*args):
        t, c = split(args)
        return (jnp.maximum(t * per - 1, 0), c)

    def main(*args):
        return split(args)

    def nxt(*args):
        t, c = split(args)
        return (jnp.minimum((t + 1) * per, last), c)

    return [pl.BlockSpec((HALO, cols), prev), pl.BlockSpec((rows, cols), main),
            pl.BlockSpec((HALO, cols), nxt)]


def _extend(prev_ref, main_ref, next_ref, is_first, is_last):
    p = jnp.where(is_first, 0.0, prev_ref[...].astype(F32))
    n = jnp.where(is_last, 0.0, next_ref[...].astype(F32))
    return jnp.concatenate([p, main_ref[...].astype(F32), n], axis=0)


def _shifted(ext, offset, rows):
    if offset == 0:
        return ext[HALO:HALO + rows]
    n = ext.shape[0]
    return pltpu.roll(ext, (-offset) % n, 0)[HALO:HALO + rows]


def _conv(ext, w, left, rows):
    out = None
    for k in range(w.shape[0]):
        term = _shifted(ext, k - left, rows) * w[k:k + 1]
        out = term if out is None else out + term
    return out


def _conv_transpose(ext, w, left, rows):
    out = None
    for k in range(w.shape[0]):
        term = _shifted(ext, left - k, rows) * w[k:k + 1]
        out = term if out is None else out + term
    return out


def _colsum(x):
    return jnp.sum(x, axis=0, keepdims=True)


def _accumulate(ref, value, step):
    @pl.when(step == 0)
    def _():
        ref[...] = value

    @pl.when(step > 0)
    def _():
        ref[...] += value


PROJ_TILE_BYTES = 7 * 2 ** 20


def _proj_row_tile(rows, width, dtype):
    tm = min(ROW_TILE, rows)
    while tm * width * jnp.dtype(dtype).itemsize > PROJ_TILE_BYTES and tm % (2 * HALO) == 0:
        tm //= 2
    return tm


def norm_matmul(x, gain, w, out_dtype, name):
    rows, d = x.shape
    n_col_tiles, _, tn = w.shape
    tm = _proj_row_tile(rows, n_col_tiles * tn, out_dtype)

    def body(x_ref, g_ref, w_ref, proj_ref, h_ref):
        xv = x_ref[...]
        rstd = lax.rsqrt(jnp.mean(xv * xv, axis=-1, keepdims=True) + NORM_EPS)
        hv = (xv * rstd * g_ref[...]).astype(BF16)
        h_ref[...] = hv
        for j in range(n_col_tiles):
            proj_ref[:, j * tn:(j + 1) * tn] = _dot(hv, w_ref[j]).astype(out_dtype)

    row = lambda cols: pl.BlockSpec((tm, cols), lambda i: (i, 0))
    return pl.pallas_call(
        body, name=name,
        out_shape=(jax.ShapeDtypeStruct((rows, n_col_tiles * tn), out_dtype), jax.ShapeDtypeStruct((rows, d), BF16)),
        grid=(rows // tm,),
        in_specs=[row(d), _full((1, d)), _full(w.shape)],
        out_specs=(row(n_col_tiles * tn), row(d)),
        compiler_params=_params("parallel"),
    )(x, gain, w)


def inproj_bwd(dproj, w, x, gain, dres, name):
    rows, d = x.shape
    n_col_tiles, _, tn = w.shape
    tm = _proj_row_tile(rows, n_col_tiles * tn, dproj.dtype)

    def body(dp_ref, w_ref, x_ref, g_ref, dres_ref, dx_ref, dg_ref):
        dh = None
        for j in range(n_col_tiles):
            part = _dot_nt(dp_ref[:, j * tn:(j + 1) * tn], w_ref[j])
            dh = part if dh is None else dh + part
        _inproj_finish(dh, x_ref, g_ref, dres_ref, dx_ref, dg_ref, pl.program_id(0))

    row = lambda cols: pl.BlockSpec((tm, cols), lambda i: (i, 0))
    return pl.pallas_call(
        body, name=name,
        out_shape=(jax.ShapeDtypeStruct((rows, d), F32), jax.ShapeDtypeStruct((1, d), F32)),
        grid=(rows // tm,),
        in_specs=[row(n_col_tiles * tn), _full(w.shape), row(d), _full((1, d)), row(d)],
        out_specs=(row(d), _full((1, d))),
        compiler_params=_params("arbitrary"),
    )(dproj, w, x, gain, dres)


def _inproj_finish(dh, x_ref, g_ref, dres_ref, dx_ref, dg_ref, step):
    xv = x_ref[...]
    rstd = lax.rsqrt(jnp.mean(xv * xv, axis=-1, keepdims=True) + NORM_EPS)
    xhat = xv * rstd
    dxn = dh * g_ref[...]
    dx_ref[...] = dres_ref[...] + rstd * (dxn - xhat * jnp.mean(dxn * xhat, axis=-1, keepdims=True))
    _accumulate(dg_ref, _colsum(dh * xhat), step)


def inproj_bwd_pieces(pieces, w, x, gain, dres, name):
    rows, d = x.shape
    tm = min(ROW_TILE, rows)
    n = len(pieces)
    widths = [p.shape[1] for p in pieces]
    assert sum(widths) == w.shape[2]

    def body(*refs):
        w_ref, x_ref, g_ref, dres_ref, dx_ref, dg_ref = refs[n:]
        dproj = jnp.concatenate([refs[k][...] for k in range(n)], axis=1)
        _inproj_finish(_dot_nt(dproj, w_ref[0]), x_ref, g_ref, dres_ref, dx_ref, dg_ref, pl.program_id(0))

    row = lambda cols: pl.BlockSpec((tm, cols), lambda i: (i, 0))
    return pl.pallas_call(
        body, name=name,
        out_shape=(jax.ShapeDtypeStruct((rows, d), F32), jax.ShapeDtypeStruct((1, d), F32)),
        grid=(rows // tm,),
        in_specs=[row(wd) for wd in widths] + [_full(w.shape), row(d), _full((1, d)), row(d)],
        out_specs=(row(d), _full((1, d))),
        compiler_params=_params("arbitrary"),
    )(*pieces, w, x, gain, dres)


def matmul_dw_pieces(a, pieces, name):
    rows, m = a.shape
    tk = min(2 * ROW_TILE, rows)
    steps = rows // tk
    n = len(pieces)
    width = sum(p.shape[1] for p in pieces)

    def body(*refs):
        a_ref, ins, out, out16 = refs[0], refs[1:1 + n], refs[1 + n], refs[2 + n]
        b = jnp.concatenate([ref[...] for ref in ins], axis=1)
        _accumulate(out, _dot_tn(a_ref[...], b), pl.program_id(0))

        @pl.when(pl.program_id(0) == steps - 1)
        def _():
            out16[...] = out[...].astype(BF16)

    return pl.pallas_call(
        body, name=name,
        out_shape=(jax.ShapeDtypeStruct((m, width), F32), jax.ShapeDtypeStruct((m, width), BF16)),
        grid=(steps,),
        in_specs=[pl.BlockSpec((tk, m), lambda k: (k, 0))]
        + [pl.BlockSpec((tk, p.shape[1]), lambda k: (k, 0)) for p in pieces],
        out_specs=(_full((m, width)), _full((m, width))),
        compiler_params=_params("arbitrary"),
    )(a, *pieces)


def matmul_dw(a, b, bn, name):
    rows, m = a.shape
    n = b.shape[1]
    tk = min((4 if n > bn else 2) * ROW_TILE, rows)
    steps = rows // tk

    def body(a_ref, b_ref, o_ref, o16_ref):
        part = _dot_tn(a_ref[...], b_ref[...])

        @pl.when(pl.program_id(1) == 0)
        def _():
            o_ref[0] = part

        @pl.when(pl.program_id(1) > 0)
        def _():
            o_ref[0] += part

        @pl.when(pl.program_id(1) == steps - 1)
        def _():
            o16_ref[0] = o_ref[0].astype(BF16)

    out = pl.BlockSpec((1, m, bn), lambda j, k: (j, 0, 0))
    return pl.pallas_call(
        body, name=name,
        out_shape=(jax.ShapeDtypeStruct((n // bn, m, bn), F32), jax.ShapeDtypeStruct((n // bn, m, bn), BF16)),
        grid=(n // bn, steps),
        in_specs=[pl.BlockSpec((tk, m), lambda j, k: (k, 0)), pl.BlockSpec((tk, bn), lambda j, k: (k, j))],
        out_specs=(out, out),
        compiler_params=_params("parallel", "arbitrary"),
    )(a, b)


def _scan(a, b, carry, reverse):
    n, c = a.shape
    blocks = n // SUBLANES
    a = a.reshape(blocks, SUBLANES, c)
    b = b.reshape(blocks, SUBLANES, c)
    pos = lax.broadcasted_iota(jnp.int32, (1, SUBLANES, c), 1)
    s = 1
    while s < SUBLANES:
        shift, valid = (SUBLANES - s, pos < SUBLANES - s) if reverse else (s, pos >= s)
        a_s, b_s = pltpu.roll(a, shift, 1), pltpu.roll(b, shift, 1)
        b = jnp.where(valid, a * b_s + b, b)
        a = jnp.where(valid, a * a_s, a)
        s *= 2
    out = [None] * blocks
    for k in (range(blocks - 1, -1, -1) if reverse else range(blocks)):
        h = a[k] * carry + b[k]
        out[k] = h
        carry = h[0:1] if reverse else h[SUBLANES - 1:SUBLANES]
    return jnp.concatenate(out, axis=0)


def _rg_gates(ua, gw_ref, gb, lam):
    ub = ua.astype(BF16)
    pre_r, pre_i = [], []
    for h in range(RG_HEADS):
        z = _dot(ub[:, h * RG_HEAD_DIM:(h + 1) * RG_HEAD_DIM], gw_ref[h])
        pre_r.append(z[:, :RG_HEAD_DIM])
        pre_i.append(z[:, RG_HEAD_DIM:])
    r = _sigmoid(jnp.concatenate(pre_r, axis=1) + gb[0:1])
    i = _sigmoid(jnp.concatenate(pre_i, axis=1) + gb[1:2])
    sp = _softplus(-lam)
    log_a = -RG_C * r * sp
    a = jnp.exp(log_a)
    mult = jnp.sqrt(1.0 - a * a)
    return r, i, sp, a, mult


def _rg_weight_specs():
    return [_full((4, D_MODEL)), _full((1, D_MODEL)), _full((RG_HEADS, RG_HEAD_DIM, 2 * RG_HEAD_DIM)),
            _full((2, D_MODEL)), _full((1, D_MODEL))]


def rglru_fwd(proj, conv_w, conv_b, gate_w, gate_b, lam, reverse, name):
    rows_total = proj.shape[0]
    rows = min(SCAN_TILE, rows_total)
    n_tiles = rows_total // rows
    tix = (lambda i: n_tiles - 1 - i) if reverse else (lambda i: i)

    def body(xp, xm, xn, cw_ref, cb_ref, gw_ref, gb_ref, lam_ref, h_ref, acts_ref, carry):
        i = pl.program_id(0)
        t = tix(i)
        ext = _extend(xp, xm, xn, t == 0, t == n_tiles - 1)
        ua = _conv(ext, cw_ref[...], 2, rows) + cb_ref[...]
        r, gi, _, a, mult = _rg_gates(ua, gw_ref, gb_ref[...], lam_ref[...])
        for k, saved in enumerate((ua, r, gi, a, mult)):
            acts_ref[k] = saved
        b = mult * (gi * ua)

        @pl.when(i == 0)
        def _():
            carry[...] = jnp.zeros_like(carry)

        h = _scan(a, b, carry[0:1], reverse)
        h_ref[...] = h
        edge = h[0:1] if reverse else h[rows - 1:rows]
        carry[...] = jnp.broadcast_to(edge, carry.shape)

    return pl.pallas_call(
        body, name=name,
        out_shape=(jax.ShapeDtypeStruct((rows_total, D_MODEL), F32),
                   jax.ShapeDtypeStruct((5, rows_total, D_MODEL), F32)),
        grid=(n_tiles,),
        in_specs=_halo_specs(rows, D_MODEL, 0, n_tiles, tix) + _rg_weight_specs(),
        out_specs=(pl.BlockSpec((rows, D_MODEL), lambda i: (tix(i), 0)),
                   pl.BlockSpec((5, rows, D_MODEL), lambda i: (0, tix(i), 0))),
        scratch_shapes=[pltpu.VMEM((SUBLANES, D_MODEL), F32)],
        compiler_params=_params("arbitrary"),
    )(proj, proj, proj, conv_w, conv_b, gate_w, gate_b, lam)


def rglru_bwd(proj, dycat, h_dir, acts, gate_w, lam, add_dua, reverse, name):
    rows_total = proj.shape[0]
    rows = min(SCAN_TILE, rows_total)
    n_tiles = rows_total // rows
    tix = (lambda i: i) if reverse else (lambda i: n_tiles - 1 - i)
    za_block = 1

    def body(acts_ref, za_ref, dya_ref, hp, hm, hn, gw_ref, lam_ref, *rest):
        other = rest[0][...] if add_dua is not None else 0.0
        dua_ref, dgw_ref, dgb_ref, dlam_ref, carry = rest[-5:]
        step = pl.program_id(0)
        t = tix(step)
        first, last = t == 0, t == n_tiles - 1
        ua, r, gi, a, mult = (acts_ref[k] for k in range(5))
        lam_v = lam_ref[...]
        sp = _softplus(-lam_v)
        za = za_ref[...].astype(F32)
        dh = dya_ref[...] * (za * _sigmoid(za))

        @pl.when(step == 0)
        def _():
            carry[...] = jnp.zeros_like(carry)

        old = carry[0:1]
        mu = _scan(a, a * dh, old, not reverse)
        row = lax.broadcasted_iota(jnp.int32, mu.shape, 0)
        if reverse:
            mu_next = jnp.where(row == 0, old, pltpu.roll(mu, 1, 0))
            carry[...] = jnp.broadcast_to(mu[rows - 1:rows], carry.shape)
            h_ext = _extend(hp, hm, hn, first, last)
            h_prev = _shifted(h_ext, 1, rows)
        else:
            mu_next = jnp.where(row == rows - 1, old, pltpu.roll(mu, rows - 1, 0))
            carry[...] = jnp.broadcast_to(mu[0:1], carry.shape)
            h_ext = _extend(hp, hm, hn, first, last)
            h_prev = _shifted(h_ext, -1, rows)
        db = dh + mu_next
        da = db * h_prev
        d_mult = db * (gi * ua)
        di = db * (mult * ua)
        dua = db * (mult * gi)
        dlog_a = da * a - d_mult * (a * a) / mult
        dr = dlog_a * (-RG_C * sp)
        dlam = _colsum(dlog_a * (-RG_C * r)) * (-_sigmoid(-lam_v))
        dpr = dr * (r * (1.0 - r))
        dpi = di * (gi * (1.0 - gi))
        dgb = jnp.concatenate([_colsum(dpr), _colsum(dpi)], axis=0)
        ub = ua.astype(BF16)
        dua_heads, dgw_heads = [], []
        for h in range(RG_HEADS):
            cols = slice(h * RG_HEAD_DIM, (h + 1) * RG_HEAD_DIM)
            dz = jnp.concatenate([dpr[:, cols], dpi[:, cols]], axis=1).astype(BF16)
            dgw_heads.append(_dot_tn(ub[:, cols], dz))
            dua_heads.append(_dot_nt(dz, gw_ref[h]))
        dua_ref[...] = dua + jnp.concatenate(dua_heads, axis=1) + other

        @pl.when(step == 0)
        def _():
            for h in range(RG_HEADS):
                dgw_ref[h] = dgw_heads[h]
            dgb_ref[...] = dgb
            dlam_ref[...] = dlam

        @pl.when(step > 0)
        def _():
            for h in range(RG_HEADS):
                dgw_ref[h] += dgw_heads[h]
            dgb_ref[...] += dgb
            dlam_ref[...] += dlam

    row_spec = lambda col: pl.BlockSpec((rows, D_MODEL), lambda i: (tix(i), col))
    return pl.pallas_call(
        body, name=name,
        out_shape=(jax.ShapeDtypeStruct((rows_total, D_MODEL), F32),
                   jax.ShapeDtypeStruct((RG_HEADS, RG_HEAD_DIM, 2 * RG_HEAD_DIM), F32),
                   jax.ShapeDtypeStruct((2, D_MODEL), F32), jax.ShapeDtypeStruct((1, D_MODEL), F32)),
        grid=(n_tiles,),
        in_specs=([pl.BlockSpec((5, rows, D_MODEL), lambda i: (0, tix(i), 0)), row_spec(za_block), row_spec(0)]
                  + _halo_specs(rows, D_MODEL, 0, n_tiles, tix)
                  + [_full((RG_HEADS, RG_HEAD_DIM, 2 * RG_HEAD_DIM)), _full((1, D_MODEL))]
                  + ([] if add_dua is None else [row_spec(0)])),
        out_specs=(row_spec(0), _full((RG_HEADS, RG_HEAD_DIM, 2 * RG_HEAD_DIM)), _full((2, D_MODEL)),
                   _full((1, D_MODEL))),
        scratch_shapes=[pltpu.VMEM((SUBLANES, D_MODEL), F32)],
        compiler_params=_params("arbitrary"),
    )(acts, proj, dycat, h_dir, h_dir, h_dir, gate_w, lam, *([] if add_dua is None else [add_dua]))


def _extend_cols(refs, block, is_first, is_last):
    cols = slice(block * D_MODEL, (block + 1) * D_MODEL)
    prev_ref, main_ref, next_ref = refs
    p = jnp.where(is_first, 0.0, prev_ref[:, cols].astype(F32))
    n = jnp.where(is_last, 0.0, next_ref[:, cols].astype(F32))
    return jnp.concatenate([p, main_ref[:, cols].astype(F32), n], axis=0)


def even_mix_fwd(proj, h_f, h_b, sc_w, name):
    rows_total = proj.shape[0]
    rows = min(2 * MIX_TILE, rows_total)
    n_tiles = rows_total // rows
    ident = lambda i: i

    def body(za_ref, hf_ref, hb_ref, xbp, xbm, xbn, gcp, gcm, gcn, gb_ref, zb_ref, w_ref, y_ref):
        t = pl.program_id(0)
        first, last = t == 0, t == n_tiles - 1
        za = za_ref[...].astype(F32)
        y_ref[:, 0:D_MODEL] = ((hf_ref[...] + hb_ref[...]) * (za * _sigmoid(za))).astype(BF16)
        p_ext = _extend(xbp, xbm, xbn, first, last) * _extend(gcp, gcm, gcn, first, last)
        cv = _conv(p_ext, w_ref[...], 1, rows)
        zb = zb_ref[...].astype(F32)
        y_ref[:, D_MODEL:2 * D_MODEL] = (gb_ref[...].astype(F32) * cv * (zb * _sigmoid(zb))).astype(BF16)

    blk = lambda col: pl.BlockSpec((rows, D_MODEL), lambda i: (i, col))
    return pl.pallas_call(
        body, name=name,
        out_shape=jax.ShapeDtypeStruct((rows_total, 2 * D_MODEL), BF16),
        grid=(n_tiles,),
        in_specs=([blk(1), blk(0), blk(0)] + _halo_specs(rows, D_MODEL, 2, n_tiles, ident)
                  + _halo_specs(rows, D_MODEL, 4, n_tiles, ident) + [blk(3), blk(5), _full((3, D_MODEL))]),
        out_specs=pl.BlockSpec((rows, 2 * D_MODEL), lambda i: (i, 0)),
        compiler_params=_params("parallel"),
    )(proj, h_f, h_b, proj, proj, proj, proj, proj, proj, proj, proj, sc_w)


def even_mix_bwd(proj, dycat, h_f, h_b, dua, conv_w, sc_w, name):
    rows_total, width = proj.shape
    rows = min(MIX_TILE, rows_total)
    n_tiles = rows_total // rows
    ident = lambda i: i

    def body(pp, pm, pn, dyp, dym, dyn, hf_ref, hb_ref, dup, dum, dun, cw_ref, sw_ref,
             dp_ref, dcw_ref, dcb_ref, dsw_ref):
        def put(k, value):
            dp_ref[:, k * D_MODEL:(k + 1) * D_MODEL] = value.astype(BF16)

        t = pl.program_id(0)
        first, last = t == 0, t == n_tiles - 1
        proj_ext = lambda k: _extend_cols((pp, pm, pn), k, first, last)
        mid = slice(HALO, HALO + rows)
        za = pm[:, D_MODEL:2 * D_MODEL].astype(F32)
        sa = _sigmoid(za)
        put(1, dym[:, 0:D_MODEL] * (hf_ref[...] + hb_ref[...]) * (sa * (1.0 + za * (1.0 - sa))))
        dua_ext = _extend(dup, dum, dun, first, last)
        cw = cw_ref[...]
        put(0, _conv_transpose(dua_ext, cw, 2, rows))
        dua_mid = dua_ext[mid]
        xa_ext = proj_ext(0)
        dcw = jnp.concatenate([_colsum(dua_mid * _shifted(xa_ext, k - 2, rows)) for k in range(4)], axis=0)
        dcb = _colsum(dua_mid)
        xb_ext, gb_ext, gc_ext, zb_ext = proj_ext(2), proj_ext(3), proj_ext(4), proj_ext(5)
        p_ext = xb_ext * gc_ext
        sb_ext = _sigmoid(zb_ext)
        dyb_ext = _extend_cols((dyp, dym, dyn), 1, first, last)
        dcv_ext = dyb_ext * gb_ext * (zb_ext * sb_ext)
        sw = sw_ref[...]
        p_at = [_shifted(p_ext, k - 1, rows) for k in range(3)]
        cv = (p_at[0] * sw[0:1] + p_at[1] * sw[1:2]) + p_at[2] * sw[2:3]
        zb, sb, dyb, gb = zb_ext[mid], sb_ext[mid], dyb_ext[mid], gb_ext[mid]
        put(3, dyb * cv * (zb * sb))
        put(5, dyb * gb * cv * (sb * (1.0 + zb * (1.0 - sb))))
        dp = _conv_transpose(dcv_ext, sw, 1, rows)
        put(4, dp * xb_ext[mid])
        put(2, dp * gc_ext[mid])
        dcv = dcv_ext[mid]
        dsw = jnp.concatenate([_colsum(dcv * p_at[k]) for k in range(3)], axis=0)
        _accumulate(dcw_ref, dcw, t)
        _accumulate(dcb_ref, dcb, t)
        _accumulate(dsw_ref, dsw, t)

    own = pl.BlockSpec((rows, D_MODEL), lambda i: (i, 0))
    return pl.pallas_call(
        body, name=name,
        out_shape=(jax.ShapeDtypeStruct((rows_total, 6 * D_MODEL), BF16),
                   jax.ShapeDtypeStruct((4, D_MODEL), F32), jax.ShapeDtypeStruct((1, D_MODEL), F32),
                   jax.ShapeDtypeStruct((3, D_MODEL), F32)),
        grid=(n_tiles,),
        in_specs=(_halo_specs(rows, width, 0, n_tiles, ident) + _halo_specs(rows, 2 * D_MODEL, 0, n_tiles, ident)
                  + [own, own] + _halo_specs(rows, D_MODEL, 0, n_tiles, ident)
                  + [_full((4, D_MODEL)), _full((3, D_MODEL))]),
        out_specs=(pl.BlockSpec((rows, 6 * D_MODEL), lambda i: (i, 0)), _full((4, D_MODEL)), _full((1, D_MODEL)),
                   _full((3, D_MODEL))),
        compiler_params=_params("arbitrary"),
    )(proj, proj, proj, dycat, dycat, dycat, h_f, h_b, dua, dua, dua, conv_w, sc_w)


def even_out_fwd(ycat, w_out, gain, x, name):
    rows, d = x.shape
    k = ycat.shape[1]
    tm = min(ROW_TILE, rows)

    def body(yc_ref, w_ref, g_ref, x_ref, x1_ref, y_ref):
        y = _dot(yc_ref[...], w_ref[...])
        y_ref[...] = y
        rstd = lax.rsqrt(jnp.mean(y * y, axis=-1, keepdims=True) + NORM_EPS)
        x1_ref[...] = x_ref[...] + y * rstd * g_ref[...]

    row = lambda n: pl.BlockSpec((tm, n), lambda i: (i, 0))
    return pl.pallas_call(
        body, name=name,
        out_shape=(jax.ShapeDtypeStruct((rows, d), F32),) * 2,
        grid=(rows // tm,),
        in_specs=[row(k), _full((k, d)), _full((1, d)), row(d)],
        out_specs=(row(d), row(d)),
        compiler_params=_params("parallel"),
    )(ycat, w_out, gain, x)


def _rmsnorm_bwd(dout, y, gain):
    rstd = lax.rsqrt(jnp.mean(y * y, axis=-1, keepdims=True) + NORM_EPS)
    yhat = y * rstd
    dyn = dout * gain
    dy = rstd * (dyn - yhat * jnp.mean(dyn * yhat, axis=-1, keepdims=True))
    return dy, dout * yhat


def even_out_bwd(dx1, y, gain, w_out, name):
    rows, d = y.shape
    k = w_out.shape[0]
    tm = min(ROW_TILE, rows)

    def body(dx_ref, y_ref, g_ref, w_ref, dy_ref, dyc_ref, dg_ref):
        dy, dg_rows = _rmsnorm_bwd(dx_ref[...], y_ref[...], g_ref[...])
        dyb = dy.astype(BF16)
        dy_ref[...] = dyb
        dyc_ref[...] = _dot_nt(dyb, w_ref[...])
        _accumulate(dg_ref, _colsum(dg_rows), pl.program_id(0))

    row = lambda n: pl.BlockSpec((tm, n), lambda i: (i, 0))
    return pl.pallas_call(
        body, name=name,
        out_shape=(jax.ShapeDtypeStruct((rows, d), BF16), jax.ShapeDtypeStruct((rows, k), F32),
                   jax.ShapeDtypeStruct((1, d), F32)),
        grid=(rows // tm,),
        in_specs=[row(d), row(d), _full((1, d)), _full((k, d))],
        out_specs=(row(d), row(k), _full((1, d))),
        compiler_params=_params("arbitrary"),
    )(dx1, y, gain, w_out)


def _chunk_cumsum(g, reverse):
    n, c = g.shape
    chunks, per = n // GLA_CHUNK, GLA_CHUNK // SUBLANES
    g = g.reshape(n // SUBLANES, SUBLANES, c)
    pos = lax.broadcasted_iota(jnp.int32, (1, SUBLANES, c), 1)
    s = 1
    while s < SUBLANES:
        if reverse:
            g = g + jnp.where(pos < SUBLANES - s, pltpu.roll(g, SUBLANES - s, 1), 0.0)
        else:
            g = g + jnp.where(pos >= s, pltpu.roll(g, s, 1), 0.0)
        s *= 2
    g = g.reshape(chunks, per, SUBLANES, c)
    out, carry = [None] * per, None
    for k in (range(per - 1, -1, -1) if reverse else range(per)):
        out[k] = g[:, k] if carry is None else g[:, k] + carry
        carry = out[k][:, 0:1] if reverse else out[k][:, SUBLANES - 1:SUBLANES]
    return jnp.stack(out, axis=1).reshape(n, c)


def _gla_prepare(q_ref, k_ref, lr_ref, wg_ref, bg_ref, reverse, n_chunks):
    z = _dot(lr_ref[...].astype(BF16), wg_ref[0]) + bg_ref[0]
    g = -_softplus(-z) * (1.0 / GLA_NORMALIZER)
    bcum = _chunk_cumsum(g, reverse).reshape(n_chunks, GLA_CHUNK, GLA_DK)
    edge = 0 if reverse else GLA_CHUNK - 1
    btot = bcum[:, edge:edge + 1, :]
    e_pos = jnp.exp(bcum)
    e_neg = jnp.exp(-bcum)
    e_st = jnp.exp(btot - bcum)
    q3 = q_ref[...].reshape(n_chunks, GLA_CHUNK, GLA_DK)
    k3 = k_ref[...].reshape(n_chunks, GLA_CHUNK, GLA_DK)
    scale = GLA_DK ** -0.5
    q_in = q3 * scale * e_pos
    k_in = k3 * e_neg
    k_st = k3 * e_st
    dec = jnp.exp(btot)
    return z, q_in, k_in, k_st, dec, (scale * e_pos, e_neg, e_st)


def _gla_mask(reverse):
    i = lax.broadcasted_iota(jnp.int32, (GLA_CHUNK, GLA_CHUNK), 0)
    j = lax.broadcasted_iota(jnp.int32, (GLA_CHUNK, GLA_CHUNK), 1)
    return (j >= i) if reverse else (j <= i)


def _gla_specs(rows, n_blocks, reverse):
    tix = (lambda s: n_blocks - 1 - s) if reverse else (lambda s: s)
    d = 1 if reverse else 0
    lr_block = LR_COL // LANES
    specs = [pl.BlockSpec((rows, GLA_DK), lambda h, s: (tix(s), h)),
             pl.BlockSpec((rows, GLA_DK), lambda h, s: (tix(s), GLA_HEADS + h)),
             pl.BlockSpec((rows, GLA_DV), lambda h, s: (tix(s), GLA_HEADS + h)),
             pl.BlockSpec((rows, LANES), lambda h, s: (tix(s), lr_block)),
             pl.BlockSpec((1, LANES, GLA_DK), lambda h, s: (d, 0, h)),
             pl.BlockSpec((1, 1, GLA_DK), lambda h, s: (d, 0, h))]
    return specs, tix


def gla_fwd(proj, wg_pad, bg, add_o, reverse, name):
    rows_total = proj.shape[0]
    rows = min(GLA_BLOCK, rows_total)
    n_blocks = rows_total // rows
    n_chunks = rows // GLA_CHUNK
    specs, tix = _gla_specs(rows, n_blocks, reverse)

    def body(q_ref, k_ref, v_ref, lr_ref, wg_ref, bg_ref, *rest):
        o_ref, st_ref, state, kv_scr, dec_scr = rest[-5:]
        _, q_in, k_in, k_st, dec, _ = _gla_prepare(q_ref, k_ref, lr_ref, wg_ref, bg_ref, reverse, n_chunks)
        vb = v_ref[...].reshape(n_chunks, GLA_CHUNK, GLA_DV).astype(BF16)
        qb = q_in.astype(BF16)
        p = jnp.where(_gla_mask(reverse), _bdot(qb, k_in.astype(BF16), 2, 2), 0.0)
        o = _bdot(p.astype(BF16), vb, 2, 1)
        kv_scr[...] = _bdot(vb, k_st.astype(BF16), 1, 1)
        dec_scr[...] = jnp.broadcast_to(dec, dec_scr.shape)

        @pl.when(pl.program_id(1) == 0)
        def _():
            state[...] = jnp.zeros_like(state)

        for c in range(n_chunks):
            cc = n_chunks - 1 - c if reverse else c
            st_ref[0, cc] = state[...]
            state[...] = state[...] * dec_scr[cc, 0:1] + kv_scr[cc]
        o = o + _bdot(qb, st_ref[0].astype(BF16), 2, 2)
        o = o.reshape(rows, GLA_DV)
        o_ref[...] = o if add_o is None else o + rest[0][...]

    o_spec = pl.BlockSpec((rows, GLA_DV), lambda h, s: (tix(s), h))
    return pl.pallas_call(
        body, name=name,
        out_shape=(jax.ShapeDtypeStruct((rows_total, GLA_HEADS * GLA_DV), F32),
                   jax.ShapeDtypeStruct((GLA_HEADS, rows_total // GLA_CHUNK, GLA_DV, GLA_DK), F32)),
        grid=(GLA_HEADS, n_blocks),
        in_specs=specs + ([] if add_o is None else [o_spec]),
        out_specs=(o_spec,
                   pl.BlockSpec((1, n_chunks, GLA_DV, GLA_DK), lambda h, s: (h, tix(s), 0, 0))),
        scratch_shapes=[pltpu.VMEM((GLA_DV, GLA_DK), F32), pltpu.VMEM((n_chunks, GLA_DV, GLA_DK), F32),
                        pltpu.VMEM((n_chunks, SUBLANES, GLA_DK), F32)],
        compiler_params=_params("parallel", "arbitrary"),
    )(proj, proj, proj, proj, wg_pad, bg, *([] if add_o is None else [add_o]))


def gla_bwd(proj, wg_pad, bg, d_o, states, dqkv_in, reverse, name):
    rows_total = proj.shape[0]
    rows = min(GLA_BLOCK, rows_total)
    n_blocks = rows_total // rows
    n_chunks = rows // GLA_CHUNK
    specs, tix = _gla_specs(rows, n_blocks, not reverse)
    d = 1 if reverse else 0
    specs[4] = pl.BlockSpec((1, LANES, GLA_DK), lambda h, s: (d, 0, h))
    specs[5] = pl.BlockSpec((1, 1, GLA_DK), lambda h, s: (d, 0, h))
    add = dqkv_in is not None

    def body(*refs):
        q_ref, k_ref, v_ref, lr_ref, wg_ref, bg_ref, do_ref, st_ref = refs[:8]
        refs = refs[8:]
        if add:
            aq_ref, ak_ref, av_ref = refs[:3]
            refs = refs[3:]
        dq_ref, dk_ref, dv_ref, dz_ref, dstate, g_scr, dec_scr, dsn_scr = refs
        z, q_in, k_in, k_st, dec, (f_q, f_k, f_s) = _gla_prepare(q_ref, k_ref, lr_ref, wg_ref, bg_ref, reverse,
                                                                 n_chunks)
        mask = _gla_mask(reverse)
        vb = v_ref[...].reshape(n_chunks, GLA_CHUNK, GLA_DV).astype(BF16)
        dob = do_ref[...].reshape(n_chunks, GLA_CHUNK, GLA_DV).astype(BF16)
        qb, kb, ksb = q_in.astype(BF16), k_in.astype(BF16), k_st.astype(BF16)
        st = st_ref[0]
        stb = st.astype(BF16)
        pb = jnp.where(mask, _bdot(qb, kb, 2, 2), 0.0).astype(BF16)
        dpb = jnp.where(mask, _bdot(dob, vb, 2, 2), 0.0).astype(BF16)
        d_qin = _bdot(dpb, kb, 2, 1) + _bdot(dob, stb, 2, 1)
        d_kin = _bdot(dpb, qb, 1, 1)
        dv = _bdot(pb, dob, 1, 1)
        g_scr[...] = _bdot(dob, qb, 1, 1)
        dec_scr[...] = jnp.broadcast_to(dec, dec_scr.shape)

        @pl.when(pl.program_id(1) == 0)
        def _():
            dstate[...] = jnp.zeros_like(dstate)

        for c in range(n_chunks):
            cc = c if reverse else n_chunks - 1 - c
            dsn_scr[cc] = dstate[...]
            dstate[...] = dstate[...] * dec_scr[cc, 0:1] + g_scr[cc]
        dsn = dsn_scr[...]
        dsnb = dsn.astype(BF16)
        dv = dv + _bdot(ksb, dsnb, 2, 2)
        d_kst = _bdot(vb, dsnb, 2, 1)
        d_dec = jnp.sum(dsn * st, axis=1, keepdims=True)
        ks_term = d_kst * k_st
        d_btot = d_dec * dec + jnp.sum(ks_term, axis=1, keepdims=True)
        d_b = d_qin * q_in - d_kin * k_in - ks_term
        pos = lax.broadcasted_iota(jnp.int32, d_b.shape, 1)
        edge = 0 if reverse else GLA_CHUNK - 1
        d_b = d_b + jnp.where(pos == edge, d_btot, 0.0)
        dg = _chunk_cumsum(d_b.reshape(rows, GLA_DK), not reverse)
        dz_ref[...] = dg * (1.0 / GLA_NORMALIZER) * _sigmoid(-z)
        dq = (d_qin * f_q).reshape(rows, GLA_DK)
        dk = (d_kin * f_k + d_kst * f_s).reshape(rows, GLA_DK)
        dv = dv.reshape(rows, GLA_DV)
        if add:
            dq_ref[...] = (dq + aq_ref[...]).astype(BF16)
            dk_ref[...] = (dk + ak_ref[...]).astype(BF16)
            dv_ref[...] = (dv + av_ref[...]).astype(BF16)
        else:
            dq_ref[...] = dq
            dk_ref[...] = dk
            dv_ref[...] = dv

    qkv_specs = [pl.BlockSpec((rows, GLA_DK), lambda h, s: (tix(s), h)),
                 pl.BlockSpec((rows, GLA_DK), lambda h, s: (tix(s), h)),
                 pl.BlockSpec((rows, GLA_DV), lambda h, s: (tix(s), h))]
    in_specs = specs + [pl.BlockSpec((rows, GLA_DV), lambda h, s: (tix(s), h)),
                        pl.BlockSpec((1, n_chunks, GLA_DV, GLA_DK), lambda h, s: (h, tix(s), 0, 0))]
    args = [proj, proj, proj, proj, wg_pad, bg, d_o, states]
    out_dtype = F32
    if add:
        in_specs += qkv_specs
        args += list(dqkv_in)
        out_dtype = BF16
    return pl.pallas_call(
        body, name=name,
        out_shape=(jax.ShapeDtypeStruct((rows_total, GLA_HEADS * GLA_DK), out_dtype),
                   jax.ShapeDtypeStruct((rows_total, GLA_HEADS * GLA_DK), out_dtype),
                   jax.ShapeDtypeStruct((rows_total, GLA_HEADS * GLA_DV), out_dtype),
                   jax.ShapeDtypeStruct((rows_total, GLA_HEADS * GLA_DK), F32)),
        grid=(GLA_HEADS, n_blocks),
        in_specs=in_specs,
        out_specs=(pl.BlockSpec((rows, GLA_DK), lambda h, s: (tix(s), h)),
                   pl.BlockSpec((rows, GLA_DK), lambda h, s: (tix(s), h)),
                   pl.BlockSpec((rows, GLA_DV), lambda h, s: (tix(s), h)),
                   pl.BlockSpec((rows, GLA_DK), lambda h, s: (tix(s), h))),
        scratch_shapes=[pltpu.VMEM((GLA_DV, GLA_DK), F32), pltpu.VMEM((n_chunks, GLA_DV, GLA_DK), F32),
                        pltpu.VMEM((n_chunks, SUBLANES, GLA_DK), F32),
                        pltpu.VMEM((n_chunks, GLA_DV, GLA_DK), F32)],
        compiler_params=_params("parallel", "arbitrary"),
    )(*args)


def gla_gate_bwd(proj, dz_f, dz_b, wg_pad, name):
    rows_total = proj.shape[0]
    tm = min(ROW_TILE, rows_total)
    n_key = GLA_HEADS * GLA_DK

    def body(lr_ref, dzf_ref, dzb_ref, wg_ref, dlr_ref, dwg_ref, dbg_ref):
        step = pl.program_id(0)
        lr_t = jnp.transpose(lr_ref[...])
        dzf, dzb = dzf_ref[...], dzb_ref[...]
        dzf16, dzb16 = dzf.astype(BF16), dzb.astype(BF16)
        dlr_ref[...] = (_dot_nt(dzf16, wg_ref[0]) + _dot_nt(dzb16, wg_ref[1])).astype(BF16)
        dwf = _dot(lr_t[0:GLA_RANK].astype(BF16), dzf16)
        dwb = _dot(lr_t[GLA_RANK:2 * GLA_RANK].astype(BF16), dzb16)
        dbg = jnp.concatenate([_colsum(dzf), _colsum(dzb)], axis=0)

        @pl.when(step == 0)
        def _():
            dwg_ref[0] = dwf
            dwg_ref[1] = dwb
            dbg_ref[...] = dbg

        @pl.when(step > 0)
        def _():
            dwg_ref[0] += dwf
            dwg_ref[1] += dwb
            dbg_ref[...] += dbg

    return pl.pallas_call(
        body, name=name,
        out_shape=(jax.ShapeDtypeStruct((rows_total, LANES), BF16), jax.ShapeDtypeStruct((2, GLA_RANK, n_key), F32),
                   jax.ShapeDtypeStruct((2, n_key), F32)),
        grid=(rows_total // tm,),
        in_specs=[pl.BlockSpec((tm, LANES), lambda i: (i, LR_COL // LANES)),
                  pl.BlockSpec((tm, n_key), lambda i: (i, 0)), pl.BlockSpec((tm, n_key), lambda i: (i, 0)),
                  _full((2, LANES, n_key))],
        out_specs=(pl.BlockSpec((tm, LANES), lambda i: (i, 0)), _full((2, GLA_RANK, n_key)), _full((2, n_key))),
        compiler_params=_params("arbitrary"),
    )(proj, dz_f, dz_b, wg_pad)


def _head_norm(o, gain):
    outs, hats, rstds = [], [], []
    for h in range(GLA_HEADS):
        oh = o[:, h * GLA_DV:(h + 1) * GLA_DV]
        rstd = lax.rsqrt(jnp.mean(oh * oh, axis=-1, keepdims=True) + NORM_EPS)
        hat = oh * rstd
        outs.append(hat * gain)
        hats.append(hat)
        rstds.append(rstd)
    return outs, hats, rstds


def odd_out_fwd(o, proj, head_gain, w_out, gain, x1, target, name):
    rows, d = x1.shape
    tm = min(ROW_TILE, rows)
    r_block = (2 * GLA_HEADS * GLA_DK + GLA_HEADS * GLA_DV) // d

    def body(o_ref, r_ref, hg_ref, w_ref, g_ref, x1_ref, tgt_ref, y2_ref, dy_ref, dx2_ref, loss_ref, dg_ref):
        step = pl.program_id(0)
        on, _, _ = _head_norm(o_ref[...], hg_ref[...])
        r = r_ref[...]
        y2 = (jnp.concatenate(on, axis=1) * (r * _sigmoid(r))).astype(BF16)
        y2_ref[...] = y2
        y = _dot(y2, w_ref[...])
        gain_v = g_ref[...]
        rstd = lax.rsqrt(jnp.mean(y * y, axis=-1, keepdims=True) + NORM_EPS)
        x2 = x1_ref[...] + y * rstd * gain_v
        diff = x2 - tgt_ref[...]
        loss = 0.5 * jnp.sum(jnp.mean(diff * diff, axis=-1, keepdims=True), axis=0, keepdims=True)
        dx2 = diff * (1.0 / d)
        dx2_ref[...] = dx2
        dy, dg_rows = _rmsnorm_bwd(dx2, y, gain_v)
        dy_ref[...] = dy.astype(BF16)
        _accumulate(loss_ref, jnp.broadcast_to(loss, loss_ref.shape), step)
        _accumulate(dg_ref, _colsum(dg_rows), step)

    row = lambda n, col=0: pl.BlockSpec((tm, n), lambda i: (i, col))
    return pl.pallas_call(
        body, name=name,
        out_shape=(jax.ShapeDtypeStruct((rows, d), BF16), jax.ShapeDtypeStruct((rows, d), BF16),
                   jax.ShapeDtypeStruct((rows, d), F32), jax.ShapeDtypeStruct((SUBLANES, LANES), F32),
                   jax.ShapeDtypeStruct((1, d), F32)),
        grid=(rows // tm,),
        in_specs=[row(d), row(d, r_block), _full((1, GLA_DV)), _full((d, d)), _full((1, d)), row(d), row(d)],
        out_specs=(row(d), row(d), row(d), _full((SUBLANES, LANES)), _full((1, d))),
        compiler_params=_params("arbitrary"),
    )(o, proj, head_gain, w_out, gain, x1, target)


def odd_out_bwd(dy, w_out, o, proj, head_gain, name):
    rows, d = dy.shape
    tm = min(ROW_TILE, rows)
    r_block = (2 * GLA_HEADS * GLA_DK + GLA_HEADS * GLA_DV) // d

    def body(dy_ref, w_ref, o_ref, r_ref, hg_ref, dr_ref, do_ref, dhg_ref):
        dy2 = _dot_nt(dy_ref[...], w_ref[...])
        hg = hg_ref[...]
        on, hats, rstds = _head_norm(o_ref[...], hg)
        r = r_ref[...]
        sr = _sigmoid(r)
        dr_ref[...] = (dy2 * jnp.concatenate(on, axis=1) * (sr * (1.0 + r * (1.0 - sr)))).astype(BF16)
        d_on = dy2 * (r * sr)
        d_os, dhg = [], None
        for h in range(GLA_HEADS):
            dn = d_on[:, h * GLA_DV:(h + 1) * GLA_DV]
            part = _colsum(dn * hats[h])
            dhg = part if dhg is None else dhg + part
            dng = dn * hg
            d_os.append(rstds[h] * (dng - hats[h] * jnp.mean(dng * hats[h], axis=-1, keepdims=True)))
        do_ref[...] = jnp.concatenate(d_os, axis=1)
        _accumulate(dhg_ref, dhg, pl.program_id(0))

    row = lambda n, col=0: pl.BlockSpec((tm, n), lambda i: (i, col))
    return pl.pallas_call(
        body, name=name,
        out_shape=(jax.ShapeDtypeStruct((rows, d), BF16), jax.ShapeDtypeStruct((rows, d), F32),
                   jax.ShapeDtypeStruct((1, GLA_DV), F32)),
        grid=(rows // tm,),
        in_specs=[row(d), _full((d, d)), row(d), row(d, r_block), _full((1, GLA_DV))],
        out_specs=(row(d), row(d), _full((1, GLA_DV))),
        compiler_params=_params("arbitrary"),
    )(dy, w_out, o, proj, head_gain)


def local_step(x, target, w, reduce_first=None, reduce_second=None, late_weights=None):
    g, g16 = {}, {}
    proj_e, h0 = norm_matmul(x, w["even_norm_pre"], w["even_w_in"], BF16, "even_in_proj")
    h_dir, acts = zip(*[rglru_fwd(proj_e, w["rg_conv_w"], w["rg_conv_b"], w["rg_gate_w"][d], w["rg_gate_b"][d],
                                  w["rg_lambda"][d], d == 1, "rglru_fwd_%d" % d) for d in range(2)])
    ycat = even_mix_fwd(proj_e, h_dir[0], h_dir[1], w["sc_conv_w"], "even_mix_fwd")
    if late_weights is not None:
        w = dict(w, **late_weights(ycat))
    x1, y_e = even_out_fwd(ycat, w["even_w_out"], w["even_norm_post"], x, "even_out_fwd")
    proj_o, h1 = norm_matmul(x1, w["odd_norm_pre"], w["odd_w_in"], F32, "odd_in_proj")
    o, st_dir = None, []
    for d in range(2):
        o, st = gla_fwd(proj_o, w["gla_wg_pad"], w["gla_b_gate"], o, d == 1, "gla_fwd_%d" % d)
        st_dir.append(st)
    y2, dy_o, dx2, loss, g["odd_norm_post"] = odd_out_fwd(
        o, proj_o, w["gla_norm_g"], w["odd_w_out"], w["odd_norm_post"], x1, target, "odd_out_fwd")
    g["odd_w_out"], g16["odd_w_out"] = (a[0] for a in matmul_dw(y2, dy_o, D_MODEL, "odd_w_out_grad"))
    dr, d_o, g["gla_norm_g"] = odd_out_bwd(dy_o, w["odd_w_out"], o, proj_o, w["gla_norm_g"], "odd_out_bwd")
    dq, dk, dv, dz_f = gla_bwd(proj_o, w["gla_wg_pad"], w["gla_b_gate"], d_o, st_dir[0], None, False, "gla_bwd_0")
    dq, dk, dv, dz_b = gla_bwd(proj_o, w["gla_wg_pad"], w["gla_b_gate"], d_o, st_dir[1], (dq, dk, dv), True,
                               "gla_bwd_1")
    dlr, g["gla_w_gate_lr"], g["gla_b_gate"] = gla_gate_bwd(proj_o, dz_f, dz_b, w["gla_wg_pad"], "gla_gate_bwd")
    dproj_o = [dq, dk, dv, dr, dlr]
    g["odd_w_in"], g16["odd_w_in"] = (a[:, :ODD_IN] for a in matmul_dw_pieces(h1, dproj_o, "odd_w_in_grad"))
    dx1, g["odd_norm_pre"] = inproj_bwd_pieces(dproj_o, w["odd_w_in"], x1, w["odd_norm_pre"], dx2, "odd_in_proj_bwd")
    dy_e, dycat, g["even_norm_post"] = even_out_bwd(dx1, y_e, w["even_norm_post"], w["even_w_out"], "even_out_bwd")
    g["even_w_out"], g16["even_w_out"] = (a[0] for a in matmul_dw(ycat, dy_e, D_MODEL, "even_w_out_grad"))
    lam = w["rg_lambda"] if reduce_first is None else w["rg_lambda"] + reduce_first(g, g16)
    dua, dgw, dgb, dlam = None, [], [], []
    for d in range(2):
        a, b, c, e = rglru_bwd(proj_e, dycat, h_dir[d], acts[d], w["rg_gate_w"][d], lam[d], dua, d == 1,
                               "rglru_bwd_%d" % d)
        dua = a
        dgw.append(b)
        dgb.append(c)
        dlam.append(e)
    dproj_e, g["rg_conv_w"], g["rg_conv_b"], g["sc_conv_w"] = even_mix_bwd(
        proj_e, dycat, h_dir[0], h_dir[1], dua, w["rg_conv_w"], w["sc_conv_w"], "even_mix_bwd")
    dgw = jnp.stack(dgw).reshape(2, RG_HEADS, RG_HEAD_DIM, 2, RG_HEAD_DIM)
    g["rg_gate_w"] = jnp.transpose(dgw, (0, 3, 1, 2, 4))
    g["rg_gate_b"] = jnp.stack(dgb).reshape(2, 2, RG_HEADS, RG_HEAD_DIM)
    g["rg_lambda"] = jnp.concatenate(dlam, axis=0)
    g["even_w_in"], g16["even_w_in"] = matmul_dw(h0, dproj_e, EVEN_IN // 4, "even_w_in_grad")
    gain = w["even_norm_pre"] if reduce_second is None else w["even_norm_pre"] + reduce_second(g, g16)
    grad_x, g["even_norm_pre"] = inproj_bwd(dproj_e, w["even_w_in"], x, gain, dx1, "even_in_proj_bwd")
    return loss, grad_x, g


def _prepare_weights(full):
    w = {}
    for name in ("even_norm_pre", "even_norm_post", "rg_conv_b", "odd_norm_pre", "odd_norm_post", "gla_norm_g"):
        if name in full:
            w[name] = full[name].reshape(1, -1)
    for name in ("rg_conv_w", "sc_conv_w"):
        if name in full:
            w[name] = full[name]
    for name in ("even_w_out", "odd_w_out"):
        if name in full:
            w[name] = full[name].astype(BF16)
    if "even_w_in" in full:
        w["even_w_in"] = full["even_w_in"].astype(BF16)
        if w["even_w_in"].ndim == 2:
            w["even_w_in"] = jnp.transpose(w["even_w_in"].reshape(D_MODEL, 4, EVEN_IN // 4), (1, 0, 2))
    if "rg_gate_w" in full:
        gw = jnp.transpose(full["rg_gate_w"].astype(BF16), (0, 2, 3, 1, 4))
        w["rg_gate_w"] = gw.reshape(2, RG_HEADS, RG_HEAD_DIM, 2 * RG_HEAD_DIM)
        w["rg_gate_b"] = full["rg_gate_b"].reshape(2, 2, D_MODEL)
        w["rg_lambda"] = full["rg_lambda"].reshape(2, 1, D_MODEL)
    if "odd_w_in" in full:
        w_in = jnp.pad(full["odd_w_in"].astype(BF16), ((0, 0), (0, ODD_IN_PAD - ODD_IN)))
        w["odd_w_in"] = w_in.reshape(1, D_MODEL, ODD_IN_PAD)
    if "gla_w_gate_lr" in full:
        wg = full["gla_w_gate_lr"].astype(BF16)
        w["gla_wg_pad"] = jnp.stack([jnp.pad(wg[d], ((d * GLA_RANK, LANES - (d + 1) * GLA_RANK), (0, 0)))
                                     for d in range(2)])
        w["gla_b_gate"] = full["gla_b_gate"].reshape(2, 1, GLA_HEADS * GLA_DK)
    return w


SHARDED_SMALL = (("rg_conv_w", (4, 256)), ("rg_lambda", (2, 256)), ("sc_conv_w", (3, 256)),
                 ("odd_norm_pre", (256,)), ("odd_norm_post", (256,)), ("gla_w_gate_lr", (2, 16, 128)),
                 ("gla_b_gate", (2, 128)), ("gla_norm_g", (64,)))
SHARDED_ROWS = 96
REPLICATED = (("rg_gate_w", (2, 2, 8, 128, 128)), ("even_norm_post", (1024,)), ("rg_conv_b", (1024,)),
              ("rg_gate_b", (2, 2, 8, 128)))
GATE_ROWS = 4096
LAST_REPLICATED = (("even_norm_pre", (1024,)),)
LAST_ROWS = 8
REPLICATED_ROWS = 4160
REP_PART = REPLICATED_ROWS // 8
HALF_SHARDED = SHARDED_ROWS // 2
PACK_HALF = HALF_SHARDED + REP_PART


def _seg_rows(shape):
    n = 1
    for s in shape:
        n *= s
    return -(-n // (SUBLANES * LANES)) * SUBLANES


def _pack(arrays, spec, total_rows, lead=()):
    parts = []
    for name, shape in spec:
        flat = arrays[name].reshape(lead + (-1,))
        pad = _seg_rows(shape) * LANES - flat.shape[-1]
        if pad:
            flat = jnp.pad(flat, [(0, 0)] * len(lead) + [(0, pad)])
        parts.append(flat.reshape(lead + (-1, LANES)))
    rows = jnp.concatenate(parts, axis=len(lead))
    pad = total_rows - rows.shape[len(lead)]
    return jnp.pad(rows, [(0, 0)] * len(lead) + [(0, pad), (0, 0)])


def _unpack(rows, spec, lead=()):
    out, at = {}, 0
    for name, shape in spec:
        n = 1
        for s in shape:
            n *= s
        k = _seg_rows(shape)
        seg = lax.slice_in_dim(rows, at, at + k, axis=len(lead)).reshape(lead + (-1,))
        out[name] = lax.slice_in_dim(seg, 0, n, axis=len(lead)).reshape(lead + shape)
        at += k
    return out


def _split_owners(arr):
    a = arr.reshape(arr.shape[:-1] + (4, arr.shape[-1] // 4))
    return jnp.moveaxis(a, -2, 0)


def _merge_owners(arr):
    a = jnp.moveaxis(arr, 0, -2)
    return a.reshape(a.shape[:-2] + (-1,))


HBM_SPEC = pl.BlockSpec(memory_space=pltpu.HBM)


def _position():
    x, y, c = lax.axis_index("x"), lax.axis_index("y"), lax.axis_index("c")
    chips = [(1 - x, y), (x, 1 - y), (1 - x, 1 - y)]
    return x, y, c, chips


def _remote(src, dst, send_sem, recv_sem, device):
    return pltpu.make_async_remote_copy(src_ref=src, dst_ref=dst, send_sem=send_sem, recv_sem=recv_sem,
                                        device_id=device, device_id_type=MESH)


SEM_SPEC = pl.BlockSpec(memory_space=pltpu.SEMAPHORE)
SIDE_EFFECT = pltpu.SideEffectType.DATAFLOW_SIDE_EFFECTING


def _gather_copies(ins, lands, n_h, send_sems, recv_sems):
    x, y, c, chips = _position()
    me = 2 * x + y
    copies = []
    for a in range(len(ins)):
        for k, chip in enumerate(chips):
            src = ins[a].at[c] if a < n_h else ins[a]
            dst = lands[a].at[me, c] if a < n_h else lands[a].at[me]
            copies.append(_remote(src, dst, send_sems.at[3 * a + k], recv_sems.at[3 * a + k], (chip[0], chip[1], c)))
    return copies


def gather_start(halved, whole, name):
    arrays = list(halved) + list(whole)
    n, n_h = len(arrays), len(halved)
    lands = [lax.empty((4,) + a.shape, a.dtype) for a in arrays]

    def body(*refs):
        ins, lz, send_sems, recv_sems, token = refs[:n], refs[n:2 * n], refs[2 * n], refs[2 * n + 1], refs[-1]
        for cp in _gather_copies(ins, lz, n_h, send_sems, recv_sems):
            cp.start()
        token[...] = jnp.zeros_like(token)

    operands = [pltpu.with_memory_space_constraint(a, pltpu.HBM) for a in arrays + lands]
    return pl.pallas_call(
        body, name=name,
        out_shape=(pltpu.SemaphoreType.DMA((3 * n,)), pltpu.SemaphoreType.DMA((3 * n,)))
        + tuple(pltpu.HBM(a.shape, a.dtype) for a in operands) + (jax.ShapeDtypeStruct((SUBLANES, LANES), F32),),
        in_specs=[HBM_SPEC] * (2 * n),
        out_specs=(SEM_SPEC, SEM_SPEC) + (HBM_SPEC,) * (2 * n) + (pl.BlockSpec(memory_space=pltpu.VMEM),),
        input_output_aliases={i: 2 + i for i in range(2 * n)},
        compiler_params=pltpu.CompilerParams(has_side_effects=SIDE_EFFECT),
    )(*operands)


def gather_wait(started, n_h, after, name):
    send_sems, recv_sems = started[0], started[1]
    operands = list(started[2:-1])
    n = len(operands) // 2

    def body(*refs):
        ins, lz, send_ref, recv_ref = refs[:n], refs[n:2 * n], refs[2 * n], refs[2 * n + 1]
        for cp in _gather_copies(ins, lz, n_h, send_ref, recv_ref):
            cp.wait_send()
            cp.wait_recv()

    outs = pl.pallas_call(
        body, name=name,
        out_shape=tuple(pltpu.HBM(a.shape, a.dtype) for a in operands),
        in_specs=[HBM_SPEC] * (2 * n) + [SEM_SPEC, SEM_SPEC, pl.BlockSpec(memory_space=pl.ANY)],
        out_specs=(HBM_SPEC,) * (2 * n),
        input_output_aliases={i: i for i in range(2 * n)},
        compiler_params=pltpu.CompilerParams(has_side_effects=SIDE_EFFECT),
    )(*operands, send_sems, recv_sems, after)
    return outs[n:]


def pass_to_sibling(fulls, name):
    n = len(fulls)

    def body(*refs):
        bufs = refs[n:2 * n]
        send_sems, recv_sems = refs[2 * n:]
        x, y, c, chips = _position()
        sibling = (x, y, 1 - c)
        copies = []
        for a in range(n):
            for k, chip in enumerate(chips):
                q = 2 * chip[0] + chip[1]
                cp = _remote(bufs[a].at[q, c], bufs[a].at[q, c], send_sems.at[3 * a + k], recv_sems.at[3 * a + k],
                             sibling)
                cp.start()
                copies.append(cp)
        for a in range(n):
            for k, chip in enumerate(chips):
                q = 2 * chip[0] + chip[1]
                passed = bufs[a].at[q, 1 - c]
                _remote(passed, passed, send_sems.at[3 * a + k], recv_sems.at[3 * a + k], sibling).wait_recv()
        for cp in copies:
            cp.wait_send()

    return pl.pallas_call(
        body, name=name,
        out_shape=[jax.ShapeDtypeStruct(a.shape, a.dtype) for a in fulls],
        in_specs=[HBM_SPEC] * n, out_specs=[HBM_SPEC] * n,
        input_output_aliases={i: i for i in range(n)},
        scratch_shapes=[pltpu.SemaphoreType.DMA((3 * n,)), pltpu.SemaphoreType.DMA((3 * n,))],
    )(*fulls)


def place_own(full, own, chip, name):
    _, _, r, cols = full.shape
    tr = _row_tile(r, cols)

    def body(p_ref, own_ref, full_ref, o_ref):
        o_ref[0] = own_ref[...]

    return pl.pallas_call(
        body, name=name,
        out_shape=jax.ShapeDtypeStruct(full.shape, full.dtype),
        grid_spec=pltpu.PrefetchScalarGridSpec(
            num_scalar_prefetch=1, grid=(2, r // tr),
            in_specs=[pl.BlockSpec((1, tr, cols), lambda h, i, p_ref: (h, i, 0)), pl.BlockSpec(memory_space=pl.ANY)],
            out_specs=pl.BlockSpec((1, 1, tr, cols), lambda h, i, p_ref: (p_ref[0], h, i, 0))),
        input_output_aliases={2: 0},
        compiler_params=_params("parallel", "parallel"),
    )(chip, own, full)


def exchange_with_sibling(arrays, name):
    n = len(arrays)

    def body(*refs):
        ins, outs = refs[:n], refs[n:2 * n]
        send_sems, recv_sems = refs[2 * n:]
        x, y, c, _ = _position()
        copies = []
        for a in range(n):
            cp = _remote(ins[a].at[:, 1 - c], outs[a], send_sems.at[a], recv_sems.at[a], (x, y, 1 - c))
            cp.start()
            copies.append(cp)
        for cp in copies:
            cp.wait()

    return pl.pallas_call(
        body, name=name,
        out_shape=[jax.ShapeDtypeStruct((a.shape[0],) + a.shape[2:], a.dtype) for a in arrays],
        in_specs=[HBM_SPEC] * n, out_specs=[HBM_SPEC] * n,
        scratch_shapes=[pltpu.SemaphoreType.DMA((n,)), pltpu.SemaphoreType.DMA((n,))],
    )(*arrays)


def _chip_copies(ins, lands, send_sems, recv_sems):
    x, y, c, chips = _position()
    copies = []
    for a in range(len(ins)):
        for k, chip in enumerate(chips):
            q = 2 * chip[0] + chip[1]
            copies.append(_remote(ins[a].at[q], lands[a].at[k], send_sems.at[3 * a + k], recv_sems.at[3 * a + k],
                                  (chip[0], chip[1], c)))
    return copies


def exchange_with_chips_start(arrays, name):
    n = len(arrays)
    lands = [lax.empty((3,) + a.shape[1:], a.dtype) for a in arrays]

    def body(*refs):
        ins, lz, send_sems, recv_sems, token = refs[:n], refs[n:2 * n], refs[2 * n], refs[2 * n + 1], refs[-1]
        for cp in _chip_copies(ins, lz, send_sems, recv_sems):
            cp.start()
        token[...] = jnp.zeros_like(token)

    operands = [pltpu.with_memory_space_constraint(a, pltpu.HBM) for a in list(arrays) + lands]
    return pl.pallas_call(
        body, name=name,
        out_shape=(pltpu.SemaphoreType.DMA((3 * n,)), pltpu.SemaphoreType.DMA((3 * n,)))
        + tuple(pltpu.HBM(a.shape, a.dtype) for a in operands) + (jax.ShapeDtypeStruct((SUBLANES, LANES), F32),),
        in_specs=[HBM_SPEC] * (2 * n),
        out_specs=(SEM_SPEC, SEM_SPEC) + (HBM_SPEC,) * (2 * n) + (pl.BlockSpec(memory_space=pltpu.VMEM),),
        input_output_aliases={i: 2 + i for i in range(2 * n)},
        compiler_params=pltpu.CompilerParams(has_side_effects=SIDE_EFFECT),
    )(*operands)


def exchange_with_chips_wait(started, after, name):
    send_sems, recv_sems = started[0], started[1]
    operands = list(started[2:-1])
    n = len(operands) // 2

    def body(*refs):
        ins, lz, send_ref, recv_ref = refs[:n], refs[n:2 * n], refs[2 * n], refs[2 * n + 1]
        for cp in _chip_copies(ins, lz, send_ref, recv_ref):
            cp.wait_send()
            cp.wait_recv()

    outs = pl.pallas_call(
        body, name=name,
        out_shape=tuple(pltpu.HBM(a.shape, a.dtype) for a in operands),
        in_specs=[HBM_SPEC] * (2 * n) + [SEM_SPEC, SEM_SPEC, pl.BlockSpec(memory_space=pl.ANY)],
        out_specs=(HBM_SPEC,) * (2 * n),
        input_output_aliases={i: i for i in range(2 * n)},
        compiler_params=pltpu.CompilerParams(has_side_effects=SIDE_EFFECT),
    )(*operands, send_sems, recv_sems, after)
    return outs[:n], outs[n:]


def share_totals(totals, pack_total, last_part):
    arrays = list(totals) + [pack_total]
    n = len(arrays)

    def body(*refs):
        ins, last, outs, rep, last_all = refs[:n], refs[n], refs[n + 1:2 * n + 1], refs[2 * n + 1], refs[2 * n + 2]
        send_sems, recv_sems, rep_send, rep_recv, last_send, last_recv = refs[2 * n + 3:]
        x, y, c, chips = _position()
        sibling = (x, y, 1 - c)
        me = 4 * x + 2 * y + c
        sends = []
        for a in range(n):
            cp = _remote(ins[a], outs[a], send_sems.at[a], recv_sems.at[a], sibling)
            cp.start()
            sends.append(cp)
        mine = ins[n - 1].at[pl.ds(HALF_SHARDED, REP_PART)]
        peers = [sibling]
        for chip in chips:
            peers += [(chip[0], chip[1], c), (chip[0], chip[1], 1 - c)]
        for j, peer in enumerate(peers):
            for src, dst, s_sem, r_sem in ((mine, rep, rep_send, rep_recv), (last, last_all, last_send, last_recv)):
                cp = _remote(src, dst.at[me], s_sem.at[j], r_sem.at[j], peer)
                cp.start()
                sends.append(cp)
        for a in range(n):
            _remote(outs[a], outs[a], send_sems.at[a], recv_sems.at[a], sibling).wait_recv()
        for j, peer in enumerate(peers):
            it = 4 * peer[0] + 2 * peer[1] + peer[2]
            _remote(rep.at[it], rep.at[it], rep_send.at[j], rep_recv.at[j], peer).wait_recv()
            _remote(last_all.at[it], last_all.at[it], last_send.at[j], last_recv.at[j], peer).wait_recv()
        for cp in sends:
            cp.wait_send()

    outs = pl.pallas_call(
        body, name="grad_share_totals",
        out_shape=[jax.ShapeDtypeStruct(a.shape, a.dtype) for a in arrays]
        + [jax.ShapeDtypeStruct((8, REP_PART, LANES), F32), jax.ShapeDtypeStruct((8,) + last_part.shape, F32)],
        in_specs=[HBM_SPEC] * (n + 1), out_specs=[HBM_SPEC] * (n + 2),
        scratch_shapes=[pltpu.SemaphoreType.DMA((n,)), pltpu.SemaphoreType.DMA((n,))]
        + [pltpu.SemaphoreType.DMA((7,))] * 4,
    )(*arrays, last_part)
    return outs[:n], outs[n], outs[n + 1]


def sum_parts(parts, name):
    def body(p_ref, o_ref):
        total = p_ref[0]
        for k in range(1, parts.shape[0]):
            total = total + p_ref[k]
        o_ref[...] = total

    return pl.pallas_call(body, name=name, out_shape=jax.ShapeDtypeStruct(parts.shape[1:], parts.dtype))(parts)


TILE_BYTES = 2 << 20


def _row_tile(rows, cols):
    best = None
    for t in range(SUBLANES, rows + 1, SUBLANES):
        if rows % t == 0 and t * cols * 4 <= TILE_BYTES:
            best = t
    return best if best is not None else rows


def add_sibling(mine, received, core, out_dtype, name):
    _, _, r, cols = mine.shape
    tr = _row_tile(r, cols)

    def body(c_ref, a_ref, b_ref, o_ref):
        o_ref[...] = (a_ref[0] + b_ref[...].astype(F32)).astype(out_dtype)

    return pl.pallas_call(
        body, name=name,
        out_shape=jax.ShapeDtypeStruct((4, r, cols), out_dtype),
        grid_spec=pltpu.PrefetchScalarGridSpec(
            num_scalar_prefetch=1, grid=(4, r // tr),
            in_specs=[pl.BlockSpec((1, 1, tr, cols), lambda o, i, c_ref: (o, c_ref[0], i, 0)),
                      pl.BlockSpec((1, tr, cols), lambda o, i, c_ref: (o, i, 0))],
            out_specs=pl.BlockSpec((1, tr, cols), lambda o, i, c_ref: (o, i, 0))),
        compiler_params=_params("parallel", "parallel"),
    )(core, mine, received)


def add_chips(own, received, chip, name):
    _, r, cols = own.shape
    tr = _row_tile(r, cols)

    def body(p_ref, a_ref, b0, b1, b2, o_ref):
        o_ref[...] = ((a_ref[0].astype(F32) + b0[0].astype(F32)) + b1[0].astype(F32)) + b2[0].astype(F32)

    rb = lambda k: pl.BlockSpec((1, tr, cols), lambda i, p_ref: (k, i, 0))
    return pl.pallas_call(
        body, name=name,
        out_shape=jax.ShapeDtypeStruct((r, cols), F32),
        grid_spec=pltpu.PrefetchScalarGridSpec(
            num_scalar_prefetch=1, grid=(r // tr,),
            in_specs=[pl.BlockSpec((1, tr, cols), lambda i, p_ref: (p_ref[0], i, 0)), rb(0), rb(1), rb(2)],
            out_specs=pl.BlockSpec((tr, cols), lambda i, p_ref: (i, 0))),
        compiler_params=_params("parallel"),
    )(chip, own, received, received, received)


def _adamw_update(gv, w_ref, m_ref, v_ref, d_ref, nm_ref, nv_ref):
    nm = ADAM_B1 * m_ref[...] + (1.0 - ADAM_B1) * gv
    nv = ADAM_B2 * v_ref[...] + (1.0 - ADAM_B2) * (gv * gv)
    nm_ref[...] = nm
    nv_ref[...] = nv
    m_hat = nm / (1.0 - ADAM_B1 ** ADAM_STEP)
    v_hat = nv / (1.0 - ADAM_B2 ** ADAM_STEP)
    d_ref[...] = -ADAM_LR * (m_hat / (jnp.sqrt(v_hat) + ADAM_EPS) + ADAM_WD * w_ref[...])


def adamw_halves(w, own, received, m, v, core, name, by_columns=False):
    rows, cols = w.shape

    def body(c_ref, w_ref, own_ref, rec_ref, m_ref, v_ref, g_ref, d_ref, nm_ref, nv_ref):
        gv = jnp.where(pl.program_id(0) == c_ref[0], own_ref[...], rec_ref[...])
        g_ref[...] = gv
        _adamw_update(gv, w_ref, m_ref, v_ref, d_ref, nm_ref, nv_ref)

    if by_columns:
        nr = 1
        whole = pl.BlockSpec((rows, cols // 2), lambda h, i, c_ref: (0, h))
        half = pl.BlockSpec((rows, cols // 2), lambda h, i, c_ref: (0, 0))
    else:
        r = rows // 2
        tr = _row_tile(r, cols)
        nr = r // tr
        whole = pl.BlockSpec((tr, cols), lambda h, i, c_ref: (h * nr + i, 0))
        half = pl.BlockSpec((tr, cols), lambda h, i, c_ref: (i, 0))
    return pl.pallas_call(
        body, name=name,
        out_shape=(jax.ShapeDtypeStruct((rows, cols), F32),) * 4,
        grid_spec=pltpu.PrefetchScalarGridSpec(
            num_scalar_prefetch=1, grid=(2, nr),
            in_specs=[whole, half, half, whole, whole], out_specs=(whole,) * 4),
        compiler_params=_params("parallel", "parallel"),
    )(core, w, own, received, m, v)


def adamw_many(ws, gs, ms, vs, name):
    n = len(ws)

    def body(*refs):
        ins, outs = refs[:4 * n], refs[4 * n:]
        for k in range(n):
            w_ref, g_ref, m_ref, v_ref = (ins[j * n + k] for j in range(4))
            d_ref, nm_ref, nv_ref = outs[3 * k:3 * k + 3]
            _adamw_update(g_ref[...], w_ref, m_ref, v_ref, d_ref, nm_ref, nv_ref)

    flat = pl.pallas_call(
        body, name=name,
        out_shape=[jax.ShapeDtypeStruct(w.shape, F32) for w in ws for _ in range(3)],
    )(*ws, *gs, *ms, *vs)
    return [tuple(flat[3 * k:3 * k + 3]) for k in range(n)]


def adamw(w, g, m, v, name):
    r, cols = w.shape
    tr = _row_tile(r, cols)

    def body(w_ref, g_ref, m_ref, v_ref, g_out, d_ref, nm_ref, nv_ref):
        gv = g_ref[...]
        g_out[...] = gv
        _adamw_update(gv, w_ref, m_ref, v_ref, d_ref, nm_ref, nv_ref)

    blk = pl.BlockSpec((tr, cols), lambda i: (i, 0))
    return pl.pallas_call(
        body, name=name,
        out_shape=(jax.ShapeDtypeStruct((r, cols), F32),) * 4,
        grid=(r // tr,),
        in_specs=[blk] * 4, out_specs=(blk,) * 4,
        compiler_params=_params("parallel"),
    )(w, g, m, v)


WEIGHTS = ("even_norm_pre", "even_norm_post", "even_w_in", "rg_conv_w", "rg_conv_b", "rg_gate_w", "rg_gate_b",
           "rg_lambda", "sc_conv_w", "even_w_out", "odd_norm_pre", "odd_norm_post", "odd_w_in", "gla_w_gate_lr",
           "gla_b_gate", "gla_norm_g", "odd_w_out")
BIG = ("even_w_in", "even_w_out", "odd_w_in", "odd_w_out")


def _halves(a):
    return a.reshape((2, a.shape[0] // 2) + a.shape[1:])


def kernel(x, even_norm_pre, even_norm_post, even_w_in, rg_conv_w, rg_conv_b, rg_gate_w, rg_gate_b, rg_lambda, sc_conv_w, even_w_out, odd_norm_pre, odd_norm_post, odd_w_in, gla_w_gate_lr, gla_b_gate, gla_norm_g, odd_w_out, loss_target, m_even_norm_pre, m_even_norm_post, m_even_w_in, m_rg_conv_w, m_rg_conv_b, m_rg_gate_w, m_rg_gate_b, m_rg_lambda, m_sc_conv_w, m_even_w_out, m_odd_norm_pre, m_odd_norm_post, m_odd_w_in, m_gla_w_gate_lr, m_gla_b_gate, m_gla_norm_g, m_odd_w_out, v_even_norm_pre, v_even_norm_post, v_even_w_in, v_rg_conv_w, v_rg_conv_b, v_rg_gate_w, v_rg_gate_b, v_rg_lambda, v_sc_conv_w, v_even_w_out, v_odd_norm_pre, v_odd_norm_post, v_odd_w_in, v_gla_w_gate_lr, v_gla_b_gate, v_gla_norm_g, v_odd_w_out):
    given = dict(locals())
    shard = {n: given[n][0] for n in WEIGHTS}
    m_in = {n: given["m_" + n][0] for n in WEIGHTS}
    v_in = {n: given["v_" + n][0] for n in WEIGHTS}
    mx, my, mc = lax.axis_index("x"), lax.axis_index("y"), lax.axis_index("c")
    core = jnp.reshape(mc, (1,)).astype(jnp.int32)
    chip = jnp.reshape(2 * mx + my, (1,)).astype(jnp.int32)

    small_shard = _pack(shard, SHARDED_SMALL, SHARDED_ROWS)
    big_own = [_halves(shard[n].astype(BF16)) for n in BIG]
    started_a = gather_start(big_own[:1], [small_shard], "gather_start_a")
    started_b = gather_start(big_own[1:], [], "gather_start_b")
    even_w_in_full, small_full = gather_wait(started_a, 1, started_b[-1], "gather_wait_a")
    (even_w_in_full,) = pass_to_sibling([even_w_in_full], "gather_pass_a")
    even_w_in_full = place_own(even_w_in_full, big_own[0], chip, "place_even_w_in")
    small_full = lax.dynamic_update_slice(small_full, small_shard[None], (chip[0], 0, 0))
    full = {n: shard[n] for n, _ in REPLICATED + LAST_REPLICATED}
    full.update({n: _merge_owners(a) for n, a in _unpack(small_full, SHARDED_SMALL, lead=(4,)).items()})
    full["even_w_in"] = even_w_in_full.reshape(4, D_MODEL, EVEN_IN // 4)

    def late_weights(after):
        lands = pass_to_sibling(list(gather_wait(started_b, 3, after, "gather_wait_b")), "gather_pass_b")
        lands = [place_own(a, b, chip, "place_" + n) for a, b, n in zip(lands, big_own[1:], BIG[1:])]
        odd_w_in = jnp.transpose(lands[1].reshape(4, D_MODEL, ODD_IN // 4), (1, 0, 2)).reshape(D_MODEL, ODD_IN)
        return _prepare_weights({"even_w_out": lands[0].reshape(2 * D_MODEL, D_MODEL), "odd_w_in": odd_w_in,
                                 "odd_w_out": lands[2].reshape(D_MODEL, D_MODEL)})

    pending = {}

    def slab(a):
        return a.reshape((4, 2, a.shape[1] // 2) + a.shape[2:])

    def begin(tag, slabs, to_send, dtypes):
        got = exchange_with_sibling(to_send, "grad_sibling_" + tag)
        sums = [add_sibling(a, b, core, dt, "grad_add_sibling_%s%d" % (tag, i))
                for i, (a, b, dt) in enumerate(zip(slabs, got, dtypes))]
        pending[tag] = exchange_with_chips_start(sums, "grad_chips_start_" + tag)
        return pending[tag][-1][0, 0]

    def finish(tag, after):
        sums, got = exchange_with_chips_wait(pending[tag], after, "grad_chips_wait_" + tag)
        return [add_chips(a, b, chip, "grad_add_chips_%s%d" % (tag, i)) for i, (a, b) in enumerate(zip(sums, got))]

    def reduce_first(g, g16):
        odd_w_in, odd_w_in16 = (slab(jnp.transpose(a["odd_w_in"].reshape(D_MODEL, 4, ODD_IN // 4), (1, 0, 2)))
                                for a in (g, g16))
        slabs = [odd_w_in] + [slab(g[n].reshape(4, -1, D_MODEL)) for n in ("odd_w_out", "even_w_out")]
        to_send = [odd_w_in16] + [slab(g16[n].reshape(4, -1, D_MODEL)) for n in ("odd_w_out", "even_w_out")]
        return begin("a", slabs, to_send, [BF16] * 3)

    def reduce_second(g, g16):
        pending["totals_a"] = finish("a", g["even_w_in"])
        rep_rows = _pack(g, REPLICATED, REPLICATED_ROWS).reshape(4, 2, REP_PART, LANES)
        sh_rows = _pack({n: _split_owners(g[n]) for n, _ in SHARDED_SMALL}, SHARDED_SMALL, SHARDED_ROWS, lead=(4,))
        pack = jnp.concatenate([sh_rows.reshape(4, 2, HALF_SHARDED, LANES), rep_rows], axis=2)
        return begin("b", [slab(g["even_w_in"]), pack], [slab(g16["even_w_in"]), pack], [BF16, F32])

    loss, grad_x, g = local_step(x[0], loss_target[0], _prepare_weights(full), reduce_first, reduce_second,
                                 late_weights)
    odd_w_in_t, odd_w_out_t, even_w_out_t = pending["totals_a"]
    even_w_in_t, pack_t = finish("b", grad_x)
    totals = [even_w_in_t, even_w_out_t, odd_w_in_t, odd_w_out_t]
    last_part = jnp.concatenate([_pack(g, LAST_REPLICATED, LAST_ROWS), loss])
    from_core, rep_all, last_all = share_totals(totals, pack_t, last_part)
    me = 2 * chip[0] + core[0]
    mine, theirs = pack_t[:HALF_SHARDED], from_core[4][:HALF_SHARDED]
    sh_total = jnp.where(mc == 0, jnp.concatenate([mine, theirs]), jnp.concatenate([theirs, mine]))
    rep_all = lax.dynamic_update_slice(rep_all, pack_t[None, HALF_SHARDED:], (me, 0, 0))
    rep_total = rep_all.reshape(REPLICATED_ROWS, LANES)
    last_total = sum_parts(lax.dynamic_update_slice(last_all, last_part[None], (me, 0, 0)), "grad_sum_last")
    last_total, loss = last_total[:LAST_ROWS], last_total[LAST_ROWS, 0]
    grads = {}

    delta, new_m, new_v = {}, {}, {}
    for i, n in enumerate(BIG):
        if shard[n].shape[1] % LANES:
            outs = adamw_halves(shard[n].T, totals[i].T, from_core[i].T, m_in[n].T, v_in[n].T, core, "adamw_" + n,
                                by_columns=True)
            grads[n], delta[n], new_m[n], new_v[n] = [o.T for o in outs]
        else:
            grads[n], delta[n], new_m[n], new_v[n] = adamw_halves(shard[n], totals[i], from_core[i], m_in[n],
                                                                  v_in[n], core, "adamw_" + n)
    gate = [src["rg_gate_w"].reshape(GATE_ROWS, LANES) for src in (shard, m_in, v_in)]
    grads["rg_gate_w"], delta["rg_gate_w"], new_m["rg_gate_w"], new_v["rg_gate_w"] = adamw(
        gate[0], rep_total, gate[1], gate[2], "adamw_rg_gate_w")
    rest = REPLICATED[1:]
    rest_rows = sum(_seg_rows(shape) for _, shape in rest)
    grads.update(_unpack(sh_total, SHARDED_SMALL))
    grads.update(_unpack(rep_total[GATE_ROWS:GATE_ROWS + rest_rows], rest))
    grads.update(_unpack(last_total, LAST_REPLICATED))
    names = [n for n, _ in SHARDED_SMALL + rest + LAST_REPLICATED]
    rows_of = lambda a, n: a.reshape(-1, given[n].shape[-1])
    outs = adamw_many([rows_of(given[n], n) for n in names], [rows_of(grads[n], n) for n in names],
                      [rows_of(given["m_" + n], n) for n in names], [rows_of(given["v_" + n], n) for n in names],
                      "adamw_small")
    for n, (d, nm, nv) in zip(names, outs):
        delta[n], new_m[n], new_v[n] = d, nm, nv
    result = [loss, grad_x[None]]
    for group in (grads, delta, new_m, new_v):
        result += [group[n].reshape(given[n].shape) for n in WEIGHTS]
    return tuple(result)
```

```python
import jax
import jax.numpy as jnp
from jax import lax
from jax.experimental import pallas as pl
from jax.experimental.pallas import tpu as pltpu

F32 = jnp.float32
BF16 = jnp.bfloat16
MESH = pl.DeviceIdType.MESH

D_MODEL = 1024
NORM_EPS = 1e-6
RG_HEADS = 8
RG_HEAD_DIM = 128
RG_C = 8.0
EVEN_IN = 6144
ODD_IN = 3104
ODD_IN_PAD = 3200
GLA_HEADS = 4
GLA_DK = 128
GLA_DV = 256
GLA_RANK = 16
GLA_NORMALIZER = 16.0
GLA_CHUNK = 128
LR_COL = 3072

ADAM_LR = 0.001
ADAM_B1 = 0.9
ADAM_B2 = 0.999
ADAM_EPS = 1e-08
ADAM_WD = 0.01
ADAM_STEP = 10

SUBLANES = 8
HALO = 16
LANES = 128
VMEM_LIMIT = 56 * 2 ** 20

ROW_TILE = 512
SCAN_TILE = 256
GLA_BLOCK = 2048
MIX_TILE = 256


def _params(*sem):
    return pltpu.CompilerParams(dimension_semantics=sem, vmem_limit_bytes=VMEM_LIMIT)


def _full(shape):
    n = len(shape)
    return pl.BlockSpec(shape, lambda *_: (0,) * n)


def _sigmoid(x):
    return 0.5 + 0.5 * jnp.tanh(0.5 * x)


def _softplus(x):
    return jnp.maximum(x, 0.0) + jnp.log(1.0 + jnp.exp(-jnp.abs(x)))


def _dot(a, b):
    return jnp.dot(a, b, preferred_element_type=F32)


def _dot_nt(a, b):
    return lax.dot_general(a, b, (((1,), (1,)), ((), ())), preferred_element_type=F32)


def _dot_tn(a, b):
    return lax.dot_general(a, b, (((0,), (0,)), ((), ())), preferred_element_type=F32)


def _bdot(a, b, ca, cb):
    return lax.dot_general(a, b, (((ca,), (cb,)), ((0,), (0,))), preferred_element_type=F32)


def _halo_specs(rows, cols, col_block, n_row_tiles, tix):
    per = rows // HALO
    last = n_row_tiles * per - 1

    def split(args):
        if len(args) == 2:
            return tix(args[1]), col_block + args[0]
        return tix(args[0]), col_block

    def prev(*args):
        t, c = split(args)
        return (jnp.maximum(t * per - 1, 0), c)

    def main(*args):
        return split(args)

    def nxt(*args):
        t, c = split(args)
        return (jnp.minimum((t + 1) * per, last), c)

    return [pl.BlockSpec((HALO, cols), prev), pl.BlockSpec((rows, cols), main),
            pl.BlockSpec((HALO, cols), nxt)]


def _extend(prev_ref, main_ref, next_ref, is_first, is_last):
    p = jnp.where(is_first, 0.0, prev_ref[...].astype(F32))
    n = jnp.where(is_last, 0.0, next_ref[...].astype(F32))
    return jnp.concatenate([p, main_ref[...].astype(F32), n], axis=0)


def _shifted(ext, offset, rows):
    if offset == 0:
        return ext[HALO:HALO + rows]
    n = ext.shape[0]
    return pltpu.roll(ext, (-offset) % n, 0)[HALO:HALO + rows]


def _conv(ext, w, left, rows):
    out = None
    for k in range(w.shape[0]):
        term = _shifted(ext, k - left, rows) * w[k:k + 1]
        out = term if out is None else out + term
    return out


def _conv_transpose(ext, w, left, rows):
    out = None
    for k in range(w.shape[0]):
        term = _shifted(ext, left - k, rows) * w[k:k + 1]
        out = term if out is None else out + term
    return out


def _colsum(x):
    return jnp.sum(x, axis=0, keepdims=True)


def _accumulate(ref, value, step):
    @pl.when(step == 0)
    def _():
        ref[...] = value

    @pl.when(step > 0)
    def _():
        ref[...] += value


PROJ_TILE_BYTES = 7 * 2 ** 20


def _proj_row_tile(rows, width, dtype):
    tm = min(ROW_TILE, rows)
    while tm * width * jnp.dtype(dtype).itemsize > PROJ_TILE_BYTES and tm % (2 * HALO) == 0:
        tm //= 2
    return tm


def norm_matmul(x, gain, w, out_dtype, name):
    rows, d = x.shape
    n_col_tiles, _, tn = w.shape
    tm = _proj_row_tile(rows, n_col_tiles * tn, out_dtype)

    def body(x_ref, g_ref, w_ref, proj_ref, h_ref):
        xv = x_ref[...]
        rstd = lax.rsqrt(jnp.mean(xv * xv, axis=-1, keepdims=True) + NORM_EPS)
        hv = (xv * rstd * g_ref[...]).astype(BF16)
        h_ref[...] = hv
        for j in range(n_col_tiles):
            proj_ref[:, j * tn:(j + 1) * tn] = _dot(hv, w_ref[j]).astype(out_dtype)

    row = lambda cols: pl.BlockSpec((tm, cols), lambda i: (i, 0))
    return pl.pallas_call(
        body, name=name,
        out_shape=(jax.ShapeDtypeStruct((rows, n_col_tiles * tn), out_dtype), jax.ShapeDtypeStruct((rows, d), BF16)),
        grid=(rows // tm,),
        in_specs=[row(d), _full((1, d)), _full(w.shape)],
        out_specs=(row(n_col_tiles * tn), row(d)),
        compiler_params=_params("parallel"),
    )(x, gain, w)


def norm_matmul_own(x, gain, w_own, chip, n_col_tiles, out_dtype, name):
    rows, d = x.shape
    tn = w_own.shape[1]
    tm = min(ROW_TILE, rows)

    def body(c_ref, x_ref, g_ref, w_ref, proj_ref, h_ref):
        xv = x_ref[...]
        rstd = lax.rsqrt(jnp.mean(xv * xv, axis=-1, keepdims=True) + NORM_EPS)
        hv = (xv * rstd * g_ref[...]).astype(BF16)
        h_ref[...] = hv
        proj_ref[...] = _dot(hv, w_ref[...]).astype(out_dtype)

    return pl.pallas_call(
        body, name=name,
        out_shape=(jax.ShapeDtypeStruct((rows, n_col_tiles * tn), out_dtype), jax.ShapeDtypeStruct((rows, d), BF16)),
        grid_spec=pltpu.PrefetchScalarGridSpec(
            num_scalar_prefetch=1, grid=(rows // tm,),
            in_specs=[pl.BlockSpec((tm, d), lambda i, c: (i, 0)), pl.BlockSpec((1, d), lambda i, c: (0, 0)),
                      pl.BlockSpec((d, tn), lambda i, c: (0, 0))],
            out_specs=(pl.BlockSpec((tm, tn), lambda i, c: (i, c[0])), pl.BlockSpec((tm, d), lambda i, c: (i, 0)))),
        compiler_params=_params("parallel"),
    )(chip, x, gain, w_own)


def matmul_other_tiles(h, w, proj, chip, name):
    rows, d = h.shape
    n_col_tiles, _, tn = w.shape
    tm = min(ROW_TILE, rows)
    tile = lambda k, c: (c[0] + 1 + k) % n_col_tiles

    def body(c_ref, h_ref, w_ref, p_ref, o_ref):
        o_ref[...] = _dot(h_ref[...], w_ref[0]).astype(o_ref.dtype)

    return pl.pallas_call(
        body, name=name,
        out_shape=jax.ShapeDtypeStruct(proj.shape, proj.dtype),
        grid_spec=pltpu.PrefetchScalarGridSpec(
            num_scalar_prefetch=1, grid=(n_col_tiles - 1, rows // tm),
            in_specs=[pl.BlockSpec((tm, d), lambda k, i, c: (i, 0)),
                      pl.BlockSpec((1, d, tn), lambda k, i, c: (tile(k, c), 0, 0)),
                      pl.BlockSpec(memory_space=pl.ANY)],
            out_specs=pl.BlockSpec((tm, tn), lambda k, i, c: (i, tile(k, c)))),
        input_output_aliases={3: 0},
        compiler_params=_params("arbitrary", "arbitrary"),
    )(chip, h, w, proj)


def inproj_bwd(dproj, w, x, gain, dres, name):
    rows, d = x.shape
    n_col_tiles, _, tn = w.shape
    tm = _proj_row_tile(rows, n_col_tiles * tn, dproj.dtype)

    def body(dp_ref, w_ref, x_ref, g_ref, dres_ref, dx_ref, dg_ref):
        dh = None
        for j in range(n_col_tiles):
            part = _dot_nt(dp_ref[:, j * tn:(j + 1) * tn], w_ref[j])
            dh = part if dh is None else dh + part
        _inproj_finish(dh, x_ref, g_ref, dres_ref, dx_ref, dg_ref, pl.program_id(0))

    row = lambda cols: pl.BlockSpec((tm, cols), lambda i: (i, 0))
    return pl.pallas_call(
        body, name=name,
        out_shape=(jax.ShapeDtypeStruct((rows, d), F32), jax.ShapeDtypeStruct((1, d), F32)),
        grid=(rows // tm,),
        in_specs=[row(n_col_tiles * tn), _full(w.shape), row(d), _full((1, d)), row(d)],
        out_specs=(row(d), _full((1, d))),
        compiler_params=_params("arbitrary"),
    )(dproj, w, x, gain, dres)


def _inproj_finish(dh, x_ref, g_ref, dres_ref, dx_ref, dg_ref, step):
    xv = x_ref[...]
    rstd = lax.rsqrt(jnp.mean(xv * xv, axis=-1, keepdims=True) + NORM_EPS)
    xhat = xv * rstd
    dxn = dh * g_ref[...]
    dx_ref[...] = dres_ref[...] + rstd * (dxn - xhat * jnp.mean(dxn * xhat, axis=-1, keepdims=True))
    _accumulate(dg_ref, _colsum(dh * xhat), step)


def inproj_bwd_pieces(pieces, w, x, gain, dres, name):
    rows, d = x.shape
    tm = min(ROW_TILE, rows)
    n = len(pieces)
    widths = [p.shape[1] for p in pieces]
    assert sum(widths) == w.shape[2]

    def body(*refs):
        w_ref, x_ref, g_ref, dres_ref, dx_ref, dg_ref = refs[n:]
        dproj = jnp.concatenate([refs[k][...] for k in range(n)], axis=1)
        _inproj_finish(_dot_nt(dproj, w_ref[0]), x_ref, g_ref, dres_ref, dx_ref, dg_ref, pl.program_id(0))

    row = lambda cols: pl.BlockSpec((tm, cols), lambda i: (i, 0))
    return pl.pallas_call(
        body, name=name,
        out_shape=(jax.ShapeDtypeStruct((rows, d), F32), jax.ShapeDtypeStruct((1, d), F32)),
        grid=(rows // tm,),
        in_specs=[row(wd) for wd in widths] + [_full(w.shape), row(d), _full((1, d)), row(d)],
        out_specs=(row(d), _full((1, d))),
        compiler_params=_params("arbitrary"),
    )(*pieces, w, x, gain, dres)


def matmul_dw_pieces(a, pieces, name):
    rows, m = a.shape
    tk = min(2 * ROW_TILE, rows)
    n = len(pieces)
    width = sum(p.shape[1] for p in pieces)

    def body(*refs):
        a_ref, ins, out = refs[0], refs[1:1 + n], refs[1 + n]
        b = jnp.concatenate([ref[...] for ref in ins], axis=1)
        _accumulate(out, _dot_tn(a_ref[...], b), pl.program_id(0))

    return pl.pallas_call(
        body, name=name,
        out_shape=jax.ShapeDtypeStruct((m, width), F32),
        grid=(rows // tk,),
        in_specs=[pl.BlockSpec((tk, m), lambda k: (k, 0))]
        + [pl.BlockSpec((tk, p.shape[1]), lambda k: (k, 0)) for p in pieces],
        out_specs=_full((m, width)),
        compiler_params=_params("arbitrary"),
    )(a, *pieces)


def matmul_dw(a, b, bn, name):
    rows, m = a.shape
    n = b.shape[1]
    tk = min((4 if n > bn else 2) * ROW_TILE, rows)
    steps = rows // tk

    def body(a_ref, b_ref, o_ref, o16_ref):
        part = _dot_tn(a_ref[...], b_ref[...])

        @pl.when(pl.program_id(1) == 0)
        def _():
            o_ref[0] = part

        @pl.when(pl.program_id(1) > 0)
        def _():
            o_ref[0] += part

        @pl.when(pl.program_id(1) == steps - 1)
        def _():
            o16_ref[0] = o_ref[0].astype(BF16)

    out = pl.BlockSpec((1, m, bn), lambda j, k: (j, 0, 0))
    return pl.pallas_call(
        body, name=name,
        out_shape=(jax.ShapeDtypeStruct((n // bn, m, bn), F32), jax.ShapeDtypeStruct((n // bn, m, bn), BF16)),
        grid=(n // bn, steps),
        in_specs=[pl.BlockSpec((tk, m), lambda j, k: (k, 0)), pl.BlockSpec((tk, bn), lambda j, k: (k, j))],
        out_specs=(out, out),
        compiler_params=_params("parallel", "arbitrary"),
    )(a, b)


def _scan(a, b, carry, reverse):
    n, c = a.shape
    blocks = n // SUBLANES
    a = a.reshape(blocks, SUBLANES, c)
    b = b.reshape(blocks, SUBLANES, c)
    pos = lax.broadcasted_iota(jnp.int32, (1, SUBLANES, c), 1)
    s = 1
    while s < SUBLANES:
        shift, valid = (SUBLANES - s, pos < SUBLANES - s) if reverse else (s, pos >= s)
        a_s, b_s = pltpu.roll(a, shift, 1), pltpu.roll(b, shift, 1)
        b = jnp.where(valid, a * b_s + b, b)
        a = jnp.where(valid, a * a_s, a)
        s *= 2
    out = [None] * blocks
    for k in (range(blocks - 1, -1, -1) if reverse else range(blocks)):
        h = a[k] * carry + b[k]
        out[k] = h
        carry = h[0:1] if reverse else h[SUBLANES - 1:SUBLANES]
    return jnp.concatenate(out, axis=0)


def _rg_gates(ua, gw_ref, gb, lam):
    ub = ua.astype(BF16)
    pre_r, pre_i = [], []
    for h in range(RG_HEADS):
        z = _dot(ub[:, h * RG_HEAD_DIM:(h + 1) * RG_HEAD_DIM], gw_ref[h])
        pre_r.append(z[:, :RG_HEAD_DIM])
        pre_i.append(z[:, RG_HEAD_DIM:])
    r = _sigmoid(jnp.concatenate(pre_r, axis=1) + gb[0:1])
    i = _sigmoid(jnp.concatenate(pre_i, axis=1) + gb[1:2])
    sp = _softplus(-lam)
    log_a = -RG_C * r * sp
    a = jnp.exp(log_a)
    mult = jnp.sqrt(1.0 - a * a)
    return r, i, sp, a, mult


def _rg_weight_specs():
    return [_full((4, D_MODEL)), _full((1, D_MODEL)), _full((RG_HEADS, RG_HEAD_DIM, 2 * RG_HEAD_DIM)),
            _full((2, D_MODEL)), _full((1, D_MODEL))]


def rglru_fwd(proj, conv_w, conv_b, gate_w, gate_b, lam, reverse, name):
    rows_total = proj.shape[0]
    rows = min(SCAN_TILE, rows_total)
    n_tiles = rows_total // rows
    tix = (lambda i: n_tiles - 1 - i) if reverse else (lambda i: i)

    def body(xp, xm, xn, cw_ref, cb_ref, gw_ref, gb_ref, lam_ref, h_ref, acts_ref, carry):
        i = pl.program_id(0)
        t = tix(i)
        ext = _extend(xp, xm, xn, t == 0, t == n_tiles - 1)
        ua = _conv(ext, cw_ref[...], 2, rows) + cb_ref[...]
        r, gi, _, a, mult = _rg_gates(ua, gw_ref, gb_ref[...], lam_ref[...])
        for k, saved in enumerate((ua, r, gi, a, mult)):
            acts_ref[k] = saved
        b = mult * (gi * ua)

        @pl.when(i == 0)
        def _():
            carry[...] = jnp.zeros_like(carry)

        h = _scan(a, b, carry[0:1], reverse)
        h_ref[...] = h
        edge = h[0:1] if reverse else h[rows - 1:rows]
        carry[...] = jnp.broadcast_to(edge, carry.shape)

    return pl.pallas_call(
        body, name=name,
        out_shape=(jax.ShapeDtypeStruct((rows_total, D_MODEL), F32),
                   jax.ShapeDtypeStruct((5, rows_total, D_MODEL), F32)),
        grid=(n_tiles,),
        in_specs=_halo_specs(rows, D_MODEL, 0, n_tiles, tix) + _rg_weight_specs(),
        out_specs=(pl.BlockSpec((rows, D_MODEL), lambda i: (tix(i), 0)),
                   pl.BlockSpec((5, rows, D_MODEL), lambda i: (0, tix(i), 0))),
        scratch_shapes=[pltpu.VMEM((SUBLANES, D_MODEL), F32)],
        compiler_params=_params("arbitrary"),
    )(proj, proj, proj, conv_w, conv_b, gate_w, gate_b, lam)


def rglru_bwd(proj, dycat, h_dir, acts, gate_w, lam, add_dua, reverse, name):
    rows_total = proj.shape[0]
    rows = min(SCAN_TILE, rows_total)
    n_tiles = rows_total // rows
    tix = (lambda i: i) if reverse else (lambda i: n_tiles - 1 - i)
    za_block = 1

    def body(acts_ref, za_ref, dya_ref, hp, hm, hn, gw_ref, lam_ref, *rest):
        other = rest[0][...] if add_dua is not None else 0.0
        dua_ref, dgw_ref, dgb_ref, dlam_ref, carry = rest[-5:]
        step = pl.program_id(0)
        t = tix(step)
        first, last = t == 0, t == n_tiles - 1
        ua, r, gi, a, mult = (acts_ref[k] for k in range(5))
        lam_v = lam_ref[...]
        sp = _softplus(-lam_v)
        za = za_ref[...].astype(F32)
        dh = dya_ref[...] * (za * _sigmoid(za))

        @pl.when(step == 0)
        def _():
            carry[...] = jnp.zeros_like(carry)

        old = carry[0:1]
        mu = _scan(a, a * dh, old, not reverse)
        row = lax.broadcasted_iota(jnp.int32, mu.shape, 0)
        if reverse:
            mu_next = jnp.where(row == 0, old, pltpu.roll(mu, 1, 0))
            carry[...] = jnp.broadcast_to(mu[rows - 1:rows], carry.shape)
            h_ext = _extend(hp, hm, hn, first, last)
            h_prev = _shifted(h_ext, 1, rows)
        else:
            mu_next = jnp.where(row == rows - 1, old, pltpu.roll(mu, rows - 1, 0))
            carry[...] = jnp.broadcast_to(mu[0:1], carry.shape)
            h_ext = _extend(hp, hm, hn, first, last)
            h_prev = _shifted(h_ext, -1, rows)
        db = dh + mu_next
        da = db * h_prev
        d_mult = db * (gi * ua)
        di = db * (mult * ua)
        dua = db * (mult * gi)
        dlog_a = da * a - d_mult * (a * a) / mult
        dr = dlog_a * (-RG_C * sp)
        dlam = _colsum(dlog_a * (-RG_C * r)) * (-_sigmoid(-lam_v))
        dpr = dr * (r * (1.0 - r))
        dpi = di * (gi * (1.0 - gi))
        dgb = jnp.concatenate([_colsum(dpr), _colsum(dpi)], axis=0)
        ub = ua.astype(BF16)
        dua_heads, dgw_heads = [], []
        for h in range(RG_HEADS):
            cols = slice(h * RG_HEAD_DIM, (h + 1) * RG_HEAD_DIM)
            dz = jnp.concatenate([dpr[:, cols], dpi[:, cols]], axis=1).astype(BF16)
            dgw_heads.append(_dot_tn(ub[:, cols], dz))
            dua_heads.append(_dot_nt(dz, gw_ref[h]))
        dua_ref[...] = dua + jnp.concatenate(dua_heads, axis=1) + other

        @pl.when(step == 0)
        def _():
            for h in range(RG_HEADS):
                dgw_ref[h] = dgw_heads[h]
            dgb_ref[...] = dgb
            dlam_ref[...] = dlam

        @pl.when(step > 0)
        def _():
            for h in range(RG_HEADS):
                dgw_ref[h] += dgw_heads[h]
            dgb_ref[...] += dgb
            dlam_ref[...] += dlam

    row_spec = lambda col: pl.BlockSpec((rows, D_MODEL), lambda i: (tix(i), col))
    return pl.pallas_call(
        body, name=name,
        out_shape=(jax.ShapeDtypeStruct((rows_total, D_MODEL), F32),
                   jax.ShapeDtypeStruct((RG_HEADS, RG_HEAD_DIM, 2 * RG_HEAD_DIM), F32),
                   jax.ShapeDtypeStruct((2, D_MODEL), F32), jax.ShapeDtypeStruct((1, D_MODEL), F32)),
        grid=(n_tiles,),
        in_specs=([pl.BlockSpec((5, rows, D_MODEL), lambda i: (0, tix(i), 0)), row_spec(za_block), row_spec(0)]
                  + _halo_specs(rows, D_MODEL, 0, n_tiles, tix)
                  + [_full((RG_HEADS, RG_HEAD_DIM, 2 * RG_HEAD_DIM)), _full((1, D_MODEL))]
                  + ([] if add_dua is None else [row_spec(0)])),
        out_specs=(row_spec(0), _full((RG_HEADS, RG_HEAD_DIM, 2 * RG_HEAD_DIM)), _full((2, D_MODEL)),
                   _full((1, D_MODEL))),
        scratch_shapes=[pltpu.VMEM((SUBLANES, D_MODEL), F32)],
        compiler_params=_params("arbitrary"),
    )(acts, proj, dycat, h_dir, h_dir, h_dir, gate_w, lam, *([] if add_dua is None else [add_dua]))


def _extend_cols(refs, block, is_first, is_last):
    cols = slice(block * D_MODEL, (block + 1) * D_MODEL)
    prev_ref, main_ref, next_ref = refs
    p = jnp.where(is_first, 0.0, prev_ref[:, cols].astype(F32))
    n = jnp.where(is_last, 0.0, next_ref[:, cols].astype(F32))
    return jnp.concatenate([p, main_ref[:, cols].astype(F32), n], axis=0)


def even_mix_fwd(proj, h_f, h_b, sc_w, name):
    rows_total = proj.shape[0]
    rows = min(2 * MIX_TILE, rows_total)
    n_tiles = rows_total // rows
    ident = lambda i: i

    def body(za_ref, hf_ref, hb_ref, xbp, xbm, xbn, gcp, gcm, gcn, gb_ref, zb_ref, w_ref, y_ref):
        t = pl.program_id(0)
        first, last = t == 0, t == n_tiles - 1
        za = za_ref[...].astype(F32)
        y_ref[:, 0:D_MODEL] = ((hf_ref[...] + hb_ref[...]) * (za * _sigmoid(za))).astype(BF16)
        p_ext = _extend(xbp, xbm, xbn, first, last) * _extend(gcp, gcm, gcn, first, last)
        cv = _conv(p_ext, w_ref[...], 1, rows)
        zb = zb_ref[...].astype(F32)
        y_ref[:, D_MODEL:2 * D_MODEL] = (gb_ref[...].astype(F32) * cv * (zb * _sigmoid(zb))).astype(BF16)

    blk = lambda col: pl.BlockSpec((rows, D_MODEL), lambda i: (i, col))
    return pl.pallas_call(
        body, name=name,
        out_shape=jax.ShapeDtypeStruct((rows_total, 2 * D_MODEL), BF16),
        grid=(n_tiles,),
        in_specs=([blk(1), blk(0), blk(0)] + _halo_specs(rows, D_MODEL, 2, n_tiles, ident)
                  + _halo_specs(rows, D_MODEL, 4, n_tiles, ident) + [blk(3), blk(5), _full((3, D_MODEL))]),
        out_specs=pl.BlockSpec((rows, 2 * D_MODEL), lambda i: (i, 0)),
        compiler_params=_params("parallel"),
    )(proj, h_f, h_b, proj, proj, proj, proj, proj, proj, proj, proj, sc_w)


def even_mix_bwd(proj, dycat, h_f, h_b, dua, conv_w, sc_w, name):
    rows_total, width = proj.shape
    rows = min(MIX_TILE, rows_total)
    n_tiles = rows_total // rows
    ident = lambda i: i

    def body(pp, pm, pn, dyp, dym, dyn, hf_ref, hb_ref, dup, dum, dun, cw_ref, sw_ref,
             dp_ref, dcw_ref, dcb_ref, dsw_ref):
        def put(k, value):
            dp_ref[:, k * D_MODEL:(k + 1) * D_MODEL] = value.astype(BF16)

        t = pl.program_id(0)
        first, last = t == 0, t == n_tiles - 1
        proj_ext = lambda k: _extend_cols((pp, pm, pn), k, first, last)
        mid = slice(HALO, HALO + rows)
        za = pm[:, D_MODEL:2 * D_MODEL].astype(F32)
        sa = _sigmoid(za)
        put(1, dym[:, 0:D_MODEL] * (hf_ref[...] + hb_ref[...]) * (sa * (1.0 + za * (1.0 - sa))))
        dua_ext = _extend(dup, dum, dun, first, last)
        cw = cw_ref[...]
        put(0, _conv_transpose(dua_ext, cw, 2, rows))
        dua_mid = dua_ext[mid]
        xa_ext = proj_ext(0)
        dcw = jnp.concatenate([_colsum(dua_mid * _shifted(xa_ext, k - 2, rows)) for k in range(4)], axis=0)
        dcb = _colsum(dua_mid)
        xb_ext, gb_ext, gc_ext, zb_ext = proj_ext(2), proj_ext(3), proj_ext(4), proj_ext(5)
        p_ext = xb_ext * gc_ext
        sb_ext = _sigmoid(zb_ext)
        dyb_ext = _extend_cols((dyp, dym, dyn), 1, first, last)
        dcv_ext = dyb_ext * gb_ext * (zb_ext * sb_ext)
        sw = sw_ref[...]
        p_at = [_shifted(p_ext, k - 1, rows) for k in range(3)]
        cv = (p_at[0] * sw[0:1] + p_at[1] * sw[1:2]) + p_at[2] * sw[2:3]
        zb, sb, dyb, gb = zb_ext[mid], sb_ext[mid], dyb_ext[mid], gb_ext[mid]
        put(3, dyb * cv * (zb * sb))
        put(5, dyb * gb * cv * (sb * (1.0 + zb * (1.0 - sb))))
        dp = _conv_transpose(dcv_ext, sw, 1, rows)
        put(4, dp * xb_ext[mid])
        put(2, dp * gc_ext[mid])
        dcv = dcv_ext[mid]
        dsw = jnp.concatenate([_colsum(dcv * p_at[k]) for k in range(3)], axis=0)
        _accumulate(dcw_ref, dcw, t)
        _accumulate(dcb_ref, dcb, t)
        _accumulate(dsw_ref, dsw, t)

    own = pl.BlockSpec((rows, D_MODEL), lambda i: (i, 0))
    return pl.pallas_call(
        body, name=name,
        out_shape=(jax.ShapeDtypeStruct((rows_total, 6 * D_MODEL), BF16),
                   jax.ShapeDtypeStruct((4, D_MODEL), F32), jax.ShapeDtypeStruct((1, D_MODEL), F32),
                   jax.ShapeDtypeStruct((3, D_MODEL), F32)),
        grid=(n_tiles,),
        in_specs=(_halo_specs(rows, width, 0, n_tiles, ident) + _halo_specs(rows, 2 * D_MODEL, 0, n_tiles, ident)
                  + [own, own] + _halo_specs(rows, D_MODEL, 0, n_tiles, ident)
                  + [_full((4, D_MODEL)), _full((3, D_MODEL))]),
        out_specs=(pl.BlockSpec((rows, 6 * D_MODEL), lambda i: (i, 0)), _full((4, D_MODEL)), _full((1, D_MODEL)),
                   _full((3, D_MODEL))),
        compiler_params=_params("arbitrary"),
    )(proj, proj, proj, dycat, dycat, dycat, h_f, h_b, dua, dua, dua, conv_w, sc_w)


def even_out_fwd(ycat, w_out, gain, x, name):
    rows, d = x.shape
    k = ycat.shape[1]
    tm = min(ROW_TILE, rows)

    def body(yc_ref, w_ref, g_ref, x_ref, x1_ref, y_ref):
        y = _dot(yc_ref[...], w_ref[...])
        y_ref[...] = y
        rstd = lax.rsqrt(jnp.mean(y * y, axis=-1, keepdims=True) + NORM_EPS)
        x1_ref[...] = x_ref[...] + y * rstd * g_ref[...]

    row = lambda n: pl.BlockSpec((tm, n), lambda i: (i, 0))
    return pl.pallas_call(
        body, name=name,
        out_shape=(jax.ShapeDtypeStruct((rows, d), F32),) * 2,
        grid=(rows // tm,),
        in_specs=[row(k), _full((k, d)), _full((1, d)), row(d)],
        out_specs=(row(d), row(d)),
        compiler_params=_params("parallel"),
    )(ycat, w_out, gain, x)


def _rmsnorm_bwd(dout, y, gain):
    rstd = lax.rsqrt(jnp.mean(y * y, axis=-1, keepdims=True) + NORM_EPS)
    yhat = y * rstd
    dyn = dout * gain
    dy = rstd * (dyn - yhat * jnp.mean(dyn * yhat, axis=-1, keepdims=True))
    return dy, dout * yhat


def even_out_bwd(dx1, y, gain, w_out, name):
    rows, d = y.shape
    k = w_out.shape[0]
    tm = min(ROW_TILE, rows)

    def body(dx_ref, y_ref, g_ref, w_ref, dy_ref, dyc_ref, dg_ref):
        dy, dg_rows = _rmsnorm_bwd(dx_ref[...], y_ref[...], g_ref[...])
        dyb = dy.astype(BF16)
        dy_ref[...] = dyb
        dyc_ref[...] = _dot_nt(dyb, w_ref[...])
        _accumulate(dg_ref, _colsum(dg_rows), pl.program_id(0))

    row = lambda n: pl.BlockSpec((tm, n), lambda i: (i, 0))
    return pl.pallas_call(
        body, name=name,
        out_shape=(jax.ShapeDtypeStruct((rows, d), BF16), jax.ShapeDtypeStruct((rows, k), F32),
                   jax.ShapeDtypeStruct((1, d), F32)),
        grid=(rows // tm,),
        in_specs=[row(d), row(d), _full((1, d)), _full((k, d))],
        out_specs=(row(d), row(k), _full((1, d))),
        compiler_params=_params("arbitrary"),
    )(dx1, y, gain, w_out)


def _chunk_cumsum(g, reverse):
    n, c = g.shape
    chunks, per = n // GLA_CHUNK, GLA_CHUNK // SUBLANES
    g = g.reshape(n // SUBLANES, SUBLANES, c)
    pos = lax.broadcasted_iota(jnp.int32, (1, SUBLANES, c), 1)
    s = 1
    while s < SUBLANES:
        if reverse:
            g = g + jnp.where(pos < SUBLANES - s, pltpu.roll(g, SUBLANES - s, 1), 0.0)
        else:
            g = g + jnp.where(pos >= s, pltpu.roll(g, s, 1), 0.0)
        s *= 2
    g = g.reshape(chunks, per, SUBLANES, c)
    out, carry = [None] * per, None
    for k in (range(per - 1, -1, -1) if reverse else range(per)):
        out[k] = g[:, k] if carry is None else g[:, k] + carry
        carry = out[k][:, 0:1] if reverse else out[k][:, SUBLANES - 1:SUBLANES]
    return jnp.stack(out, axis=1).reshape(n, c)


def _gla_prepare(q_ref, k_ref, lr_ref, wg_ref, bg_ref, reverse, n_chunks):
    z = _dot(lr_ref[...].astype(BF16), wg_ref[0]) + bg_ref[0]
    g = -_softplus(-z) * (1.0 / GLA_NORMALIZER)
    bcum = _chunk_cumsum(g, reverse).reshape(n_chunks, GLA_CHUNK, GLA_DK)
    edge = 0 if reverse else GLA_CHUNK - 1
    btot = bcum[:, edge:edge + 1, :]
    e_pos = jnp.exp(bcum)
    e_neg = jnp.exp(-bcum)
    e_st = jnp.exp(btot - bcum)
    q3 = q_ref[...].reshape(n_chunks, GLA_CHUNK, GLA_DK)
    k3 = k_ref[...].reshape(n_chunks, GLA_CHUNK, GLA_DK)
    scale = GLA_DK ** -0.5
    q_in = q3 * scale * e_pos
    k_in = k3 * e_neg
    k_st = k3 * e_st
    dec = jnp.exp(btot)
    return z, q_in, k_in, k_st, dec, (scale * e_pos, e_neg, e_st)


def _gla_mask(reverse):
    i = lax.broadcasted_iota(jnp.int32, (GLA_CHUNK, GLA_CHUNK), 0)
    j = lax.broadcasted_iota(jnp.int32, (GLA_CHUNK, GLA_CHUNK), 1)
    return (j >= i) if reverse else (j <= i)


def _gla_specs(rows, n_blocks, reverse):
    tix = (lambda s: n_blocks - 1 - s) if reverse else (lambda s: s)
    d = 1 if reverse else 0
    lr_block = LR_COL // LANES
    specs = [pl.BlockSpec((rows, GLA_DK), lambda h, s: (tix(s), h)),
             pl.BlockSpec((rows, GLA_DK), lambda h, s: (tix(s), GLA_HEADS + h)),
             pl.BlockSpec((rows, GLA_DV), lambda h, s: (tix(s), GLA_HEADS + h)),
             pl.BlockSpec((rows, LANES), lambda h, s: (tix(s), lr_block)),
             pl.BlockSpec((1, LANES, GLA_DK), lambda h, s: (d, 0, h)),
             pl.BlockSpec((1, 1, GLA_DK), lambda h, s: (d, 0, h))]
    return specs, tix


def gla_fwd(proj, wg_pad, bg, add_o, reverse, name):
    rows_total = proj.shape[0]
    rows = min(GLA_BLOCK, rows_total)
    n_blocks = rows_total // rows
    n_chunks = rows // GLA_CHUNK
    specs, tix = _gla_specs(rows, n_blocks, reverse)

    def body(q_ref, k_ref, v_ref, lr_ref, wg_ref, bg_ref, *rest):
        o_ref, st_ref, state, kv_scr, dec_scr = rest[-5:]
        _, q_in, k_in, k_st, dec, _ = _gla_prepare(q_ref, k_ref, lr_ref, wg_ref, bg_ref, reverse, n_chunks)
        vb = v_ref[...].reshape(n_chunks, GLA_CHUNK, GLA_DV).astype(BF16)
        qb = q_in.astype(BF16)
        p = jnp.where(_gla_mask(reverse), _bdot(qb, k_in.astype(BF16), 2, 2), 0.0)
        o = _bdot(p.astype(BF16), vb, 2, 1)
        kv_scr[...] = _bdot(vb, k_st.astype(BF16), 1, 1)
        dec_scr[...] = jnp.broadcast_to(dec, dec_scr.shape)

        @pl.when(pl.program_id(1) == 0)
        def _():
            state[...] = jnp.zeros_like(state)

        for c in range(n_chunks):
            cc = n_chunks - 1 - c if reverse else c
            st_ref[0, cc] = state[...]
            state[...] = state[...] * dec_scr[cc, 0:1] + kv_scr[cc]
        o = o + _bdot(qb, st_ref[0].astype(BF16), 2, 2)
        o = o.reshape(rows, GLA_DV)
        o_ref[...] = o if add_o is None else o + rest[0][...]

    o_spec = pl.BlockSpec((rows, GLA_DV), lambda h, s: (tix(s), h))
    return pl.pallas_call(
        body, name=name,
        out_shape=(jax.ShapeDtypeStruct((rows_total, GLA_HEADS * GLA_DV), F32),
                   jax.ShapeDtypeStruct((GLA_HEADS, rows_total // GLA_CHUNK, GLA_DV, GLA_DK), F32)),
        grid=(GLA_HEADS, n_blocks),
        in_specs=specs + ([] if add_o is None else [o_spec]),
        out_specs=(o_spec,
                   pl.BlockSpec((1, n_chunks, GLA_DV, GLA_DK), lambda h, s: (h, tix(s), 0, 0))),
        scratch_shapes=[pltpu.VMEM((GLA_DV, GLA_DK), F32), pltpu.VMEM((n_chunks, GLA_DV, GLA_DK), F32),
                        pltpu.VMEM((n_chunks, SUBLANES, GLA_DK), F32)],
        compiler_params=_params("parallel", "arbitrary"),
    )(proj, proj, proj, proj, wg_pad, bg, *([] if add_o is None else [add_o]))


def gla_bwd(proj, wg_pad, bg, d_o, states, dqkv_in, reverse, name):
    rows_total = proj.shape[0]
    rows = min(GLA_BLOCK, rows_total)
    n_blocks = rows_total // rows
    n_chunks = rows // GLA_CHUNK
    specs, tix = _gla_specs(rows, n_blocks, not reverse)
    d = 1 if reverse else 0
    specs[4] = pl.BlockSpec((1, LANES, GLA_DK), lambda h, s: (d, 0, h))
    specs[5] = pl.BlockSpec((1, 1, GLA_DK), lambda h, s: (d, 0, h))
    add = dqkv_in is not None

    def body(*refs):
        q_ref, k_ref, v_ref, lr_ref, wg_ref, bg_ref, do_ref, st_ref = refs[:8]
        refs = refs[8:]
        if add:
            aq_ref, ak_ref, av_ref = refs[:3]
            refs = refs[3:]
        dq_ref, dk_ref, dv_ref, dz_ref, dstate, g_scr, dec_scr, dsn_scr = refs
        z, q_in, k_in, k_st, dec, (f_q, f_k, f_s) = _gla_prepare(q_ref, k_ref, lr_ref, wg_ref, bg_ref, reverse,
                                                                 n_chunks)
        mask = _gla_mask(reverse)
        vb = v_ref[...].reshape(n_chunks, GLA_CHUNK, GLA_DV).astype(BF16)
        dob = do_ref[...].reshape(n_chunks, GLA_CHUNK, GLA_DV).astype(BF16)
        qb, kb, ksb = q_in.astype(BF16), k_in.astype(BF16), k_st.astype(BF16)
        st = st_ref[0]
        stb = st.astype(BF16)
        pb = jnp.where(mask, _bdot(qb, kb, 2, 2), 0.0).astype(BF16)
        dpb = jnp.where(mask, _bdot(dob, vb, 2, 2), 0.0).astype(BF16)
        d_qin = _bdot(dpb, kb, 2, 1) + _bdot(dob, stb, 2, 1)
        d_kin = _bdot(dpb, qb, 1, 1)
        dv = _bdot(pb, dob, 1, 1)
        g_scr[...] = _bdot(dob, qb, 1, 1)
        dec_scr[...] = jnp.broadcast_to(dec, dec_scr.shape)

        @pl.when(pl.program_id(1) == 0)
        def _():
            dstate[...] = jnp.zeros_like(dstate)

        for c in range(n_chunks):
            cc = c if reverse else n_chunks - 1 - c
            dsn_scr[cc] = dstate[...]
            dstate[...] = dstate[...] * dec_scr[cc, 0:1] + g_scr[cc]
        dsn = dsn_scr[...]
        dsnb = dsn.astype(BF16)
        dv = dv + _bdot(ksb, dsnb, 2, 2)
        d_kst = _bdot(vb, dsnb, 2, 1)
        d_dec = jnp.sum(dsn * st, axis=1, keepdims=True)
        ks_term = d_kst * k_st
        d_btot = d_dec * dec + jnp.sum(ks_term, axis=1, keepdims=True)
        d_b = d_qin * q_in - d_kin * k_in - ks_term
        pos = lax.broadcasted_iota(jnp.int32, d_b.shape, 1)
        edge = 0 if reverse else GLA_CHUNK - 1
        d_b = d_b + jnp.where(pos == edge, d_btot, 0.0)
        dg = _chunk_cumsum(d_b.reshape(rows, GLA_DK), not reverse)
        dz_ref[...] = dg * (1.0 / GLA_NORMALIZER) * _sigmoid(-z)
        dq = (d_qin * f_q).reshape(rows, GLA_DK)
        dk = (d_kin * f_k + d_kst * f_s).reshape(rows, GLA_DK)
        dv = dv.reshape(rows, GLA_DV)
        if add:
            dq_ref[...] = (dq + aq_ref[...]).astype(BF16)
            dk_ref[...] = (dk + ak_ref[...]).astype(BF16)
            dv_ref[...] = (dv + av_ref[...]).astype(BF16)
        else:
            dq_ref[...] = dq
            dk_ref[...] = dk
            dv_ref[...] = dv

    qkv_specs = [pl.BlockSpec((rows, GLA_DK), lambda h, s: (tix(s), h)),
                 pl.BlockSpec((rows, GLA_DK), lambda h, s: (tix(s), h)),
                 pl.BlockSpec((rows, GLA_DV), lambda h, s: (tix(s), h))]
    in_specs = specs + [pl.BlockSpec((rows, GLA_DV), lambda h, s: (tix(s), h)),
                        pl.BlockSpec((1, n_chunks, GLA_DV, GLA_DK), lambda h, s: (h, tix(s), 0, 0))]
    args = [proj, proj, proj, proj, wg_pad, bg, d_o, states]
    out_dtype = F32
    if add:
        in_specs += qkv_specs
        args += list(dqkv_in)
        out_dtype = BF16
    return pl.pallas_call(
        body, name=name,
        out_shape=(jax.ShapeDtypeStruct((rows_total, GLA_HEADS * GLA_DK), out_dtype),
                   jax.ShapeDtypeStruct((rows_total, GLA_HEADS * GLA_DK), out_dtype),
                   jax.ShapeDtypeStruct((rows_total, GLA_HEADS * GLA_DV), out_dtype),
                   jax.ShapeDtypeStruct((rows_total, GLA_HEADS * GLA_DK), F32)),
        grid=(GLA_HEADS, n_blocks),
        in_specs=in_specs,
        out_specs=(pl.BlockSpec((rows, GLA_DK), lambda h, s: (tix(s), h)),
                   pl.BlockSpec((rows, GLA_DK), lambda h, s: (tix(s), h)),
                   pl.BlockSpec((rows, GLA_DV), lambda h, s: (tix(s), h)),
                   pl.BlockSpec((rows, GLA_DK), lambda h, s: (tix(s), h))),
        scratch_shapes=[pltpu.VMEM((GLA_DV, GLA_DK), F32), pltpu.VMEM((n_chunks, GLA_DV, GLA_DK), F32),
                        pltpu.VMEM((n_chunks, SUBLANES, GLA_DK), F32),
                        pltpu.VMEM((n_chunks, GLA_DV, GLA_DK), F32)],
        compiler_params=_params("parallel", "arbitrary"),
    )(*args)


def gla_gate_bwd(proj, dz_f, dz_b, wg_pad, name):
    rows_total = proj.shape[0]
    tm = min(ROW_TILE, rows_total)
    n_key = GLA_HEADS * GLA_DK

    def body(lr_ref, dzf_ref, dzb_ref, wg_ref, dlr_ref, dwg_ref, dbg_ref):
        step = pl.program_id(0)
        lr_t = jnp.transpose(lr_ref[...])
        dzf, dzb = dzf_ref[...], dzb_ref[...]
        dzf16, dzb16 = dzf.astype(BF16), dzb.astype(BF16)
        dlr_ref[...] = (_dot_nt(dzf16, wg_ref[0]) + _dot_nt(dzb16, wg_ref[1])).astype(BF16)
        dwf = _dot(lr_t[0:GLA_RANK].astype(BF16), dzf16)
        dwb = _dot(lr_t[GLA_RANK:2 * GLA_RANK].astype(BF16), dzb16)
        dbg = jnp.concatenate([_colsum(dzf), _colsum(dzb)], axis=0)

        @pl.when(step == 0)
        def _():
            dwg_ref[0] = dwf
            dwg_ref[1] = dwb
            dbg_ref[...] = dbg

        @pl.when(step > 0)
        def _():
            dwg_ref[0] += dwf
            dwg_ref[1] += dwb
            dbg_ref[...] += dbg

    return pl.pallas_call(
        body, name=name,
        out_shape=(jax.ShapeDtypeStruct((rows_total, LANES), BF16), jax.ShapeDtypeStruct((2, GLA_RANK, n_key), F32),
                   jax.ShapeDtypeStruct((2, n_key), F32)),
        grid=(rows_total // tm,),
        in_specs=[pl.BlockSpec((tm, LANES), lambda i: (i, LR_COL // LANES)),
                  pl.BlockSpec((tm, n_key), lambda i: (i, 0)), pl.BlockSpec((tm, n_key), lambda i: (i, 0)),
                  _full((2, LANES, n_key))],
        out_specs=(pl.BlockSpec((tm, LANES), lambda i: (i, 0)), _full((2, GLA_RANK, n_key)), _full((2, n_key))),
        compiler_params=_params("arbitrary"),
    )(proj, dz_f, dz_b, wg_pad)


def _head_norm(o, gain):
    outs, hats, rstds = [], [], []
    for h in range(GLA_HEADS):
        oh = o[:, h * GLA_DV:(h + 1) * GLA_DV]
        rstd = lax.rsqrt(jnp.mean(oh * oh, axis=-1, keepdims=True) + NORM_EPS)
        hat = oh * rstd
        outs.append(hat * gain)
        hats.append(hat)
        rstds.append(rstd)
    return outs, hats, rstds


def odd_out_fwd(o, proj, head_gain, w_out, gain, x1, target, name):
    rows, d = x1.shape
    tm = min(ROW_TILE, rows)
    r_block = (2 * GLA_HEADS * GLA_DK + GLA_HEADS * GLA_DV) // d

    def body(o_ref, r_ref, hg_ref, w_ref, g_ref, x1_ref, tgt_ref, y2_ref, dy_ref, dx2_ref, loss_ref, dg_ref):
        step = pl.program_id(0)
        on, _, _ = _head_norm(o_ref[...], hg_ref[...])
        r = r_ref[...]
        y2 = (jnp.concatenate(on, axis=1) * (r * _sigmoid(r))).astype(BF16)
        y2_ref[...] = y2
        y = _dot(y2, w_ref[...])
        gain_v = g_ref[...]
        rstd = lax.rsqrt(jnp.mean(y * y, axis=-1, keepdims=True) + NORM_EPS)
        x2 = x1_ref[...] + y * rstd * gain_v
        diff = x2 - tgt_ref[...]
        loss = 0.5 * jnp.sum(jnp.mean(diff * diff, axis=-1, keepdims=True), axis=0, keepdims=True)
        dx2 = diff * (1.0 / d)
        dx2_ref[...] = dx2
        dy, dg_rows = _rmsnorm_bwd(dx2, y, gain_v)
        dy_ref[...] = dy.astype(BF16)
        _accumulate(loss_ref, jnp.broadcast_to(loss, loss_ref.shape), step)
        _accumulate(dg_ref, _colsum(dg_rows), step)

    row = lambda n, col=0: pl.BlockSpec((tm, n), lambda i: (i, col))
    return pl.pallas_call(
        body, name=name,
        out_shape=(jax.ShapeDtypeStruct((rows, d), BF16), jax.ShapeDtypeStruct((rows, d), BF16),
                   jax.ShapeDtypeStruct((rows, d), F32), jax.ShapeDtypeStruct((SUBLANES, LANES), F32),
                   jax.ShapeDtypeStruct((1, d), F32)),
        grid=(rows // tm,),
        in_specs=[row(d), row(d, r_block), _full((1, GLA_DV)), _full((d, d)), _full((1, d)), row(d), row(d)],
        out_specs=(row(d), row(d), row(d), _full((SUBLANES, LANES)), _full((1, d))),
        compiler_params=_params("arbitrary"),
    )(o, proj, head_gain, w_out, gain, x1, target)


def odd_out_bwd(dy, w_out, o, proj, head_gain, name):
    rows, d = dy.shape
    tm = min(ROW_TILE, rows)
    r_block = (2 * GLA_HEADS * GLA_DK + GLA_HEADS * GLA_DV) // d

    def body(dy_ref, w_ref, o_ref, r_ref, hg_ref, dr_ref, do_ref, dhg_ref):
        dy2 = _dot_nt(dy_ref[...], w_ref[...])
        hg = hg_ref[...]
        on, hats, rstds = _head_norm(o_ref[...], hg)
        r = r_ref[...]
        sr = _sigmoid(r)
        dr_ref[...] = (dy2 * jnp.concatenate(on, axis=1) * (sr * (1.0 + r * (1.0 - sr)))).astype(BF16)
        d_on = dy2 * (r * sr)
        d_os, dhg = [], None
        for h in range(GLA_HEADS):
            dn = d_on[:, h * GLA_DV:(h + 1) * GLA_DV]
            part = _colsum(dn * hats[h])
            dhg = part if dhg is None else dhg + part
            dng = dn * hg
            d_os.append(rstds[h] * (dng - hats[h] * jnp.mean(dng * hats[h], axis=-1, keepdims=True)))
        do_ref[...] = jnp.concatenate(d_os, axis=1)
        _accumulate(dhg_ref, dhg, pl.program_id(0))

    row = lambda n, col=0: pl.BlockSpec((tm, n), lambda i: (i, col))
    return pl.pallas_call(
        body, name=name,
        out_shape=(jax.ShapeDtypeStruct((rows, d), BF16), jax.ShapeDtypeStruct((rows, d), F32),
                   jax.ShapeDtypeStruct((1, GLA_DV), F32)),
        grid=(rows // tm,),
        in_specs=[row(d), _full((d, d)), row(d), row(d, r_block), _full((1, GLA_DV))],
        out_specs=(row(d), row(d), _full((1, GLA_DV))),
        compiler_params=_params("arbitrary"),
    )(dy, w_out, o, proj, head_gain)


def local_step(x, target, w, reduce_first=None, reduce_second=None, late_weights=None, even_proj=None):
    g, g16 = {}, {}
    if even_proj is None:
        proj_e, h0 = norm_matmul(x, w["even_norm_pre"], w["even_w_in"], BF16, "even_in_proj")
    else:
        proj_e, h0 = even_proj
    h_dir, acts = zip(*[rglru_fwd(proj_e, w["rg_conv_w"], w["rg_conv_b"], w["rg_gate_w"][d], w["rg_gate_b"][d],
                                  w["rg_lambda"][d], d == 1, "rglru_fwd_%d" % d) for d in range(2)])
    ycat = even_mix_fwd(proj_e, h_dir[0], h_dir[1], w["sc_conv_w"], "even_mix_fwd")
    if late_weights is not None:
        w = dict(w, **late_weights(ycat))
    x1, y_e = even_out_fwd(ycat, w["even_w_out"], w["even_norm_post"], x, "even_out_fwd")
    proj_o, h1 = norm_matmul(x1, w["odd_norm_pre"], w["odd_w_in"], F32, "odd_in_proj")
    o, st_dir = None, []
    for d in range(2):
        o, st = gla_fwd(proj_o, w["gla_wg_pad"], w["gla_b_gate"], o, d == 1, "gla_fwd_%d" % d)
        st_dir.append(st)
    y2, dy_o, dx2, loss, g["odd_norm_post"] = odd_out_fwd(
        o, proj_o, w["gla_norm_g"], w["odd_w_out"], w["odd_norm_post"], x1, target, "odd_out_fwd")
    g["odd_w_out"], g16["odd_w_out"] = (a[0] for a in matmul_dw(y2, dy_o, D_MODEL, "odd_w_out_grad"))
    dr, d_o, g["gla_norm_g"] = odd_out_bwd(dy_o, w["odd_w_out"], o, proj_o, w["gla_norm_g"], "odd_out_bwd")
    dq, dk, dv, dz_f = gla_bwd(proj_o, w["gla_wg_pad"], w["gla_b_gate"], d_o, st_dir[0], None, False, "gla_bwd_0")
    dq, dk, dv, dz_b = gla_bwd(proj_o, w["gla_wg_pad"], w["gla_b_gate"], d_o, st_dir[1], (dq, dk, dv), True,
                               "gla_bwd_1")
    dlr, g["gla_w_gate_lr"], g["gla_b_gate"] = gla_gate_bwd(proj_o, dz_f, dz_b, w["gla_wg_pad"], "gla_gate_bwd")
    dproj_o = [dq, dk, dv, dr, dlr]
    g["odd_w_in"] = matmul_dw_pieces(h1, dproj_o, "odd_w_in_grad")[:, :ODD_IN]
    dx1, g["odd_norm_pre"] = inproj_bwd_pieces(dproj_o, w["odd_w_in"], x1, w["odd_norm_pre"], dx2, "odd_in_proj_bwd")
    dy_e, dycat, g["even_norm_post"] = even_out_bwd(dx1, y_e, w["even_norm_post"], w["even_w_out"], "even_out_bwd")
    g["even_w_out"], g16["even_w_out"] = (a[0] for a in matmul_dw(ycat, dy_e, D_MODEL, "even_w_out_grad"))
    lam = w["rg_lambda"] if reduce_first is None else w["rg_lambda"] + reduce_first(g, g16)
    dua, dgw, dgb, dlam = None, [], [], []
    for d in range(2):
        a, b, c, e = rglru_bwd(proj_e, dycat, h_dir[d], acts[d], w["rg_gate_w"][d], lam[d], dua, d == 1,
                               "rglru_bwd_%d" % d)
        dua = a
        dgw.append(b)
        dgb.append(c)
        dlam.append(e)
    dproj_e, g["rg_conv_w"], g["rg_conv_b"], g["sc_conv_w"] = even_mix_bwd(
        proj_e, dycat, h_dir[0], h_dir[1], dua, w["rg_conv_w"], w["sc_conv_w"], "even_mix_bwd")
    dgw = jnp.stack(dgw).reshape(2, RG_HEADS, RG_HEAD_DIM, 2, RG_HEAD_DIM)
    g["rg_gate_w"] = jnp.transpose(dgw, (0, 3, 1, 2, 4))
    g["rg_gate_b"] = jnp.stack(dgb).reshape(2, 2, RG_HEADS, RG_HEAD_DIM)
    g["rg_lambda"] = jnp.concatenate(dlam, axis=0)
    g["even_w_in"], g16["even_w_in"] = matmul_dw(h0, dproj_e, EVEN_IN // 4, "even_w_in_grad")
    gain = w["even_norm_pre"] if reduce_second is None else w["even_norm_pre"] + reduce_second(g, g16)
    grad_x, g["even_norm_pre"] = inproj_bwd(dproj_e, w["even_w_in"], x, gain, dx1, "even_in_proj_bwd")
    return loss, grad_x, g


def _prepare_weights(full):
    w = {}
    for name in ("even_norm_pre", "even_norm_post", "rg_conv_b", "odd_norm_pre", "odd_norm_post", "gla_norm_g"):
        if name in full:
            w[name] = full[name].reshape(1, -1)
    for name in ("rg_conv_w", "sc_conv_w"):
        if name in full:
            w[name] = full[name]
    for name in ("even_w_out", "odd_w_out"):
        if name in full:
            w[name] = full[name].astype(BF16)
    if "even_w_in" in full:
        w["even_w_in"] = full["even_w_in"].astype(BF16)
        if w["even_w_in"].ndim == 2:
            w["even_w_in"] = jnp.transpose(w["even_w_in"].reshape(D_MODEL, 4, EVEN_IN // 4), (1, 0, 2))
    if "rg_gate_w" in full:
        gw = jnp.transpose(full["rg_gate_w"].astype(BF16), (0, 2, 3, 1, 4))
        w["rg_gate_w"] = gw.reshape(2, RG_HEADS, RG_HEAD_DIM, 2 * RG_HEAD_DIM)
        w["rg_gate_b"] = full["rg_gate_b"].reshape(2, 2, D_MODEL)
        w["rg_lambda"] = full["rg_lambda"].reshape(2, 1, D_MODEL)
    if "odd_w_in" in full:
        w_in = jnp.pad(full["odd_w_in"].astype(BF16), ((0, 0), (0, ODD_IN_PAD - ODD_IN)))
        w["odd_w_in"] = w_in.reshape(1, D_MODEL, ODD_IN_PAD)
    if "gla_w_gate_lr" in full:
        wg = full["gla_w_gate_lr"].astype(BF16)
        w["gla_wg_pad"] = jnp.stack([jnp.pad(wg[d], ((d * GLA_RANK, LANES - (d + 1) * GLA_RANK), (0, 0)))
                                     for d in range(2)])
        w["gla_b_gate"] = full["gla_b_gate"].reshape(2, 1, GLA_HEADS * GLA_DK)
    return w


SHARDED_SMALL = (("rg_conv_w", (4, 256)), ("rg_lambda", (2, 256)), ("sc_conv_w", (3, 256)),
                 ("odd_norm_pre", (256,)), ("odd_norm_post", (256,)), ("gla_w_gate_lr", (2, 16, 128)),
                 ("gla_b_gate", (2, 128)), ("gla_norm_g", (64,)))
SHARDED_ROWS = 96
REPLICATED = (("rg_gate_w", (2, 2, 8, 128, 128)), ("even_norm_post", (1024,)), ("rg_conv_b", (1024,)),
              ("rg_gate_b", (2, 2, 8, 128)))
GATE_ROWS = 4096
LAST_REPLICATED = (("even_norm_pre", (1024,)),)
LAST_ROWS = 8
REPLICATED_ROWS = 4160
REP_PART = REPLICATED_ROWS // 8
HALF_SHARDED = SHARDED_ROWS // 2
PACK_HALF = HALF_SHARDED + REP_PART


def _seg_rows(shape):
    n = 1
    for s in shape:
        n *= s
    return -(-n // (SUBLANES * LANES)) * SUBLANES


def _pack(arrays, spec, total_rows, lead=()):
    parts = []
    for name, shape in spec:
        flat = arrays[name].reshape(lead + (-1,))
        pad = _seg_rows(shape) * LANES - flat.shape[-1]
        if pad:
            flat = jnp.pad(flat, [(0, 0)] * len(lead) + [(0, pad)])
        parts.append(flat.reshape(lead + (-1, LANES)))
    rows = jnp.concatenate(parts, axis=len(lead))
    pad = total_rows - rows.shape[len(lead)]
    return jnp.pad(rows, [(0, 0)] * len(lead) + [(0, pad), (0, 0)])


def _unpack(rows, spec, lead=()):
    out, at = {}, 0
    for name, shape in spec:
        n = 1
        for s in shape:
            n *= s
        k = _seg_rows(shape)
        seg = lax.slice_in_dim(rows, at, at + k, axis=len(lead)).reshape(lead + (-1,))
        out[name] = lax.slice_in_dim(seg, 0, n, axis=len(lead)).reshape(lead + shape)
        at += k
    return out


def _split_owners(arr):
    a = arr.reshape(arr.shape[:-1] + (4, arr.shape[-1] // 4))
    return jnp.moveaxis(a, -2, 0)


def _merge_owners(arr):
    a = jnp.moveaxis(arr, 0, -2)
    return a.reshape(a.shape[:-2] + (-1,))


HBM_SPEC = pl.BlockSpec(memory_space=pltpu.HBM)


def _position():
    x, y, c = lax.axis_index("x"), lax.axis_index("y"), lax.axis_index("c")
    chips = [(1 - x, y), (x, 1 - y), (1 - x, 1 - y)]
    return x, y, c, chips


def _remote(src, dst, send_sem, recv_sem, device):
    return pltpu.make_async_remote_copy(src_ref=src, dst_ref=dst, send_sem=send_sem, recv_sem=recv_sem,
                                        device_id=device, device_id_type=MESH)


SEM_SPEC = pl.BlockSpec(memory_space=pltpu.SEMAPHORE)
SIDE_EFFECT = pltpu.SideEffectType.DATAFLOW_SIDE_EFFECTING


def _gather_copies(ins, lands, n_h, send_sems, recv_sems):
    x, y, c, chips = _position()
    me = 2 * x + y
    copies = []
    for a in range(len(ins)):
        for k, chip in enumerate(chips):
            src = ins[a].at[c] if a < n_h else ins[a]
            dst = lands[a].at[me, c] if a < n_h else lands[a].at[me]
            copies.append(_remote(src, dst, send_sems.at[3 * a + k], recv_sems.at[3 * a + k], (chip[0], chip[1], c)))
    return copies


def gather_start(halved, whole, name):
    arrays = list(halved) + list(whole)
    n, n_h = len(arrays), len(halved)
    lands = [lax.empty((4,) + a.shape, a.dtype) for a in arrays]

    def body(*refs):
        ins, lz, send_sems, recv_sems, token = refs[:n], refs[n:2 * n], refs[2 * n], refs[2 * n + 1], refs[-1]
        for cp in _gather_copies(ins, lz, n_h, send_sems, recv_sems):
            cp.start()
        token[...] = jnp.zeros_like(token)

    operands = [pltpu.with_memory_space_constraint(a, pltpu.HBM) for a in arrays + lands]
    return pl.pallas_call(
        body, name=name,
        out_shape=(pltpu.SemaphoreType.DMA((3 * n,)), pltpu.SemaphoreType.DMA((3 * n,)))
        + tuple(pltpu.HBM(a.shape, a.dtype) for a in operands) + (jax.ShapeDtypeStruct((SUBLANES, LANES), F32),),
        in_specs=[HBM_SPEC] * (2 * n),
        out_specs=(SEM_SPEC, SEM_SPEC) + (HBM_SPEC,) * (2 * n) + (pl.BlockSpec(memory_space=pltpu.VMEM),),
        input_output_aliases={i: 2 + i for i in range(2 * n)},
        compiler_params=pltpu.CompilerParams(has_side_effects=SIDE_EFFECT),
    )(*operands)


def gather_wait(started, n_h, after, name):
    send_sems, recv_sems = started[0], started[1]
    operands = list(started[2:-1])
    n = len(operands) // 2

    def body(*refs):
        ins, lz, send_ref, recv_ref = refs[:n], refs[n:2 * n], refs[2 * n], refs[2 * n + 1]
        for cp in _gather_copies(ins, lz, n_h, send_ref, recv_ref):
            cp.wait_send()
            cp.wait_recv()

    outs = pl.pallas_call(
        body, name=name,
        out_shape=tuple(pltpu.HBM(a.shape, a.dtype) for a in operands),
        in_specs=[HBM_SPEC] * (2 * n) + [SEM_SPEC, SEM_SPEC, pl.BlockSpec(memory_space=pl.ANY)],
        out_specs=(HBM_SPEC,) * (2 * n),
        input_output_aliases={i: i for i in range(2 * n)},
        compiler_params=pltpu.CompilerParams(has_side_effects=SIDE_EFFECT),
    )(*operands, send_sems, recv_sems, after)
    return outs[n:]


def pass_to_sibling(fulls, name):
    n = len(fulls)

    def body(*refs):
        bufs = refs[n:2 * n]
        send_sems, recv_sems = refs[2 * n:]
        x, y, c, chips = _position()
        sibling = (x, y, 1 - c)
        copies = []
        for a in range(n):
            for k, chip in enumerate(chips):
                q = 2 * chip[0] + chip[1]
                cp = _remote(bufs[a].at[q, c], bufs[a].at[q, c], send_sems.at[3 * a + k], recv_sems.at[3 * a + k],
                             sibling)
                cp.start()
                copies.append(cp)
        for a in range(n):
            for k, chip in enumerate(chips):
                q = 2 * chip[0] + chip[1]
                passed = bufs[a].at[q, 1 - c]
                _remote(passed, passed, send_sems.at[3 * a + k], recv_sems.at[3 * a + k], sibling).wait_recv()
        for cp in copies:
            cp.wait_send()

    return pl.pallas_call(
        body, name=name,
        out_shape=[jax.ShapeDtypeStruct(a.shape, a.dtype) for a in fulls],
        in_specs=[HBM_SPEC] * n, out_specs=[HBM_SPEC] * n,
        input_output_aliases={i: i for i in range(n)},
        scratch_shapes=[pltpu.SemaphoreType.DMA((3 * n,)), pltpu.SemaphoreType.DMA((3 * n,))],
    )(*fulls)


def place_own(full, own, chip, name):
    _, _, r, cols = full.shape
    tr = _row_tile(r, cols)

    def body(p_ref, own_ref, full_ref, o_ref):
        o_ref[0] = own_ref[...]

    return pl.pallas_call(
        body, name=name,
        out_shape=jax.ShapeDtypeStruct(full.shape, full.dtype),
        grid_spec=pltpu.PrefetchScalarGridSpec(
            num_scalar_prefetch=1, grid=(2, r // tr),
            in_specs=[pl.BlockSpec((1, tr, cols), lambda h, i, p_ref: (h, i, 0)), pl.BlockSpec(memory_space=pl.ANY)],
            out_specs=pl.BlockSpec((1, 1, tr, cols), lambda h, i, p_ref: (p_ref[0], h, i, 0))),
        input_output_aliases={2: 0},
        compiler_params=_params("parallel", "parallel"),
    )(chip, own, full)


def exchange_with_sibling(arrays, name):
    n = len(arrays)

    def body(*refs):
        ins, outs = refs[:n], refs[n:2 * n]
        send_sems, recv_sems = refs[2 * n:]
        x, y, c, _ = _position()
        copies = []
        for a in range(n):
            cp = _remote(ins[a].at[:, 1 - c], outs[a], send_sems.at[a], recv_sems.at[a], (x, y, 1 - c))
            cp.start()
            copies.append(cp)
        for cp in copies:
            cp.wait()

    return pl.pallas_call(
        body, name=name,
        out_shape=[jax.ShapeDtypeStruct((a.shape[0],) + a.shape[2:], a.dtype) for a in arrays],
        in_specs=[HBM_SPEC] * n, out_specs=[HBM_SPEC] * n,
        scratch_shapes=[pltpu.SemaphoreType.DMA((n,)), pltpu.SemaphoreType.DMA((n,))],
    )(*arrays)


def _chip_copies(ins, lands, send_sems, recv_sems):
    x, y, c, chips = _position()
    copies = []
    for a in range(len(ins)):
        for k, chip in enumerate(chips):
            q = 2 * chip[0] + chip[1]
            copies.append(_remote(ins[a].at[q], lands[a].at[k], send_sems.at[3 * a + k], recv_sems.at[3 * a + k],
                                  (chip[0], chip[1], c)))
    return copies


def exchange_with_chips_start(arrays, name):
    n = len(arrays)
    lands = [lax.empty((3,) + a.shape[1:], a.dtype) for a in arrays]

    def body(*refs):
        ins, lz, send_sems, recv_sems, token = refs[:n], refs[n:2 * n], refs[2 * n], refs[2 * n + 1], refs[-1]
        for cp in _chip_copies(ins, lz, send_sems, recv_sems):
            cp.start()
        token[...] = jnp.zeros_like(token)

    operands = [pltpu.with_memory_space_constraint(a, pltpu.HBM) for a in list(arrays) + lands]
    return pl.pallas_call(
        body, name=name,
        out_shape=(pltpu.SemaphoreType.DMA((3 * n,)), pltpu.SemaphoreType.DMA((3 * n,)))
        + tuple(pltpu.HBM(a.shape, a.dtype) for a in operands) + (jax.ShapeDtypeStruct((SUBLANES, LANES), F32),),
        in_specs=[HBM_SPEC] * (2 * n),
        out_specs=(SEM_SPEC, SEM_SPEC) + (HBM_SPEC,) * (2 * n) + (pl.BlockSpec(memory_space=pltpu.VMEM),),
        input_output_aliases={i: 2 + i for i in range(2 * n)},
        compiler_params=pltpu.CompilerParams(has_side_effects=SIDE_EFFECT),
    )(*operands)


def exchange_with_chips_wait(started, after, name):
    send_sems, recv_sems = started[0], started[1]
    operands = list(started[2:-1])
    n = len(operands) // 2

    def body(*refs):
        ins, lz, send_ref, recv_ref = refs[:n], refs[n:2 * n], refs[2 * n], refs[2 * n + 1]
        for cp in _chip_copies(ins, lz, send_ref, recv_ref):
            cp.wait_send()
            cp.wait_recv()

    outs = pl.pallas_call(
        body, name=name,
        out_shape=tuple(pltpu.HBM(a.shape, a.dtype) for a in operands),
        in_specs=[HBM_SPEC] * (2 * n) + [SEM_SPEC, SEM_SPEC, pl.BlockSpec(memory_space=pl.ANY)],
        out_specs=(HBM_SPEC,) * (2 * n),
        input_output_aliases={i: i for i in range(2 * n)},
        compiler_params=pltpu.CompilerParams(has_side_effects=SIDE_EFFECT),
    )(*operands, send_sems, recv_sems, after)
    return outs[:n], outs[n:]


def share_totals(totals, pack_total, last_part):
    arrays = list(totals) + [pack_total]
    n = len(arrays)

    def body(*refs):
        ins, last, outs, rep, last_all = refs[:n], refs[n], refs[n + 1:2 * n + 1], refs[2 * n + 1], refs[2 * n + 2]
        send_sems, recv_sems, rep_send, rep_recv, last_send, last_recv = refs[2 * n + 3:]
        x, y, c, chips = _position()
        sibling = (x, y, 1 - c)
        me = 4 * x + 2 * y + c
        sends = []
        for a in range(n):
            cp = _remote(ins[a], outs[a], send_sems.at[a], recv_sems.at[a], sibling)
            cp.start()
            sends.append(cp)
        mine = ins[n - 1].at[pl.ds(HALF_SHARDED, REP_PART)]
        peers = [sibling]
        for chip in chips:
            peers += [(chip[0], chip[1], c), (chip[0], chip[1], 1 - c)]
        for j, peer in enumerate(peers):
            for src, dst, s_sem, r_sem in ((mine, rep, rep_send, rep_recv), (last, last_all, last_send, last_recv)):
                cp = _remote(src, dst.at[me], s_sem.at[j], r_sem.at[j], peer)
                cp.start()
                sends.append(cp)
        for a in range(n):
            _remote(outs[a], outs[a], send_sems.at[a], recv_sems.at[a], sibling).wait_recv()
        for j, peer in enumerate(peers):
            it = 4 * peer[0] + 2 * peer[1] + peer[2]
            _remote(rep.at[it], rep.at[it], rep_send.at[j], rep_recv.at[j], peer).wait_recv()
            _remote(last_all.at[it], last_all.at[it], last_send.at[j], last_recv.at[j], peer).wait_recv()
        for cp in sends:
            cp.wait_send()

    outs = pl.pallas_call(
        body, name="grad_share_totals",
        out_shape=[jax.ShapeDtypeStruct(a.shape, a.dtype) for a in arrays]
        + [jax.ShapeDtypeStruct((8, REP_PART, LANES), F32), jax.ShapeDtypeStruct((8,) + last_part.shape, F32)],
        in_specs=[HBM_SPEC] * (n + 1), out_specs=[HBM_SPEC] * (n + 2),
        scratch_shapes=[pltpu.SemaphoreType.DMA((n,)), pltpu.SemaphoreType.DMA((n,))]
        + [pltpu.SemaphoreType.DMA((7,))] * 4,
    )(*arrays, last_part)
    return outs[:n], outs[n], outs[n + 1]


def sum_parts(parts, name):
    def body(p_ref, o_ref):
        total = p_ref[0]
        for k in range(1, parts.shape[0]):
            total = total + p_ref[k]
        o_ref[...] = total

    return pl.pallas_call(body, name=name, out_shape=jax.ShapeDtypeStruct(parts.shape[1:], parts.dtype))(parts)


TILE_BYTES = 2 << 20


def _row_tile(rows, cols):
    best = None
    for t in range(SUBLANES, rows + 1, SUBLANES):
        if rows % t == 0 and t * cols * 4 <= TILE_BYTES:
            best = t
    return best if best is not None else rows


def add_sibling(mine, received, core, out_dtype, name):
    _, _, r, cols = mine.shape
    tr = _row_tile(r, cols)

    def body(c_ref, a_ref, b_ref, o_ref):
        o_ref[...] = (a_ref[0] + b_ref[...].astype(F32)).astype(out_dtype)

    return pl.pallas_call(
        body, name=name,
        out_shape=jax.ShapeDtypeStruct((4, r, cols), out_dtype),
        grid_spec=pltpu.PrefetchScalarGridSpec(
            num_scalar_prefetch=1, grid=(4, r // tr),
            in_specs=[pl.BlockSpec((1, 1, tr, cols), lambda o, i, c_ref: (o, c_ref[0], i, 0)),
                      pl.BlockSpec((1, tr, cols), lambda o, i, c_ref: (o, i, 0))],
            out_specs=pl.BlockSpec((1, tr, cols), lambda o, i, c_ref: (o, i, 0))),
        compiler_params=_params("parallel", "parallel"),
    )(core, mine, received)


def add_chips(own, received, chip, name):
    _, r, cols = own.shape
    tr = _row_tile(r, cols)

    def body(p_ref, a_ref, b0, b1, b2, o_ref):
        o_ref[...] = ((a_ref[0].astype(F32) + b0[0].astype(F32)) + b1[0].astype(F32)) + b2[0].astype(F32)

    rb = lambda k: pl.BlockSpec((1, tr, cols), lambda i, p_ref: (k, i, 0))
    return pl.pallas_call(
        body, name=name,
        out_shape=jax.ShapeDtypeStruct((r, cols), F32),
        grid_spec=pltpu.PrefetchScalarGridSpec(
            num_scalar_prefetch=1, grid=(r // tr,),
            in_specs=[pl.BlockSpec((1, tr, cols), lambda i, p_ref: (p_ref[0], i, 0)), rb(0), rb(1), rb(2)],
            out_specs=pl.BlockSpec((tr, cols), lambda i, p_ref: (i, 0))),
        compiler_params=_params("parallel"),
    )(chip, own, received, received, received)


def _adamw_update(gv, w_ref, m_ref, v_ref, d_ref, nm_ref, nv_ref):
    nm = ADAM_B1 * m_ref[...] + (1.0 - ADAM_B1) * gv
    nv = ADAM_B2 * v_ref[...] + (1.0 - ADAM_B2) * (gv * gv)
    nm_ref[...] = nm
    nv_ref[...] = nv
    m_hat = nm / (1.0 - ADAM_B1 ** ADAM_STEP)
    v_hat = nv / (1.0 - ADAM_B2 ** ADAM_STEP)
    d_ref[...] = -ADAM_LR * (m_hat / (jnp.sqrt(v_hat) + ADAM_EPS) + ADAM_WD * w_ref[...])


def adamw_halves(w, own, received, m, v, core, name, by_columns=False):
    rows, cols = w.shape

    def body(c_ref, w_ref, own_ref, rec_ref, m_ref, v_ref, g_ref, d_ref, nm_ref, nv_ref):
        gv = jnp.where(pl.program_id(0) == c_ref[0], own_ref[...], rec_ref[...])
        g_ref[...] = gv
        _adamw_update(gv, w_ref, m_ref, v_ref, d_ref, nm_ref, nv_ref)

    if by_columns:
        nr = 1
        whole = pl.BlockSpec((rows, cols // 2), lambda h, i, c_ref: (0, h))
        half = pl.BlockSpec((rows, cols // 2), lambda h, i, c_ref: (0, 0))
    else:
        r = rows // 2
        tr = _row_tile(r, cols)
        nr = r // tr
        whole = pl.BlockSpec((tr, cols), lambda h, i, c_ref: (h * nr + i, 0))
        half = pl.BlockSpec((tr, cols), lambda h, i, c_ref: (i, 0))
    return pl.pallas_call(
        body, name=name,
        out_shape=(jax.ShapeDtypeStruct((rows, cols), F32),) * 4,
        grid_spec=pltpu.PrefetchScalarGridSpec(
            num_scalar_prefetch=1, grid=(2, nr),
            in_specs=[whole, half, half, whole, whole], out_specs=(whole,) * 4),
        compiler_params=_params("parallel", "parallel"),
    )(core, w, own, received, m, v)


def adamw_many(ws, gs, ms, vs, name):
    n = len(ws)

    def body(*refs):
        ins, outs = refs[:4 * n], refs[4 * n:]
        for k in range(n):
            w_ref, g_ref, m_ref, v_ref = (ins[j * n + k] for j in range(4))
            d_ref, nm_ref, nv_ref = outs[3 * k:3 * k + 3]
            _adamw_update(g_ref[...], w_ref, m_ref, v_ref, d_ref, nm_ref, nv_ref)

    flat = pl.pallas_call(
        body, name=name,
        out_shape=[jax.ShapeDtypeStruct(w.shape, F32) for w in ws for _ in range(3)],
    )(*ws, *gs, *ms, *vs)
    return [tuple(flat[3 * k:3 * k + 3]) for k in range(n)]


def adamw(w, g, m, v, name):
    r, cols = w.shape
    tr = _row_tile(r, cols)

    def body(w_ref, g_ref, m_ref, v_ref, g_out, d_ref, nm_ref, nv_ref):
        gv = g_ref[...]
        g_out[...] = gv
        _adamw_update(gv, w_ref, m_ref, v_ref, d_ref, nm_ref, nv_ref)

    blk = pl.BlockSpec((tr, cols), lambda i: (i, 0))
    return pl.pallas_call(
        body, name=name,
        out_shape=(jax.ShapeDtypeStruct((r, cols), F32),) * 4,
        grid=(r // tr,),
        in_specs=[blk] * 4, out_specs=(blk,) * 4,
        compiler_params=_params("parallel"),
    )(w, g, m, v)


WEIGHTS = ("even_norm_pre", "even_norm_post", "even_w_in", "rg_conv_w", "rg_conv_b", "rg_gate_w", "rg_gate_b",
           "rg_lambda", "sc_conv_w", "even_w_out", "odd_norm_pre", "odd_norm_post", "odd_w_in", "gla_w_gate_lr",
           "gla_b_gate", "gla_norm_g", "odd_w_out")
BIG = ("even_w_in", "even_w_out", "odd_w_in", "odd_w_out")


def _halves(a):
    return a.reshape((2, a.shape[0] // 2) + a.shape[1:])


def kernel(x, even_norm_pre, even_norm_post, even_w_in, rg_conv_w, rg_conv_b, rg_gate_w, rg_gate_b, rg_lambda, sc_conv_w, even_w_out, odd_norm_pre, odd_norm_post, odd_w_in, gla_w_gate_lr, gla_b_gate, gla_norm_g, odd_w_out, loss_target, m_even_norm_pre, m_even_norm_post, m_even_w_in, m_rg_conv_w, m_rg_conv_b, m_rg_gate_w, m_rg_gate_b, m_rg_lambda, m_sc_conv_w, m_even_w_out, m_odd_norm_pre, m_odd_norm_post, m_odd_w_in, m_gla_w_gate_lr, m_gla_b_gate, m_gla_norm_g, m_odd_w_out, v_even_norm_pre, v_even_norm_post, v_even_w_in, v_rg_conv_w, v_rg_conv_b, v_rg_gate_w, v_rg_gate_b, v_rg_lambda, v_sc_conv_w, v_even_w_out, v_odd_norm_pre, v_odd_norm_post, v_odd_w_in, v_gla_w_gate_lr, v_gla_b_gate, v_gla_norm_g, v_odd_w_out):
    given = dict(locals())
    shard = {n: given[n][0] for n in WEIGHTS}
    m_in = {n: given["m_" + n][0] for n in WEIGHTS}
    v_in = {n: given["v_" + n][0] for n in WEIGHTS}
    mx, my, mc = lax.axis_index("x"), lax.axis_index("y"), lax.axis_index("c")
    core = jnp.reshape(mc, (1,)).astype(jnp.int32)
    chip = jnp.reshape(2 * mx + my, (1,)).astype(jnp.int32)

    small_shard = _pack(shard, SHARDED_SMALL, SHARDED_ROWS)
    big_own = [_halves(shard[n].astype(BF16)) for n in BIG]
    started_a = gather_start(big_own[:1], [small_shard], "gather_start_a")
    started_b = gather_start(big_own[1:], [], "gather_start_b")
    proj_own, h0 = norm_matmul_own(x[0], even_norm_pre + started_b[-1][0, 0], shard["even_w_in"].astype(BF16), chip, 4,
                                   BF16, "even_in_proj_own")
    even_w_in_full, small_full = gather_wait(started_a, 1, proj_own, "gather_wait_a")
    (even_w_in_full,) = pass_to_sibling([even_w_in_full], "gather_pass_a")
    proj_e = matmul_other_tiles(h0, even_w_in_full.reshape(4, D_MODEL, EVEN_IN // 4), proj_own, chip,
                                "even_in_proj_rest")
    even_w_in_full = place_own(even_w_in_full, big_own[0], chip, "place_even_w_in")
    small_full = lax.dynamic_update_slice(small_full, small_shard[None], (chip[0], 0, 0))
    full = {n: shard[n] for n, _ in REPLICATED + LAST_REPLICATED}
    full.update({n: _merge_owners(a) for n, a in _unpack(small_full, SHARDED_SMALL, lead=(4,)).items()})
    full["even_w_in"] = even_w_in_full.reshape(4, D_MODEL, EVEN_IN // 4)

    def late_weights(after):
        lands = pass_to_sibling(list(gather_wait(started_b, 3, after, "gather_wait_b")), "gather_pass_b")
        lands = [place_own(a, b, chip, "place_" + n) for a, b, n in zip(lands, big_own[1:], BIG[1:])]
        odd_w_in = jnp.transpose(lands[1].reshape(4, D_MODEL, ODD_IN // 4), (1, 0, 2)).reshape(D_MODEL, ODD_IN)
        return _prepare_weights({"even_w_out": lands[0].reshape(2 * D_MODEL, D_MODEL), "odd_w_in": odd_w_in,
                                 "odd_w_out": lands[2].reshape(D_MODEL, D_MODEL)})

    pending = {}

    def slab(a):
        return a.reshape((4, 2, a.shape[1] // 2) + a.shape[2:])

    def begin(tag, slabs, to_send, dtypes):
        got = exchange_with_sibling(to_send, "grad_sibling_" + tag)
        sums = [add_sibling(a, b, core, dt, "grad_add_sibling_%s%d" % (tag, i))
                for i, (a, b, dt) in enumerate(zip(slabs, got, dtypes))]
        pending[tag] = exchange_with_chips_start(sums, "grad_chips_start_" + tag)
        return pending[tag][-1][0, 0]

    def finish(tag, after):
        sums, got = exchange_with_chips_wait(pending[tag], after, "grad_chips_wait_" + tag)
        return [add_chips(a, b, chip, "grad_add_chips_%s%d" % (tag, i)) for i, (a, b) in enumerate(zip(sums, got))]

    def reduce_first(g, g16):
        odd_w_in = slab(jnp.transpose(g["odd_w_in"].reshape(D_MODEL, 4, ODD_IN // 4), (1, 0, 2)))
        slabs = [odd_w_in] + [slab(g[n].reshape(4, -1, D_MODEL)) for n in ("odd_w_out", "even_w_out")]
        to_send = [odd_w_in.astype(BF16)] + [slab(g16[n].reshape(4, -1, D_MODEL)) for n in ("odd_w_out", "even_w_out")]
        return begin("a", slabs, to_send, [BF16] * 3)

    def reduce_second(g, g16):
        pending["totals_a"] = finish("a", g["even_w_in"])
        rep_rows = _pack(g, REPLICATED, REPLICATED_ROWS).reshape(4, 2, REP_PART, LANES)
        sh_rows = _pack({n: _split_owners(g[n]) for n, _ in SHARDED_SMALL}, SHARDED_SMALL, SHARDED_ROWS, lead=(4,))
        pack = jnp.concatenate([sh_rows.reshape(4, 2, HALF_SHARDED, LANES), rep_rows], axis=2)
        return begin("b", [slab(g["even_w_in"]), pack], [slab(g16["even_w_in"]), pack], [BF16, F32])

    loss, grad_x, g = local_step(x[0], loss_target[0], _prepare_weights(full), reduce_first, reduce_second,
                                 late_weights, (proj_e, h0))
    odd_w_in_t, odd_w_out_t, even_w_out_t = pending["totals_a"]
    even_w_in_t, pack_t = finish("b", grad_x)
    totals = [even_w_in_t, even_w_out_t, odd_w_in_t, odd_w_out_t]
    last_part = jnp.concatenate([_pack(g, LAST_REPLICATED, LAST_ROWS), loss])
    from_core, rep_all, last_all = share_totals(totals, pack_t, last_part)
    me = 2 * chip[0] + core[0]
    mine, theirs = pack_t[:HALF_SHARDED], from_core[4][:HALF_SHARDED]
    sh_total = jnp.where(mc == 0, jnp.concatenate([mine, theirs]), jnp.concatenate([theirs, mine]))
    rep_all = lax.dynamic_update_slice(rep_all, pack_t[None, HALF_SHARDED:], (me, 0, 0))
    rep_total = rep_all.reshape(REPLICATED_ROWS, LANES)
    last_total = sum_parts(lax.dynamic_update_slice(last_all, last_part[None], (me, 0, 0)), "grad_sum_last")
    last_total, loss = last_total[:LAST_ROWS], last_total[LAST_ROWS, 0]
    grads = {}

    delta, new_m, new_v = {}, {}, {}
    for i, n in enumerate(BIG):
        if shard[n].shape[1] % LANES:
            outs = adamw_halves(shard[n].T, totals[i].T, from_core[i].T, m_in[n].T, v_in[n].T, core, "adamw_" + n,
                                by_columns=True)
            grads[n], delta[n], new_m[n], new_v[n] = [o.T for o in outs]
        else:
            grads[n], delta[n], new_m[n], new_v[n] = adamw_halves(shard[n], totals[i], from_core[i], m_in[n],
                                                                  v_in[n], core, "adamw_" + n)
    gate = [src["rg_gate_w"].reshape(GATE_ROWS, LANES) for src in (shard, m_in, v_in)]
    grads["rg_gate_w"], delta["rg_gate_w"], new_m["rg_gate_w"], new_v["rg_gate_w"] = adamw(
        gate[0], rep_total, gate[1], gate[2], "adamw_rg_gate_w")
    rest = REPLICATED[1:]
    rest_rows = sum(_seg_rows(shape) for _, shape in rest)
    grads.update(_unpack(sh_total, SHARDED_SMALL))
    grads.update(_unpack(rep_total[GATE_ROWS:GATE_ROWS + rest_rows], rest))
    grads.update(_unpack(last_total, LAST_REPLICATED))
    names = [n for n, _ in SHARDED_SMALL + rest + LAST_REPLICATED]
    rows_of = lambda a, n: a.reshape(-1, given[n].shape[-1])
    outs = adamw_many([rows_of(given[n], n) for n in names], [rows_of(grads[n], n) for n in names],
                      [rows_of(given["m_" + n], n) for n in names], [rows_of(given["v_" + n], n) for n in names],
                      "adamw_small")
    for n, (d, nm, nv) in zip(names, outs):
        delta[n], new_m[n], new_v[n] = d, nm, nv
    result = [loss, grad_x[None]]
    for group in (grads, delta, new_m, new_v):
        result += [group[n].reshape(given[n].shape) for n in WEIGHTS]
    return tuple(result)
```

```python
import jax
import jax.numpy as jnp
from jax import lax
from jax.experimental import pallas as pl
from jax.experimental.pallas import tpu as pltpu

F32 = jnp.float32
BF16 = jnp.bfloat16
MESH = pl.DeviceIdType.MESH

D_MODEL = 1024
NORM_EPS = 1e-6
RG_HEADS = 8
RG_HEAD_DIM = 128
RG_C = 8.0
EVEN_IN = 6144
ODD_IN = 3104
ODD_IN_PAD = 3200
GLA_HEADS = 4
GLA_DK = 128
GLA_DV = 256
GLA_RANK = 16
GLA_NORMALIZER = 16.0
GLA_CHUNK = 128
LR_COL = 3072

ADAM_LR = 0.001
ADAM_B1 = 0.9
ADAM_B2 = 0.999
ADAM_EPS = 1e-08
ADAM_WD = 0.01
ADAM_STEP = 10

SUBLANES = 8
HALO = 16
LANES = 128
VMEM_LIMIT = 56 * 2 ** 20

ROW_TILE = 512
SCAN_TILE = 256
GLA_BLOCK = 2048
MIX_TILE = 256


def _params(*sem):
    return pltpu.CompilerParams(dimension_semantics=sem, vmem_limit_bytes=VMEM_LIMIT)


def _full(shape):
    n = len(shape)
    return pl.BlockSpec(shape, lambda *_: (0,) * n)


def _sigmoid(x):
    return 0.5 + 0.5 * jnp.tanh(0.5 * x)


def _softplus(x):
    return jnp.maximum(x, 0.0) + jnp.log(1.0 + jnp.exp(-jnp.abs(x)))


def _dot(a, b):
    return jnp.dot(a, b, preferred_element_type=F32)


def _dot_nt(a, b):
    return lax.dot_general(a, b, (((1,), (1,)), ((), ())), preferred_element_type=F32)


def _dot_tn(a, b):
    return lax.dot_general(a, b, (((0,), (0,)), ((), ())), preferred_element_type=F32)


def _bdot(a, b, ca, cb):
    return lax.dot_general(a, b, (((ca,), (cb,)), ((0,), (0,))), preferred_element_type=F32)


def _halo_specs(rows, cols, col_block, n_row_tiles, tix):
    per = rows // HALO
    last = n_row_tiles * per - 1

    def split(args):
        if len(args) == 2:
            return tix(args[1]), col_block + args[0]
        return tix(args[0]), col_block

    def prev(*args):
        t, c = split(args)
        return (jnp.maximum(t * per - 1, 0), c)

    def main(*args):
        return split(args)

    def nxt(*args):
        t, c = split(args)
        return (jnp.minimum((t + 1) * per, last), c)

    return [pl.BlockSpec((HALO, cols), prev), pl.BlockSpec((rows, cols), main),
            pl.BlockSpec((HALO, cols), nxt)]


def _extend(prev_ref, main_ref, next_ref, is_first, is_last):
    p = jnp.where(is_first, 0.0, prev_ref[...].astype(F32))
    n = jnp.where(is_last, 0.0, next_ref[...].astype(F32))
    return jnp.concatenate([p, main_ref[...].astype(F32), n], axis=0)


def _shifted(ext, offset, rows):
    if offset == 0:
        return ext[HALO:HALO + rows]
    n = ext.shape[0]
    return pltpu.roll(ext, (-offset) % n, 0)[HALO:HALO + rows]


def _conv(ext, w, left, rows):
    out = None
    for k in range(w.shape[0]):
        term = _shifted(ext, k - left, rows) * w[k:k + 1]
        out = term if out is None else out + term
    return out


def _conv_transpose(ext, w, left, rows):
    out = None
    for k in range(w.shape[0]):
        term = _shifted(ext, left - k, rows) * w[k:k + 1]
        out = term if out is None else out + term
    return out


def _colsum(x):
    return jnp.sum(x, axis=0, keepdims=True)


def _accumulate(ref, value, step):
    @pl.when(step == 0)
    def _():
        ref[...] = value

    @pl.when(step > 0)
    def _():
        ref[...] += value


PROJ_TILE_BYTES = 7 * 2 ** 20


def _proj_row_tile(rows, width, dtype):
    tm = min(ROW_TILE, rows)
    while tm * width * jnp.dtype(dtype).itemsize > PROJ_TILE_BYTES and tm % (2 * HALO) == 0:
        tm //= 2
    return tm


def norm_matmul(x, gain, w, out_dtype, name):
    rows, d = x.shape
    n_col_tiles, _, tn = w.shape
    tm = _proj_row_tile(rows, n_col_tiles * tn, out_dtype)

    def body(x_ref, g_ref, w_ref, proj_ref, h_ref):
        xv = x_ref[...]
        rstd = lax.rsqrt(jnp.mean(xv * xv, axis=-1, keepdims=True) + NORM_EPS)
        hv = (xv * rstd * g_ref[...]).astype(BF16)
        h_ref[...] = hv
        for j in range(n_col_tiles):
            proj_ref[:, j * tn:(j + 1) * tn] = _dot(hv, w_ref[j]).astype(out_dtype)

    row = lambda cols: pl.BlockSpec((tm, cols), lambda i: (i, 0))
    return pl.pallas_call(
        body, name=name,
        out_shape=(jax.ShapeDtypeStruct((rows, n_col_tiles * tn), out_dtype), jax.ShapeDtypeStruct((rows, d), BF16)),
        grid=(rows // tm,),
        in_specs=[row(d), _full((1, d)), _full(w.shape)],
        out_specs=(row(n_col_tiles * tn), row(d)),
        compiler_params=_params("parallel"),
    )(x, gain, w)


def inproj_bwd(dproj, w, x, gain, dres, name):
    rows, d = x.shape
    n_col_tiles, _, tn = w.shape
    tm = _proj_row_tile(rows, n_col_tiles * tn, dproj.dtype)

    def body(dp_ref, w_ref, x_ref, g_ref, dres_ref, dx_ref, dg_ref):
        dh = None
        for j in range(n_col_tiles):
            part = _dot_nt(dp_ref[:, j * tn:(j + 1) * tn], w_ref[j])
            dh = part if dh is None else dh + part
        _inproj_finish(dh, x_ref, g_ref, dres_ref, dx_ref, dg_ref, pl.program_id(0))

    row = lambda cols: pl.BlockSpec((tm, cols), lambda i: (i, 0))
    return pl.pallas_call(
        body, name=name,
        out_shape=(jax.ShapeDtypeStruct((rows, d), F32), jax.ShapeDtypeStruct((1, d), F32)),
        grid=(rows // tm,),
        in_specs=[row(n_col_tiles * tn), _full(w.shape), row(d), _full((1, d)), row(d)],
        out_specs=(row(d), _full((1, d))),
        compiler_params=_params("arbitrary"),
    )(dproj, w, x, gain, dres)


def _inproj_finish(dh, x_ref, g_ref, dres_ref, dx_ref, dg_ref, step):
    xv = x_ref[...]
    rstd = lax.rsqrt(jnp.mean(xv * xv, axis=-1, keepdims=True) + NORM_EPS)
    xhat = xv * rstd
    dxn = dh * g_ref[...]
    dx_ref[...] = dres_ref[...] + rstd * (dxn - xhat * jnp.mean(dxn * xhat, axis=-1, keepdims=True))
    _accumulate(dg_ref, _colsum(dh * xhat), step)


def inproj_bwd_pieces(pieces, w, x, gain, dres, name):
    rows, d = x.shape
    tm = min(ROW_TILE, rows)
    n = len(pieces)
    widths = [p.shape[1] for p in pieces]
    assert sum(widths) == w.shape[2]

    def body(*refs):
        w_ref, x_ref, g_ref, dres_ref, dx_ref, dg_ref = refs[n:]
        dproj = jnp.concatenate([refs[k][...] for k in range(n)], axis=1)
        _inproj_finish(_dot_nt(dproj, w_ref[0]), x_ref, g_ref, dres_ref, dx_ref, dg_ref, pl.program_id(0))

    row = lambda cols: pl.BlockSpec((tm, cols), lambda i: (i, 0))
    return pl.pallas_call(
        body, name=name,
        out_shape=(jax.ShapeDtypeStruct((rows, d), F32), jax.ShapeDtypeStruct((1, d), F32)),
        grid=(rows // tm,),
        in_specs=[row(wd) for wd in widths] + [_full(w.shape), row(d), _full((1, d)), row(d)],
        out_specs=(row(d), _full((1, d))),
        compiler_params=_params("arbitrary"),
    )(*pieces, w, x, gain, dres)


def matmul_dw_pieces(a, pieces, name):
    rows, m = a.shape
    tk = min(2 * ROW_TILE, rows)
    n = len(pieces)
    width = sum(p.shape[1] for p in pieces)

    def body(*refs):
        a_ref, ins, out = refs[0], refs[1:1 + n], refs[1 + n]
        b = jnp.concatenate([ref[...] for ref in ins], axis=1)
        _accumulate(out, _dot_tn(a_ref[...], b), pl.program_id(0))

    return pl.pallas_call(
        body, name=name,
        out_shape=jax.ShapeDtypeStruct((m, width), F32),
        grid=(rows // tk,),
        in_specs=[pl.BlockSpec((tk, m), lambda k: (k, 0))]
        + [pl.BlockSpec((tk, p.shape[1]), lambda k: (k, 0)) for p in pieces],
        out_specs=_full((m, width)),
        compiler_params=_params("arbitrary"),
    )(a, *pieces)


def matmul_dw(a, b, bn, name):
    rows, m = a.shape
    n = b.shape[1]
    tk = min((4 if n > bn else 2) * ROW_TILE, rows)
    steps = rows // tk

    def body(a_ref, b_ref, o_ref, o16_ref):
        part = _dot_tn(a_ref[...], b_ref[...])

        @pl.when(pl.program_id(1) == 0)
        def _():
            o_ref[0] = part

        @pl.when(pl.program_id(1) > 0)
        def _():
            o_ref[0] += part

        @pl.when(pl.program_id(1) == steps - 1)
        def _():
            o16_ref[0] = o_ref[0].astype(BF16)

    out = pl.BlockSpec((1, m, bn), lambda j, k: (j, 0, 0))
    return pl.pallas_call(
        body, name=name,
        out_shape=(jax.ShapeDtypeStruct((n // bn, m, bn), F32), jax.ShapeDtypeStruct((n // bn, m, bn), BF16)),
        grid=(n // bn, steps),
        in_specs=[pl.BlockSpec((tk, m), lambda j, k: (k, 0)), pl.BlockSpec((tk, bn), lambda j, k: (k, j))],
        out_specs=(out, out),
        compiler_params=_params("parallel", "arbitrary"),
    )(a, b)


def _scan(a, b, carry, reverse):
    n, c = a.shape
    blocks = n // SUBLANES
    a = a.reshape(blocks, SUBLANES, c)
    b = b.reshape(blocks, SUBLANES, c)
    pos = lax.broadcasted_iota(jnp.int32, (1, SUBLANES, c), 1)
    s = 1
    while s < SUBLANES:
        shift, valid = (SUBLANES - s, pos < SUBLANES - s) if reverse else (s, pos >= s)
        a_s, b_s = pltpu.roll(a, shift, 1), pltpu.roll(b, shift, 1)
        b = jnp.where(valid, a * b_s + b, b)
        a = jnp.where(valid, a * a_s, a)
        s *= 2
    out = [None] * blocks
    for k in (range(blocks - 1, -1, -1) if reverse else range(blocks)):
        h = a[k] * carry + b[k]
        out[k] = h
        carry = h[0:1] if reverse else h[SUBLANES - 1:SUBLANES]
    return jnp.concatenate(out, axis=0)


def _rg_gates(ua, gw_ref, gb, lam):
    ub = ua.astype(BF16)
    pre_r, pre_i = [], []
    for h in range(RG_HEADS):
        z = _dot(ub[:, h * RG_HEAD_DIM:(h + 1) * RG_HEAD_DIM], gw_ref[h])
        pre_r.append(z[:, :RG_HEAD_DIM])
        pre_i.append(z[:, RG_HEAD_DIM:])
    r = _sigmoid(jnp.concatenate(pre_r, axis=1) + gb[0:1])
    i = _sigmoid(jnp.concatenate(pre_i, axis=1) + gb[1:2])
    sp = _softplus(-lam)
    log_a = -RG_C * r * sp
    a = jnp.exp(log_a)
    mult = jnp.sqrt(1.0 - a * a)
    return r, i, sp, a, mult


def _rg_weight_specs():
    return [_full((4, D_MODEL)), _full((1, D_MODEL)), _full((RG_HEADS, RG_HEAD_DIM, 2 * RG_HEAD_DIM)),
            _full((2, D_MODEL)), _full((1, D_MODEL))]


def rglru_fwd(proj, conv_w, conv_b, gate_w, gate_b, lam, reverse, name):
    rows_total = proj.shape[0]
    rows = min(SCAN_TILE, rows_total)
    n_tiles = rows_total // rows
    tix = (lambda i: n_tiles - 1 - i) if reverse else (lambda i: i)

    def body(xp, xm, xn, cw_ref, cb_ref, gw_ref, gb_ref, lam_ref, h_ref, acts_ref, carry):
        i = pl.program_id(0)
        t = tix(i)
        ext = _extend(xp, xm, xn, t == 0, t == n_tiles - 1)
        ua = _conv(ext, cw_ref[...], 2, rows) + cb_ref[...]
        r, gi, _, a, mult = _rg_gates(ua, gw_ref, gb_ref[...], lam_ref[...])
        for k, saved in enumerate((ua, r, gi, a, mult)):
            acts_ref[k] = saved
        b = mult * (gi * ua)

        @pl.when(i == 0)
        def _():
            carry[...] = jnp.zeros_like(carry)

        h = _scan(a, b, carry[0:1], reverse)
        h_ref[...] = h
        edge = h[0:1] if reverse else h[rows - 1:rows]
        carry[...] = jnp.broadcast_to(edge, carry.shape)

    return pl.pallas_call(
        body, name=name,
        out_shape=(jax.ShapeDtypeStruct((rows_total, D_MODEL), F32),
                   jax.ShapeDtypeStruct((5, rows_total, D_MODEL), F32)),
        grid=(n_tiles,),
        in_specs=_halo_specs(rows, D_MODEL, 0, n_tiles, tix) + _rg_weight_specs(),
        out_specs=(pl.BlockSpec((rows, D_MODEL), lambda i: (tix(i), 0)),
                   pl.BlockSpec((5, rows, D_MODEL), lambda i: (0, tix(i), 0))),
        scratch_shapes=[pltpu.VMEM((SUBLANES, D_MODEL), F32)],
        compiler_params=_params("arbitrary"),
    )(proj, proj, proj, conv_w, conv_b, gate_w, gate_b, lam)


def rglru_bwd(proj, dycat, h_dir, acts, gate_w, lam, add_dua, reverse, name):
    rows_total = proj.shape[0]
    rows = min(SCAN_TILE, rows_total)
    n_tiles = rows_total // rows
    tix = (lambda i: i) if reverse else (lambda i: n_tiles - 1 - i)
    za_block = 1

    def body(acts_ref, za_ref, dya_ref, hp, hm, hn, gw_ref, lam_ref, *rest):
        other = rest[0][...] if add_dua is not None else 0.0
        dua_ref, dgw_ref, dgb_ref, dlam_ref, carry = rest[-5:]
        step = pl.program_id(0)
        t = tix(step)
        first, last = t == 0, t == n_tiles - 1
        ua, r, gi, a, mult = (acts_ref[k] for k in range(5))
        lam_v = lam_ref[...]
        sp = _softplus(-lam_v)
        za = za_ref[...].astype(F32)
        dh = dya_ref[...] * (za * _sigmoid(za))

        @pl.when(step == 0)
        def _():
            carry[...] = jnp.zeros_like(carry)

        old = carry[0:1]
        mu = _scan(a, a * dh, old, not reverse)
        row = lax.broadcasted_iota(jnp.int32, mu.shape, 0)
        if reverse:
            mu_next = jnp.where(row == 0, old, pltpu.roll(mu, 1, 0))
            carry[...] = jnp.broadcast_to(mu[rows - 1:rows], carry.shape)
            h_ext = _extend(hp, hm, hn, first, last)
            h_prev = _shifted(h_ext, 1, rows)
        else:
            mu_next = jnp.where(row == rows - 1, old, pltpu.roll(mu, rows - 1, 0))
            carry[...] = jnp.broadcast_to(mu[0:1], carry.shape)
            h_ext = _extend(hp, hm, hn, first, last)
            h_prev = _shifted(h_ext, -1, rows)
        db = dh + mu_next
        da = db * h_prev
        d_mult = db * (gi * ua)
        di = db * (mult * ua)
        dua = db * (mult * gi)
        dlog_a = da * a - d_mult * (a * a) / mult
        dr = dlog_a * (-RG_C * sp)
        dlam = _colsum(dlog_a * (-RG_C * r)) * (-_sigmoid(-lam_v))
        dpr = dr * (r * (1.0 - r))
        dpi = di * (gi * (1.0 - gi))
        dgb = jnp.concatenate([_colsum(dpr), _colsum(dpi)], axis=0)
        ub = ua.astype(BF16)
        dua_heads, dgw_heads = [], []
        for h in range(RG_HEADS):
            cols = slice(h * RG_HEAD_DIM, (h + 1) * RG_HEAD_DIM)
            dz = jnp.concatenate([dpr[:, cols], dpi[:, cols]], axis=1).astype(BF16)
            dgw_heads.append(_dot_tn(ub[:, cols], dz))
            dua_heads.append(_dot_nt(dz, gw_ref[h]))
        dua_ref[...] = dua + jnp.concatenate(dua_heads, axis=1) + other

        @pl.when(step == 0)
        def _():
            for h in range(RG_HEADS):
                dgw_ref[h] = dgw_heads[h]
            dgb_ref[...] = dgb
            dlam_ref[...] = dlam

        @pl.when(step > 0)
        def _():
            for h in range(RG_HEADS):
                dgw_ref[h] += dgw_heads[h]
            dgb_ref[...] += dgb
            dlam_ref[...] += dlam

    row_spec = lambda col: pl.BlockSpec((rows, D_MODEL), lambda i: (tix(i), col))
    return pl.pallas_call(
        body, name=name,
        out_shape=(jax.ShapeDtypeStruct((rows_total, D_MODEL), F32),
                   jax.ShapeDtypeStruct((RG_HEADS, RG_HEAD_DIM, 2 * RG_HEAD_DIM), F32),
                   jax.ShapeDtypeStruct((2, D_MODEL), F32), jax.ShapeDtypeStruct((1, D_MODEL), F32)),
        grid=(n_tiles,),
        in_specs=([pl.BlockSpec((5, rows, D_MODEL), lambda i: (0, tix(i), 0)), row_spec(za_block), row_spec(0)]
                  + _halo_specs(rows, D_MODEL, 0, n_tiles, tix)
                  + [_full((RG_HEADS, RG_HEAD_DIM, 2 * RG_HEAD_DIM)), _full((1, D_MODEL))]
                  + ([] if add_dua is None else [row_spec(0)])),
        out_specs=(row_spec(0), _full((RG_HEADS, RG_HEAD_DIM, 2 * RG_HEAD_DIM)), _full((2, D_MODEL)),
                   _full((1, D_MODEL))),
        scratch_shapes=[pltpu.VMEM((SUBLANES, D_MODEL), F32)],
        compiler_params=_params("arbitrary"),
    )(acts, proj, dycat, h_dir, h_dir, h_dir, gate_w, lam, *([] if add_dua is None else [add_dua]))


def _extend_cols(refs, block, is_first, is_last):
    cols = slice(block * D_MODEL, (block + 1) * D_MODEL)
    prev_ref, main_ref, next_ref = refs
    p = jnp.where(is_first, 0.0, prev_ref[:, cols].astype(F32))
    n = jnp.where(is_last, 0.0, next_ref[:, cols].astype(F32))
    return jnp.concatenate([p, main_ref[:, cols].astype(F32), n], axis=0)


def even_mix_fwd(proj, h_f, h_b, sc_w, name):
    rows_total = proj.shape[0]
    rows = min(2 * MIX_TILE, rows_total)
    n_tiles = rows_total // rows
    ident = lambda i: i

    def body(za_ref, hf_ref, hb_ref, xbp, xbm, xbn, gcp, gcm, gcn, gb_ref, zb_ref, w_ref, y_ref):
        t = pl.program_id(0)
        first, last = t == 0, t == n_tiles - 1
        za = za_ref[...].astype(F32)
        y_ref[:, 0:D_MODEL] = ((hf_ref[...] + hb_ref[...]) * (za * _sigmoid(za))).astype(BF16)
        p_ext = _extend(xbp, xbm, xbn, first, last) * _extend(gcp, gcm, gcn, first, last)
        cv = _conv(p_ext, w_ref[...], 1, rows)
        zb = zb_ref[...].astype(F32)
        y_ref[:, D_MODEL:2 * D_MODEL] = (gb_ref[...].astype(F32) * cv * (zb * _sigmoid(zb))).astype(BF16)

    blk = lambda col: pl.BlockSpec((rows, D_MODEL), lambda i: (i, col))
    return pl.pallas_call(
        body, name=name,
        out_shape=jax.ShapeDtypeStruct((rows_total, 2 * D_MODEL), BF16),
        grid=(n_tiles,),
        in_specs=([blk(1), blk(0), blk(0)] + _halo_specs(rows, D_MODEL, 2, n_tiles, ident)
                  + _halo_specs(rows, D_MODEL, 4, n_tiles, ident) + [blk(3), blk(5), _full((3, D_MODEL))]),
        out_specs=pl.BlockSpec((rows, 2 * D_MODEL), lambda i: (i, 0)),
        compiler_params=_params("parallel"),
    )(proj, h_f, h_b, proj, proj, proj, proj, proj, proj, proj, proj, sc_w)


def even_mix_bwd(proj, dycat, h_f, h_b, dua, conv_w, sc_w, name):
    rows_total, width = proj.shape
    rows = min(MIX_TILE, rows_total)
    n_tiles = rows_total // rows
    ident = lambda i: i

    def body(pp, pm, pn, dyp, dym, dyn, hf_ref, hb_ref, dup, dum, dun, cw_ref, sw_ref,
             dp_ref, dcw_ref, dcb_ref, dsw_ref):
        def put(k, value):
            dp_ref[:, k * D_MODEL:(k + 1) * D_MODEL] = value.astype(BF16)

        t = pl.program_id(0)
        first, last = t == 0, t == n_tiles - 1
        proj_ext = lambda k: _extend_cols((pp, pm, pn), k, first, last)
        mid = slice(HALO, HALO + rows)
        za = pm[:, D_MODEL:2 * D_MODEL].astype(F32)
        sa = _sigmoid(za)
        put(1, dym[:, 0:D_MODEL] * (hf_ref[...] + hb_ref[...]) * (sa * (1.0 + za * (1.0 - sa))))
        dua_ext = _extend(dup, dum, dun, first, last)
        cw = cw_ref[...]
        put(0, _conv_transpose(dua_ext, cw, 2, rows))
        dua_mid = dua_ext[mid]
        xa_ext = proj_ext(0)
        dcw = jnp.concatenate([_colsum(dua_mid * _shifted(xa_ext, k - 2, rows)) for k in range(4)], axis=0)
        dcb = _colsum(dua_mid)
        xb_ext, gb_ext, gc_ext, zb_ext = proj_ext(2), proj_ext(3), proj_ext(4), proj_ext(5)
        p_ext = xb_ext * gc_ext
        sb_ext = _sigmoid(zb_ext)
        dyb_ext = _extend_cols((dyp, dym, dyn), 1, first, last)
        dcv_ext = dyb_ext * gb_ext * (zb_ext * sb_ext)
        sw = sw_ref[...]
        p_at = [_shifted(p_ext, k - 1, rows) for k in range(3)]
        cv = (p_at[0] * sw[0:1] + p_at[1] * sw[1:2]) + p_at[2] * sw[2:3]
        zb, sb, dyb, gb = zb_ext[mid], sb_ext[mid], dyb_ext[mid], gb_ext[mid]
        put(3, dyb * cv * (zb * sb))
        put(5, dyb * gb * cv * (sb * (1.0 + zb * (1.0 - sb))))
        dp = _conv_transpose(dcv_ext, sw, 1, rows)
        put(4, dp * xb_ext[mid])
        put(2, dp * gc_ext[mid])
        dcv = dcv_ext[mid]
        dsw = jnp.concatenate([_colsum(dcv * p_at[k]) for k in range(3)], axis=0)
        _accumulate(dcw_ref, dcw, t)
        _accumulate(dcb_ref, dcb, t)
        _accumulate(dsw_ref, dsw, t)

    own = pl.BlockSpec((rows, D_MODEL), lambda i: (i, 0))
    return pl.pallas_call(
        body, name=name,
        out_shape=(jax.ShapeDtypeStruct((rows_total, 6 * D_MODEL), BF16),
                   jax.ShapeDtypeStruct((4, D_MODEL), F32), jax.ShapeDtypeStruct((1, D_MODEL), F32),
                   jax.ShapeDtypeStruct((3, D_MODEL), F32)),
        grid=(n_tiles,),
        in_specs=(_halo_specs(rows, width, 0, n_tiles, ident) + _halo_specs(rows, 2 * D_MODEL, 0, n_tiles, ident)
                  + [own, own] + _halo_specs(rows, D_MODEL, 0, n_tiles, ident)
                  + [_full((4, D_MODEL)), _full((3, D_MODEL))]),
        out_specs=(pl.BlockSpec((rows, 6 * D_MODEL), lambda i: (i, 0)), _full((4, D_MODEL)), _full((1, D_MODEL)),
                   _full((3, D_MODEL))),
        compiler_params=_params("arbitrary"),
    )(proj, proj, proj, dycat, dycat, dycat, h_f, h_b, dua, dua, dua, conv_w, sc_w)


def even_out_fwd(ycat, w_out, gain, x, name):
    rows, d = x.shape
    k = ycat.shape[1]
    tm = min(ROW_TILE, rows)

    def body(yc_ref, w_ref, g_ref, x_ref, x1_ref, y_ref):
        y = _dot(yc_ref[...], w_ref[...])
        y_ref[...] = y
        rstd = lax.rsqrt(jnp.mean(y * y, axis=-1, keepdims=True) + NORM_EPS)
        x1_ref[...] = x_ref[...] + y * rstd * g_ref[...]

    row = lambda n: pl.BlockSpec((tm, n), lambda i: (i, 0))
    return pl.pallas_call(
        body, name=name,
        out_shape=(jax.ShapeDtypeStruct((rows, d), F32),) * 2,
        grid=(rows // tm,),
        in_specs=[row(k), _full((k, d)), _full((1, d)), row(d)],
        out_specs=(row(d), row(d)),
        compiler_params=_params("parallel"),
    )(ycat, w_out, gain, x)


def _rmsnorm_bwd(dout, y, gain):
    rstd = lax.rsqrt(jnp.mean(y * y, axis=-1, keepdims=True) + NORM_EPS)
    yhat = y * rstd
    dyn = dout * gain
    dy = rstd * (dyn - yhat * jnp.mean(dyn * yhat, axis=-1, keepdims=True))
    return dy, dout * yhat


def even_out_bwd(dx1, y, gain, w_out, name):
    rows, d = y.shape
    k = w_out.shape[0]
    tm = min(ROW_TILE, rows)

    def body(dx_ref, y_ref, g_ref, w_ref, dy_ref, dyc_ref, dg_ref):
        dy, dg_rows = _rmsnorm_bwd(dx_ref[...], y_ref[...], g_ref[...])
        dyb = dy.astype(BF16)
        dy_ref[...] = dyb
        dyc_ref[...] = _dot_nt(dyb, w_ref[...])
        _accumulate(dg_ref, _colsum(dg_rows), pl.program_id(0))

    row = lambda n: pl.BlockSpec((tm, n), lambda i: (i, 0))
    return pl.pallas_call(
        body, name=name,
        out_shape=(jax.ShapeDtypeStruct((rows, d), BF16), jax.ShapeDtypeStruct((rows, k), F32),
                   jax.ShapeDtypeStruct((1, d), F32)),
        grid=(rows // tm,),
        in_specs=[row(d), row(d), _full((1, d)), _full((k, d))],
        out_specs=(row(d), row(k), _full((1, d))),
        compiler_params=_params("arbitrary"),
    )(dx1, y, gain, w_out)


def _chunk_cumsum(g, reverse):
    n, c = g.shape
    chunks, per = n // GLA_CHUNK, GLA_CHUNK // SUBLANES
    g = g.reshape(n // SUBLANES, SUBLANES, c)
    pos = lax.broadcasted_iota(jnp.int32, (1, SUBLANES, c), 1)
    s = 1
    while s < SUBLANES:
        if reverse:
            g = g + jnp.where(pos < SUBLANES - s, pltpu.roll(g, SUBLANES - s, 1), 0.0)
        else:
            g = g + jnp.where(pos >= s, pltpu.roll(g, s, 1), 0.0)
        s *= 2
    g = g.reshape(chunks, per, SUBLANES, c)
    out, carry = [None] * per, None
    for k in (range(per - 1, -1, -1) if reverse else range(per)):
        out[k] = g[:, k] if carry is None else g[:, k] + carry
        carry = out[k][:, 0:1] if reverse else out[k][:, SUBLANES - 1:SUBLANES]
    return jnp.stack(out, axis=1).reshape(n, c)


def _gla_prepare(q_ref, k_ref, lr_ref, wg_ref, bg_ref, reverse, n_chunks):
    z = _dot(lr_ref[...].astype(BF16), wg_ref[0]) + bg_ref[0]
    g = -_softplus(-z) * (1.0 / GLA_NORMALIZER)
    bcum = _chunk_cumsum(g, reverse).reshape(n_chunks, GLA_CHUNK, GLA_DK)
    edge = 0 if reverse else GLA_CHUNK - 1
    btot = bcum[:, edge:edge + 1, :]
    e_pos = jnp.exp(bcum)
    e_neg = jnp.exp(-bcum)
    e_st = jnp.exp(btot - bcum)
    q3 = q_ref[...].reshape(n_chunks, GLA_CHUNK, GLA_DK)
    k3 = k_ref[...].reshape(n_chunks, GLA_CHUNK, GLA_DK)
    scale = GLA_DK ** -0.5
    q_in = q3 * scale * e_pos
    k_in = k3 * e_neg
    k_st = k3 * e_st
    dec = jnp.exp(btot)
    return z, q_in, k_in, k_st, dec, (scale * e_pos, e_neg, e_st)


def _gla_mask(reverse):
    i = lax.broadcasted_iota(jnp.int32, (GLA_CHUNK, GLA_CHUNK), 0)
    j = lax.broadcasted_iota(jnp.int32, (GLA_CHUNK, GLA_CHUNK), 1)
    return (j >= i) if reverse else (j <= i)


def _gla_specs(rows, n_blocks, reverse):
    tix = (lambda s: n_blocks - 1 - s) if reverse else (lambda s: s)
    d = 1 if reverse else 0
    lr_block = LR_COL // LANES
    specs = [pl.BlockSpec((rows, GLA_DK), lambda h, s: (tix(s), h)),
             pl.BlockSpec((rows, GLA_DK), lambda h, s: (tix(s), GLA_HEADS + h)),
             pl.BlockSpec((rows, GLA_DV), lambda h, s: (tix(s), GLA_HEADS + h)),
             pl.BlockSpec((rows, LANES), lambda h, s: (tix(s), lr_block)),
             pl.BlockSpec((1, LANES, GLA_DK), lambda h, s: (d, 0, h)),
             pl.BlockSpec((1, 1, GLA_DK), lambda h, s: (d, 0, h))]
    return specs, tix


def gla_fwd(proj, wg_pad, bg, add_o, reverse, name):
    rows_total = proj.shape[0]
    rows = min(GLA_BLOCK, rows_total)
    n_blocks = rows_total // rows
    n_chunks = rows // GLA_CHUNK
    specs, tix = _gla_specs(rows, n_blocks, reverse)

    def body(q_ref, k_ref, v_ref, lr_ref, wg_ref, bg_ref, *rest):
        o_ref, st_ref, state, kv_scr, dec_scr = rest[-5:]
        _, q_in, k_in, k_st, dec, _ = _gla_prepare(q_ref, k_ref, lr_ref, wg_ref, bg_ref, reverse, n_chunks)
        vb = v_ref[...].reshape(n_chunks, GLA_CHUNK, GLA_DV).astype(BF16)
        qb = q_in.astype(BF16)
        p = jnp.where(_gla_mask(reverse), _bdot(qb, k_in.astype(BF16), 2, 2), 0.0)
        o = _bdot(p.astype(BF16), vb, 2, 1)
        kv_scr[...] = _bdot(vb, k_st.astype(BF16), 1, 1)
        dec_scr[...] = jnp.broadcast_to(dec, dec_scr.shape)

        @pl.when(pl.program_id(1) == 0)
        def _():
            state[...] = jnp.zeros_like(state)

        for c in range(n_chunks):
            cc = n_chunks - 1 - c if reverse else c
            st_ref[0, cc] = state[...]
            state[...] = state[...] * dec_scr[cc, 0:1] + kv_scr[cc]
        o = o + _bdot(qb, st_ref[0].astype(BF16), 2, 2)
        o = o.reshape(rows, GLA_DV)
        o_ref[...] = o if add_o is None else o + rest[0][...]

    o_spec = pl.BlockSpec((rows, GLA_DV), lambda h, s: (tix(s), h))
    return pl.pallas_call(
        body, name=name,
        out_shape=(jax.ShapeDtypeStruct((rows_total, GLA_HEADS * GLA_DV), F32),
                   jax.ShapeDtypeStruct((GLA_HEADS, rows_total // GLA_CHUNK, GLA_DV, GLA_DK), F32)),
        grid=(GLA_HEADS, n_blocks),
        in_specs=specs + ([] if add_o is None else [o_spec]),
        out_specs=(o_spec,
                   pl.BlockSpec((1, n_chunks, GLA_DV, GLA_DK), lambda h, s: (h, tix(s), 0, 0))),
        scratch_shapes=[pltpu.VMEM((GLA_DV, GLA_DK), F32), pltpu.VMEM((n_chunks, GLA_DV, GLA_DK), F32),
                        pltpu.VMEM((n_chunks, SUBLANES, GLA_DK), F32)],
        compiler_params=_params("parallel", "arbitrary"),
    )(proj, proj, proj, proj, wg_pad, bg, *([] if add_o is None else [add_o]))


def gla_bwd(proj, wg_pad, bg, d_o, states, dqkv_in, reverse, name):
    rows_total = proj.shape[0]
    rows = min(GLA_BLOCK, rows_total)
    n_blocks = rows_total // rows
    n_chunks = rows // GLA_CHUNK
    specs, tix = _gla_specs(rows, n_blocks, not reverse)
    d = 1 if reverse else 0
    specs[4] = pl.BlockSpec((1, LANES, GLA_DK), lambda h, s: (d, 0, h))
    specs[5] = pl.BlockSpec((1, 1, GLA_DK), lambda h, s: (d, 0, h))
    add = dqkv_in is not None

    def body(*refs):
        q_ref, k_ref, v_ref, lr_ref, wg_ref, bg_ref, do_ref, st_ref = refs[:8]
        refs = refs[8:]
        if add:
            aq_ref, ak_ref, av_ref = refs[:3]
            refs = refs[3:]
        dq_ref, dk_ref, dv_ref, dz_ref, dstate, g_scr, dec_scr, dsn_scr = refs
        z, q_in, k_in, k_st, dec, (f_q, f_k, f_s) = _gla_prepare(q_ref, k_ref, lr_ref, wg_ref, bg_ref, reverse,
                                                                 n_chunks)
        mask = _gla_mask(reverse)
        vb = v_ref[...].reshape(n_chunks, GLA_CHUNK, GLA_DV).astype(BF16)
        dob = do_ref[...].reshape(n_chunks, GLA_CHUNK, GLA_DV).astype(BF16)
        qb, kb, ksb = q_in.astype(BF16), k_in.astype(BF16), k_st.astype(BF16)
        st = st_ref[0]
        stb = st.astype(BF16)
        pb = jnp.where(mask, _bdot(qb, kb, 2, 2), 0.0).astype(BF16)
        dpb = jnp.where(mask, _bdot(dob, vb, 2, 2), 0.0).astype(BF16)
        d_qin = _bdot(dpb, kb, 2, 1) + _bdot(dob, stb, 2, 1)
        d_kin = _bdot(dpb, qb, 1, 1)
        dv = _bdot(pb, dob, 1, 1)
        g_scr[...] = _bdot(dob, qb, 1, 1)
        dec_scr[...] = jnp.broadcast_to(dec, dec_scr.shape)

        @pl.when(pl.program_id(1) == 0)
        def _():
            dstate[...] = jnp.zeros_like(dstate)

        for c in range(n_chunks):
            cc = c if reverse else n_chunks - 1 - c
            dsn_scr[cc] = dstate[...]
            dstate[...] = dstate[...] * dec_scr[cc, 0:1] + g_scr[cc]
        dsn = dsn_scr[...]
        dsnb = dsn.astype(BF16)
        dv = dv + _bdot(ksb, dsnb, 2, 2)
        d_kst = _bdot(vb, dsnb, 2, 1)
        d_dec = jnp.sum(dsn * st, axis=1, keepdims=True)
        ks_term = d_kst * k_st
        d_btot = d_dec * dec + jnp.sum(ks_term, axis=1, keepdims=True)
        d_b = d_qin * q_in - d_kin * k_in - ks_term
        pos = lax.broadcasted_iota(jnp.int32, d_b.shape, 1)
        edge = 0 if reverse else GLA_CHUNK - 1
        d_b = d_b + jnp.where(pos == edge, d_btot, 0.0)
        dg = _chunk_cumsum(d_b.reshape(rows, GLA_DK), not reverse)
        dz_ref[...] = dg * (1.0 / GLA_NORMALIZER) * _sigmoid(-z)
        dq = (d_qin * f_q).reshape(rows, GLA_DK)
        dk = (d_kin * f_k + d_kst * f_s).reshape(rows, GLA_DK)
        dv = dv.reshape(rows, GLA_DV)
        if add:
            dq_ref[...] = (dq + aq_ref[...]).astype(BF16)
            dk_ref[...] = (dk + ak_ref[...]).astype(BF16)
            dv_ref[...] = (dv + av_ref[...]).astype(BF16)
        else:
            dq_ref[...] = dq
            dk_ref[...] = dk
            dv_ref[...] = dv

    qkv_specs = [pl.BlockSpec((rows, GLA_DK), lambda h, s: (tix(s), h)),
                 pl.BlockSpec((rows, GLA_DK), lambda h, s: (tix(s), h)),
                 pl.BlockSpec((rows, GLA_DV), lambda h, s: (tix(s), h))]
    in_specs = specs + [pl.BlockSpec((rows, GLA_DV), lambda h, s: (tix(s), h)),
                        pl.BlockSpec((1, n_chunks, GLA_DV, GLA_DK), lambda h, s: (h, tix(s), 0, 0))]
    args = [proj, proj, proj, proj, wg_pad, bg, d_o, states]
    out_dtype = F32
    if add:
        in_specs += qkv_specs
        args += list(dqkv_in)
        out_dtype = BF16
    return pl.pallas_call(
        body, name=name,
        out_shape=(jax.ShapeDtypeStruct((rows_total, GLA_HEADS * GLA_DK), out_dtype),
                   jax.ShapeDtypeStruct((rows_total, GLA_HEADS * GLA_DK), out_dtype),
                   jax.ShapeDtypeStruct((rows_total, GLA_HEADS * GLA_DV), out_dtype),
                   jax.ShapeDtypeStruct((rows_total, GLA_HEADS * GLA_DK), F32)),
        grid=(GLA_HEADS, n_blocks),
        in_specs=in_specs,
        out_specs=(pl.BlockSpec((rows, GLA_DK), lambda h, s: (tix(s), h)),
                   pl.BlockSpec((rows, GLA_DK), lambda h, s: (tix(s), h)),
                   pl.BlockSpec((rows, GLA_DV), lambda h, s: (tix(s), h)),
                   pl.BlockSpec((rows, GLA_DK), lambda h, s: (tix(s), h))),
        scratch_shapes=[pltpu.VMEM((GLA_DV, GLA_DK), F32), pltpu.VMEM((n_chunks, GLA_DV, GLA_DK), F32),
                        pltpu.VMEM((n_chunks, SUBLANES, GLA_DK), F32),
                        pltpu.VMEM((n_chunks, GLA_DV, GLA_DK), F32)],
        compiler_params=_params("parallel", "arbitrary"),
    )(*args)


def gla_gate_bwd(proj, dz_f, dz_b, wg_pad, name):
    rows_total = proj.shape[0]
    tm = min(ROW_TILE, rows_total)
    n_key = GLA_HEADS * GLA_DK

    def body(lr_ref, dzf_ref, dzb_ref, wg_ref, dlr_ref, dwg_ref, dbg_ref):
        step = pl.program_id(0)
        lr_t = jnp.transpose(lr_ref[...])
        dzf, dzb = dzf_ref[...], dzb_ref[...]
        dzf16, dzb16 = dzf.astype(BF16), dzb.astype(BF16)
        dlr_ref[...] = (_dot_nt(dzf16, wg_ref[0]) + _dot_nt(dzb16, wg_ref[1])).astype(BF16)
        dwf = _dot(lr_t[0:GLA_RANK].astype(BF16), dzf16)
        dwb = _dot(lr_t[GLA_RANK:2 * GLA_RANK].astype(BF16), dzb16)
        dbg = jnp.concatenate([_colsum(dzf), _colsum(dzb)], axis=0)

        @pl.when(step == 0)
        def _():
            dwg_ref[0] = dwf
            dwg_ref[1] = dwb
            dbg_ref[...] = dbg

        @pl.when(step > 0)
        def _():
            dwg_ref[0] += dwf
            dwg_ref[1] += dwb
            dbg_ref[...] += dbg

    return pl.pallas_call(
        body, name=name,
        out_shape=(jax.ShapeDtypeStruct((rows_total, LANES), BF16), jax.ShapeDtypeStruct((2, GLA_RANK, n_key), F32),
                   jax.ShapeDtypeStruct((2, n_key), F32)),
        grid=(rows_total // tm,),
        in_specs=[pl.BlockSpec((tm, LANES), lambda i: (i, LR_COL // LANES)),
                  pl.BlockSpec((tm, n_key), lambda i: (i, 0)), pl.BlockSpec((tm, n_key), lambda i: (i, 0)),
                  _full((2, LANES, n_key))],
        out_specs=(pl.BlockSpec((tm, LANES), lambda i: (i, 0)), _full((2, GLA_RANK, n_key)), _full((2, n_key))),
        compiler_params=_params("arbitrary"),
    )(proj, dz_f, dz_b, wg_pad)


def _head_norm(o, gain):
    outs, hats, rstds = [], [], []
    for h in range(GLA_HEADS):
        oh = o[:, h * GLA_DV:(h + 1) * GLA_DV]
        rstd = lax.rsqrt(jnp.mean(oh * oh, axis=-1, keepdims=True) + NORM_EPS)
        hat = oh * rstd
        outs.append(hat * gain)
        hats.append(hat)
        rstds.append(rstd)
    return outs, hats, rstds


def odd_out_fwd(o, proj, head_gain, w_out, gain, x1, target, name):
    rows, d = x1.shape
    tm = min(ROW_TILE, rows)
    r_block = (2 * GLA_HEADS * GLA_DK + GLA_HEADS * GLA_DV) // d

    def body(o_ref, r_ref, hg_ref, w_ref, g_ref, x1_ref, tgt_ref, y2_ref, dy_ref, dx2_ref, loss_ref, dg_ref):
        step = pl.program_id(0)
        on, _, _ = _head_norm(o_ref[...], hg_ref[...])
        r = r_ref[...]
        y2 = (jnp.concatenate(on, axis=1) * (r * _sigmoid(r))).astype(BF16)
        y2_ref[...] = y2
        y = _dot(y2, w_ref[...])
        gain_v = g_ref[...]
        rstd = lax.rsqrt(jnp.mean(y * y, axis=-1, keepdims=True) + NORM_EPS)
        x2 = x1_ref[...] + y * rstd * gain_v
        diff = x2 - tgt_ref[...]
        loss = 0.5 * jnp.sum(jnp.mean(diff * diff, axis=-1, keepdims=True), axis=0, keepdims=True)
        dx2 = diff * (1.0 / d)
        dx2_ref[...] = dx2
        dy, dg_rows = _rmsnorm_bwd(dx2, y, gain_v)
        dy_ref[...] = dy.astype(BF16)
        _accumulate(loss_ref, jnp.broadcast_to(loss, loss_ref.shape), step)
        _accumulate(dg_ref, _colsum(dg_rows), step)

    row = lambda n, col=0: pl.BlockSpec((tm, n), lambda i: (i, col))
    return pl.pallas_call(
        body, name=name,
        out_shape=(jax.ShapeDtypeStruct((rows, d), BF16), jax.ShapeDtypeStruct((rows, d), BF16),
                   jax.ShapeDtypeStruct((rows, d), F32), jax.ShapeDtypeStruct((SUBLANES, LANES), F32),
                   jax.ShapeDtypeStruct((1, d), F32)),
        grid=(rows // tm,),
        in_specs=[row(d), row(d, r_block), _full((1, GLA_DV)), _full((d, d)), _full((1, d)), row(d), row(d)],
        out_specs=(row(d), row(d), row(d), _full((SUBLANES, LANES)), _full((1, d))),
        compiler_params=_params("arbitrary"),
    )(o, proj, head_gain, w_out, gain, x1, target)


def odd_out_bwd(dy, w_out, o, proj, head_gain, name):
    rows, d = dy.shape
    tm = min(ROW_TILE, rows)
    r_block = (2 * GLA_HEADS * GLA_DK + GLA_HEADS * GLA_DV) // d

    def body(dy_ref, w_ref, o_ref, r_ref, hg_ref, dr_ref, do_ref, dhg_ref):
        dy2 = _dot_nt(dy_ref[...], w_ref[...])
        hg = hg_ref[...]
        on, hats, rstds = _head_norm(o_ref[...], hg)
        r = r_ref[...]
        sr = _sigmoid(r)
        dr_ref[...] = (dy2 * jnp.concatenate(on, axis=1) * (sr * (1.0 + r * (1.0 - sr)))).astype(BF16)
        d_on = dy2 * (r * sr)
        d_os, dhg = [], None
        for h in range(GLA_HEADS):
            dn = d_on[:, h * GLA_DV:(h + 1) * GLA_DV]
            part = _colsum(dn * hats[h])
            dhg = part if dhg is None else dhg + part
            dng = dn * hg
            d_os.append(rstds[h] * (dng - hats[h] * jnp.mean(dng * hats[h], axis=-1, keepdims=True)))
        do_ref[...] = jnp.concatenate(d_os, axis=1)
        _accumulate(dhg_ref, dhg, pl.program_id(0))

    row = lambda n, col=0: pl.BlockSpec((tm, n), lambda i: (i, col))
    return pl.pallas_call(
        body, name=name,
        out_shape=(jax.ShapeDtypeStruct((rows, d), BF16), jax.ShapeDtypeStruct((rows, d), F32),
                   jax.ShapeDtypeStruct((1, GLA_DV), F32)),
        grid=(rows // tm,),
        in_specs=[row(d), _full((d, d)), row(d), row(d, r_block), _full((1, GLA_DV))],
        out_specs=(row(d), row(d), _full((1, GLA_DV))),
        compiler_params=_params("arbitrary"),
    )(dy, w_out, o, proj, head_gain)


def local_step(x, target, w, reduce_first=None, reduce_second=None, late_weights=None):
    g, g16 = {}, {}
    proj_e, h0 = norm_matmul(x, w["even_norm_pre"], w["even_w_in"], BF16, "even_in_proj")
    h_dir, acts = zip(*[rglru_fwd(proj_e, w["rg_conv_w"], w["rg_conv_b"], w["rg_gate_w"][d], w["rg_gate_b"][d],
                                  w["rg_lambda"][d], d == 1, "rglru_fwd_%d" % d) for d in range(2)])
    ycat = even_mix_fwd(proj_e, h_dir[0], h_dir[1], w["sc_conv_w"], "even_mix_fwd")
    if late_weights is not None:
        w = dict(w, **late_weights(ycat))
    x1, y_e = even_out_fwd(ycat, w["even_w_out"], w["even_norm_post"], x, "even_out_fwd")
    proj_o, h1 = norm_matmul(x1, w["odd_norm_pre"], w["odd_w_in"], F32, "odd_in_proj")
    o, st_dir = None, []
    for d in range(2):
        o, st = gla_fwd(proj_o, w["gla_wg_pad"], w["gla_b_gate"], o, d == 1, "gla_fwd_%d" % d)
        st_dir.append(st)
    y2, dy_o, dx2, loss, g["odd_norm_post"] = odd_out_fwd(
        o, proj_o, w["gla_norm_g"], w["odd_w_out"], w["odd_norm_post"], x1, target, "odd_out_fwd")
    g["odd_w_out"], g16["odd_w_out"] = (a[0] for a in matmul_dw(y2, dy_o, D_MODEL, "odd_w_out_grad"))
    dr, d_o, g["gla_norm_g"] = odd_out_bwd(dy_o, w["odd_w_out"], o, proj_o, w["gla_norm_g"], "odd_out_bwd")
    dq, dk, dv, dz_f = gla_bwd(proj_o, w["gla_wg_pad"], w["gla_b_gate"], d_o, st_dir[0], None, False, "gla_bwd_0")
    dq, dk, dv, dz_b = gla_bwd(proj_o, w["gla_wg_pad"], w["gla_b_gate"], d_o, st_dir[1], (dq, dk, dv), True,
                               "gla_bwd_1")
    dlr, g["gla_w_gate_lr"], g["gla_b_gate"] = gla_gate_bwd(proj_o, dz_f, dz_b, w["gla_wg_pad"], "gla_gate_bwd")
    dproj_o = [dq, dk, dv, dr, dlr]
    g["odd_w_in"] = matmul_dw_pieces(h1, dproj_o, "odd_w_in_grad")[:, :ODD_IN]
    dx1, g["odd_norm_pre"] = inproj_bwd_pieces(dproj_o, w["odd_w_in"], x1, w["odd_norm_pre"], dx2, "odd_in_proj_bwd")
    dy_e, dycat, g["even_norm_post"] = even_out_bwd(dx1, y_e, w["even_norm_post"], w["even_w_out"], "even_out_bwd")
    g["even_w_out"], g16["even_w_out"] = (a[0] for a in matmul_dw(ycat, dy_e, D_MODEL, "even_w_out_grad"))
    lam = w["rg_lambda"] if reduce_first is None else w["rg_lambda"] + reduce_first(g, g16)
    dua, dgw, dgb, dlam = None, [], [], []
    for d in range(2):
        a, b, c, e = rglru_bwd(proj_e, dycat, h_dir[d], acts[d], w["rg_gate_w"][d], lam[d], dua, d == 1,
                               "rglru_bwd_%d" % d)
        dua = a
        dgw.append(b)
        dgb.append(c)
        dlam.append(e)
    dproj_e, g["rg_conv_w"], g["rg_conv_b"], g["sc_conv_w"] = even_mix_bwd(
        proj_e, dycat, h_dir[0], h_dir[1], dua, w["rg_conv_w"], w["sc_conv_w"], "even_mix_bwd")
    dgw = jnp.stack(dgw).reshape(2, RG_HEADS, RG_HEAD_DIM, 2, RG_HEAD_DIM)
    g["rg_gate_w"] = jnp.transpose(dgw, (0, 3, 1, 2, 4))
    g["rg_gate_b"] = jnp.stack(dgb).reshape(2, 2, RG_HEADS, RG_HEAD_DIM)
    g["rg_lambda"] = jnp.concatenate(dlam, axis=0)
    g["even_w_in"], g16["even_w_in"] = matmul_dw(h0, dproj_e, EVEN_IN // 4, "even_w_in_grad")
    gain = w["even_norm_pre"] if reduce_second is None else w["even_norm_pre"] + reduce_second(g, g16)
    grad_x, g["even_norm_pre"] = inproj_bwd(dproj_e, w["even_w_in"], x, gain, dx1, "even_in_proj_bwd")
    return loss, grad_x, g


def _prepare_weights(full):
    w = {}
    for name in ("even_norm_pre", "even_norm_post", "rg_conv_b", "odd_norm_pre", "odd_norm_post", "gla_norm_g"):
        if name in full:
            w[name] = full[name].reshape(1, -1)
    for name in ("rg_conv_w", "sc_conv_w"):
        if name in full:
            w[name] = full[name]
    for name in ("even_w_out", "odd_w_out"):
        if name in full:
            w[name] = full[name].astype(BF16)
    if "even_w_in" in full:
        w["even_w_in"] = full["even_w_in"].astype(BF16)
        if w["even_w_in"].ndim == 2:
            w["even_w_in"] = jnp.transpose(w["even_w_in"].reshape(D_MODEL, 4, EVEN_IN // 4), (1, 0, 2))
    if "rg_gate_w" in full:
        gw = jnp.transpose(full["rg_gate_w"].astype(BF16), (0, 2, 3, 1, 4))
        w["rg_gate_w"] = gw.reshape(2, RG_HEADS, RG_HEAD_DIM, 2 * RG_HEAD_DIM)
        w["rg_gate_b"] = full["rg_gate_b"].reshape(2, 2, D_MODEL)
        w["rg_lambda"] = full["rg_lambda"].reshape(2, 1, D_MODEL)
    if "odd_w_in" in full:
        w_in = jnp.pad(full["odd_w_in"].astype(BF16), ((0, 0), (0, ODD_IN_PAD - ODD_IN)))
        w["odd_w_in"] = w_in.reshape(1, D_MODEL, ODD_IN_PAD)
    if "gla_w_gate_lr" in full:
        wg = full["gla_w_gate_lr"].astype(BF16)
        w["gla_wg_pad"] = jnp.stack([jnp.pad(wg[d], ((d * GLA_RANK, LANES - (d + 1) * GLA_RANK), (0, 0)))
                                     for d in range(2)])
        w["gla_b_gate"] = full["gla_b_gate"].reshape(2, 1, GLA_HEADS * GLA_DK)
    return w


SHARDED_SMALL = (("rg_conv_w", (4, 256)), ("rg_lambda", (2, 256)), ("sc_conv_w", (3, 256)),
                 ("odd_norm_pre", (256,)), ("odd_norm_post", (256,)), ("gla_w_gate_lr", (2, 16, 128)),
                 ("gla_b_gate", (2, 128)), ("gla_norm_g", (64,)))
SHARDED_ROWS = 96
REPLICATED = (("rg_gate_w", (2, 2, 8, 128, 128)), ("even_norm_post", (1024,)), ("rg_conv_b", (1024,)),
              ("rg_gate_b", (2, 2, 8, 128)))
GATE_ROWS = 4096
LAST_REPLICATED = (("even_norm_pre", (1024,)),)
LAST_ROWS = 8
REPLICATED_ROWS = 4160
REP_PART = REPLICATED_ROWS // 8
HALF_SHARDED = SHARDED_ROWS // 2
PACK_HALF = HALF_SHARDED + REP_PART


def _seg_rows(shape):
    n = 1
    for s in shape:
        n *= s
    return -(-n // (SUBLANES * LANES)) * SUBLANES


def _pack(arrays, spec, total_rows, lead=()):
    parts = []
    for name, shape in spec:
        flat = arrays[name].reshape(lead + (-1,))
        pad = _seg_rows(shape) * LANES - flat.shape[-1]
        if pad:
            flat = jnp.pad(flat, [(0, 0)] * len(lead) + [(0, pad)])
        parts.append(flat.reshape(lead + (-1, LANES)))
    rows = jnp.concatenate(parts, axis=len(lead))
    pad = total_rows - rows.shape[len(lead)]
    return jnp.pad(rows, [(0, 0)] * len(lead) + [(0, pad), (0, 0)])


def _unpack(rows, spec, lead=()):
    out, at = {}, 0
    for name, shape in spec:
        n = 1
        for s in shape:
            n *= s
        k = _seg_rows(shape)
        seg = lax.slice_in_dim(rows, at, at + k, axis=len(lead)).reshape(lead + (-1,))
        out[name] = lax.slice_in_dim(seg, 0, n, axis=len(lead)).reshape(lead + shape)
        at += k
    return out


def _split_owners(arr):
    a = arr.reshape(arr.shape[:-1] + (4, arr.shape[-1] // 4))
    return jnp.moveaxis(a, -2, 0)


def _merge_owners(arr):
    a = jnp.moveaxis(arr, 0, -2)
    return a.reshape(a.shape[:-2] + (-1,))


HBM_SPEC = pl.BlockSpec(memory_space=pltpu.HBM)


def _position():
    x, y, c = lax.axis_index("x"), lax.axis_index("y"), lax.axis_index("c")
    chips = [(1 - x, 1 - y), (1 - x, y), (x, 1 - y)]
    return x, y, c, chips


def _remote(src, dst, send_sem, recv_sem, device):
    return pltpu.make_async_remote_copy(src_ref=src, dst_ref=dst, send_sem=send_sem, recv_sem=recv_sem,
                                        device_id=device, device_id_type=MESH)


SEM_SPEC = pl.BlockSpec(memory_space=pltpu.SEMAPHORE)
SIDE_EFFECT = pltpu.SideEffectType.DATAFLOW_SIDE_EFFECTING


def _gather_copies(ins, lands, n_h, send_sems, recv_sems):
    x, y, c, chips = _position()
    me = 2 * x + y
    copies = []
    for a in range(len(ins)):
        for k, chip in enumerate(chips):
            src = ins[a].at[c] if a < n_h else ins[a]
            dst = lands[a].at[me, c] if a < n_h else lands[a].at[me]
            copies.append(_remote(src, dst, send_sems.at[3 * a + k], recv_sems.at[3 * a + k], (chip[0], chip[1], c)))
    return copies


def gather_start(halved, whole, name):
    arrays = list(halved) + list(whole)
    n, n_h = len(arrays), len(halved)
    lands = [lax.empty((4,) + a.shape, a.dtype) for a in arrays]

    def body(*refs):
        ins, lz, send_sems, recv_sems, token = refs[:n], refs[n:2 * n], refs[2 * n], refs[2 * n + 1], refs[-1]
        for cp in _gather_copies(ins, lz, n_h, send_sems, recv_sems):
            cp.start()
        token[...] = jnp.zeros_like(token)

    operands = [pltpu.with_memory_space_constraint(a, pltpu.HBM) for a in arrays + lands]
    return pl.pallas_call(
        body, name=name,
        out_shape=(pltpu.SemaphoreType.DMA((3 * n,)), pltpu.SemaphoreType.DMA((3 * n,)))
        + tuple(pltpu.HBM(a.shape, a.dtype) for a in operands) + (jax.ShapeDtypeStruct((SUBLANES, LANES), F32),),
        in_specs=[HBM_SPEC] * (2 * n),
        out_specs=(SEM_SPEC, SEM_SPEC) + (HBM_SPEC,) * (2 * n) + (pl.BlockSpec(memory_space=pltpu.VMEM),),
        input_output_aliases={i: 2 + i for i in range(2 * n)},
        compiler_params=pltpu.CompilerParams(has_side_effects=SIDE_EFFECT),
    )(*operands)


def gather_wait(started, n_h, after, name):
    send_sems, recv_sems = started[0], started[1]
    operands = list(started[2:-1])
    n = len(operands) // 2

    def body(*refs):
        ins, lz, send_ref, recv_ref = refs[:n], refs[n:2 * n], refs[2 * n], refs[2 * n + 1]
        for cp in _gather_copies(ins, lz, n_h, send_ref, recv_ref):
            cp.wait_send()
            cp.wait_recv()

    outs = pl.pallas_call(
        body, name=name,
        out_shape=tuple(pltpu.HBM(a.shape, a.dtype) for a in operands),
        in_specs=[HBM_SPEC] * (2 * n) + [SEM_SPEC, SEM_SPEC, pl.BlockSpec(memory_space=pl.ANY)],
        out_specs=(HBM_SPEC,) * (2 * n),
        input_output_aliases={i: i for i in range(2 * n)},
        compiler_params=pltpu.CompilerParams(has_side_effects=SIDE_EFFECT),
    )(*operands, send_sems, recv_sems, after)
    return outs[n:]


def pass_to_sibling(fulls, name):
    n = len(fulls)

    def body(*refs):
        bufs = refs[n:2 * n]
        send_sems, recv_sems = refs[2 * n:]
        x, y, c, chips = _position()
        sibling = (x, y, 1 - c)
        copies = []
        for a in range(n):
            for k, chip in enumerate(chips):
                q = 2 * chip[0] + chip[1]
                cp = _remote(bufs[a].at[q, c], bufs[a].at[q, c], send_sems.at[3 * a + k], recv_sems.at[3 * a + k],
                             sibling)
                cp.start()
                copies.append(cp)
        for a in range(n):
            for k, chip in enumerate(chips):
                q = 2 * chip[0] + chip[1]
                passed = bufs[a].at[q, 1 - c]
                _remote(passed, passed, send_sems.at[3 * a + k], recv_sems.at[3 * a + k], sibling).wait_recv()
        for cp in copies:
            cp.wait_send()

    return pl.pallas_call(
        body, name=name,
        out_shape=[jax.ShapeDtypeStruct(a.shape, a.dtype) for a in fulls],
        in_specs=[HBM_SPEC] * n, out_specs=[HBM_SPEC] * n,
        input_output_aliases={i: i for i in range(n)},
        scratch_shapes=[pltpu.SemaphoreType.DMA((3 * n,)), pltpu.SemaphoreType.DMA((3 * n,))],
    )(*fulls)


def place_own(full, own, chip, name):
    _, _, r, cols = full.shape
    tr = _row_tile(r, cols)

    def body(p_ref, own_ref, full_ref, o_ref):
        o_ref[0] = own_ref[...]

    return pl.pallas_call(
        body, name=name,
        out_shape=jax.ShapeDtypeStruct(full.shape, full.dtype),
        grid_spec=pltpu.PrefetchScalarGridSpec(
            num_scalar_prefetch=1, grid=(2, r // tr),
            in_specs=[pl.BlockSpec((1, tr, cols), lambda h, i, p_ref: (h, i, 0)), pl.BlockSpec(memory_space=pl.ANY)],
            out_specs=pl.BlockSpec((1, 1, tr, cols), lambda h, i, p_ref: (p_ref[0], h, i, 0))),
        input_output_aliases={2: 0},
        compiler_params=_params("parallel", "parallel"),
    )(chip, own, full)


def exchange_with_sibling(arrays, name):
    n = len(arrays)

    def body(*refs):
        ins, outs = refs[:n], refs[n:2 * n]
        send_sems, recv_sems = refs[2 * n:]
        x, y, c, _ = _position()
        copies = []
        for a in range(n):
            cp = _remote(ins[a].at[:, 1 - c], outs[a], send_sems.at[a], recv_sems.at[a], (x, y, 1 - c))
            cp.start()
            copies.append(cp)
        for cp in copies:
            cp.wait()

    return pl.pallas_call(
        body, name=name,
        out_shape=[jax.ShapeDtypeStruct((a.shape[0],) + a.shape[2:], a.dtype) for a in arrays],
        in_specs=[HBM_SPEC] * n, out_specs=[HBM_SPEC] * n,
        scratch_shapes=[pltpu.SemaphoreType.DMA((n,)), pltpu.SemaphoreType.DMA((n,))],
    )(*arrays)


def _chip_copies(ins, lands, send_sems, recv_sems):
    x, y, c, chips = _position()
    copies = []
    for a in range(len(ins)):
        for k, chip in enumerate(chips):
            q = 2 * chip[0] + chip[1]
            copies.append(_remote(ins[a].at[q], lands[a].at[k], send_sems.at[3 * a + k], recv_sems.at[3 * a + k],
                                  (chip[0], chip[1], c)))
    return copies


def exchange_with_chips_start(arrays, name):
    n = len(arrays)
    lands = [lax.empty((3,) + a.shape[1:], a.dtype) for a in arrays]

    def body(*refs):
        ins, lz, send_sems, recv_sems, token = refs[:n], refs[n:2 * n], refs[2 * n], refs[2 * n + 1], refs[-1]
        for cp in _chip_copies(ins, lz, send_sems, recv_sems):
            cp.start()
        token[...] = jnp.zeros_like(token)

    operands = [pltpu.with_memory_space_constraint(a, pltpu.HBM) for a in list(arrays) + lands]
    return pl.pallas_call(
        body, name=name,
        out_shape=(pltpu.SemaphoreType.DMA((3 * n,)), pltpu.SemaphoreType.DMA((3 * n,)))
        + tuple(pltpu.HBM(a.shape, a.dtype) for a in operands) + (jax.ShapeDtypeStruct((SUBLANES, LANES), F32),),
        in_specs=[HBM_SPEC] * (2 * n),
        out_specs=(SEM_SPEC, SEM_SPEC) + (HBM_SPEC,) * (2 * n) + (pl.BlockSpec(memory_space=pltpu.VMEM),),
        input_output_aliases={i: 2 + i for i in range(2 * n)},
        compiler_params=pltpu.CompilerParams(has_side_effects=SIDE_EFFECT),
    )(*operands)


def exchange_with_chips_wait(started, after, name):
    send_sems, recv_sems = started[0], started[1]
    operands = list(started[2:-1])
    n = len(operands) // 2

    def body(*refs):
        ins, lz, send_ref, recv_ref = refs[:n], refs[n:2 * n], refs[2 * n], refs[2 * n + 1]
        for cp in _chip_copies(ins, lz, send_ref, recv_ref):
            cp.wait_send()
            cp.wait_recv()

    outs = pl.pallas_call(
        body, name=name,
        out_shape=tuple(pltpu.HBM(a.shape, a.dtype) for a in operands),
        in_specs=[HBM_SPEC] * (2 * n) + [SEM_SPEC, SEM_SPEC, pl.BlockSpec(memory_space=pl.ANY)],
        out_specs=(HBM_SPEC,) * (2 * n),
        input_output_aliases={i: i for i in range(2 * n)},
        compiler_params=pltpu.CompilerParams(has_side_effects=SIDE_EFFECT),
    )(*operands, send_sems, recv_sems, after)
    return outs[:n], outs[n:]


def share_totals(totals, pack_total, last_part):
    arrays = list(totals) + [pack_total]
    n = len(arrays)

    def body(*refs):
        ins, last, outs, rep, last_all = refs[:n], refs[n], refs[n + 1:2 * n + 1], refs[2 * n + 1], refs[2 * n + 2]
        send_sems, recv_sems, rep_send, rep_recv, last_send, last_recv = refs[2 * n + 3:]
        x, y, c, chips = _position()
        sibling = (x, y, 1 - c)
        me = 4 * x + 2 * y + c
        sends = []
        for a in range(n):
            cp = _remote(ins[a], outs[a], send_sems.at[a], recv_sems.at[a], sibling)
            cp.start()
            sends.append(cp)
        mine = ins[n - 1].at[pl.ds(HALF_SHARDED, REP_PART)]
        peers = [sibling]
        for chip in chips:
            peers += [(chip[0], chip[1], c), (chip[0], chip[1], 1 - c)]
        for j, peer in enumerate(peers):
            for src, dst, s_sem, r_sem in ((mine, rep, rep_send, rep_recv), (last, last_all, last_send, last_recv)):
                cp = _remote(src, dst.at[me], s_sem.at[j], r_sem.at[j], peer)
                cp.start()
                sends.append(cp)
        for a in range(n):
            _remote(outs[a], outs[a], send_sems.at[a], recv_sems.at[a], sibling).wait_recv()
        for j, peer in enumerate(peers):
            it = 4 * peer[0] + 2 * peer[1] + peer[2]
            _remote(rep.at[it], rep.at[it], rep_send.at[j], rep_recv.at[j], peer).wait_recv()
            _remote(last_all.at[it], last_all.at[it], last_send.at[j], last_recv.at[j], peer).wait_recv()
        for cp in sends:
            cp.wait_send()

    outs = pl.pallas_call(
        body, name="grad_share_totals",
        out_shape=[jax.ShapeDtypeStruct(a.shape, a.dtype) for a in arrays]
        + [jax.ShapeDtypeStruct((8, REP_PART, LANES), F32), jax.ShapeDtypeStruct((8,) + last_part.shape, F32)],
        in_specs=[HBM_SPEC] * (n + 1), out_specs=[HBM_SPEC] * (n + 2),
        scratch_shapes=[pltpu.SemaphoreType.DMA((n,)), pltpu.SemaphoreType.DMA((n,))]
        + [pltpu.SemaphoreType.DMA((7,))] * 4,
    )(*arrays, last_part)
    return outs[:n], outs[n], outs[n + 1]


def sum_parts(parts, name):
    def body(p_ref, o_ref):
        total = p_ref[0]
        for k in range(1, parts.shape[0]):
            total = total + p_ref[k]
        o_ref[...] = total

    return pl.pallas_call(body, name=name, out_shape=jax.ShapeDtypeStruct(parts.shape[1:], parts.dtype))(parts)


TILE_BYTES = 2 << 20


def _row_tile(rows, cols):
    best = None
    for t in range(SUBLANES, rows + 1, SUBLANES):
        if rows % t == 0 and t * cols * 4 <= TILE_BYTES:
            best = t
    return best if best is not None else rows


def add_sibling(mine, received, core, out_dtype, name):
    _, _, r, cols = mine.shape
    tr = _row_tile(r, cols)

    def body(c_ref, a_ref, b_ref, o_ref):
        o_ref[...] = (a_ref[0] + b_ref[...].astype(F32)).astype(out_dtype)

    return pl.pallas_call(
        body, name=name,
        out_shape=jax.ShapeDtypeStruct((4, r, cols), out_dtype),
        grid_spec=pltpu.PrefetchScalarGridSpec(
            num_scalar_prefetch=1, grid=(4, r // tr),
            in_specs=[pl.BlockSpec((1, 1, tr, cols), lambda o, i, c_ref: (o, c_ref[0], i, 0)),
                      pl.BlockSpec((1, tr, cols), lambda o, i, c_ref: (o, i, 0))],
            out_specs=pl.BlockSpec((1, tr, cols), lambda o, i, c_ref: (o, i, 0))),
        compiler_params=_params("parallel", "parallel"),
    )(core, mine, received)


def add_chips(own, received, chip, name):
    _, r, cols = own.shape
    tr = _row_tile(r, cols)

    def body(p_ref, a_ref, b0, b1, b2, o_ref):
        o_ref[...] = ((a_ref[0].astype(F32) + b0[0].astype(F32)) + b1[0].astype(F32)) + b2[0].astype(F32)

    rb = lambda k: pl.BlockSpec((1, tr, cols), lambda i, p_ref: (k, i, 0))
    return pl.pallas_call(
        body, name=name,
        out_shape=jax.ShapeDtypeStruct((r, cols), F32),
        grid_spec=pltpu.PrefetchScalarGridSpec(
            num_scalar_prefetch=1, grid=(r // tr,),
            in_specs=[pl.BlockSpec((1, tr, cols), lambda i, p_ref: (p_ref[0], i, 0)), rb(0), rb(1), rb(2)],
            out_specs=pl.BlockSpec((tr, cols), lambda i, p_ref: (i, 0))),
        compiler_params=_params("parallel"),
    )(chip, own, received, received, received)


def _adamw_update(gv, w_ref, m_ref, v_ref, d_ref, nm_ref, nv_ref):
    nm = ADAM_B1 * m_ref[...] + (1.0 - ADAM_B1) * gv
    nv = ADAM_B2 * v_ref[...] + (1.0 - ADAM_B2) * (gv * gv)
    nm_ref[...] = nm
    nv_ref[...] = nv
    m_hat = nm / (1.0 - ADAM_B1 ** ADAM_STEP)
    v_hat = nv / (1.0 - ADAM_B2 ** ADAM_STEP)
    d_ref[...] = -ADAM_LR * (m_hat / (jnp.sqrt(v_hat) + ADAM_EPS) + ADAM_WD * w_ref[...])


def adamw_halves(w, own, received, m, v, core, name, by_columns=False):
    rows, cols = w.shape

    def body(c_ref, w_ref, own_ref, rec_ref, m_ref, v_ref, g_ref, d_ref, nm_ref, nv_ref):
        gv = jnp.where(pl.program_id(0) == c_ref[0], own_ref[...], rec_ref[...])
        g_ref[...] = gv
        _adamw_update(gv, w_ref, m_ref, v_ref, d_ref, nm_ref, nv_ref)

    if by_columns:
        nr = 1
        whole = pl.BlockSpec((rows, cols // 2), lambda h, i, c_ref: (0, h))
        half = pl.BlockSpec((rows, cols // 2), lambda h, i, c_ref: (0, 0))
    else:
        r = rows // 2
        tr = _row_tile(r, cols)
        nr = r // tr
        whole = pl.BlockSpec((tr, cols), lambda h, i, c_ref: (h * nr + i, 0))
        half = pl.BlockSpec((tr, cols), lambda h, i, c_ref: (i, 0))
    return pl.pallas_call(
        body, name=name,
        out_shape=(jax.ShapeDtypeStruct((rows, cols), F32),) * 4,
        grid_spec=pltpu.PrefetchScalarGridSpec(
            num_scalar_prefetch=1, grid=(2, nr),
            in_specs=[whole, half, half, whole, whole], out_specs=(whole,) * 4),
        compiler_params=_params("parallel", "parallel"),
    )(core, w, own, received, m, v)


def adamw_many(ws, gs, ms, vs, name):
    n = len(ws)

    def body(*refs):
        ins, outs = refs[:4 * n], refs[4 * n:]
        for k in range(n):
            w_ref, g_ref, m_ref, v_ref = (ins[j * n + k] for j in range(4))
            d_ref, nm_ref, nv_ref = outs[3 * k:3 * k + 3]
            _adamw_update(g_ref[...], w_ref, m_ref, v_ref, d_ref, nm_ref, nv_ref)

    flat = pl.pallas_call(
        body, name=name,
        out_shape=[jax.ShapeDtypeStruct(w.shape, F32) for w in ws for _ in range(3)],
    )(*ws, *gs, *ms, *vs)
    return [tuple(flat[3 * k:3 * k + 3]) for k in range(n)]


def adamw(w, g, m, v, name):
    r, cols = w.shape
    tr = _row_tile(r, cols)

    def body(w_ref, g_ref, m_ref, v_ref, g_out, d_ref, nm_ref, nv_ref):
        gv = g_ref[...]
        g_out[...] = gv
        _adamw_update(gv, w_ref, m_ref, v_ref, d_ref, nm_ref, nv_ref)

    blk = pl.BlockSpec((tr, cols), lambda i: (i, 0))
    return pl.pallas_call(
        body, name=name,
        out_shape=(jax.ShapeDtypeStruct((r, cols), F32),) * 4,
        grid=(r // tr,),
        in_specs=[blk] * 4, out_specs=(blk,) * 4,
        compiler_params=_params("parallel"),
    )(w, g, m, v)


WEIGHTS = ("even_norm_pre", "even_norm_post", "even_w_in", "rg_conv_w", "rg_conv_b", "rg_gate_w", "rg_gate_b",
           "rg_lambda", "sc_conv_w", "even_w_out", "odd_norm_pre", "odd_norm_post", "odd_w_in", "gla_w_gate_lr",
           "gla_b_gate", "gla_norm_g", "odd_w_out")
BIG = ("even_w_in", "even_w_out", "odd_w_in", "odd_w_out")


def _halves(a):
    return a.reshape((2, a.shape[0] // 2) + a.shape[1:])


def kernel(x, even_norm_pre, even_norm_post, even_w_in, rg_conv_w, rg_conv_b, rg_gate_w, rg_gate_b, rg_lambda, sc_conv_w, even_w_out, odd_norm_pre, odd_norm_post, odd_w_in, gla_w_gate_lr, gla_b_gate, gla_norm_g, odd_w_out, loss_target, m_even_norm_pre, m_even_norm_post, m_even_w_in, m_rg_conv_w, m_rg_conv_b, m_rg_gate_w, m_rg_gate_b, m_rg_lambda, m_sc_conv_w, m_even_w_out, m_odd_norm_pre, m_odd_norm_post, m_odd_w_in, m_gla_w_gate_lr, m_gla_b_gate, m_gla_norm_g, m_odd_w_out, v_even_norm_pre, v_even_norm_post, v_even_w_in, v_rg_conv_w, v_rg_conv_b, v_rg_gate_w, v_rg_gate_b, v_rg_lambda, v_sc_conv_w, v_even_w_out, v_odd_norm_pre, v_odd_norm_post, v_odd_w_in, v_gla_w_gate_lr, v_gla_b_gate, v_gla_norm_g, v_odd_w_out):
    given = dict(locals())
    shard = {n: given[n][0] for n in WEIGHTS}
    m_in = {n: given["m_" + n][0] for n in WEIGHTS}
    v_in = {n: given["v_" + n][0] for n in WEIGHTS}
    mx, my, mc = lax.axis_index("x"), lax.axis_index("y"), lax.axis_index("c")
    core = jnp.reshape(mc, (1,)).astype(jnp.int32)
    chip = jnp.reshape(2 * mx + my, (1,)).astype(jnp.int32)

    small_shard = _pack(shard, SHARDED_SMALL, SHARDED_ROWS)
    big_own = [_halves(shard[n].astype(BF16)) for n in BIG]
    started_a = gather_start(big_own[:1], [small_shard], "gather_start_a")
    started_b = gather_start(big_own[1:], [], "gather_start_b")
    even_w_in_full, small_full = gather_wait(started_a, 1, started_b[-1], "gather_wait_a")
    (even_w_in_full,) = pass_to_sibling([even_w_in_full], "gather_pass_a")
    even_w_in_full = place_own(even_w_in_full, big_own[0], chip, "place_even_w_in")
    small_full = lax.dynamic_update_slice(small_full, small_shard[None], (chip[0], 0, 0))
    full = {n: shard[n] for n, _ in REPLICATED + LAST_REPLICATED}
    full.update({n: _merge_owners(a) for n, a in _unpack(small_full, SHARDED_SMALL, lead=(4,)).items()})
    full["even_w_in"] = even_w_in_full.reshape(4, D_MODEL, EVEN_IN // 4)

    def late_weights(after):
        lands = pass_to_sibling(list(gather_wait(started_b, 3, after, "gather_wait_b")), "gather_pass_b")
        lands = [place_own(a, b, chip, "place_" + n) for a, b, n in zip(lands, big_own[1:], BIG[1:])]
        odd_w_in = jnp.transpose(lands[1].reshape(4, D_MODEL, ODD_IN // 4), (1, 0, 2)).reshape(D_MODEL, ODD_IN)
        return _prepare_weights({"even_w_out": lands[0].reshape(2 * D_MODEL, D_MODEL), "odd_w_in": odd_w_in,
                                 "odd_w_out": lands[2].reshape(D_MODEL, D_MODEL)})

    pending = {}

    def slab(a):
        return a.reshape((4, 2, a.shape[1] // 2) + a.shape[2:])

    def begin(tag, slabs, to_send, dtypes):
        got = exchange_with_sibling(to_send, "grad_sibling_" + tag)
        sums = [add_sibling(a, b, core, dt, "grad_add_sibling_%s%d" % (tag, i))
                for i, (a, b, dt) in enumerate(zip(slabs, got, dtypes))]
        pending[tag] = exchange_with_chips_start(sums, "grad_chips_start_" + tag)
        return pending[tag][-1][0, 0]

    def finish(tag, after):
        sums, got = exchange_with_chips_wait(pending[tag], after, "grad_chips_wait_" + tag)
        return [add_chips(a, b, chip, "grad_add_chips_%s%d" % (tag, i)) for i, (a, b) in enumerate(zip(sums, got))]

    def reduce_first(g, g16):
        odd_w_in = slab(jnp.transpose(g["odd_w_in"].reshape(D_MODEL, 4, ODD_IN // 4), (1, 0, 2)))
        slabs = [odd_w_in] + [slab(g[n].reshape(4, -1, D_MODEL)) for n in ("odd_w_out", "even_w_out")]
        to_send = [odd_w_in.astype(BF16)] + [slab(g16[n].reshape(4, -1, D_MODEL)) for n in ("odd_w_out", "even_w_out")]
        return begin("a", slabs, to_send, [BF16] * 3)

    def reduce_second(g, g16):
        pending["totals_a"] = finish("a", g["even_w_in"])
        rep_rows = _pack(g, REPLICATED, REPLICATED_ROWS).reshape(4, 2, REP_PART, LANES)
        sh_rows = _pack({n: _split_owners(g[n]) for n, _ in SHARDED_SMALL}, SHARDED_SMALL, SHARDED_ROWS, lead=(4,))
        pack = jnp.concatenate([sh_rows.reshape(4, 2, HALF_SHARDED, LANES), rep_rows], axis=2)
        return begin("b", [slab(g["even_w_in"]), pack], [slab(g16["even_w_in"]), pack], [BF16, F32])

    loss, grad_x, g = local_step(x[0], loss_target[0], _prepare_weights(full), reduce_first, reduce_second,
                                 late_weights)
    odd_w_in_t, odd_w_out_t, even_w_out_t = pending["totals_a"]
    even_w_in_t, pack_t = finish("b", grad_x)
    totals = [even_w_in_t, even_w_out_t, odd_w_in_t, odd_w_out_t]
    last_part = jnp.concatenate([_pack(g, LAST_REPLICATED, LAST_ROWS), loss])
    from_core, rep_all, last_all = share_totals(totals, pack_t, last_part)
    me = 2 * chip[0] + core[0]
    mine, theirs = pack_t[:HALF_SHARDED], from_core[4][:HALF_SHARDED]
    sh_total = jnp.where(mc == 0, jnp.concatenate([mine, theirs]), jnp.concatenate([theirs, mine]))
    rep_all = lax.dynamic_update_slice(rep_all, pack_t[None, HALF_SHARDED:], (me, 0, 0))
    rep_total = rep_all.reshape(REPLICATED_ROWS, LANES)
    last_total = sum_parts(lax.dynamic_update_slice(last_all, last_part[None], (me, 0, 0)), "grad_sum_last")
    last_total, loss = last_total[:LAST_ROWS], last_total[LAST_ROWS, 0]
    grads = {}

    delta, new_m, new_v = {}, {}, {}
    for i, n in enumerate(BIG):
        if shard[n].shape[1] % LANES:
            outs = adamw_halves(shard[n].T, totals[i].T, from_core[i].T, m_in[n].T, v_in[n].T, core, "adamw_" + n,
                                by_columns=True)
            grads[n], delta[n], new_m[n], new_v[n] = [o.T for o in outs]
        else:
            grads[n], delta[n], new_m[n], new_v[n] = adamw_halves(shard[n], totals[i], from_core[i], m_in[n],
                                                                  v_in[n], core, "adamw_" + n)
    gate = [src["rg_gate_w"].reshape(GATE_ROWS, LANES) for src in (shard, m_in, v_in)]
    grads["rg_gate_w"], delta["rg_gate_w"], new_m["rg_gate_w"], new_v["rg_gate_w"] = adamw(
        gate[0], rep_total, gate[1], gate[2], "adamw_rg_gate_w")
    rest = REPLICATED[1:]
    rest_rows = sum(_seg_rows(shape) for _, shape in rest)
    grads.update(_unpack(sh_total, SHARDED_SMALL))
    grads.update(_unpack(rep_total[GATE_ROWS:GATE_ROWS + rest_rows], rest))
    grads.update(_unpack(last_total, LAST_REPLICATED))
    names = [n for n, _ in SHARDED_SMALL + rest + LAST_REPLICATED]
    rows_of = lambda a, n: a.reshape(-1, given[n].shape[-1])
    outs = adamw_many([rows_of(given[n], n) for n in names], [rows_of(grads[n], n) for n in names],
                      [rows_of(given["m_" + n], n) for n in names], [rows_of(given["v_" + n], n) for n in names],
                      "adamw_small")
    for n, (d, nm, nv) in zip(names, outs):
        delta[n], new_m[n], new_v[n] = d, nm, nv
    result = [loss, grad_x[None]]
    for group in (grads, delta, new_m, new_v):
        result += [group[n].reshape(given[n].shape) for n in WEIGHTS]
    return tuple(result)
```
